```python
import jax, jax.numpy as jnp
from jax import lax
import numpy as np

D_MODEL = 1024
BATCH = 8
SEQ = 4096
DEPTH = 1

CHUNK = 64
LRU_WIDTH = 512
LRU_HEADS = 8
LRU_HEAD_DIM = LRU_WIDTH // LRU_HEADS
LRU_CONV_WIDTH = 4
LRU_C = 8.0
GMLP_WIDTH = 512
GMLP_GROUPS = 4
GMLP_GROUP_DIM = GMLP_WIDTH // GMLP_GROUPS
GMLP_BLOCK = 128
MIX_WIDTH = LRU_WIDTH + GMLP_WIDTH
IN_COLS = 2 * LRU_WIDTH + 2 * GMLP_WIDTH
D_FF = 3 * D_MODEL
FFN_CONV_WIDTH = 3
N_MOD = 6
EPS = 1e-6

kernel_name = "hybrid_rglru_gmlp_convffn_block"


def rmsnorm(x, g):
    xf = x.astype(jnp.float32)
    y = xf * lax.rsqrt(jnp.mean(xf * xf, axis=-1, keepdims=True) + EPS)
    return (y * g.astype(jnp.float32)).astype(x.dtype)


def layernorm(x, g, b):
    xf = x.astype(jnp.float32)
    mu = jnp.mean(xf, axis=-1, keepdims=True)
    var = jnp.mean(jnp.square(xf - mu), axis=-1, keepdims=True)
    y = (xf - mu) * lax.rsqrt(var + EPS)
    return (y * g.astype(jnp.float32) + b.astype(jnp.float32)).astype(x.dtype)


def causal_depthwise_conv(x, w, b):
    k_width = w.shape[0]
    s = x.shape[1]
    xp = jnp.pad(x, ((0, 0), (k_width - 1, 0), (0, 0)))
    out = xp[:, 0:s] * w[0]
    for k in range(1, k_width):
        out = out + xp[:, k:k + s] * w[k]
    return out + b


def _lin_rec_combine(left, right):
    a1, b1 = left
    a2, b2 = right
    return a1 * a2, a2 * b1 + b2


def rg_lru_group(x_raw, gate_raw, conv_w, conv_b, w_rgate, b_rgate, w_igate, b_igate, lru_a):
    bsz, s, _ = x_raw.shape
    xc = causal_depthwise_conv(x_raw, conv_w, conv_b)
    xh = xc.reshape(bsz, s, LRU_HEADS, LRU_HEAD_DIM)
    r = jax.nn.sigmoid(jnp.einsum('bshi,hij->bshj', xh, w_rgate) + b_rgate).reshape(bsz, s, LRU_WIDTH)
    i = jax.nn.sigmoid(jnp.einsum('bshi,hij->bshj', xh, w_igate) + b_igate).reshape(bsz, s, LRU_WIDTH)
    log_a = -LRU_C * r.astype(jnp.float32) * jax.nn.softplus(-lru_a.astype(jnp.float32))
    a = jnp.exp(log_a)
    mult = jnp.sqrt(-jnp.expm1(2.0 * log_a))
    bx = mult * (i * xc).astype(jnp.float32)
    _, h = lax.associative_scan(_lin_rec_combine, (a, bx), axis=1)
    return h.astype(x_raw.dtype) * jax.nn.gelu(gate_raw)


def gmlp_group(u_raw, v_raw, v_norm_g, v_norm_b, w_spatial, b_spatial):
    bsz, s, _ = u_raw.shape
    u = jax.nn.gelu(u_raw)
    v = layernorm(jax.nn.gelu(v_raw), v_norm_g, v_norm_b)
    vb = v.reshape(bsz, s // GMLP_BLOCK, GMLP_BLOCK, GMLP_GROUPS, GMLP_GROUP_DIM)
    pos = jnp.arange(GMLP_BLOCK)
    mask = (pos[None, :] // CHUNK) <= (pos[:, None] // CHUNK)
    ws = jnp.where(mask[None], w_spatial, jnp.zeros_like(w_spatial))
    sp = jnp.einsum('gij,bnjgc->bnigc', ws, vb) + b_spatial.T[None, None, :, :, None]
    return u * sp.reshape(bsz, s, GMLP_WIDTH)


def _fwd_setup_inputs(seed: int = 0) -> dict:
    key = jax.random.key(seed)
    ks = jax.random.split(key, 32)
    L = DEPTH

    def nrm(k, shape, scale):
        return jax.random.normal(k, shape, jnp.float32) * scale

    x = nrm(ks[0], (BATCH, SEQ, D_MODEL), 1.0)
    c = nrm(ks[1], (BATCH, D_MODEL), 1.0)
    w_ada = nrm(ks[2], (L, D_MODEL, N_MOD * D_MODEL), 0.5 * D_MODEL ** -0.5)
    b_ada = nrm(ks[3], (L, N_MOD * D_MODEL), 0.02)
    g_mix_pre = 1.0 + nrm(ks[4], (L, D_MODEL), 0.02)
    g_mix_post = 1.0 + nrm(ks[5], (L, D_MODEL), 0.02)
    w_in = nrm(ks[6], (L, D_MODEL, IN_COLS), D_MODEL ** -0.5)
    conv_w = nrm(ks[7], (L, LRU_CONV_WIDTH, LRU_WIDTH), LRU_CONV_WIDTH ** -0.5)
    conv_b = nrm(ks[8], (L, LRU_WIDTH), 0.02)
    w_rgate = nrm(ks[9], (L, LRU_HEADS, LRU_HEAD_DIM, LRU_HEAD_DIM), LRU_HEAD_DIM ** -0.5)
    b_rgate = nrm(ks[10], (L, LRU_HEADS, LRU_HEAD_DIM), 0.02)
    w_igate = nrm(ks[11], (L, LRU_HEADS, LRU_HEAD_DIM, LRU_HEAD_DIM), LRU_HEAD_DIM ** -0.5)
    b_igate = nrm(ks[12], (L, LRU_HEADS, LRU_HEAD_DIM), 0.02)
    a_c = jax.random.uniform(ks[13], (L, LRU_WIDTH), jnp.float32, 0.9, 0.999)
    p = a_c ** (1.0 / LRU_C)
    lru_a = jnp.log(p) - jnp.log1p(-p)
    v_norm_g = 1.0 + nrm(ks[14], (L, GMLP_WIDTH), 0.02)
    v_norm_b = nrm(ks[15], (L, GMLP_WIDTH), 0.02)
    w_spatial = nrm(ks[16], (L, GMLP_GROUPS, GMLP_BLOCK, GMLP_BLOCK), GMLP_BLOCK ** -0.5)
    b_spatial = 1.0 + nrm(ks[17], (L, GMLP_GROUPS, GMLP_BLOCK), 0.02)
    g_lru_out = 1.0 + nrm(ks[18], (L, LRU_WIDTH), 0.02)
    g_gmlp_out = 1.0 + nrm(ks[19], (L, GMLP_WIDTH), 0.02)
    w_out = nrm(ks[20], (L, MIX_WIDTH, D_MODEL), MIX_WIDTH ** -0.5)
    g_ffn_pre = 1.0 + nrm(ks[21], (L, D_MODEL), 0.02)
    g_ffn_post = 1.0 + nrm(ks[22], (L, D_MODEL), 0.02)
    w_up = nrm(ks[23], (L, D_MODEL, 2 * D_FF), D_MODEL ** -0.5)
    ffn_conv_w = nrm(ks[24], (L, FFN_CONV_WIDTH, 2 * D_FF), FFN_CONV_WIDTH ** -0.5)
    ffn_conv_b = nrm(ks[25], (L, 2 * D_FF), 0.02)
    w_down = nrm(ks[26], (L, D_FF, D_MODEL), D_FF ** -0.5)
    return {"x": x, "c": c, "w_ada": w_ada, "b_ada": b_ada,
            "g_mix_pre": g_mix_pre, "g_mix_post": g_mix_post, "w_in": w_in,
            "conv_w": conv_w, "conv_b": conv_b, "w_rgate": w_rgate, "b_rgate": b_rgate,
            "w_igate": w_igate, "b_igate": b_igate, "lru_a": lru_a,
            "v_norm_g": v_norm_g, "v_norm_b": v_norm_b, "w_spatial": w_spatial, "b_spatial": b_spatial,
            "g_lru_out": g_lru_out, "g_gmlp_out": g_gmlp_out, "w_out": w_out,
            "g_ffn_pre": g_ffn_pre, "g_ffn_post": g_ffn_post, "w_up": w_up,
            "ffn_conv_w": ffn_conv_w, "ffn_conv_b": ffn_conv_b, "w_down": w_down}


def _fwd_reference(x, c, w_ada, b_ada, g_mix_pre, g_mix_post, w_in, conv_w, conv_b,
              w_rgate, b_rgate, w_igate, b_igate, lru_a, v_norm_g, v_norm_b,
              w_spatial, b_spatial, g_lru_out, g_gmlp_out, w_out,
              g_ffn_pre, g_ffn_post, w_up, ffn_conv_w, ffn_conv_b, w_down):
    c_act = jax.nn.silu(c)
    for l in range(DEPTH):
        mod = c_act @ w_ada[l] + b_ada[l]
        sh_m, sc_m, gt_m, sh_f, sc_f, gt_f = [m[:, None, :] for m in jnp.split(mod, N_MOD, axis=-1)]

        h = rmsnorm(x, g_mix_pre[l]) * (1.0 + sc_m) + sh_m
        z = h @ w_in[l]
        lru_x, lru_gate, g_u, g_v = jnp.split(
            z, [LRU_WIDTH, 2 * LRU_WIDTH, 2 * LRU_WIDTH + GMLP_WIDTH], axis=-1)
        y_lru = rg_lru_group(lru_x, lru_gate, conv_w[l], conv_b[l], w_rgate[l], b_rgate[l],
                             w_igate[l], b_igate[l], lru_a[l])
        y_gmlp = gmlp_group(g_u, g_v, v_norm_g[l], v_norm_b[l], w_spatial[l], b_spatial[l])
        y = jnp.concatenate([rmsnorm(y_lru, g_lru_out[l]), rmsnorm(y_gmlp, g_gmlp_out[l])], axis=-1)
        y = y @ w_out[l]
        x = x + gt_m * rmsnorm(y, g_mix_post[l])

        h = rmsnorm(x, g_ffn_pre[l]) * (1.0 + sc_f) + sh_f
        up = causal_depthwise_conv(h @ w_up[l], ffn_conv_w[l], ffn_conv_b[l])
        g_ff, v_ff = jnp.split(up, 2, axis=-1)
        y = (jax.nn.gelu(g_ff) * v_ff) @ w_down[l]
        x = x + gt_f * rmsnorm(y, g_ffn_post[l])
    return x


import jax as _jax
import jax.numpy as _jnp

TWIN_FORMAT = 'train_step'
FWD_PARAMS = ['x', 'c', 'w_ada', 'b_ada', 'g_mix_pre', 'g_mix_post', 'w_in', 'conv_w', 'conv_b', 'w_rgate', 'b_rgate', 'w_igate', 'b_igate', 'lru_a', 'v_norm_g', 'v_norm_b', 'w_spatial', 'b_spatial', 'g_lru_out', 'g_gmlp_out', 'w_out', 'g_ffn_pre', 'g_ffn_post', 'w_up', 'ffn_conv_w', 'ffn_conv_b', 'w_down']
TWIN_WEIGHTS = ['w_ada', 'b_ada', 'g_mix_pre', 'g_mix_post', 'w_in', 'conv_w', 'conv_b', 'w_rgate', 'b_rgate', 'w_igate', 'b_igate', 'lru_a', 'v_norm_g', 'v_norm_b', 'w_spatial', 'b_spatial', 'g_lru_out', 'g_gmlp_out', 'w_out', 'g_ffn_pre', 'g_ffn_post', 'w_up', 'ffn_conv_w', 'ffn_conv_b', 'w_down']
TWIN_DIFF_INPUT = 'x'
TWIN_INPUTS = ['x', 'c', 'w_ada', 'b_ada', 'g_mix_pre', 'g_mix_post', 'w_in', 'conv_w', 'conv_b', 'w_rgate', 'b_rgate', 'w_igate', 'b_igate', 'lru_a', 'v_norm_g', 'v_norm_b', 'w_spatial', 'b_spatial', 'g_lru_out', 'g_gmlp_out', 'w_out', 'g_ffn_pre', 'g_ffn_post', 'w_up', 'ffn_conv_w', 'ffn_conv_b', 'w_down', 'loss_target', 'm_w_ada', 'm_b_ada', 'm_g_mix_pre', 'm_g_mix_post', 'm_w_in', 'm_conv_w', 'm_conv_b', 'm_w_rgate', 'm_b_rgate', 'm_w_igate', 'm_b_igate', 'm_lru_a', 'm_v_norm_g', 'm_v_norm_b', 'm_w_spatial', 'm_b_spatial', 'm_g_lru_out', 'm_g_gmlp_out', 'm_w_out', 'm_g_ffn_pre', 'm_g_ffn_post', 'm_w_up', 'm_ffn_conv_w', 'm_ffn_conv_b', 'm_w_down', 'v_w_ada', 'v_b_ada', 'v_g_mix_pre', 'v_g_mix_post', 'v_w_in', 'v_conv_w', 'v_conv_b', 'v_w_rgate', 'v_b_rgate', 'v_w_igate', 'v_b_igate', 'v_lru_a', 'v_v_norm_g', 'v_v_norm_b', 'v_w_spatial', 'v_b_spatial', 'v_g_lru_out', 'v_g_gmlp_out', 'v_w_out', 'v_g_ffn_pre', 'v_g_ffn_post', 'v_w_up', 'v_ffn_conv_w', 'v_ffn_conv_b', 'v_w_down']
TWIN_OUTPUTS = ['loss', 'grad_x', 'grad_w_ada', 'grad_b_ada', 'grad_g_mix_pre', 'grad_g_mix_post', 'grad_w_in', 'grad_conv_w', 'grad_conv_b', 'grad_w_rgate', 'grad_b_rgate', 'grad_w_igate', 'grad_b_igate', 'grad_lru_a', 'grad_v_norm_g', 'grad_v_norm_b', 'grad_w_spatial', 'grad_b_spatial', 'grad_g_lru_out', 'grad_g_gmlp_out', 'grad_w_out', 'grad_g_ffn_pre', 'grad_g_ffn_post', 'grad_w_up', 'grad_ffn_conv_w', 'grad_ffn_conv_b', 'grad_w_down', 'delta_w_ada', 'delta_b_ada', 'delta_g_mix_pre', 'delta_g_mix_post', 'delta_w_in', 'delta_conv_w', 'delta_conv_b', 'delta_w_rgate', 'delta_b_rgate', 'delta_w_igate', 'delta_b_igate', 'delta_lru_a', 'delta_v_norm_g', 'delta_v_norm_b', 'delta_w_spatial', 'delta_b_spatial', 'delta_g_lru_out', 'delta_g_gmlp_out', 'delta_w_out', 'delta_g_ffn_pre', 'delta_g_ffn_post', 'delta_w_up', 'delta_ffn_conv_w', 'delta_ffn_conv_b', 'delta_w_down', 'new_m_w_ada', 'new_m_b_ada', 'new_m_g_mix_pre', 'new_m_g_mix_post', 'new_m_w_in', 'new_m_conv_w', 'new_m_conv_b', 'new_m_w_rgate', 'new_m_b_rgate', 'new_m_w_igate', 'new_m_b_igate', 'new_m_lru_a', 'new_m_v_norm_g', 'new_m_v_norm_b', 'new_m_w_spatial', 'new_m_b_spatial', 'new_m_g_lru_out', 'new_m_g_gmlp_out', 'new_m_w_out', 'new_m_g_ffn_pre', 'new_m_g_ffn_post', 'new_m_w_up', 'new_m_ffn_conv_w', 'new_m_ffn_conv_b', 'new_m_w_down', 'new_v_w_ada', 'new_v_b_ada', 'new_v_g_mix_pre', 'new_v_g_mix_post', 'new_v_w_in', 'new_v_conv_w', 'new_v_conv_b', 'new_v_w_rgate', 'new_v_b_rgate', 'new_v_w_igate', 'new_v_b_igate', 'new_v_lru_a', 'new_v_v_norm_g', 'new_v_v_norm_b', 'new_v_w_spatial', 'new_v_b_spatial', 'new_v_g_lru_out', 'new_v_g_gmlp_out', 'new_v_w_out', 'new_v_g_ffn_pre', 'new_v_g_ffn_post', 'new_v_w_up', 'new_v_ffn_conv_w', 'new_v_ffn_conv_b', 'new_v_w_down']
TWIN_LEAF_KINDS = {'loss': 'loss', 'grad_x': 'grad_x', 'grad_w_ada': 'grad_w', 'grad_b_ada': 'grad_w', 'grad_g_mix_pre': 'grad_w', 'grad_g_mix_post': 'grad_w', 'grad_w_in': 'grad_w', 'grad_conv_w': 'grad_w', 'grad_conv_b': 'grad_w', 'grad_w_rgate': 'grad_w', 'grad_b_rgate': 'grad_w', 'grad_w_igate': 'grad_w', 'grad_b_igate': 'grad_w', 'grad_lru_a': 'grad_w', 'grad_v_norm_g': 'grad_w', 'grad_v_norm_b': 'grad_w', 'grad_w_spatial': 'grad_w', 'grad_b_spatial': 'grad_w', 'grad_g_lru_out': 'grad_w', 'grad_g_gmlp_out': 'grad_w', 'grad_w_out': 'grad_w', 'grad_g_ffn_pre': 'grad_w', 'grad_g_ffn_post': 'grad_w', 'grad_w_up': 'grad_w', 'grad_ffn_conv_w': 'grad_w', 'grad_ffn_conv_b': 'grad_w', 'grad_w_down': 'grad_w', 'delta_w_ada': 'delta_w', 'delta_b_ada': 'delta_w', 'delta_g_mix_pre': 'delta_w', 'delta_g_mix_post': 'delta_w', 'delta_w_in': 'delta_w', 'delta_conv_w': 'delta_w', 'delta_conv_b': 'delta_w', 'delta_w_rgate': 'delta_w', 'delta_b_rgate': 'delta_w', 'delta_w_igate': 'delta_w', 'delta_b_igate': 'delta_w', 'delta_lru_a': 'delta_w', 'delta_v_norm_g': 'delta_w', 'delta_v_norm_b': 'delta_w', 'delta_w_spatial': 'delta_w', 'delta_b_spatial': 'delta_w', 'delta_g_lru_out': 'delta_w', 'delta_g_gmlp_out': 'delta_w', 'delta_w_out': 'delta_w', 'delta_g_ffn_pre': 'delta_w', 'delta_g_ffn_post': 'delta_w', 'delta_w_up': 'delta_w', 'delta_ffn_conv_w': 'delta_w', 'delta_ffn_conv_b': 'delta_w', 'delta_w_down': 'delta_w', 'new_m_w_ada': 'new_m', 'new_m_b_ada': 'new_m', 'new_m_g_mix_pre': 'new_m', 'new_m_g_mix_post': 'new_m', 'new_m_w_in': 'new_m', 'new_m_conv_w': 'new_m', 'new_m_conv_b': 'new_m', 'new_m_w_rgate': 'new_m', 'new_m_b_rgate': 'new_m', 'new_m_w_igate': 'new_m', 'new_m_b_igate': 'new_m', 'new_m_lru_a': 'new_m', 'new_m_v_norm_g': 'new_m', 'new_m_v_norm_b': 'new_m', 'new_m_w_spatial': 'new_m', 'new_m_b_spatial': 'new_m', 'new_m_g_lru_out': 'new_m', 'new_m_g_gmlp_out': 'new_m', 'new_m_w_out': 'new_m', 'new_m_g_ffn_pre': 'new_m', 'new_m_g_ffn_post': 'new_m', 'new_m_w_up': 'new_m', 'new_m_ffn_conv_w': 'new_m', 'new_m_ffn_conv_b': 'new_m', 'new_m_w_down': 'new_m', 'new_v_w_ada': 'new_v', 'new_v_b_ada': 'new_v', 'new_v_g_mix_pre': 'new_v', 'new_v_g_mix_post': 'new_v', 'new_v_w_in': 'new_v', 'new_v_conv_w': 'new_v', 'new_v_conv_b': 'new_v', 'new_v_w_rgate': 'new_v', 'new_v_b_rgate': 'new_v', 'new_v_w_igate': 'new_v', 'new_v_b_igate': 'new_v', 'new_v_lru_a': 'new_v', 'new_v_v_norm_g': 'new_v', 'new_v_v_norm_b': 'new_v', 'new_v_w_spatial': 'new_v', 'new_v_b_spatial': 'new_v', 'new_v_g_lru_out': 'new_v', 'new_v_g_gmlp_out': 'new_v', 'new_v_w_out': 'new_v', 'new_v_g_ffn_pre': 'new_v', 'new_v_g_ffn_post': 'new_v', 'new_v_w_up': 'new_v', 'new_v_ffn_conv_w': 'new_v', 'new_v_ffn_conv_b': 'new_v', 'new_v_w_down': 'new_v'}


def _forward(args):
    return _fwd_reference(*[args[k] for k in FWD_PARAMS])


def _output_shape():
    def fwd():
        inp = _fwd_setup_inputs(0)
        return _fwd_reference(*[inp[k] for k in FWD_PARAMS])
    out = _jax.eval_shape(fwd)
    return out.shape, out.dtype

N_MICROBATCH = 1
ADAM_LR = 0.001
ADAM_B1 = 0.9
ADAM_B2 = 0.999
ADAM_EPS = 1e-08
ADAM_WD = 0.01
ADAM_STEP = 10
PER_EXAMPLE_BATCH_AXIS = {'x': 0, 'c': 0, 'loss_target': 0}
SHARED_INPUTS = []
_WEIGHT_DTYPES = {'w_ada': _jnp.float32, 'b_ada': _jnp.float32, 'g_mix_pre': _jnp.float32, 'g_mix_post': _jnp.float32, 'w_in': _jnp.float32, 'conv_w': _jnp.float32, 'conv_b': _jnp.float32, 'w_rgate': _jnp.float32, 'b_rgate': _jnp.float32, 'w_igate': _jnp.float32, 'b_igate': _jnp.float32, 'lru_a': _jnp.float32, 'v_norm_g': _jnp.float32, 'v_norm_b': _jnp.float32, 'w_spatial': _jnp.float32, 'b_spatial': _jnp.float32, 'g_lru_out': _jnp.float32, 'g_gmlp_out': _jnp.float32, 'w_out': _jnp.float32, 'g_ffn_pre': _jnp.float32, 'g_ffn_post': _jnp.float32, 'w_up': _jnp.float32, 'ffn_conv_w': _jnp.float32, 'ffn_conv_b': _jnp.float32, 'w_down': _jnp.float32}
MOMENT_SCALE = {'w_ada': 1.290081e+00, 'b_ada': 2.799277e+00, 'g_mix_pre': 1.118253e-01, 'g_mix_post': 3.443363e+00, 'w_in': 1.762813e-01, 'conv_w': 3.088939e-01, 'conv_b': 1.089324e+00, 'w_rgate': 4.585025e-02, 'b_rgate': 4.452789e-02, 'w_igate': 8.631447e-02, 'b_igate': 1.161979e-01, 'lru_a': 1.096933e-01, 'v_norm_g': 6.472887e-02, 'v_norm_b': 5.458499e-02, 'w_spatial': 5.719889e-02, 'b_spatial': 6.185866e-02, 'g_lru_out': 3.560307e-01, 'g_gmlp_out': 2.828816e-01, 'w_out': 2.922398e-01, 'g_ffn_pre': 1.253795e-01, 'g_ffn_post': 3.375510e+00, 'w_up': 5.696696e-02, 'ffn_conv_w': 6.037852e-02, 'ffn_conv_b': 1.034610e-01, 'w_down': 1.159122e-01}


def _to_microbatches(a, axis):
    t = _jnp.moveaxis(a, axis, 0)
    t = t.reshape((N_MICROBATCH, t.shape[0] // N_MICROBATCH) + t.shape[1:])
    return _jnp.moveaxis(t, 1, axis + 1)


def setup_inputs(seed: int = 0) -> dict:
    inp = _fwd_setup_inputs(seed)
    key = _jax.random.fold_in(_jax.random.key(seed), 7919)
    shape, _ = _output_shape()
    out = dict(inp)
    out["loss_target"] = _jax.random.normal(_jax.random.fold_in(key, 0), shape, _jnp.float32)
    for i, name in enumerate(TWIN_WEIGHTS):
        w = inp[name].astype(_jnp.float32)
        if MOMENT_SCALE is None:
            s = _jnp.sqrt(_jnp.mean(_jnp.square(w)) + 1e-30)
        else:
            s = MOMENT_SCALE[name]
        km, kv = _jax.random.split(_jax.random.fold_in(key, i + 1))
        out[name] = w
        out["m_" + name] = s * _jax.random.normal(km, w.shape, _jnp.float32)
        out["v_" + name] = (s * s) * _jax.random.uniform(kv, w.shape, _jnp.float32, 0.5, 1.5)
    if N_MICROBATCH > 1:
        for name, axis in PER_EXAMPLE_BATCH_AXIS.items():
            out[name] = _to_microbatches(out[name], axis)
    return {'x': out['x'], 'c': out['c'], 'w_ada': out['w_ada'], 'b_ada': out['b_ada'], 'g_mix_pre': out['g_mix_pre'], 'g_mix_post': out['g_mix_post'], 'w_in': out['w_in'], 'conv_w': out['conv_w'], 'conv_b': out['conv_b'], 'w_rgate': out['w_rgate'], 'b_rgate': out['b_rgate'], 'w_igate': out['w_igate'], 'b_igate': out['b_igate'], 'lru_a': out['lru_a'], 'v_norm_g': out['v_norm_g'], 'v_norm_b': out['v_norm_b'], 'w_spatial': out['w_spatial'], 'b_spatial': out['b_spatial'], 'g_lru_out': out['g_lru_out'], 'g_gmlp_out': out['g_gmlp_out'], 'w_out': out['w_out'], 'g_ffn_pre': out['g_ffn_pre'], 'g_ffn_post': out['g_ffn_post'], 'w_up': out['w_up'], 'ffn_conv_w': out['ffn_conv_w'], 'ffn_conv_b': out['ffn_conv_b'], 'w_down': out['w_down'], 'loss_target': out['loss_target'], 'm_w_ada': out['m_w_ada'], 'm_b_ada': out['m_b_ada'], 'm_g_mix_pre': out['m_g_mix_pre'], 'm_g_mix_post': out['m_g_mix_post'], 'm_w_in': out['m_w_in'], 'm_conv_w': out['m_conv_w'], 'm_conv_b': out['m_conv_b'], 'm_w_rgate': out['m_w_rgate'], 'm_b_rgate': out['m_b_rgate'], 'm_w_igate': out['m_w_igate'], 'm_b_igate': out['m_b_igate'], 'm_lru_a': out['m_lru_a'], 'm_v_norm_g': out['m_v_norm_g'], 'm_v_norm_b': out['m_v_norm_b'], 'm_w_spatial': out['m_w_spatial'], 'm_b_spatial': out['m_b_spatial'], 'm_g_lru_out': out['m_g_lru_out'], 'm_g_gmlp_out': out['m_g_gmlp_out'], 'm_w_out': out['m_w_out'], 'm_g_ffn_pre': out['m_g_ffn_pre'], 'm_g_ffn_post': out['m_g_ffn_post'], 'm_w_up': out['m_w_up'], 'm_ffn_conv_w': out['m_ffn_conv_w'], 'm_ffn_conv_b': out['m_ffn_conv_b'], 'm_w_down': out['m_w_down'], 'v_w_ada': out['v_w_ada'], 'v_b_ada': out['v_b_ada'], 'v_g_mix_pre': out['v_g_mix_pre'], 'v_g_mix_post': out['v_g_mix_post'], 'v_w_in': out['v_w_in'], 'v_conv_w': out['v_conv_w'], 'v_conv_b': out['v_conv_b'], 'v_w_rgate': out['v_w_rgate'], 'v_b_rgate': out['v_b_rgate'], 'v_w_igate': out['v_w_igate'], 'v_b_igate': out['v_b_igate'], 'v_lru_a': out['v_lru_a'], 'v_v_norm_g': out['v_v_norm_g'], 'v_v_norm_b': out['v_v_norm_b'], 'v_w_spatial': out['v_w_spatial'], 'v_b_spatial': out['v_b_spatial'], 'v_g_lru_out': out['v_g_lru_out'], 'v_g_gmlp_out': out['v_g_gmlp_out'], 'v_w_out': out['v_w_out'], 'v_g_ffn_pre': out['v_g_ffn_pre'], 'v_g_ffn_post': out['v_g_ffn_post'], 'v_w_up': out['v_w_up'], 'v_ffn_conv_w': out['v_ffn_conv_w'], 'v_ffn_conv_b': out['v_ffn_conv_b'], 'v_w_down': out['v_w_down']}


def _loss(weights, diff, rest, loss_target):
    with _jax.named_scope("forward"):
        args = {**rest, TWIN_DIFF_INPUT: diff, **{k: w.astype(_WEIGHT_DTYPES[k]) for k, w in weights.items()}}
        y = _forward(args)
    with _jax.named_scope("loss_head"):
        err = _jnp.square(y.astype(_jnp.float32) - loss_target)
        return 0.5 * _jnp.sum(_jnp.mean(err, axis=-1)) if err.ndim else 0.5 * err


def _adamw(w, g, m, v):
    m = ADAM_B1 * m + (1.0 - ADAM_B1) * g
    v = ADAM_B2 * v + (1.0 - ADAM_B2) * _jnp.square(g)
    m_hat = m / (1.0 - ADAM_B1 ** ADAM_STEP)
    v_hat = v / (1.0 - ADAM_B2 ** ADAM_STEP)
    delta = -ADAM_LR * (m_hat / (_jnp.sqrt(v_hat) + ADAM_EPS) + ADAM_WD * w)
    return delta, m, v


def reference(x, c, w_ada, b_ada, g_mix_pre, g_mix_post, w_in, conv_w, conv_b, w_rgate, b_rgate, w_igate, b_igate, lru_a, v_norm_g, v_norm_b, w_spatial, b_spatial, g_lru_out, g_gmlp_out, w_out, g_ffn_pre, g_ffn_post, w_up, ffn_conv_w, ffn_conv_b, w_down, loss_target, m_w_ada, m_b_ada, m_g_mix_pre, m_g_mix_post, m_w_in, m_conv_w, m_conv_b, m_w_rgate, m_b_rgate, m_w_igate, m_b_igate, m_lru_a, m_v_norm_g, m_v_norm_b, m_w_spatial, m_b_spatial, m_g_lru_out, m_g_gmlp_out, m_w_out, m_g_ffn_pre, m_g_ffn_post, m_w_up, m_ffn_conv_w, m_ffn_conv_b, m_w_down, v_w_ada, v_b_ada, v_g_mix_pre, v_g_mix_post, v_w_in, v_conv_w, v_conv_b, v_w_rgate, v_b_rgate, v_w_igate, v_b_igate, v_lru_a, v_v_norm_g, v_v_norm_b, v_w_spatial, v_b_spatial, v_g_lru_out, v_g_gmlp_out, v_w_out, v_g_ffn_pre, v_g_ffn_post, v_w_up, v_ffn_conv_w, v_ffn_conv_b, v_w_down):
    given = dict(x=x, c=c, w_ada=w_ada, b_ada=b_ada, g_mix_pre=g_mix_pre, g_mix_post=g_mix_post, w_in=w_in, conv_w=conv_w, conv_b=conv_b, w_rgate=w_rgate, b_rgate=b_rgate, w_igate=w_igate, b_igate=b_igate, lru_a=lru_a, v_norm_g=v_norm_g, v_norm_b=v_norm_b, w_spatial=w_spatial, b_spatial=b_spatial, g_lru_out=g_lru_out, g_gmlp_out=g_gmlp_out, w_out=w_out, g_ffn_pre=g_ffn_pre, g_ffn_post=g_ffn_post, w_up=w_up, ffn_conv_w=ffn_conv_w, ffn_conv_b=ffn_conv_b, w_down=w_down, loss_target=loss_target, m_w_ada=m_w_ada, m_b_ada=m_b_ada, m_g_mix_pre=m_g_mix_pre, m_g_mix_post=m_g_mix_post, m_w_in=m_w_in, m_conv_w=m_conv_w, m_conv_b=m_conv_b, m_w_rgate=m_w_rgate, m_b_rgate=m_b_rgate, m_w_igate=m_w_igate, m_b_igate=m_b_igate, m_lru_a=m_lru_a, m_v_norm_g=m_v_norm_g, m_v_norm_b=m_v_norm_b, m_w_spatial=m_w_spatial, m_b_spatial=m_b_spatial, m_g_lru_out=m_g_lru_out, m_g_gmlp_out=m_g_gmlp_out, m_w_out=m_w_out, m_g_ffn_pre=m_g_ffn_pre, m_g_ffn_post=m_g_ffn_post, m_w_up=m_w_up, m_ffn_conv_w=m_ffn_conv_w, m_ffn_conv_b=m_ffn_conv_b, m_w_down=m_w_down, v_w_ada=v_w_ada, v_b_ada=v_b_ada, v_g_mix_pre=v_g_mix_pre, v_g_mix_post=v_g_mix_post, v_w_in=v_w_in, v_conv_w=v_conv_w, v_conv_b=v_conv_b, v_w_rgate=v_w_rgate, v_b_rgate=v_b_rgate, v_w_igate=v_w_igate, v_b_igate=v_b_igate, v_lru_a=v_lru_a, v_v_norm_g=v_v_norm_g, v_v_norm_b=v_v_norm_b, v_w_spatial=v_w_spatial, v_b_spatial=v_b_spatial, v_g_lru_out=v_g_lru_out, v_g_gmlp_out=v_g_gmlp_out, v_w_out=v_w_out, v_g_ffn_pre=v_g_ffn_pre, v_g_ffn_post=v_g_ffn_post, v_w_up=v_w_up, v_ffn_conv_w=v_ffn_conv_w, v_ffn_conv_b=v_ffn_conv_b, v_w_down=v_w_down)
    weights = {n: given[n] for n in TWIN_WEIGHTS}
    shared = {n: given[n] for n in SHARED_INPUTS}
    per_example = {n: given[n] for n in ['x', 'c']}
    grad_fn = _jax.value_and_grad(_loss, argnums=(0, 1))

    def one_microbatch(ex, loss_target):
        ex = dict(ex)
        diff = ex.pop(TWIN_DIFF_INPUT)
        return grad_fn(weights, diff, {**shared, **ex}, loss_target)

    if N_MICROBATCH == 1:
        loss, (grad_w, grad_x) = one_microbatch(per_example, given["loss_target"])
    else:
        def body(carry, xs):
            loss_sum, grad_sum = carry
            l_k, (gw_k, gx_k) = one_microbatch(xs[0], xs[1])
            with _jax.named_scope("update"):
                return (loss_sum + l_k, _jax.tree.map(_jnp.add, grad_sum, gw_k)), gx_k

        init = (_jnp.zeros((), _jnp.float32), _jax.tree.map(_jnp.zeros_like, weights))
        (loss, grad_w), grad_x = _jax.lax.scan(body, init, (per_example, given["loss_target"]))
    with _jax.named_scope("update"):
        delta_w, new_m, new_v = {}, {}, {}
        for n in TWIN_WEIGHTS:
            delta_w[n], new_m[n], new_v[n] = _adamw(weights[n], grad_w[n], given["m_" + n], given["v_" + n])
    return (loss, grad_x, *[grad_w[n] for n in TWIN_WEIGHTS], *[delta_w[n] for n in TWIN_WEIGHTS],
            *[new_m[n] for n in TWIN_WEIGHTS], *[new_v[n] for n in TWIN_WEIGHTS])
```

```python
import functools
import math

import jax
import jax.numpy as jnp
from jax import lax
from jax.experimental import pallas as pl
from jax.experimental.pallas import tpu as pltpu

F32 = jnp.float32
BF16 = jnp.bfloat16
MESH = pl.DeviceIdType.MESH

D_MODEL = 1024
LRU_WIDTH = 512
LRU_HEADS = 8
GMLP_WIDTH = 512
GMLP_GROUPS = 4
GMLP_BLOCK = 128
CHUNK = 64
D_FF = 3072
N_MOD = 6
EPS = 1e-6
LRU_C = 8.0
N_CHIPS = 4
N_DEV = 8

ADAM_LR = 0.001
ADAM_B1 = 0.9
ADAM_B2 = 0.999
ADAM_EPS = 1e-08
ADAM_WD = 0.01
ADAM_STEP = 10

GELU_C0 = math.sqrt(2.0 / math.pi)
GELU_C1 = 0.044715

VMEM_LIMIT_BYTES = 56 * 1024 * 1024
SUBLANES = 8
BF16_SUBLANES = 16
FFN_CHUNK = 768


def _gelu(x):
    t = jnp.tanh(GELU_C0 * (x + GELU_C1 * x * x * x))
    return 0.5 * x * (1.0 + t)


def _gelu_and_grad(x):
    x2 = x * x
    t = jnp.tanh(GELU_C0 * x * (1.0 + GELU_C1 * x2))
    g = 0.5 * x * (1.0 + t)
    dg = 0.5 * (1.0 + t) + 0.5 * x * (1.0 - t * t) * (GELU_C0 * (1.0 + 3.0 * GELU_C1 * x2))
    return g, dg


def _sigmoid(x):
    return 1.0 / (1.0 + jnp.exp(-x))


def _log1p(u):
    w = 1.0 + u
    return jnp.where(w == 1.0, u, jnp.log(w) * (u / (w - 1.0)))


def _softplus(x):
    return jnp.maximum(x, 0.0) + _log1p(jnp.exp(-jnp.abs(x)))


def _neg_expm1(x):
    u = jnp.exp(x)
    um1 = u - 1.0
    tiny = um1 == 0.0
    small = um1 * (x / jnp.log(jnp.where(tiny, 2.0, jnp.maximum(u, 0.25))))
    return -jnp.where(tiny, x, jnp.where(x < -1.0, um1, small))


def _msq_rsqrt(v):
    return lax.rsqrt(jnp.mean(v * v, axis=-1, keepdims=True) + EPS)


def _rms_bwd(dyn, yn, r):
    return r * (dyn - yn * jnp.mean(dyn * yn, axis=-1, keepdims=True))


def _colsum(v):
    return jnp.sum(v, axis=0, keepdims=True)


def _shift_down(cur, prev8, k):
    rolled = pltpu.roll(cur, k, 0)
    head = pltpu.roll(prev8, k, 0)
    row8 = lax.broadcasted_iota(jnp.int32, (SUBLANES, cur.shape[1]), 0)
    first = jnp.where(row8 < k, head, rolled[0:SUBLANES])
    return jnp.concatenate([first, rolled[SUBLANES:]], axis=0)


def _shift_up(cur, next8, k):
    t = cur.shape[0]
    rolled = pltpu.roll(cur, t - k, 0)
    tail = pltpu.roll(next8, SUBLANES - k, 0)
    row8 = lax.broadcasted_iota(jnp.int32, (SUBLANES, cur.shape[1]), 0)
    last = jnp.where(row8 >= SUBLANES - k, tail, rolled[t - SUBLANES:])
    return jnp.concatenate([rolled[:t - SUBLANES], last], axis=0)


def _scan_fwd(a, b):
    t = a.shape[0]
    row = lax.broadcasted_iota(jnp.int32, a.shape, 0)
    d = 1
    while d < t:
        keep = row >= d
        a_s = jnp.where(keep, pltpu.roll(a, d, 0), 1.0)
        b_s = jnp.where(keep, pltpu.roll(b, d, 0), 0.0)
        b = a * b_s + b
        a = a * a_s
        d *= 2
    return a, b


def _scan_bwd(a, g):
    t = a.shape[0]
    row = lax.broadcasted_iota(jnp.int32, a.shape, 0)
    d = 1
    while d < t:
        keep = row < t - d
        a_s = jnp.where(keep, pltpu.roll(a, t - d, 0), 1.0)
        g_s = jnp.where(keep, pltpu.roll(g, t - d, 0), 0.0)
        g = a * g_s + g
        a = a * a_s
        d *= 2
    return a, g


def _dot(a, b):
    return jnp.dot(a, b, preferred_element_type=F32)


def _dot_nt(a, b):
    return lax.dot_general(a, b, (((1,), (1,)), ((), ())), preferred_element_type=F32)


def _dot_tn(a, b):
    return lax.dot_general(a, b, (((0,), (0,)), ((), ())), preferred_element_type=F32)


def _rows(ts, cols, rev_of=None):
    if rev_of is None:
        return pl.BlockSpec((ts, cols), lambda i: (i, 0))
    return pl.BlockSpec((ts, cols), lambda i: (rev_of - 1 - i, 0))


def _halo_prev(ts, cols, halo, rev_of=None, col_block=0):
    per = ts // halo
    if rev_of is None:
        return pl.BlockSpec((halo, cols), lambda i: (jnp.maximum(i * per - 1, 0), col_block))
    return pl.BlockSpec((halo, cols), lambda i: (jnp.maximum((rev_of - 1 - i) * per - 1, 0), col_block))


def _full(shape):
    nd = len(shape)
    return pl.BlockSpec(shape, lambda *_: (0,) * nd)


_RESIDENT = pl.BlockSpec(memory_space=pltpu.VMEM)


def _params(sem):
    return pltpu.CompilerParams(dimension_semantics=sem, vmem_limit_bytes=VMEM_LIMIT_BYTES)


def _sds(shape, dtype):
    return jax.ShapeDtypeStruct(shape, dtype)


def _mix_in(x, sc, sh, g, w_in4, ts=256):
    s, d = x.shape

    def body(x_ref, sc_ref, sh_ref, g_ref, w_ref, z_ref, h_ref):
        xv = x_ref[...]
        h = (xv * _msq_rsqrt(xv) * g_ref[...]) * (1.0 + sc_ref[...]) + sh_ref[...]
        hb = h.astype(BF16)
        h_ref[...] = hb
        for k in range(N_CHIPS):
            z_ref[:, k * 512:(k + 1) * 512] = _dot(hb, w_ref[k])

    return pl.pallas_call(
        body, grid=(s // ts,), name="mix_in",
        in_specs=[_rows(ts, d), _full((1, d)), _full((1, d)), _full((1, d)), _full(w_in4.shape)],
        out_specs=[_rows(ts, 2048), _rows(ts, d)],
        out_shape=[_sds((s, 2048), F32), _sds((s, d), BF16)],
        compiler_params=_params(("parallel",)),
    )(x, sc, sh, g, w_in4)


def _seq_recompute(z_ref, zprev_ref, first_tile, p):
    (cw_ref, cb_ref, bdr_ref, bdi_ref, br_ref, bi_ref, la_ref, ng_ref, nb_ref, ws_ref, bst_ref) = p
    lx = z_ref[:, 0:512]
    lg = z_ref[:, 512:1024]
    gu = z_ref[:, 1024:1536]
    gv = z_ref[:, 1536:2048]
    prev8 = jnp.where(first_tile, 0.0, zprev_ref[...])
    s1 = _shift_down(lx, prev8, 1)
    s2 = _shift_down(lx, prev8, 2)
    s3 = _shift_down(lx, prev8, 3)
    xc = cw_ref[3:4, :] * lx + cw_ref[2:3, :] * s1 + cw_ref[1:2, :] * s2 + cw_ref[0:1, :] * s3 + cb_ref[...]
    xcb = xc.astype(BF16)
    r = _sigmoid(_dot(xcb, bdr_ref[...]) + br_ref[...])
    ig = _sigmoid(_dot(xcb, bdi_ref[...]) + bi_ref[...])
    spa = _softplus(-la_ref[...])
    log_a = (-LRU_C) * r * spa
    a = jnp.exp(log_a)
    mult = jnp.sqrt(_neg_expm1(2.0 * log_a))
    return dict(lx=lx, lg=lg, gu=gu, gv=gv, s1=s1, s2=s2, s3=s3, xc=xc, xcb=xcb, r=r, ig=ig, spa=spa,
                a=a, mult=mult)


def _gmlp_fwd(gu, gv, ng_ref, nb_ref, ws_ref, bst_ref, sp_scr):
    ts = gu.shape[0]
    u, du = _gelu_and_grad(gu)
    vg, dvg = _gelu_and_grad(gv)
    mu = jnp.mean(vg, axis=-1, keepdims=True)
    vc = vg - mu
    rstd = lax.rsqrt(jnp.mean(vc * vc, axis=-1, keepdims=True) + EPS)
    vhat = vc * rstd
    v = vhat * ng_ref[...] + nb_ref[...]
    vb = v.astype(BF16)
    for n in range(ts // GMLP_BLOCK):
        rs = slice(n * GMLP_BLOCK, (n + 1) * GMLP_BLOCK)
        for g in range(GMLP_GROUPS):
            cs = slice(g * 128, (g + 1) * 128)
            sp_scr[rs, cs] = _dot(ws_ref[g], vb[rs, cs]) + bst_ref[:, g:g + 1]
    spb = sp_scr[...]
    return dict(u=u, du=du, dvg=dvg, rstd=rstd, vhat=vhat, vb=vb, spb=spb, y_g=u * spb)


def _seq_specs(ts, nt, rev):
    rev_of = nt if rev else None
    return [
        _rows(ts, 2048, rev_of),
        _halo_prev(ts, 512, SUBLANES, rev_of),
    ]


def _seq_param_specs():
    return [_full((4, 512)), _full((1, 512)), _full((512, 512)), _full((512, 512)), _full((1, 512)),
            _full((1, 512)), _full((1, 512)), _full((1, 512)), _full((1, 512)), _full((4, 128, 128)),
            _full((128, 4))]


def _seqmix(z, seq_params, glo, ggo, ts=256):
    s = z.shape[0]
    nt = s // ts

    def body(z_ref, zprev_ref, *rest):
        p = rest[:11]
        glo_ref, ggo_ref, ycat_ref, hst_ref, hcarry, sp_scr = rest[11:]
        i = pl.program_id(0)

        @pl.when(i == 0)
        def _():
            hcarry[...] = jnp.zeros_like(hcarry)

        f = _seq_recompute(z_ref, zprev_ref, i == 0, p)
        bx = f["mult"] * (f["ig"] * f["xc"])
        acum, hloc = _scan_fwd(f["a"], bx)
        h = hloc + acum * hcarry[...]
        hcarry[...] = h[ts - 1:ts, :]
        hst_ref[...] = h
        y_l = h * _gelu(f["lg"])
        gm = _gmlp_fwd(f["gu"], f["gv"], p[7], p[8], p[9], p[10], sp_scr)
        y_g = gm["y_g"]
        ycat_ref[:, 0:512] = (y_l * _msq_rsqrt(y_l) * glo_ref[...]).astype(BF16)
        ycat_ref[:, 512:1024] = (y_g * _msq_rsqrt(y_g) * ggo_ref[...]).astype(BF16)

    return pl.pallas_call(
        body, grid=(nt,), name="seqmix",
        in_specs=_seq_specs(ts, nt, False) + _seq_param_specs() + [_full((1, 512)), _full((1, 512))],
        out_specs=[_rows(ts, 1024), _rows(ts, 512)],
        out_shape=[_sds((s, 1024), BF16), _sds((s, 512), F32)],
        scratch_shapes=[pltpu.VMEM((1, 512), F32), pltpu.VMEM((ts, 512), F32)],
        compiler_params=_params(("arbitrary",)),
    )(z, z, *seq_params, glo, ggo)


def _mix_out(ycat, x, w_out, gt_m, g_post, g_pre2, sc_f, sh_f, ts=256):
    s, d = x.shape

    def body(yc_ref, x_ref, w_ref, gt_ref, gp_ref, g2_ref, sc_ref, sh_ref, y_ref, x1_ref, h2_ref):
        y = _dot(yc_ref[...], w_ref[...])
        y_ref[...] = y
        x1 = x_ref[...] + gt_ref[...] * (y * _msq_rsqrt(y) * gp_ref[...])
        x1_ref[...] = x1
        h2 = (x1 * _msq_rsqrt(x1) * g2_ref[...]) * (1.0 + sc_ref[...]) + sh_ref[...]
        h2_ref[...] = h2.astype(BF16)

    vec = _full((1, d))
    return pl.pallas_call(
        body, grid=(s // ts,), name="mix_out",
        in_specs=[_rows(ts, d), _rows(ts, d), _full((d, d)), vec, vec, vec, vec, vec],
        out_specs=[_rows(ts, d), _rows(ts, d), _rows(ts, d)],
        out_shape=[_sds((s, d), F32), _sds((s, d), F32), _sds((s, d), BF16)],
        compiler_params=_params(("parallel",)),
    )(ycat, x, w_out, gt_m, g_post, g_pre2, sc_f, sh_f)


def _ffn_cols(j):
    per = (2 * D_FF // N_CHIPS) // FFN_CHUNK
    return j // per, (j % per) * FFN_CHUNK, j * FFN_CHUNK


def _ffn_fwd(h2, x1, tgt, w_up4, w_down, fw, fb, gt_f, g_post, ts=256):
    s, d = x1.shape
    nch = D_FF // FFN_CHUNK

    def body(h2_ref, x1_ref, tgt_ref, wup_ref, wdn_ref, fw_ref, fb_ref, gt_ref, gp_ref,
             up0_ref, act_ref, dy2_ref, dx2_ref, loss_ref, dgt_ref, dgp_ref, tail_ref):
        i = pl.program_id(0)

        @pl.when(i == 0)
        def _():
            tail_ref[...] = jnp.zeros_like(tail_ref)
            loss_ref[...] = jnp.zeros_like(loss_ref)
            dgt_ref[...] = jnp.zeros_like(dgt_ref)
            dgp_ref[...] = jnp.zeros_like(dgp_ref)

        hb = h2_ref[...]
        y2 = jnp.zeros((ts, d), F32)
        for j in range(nch):
            sh_g, off, col = _ffn_cols(j)
            halves = []
            for shard, c0 in ((sh_g, col), (sh_g + 2, D_FF + col)):
                cs = slice(c0, c0 + FFN_CHUNK)
                ub = _dot(hb, wup_ref[shard, :, off:off + FFN_CHUNK]).astype(BF16)
                up0_ref[:, cs] = ub
                u = ub.astype(F32)
                prev8 = tail_ref[:, cs]
                tail_ref[:, cs] = u[ts - SUBLANES:, :]
                halves.append(fw_ref[2:3, cs] * u + fw_ref[1:2, cs] * _shift_down(u, prev8, 1)
                              + fw_ref[0:1, cs] * _shift_down(u, prev8, 2) + fb_ref[:, cs])
            act = (_gelu(halves[0]) * halves[1]).astype(BF16)
            act_ref[:, col:col + FFN_CHUNK] = act
            y2 = y2 + _dot(act, wdn_ref[col:col + FFN_CHUNK, :])
        r2 = _msq_rsqrt(y2)
        yn = y2 * r2
        yng = yn * gp_ref[...]
        e = x1_ref[...] + gt_ref[...] * yng - tgt_ref[...]
        loss_ref[...] += jnp.sum(e * e) * (0.5 / d)
        dx2 = e * (1.0 / d)
        dx2_ref[...] = dx2
        dgt_ref[...] += _colsum(dx2 * yng)
        dyng = dx2 * gt_ref[...]
        dgp_ref[...] += _colsum(dyng * yn)
        dy2_ref[...] = _rms_bwd(dyng * gp_ref[...], yn, r2).astype(BF16)

    vec = _full((1, d))
    return pl.pallas_call(
        body, grid=(s // ts,), name="ffn_fwd",
        in_specs=[_rows(ts, d), _rows(ts, d), _rows(ts, d), _RESIDENT, _RESIDENT,
                  _full((3, 2 * D_FF)), _full((1, 2 * D_FF)), vec, vec],
        out_specs=[_rows(ts, 2 * D_FF), _rows(ts, D_FF), _rows(ts, d), _rows(ts, d),
                   _full((1, 128)), vec, vec],
        out_shape=[_sds((s, 2 * D_FF), BF16), _sds((s, D_FF), BF16), _sds((s, d), BF16), _sds((s, d), F32),
                   _sds((1, 128), F32), _sds((1, d), F32), _sds((1, d), F32)],
        scratch_shapes=[pltpu.VMEM((SUBLANES, 2 * D_FF), F32)],
        compiler_params=_params(("arbitrary",)),
    )(h2, x1, tgt, w_up4, w_down, fw, fb, gt_f, g_post)


def _ffn_bwd_a(dy2, up0, w_down, fw, fb, ts=256):
    s, d = dy2.shape
    nt = s // ts
    nch = D_FF // FFN_CHUNK
    wide = 2 * D_FF

    def body(dy2_ref, up0_ref, upprev_ref, wdn_ref, fw_ref, fb_ref, dup0_ref, dfw_ref, dfb_ref, next_ref):
        i = pl.program_id(0)

        @pl.when(i == 0)
        def _():
            next_ref[...] = jnp.zeros_like(next_ref)
            dfw_ref[...] = jnp.zeros_like(dfw_ref)
            dfb_ref[...] = jnp.zeros_like(dfb_ref)

        first_tile = i == nt - 1
        dyb = dy2_ref[...]
        for j in range(nch):
            _, _, col = _ffn_cols(j)
            dact = _dot_nt(dyb, wdn_ref[col:col + FFN_CHUNK, :])
            pre, shifted = [], []
            for c0 in (col, D_FF + col):
                cs = slice(c0, c0 + FFN_CHUNK)
                u = up0_ref[:, cs].astype(F32)
                prev8 = jnp.where(first_tile, 0.0, upprev_ref[:, cs].astype(F32)[SUBLANES:, :])
                sd = (u, _shift_down(u, prev8, 1), _shift_down(u, prev8, 2))
                shifted.append(sd)
                pre.append(fw_ref[2:3, cs] * sd[0] + fw_ref[1:2, cs] * sd[1] + fw_ref[0:1, cs] * sd[2]
                           + fb_ref[:, cs])
            gl, dgl = _gelu_and_grad(pre[0])
            dpre = (dact * pre[1] * dgl, dact * gl)
            for half, c0 in enumerate((col, D_FF + col)):
                cs = slice(c0, c0 + FFN_CHUNK)
                dp = dpre[half]
                sd = shifted[half]
                dfb_ref[:, cs] += _colsum(dp)
                dfw_ref[2:3, cs] += _colsum(dp * sd[0])
                dfw_ref[1:2, cs] += _colsum(dp * sd[1])
                dfw_ref[0:1, cs] += _colsum(dp * sd[2])
                nxt = next_ref[:, cs]
                next_ref[:, cs] = dp[0:SUBLANES, :]
                dup0 = (fw_ref[2:3, cs] * dp + fw_ref[1:2, cs] * _shift_up(dp, nxt, 1)
                        + fw_ref[0:1, cs] * _shift_up(dp, nxt, 2))
                dup0_ref[:, cs] = dup0.astype(BF16)

    return pl.pallas_call(
        body, grid=(nt,), name="ffn_bwd_a",
        in_specs=[_rows(ts, d, nt), _rows(ts, wide, nt), _halo_prev(ts, wide, BF16_SUBLANES, nt), _RESIDENT,
                  _full((3, wide)), _full((1, wide))],
        out_specs=[_rows(ts, wide, nt), _full((3, wide)), _full((1, wide))],
        out_shape=[_sds((s, wide), BF16), _sds((3, wide), F32), _sds((1, wide), F32)],
        scratch_shapes=[pltpu.VMEM((SUBLANES, wide), F32)],
        compiler_params=_params(("arbitrary",)),
    )(dy2, up0, up0, w_down, fw, fb)


def _ffn_bwd_b(dup0, x1, y, dx2, w_up4, g_pre2, sc_f, sh_f, gt_m, g_post_m, ts=256):
    s, d = x1.shape
    shard_cols = 2 * D_FF // N_CHIPS

    def body(dup_ref, x1_ref, y_ref, dx2_ref, wup_ref, g2_ref, sc_ref, sh_ref, gt_ref, gp_ref,
             dx1_ref, dy_ref, dsh_ref, dsc_ref, dg2_ref, dgt_ref, dgp_ref):
        i = pl.program_id(0)

        @pl.when(i == 0)
        def _():
            for ref in (dsh_ref, dsc_ref, dg2_ref, dgt_ref, dgp_ref):
                ref[...] = jnp.zeros_like(ref)

        dh2 = jnp.zeros((ts, d), F32)
        for k in range(N_CHIPS):
            dh2 = dh2 + _dot_nt(dup_ref[:, k * shard_cols:(k + 1) * shard_cols], wup_ref[k])
        x1v = x1_ref[...]
        r2 = _msq_rsqrt(x1v)
        xn = x1v * r2
        hn = xn * g2_ref[...]
        dsh_ref[...] += _colsum(dh2)
        dsc_ref[...] += _colsum(dh2 * hn)
        dhn = dh2 * (1.0 + sc_ref[...])
        dg2_ref[...] += _colsum(dhn * xn)
        dx1 = dx2_ref[...] + _rms_bwd(dhn * g2_ref[...], xn, r2)
        dx1_ref[...] = dx1
        yv = y_ref[...]
        ry = _msq_rsqrt(yv)
        yn = yv * ry
        dgt_ref[...] += _colsum(dx1 * (yn * gp_ref[...]))
        dyng = dx1 * gt_ref[...]
        dgp_ref[...] += _colsum(dyng * yn)
        dy_ref[...] = _rms_bwd(dyng * gp_ref[...], yn, ry).astype(BF16)

    vec = _full((1, d))
    return pl.pallas_call(
        body, grid=(s // ts,), name="ffn_bwd_b",
        in_specs=[_rows(ts, 2 * D_FF), _rows(ts, d), _rows(ts, d), _rows(ts, d), _RESIDENT,
                  vec, vec, vec, vec, vec],
        out_specs=[_rows(ts, d), _rows(ts, d), vec, vec, vec, vec, vec],
        out_shape=[_sds((s, d), F32), _sds((s, d), BF16)] + [_sds((1, d), F32)] * 5,
        compiler_params=_params(("arbitrary",)),
    )(dup0, x1, y, dx2, w_up4, g_pre2, sc_f, sh_f, gt_m, g_post_m)


def _seqmix_bwd(z, hst, dy, w_out, seq_params, ws_t, glo, ggo, ts=256):
    s = z.shape[0]
    nt = s // ts
    small_shapes = [(4, 512), (1, 512), (512, 512), (512, 512), (1, 512), (1, 512), (1, 512),
                    (1, 512), (1, 512), (4, 128, 128), (128, 4), (1, 512), (1, 512)]

    def body(z_ref, zprev_ref, hst_ref, hprev_ref, dy_ref, wout_ref, *rest):
        p = rest[:11]
        wst_ref, glo_ref, ggo_ref = rest[11:14]
        dz_ref = rest[14]
        (dcw_ref, dcb_ref, dwr_ref, dwi_ref, dbr_ref, dbi_ref, dspa_ref, dng_ref, dnb_ref, dws_ref, dbs_ref,
         dglo_ref, dggo_ref) = rest[15:28]
        gcarry, anext, dxcnext, sp_scr, dv_scr = rest[28:]
        i = pl.program_id(0)

        @pl.when(i == 0)
        def _():
            for ref in rest[15:28]:
                ref[...] = jnp.zeros_like(ref)
            gcarry[...] = jnp.zeros_like(gcarry)
            anext[...] = jnp.ones_like(anext)
            dxcnext[...] = jnp.zeros_like(dxcnext)

        first_tile = i == nt - 1
        f = _seq_recompute(z_ref, zprev_ref, first_tile, p)
        xc, r, ig, a, mult, lx = f["xc"], f["r"], f["ig"], f["a"], f["mult"], f["lx"]
        h = hst_ref[...]
        hprev = _shift_down(h, jnp.where(first_tile, 0.0, hprev_ref[...]), 1)
        gl, dgl = _gelu_and_grad(f["lg"])
        y_l = h * gl
        gm = _gmlp_fwd(f["gu"], f["gv"], p[7], p[8], p[9], p[10], sp_scr)
        y_g = gm["y_g"]

        dycat = _dot_nt(dy_ref[...], wout_ref[...])
        rl = _msq_rsqrt(y_l)
        yln = y_l * rl
        dyl = dycat[:, 0:512]
        dglo_ref[...] += _colsum(dyl * yln)
        dy_l = _rms_bwd(dyl * glo_ref[...], yln, rl)
        rg = _msq_rsqrt(y_g)
        ygn = y_g * rg
        dyg = dycat[:, 512:1024]
        dggo_ref[...] += _colsum(dyg * ygn)
        dy_g = _rms_bwd(dyg * ggo_ref[...], ygn, rg)

        dz_ref[:, 512:1024] = (dy_l * h * dgl).astype(BF16)
        a_up = _shift_up(a, anext[...], 1)
        acum, gloc = _scan_bwd(a_up, dy_l * gl)
        gg = gloc + acum * gcarry[...]
        gcarry[...] = gg[0:1, :]
        anext[...] = a[0:SUBLANES, :]
        da = gg * hprev
        t1 = gg * mult
        di = t1 * xc
        dxc = t1 * ig
        dmult = gg * ig * xc
        dla = da * a - dmult * (a * a / mult)
        spa = f["spa"]
        dspa_ref[...] += _colsum(dla * r) * (-LRU_C)
        dpr = dla * ((-LRU_C) * spa) * r * (1.0 - r)
        dpi = di * ig * (1.0 - ig)
        dbr_ref[...] += _colsum(dpr)
        dbi_ref[...] += _colsum(dpi)
        dprb = dpr.astype(BF16)
        dpib = dpi.astype(BF16)
        dwr_ref[...] += _dot_tn(f["xcb"], dprb)
        dwi_ref[...] += _dot_tn(f["xcb"], dpib)
        dxc = dxc + _dot_nt(dprb, p[2][...]) + _dot_nt(dpib, p[3][...])
        dcb_ref[...] += _colsum(dxc)
        dcw_ref[3:4, :] += _colsum(dxc * lx)
        dcw_ref[2:3, :] += _colsum(dxc * f["s1"])
        dcw_ref[1:2, :] += _colsum(dxc * f["s2"])
        dcw_ref[0:1, :] += _colsum(dxc * f["s3"])
        nxt = dxcnext[...]
        dxcnext[...] = dxc[0:SUBLANES, :]
        cw_ref = p[0]
        dlx = (cw_ref[3:4, :] * dxc + cw_ref[2:3, :] * _shift_up(dxc, nxt, 1)
               + cw_ref[1:2, :] * _shift_up(dxc, nxt, 2) + cw_ref[0:1, :] * _shift_up(dxc, nxt, 3))
        dz_ref[:, 0:512] = dlx.astype(BF16)

        dz_ref[:, 1024:1536] = (dy_g * gm["spb"] * gm["du"]).astype(BF16)
        dsp = dy_g * gm["u"]
        vb = gm["vb"]
        for n in range(ts // GMLP_BLOCK):
            rs = slice(n * GMLP_BLOCK, (n + 1) * GMLP_BLOCK)
            for g in range(GMLP_GROUPS):
                cs = slice(g * 128, (g + 1) * 128)
                dbs_ref[:, g:g + 1] += jnp.sum(dsp[rs, cs], axis=1, keepdims=True)
                blk = dsp[rs, cs].astype(BF16)
                dws_ref[g] += _dot_nt(blk, vb[rs, cs])
                dv_scr[rs, cs] = _dot(wst_ref[g], blk)
        dv = dv_scr[...]
        vhat = gm["vhat"]
        dng_ref[...] += _colsum(dv * vhat)
        dnb_ref[...] += _colsum(dv)
        dvh = dv * p[7][...]
        dvg = gm["rstd"] * (dvh - jnp.mean(dvh, axis=-1, keepdims=True)
                            - vhat * jnp.mean(dvh * vhat, axis=-1, keepdims=True))
        dz_ref[:, 1536:2048] = (dvg * gm["dvg"]).astype(BF16)

        @pl.when(i == nt - 1)
        def _():
            pos = lax.broadcasted_iota(jnp.int32, (GMLP_BLOCK, GMLP_BLOCK), 0) // CHUNK
            src = lax.broadcasted_iota(jnp.int32, (GMLP_BLOCK, GMLP_BLOCK), 1) // CHUNK
            for g in range(GMLP_GROUPS):
                dws_ref[g] = jnp.where(src <= pos, dws_ref[g], 0.0)
            dspa_ref[...] = dspa_ref[...] * (-_sigmoid(-p[6][...]))

    in_specs = (_seq_specs(ts, nt, True)
                + [_rows(ts, 512, nt), _halo_prev(ts, 512, SUBLANES, nt), _rows(ts, 1024, nt), _full((1024, 1024))]
                + _seq_param_specs() + [_full((4, 128, 128)), _full((1, 512)), _full((1, 512))])
    return pl.pallas_call(
        body, grid=(nt,), name="seqmix_bwd",
        in_specs=in_specs,
        out_specs=[_rows(ts, 2048, nt)] + [_full(sh) for sh in small_shapes],
        out_shape=[_sds((s, 2048), BF16)] + [_sds(sh, F32) for sh in small_shapes],
        scratch_shapes=[pltpu.VMEM((1, 512), F32), pltpu.VMEM((SUBLANES, 512), F32),
                        pltpu.VMEM((SUBLANES, 512), F32), pltpu.VMEM((ts, 512), F32), pltpu.VMEM((ts, 512), F32)],
        compiler_params=_params(("arbitrary",)),
    )(z, z, hst, hst, dy, w_out, *seq_params, ws_t, glo, ggo)


def _mix_in_bwd(x, dz, dx1, w_in4, g, sc, ts=256):
    s, d = x.shape

    def body(x_ref, dz_ref, dx1_ref, w_ref, g_ref, sc_ref, gx_ref, dsh_ref, dsc_ref, dg_ref):
        i = pl.program_id(0)

        @pl.when(i == 0)
        def _():
            for ref in (dsh_ref, dsc_ref, dg_ref):
                ref[...] = jnp.zeros_like(ref)

        dh = jnp.zeros((ts, d), F32)
        for k in range(N_CHIPS):
            dh = dh + _dot_nt(dz_ref[:, k * 512:(k + 1) * 512], w_ref[k])
        xv = x_ref[...]
        r = _msq_rsqrt(xv)
        xn = xv * r
        dsh_ref[...] += _colsum(dh)
        dsc_ref[...] += _colsum(dh * (xn * g_ref[...]))
        dhn = dh * (1.0 + sc_ref[...])
        dg_ref[...] += _colsum(dhn * xn)
        gx_ref[...] = dx1_ref[...] + _rms_bwd(dhn * g_ref[...], xn, r)

    vec = _full((1, d))
    return pl.pallas_call(
        body, grid=(s // ts,), name="mix_in_bwd",
        in_specs=[_rows(ts, d), _rows(ts, 2048), _rows(ts, d), _full(w_in4.shape), vec, vec],
        out_specs=[_rows(ts, d), vec, vec, vec],
        out_shape=[_sds((s, d), F32)] + [_sds((1, d), F32)] * 3,
        compiler_params=_params(("arbitrary",)),
    )(x, dz, dx1, w_in4, g, sc)


def _wgrad(a, b, n_chunks, name, chunk_major, ts=512):
    s, m = a.shape
    n = b.shape[1]
    nc = n // n_chunks
    nt = s // ts

    def body(a_ref, b_ref, o_ref, acc):
        i = pl.program_id(1)

        @pl.when(i == 0)
        def _():
            acc[...] = jnp.zeros_like(acc)

        acc[...] += _dot_tn(a_ref[...], b_ref[...])

        @pl.when(i == nt - 1)
        def _():
            if chunk_major:
                o_ref[0] = acc[...].astype(BF16)
            else:
                o_ref[...] = acc[...].astype(BF16)

    if chunk_major:
        out_spec, out_shape = pl.BlockSpec((1, m, nc), lambda c, i: (c, 0, 0)), _sds((n_chunks, m, nc), BF16)
    else:
        out_spec, out_shape = pl.BlockSpec((m, nc), lambda c, i: (0, c)), _sds((m, n), BF16)
    return pl.pallas_call(
        body, grid=(n_chunks, nt), name=name,
        in_specs=[pl.BlockSpec((ts, m), lambda c, i: (i, 0)), pl.BlockSpec((ts, nc), lambda c, i: (i, c))],
        out_specs=out_spec,
        out_shape=out_shape,
        scratch_shapes=[pltpu.VMEM((m, nc), F32)],
        compiler_params=_params(("parallel", "arbitrary")),
    )(a, b)


def _block_diag(w):
    heads, hd, _ = w.shape
    eye = jnp.eye(heads, dtype=w.dtype)
    return (eye[:, None, :, None] * w[:, :, None, :]).reshape(heads * hd, heads * hd)


def _diag_blocks(m):
    hd = LRU_WIDTH // LRU_HEADS
    m4 = m.reshape(LRU_HEADS, hd, LRU_HEADS, hd)
    return jnp.stack([m4[k, :, k, :] for k in range(LRU_HEADS)])


def _local_step(x, tgt, mod, w_in4, w_out_b, w_up4, w_down_b, small):
    sh_m, sc_m, gt_m, sh_f, sc_f, gt_f = [mod[k:k + 1] for k in range(N_MOD)]
    row = lambda v: v.reshape(1, -1)
    pos = jnp.arange(GMLP_BLOCK)
    mask = (pos[None, :] // CHUNK) <= (pos[:, None] // CHUNK)
    ws = jnp.where(mask[None], small["w_spatial"], 0.0)
    seq_params = (small["conv_w"], row(small["conv_b"]),
                  _block_diag(small["w_rgate"]).astype(BF16), _block_diag(small["w_igate"]).astype(BF16),
                  row(small["b_rgate"]), row(small["b_igate"]), row(small["lru_a"]),
                  row(small["v_norm_g"]), row(small["v_norm_b"]), ws.astype(BF16), small["b_spatial"].T)
    ws_t = jnp.swapaxes(ws, 1, 2).astype(BF16)
    glo, ggo = row(small["g_lru_out"]), row(small["g_gmlp_out"])
    g_pre, g_post = row(small["g_mix_pre"]), row(small["g_mix_post"])
    g_pre2, g_post2 = row(small["g_ffn_pre"]), row(small["g_ffn_post"])
    fw, fb = small["ffn_conv_w"], row(small["ffn_conv_b"])

    z, h = _mix_in(x, sc_m, sh_m, g_pre, w_in4)
    ycat, hst = _seqmix(z, seq_params, glo, ggo)
    y, x1, h2 = _mix_out(ycat, x, w_out_b, gt_m, g_post, g_pre2, sc_f, sh_f)
    up0, act, dy2, dx2, loss, dgt_f, dg_post2 = _ffn_fwd(h2, x1, tgt, w_up4, w_down_b, fw, fb, gt_f, g_post2)

    dup0, dfw, dfb = _ffn_bwd_a(dy2, up0, w_down_b, fw, fb)
    dx1, dy, dsh_f, dsc_f, dg_pre2, dgt_m, dg_post = _ffn_bwd_b(dup0, x1, y, dx2, w_up4, g_pre2, sc_f, sh_f,
                                                                gt_m, g_post)
    (dz, dcw, dcb, dwr, dwi, dbr, dbi, dspa, dng, dnb, dws, dbs_t, dglo, dggo) = _seqmix_bwd(
        z, hst, dy, w_out_b, seq_params, ws_t, glo, ggo)
    gx, dsh_m, dsc_m, dg_pre = _mix_in_bwd(x, dz, dx1, w_in4, g_pre, sc_m)

    gw_in = _wgrad(h, dz, N_CHIPS, "wgrad_in", True)
    gw_out = _wgrad(ycat, dy, 1, "wgrad_out", False)
    gw_up = _wgrad(h2, dup0, N_CHIPS, "wgrad_up", True)
    gw_down = _wgrad(act, dy2, 2, "wgrad_down", False)

    small_grads = dict(
        g_mix_pre=dg_pre[0], g_mix_post=dg_post[0], conv_w=dcw, conv_b=dcb[0],
        w_rgate=_diag_blocks(dwr), b_rgate=dbr.reshape(LRU_HEADS, -1),
        w_igate=_diag_blocks(dwi), b_igate=dbi.reshape(LRU_HEADS, -1), lru_a=dspa[0],
        v_norm_g=dng[0], v_norm_b=dnb[0], w_spatial=dws, b_spatial=dbs_t.T,
        g_lru_out=dglo[0], g_gmlp_out=dggo[0], g_ffn_pre=dg_pre2[0], g_ffn_post=dg_post2[0],
        ffn_conv_w=dfw, ffn_conv_b=dfb[0])
    dmod = jnp.concatenate([dsh_m, dsc_m, dgt_m, dsh_f, dsc_f, dgt_f], axis=1)
    return loss, gx, (gw_in, gw_out, gw_up, gw_down), small_grads, dmod


_ANY = pl.BlockSpec(memory_space=pl.ANY)
_CHIP_FLIPS = ((1, 0), (0, 1), (1, 1))


def _position():
    return lax.axis_index("x"), lax.axis_index("y"), lax.axis_index("c")


def _flip(v, f):
    return 1 - v if f else v


def _remote(src, dst, send_sem, recv_sem, peer):
    return pltpu.make_async_remote_copy(src_ref=src, dst_ref=dst, send_sem=send_sem, recv_sem=recv_sem,
                                        device_id=peer, device_id_type=MESH)


def _allgather8(block, name, reduce):
    r, n = block.shape

    def body(x_ref, out_ref, *scratch):
        if reduce:
            gath, send_sems, recv_sems, loc_sem = scratch
        else:
            gath = out_ref
            send_sems, recv_sems, loc_sem = scratch
        x, y, c = _position()
        me = 4 * x + 2 * y + c
        loc = pltpu.make_async_copy(x_ref, gath.at[me], loc_sem)
        loc.start()
        peers = []
        for k in range(1, N_DEV):
            px, py, pc = _flip(x, k & 4), _flip(y, k & 2), _flip(c, k & 1)
            peers.append((px, py, pc))
            _remote(x_ref, gath.at[me], send_sems.at[k - 1], recv_sems.at[k - 1], (px, py, pc)).start()
        for k, (px, py, pc) in enumerate(peers):
            src = 4 * px + 2 * py + pc
            _remote(x_ref, gath.at[src], send_sems.at[k], recv_sems.at[k], (px, py, pc)).wait_recv()
        for k, peer in enumerate(peers):
            _remote(x_ref, gath.at[me], send_sems.at[k], recv_sems.at[k], peer).wait_send()
        loc.wait()
        if reduce:
            acc = gath[0]
            for k in range(1, N_DEV):
                acc = acc + gath[k]
            out_ref[...] = acc

    sems = [pltpu.SemaphoreType.DMA((N_DEV - 1,)), pltpu.SemaphoreType.DMA((N_DEV - 1,)), pltpu.SemaphoreType.DMA]
    if reduce:
        out_shape = _sds((r, n), F32)
        scratch = [pltpu.VMEM((N_DEV, r, n), F32)] + sems
    else:
        out_shape = _sds((N_DEV, r, n), F32)
        scratch = sems
    return pl.pallas_call(
        body, name=name, out_shape=out_shape,
        in_specs=[pl.BlockSpec(memory_space=pltpu.VMEM)], out_specs=pl.BlockSpec(memory_space=pltpu.VMEM),
        scratch_shapes=scratch,
        compiler_params=pltpu.CompilerParams(vmem_limit_bytes=VMEM_LIMIT_BYTES),
    )(block)


def _half(ref, c, rows):
    hr = rows // 2
    return ref.at[pl.ds(pl.multiple_of(c * hr, BF16_SUBLANES), hr), :]


def _gather_weights(shards):
    na = len(shards)

    def body(*refs):
        ins, outs = refs[:na], refs[na:2 * na]
        ici_send, ici_recv, d2d_send, d2d_recv, loc_sem = refs[2 * na:]
        x, y, c = _position()
        chip = 2 * x + y
        sibling = (x, y, 1 - c)
        local = []
        for a in range(na):
            local.append(pltpu.make_async_copy(ins[a], outs[a].at[chip], loc_sem.at[a]))
            local[-1].start()
        sends = []
        for a in range(na):
            rows = shards[a].shape[0]
            for j, (fx, fy) in enumerate(_CHIP_FLIPS):
                peer = (_flip(x, fx), _flip(y, fy), c)
                sends.append(_remote(_half(ins[a], c, rows), _half(outs[a].at[chip], c, rows),
                                     ici_send.at[a * 3 + j], ici_recv.at[a * 3 + j], peer))
                sends[-1].start()
        for a in range(na):
            rows = shards[a].shape[0]
            for j, (fx, fy) in enumerate(_CHIP_FLIPS):
                src_chip = 2 * _flip(x, fx) + _flip(y, fy)
                landed = _half(outs[a].at[src_chip], c, rows)
                _remote(landed, landed, ici_send.at[a * 3 + j], ici_recv.at[a * 3 + j], sibling).wait_recv()
                sends.append(_remote(landed, landed, d2d_send.at[a * 3 + j], d2d_recv.at[a * 3 + j], sibling))
                sends[-1].start()
        for a in range(na):
            rows = shards[a].shape[0]
            for j, (fx, fy) in enumerate(_CHIP_FLIPS):
                src_chip = 2 * _flip(x, fx) + _flip(y, fy)
                other = _half(outs[a].at[src_chip], 1 - c, rows)
                _remote(other, other, d2d_send.at[a * 3 + j], d2d_recv.at[a * 3 + j], sibling).wait_recv()
        for cp in sends:
            cp.wait_send()
        for cp in local:
            cp.wait()

    return pl.pallas_call(
        body, name="gather_weights",
        out_shape=[_sds((N_CHIPS,) + w.shape, w.dtype) for w in shards],
        in_specs=[_ANY] * na, out_specs=[_ANY] * na,
        scratch_shapes=[pltpu.SemaphoreType.DMA((3 * na,))] * 4 + [pltpu.SemaphoreType.DMA((na,))],
    )(*shards)


def _swap_halves(parts):
    na = len(parts)

    def body(*refs):
        ins, outs = refs[:na], refs[na:2 * na]
        send_sems, recv_sems = refs[2 * na:]
        x, y, c = _position()
        sibling = (x, y, 1 - c)
        cps = []
        for a in range(na):
            hr = parts[a].shape[1] // 2
            src = ins[a].at[:, pl.ds(pl.multiple_of((1 - c) * hr, BF16_SUBLANES), hr), :]
            cps.append(_remote(src, outs[a], send_sems.at[a], recv_sems.at[a], sibling))
            cps[-1].start()
        for cp in cps:
            cp.wait()

    return pl.pallas_call(
        body, name="swap_halves",
        out_shape=[_sds((N_CHIPS, p.shape[1] // 2, p.shape[2]), p.dtype) for p in parts],
        in_specs=[_ANY] * na, out_specs=[_ANY] * na,
        scratch_shapes=[pltpu.SemaphoreType.DMA((na,))] * 2,
    )(*parts)


def _chip_sum(part, recv, c_arr, name):
    _, rows, cols = part.shape
    hr = rows // 2

    def body(c_ref, p_ref, r_ref, o_ref):
        o_ref[...] = (p_ref[...].astype(F32) + r_ref[...].astype(F32)).astype(BF16)

    grid_spec = pltpu.PrefetchScalarGridSpec(
        num_scalar_prefetch=1, grid=(N_CHIPS,),
        in_specs=[pl.BlockSpec((1, hr, cols), lambda k, c_ref: (k, c_ref[0], 0)),
                  pl.BlockSpec((1, hr, cols), lambda k, c_ref: (k, 0, 0))],
        out_specs=pl.BlockSpec((1, hr, cols), lambda k, c_ref: (k, 0, 0)))
    return pl.pallas_call(
        body, name=name, grid_spec=grid_spec, out_shape=_sds((N_CHIPS, hr, cols), BF16),
        compiler_params=_params(("arbitrary",)),
    )(c_arr, part, recv)


def _exchange_chips(sums):
    na = len(sums)

    def body(*refs):
        ins, outs = refs[:na], refs[na:2 * na]
        send_sems, recv_sems, loc_sem = refs[2 * na:]
        x, y, c = _position()
        chip = 2 * x + y
        local = []
        for a in range(na):
            local.append(pltpu.make_async_copy(ins[a].at[chip], outs[a].at[chip], loc_sem.at[a]))
            local[-1].start()
        cps = []
        for a in range(na):
            for j, (fx, fy) in enumerate(_CHIP_FLIPS):
                px, py = _flip(x, fx), _flip(y, fy)
                cps.append(_remote(ins[a].at[2 * px + py], outs[a].at[chip],
                                   send_sems.at[a * 3 + j], recv_sems.at[a * 3 + j], (px, py, c)))
                cps[-1].start()
        for a in range(na):
            for j, (fx, fy) in enumerate(_CHIP_FLIPS):
                src_chip = 2 * _flip(x, fx) + _flip(y, fy)
                landed = outs[a].at[src_chip]
                _remote(landed, landed, send_sems.at[a * 3 + j], recv_sems.at[a * 3 + j], (x, y, c)).wait_recv()
        for cp in cps:
            cp.wait_send()
        for cp in local:
            cp.wait()

    return pl.pallas_call(
        body, name="exchange_chips",
        out_shape=[_sds(s.shape, s.dtype) for s in sums],
        in_specs=[_ANY] * na, out_specs=[_ANY] * na,
        scratch_shapes=[pltpu.SemaphoreType.DMA((3 * na,))] * 2 + [pltpu.SemaphoreType.DMA((na,))],
    )(*sums)


def _sum_chips(gath, name, tr=128):
    _, hr, cols = gath.shape
    tr = min(tr, hr)

    def body(g_ref, o_ref):
        acc = g_ref[0].astype(F32)
        for k in range(1, N_CHIPS):
            acc = acc + g_ref[k].astype(F32)
        o_ref[...] = acc

    return pl.pallas_call(
        body, name=name, grid=(hr // tr,),
        in_specs=[pl.BlockSpec((N_CHIPS, tr, cols), lambda i: (0, i, 0))],
        out_specs=pl.BlockSpec((tr, cols), lambda i: (i, 0)),
        out_shape=_sds((hr, cols), F32),
        compiler_params=_params(("parallel",)),
    )(gath)


def _join_halves(halves):
    na = len(halves)

    def body(*refs):
        ins, outs = refs[:na], refs[na:2 * na]
        send_sems, recv_sems, loc_sem = refs[2 * na:]
        x, y, c = _position()
        sibling = (x, y, 1 - c)
        cps, local = [], []
        for a in range(na):
            rows = 2 * halves[a].shape[0]
            mine = _half(outs[a], c, rows)
            local.append(pltpu.make_async_copy(ins[a], mine, loc_sem.at[a]))
            local[-1].start()
            cps.append(_remote(ins[a], mine, send_sems.at[a], recv_sems.at[a], sibling))
            cps[-1].start()
        for a in range(na):
            rows = 2 * halves[a].shape[0]
            other = _half(outs[a], 1 - c, rows)
            _remote(ins[a], other, send_sems.at[a], recv_sems.at[a], sibling).wait_recv()
        for cp in cps:
            cp.wait_send()
        for cp in local:
            cp.wait()

    return pl.pallas_call(
        body, name="join_halves",
        out_shape=[_sds((2 * h.shape[0], h.shape[1]), h.dtype) for h in halves],
        in_specs=[_ANY] * na, out_specs=[_ANY] * na,
        scratch_shapes=[pltpu.SemaphoreType.DMA((na,))] * 3,
    )(*halves)


def _silu(v):
    return v * _sigmoid(v)


def _ada_fwd(c8, w_ada):
    def body(c_ref, w_ref, o_ref):
        o_ref[...] = jnp.dot(_silu(c_ref[...]), w_ref[...], preferred_element_type=F32,
                             precision=lax.Precision.HIGHEST)

    return pl.pallas_call(
        body, name="ada_fwd", out_shape=_sds((N_DEV, w_ada.shape[1]), F32),
        compiler_params=pltpu.CompilerParams(vmem_limit_bytes=VMEM_LIMIT_BYTES),
    )(c8, w_ada)


def _mod_select(parts, b_ada, me_arr):
    cols = parts.shape[2]

    def body(me_ref, p_ref, b_ref, o_ref):
        me = me_ref[0]
        for k in range(N_CHIPS):
            cs = slice(k * cols, (k + 1) * cols)
            o_ref[:, cs] = p_ref[2 * k, pl.ds(me, 1), :] + b_ref[:, cs]

    grid_spec = pltpu.PrefetchScalarGridSpec(
        num_scalar_prefetch=1, grid=(1,),
        in_specs=[pl.BlockSpec(parts.shape, lambda i, m: (0, 0, 0)), pl.BlockSpec(b_ada.shape, lambda i, m: (0, 0))],
        out_specs=pl.BlockSpec(b_ada.shape, lambda i, m: (0, 0)))
    return pl.pallas_call(body, name="mod_select", grid_spec=grid_spec, out_shape=_sds(b_ada.shape, F32))(
        me_arr, parts, b_ada)


def _ada_bwd(c8, dmod8, chip_arr):
    d = c8.shape[1]
    cols = dmod8.shape[1] // N_CHIPS

    def body(chip_ref, c_ref, dm_ref, dmall_ref, gw_ref, gb_ref):
        gw_ref[...] = lax.dot_general(_silu(c_ref[...]), dm_ref[...], (((0,), (0,)), ((), ())),
                                      preferred_element_type=F32, precision=lax.Precision.HIGHEST)
        acc = dmall_ref[0:1, :]
        for k in range(1, N_DEV):
            acc = acc + dmall_ref[k:k + 1, :]
        gb_ref[...] = acc

    grid_spec = pltpu.PrefetchScalarGridSpec(
        num_scalar_prefetch=1, grid=(1,),
        in_specs=[pl.BlockSpec(c8.shape, lambda i, ch: (0, 0)),
                  pl.BlockSpec((N_DEV, cols), lambda i, ch: (0, ch[0])),
                  pl.BlockSpec(dmod8.shape, lambda i, ch: (0, 0))],
        out_specs=[pl.BlockSpec((d, cols), lambda i, ch: (0, 0)), pl.BlockSpec((1, dmod8.shape[1]), lambda i, ch: (0, 0))])
    return pl.pallas_call(
        body, name="ada_bwd", grid_spec=grid_spec,
        out_shape=[_sds((d, cols), F32), _sds((1, dmod8.shape[1]), F32)],
        compiler_params=_params(("arbitrary",)),
    )(chip_arr, c8, dmod8, dmod8)


def _adam_math(w, g, m, v):
    m = ADAM_B1 * m + (1.0 - ADAM_B1) * g
    v = ADAM_B2 * v + (1.0 - ADAM_B2) * (g * g)
    m_hat = m / (1.0 - ADAM_B1 ** ADAM_STEP)
    v_hat = v / (1.0 - ADAM_B2 ** ADAM_STEP)
    delta = -ADAM_LR * (m_hat / (jnp.sqrt(v_hat) + ADAM_EPS) + ADAM_WD * w)
    return delta, m, v


def _adam(w, g, m, v, name, tr=256):
    rows, cols = w.shape
    if rows % tr:
        tr = rows

    def body(w_ref, g_ref, m_ref, v_ref, d_ref, nm_ref, nv_ref):
        d_ref[...], nm_ref[...], nv_ref[...] = _adam_math(w_ref[...], g_ref[...], m_ref[...], v_ref[...])

    spec = pl.BlockSpec((tr, cols), lambda i: (i, 0))
    return pl.pallas_call(
        body, name=name, grid=(rows // tr,), in_specs=[spec] * 4, out_specs=[spec] * 3,
        out_shape=[_sds(w.shape, F32)] * 3, compiler_params=_params(("parallel",)),
    )(w, g, m, v)


def _adam_cols(w, g_full, m, v, chip_arr, name):
    rows, cols = w.shape

    def body(chip_ref, w_ref, g_ref, m_ref, v_ref, gs_ref, d_ref, nm_ref, nv_ref):
        g = g_ref[...]
        gs_ref[...] = g
        d_ref[...], nm_ref[...], nv_ref[...] = _adam_math(w_ref[...], g, m_ref[...], v_ref[...])

    own = pl.BlockSpec((rows, cols), lambda i, ch: (0, 0))
    grid_spec = pltpu.PrefetchScalarGridSpec(
        num_scalar_prefetch=1, grid=(1,),
        in_specs=[own, pl.BlockSpec((rows, cols), lambda i, ch: (0, ch[0])), own, own],
        out_specs=[own] * 4)
    return pl.pallas_call(body, name=name, grid_spec=grid_spec, out_shape=[_sds(w.shape, F32)] * 4)(
        chip_arr, w, g_full, m, v)


PACK_COLS = 512
SMALL_REPLICATED = ("g_mix_pre", "g_mix_post", "conv_b", "w_rgate", "b_rgate", "w_igate", "b_igate", "lru_a",
                    "v_norm_g", "v_norm_b", "w_spatial", "b_spatial", "g_lru_out", "g_gmlp_out", "g_ffn_pre",
                    "g_ffn_post", "ffn_conv_b")
SMALL_COLUMN_SHARDED = ("conv_w", "ffn_conv_w")


def _pack(arrays):
    parts = []
    for arr in arrays:
        p = arr.reshape(-1, PACK_COLS)
        pad = (-p.shape[0]) % SUBLANES
        parts.append(jnp.pad(p, ((0, pad), (0, 0))) if pad else p)
    return jnp.concatenate(parts, axis=0)


def _unpack(packed, shapes):
    out, row = [], 0
    for shape in shapes:
        n = math.prod(shape) // PACK_COLS
        out.append(packed[row:row + n].reshape(shape))
        row += n + (-n) % SUBLANES
    return out


def kernel(x, c, w_ada, b_ada, g_mix_pre, g_mix_post, w_in, conv_w, conv_b, w_rgate, b_rgate, w_igate, b_igate, lru_a, v_norm_g, v_norm_b, w_spatial, b_spatial, g_lru_out, g_gmlp_out, w_out, g_ffn_pre, g_ffn_post, w_up, ffn_conv_w, ffn_conv_b, w_down, loss_target, m_w_ada, m_b_ada, m_g_mix_pre, m_g_mix_post, m_w_in, m_conv_w, m_conv_b, m_w_rgate, m_b_rgate, m_w_igate, m_b_igate, m_lru_a, m_v_norm_g, m_v_norm_b, m_w_spatial, m_b_spatial, m_g_lru_out, m_g_gmlp_out, m_w_out, m_g_ffn_pre, m_g_ffn_post, m_w_up, m_ffn_conv_w, m_ffn_conv_b, m_w_down, v_w_ada, v_b_ada, v_g_mix_pre, v_g_mix_post, v_w_in, v_conv_w, v_conv_b, v_w_rgate, v_b_rgate, v_w_igate, v_b_igate, v_lru_a, v_v_norm_g, v_v_norm_b, v_w_spatial, v_b_spatial, v_g_lru_out, v_g_gmlp_out, v_w_out, v_g_ffn_pre, v_g_ffn_post, v_w_up, v_ffn_conv_w, v_ffn_conv_b, v_w_down):
    args = dict(locals())
    names = ("w_ada", "b_ada", "g_mix_pre", "g_mix_post", "w_in", "conv_w", "conv_b", "w_rgate", "b_rgate",
             "w_igate", "b_igate", "lru_a", "v_norm_g", "v_norm_b", "w_spatial", "b_spatial", "g_lru_out",
             "g_gmlp_out", "w_out", "g_ffn_pre", "g_ffn_post", "w_up", "ffn_conv_w", "ffn_conv_b", "w_down")
    w = {n: args[n][0] for n in names}
    m = {n: args["m_" + n][0] for n in names}
    v = {n: args["v_" + n][0] for n in names}
    xi, yi, ci = _position()
    me_arr = jnp.reshape(4 * xi + 2 * yi + ci, (1,)).astype(jnp.int32)
    chip_arr = jnp.reshape(2 * xi + yi, (1,)).astype(jnp.int32)
    c_arr = jnp.reshape(ci, (1,)).astype(jnp.int32)

    big = ("w_in", "w_out", "w_up", "w_down")
    w_in4, w_out4, w_up4, w_down4 = _gather_weights([w[n].astype(BF16) for n in big])
    w_out_b = w_out4.reshape(D_MODEL, D_MODEL)
    w_down_b = w_down4.reshape(D_FF, D_MODEL)

    row0 = jnp.concatenate([c, w["conv_w"].reshape(1, -1), w["ffn_conv_w"].reshape(1, -1)], axis=1)
    g0 = _allgather8(row0, "gather_cond", False)[:, 0, :]
    c8 = g0[:, :D_MODEL]
    per_chip = g0[0::2]
    conv_w_full = per_chip[:, D_MODEL:D_MODEL + 512].reshape(N_CHIPS, 4, 128).transpose(1, 0, 2).reshape(4, 512)
    ffn_conv_w_full = per_chip[:, D_MODEL + 512:].reshape(N_CHIPS, 3, 1536).transpose(1, 0, 2).reshape(3, 2 * D_FF)
    mod_parts = _allgather8(_ada_fwd(c8, w["w_ada"]), "gather_mod", False)
    mod = _mod_select(mod_parts, w["b_ada"].reshape(1, -1), me_arr).reshape(N_MOD, D_MODEL)

    small = {n: w[n] for n in SMALL_REPLICATED}
    small["conv_w"] = conv_w_full
    small["ffn_conv_w"] = ffn_conv_w_full
    loss, grad_x, big_parts, small_grads, dmod = _local_step(
        x[0], loss_target[0], mod, w_in4, w_out_b, w_up4, w_down_b, small)

    dmod8 = _allgather8(dmod, "gather_dmod", False)[:, 0, :]
    g_w_ada, g_b_ada = _ada_bwd(c8, dmod8, chip_arr)

    gw_in, gw_out, gw_up, gw_down = big_parts
    parts = [gw_in, gw_out.reshape(N_CHIPS, -1, D_MODEL), gw_up, gw_down.reshape(N_CHIPS, -1, D_MODEL)]
    recv = _swap_halves(parts)
    sums = [_chip_sum(p, r, c_arr, "chip_sum_" + n) for p, r, n in zip(parts, recv, big)]
    gath = _exchange_chips(sums)
    halves = [_sum_chips(g, "sum_chips_" + n) for g, n in zip(gath, big)]
    g_big = dict(zip(big, _join_halves(halves)))
    g_big["w_ada"] = g_w_ada

    packed_names = SMALL_REPLICATED + SMALL_COLUMN_SHARDED
    g_small = _allgather8(_pack([small_grads[n] for n in packed_names]), "reduce_small", True)
    g_small_list = _unpack(g_small, [small_grads[n].shape for n in packed_names])
    g_rep = dict(zip(packed_names, g_small_list))

    grads, deltas, new_m, new_v = {}, {}, {}, {}
    for n in ("w_ada",) + big:
        grads[n] = g_big[n]
        deltas[n], new_m[n], new_v[n] = _adam(w[n], g_big[n], m[n], v[n], "adam_" + n)
    rep = SMALL_REPLICATED
    d_p, m_p, v_p = _adam(_pack([w[n] for n in rep]), _pack([g_rep[n] for n in rep]),
                          _pack([m[n] for n in rep]), _pack([v[n] for n in rep]), "adam_small")
    shapes = [w[n].shape for n in rep]
    for n, dd, mm, vv in zip(rep, _unpack(d_p, shapes), _unpack(m_p, shapes), _unpack(v_p, shapes)):
        grads[n], deltas[n], new_m[n], new_v[n] = g_rep[n], dd, mm, vv
    for n in SMALL_COLUMN_SHARDED:
        grads[n], deltas[n], new_m[n], new_v[n] = _adam_cols(w[n], g_rep[n], m[n], v[n], chip_arr, "adam_" + n)
    b2 = lambda a: a.reshape(-1, PACK_COLS)
    d_b, m_b, v_b = _adam(b2(w["b_ada"]), b2(g_b_ada), b2(m["b_ada"]), b2(v["b_ada"]), "adam_b_ada")
    grads["b_ada"], deltas["b_ada"], new_m["b_ada"], new_v["b_ada"] = (
        g_b_ada.reshape(-1), d_b.reshape(-1), m_b.reshape(-1), v_b.reshape(-1))

    total = lax.psum(loss[0, 0], ("x", "y", "c"))
    outs = [total, grad_x[None]]
    for group in (grads, deltas, new_m, new_v):
        outs.extend(group[n][None] for n in names)
    return tuple(outs)
```

```python
import functools
import math

import jax
import jax.numpy as jnp
from jax import lax
from jax.experimental import pallas as pl
from jax.experimental.pallas import tpu as pltpu

F32 = jnp.float32
BF16 = jnp.bfloat16
MESH = pl.DeviceIdType.MESH

D_MODEL = 1024
LRU_WIDTH = 512
LRU_HEADS = 8
GMLP_WIDTH = 512
GMLP_GROUPS = 4
GMLP_BLOCK = 128
CHUNK = 64
D_FF = 3072
N_MOD = 6
EPS = 1e-6
LRU_C = 8.0
N_CHIPS = 4
N_DEV = 8

ADAM_LR = 0.001
ADAM_B1 = 0.9
ADAM_B2 = 0.999
ADAM_EPS = 1e-08
ADAM_WD = 0.01
ADAM_STEP = 10

GELU_C0 = math.sqrt(2.0 / math.pi)
GELU_C1 = 0.044715

VMEM_LIMIT_BYTES = 56 * 1024 * 1024
SUBLANES = 8
BF16_SUBLANES = 16
FFN_CHUNK = 768


def _gelu(x):
    t = jnp.tanh(GELU_C0 * (x + GELU_C1 * x * x * x))
    return 0.5 * x * (1.0 + t)


def _gelu_and_grad(x):
    x2 = x * x
    t = jnp.tanh(GELU_C0 * x * (1.0 + GELU_C1 * x2))
    g = 0.5 * x * (1.0 + t)
    dg = 0.5 * (1.0 + t) + 0.5 * x * (1.0 - t * t) * (GELU_C0 * (1.0 + 3.0 * GELU_C1 * x2))
    return g, dg


def _sigmoid(x):
    return 1.0 / (1.0 + jnp.exp(-x))


def _log1p(u):
    w = 1.0 + u
    return jnp.where(w == 1.0, u, jnp.log(w) * (u / (w - 1.0)))


def _softplus(x):
    return jnp.maximum(x, 0.0) + _log1p(jnp.exp(-jnp.abs(x)))


def _neg_expm1(x):
    u = jnp.exp(x)
    um1 = u - 1.0
    tiny = um1 == 0.0
    small = um1 * (x / jnp.log(jnp.where(tiny, 2.0, jnp.maximum(u, 0.25))))
    return -jnp.where(tiny, x, jnp.where(x < -1.0, um1, small))


def _msq_rsqrt(v):
    return lax.rsqrt(jnp.mean(v * v, axis=-1, keepdims=True) + EPS)


def _rms_bwd(dyn, yn, r):
    return r * (dyn - yn * jnp.mean(dyn * yn, axis=-1, keepdims=True))


def _colsum(v):
    return jnp.sum(v, axis=0, keepdims=True)


def _shift_down(cur, prev8, k):
    rolled = pltpu.roll(cur, k, 0)
    head = pltpu.roll(prev8, k, 0)
    row8 = lax.broadcasted_iota(jnp.int32, (SUBLANES, cur.shape[1]), 0)
    first = jnp.where(row8 < k, head, rolled[0:SUBLANES])
    return jnp.concatenate([first, rolled[SUBLANES:]], axis=0)


def _shift_up(cur, next8, k):
    t = cur.shape[0]
    rolled = pltpu.roll(cur, t - k, 0)
    tail = pltpu.roll(next8, SUBLANES - k, 0)
    row8 = lax.broadcasted_iota(jnp.int32, (SUBLANES, cur.shape[1]), 0)
    last = jnp.where(row8 >= SUBLANES - k, tail, rolled[t - SUBLANES:])
    return jnp.concatenate([rolled[:t - SUBLANES], last], axis=0)


def _scan_fwd(a, b):
    t = a.shape[0]
    row = lax.broadcasted_iota(jnp.int32, a.shape, 0)
    d = 1
    while d < t:
        keep = row >= d
        a_s = jnp.where(keep, pltpu.roll(a, d, 0), 1.0)
        b_s = jnp.where(keep, pltpu.roll(b, d, 0), 0.0)
        b = a * b_s + b
        a = a * a_s
        d *= 2
    return a, b


def _scan_bwd(a, g):
    t = a.shape[0]
    row = lax.broadcasted_iota(jnp.int32, a.shape, 0)
    d = 1
    while d < t:
        keep = row < t - d
        a_s = jnp.where(keep, pltpu.roll(a, t - d, 0), 1.0)
        g_s = jnp.where(keep, pltpu.roll(g, t - d, 0), 0.0)
        g = a * g_s + g
        a = a * a_s
        d *= 2
    return a, g


def _dot(a, b):
    return jnp.dot(a, b, preferred_element_type=F32)


def _dot_nt(a, b):
    return lax.dot_general(a, b, (((1,), (1,)), ((), ())), preferred_element_type=F32)


def _dot_tn(a, b):
    return lax.dot_general(a, b, (((0,), (0,)), ((), ())), preferred_element_type=F32)


def _rows(ts, cols, rev_of=None):
    if rev_of is None:
        return pl.BlockSpec((ts, cols), lambda i: (i, 0))
    return pl.BlockSpec((ts, cols), lambda i: (rev_of - 1 - i, 0))


def _halo_prev(ts, cols, halo, rev_of=None, col_block=0):
    per = ts // halo
    if rev_of is None:
        return pl.BlockSpec((halo, cols), lambda i: (jnp.maximum(i * per - 1, 0), col_block))
    return pl.BlockSpec((halo, cols), lambda i: (jnp.maximum((rev_of - 1 - i) * per - 1, 0), col_block))


def _full(shape):
    nd = len(shape)
    return pl.BlockSpec(shape, lambda *_: (0,) * nd)


_RESIDENT = pl.BlockSpec(memory_space=pltpu.VMEM)


def _params(sem):
    return pltpu.CompilerParams(dimension_semantics=sem, vmem_limit_bytes=VMEM_LIMIT_BYTES)


def _sds(shape, dtype):
    return jax.ShapeDtypeStruct(shape, dtype)


def _mix_in(x, sc, sh, g, w_in4, ts=256):
    s, d = x.shape

    def body(x_ref, sc_ref, sh_ref, g_ref, w_ref, z_ref, h_ref):
        xv = x_ref[...]
        h = (xv * _msq_rsqrt(xv) * g_ref[...]) * (1.0 + sc_ref[...]) + sh_ref[...]
        hb = h.astype(BF16)
        h_ref[...] = hb
        for k in range(N_CHIPS):
            z_ref[:, k * 512:(k + 1) * 512] = _dot(hb, w_ref[k])

    return pl.pallas_call(
        body, grid=(s // ts,), name="mix_in",
        in_specs=[_rows(ts, d), _full((1, d)), _full((1, d)), _full((1, d)), _full(w_in4.shape)],
        out_specs=[_rows(ts, 2048), _rows(ts, d)],
        out_shape=[_sds((s, 2048), F32), _sds((s, d), BF16)],
        compiler_params=_params(("parallel",)),
    )(x, sc, sh, g, w_in4)


def _seq_recompute(z_ref, zprev_ref, first_tile, p):
    (cw_ref, cb_ref, bdr_ref, bdi_ref, br_ref, bi_ref, la_ref, ng_ref, nb_ref, ws_ref, bst_ref) = p
    lx = z_ref[:, 0:512]
    lg = z_ref[:, 512:1024]
    gu = z_ref[:, 1024:1536]
    gv = z_ref[:, 1536:2048]
    prev8 = jnp.where(first_tile, 0.0, zprev_ref[...])
    s1 = _shift_down(lx, prev8, 1)
    s2 = _shift_down(lx, prev8, 2)
    s3 = _shift_down(lx, prev8, 3)
    xc = cw_ref[3:4, :] * lx + cw_ref[2:3, :] * s1 + cw_ref[1:2, :] * s2 + cw_ref[0:1, :] * s3 + cb_ref[...]
    xcb = xc.astype(BF16)
    r = _sigmoid(_dot(xcb, bdr_ref[...]) + br_ref[...])
    ig = _sigmoid(_dot(xcb, bdi_ref[...]) + bi_ref[...])
    spa = _softplus(-la_ref[...])
    log_a = (-LRU_C) * r * spa
    a = jnp.exp(log_a)
    mult = jnp.sqrt(_neg_expm1(2.0 * log_a))
    return dict(lx=lx, lg=lg, gu=gu, gv=gv, s1=s1, s2=s2, s3=s3, xc=xc, xcb=xcb, r=r, ig=ig, spa=spa,
                a=a, mult=mult)


def _gmlp_fwd(gu, gv, ng_ref, nb_ref, ws_ref, bst_ref, sp_scr):
    ts = gu.shape[0]
    u, du = _gelu_and_grad(gu)
    vg, dvg = _gelu_and_grad(gv)
    mu = jnp.mean(vg, axis=-1, keepdims=True)
    vc = vg - mu
    rstd = lax.rsqrt(jnp.mean(vc * vc, axis=-1, keepdims=True) + EPS)
    vhat = vc * rstd
    v = vhat * ng_ref[...] + nb_ref[...]
    vb = v.astype(BF16)
    for n in range(ts // GMLP_BLOCK):
        rs = slice(n * GMLP_BLOCK, (n + 1) * GMLP_BLOCK)
        for g in range(GMLP_GROUPS):
            cs = slice(g * 128, (g + 1) * 128)
            sp_scr[rs, cs] = _dot(ws_ref[g], vb[rs, cs]) + bst_ref[:, g:g + 1]
    spb = sp_scr[...]
    return dict(u=u, du=du, dvg=dvg, rstd=rstd, vhat=vhat, vb=vb, spb=spb, y_g=u * spb)


def _seq_specs(ts, nt, rev):
    rev_of = nt if rev else None
    return [
        _rows(ts, 2048, rev_of),
        _halo_prev(ts, 512, SUBLANES, rev_of),
    ]


def _seq_param_specs():
    return [_full((4, 512)), _full((1, 512)), _full((512, 512)), _full((512, 512)), _full((1, 512)),
            _full((1, 512)), _full((1, 512)), _full((1, 512)), _full((1, 512)), _full((4, 128, 128)),
            _full((128, 4))]


def _seqmix(z, seq_params, glo, ggo, ts=256):
    s = z.shape[0]
    nt = s // ts

    def body(z_ref, zprev_ref, *rest):
        p = rest[:11]
        glo_ref, ggo_ref, ycat_ref, hst_ref, hcarry, sp_scr = rest[11:]
        i = pl.program_id(0)

        @pl.when(i == 0)
        def _():
            hcarry[...] = jnp.zeros_like(hcarry)

        f = _seq_recompute(z_ref, zprev_ref, i == 0, p)
        bx = f["mult"] * (f["ig"] * f["xc"])
        acum, hloc = _scan_fwd(f["a"], bx)
        h = hloc + acum * hcarry[...]
        hcarry[...] = h[ts - 1:ts, :]
        hst_ref[...] = h
        y_l = h * _gelu(f["lg"])
        gm = _gmlp_fwd(f["gu"], f["gv"], p[7], p[8], p[9], p[10], sp_scr)
        y_g = gm["y_g"]
        ycat_ref[:, 0:512] = (y_l * _msq_rsqrt(y_l) * glo_ref[...]).astype(BF16)
        ycat_ref[:, 512:1024] = (y_g * _msq_rsqrt(y_g) * ggo_ref[...]).astype(BF16)

    return pl.pallas_call(
        body, grid=(nt,), name="seqmix",
        in_specs=_seq_specs(ts, nt, False) + _seq_param_specs() + [_full((1, 512)), _full((1, 512))],
        out_specs=[_rows(ts, 1024), _rows(ts, 512)],
        out_shape=[_sds((s, 1024), BF16), _sds((s, 512), F32)],
        scratch_shapes=[pltpu.VMEM((1, 512), F32), pltpu.VMEM((ts, 512), F32)],
        compiler_params=_params(("arbitrary",)),
    )(z, z, *seq_params, glo, ggo)


def _mix_out(ycat, x, w_out, gt_m, g_post, g_pre2, sc_f, sh_f, ts=256):
    s, d = x.shape

    def body(yc_ref, x_ref, w_ref, gt_ref, gp_ref, g2_ref, sc_ref, sh_ref, y_ref, x1_ref, h2_ref):
        y = _dot(yc_ref[...], w_ref[...])
        y_ref[...] = y
        x1 = x_ref[...] + gt_ref[...] * (y * _msq_rsqrt(y) * gp_ref[...])
        x1_ref[...] = x1
        h2 = (x1 * _msq_rsqrt(x1) * g2_ref[...]) * (1.0 + sc_ref[...]) + sh_ref[...]
        h2_ref[...] = h2.astype(BF16)

    vec = _full((1, d))
    return pl.pallas_call(
        body, grid=(s // ts,), name="mix_out",
        in_specs=[_rows(ts, d), _rows(ts, d), _full((d, d)), vec, vec, vec, vec, vec],
        out_specs=[_rows(ts, d), _rows(ts, d), _rows(ts, d)],
        out_shape=[_sds((s, d), F32), _sds((s, d), F32), _sds((s, d), BF16)],
        compiler_params=_params(("parallel",)),
    )(ycat, x, w_out, gt_m, g_post, g_pre2, sc_f, sh_f)


def _ffn_cols(j):
    per = (2 * D_FF // N_CHIPS) // FFN_CHUNK
    return j // per, (j % per) * FFN_CHUNK, j * FFN_CHUNK


def _ffn_fwd(h2, x1, tgt, w_up4, w_down, fw, fb, gt_f, g_post, ts=256):
    s, d = x1.shape
    nch = D_FF // FFN_CHUNK

    def body(h2_ref, x1_ref, tgt_ref, wup_ref, wdn_ref, fw_ref, fb_ref, gt_ref, gp_ref,
             up0_ref, act_ref, dy2_ref, dx2_ref, loss_ref, dgt_ref, dgp_ref, tail_ref):
        i = pl.program_id(0)

        @pl.when(i == 0)
        def _():
            tail_ref[...] = jnp.zeros_like(tail_ref)
            loss_ref[...] = jnp.zeros_like(loss_ref)
            dgt_ref[...] = jnp.zeros_like(dgt_ref)
            dgp_ref[...] = jnp.zeros_like(dgp_ref)

        hb = h2_ref[...]
        y2 = jnp.zeros((ts, d), F32)
        for j in range(nch):
            sh_g, off, col = _ffn_cols(j)
            halves = []
            for shard, c0 in ((sh_g, col), (sh_g + 2, D_FF + col)):
                cs = slice(c0, c0 + FFN_CHUNK)
                ub = _dot(hb, wup_ref[shard, :, off:off + FFN_CHUNK]).astype(BF16)
                up0_ref[:, cs] = ub
                u = ub.astype(F32)
                prev8 = tail_ref[:, cs]
                tail_ref[:, cs] = u[ts - SUBLANES:, :]
                halves.append(fw_ref[2:3, cs] * u + fw_ref[1:2, cs] * _shift_down(u, prev8, 1)
                              + fw_ref[0:1, cs] * _shift_down(u, prev8, 2) + fb_ref[:, cs])
            act = (_gelu(halves[0]) * halves[1]).astype(BF16)
            act_ref[:, col:col + FFN_CHUNK] = act
            y2 = y2 + _dot(act, wdn_ref[col:col + FFN_CHUNK, :])
        r2 = _msq_rsqrt(y2)
        yn = y2 * r2
        yng = yn * gp_ref[...]
        e = x1_ref[...] + gt_ref[...] * yng - tgt_ref[...]
        loss_ref[...] += jnp.sum(e * e) * (0.5 / d)
        dx2 = e * (1.0 / d)
        dx2_ref[...] = dx2
        dgt_ref[...] += _colsum(dx2 * yng)
        dyng = dx2 * gt_ref[...]
        dgp_ref[...] += _colsum(dyng * yn)
        dy2_ref[...] = _rms_bwd(dyng * gp_ref[...], yn, r2).astype(BF16)

    vec = _full((1, d))
    return pl.pallas_call(
        body, grid=(s // ts,), name="ffn_fwd",
        in_specs=[_rows(ts, d), _rows(ts, d), _rows(ts, d), _RESIDENT, _RESIDENT,
                  _full((3, 2 * D_FF)), _full((1, 2 * D_FF)), vec, vec],
        out_specs=[_rows(ts, 2 * D_FF), _rows(ts, D_FF), _rows(ts, d), _rows(ts, d),
                   _full((1, 128)), vec, vec],
        out_shape=[_sds((s, 2 * D_FF), BF16), _sds((s, D_FF), BF16), _sds((s, d), BF16), _sds((s, d), F32),
                   _sds((1, 128), F32), _sds((1, d), F32), _sds((1, d), F32)],
        scratch_shapes=[pltpu.VMEM((SUBLANES, 2 * D_FF), F32)],
        compiler_params=_params(("arbitrary",)),
    )(h2, x1, tgt, w_up4, w_down, fw, fb, gt_f, g_post)


def _ffn_bwd_a(dy2, up0, w_down, fw, fb, ts=256):
    s, d = dy2.shape
    nt = s // ts
    nch = D_FF // FFN_CHUNK
    wide = 2 * D_FF

    def body(dy2_ref, up0_ref, upprev_ref, wdn_ref, fw_ref, fb_ref, dup0_ref, dfw_ref, dfb_ref, next_ref):
        i = pl.program_id(0)

        @pl.when(i == 0)
        def _():
            next_ref[...] = jnp.zeros_like(next_ref)
            dfw_ref[...] = jnp.zeros_like(dfw_ref)
            dfb_ref[...] = jnp.zeros_like(dfb_ref)

        first_tile = i == nt - 1
        dyb = dy2_ref[...]
        for j in range(nch):
            _, _, col = _ffn_cols(j)
            dact = _dot_nt(dyb, wdn_ref[col:col + FFN_CHUNK, :])
            pre, shifted = [], []
            for c0 in (col, D_FF + col):
                cs = slice(c0, c0 + FFN_CHUNK)
                u = up0_ref[:, cs].astype(F32)
                prev8 = jnp.where(first_tile, 0.0, upprev_ref[:, cs].astype(F32)[SUBLANES:, :])
                sd = (u, _shift_down(u, prev8, 1), _shift_down(u, prev8, 2))
                shifted.append(sd)
                pre.append(fw_ref[2:3, cs] * sd[0] + fw_ref[1:2, cs] * sd[1] + fw_ref[0:1, cs] * sd[2]
                           + fb_ref[:, cs])
            gl, dgl = _gelu_and_grad(pre[0])
            dpre = (dact * pre[1] * dgl, dact * gl)
            for half, c0 in enumerate((col, D_FF + col)):
                cs = slice(c0, c0 + FFN_CHUNK)
                dp = dpre[half]
                sd = shifted[half]
                dfb_ref[:, cs] += _colsum(dp)
                dfw_ref[2:3, cs] += _colsum(dp * sd[0])
                dfw_ref[1:2, cs] += _colsum(dp * sd[1])
                dfw_ref[0:1, cs] += _colsum(dp * sd[2])
                nxt = next_ref[:, cs]
                next_ref[:, cs] = dp[0:SUBLANES, :]
                dup0 = (fw_ref[2:3, cs] * dp + fw_ref[1:2, cs] * _shift_up(dp, nxt, 1)
                        + fw_ref[0:1, cs] * _shift_up(dp, nxt, 2))
                dup0_ref[:, cs] = dup0.astype(BF16)

    return pl.pallas_call(
        body, grid=(nt,), name="ffn_bwd_a",
        in_specs=[_rows(ts, d, nt), _rows(ts, wide, nt), _halo_prev(ts, wide, BF16_SUBLANES, nt), _RESIDENT,
                  _full((3, wide)), _full((1, wide))],
        out_specs=[_rows(ts, wide, nt), _full((3, wide)), _full((1, wide))],
        out_shape=[_sds((s, wide), BF16), _sds((3, wide), F32), _sds((1, wide), F32)],
        scratch_shapes=[pltpu.VMEM((SUBLANES, wide), F32)],
        compiler_params=_params(("arbitrary",)),
    )(dy2, up0, up0, w_down, fw, fb)


def _ffn_bwd_b(dup0, x1, y, dx2, w_up4, g_pre2, sc_f, sh_f, gt_m, g_post_m, ts=256):
    s, d = x1.shape
    shard_cols = 2 * D_FF // N_CHIPS

    def body(dup_ref, x1_ref, y_ref, dx2_ref, wup_ref, g2_ref, sc_ref, sh_ref, gt_ref, gp_ref,
             dx1_ref, dy_ref, dsh_ref, dsc_ref, dg2_ref, dgt_ref, dgp_ref):
        i = pl.program_id(0)

        @pl.when(i == 0)
        def _():
            for ref in (dsh_ref, dsc_ref, dg2_ref, dgt_ref, dgp_ref):
                ref[...] = jnp.zeros_like(ref)

        dh2 = jnp.zeros((ts, d), F32)
        for k in range(N_CHIPS):
            dh2 = dh2 + _dot_nt(dup_ref[:, k * shard_cols:(k + 1) * shard_cols], wup_ref[k])
        x1v = x1_ref[...]
        r2 = _msq_rsqrt(x1v)
        xn = x1v * r2
        hn = xn * g2_ref[...]
        dsh_ref[...] += _colsum(dh2)
        dsc_ref[...] += _colsum(dh2 * hn)
        dhn = dh2 * (1.0 + sc_ref[...])
        dg2_ref[...] += _colsum(dhn * xn)
        dx1 = dx2_ref[...] + _rms_bwd(dhn * g2_ref[...], xn, r2)
        dx1_ref[...] = dx1
        yv = y_ref[...]
        ry = _msq_rsqrt(yv)
        yn = yv * ry
        dgt_ref[...] += _colsum(dx1 * (yn * gp_ref[...]))
        dyng = dx1 * gt_ref[...]
        dgp_ref[...] += _colsum(dyng * yn)
        dy_ref[...] = _rms_bwd(dyng * gp_ref[...], yn, ry).astype(BF16)

    vec = _full((1, d))
    return pl.pallas_call(
        body, grid=(s // ts,), name="ffn_bwd_b",
        in_specs=[_rows(ts, 2 * D_FF), _rows(ts, d), _rows(ts, d), _rows(ts, d), _RESIDENT,
                  vec, vec, vec, vec, vec],
        out_specs=[_rows(ts, d), _rows(ts, d), vec, vec, vec, vec, vec],
        out_shape=[_sds((s, d), F32), _sds((s, d), BF16)] + [_sds((1, d), F32)] * 5,
        compiler_params=_params(("arbitrary",)),
    )(dup0, x1, y, dx2, w_up4, g_pre2, sc_f, sh_f, gt_m, g_post_m)


def _seqmix_bwd(z, hst, dy, w_out, seq_params, ws_t, glo, ggo, ts=256):
    s = z.shape[0]
    nt = s // ts
    small_shapes = [(4, 512), (1, 512), (512, 512), (512, 512), (1, 512), (1, 512), (1, 512),
                    (1, 512), (1, 512), (4, 128, 128), (128, 4), (1, 512), (1, 512)]

    def body(z_ref, zprev_ref, hst_ref, hprev_ref, dy_ref, wout_ref, *rest):
        p = rest[:11]
        wst_ref, glo_ref, ggo_ref = rest[11:14]
        dz_ref = rest[14]
        (dcw_ref, dcb_ref, dwr_ref, dwi_ref, dbr_ref, dbi_ref, dspa_ref, dng_ref, dnb_ref, dws_ref, dbs_ref,
         dglo_ref, dggo_ref) = rest[15:28]
        gcarry, anext, dxcnext, sp_scr, dv_scr = rest[28:]
        i = pl.program_id(0)

        @pl.when(i == 0)
        def _():
            for ref in rest[15:28]:
                ref[...] = jnp.zeros_like(ref)
            gcarry[...] = jnp.zeros_like(gcarry)
            anext[...] = jnp.ones_like(anext)
            dxcnext[...] = jnp.zeros_like(dxcnext)

        first_tile = i == nt - 1
        f = _seq_recompute(z_ref, zprev_ref, first_tile, p)
        xc, r, ig, a, mult, lx = f["xc"], f["r"], f["ig"], f["a"], f["mult"], f["lx"]
        h = hst_ref[...]
        hprev = _shift_down(h, jnp.where(first_tile, 0.0, hprev_ref[...]), 1)
        gl, dgl = _gelu_and_grad(f["lg"])
        y_l = h * gl
        gm = _gmlp_fwd(f["gu"], f["gv"], p[7], p[8], p[9], p[10], sp_scr)
        y_g = gm["y_g"]

        dycat = _dot_nt(dy_ref[...], wout_ref[...])
        rl = _msq_rsqrt(y_l)
        yln = y_l * rl
        dyl = dycat[:, 0:512]
        dglo_ref[...] += _colsum(dyl * yln)
        dy_l = _rms_bwd(dyl * glo_ref[...], yln, rl)
        rg = _msq_rsqrt(y_g)
        ygn = y_g * rg
        dyg = dycat[:, 512:1024]
        dggo_ref[...] += _colsum(dyg * ygn)
        dy_g = _rms_bwd(dyg * ggo_ref[...], ygn, rg)

        dz_ref[:, 512:1024] = (dy_l * h * dgl).astype(BF16)
        a_up = _shift_up(a, anext[...], 1)
        acum, gloc = _scan_bwd(a_up, dy_l * gl)
        gg = gloc + acum * gcarry[...]
        gcarry[...] = gg[0:1, :]
        anext[...] = a[0:SUBLANES, :]
        da = gg * hprev
        t1 = gg * mult
        di = t1 * xc
        dxc = t1 * ig
        dmult = gg * ig * xc
        dla = da * a - dmult * (a * a / mult)
        spa = f["spa"]
        dspa_ref[...] += _colsum(dla * r) * (-LRU_C)
        dpr = dla * ((-LRU_C) * spa) * r * (1.0 - r)
        dpi = di * ig * (1.0 - ig)
        dbr_ref[...] += _colsum(dpr)
        dbi_ref[...] += _colsum(dpi)
        dprb = dpr.astype(BF16)
        dpib = dpi.astype(BF16)
        dwr_ref[...] += _dot_tn(f["xcb"], dprb)
        dwi_ref[...] += _dot_tn(f["xcb"], dpib)
        dxc = dxc + _dot_nt(dprb, p[2][...]) + _dot_nt(dpib, p[3][...])
        dcb_ref[...] += _colsum(dxc)
        dcw_ref[3:4, :] += _colsum(dxc * lx)
        dcw_ref[2:3, :] += _colsum(dxc * f["s1"])
        dcw_ref[1:2, :] += _colsum(dxc * f["s2"])
        dcw_ref[0:1, :] += _colsum(dxc * f["s3"])
        nxt = dxcnext[...]
        dxcnext[...] = dxc[0:SUBLANES, :]
        cw_ref = p[0]
        dlx = (cw_ref[3:4, :] * dxc + cw_ref[2:3, :] * _shift_up(dxc, nxt, 1)
               + cw_ref[1:2, :] * _shift_up(dxc, nxt, 2) + cw_ref[0:1, :] * _shift_up(dxc, nxt, 3))
        dz_ref[:, 0:512] = dlx.astype(BF16)

        dz_ref[:, 1024:1536] = (dy_g * gm["spb"] * gm["du"]).astype(BF16)
        dsp = dy_g * gm["u"]
        vb = gm["vb"]
        for n in range(ts // GMLP_BLOCK):
            rs = slice(n * GMLP_BLOCK, (n + 1) * GMLP_BLOCK)
            for g in range(GMLP_GROUPS):
                cs = slice(g * 128, (g + 1) * 128)
                dbs_ref[:, g:g + 1] += jnp.sum(dsp[rs, cs], axis=1, keepdims=True)
                blk = dsp[rs, cs].astype(BF16)
                dws_ref[g] += _dot_nt(blk, vb[rs, cs])
                dv_scr[rs, cs] = _dot(wst_ref[g], blk)
        dv = dv_scr[...]
        vhat = gm["vhat"]
        dng_ref[...] += _colsum(dv * vhat)
        dnb_ref[...] += _colsum(dv)
        dvh = dv * p[7][...]
        dvg = gm["rstd"] * (dvh - jnp.mean(dvh, axis=-1, keepdims=True)
                            - vhat * jnp.mean(dvh * vhat, axis=-1, keepdims=True))
        dz_ref[:, 1536:2048] = (dvg * gm["dvg"]).astype(BF16)

        @pl.when(i == nt - 1)
        def _():
            pos = lax.broadcasted_iota(jnp.int32, (GMLP_BLOCK, GMLP_BLOCK), 0) // CHUNK
            src = lax.broadcasted_iota(jnp.int32, (GMLP_BLOCK, GMLP_BLOCK), 1) // CHUNK
            for g in range(GMLP_GROUPS):
                dws_ref[g] = jnp.where(src <= pos, dws_ref[g], 0.0)
            dspa_ref[...] = dspa_ref[...] * (-_sigmoid(-p[6][...]))

    in_specs = (_seq_specs(ts, nt, True)
                + [_rows(ts, 512, nt), _halo_prev(ts, 512, SUBLANES, nt), _rows(ts, 1024, nt), _full((1024, 1024))]
                + _seq_param_specs() + [_full((4, 128, 128)), _full((1, 512)), _full((1, 512))])
    return pl.pallas_call(
        body, grid=(nt,), name="seqmix_bwd",
        in_specs=in_specs,
        out_specs=[_rows(ts, 2048, nt)] + [_full(sh) for sh in small_shapes],
        out_shape=[_sds((s, 2048), BF16)] + [_sds(sh, F32) for sh in small_shapes],
        scratch_shapes=[pltpu.VMEM((1, 512), F32), pltpu.VMEM((SUBLANES, 512), F32),
                        pltpu.VMEM((SUBLANES, 512), F32), pltpu.VMEM((ts, 512), F32), pltpu.VMEM((ts, 512), F32)],
        compiler_params=_params(("arbitrary",)),
    )(z, z, hst, hst, dy, w_out, *seq_params, ws_t, glo, ggo)


def _mix_in_bwd(x, dz, dx1, w_in4, g, sc, ts=256):
    s, d = x.shape

    def body(x_ref, dz_ref, dx1_ref, w_ref, g_ref, sc_ref, gx_ref, dsh_ref, dsc_ref, dg_ref):
        i = pl.program_id(0)

        @pl.when(i == 0)
        def _():
            for ref in (dsh_ref, dsc_ref, dg_ref):
                ref[...] = jnp.zeros_like(ref)

        dh = jnp.zeros((ts, d), F32)
        for k in range(N_CHIPS):
            dh = dh + _dot_nt(dz_ref[:, k * 512:(k + 1) * 512], w_ref[k])
        xv = x_ref[...]
        r = _msq_rsqrt(xv)
        xn = xv * r
        dsh_ref[...] += _colsum(dh)
        dsc_ref[...] += _colsum(dh * (xn * g_ref[...]))
        dhn = dh * (1.0 + sc_ref[...])
        dg_ref[...] += _colsum(dhn * xn)
        gx_ref[...] = dx1_ref[...] + _rms_bwd(dhn * g_ref[...], xn, r)

    vec = _full((1, d))
    return pl.pallas_call(
        body, grid=(s // ts,), name="mix_in_bwd",
        in_specs=[_rows(ts, d), _rows(ts, 2048), _rows(ts, d), _full(w_in4.shape), vec, vec],
        out_specs=[_rows(ts, d), vec, vec, vec],
        out_shape=[_sds((s, d), F32)] + [_sds((1, d), F32)] * 3,
        compiler_params=_params(("arbitrary",)),
    )(x, dz, dx1, w_in4, g, sc)


def _wgrad(a, b, n_chunks, name, chunk_major, ts=512):
    s, m = a.shape
    n = b.shape[1]
    nc = n // n_chunks
    nt = s // ts

    def body(a_ref, b_ref, o_ref, acc):
        i = pl.program_id(1)

        @pl.when(i == 0)
        def _():
            acc[...] = jnp.zeros_like(acc)

        acc[...] += _dot_tn(a_ref[...], b_ref[...])

        @pl.when(i == nt - 1)
        def _():
            if chunk_major:
                o_ref[0] = acc[...].astype(BF16)
            else:
                o_ref[...] = acc[...].astype(BF16)

    if chunk_major:
        out_spec, out_shape = pl.BlockSpec((1, m, nc), lambda c, i: (c, 0, 0)), _sds((n_chunks, m, nc), BF16)
    else:
        out_spec, out_shape = pl.BlockSpec((m, nc), lambda c, i: (0, c)), _sds((m, n), BF16)
    return pl.pallas_call(
        body, grid=(n_chunks, nt), name=name,
        in_specs=[pl.BlockSpec((ts, m), lambda c, i: (i, 0)), pl.BlockSpec((ts, nc), lambda c, i: (i, c))],
        out_specs=out_spec,
        out_shape=out_shape,
        scratch_shapes=[pltpu.VMEM((m, nc), F32)],
        compiler_params=_params(("parallel", "arbitrary")),
    )(a, b)


def _block_diag(w):
    heads, hd, _ = w.shape
    eye = jnp.eye(heads, dtype=w.dtype)
    return (eye[:, None, :, None] * w[:, :, None, :]).reshape(heads * hd, heads * hd)


def _diag_blocks(m):
    hd = LRU_WIDTH // LRU_HEADS
    m4 = m.reshape(LRU_HEADS, hd, LRU_HEADS, hd)
    return jnp.stack([m4[k, :, k, :] for k in range(LRU_HEADS)])


def _seq_params(small):
    row = lambda v: v.reshape(1, -1)
    pos = jnp.arange(GMLP_BLOCK)
    mask = (pos[None, :] // CHUNK) <= (pos[:, None] // CHUNK)
    ws = jnp.where(mask[None], small["w_spatial"], 0.0)
    seq_params = (small["conv_w"], row(small["conv_b"]),
                  _block_diag(small["w_rgate"]).astype(BF16), _block_diag(small["w_igate"]).astype(BF16),
                  row(small["b_rgate"]), row(small["b_igate"]), row(small["lru_a"]),
                  row(small["v_norm_g"]), row(small["v_norm_b"]), ws.astype(BF16), small["b_spatial"].T)
    return seq_params, jnp.swapaxes(ws, 1, 2).astype(BF16)


_ANY = pl.BlockSpec(memory_space=pl.ANY)
_CHIP_FLIPS = ((1, 0), (0, 1), (1, 1))


def _position():
    return lax.axis_index("x"), lax.axis_index("y"), lax.axis_index("c")


def _flip(v, f):
    return 1 - v if f else v


def _remote(src, dst, send_sem, recv_sem, peer):
    return pltpu.make_async_remote_copy(src_ref=src, dst_ref=dst, send_sem=send_sem, recv_sem=recv_sem,
                                        device_id=peer, device_id_type=MESH)


def _allgather8(block, name, reduce):
    r, n = block.shape

    def body(x_ref, out_ref, *scratch):
        if reduce:
            gath, send_sems, recv_sems, loc_sem = scratch
        else:
            gath = out_ref
            send_sems, recv_sems, loc_sem = scratch
        x, y, c = _position()
        me = 4 * x + 2 * y + c
        loc = pltpu.make_async_copy(x_ref, gath.at[me], loc_sem)
        loc.start()
        peers = []
        for k in range(1, N_DEV):
            px, py, pc = _flip(x, k & 4), _flip(y, k & 2), _flip(c, k & 1)
            peers.append((px, py, pc))
            _remote(x_ref, gath.at[me], send_sems.at[k - 1], recv_sems.at[k - 1], (px, py, pc)).start()
        for k, (px, py, pc) in enumerate(peers):
            src = 4 * px + 2 * py + pc
            _remote(x_ref, gath.at[src], send_sems.at[k], recv_sems.at[k], (px, py, pc)).wait_recv()
        for k, peer in enumerate(peers):
            _remote(x_ref, gath.at[me], send_sems.at[k], recv_sems.at[k], peer).wait_send()
        loc.wait()
        if reduce:
            acc = gath[0]
            for k in range(1, N_DEV):
                acc = acc + gath[k]
            out_ref[...] = acc

    sems = [pltpu.SemaphoreType.DMA((N_DEV - 1,)), pltpu.SemaphoreType.DMA((N_DEV - 1,)), pltpu.SemaphoreType.DMA]
    if reduce:
        out_shape = _sds((r, n), F32)
        scratch = [pltpu.VMEM((N_DEV, r, n), F32)] + sems
    else:
        out_shape = _sds((N_DEV, r, n), F32)
        scratch = sems
    return pl.pallas_call(
        body, name=name, out_shape=out_shape,
        in_specs=[pl.BlockSpec(memory_space=pltpu.VMEM)], out_specs=pl.BlockSpec(memory_space=pltpu.VMEM),
        scratch_shapes=scratch,
        compiler_params=pltpu.CompilerParams(vmem_limit_bytes=VMEM_LIMIT_BYTES),
    )(block)


def _half(ref, c, rows):
    hr = rows // 2
    return ref.at[pl.ds(pl.multiple_of(c * hr, BF16_SUBLANES), hr), :]


def _gather_weights(shards):
    na = len(shards)

    def body(*refs):
        ins, outs = refs[:na], refs[na:2 * na]
        ici_send, ici_recv, d2d_send, d2d_recv, loc_sem = refs[2 * na:]
        x, y, c = _position()
        chip = 2 * x + y
        sibling = (x, y, 1 - c)
        local = []
        for a in range(na):
            local.append(pltpu.make_async_copy(ins[a], outs[a].at[chip], loc_sem.at[a]))
            local[-1].start()
        sends = []
        for a in range(na):
            rows = shards[a].shape[0]
            for j, (fx, fy) in enumerate(_CHIP_FLIPS):
                peer = (_flip(x, fx), _flip(y, fy), c)
                sends.append(_remote(_half(ins[a], c, rows), _half(outs[a].at[chip], c, rows),
                                     ici_send.at[a * 3 + j], ici_recv.at[a * 3 + j], peer))
                sends[-1].start()
        for a in range(na):
            rows = shards[a].shape[0]
            for j, (fx, fy) in enumerate(_CHIP_FLIPS):
                src_chip = 2 * _flip(x, fx) + _flip(y, fy)
                landed = _half(outs[a].at[src_chip], c, rows)
                _remote(landed, landed, ici_send.at[a * 3 + j], ici_recv.at[a * 3 + j], sibling).wait_recv()
                sends.append(_remote(landed, landed, d2d_send.at[a * 3 + j], d2d_recv.at[a * 3 + j], sibling))
                sends[-1].start()
        for a in range(na):
            rows = shards[a].shape[0]
            for j, (fx, fy) in enumerate(_CHIP_FLIPS):
                src_chip = 2 * _flip(x, fx) + _flip(y, fy)
                other = _half(outs[a].at[src_chip], 1 - c, rows)
                _remote(other, other, d2d_send.at[a * 3 + j], d2d_recv.at[a * 3 + j], sibling).wait_recv()
        for cp in sends:
            cp.wait_send()
        for cp in local:
            cp.wait()

    return pl.pallas_call(
        body, name="gather_weights",
        out_shape=[_sds((N_CHIPS,) + w.shape, w.dtype) for w in shards],
        in_specs=[_ANY] * na, out_specs=[_ANY] * na,
        scratch_shapes=[pltpu.SemaphoreType.DMA((3 * na,))] * 4 + [pltpu.SemaphoreType.DMA((na,))],
    )(*shards)


def _swap_halves(parts, name):
    na = len(parts)

    def body(*refs):
        ins, outs = refs[:na], refs[na:2 * na]
        send_sems, recv_sems = refs[2 * na:]
        x, y, c = _position()
        sibling = (x, y, 1 - c)
        cps = []
        for a in range(na):
            hr = parts[a].shape[1] // 2
            src = ins[a].at[:, pl.ds(pl.multiple_of((1 - c) * hr, BF16_SUBLANES), hr), :]
            cps.append(_remote(src, outs[a], send_sems.at[a], recv_sems.at[a], sibling))
            cps[-1].start()
        for cp in cps:
            cp.wait()

    return pl.pallas_call(
        body, name=name,
        out_shape=[_sds((N_CHIPS, p.shape[1] // 2, p.shape[2]), p.dtype) for p in parts],
        in_specs=[_ANY] * na, out_specs=[_ANY] * na,
        scratch_shapes=[pltpu.SemaphoreType.DMA((na,))] * 2,
    )(*parts)


def _chip_sum(part, recv, c_arr, name):
    _, rows, cols = part.shape
    hr = rows // 2

    def body(c_ref, p_ref, r_ref, o_ref):
        o_ref[...] = (p_ref[...].astype(F32) + r_ref[...].astype(F32)).astype(BF16)

    grid_spec = pltpu.PrefetchScalarGridSpec(
        num_scalar_prefetch=1, grid=(N_CHIPS,),
        in_specs=[pl.BlockSpec((1, hr, cols), lambda k, c_ref: (k, c_ref[0], 0)),
                  pl.BlockSpec((1, hr, cols), lambda k, c_ref: (k, 0, 0))],
        out_specs=pl.BlockSpec((1, hr, cols), lambda k, c_ref: (k, 0, 0)))
    return pl.pallas_call(
        body, name=name, grid_spec=grid_spec, out_shape=_sds((N_CHIPS, hr, cols), BF16),
        compiler_params=_params(("arbitrary",)),
    )(c_arr, part, recv)


def _exchange_chips(sums):
    na = len(sums)

    def body(*refs):
        ins, outs = refs[:na], refs[na:2 * na]
        send_sems, recv_sems, loc_sem = refs[2 * na:]
        x, y, c = _position()
        chip = 2 * x + y
        local = []
        for a in range(na):
            local.append(pltpu.make_async_copy(ins[a].at[chip], outs[a].at[chip], loc_sem.at[a]))
            local[-1].start()
        cps = []
        for a in range(na):
            for j, (fx, fy) in enumerate(_CHIP_FLIPS):
                px, py = _flip(x, fx), _flip(y, fy)
                cps.append(_remote(ins[a].at[2 * px + py], outs[a].at[chip],
                                   send_sems.at[a * 3 + j], recv_sems.at[a * 3 + j], (px, py, c)))
                cps[-1].start()
        for a in range(na):
            for j, (fx, fy) in enumerate(_CHIP_FLIPS):
                src_chip = 2 * _flip(x, fx) + _flip(y, fy)
                landed = outs[a].at[src_chip]
                _remote(landed, landed, send_sems.at[a * 3 + j], recv_sems.at[a * 3 + j], (x, y, c)).wait_recv()
        for cp in cps:
            cp.wait_send()
        for cp in local:
            cp.wait()

    return pl.pallas_call(
        body, name="exchange_chips",
        out_shape=[_sds(s.shape, s.dtype) for s in sums],
        in_specs=[_ANY] * na, out_specs=[_ANY] * na,
        scratch_shapes=[pltpu.SemaphoreType.DMA((3 * na,))] * 2 + [pltpu.SemaphoreType.DMA((na,))],
    )(*sums)


def _sum_chips(gath, name, tr=128):
    _, hr, cols = gath.shape
    tr = min(tr, hr)

    def body(g_ref, o_ref):
        acc = g_ref[0].astype(F32)
        for k in range(1, N_CHIPS):
            acc = acc + g_ref[k].astype(F32)
        o_ref[...] = acc

    return pl.pallas_call(
        body, name=name, grid=(hr // tr,),
        in_specs=[pl.BlockSpec((N_CHIPS, tr, cols), lambda i: (0, i, 0))],
        out_specs=pl.BlockSpec((tr, cols), lambda i: (i, 0)),
        out_shape=_sds((hr, cols), F32),
        compiler_params=_params(("parallel",)),
    )(gath)


def _join_halves(halves):
    na = len(halves)

    def body(*refs):
        ins, outs = refs[:na], refs[na:2 * na]
        send_sems, recv_sems, loc_sem = refs[2 * na:]
        x, y, c = _position()
        sibling = (x, y, 1 - c)
        cps, local = [], []
        for a in range(na):
            rows = 2 * halves[a].shape[0]
            mine = _half(outs[a], c, rows)
            local.append(pltpu.make_async_copy(ins[a], mine, loc_sem.at[a]))
            local[-1].start()
            cps.append(_remote(ins[a], mine, send_sems.at[a], recv_sems.at[a], sibling))
            cps[-1].start()
        for a in range(na):
            rows = 2 * halves[a].shape[0]
            other = _half(outs[a], 1 - c, rows)
            _remote(ins[a], other, send_sems.at[a], recv_sems.at[a], sibling).wait_recv()
        for cp in cps:
            cp.wait_send()
        for cp in local:
            cp.wait()

    return pl.pallas_call(
        body, name="join_halves",
        out_shape=[_sds((2 * h.shape[0], h.shape[1]), h.dtype) for h in halves],
        in_specs=[_ANY] * na, out_specs=[_ANY] * na,
        scratch_shapes=[pltpu.SemaphoreType.DMA((na,))] * 3,
    )(*halves)


_HBM = pl.BlockSpec(memory_space=pltpu.HBM)
_SEM = pl.BlockSpec(memory_space=pltpu.SEMAPHORE)
_EFFECT = pltpu.SideEffectType.DATAFLOW_SIDE_EFFECTING


def _in_hbm(a):
    return pltpu.with_memory_space_constraint(a, pltpu.HBM)


def _split_start(srcs, lands, plan, n_copies, name):
    ns, nl = len(srcs), len(lands)
    bufs = list(srcs) + list(lands)

    def body(*refs):
        send_sems, recv_sems = refs[ns + nl], refs[ns + nl + 1]
        token = refs[-1]
        for k, (src, dst, peer) in enumerate(plan(refs[:ns], refs[ns:ns + nl])):
            _remote(src, dst, send_sems.at[k], recv_sems.at[k], peer).start()
        token[...] = jnp.zeros_like(token)

    out = pl.pallas_call(
        body, name=name,
        out_shape=(pltpu.SemaphoreType.DMA((n_copies,)), pltpu.SemaphoreType.DMA((n_copies,)),
                   *[pltpu.HBM(b.shape, b.dtype) for b in bufs], _sds((SUBLANES, 128), F32)),
        in_specs=[_HBM] * (ns + nl),
        out_specs=(_SEM, _SEM, *[_HBM] * (ns + nl), pl.BlockSpec(memory_space=pltpu.VMEM)),
        input_output_aliases={i: 2 + i for i in range(ns + nl)},
        compiler_params=pltpu.CompilerParams(has_side_effects=_EFFECT),
    )(*[_in_hbm(b) for b in bufs])
    return out[0], out[1], list(out[2:2 + ns]), list(out[2 + ns:2 + ns + nl]), out[-1]


def _split_wait(send_sems, recv_sems, srcs, lands, plan, after, name):
    ns, nl = len(srcs), len(lands)
    bufs = list(srcs) + list(lands)

    def body(*refs):
        send_ref, recv_ref = refs[ns + nl], refs[ns + nl + 1]
        me = _position()
        for k, src, dst in plan(refs[:ns], refs[ns:ns + nl]):
            cp = _remote(src, dst, send_ref.at[k], recv_ref.at[k], me)
            cp.wait_send()
            cp.wait_recv()

    out = pl.pallas_call(
        body, name=name,
        out_shape=[pltpu.HBM(b.shape, b.dtype) for b in bufs],
        in_specs=[_HBM] * (ns + nl) + [_SEM, _SEM, _ANY],
        out_specs=[_HBM] * (ns + nl),
        input_output_aliases={i: i for i in range(ns + nl)},
        compiler_params=pltpu.CompilerParams(has_side_effects=_EFFECT),
    )(*bufs, send_sems, recv_sems, after)
    return list(out[:ns]), list(out[ns:])


def _gather_plan(rows_of):
    def start(src_refs, land_refs):
        x, y, c = _position()
        chip = 2 * x + y
        return [(_half(src_refs[a], c, rows), _half(land_refs[a].at[chip], c, rows),
                 (_flip(x, fx), _flip(y, fy), c))
                for a, rows in enumerate(rows_of) for fx, fy in _CHIP_FLIPS]

    def wait(first):
        def plan(src_refs, land_refs):
            x, y, c = _position()
            out = []
            for a in range(len(src_refs)):
                rows = rows_of[first + a]
                for j, (fx, fy) in enumerate(_CHIP_FLIPS):
                    src_chip = 2 * _flip(x, fx) + _flip(y, fy)
                    out.append((3 * (first + a) + j, _half(src_refs[a], c, rows),
                                _half(land_refs[a].at[src_chip], c, rows)))
            return out
        return plan

    return start, wait


def _exchange_plan(n_arrays):
    def start(src_refs, land_refs):
        x, y, c = _position()
        chip = 2 * x + y
        out = []
        for a in range(n_arrays):
            for fx, fy in _CHIP_FLIPS:
                px, py = _flip(x, fx), _flip(y, fy)
                out.append((src_refs[a].at[2 * px + py], land_refs[a].at[0, chip], (px, py, c)))
        return out

    def wait(src_refs, land_refs):
        x, y, c = _position()
        out = []
        for a in range(n_arrays):
            for j, (fx, fy) in enumerate(_CHIP_FLIPS):
                src_chip = 2 * _flip(x, fx) + _flip(y, fy)
                out.append((3 * a + j, src_refs[a].at[src_chip], land_refs[a].at[0, src_chip]))
        return out

    return start, wait


def _forward_to_sibling(lands, shards, name):
    na = len(lands)

    def body(*refs):
        shard_refs, land_refs = refs[na:2 * na], refs[2 * na:3 * na]
        send_sems, recv_sems, loc_sem = refs[3 * na:]
        x, y, c = _position()
        chip = 2 * x + y
        sibling = (x, y, 1 - c)
        local, sends = [], []
        for a in range(na):
            local.append(pltpu.make_async_copy(shard_refs[a], land_refs[a].at[chip], loc_sem.at[a]))
            local[-1].start()
            rows = shards[a].shape[0]
            for j, (fx, fy) in enumerate(_CHIP_FLIPS):
                landed = _half(land_refs[a].at[2 * _flip(x, fx) + _flip(y, fy)], c, rows)
                sends.append(_remote(landed, landed, send_sems.at[3 * a + j], recv_sems.at[3 * a + j], sibling))
                sends[-1].start()
        for a in range(na):
            rows = shards[a].shape[0]
            for j, (fx, fy) in enumerate(_CHIP_FLIPS):
                other = _half(land_refs[a].at[2 * _flip(x, fx) + _flip(y, fy)], 1 - c, rows)
                _remote(other, other, send_sems.at[3 * a + j], recv_sems.at[3 * a + j], sibling).wait_recv()
        for cp in sends:
            cp.wait_send()
        for cp in local:
            cp.wait()

    return pl.pallas_call(
        body, name=name,
        out_shape=[_sds(l.shape, l.dtype) for l in lands],
        in_specs=[_ANY] * (2 * na), out_specs=[_ANY] * na,
        input_output_aliases={a: a for a in range(na)},
        scratch_shapes=[pltpu.SemaphoreType.DMA((3 * na,))] * 2 + [pltpu.SemaphoreType.DMA((na,))],
    )(*lands, *shards)


def _swap_gathered(sums, gath, name):
    na = len(sums)

    def body(*refs):
        sum_refs, gath_refs = refs[:na], refs[2 * na:3 * na]
        send_sems, recv_sems, loc_sem = refs[3 * na:]
        x, y, c = _position()
        chip = 2 * x + y
        sibling = (x, y, 1 - c)
        for a in range(na):
            own = pltpu.make_async_copy(sum_refs[a].at[chip], gath_refs[a].at[0, chip], loc_sem.at[a])
            own.start()
            own.wait()
        cps = [_remote(gath_refs[a].at[0], gath_refs[a].at[1], send_sems.at[a], recv_sems.at[a], sibling)
               for a in range(na)]
        for cp in cps:
            cp.start()
        for cp in cps:
            cp.wait()

    return pl.pallas_call(
        body, name=name,
        out_shape=[_sds(g.shape, g.dtype) for g in gath],
        in_specs=[_ANY] * (2 * na), out_specs=[_ANY] * na,
        input_output_aliases={na + a: a for a in range(na)},
        scratch_shapes=[pltpu.SemaphoreType.DMA((na,))] * 3,
    )(*sums, *gath)


def _adam_gathered(w, gath, m, v, c_arr, name, tr=128):
    rows, cols = w.shape
    hr = rows // 2
    per = hr // tr

    def body(c_ref, w_ref, g_ref, m_ref, v_ref, go_ref, d_ref, nm_ref, nv_ref):
        g = g_ref[0, 0].astype(F32)
        for k in range(1, N_CHIPS):
            g = g + g_ref[0, k].astype(F32)
        go_ref[...] = g
        d_ref[...], nm_ref[...], nv_ref[...] = _adam_math(w_ref[...], g, m_ref[...], v_ref[...])

    def rows_of(h, i, c_ref):
        c = c_ref[0]
        return ((c + h - 2 * c * h) * per + i, 0)

    blk = pl.BlockSpec((tr, cols), rows_of)
    grid_spec = pltpu.PrefetchScalarGridSpec(
        num_scalar_prefetch=1, grid=(2, per),
        in_specs=[blk, pl.BlockSpec((1, N_CHIPS, tr, cols), lambda h, i, c_ref: (h, 0, i, 0)), blk, blk],
        out_specs=[blk] * 4)
    return pl.pallas_call(
        body, name=name, grid_spec=grid_spec, out_shape=[_sds(w.shape, F32)] * 4,
        compiler_params=_params(("arbitrary", "arbitrary")),
    )(c_arr, w, gath, m, v)


def _allreduce_small(block, name):
    r, n = block.shape
    hr = r // 2

    def body(x_ref, out_ref, sib, chipsum, gath, d2d_send, d2d_recv, ici_send, ici_recv):
        x, y, c = _position()
        chip = 2 * x + y
        sibling = (x, y, 1 - c)
        first = _remote(x_ref, sib, d2d_send.at[0], d2d_recv.at[0], sibling)
        first.start()
        first.wait()
        chipsum[...] = x_ref[...] + sib[...]
        mine = pl.ds(pl.multiple_of(c * hr, SUBLANES), hr)
        theirs = pl.ds(pl.multiple_of((1 - c) * hr, SUBLANES), hr)
        sends = []
        for j, (fx, fy) in enumerate(_CHIP_FLIPS):
            sends.append(_remote(chipsum.at[mine, :], gath.at[chip], ici_send.at[j], ici_recv.at[j],
                                 (_flip(x, fx), _flip(y, fy), c)))
            sends[-1].start()
        gath[chip] = chipsum[mine, :]
        for j, (fx, fy) in enumerate(_CHIP_FLIPS):
            landed = gath.at[2 * _flip(x, fx) + _flip(y, fy)]
            _remote(landed, landed, ici_send.at[j], ici_recv.at[j], sibling).wait_recv()
        for cp in sends:
            cp.wait_send()
        total = gath[0]
        for k in range(1, N_CHIPS):
            total = total + gath[k]
        out_ref[mine, :] = total
        last = _remote(out_ref.at[mine, :], out_ref.at[mine, :], d2d_send.at[1], d2d_recv.at[1], sibling)
        last.start()
        _remote(out_ref.at[theirs, :], out_ref.at[theirs, :], d2d_send.at[1], d2d_recv.at[1], sibling).wait_recv()
        last.wait_send()

    vmem = pl.BlockSpec(memory_space=pltpu.VMEM)
    return pl.pallas_call(
        body, name=name, out_shape=_sds((r, n), F32), in_specs=[vmem], out_specs=vmem,
        scratch_shapes=[pltpu.VMEM((r, n), F32), pltpu.VMEM((r, n), F32), pltpu.VMEM((N_CHIPS, hr, n), F32),
                        pltpu.SemaphoreType.DMA((2,)), pltpu.SemaphoreType.DMA((2,)),
                        pltpu.SemaphoreType.DMA((3,)), pltpu.SemaphoreType.DMA((3,))],
        compiler_params=pltpu.CompilerParams(vmem_limit_bytes=VMEM_LIMIT_BYTES),
    )(block)


def _silu(v):
    return v * _sigmoid(v)


def _ada_fwd(c8, w_ada):
    def body(c_ref, w_ref, o_ref):
        o_ref[...] = jnp.dot(_silu(c_ref[...]), w_ref[...], preferred_element_type=F32,
                             precision=lax.Precision.HIGHEST)

    return pl.pallas_call(
        body, name="ada_fwd", out_shape=_sds((N_DEV, w_ada.shape[1]), F32),
        compiler_params=pltpu.CompilerParams(vmem_limit_bytes=VMEM_LIMIT_BYTES),
    )(c8, w_ada)


def _mod_select(parts, b_ada, me_arr):
    cols = parts.shape[2]

    def body(me_ref, p_ref, b_ref, o_ref):
        me = me_ref[0]
        for k in range(N_CHIPS):
            cs = slice(k * cols, (k + 1) * cols)
            o_ref[:, cs] = p_ref[2 * k, pl.ds(me, 1), :] + b_ref[:, cs]

    grid_spec = pltpu.PrefetchScalarGridSpec(
        num_scalar_prefetch=1, grid=(1,),
        in_specs=[pl.BlockSpec(parts.shape, lambda i, m: (0, 0, 0)), pl.BlockSpec(b_ada.shape, lambda i, m: (0, 0))],
        out_specs=pl.BlockSpec(b_ada.shape, lambda i, m: (0, 0)))
    return pl.pallas_call(body, name="mod_select", grid_spec=grid_spec, out_shape=_sds(b_ada.shape, F32))(
        me_arr, parts, b_ada)


def _ada_bwd(c8, dmod8, chip_arr):
    d = c8.shape[1]
    cols = dmod8.shape[1] // N_CHIPS

    def body(chip_ref, c_ref, dm_ref, dmall_ref, gw_ref, gb_ref):
        gw_ref[...] = lax.dot_general(_silu(c_ref[...]), dm_ref[...], (((0,), (0,)), ((), ())),
                                      preferred_element_type=F32, precision=lax.Precision.HIGHEST)
        acc = dmall_ref[0:1, :]
        for k in range(1, N_DEV):
            acc = acc + dmall_ref[k:k + 1, :]
        gb_ref[...] = acc

    grid_spec = pltpu.PrefetchScalarGridSpec(
        num_scalar_prefetch=1, grid=(1,),
        in_specs=[pl.BlockSpec(c8.shape, lambda i, ch: (0, 0)),
                  pl.BlockSpec((N_DEV, cols), lambda i, ch: (0, ch[0])),
                  pl.BlockSpec(dmod8.shape, lambda i, ch: (0, 0))],
        out_specs=[pl.BlockSpec((d, cols), lambda i, ch: (0, 0)), pl.BlockSpec((1, dmod8.shape[1]), lambda i, ch: (0, 0))])
    return pl.pallas_call(
        body, name="ada_bwd", grid_spec=grid_spec,
        out_shape=[_sds((d, cols), F32), _sds((1, dmod8.shape[1]), F32)],
        compiler_params=_params(("arbitrary",)),
    )(chip_arr, c8, dmod8, dmod8)


def _adam_math(w, g, m, v):
    m = ADAM_B1 * m + (1.0 - ADAM_B1) * g
    v = ADAM_B2 * v + (1.0 - ADAM_B2) * (g * g)
    m_hat = m / (1.0 - ADAM_B1 ** ADAM_STEP)
    v_hat = v / (1.0 - ADAM_B2 ** ADAM_STEP)
    delta = -ADAM_LR * (m_hat / (jnp.sqrt(v_hat) + ADAM_EPS) + ADAM_WD * w)
    return delta, m, v


def _adam(w, g, m, v, name, tr=256):
    rows, cols = w.shape
    if rows % tr:
        tr = rows

    def body(w_ref, g_ref, m_ref, v_ref, d_ref, nm_ref, nv_ref):
        d_ref[...], nm_ref[...], nv_ref[...] = _adam_math(w_ref[...], g_ref[...], m_ref[...], v_ref[...])

    spec = pl.BlockSpec((tr, cols), lambda i: (i, 0))
    return pl.pallas_call(
        body, name=name, grid=(rows // tr,), in_specs=[spec] * 4, out_specs=[spec] * 3,
        out_shape=[_sds(w.shape, F32)] * 3, compiler_params=_params(("parallel",)),
    )(w, g, m, v)


def _adam_cols(w, g_full, m, v, chip_arr, name):
    rows, cols = w.shape

    def body(chip_ref, w_ref, g_ref, m_ref, v_ref, gs_ref, d_ref, nm_ref, nv_ref):
        g = g_ref[...]
        gs_ref[...] = g
        d_ref[...], nm_ref[...], nv_ref[...] = _adam_math(w_ref[...], g, m_ref[...], v_ref[...])

    own = pl.BlockSpec((rows, cols), lambda i, ch: (0, 0))
    grid_spec = pltpu.PrefetchScalarGridSpec(
        num_scalar_prefetch=1, grid=(1,),
        in_specs=[own, pl.BlockSpec((rows, cols), lambda i, ch: (0, ch[0])), own, own],
        out_specs=[own] * 4)
    return pl.pallas_call(body, name=name, grid_spec=grid_spec, out_shape=[_sds(w.shape, F32)] * 4)(
        chip_arr, w, g_full, m, v)


PACK_COLS = 512
SMALL_REPLICATED = ("g_mix_pre", "g_mix_post", "conv_b", "w_rgate", "b_rgate", "w_igate", "b_igate", "lru_a",
                    "v_norm_g", "v_norm_b", "w_spatial", "b_spatial", "g_lru_out", "g_gmlp_out", "g_ffn_pre",
                    "g_ffn_post", "ffn_conv_b")
SMALL_COLUMN_SHARDED = ("conv_w", "ffn_conv_w")


def _pack(arrays):
    parts = []
    for arr in arrays:
        p = arr.reshape(-1, PACK_COLS)
        pad = (-p.shape[0]) % SUBLANES
        parts.append(jnp.pad(p, ((0, pad), (0, 0))) if pad else p)
    total = sum(p.shape[0] for p in parts)
    if total % (2 * SUBLANES):
        parts.append(jnp.zeros((SUBLANES, PACK_COLS), parts[0].dtype))
    return jnp.concatenate(parts, axis=0)


def _unpack(packed, shapes):
    out, row = [], 0
    for shape in shapes:
        n = math.prod(shape) // PACK_COLS
        out.append(packed[row:row + n].reshape(shape))
        row += n + (-n) % SUBLANES
    return out


def kernel(x, c, w_ada, b_ada, g_mix_pre, g_mix_post, w_in, conv_w, conv_b, w_rgate, b_rgate, w_igate, b_igate, lru_a, v_norm_g, v_norm_b, w_spatial, b_spatial, g_lru_out, g_gmlp_out, w_out, g_ffn_pre, g_ffn_post, w_up, ffn_conv_w, ffn_conv_b, w_down, loss_target, m_w_ada, m_b_ada, m_g_mix_pre, m_g_mix_post, m_w_in, m_conv_w, m_conv_b, m_w_rgate, m_b_rgate, m_w_igate, m_b_igate, m_lru_a, m_v_norm_g, m_v_norm_b, m_w_spatial, m_b_spatial, m_g_lru_out, m_g_gmlp_out, m_w_out, m_g_ffn_pre, m_g_ffn_post, m_w_up, m_ffn_conv_w, m_ffn_conv_b, m_w_down, v_w_ada, v_b_ada, v_g_mix_pre, v_g_mix_post, v_w_in, v_conv_w, v_conv_b, v_w_rgate, v_b_rgate, v_w_igate, v_b_igate, v_lru_a, v_v_norm_g, v_v_norm_b, v_w_spatial, v_b_spatial, v_g_lru_out, v_g_gmlp_out, v_w_out, v_g_ffn_pre, v_g_ffn_post, v_w_up, v_ffn_conv_w, v_ffn_conv_b, v_w_down):
    args = dict(locals())
    names = ("w_ada", "b_ada", "g_mix_pre", "g_mix_post", "w_in", "conv_w", "conv_b", "w_rgate", "b_rgate",
             "w_igate", "b_igate", "lru_a", "v_norm_g", "v_norm_b", "w_spatial", "b_spatial", "g_lru_out",
             "g_gmlp_out", "w_out", "g_ffn_pre", "g_ffn_post", "w_up", "ffn_conv_w", "ffn_conv_b", "w_down")
    w = {n: args[n][0] for n in names}
    m = {n: args["m_" + n][0] for n in names}
    v = {n: args["v_" + n][0] for n in names}
    xi, yi, ci = _position()
    me_arr = jnp.reshape(4 * xi + 2 * yi + ci, (1,)).astype(jnp.int32)
    chip_arr = jnp.reshape(2 * xi + yi, (1,)).astype(jnp.int32)
    c_arr = jnp.reshape(ci, (1,)).astype(jnp.int32)

    big = ("w_in", "w_out", "w_up", "w_down")
    shards = [w[n].astype(BF16) for n in big]
    gather_start, gather_wait = _gather_plan([s.shape[0] for s in shards])
    lands = [lax.empty((N_CHIPS,) + s.shape, BF16) for s in shards]
    g_send, g_recv, shards, lands, token = _split_start(shards, lands, gather_start, 3 * len(big), "gather_start")

    row0 = jnp.concatenate([c + token[0:1, 0:1], w["conv_w"].reshape(1, -1), w["ffn_conv_w"].reshape(1, -1)], axis=1)
    g0 = _allgather8(row0, "gather_cond", False)[:, 0, :]
    c8 = g0[:, :D_MODEL]
    per_chip = g0[0::2]
    conv_w_full = per_chip[:, D_MODEL:D_MODEL + 512].reshape(N_CHIPS, 4, 128).transpose(1, 0, 2).reshape(4, 512)
    ffn_conv_w_full = per_chip[:, D_MODEL + 512:].reshape(N_CHIPS, 3, 1536).transpose(1, 0, 2).reshape(3, 2 * D_FF)
    mod_parts = _allgather8(_ada_fwd(c8, w["w_ada"]), "gather_mod", False)
    mod = _mod_select(mod_parts, w["b_ada"].reshape(1, -1), me_arr).reshape(N_MOD, D_MODEL)
    sh_m, sc_m, gt_m, sh_f, sc_f, gt_f = [mod[k:k + 1] for k in range(N_MOD)]

    small = {n: w[n] for n in SMALL_REPLICATED}
    small["conv_w"] = conv_w_full
    small["ffn_conv_w"] = ffn_conv_w_full
    row = lambda a: a.reshape(1, -1)
    seq_params, ws_t = _seq_params(small)
    glo, ggo = row(small["g_lru_out"]), row(small["g_gmlp_out"])
    g_pre, g_post = row(small["g_mix_pre"]), row(small["g_mix_post"])
    g_pre2, g_post2 = row(small["g_ffn_pre"]), row(small["g_ffn_post"])
    fw, fb = small["ffn_conv_w"], row(small["ffn_conv_b"])
    xs, tgt = x[0], loss_target[0]

    sh_a, lands_a = _split_wait(g_send, g_recv, shards[:2], lands[:2], gather_wait(0), mod, "gather_wait_a")
    w_in4, w_out4 = _forward_to_sibling(lands_a, sh_a, "forward_a")
    w_out_b = w_out4.reshape(D_MODEL, D_MODEL)
    z, h = _mix_in(xs, sc_m, sh_m, g_pre, w_in4)
    ycat, hst = _seqmix(z, seq_params, glo, ggo)
    y, x1, h2 = _mix_out(ycat, xs, w_out_b, gt_m, g_post, g_pre2, sc_f, sh_f)
    sh_b, lands_b = _split_wait(g_send, g_recv, shards[2:], lands[2:], gather_wait(2), h2, "gather_wait_b")
    w_up4, w_down4 = _forward_to_sibling(lands_b, sh_b, "forward_b")
    w_down_b = w_down4.reshape(D_FF, D_MODEL)
    up0, act, dy2, dx2, loss, dgt_f, dg_post2 = _ffn_fwd(h2, x1, tgt, w_up4, w_down_b, fw, fb, gt_f, g_post2)

    dup0, dfw, dfb = _ffn_bwd_a(dy2, up0, w_down_b, fw, fb)
    gw_up = _wgrad(h2, dup0, N_CHIPS, "wgrad_up", True)
    gw_down = _wgrad(act, dy2, 2, "wgrad_down", False)
    ex_start, ex_wait = _exchange_plan(2)

    def reduce_start(parts, tags, name):
        recv = _swap_halves(parts, "swap_halves_" + name)
        sums = [_chip_sum(p, r, c_arr, "chip_sum_" + t) for p, r, t in zip(parts, recv, tags)]
        gath = [lax.empty((2,) + s.shape, BF16) for s in sums]
        return _split_start(sums, gath, ex_start, 3 * len(parts), "exchange_start_" + name)

    e_send_b, e_recv_b, sums_b, gath_b, token_b = reduce_start(
        [gw_up, gw_down.reshape(N_CHIPS, -1, D_MODEL)], ("w_up", "w_down"), "b")

    dx1, dy, dsh_f, dsc_f, dg_pre2, dgt_m, dg_post = _ffn_bwd_b(
        dup0, x1, y, dx2, w_up4, g_pre2, sc_f + token_b[0:1, 0:1], sh_f, gt_m, g_post)
    (dz, dcw, dcb, dwr, dwi, dbr, dbi, dspa, dng, dnb, dws, dbs_t, dglo, dggo) = _seqmix_bwd(
        z, hst, dy, w_out_b, seq_params, ws_t, glo, ggo)
    grad_x, dsh_m, dsc_m, dg_pre = _mix_in_bwd(xs, dz, dx1, w_in4, g_pre, sc_m)
    gw_in = _wgrad(h, dz, N_CHIPS, "wgrad_in", True)
    gw_out = _wgrad(ycat, dy, 1, "wgrad_out", False)
    e_send_a, e_recv_a, sums_a, gath_a, token_a = reduce_start(
        [gw_in, gw_out.reshape(N_CHIPS, -1, D_MODEL)], ("w_in", "w_out"), "a")

    grads, deltas, new_m, new_v = {}, {}, {}, {}

    def reduce_finish(send, recv, sums, gath, tags, after, name):
        sums, gath = _split_wait(send, recv, sums, gath, ex_wait, after, "exchange_wait_" + name)
        gath = _swap_gathered(sums, gath, "swap_gathered_" + name)
        for g, t in zip(gath, tags):
            grads[t], deltas[t], new_m[t], new_v[t] = _adam_gathered(w[t], g, m[t], v[t], c_arr, "adam_" + t)

    reduce_finish(e_send_b, e_recv_b, sums_b, gath_b, ("w_up", "w_down"), token_a, "b")
    reduce_finish(e_send_a, e_recv_a, sums_a, gath_a, ("w_in", "w_out"), deltas["w_down"], "a")

    dmod = jnp.concatenate([dsh_m, dsc_m, dgt_m, dsh_f, dsc_f, dgt_f], axis=1)
    dmod8 = _allgather8(dmod, "gather_dmod", False)[:, 0, :]
    g_w_ada, g_b_ada = _ada_bwd(c8, dmod8, chip_arr)
    grads["w_ada"] = g_w_ada
    deltas["w_ada"], new_m["w_ada"], new_v["w_ada"] = _adam(w["w_ada"], g_w_ada, m["w_ada"], v["w_ada"], "adam_w_ada")

    small_grads = dict(
        g_mix_pre=dg_pre[0], g_mix_post=dg_post[0], conv_w=dcw, conv_b=dcb[0],
        w_rgate=_diag_blocks(dwr), b_rgate=dbr.reshape(LRU_HEADS, -1),
        w_igate=_diag_blocks(dwi), b_igate=dbi.reshape(LRU_HEADS, -1), lru_a=dspa[0],
        v_norm_g=dng[0], v_norm_b=dnb[0], w_spatial=dws, b_spatial=dbs_t.T,
        g_lru_out=dglo[0], g_gmlp_out=dggo[0], g_ffn_pre=dg_pre2[0], g_ffn_post=dg_post2[0],
        ffn_conv_w=dfw, ffn_conv_b=dfb[0])
    packed_names = SMALL_REPLICATED + SMALL_COLUMN_SHARDED
    g_small = _allreduce_small(_pack([small_grads[n] for n in packed_names]), "reduce_small")
    g_small_list = _unpack(g_small, [small_grads[n].shape for n in packed_names])
    g_rep = dict(zip(packed_names, g_small_list))

    rep = SMALL_REPLICATED
    d_p, m_p, v_p = _adam(_pack([w[n] for n in rep]), _pack([g_rep[n] for n in rep]),
                          _pack([m[n] for n in rep]), _pack([v[n] for n in rep]), "adam_small")
    shapes = [w[n].shape for n in rep]
    for n, dd, mm, vv in zip(rep, _unpack(d_p, shapes), _unpack(m_p, shapes), _unpack(v_p, shapes)):
        grads[n], deltas[n], new_m[n], new_v[n] = g_rep[n], dd, mm, vv
    for n in SMALL_COLUMN_SHARDED:
        grads[n], deltas[n], new_m[n], new_v[n] = _adam_cols(w[n], g_rep[n], m[n], v[n], chip_arr, "adam_" + n)
    b2 = lambda a: a.reshape(-1, PACK_COLS)
    d_b, m_b, v_b = _adam(b2(w["b_ada"]), b2(g_b_ada), b2(m["b_ada"]), b2(v["b_ada"]), "adam_b_ada")
    grads["b_ada"], deltas["b_ada"], new_m["b_ada"], new_v["b_ada"] = (
        g_b_ada.reshape(-1), d_b.reshape(-1), m_b.reshape(-1), v_b.reshape(-1))

    total = lax.psum(loss[0, 0], ("x", "y", "c"))
    outs = [total, grad_x[None]]
    for group in (grads, deltas, new_m, new_v):
        outs.extend(group[n][None] for n in names)
    return tuple(outs)
```

```python
import functools
import math

import jax
import jax.numpy as jnp
from jax import lax
from jax.experimental import pallas as pl
from jax.experimental.pallas import tpu as pltpu

F32 = jnp.float32
BF16 = jnp.bfloat16
MESH = pl.DeviceIdType.MESH

D_MODEL = 1024
LRU_WIDTH = 512
LRU_HEADS = 8
GMLP_WIDTH = 512
GMLP_GROUPS = 4
GMLP_BLOCK = 128
CHUNK = 64
D_FF = 3072
N_MOD = 6
EPS = 1e-6
LRU_C = 8.0
N_CHIPS = 4
N_DEV = 8

ADAM_LR = 0.001
ADAM_B1 = 0.9
ADAM_B2 = 0.999
ADAM_EPS = 1e-08
ADAM_WD = 0.01
ADAM_STEP = 10

GELU_C0 = math.sqrt(2.0 / math.pi)
GELU_C1 = 0.044715

VMEM_LIMIT_BYTES = 56 * 1024 * 1024
SUBLANES = 8
BF16_SUBLANES = 16
FFN_CHUNK = 768


def _gelu(x):
    t = jnp.tanh(GELU_C0 * (x + GELU_C1 * x * x * x))
    return 0.5 * x * (1.0 + t)


def _gelu_and_grad(x):
    x2 = x * x
    t = jnp.tanh(GELU_C0 * x * (1.0 + GELU_C1 * x2))
    g = 0.5 * x * (1.0 + t)
    dg = 0.5 * (1.0 + t) + 0.5 * x * (1.0 - t * t) * (GELU_C0 * (1.0 + 3.0 * GELU_C1 * x2))
    return g, dg


def _sigmoid(x):
    return 1.0 / (1.0 + jnp.exp(-x))


def _log1p(u):
    w = 1.0 + u
    return jnp.where(w == 1.0, u, jnp.log(w) * (u / (w - 1.0)))


def _softplus(x):
    return jnp.maximum(x, 0.0) + _log1p(jnp.exp(-jnp.abs(x)))


def _neg_expm1(x):
    u = jnp.exp(x)
    um1 = u - 1.0
    tiny = um1 == 0.0
    small = um1 * (x / jnp.log(jnp.where(tiny, 2.0, jnp.maximum(u, 0.25))))
    return -jnp.where(tiny, x, jnp.where(x < -1.0, um1, small))


def _msq_rsqrt(v):
    return lax.rsqrt(jnp.mean(v * v, axis=-1, keepdims=True) + EPS)


def _rms_bwd(dyn, yn, r):
    return r * (dyn - yn * jnp.mean(dyn * yn, axis=-1, keepdims=True))


def _colsum(v):
    return jnp.sum(v, axis=0, keepdims=True)


def _shift_down(cur, prev8, k):
    rolled = pltpu.roll(cur, k, 0)
    head = pltpu.roll(prev8, k, 0)
    row8 = lax.broadcasted_iota(jnp.int32, (SUBLANES, cur.shape[1]), 0)
    first = jnp.where(row8 < k, head, rolled[0:SUBLANES])
    return jnp.concatenate([first, rolled[SUBLANES:]], axis=0)


def _shift_up(cur, next8, k):
    t = cur.shape[0]
    rolled = pltpu.roll(cur, t - k, 0)
    tail = pltpu.roll(next8, SUBLANES - k, 0)
    row8 = lax.broadcasted_iota(jnp.int32, (SUBLANES, cur.shape[1]), 0)
    last = jnp.where(row8 >= SUBLANES - k, tail, rolled[t - SUBLANES:])
    return jnp.concatenate([rolled[:t - SUBLANES], last], axis=0)


def _scan_fwd(a, b):
    t = a.shape[0]
    row = lax.broadcasted_iota(jnp.int32, a.shape, 0)
    d = 1
    while d < t:
        keep = row >= d
        a_s = jnp.where(keep, pltpu.roll(a, d, 0), 1.0)
        b_s = jnp.where(keep, pltpu.roll(b, d, 0), 0.0)
        b = a * b_s + b
        a = a * a_s
        d *= 2
    return a, b


def _scan_bwd(a, g):
    t = a.shape[0]
    row = lax.broadcasted_iota(jnp.int32, a.shape, 0)
    d = 1
    while d < t:
        keep = row < t - d
        a_s = jnp.where(keep, pltpu.roll(a, t - d, 0), 1.0)
        g_s = jnp.where(keep, pltpu.roll(g, t - d, 0), 0.0)
        g = a * g_s + g
        a = a * a_s
        d *= 2
    return a, g


def _dot(a, b):
    return jnp.dot(a, b, preferred_element_type=F32)


def _dot_nt(a, b):
    return lax.dot_general(a, b, (((1,), (1,)), ((), ())), preferred_element_type=F32)


def _dot_tn(a, b):
    return lax.dot_general(a, b, (((0,), (0,)), ((), ())), preferred_element_type=F32)


def _rows(ts, cols, rev_of=None):
    if rev_of is None:
        return pl.BlockSpec((ts, cols), lambda i: (i, 0))
    return pl.BlockSpec((ts, cols), lambda i: (rev_of - 1 - i, 0))


def _halo_prev(ts, cols, halo, rev_of=None, col_block=0):
    per = ts // halo
    if rev_of is None:
        return pl.BlockSpec((halo, cols), lambda i: (jnp.maximum(i * per - 1, 0), col_block))
    return pl.BlockSpec((halo, cols), lambda i: (jnp.maximum((rev_of - 1 - i) * per - 1, 0), col_block))


def _full(shape):
    nd = len(shape)
    return pl.BlockSpec(shape, lambda *_: (0,) * nd)


_RESIDENT = pl.BlockSpec(memory_space=pltpu.VMEM)


def _params(sem):
    return pltpu.CompilerParams(dimension_semantics=sem, vmem_limit_bytes=VMEM_LIMIT_BYTES)


def _sds(shape, dtype):
    return jax.ShapeDtypeStruct(shape, dtype)


def _mix_in(x, sc, sh, g, w_in4, ts=256):
    s, d = x.shape

    def body(x_ref, sc_ref, sh_ref, g_ref, w_ref, z_ref, h_ref):
        xv = x_ref[...]
        h = (xv * _msq_rsqrt(xv) * g_ref[...]) * (1.0 + sc_ref[...]) + sh_ref[...]
        hb = h.astype(BF16)
        h_ref[...] = hb
        for k in range(N_CHIPS):
            z_ref[:, k * 512:(k + 1) * 512] = _dot(hb, w_ref[k])

    return pl.pallas_call(
        body, grid=(s // ts,), name="mix_in",
        in_specs=[_rows(ts, d), _full((1, d)), _full((1, d)), _full((1, d)), _full(w_in4.shape)],
        out_specs=[_rows(ts, 2048), _rows(ts, d)],
        out_shape=[_sds((s, 2048), F32), _sds((s, d), BF16)],
        compiler_params=_params(("parallel",)),
    )(x, sc, sh, g, w_in4)


def _seq_recompute(z_ref, zprev_ref, first_tile, p):
    (cw_ref, cb_ref, bdr_ref, bdi_ref, br_ref, bi_ref, la_ref, ng_ref, nb_ref, ws_ref, bst_ref) = p
    lx = z_ref[:, 0:512]
    lg = z_ref[:, 512:1024]
    gu = z_ref[:, 1024:1536]
    gv = z_ref[:, 1536:2048]
    prev8 = jnp.where(first_tile, 0.0, zprev_ref[...])
    s1 = _shift_down(lx, prev8, 1)
    s2 = _shift_down(lx, prev8, 2)
    s3 = _shift_down(lx, prev8, 3)
    xc = cw_ref[3:4, :] * lx + cw_ref[2:3, :] * s1 + cw_ref[1:2, :] * s2 + cw_ref[0:1, :] * s3 + cb_ref[...]
    xcb = xc.astype(BF16)
    r = _sigmoid(_dot(xcb, bdr_ref[...]) + br_ref[...])
    ig = _sigmoid(_dot(xcb, bdi_ref[...]) + bi_ref[...])
    spa = _softplus(-la_ref[...])
    log_a = (-LRU_C) * r * spa
    a = jnp.exp(log_a)
    mult = jnp.sqrt(_neg_expm1(2.0 * log_a))
    return dict(lx=lx, lg=lg, gu=gu, gv=gv, s1=s1, s2=s2, s3=s3, xc=xc, xcb=xcb, r=r, ig=ig, spa=spa,
                a=a, mult=mult)


def _gmlp_fwd(gu, gv, ng_ref, nb_ref, ws_ref, bst_ref, sp_scr):
    ts = gu.shape[0]
    u, du = _gelu_and_grad(gu)
    vg, dvg = _gelu_and_grad(gv)
    mu = jnp.mean(vg, axis=-1, keepdims=True)
    vc = vg - mu
    rstd = lax.rsqrt(jnp.mean(vc * vc, axis=-1, keepdims=True) + EPS)
    vhat = vc * rstd
    v = vhat * ng_ref[...] + nb_ref[...]
    vb = v.astype(BF16)
    for n in range(ts // GMLP_BLOCK):
        rs = slice(n * GMLP_BLOCK, (n + 1) * GMLP_BLOCK)
        for g in range(GMLP_GROUPS):
            cs = slice(g * 128, (g + 1) * 128)
            sp_scr[rs, cs] = _dot(ws_ref[g], vb[rs, cs]) + bst_ref[:, g:g + 1]
    spb = sp_scr[...]
    return dict(u=u, du=du, dvg=dvg, rstd=rstd, vhat=vhat, vb=vb, spb=spb, y_g=u * spb)


def _seq_specs(ts, nt, rev):
    rev_of = nt if rev else None
    return [
        _rows(ts, 2048, rev_of),
        _halo_prev(ts, 512, SUBLANES, rev_of),
    ]


def _seq_param_specs():
    return [_full((4, 512)), _full((1, 512)), _full((512, 512)), _full((512, 512)), _full((1, 512)),
            _full((1, 512)), _full((1, 512)), _full((1, 512)), _full((1, 512)), _full((4, 128, 128)),
            _full((128, 4))]


def _seqmix(z, seq_params, glo, ggo, ts=256):
    s = z.shape[0]
    nt = s // ts

    def body(z_ref, zprev_ref, *rest):
        p = rest[:11]
        glo_ref, ggo_ref, ycat_ref, hst_ref, hcarry, sp_scr = rest[11:]
        i = pl.program_id(0)

        @pl.when(i == 0)
        def _():
            hcarry[...] = jnp.zeros_like(hcarry)

        f = _seq_recompute(z_ref, zprev_ref, i == 0, p)
        bx = f["mult"] * (f["ig"] * f["xc"])
        acum, hloc = _scan_fwd(f["a"], bx)
        h = hloc + acum * hcarry[...]
        hcarry[...] = h[ts - 1:ts, :]
        hst_ref[...] = h
        y_l = h * _gelu(f["lg"])
        gm = _gmlp_fwd(f["gu"], f["gv"], p[7], p[8], p[9], p[10], sp_scr)
        y_g = gm["y_g"]
        ycat_ref[:, 0:512] = (y_l * _msq_rsqrt(y_l) * glo_ref[...]).astype(BF16)
        ycat_ref[:, 512:1024] = (y_g * _msq_rsqrt(y_g) * ggo_ref[...]).astype(BF16)

    return pl.pallas_call(
        body, grid=(nt,), name="seqmix",
        in_specs=_seq_specs(ts, nt, False) + _seq_param_specs() + [_full((1, 512)), _full((1, 512))],
        out_specs=[_rows(ts, 1024), _rows(ts, 512)],
        out_shape=[_sds((s, 1024), BF16), _sds((s, 512), F32)],
        scratch_shapes=[pltpu.VMEM((1, 512), F32), pltpu.VMEM((ts, 512), F32)],
        compiler_params=_params(("arbitrary",)),
    )(z, z, *seq_params, glo, ggo)


def _mix_out(ycat, x, w_out, gt_m, g_post, g_pre2, sc_f, sh_f, ts=256):
    s, d = x.shape

    def body(yc_ref, x_ref, w_ref, gt_ref, gp_ref, g2_ref, sc_ref, sh_ref, y_ref, x1_ref, h2_ref):
        y = _dot(yc_ref[...], w_ref[...])
        y_ref[...] = y
        x1 = x_ref[...] + gt_ref[...] * (y * _msq_rsqrt(y) * gp_ref[...])
        x1_ref[...] = x1
        h2 = (x1 * _msq_rsqrt(x1) * g2_ref[...]) * (1.0 + sc_ref[...]) + sh_ref[...]
        h2_ref[...] = h2.astype(BF16)

    vec = _full((1, d))
    return pl.pallas_call(
        body, grid=(s // ts,), name="mix_out",
        in_specs=[_rows(ts, d), _rows(ts, d), _full((d, d)), vec, vec, vec, vec, vec],
        out_specs=[_rows(ts, d), _rows(ts, d), _rows(ts, d)],
        out_shape=[_sds((s, d), F32), _sds((s, d), F32), _sds((s, d), BF16)],
        compiler_params=_params(("parallel",)),
    )(ycat, x, w_out, gt_m, g_post, g_pre2, sc_f, sh_f)


def _ffn_cols(j):
    per = (2 * D_FF // N_CHIPS) // FFN_CHUNK
    return j // per, (j % per) * FFN_CHUNK, j * FFN_CHUNK


def _ffn_fwd(h2, x1, tgt, w_up4, w_down, fw, fb, gt_f, g_post, ts=256):
    s, d = x1.shape
    nch = D_FF // FFN_CHUNK

    def body(h2_ref, x1_ref, tgt_ref, wup_ref, wdn_ref, fw_ref, fb_ref, gt_ref, gp_ref,
             up0_ref, act_ref, dy2_ref, dx2_ref, loss_ref, dgt_ref, dgp_ref, tail_ref):
        i = pl.program_id(0)

        @pl.when(i == 0)
        def _():
            tail_ref[...] = jnp.zeros_like(tail_ref)
            loss_ref[...] = jnp.zeros_like(loss_ref)
            dgt_ref[...] = jnp.zeros_like(dgt_ref)
            dgp_ref[...] = jnp.zeros_like(dgp_ref)

        hb = h2_ref[...]
        y2 = jnp.zeros((ts, d), F32)
        for j in range(nch):
            sh_g, off, col = _ffn_cols(j)
            halves = []
            for shard, c0 in ((sh_g, col), (sh_g + 2, D_FF + col)):
                cs = slice(c0, c0 + FFN_CHUNK)
                ub = _dot(hb, wup_ref[shard, :, off:off + FFN_CHUNK]).astype(BF16)
                up0_ref[:, cs] = ub
                u = ub.astype(F32)
                prev8 = tail_ref[:, cs]
                tail_ref[:, cs] = u[ts - SUBLANES:, :]
                halves.append(fw_ref[2:3, cs] * u + fw_ref[1:2, cs] * _shift_down(u, prev8, 1)
                              + fw_ref[0:1, cs] * _shift_down(u, prev8, 2) + fb_ref[:, cs])
            act = (_gelu(halves[0]) * halves[1]).astype(BF16)
            act_ref[:, col:col + FFN_CHUNK] = act
            y2 = y2 + _dot(act, wdn_ref[col:col + FFN_CHUNK, :])
        r2 = _msq_rsqrt(y2)
        yn = y2 * r2
        yng = yn * gp_ref[...]
        e = x1_ref[...] + gt_ref[...] * yng - tgt_ref[...]
        loss_ref[...] += jnp.sum(e * e) * (0.5 / d)
        dx2 = e * (1.0 / d)
        dx2_ref[...] = dx2
        dgt_ref[...] += _colsum(dx2 * yng)
        dyng = dx2 * gt_ref[...]
        dgp_ref[...] += _colsum(dyng * yn)
        dy2_ref[...] = _rms_bwd(dyng * gp_ref[...], yn, r2).astype(BF16)

    vec = _full((1, d))
    return pl.pallas_call(
        body, grid=(s // ts,), name="ffn_fwd",
        in_specs=[_rows(ts, d), _rows(ts, d), _rows(ts, d), _RESIDENT, _RESIDENT,
                  _full((3, 2 * D_FF)), _full((1, 2 * D_FF)), vec, vec],
        out_specs=[_rows(ts, 2 * D_FF), _rows(ts, D_FF), _rows(ts, d), _rows(ts, d),
                   _full((1, 128)), vec, vec],
        out_shape=[_sds((s, 2 * D_FF), BF16), _sds((s, D_FF), BF16), _sds((s, d), BF16), _sds((s, d), F32),
                   _sds((1, 128), F32), _sds((1, d), F32), _sds((1, d), F32)],
        scratch_shapes=[pltpu.VMEM((SUBLANES, 2 * D_FF), F32)],
        compiler_params=_params(("arbitrary",)),
    )(h2, x1, tgt, w_up4, w_down, fw, fb, gt_f, g_post)


def _ffn_bwd_a(dy2, up0, w_down, fw, fb, ts=256):
    s, d = dy2.shape
    nt = s // ts
    nch = D_FF // FFN_CHUNK
    wide = 2 * D_FF

    def body(dy2_ref, up0_ref, upprev_ref, wdn_ref, fw_ref, fb_ref, dup0_ref, dfw_ref, dfb_ref, next_ref):
        i = pl.program_id(0)

        @pl.when(i == 0)
        def _():
            next_ref[...] = jnp.zeros_like(next_ref)
            dfw_ref[...] = jnp.zeros_like(dfw_ref)
            dfb_ref[...] = jnp.zeros_like(dfb_ref)

        first_tile = i == nt - 1
        dyb = dy2_ref[...]
        for j in range(nch):
            _, _, col = _ffn_cols(j)
            dact = _dot_nt(dyb, wdn_ref[col:col + FFN_CHUNK, :])
            pre, shifted = [], []
            for c0 in (col, D_FF + col):
                cs = slice(c0, c0 + FFN_CHUNK)
                u = up0_ref[:, cs].astype(F32)
                prev8 = jnp.where(first_tile, 0.0, upprev_ref[:, cs].astype(F32)[SUBLANES:, :])
                sd = (u, _shift_down(u, prev8, 1), _shift_down(u, prev8, 2))
                shifted.append(sd)
                pre.append(fw_ref[2:3, cs] * sd[0] + fw_ref[1:2, cs] * sd[1] + fw_ref[0:1, cs] * sd[2]
                           + fb_ref[:, cs])
            gl, dgl = _gelu_and_grad(pre[0])
            dpre = (dact * pre[1] * dgl, dact * gl)
            for half, c0 in enumerate((col, D_FF + col)):
                cs = slice(c0, c0 + FFN_CHUNK)
                dp = dpre[half]
                sd = shifted[half]
                dfb_ref[:, cs] += _colsum(dp)
                dfw_ref[2:3, cs] += _colsum(dp * sd[0])
                dfw_ref[1:2, cs] += _colsum(dp * sd[1])
                dfw_ref[0:1, cs] += _colsum(dp * sd[2])
                nxt = next_ref[:, cs]
                next_ref[:, cs] = dp[0:SUBLANES, :]
                dup0 = (fw_ref[2:3, cs] * dp + fw_ref[1:2, cs] * _shift_up(dp, nxt, 1)
                        + fw_ref[0:1, cs] * _shift_up(dp, nxt, 2))
                dup0_ref[:, cs] = dup0.astype(BF16)

    return pl.pallas_call(
        body, grid=(nt,), name="ffn_bwd_a",
        in_specs=[_rows(ts, d, nt), _rows(ts, wide, nt), _halo_prev(ts, wide, BF16_SUBLANES, nt), _RESIDENT,
                  _full((3, wide)), _full((1, wide))],
        out_specs=[_rows(ts, wide, nt), _full((3, wide)), _full((1, wide))],
        out_shape=[_sds((s, wide), BF16), _sds((3, wide), F32), _sds((1, wide), F32)],
        scratch_shapes=[pltpu.VMEM((SUBLANES, wide), F32)],
        compiler_params=_params(("arbitrary",)),
    )(dy2, up0, up0, w_down, fw, fb)


def _ffn_bwd_b(dup0, x1, y, dx2, w_up4, g_pre2, sc_f, sh_f, gt_m, g_post_m, ts=256):
    s, d = x1.shape
    shard_cols = 2 * D_FF // N_CHIPS

    def body(dup_ref, x1_ref, y_ref, dx2_ref, wup_ref, g2_ref, sc_ref, sh_ref, gt_ref, gp_ref,
             dx1_ref, dy_ref, dsh_ref, dsc_ref, dg2_ref, dgt_ref, dgp_ref):
        i = pl.program_id(0)

        @pl.when(i == 0)
        def _():
            for ref in (dsh_ref, dsc_ref, dg2_ref, dgt_ref, dgp_ref):
                ref[...] = jnp.zeros_like(ref)

        dh2 = jnp.zeros((ts, d), F32)
        for k in range(N_CHIPS):
            dh2 = dh2 + _dot_nt(dup_ref[:, k * shard_cols:(k + 1) * shard_cols], wup_ref[k])
        x1v = x1_ref[...]
        r2 = _msq_rsqrt(x1v)
        xn = x1v * r2
        hn = xn * g2_ref[...]
        dsh_ref[...] += _colsum(dh2)
        dsc_ref[...] += _colsum(dh2 * hn)
        dhn = dh2 * (1.0 + sc_ref[...])
        dg2_ref[...] += _colsum(dhn * xn)
        dx1 = dx2_ref[...] + _rms_bwd(dhn * g2_ref[...], xn, r2)
        dx1_ref[...] = dx1
        yv = y_ref[...]
        ry = _msq_rsqrt(yv)
        yn = yv * ry
        dgt_ref[...] += _colsum(dx1 * (yn * gp_ref[...]))
        dyng = dx1 * gt_ref[...]
        dgp_ref[...] += _colsum(dyng * yn)
        dy_ref[...] = _rms_bwd(dyng * gp_ref[...], yn, ry).astype(BF16)

    vec = _full((1, d))
    return pl.pallas_call(
        body, grid=(s // ts,), name="ffn_bwd_b",
        in_specs=[_rows(ts, 2 * D_FF), _rows(ts, d), _rows(ts, d), _rows(ts, d), _RESIDENT,
                  vec, vec, vec, vec, vec],
        out_specs=[_rows(ts, d), _rows(ts, d), vec, vec, vec, vec, vec],
        out_shape=[_sds((s, d), F32), _sds((s, d), BF16)] + [_sds((1, d), F32)] * 5,
        compiler_params=_params(("arbitrary",)),
    )(dup0, x1, y, dx2, w_up4, g_pre2, sc_f, sh_f, gt_m, g_post_m)


def _seqmix_bwd(z, hst, dy, w_out, seq_params, ws_t, glo, ggo, ts=256):
    s = z.shape[0]
    nt = s // ts
    small_shapes = [(4, 512), (1, 512), (512, 512), (512, 512), (1, 512), (1, 512), (1, 512),
                    (1, 512), (1, 512), (4, 128, 128), (128, 4), (1, 512), (1, 512)]

    def body(z_ref, zprev_ref, hst_ref, hprev_ref, dy_ref, wout_ref, *rest):
        p = rest[:11]
        wst_ref, glo_ref, ggo_ref = rest[11:14]
        dz_ref = rest[14]
        (dcw_ref, dcb_ref, dwr_ref, dwi_ref, dbr_ref, dbi_ref, dspa_ref, dng_ref, dnb_ref, dws_ref, dbs_ref,
         dglo_ref, dggo_ref) = rest[15:28]
        gcarry, anext, dxcnext, sp_scr, dv_scr = rest[28:]
        i = pl.program_id(0)

        @pl.when(i == 0)
        def _():
            for ref in rest[15:28]:
                ref[...] = jnp.zeros_like(ref)
            gcarry[...] = jnp.zeros_like(gcarry)
            anext[...] = jnp.ones_like(anext)
            dxcnext[...] = jnp.zeros_like(dxcnext)

        first_tile = i == nt - 1
        f = _seq_recompute(z_ref, zprev_ref, first_tile, p)
        xc, r, ig, a, mult, lx = f["xc"], f["r"], f["ig"], f["a"], f["mult"], f["lx"]
        h = hst_ref[...]
        hprev = _shift_down(h, jnp.where(first_tile, 0.0, hprev_ref[...]), 1)
        gl, dgl = _gelu_and_grad(f["lg"])
        y_l = h * gl
        gm = _gmlp_fwd(f["gu"], f["gv"], p[7], p[8], p[9], p[10], sp_scr)
        y_g = gm["y_g"]

        dycat = _dot_nt(dy_ref[...], wout_ref[...])
        rl = _msq_rsqrt(y_l)
        yln = y_l * rl
        dyl = dycat[:, 0:512]
        dglo_ref[...] += _colsum(dyl * yln)
        dy_l = _rms_bwd(dyl * glo_ref[...], yln, rl)
        rg = _msq_rsqrt(y_g)
        ygn = y_g * rg
        dyg = dycat[:, 512:1024]
        dggo_ref[...] += _colsum(dyg * ygn)
        dy_g = _rms_bwd(dyg * ggo_ref[...], ygn, rg)

        dz_ref[:, 512:1024] = (dy_l * h * dgl).astype(BF16)
        a_up = _shift_up(a, anext[...], 1)
        acum, gloc = _scan_bwd(a_up, dy_l * gl)
        gg = gloc + acum * gcarry[...]
        gcarry[...] = gg[0:1, :]
        anext[...] = a[0:SUBLANES, :]
        da = gg * hprev
        t1 = gg * mult
        di = t1 * xc
        dxc = t1 * ig
        dmult = gg * ig * xc
        dla = da * a - dmult * (a * a / mult)
        spa = f["spa"]
        dspa_ref[...] += _colsum(dla * r) * (-LRU_C)
        dpr = dla * ((-LRU_C) * spa) * r * (1.0 - r)
        dpi = di * ig * (1.0 - ig)
        dbr_ref[...] += _colsum(dpr)
        dbi_ref[...] += _colsum(dpi)
        dprb = dpr.astype(BF16)
        dpib = dpi.astype(BF16)
        dwr_ref[...] += _dot_tn(f["xcb"], dprb)
        dwi_ref[...] += _dot_tn(f["xcb"], dpib)
        dxc = dxc + _dot_nt(dprb, p[2][...]) + _dot_nt(dpib, p[3][...])
        dcb_ref[...] += _colsum(dxc)
        dcw_ref[3:4, :] += _colsum(dxc * lx)
        dcw_ref[2:3, :] += _colsum(dxc * f["s1"])
        dcw_ref[1:2, :] += _colsum(dxc * f["s2"])
        dcw_ref[0:1, :] += _colsum(dxc * f["s3"])
        nxt = dxcnext[...]
        dxcnext[...] = dxc[0:SUBLANES, :]
        cw_ref = p[0]
        dlx = (cw_ref[3:4, :] * dxc + cw_ref[2:3, :] * _shift_up(dxc, nxt, 1)
               + cw_ref[1:2, :] * _shift_up(dxc, nxt, 2) + cw_ref[0:1, :] * _shift_up(dxc, nxt, 3))
        dz_ref[:, 0:512] = dlx.astype(BF16)

        dz_ref[:, 1024:1536] = (dy_g * gm["spb"] * gm["du"]).astype(BF16)
        dsp = dy_g * gm["u"]
        vb = gm["vb"]
        for n in range(ts // GMLP_BLOCK):
            rs = slice(n * GMLP_BLOCK, (n + 1) * GMLP_BLOCK)
            for g in range(GMLP_GROUPS):
                cs = slice(g * 128, (g + 1) * 128)
                dbs_ref[:, g:g + 1] += jnp.sum(dsp[rs, cs], axis=1, keepdims=True)
                blk = dsp[rs, cs].astype(BF16)
                dws_ref[g] += _dot_nt(blk, vb[rs, cs])
                dv_scr[rs, cs] = _dot(wst_ref[g], blk)
        dv = dv_scr[...]
        vhat = gm["vhat"]
        dng_ref[...] += _colsum(dv * vhat)
        dnb_ref[...] += _colsum(dv)
        dvh = dv * p[7][...]
        dvg = gm["rstd"] * (dvh - jnp.mean(dvh, axis=-1, keepdims=True)
                            - vhat * jnp.mean(dvh * vhat, axis=-1, keepdims=True))
        dz_ref[:, 1536:2048] = (dvg * gm["dvg"]).astype(BF16)

        @pl.when(i == nt - 1)
        def _():
            pos = lax.broadcasted_iota(jnp.int32, (GMLP_BLOCK, GMLP_BLOCK), 0) // CHUNK
            src = lax.broadcasted_iota(jnp.int32, (GMLP_BLOCK, GMLP_BLOCK), 1) // CHUNK
            for g in range(GMLP_GROUPS):
                dws_ref[g] = jnp.where(src <= pos, dws_ref[g], 0.0)
            dspa_ref[...] = dspa_ref[...] * (-_sigmoid(-p[6][...]))

    in_specs = (_seq_specs(ts, nt, True)
                + [_rows(ts, 512, nt), _halo_prev(ts, 512, SUBLANES, nt), _rows(ts, 1024, nt), _full((1024, 1024))]
                + _seq_param_specs() + [_full((4, 128, 128)), _full((1, 512)), _full((1, 512))])
    return pl.pallas_call(
        body, grid=(nt,), name="seqmix_bwd",
        in_specs=in_specs,
        out_specs=[_rows(ts, 2048, nt)] + [_full(sh) for sh in small_shapes],
        out_shape=[_sds((s, 2048), BF16)] + [_sds(sh, F32) for sh in small_shapes],
        scratch_shapes=[pltpu.VMEM((1, 512), F32), pltpu.VMEM((SUBLANES, 512), F32),
                        pltpu.VMEM((SUBLANES, 512), F32), pltpu.VMEM((ts, 512), F32), pltpu.VMEM((ts, 512), F32)],
        compiler_params=_params(("arbitrary",)),
    )(z, z, hst, hst, dy, w_out, *seq_params, ws_t, glo, ggo)


def _mix_in_bwd(x, dz, dx1, w_in4, g, sc, ts=256):
    s, d = x.shape

    def body(x_ref, dz_ref, dx1_ref, w_ref, g_ref, sc_ref, gx_ref, dsh_ref, dsc_ref, dg_ref):
        i = pl.program_id(0)

        @pl.when(i == 0)
        def _():
            for ref in (dsh_ref, dsc_ref, dg_ref):
                ref[...] = jnp.zeros_like(ref)

        dh = jnp.zeros((ts, d), F32)
        for k in range(N_CHIPS):
            dh = dh + _dot_nt(dz_ref[:, k * 512:(k + 1) * 512], w_ref[k])
        xv = x_ref[...]
        r = _msq_rsqrt(xv)
        xn = xv * r
        dsh_ref[...] += _colsum(dh)
        dsc_ref[...] += _colsum(dh * (xn * g_ref[...]))
        dhn = dh * (1.0 + sc_ref[...])
        dg_ref[...] += _colsum(dhn * xn)
        gx_ref[...] = dx1_ref[...] + _rms_bwd(dhn * g_ref[...], xn, r)

    vec = _full((1, d))
    return pl.pallas_call(
        body, grid=(s // ts,), name="mix_in_bwd",
        in_specs=[_rows(ts, d), _rows(ts, 2048), _rows(ts, d), _full(w_in4.shape), vec, vec],
        out_specs=[_rows(ts, d), vec, vec, vec],
        out_shape=[_sds((s, d), F32)] + [_sds((1, d), F32)] * 3,
        compiler_params=_params(("arbitrary",)),
    )(x, dz, dx1, w_in4, g, sc)


def _wgrad(a, b, n_chunks, name, chunk_major, ts=512):
    s, m = a.shape
    n = b.shape[1]
    nc = n // n_chunks
    nt = s // ts

    def body(a_ref, b_ref, o_ref, acc):
        i = pl.program_id(1)

        @pl.when(i == 0)
        def _():
            acc[...] = jnp.zeros_like(acc)

        acc[...] += _dot_tn(a_ref[...], b_ref[...])

        @pl.when(i == nt - 1)
        def _():
            if chunk_major:
                o_ref[0] = acc[...].astype(BF16)
            else:
                o_ref[...] = acc[...].astype(BF16)

    if chunk_major:
        out_spec, out_shape = pl.BlockSpec((1, m, nc), lambda c, i: (c, 0, 0)), _sds((n_chunks, m, nc), BF16)
    else:
        out_spec, out_shape = pl.BlockSpec((m, nc), lambda c, i: (0, c)), _sds((m, n), BF16)
    return pl.pallas_call(
        body, grid=(n_chunks, nt), name=name,
        in_specs=[pl.BlockSpec((ts, m), lambda c, i: (i, 0)), pl.BlockSpec((ts, nc), lambda c, i: (i, c))],
        out_specs=out_spec,
        out_shape=out_shape,
        scratch_shapes=[pltpu.VMEM((m, nc), F32)],
        compiler_params=_params(("parallel", "arbitrary")),
    )(a, b)


def _block_diag(w):
    heads, hd, _ = w.shape
    eye = jnp.eye(heads, dtype=w.dtype)
    return (eye[:, None, :, None] * w[:, :, None, :]).reshape(heads * hd, heads * hd)


def _diag_blocks(m):
    hd = LRU_WIDTH // LRU_HEADS
    m4 = m.reshape(LRU_HEADS, hd, LRU_HEADS, hd)
    return jnp.stack([m4[k, :, k, :] for k in range(LRU_HEADS)])


def _seq_params(small):
    row = lambda v: v.reshape(1, -1)
    pos = jnp.arange(GMLP_BLOCK)
    mask = (pos[None, :] // CHUNK) <= (pos[:, None] // CHUNK)
    ws = jnp.where(mask[None], small["w_spatial"], 0.0)
    seq_params = (small["conv_w"], row(small["conv_b"]),
                  _block_diag(small["w_rgate"]).astype(BF16), _block_diag(small["w_igate"]).astype(BF16),
                  row(small["b_rgate"]), row(small["b_igate"]), row(small["lru_a"]),
                  row(small["v_norm_g"]), row(small["v_norm_b"]), ws.astype(BF16), small["b_spatial"].T)
    return seq_params, jnp.swapaxes(ws, 1, 2).astype(BF16)


_ANY = pl.BlockSpec(memory_space=pl.ANY)
_CHIP_FLIPS = ((1, 0), (0, 1), (1, 1))


def _position():
    return lax.axis_index("x"), lax.axis_index("y"), lax.axis_index("c")


def _flip(v, f):
    return 1 - v if f else v


def _remote(src, dst, send_sem, recv_sem, peer):
    return pltpu.make_async_remote_copy(src_ref=src, dst_ref=dst, send_sem=send_sem, recv_sem=recv_sem,
                                        device_id=peer, device_id_type=MESH)


def _allgather8(block, name, reduce):
    r, n = block.shape

    def body(x_ref, out_ref, *scratch):
        if reduce:
            gath, send_sems, recv_sems, loc_sem = scratch
        else:
            gath = out_ref
            send_sems, recv_sems, loc_sem = scratch
        x, y, c = _position()
        me = 4 * x + 2 * y + c
        loc = pltpu.make_async_copy(x_ref, gath.at[me], loc_sem)
        loc.start()
        peers = []
        for k in range(1, N_DEV):
            px, py, pc = _flip(x, k & 4), _flip(y, k & 2), _flip(c, k & 1)
            peers.append((px, py, pc))
            _remote(x_ref, gath.at[me], send_sems.at[k - 1], recv_sems.at[k - 1], (px, py, pc)).start()
        for k, (px, py, pc) in enumerate(peers):
            src = 4 * px + 2 * py + pc
            _remote(x_ref, gath.at[src], send_sems.at[k], recv_sems.at[k], (px, py, pc)).wait_recv()
        for k, peer in enumerate(peers):
            _remote(x_ref, gath.at[me], send_sems.at[k], recv_sems.at[k], peer).wait_send()
        loc.wait()
        if reduce:
            acc = gath[0]
            for k in range(1, N_DEV):
                acc = acc + gath[k]
            out_ref[...] = acc

    sems = [pltpu.SemaphoreType.DMA((N_DEV - 1,)), pltpu.SemaphoreType.DMA((N_DEV - 1,)), pltpu.SemaphoreType.DMA]
    if reduce:
        out_shape = _sds((r, n), F32)
        scratch = [pltpu.VMEM((N_DEV, r, n), F32)] + sems
    else:
        out_shape = _sds((N_DEV, r, n), F32)
        scratch = sems
    return pl.pallas_call(
        body, name=name, out_shape=out_shape,
        in_specs=[pl.BlockSpec(memory_space=pltpu.VMEM)], out_specs=pl.BlockSpec(memory_space=pltpu.VMEM),
        scratch_shapes=scratch,
        compiler_params=pltpu.CompilerParams(vmem_limit_bytes=VMEM_LIMIT_BYTES),
    )(block)


def _half(ref, c, rows):
    hr = rows // 2
    return ref.at[pl.ds(pl.multiple_of(c * hr, BF16_SUBLANES), hr), :]


def _gather_weights(shards):
    na = len(shards)

    def body(*refs):
        ins, outs = refs[:na], refs[na:2 * na]
        ici_send, ici_recv, d2d_send, d2d_recv, loc_sem = refs[2 * na:]
        x, y, c = _position()
        chip = 2 * x + y
        sibling = (x, y, 1 - c)
        local = []
        for a in range(na):
            local.append(pltpu.make_async_copy(ins[a], outs[a].at[chip], loc_sem.at[a]))
            local[-1].start()
        sends = []
        for a in range(na):
            rows = shards[a].shape[0]
            for j, (fx, fy) in enumerate(_CHIP_FLIPS):
                peer = (_flip(x, fx), _flip(y, fy), c)
                sends.append(_remote(_half(ins[a], c, rows), _half(outs[a].at[chip], c, rows),
                                     ici_send.at[a * 3 + j], ici_recv.at[a * 3 + j], peer))
                sends[-1].start()
        for a in range(na):
            rows = shards[a].shape[0]
            for j, (fx, fy) in enumerate(_CHIP_FLIPS):
                src_chip = 2 * _flip(x, fx) + _flip(y, fy)
                landed = _half(outs[a].at[src_chip], c, rows)
                _remote(landed, landed, ici_send.at[a * 3 + j], ici_recv.at[a * 3 + j], sibling).wait_recv()
                sends.append(_remote(landed, landed, d2d_send.at[a * 3 + j], d2d_recv.at[a * 3 + j], sibling))
                sends[-1].start()
        for a in range(na):
            rows = shards[a].shape[0]
            for j, (fx, fy) in enumerate(_CHIP_FLIPS):
                src_chip = 2 * _flip(x, fx) + _flip(y, fy)
                other = _half(outs[a].at[src_chip], 1 - c, rows)
                _remote(other, other, d2d_send.at[a * 3 + j], d2d_recv.at[a * 3 + j], sibling).wait_recv()
        for cp in sends:
            cp.wait_send()
        for cp in local:
            cp.wait()

    return pl.pallas_call(
        body, name="gather_weights",
        out_shape=[_sds((N_CHIPS,) + w.shape, w.dtype) for w in shards],
        in_specs=[_ANY] * na, out_specs=[_ANY] * na,
        scratch_shapes=[pltpu.SemaphoreType.DMA((3 * na,))] * 4 + [pltpu.SemaphoreType.DMA((na,))],
    )(*shards)


def _swap_halves(parts, name):
    na = len(parts)

    def body(*refs):
        ins, outs = refs[:na], refs[na:2 * na]
        send_sems, recv_sems = refs[2 * na:]
        x, y, c = _position()
        sibling = (x, y, 1 - c)
        cps = []
        for a in range(na):
            hr = parts[a].shape[1] // 2
            src = ins[a].at[:, pl.ds(pl.multiple_of((1 - c) * hr, BF16_SUBLANES), hr), :]
            cps.append(_remote(src, outs[a], send_sems.at[a], recv_sems.at[a], sibling))
            cps[-1].start()
        for cp in cps:
            cp.wait()

    return pl.pallas_call(
        body, name=name,
        out_shape=[_sds((N_CHIPS, p.shape[1] // 2, p.shape[2]), p.dtype) for p in parts],
        in_specs=[_ANY] * na, out_specs=[_ANY] * na,
        scratch_shapes=[pltpu.SemaphoreType.DMA((na,))] * 2,
    )(*parts)


def _chip_sum(part, recv, pos_arr, name):
    _, rows, cols = part.shape
    hr = rows // 2

    def body(pos_ref, p_ref, r_ref, o_ref, g_ref):
        total = (p_ref[...].astype(F32) + r_ref[...].astype(F32)).astype(BF16)
        o_ref[...] = total

        @pl.when(pl.program_id(0) == pos_ref[1])
        def _():
            g_ref[0] = total

    grid_spec = pltpu.PrefetchScalarGridSpec(
        num_scalar_prefetch=1, grid=(N_CHIPS,),
        in_specs=[pl.BlockSpec((1, hr, cols), lambda k, pos: (k, pos[0], 0)),
                  pl.BlockSpec((1, hr, cols), lambda k, pos: (k, 0, 0))],
        out_specs=[pl.BlockSpec((1, hr, cols), lambda k, pos: (k, 0, 0)),
                   pl.BlockSpec((1, 1, hr, cols), lambda k, pos: (0, pos[1], 0, 0))])
    return pl.pallas_call(
        body, name=name, grid_spec=grid_spec,
        out_shape=[_sds((N_CHIPS, hr, cols), BF16), _sds((2, N_CHIPS, hr, cols), BF16)],
        compiler_params=_params(("arbitrary",)),
    )(pos_arr, part, recv)


def _exchange_chips(sums):
    na = len(sums)

    def body(*refs):
        ins, outs = refs[:na], refs[na:2 * na]
        send_sems, recv_sems, loc_sem = refs[2 * na:]
        x, y, c = _position()
        chip = 2 * x + y
        local = []
        for a in range(na):
            local.append(pltpu.make_async_copy(ins[a].at[chip], outs[a].at[chip], loc_sem.at[a]))
            local[-1].start()
        cps = []
        for a in range(na):
            for j, (fx, fy) in enumerate(_CHIP_FLIPS):
                px, py = _flip(x, fx), _flip(y, fy)
                cps.append(_remote(ins[a].at[2 * px + py], outs[a].at[chip],
                                   send_sems.at[a * 3 + j], recv_sems.at[a * 3 + j], (px, py, c)))
                cps[-1].start()
        for a in range(na):
            for j, (fx, fy) in enumerate(_CHIP_FLIPS):
                src_chip = 2 * _flip(x, fx) + _flip(y, fy)
                landed = outs[a].at[src_chip]
                _remote(landed, landed, send_sems.at[a * 3 + j], recv_sems.at[a * 3 + j], (x, y, c)).wait_recv()
        for cp in cps:
            cp.wait_send()
        for cp in local:
            cp.wait()

    return pl.pallas_call(
        body, name="exchange_chips",
        out_shape=[_sds(s.shape, s.dtype) for s in sums],
        in_specs=[_ANY] * na, out_specs=[_ANY] * na,
        scratch_shapes=[pltpu.SemaphoreType.DMA((3 * na,))] * 2 + [pltpu.SemaphoreType.DMA((na,))],
    )(*sums)


def _sum_chips(gath, name, tr=128):
    _, hr, cols = gath.shape
    tr = min(tr, hr)

    def body(g_ref, o_ref):
        acc = g_ref[0].astype(F32)
        for k in range(1, N_CHIPS):
            acc = acc + g_ref[k].astype(F32)
        o_ref[...] = acc

    return pl.pallas_call(
        body, name=name, grid=(hr // tr,),
        in_specs=[pl.BlockSpec((N_CHIPS, tr, cols), lambda i: (0, i, 0))],
        out_specs=pl.BlockSpec((tr, cols), lambda i: (i, 0)),
        out_shape=_sds((hr, cols), F32),
        compiler_params=_params(("parallel",)),
    )(gath)


def _join_halves(halves):
    na = len(halves)

    def body(*refs):
        ins, outs = refs[:na], refs[na:2 * na]
        send_sems, recv_sems, loc_sem = refs[2 * na:]
        x, y, c = _position()
        sibling = (x, y, 1 - c)
        cps, local = [], []
        for a in range(na):
            rows = 2 * halves[a].shape[0]
            mine = _half(outs[a], c, rows)
            local.append(pltpu.make_async_copy(ins[a], mine, loc_sem.at[a]))
            local[-1].start()
            cps.append(_remote(ins[a], mine, send_sems.at[a], recv_sems.at[a], sibling))
            cps[-1].start()
        for a in range(na):
            rows = 2 * halves[a].shape[0]
            other = _half(outs[a], 1 - c, rows)
            _remote(ins[a], other, send_sems.at[a], recv_sems.at[a], sibling).wait_recv()
        for cp in cps:
            cp.wait_send()
        for cp in local:
            cp.wait()

    return pl.pallas_call(
        body, name="join_halves",
        out_shape=[_sds((2 * h.shape[0], h.shape[1]), h.dtype) for h in halves],
        in_specs=[_ANY] * na, out_specs=[_ANY] * na,
        scratch_shapes=[pltpu.SemaphoreType.DMA((na,))] * 3,
    )(*halves)


_HBM = pl.BlockSpec(memory_space=pltpu.HBM)
_SEM = pl.BlockSpec(memory_space=pltpu.SEMAPHORE)
_EFFECT = pltpu.SideEffectType.DATAFLOW_SIDE_EFFECTING


def _in_hbm(a):
    return pltpu.with_memory_space_constraint(a, pltpu.HBM)


def _split_start(srcs, lands, plan, n_copies, after, name):
    ns, nl = len(srcs), len(lands)
    bufs = list(srcs) + list(lands)

    def body(*refs):
        send_sems, recv_sems = refs[ns + nl + 1], refs[ns + nl + 2]
        token = refs[-1]
        for k, (src, dst, peer) in enumerate(plan(refs[:ns], refs[ns:ns + nl])):
            _remote(src, dst, send_sems.at[k], recv_sems.at[k], peer).start()
        token[...] = jnp.zeros_like(token)

    out = pl.pallas_call(
        body, name=name,
        out_shape=(pltpu.SemaphoreType.DMA((n_copies,)), pltpu.SemaphoreType.DMA((n_copies,)),
                   *[pltpu.HBM(b.shape, b.dtype) for b in bufs], _sds((SUBLANES, 128), F32)),
        in_specs=[_HBM] * (ns + nl) + [_ANY],
        out_specs=(_SEM, _SEM, *[_HBM] * (ns + nl), pl.BlockSpec(memory_space=pltpu.VMEM)),
        input_output_aliases={i: 2 + i for i in range(ns + nl)},
        compiler_params=pltpu.CompilerParams(has_side_effects=_EFFECT),
    )(*[_in_hbm(b) for b in bufs], after)
    return out[0], out[1], list(out[2:2 + ns]), list(out[2 + ns:2 + ns + nl]), out[-1]


def _split_wait(send_sems, recv_sems, srcs, lands, plan, after, name):
    ns, nl = len(srcs), len(lands)
    bufs = list(srcs) + list(lands)

    def body(*refs):
        send_ref, recv_ref = refs[ns + nl], refs[ns + nl + 1]
        me = _position()
        for k, src, dst in plan(refs[:ns], refs[ns:ns + nl]):
            cp = _remote(src, dst, send_ref.at[k], recv_ref.at[k], me)
            cp.wait_send()
            cp.wait_recv()

    out = pl.pallas_call(
        body, name=name,
        out_shape=[pltpu.HBM(b.shape, b.dtype) for b in bufs],
        in_specs=[_HBM] * (ns + nl) + [_SEM, _SEM, _ANY],
        out_specs=[_HBM] * (ns + nl),
        input_output_aliases={i: i for i in range(ns + nl)},
        compiler_params=pltpu.CompilerParams(has_side_effects=_EFFECT),
    )(*bufs, send_sems, recv_sems, after)
    return list(out[:ns]), list(out[ns:])


def _gather_plan(rows_of):
    def start(src_refs, land_refs):
        x, y, c = _position()
        chip = 2 * x + y
        out = []
        for a, rows in enumerate(rows_of):
            mine = _half(land_refs[a].at[chip], c, rows)
            out.extend((mine, mine, (_flip(x, fx), _flip(y, fy), c)) for fx, fy in _CHIP_FLIPS)
        return out

    def wait(src_refs, land_refs):
        x, y, c = _position()
        chip = 2 * x + y
        out = []
        for a, rows in enumerate(rows_of):
            for j, (fx, fy) in enumerate(_CHIP_FLIPS):
                src_chip = 2 * _flip(x, fx) + _flip(y, fy)
                out.append((3 * a + j, _half(land_refs[a].at[chip], c, rows),
                            _half(land_refs[a].at[src_chip], c, rows)))
        return out

    return start, wait


def _exchange_plan(n_arrays):
    def start(src_refs, land_refs):
        x, y, c = _position()
        chip = 2 * x + y
        out = []
        for a in range(n_arrays):
            for fx, fy in _CHIP_FLIPS:
                px, py = _flip(x, fx), _flip(y, fy)
                out.append((src_refs[a].at[2 * px + py], land_refs[a].at[0, chip], (px, py, c)))
        return out

    def wait(src_refs, land_refs):
        x, y, c = _position()
        out = []
        for a in range(n_arrays):
            for j, (fx, fy) in enumerate(_CHIP_FLIPS):
                src_chip = 2 * _flip(x, fx) + _flip(y, fy)
                out.append((3 * a + j, src_refs[a].at[src_chip], land_refs[a].at[0, src_chip]))
        return out

    return start, wait


def _forward_to_sibling(lands, name):
    na = len(lands)

    def body(*refs):
        land_refs = refs[na:2 * na]
        send_sems, recv_sems = refs[2 * na:]
        x, y, c = _position()
        sibling = (x, y, 1 - c)
        sends = []
        for a in range(na):
            rows = lands[a].shape[1]
            for j, (fx, fy) in enumerate(_CHIP_FLIPS):
                landed = _half(land_refs[a].at[2 * _flip(x, fx) + _flip(y, fy)], c, rows)
                sends.append(_remote(landed, landed, send_sems.at[3 * a + j], recv_sems.at[3 * a + j], sibling))
                sends[-1].start()
        for a in range(na):
            rows = lands[a].shape[1]
            for j, (fx, fy) in enumerate(_CHIP_FLIPS):
                other = _half(land_refs[a].at[2 * _flip(x, fx) + _flip(y, fy)], 1 - c, rows)
                _remote(other, other, send_sems.at[3 * a + j], recv_sems.at[3 * a + j], sibling).wait_recv()
        for cp in sends:
            cp.wait_send()

    return pl.pallas_call(
        body, name=name,
        out_shape=[_sds(l.shape, l.dtype) for l in lands],
        in_specs=[_ANY] * na, out_specs=[_ANY] * na,
        input_output_aliases={a: a for a in range(na)},
        scratch_shapes=[pltpu.SemaphoreType.DMA((3 * na,))] * 2,
    )(*lands)


def _swap_gathered(gath, name):
    na = len(gath)

    def body(*refs):
        gath_refs = refs[na:2 * na]
        send_sems, recv_sems = refs[2 * na:]
        x, y, c = _position()
        cps = [_remote(gath_refs[a].at[0], gath_refs[a].at[1], send_sems.at[a], recv_sems.at[a], (x, y, 1 - c))
               for a in range(na)]
        for cp in cps:
            cp.start()
        for cp in cps:
            cp.wait()

    return pl.pallas_call(
        body, name=name,
        out_shape=[_sds(g.shape, g.dtype) for g in gath],
        in_specs=[_ANY] * na, out_specs=[_ANY] * na,
        input_output_aliases={a: a for a in range(na)},
        scratch_shapes=[pltpu.SemaphoreType.DMA((na,))] * 2,
    )(*gath)


def _adam_gathered(w, gath, m, v, c_arr, name, tr=128):
    rows, cols = w.shape
    hr = rows // 2
    per = hr // tr

    def body(c_ref, w_ref, g_ref, m_ref, v_ref, go_ref, d_ref, nm_ref, nv_ref):
        g = g_ref[0, 0].astype(F32)
        for k in range(1, N_CHIPS):
            g = g + g_ref[0, k].astype(F32)
        go_ref[...] = g
        d_ref[...], nm_ref[...], nv_ref[...] = _adam_math(w_ref[...], g, m_ref[...], v_ref[...])

    def rows_of(h, i, c_ref):
        c = c_ref[0]
        return ((c + h - 2 * c * h) * per + i, 0)

    blk = pl.BlockSpec((tr, cols), rows_of)
    grid_spec = pltpu.PrefetchScalarGridSpec(
        num_scalar_prefetch=1, grid=(2, per),
        in_specs=[blk, pl.BlockSpec((1, N_CHIPS, tr, cols), lambda h, i, c_ref: (h, 0, i, 0)), blk, blk],
        out_specs=[blk] * 4)
    return pl.pallas_call(
        body, name=name, grid_spec=grid_spec, out_shape=[_sds(w.shape, F32)] * 4,
        compiler_params=_params(("arbitrary", "arbitrary")),
    )(c_arr, w, gath, m, v)


def _allreduce_small(block, name):
    r, n = block.shape
    hr = r // 2

    def body(x_ref, out_ref, sib, chipsum, gath, d2d_send, d2d_recv, ici_send, ici_recv):
        x, y, c = _position()
        chip = 2 * x + y
        sibling = (x, y, 1 - c)
        first = _remote(x_ref, sib, d2d_send.at[0], d2d_recv.at[0], sibling)
        first.start()
        first.wait()
        chipsum[...] = x_ref[...] + sib[...]
        mine = pl.ds(pl.multiple_of(c * hr, SUBLANES), hr)
        theirs = pl.ds(pl.multiple_of((1 - c) * hr, SUBLANES), hr)
        sends = []
        for j, (fx, fy) in enumerate(_CHIP_FLIPS):
            sends.append(_remote(chipsum.at[mine, :], gath.at[chip], ici_send.at[j], ici_recv.at[j],
                                 (_flip(x, fx), _flip(y, fy), c)))
            sends[-1].start()
        gath[chip] = chipsum[mine, :]
        for j, (fx, fy) in enumerate(_CHIP_FLIPS):
            landed = gath.at[2 * _flip(x, fx) + _flip(y, fy)]
            _remote(landed, landed, ici_send.at[j], ici_recv.at[j], sibling).wait_recv()
        for cp in sends:
            cp.wait_send()
        total = gath[0]
        for k in range(1, N_CHIPS):
            total = total + gath[k]
        out_ref[mine, :] = total
        last = _remote(out_ref.at[mine, :], out_ref.at[mine, :], d2d_send.at[1], d2d_recv.at[1], sibling)
        last.start()
        _remote(out_ref.at[theirs, :], out_ref.at[theirs, :], d2d_send.at[1], d2d_recv.at[1], sibling).wait_recv()
        last.wait_send()

    vmem = pl.BlockSpec(memory_space=pltpu.VMEM)
    return pl.pallas_call(
        body, name=name, out_shape=_sds((r, n), F32), in_specs=[vmem], out_specs=vmem,
        scratch_shapes=[pltpu.VMEM((r, n), F32), pltpu.VMEM((r, n), F32), pltpu.VMEM((N_CHIPS, hr, n), F32),
                        pltpu.SemaphoreType.DMA((2,)), pltpu.SemaphoreType.DMA((2,)),
                        pltpu.SemaphoreType.DMA((3,)), pltpu.SemaphoreType.DMA((3,))],
        compiler_params=pltpu.CompilerParams(vmem_limit_bytes=VMEM_LIMIT_BYTES),
    )(block)


def _cast_place(shards, chip_arr):
    na = len(shards)
    steps = 4

    def body(chip_ref, *refs):
        for a in range(na):
            refs[na + a][0] = refs[a][...].astype(BF16)

    grid_spec = pltpu.PrefetchScalarGridSpec(
        num_scalar_prefetch=1, grid=(steps,),
        in_specs=[pl.BlockSpec((s.shape[0] // steps, s.shape[1]), lambda i, ch: (i, 0)) for s in shards],
        out_specs=[pl.BlockSpec((1, s.shape[0] // steps, s.shape[1]), lambda i, ch: (ch[0], i, 0)) for s in shards])
    return pl.pallas_call(
        body, name="cast_place", grid_spec=grid_spec,
        out_shape=[_sds((N_CHIPS,) + s.shape, BF16) for s in shards],
        compiler_params=_params(("arbitrary",)),
    )(chip_arr, *shards)


def _silu(v):
    return v * _sigmoid(v)


def _ada_fwd(c8, w_ada):
    def body(c_ref, w_ref, o_ref):
        o_ref[...] = jnp.dot(_silu(c_ref[...]), w_ref[...], preferred_element_type=F32,
                             precision=lax.Precision.HIGHEST)

    return pl.pallas_call(
        body, name="ada_fwd", out_shape=_sds((N_DEV, w_ada.shape[1]), F32),
        compiler_params=pltpu.CompilerParams(vmem_limit_bytes=VMEM_LIMIT_BYTES),
    )(c8, w_ada)


def _mod_select(parts, b_ada, me_arr):
    cols = parts.shape[2]

    def body(me_ref, p_ref, b_ref, o_ref):
        me = me_ref[0]
        for k in range(N_CHIPS):
            cs = slice(k * cols, (k + 1) * cols)
            o_ref[:, cs] = p_ref[2 * k, pl.ds(me, 1), :] + b_ref[:, cs]

    grid_spec = pltpu.PrefetchScalarGridSpec(
        num_scalar_prefetch=1, grid=(1,),
        in_specs=[pl.BlockSpec(parts.shape, lambda i, m: (0, 0, 0)), pl.BlockSpec(b_ada.shape, lambda i, m: (0, 0))],
        out_specs=pl.BlockSpec(b_ada.shape, lambda i, m: (0, 0)))
    return pl.pallas_call(body, name="mod_select", grid_spec=grid_spec, out_shape=_sds(b_ada.shape, F32))(
        me_arr, parts, b_ada)


def _ada_bwd(c8, dmod8, chip_arr):
    d = c8.shape[1]
    cols = dmod8.shape[1] // N_CHIPS

    def body(chip_ref, c_ref, dm_ref, dmall_ref, gw_ref, gb_ref):
        gw_ref[...] = lax.dot_general(_silu(c_ref[...]), dm_ref[...], (((0,), (0,)), ((), ())),
                                      preferred_element_type=F32, precision=lax.Precision.HIGHEST)
        acc = dmall_ref[0:1, :]
        for k in range(1, N_DEV):
            acc = acc + dmall_ref[k:k + 1, :]
        gb_ref[...] = acc

    grid_spec = pltpu.PrefetchScalarGridSpec(
        num_scalar_prefetch=1, grid=(1,),
        in_specs=[pl.BlockSpec(c8.shape, lambda i, ch: (0, 0)),
                  pl.BlockSpec((N_DEV, cols), lambda i, ch: (0, ch[0])),
                  pl.BlockSpec(dmod8.shape, lambda i, ch: (0, 0))],
        out_specs=[pl.BlockSpec((d, cols), lambda i, ch: (0, 0)), pl.BlockSpec((1, dmod8.shape[1]), lambda i, ch: (0, 0))])
    return pl.pallas_call(
        body, name="ada_bwd", grid_spec=grid_spec,
        out_shape=[_sds((d, cols), F32), _sds((1, dmod8.shape[1]), F32)],
        compiler_params=_params(("arbitrary",)),
    )(chip_arr, c8, dmod8, dmod8)


def _adam_math(w, g, m, v):
    m = ADAM_B1 * m + (1.0 - ADAM_B1) * g
    v = ADAM_B2 * v + (1.0 - ADAM_B2) * (g * g)
    m_hat = m / (1.0 - ADAM_B1 ** ADAM_STEP)
    v_hat = v / (1.0 - ADAM_B2 ** ADAM_STEP)
    delta = -ADAM_LR * (m_hat / (jnp.sqrt(v_hat) + ADAM_EPS) + ADAM_WD * w)
    return delta, m, v


def _adam(w, g, m, v, name, tr=256):
    rows, cols = w.shape
    if rows % tr:
        tr = rows

    def body(w_ref, g_ref, m_ref, v_ref, d_ref, nm_ref, nv_ref):
        d_ref[...], nm_ref[...], nv_ref[...] = _adam_math(w_ref[...], g_ref[...], m_ref[...], v_ref[...])

    spec = pl.BlockSpec((tr, cols), lambda i: (i, 0))
    return pl.pallas_call(
        body, name=name, grid=(rows // tr,), in_specs=[spec] * 4, out_specs=[spec] * 3,
        out_shape=[_sds(w.shape, F32)] * 3, compiler_params=_params(("parallel",)),
    )(w, g, m, v)


def _adam_cols(w, g_full, m, v, chip_arr, name):
    rows, cols = w.shape

    def body(chip_ref, w_ref, g_ref, m_ref, v_ref, gs_ref, d_ref, nm_ref, nv_ref):
        g = g_ref[...]
        gs_ref[...] = g
        d_ref[...], nm_ref[...], nv_ref[...] = _adam_math(w_ref[...], g, m_ref[...], v_ref[...])

    own = pl.BlockSpec((rows, cols), lambda i, ch: (0, 0))
    grid_spec = pltpu.PrefetchScalarGridSpec(
        num_scalar_prefetch=1, grid=(1,),
        in_specs=[own, pl.BlockSpec((rows, cols), lambda i, ch: (0, ch[0])), own, own],
        out_specs=[own] * 4)
    return pl.pallas_call(body, name=name, grid_spec=grid_spec, out_shape=[_sds(w.shape, F32)] * 4)(
        chip_arr, w, g_full, m, v)


PACK_COLS = 512
SMALL_REPLICATED = ("g_mix_pre", "g_mix_post", "conv_b", "w_rgate", "b_rgate", "w_igate", "b_igate", "lru_a",
                    "v_norm_g", "v_norm_b", "w_spatial", "b_spatial", "g_lru_out", "g_gmlp_out", "g_ffn_pre",
                    "g_ffn_post", "ffn_conv_b")
SMALL_COLUMN_SHARDED = ("conv_w", "ffn_conv_w")


def _pack(arrays):
    parts = []
    for arr in arrays:
        p = arr.reshape(-1, PACK_COLS)
        pad = (-p.shape[0]) % SUBLANES
        parts.append(jnp.pad(p, ((0, pad), (0, 0))) if pad else p)
    total = sum(p.shape[0] for p in parts)
    if total % (2 * SUBLANES):
        parts.append(jnp.zeros((SUBLANES, PACK_COLS), parts[0].dtype))
    return jnp.concatenate(parts, axis=0)


def _unpack(packed, shapes):
    out, row = [], 0
    for shape in shapes:
        n = math.prod(shape) // PACK_COLS
        out.append(packed[row:row + n].reshape(shape))
        row += n + (-n) % SUBLANES
    return out


def kernel(x, c, w_ada, b_ada, g_mix_pre, g_mix_post, w_in, conv_w, conv_b, w_rgate, b_rgate, w_igate, b_igate, lru_a, v_norm_g, v_norm_b, w_spatial, b_spatial, g_lru_out, g_gmlp_out, w_out, g_ffn_pre, g_ffn_post, w_up, ffn_conv_w, ffn_conv_b, w_down, loss_target, m_w_ada, m_b_ada, m_g_mix_pre, m_g_mix_post, m_w_in, m_conv_w, m_conv_b, m_w_rgate, m_b_rgate, m_w_igate, m_b_igate, m_lru_a, m_v_norm_g, m_v_norm_b, m_w_spatial, m_b_spatial, m_g_lru_out, m_g_gmlp_out, m_w_out, m_g_ffn_pre, m_g_ffn_post, m_w_up, m_ffn_conv_w, m_ffn_conv_b, m_w_down, v_w_ada, v_b_ada, v_g_mix_pre, v_g_mix_post, v_w_in, v_conv_w, v_conv_b, v_w_rgate, v_b_rgate, v_w_igate, v_b_igate, v_lru_a, v_v_norm_g, v_v_norm_b, v_w_spatial, v_b_spatial, v_g_lru_out, v_g_gmlp_out, v_w_out, v_g_ffn_pre, v_g_ffn_post, v_w_up, v_ffn_conv_w, v_ffn_conv_b, v_w_down):
    args = dict(locals())
    names = ("w_ada", "b_ada", "g_mix_pre", "g_mix_post", "w_in", "conv_w", "conv_b", "w_rgate", "b_rgate",
             "w_igate", "b_igate", "lru_a", "v_norm_g", "v_norm_b", "w_spatial", "b_spatial", "g_lru_out",
             "g_gmlp_out", "w_out", "g_ffn_pre", "g_ffn_post", "w_up", "ffn_conv_w", "ffn_conv_b", "w_down")
    w = {n: args[n][0] for n in names}
    m = {n: args["m_" + n][0] for n in names}
    v = {n: args["v_" + n][0] for n in names}
    xi, yi, ci = _position()
    me_arr = jnp.reshape(4 * xi + 2 * yi + ci, (1,)).astype(jnp.int32)
    chip_arr = jnp.reshape(2 * xi + yi, (1,)).astype(jnp.int32)
    c_arr = jnp.reshape(ci, (1,)).astype(jnp.int32)
    pos_arr = jnp.stack([ci, 2 * xi + yi]).astype(jnp.int32)

    big = ("w_in", "w_out", "w_up", "w_down")
    lands = _cast_place([w[n] for n in big], chip_arr)
    start_a, wait_a = _gather_plan([w[n].shape[0] for n in big[:2]])
    start_b, wait_b = _gather_plan([w[n].shape[0] for n in big[2:]])

    row0 = jnp.concatenate([c, w["conv_w"].reshape(1, -1), w["ffn_conv_w"].reshape(1, -1)], axis=1)
    g0 = _allgather8(row0, "gather_cond", False)[:, 0, :]
    c8 = g0[:, :D_MODEL]
    per_chip = g0[0::2]
    conv_w_full = per_chip[:, D_MODEL:D_MODEL + 512].reshape(N_CHIPS, 4, 128).transpose(1, 0, 2).reshape(4, 512)
    ffn_conv_w_full = per_chip[:, D_MODEL + 512:].reshape(N_CHIPS, 3, 1536).transpose(1, 0, 2).reshape(3, 2 * D_FF)
    mod_parts = _allgather8(_ada_fwd(c8, w["w_ada"]), "gather_mod", False)
    send_a, recv_a, _, lands_a, token_a = _split_start([], lands[:2], start_a, 6, mod_parts, "gather_start_a")
    send_b, recv_b, _, lands_b, _ = _split_start([], lands[2:], start_b, 6, token_a, "gather_start_b")
    mod = _mod_select(mod_parts, w["b_ada"].reshape(1, -1), me_arr).reshape(N_MOD, D_MODEL)
    sh_m, sc_m, gt_m, sh_f, sc_f, gt_f = [mod[k:k + 1] for k in range(N_MOD)]

    small = {n: w[n] for n in SMALL_REPLICATED}
    small["conv_w"] = conv_w_full
    small["ffn_conv_w"] = ffn_conv_w_full
    row = lambda a: a.reshape(1, -1)
    seq_params, ws_t = _seq_params(small)
    glo, ggo = row(small["g_lru_out"]), row(small["g_gmlp_out"])
    g_pre, g_post = row(small["g_mix_pre"]), row(small["g_mix_post"])
    g_pre2, g_post2 = row(small["g_ffn_pre"]), row(small["g_ffn_post"])
    fw, fb = small["ffn_conv_w"], row(small["ffn_conv_b"])
    xs, tgt = x[0], loss_target[0]

    _, lands_a = _split_wait(send_a, recv_a, [], lands_a, wait_a, mod, "gather_wait_a")
    w_in4, w_out4 = _forward_to_sibling(lands_a, "forward_a")
    w_out_b = w_out4.reshape(D_MODEL, D_MODEL)
    z, h = _mix_in(xs, sc_m, sh_m, g_pre, w_in4)
    ycat, hst = _seqmix(z, seq_params, glo, ggo)
    y, x1, h2 = _mix_out(ycat, xs, w_out_b, gt_m, g_post, g_pre2, sc_f, sh_f)
    _, lands_b = _split_wait(send_b, recv_b, [], lands_b, wait_b, h2, "gather_wait_b")
    w_up4, w_down4 = _forward_to_sibling(lands_b, "forward_b")
    w_down_b = w_down4.reshape(D_FF, D_MODEL)
    up0, act, dy2, dx2, loss, dgt_f, dg_post2 = _ffn_fwd(h2, x1, tgt, w_up4, w_down_b, fw, fb, gt_f, g_post2)

    dup0, dfw, dfb = _ffn_bwd_a(dy2, up0, w_down_b, fw, fb)
    gw_up = _wgrad(h2, dup0, N_CHIPS, "wgrad_up", True)
    gw_down = _wgrad(act, dy2, 2, "wgrad_down", False)
    ex_start, ex_wait = _exchange_plan(2)

    def reduce_start(parts, tags, name):
        recv = _swap_halves(parts, "swap_halves_" + name)
        both = [_chip_sum(p, r, pos_arr, "chip_sum_" + t) for p, r, t in zip(parts, recv, tags)]
        sums, gath = [b[0] for b in both], [b[1] for b in both]
        return _split_start(sums, gath, ex_start, 3 * len(parts), pos_arr, "exchange_start_" + name)

    e_send_b, e_recv_b, sums_b, gath_b, token_b = reduce_start(
        [gw_up, gw_down.reshape(N_CHIPS, -1, D_MODEL)], ("w_up", "w_down"), "b")

    dx1, dy, dsh_f, dsc_f, dg_pre2, dgt_m, dg_post = _ffn_bwd_b(
        dup0, x1, y, dx2, w_up4, g_pre2, sc_f + token_b[0:1, 0:1], sh_f, gt_m, g_post)
    (dz, dcw, dcb, dwr, dwi, dbr, dbi, dspa, dng, dnb, dws, dbs_t, dglo, dggo) = _seqmix_bwd(
        z, hst, dy, w_out_b, seq_params, ws_t, glo, ggo)
    grad_x, dsh_m, dsc_m, dg_pre = _mix_in_bwd(xs, dz, dx1, w_in4, g_pre, sc_m)
    gw_in = _wgrad(h, dz, N_CHIPS, "wgrad_in", True)
    gw_out = _wgrad(ycat, dy, 1, "wgrad_out", False)
    e_send_a, e_recv_a, sums_a, gath_a, token_a = reduce_start(
        [gw_in, gw_out.reshape(N_CHIPS, -1, D_MODEL)], ("w_in", "w_out"), "a")

    grads, deltas, new_m, new_v = {}, {}, {}, {}

    def reduce_finish(send, recv, sums, gath, tags, after, name):
        sums, gath = _split_wait(send, recv, sums, gath, ex_wait, after, "exchange_wait_" + name)
        gath = _swap_gathered(gath, "swap_gathered_" + name)
        for g, t in zip(gath, tags):
            grads[t], deltas[t], new_m[t], new_v[t] = _adam_gathered(w[t], g, m[t], v[t], c_arr, "adam_" + t)

    reduce_finish(e_send_b, e_recv_b, sums_b, gath_b, ("w_up", "w_down"), token_a, "b")
    reduce_finish(e_send_a, e_recv_a, sums_a, gath_a, ("w_in", "w_out"), deltas["w_down"], "a")

    dmod = jnp.concatenate([dsh_m, dsc_m, dgt_m, dsh_f, dsc_f, dgt_f], axis=1)
    dmod8 = _allgather8(dmod, "gather_dmod", False)[:, 0, :]
    g_w_ada, g_b_ada = _ada_bwd(c8, dmod8, chip_arr)
    grads["w_ada"] = g_w_ada
    deltas["w_ada"], new_m["w_ada"], new_v["w_ada"] = _adam(w["w_ada"], g_w_ada, m["w_ada"], v["w_ada"], "adam_w_ada")

    small_grads = dict(
        g_mix_pre=dg_pre[0], g_mix_post=dg_post[0], conv_w=dcw, conv_b=dcb[0],
        w_rgate=_diag_blocks(dwr), b_rgate=dbr.reshape(LRU_HEADS, -1),
        w_igate=_diag_blocks(dwi), b_igate=dbi.reshape(LRU_HEADS, -1), lru_a=dspa[0],
        v_norm_g=dng[0], v_norm_b=dnb[0], w_spatial=dws, b_spatial=dbs_t.T,
        g_lru_out=dglo[0], g_gmlp_out=dggo[0], g_ffn_pre=dg_pre2[0], g_ffn_post=dg_post2[0],
        ffn_conv_w=dfw, ffn_conv_b=dfb[0])
    packed_names = SMALL_REPLICATED + SMALL_COLUMN_SHARDED
    g_small = _allreduce_small(_pack([small_grads[n] for n in packed_names]), "reduce_small")
    g_small_list = _unpack(g_small, [small_grads[n].shape for n in packed_names])
    g_rep = dict(zip(packed_names, g_small_list))

    rep = SMALL_REPLICATED
    d_p, m_p, v_p = _adam(_pack([w[n] for n in rep]), _pack([g_rep[n] for n in rep]),
                          _pack([m[n] for n in rep]), _pack([v[n] for n in rep]), "adam_small")
    shapes = [w[n].shape for n in rep]
    for n, dd, mm, vv in zip(rep, _unpack(d_p, shapes), _unpack(m_p, shapes), _unpack(v_p, shapes)):
        grads[n], deltas[n], new_m[n], new_v[n] = g_rep[n], dd, mm, vv
    for n in SMALL_COLUMN_SHARDED:
        grads[n], deltas[n], new_m[n], new_v[n] = _adam_cols(w[n], g_rep[n], m[n], v[n], chip_arr, "adam_" + n)
    b2 = lambda a: a.reshape(-1, PACK_COLS)
    d_b, m_b, v_b = _adam(b2(w["b_ada"]), b2(g_b_ada), b2(m["b_ada"]), b2(v["b_ada"]), "adam_b_ada")
    grads["b_ada"], deltas["b_ada"], new_m["b_ada"], new_v["b_ada"] = (
        g_b_ada.reshape(-1), d_b.reshape(-1), m_b.reshape(-1), v_b.reshape(-1))

    total = lax.psum(loss[0, 0], ("x", "y", "c"))
    outs = [total, grad_x[None]]
    for group in (grads, deltas, new_m, new_v):
        outs.extend(group[n][None] for n in names)
    return tuple(outs)
```

```python
import functools
import math

import jax
import jax.numpy as jnp
from jax import lax
from jax.experimental import pallas as pl
from jax.experimental.pallas import tpu as pltpu

F32 = jnp.float32
BF16 = jnp.bfloat16
MESH = pl.DeviceIdType.MESH

D_MODEL = 1024
LRU_WIDTH = 512
LRU_HEADS = 8
GMLP_WIDTH = 512
GMLP_GROUPS = 4
GMLP_BLOCK = 128
CHUNK = 64
D_FF = 3072
N_MOD = 6
EPS = 1e-6
LRU_C = 8.0
N_CHIPS = 4
N_DEV = 8

ADAM_LR = 0.001
ADAM_B1 = 0.9
ADAM_B2 = 0.999
ADAM_EPS = 1e-08
ADAM_WD = 0.01
ADAM_STEP = 10

GELU_C0 = math.sqrt(2.0 / math.pi)
GELU_C1 = 0.044715

VMEM_LIMIT_BYTES = 56 * 1024 * 1024
SUBLANES = 8
BF16_SUBLANES = 16
FFN_CHUNK = 768


def _gelu(x):
    t = jnp.tanh(GELU_C0 * (x + GELU_C1 * x * x * x))
    return 0.5 * x * (1.0 + t)


def _gelu_and_grad(x):
    x2 = x * x
    t = jnp.tanh(GELU_C0 * x * (1.0 + GELU_C1 * x2))
    g = 0.5 * x * (1.0 + t)
    dg = 0.5 * (1.0 + t) + 0.5 * x * (1.0 - t * t) * (GELU_C0 * (1.0 + 3.0 * GELU_C1 * x2))
    return g, dg


def _sigmoid(x):
    return 1.0 / (1.0 + jnp.exp(-x))


def _log1p(u):
    w = 1.0 + u
    return jnp.where(w == 1.0, u, jnp.log(w) * (u / (w - 1.0)))


def _softplus(x):
    return jnp.maximum(x, 0.0) + _log1p(jnp.exp(-jnp.abs(x)))


def _neg_expm1(x):
    u = jnp.exp(x)
    um1 = u - 1.0
    tiny = um1 == 0.0
    small = um1 * (x / jnp.log(jnp.where(tiny, 2.0, jnp.maximum(u, 0.25))))
    return -jnp.where(tiny, x, jnp.where(x < -1.0, um1, small))


def _msq_rsqrt(v):
    return lax.rsqrt(jnp.mean(v * v, axis=-1, keepdims=True) + EPS)


def _rms_bwd(dyn, yn, r):
    return r * (dyn - yn * jnp.mean(dyn * yn, axis=-1, keepdims=True))


def _colsum(v):
    return jnp.sum(v, axis=0, keepdims=True)


def _shift_down(cur, prev8, k):
    rolled = pltpu.roll(cur, k, 0)
    head = pltpu.roll(prev8, k, 0)
    row8 = lax.broadcasted_iota(jnp.int32, (SUBLANES, cur.shape[1]), 0)
    first = jnp.where(row8 < k, head, rolled[0:SUBLANES])
    return jnp.concatenate([first, rolled[SUBLANES:]], axis=0)


def _shift_up(cur, next8, k):
    t = cur.shape[0]
    rolled = pltpu.roll(cur, t - k, 0)
    tail = pltpu.roll(next8, SUBLANES - k, 0)
    row8 = lax.broadcasted_iota(jnp.int32, (SUBLANES, cur.shape[1]), 0)
    last = jnp.where(row8 >= SUBLANES - k, tail, rolled[t - SUBLANES:])
    return jnp.concatenate([rolled[:t - SUBLANES], last], axis=0)


def _scan_fwd(a, b):
    t = a.shape[0]
    row = lax.broadcasted_iota(jnp.int32, a.shape, 0)
    d = 1
    while d < t:
        keep = row >= d
        a_s = jnp.where(keep, pltpu.roll(a, d, 0), 1.0)
        b_s = jnp.where(keep, pltpu.roll(b, d, 0), 0.0)
        b = a * b_s + b
        a = a * a_s
        d *= 2
    return a, b


def _scan_bwd(a, g):
    t = a.shape[0]
    row = lax.broadcasted_iota(jnp.int32, a.shape, 0)
    d = 1
    while d < t:
        keep = row < t - d
        a_s = jnp.where(keep, pltpu.roll(a, t - d, 0), 1.0)
        g_s = jnp.where(keep, pltpu.roll(g, t - d, 0), 0.0)
        g = a * g_s + g
        a = a * a_s
        d *= 2
    return a, g


def _dot(a, b):
    return jnp.dot(a, b, preferred_element_type=F32)


def _dot_nt(a, b):
    return lax.dot_general(a, b, (((1,), (1,)), ((), ())), preferred_element_type=F32)


def _dot_tn(a, b):
    return lax.dot_general(a, b, (((0,), (0,)), ((), ())), preferred_element_type=F32)


def _rows(ts, cols, rev_of=None):
    if rev_of is None:
        return pl.BlockSpec((ts, cols), lambda i: (i, 0))
    return pl.BlockSpec((ts, cols), lambda i: (rev_of - 1 - i, 0))


def _halo_prev(ts, cols, halo, rev_of=None, col_block=0):
    per = ts // halo
    if rev_of is None:
        return pl.BlockSpec((halo, cols), lambda i: (jnp.maximum(i * per - 1, 0), col_block))
    return pl.BlockSpec((halo, cols), lambda i: (jnp.maximum((rev_of - 1 - i) * per - 1, 0), col_block))


def _full(shape):
    nd = len(shape)
    return pl.BlockSpec(shape, lambda *_: (0,) * nd)


_RESIDENT = pl.BlockSpec(memory_space=pltpu.VMEM)


def _params(sem):
    return pltpu.CompilerParams(dimension_semantics=sem, vmem_limit_bytes=VMEM_LIMIT_BYTES)


def _sds(shape, dtype):
    return jax.ShapeDtypeStruct(shape, dtype)


def _mix_in(x, sc, sh, g, w_in4, ts=256):
    s, d = x.shape

    def body(x_ref, sc_ref, sh_ref, g_ref, w_ref, z_ref, h_ref):
        xv = x_ref[...]
        h = (xv * _msq_rsqrt(xv) * g_ref[...]) * (1.0 + sc_ref[...]) + sh_ref[...]
        hb = h.astype(BF16)
        h_ref[...] = hb
        for k in range(N_CHIPS):
            z_ref[:, k * 512:(k + 1) * 512] = _dot(hb, w_ref[k])

    return pl.pallas_call(
        body, grid=(s // ts,), name="mix_in",
        in_specs=[_rows(ts, d), _full((1, d)), _full((1, d)), _full((1, d)), _full(w_in4.shape)],
        out_specs=[_rows(ts, 2048), _rows(ts, d)],
        out_shape=[_sds((s, 2048), F32), _sds((s, d), BF16)],
        compiler_params=_params(("parallel",)),
    )(x, sc, sh, g, w_in4)


def _seq_recompute(z_ref, zprev_ref, first_tile, p):
    (cw_ref, cb_ref, bdr_ref, bdi_ref, br_ref, bi_ref, la_ref, ng_ref, nb_ref, ws_ref, bst_ref) = p
    lx = z_ref[:, 0:512]
    lg = z_ref[:, 512:1024]
    gu = z_ref[:, 1024:1536]
    gv = z_ref[:, 1536:2048]
    prev8 = jnp.where(first_tile, 0.0, zprev_ref[...])
    s1 = _shift_down(lx, prev8, 1)
    s2 = _shift_down(lx, prev8, 2)
    s3 = _shift_down(lx, prev8, 3)
    xc = cw_ref[3:4, :] * lx + cw_ref[2:3, :] * s1 + cw_ref[1:2, :] * s2 + cw_ref[0:1, :] * s3 + cb_ref[...]
    xcb = xc.astype(BF16)
    r = _sigmoid(_dot(xcb, bdr_ref[...]) + br_ref[...])
    ig = _sigmoid(_dot(xcb, bdi_ref[...]) + bi_ref[...])
    spa = _softplus(-la_ref[...])
    log_a = (-LRU_C) * r * spa
    a = jnp.exp(log_a)
    mult = jnp.sqrt(_neg_expm1(2.0 * log_a))
    return dict(lx=lx, lg=lg, gu=gu, gv=gv, s1=s1, s2=s2, s3=s3, xc=xc, xcb=xcb, r=r, ig=ig, spa=spa,
                a=a, mult=mult)


def _gmlp_fwd(gu, gv, ng_ref, nb_ref, ws_ref, bst_ref, sp_scr):
    ts = gu.shape[0]
    u, du = _gelu_and_grad(gu)
    vg, dvg = _gelu_and_grad(gv)
    mu = jnp.mean(vg, axis=-1, keepdims=True)
    vc = vg - mu
    rstd = lax.rsqrt(jnp.mean(vc * vc, axis=-1, keepdims=True) + EPS)
    vhat = vc * rstd
    v = vhat * ng_ref[...] + nb_ref[...]
    vb = v.astype(BF16)
    for n in range(ts // GMLP_BLOCK):
        rs = slice(n * GMLP_BLOCK, (n + 1) * GMLP_BLOCK)
        for g in range(GMLP_GROUPS):
            cs = slice(g * 128, (g + 1) * 128)
            sp_scr[rs, cs] = _dot(ws_ref[g], vb[rs, cs]) + bst_ref[:, g:g + 1]
    spb = sp_scr[...]
    return dict(u=u, du=du, dvg=dvg, rstd=rstd, vhat=vhat, vb=vb, spb=spb, y_g=u * spb)


def _seq_specs(ts, nt, rev):
    rev_of = nt if rev else None
    return [
        _rows(ts, 2048, rev_of),
        _halo_prev(ts, 512, SUBLANES, rev_of),
    ]


def _seq_param_specs():
    return [_full((4, 512)), _full((1, 512)), _full((512, 512)), _full((512, 512)), _full((1, 512)),
            _full((1, 512)), _full((1, 512)), _full((1, 512)), _full((1, 512)), _full((4, 128, 128)),
            _full((128, 4))]


def _seqmix(z, seq_params, glo, ggo, ts=256):
    s = z.shape[0]
    nt = s // ts

    def body(z_ref, zprev_ref, *rest):
        p = rest[:11]
        glo_ref, ggo_ref, ycat_ref, hst_ref, hcarry, sp_scr = rest[11:]
        i = pl.program_id(0)

        @pl.when(i == 0)
        def _():
            hcarry[...] = jnp.zeros_like(hcarry)

        f = _seq_recompute(z_ref, zprev_ref, i == 0, p)
        bx = f["mult"] * (f["ig"] * f["xc"])
        acum, hloc = _scan_fwd(f["a"], bx)
        h = hloc + acum * hcarry[...]
        hcarry[...] = h[ts - 1:ts, :]
        hst_ref[...] = h
        y_l = h * _gelu(f["lg"])
        gm = _gmlp_fwd(f["gu"], f["gv"], p[7], p[8], p[9], p[10], sp_scr)
        y_g = gm["y_g"]
        ycat_ref[:, 0:512] = (y_l * _msq_rsqrt(y_l) * glo_ref[...]).astype(BF16)
        ycat_ref[:, 512:1024] = (y_g * _msq_rsqrt(y_g) * ggo_ref[...]).astype(BF16)

    return pl.pallas_call(
        body, grid=(nt,), name="seqmix",
        in_specs=_seq_specs(ts, nt, False) + _seq_param_specs() + [_full((1, 512)), _full((1, 512))],
        out_specs=[_rows(ts, 1024), _rows(ts, 512)],
        out_shape=[_sds((s, 1024), BF16), _sds((s, 512), F32)],
        scratch_shapes=[pltpu.VMEM((1, 512), F32), pltpu.VMEM((ts, 512), F32)],
        compiler_params=_params(("arbitrary",)),
    )(z, z, *seq_params, glo, ggo)


def _mix_out(ycat, x, w_out, gt_m, g_post, g_pre2, sc_f, sh_f, ts=256):
    s, d = x.shape

    def body(yc_ref, x_ref, w_ref, gt_ref, gp_ref, g2_ref, sc_ref, sh_ref, y_ref, x1_ref, h2_ref):
        y = _dot(yc_ref[...], w_ref[...])
        y_ref[...] = y
        x1 = x_ref[...] + gt_ref[...] * (y * _msq_rsqrt(y) * gp_ref[...])
        x1_ref[...] = x1
        h2 = (x1 * _msq_rsqrt(x1) * g2_ref[...]) * (1.0 + sc_ref[...]) + sh_ref[...]
        h2_ref[...] = h2.astype(BF16)

    vec = _full((1, d))
    return pl.pallas_call(
        body, grid=(s // ts,), name="mix_out",
        in_specs=[_rows(ts, d), _rows(ts, d), _full((d, d)), vec, vec, vec, vec, vec],
        out_specs=[_rows(ts, d), _rows(ts, d), _rows(ts, d)],
        out_shape=[_sds((s, d), F32), _sds((s, d), F32), _sds((s, d), BF16)],
        compiler_params=_params(("parallel",)),
    )(ycat, x, w_out, gt_m, g_post, g_pre2, sc_f, sh_f)


def _ffn_cols(j):
    per = (2 * D_FF // N_CHIPS) // FFN_CHUNK
    return j // per, (j % per) * FFN_CHUNK, j * FFN_CHUNK


def _ffn_fwd(h2, x1, tgt, w_up4, w_down, fw, fb, gt_f, g_post, ts=256):
    s, d = x1.shape
    nch = D_FF // FFN_CHUNK

    def body(h2_ref, x1_ref, tgt_ref, wup_ref, wdn_ref, fw_ref, fb_ref, gt_ref, gp_ref,
             up0_ref, pre_ref, act_ref, dy2_ref, dx2_ref, loss_ref, dgt_ref, dgp_ref, tail_ref):
        i = pl.program_id(0)

        @pl.when(i == 0)
        def _():
            tail_ref[...] = jnp.zeros_like(tail_ref)
            loss_ref[...] = jnp.zeros_like(loss_ref)
            dgt_ref[...] = jnp.zeros_like(dgt_ref)
            dgp_ref[...] = jnp.zeros_like(dgp_ref)

        hb = h2_ref[...]
        y2 = jnp.zeros((ts, d), F32)
        for j in range(nch):
            sh_g, off, col = _ffn_cols(j)
            halves = []
            for shard, c0 in ((sh_g, col), (sh_g + 2, D_FF + col)):
                cs = slice(c0, c0 + FFN_CHUNK)
                ub = _dot(hb, wup_ref[shard, :, off:off + FFN_CHUNK]).astype(BF16)
                up0_ref[:, cs] = ub
                u = ub.astype(F32)
                prev8 = tail_ref[:, cs]
                tail_ref[:, cs] = u[ts - SUBLANES:, :]
                halves.append(fw_ref[2:3, cs] * u + fw_ref[1:2, cs] * _shift_down(u, prev8, 1)
                              + fw_ref[0:1, cs] * _shift_down(u, prev8, 2) + fb_ref[:, cs])
                pre_ref[:, cs] = halves[-1].astype(BF16)
            act = (_gelu(halves[0]) * halves[1]).astype(BF16)
            act_ref[:, col:col + FFN_CHUNK] = act
            y2 = y2 + _dot(act, wdn_ref[col:col + FFN_CHUNK, :])
        r2 = _msq_rsqrt(y2)
        yn = y2 * r2
        yng = yn * gp_ref[...]
        e = x1_ref[...] + gt_ref[...] * yng - tgt_ref[...]
        loss_ref[...] += jnp.sum(e * e) * (0.5 / d)
        dx2 = e * (1.0 / d)
        dx2_ref[...] = dx2
        dgt_ref[...] += _colsum(dx2 * yng)
        dyng = dx2 * gt_ref[...]
        dgp_ref[...] += _colsum(dyng * yn)
        dy2_ref[...] = _rms_bwd(dyng * gp_ref[...], yn, r2).astype(BF16)

    vec = _full((1, d))
    return pl.pallas_call(
        body, grid=(s // ts,), name="ffn_fwd",
        in_specs=[_rows(ts, d), _rows(ts, d), _rows(ts, d), _RESIDENT, _RESIDENT,
                  _full((3, 2 * D_FF)), _full((1, 2 * D_FF)), vec, vec],
        out_specs=[_rows(ts, 2 * D_FF), _rows(ts, 2 * D_FF), _rows(ts, D_FF), _rows(ts, d), _rows(ts, d),
                   _full((1, 128)), vec, vec],
        out_shape=[_sds((s, 2 * D_FF), BF16), _sds((s, 2 * D_FF), BF16), _sds((s, D_FF), BF16), _sds((s, d), BF16),
                   _sds((s, d), F32), _sds((1, 128), F32), _sds((1, d), F32), _sds((1, d), F32)],
        scratch_shapes=[pltpu.VMEM((SUBLANES, 2 * D_FF), F32)],
        compiler_params=_params(("arbitrary",)),
    )(h2, x1, tgt, w_up4, w_down, fw, fb, gt_f, g_post)


def _shift_up_mxu(vb, up_mat, next8, k):
    t = vb.shape[0]
    main = _dot(up_mat, vb)
    tail = pltpu.roll(next8, SUBLANES - k, 0)
    row8 = lax.broadcasted_iota(jnp.int32, next8.shape, 0)
    last = main[t - SUBLANES:] + jnp.where(row8 >= SUBLANES - k, tail, 0.0)
    return jnp.concatenate([main[:t - SUBLANES], last], axis=0)


def _ffn_bwd_a(dy2, pre, up0, w_down, fw, ts=256):
    s, d = dy2.shape
    nt = s // ts
    nch = D_FF // FFN_CHUNK
    wide = 2 * D_FF
    up_mats = jnp.stack([jnp.eye(ts, k=1, dtype=BF16), jnp.eye(ts, k=2, dtype=BF16)])

    def body(dy2_ref, pre_ref, up0_ref, wdn_ref, fw_ref, um_ref, dup0_ref, dfw_ref, dfb_ref, next_ref):
        i = pl.program_id(0)

        @pl.when(i == 0)
        def _():
            next_ref[...] = jnp.zeros_like(next_ref)
            dfw_ref[...] = jnp.zeros_like(dfw_ref)
            dfb_ref[...] = jnp.zeros_like(dfb_ref)

        dyb = dy2_ref[...]
        for j in range(nch):
            _, _, col = _ffn_cols(j)
            dact = _dot_nt(dyb, wdn_ref[col:col + FFN_CHUNK, :])
            gl, dgl = _gelu_and_grad(pre_ref[:, col:col + FFN_CHUNK].astype(F32))
            dpre = (dact * pre_ref[:, D_FF + col:D_FF + col + FFN_CHUNK].astype(F32) * dgl, dact * gl)
            for half, c0 in enumerate((col, D_FF + col)):
                cs = slice(c0, c0 + FFN_CHUNK)
                dp = dpre[half]
                dpb = dp.astype(BF16)
                nxt = next_ref[:, cs]
                next_ref[:, cs] = dpb.astype(F32)[0:SUBLANES, :]
                su1 = _shift_up_mxu(dpb, um_ref[0], nxt, 1)
                su2 = _shift_up_mxu(dpb, um_ref[1], nxt, 2)
                u = up0_ref[:, cs].astype(F32)
                dfb_ref[:, cs] += _colsum(dp)
                dfw_ref[2:3, cs] += _colsum(dp * u)
                dfw_ref[1:2, cs] += _colsum(su1 * u)
                dfw_ref[0:1, cs] += _colsum(su2 * u)
                dup0 = fw_ref[2:3, cs] * dp + fw_ref[1:2, cs] * su1 + fw_ref[0:1, cs] * su2
                dup0_ref[:, cs] = dup0.astype(BF16)

    return pl.pallas_call(
        body, grid=(nt,), name="ffn_bwd_a",
        in_specs=[_rows(ts, d, nt), _rows(ts, wide, nt), _rows(ts, wide, nt), _RESIDENT,
                  _full((3, wide)), _full((2, ts, ts))],
        out_specs=[_rows(ts, wide, nt), _full((3, wide)), _full((1, wide))],
        out_shape=[_sds((s, wide), BF16), _sds((3, wide), F32), _sds((1, wide), F32)],
        scratch_shapes=[pltpu.VMEM((SUBLANES, wide), F32)],
        compiler_params=_params(("arbitrary",)),
    )(dy2, pre, up0, w_down, fw, up_mats)


def _ffn_bwd_b(dup0, x1, y, dx2, w_up4, g_pre2, sc_f, sh_f, gt_m, g_post_m, ts=256):
    s, d = x1.shape
    shard_cols = 2 * D_FF // N_CHIPS

    def body(dup_ref, x1_ref, y_ref, dx2_ref, wup_ref, g2_ref, sc_ref, sh_ref, gt_ref, gp_ref,
             dx1_ref, dy_ref, dsh_ref, dsc_ref, dg2_ref, dgt_ref, dgp_ref):
        i = pl.program_id(0)

        @pl.when(i == 0)
        def _():
            for ref in (dsh_ref, dsc_ref, dg2_ref, dgt_ref, dgp_ref):
                ref[...] = jnp.zeros_like(ref)

        dh2 = jnp.zeros((ts, d), F32)
        for k in range(N_CHIPS):
            dh2 = dh2 + _dot_nt(dup_ref[:, k * shard_cols:(k + 1) * shard_cols], wup_ref[k])
        x1v = x1_ref[...]
        r2 = _msq_rsqrt(x1v)
        xn = x1v * r2
        hn = xn * g2_ref[...]
        dsh_ref[...] += _colsum(dh2)
        dsc_ref[...] += _colsum(dh2 * hn)
        dhn = dh2 * (1.0 + sc_ref[...])
        dg2_ref[...] += _colsum(dhn * xn)
        dx1 = dx2_ref[...] + _rms_bwd(dhn * g2_ref[...], xn, r2)
        dx1_ref[...] = dx1
        yv = y_ref[...]
        ry = _msq_rsqrt(yv)
        yn = yv * ry
        dgt_ref[...] += _colsum(dx1 * (yn * gp_ref[...]))
        dyng = dx1 * gt_ref[...]
        dgp_ref[...] += _colsum(dyng * yn)
        dy_ref[...] = _rms_bwd(dyng * gp_ref[...], yn, ry).astype(BF16)

    vec = _full((1, d))
    return pl.pallas_call(
        body, grid=(s // ts,), name="ffn_bwd_b",
        in_specs=[_rows(ts, 2 * D_FF), _rows(ts, d), _rows(ts, d), _rows(ts, d), _RESIDENT,
                  vec, vec, vec, vec, vec],
        out_specs=[_rows(ts, d), _rows(ts, d), vec, vec, vec, vec, vec],
        out_shape=[_sds((s, d), F32), _sds((s, d), BF16)] + [_sds((1, d), F32)] * 5,
        compiler_params=_params(("arbitrary",)),
    )(dup0, x1, y, dx2, w_up4, g_pre2, sc_f, sh_f, gt_m, g_post_m)


def _seqmix_bwd(z, hst, dy, w_out, seq_params, ws_t, glo, ggo, ts=256):
    s = z.shape[0]
    nt = s // ts
    small_shapes = [(4, 512), (1, 512), (512, 512), (512, 512), (1, 512), (1, 512), (1, 512),
                    (1, 512), (1, 512), (4, 128, 128), (128, 4), (1, 512), (1, 512)]

    def body(z_ref, zprev_ref, hst_ref, hprev_ref, dy_ref, wout_ref, *rest):
        p = rest[:11]
        wst_ref, glo_ref, ggo_ref = rest[11:14]
        dz_ref = rest[14]
        (dcw_ref, dcb_ref, dwr_ref, dwi_ref, dbr_ref, dbi_ref, dspa_ref, dng_ref, dnb_ref, dws_ref, dbs_ref,
         dglo_ref, dggo_ref) = rest[15:28]
        gcarry, anext, dxcnext, sp_scr, dv_scr = rest[28:]
        i = pl.program_id(0)

        @pl.when(i == 0)
        def _():
            for ref in rest[15:28]:
                ref[...] = jnp.zeros_like(ref)
            gcarry[...] = jnp.zeros_like(gcarry)
            anext[...] = jnp.ones_like(anext)
            dxcnext[...] = jnp.zeros_like(dxcnext)

        first_tile = i == nt - 1
        f = _seq_recompute(z_ref, zprev_ref, first_tile, p)
        xc, r, ig, a, mult, lx = f["xc"], f["r"], f["ig"], f["a"], f["mult"], f["lx"]
        h = hst_ref[...]
        hprev = _shift_down(h, jnp.where(first_tile, 0.0, hprev_ref[...]), 1)
        gl, dgl = _gelu_and_grad(f["lg"])
        y_l = h * gl
        gm = _gmlp_fwd(f["gu"], f["gv"], p[7], p[8], p[9], p[10], sp_scr)
        y_g = gm["y_g"]

        dycat = _dot_nt(dy_ref[...], wout_ref[...])
        rl = _msq_rsqrt(y_l)
        yln = y_l * rl
        dyl = dycat[:, 0:512]
        dglo_ref[...] += _colsum(dyl * yln)
        dy_l = _rms_bwd(dyl * glo_ref[...], yln, rl)
        rg = _msq_rsqrt(y_g)
        ygn = y_g * rg
        dyg = dycat[:, 512:1024]
        dggo_ref[...] += _colsum(dyg * ygn)
        dy_g = _rms_bwd(dyg * ggo_ref[...], ygn, rg)

        dz_ref[:, 512:1024] = (dy_l * h * dgl).astype(BF16)
        a_up = _shift_up(a, anext[...], 1)
        acum, gloc = _scan_bwd(a_up, dy_l * gl)
        gg = gloc + acum * gcarry[...]
        gcarry[...] = gg[0:1, :]
        anext[...] = a[0:SUBLANES, :]
        da = gg * hprev
        t1 = gg * mult
        di = t1 * xc
        dxc = t1 * ig
        dmult = gg * ig * xc
        dla = da * a - dmult * (a * a / mult)
        spa = f["spa"]
        dspa_ref[...] += _colsum(dla * r) * (-LRU_C)
        dpr = dla * ((-LRU_C) * spa) * r * (1.0 - r)
        dpi = di * ig * (1.0 - ig)
        dbr_ref[...] += _colsum(dpr)
        dbi_ref[...] += _colsum(dpi)
        dprb = dpr.astype(BF16)
        dpib = dpi.astype(BF16)
        dwr_ref[...] += _dot_tn(f["xcb"], dprb)
        dwi_ref[...] += _dot_tn(f["xcb"], dpib)
        dxc = dxc + _dot_nt(dprb, p[2][...]) + _dot_nt(dpib, p[3][...])
        dcb_ref[...] += _colsum(dxc)
        dcw_ref[3:4, :] += _colsum(dxc * lx)
        dcw_ref[2:3, :] += _colsum(dxc * f["s1"])
        dcw_ref[1:2, :] += _colsum(dxc * f["s2"])
        dcw_ref[0:1, :] += _colsum(dxc * f["s3"])
        nxt = dxcnext[...]
        dxcnext[...] = dxc[0:SUBLANES, :]
        cw_ref = p[0]
        dlx = (cw_ref[3:4, :] * dxc + cw_ref[2:3, :] * _shift_up(dxc, nxt, 1)
               + cw_ref[1:2, :] * _shift_up(dxc, nxt, 2) + cw_ref[0:1, :] * _shift_up(dxc, nxt, 3))
        dz_ref[:, 0:512] = dlx.astype(BF16)

        dz_ref[:, 1024:1536] = (dy_g * gm["spb"] * gm["du"]).astype(BF16)
        dsp = dy_g * gm["u"]
        vb = gm["vb"]
        for n in range(ts // GMLP_BLOCK):
            rs = slice(n * GMLP_BLOCK, (n + 1) * GMLP_BLOCK)
            for g in range(GMLP_GROUPS):
                cs = slice(g * 128, (g + 1) * 128)
                dbs_ref[:, g:g + 1] += jnp.sum(dsp[rs, cs], axis=1, keepdims=True)
                blk = dsp[rs, cs].astype(BF16)
                dws_ref[g] += _dot_nt(blk, vb[rs, cs])
                dv_scr[rs, cs] = _dot(wst_ref[g], blk)
        dv = dv_scr[...]
        vhat = gm["vhat"]
        dng_ref[...] += _colsum(dv * vhat)
        dnb_ref[...] += _colsum(dv)
        dvh = dv * p[7][...]
        dvg = gm["rstd"] * (dvh - jnp.mean(dvh, axis=-1, keepdims=True)
                            - vhat * jnp.mean(dvh * vhat, axis=-1, keepdims=True))
        dz_ref[:, 1536:2048] = (dvg * gm["dvg"]).astype(BF16)

        @pl.when(i == nt - 1)
        def _():
            pos = lax.broadcasted_iota(jnp.int32, (GMLP_BLOCK, GMLP_BLOCK), 0) // CHUNK
            src = lax.broadcasted_iota(jnp.int32, (GMLP_BLOCK, GMLP_BLOCK), 1) // CHUNK
            for g in range(GMLP_GROUPS):
                dws_ref[g] = jnp.where(src <= pos, dws_ref[g], 0.0)
            dspa_ref[...] = dspa_ref[...] * (-_sigmoid(-p[6][...]))

    in_specs = (_seq_specs(ts, nt, True)
                + [_rows(ts, 512, nt), _halo_prev(ts, 512, SUBLANES, nt), _rows(ts, 1024, nt), _full((1024, 1024))]
                + _seq_param_specs() + [_full((4, 128, 128)), _full((1, 512)), _full((1, 512))])
    return pl.pallas_call(
        body, grid=(nt,), name="seqmix_bwd",
        in_specs=in_specs,
        out_specs=[_rows(ts, 2048, nt)] + [_full(sh) for sh in small_shapes],
        out_shape=[_sds((s, 2048), BF16)] + [_sds(sh, F32) for sh in small_shapes],
        scratch_shapes=[pltpu.VMEM((1, 512), F32), pltpu.VMEM((SUBLANES, 512), F32),
                        pltpu.VMEM((SUBLANES, 512), F32), pltpu.VMEM((ts, 512), F32), pltpu.VMEM((ts, 512), F32)],
        compiler_params=_params(("arbitrary",)),
    )(z, z, hst, hst, dy, w_out, *seq_params, ws_t, glo, ggo)


def _mix_in_bwd(x, dz, dx1, w_in4, g, sc, ts=256):
    s, d = x.shape

    def body(x_ref, dz_ref, dx1_ref, w_ref, g_ref, sc_ref, gx_ref, dsh_ref, dsc_ref, dg_ref):
        i = pl.program_id(0)

        @pl.when(i == 0)
        def _():
            for ref in (dsh_ref, dsc_ref, dg_ref):
                ref[...] = jnp.zeros_like(ref)

        dh = jnp.zeros((ts, d), F32)
        for k in range(N_CHIPS):
            dh = dh + _dot_nt(dz_ref[:, k * 512:(k + 1) * 512], w_ref[k])
        xv = x_ref[...]
        r = _msq_rsqrt(xv)
        xn = xv * r
        dsh_ref[...] += _colsum(dh)
        dsc_ref[...] += _colsum(dh * (xn * g_ref[...]))
        dhn = dh * (1.0 + sc_ref[...])
        dg_ref[...] += _colsum(dhn * xn)
        gx_ref[...] = dx1_ref[...] + _rms_bwd(dhn * g_ref[...], xn, r)

    vec = _full((1, d))
    return pl.pallas_call(
        body, grid=(s // ts,), name="mix_in_bwd",
        in_specs=[_rows(ts, d), _rows(ts, 2048), _rows(ts, d), _full(w_in4.shape), vec, vec],
        out_specs=[_rows(ts, d), vec, vec, vec],
        out_shape=[_sds((s, d), F32)] + [_sds((1, d), F32)] * 3,
        compiler_params=_params(("arbitrary",)),
    )(x, dz, dx1, w_in4, g, sc)


def _wgrad(a, b, n_chunks, name, chunk_major, ts=512):
    s, m = a.shape
    n = b.shape[1]
    nc = n // n_chunks
    nt = s // ts

    def body(a_ref, b_ref, o_ref, acc):
        i = pl.program_id(1)

        @pl.when(i == 0)
        def _():
            acc[...] = jnp.zeros_like(acc)

        acc[...] += _dot_tn(a_ref[...], b_ref[...])

        @pl.when(i == nt - 1)
        def _():
            if chunk_major:
                o_ref[0] = acc[...].astype(BF16)
            else:
                o_ref[...] = acc[...].astype(BF16)

    if chunk_major:
        out_spec, out_shape = pl.BlockSpec((1, m, nc), lambda c, i: (c, 0, 0)), _sds((n_chunks, m, nc), BF16)
    else:
        out_spec, out_shape = pl.BlockSpec((m, nc), lambda c, i: (0, c)), _sds((m, n), BF16)
    return pl.pallas_call(
        body, grid=(n_chunks, nt), name=name,
        in_specs=[pl.BlockSpec((ts, m), lambda c, i: (i, 0)), pl.BlockSpec((ts, nc), lambda c, i: (i, c))],
        out_specs=out_spec,
        out_shape=out_shape,
        scratch_shapes=[pltpu.VMEM((m, nc), F32)],
        compiler_params=_params(("parallel", "arbitrary")),
    )(a, b)


def _block_diag(w):
    heads, hd, _ = w.shape
    eye = jnp.eye(heads, dtype=w.dtype)
    return (eye[:, None, :, None] * w[:, :, None, :]).reshape(heads * hd, heads * hd)


def _diag_blocks(m):
    hd = LRU_WIDTH // LRU_HEADS
    m4 = m.reshape(LRU_HEADS, hd, LRU_HEADS, hd)
    return jnp.stack([m4[k, :, k, :] for k in range(LRU_HEADS)])


def _seq_params(small):
    row = lambda v: v.reshape(1, -1)
    pos = jnp.arange(GMLP_BLOCK)
    mask = (pos[None, :] // CHUNK) <= (pos[:, None] // CHUNK)
    ws = jnp.where(mask[None], small["w_spatial"], 0.0)
    seq_params = (small["conv_w"], row(small["conv_b"]),
                  _block_diag(small["w_rgate"]).astype(BF16), _block_diag(small["w_igate"]).astype(BF16),
                  row(small["b_rgate"]), row(small["b_igate"]), row(small["lru_a"]),
                  row(small["v_norm_g"]), row(small["v_norm_b"]), ws.astype(BF16), small["b_spatial"].T)
    return seq_params, jnp.swapaxes(ws, 1, 2).astype(BF16)


_ANY = pl.BlockSpec(memory_space=pl.ANY)
_CHIP_FLIPS = ((1, 0), (0, 1), (1, 1))


def _position():
    return lax.axis_index("x"), lax.axis_index("y"), lax.axis_index("c")


def _flip(v, f):
    return 1 - v if f else v


def _remote(src, dst, send_sem, recv_sem, peer):
    return pltpu.make_async_remote_copy(src_ref=src, dst_ref=dst, send_sem=send_sem, recv_sem=recv_sem,
                                        device_id=peer, device_id_type=MESH)


def _allgather8(block, name, reduce):
    r, n = block.shape

    def body(x_ref, out_ref, *scratch):
        if reduce:
            gath, send_sems, recv_sems, loc_sem = scratch
        else:
            gath = out_ref
            send_sems, recv_sems, loc_sem = scratch
        x, y, c = _position()
        me = 4 * x + 2 * y + c
        loc = pltpu.make_async_copy(x_ref, gath.at[me], loc_sem)
        loc.start()
        peers = []
        for k in range(1, N_DEV):
            px, py, pc = _flip(x, k & 4), _flip(y, k & 2), _flip(c, k & 1)
            peers.append((px, py, pc))
            _remote(x_ref, gath.at[me], send_sems.at[k - 1], recv_sems.at[k - 1], (px, py, pc)).start()
        for k, (px, py, pc) in enumerate(peers):
            src = 4 * px + 2 * py + pc
            _remote(x_ref, gath.at[src], send_sems.at[k], recv_sems.at[k], (px, py, pc)).wait_recv()
        for k, peer in enumerate(peers):
            _remote(x_ref, gath.at[me], send_sems.at[k], recv_sems.at[k], peer).wait_send()
        loc.wait()
        if reduce:
            acc = gath[0]
            for k in range(1, N_DEV):
                acc = acc + gath[k]
            out_ref[...] = acc

    sems = [pltpu.SemaphoreType.DMA((N_DEV - 1,)), pltpu.SemaphoreType.DMA((N_DEV - 1,)), pltpu.SemaphoreType.DMA]
    if reduce:
        out_shape = _sds((r, n), F32)
        scratch = [pltpu.VMEM((N_DEV, r, n), F32)] + sems
    else:
        out_shape = _sds((N_DEV, r, n), F32)
        scratch = sems
    return pl.pallas_call(
        body, name=name, out_shape=out_shape,
        in_specs=[pl.BlockSpec(memory_space=pltpu.VMEM)], out_specs=pl.BlockSpec(memory_space=pltpu.VMEM),
        scratch_shapes=scratch,
        compiler_params=pltpu.CompilerParams(vmem_limit_bytes=VMEM_LIMIT_BYTES),
    )(block)


def _half(ref, c, rows):
    hr = rows // 2
    return ref.at[pl.ds(pl.multiple_of(c * hr, BF16_SUBLANES), hr), :]


def _gather_weights(shards):
    na = len(shards)

    def body(*refs):
        ins, outs = refs[:na], refs[na:2 * na]
        ici_send, ici_recv, d2d_send, d2d_recv, loc_sem = refs[2 * na:]
        x, y, c = _position()
        chip = 2 * x + y
        sibling = (x, y, 1 - c)
        local = []
        for a in range(na):
            local.append(pltpu.make_async_copy(ins[a], outs[a].at[chip], loc_sem.at[a]))
            local[-1].start()
        sends = []
        for a in range(na):
            rows = shards[a].shape[0]
            for j, (fx, fy) in enumerate(_CHIP_FLIPS):
                peer = (_flip(x, fx), _flip(y, fy), c)
                sends.append(_remote(_half(ins[a], c, rows), _half(outs[a].at[chip], c, rows),
                                     ici_send.at[a * 3 + j], ici_recv.at[a * 3 + j], peer))
                sends[-1].start()
        for a in range(na):
            rows = shards[a].shape[0]
            for j, (fx, fy) in enumerate(_CHIP_FLIPS):
                src_chip = 2 * _flip(x, fx) + _flip(y, fy)
                landed = _half(outs[a].at[src_chip], c, rows)
                _remote(landed, landed, ici_send.at[a * 3 + j], ici_recv.at[a * 3 + j], sibling).wait_recv()
                sends.append(_remote(landed, landed, d2d_send.at[a * 3 + j], d2d_recv.at[a * 3 + j], sibling))
                sends[-1].start()
        for a in range(na):
            rows = shards[a].shape[0]
            for j, (fx, fy) in enumerate(_CHIP_FLIPS):
                src_chip = 2 * _flip(x, fx) + _flip(y, fy)
                other = _half(outs[a].at[src_chip], 1 - c, rows)
                _remote(other, other, d2d_send.at[a * 3 + j], d2d_recv.at[a * 3 + j], sibling).wait_recv()
        for cp in sends:
            cp.wait_send()
        for cp in local:
            cp.wait()

    return pl.pallas_call(
        body, name="gather_weights",
        out_shape=[_sds((N_CHIPS,) + w.shape, w.dtype) for w in shards],
        in_specs=[_ANY] * na, out_specs=[_ANY] * na,
        scratch_shapes=[pltpu.SemaphoreType.DMA((3 * na,))] * 4 + [pltpu.SemaphoreType.DMA((na,))],
    )(*shards)


def _swap_halves(parts, name):
    na = len(parts)

    def body(*refs):
        ins, outs = refs[:na], refs[na:2 * na]
        send_sems, recv_sems = refs[2 * na:]
        x, y, c = _position()
        sibling = (x, y, 1 - c)
        cps = []
        for a in range(na):
            hr = parts[a].shape[1] // 2
            src = ins[a].at[:, pl.ds(pl.multiple_of((1 - c) * hr, BF16_SUBLANES), hr), :]
            cps.append(_remote(src, outs[a], send_sems.at[a], recv_sems.at[a], sibling))
            cps[-1].start()
        for cp in cps:
            cp.wait()

    return pl.pallas_call(
        body, name=name,
        out_shape=[_sds((N_CHIPS, p.shape[1] // 2, p.shape[2]), p.dtype) for p in parts],
        in_specs=[_ANY] * na, out_specs=[_ANY] * na,
        scratch_shapes=[pltpu.SemaphoreType.DMA((na,))] * 2,
    )(*parts)


def _chip_sum(part, recv, pos_arr, name):
    _, rows, cols = part.shape
    hr = rows // 2

    def body(pos_ref, p_ref, r_ref, o_ref, g_ref):
        total = (p_ref[...].astype(F32) + r_ref[...].astype(F32)).astype(BF16)
        o_ref[...] = total

        @pl.when(pl.program_id(0) == pos_ref[1])
        def _():
            g_ref[0] = total

    grid_spec = pltpu.PrefetchScalarGridSpec(
        num_scalar_prefetch=1, grid=(N_CHIPS,),
        in_specs=[pl.BlockSpec((1, hr, cols), lambda k, pos: (k, pos[0], 0)),
                  pl.BlockSpec((1, hr, cols), lambda k, pos: (k, 0, 0))],
        out_specs=[pl.BlockSpec((1, hr, cols), lambda k, pos: (k, 0, 0)),
                   pl.BlockSpec((1, 1, hr, cols), lambda k, pos: (0, pos[1], 0, 0))])
    return pl.pallas_call(
        body, name=name, grid_spec=grid_spec,
        out_shape=[_sds((N_CHIPS, hr, cols), BF16), _sds((2, N_CHIPS, hr, cols), BF16)],
        compiler_params=_params(("arbitrary",)),
    )(pos_arr, part, recv)


def _exchange_chips(sums):
    na = len(sums)

    def body(*refs):
        ins, outs = refs[:na], refs[na:2 * na]
        send_sems, recv_sems, loc_sem = refs[2 * na:]
        x, y, c = _position()
        chip = 2 * x + y
        local = []
        for a in range(na):
            local.append(pltpu.make_async_copy(ins[a].at[chip], outs[a].at[chip], loc_sem.at[a]))
            local[-1].start()
        cps = []
        for a in range(na):
            for j, (fx, fy) in enumerate(_CHIP_FLIPS):
                px, py = _flip(x, fx), _flip(y, fy)
                cps.append(_remote(ins[a].at[2 * px + py], outs[a].at[chip],
                                   send_sems.at[a * 3 + j], recv_sems.at[a * 3 + j], (px, py, c)))
                cps[-1].start()
        for a in range(na):
            for j, (fx, fy) in enumerate(_CHIP_FLIPS):
                src_chip = 2 * _flip(x, fx) + _flip(y, fy)
                landed = outs[a].at[src_chip]
                _remote(landed, landed, send_sems.at[a * 3 + j], recv_sems.at[a * 3 + j], (x, y, c)).wait_recv()
        for cp in cps:
            cp.wait_send()
        for cp in local:
            cp.wait()

    return pl.pallas_call(
        body, name="exchange_chips",
        out_shape=[_sds(s.shape, s.dtype) for s in sums],
        in_specs=[_ANY] * na, out_specs=[_ANY] * na,
        scratch_shapes=[pltpu.SemaphoreType.DMA((3 * na,))] * 2 + [pltpu.SemaphoreType.DMA((na,))],
    )(*sums)


def _sum_chips(gath, name, tr=128):
    _, hr, cols = gath.shape
    tr = min(tr, hr)

    def body(g_ref, o_ref):
        acc = g_ref[0].astype(F32)
        for k in range(1, N_CHIPS):
            acc = acc + g_ref[k].astype(F32)
        o_ref[...] = acc

    return pl.pallas_call(
        body, name=name, grid=(hr // tr,),
        in_specs=[pl.BlockSpec((N_CHIPS, tr, cols), lambda i: (0, i, 0))],
        out_specs=pl.BlockSpec((tr, cols), lambda i: (i, 0)),
        out_shape=_sds((hr, cols), F32),
        compiler_params=_params(("parallel",)),
    )(gath)


def _join_halves(halves):
    na = len(halves)

    def body(*refs):
        ins, outs = refs[:na], refs[na:2 * na]
        send_sems, recv_sems, loc_sem = refs[2 * na:]
        x, y, c = _position()
        sibling = (x, y, 1 - c)
        cps, local = [], []
        for a in range(na):
            rows = 2 * halves[a].shape[0]
            mine = _half(outs[a], c, rows)
            local.append(pltpu.make_async_copy(ins[a], mine, loc_sem.at[a]))
            local[-1].start()
            cps.append(_remote(ins[a], mine, send_sems.at[a], recv_sems.at[a], sibling))
            cps[-1].start()
        for a in range(na):
            rows = 2 * halves[a].shape[0]
            other = _half(outs[a], 1 - c, rows)
            _remote(ins[a], other, send_sems.at[a], recv_sems.at[a], sibling).wait_recv()
        for cp in cps:
            cp.wait_send()
        for cp in local:
            cp.wait()

    return pl.pallas_call(
        body, name="join_halves",
        out_shape=[_sds((2 * h.shape[0], h.shape[1]), h.dtype) for h in halves],
        in_specs=[_ANY] * na, out_specs=[_ANY] * na,
        scratch_shapes=[pltpu.SemaphoreType.DMA((na,))] * 3,
    )(*halves)


_HBM = pl.BlockSpec(memory_space=pltpu.HBM)
_SEM = pl.BlockSpec(memory_space=pltpu.SEMAPHORE)
_EFFECT = pltpu.SideEffectType.DATAFLOW_SIDE_EFFECTING


def _in_hbm(a):
    return pltpu.with_memory_space_constraint(a, pltpu.HBM)


def _split_start(srcs, lands, plan, n_copies, after, name):
    ns, nl = len(srcs), len(lands)
    bufs = list(srcs) + list(lands)

    def body(*refs):
        send_sems, recv_sems = refs[ns + nl + 1], refs[ns + nl + 2]
        token = refs[-1]
        for k, (src, dst, peer) in enumerate(plan(refs[:ns], refs[ns:ns + nl])):
            _remote(src, dst, send_sems.at[k], recv_sems.at[k], peer).start()
        token[...] = jnp.zeros_like(token)

    out = pl.pallas_call(
        body, name=name,
        out_shape=(pltpu.SemaphoreType.DMA((n_copies,)), pltpu.SemaphoreType.DMA((n_copies,)),
                   *[pltpu.HBM(b.shape, b.dtype) for b in bufs], _sds((SUBLANES, 128), F32)),
        in_specs=[_HBM] * (ns + nl) + [_ANY],
        out_specs=(_SEM, _SEM, *[_HBM] * (ns + nl), pl.BlockSpec(memory_space=pltpu.VMEM)),
        input_output_aliases={i: 2 + i for i in range(ns + nl)},
        compiler_params=pltpu.CompilerParams(has_side_effects=_EFFECT),
    )(*[_in_hbm(b) for b in bufs], after)
    return out[0], out[1], list(out[2:2 + ns]), list(out[2 + ns:2 + ns + nl]), out[-1]


def _split_wait(send_sems, recv_sems, srcs, lands, plan, after, name):
    ns, nl = len(srcs), len(lands)
    bufs = list(srcs) + list(lands)

    def body(*refs):
        send_ref, recv_ref = refs[ns + nl], refs[ns + nl + 1]
        me = _position()
        for k, src, dst in plan(refs[:ns], refs[ns:ns + nl]):
            cp = _remote(src, dst, send_ref.at[k], recv_ref.at[k], me)
            cp.wait_send()
            cp.wait_recv()

    out = pl.pallas_call(
        body, name=name,
        out_shape=[pltpu.HBM(b.shape, b.dtype) for b in bufs],
        in_specs=[_HBM] * (ns + nl) + [_SEM, _SEM, _ANY],
        out_specs=[_HBM] * (ns + nl),
        input_output_aliases={i: i for i in range(ns + nl)},
        compiler_params=pltpu.CompilerParams(has_side_effects=_EFFECT),
    )(*bufs, send_sems, recv_sems, after)
    return list(out[:ns]), list(out[ns:])


def _gather_plan(rows_of):
    def start(src_refs, land_refs):
        x, y, c = _position()
        chip = 2 * x + y
        out = []
        for a, rows in enumerate(rows_of):
            mine = _half(land_refs[a].at[chip], c, rows)
            out.extend((mine, mine, (_flip(x, fx), _flip(y, fy), c)) for fx, fy in _CHIP_FLIPS)
        return out

    def wait(src_refs, land_refs):
        x, y, c = _position()
        chip = 2 * x + y
        out = []
        for a, rows in enumerate(rows_of):
            for j, (fx, fy) in enumerate(_CHIP_FLIPS):
                src_chip = 2 * _flip(x, fx) + _flip(y, fy)
                out.append((3 * a + j, _half(land_refs[a].at[chip], c, rows),
                            _half(land_refs[a].at[src_chip], c, rows)))
        return out

    return start, wait


def _exchange_plan(n_arrays):
    def start(src_refs, land_refs):
        x, y, c = _position()
        chip = 2 * x + y
        out = []
        for a in range(n_arrays):
            for fx, fy in _CHIP_FLIPS:
                px, py = _flip(x, fx), _flip(y, fy)
                out.append((src_refs[a].at[2 * px + py], land_refs[a].at[0, chip], (px, py, c)))
        return out

    def wait(src_refs, land_refs):
        x, y, c = _position()
        out = []
        for a in range(n_arrays):
            for j, (fx, fy) in enumerate(_CHIP_FLIPS):
                src_chip = 2 * _flip(x, fx) + _flip(y, fy)
                out.append((3 * a + j, src_refs[a].at[src_chip], land_refs[a].at[0, src_chip]))
        return out

    return start, wait


def _forward_to_sibling(lands, name):
    na = len(lands)

    def body(*refs):
        land_refs = refs[na:2 * na]
        send_sems, recv_sems = refs[2 * na:]
        x, y, c = _position()
        sibling = (x, y, 1 - c)
        sends = []
        for a in range(na):
            rows = lands[a].shape[1]
            for j, (fx, fy) in enumerate(_CHIP_FLIPS):
                landed = _half(land_refs[a].at[2 * _flip(x, fx) + _flip(y, fy)], c, rows)
                sends.append(_remote(landed, landed, send_sems.at[3 * a + j], recv_sems.at[3 * a + j], sibling))
                sends[-1].start()
        for a in range(na):
            rows = lands[a].shape[1]
            for j, (fx, fy) in enumerate(_CHIP_FLIPS):
                other = _half(land_refs[a].at[2 * _flip(x, fx) + _flip(y, fy)], 1 - c, rows)
                _remote(other, other, send_sems.at[3 * a + j], recv_sems.at[3 * a + j], sibling).wait_recv()
        for cp in sends:
            cp.wait_send()

    return pl.pallas_call(
        body, name=name,
        out_shape=[_sds(l.shape, l.dtype) for l in lands],
        in_specs=[_ANY] * na, out_specs=[_ANY] * na,
        input_output_aliases={a: a for a in range(na)},
        scratch_shapes=[pltpu.SemaphoreType.DMA((3 * na,))] * 2,
    )(*lands)


def _swap_gathered(gath, name):
    na = len(gath)

    def body(*refs):
        gath_refs = refs[na:2 * na]
        send_sems, recv_sems = refs[2 * na:]
        x, y, c = _position()
        cps = [_remote(gath_refs[a].at[0], gath_refs[a].at[1], send_sems.at[a], recv_sems.at[a], (x, y, 1 - c))
               for a in range(na)]
        for cp in cps:
            cp.start()
        for cp in cps:
            cp.wait()

    return pl.pallas_call(
        body, name=name,
        out_shape=[_sds(g.shape, g.dtype) for g in gath],
        in_specs=[_ANY] * na, out_specs=[_ANY] * na,
        input_output_aliases={a: a for a in range(na)},
        scratch_shapes=[pltpu.SemaphoreType.DMA((na,))] * 2,
    )(*gath)


def _adam_gathered(w, gath, m, v, c_arr, name, tr=128):
    rows, cols = w.shape
    hr = rows // 2
    per = hr // tr

    def body(c_ref, w_ref, g_ref, m_ref, v_ref, go_ref, d_ref, nm_ref, nv_ref):
        g = g_ref[0, 0].astype(F32)
        for k in range(1, N_CHIPS):
            g = g + g_ref[0, k].astype(F32)
        go_ref[...] = g
        d_ref[...], nm_ref[...], nv_ref[...] = _adam_math(w_ref[...], g, m_ref[...], v_ref[...])

    def rows_of(h, i, c_ref):
        c = c_ref[0]
        return ((c + h - 2 * c * h) * per + i, 0)

    blk = pl.BlockSpec((tr, cols), rows_of)
    grid_spec = pltpu.PrefetchScalarGridSpec(
        num_scalar_prefetch=1, grid=(2, per),
        in_specs=[blk, pl.BlockSpec((1, N_CHIPS, tr, cols), lambda h, i, c_ref: (h, 0, i, 0)), blk, blk],
        out_specs=[blk] * 4)
    return pl.pallas_call(
        body, name=name, grid_spec=grid_spec, out_shape=[_sds(w.shape, F32)] * 4,
        compiler_params=_params(("arbitrary", "arbitrary")),
    )(c_arr, w, gath, m, v)


def _allreduce_small(block, name):
    r, n = block.shape
    hr = r // 2

    def body(x_ref, out_ref, sib, chipsum, gath, d2d_send, d2d_recv, ici_send, ici_recv):
        x, y, c = _position()
        chip = 2 * x + y
        sibling = (x, y, 1 - c)
        first = _remote(x_ref, sib, d2d_send.at[0], d2d_recv.at[0], sibling)
        first.start()
        first.wait()
        chipsum[...] = x_ref[...] + sib[...]
        mine = pl.ds(pl.multiple_of(c * hr, SUBLANES), hr)
        theirs = pl.ds(pl.multiple_of((1 - c) * hr, SUBLANES), hr)
        sends = []
        for j, (fx, fy) in enumerate(_CHIP_FLIPS):
            sends.append(_remote(chipsum.at[mine, :], gath.at[chip], ici_send.at[j], ici_recv.at[j],
                                 (_flip(x, fx), _flip(y, fy), c)))
            sends[-1].start()
        gath[chip] = chipsum[mine, :]
        for j, (fx, fy) in enumerate(_CHIP_FLIPS):
            landed = gath.at[2 * _flip(x, fx) + _flip(y, fy)]
            _remote(landed, landed, ici_send.at[j], ici_recv.at[j], sibling).wait_recv()
        for cp in sends:
            cp.wait_send()
        total = gath[0]
        for k in range(1, N_CHIPS):
            total = total + gath[k]
        out_ref[mine, :] = total
        last = _remote(out_ref.at[mine, :], out_ref.at[mine, :], d2d_send.at[1], d2d_recv.at[1], sibling)
        last.start()
        _remote(out_ref.at[theirs, :], out_ref.at[theirs, :], d2d_send.at[1], d2d_recv.at[1], sibling).wait_recv()
        last.wait_send()

    vmem = pl.BlockSpec(memory_space=pltpu.VMEM)
    return pl.pallas_call(
        body, name=name, out_shape=_sds((r, n), F32), in_specs=[vmem], out_specs=vmem,
        scratch_shapes=[pltpu.VMEM((r, n), F32), pltpu.VMEM((r, n), F32), pltpu.VMEM((N_CHIPS, hr, n), F32),
                        pltpu.SemaphoreType.DMA((2,)), pltpu.SemaphoreType.DMA((2,)),
                        pltpu.SemaphoreType.DMA((3,)), pltpu.SemaphoreType.DMA((3,))],
        compiler_params=pltpu.CompilerParams(vmem_limit_bytes=VMEM_LIMIT_BYTES),
    )(block)


def _cast_place(shards, chip_arr):
    na = len(shards)
    steps = 4

    def body(chip_ref, *refs):
        for a in range(na):
            refs[na + a][0] = refs[a][...].astype(BF16)

    grid_spec = pltpu.PrefetchScalarGridSpec(
        num_scalar_prefetch=1, grid=(steps,),
        in_specs=[pl.BlockSpec((s.shape[0] // steps, s.shape[1]), lambda i, ch: (i, 0)) for s in shards],
        out_specs=[pl.BlockSpec((1, s.shape[0] // steps, s.shape[1]), lambda i, ch: (ch[0], i, 0)) for s in shards])
    return pl.pallas_call(
        body, name="cast_place", grid_spec=grid_spec,
        out_shape=[_sds((N_CHIPS,) + s.shape, BF16) for s in shards],
        compiler_params=_params(("arbitrary",)),
    )(chip_arr, *shards)


def _silu(v):
    return v * _sigmoid(v)


def _ada_fwd(c8, w_ada):
    def body(c_ref, w_ref, o_ref):
        o_ref[...] = jnp.dot(_silu(c_ref[...]), w_ref[...], preferred_element_type=F32,
                             precision=lax.Precision.HIGHEST)

    return pl.pallas_call(
        body, name="ada_fwd", out_shape=_sds((N_DEV, w_ada.shape[1]), F32),
        compiler_params=pltpu.CompilerParams(vmem_limit_bytes=VMEM_LIMIT_BYTES),
    )(c8, w_ada)


def _mod_select(parts, b_ada, me_arr, after):
    cols = parts.shape[2]

    def body(me_ref, p_ref, b_ref, after_ref, o_ref):
        me = me_ref[0]
        for k in range(N_CHIPS):
            cs = slice(k * cols, (k + 1) * cols)
            o_ref[:, cs] = p_ref[2 * k, pl.ds(me, 1), :] + b_ref[:, cs]

    grid_spec = pltpu.PrefetchScalarGridSpec(
        num_scalar_prefetch=1, grid=(1,),
        in_specs=[pl.BlockSpec(parts.shape, lambda i, m: (0, 0, 0)), pl.BlockSpec(b_ada.shape, lambda i, m: (0, 0)),
                  _ANY],
        out_specs=pl.BlockSpec(b_ada.shape, lambda i, m: (0, 0)))
    return pl.pallas_call(body, name="mod_select", grid_spec=grid_spec, out_shape=_sds(b_ada.shape, F32))(
        me_arr, parts, b_ada, after)


def _ada_bwd(c8, dmod8, chip_arr):
    d = c8.shape[1]
    cols = dmod8.shape[1] // N_CHIPS

    def body(chip_ref, c_ref, dm_ref, dmall_ref, gw_ref, gb_ref):
        gw_ref[...] = lax.dot_general(_silu(c_ref[...]), dm_ref[...], (((0,), (0,)), ((), ())),
                                      preferred_element_type=F32, precision=lax.Precision.HIGHEST)
        acc = dmall_ref[0:1, :]
        for k in range(1, N_DEV):
            acc = acc + dmall_ref[k:k + 1, :]
        gb_ref[...] = acc

    grid_spec = pltpu.PrefetchScalarGridSpec(
        num_scalar_prefetch=1, grid=(1,),
        in_specs=[pl.BlockSpec(c8.shape, lambda i, ch: (0, 0)),
                  pl.BlockSpec((N_DEV, cols), lambda i, ch: (0, ch[0])),
                  pl.BlockSpec(dmod8.shape, lambda i, ch: (0, 0))],
        out_specs=[pl.BlockSpec((d, cols), lambda i, ch: (0, 0)), pl.BlockSpec((1, dmod8.shape[1]), lambda i, ch: (0, 0))])
    return pl.pallas_call(
        body, name="ada_bwd", grid_spec=grid_spec,
        out_shape=[_sds((d, cols), F32), _sds((1, dmod8.shape[1]), F32)],
        compiler_params=_params(("arbitrary",)),
    )(chip_arr, c8, dmod8, dmod8)


def _adam_math(w, g, m, v):
    m = ADAM_B1 * m + (1.0 - ADAM_B1) * g
    v = ADAM_B2 * v + (1.0 - ADAM_B2) * (g * g)
    m_hat = m / (1.0 - ADAM_B1 ** ADAM_STEP)
    v_hat = v / (1.0 - ADAM_B2 ** ADAM_STEP)
    delta = -ADAM_LR * (m_hat / (jnp.sqrt(v_hat) + ADAM_EPS) + ADAM_WD * w)
    return delta, m, v


def _adam(w, g, m, v, name, tr=256):
    rows, cols = w.shape
    if rows % tr:
        tr = rows

    def body(w_ref, g_ref, m_ref, v_ref, d_ref, nm_ref, nv_ref):
        d_ref[...], nm_ref[...], nv_ref[...] = _adam_math(w_ref[...], g_ref[...], m_ref[...], v_ref[...])

    spec = pl.BlockSpec((tr, cols), lambda i: (i, 0))
    return pl.pallas_call(
        body, name=name, grid=(rows // tr,), in_specs=[spec] * 4, out_specs=[spec] * 3,
        out_shape=[_sds(w.shape, F32)] * 3, compiler_params=_params(("parallel",)),
    )(w, g, m, v)


def _adam_cols(w, g_full, m, v, chip_arr, name):
    rows, cols = w.shape

    def body(chip_ref, w_ref, g_ref, m_ref, v_ref, gs_ref, d_ref, nm_ref, nv_ref):
        g = g_ref[...]
        gs_ref[...] = g
        d_ref[...], nm_ref[...], nv_ref[...] = _adam_math(w_ref[...], g, m_ref[...], v_ref[...])

    own = pl.BlockSpec((rows, cols), lambda i, ch: (0, 0))
    grid_spec = pltpu.PrefetchScalarGridSpec(
        num_scalar_prefetch=1, grid=(1,),
        in_specs=[own, pl.BlockSpec((rows, cols), lambda i, ch: (0, ch[0])), own, own],
        out_specs=[own] * 4)
    return pl.pallas_call(body, name=name, grid_spec=grid_spec, out_shape=[_sds(w.shape, F32)] * 4)(
        chip_arr, w, g_full, m, v)


PACK_COLS = 512
SMALL_REPLICATED = ("g_mix_pre", "g_mix_post", "conv_b", "w_rgate", "b_rgate", "w_igate", "b_igate", "lru_a",
                    "v_norm_g", "v_norm_b", "w_spatial", "b_spatial", "g_lru_out", "g_gmlp_out", "g_ffn_pre",
                    "g_ffn_post", "ffn_conv_b")
SMALL_COLUMN_SHARDED = ("conv_w", "ffn_conv_w")


def _pack(arrays):
    parts = []
    for arr in arrays:
        p = arr.reshape(-1, PACK_COLS)
        pad = (-p.shape[0]) % SUBLANES
        parts.append(jnp.pad(p, ((0, pad), (0, 0))) if pad else p)
    total = sum(p.shape[0] for p in parts)
    if total % (2 * SUBLANES):
        parts.append(jnp.zeros((SUBLANES, PACK_COLS), parts[0].dtype))
    return jnp.concatenate(parts, axis=0)


def _unpack(packed, shapes):
    out, row = [], 0
    for shape in shapes:
        n = math.prod(shape) // PACK_COLS
        out.append(packed[row:row + n].reshape(shape))
        row += n + (-n) % SUBLANES
    return out


def kernel(x, c, w_ada, b_ada, g_mix_pre, g_mix_post, w_in, conv_w, conv_b, w_rgate, b_rgate, w_igate, b_igate, lru_a, v_norm_g, v_norm_b, w_spatial, b_spatial, g_lru_out, g_gmlp_out, w_out, g_ffn_pre, g_ffn_post, w_up, ffn_conv_w, ffn_conv_b, w_down, loss_target, m_w_ada, m_b_ada, m_g_mix_pre, m_g_mix_post, m_w_in, m_conv_w, m_conv_b, m_w_rgate, m_b_rgate, m_w_igate, m_b_igate, m_lru_a, m_v_norm_g, m_v_norm_b, m_w_spatial, m_b_spatial, m_g_lru_out, m_g_gmlp_out, m_w_out, m_g_ffn_pre, m_g_ffn_post, m_w_up, m_ffn_conv_w, m_ffn_conv_b, m_w_down, v_w_ada, v_b_ada, v_g_mix_pre, v_g_mix_post, v_w_in, v_conv_w, v_conv_b, v_w_rgate, v_b_rgate, v_w_igate, v_b_igate, v_lru_a, v_v_norm_g, v_v_norm_b, v_w_spatial, v_b_spatial, v_g_lru_out, v_g_gmlp_out, v_w_out, v_g_ffn_pre, v_g_ffn_post, v_w_up, v_ffn_conv_w, v_ffn_conv_b, v_w_down):
    args = dict(locals())
    names = ("w_ada", "b_ada", "g_mix_pre", "g_mix_post", "w_in", "conv_w", "conv_b", "w_rgate", "b_rgate",
             "w_igate", "b_igate", "lru_a", "v_norm_g", "v_norm_b", "w_spatial", "b_spatial", "g_lru_out",
             "g_gmlp_out", "w_out", "g_ffn_pre", "g_ffn_post", "w_up", "ffn_conv_w", "ffn_conv_b", "w_down")
    w = {n: args[n][0] for n in names}
    m = {n: args["m_" + n][0] for n in names}
    v = {n: args["v_" + n][0] for n in names}
    xi, yi, ci = _position()
    me_arr = jnp.reshape(4 * xi + 2 * yi + ci, (1,)).astype(jnp.int32)
    chip_arr = jnp.reshape(2 * xi + yi, (1,)).astype(jnp.int32)
    c_arr = jnp.reshape(ci, (1,)).astype(jnp.int32)
    pos_arr = jnp.stack([ci, 2 * xi + yi]).astype(jnp.int32)

    big = ("w_in", "w_out", "w_up", "w_down")
    lands = _cast_place([w[n] for n in big], chip_arr)
    start_a, wait_a = _gather_plan([w[n].shape[0] for n in big[:2]])
    start_b, wait_b = _gather_plan([w[n].shape[0] for n in big[2:]])

    row0 = jnp.concatenate([c, w["conv_w"].reshape(1, -1), w["ffn_conv_w"].reshape(1, -1)], axis=1)
    g0 = _allgather8(row0, "gather_cond", False)[:, 0, :]
    c8 = g0[:, :D_MODEL]
    per_chip = g0[0::2]
    conv_w_full = per_chip[:, D_MODEL:D_MODEL + 512].reshape(N_CHIPS, 4, 128).transpose(1, 0, 2).reshape(4, 512)
    ffn_conv_w_full = per_chip[:, D_MODEL + 512:].reshape(N_CHIPS, 3, 1536).transpose(1, 0, 2).reshape(3, 2 * D_FF)
    mod_parts = _allgather8(_ada_fwd(c8, w["w_ada"]), "gather_mod", False)
    send_a, recv_a, _, lands_a, token_a = _split_start([], lands[:2], start_a, 6, mod_parts, "gather_start_a")
    send_b, recv_b, _, lands_b, token_b = _split_start([], lands[2:], start_b, 6, token_a, "gather_start_b")
    mod = _mod_select(mod_parts, w["b_ada"].reshape(1, -1), me_arr, token_b).reshape(N_MOD, D_MODEL)
    sh_m, sc_m, gt_m, sh_f, sc_f, gt_f = [mod[k:k + 1] for k in range(N_MOD)]

    small = {n: w[n] for n in SMALL_REPLICATED}
    small["conv_w"] = conv_w_full
    small["ffn_conv_w"] = ffn_conv_w_full
    row = lambda a: a.reshape(1, -1)
    seq_params, ws_t = _seq_params(small)
    glo, ggo = row(small["g_lru_out"]), row(small["g_gmlp_out"])
    g_pre, g_post = row(small["g_mix_pre"]), row(small["g_mix_post"])
    g_pre2, g_post2 = row(small["g_ffn_pre"]), row(small["g_ffn_post"])
    fw, fb = small["ffn_conv_w"], row(small["ffn_conv_b"])
    xs, tgt = x[0], loss_target[0]

    _, lands_a = _split_wait(send_a, recv_a, [], lands_a, wait_a, mod, "gather_wait_a")
    w_in4, w_out4 = _forward_to_sibling(lands_a, "forward_a")
    w_out_b = w_out4.reshape(D_MODEL, D_MODEL)
    z, h = _mix_in(xs, sc_m, sh_m, g_pre, w_in4)
    ycat, hst = _seqmix(z, seq_params, glo, ggo)
    y, x1, h2 = _mix_out(ycat, xs, w_out_b, gt_m, g_post, g_pre2, sc_f, sh_f)
    _, lands_b = _split_wait(send_b, recv_b, [], lands_b, wait_b, h2, "gather_wait_b")
    w_up4, w_down4 = _forward_to_sibling(lands_b, "forward_b")
    w_down_b = w_down4.reshape(D_FF, D_MODEL)
    up0, pre, act, dy2, dx2, loss, dgt_f, dg_post2 = _ffn_fwd(h2, x1, tgt, w_up4, w_down_b, fw, fb, gt_f, g_post2)

    dup0, dfw, dfb = _ffn_bwd_a(dy2, pre, up0, w_down_b, fw)
    gw_up = _wgrad(h2, dup0, N_CHIPS, "wgrad_up", True)
    gw_down = _wgrad(act, dy2, 2, "wgrad_down", False)
    ex_start, ex_wait = _exchange_plan(2)

    def reduce_start(parts, tags, name):
        recv = _swap_halves(parts, "swap_halves_" + name)
        both = [_chip_sum(p, r, pos_arr, "chip_sum_" + t) for p, r, t in zip(parts, recv, tags)]
        sums, gath = [b[0] for b in both], [b[1] for b in both]
        return _split_start(sums, gath, ex_start, 3 * len(parts), pos_arr, "exchange_start_" + name)

    e_send_b, e_recv_b, sums_b, gath_b, token_b = reduce_start(
        [gw_up, gw_down.reshape(N_CHIPS, -1, D_MODEL)], ("w_up", "w_down"), "b")

    dx1, dy, dsh_f, dsc_f, dg_pre2, dgt_m, dg_post = _ffn_bwd_b(
        dup0, x1, y, dx2, w_up4, g_pre2, sc_f + token_b[0:1, 0:1], sh_f, gt_m, g_post)
    (dz, dcw, dcb, dwr, dwi, dbr, dbi, dspa, dng, dnb, dws, dbs_t, dglo, dggo) = _seqmix_bwd(
        z, hst, dy, w_out_b, seq_params, ws_t, glo, ggo)
    grad_x, dsh_m, dsc_m, dg_pre = _mix_in_bwd(xs, dz, dx1, w_in4, g_pre, sc_m)
    gw_in = _wgrad(h, dz, N_CHIPS, "wgrad_in", True)
    gw_out = _wgrad(ycat, dy, 1, "wgrad_out", False)
    e_send_a, e_recv_a, sums_a, gath_a, token_a = reduce_start(
        [gw_in, gw_out.reshape(N_CHIPS, -1, D_MODEL)], ("w_in", "w_out"), "a")

    grads, deltas, new_m, new_v = {}, {}, {}, {}

    def reduce_finish(send, recv, sums, gath, tags, after, name):
        sums, gath = _split_wait(send, recv, sums, gath, ex_wait, after, "exchange_wait_" + name)
        gath = _swap_gathered(gath, "swap_gathered_" + name)
        for g, t in zip(gath, tags):
            grads[t], deltas[t], new_m[t], new_v[t] = _adam_gathered(w[t], g, m[t], v[t], c_arr, "adam_" + t)

    reduce_finish(e_send_b, e_recv_b, sums_b, gath_b, ("w_up", "w_down"), token_a, "b")
    reduce_finish(e_send_a, e_recv_a, sums_a, gath_a, ("w_in", "w_out"), deltas["w_down"], "a")

    dmod = jnp.concatenate([dsh_m, dsc_m, dgt_m, dsh_f, dsc_f, dgt_f], axis=1)
    dmod8 = _allgather8(dmod, "gather_dmod", False)[:, 0, :]
    g_w_ada, g_b_ada = _ada_bwd(c8, dmod8, chip_arr)
    grads["w_ada"] = g_w_ada
    deltas["w_ada"], new_m["w_ada"], new_v["w_ada"] = _adam(w["w_ada"], g_w_ada, m["w_ada"], v["w_ada"], "adam_w_ada")

    small_grads = dict(
        g_mix_pre=dg_pre[0], g_mix_post=dg_post[0], conv_w=dcw, conv_b=dcb[0],
        w_rgate=_diag_blocks(dwr), b_rgate=dbr.reshape(LRU_HEADS, -1),
        w_igate=_diag_blocks(dwi), b_igate=dbi.reshape(LRU_HEADS, -1), lru_a=dspa[0],
        v_norm_g=dng[0], v_norm_b=dnb[0], w_spatial=dws, b_spatial=dbs_t.T,
        g_lru_out=dglo[0], g_gmlp_out=dggo[0], g_ffn_pre=dg_pre2[0], g_ffn_post=dg_post2[0],
        ffn_conv_w=dfw, ffn_conv_b=dfb[0])
    packed_names = SMALL_REPLICATED + SMALL_COLUMN_SHARDED
    g_small = _allreduce_small(_pack([small_grads[n] for n in packed_names]), "reduce_small")
    g_small_list = _unpack(g_small, [small_grads[n].shape for n in packed_names])
    g_rep = dict(zip(packed_names, g_small_list))

    rep = SMALL_REPLICATED
    d_p, m_p, v_p = _adam(_pack([w[n] for n in rep]), _pack([g_rep[n] for n in rep]),
                          _pack([m[n] for n in rep]), _pack([v[n] for n in rep]), "adam_small")
    shapes = [w[n].shape for n in rep]
    for n, dd, mm, vv in zip(rep, _unpack(d_p, shapes), _unpack(m_p, shapes), _unpack(v_p, shapes)):
        grads[n], deltas[n], new_m[n], new_v[n] = g_rep[n], dd, mm, vv
    for n in SMALL_COLUMN_SHARDED:
        grads[n], deltas[n], new_m[n], new_v[n] = _adam_cols(w[n], g_rep[n], m[n], v[n], chip_arr, "adam_" + n)
    b2 = lambda a: a.reshape(-1, PACK_COLS)
    d_b, m_b, v_b = _adam(b2(w["b_ada"]), b2(g_b_ada), b2(m["b_ada"]), b2(v["b_ada"]), "adam_b_ada")
    grads["b_ada"], deltas["b_ada"], new_m["b_ada"], new_v["b_ada"] = (
        g_b_ada.reshape(-1), d_b.reshape(-1), m_b.reshape(-1), v_b.reshape(-1))

    total = lax.psum(loss[0, 0], ("x", "y", "c"))
    outs = [total, grad_x[None]]
    for group in (grads, deltas, new_m, new_v):
        outs.extend(group[n][None] for n in names)
    return tuple(outs)
```

```python
import functools
import math

import jax
import jax.numpy as jnp
from jax import lax
from jax.experimental import pallas as pl
from jax.experimental.pallas import tpu as pltpu

F32 = jnp.float32
BF16 = jnp.bfloat16
MESH = pl.DeviceIdType.MESH

D_MODEL = 1024
LRU_WIDTH = 512
LRU_HEADS = 8
GMLP_WIDTH = 512
GMLP_GROUPS = 4
GMLP_BLOCK = 128
CHUNK = 64
D_FF = 3072
N_MOD = 6
EPS = 1e-6
LRU_C = 8.0
N_CHIPS = 4
N_DEV = 8

ADAM_LR = 0.001
ADAM_B1 = 0.9
ADAM_B2 = 0.999
ADAM_EPS = 1e-08
ADAM_WD = 0.01
ADAM_STEP = 10

GELU_C0 = math.sqrt(2.0 / math.pi)
GELU_C1 = 0.044715

VMEM_LIMIT_BYTES = 56 * 1024 * 1024
SUBLANES = 8
BF16_SUBLANES = 16
FFN_CHUNK = 768
SUB_ROWS = 256


def _gelu(x):
    t = jnp.tanh(GELU_C0 * (x + GELU_C1 * x * x * x))
    return 0.5 * x * (1.0 + t)


def _gelu_and_grad(x):
    x2 = x * x
    t = jnp.tanh(GELU_C0 * x * (1.0 + GELU_C1 * x2))
    g = 0.5 * x * (1.0 + t)
    dg = 0.5 * (1.0 + t) + 0.5 * x * (1.0 - t * t) * (GELU_C0 * (1.0 + 3.0 * GELU_C1 * x2))
    return g, dg


def _sigmoid(x):
    return 1.0 / (1.0 + jnp.exp(-x))


def _log1p(u):
    w = 1.0 + u
    return jnp.where(w == 1.0, u, jnp.log(w) * (u / (w - 1.0)))


def _softplus(x):
    return jnp.maximum(x, 0.0) + _log1p(jnp.exp(-jnp.abs(x)))


def _neg_expm1(x):
    u = jnp.exp(x)
    um1 = u - 1.0
    tiny = um1 == 0.0
    small = um1 * (x / jnp.log(jnp.where(tiny, 2.0, jnp.maximum(u, 0.25))))
    return -jnp.where(tiny, x, jnp.where(x < -1.0, um1, small))


def _msq_rsqrt(v):
    return lax.rsqrt(jnp.mean(v * v, axis=-1, keepdims=True) + EPS)


def _rms_bwd(dyn, yn, r):
    return r * (dyn - yn * jnp.mean(dyn * yn, axis=-1, keepdims=True))


def _colsum(v):
    return jnp.sum(v, axis=0, keepdims=True)


def _shift_down(cur, prev8, k):
    rolled = pltpu.roll(cur, k, 0)
    head = pltpu.roll(prev8, k, 0)
    row8 = lax.broadcasted_iota(jnp.int32, (SUBLANES, cur.shape[1]), 0)
    first = jnp.where(row8 < k, head, rolled[0:SUBLANES])
    return jnp.concatenate([first, rolled[SUBLANES:]], axis=0)


def _shift_up(cur, next8, k):
    t = cur.shape[0]
    rolled = pltpu.roll(cur, t - k, 0)
    tail = pltpu.roll(next8, SUBLANES - k, 0)
    row8 = lax.broadcasted_iota(jnp.int32, (SUBLANES, cur.shape[1]), 0)
    last = jnp.where(row8 >= SUBLANES - k, tail, rolled[t - SUBLANES:])
    return jnp.concatenate([rolled[:t - SUBLANES], last], axis=0)


def _scan_fwd(a, b):
    t = a.shape[0]
    row = lax.broadcasted_iota(jnp.int32, a.shape, 0)
    d = 1
    while d < t:
        keep = row >= d
        a_s = jnp.where(keep, pltpu.roll(a, d, 0), 1.0)
        b_s = jnp.where(keep, pltpu.roll(b, d, 0), 0.0)
        b = a * b_s + b
        a = a * a_s
        d *= 2
    return a, b


def _scan_bwd(a, g):
    t = a.shape[0]
    row = lax.broadcasted_iota(jnp.int32, a.shape, 0)
    d = 1
    while d < t:
        keep = row < t - d
        a_s = jnp.where(keep, pltpu.roll(a, t - d, 0), 1.0)
        g_s = jnp.where(keep, pltpu.roll(g, t - d, 0), 0.0)
        g = a * g_s + g
        a = a * a_s
        d *= 2
    return a, g


def _dot(a, b):
    return jnp.dot(a, b, preferred_element_type=F32)


def _dot_nt(a, b):
    return lax.dot_general(a, b, (((1,), (1,)), ((), ())), preferred_element_type=F32)


def _dot_tn(a, b):
    return lax.dot_general(a, b, (((0,), (0,)), ((), ())), preferred_element_type=F32)


def _rows(ts, cols, rev_of=None):
    if rev_of is None:
        return pl.BlockSpec((ts, cols), lambda i: (i, 0))
    return pl.BlockSpec((ts, cols), lambda i: (rev_of - 1 - i, 0))


def _halo_prev(ts, cols, halo, rev_of=None, col_block=0):
    per = ts // halo
    if rev_of is None:
        return pl.BlockSpec((halo, cols), lambda i: (jnp.maximum(i * per - 1, 0), col_block))
    return pl.BlockSpec((halo, cols), lambda i: (jnp.maximum((rev_of - 1 - i) * per - 1, 0), col_block))


def _full(shape):
    nd = len(shape)
    return pl.BlockSpec(shape, lambda *_: (0,) * nd)


_RESIDENT = pl.BlockSpec(memory_space=pltpu.VMEM)


def _params(sem):
    return pltpu.CompilerParams(dimension_semantics=sem, vmem_limit_bytes=VMEM_LIMIT_BYTES)


def _sds(shape, dtype):
    return jax.ShapeDtypeStruct(shape, dtype)


def _sub_tiles(ts):
    return [slice(r0, r0 + SUB_ROWS) for r0 in range(0, ts, SUB_ROWS)]


def _mix_in(x, sc, sh, g, w_in4, ts=512):
    s, d = x.shape

    def body(x_ref, sc_ref, sh_ref, g_ref, w_ref, z_ref, h_ref):
        for rs in _sub_tiles(ts):
            xv = x_ref[rs, :]
            h = (xv * _msq_rsqrt(xv) * g_ref[...]) * (1.0 + sc_ref[...]) + sh_ref[...]
            hb = h.astype(BF16)
            h_ref[rs, :] = hb
            for k in range(N_CHIPS):
                z_ref[rs, k * 512:(k + 1) * 512] = _dot(hb, w_ref[k])

    return pl.pallas_call(
        body, grid=(s // ts,), name="mix_in",
        in_specs=[_rows(ts, d), _full((1, d)), _full((1, d)), _full((1, d)), _full(w_in4.shape)],
        out_specs=[_rows(ts, 2048), _rows(ts, d)],
        out_shape=[_sds((s, 2048), F32), _sds((s, d), BF16)],
        compiler_params=_params(("parallel",)),
    )(x, sc, sh, g, w_in4)


def _seq_recompute(z_ref, zprev_ref, first_tile, p):
    (cw_ref, cb_ref, bdr_ref, bdi_ref, br_ref, bi_ref, la_ref, ng_ref, nb_ref, ws_ref, bst_ref) = p
    lx = z_ref[:, 0:512]
    lg = z_ref[:, 512:1024]
    gu = z_ref[:, 1024:1536]
    gv = z_ref[:, 1536:2048]
    prev8 = jnp.where(first_tile, 0.0, zprev_ref[...])
    s1 = _shift_down(lx, prev8, 1)
    s2 = _shift_down(lx, prev8, 2)
    s3 = _shift_down(lx, prev8, 3)
    xc = cw_ref[3:4, :] * lx + cw_ref[2:3, :] * s1 + cw_ref[1:2, :] * s2 + cw_ref[0:1, :] * s3 + cb_ref[...]
    xcb = xc.astype(BF16)
    r = _sigmoid(_dot(xcb, bdr_ref[...]) + br_ref[...])
    ig = _sigmoid(_dot(xcb, bdi_ref[...]) + bi_ref[...])
    spa = _softplus(-la_ref[...])
    log_a = (-LRU_C) * r * spa
    a = jnp.exp(log_a)
    mult = jnp.sqrt(_neg_expm1(2.0 * log_a))
    return dict(lx=lx, lg=lg, gu=gu, gv=gv, s1=s1, s2=s2, s3=s3, xc=xc, xcb=xcb, r=r, ig=ig, spa=spa,
                a=a, mult=mult)


def _gmlp_fwd(gu, gv, ng_ref, nb_ref, ws_ref, bst_ref, sp_scr):
    ts = gu.shape[0]
    u, du = _gelu_and_grad(gu)
    vg, dvg = _gelu_and_grad(gv)
    mu = jnp.mean(vg, axis=-1, keepdims=True)
    vc = vg - mu
    rstd = lax.rsqrt(jnp.mean(vc * vc, axis=-1, keepdims=True) + EPS)
    vhat = vc * rstd
    v = vhat * ng_ref[...] + nb_ref[...]
    vb = v.astype(BF16)
    for n in range(ts // GMLP_BLOCK):
        rs = slice(n * GMLP_BLOCK, (n + 1) * GMLP_BLOCK)
        for g in range(GMLP_GROUPS):
            cs = slice(g * 128, (g + 1) * 128)
            sp_scr[rs, cs] = _dot(ws_ref[g], vb[rs, cs]) + bst_ref[:, g:g + 1]
    spb = sp_scr[...]
    return dict(u=u, du=du, dvg=dvg, rstd=rstd, vhat=vhat, vb=vb, spb=spb, y_g=u * spb)


def _seq_specs(ts, nt, rev):
    rev_of = nt if rev else None
    return [
        _rows(ts, 2048, rev_of),
        _halo_prev(ts, 512, SUBLANES, rev_of),
    ]


def _seq_param_specs():
    return [_full((4, 512)), _full((1, 512)), _full((512, 512)), _full((512, 512)), _full((1, 512)),
            _full((1, 512)), _full((1, 512)), _full((1, 512)), _full((1, 512)), _full((4, 128, 128)),
            _full((128, 4))]


def _seqmix(z, seq_params, glo, ggo, ts=256):
    s = z.shape[0]
    nt = s // ts

    def body(z_ref, zprev_ref, *rest):
        p = rest[:11]
        glo_ref, ggo_ref, ycat_ref, hst_ref, hcarry, sp_scr = rest[11:]
        i = pl.program_id(0)

        @pl.when(i == 0)
        def _():
            hcarry[...] = jnp.zeros_like(hcarry)

        f = _seq_recompute(z_ref, zprev_ref, i == 0, p)
        bx = f["mult"] * (f["ig"] * f["xc"])
        acum, hloc = _scan_fwd(f["a"], bx)
        h = hloc + acum * hcarry[...]
        hcarry[...] = h[ts - 1:ts, :]
        hst_ref[...] = h
        y_l = h * _gelu(f["lg"])
        gm = _gmlp_fwd(f["gu"], f["gv"], p[7], p[8], p[9], p[10], sp_scr)
        y_g = gm["y_g"]
        ycat_ref[:, 0:512] = (y_l * _msq_rsqrt(y_l) * glo_ref[...]).astype(BF16)
        ycat_ref[:, 512:1024] = (y_g * _msq_rsqrt(y_g) * ggo_ref[...]).astype(BF16)

    return pl.pallas_call(
        body, grid=(nt,), name="seqmix",
        in_specs=_seq_specs(ts, nt, False) + _seq_param_specs() + [_full((1, 512)), _full((1, 512))],
        out_specs=[_rows(ts, 1024), _rows(ts, 512)],
        out_shape=[_sds((s, 1024), BF16), _sds((s, 512), F32)],
        scratch_shapes=[pltpu.VMEM((1, 512), F32), pltpu.VMEM((ts, 512), F32)],
        compiler_params=_params(("arbitrary",)),
    )(z, z, *seq_params, glo, ggo)


def _mix_out(ycat, x, w_out, gt_m, g_post, g_pre2, sc_f, sh_f, ts=512):
    s, d = x.shape

    def body(yc_ref, x_ref, w_ref, gt_ref, gp_ref, g2_ref, sc_ref, sh_ref, y_ref, x1_ref, h2_ref):
        for rs in _sub_tiles(ts):
            y = _dot(yc_ref[rs, :], w_ref[...])
            y_ref[rs, :] = y
            x1 = x_ref[rs, :] + gt_ref[...] * (y * _msq_rsqrt(y) * gp_ref[...])
            x1_ref[rs, :] = x1
            h2 = (x1 * _msq_rsqrt(x1) * g2_ref[...]) * (1.0 + sc_ref[...]) + sh_ref[...]
            h2_ref[rs, :] = h2.astype(BF16)

    vec = _full((1, d))
    return pl.pallas_call(
        body, grid=(s // ts,), name="mix_out",
        in_specs=[_rows(ts, d), _rows(ts, d), _full((d, d)), vec, vec, vec, vec, vec],
        out_specs=[_rows(ts, d), _rows(ts, d), _rows(ts, d)],
        out_shape=[_sds((s, d), F32), _sds((s, d), F32), _sds((s, d), BF16)],
        compiler_params=_params(("parallel",)),
    )(ycat, x, w_out, gt_m, g_post, g_pre2, sc_f, sh_f)


def _ffn_cols(j):
    per = (2 * D_FF // N_CHIPS) // FFN_CHUNK
    return j // per, (j % per) * FFN_CHUNK, j * FFN_CHUNK


def _ffn_fwd(h2, x1, tgt, w_up4, w_down, fw, fb, gt_f, g_post, ts=256):
    s, d = x1.shape
    nch = D_FF // FFN_CHUNK

    def body(h2_ref, x1_ref, tgt_ref, wup_ref, wdn_ref, fw_ref, fb_ref, gt_ref, gp_ref,
             up0_ref, pre_ref, act_ref, dy2_ref, dx2_ref, loss_ref, dgt_ref, dgp_ref, tail_ref):
        i = pl.program_id(0)

        @pl.when(i == 0)
        def _():
            tail_ref[...] = jnp.zeros_like(tail_ref)
            loss_ref[...] = jnp.zeros_like(loss_ref)
            dgt_ref[...] = jnp.zeros_like(dgt_ref)
            dgp_ref[...] = jnp.zeros_like(dgp_ref)

        hb = h2_ref[...]

        def up_project(j):
            sh_g, off, _ = _ffn_cols(j)
            return [_dot(hb, wup_ref[shard, :, off:off + FFN_CHUNK]).astype(BF16) for shard in (sh_g, sh_g + 2)]

        y2 = jnp.zeros((ts, d), F32)
        ahead = up_project(0)
        for j in range(nch):
            _, _, col = _ffn_cols(j)
            ubs = ahead
            if j + 1 < nch:
                ahead = up_project(j + 1)
            halves = []
            for ub, c0 in zip(ubs, (col, D_FF + col)):
                cs = slice(c0, c0 + FFN_CHUNK)
                up0_ref[:, cs] = ub
                u = ub.astype(F32)
                prev8 = tail_ref[:, cs]
                tail_ref[:, cs] = u[ts - SUBLANES:, :]
                halves.append(fw_ref[2:3, cs] * u + fw_ref[1:2, cs] * _shift_down(u, prev8, 1)
                              + fw_ref[0:1, cs] * _shift_down(u, prev8, 2) + fb_ref[:, cs])
                pre_ref[:, cs] = halves[-1].astype(BF16)
            act = (_gelu(halves[0]) * halves[1]).astype(BF16)
            act_ref[:, col:col + FFN_CHUNK] = act
            y2 = y2 + _dot(act, wdn_ref[col:col + FFN_CHUNK, :])
        r2 = _msq_rsqrt(y2)
        yn = y2 * r2
        yng = yn * gp_ref[...]
        e = x1_ref[...] + gt_ref[...] * yng - tgt_ref[...]
        loss_ref[...] += jnp.sum(e * e) * (0.5 / d)
        dx2 = e * (1.0 / d)
        dx2_ref[...] = dx2
        dgt_ref[...] += _colsum(dx2 * yng)
        dyng = dx2 * gt_ref[...]
        dgp_ref[...] += _colsum(dyng * yn)
        dy2_ref[...] = _rms_bwd(dyng * gp_ref[...], yn, r2).astype(BF16)

    vec = _full((1, d))
    return pl.pallas_call(
        body, grid=(s // ts,), name="ffn_fwd",
        in_specs=[_rows(ts, d), _rows(ts, d), _rows(ts, d), _RESIDENT, _RESIDENT,
                  _full((3, 2 * D_FF)), _full((1, 2 * D_FF)), vec, vec],
        out_specs=[_rows(ts, 2 * D_FF), _rows(ts, 2 * D_FF), _rows(ts, D_FF), _rows(ts, d), _rows(ts, d),
                   _full((1, 128)), vec, vec],
        out_shape=[_sds((s, 2 * D_FF), BF16), _sds((s, 2 * D_FF), BF16), _sds((s, D_FF), BF16), _sds((s, d), BF16),
                   _sds((s, d), F32), _sds((1, 128), F32), _sds((1, d), F32), _sds((1, d), F32)],
        scratch_shapes=[pltpu.VMEM((SUBLANES, 2 * D_FF), F32)],
        compiler_params=_params(("arbitrary",)),
    )(h2, x1, tgt, w_up4, w_down, fw, fb, gt_f, g_post)


def _shift_up_mxu(vb, up_mat, next8, k):
    t = vb.shape[0]
    main = _dot(up_mat, vb)
    tail = pltpu.roll(next8, SUBLANES - k, 0)
    row8 = lax.broadcasted_iota(jnp.int32, next8.shape, 0)
    last = main[t - SUBLANES:] + jnp.where(row8 >= SUBLANES - k, tail, 0.0)
    return jnp.concatenate([main[:t - SUBLANES], last], axis=0)


def _ffn_bwd_a(dy2, pre, up0, w_down, fw, ts=256):
    s, d = dy2.shape
    nt = s // ts
    nch = D_FF // FFN_CHUNK
    wide = 2 * D_FF
    up_mats = jnp.stack([jnp.eye(ts, k=1, dtype=BF16), jnp.eye(ts, k=2, dtype=BF16)])

    def body(dy2_ref, pre_ref, up0_ref, wdn_ref, fw_ref, um_ref, dup0_ref, dfw_ref, dfb_ref, next_ref):
        i = pl.program_id(0)

        @pl.when(i == 0)
        def _():
            next_ref[...] = jnp.zeros_like(next_ref)
            dfw_ref[...] = jnp.zeros_like(dfw_ref)
            dfb_ref[...] = jnp.zeros_like(dfb_ref)

        dyb = dy2_ref[...]
        for j in range(nch):
            _, _, col = _ffn_cols(j)
            dact = _dot_nt(dyb, wdn_ref[col:col + FFN_CHUNK, :])
            gl, dgl = _gelu_and_grad(pre_ref[:, col:col + FFN_CHUNK].astype(F32))
            dpre = (dact * pre_ref[:, D_FF + col:D_FF + col + FFN_CHUNK].astype(F32) * dgl, dact * gl)
            for half, c0 in enumerate((col, D_FF + col)):
                cs = slice(c0, c0 + FFN_CHUNK)
                dp = dpre[half]
                dpb = dp.astype(BF16)
                nxt = next_ref[:, cs]
                next_ref[:, cs] = dpb.astype(F32)[0:SUBLANES, :]
                su1 = _shift_up_mxu(dpb, um_ref[0], nxt, 1)
                su2 = _shift_up_mxu(dpb, um_ref[1], nxt, 2)
                u = up0_ref[:, cs].astype(F32)
                dfb_ref[:, cs] += _colsum(dp)
                dfw_ref[2:3, cs] += _colsum(dp * u)
                dfw_ref[1:2, cs] += _colsum(su1 * u)
                dfw_ref[0:1, cs] += _colsum(su2 * u)
                dup0 = fw_ref[2:3, cs] * dp + fw_ref[1:2, cs] * su1 + fw_ref[0:1, cs] * su2
                dup0_ref[:, cs] = dup0.astype(BF16)

    return pl.pallas_call(
        body, grid=(nt,), name="ffn_bwd_a",
        in_specs=[_rows(ts, d, nt), _rows(ts, wide, nt), _rows(ts, wide, nt), _RESIDENT,
                  _full((3, wide)), _full((2, ts, ts))],
        out_specs=[_rows(ts, wide, nt), _full((3, wide)), _full((1, wide))],
        out_shape=[_sds((s, wide), BF16), _sds((3, wide), F32), _sds((1, wide), F32)],
        scratch_shapes=[pltpu.VMEM((SUBLANES, wide), F32)],
        compiler_params=_params(("arbitrary",)),
    )(dy2, pre, up0, w_down, fw, up_mats)


def _ffn_bwd_b(dup0, x1, y, dx2, w_up4, g_pre2, sc_f, sh_f, gt_m, g_post_m, ts=512):
    s, d = x1.shape
    shard_cols = 2 * D_FF // N_CHIPS

    def body(dup_ref, x1_ref, y_ref, dx2_ref, wup_ref, g2_ref, sc_ref, sh_ref, gt_ref, gp_ref,
             dx1_ref, dy_ref, dsh_ref, dsc_ref, dg2_ref, dgt_ref, dgp_ref):
        i = pl.program_id(0)

        @pl.when(i == 0)
        def _():
            for ref in (dsh_ref, dsc_ref, dg2_ref, dgt_ref, dgp_ref):
                ref[...] = jnp.zeros_like(ref)

        for rs in _sub_tiles(ts):
            dh2 = jnp.zeros((SUB_ROWS, d), F32)
            for k in range(N_CHIPS):
                dh2 = dh2 + _dot_nt(dup_ref[rs, k * shard_cols:(k + 1) * shard_cols], wup_ref[k])
            x1v = x1_ref[rs, :]
            r2 = _msq_rsqrt(x1v)
            xn = x1v * r2
            hn = xn * g2_ref[...]
            dsh_ref[...] += _colsum(dh2)
            dsc_ref[...] += _colsum(dh2 * hn)
            dhn = dh2 * (1.0 + sc_ref[...])
            dg2_ref[...] += _colsum(dhn * xn)
            dx1 = dx2_ref[rs, :] + _rms_bwd(dhn * g2_ref[...], xn, r2)
            dx1_ref[rs, :] = dx1
            yv = y_ref[rs, :]
            ry = _msq_rsqrt(yv)
            yn = yv * ry
            dgt_ref[...] += _colsum(dx1 * (yn * gp_ref[...]))
            dyng = dx1 * gt_ref[...]
            dgp_ref[...] += _colsum(dyng * yn)
            dy_ref[rs, :] = _rms_bwd(dyng * gp_ref[...], yn, ry).astype(BF16)

    vec = _full((1, d))
    return pl.pallas_call(
        body, grid=(s // ts,), name="ffn_bwd_b",
        in_specs=[_rows(ts, 2 * D_FF), _rows(ts, d), _rows(ts, d), _rows(ts, d), _RESIDENT,
                  vec, vec, vec, vec, vec],
        out_specs=[_rows(ts, d), _rows(ts, d), vec, vec, vec, vec, vec],
        out_shape=[_sds((s, d), F32), _sds((s, d), BF16)] + [_sds((1, d), F32)] * 5,
        compiler_params=_params(("arbitrary",)),
    )(dup0, x1, y, dx2, w_up4, g_pre2, sc_f, sh_f, gt_m, g_post_m)


def _seqmix_bwd(z, hst, dy, w_out, seq_params, ws_t, glo, ggo, ts=256):
    s = z.shape[0]
    nt = s // ts
    small_shapes = [(4, 512), (1, 512), (512, 512), (512, 512), (1, 512), (1, 512), (1, 512),
                    (1, 512), (1, 512), (4, 128, 128), (128, 4), (1, 512), (1, 512)]

    def body(z_ref, zprev_ref, hst_ref, hprev_ref, dy_ref, wout_ref, *rest):
        p = rest[:11]
        wst_ref, glo_ref, ggo_ref = rest[11:14]
        dz_ref = rest[14]
        (dcw_ref, dcb_ref, dwr_ref, dwi_ref, dbr_ref, dbi_ref, dspa_ref, dng_ref, dnb_ref, dws_ref, dbs_ref,
         dglo_ref, dggo_ref) = rest[15:28]
        gcarry, anext, dxcnext, sp_scr, dv_scr = rest[28:]
        i = pl.program_id(0)

        @pl.when(i == 0)
        def _():
            for ref in rest[15:28]:
                ref[...] = jnp.zeros_like(ref)
            gcarry[...] = jnp.zeros_like(gcarry)
            anext[...] = jnp.ones_like(anext)
            dxcnext[...] = jnp.zeros_like(dxcnext)

        first_tile = i == nt - 1
        f = _seq_recompute(z_ref, zprev_ref, first_tile, p)
        xc, r, ig, a, mult, lx = f["xc"], f["r"], f["ig"], f["a"], f["mult"], f["lx"]
        h = hst_ref[...]
        hprev = _shift_down(h, jnp.where(first_tile, 0.0, hprev_ref[...]), 1)
        gl, dgl = _gelu_and_grad(f["lg"])
        y_l = h * gl
        gm = _gmlp_fwd(f["gu"], f["gv"], p[7], p[8], p[9], p[10], sp_scr)
        y_g = gm["y_g"]

        dycat = _dot_nt(dy_ref[...], wout_ref[...])
        rl = _msq_rsqrt(y_l)
        yln = y_l * rl
        dyl = dycat[:, 0:512]
        dglo_ref[...] += _colsum(dyl * yln)
        dy_l = _rms_bwd(dyl * glo_ref[...], yln, rl)
        rg = _msq_rsqrt(y_g)
        ygn = y_g * rg
        dyg = dycat[:, 512:1024]
        dggo_ref[...] += _colsum(dyg * ygn)
        dy_g = _rms_bwd(dyg * ggo_ref[...], ygn, rg)

        dz_ref[:, 512:1024] = (dy_l * h * dgl).astype(BF16)
        a_up = _shift_up(a, anext[...], 1)
        acum, gloc = _scan_bwd(a_up, dy_l * gl)
        gg = gloc + acum * gcarry[...]
        gcarry[...] = gg[0:1, :]
        anext[...] = a[0:SUBLANES, :]
        da = gg * hprev
        t1 = gg * mult
        di = t1 * xc
        dxc = t1 * ig
        dmult = gg * ig * xc
        dla = da * a - dmult * (a * a / mult)
        spa = f["spa"]
        dspa_ref[...] += _colsum(dla * r) * (-LRU_C)
        dpr = dla * ((-LRU_C) * spa) * r * (1.0 - r)
        dpi = di * ig * (1.0 - ig)
        dbr_ref[...] += _colsum(dpr)
        dbi_ref[...] += _colsum(dpi)
        dprb = dpr.astype(BF16)
        dpib = dpi.astype(BF16)
        dwr_ref[...] += _dot_tn(f["xcb"], dprb)
        dwi_ref[...] += _dot_tn(f["xcb"], dpib)
        dxc = dxc + _dot_nt(dprb, p[2][...]) + _dot_nt(dpib, p[3][...])
        dcb_ref[...] += _colsum(dxc)
        dcw_ref[3:4, :] += _colsum(dxc * lx)
        dcw_ref[2:3, :] += _colsum(dxc * f["s1"])
        dcw_ref[1:2, :] += _colsum(dxc * f["s2"])
        dcw_ref[0:1, :] += _colsum(dxc * f["s3"])
        nxt = dxcnext[...]
        dxcnext[...] = dxc[0:SUBLANES, :]
        cw_ref = p[0]
        dlx = (cw_ref[3:4, :] * dxc + cw_ref[2:3, :] * _shift_up(dxc, nxt, 1)
               + cw_ref[1:2, :] * _shift_up(dxc, nxt, 2) + cw_ref[0:1, :] * _shift_up(dxc, nxt, 3))
        dz_ref[:, 0:512] = dlx.astype(BF16)

        dz_ref[:, 1024:1536] = (dy_g * gm["spb"] * gm["du"]).astype(BF16)
        dsp = dy_g * gm["u"]
        vb = gm["vb"]
        for n in range(ts // GMLP_BLOCK):
            rs = slice(n * GMLP_BLOCK, (n + 1) * GMLP_BLOCK)
            for g in range(GMLP_GROUPS):
                cs = slice(g * 128, (g + 1) * 128)
                dbs_ref[:, g:g + 1] += jnp.sum(dsp[rs, cs], axis=1, keepdims=True)
                blk = dsp[rs, cs].astype(BF16)
                dws_ref[g] += _dot_nt(blk, vb[rs, cs])
                dv_scr[rs, cs] = _dot(wst_ref[g], blk)
        dv = dv_scr[...]
        vhat = gm["vhat"]
        dng_ref[...] += _colsum(dv * vhat)
        dnb_ref[...] += _colsum(dv)
        dvh = dv * p[7][...]
        dvg = gm["rstd"] * (dvh - jnp.mean(dvh, axis=-1, keepdims=True)
                            - vhat * jnp.mean(dvh * vhat, axis=-1, keepdims=True))
        dz_ref[:, 1536:2048] = (dvg * gm["dvg"]).astype(BF16)

        @pl.when(i == nt - 1)
        def _():
            pos = lax.broadcasted_iota(jnp.int32, (GMLP_BLOCK, GMLP_BLOCK), 0) // CHUNK
            src = lax.broadcasted_iota(jnp.int32, (GMLP_BLOCK, GMLP_BLOCK), 1) // CHUNK
            for g in range(GMLP_GROUPS):
                dws_ref[g] = jnp.where(src <= pos, dws_ref[g], 0.0)
            dspa_ref[...] = dspa_ref[...] * (-_sigmoid(-p[6][...]))

    in_specs = (_seq_specs(ts, nt, True)
                + [_rows(ts, 512, nt), _halo_prev(ts, 512, SUBLANES, nt), _rows(ts, 1024, nt), _full((1024, 1024))]
                + _seq_param_specs() + [_full((4, 128, 128)), _full((1, 512)), _full((1, 512))])
    return pl.pallas_call(
        body, grid=(nt,), name="seqmix_bwd",
        in_specs=in_specs,
        out_specs=[_rows(ts, 2048, nt)] + [_full(sh) for sh in small_shapes],
        out_shape=[_sds((s, 2048), BF16)] + [_sds(sh, F32) for sh in small_shapes],
        scratch_shapes=[pltpu.VMEM((1, 512), F32), pltpu.VMEM((SUBLANES, 512), F32),
                        pltpu.VMEM((SUBLANES, 512), F32), pltpu.VMEM((ts, 512), F32), pltpu.VMEM((ts, 512), F32)],
        compiler_params=_params(("arbitrary",)),
    )(z, z, hst, hst, dy, w_out, *seq_params, ws_t, glo, ggo)


def _mix_in_bwd(x, dz, dx1, w_in4, g, sc, ts=512):
    s, d = x.shape

    def body(x_ref, dz_ref, dx1_ref, w_ref, g_ref, sc_ref, gx_ref, dsh_ref, dsc_ref, dg_ref):
        i = pl.program_id(0)

        @pl.when(i == 0)
        def _():
            for ref in (dsh_ref, dsc_ref, dg_ref):
                ref[...] = jnp.zeros_like(ref)

        for rs in _sub_tiles(ts):
            dh = jnp.zeros((SUB_ROWS, d), F32)
            for k in range(N_CHIPS):
                dh = dh + _dot_nt(dz_ref[rs, k * 512:(k + 1) * 512], w_ref[k])
            xv = x_ref[rs, :]
            r = _msq_rsqrt(xv)
            xn = xv * r
            dsh_ref[...] += _colsum(dh)
            dsc_ref[...] += _colsum(dh * (xn * g_ref[...]))
            dhn = dh * (1.0 + sc_ref[...])
            dg_ref[...] += _colsum(dhn * xn)
            gx_ref[rs, :] = dx1_ref[rs, :] + _rms_bwd(dhn * g_ref[...], xn, r)

    vec = _full((1, d))
    return pl.pallas_call(
        body, grid=(s // ts,), name="mix_in_bwd",
        in_specs=[_rows(ts, d), _rows(ts, 2048), _rows(ts, d), _full(w_in4.shape), vec, vec],
        out_specs=[_rows(ts, d), vec, vec, vec],
        out_shape=[_sds((s, d), F32)] + [_sds((1, d), F32)] * 3,
        compiler_params=_params(("arbitrary",)),
    )(x, dz, dx1, w_in4, g, sc)


def _wgrad(a, b, n_chunks, name, chunk_major, ts=2048):
    s, m = a.shape
    n = b.shape[1]
    nc = n // n_chunks
    nt = s // ts

    def body(a_ref, b_ref, o_ref, acc):
        i = pl.program_id(1)

        @pl.when(i == 0)
        def _():
            acc[...] = jnp.zeros_like(acc)

        acc[...] += _dot_tn(a_ref[...], b_ref[...])

        @pl.when(i == nt - 1)
        def _():
            if chunk_major:
                o_ref[0] = acc[...].astype(BF16)
            else:
                o_ref[...] = acc[...].astype(BF16)

    if chunk_major:
        out_spec, out_shape = pl.BlockSpec((1, m, nc), lambda c, i: (c, 0, 0)), _sds((n_chunks, m, nc), BF16)
    else:
        out_spec, out_shape = pl.BlockSpec((m, nc), lambda c, i: (0, c)), _sds((m, n), BF16)
    return pl.pallas_call(
        body, grid=(n_chunks, nt), name=name,
        in_specs=[pl.BlockSpec((ts, m), lambda c, i: (i, 0)), pl.BlockSpec((ts, nc), lambda c, i: (i, c))],
        out_specs=out_spec,
        out_shape=out_shape,
        scratch_shapes=[pltpu.VMEM((m, nc), F32)],
        compiler_params=_params(("parallel", "arbitrary")),
    )(a, b)


def _block_diag(w):
    heads, hd, _ = w.shape
    eye = jnp.eye(heads, dtype=w.dtype)
    return (eye[:, None, :, None] * w[:, :, None, :]).reshape(heads * hd, heads * hd)


def _diag_blocks(m):
    hd = LRU_WIDTH // LRU_HEADS
    m4 = m.reshape(LRU_HEADS, hd, LRU_HEADS, hd)
    return jnp.stack([m4[k, :, k, :] for k in range(LRU_HEADS)])


def _seq_params(small):
    row = lambda v: v.reshape(1, -1)
    pos = jnp.arange(GMLP_BLOCK)
    mask = (pos[None, :] // CHUNK) <= (pos[:, None] // CHUNK)
    ws = jnp.where(mask[None], small["w_spatial"], 0.0)
    seq_params = (small["conv_w"], row(small["conv_b"]),
                  _block_diag(small["w_rgate"]).astype(BF16), _block_diag(small["w_igate"]).astype(BF16),
                  row(small["b_rgate"]), row(small["b_igate"]), row(small["lru_a"]),
                  row(small["v_norm_g"]), row(small["v_norm_b"]), ws.astype(BF16), small["b_spatial"].T)
    return seq_params, jnp.swapaxes(ws, 1, 2).astype(BF16)


_ANY = pl.BlockSpec(memory_space=pl.ANY)
_CHIP_FLIPS = ((1, 0), (0, 1), (1, 1))


def _position():
    return lax.axis_index("x"), lax.axis_index("y"), lax.axis_index("c")


def _flip(v, f):
    return 1 - v if f else v


def _remote(src, dst, send_sem, recv_sem, peer):
    return pltpu.make_async_remote_copy(src_ref=src, dst_ref=dst, send_sem=send_sem, recv_sem=recv_sem,
                                        device_id=peer, device_id_type=MESH)


def _allgather8(block, name, reduce):
    r, n = block.shape

    def body(x_ref, out_ref, *scratch):
        if reduce:
            gath, send_sems, recv_sems, loc_sem = scratch
        else:
            gath = out_ref
            send_sems, recv_sems, loc_sem = scratch
        x, y, c = _position()
        me = 4 * x + 2 * y + c
        loc = pltpu.make_async_copy(x_ref, gath.at[me], loc_sem)
        loc.start()
        peers = []
        for k in range(1, N_DEV):
            px, py, pc = _flip(x, k & 4), _flip(y, k & 2), _flip(c, k & 1)
            peers.append((px, py, pc))
            _remote(x_ref, gath.at[me], send_sems.at[k - 1], recv_sems.at[k - 1], (px, py, pc)).start()
        for k, (px, py, pc) in enumerate(peers):
            src = 4 * px + 2 * py + pc
            _remote(x_ref, gath.at[src], send_sems.at[k], recv_sems.at[k], (px, py, pc)).wait_recv()
        for k, peer in enumerate(peers):
            _remote(x_ref, gath.at[me], send_sems.at[k], recv_sems.at[k], peer).wait_send()
        loc.wait()
        if reduce:
            acc = gath[0]
            for k in range(1, N_DEV):
                acc = acc + gath[k]
            out_ref[...] = acc

    sems = [pltpu.SemaphoreType.DMA((N_DEV - 1,)), pltpu.SemaphoreType.DMA((N_DEV - 1,)), pltpu.SemaphoreType.DMA]
    if reduce:
        out_shape = _sds((r, n), F32)
        scratch = [pltpu.VMEM((N_DEV, r, n), F32)] + sems
    else:
        out_shape = _sds((N_DEV, r, n), F32)
        scratch = sems
    return pl.pallas_call(
        body, name=name, out_shape=out_shape,
        in_specs=[pl.BlockSpec(memory_space=pltpu.VMEM)], out_specs=pl.BlockSpec(memory_space=pltpu.VMEM),
        scratch_shapes=scratch,
        compiler_params=pltpu.CompilerParams(vmem_limit_bytes=VMEM_LIMIT_BYTES),
    )(block)


def _half(ref, c, rows):
    hr = rows // 2
    return ref.at[pl.ds(pl.multiple_of(c * hr, BF16_SUBLANES), hr), :]


def _gather_weights(shards):
    na = len(shards)

    def body(*refs):
        ins, outs = refs[:na], refs[na:2 * na]
        ici_send, ici_recv, d2d_send, d2d_recv, loc_sem = refs[2 * na:]
        x, y, c = _position()
        chip = 2 * x + y
        sibling = (x, y, 1 - c)
        local = []
        for a in range(na):
            local.append(pltpu.make_async_copy(ins[a], outs[a].at[chip], loc_sem.at[a]))
            local[-1].start()
        sends = []
        for a in range(na):
            rows = shards[a].shape[0]
            for j, (fx, fy) in enumerate(_CHIP_FLIPS):
                peer = (_flip(x, fx), _flip(y, fy), c)
                sends.append(_remote(_half(ins[a], c, rows), _half(outs[a].at[chip], c, rows),
                                     ici_send.at[a * 3 + j], ici_recv.at[a * 3 + j], peer))
                sends[-1].start()
        for a in range(na):
            rows = shards[a].shape[0]
            for j, (fx, fy) in enumerate(_CHIP_FLIPS):
                src_chip = 2 * _flip(x, fx) + _flip(y, fy)
                landed = _half(outs[a].at[src_chip], c, rows)
                _remote(landed, landed, ici_send.at[a * 3 + j], ici_recv.at[a * 3 + j], sibling).wait_recv()
                sends.append(_remote(landed, landed, d2d_send.at[a * 3 + j], d2d_recv.at[a * 3 + j], sibling))
                sends[-1].start()
        for a in range(na):
            rows = shards[a].shape[0]
            for j, (fx, fy) in enumerate(_CHIP_FLIPS):
                src_chip = 2 * _flip(x, fx) + _flip(y, fy)
                other = _half(outs[a].at[src_chip], 1 - c, rows)
                _remote(other, other, d2d_send.at[a * 3 + j], d2d_recv.at[a * 3 + j], sibling).wait_recv()
        for cp in sends:
            cp.wait_send()
        for cp in local:
            cp.wait()

    return pl.pallas_call(
        body, name="gather_weights",
        out_shape=[_sds((N_CHIPS,) + w.shape, w.dtype) for w in shards],
        in_specs=[_ANY] * na, out_specs=[_ANY] * na,
        scratch_shapes=[pltpu.SemaphoreType.DMA((3 * na,))] * 4 + [pltpu.SemaphoreType.DMA((na,))],
    )(*shards)


def _swap_halves(parts, name):
    na = len(parts)

    def body(*refs):
        ins, outs = refs[:na], refs[na:2 * na]
        send_sems, recv_sems = refs[2 * na:]
        x, y, c = _position()
        sibling = (x, y, 1 - c)
        cps = []
        for a in range(na):
            hr = parts[a].shape[1] // 2
            src = ins[a].at[:, pl.ds(pl.multiple_of((1 - c) * hr, BF16_SUBLANES), hr), :]
            cps.append(_remote(src, outs[a], send_sems.at[a], recv_sems.at[a], sibling))
            cps[-1].start()
        for cp in cps:
            cp.wait()

    return pl.pallas_call(
        body, name=name,
        out_shape=[_sds((N_CHIPS, p.shape[1] // 2, p.shape[2]), p.dtype) for p in parts],
        in_specs=[_ANY] * na, out_specs=[_ANY] * na,
        scratch_shapes=[pltpu.SemaphoreType.DMA((na,))] * 2,
    )(*parts)


def _chip_sum(part, recv, pos_arr, name):
    _, rows, cols = part.shape
    hr = rows // 2

    def body(pos_ref, p_ref, r_ref, o_ref, g_ref):
        total = (p_ref[...].astype(F32) + r_ref[...].astype(F32)).astype(BF16)
        o_ref[...] = total

        @pl.when(pl.program_id(0) == pos_ref[1])
        def _():
            g_ref[0] = total

    grid_spec = pltpu.PrefetchScalarGridSpec(
        num_scalar_prefetch=1, grid=(N_CHIPS,),
        in_specs=[pl.BlockSpec((1, hr, cols), lambda k, pos: (k, pos[0], 0)),
                  pl.BlockSpec((1, hr, cols), lambda k, pos: (k, 0, 0))],
        out_specs=[pl.BlockSpec((1, hr, cols), lambda k, pos: (k, 0, 0)),
                   pl.BlockSpec((1, 1, hr, cols), lambda k, pos: (0, pos[1], 0, 0))])
    return pl.pallas_call(
        body, name=name, grid_spec=grid_spec,
        out_shape=[_sds((N_CHIPS, hr, cols), BF16), _sds((2, N_CHIPS, hr, cols), BF16)],
        compiler_params=_params(("arbitrary",)),
    )(pos_arr, part, recv)


def _exchange_chips(sums):
    na = len(sums)

    def body(*refs):
        ins, outs = refs[:na], refs[na:2 * na]
        send_sems, recv_sems, loc_sem = refs[2 * na:]
        x, y, c = _position()
        chip = 2 * x + y
        local = []
        for a in range(na):
            local.append(pltpu.make_async_copy(ins[a].at[chip], outs[a].at[chip], loc_sem.at[a]))
            local[-1].start()
        cps = []
        for a in range(na):
            for j, (fx, fy) in enumerate(_CHIP_FLIPS):
                px, py = _flip(x, fx), _flip(y, fy)
                cps.append(_remote(ins[a].at[2 * px + py], outs[a].at[chip],
                                   send_sems.at[a * 3 + j], recv_sems.at[a * 3 + j], (px, py, c)))
                cps[-1].start()
        for a in range(na):
            for j, (fx, fy) in enumerate(_CHIP_FLIPS):
                src_chip = 2 * _flip(x, fx) + _flip(y, fy)
                landed = outs[a].at[src_chip]
                _remote(landed, landed, send_sems.at[a * 3 + j], recv_sems.at[a * 3 + j], (x, y, c)).wait_recv()
        for cp in cps:
            cp.wait_send()
        for cp in local:
            cp.wait()

    return pl.pallas_call(
        body, name="exchange_chips",
        out_shape=[_sds(s.shape, s.dtype) for s in sums],
        in_specs=[_ANY] * na, out_specs=[_ANY] * na,
        scratch_shapes=[pltpu.SemaphoreType.DMA((3 * na,))] * 2 + [pltpu.SemaphoreType.DMA((na,))],
    )(*sums)


def _sum_chips(gath, name, tr=128):
    _, hr, cols = gath.shape
    tr = min(tr, hr)

    def body(g_ref, o_ref):
        acc = g_ref[0].astype(F32)
        for k in range(1, N_CHIPS):
            acc = acc + g_ref[k].astype(F32)
        o_ref[...] = acc

    return pl.pallas_call(
        body, name=name, grid=(hr // tr,),
        in_specs=[pl.BlockSpec((N_CHIPS, tr, cols), lambda i: (0, i, 0))],
        out_specs=pl.BlockSpec((tr, cols), lambda i: (i, 0)),
        out_shape=_sds((hr, cols), F32),
        compiler_params=_params(("parallel",)),
    )(gath)


def _join_halves(halves):
    na = len(halves)

    def body(*refs):
        ins, outs = refs[:na], refs[na:2 * na]
        send_sems, recv_sems, loc_sem = refs[2 * na:]
        x, y, c = _position()
        sibling = (x, y, 1 - c)
        cps, local = [], []
        for a in range(na):
            rows = 2 * halves[a].shape[0]
            mine = _half(outs[a], c, rows)
            local.append(pltpu.make_async_copy(ins[a], mine, loc_sem.at[a]))
            local[-1].start()
            cps.append(_remote(ins[a], mine, send_sems.at[a], recv_sems.at[a], sibling))
            cps[-1].start()
        for a in range(na):
            rows = 2 * halves[a].shape[0]
            other = _half(outs[a], 1 - c, rows)
            _remote(ins[a], other, send_sems.at[a], recv_sems.at[a], sibling).wait_recv()
        for cp in cps:
            cp.wait_send()
        for cp in local:
            cp.wait()

    return pl.pallas_call(
        body, name="join_halves",
        out_shape=[_sds((2 * h.shape[0], h.shape[1]), h.dtype) for h in halves],
        in_specs=[_ANY] * na, out_specs=[_ANY] * na,
        scratch_shapes=[pltpu.SemaphoreType.DMA((na,))] * 3,
    )(*halves)


_HBM = pl.BlockSpec(memory_space=pltpu.HBM)
_SEM = pl.BlockSpec(memory_space=pltpu.SEMAPHORE)
_EFFECT = pltpu.SideEffectType.DATAFLOW_SIDE_EFFECTING


def _in_hbm(a):
    return pltpu.with_memory_space_constraint(a, pltpu.HBM)


def _split_start(srcs, lands, plan, n_copies, after, name):
    ns, nl = len(srcs), len(lands)
    bufs = list(srcs) + list(lands)

    def body(*refs):
        send_sems, recv_sems = refs[ns + nl + 1], refs[ns + nl + 2]
        token = refs[-1]
        for k, (src, dst, peer) in enumerate(plan(refs[:ns], refs[ns:ns + nl])):
            _remote(src, dst, send_sems.at[k], recv_sems.at[k], peer).start()
        token[...] = jnp.zeros_like(token)

    out = pl.pallas_call(
        body, name=name,
        out_shape=(pltpu.SemaphoreType.DMA((n_copies,)), pltpu.SemaphoreType.DMA((n_copies,)),
                   *[pltpu.HBM(b.shape, b.dtype) for b in bufs], _sds((SUBLANES, 128), F32)),
        in_specs=[_HBM] * (ns + nl) + [_ANY],
        out_specs=(_SEM, _SEM, *[_HBM] * (ns + nl), pl.BlockSpec(memory_space=pltpu.VMEM)),
        input_output_aliases={i: 2 + i for i in range(ns + nl)},
        compiler_params=pltpu.CompilerParams(has_side_effects=_EFFECT),
    )(*[_in_hbm(b) for b in bufs], after)
    return out[0], out[1], list(out[2:2 + ns]), list(out[2 + ns:2 + ns + nl]), out[-1]


def _split_wait(send_sems, recv_sems, srcs, lands, plan, after, name):
    ns, nl = len(srcs), len(lands)
    bufs = list(srcs) + list(lands)

    def body(*refs):
        send_ref, recv_ref = refs[ns + nl], refs[ns + nl + 1]
        me = _position()
        for k, src, dst in plan(refs[:ns], refs[ns:ns + nl]):
            cp = _remote(src, dst, send_ref.at[k], recv_ref.at[k], me)
            cp.wait_send()
            cp.wait_recv()

    out = pl.pallas_call(
        body, name=name,
        out_shape=[pltpu.HBM(b.shape, b.dtype) for b in bufs],
        in_specs=[_HBM] * (ns + nl) + [_SEM, _SEM, _ANY],
        out_specs=[_HBM] * (ns + nl),
        input_output_aliases={i: i for i in range(ns + nl)},
        compiler_params=pltpu.CompilerParams(has_side_effects=_EFFECT),
    )(*bufs, send_sems, recv_sems, after)
    return list(out[:ns]), list(out[ns:])


def _gather_plan(rows_of):
    def start(src_refs, land_refs):
        x, y, c = _position()
        chip = 2 * x + y
        out = []
        for a, rows in enumerate(rows_of):
            mine = _half(land_refs[a].at[chip], c, rows)
            out.extend((mine, mine, (_flip(x, fx), _flip(y, fy), c)) for fx, fy in _CHIP_FLIPS)
        return out

    def wait(src_refs, land_refs):
        x, y, c = _position()
        chip = 2 * x + y
        out = []
        for a, rows in enumerate(rows_of):
            for j, (fx, fy) in enumerate(_CHIP_FLIPS):
                src_chip = 2 * _flip(x, fx) + _flip(y, fy)
                out.append((3 * a + j, _half(land_refs[a].at[chip], c, rows),
                            _half(land_refs[a].at[src_chip], c, rows)))
        return out

    return start, wait


def _exchange_plan(n_arrays):
    def start(src_refs, land_refs):
        x, y, c = _position()
        chip = 2 * x + y
        out = []
        for a in range(n_arrays):
            for fx, fy in _CHIP_FLIPS:
                px, py = _flip(x, fx), _flip(y, fy)
                out.append((src_refs[a].at[2 * px + py], land_refs[a].at[0, chip], (px, py, c)))
        return out

    def wait(src_refs, land_refs):
        x, y, c = _position()
        out = []
        for a in range(n_arrays):
            for j, (fx, fy) in enumerate(_CHIP_FLIPS):
                src_chip = 2 * _flip(x, fx) + _flip(y, fy)
                out.append((3 * a + j, src_refs[a].at[src_chip], land_refs[a].at[0, src_chip]))
        return out

    return start, wait


def _forward_to_sibling(lands, name):
    na = len(lands)

    def body(*refs):
        land_refs = refs[na:2 * na]
        send_sems, recv_sems = refs[2 * na:]
        x, y, c = _position()
        sibling = (x, y, 1 - c)
        sends = []
        for a in range(na):
            rows = lands[a].shape[1]
            for j, (fx, fy) in enumerate(_CHIP_FLIPS):
                landed = _half(land_refs[a].at[2 * _flip(x, fx) + _flip(y, fy)], c, rows)
                sends.append(_remote(landed, landed, send_sems.at[3 * a + j], recv_sems.at[3 * a + j], sibling))
                sends[-1].start()
        for a in range(na):
            rows = lands[a].shape[1]
            for j, (fx, fy) in enumerate(_CHIP_FLIPS):
                other = _half(land_refs[a].at[2 * _flip(x, fx) + _flip(y, fy)], 1 - c, rows)
                _remote(other, other, send_sems.at[3 * a + j], recv_sems.at[3 * a + j], sibling).wait_recv()
        for cp in sends:
            cp.wait_send()

    return pl.pallas_call(
        body, name=name,
        out_shape=[_sds(l.shape, l.dtype) for l in lands],
        in_specs=[_ANY] * na, out_specs=[_ANY] * na,
        input_output_aliases={a: a for a in range(na)},
        scratch_shapes=[pltpu.SemaphoreType.DMA((3 * na,))] * 2,
    )(*lands)


def _swap_gathered(gath, name):
    na = len(gath)

    def body(*refs):
        gath_refs = refs[na:2 * na]
        send_sems, recv_sems = refs[2 * na:]
        x, y, c = _position()
        cps = [_remote(gath_refs[a].at[0], gath_refs[a].at[1], send_sems.at[a], recv_sems.at[a], (x, y, 1 - c))
               for a in range(na)]
        for cp in cps:
            cp.start()
        for cp in cps:
            cp.wait()

    return pl.pallas_call(
        body, name=name,
        out_shape=[_sds(g.shape, g.dtype) for g in gath],
        in_specs=[_ANY] * na, out_specs=[_ANY] * na,
        input_output_aliases={a: a for a in range(na)},
        scratch_shapes=[pltpu.SemaphoreType.DMA((na,))] * 2,
    )(*gath)


def _adam_gathered(w, gath, m, v, c_arr, name, tr=128):
    rows, cols = w.shape
    hr = rows // 2
    per = hr // tr

    def body(c_ref, w_ref, g_ref, m_ref, v_ref, go_ref, d_ref, nm_ref, nv_ref):
        g = g_ref[0, 0].astype(F32)
        for k in range(1, N_CHIPS):
            g = g + g_ref[0, k].astype(F32)
        go_ref[...] = g
        d_ref[...], nm_ref[...], nv_ref[...] = _adam_math(w_ref[...], g, m_ref[...], v_ref[...])

    def rows_of(h, i, c_ref):
        c = c_ref[0]
        return ((c + h - 2 * c * h) * per + i, 0)

    blk = pl.BlockSpec((tr, cols), rows_of)
    grid_spec = pltpu.PrefetchScalarGridSpec(
        num_scalar_prefetch=1, grid=(2, per),
        in_specs=[blk, pl.BlockSpec((1, N_CHIPS, tr, cols), lambda h, i, c_ref: (h, 0, i, 0)), blk, blk],
        out_specs=[blk] * 4)
    return pl.pallas_call(
        body, name=name, grid_spec=grid_spec, out_shape=[_sds(w.shape, F32)] * 4,
        compiler_params=_params(("arbitrary", "arbitrary")),
    )(c_arr, w, gath, m, v)


def _allreduce_small(block, name):
    r, n = block.shape
    hr = r // 2

    def body(x_ref, out_ref, sib, chipsum, gath, d2d_send, d2d_recv, ici_send, ici_recv):
        x, y, c = _position()
        chip = 2 * x + y
        sibling = (x, y, 1 - c)
        first = _remote(x_ref, sib, d2d_send.at[0], d2d_recv.at[0], sibling)
        first.start()
        first.wait()
        chipsum[...] = x_ref[...] + sib[...]
        mine = pl.ds(pl.multiple_of(c * hr, SUBLANES), hr)
        theirs = pl.ds(pl.multiple_of((1 - c) * hr, SUBLANES), hr)
        sends = []
        for j, (fx, fy) in enumerate(_CHIP_FLIPS):
            sends.append(_remote(chipsum.at[mine, :], gath.at[chip], ici_send.at[j], ici_recv.at[j],
                                 (_flip(x, fx), _flip(y, fy), c)))
            sends[-1].start()
        gath[chip] = chipsum[mine, :]
        for j, (fx, fy) in enumerate(_CHIP_FLIPS):
            landed = gath.at[2 * _flip(x, fx) + _flip(y, fy)]
            _remote(landed, landed, ici_send.at[j], ici_recv.at[j], sibling).wait_recv()
        for cp in sends:
            cp.wait_send()
        total = gath[0]
        for k in range(1, N_CHIPS):
            total = total + gath[k]
        out_ref[mine, :] = total
        last = _remote(out_ref.at[mine, :], out_ref.at[mine, :], d2d_send.at[1], d2d_recv.at[1], sibling)
        last.start()
        _remote(out_ref.at[theirs, :], out_ref.at[theirs, :], d2d_send.at[1], d2d_recv.at[1], sibling).wait_recv()
        last.wait_send()

    vmem = pl.BlockSpec(memory_space=pltpu.VMEM)
    return pl.pallas_call(
        body, name=name, out_shape=_sds((r, n), F32), in_specs=[vmem], out_specs=vmem,
        scratch_shapes=[pltpu.VMEM((r, n), F32), pltpu.VMEM((r, n), F32), pltpu.VMEM((N_CHIPS, hr, n), F32),
                        pltpu.SemaphoreType.DMA((2,)), pltpu.SemaphoreType.DMA((2,)),
                        pltpu.SemaphoreType.DMA((3,)), pltpu.SemaphoreType.DMA((3,))],
        compiler_params=pltpu.CompilerParams(vmem_limit_bytes=VMEM_LIMIT_BYTES),
    )(block)


def _cast_place(shards, chip_arr):
    na = len(shards)
    steps = 4

    def body(chip_ref, *refs):
        for a in range(na):
            refs[na + a][0] = refs[a][...].astype(BF16)

    grid_spec = pltpu.PrefetchScalarGridSpec(
        num_scalar_prefetch=1, grid=(steps,),
        in_specs=[pl.BlockSpec((s.shape[0] // steps, s.shape[1]), lambda i, ch: (i, 0)) for s in shards],
        out_specs=[pl.BlockSpec((1, s.shape[0] // steps, s.shape[1]), lambda i, ch: (ch[0], i, 0)) for s in shards])
    return pl.pallas_call(
        body, name="cast_place", grid_spec=grid_spec,
        out_shape=[_sds((N_CHIPS,) + s.shape, BF16) for s in shards],
        compiler_params=_params(("arbitrary",)),
    )(chip_arr, *shards)


def _silu(v):
    return v * _sigmoid(v)


def _ada_fwd(c8, w_ada):
    def body(c_ref, w_ref, o_ref):
        o_ref[...] = jnp.dot(_silu(c_ref[...]), w_ref[...], preferred_element_type=F32,
                             precision=lax.Precision.HIGHEST)

    return pl.pallas_call(
        body, name="ada_fwd", out_shape=_sds((N_DEV, w_ada.shape[1]), F32),
        compiler_params=pltpu.CompilerParams(vmem_limit_bytes=VMEM_LIMIT_BYTES),
    )(c8, w_ada)


def _mod_select(parts, b_ada, me_arr, after):
    cols = parts.shape[2]

    def body(me_ref, p_ref, b_ref, after_ref, o_ref):
        me = me_ref[0]
        for k in range(N_CHIPS):
            cs = slice(k * cols, (k + 1) * cols)
            o_ref[:, cs] = p_ref[2 * k, pl.ds(me, 1), :] + b_ref[:, cs]

    grid_spec = pltpu.PrefetchScalarGridSpec(
        num_scalar_prefetch=1, grid=(1,),
        in_specs=[pl.BlockSpec(parts.shape, lambda i, m: (0, 0, 0)), pl.BlockSpec(b_ada.shape, lambda i, m: (0, 0)),
                  _ANY],
        out_specs=pl.BlockSpec(b_ada.shape, lambda i, m: (0, 0)))
    return pl.pallas_call(body, name="mod_select", grid_spec=grid_spec, out_shape=_sds(b_ada.shape, F32))(
        me_arr, parts, b_ada, after)


def _ada_bwd(c8, dmod8, chip_arr):
    d = c8.shape[1]
    cols = dmod8.shape[1] // N_CHIPS

    def body(chip_ref, c_ref, dm_ref, dmall_ref, gw_ref, gb_ref):
        gw_ref[...] = lax.dot_general(_silu(c_ref[...]), dm_ref[...], (((0,), (0,)), ((), ())),
                                      preferred_element_type=F32, precision=lax.Precision.HIGHEST)
        acc = dmall_ref[0:1, :]
        for k in range(1, N_DEV):
            acc = acc + dmall_ref[k:k + 1, :]
        gb_ref[...] = acc

    grid_spec = pltpu.PrefetchScalarGridSpec(
        num_scalar_prefetch=1, grid=(1,),
        in_specs=[pl.BlockSpec(c8.shape, lambda i, ch: (0, 0)),
                  pl.BlockSpec((N_DEV, cols), lambda i, ch: (0, ch[0])),
                  pl.BlockSpec(dmod8.shape, lambda i, ch: (0, 0))],
        out_specs=[pl.BlockSpec((d, cols), lambda i, ch: (0, 0)), pl.BlockSpec((1, dmod8.shape[1]), lambda i, ch: (0, 0))])
    return pl.pallas_call(
        body, name="ada_bwd", grid_spec=grid_spec,
        out_shape=[_sds((d, cols), F32), _sds((1, dmod8.shape[1]), F32)],
        compiler_params=_params(("arbitrary",)),
    )(chip_arr, c8, dmod8, dmod8)


def _adam_math(w, g, m, v):
    m = ADAM_B1 * m + (1.0 - ADAM_B1) * g
    v = ADAM_B2 * v + (1.0 - ADAM_B2) * (g * g)
    m_hat = m / (1.0 - ADAM_B1 ** ADAM_STEP)
    v_hat = v / (1.0 - ADAM_B2 ** ADAM_STEP)
    delta = -ADAM_LR * (m_hat / (jnp.sqrt(v_hat) + ADAM_EPS) + ADAM_WD * w)
    return delta, m, v


def _adam(w, g, m, v, name, tr=256):
    rows, cols = w.shape
    if rows % tr:
        tr = rows

    def body(w_ref, g_ref, m_ref, v_ref, d_ref, nm_ref, nv_ref):
        d_ref[...], nm_ref[...], nv_ref[...] = _adam_math(w_ref[...], g_ref[...], m_ref[...], v_ref[...])

    spec = pl.BlockSpec((tr, cols), lambda i: (i, 0))
    return pl.pallas_call(
        body, name=name, grid=(rows // tr,), in_specs=[spec] * 4, out_specs=[spec] * 3,
        out_shape=[_sds(w.shape, F32)] * 3, compiler_params=_params(("parallel",)),
    )(w, g, m, v)


def _adam_cols(w, g_full, m, v, chip_arr, name):
    rows, cols = w.shape

    def body(chip_ref, w_ref, g_ref, m_ref, v_ref, gs_ref, d_ref, nm_ref, nv_ref):
        g = g_ref[...]
        gs_ref[...] = g
        d_ref[...], nm_ref[...], nv_ref[...] = _adam_math(w_ref[...], g, m_ref[...], v_ref[...])

    own = pl.BlockSpec((rows, cols), lambda i, ch: (0, 0))
    grid_spec = pltpu.PrefetchScalarGridSpec(
        num_scalar_prefetch=1, grid=(1,),
        in_specs=[own, pl.BlockSpec((rows, cols), lambda i, ch: (0, ch[0])), own, own],
        out_specs=[own] * 4)
    return pl.pallas_call(body, name=name, grid_spec=grid_spec, out_shape=[_sds(w.shape, F32)] * 4)(
        chip_arr, w, g_full, m, v)


PACK_COLS = 512
SMALL_REPLICATED = ("g_mix_pre", "g_mix_post", "conv_b", "w_rgate", "b_rgate", "w_igate", "b_igate", "lru_a",
                    "v_norm_g", "v_norm_b", "w_spatial", "b_spatial", "g_lru_out", "g_gmlp_out", "g_ffn_pre",
                    "g_ffn_post", "ffn_conv_b")
SMALL_COLUMN_SHARDED = ("conv_w", "ffn_conv_w")


def _pack(arrays):
    parts = []
    for arr in arrays:
        p = arr.reshape(-1, PACK_COLS)
        pad = (-p.shape[0]) % SUBLANES
        parts.append(jnp.pad(p, ((0, pad), (0, 0))) if pad else p)
    total = sum(p.shape[0] for p in parts)
    if total % (2 * SUBLANES):
        parts.append(jnp.zeros((SUBLANES, PACK_COLS), parts[0].dtype))
    return jnp.concatenate(parts, axis=0)


def _unpack(packed, shapes):
    out, row = [], 0
    for shape in shapes:
        n = math.prod(shape) // PACK_COLS
        out.append(packed[row:row + n].reshape(shape))
        row += n + (-n) % SUBLANES
    return out


def kernel(x, c, w_ada, b_ada, g_mix_pre, g_mix_post, w_in, conv_w, conv_b, w_rgate, b_rgate, w_igate, b_igate, lru_a, v_norm_g, v_norm_b, w_spatial, b_spatial, g_lru_out, g_gmlp_out, w_out, g_ffn_pre, g_ffn_post, w_up, ffn_conv_w, ffn_conv_b, w_down, loss_target, m_w_ada, m_b_ada, m_g_mix_pre, m_g_mix_post, m_w_in, m_conv_w, m_conv_b, m_w_rgate, m_b_rgate, m_w_igate, m_b_igate, m_lru_a, m_v_norm_g, m_v_norm_b, m_w_spatial, m_b_spatial, m_g_lru_out, m_g_gmlp_out, m_w_out, m_g_ffn_pre, m_g_ffn_post, m_w_up, m_ffn_conv_w, m_ffn_conv_b, m_w_down, v_w_ada, v_b_ada, v_g_mix_pre, v_g_mix_post, v_w_in, v_conv_w, v_conv_b, v_w_rgate, v_b_rgate, v_w_igate, v_b_igate, v_lru_a, v_v_norm_g, v_v_norm_b, v_w_spatial, v_b_spatial, v_g_lru_out, v_g_gmlp_out, v_w_out, v_g_ffn_pre, v_g_ffn_post, v_w_up, v_ffn_conv_w, v_ffn_conv_b, v_w_down):
    args = dict(locals())
    names = ("w_ada", "b_ada", "g_mix_pre", "g_mix_post", "w_in", "conv_w", "conv_b", "w_rgate", "b_rgate",
             "w_igate", "b_igate", "lru_a", "v_norm_g", "v_norm_b", "w_spatial", "b_spatial", "g_lru_out",
             "g_gmlp_out", "w_out", "g_ffn_pre", "g_ffn_post", "w_up", "ffn_conv_w", "ffn_conv_b", "w_down")
    w = {n: args[n][0] for n in names}
    m = {n: args["m_" + n][0] for n in names}
    v = {n: args["v_" + n][0] for n in names}
    xi, yi, ci = _position()
    me_arr = jnp.reshape(4 * xi + 2 * yi + ci, (1,)).astype(jnp.int32)
    chip_arr = jnp.reshape(2 * xi + yi, (1,)).astype(jnp.int32)
    c_arr = jnp.reshape(ci, (1,)).astype(jnp.int32)
    pos_arr = jnp.stack([ci, 2 * xi + yi]).astype(jnp.int32)

    big = ("w_in", "w_out", "w_up", "w_down")
    lands = _cast_place([w[n] for n in big], chip_arr)
    start_a, wait_a = _gather_plan([w[n].shape[0] for n in big[:2]])
    start_b, wait_b = _gather_plan([w[n].shape[0] for n in big[2:]])

    row0 = jnp.concatenate([c, w["conv_w"].reshape(1, -1), w["ffn_conv_w"].reshape(1, -1)], axis=1)
    g0 = _allgather8(row0, "gather_cond", False)[:, 0, :]
    c8 = g0[:, :D_MODEL]
    per_chip = g0[0::2]
    conv_w_full = per_chip[:, D_MODEL:D_MODEL + 512].reshape(N_CHIPS, 4, 128).transpose(1, 0, 2).reshape(4, 512)
    ffn_conv_w_full = per_chip[:, D_MODEL + 512:].reshape(N_CHIPS, 3, 1536).transpose(1, 0, 2).reshape(3, 2 * D_FF)
    mod_parts = _allgather8(_ada_fwd(c8, w["w_ada"]), "gather_mod", False)
    send_a, recv_a, _, lands_a, token_a = _split_start([], lands[:2], start_a, 6, mod_parts, "gather_start_a")
    send_b, recv_b, _, lands_b, token_b = _split_start([], lands[2:], start_b, 6, token_a, "gather_start_b")
    mod = _mod_select(mod_parts, w["b_ada"].reshape(1, -1), me_arr, token_b).reshape(N_MOD, D_MODEL)
    sh_m, sc_m, gt_m, sh_f, sc_f, gt_f = [mod[k:k + 1] for k in range(N_MOD)]

    small = {n: w[n] for n in SMALL_REPLICATED}
    small["conv_w"] = conv_w_full
    small["ffn_conv_w"] = ffn_conv_w_full
    row = lambda a: a.reshape(1, -1)
    seq_params, ws_t = _seq_params(small)
    glo, ggo = row(small["g_lru_out"]), row(small["g_gmlp_out"])
    g_pre, g_post = row(small["g_mix_pre"]), row(small["g_mix_post"])
    g_pre2, g_post2 = row(small["g_ffn_pre"]), row(small["g_ffn_post"])
    fw, fb = small["ffn_conv_w"], row(small["ffn_conv_b"])
    xs, tgt = x[0], loss_target[0]

    _, lands_a = _split_wait(send_a, recv_a, [], lands_a, wait_a, mod, "gather_wait_a")
    w_in4, w_out4 = _forward_to_sibling(lands_a, "forward_a")
    w_out_b = w_out4.reshape(D_MODEL, D_MODEL)
    z, h = _mix_in(xs, sc_m, sh_m, g_pre, w_in4)
    ycat, hst = _seqmix(z, seq_params, glo, ggo)
    y, x1, h2 = _mix_out(ycat, xs, w_out_b, gt_m, g_post, g_pre2, sc_f, sh_f)
    _, lands_b = _split_wait(send_b, recv_b, [], lands_b, wait_b, h2, "gather_wait_b")
    w_up4, w_down4 = _forward_to_sibling(lands_b, "forward_b")
    w_down_b = w_down4.reshape(D_FF, D_MODEL)
    up0, pre, act, dy2, dx2, loss, dgt_f, dg_post2 = _ffn_fwd(h2, x1, tgt, w_up4, w_down_b, fw, fb, gt_f, g_post2)

    dup0, dfw, dfb = _ffn_bwd_a(dy2, pre, up0, w_down_b, fw)
    gw_up = _wgrad(h2, dup0, N_CHIPS, "wgrad_up", True)
    gw_down = _wgrad(act, dy2, 2, "wgrad_down", False)
    ex_start, ex_wait = _exchange_plan(2)

    def reduce_start(parts, tags, name):
        recv = _swap_halves(parts, "swap_halves_" + name)
        both = [_chip_sum(p, r, pos_arr, "chip_sum_" + t) for p, r, t in zip(parts, recv, tags)]
        sums, gath = [b[0] for b in both], [b[1] for b in both]
        return _split_start(sums, gath, ex_start, 3 * len(parts), pos_arr, "exchange_start_" + name)

    e_send_b, e_recv_b, sums_b, gath_b, token_b = reduce_start(
        [gw_up, gw_down.reshape(N_CHIPS, -1, D_MODEL)], ("w_up", "w_down"), "b")

    dx1, dy, dsh_f, dsc_f, dg_pre2, dgt_m, dg_post = _ffn_bwd_b(
        dup0, x1, y, dx2, w_up4, g_pre2, sc_f + token_b[0:1, 0:1], sh_f, gt_m, g_post)
    (dz, dcw, dcb, dwr, dwi, dbr, dbi, dspa, dng, dnb, dws, dbs_t, dglo, dggo) = _seqmix_bwd(
        z, hst, dy, w_out_b, seq_params, ws_t, glo, ggo)
    grad_x, dsh_m, dsc_m, dg_pre = _mix_in_bwd(xs, dz, dx1, w_in4, g_pre, sc_m)
    gw_in = _wgrad(h, dz, N_CHIPS, "wgrad_in", True)
    gw_out = _wgrad(ycat, dy, 1, "wgrad_out", False)
    e_send_a, e_recv_a, sums_a, gath_a, token_a = reduce_start(
        [gw_in, gw_out.reshape(N_CHIPS, -1, D_MODEL)], ("w_in", "w_out"), "a")

    grads, deltas, new_m, new_v = {}, {}, {}, {}

    def reduce_finish(send, recv, sums, gath, tags, after, name):
        sums, gath = _split_wait(send, recv, sums, gath, ex_wait, after, "exchange_wait_" + name)
        gath = _swap_gathered(gath, "swap_gathered_" + name)
        for g, t in zip(gath, tags):
            grads[t], deltas[t], new_m[t], new_v[t] = _adam_gathered(w[t], g, m[t], v[t], c_arr, "adam_" + t)

    reduce_finish(e_send_b, e_recv_b, sums_b, gath_b, ("w_up", "w_down"), token_a, "b")
    reduce_finish(e_send_a, e_recv_a, sums_a, gath_a, ("w_in", "w_out"), deltas["w_down"], "a")

    dmod = jnp.concatenate([dsh_m, dsc_m, dgt_m, dsh_f, dsc_f, dgt_f], axis=1)
    dmod8 = _allgather8(dmod, "gather_dmod", False)[:, 0, :]
    g_w_ada, g_b_ada = _ada_bwd(c8, dmod8, chip_arr)
    grads["w_ada"] = g_w_ada
    deltas["w_ada"], new_m["w_ada"], new_v["w_ada"] = _adam(w["w_ada"], g_w_ada, m["w_ada"], v["w_ada"], "adam_w_ada")

    small_grads = dict(
        g_mix_pre=dg_pre[0], g_mix_post=dg_post[0], conv_w=dcw, conv_b=dcb[0],
        w_rgate=_diag_blocks(dwr), b_rgate=dbr.reshape(LRU_HEADS, -1),
        w_igate=_diag_blocks(dwi), b_igate=dbi.reshape(LRU_HEADS, -1), lru_a=dspa[0],
        v_norm_g=dng[0], v_norm_b=dnb[0], w_spatial=dws, b_spatial=dbs_t.T,
        g_lru_out=dglo[0], g_gmlp_out=dggo[0], g_ffn_pre=dg_pre2[0], g_ffn_post=dg_post2[0],
        ffn_conv_w=dfw, ffn_conv_b=dfb[0])
    packed_names = SMALL_REPLICATED + SMALL_COLUMN_SHARDED
    g_small = _allreduce_small(_pack([small_grads[n] for n in packed_names]), "reduce_small")
    g_small_list = _unpack(g_small, [small_grads[n].shape for n in packed_names])
    g_rep = dict(zip(packed_names, g_small_list))

    rep = SMALL_REPLICATED
    d_p, m_p, v_p = _adam(_pack([w[n] for n in rep]), _pack([g_rep[n] for n in rep]),
                          _pack([m[n] for n in rep]), _pack([v[n] for n in rep]), "adam_small")
    shapes = [w[n].shape for n in rep]
    for n, dd, mm, vv in zip(rep, _unpack(d_p, shapes), _unpack(m_p, shapes), _unpack(v_p, shapes)):
        grads[n], deltas[n], new_m[n], new_v[n] = g_rep[n], dd, mm, vv
    for n in SMALL_COLUMN_SHARDED:
        grads[n], deltas[n], new_m[n], new_v[n] = _adam_cols(w[n], g_rep[n], m[n], v[n], chip_arr, "adam_" + n)
    b2 = lambda a: a.reshape(-1, PACK_COLS)
    d_b, m_b, v_b = _adam(b2(w["b_ada"]), b2(g_b_ada), b2(m["b_ada"]), b2(v["b_ada"]), "adam_b_ada")
    grads["b_ada"], deltas["b_ada"], new_m["b_ada"], new_v["b_ada"] = (
        g_b_ada.reshape(-1), d_b.reshape(-1), m_b.reshape(-1), v_b.reshape(-1))

    total = lax.psum(loss[0, 0], ("x", "y", "c"))
    outs = [total, grad_x[None]]
    for group in (grads, deltas, new_m, new_v):
        outs.extend(group[n][None] for n in names)
    return tuple(outs)
```

```python
import functools
import math

import jax
import jax.numpy as jnp
from jax import lax
from jax.experimental import pallas as pl
from jax.experimental.pallas import tpu as pltpu

F32 = jnp.float32
BF16 = jnp.bfloat16
MESH = pl.DeviceIdType.MESH

D_MODEL = 1024
LRU_WIDTH = 512
LRU_HEADS = 8
GMLP_WIDTH = 512
GMLP_GROUPS = 4
GMLP_BLOCK = 128
CHUNK = 64
D_FF = 3072
N_MOD = 6
EPS = 1e-6
LRU_C = 8.0
N_CHIPS = 4
N_DEV = 8

ADAM_LR = 0.001
ADAM_B1 = 0.9
ADAM_B2 = 0.999
ADAM_EPS = 1e-08
ADAM_WD = 0.01
ADAM_STEP = 10

GELU_C0 = math.sqrt(2.0 / math.pi)
GELU_C1 = 0.044715

VMEM_LIMIT_BYTES = 56 * 1024 * 1024
SUBLANES = 8
BF16_SUBLANES = 16
FFN_CHUNK = 768
SUB_ROWS = 256


def _gelu_gate(x):
    x2 = x * x
    z = x * ((2.0 * GELU_C0 * GELU_C1) * x2 + 2.0 * GELU_C0)
    return 1.0 / (1.0 + jnp.exp(-z)), x2


def _gelu(x):
    t = jnp.tanh(GELU_C0 * (x + GELU_C1 * x * x * x))
    return 0.5 * x * (1.0 + t)


def _gelu_and_grad(x):
    s, x2 = _gelu_gate(x)
    g = x * s
    dz = (6.0 * GELU_C0 * GELU_C1) * x2 + 2.0 * GELU_C0
    return g, s + g * (1.0 - s) * dz


def _sigmoid(x):
    return 1.0 / (1.0 + jnp.exp(-x))


def _log1p(u):
    w = 1.0 + u
    return jnp.where(w == 1.0, u, jnp.log(w) * (u / (w - 1.0)))


def _softplus(x):
    return jnp.maximum(x, 0.0) + _log1p(jnp.exp(-jnp.abs(x)))


def _neg_expm1(x):
    u = jnp.exp(x)
    um1 = u - 1.0
    tiny = um1 == 0.0
    small = um1 * (x / jnp.log(jnp.where(tiny, 2.0, jnp.maximum(u, 0.25))))
    return -jnp.where(tiny, x, jnp.where(x < -1.0, um1, small))


def _msq_rsqrt(v):
    return lax.rsqrt(jnp.mean(v * v, axis=-1, keepdims=True) + EPS)


def _rms_bwd(dyn, yn, r):
    return r * (dyn - yn * jnp.mean(dyn * yn, axis=-1, keepdims=True))


def _colsum(v):
    return jnp.sum(v, axis=0, keepdims=True)


def _shift_down(cur, prev8, k):
    rolled = pltpu.roll(cur, k, 0)
    head = pltpu.roll(prev8, k, 0)
    row8 = lax.broadcasted_iota(jnp.int32, (SUBLANES, cur.shape[1]), 0)
    first = jnp.where(row8 < k, head, rolled[0:SUBLANES])
    return jnp.concatenate([first, rolled[SUBLANES:]], axis=0)


def _shift_up(cur, next8, k):
    t = cur.shape[0]
    rolled = pltpu.roll(cur, t - k, 0)
    tail = pltpu.roll(next8, SUBLANES - k, 0)
    row8 = lax.broadcasted_iota(jnp.int32, (SUBLANES, cur.shape[1]), 0)
    last = jnp.where(row8 >= SUBLANES - k, tail, rolled[t - SUBLANES:])
    return jnp.concatenate([rolled[:t - SUBLANES], last], axis=0)


def _scan_fwd(a, b):
    t = a.shape[0]
    row = lax.broadcasted_iota(jnp.int32, a.shape, 0)
    d = 1
    while d < t:
        keep = row >= d
        a_s = jnp.where(keep, pltpu.roll(a, d, 0), 1.0)
        b_s = jnp.where(keep, pltpu.roll(b, d, 0), 0.0)
        b = a * b_s + b
        a = a * a_s
        d *= 2
    return a, b


def _scan_bwd(a, g):
    t = a.shape[0]
    row = lax.broadcasted_iota(jnp.int32, a.shape, 0)
    d = 1
    while d < t:
        keep = row < t - d
        a_s = jnp.where(keep, pltpu.roll(a, t - d, 0), 1.0)
        g_s = jnp.where(keep, pltpu.roll(g, t - d, 0), 0.0)
        g = a * g_s + g
        a = a * a_s
        d *= 2
    return a, g


def _dot(a, b):
    return jnp.dot(a, b, preferred_element_type=F32)


def _dot_nt(a, b):
    return lax.dot_general(a, b, (((1,), (1,)), ((), ())), preferred_element_type=F32)


def _dot_tn(a, b):
    return lax.dot_general(a, b, (((0,), (0,)), ((), ())), preferred_element_type=F32)


def _rows(ts, cols, rev_of=None):
    if rev_of is None:
        return pl.BlockSpec((ts, cols), lambda i: (i, 0))
    return pl.BlockSpec((ts, cols), lambda i: (rev_of - 1 - i, 0))


def _halo_prev(ts, cols, halo, rev_of=None, col_block=0):
    per = ts // halo
    if rev_of is None:
        return pl.BlockSpec((halo, cols), lambda i: (jnp.maximum(i * per - 1, 0), col_block))
    return pl.BlockSpec((halo, cols), lambda i: (jnp.maximum((rev_of - 1 - i) * per - 1, 0), col_block))


def _full(shape):
    nd = len(shape)
    return pl.BlockSpec(shape, lambda *_: (0,) * nd)


_RESIDENT = pl.BlockSpec(memory_space=pltpu.VMEM)


def _params(sem):
    return pltpu.CompilerParams(dimension_semantics=sem, vmem_limit_bytes=VMEM_LIMIT_BYTES)


def _sds(shape, dtype):
    return jax.ShapeDtypeStruct(shape, dtype)


def _sub_tiles(ts):
    return [slice(r0, r0 + SUB_ROWS) for r0 in range(0, ts, SUB_ROWS)]


def _mix_in(x, sc, sh, g, w_in4, ts=512):
    s, d = x.shape

    def body(x_ref, sc_ref, sh_ref, g_ref, w_ref, z_ref, h_ref):
        for rs in _sub_tiles(ts):
            xv = x_ref[rs, :]
            h = (xv * _msq_rsqrt(xv) * g_ref[...]) * (1.0 + sc_ref[...]) + sh_ref[...]
            hb = h.astype(BF16)
            h_ref[rs, :] = hb
            for k in range(N_CHIPS):
                z_ref[rs, k * 512:(k + 1) * 512] = _dot(hb, w_ref[k])

    return pl.pallas_call(
        body, grid=(s // ts,), name="mix_in",
        in_specs=[_rows(ts, d), _full((1, d)), _full((1, d)), _full((1, d)), _full(w_in4.shape)],
        out_specs=[_rows(ts, 2048), _rows(ts, d)],
        out_shape=[_sds((s, 2048), F32), _sds((s, d), BF16)],
        compiler_params=_params(("parallel",)),
    )(x, sc, sh, g, w_in4)


N_STASH = 12
(ST_XC, ST_R, ST_IG, ST_A, ST_MULT, ST_GL, ST_DGL, ST_U, ST_DU, ST_Q, ST_VHAT, ST_SPB) = range(N_STASH)


def _seq_param_specs():
    return [_full((4, 512)), _full((1, 512)), _full((512, 512)), _full((512, 512)), _full((1, 512)),
            _full((1, 512)), _full((1, 512)), _full((1, 512)), _full((1, 512)), _full((4, 128, 128)),
            _full((128, 4))]


def _seqmix(z, seq_params, glo, ggo, ts=256):
    s = z.shape[0]
    nt = s // ts

    def body(z_ref, zprev_ref, cw_ref, cb_ref, bdr_ref, bdi_ref, br_ref, bi_ref, la_ref, ng_ref, nb_ref,
             ws_ref, bst_ref, glo_ref, ggo_ref, ycat_ref, hst_ref, st_ref, hcarry, sp_scr):
        i = pl.program_id(0)

        @pl.when(i == 0)
        def _():
            hcarry[...] = jnp.zeros_like(hcarry)

        lx = z_ref[:, 0:512]
        prev8 = jnp.where(i == 0, 0.0, zprev_ref[...])
        xc = (cw_ref[3:4, :] * lx + cw_ref[2:3, :] * _shift_down(lx, prev8, 1)
              + cw_ref[1:2, :] * _shift_down(lx, prev8, 2) + cw_ref[0:1, :] * _shift_down(lx, prev8, 3)
              + cb_ref[...])
        xcb = xc.astype(BF16)
        r = _sigmoid(_dot(xcb, bdr_ref[...]) + br_ref[...])
        ig = _sigmoid(_dot(xcb, bdi_ref[...]) + bi_ref[...])
        log_a = (-LRU_C) * r * _softplus(-la_ref[...])
        a = jnp.exp(log_a)
        mult = jnp.sqrt(_neg_expm1(2.0 * log_a))
        acum, hloc = _scan_fwd(a, mult * (ig * xc))
        h = hloc + acum * hcarry[...]
        hcarry[...] = h[ts - 1:ts, :]
        hst_ref[...] = h
        gl, dgl = _gelu_and_grad(z_ref[:, 512:1024])
        y_l = h * gl
        for slot, val in ((ST_XC, xc), (ST_R, r), (ST_IG, ig), (ST_A, a), (ST_MULT, mult), (ST_GL, gl),
                          (ST_DGL, dgl)):
            st_ref[slot] = val

        u, du = _gelu_and_grad(z_ref[:, 1024:1536])
        vg, dvg = _gelu_and_grad(z_ref[:, 1536:2048])
        vc = vg - jnp.mean(vg, axis=-1, keepdims=True)
        rstd = lax.rsqrt(jnp.mean(vc * vc, axis=-1, keepdims=True) + EPS)
        vhat = vc * rstd
        vb = (vhat * ng_ref[...] + nb_ref[...]).astype(BF16)
        for n in range(ts // GMLP_BLOCK):
            rs = slice(n * GMLP_BLOCK, (n + 1) * GMLP_BLOCK)
            for g in range(GMLP_GROUPS):
                cs = slice(g * 128, (g + 1) * 128)
                sp_scr[rs, cs] = _dot(ws_ref[g], vb[rs, cs]) + bst_ref[:, g:g + 1]
        spb = sp_scr[...]
        y_g = u * spb
        for slot, val in ((ST_U, u), (ST_DU, du), (ST_Q, rstd * dvg), (ST_VHAT, vhat), (ST_SPB, spb)):
            st_ref[slot] = val

        ycat_ref[:, 0:512] = (y_l * _msq_rsqrt(y_l) * glo_ref[...]).astype(BF16)
        ycat_ref[:, 512:1024] = (y_g * _msq_rsqrt(y_g) * ggo_ref[...]).astype(BF16)

    return pl.pallas_call(
        body, grid=(nt,), name="seqmix",
        in_specs=[_rows(ts, 2048), _halo_prev(ts, 512, SUBLANES)] + _seq_param_specs()
        + [_full((1, 512)), _full((1, 512))],
        out_specs=[_rows(ts, 1024), _rows(ts, 512), pl.BlockSpec((N_STASH, ts, 512), lambda i: (0, i, 0))],
        out_shape=[_sds((s, 1024), BF16), _sds((s, 512), F32), _sds((N_STASH, s, 512), F32)],
        scratch_shapes=[pltpu.VMEM((1, 512), F32), pltpu.VMEM((ts, 512), F32)],
        compiler_params=_params(("arbitrary",)),
    )(z, z, *seq_params, glo, ggo)


def _mix_out(ycat, x, w_out, gt_m, g_post, g_pre2, sc_f, sh_f, ts=512):
    s, d = x.shape

    def body(yc_ref, x_ref, w_ref, gt_ref, gp_ref, g2_ref, sc_ref, sh_ref, y_ref, x1_ref, h2_ref):
        for rs in _sub_tiles(ts):
            y = _dot(yc_ref[rs, :], w_ref[...])
            y_ref[rs, :] = y
            x1 = x_ref[rs, :] + gt_ref[...] * (y * _msq_rsqrt(y) * gp_ref[...])
            x1_ref[rs, :] = x1
            h2 = (x1 * _msq_rsqrt(x1) * g2_ref[...]) * (1.0 + sc_ref[...]) + sh_ref[...]
            h2_ref[rs, :] = h2.astype(BF16)

    vec = _full((1, d))
    return pl.pallas_call(
        body, grid=(s // ts,), name="mix_out",
        in_specs=[_rows(ts, d), _rows(ts, d), _full((d, d)), vec, vec, vec, vec, vec],
        out_specs=[_rows(ts, d), _rows(ts, d), _rows(ts, d)],
        out_shape=[_sds((s, d), F32), _sds((s, d), F32), _sds((s, d), BF16)],
        compiler_params=_params(("parallel",)),
    )(ycat, x, w_out, gt_m, g_post, g_pre2, sc_f, sh_f)


def _ffn_cols(j):
    per = (2 * D_FF // N_CHIPS) // FFN_CHUNK
    return j // per, (j % per) * FFN_CHUNK, j * FFN_CHUNK


def _ffn_fwd(h2, x1, tgt, w_up4, w_down, fw, fb, gt_f, g_post, ts=256):
    s, d = x1.shape
    nch = D_FF // FFN_CHUNK

    def body(h2_ref, x1_ref, tgt_ref, wup_ref, wdn_ref, fw_ref, fb_ref, gt_ref, gp_ref,
             up0_ref, pre_ref, act_ref, dy2_ref, dx2_ref, loss_ref, dgt_ref, dgp_ref, tail_ref):
        i = pl.program_id(0)

        @pl.when(i == 0)
        def _():
            tail_ref[...] = jnp.zeros_like(tail_ref)
            loss_ref[...] = jnp.zeros_like(loss_ref)
            dgt_ref[...] = jnp.zeros_like(dgt_ref)
            dgp_ref[...] = jnp.zeros_like(dgp_ref)

        hb = h2_ref[...]

        def up_project(j):
            sh_g, off, _ = _ffn_cols(j)
            return [_dot(hb, wup_ref[shard, :, off:off + FFN_CHUNK]).astype(BF16) for shard in (sh_g, sh_g + 2)]

        y2 = jnp.zeros((ts, d), F32)
        ahead = up_project(0)
        for j in range(nch):
            _, _, col = _ffn_cols(j)
            ubs = ahead
            if j + 1 < nch:
                ahead = up_project(j + 1)
            halves = []
            for ub, c0 in zip(ubs, (col, D_FF + col)):
                cs = slice(c0, c0 + FFN_CHUNK)
                up0_ref[:, cs] = ub
                u = ub.astype(F32)
                prev8 = tail_ref[:, cs]
                tail_ref[:, cs] = u[ts - SUBLANES:, :]
                halves.append(fw_ref[2:3, cs] * u + fw_ref[1:2, cs] * _shift_down(u, prev8, 1)
                              + fw_ref[0:1, cs] * _shift_down(u, prev8, 2) + fb_ref[:, cs])
                pre_ref[:, cs] = halves[-1].astype(BF16)
            act = (_gelu(halves[0]) * halves[1]).astype(BF16)
            act_ref[:, col:col + FFN_CHUNK] = act
            y2 = y2 + _dot(act, wdn_ref[col:col + FFN_CHUNK, :])
        r2 = _msq_rsqrt(y2)
        yn = y2 * r2
        yng = yn * gp_ref[...]
        e = x1_ref[...] + gt_ref[...] * yng - tgt_ref[...]
        loss_ref[...] += jnp.sum(e * e) * (0.5 / d)
        dx2 = e * (1.0 / d)
        dx2_ref[...] = dx2
        dgt_ref[...] += _colsum(dx2 * yng)
        dyng = dx2 * gt_ref[...]
        dgp_ref[...] += _colsum(dyng * yn)
        dy2_ref[...] = _rms_bwd(dyng * gp_ref[...], yn, r2).astype(BF16)

    vec = _full((1, d))
    return pl.pallas_call(
        body, grid=(s // ts,), name="ffn_fwd",
        in_specs=[_rows(ts, d), _rows(ts, d), _rows(ts, d), _RESIDENT, _RESIDENT,
                  _full((3, 2 * D_FF)), _full((1, 2 * D_FF)), vec, vec],
        out_specs=[_rows(ts, 2 * D_FF), _rows(ts, 2 * D_FF), _rows(ts, D_FF), _rows(ts, d), _rows(ts, d),
                   _full((1, 128)), vec, vec],
        out_shape=[_sds((s, 2 * D_FF), BF16), _sds((s, 2 * D_FF), BF16), _sds((s, D_FF), BF16), _sds((s, d), BF16),
                   _sds((s, d), F32), _sds((1, 128), F32), _sds((1, d), F32), _sds((1, d), F32)],
        scratch_shapes=[pltpu.VMEM((SUBLANES, 2 * D_FF), F32)],
        compiler_params=_params(("arbitrary",)),
    )(h2, x1, tgt, w_up4, w_down, fw, fb, gt_f, g_post)


def _shift_up_mxu(vb, up_mat, next8, k):
    t = vb.shape[0]
    main = _dot(up_mat, vb)
    tail = pltpu.roll(next8, SUBLANES - k, 0)
    row8 = lax.broadcasted_iota(jnp.int32, next8.shape, 0)
    last = main[t - SUBLANES:] + jnp.where(row8 >= SUBLANES - k, tail, 0.0)
    return jnp.concatenate([main[:t - SUBLANES], last], axis=0)


def _ffn_bwd_a(dy2, pre, up0, w_down, fw, ts=256):
    s, d = dy2.shape
    nt = s // ts
    nch = D_FF // FFN_CHUNK
    wide = 2 * D_FF
    up_mats = jnp.stack([jnp.eye(ts, k=1, dtype=BF16), jnp.eye(ts, k=2, dtype=BF16)])

    def body(dy2_ref, pre_ref, up0_ref, wdn_ref, fw_ref, um_ref, dup0_ref, dfw_ref, dfb_ref, next_ref):
        i = pl.program_id(0)

        @pl.when(i == 0)
        def _():
            next_ref[...] = jnp.zeros_like(next_ref)
            dfw_ref[...] = jnp.zeros_like(dfw_ref)
            dfb_ref[...] = jnp.zeros_like(dfb_ref)

        dyb = dy2_ref[...]
        for j in range(nch):
            _, _, col = _ffn_cols(j)
            dact = _dot_nt(dyb, wdn_ref[col:col + FFN_CHUNK, :])
            gl, dgl = _gelu_and_grad(pre_ref[:, col:col + FFN_CHUNK].astype(F32))
            dpre = (dact * pre_ref[:, D_FF + col:D_FF + col + FFN_CHUNK].astype(F32) * dgl, dact * gl)
            for half, c0 in enumerate((col, D_FF + col)):
                cs = slice(c0, c0 + FFN_CHUNK)
                dp = dpre[half]
                dpb = dp.astype(BF16)
                nxt = next_ref[:, cs]
                next_ref[:, cs] = dpb.astype(F32)[0:SUBLANES, :]
                su1 = _shift_up_mxu(dpb, um_ref[0], nxt, 1)
                su2 = _shift_up_mxu(dpb, um_ref[1], nxt, 2)
                u = up0_ref[:, cs].astype(F32)
                dfb_ref[:, cs] += _colsum(dp)
                dfw_ref[2:3, cs] += _colsum(dp * u)
                dfw_ref[1:2, cs] += _colsum(su1 * u)
                dfw_ref[0:1, cs] += _colsum(su2 * u)
                dup0 = fw_ref[2:3, cs] * dp + fw_ref[1:2, cs] * su1 + fw_ref[0:1, cs] * su2
                dup0_ref[:, cs] = dup0.astype(BF16)

    return pl.pallas_call(
        body, grid=(nt,), name="ffn_bwd_a",
        in_specs=[_rows(ts, d, nt), _rows(ts, wide, nt), _rows(ts, wide, nt), _RESIDENT,
                  _full((3, wide)), _full((2, ts, ts))],
        out_specs=[_rows(ts, wide, nt), _full((3, wide)), _full((1, wide))],
        out_shape=[_sds((s, wide), BF16), _sds((3, wide), F32), _sds((1, wide), F32)],
        scratch_shapes=[pltpu.VMEM((SUBLANES, wide), F32)],
        compiler_params=_params(("arbitrary",)),
    )(dy2, pre, up0, w_down, fw, up_mats)


def _ffn_bwd_b(dup0, x1, y, dx2, w_up4, g_pre2, sc_f, sh_f, gt_m, g_post_m, ts=512):
    s, d = x1.shape
    shard_cols = 2 * D_FF // N_CHIPS

    def body(dup_ref, x1_ref, y_ref, dx2_ref, wup_ref, g2_ref, sc_ref, sh_ref, gt_ref, gp_ref,
             dx1_ref, dy_ref, dsh_ref, dsc_ref, dg2_ref, dgt_ref, dgp_ref):
        i = pl.program_id(0)

        @pl.when(i == 0)
        def _():
            for ref in (dsh_ref, dsc_ref, dg2_ref, dgt_ref, dgp_ref):
                ref[...] = jnp.zeros_like(ref)

        for rs in _sub_tiles(ts):
            dh2 = jnp.zeros((SUB_ROWS, d), F32)
            for k in range(N_CHIPS):
                dh2 = dh2 + _dot_nt(dup_ref[rs, k * shard_cols:(k + 1) * shard_cols], wup_ref[k])
            x1v = x1_ref[rs, :]
            r2 = _msq_rsqrt(x1v)
            xn = x1v * r2
            hn = xn * g2_ref[...]
            dsh_ref[...] += _colsum(dh2)
            dsc_ref[...] += _colsum(dh2 * hn)
            dhn = dh2 * (1.0 + sc_ref[...])
            dg2_ref[...] += _colsum(dhn * xn)
            dx1 = dx2_ref[rs, :] + _rms_bwd(dhn * g2_ref[...], xn, r2)
            dx1_ref[rs, :] = dx1
            yv = y_ref[rs, :]
            ry = _msq_rsqrt(yv)
            yn = yv * ry
            dgt_ref[...] += _colsum(dx1 * (yn * gp_ref[...]))
            dyng = dx1 * gt_ref[...]
            dgp_ref[...] += _colsum(dyng * yn)
            dy_ref[rs, :] = _rms_bwd(dyng * gp_ref[...], yn, ry).astype(BF16)

    vec = _full((1, d))
    return pl.pallas_call(
        body, grid=(s // ts,), name="ffn_bwd_b",
        in_specs=[_rows(ts, 2 * D_FF), _rows(ts, d), _rows(ts, d), _rows(ts, d), _RESIDENT,
                  vec, vec, vec, vec, vec],
        out_specs=[_rows(ts, d), _rows(ts, d), vec, vec, vec, vec, vec],
        out_shape=[_sds((s, d), F32), _sds((s, d), BF16)] + [_sds((1, d), F32)] * 5,
        compiler_params=_params(("arbitrary",)),
    )(dup0, x1, y, dx2, w_up4, g_pre2, sc_f, sh_f, gt_m, g_post_m)


def _seqmix_bwd(z, hst, stash, dy, w_out, seq_params, ws_t, glo, ggo, ts=256):
    s = z.shape[0]
    nt = s // ts
    small_shapes = [(4, 512), (1, 512), (512, 512), (512, 512), (1, 512), (1, 512), (1, 512),
                    (1, 512), (1, 512), (4, 128, 128), (128, 4), (1, 512), (1, 512)]

    def body(lx_ref, hst_ref, hprev_ref, st_ref, dy_ref, wout_ref, cw_ref, cb_ref, bdr_ref, bdi_ref, br_ref,
             bi_ref, la_ref, ng_ref, nb_ref, ws_ref, bst_ref, wst_ref, glo_ref, ggo_ref, dz_ref, *rest):
        small_refs = rest[:13]
        (dcw_ref, dcb_ref, dwr_ref, dwi_ref, dbr_ref, dbi_ref, dspa_ref, dng_ref, dnb_ref, dws_ref, dbs_ref,
         dglo_ref, dggo_ref) = small_refs
        gcarry, anext, dxcnext, dv_scr = rest[13:]
        i = pl.program_id(0)

        @pl.when(i == 0)
        def _():
            for ref in small_refs:
                ref[...] = jnp.zeros_like(ref)
            gcarry[...] = jnp.zeros_like(gcarry)
            anext[...] = jnp.ones_like(anext)
            dxcnext[...] = jnp.zeros_like(dxcnext)

        first_tile = i == nt - 1
        xc, r, ig, a, mult = st_ref[ST_XC], st_ref[ST_R], st_ref[ST_IG], st_ref[ST_A], st_ref[ST_MULT]
        gl, u, spb, vhat = st_ref[ST_GL], st_ref[ST_U], st_ref[ST_SPB], st_ref[ST_VHAT]
        lx = lx_ref[...]
        h = hst_ref[...]
        hprev = _shift_down(h, jnp.where(first_tile, 0.0, hprev_ref[...]), 1)
        y_l = h * gl
        y_g = u * spb

        dycat = _dot_nt(dy_ref[...], wout_ref[...])
        rl = _msq_rsqrt(y_l)
        yln = y_l * rl
        dyl = dycat[:, 0:512]
        dglo_ref[...] += _colsum(dyl * yln)
        dy_l = _rms_bwd(dyl * glo_ref[...], yln, rl)
        rg = _msq_rsqrt(y_g)
        ygn = y_g * rg
        dyg = dycat[:, 512:1024]
        dggo_ref[...] += _colsum(dyg * ygn)
        dy_g = _rms_bwd(dyg * ggo_ref[...], ygn, rg)

        dz_ref[:, 512:1024] = (dy_l * h * st_ref[ST_DGL]).astype(BF16)
        a_up = _shift_up(a, anext[...], 1)
        acum, gloc = _scan_bwd(a_up, dy_l * gl)
        gg = gloc + acum * gcarry[...]
        gcarry[...] = gg[0:1, :]
        anext[...] = a[0:SUBLANES, :]
        da = gg * hprev
        t1 = gg * mult
        di = t1 * xc
        dxc = t1 * ig
        dmult = gg * ig * xc
        dla = da * a - dmult * (a * a / mult)
        dspa_ref[...] += _colsum(dla * r) * (-LRU_C)
        dpr = dla * ((-LRU_C) * _softplus(-la_ref[...])) * r * (1.0 - r)
        dpi = di * ig * (1.0 - ig)
        dbr_ref[...] += _colsum(dpr)
        dbi_ref[...] += _colsum(dpi)
        dprb = dpr.astype(BF16)
        dpib = dpi.astype(BF16)
        xcb = xc.astype(BF16)
        dwr_ref[...] += _dot_tn(xcb, dprb)
        dwi_ref[...] += _dot_tn(xcb, dpib)
        dxc = dxc + _dot_nt(dprb, bdr_ref[...]) + _dot_nt(dpib, bdi_ref[...])
        nxt = dxcnext[...]
        dxcnext[...] = dxc[0:SUBLANES, :]
        up1, up2, up3 = _shift_up(dxc, nxt, 1), _shift_up(dxc, nxt, 2), _shift_up(dxc, nxt, 3)
        dcb_ref[...] += _colsum(dxc)
        dcw_ref[3:4, :] += _colsum(dxc * lx)
        dcw_ref[2:3, :] += _colsum(up1 * lx)
        dcw_ref[1:2, :] += _colsum(up2 * lx)
        dcw_ref[0:1, :] += _colsum(up3 * lx)
        dlx = cw_ref[3:4, :] * dxc + cw_ref[2:3, :] * up1 + cw_ref[1:2, :] * up2 + cw_ref[0:1, :] * up3
        dz_ref[:, 0:512] = dlx.astype(BF16)

        dz_ref[:, 1024:1536] = (dy_g * spb * st_ref[ST_DU]).astype(BF16)
        dsp = dy_g * u
        vb = (vhat * ng_ref[...] + nb_ref[...]).astype(BF16)
        for n in range(ts // GMLP_BLOCK):
            rs = slice(n * GMLP_BLOCK, (n + 1) * GMLP_BLOCK)
            for g in range(GMLP_GROUPS):
                cs = slice(g * 128, (g + 1) * 128)
                dbs_ref[:, g:g + 1] += jnp.sum(dsp[rs, cs], axis=1, keepdims=True)
                blk = dsp[rs, cs].astype(BF16)
                dws_ref[g] += _dot_nt(blk, vb[rs, cs])
                dv_scr[rs, cs] = _dot(wst_ref[g], blk)
        dv = dv_scr[...]
        dng_ref[...] += _colsum(dv * vhat)
        dnb_ref[...] += _colsum(dv)
        dvh = dv * ng_ref[...]
        dvg = dvh - jnp.mean(dvh, axis=-1, keepdims=True) - vhat * jnp.mean(dvh * vhat, axis=-1, keepdims=True)
        dz_ref[:, 1536:2048] = (dvg * st_ref[ST_Q]).astype(BF16)

        @pl.when(i == nt - 1)
        def _():
            pos = lax.broadcasted_iota(jnp.int32, (GMLP_BLOCK, GMLP_BLOCK), 0) // CHUNK
            src = lax.broadcasted_iota(jnp.int32, (GMLP_BLOCK, GMLP_BLOCK), 1) // CHUNK
            for g in range(GMLP_GROUPS):
                dws_ref[g] = jnp.where(src <= pos, dws_ref[g], 0.0)
            dspa_ref[...] = dspa_ref[...] * (-_sigmoid(-la_ref[...]))

    in_specs = ([_rows(ts, 512, nt), _rows(ts, 512, nt), _halo_prev(ts, 512, SUBLANES, nt),
                 pl.BlockSpec((N_STASH, ts, 512), lambda i: (0, nt - 1 - i, 0)), _rows(ts, 1024, nt),
                 _full((1024, 1024))]
                + _seq_param_specs() + [_full((4, 128, 128)), _full((1, 512)), _full((1, 512))])
    return pl.pallas_call(
        body, grid=(nt,), name="seqmix_bwd",
        in_specs=in_specs,
        out_specs=[_rows(ts, 2048, nt)] + [_full(sh) for sh in small_shapes],
        out_shape=[_sds((s, 2048), BF16)] + [_sds(sh, F32) for sh in small_shapes],
        scratch_shapes=[pltpu.VMEM((1, 512), F32), pltpu.VMEM((SUBLANES, 512), F32),
                        pltpu.VMEM((SUBLANES, 512), F32), pltpu.VMEM((ts, 512), F32)],
        compiler_params=_params(("arbitrary",)),
    )(z, hst, hst, stash, dy, w_out, *seq_params, ws_t, glo, ggo)


def _seqmix_bwd_recomputing_unused(z, hst, dy, w_out, seq_params, ws_t, glo, ggo, ts=256):
    s = z.shape[0]
    nt = s // ts
    small_shapes = [(4, 512), (1, 512), (512, 512), (512, 512), (1, 512), (1, 512), (1, 512),
                    (1, 512), (1, 512), (4, 128, 128), (128, 4), (1, 512), (1, 512)]

    def body(z_ref, zprev_ref, hst_ref, hprev_ref, dy_ref, wout_ref, *rest):
        p = rest[:11]
        wst_ref, glo_ref, ggo_ref = rest[11:14]
        dz_ref = rest[14]
        (dcw_ref, dcb_ref, dwr_ref, dwi_ref, dbr_ref, dbi_ref, dspa_ref, dng_ref, dnb_ref, dws_ref, dbs_ref,
         dglo_ref, dggo_ref) = rest[15:28]
        gcarry, anext, dxcnext, sp_scr, dv_scr = rest[28:]
        i = pl.program_id(0)

        @pl.when(i == 0)
        def _():
            for ref in rest[15:28]:
                ref[...] = jnp.zeros_like(ref)
            gcarry[...] = jnp.zeros_like(gcarry)
            anext[...] = jnp.ones_like(anext)
            dxcnext[...] = jnp.zeros_like(dxcnext)

        first_tile = i == nt - 1
        f = _seq_recompute(z_ref, zprev_ref, first_tile, p)
        xc, r, ig, a, mult, lx = f["xc"], f["r"], f["ig"], f["a"], f["mult"], f["lx"]
        h = hst_ref[...]
        hprev = _shift_down(h, jnp.where(first_tile, 0.0, hprev_ref[...]), 1)
        gl, dgl = _gelu_and_grad(f["lg"])
        y_l = h * gl
        gm = _gmlp_fwd(f["gu"], f["gv"], p[7], p[8], p[9], p[10], sp_scr)
        y_g = gm["y_g"]

        dycat = _dot_nt(dy_ref[...], wout_ref[...])
        rl = _msq_rsqrt(y_l)
        yln = y_l * rl
        dyl = dycat[:, 0:512]
        dglo_ref[...] += _colsum(dyl * yln)
        dy_l = _rms_bwd(dyl * glo_ref[...], yln, rl)
        rg = _msq_rsqrt(y_g)
        ygn = y_g * rg
        dyg = dycat[:, 512:1024]
        dggo_ref[...] += _colsum(dyg * ygn)
        dy_g = _rms_bwd(dyg * ggo_ref[...], ygn, rg)

        dz_ref[:, 512:1024] = (dy_l * h * dgl).astype(BF16)
        a_up = _shift_up(a, anext[...], 1)
        acum, gloc = _scan_bwd(a_up, dy_l * gl)
        gg = gloc + acum * gcarry[...]
        gcarry[...] = gg[0:1, :]
        anext[...] = a[0:SUBLANES, :]
        da = gg * hprev
        t1 = gg * mult
        di = t1 * xc
        dxc = t1 * ig
        dmult = gg * ig * xc
        dla = da * a - dmult * (a * a / mult)
        spa = f["spa"]
        dspa_ref[...] += _colsum(dla * r) * (-LRU_C)
        dpr = dla * ((-LRU_C) * spa) * r * (1.0 - r)
        dpi = di * ig * (1.0 - ig)
        dbr_ref[...] += _colsum(dpr)
        dbi_ref[...] += _colsum(dpi)
        dprb = dpr.astype(BF16)
        dpib = dpi.astype(BF16)
        dwr_ref[...] += _dot_tn(f["xcb"], dprb)
        dwi_ref[...] += _dot_tn(f["xcb"], dpib)
        dxc = dxc + _dot_nt(dprb, p[2][...]) + _dot_nt(dpib, p[3][...])
        dcb_ref[...] += _colsum(dxc)
        dcw_ref[3:4, :] += _colsum(dxc * lx)
        dcw_ref[2:3, :] += _colsum(dxc * f["s1"])
        dcw_ref[1:2, :] += _colsum(dxc * f["s2"])
        dcw_ref[0:1, :] += _colsum(dxc * f["s3"])
        nxt = dxcnext[...]
        dxcnext[...] = dxc[0:SUBLANES, :]
        cw_ref = p[0]
        dlx = (cw_ref[3:4, :] * dxc + cw_ref[2:3, :] * _shift_up(dxc, nxt, 1)
               + cw_ref[1:2, :] * _shift_up(dxc, nxt, 2) + cw_ref[0:1, :] * _shift_up(dxc, nxt, 3))
        dz_ref[:, 0:512] = dlx.astype(BF16)

        dz_ref[:, 1024:1536] = (dy_g * gm["spb"] * gm["du"]).astype(BF16)
        dsp = dy_g * gm["u"]
        vb = gm["vb"]
        for n in range(ts // GMLP_BLOCK):
            rs = slice(n * GMLP_BLOCK, (n + 1) * GMLP_BLOCK)
            for g in range(GMLP_GROUPS):
                cs = slice(g * 128, (g + 1) * 128)
                dbs_ref[:, g:g + 1] += jnp.sum(dsp[rs, cs], axis=1, keepdims=True)
                blk = dsp[rs, cs].astype(BF16)
                dws_ref[g] += _dot_nt(blk, vb[rs, cs])
                dv_scr[rs, cs] = _dot(wst_ref[g], blk)
        dv = dv_scr[...]
        vhat = gm["vhat"]
        dng_ref[...] += _colsum(dv * vhat)
        dnb_ref[...] += _colsum(dv)
        dvh = dv * p[7][...]
        dvg = gm["rstd"] * (dvh - jnp.mean(dvh, axis=-1, keepdims=True)
                            - vhat * jnp.mean(dvh * vhat, axis=-1, keepdims=True))
        dz_ref[:, 1536:2048] = (dvg * gm["dvg"]).astype(BF16)

        @pl.when(i == nt - 1)
        def _():
            pos = lax.broadcasted_iota(jnp.int32, (GMLP_BLOCK, GMLP_BLOCK), 0) // CHUNK
            src = lax.broadcasted_iota(jnp.int32, (GMLP_BLOCK, GMLP_BLOCK), 1) // CHUNK
            for g in range(GMLP_GROUPS):
                dws_ref[g] = jnp.where(src <= pos, dws_ref[g], 0.0)
            dspa_ref[...] = dspa_ref[...] * (-_sigmoid(-p[6][...]))

    in_specs = (_seq_specs(ts, nt, True)
                + [_rows(ts, 512, nt), _halo_prev(ts, 512, SUBLANES, nt), _rows(ts, 1024, nt), _full((1024, 1024))]
                + _seq_param_specs() + [_full((4, 128, 128)), _full((1, 512)), _full((1, 512))])
    return pl.pallas_call(
        body, grid=(nt,), name="seqmix_bwd",
        in_specs=in_specs,
        out_specs=[_rows(ts, 2048, nt)] + [_full(sh) for sh in small_shapes],
        out_shape=[_sds((s, 2048), BF16)] + [_sds(sh, F32) for sh in small_shapes],
        scratch_shapes=[pltpu.VMEM((1, 512), F32), pltpu.VMEM((SUBLANES, 512), F32),
                        pltpu.VMEM((SUBLANES, 512), F32), pltpu.VMEM((ts, 512), F32), pltpu.VMEM((ts, 512), F32)],
        compiler_params=_params(("arbitrary",)),
    )(z, z, hst, hst, dy, w_out, *seq_params, ws_t, glo, ggo)


def _mix_in_bwd(x, dz, dx1, w_in4, g, sc, ts=512):
    s, d = x.shape

    def body(x_ref, dz_ref, dx1_ref, w_ref, g_ref, sc_ref, gx_ref, dsh_ref, dsc_ref, dg_ref):
        i = pl.program_id(0)

        @pl.when(i == 0)
        def _():
            for ref in (dsh_ref, dsc_ref, dg_ref):
                ref[...] = jnp.zeros_like(ref)

        for rs in _sub_tiles(ts):
            dh = jnp.zeros((SUB_ROWS, d), F32)
            for k in range(N_CHIPS):
                dh = dh + _dot_nt(dz_ref[rs, k * 512:(k + 1) * 512], w_ref[k])
            xv = x_ref[rs, :]
            r = _msq_rsqrt(xv)
            xn = xv * r
            dsh_ref[...] += _colsum(dh)
            dsc_ref[...] += _colsum(dh * (xn * g_ref[...]))
            dhn = dh * (1.0 + sc_ref[...])
            dg_ref[...] += _colsum(dhn * xn)
            gx_ref[rs, :] = dx1_ref[rs, :] + _rms_bwd(dhn * g_ref[...], xn, r)

    vec = _full((1, d))
    return pl.pallas_call(
        body, grid=(s // ts,), name="mix_in_bwd",
        in_specs=[_rows(ts, d), _rows(ts, 2048), _rows(ts, d), _full(w_in4.shape), vec, vec],
        out_specs=[_rows(ts, d), vec, vec, vec],
        out_shape=[_sds((s, d), F32)] + [_sds((1, d), F32)] * 3,
        compiler_params=_params(("arbitrary",)),
    )(x, dz, dx1, w_in4, g, sc)


def _wgrad(a, b, n_chunks, name, chunk_major, ts=2048):
    s, m = a.shape
    n = b.shape[1]
    nc = n // n_chunks
    nt = s // ts

    def body(a_ref, b_ref, o_ref, acc):
        i = pl.program_id(1)

        @pl.when(i == 0)
        def _():
            acc[...] = jnp.zeros_like(acc)

        acc[...] += _dot_tn(a_ref[...], b_ref[...])

        @pl.when(i == nt - 1)
        def _():
            if chunk_major:
                o_ref[0] = acc[...].astype(BF16)
            else:
                o_ref[...] = acc[...].astype(BF16)

    if chunk_major:
        out_spec, out_shape = pl.BlockSpec((1, m, nc), lambda c, i: (c, 0, 0)), _sds((n_chunks, m, nc), BF16)
    else:
        out_spec, out_shape = pl.BlockSpec((m, nc), lambda c, i: (0, c)), _sds((m, n), BF16)
    return pl.pallas_call(
        body, grid=(n_chunks, nt), name=name,
        in_specs=[pl.BlockSpec((ts, m), lambda c, i: (i, 0)), pl.BlockSpec((ts, nc), lambda c, i: (i, c))],
        out_specs=out_spec,
        out_shape=out_shape,
        scratch_shapes=[pltpu.VMEM((m, nc), F32)],
        compiler_params=_params(("parallel", "arbitrary")),
    )(a, b)


def _block_diag(w):
    heads, hd, _ = w.shape
    eye = jnp.eye(heads, dtype=w.dtype)
    return (eye[:, None, :, None] * w[:, :, None, :]).reshape(heads * hd, heads * hd)


def _diag_blocks(m):
    hd = LRU_WIDTH // LRU_HEADS
    m4 = m.reshape(LRU_HEADS, hd, LRU_HEADS, hd)
    return jnp.stack([m4[k, :, k, :] for k in range(LRU_HEADS)])


def _seq_params(small):
    row = lambda v: v.reshape(1, -1)
    pos = jnp.arange(GMLP_BLOCK)
    mask = (pos[None, :] // CHUNK) <= (pos[:, None] // CHUNK)
    ws = jnp.where(mask[None], small["w_spatial"], 0.0)
    seq_params = (small["conv_w"], row(small["conv_b"]),
                  _block_diag(small["w_rgate"]).astype(BF16), _block_diag(small["w_igate"]).astype(BF16),
                  row(small["b_rgate"]), row(small["b_igate"]), row(small["lru_a"]),
                  row(small["v_norm_g"]), row(small["v_norm_b"]), ws.astype(BF16), small["b_spatial"].T)
    return seq_params, jnp.swapaxes(ws, 1, 2).astype(BF16)


_ANY = pl.BlockSpec(memory_space=pl.ANY)
_CHIP_FLIPS = ((1, 0), (0, 1), (1, 1))


def _position():
    return lax.axis_index("x"), lax.axis_index("y"), lax.axis_index("c")


def _flip(v, f):
    return 1 - v if f else v


def _remote(src, dst, send_sem, recv_sem, peer):
    return pltpu.make_async_remote_copy(src_ref=src, dst_ref=dst, send_sem=send_sem, recv_sem=recv_sem,
                                        device_id=peer, device_id_type=MESH)


def _allgather8(block, name, reduce):
    r, n = block.shape

    def body(x_ref, out_ref, *scratch):
        if reduce:
            gath, send_sems, recv_sems, loc_sem = scratch
        else:
            gath = out_ref
            send_sems, recv_sems, loc_sem = scratch
        x, y, c = _position()
        me = 4 * x + 2 * y + c
        loc = pltpu.make_async_copy(x_ref, gath.at[me], loc_sem)
        loc.start()
        peers = []
        for k in range(1, N_DEV):
            px, py, pc = _flip(x, k & 4), _flip(y, k & 2), _flip(c, k & 1)
            peers.append((px, py, pc))
            _remote(x_ref, gath.at[me], send_sems.at[k - 1], recv_sems.at[k - 1], (px, py, pc)).start()
        for k, (px, py, pc) in enumerate(peers):
            src = 4 * px + 2 * py + pc
            _remote(x_ref, gath.at[src], send_sems.at[k], recv_sems.at[k], (px, py, pc)).wait_recv()
        for k, peer in enumerate(peers):
            _remote(x_ref, gath.at[me], send_sems.at[k], recv_sems.at[k], peer).wait_send()
        loc.wait()
        if reduce:
            acc = gath[0]
            for k in range(1, N_DEV):
                acc = acc + gath[k]
            out_ref[...] = acc

    sems = [pltpu.SemaphoreType.DMA((N_DEV - 1,)), pltpu.SemaphoreType.DMA((N_DEV - 1,)), pltpu.SemaphoreType.DMA]
    if reduce:
        out_shape = _sds((r, n), F32)
        scratch = [pltpu.VMEM((N_DEV, r, n), F32)] + sems
    else:
        out_shape = _sds((N_DEV, r, n), F32)
        scratch = sems
    return pl.pallas_call(
        body, name=name, out_shape=out_shape,
        in_specs=[pl.BlockSpec(memory_space=pltpu.VMEM)], out_specs=pl.BlockSpec(memory_space=pltpu.VMEM),
        scratch_shapes=scratch,
        compiler_params=pltpu.CompilerParams(vmem_limit_bytes=VMEM_LIMIT_BYTES),
    )(block)


def _half(ref, c, rows):
    hr = rows // 2
    return ref.at[pl.ds(pl.multiple_of(c * hr, BF16_SUBLANES), hr), :]


def _gather_weights(shards):
    na = len(shards)

    def body(*refs):
        ins, outs = refs[:na], refs[na:2 * na]
        ici_send, ici_recv, d2d_send, d2d_recv, loc_sem = refs[2 * na:]
        x, y, c = _position()
        chip = 2 * x + y
        sibling = (x, y, 1 - c)
        local = []
        for a in range(na):
            local.append(pltpu.make_async_copy(ins[a], outs[a].at[chip], loc_sem.at[a]))
            local[-1].start()
        sends = []
        for a in range(na):
            rows = shards[a].shape[0]
            for j, (fx, fy) in enumerate(_CHIP_FLIPS):
                peer = (_flip(x, fx), _flip(y, fy), c)
                sends.append(_remote(_half(ins[a], c, rows), _half(outs[a].at[chip], c, rows),
                                     ici_send.at[a * 3 + j], ici_recv.at[a * 3 + j], peer))
                sends[-1].start()
        for a in range(na):
            rows = shards[a].shape[0]
            for j, (fx, fy) in enumerate(_CHIP_FLIPS):
                src_chip = 2 * _flip(x, fx) + _flip(y, fy)
                landed = _half(outs[a].at[src_chip], c, rows)
                _remote(landed, landed, ici_send.at[a * 3 + j], ici_recv.at[a * 3 + j], sibling).wait_recv()
                sends.append(_remote(landed, landed, d2d_send.at[a * 3 + j], d2d_recv.at[a * 3 + j], sibling))
                sends[-1].start()
        for a in range(na):
            rows = shards[a].shape[0]
            for j, (fx, fy) in enumerate(_CHIP_FLIPS):
                src_chip = 2 * _flip(x, fx) + _flip(y, fy)
                other = _half(outs[a].at[src_chip], 1 - c, rows)
                _remote(other, other, d2d_send.at[a * 3 + j], d2d_recv.at[a * 3 + j], sibling).wait_recv()
        for cp in sends:
            cp.wait_send()
        for cp in local:
            cp.wait()

    return pl.pallas_call(
        body, name="gather_weights",
        out_shape=[_sds((N_CHIPS,) + w.shape, w.dtype) for w in shards],
        in_specs=[_ANY] * na, out_specs=[_ANY] * na,
        scratch_shapes=[pltpu.SemaphoreType.DMA((3 * na,))] * 4 + [pltpu.SemaphoreType.DMA((na,))],
    )(*shards)


def _swap_halves(parts, name):
    na = len(parts)

    def body(*refs):
        ins, outs = refs[:na], refs[na:2 * na]
        send_sems, recv_sems = refs[2 * na:]
        x, y, c = _position()
        sibling = (x, y, 1 - c)
        cps = []
        for a in range(na):
            hr = parts[a].shape[1] // 2
            src = ins[a].at[:, pl.ds(pl.multiple_of((1 - c) * hr, BF16_SUBLANES), hr), :]
            cps.append(_remote(src, outs[a], send_sems.at[a], recv_sems.at[a], sibling))
            cps[-1].start()
        for cp in cps:
            cp.wait()

    return pl.pallas_call(
        body, name=name,
        out_shape=[_sds((N_CHIPS, p.shape[1] // 2, p.shape[2]), p.dtype) for p in parts],
        in_specs=[_ANY] * na, out_specs=[_ANY] * na,
        scratch_shapes=[pltpu.SemaphoreType.DMA((na,))] * 2,
    )(*parts)


def _chip_sum(part, recv, pos_arr, name):
    _, rows, cols = part.shape
    hr = rows // 2

    def body(pos_ref, p_ref, r_ref, o_ref, g_ref):
        total = (p_ref[...].astype(F32) + r_ref[...].astype(F32)).astype(BF16)
        o_ref[...] = total

        @pl.when(pl.program_id(0) == pos_ref[1])
        def _():
            g_ref[0] = total

    grid_spec = pltpu.PrefetchScalarGridSpec(
        num_scalar_prefetch=1, grid=(N_CHIPS,),
        in_specs=[pl.BlockSpec((1, hr, cols), lambda k, pos: (k, pos[0], 0)),
                  pl.BlockSpec((1, hr, cols), lambda k, pos: (k, 0, 0))],
        out_specs=[pl.BlockSpec((1, hr, cols), lambda k, pos: (k, 0, 0)),
                   pl.BlockSpec((1, 1, hr, cols), lambda k, pos: (0, pos[1], 0, 0))])
    return pl.pallas_call(
        body, name=name, grid_spec=grid_spec,
        out_shape=[_sds((N_CHIPS, hr, cols), BF16), _sds((2, N_CHIPS, hr, cols), BF16)],
        compiler_params=_params(("arbitrary",)),
    )(pos_arr, part, recv)


def _exchange_chips(sums):
    na = len(sums)

    def body(*refs):
        ins, outs = refs[:na], refs[na:2 * na]
        send_sems, recv_sems, loc_sem = refs[2 * na:]
        x, y, c = _position()
        chip = 2 * x + y
        local = []
        for a in range(na):
            local.append(pltpu.make_async_copy(ins[a].at[chip], outs[a].at[chip], loc_sem.at[a]))
            local[-1].start()
        cps = []
        for a in range(na):
            for j, (fx, fy) in enumerate(_CHIP_FLIPS):
                px, py = _flip(x, fx), _flip(y, fy)
                cps.append(_remote(ins[a].at[2 * px + py], outs[a].at[chip],
                                   send_sems.at[a * 3 + j], recv_sems.at[a * 3 + j], (px, py, c)))
                cps[-1].start()
        for a in range(na):
            for j, (fx, fy) in enumerate(_CHIP_FLIPS):
                src_chip = 2 * _flip(x, fx) + _flip(y, fy)
                landed = outs[a].at[src_chip]
                _remote(landed, landed, send_sems.at[a * 3 + j], recv_sems.at[a * 3 + j], (x, y, c)).wait_recv()
        for cp in cps:
            cp.wait_send()
        for cp in local:
            cp.wait()

    return pl.pallas_call(
        body, name="exchange_chips",
        out_shape=[_sds(s.shape, s.dtype) for s in sums],
        in_specs=[_ANY] * na, out_specs=[_ANY] * na,
        scratch_shapes=[pltpu.SemaphoreType.DMA((3 * na,))] * 2 + [pltpu.SemaphoreType.DMA((na,))],
    )(*sums)


def _sum_chips(gath, name, tr=128):
    _, hr, cols = gath.shape
    tr = min(tr, hr)

    def body(g_ref, o_ref):
        acc = g_ref[0].astype(F32)
        for k in range(1, N_CHIPS):
            acc = acc + g_ref[k].astype(F32)
        o_ref[...] = acc

    return pl.pallas_call(
        body, name=name, grid=(hr // tr,),
        in_specs=[pl.BlockSpec((N_CHIPS, tr, cols), lambda i: (0, i, 0))],
        out_specs=pl.BlockSpec((tr, cols), lambda i: (i, 0)),
        out_shape=_sds((hr, cols), F32),
        compiler_params=_params(("parallel",)),
    )(gath)


def _join_halves(halves):
    na = len(halves)

    def body(*refs):
        ins, outs = refs[:na], refs[na:2 * na]
        send_sems, recv_sems, loc_sem = refs[2 * na:]
        x, y, c = _position()
        sibling = (x, y, 1 - c)
        cps, local = [], []
        for a in range(na):
            rows = 2 * halves[a].shape[0]
            mine = _half(outs[a], c, rows)
            local.append(pltpu.make_async_copy(ins[a], mine, loc_sem.at[a]))
            local[-1].start()
            cps.append(_remote(ins[a], mine, send_sems.at[a], recv_sems.at[a], sibling))
            cps[-1].start()
        for a in range(na):
            rows = 2 * halves[a].shape[0]
            other = _half(outs[a], 1 - c, rows)
            _remote(ins[a], other, send_sems.at[a], recv_sems.at[a], sibling).wait_recv()
        for cp in cps:
            cp.wait_send()
        for cp in local:
            cp.wait()

    return pl.pallas_call(
        body, name="join_halves",
        out_shape=[_sds((2 * h.shape[0], h.shape[1]), h.dtype) for h in halves],
        in_specs=[_ANY] * na, out_specs=[_ANY] * na,
        scratch_shapes=[pltpu.SemaphoreType.DMA((na,))] * 3,
    )(*halves)


_HBM = pl.BlockSpec(memory_space=pltpu.HBM)
_SEM = pl.BlockSpec(memory_space=pltpu.SEMAPHORE)
_EFFECT = pltpu.SideEffectType.DATAFLOW_SIDE_EFFECTING


def _in_hbm(a):
    return pltpu.with_memory_space_constraint(a, pltpu.HBM)


def _split_start(srcs, lands, plan, n_copies, after, name):
    ns, nl = len(srcs), len(lands)
    bufs = list(srcs) + list(lands)

    def body(*refs):
        send_sems, recv_sems = refs[ns + nl + 1], refs[ns + nl + 2]
        token = refs[-1]
        for k, (src, dst, peer) in enumerate(plan(refs[:ns], refs[ns:ns + nl])):
            _remote(src, dst, send_sems.at[k], recv_sems.at[k], peer).start()
        token[...] = jnp.zeros_like(token)

    out = pl.pallas_call(
        body, name=name,
        out_shape=(pltpu.SemaphoreType.DMA((n_copies,)), pltpu.SemaphoreType.DMA((n_copies,)),
                   *[pltpu.HBM(b.shape, b.dtype) for b in bufs], _sds((SUBLANES, 128), F32)),
        in_specs=[_HBM] * (ns + nl) + [_ANY],
        out_specs=(_SEM, _SEM, *[_HBM] * (ns + nl), pl.BlockSpec(memory_space=pltpu.VMEM)),
        input_output_aliases={i: 2 + i for i in range(ns + nl)},
        compiler_params=pltpu.CompilerParams(has_side_effects=_EFFECT),
    )(*[_in_hbm(b) for b in bufs], after)
    return out[0], out[1], list(out[2:2 + ns]), list(out[2 + ns:2 + ns + nl]), out[-1]


def _split_wait(send_sems, recv_sems, srcs, lands, plan, after, name):
    ns, nl = len(srcs), len(lands)
    bufs = list(srcs) + list(lands)

    def body(*refs):
        send_ref, recv_ref = refs[ns + nl], refs[ns + nl + 1]
        me = _position()
        for k, src, dst in plan(refs[:ns], refs[ns:ns + nl]):
            cp = _remote(src, dst, send_ref.at[k], recv_ref.at[k], me)
            cp.wait_send()
            cp.wait_recv()

    out = pl.pallas_call(
        body, name=name,
        out_shape=[pltpu.HBM(b.shape, b.dtype) for b in bufs],
        in_specs=[_HBM] * (ns + nl) + [_SEM, _SEM, _ANY],
        out_specs=[_HBM] * (ns + nl),
        input_output_aliases={i: i for i in range(ns + nl)},
        compiler_params=pltpu.CompilerParams(has_side_effects=_EFFECT),
    )(*bufs, send_sems, recv_sems, after)
    return list(out[:ns]), list(out[ns:])


def _gather_plan(rows_of):
    def start(src_refs, land_refs):
        x, y, c = _position()
        chip = 2 * x + y
        out = []
        for a, rows in enumerate(rows_of):
            mine = _half(land_refs[a].at[chip], c, rows)
            out.extend((mine, mine, (_flip(x, fx), _flip(y, fy), c)) for fx, fy in _CHIP_FLIPS)
        return out

    def wait(src_refs, land_refs):
        x, y, c = _position()
        chip = 2 * x + y
        out = []
        for a, rows in enumerate(rows_of):
            for j, (fx, fy) in enumerate(_CHIP_FLIPS):
                src_chip = 2 * _flip(x, fx) + _flip(y, fy)
                out.append((3 * a + j, _half(land_refs[a].at[chip], c, rows),
                            _half(land_refs[a].at[src_chip], c, rows)))
        return out

    return start, wait


def _exchange_plan(n_arrays):
    def start(src_refs, land_refs):
        x, y, c = _position()
        chip = 2 * x + y
        out = []
        for a in range(n_arrays):
            for fx, fy in _CHIP_FLIPS:
                px, py = _flip(x, fx), _flip(y, fy)
                out.append((src_refs[a].at[2 * px + py], land_refs[a].at[0, chip], (px, py, c)))
        return out

    def wait(src_refs, land_refs):
        x, y, c = _position()
        out = []
        for a in range(n_arrays):
            for j, (fx, fy) in enumerate(_CHIP_FLIPS):
                src_chip = 2 * _flip(x, fx) + _flip(y, fy)
                out.append((3 * a + j, src_refs[a].at[src_chip], land_refs[a].at[0, src_chip]))
        return out

    return start, wait


def _forward_to_sibling(lands, name):
    na = len(lands)

    def body(*refs):
        land_refs = refs[na:2 * na]
        send_sems, recv_sems = refs[2 * na:]
        x, y, c = _position()
        sibling = (x, y, 1 - c)
        sends = []
        for a in range(na):
            rows = lands[a].shape[1]
            for j, (fx, fy) in enumerate(_CHIP_FLIPS):
                landed = _half(land_refs[a].at[2 * _flip(x, fx) + _flip(y, fy)], c, rows)
                sends.append(_remote(landed, landed, send_sems.at[3 * a + j], recv_sems.at[3 * a + j], sibling))
                sends[-1].start()
        for a in range(na):
            rows = lands[a].shape[1]
            for j, (fx, fy) in enumerate(_CHIP_FLIPS):
                other = _half(land_refs[a].at[2 * _flip(x, fx) + _flip(y, fy)], 1 - c, rows)
                _remote(other, other, send_sems.at[3 * a + j], recv_sems.at[3 * a + j], sibling).wait_recv()
        for cp in sends:
            cp.wait_send()

    return pl.pallas_call(
        body, name=name,
        out_shape=[_sds(l.shape, l.dtype) for l in lands],
        in_specs=[_ANY] * na, out_specs=[_ANY] * na,
        input_output_aliases={a: a for a in range(na)},
        scratch_shapes=[pltpu.SemaphoreType.DMA((3 * na,))] * 2,
    )(*lands)


def _swap_gathered(gath, name):
    na = len(gath)

    def body(*refs):
        gath_refs = refs[na:2 * na]
        send_sems, recv_sems = refs[2 * na:]
        x, y, c = _position()
        cps = [_remote(gath_refs[a].at[0], gath_refs[a].at[1], send_sems.at[a], recv_sems.at[a], (x, y, 1 - c))
               for a in range(na)]
        for cp in cps:
            cp.start()
        for cp in cps:
            cp.wait()

    return pl.pallas_call(
        body, name=name,
        out_shape=[_sds(g.shape, g.dtype) for g in gath],
        in_specs=[_ANY] * na, out_specs=[_ANY] * na,
        input_output_aliases={a: a for a in range(na)},
        scratch_shapes=[pltpu.SemaphoreType.DMA((na,))] * 2,
    )(*gath)


def _adam_gathered(w, gath, m, v, c_arr, name, tr=128):
    rows, cols = w.shape
    hr = rows // 2
    per = hr // tr

    def body(c_ref, w_ref, g_ref, m_ref, v_ref, go_ref, d_ref, nm_ref, nv_ref):
        g = g_ref[0, 0].astype(F32)
        for k in range(1, N_CHIPS):
            g = g + g_ref[0, k].astype(F32)
        go_ref[...] = g
        d_ref[...], nm_ref[...], nv_ref[...] = _adam_math(w_ref[...], g, m_ref[...], v_ref[...])

    def rows_of(h, i, c_ref):
        c = c_ref[0]
        return ((c + h - 2 * c * h) * per + i, 0)

    blk = pl.BlockSpec((tr, cols), rows_of)
    grid_spec = pltpu.PrefetchScalarGridSpec(
        num_scalar_prefetch=1, grid=(2, per),
        in_specs=[blk, pl.BlockSpec((1, N_CHIPS, tr, cols), lambda h, i, c_ref: (h, 0, i, 0)), blk, blk],
        out_specs=[blk] * 4)
    return pl.pallas_call(
        body, name=name, grid_spec=grid_spec, out_shape=[_sds(w.shape, F32)] * 4,
        compiler_params=_params(("arbitrary", "arbitrary")),
    )(c_arr, w, gath, m, v)


def _allreduce_small(block, name):
    r, n = block.shape
    hr = r // 2

    def body(x_ref, out_ref, sib, chipsum, gath, d2d_send, d2d_recv, ici_send, ici_recv):
        x, y, c = _position()
        chip = 2 * x + y
        sibling = (x, y, 1 - c)
        first = _remote(x_ref, sib, d2d_send.at[0], d2d_recv.at[0], sibling)
        first.start()
        first.wait()
        chipsum[...] = x_ref[...] + sib[...]
        mine = pl.ds(pl.multiple_of(c * hr, SUBLANES), hr)
        theirs = pl.ds(pl.multiple_of((1 - c) * hr, SUBLANES), hr)
        sends = []
        for j, (fx, fy) in enumerate(_CHIP_FLIPS):
            sends.append(_remote(chipsum.at[mine, :], gath.at[chip], ici_send.at[j], ici_recv.at[j],
                                 (_flip(x, fx), _flip(y, fy), c)))
            sends[-1].start()
        gath[chip] = chipsum[mine, :]
        for j, (fx, fy) in enumerate(_CHIP_FLIPS):
            landed = gath.at[2 * _flip(x, fx) + _flip(y, fy)]
            _remote(landed, landed, ici_send.at[j], ici_recv.at[j], sibling).wait_recv()
        for cp in sends:
            cp.wait_send()
        total = gath[0]
        for k in range(1, N_CHIPS):
            total = total + gath[k]
        out_ref[mine, :] = total
        last = _remote(out_ref.at[mine, :], out_ref.at[mine, :], d2d_send.at[1], d2d_recv.at[1], sibling)
        last.start()
        _remote(out_ref.at[theirs, :], out_ref.at[theirs, :], d2d_send.at[1], d2d_recv.at[1], sibling).wait_recv()
        last.wait_send()

    vmem = pl.BlockSpec(memory_space=pltpu.VMEM)
    return pl.pallas_call(
        body, name=name, out_shape=_sds((r, n), F32), in_specs=[vmem], out_specs=vmem,
        scratch_shapes=[pltpu.VMEM((r, n), F32), pltpu.VMEM((r, n), F32), pltpu.VMEM((N_CHIPS, hr, n), F32),
                        pltpu.SemaphoreType.DMA((2,)), pltpu.SemaphoreType.DMA((2,)),
                        pltpu.SemaphoreType.DMA((3,)), pltpu.SemaphoreType.DMA((3,))],
        compiler_params=pltpu.CompilerParams(vmem_limit_bytes=VMEM_LIMIT_BYTES),
    )(block)


def _cast_place(shards, chip_arr):
    na = len(shards)
    steps = 4

    def body(chip_ref, *refs):
        for a in range(na):
            refs[na + a][0] = refs[a][...].astype(BF16)

    grid_spec = pltpu.PrefetchScalarGridSpec(
        num_scalar_prefetch=1, grid=(steps,),
        in_specs=[pl.BlockSpec((s.shape[0] // steps, s.shape[1]), lambda i, ch: (i, 0)) for s in shards],
        out_specs=[pl.BlockSpec((1, s.shape[0] // steps, s.shape[1]), lambda i, ch: (ch[0], i, 0)) for s in shards])
    return pl.pallas_call(
        body, name="cast_place", grid_spec=grid_spec,
        out_shape=[_sds((N_CHIPS,) + s.shape, BF16) for s in shards],
        compiler_params=_params(("arbitrary",)),
    )(chip_arr, *shards)


def _silu(v):
    return v * _sigmoid(v)


def _ada_fwd(c8, w_ada):
    def body(c_ref, w_ref, o_ref):
        o_ref[...] = jnp.dot(_silu(c_ref[...]), w_ref[...], preferred_element_type=F32,
                             precision=lax.Precision.HIGHEST)

    return pl.pallas_call(
        body, name="ada_fwd", out_shape=_sds((N_DEV, w_ada.shape[1]), F32),
        compiler_params=pltpu.CompilerParams(vmem_limit_bytes=VMEM_LIMIT_BYTES),
    )(c8, w_ada)


def _mod_select(parts, b_ada, me_arr, after):
    cols = parts.shape[2]

    def body(me_ref, p_ref, b_ref, after_ref, o_ref):
        me = me_ref[0]
        for k in range(N_CHIPS):
            cs = slice(k * cols, (k + 1) * cols)
            o_ref[:, cs] = p_ref[2 * k, pl.ds(me, 1), :] + b_ref[:, cs]

    grid_spec = pltpu.PrefetchScalarGridSpec(
        num_scalar_prefetch=1, grid=(1,),
        in_specs=[pl.BlockSpec(parts.shape, lambda i, m: (0, 0, 0)), pl.BlockSpec(b_ada.shape, lambda i, m: (0, 0)),
                  _ANY],
        out_specs=pl.BlockSpec(b_ada.shape, lambda i, m: (0, 0)))
    return pl.pallas_call(body, name="mod_select", grid_spec=grid_spec, out_shape=_sds(b_ada.shape, F32))(
        me_arr, parts, b_ada, after)


def _ada_bwd(c8, dmod8, chip_arr):
    d = c8.shape[1]
    cols = dmod8.shape[1] // N_CHIPS

    def body(chip_ref, c_ref, dm_ref, dmall_ref, gw_ref, gb_ref):
        gw_ref[...] = lax.dot_general(_silu(c_ref[...]), dm_ref[...], (((0,), (0,)), ((), ())),
                                      preferred_element_type=F32, precision=lax.Precision.HIGHEST)
        acc = dmall_ref[0:1, :]
        for k in range(1, N_DEV):
            acc = acc + dmall_ref[k:k + 1, :]
        gb_ref[...] = acc

    grid_spec = pltpu.PrefetchScalarGridSpec(
        num_scalar_prefetch=1, grid=(1,),
        in_specs=[pl.BlockSpec(c8.shape, lambda i, ch: (0, 0)),
                  pl.BlockSpec((N_DEV, cols), lambda i, ch: (0, ch[0])),
                  pl.BlockSpec(dmod8.shape, lambda i, ch: (0, 0))],
        out_specs=[pl.BlockSpec((d, cols), lambda i, ch: (0, 0)), pl.BlockSpec((1, dmod8.shape[1]), lambda i, ch: (0, 0))])
    return pl.pallas_call(
        body, name="ada_bwd", grid_spec=grid_spec,
        out_shape=[_sds((d, cols), F32), _sds((1, dmod8.shape[1]), F32)],
        compiler_params=_params(("arbitrary",)),
    )(chip_arr, c8, dmod8, dmod8)


def _adam_math(w, g, m, v):
    m = ADAM_B1 * m + (1.0 - ADAM_B1) * g
    v = ADAM_B2 * v + (1.0 - ADAM_B2) * (g * g)
    m_hat = m / (1.0 - ADAM_B1 ** ADAM_STEP)
    v_hat = v / (1.0 - ADAM_B2 ** ADAM_STEP)
    delta = -ADAM_LR * (m_hat / (jnp.sqrt(v_hat) + ADAM_EPS) + ADAM_WD * w)
    return delta, m, v


def _adam(w, g, m, v, name, tr=256):
    rows, cols = w.shape
    if rows % tr:
        tr = rows

    def body(w_ref, g_ref, m_ref, v_ref, d_ref, nm_ref, nv_ref):
        d_ref[...], nm_ref[...], nv_ref[...] = _adam_math(w_ref[...], g_ref[...], m_ref[...], v_ref[...])

    spec = pl.BlockSpec((tr, cols), lambda i: (i, 0))
    return pl.pallas_call(
        body, name=name, grid=(rows // tr,), in_specs=[spec] * 4, out_specs=[spec] * 3,
        out_shape=[_sds(w.shape, F32)] * 3, compiler_params=_params(("parallel",)),
    )(w, g, m, v)


def _adam_cols(w, g_full, m, v, chip_arr, name):
    rows, cols = w.shape

    def body(chip_ref, w_ref, g_ref, m_ref, v_ref, gs_ref, d_ref, nm_ref, nv_ref):
        g = g_ref[...]
        gs_ref[...] = g
        d_ref[...], nm_ref[...], nv_ref[...] = _adam_math(w_ref[...], g, m_ref[...], v_ref[...])

    own = pl.BlockSpec((rows, cols), lambda i, ch: (0, 0))
    grid_spec = pltpu.PrefetchScalarGridSpec(
        num_scalar_prefetch=1, grid=(1,),
        in_specs=[own, pl.BlockSpec((rows, cols), lambda i, ch: (0, ch[0])), own, own],
        out_specs=[own] * 4)
    return pl.pallas_call(body, name=name, grid_spec=grid_spec, out_shape=[_sds(w.shape, F32)] * 4)(
        chip_arr, w, g_full, m, v)


PACK_COLS = 512
SMALL_REPLICATED = ("g_mix_pre", "g_mix_post", "conv_b", "w_rgate", "b_rgate", "w_igate", "b_igate", "lru_a",
                    "v_norm_g", "v_norm_b", "w_spatial", "b_spatial", "g_lru_out", "g_gmlp_out", "g_ffn_pre",
                    "g_ffn_post", "ffn_conv_b")
SMALL_COLUMN_SHARDED = ("conv_w", "ffn_conv_w")


def _pack(arrays):
    parts = []
    for arr in arrays:
        p = arr.reshape(-1, PACK_COLS)
        pad = (-p.shape[0]) % SUBLANES
        parts.append(jnp.pad(p, ((0, pad), (0, 0))) if pad else p)
    total = sum(p.shape[0] for p in parts)
    if total % (2 * SUBLANES):
        parts.append(jnp.zeros((SUBLANES, PACK_COLS), parts[0].dtype))
    return jnp.concatenate(parts, axis=0)


def _unpack(packed, shapes):
    out, row = [], 0
    for shape in shapes:
        n = math.prod(shape) // PACK_COLS
        out.append(packed[row:row + n].reshape(shape))
        row += n + (-n) % SUBLANES
    return out


def kernel(x, c, w_ada, b_ada, g_mix_pre, g_mix_post, w_in, conv_w, conv_b, w_rgate, b_rgate, w_igate, b_igate, lru_a, v_norm_g, v_norm_b, w_spatial, b_spatial, g_lru_out, g_gmlp_out, w_out, g_ffn_pre, g_ffn_post, w_up, ffn_conv_w, ffn_conv_b, w_down, loss_target, m_w_ada, m_b_ada, m_g_mix_pre, m_g_mix_post, m_w_in, m_conv_w, m_conv_b, m_w_rgate, m_b_rgate, m_w_igate, m_b_igate, m_lru_a, m_v_norm_g, m_v_norm_b, m_w_spatial, m_b_spatial, m_g_lru_out, m_g_gmlp_out, m_w_out, m_g_ffn_pre, m_g_ffn_post, m_w_up, m_ffn_conv_w, m_ffn_conv_b, m_w_down, v_w_ada, v_b_ada, v_g_mix_pre, v_g_mix_post, v_w_in, v_conv_w, v_conv_b, v_w_rgate, v_b_rgate, v_w_igate, v_b_igate, v_lru_a, v_v_norm_g, v_v_norm_b, v_w_spatial, v_b_spatial, v_g_lru_out, v_g_gmlp_out, v_w_out, v_g_ffn_pre, v_g_ffn_post, v_w_up, v_ffn_conv_w, v_ffn_conv_b, v_w_down):
    args = dict(locals())
    names = ("w_ada", "b_ada", "g_mix_pre", "g_mix_post", "w_in", "conv_w", "conv_b", "w_rgate", "b_rgate",
             "w_igate", "b_igate", "lru_a", "v_norm_g", "v_norm_b", "w_spatial", "b_spatial", "g_lru_out",
             "g_gmlp_out", "w_out", "g_ffn_pre", "g_ffn_post", "w_up", "ffn_conv_w", "ffn_conv_b", "w_down")
    drop = lambda a: a if a.ndim == 2 else a[0]
    w = {n: drop(args[n]) for n in names}
    m = {n: drop(args["m_" + n]) for n in names}
    v = {n: drop(args["v_" + n]) for n in names}
    xi, yi, ci = _position()
    me_arr = jnp.reshape(4 * xi + 2 * yi + ci, (1,)).astype(jnp.int32)
    chip_arr = jnp.reshape(2 * xi + yi, (1,)).astype(jnp.int32)
    c_arr = jnp.reshape(ci, (1,)).astype(jnp.int32)
    pos_arr = jnp.stack([ci, 2 * xi + yi]).astype(jnp.int32)

    big = ("w_in", "w_out", "w_up", "w_down")
    lands = _cast_place([w[n] for n in big], chip_arr)
    start_a, wait_a = _gather_plan([w[n].shape[0] for n in big[:2]])
    start_b, wait_b = _gather_plan([w[n].shape[0] for n in big[2:]])

    row0 = jnp.concatenate([c, w["conv_w"].reshape(1, -1), w["ffn_conv_w"].reshape(1, -1)], axis=1)
    g0 = _allgather8(row0, "gather_cond", False)[:, 0, :]
    c8 = g0[:, :D_MODEL]
    per_chip = g0[0::2]
    conv_w_full = per_chip[:, D_MODEL:D_MODEL + 512].reshape(N_CHIPS, 4, 128).transpose(1, 0, 2).reshape(4, 512)
    ffn_conv_w_full = per_chip[:, D_MODEL + 512:].reshape(N_CHIPS, 3, 1536).transpose(1, 0, 2).reshape(3, 2 * D_FF)
    mod_parts = _allgather8(_ada_fwd(c8, w["w_ada"]), "gather_mod", False)
    send_a, recv_a, _, lands_a, token_a = _split_start([], lands[:2], start_a, 6, mod_parts, "gather_start_a")
    send_b, recv_b, _, lands_b, token_b = _split_start([], lands[2:], start_b, 6, token_a, "gather_start_b")
    mod = _mod_select(mod_parts, w["b_ada"].reshape(1, -1), me_arr, token_b).reshape(N_MOD, D_MODEL)
    sh_m, sc_m, gt_m, sh_f, sc_f, gt_f = [mod[k:k + 1] for k in range(N_MOD)]

    small = {n: w[n] for n in SMALL_REPLICATED}
    small["conv_w"] = conv_w_full
    small["ffn_conv_w"] = ffn_conv_w_full
    row = lambda a: a.reshape(1, -1)
    seq_params, ws_t = _seq_params(small)
    glo, ggo = row(small["g_lru_out"]), row(small["g_gmlp_out"])
    g_pre, g_post = row(small["g_mix_pre"]), row(small["g_mix_post"])
    g_pre2, g_post2 = row(small["g_ffn_pre"]), row(small["g_ffn_post"])
    fw, fb = small["ffn_conv_w"], row(small["ffn_conv_b"])
    xs, tgt = x[0], loss_target[0]

    _, lands_a = _split_wait(send_a, recv_a, [], lands_a, wait_a, mod, "gather_wait_a")
    w_in4, w_out4 = _forward_to_sibling(lands_a, "forward_a")
    w_out_b = w_out4.reshape(D_MODEL, D_MODEL)
    z, h = _mix_in(xs, sc_m, sh_m, g_pre, w_in4)
    ycat, hst, stash = _seqmix(z, seq_params, glo, ggo)
    y, x1, h2 = _mix_out(ycat, xs, w_out_b, gt_m, g_post, g_pre2, sc_f, sh_f)
    _, lands_b = _split_wait(send_b, recv_b, [], lands_b, wait_b, h2, "gather_wait_b")
    w_up4, w_down4 = _forward_to_sibling(lands_b, "forward_b")
    w_down_b = w_down4.reshape(D_FF, D_MODEL)
    up0, pre, act, dy2, dx2, loss, dgt_f, dg_post2 = _ffn_fwd(h2, x1, tgt, w_up4, w_down_b, fw, fb, gt_f, g_post2)

    dup0, dfw, dfb = _ffn_bwd_a(dy2, pre, up0, w_down_b, fw)
    gw_up = _wgrad(h2, dup0, N_CHIPS, "wgrad_up", True)
    gw_down = _wgrad(act, dy2, 2, "wgrad_down", False)
    ex_start, ex_wait = _exchange_plan(2)

    def reduce_start(parts, tags, name):
        recv = _swap_halves(parts, "swap_halves_" + name)
        both = [_chip_sum(p, r, pos_arr, "chip_sum_" + t) for p, r, t in zip(parts, recv, tags)]
        sums, gath = [b[0] for b in both], [b[1] for b in both]
        return _split_start(sums, gath, ex_start, 3 * len(parts), pos_arr, "exchange_start_" + name)

    e_send_b, e_recv_b, sums_b, gath_b, token_b = reduce_start(
        [gw_up, gw_down.reshape(N_CHIPS, -1, D_MODEL)], ("w_up", "w_down"), "b")

    dx1, dy, dsh_f, dsc_f, dg_pre2, dgt_m, dg_post = _ffn_bwd_b(
        dup0, x1, y, dx2, w_up4, g_pre2, sc_f + token_b[0:1, 0:1], sh_f, gt_m, g_post)
    (dz, dcw, dcb, dwr, dwi, dbr, dbi, dspa, dng, dnb, dws, dbs_t, dglo, dggo) = _seqmix_bwd(
        z, hst, stash, dy, w_out_b, seq_params, ws_t, glo, ggo)
    grad_x, dsh_m, dsc_m, dg_pre = _mix_in_bwd(xs, dz, dx1, w_in4, g_pre, sc_m)
    gw_in = _wgrad(h, dz, N_CHIPS, "wgrad_in", True)
    gw_out = _wgrad(ycat, dy, 1, "wgrad_out", False)
    e_send_a, e_recv_a, sums_a, gath_a, token_a = reduce_start(
        [gw_in, gw_out.reshape(N_CHIPS, -1, D_MODEL)], ("w_in", "w_out"), "a")

    grads, deltas, new_m, new_v = {}, {}, {}, {}

    def reduce_finish(send, recv, sums, gath, tags, after, name):
        sums, gath = _split_wait(send, recv, sums, gath, ex_wait, after, "exchange_wait_" + name)
        gath = _swap_gathered(gath, "swap_gathered_" + name)
        for g, t in zip(gath, tags):
            grads[t], deltas[t], new_m[t], new_v[t] = _adam_gathered(w[t], g, m[t], v[t], c_arr, "adam_" + t)

    reduce_finish(e_send_b, e_recv_b, sums_b, gath_b, ("w_up", "w_down"), token_a, "b")
    reduce_finish(e_send_a, e_recv_a, sums_a, gath_a, ("w_in", "w_out"), deltas["w_down"], "a")

    dmod = jnp.concatenate([dsh_m, dsc_m, dgt_m, dsh_f, dsc_f, dgt_f], axis=1)
    dmod8 = _allgather8(dmod, "gather_dmod", False)[:, 0, :]
    g_w_ada, g_b_ada = _ada_bwd(c8, dmod8, chip_arr)
    grads["w_ada"] = g_w_ada
    deltas["w_ada"], new_m["w_ada"], new_v["w_ada"] = _adam(w["w_ada"], g_w_ada, m["w_ada"], v["w_ada"], "adam_w_ada")

    small_grads = dict(
        g_mix_pre=dg_pre, g_mix_post=dg_post, conv_w=dcw, conv_b=dcb,
        w_rgate=_diag_blocks(dwr), b_rgate=dbr.reshape(LRU_HEADS, -1),
        w_igate=_diag_blocks(dwi), b_igate=dbi.reshape(LRU_HEADS, -1), lru_a=dspa,
        v_norm_g=dng, v_norm_b=dnb, w_spatial=dws, b_spatial=dbs_t.T,
        g_lru_out=dglo, g_gmlp_out=dggo, g_ffn_pre=dg_pre2, g_ffn_post=dg_post2,
        ffn_conv_w=dfw, ffn_conv_b=dfb)
    packed_names = SMALL_REPLICATED + SMALL_COLUMN_SHARDED
    g_small = _allreduce_small(_pack([small_grads[n] for n in packed_names]), "reduce_small")
    g_small_list = _unpack(g_small, [small_grads[n].shape for n in packed_names])
    g_rep = dict(zip(packed_names, g_small_list))

    rep = SMALL_REPLICATED
    d_p, m_p, v_p = _adam(_pack([w[n] for n in rep]), _pack([g_rep[n] for n in rep]),
                          _pack([m[n] for n in rep]), _pack([v[n] for n in rep]), "adam_small")
    shapes = [w[n].shape for n in rep]
    for n, dd, mm, vv in zip(rep, _unpack(d_p, shapes), _unpack(m_p, shapes), _unpack(v_p, shapes)):
        grads[n], deltas[n], new_m[n], new_v[n] = g_rep[n], dd, mm, vv
    for n in SMALL_COLUMN_SHARDED:
        grads[n], deltas[n], new_m[n], new_v[n] = _adam_cols(w[n], g_rep[n], m[n], v[n], chip_arr, "adam_" + n)
    b2 = lambda a: a.reshape(-1, PACK_COLS)
    d_b, m_b, v_b = _adam(b2(w["b_ada"]), b2(g_b_ada), b2(m["b_ada"]), b2(v["b_ada"]), "adam_b_ada")
    grads["b_ada"], deltas["b_ada"], new_m["b_ada"], new_v["b_ada"] = g_b_ada, d_b, m_b, v_b

    total = lax.psum(loss[0, 0], ("x", "y", "c"))
    outs = [total, grad_x[None]]
    for group in (grads, deltas, new_m, new_v):
        outs.extend(group[n].reshape(args[n].shape) for n in names)
    return tuple(outs)
```

```python
import functools
import math

import jax
import jax.numpy as jnp
from jax import lax
from jax.experimental import pallas as pl
from jax.experimental.pallas import tpu as pltpu

F32 = jnp.float32
BF16 = jnp.bfloat16
MESH = pl.DeviceIdType.MESH

D_MODEL = 1024
LRU_WIDTH = 512
LRU_HEADS = 8
GMLP_WIDTH = 512
GMLP_GROUPS = 4
GMLP_BLOCK = 128
CHUNK = 64
D_FF = 3072
N_MOD = 6
EPS = 1e-6
LRU_C = 8.0
N_CHIPS = 4
N_DEV = 8

ADAM_LR = 0.001
ADAM_B1 = 0.9
ADAM_B2 = 0.999
ADAM_EPS = 1e-08
ADAM_WD = 0.01
ADAM_STEP = 10

GELU_C0 = math.sqrt(2.0 / math.pi)
GELU_C1 = 0.044715

VMEM_LIMIT_BYTES = 56 * 1024 * 1024
SUBLANES = 8
LANES = 128
BF16_SUBLANES = 16
FFN_CHUNK = 768
SUB_ROWS = 256


def _gelu_gate(x):
    x2 = x * x
    z = x * ((2.0 * GELU_C0 * GELU_C1) * x2 + 2.0 * GELU_C0)
    return 1.0 / (1.0 + jnp.exp(-z)), x2


def _gelu(x):
    t = jnp.tanh(GELU_C0 * (x + GELU_C1 * x * x * x))
    return 0.5 * x * (1.0 + t)


def _gelu_and_grad(x):
    s, x2 = _gelu_gate(x)
    g = x * s
    dz = (6.0 * GELU_C0 * GELU_C1) * x2 + 2.0 * GELU_C0
    return g, s + g * (1.0 - s) * dz


def _sigmoid(x):
    return 1.0 / (1.0 + jnp.exp(-x))


def _log1p(u):
    w = 1.0 + u
    return jnp.where(w == 1.0, u, jnp.log(w) * (u / (w - 1.0)))


def _softplus(x):
    return jnp.maximum(x, 0.0) + _log1p(jnp.exp(-jnp.abs(x)))


def _neg_expm1(x):
    u = jnp.exp(x)
    um1 = u - 1.0
    tiny = um1 == 0.0
    small = um1 * (x / jnp.log(jnp.where(tiny, 2.0, jnp.maximum(u, 0.25))))
    return -jnp.where(tiny, x, jnp.where(x < -1.0, um1, small))


def _msq_rsqrt(v):
    return lax.rsqrt(jnp.mean(v * v, axis=-1, keepdims=True) + EPS)


def _rms_bwd(dyn, yn, r):
    return r * (dyn - yn * jnp.mean(dyn * yn, axis=-1, keepdims=True))


def _colsum(v):
    return jnp.sum(v, axis=0, keepdims=True)


def _shift_down(cur, prev8, k):
    rolled = pltpu.roll(cur, k, 0)
    head = pltpu.roll(prev8, k, 0)
    row8 = lax.broadcasted_iota(jnp.int32, (SUBLANES, cur.shape[1]), 0)
    first = jnp.where(row8 < k, head, rolled[0:SUBLANES])
    return jnp.concatenate([first, rolled[SUBLANES:]], axis=0)


def _shift_up(cur, next8, k):
    t = cur.shape[0]
    rolled = pltpu.roll(cur, t - k, 0)
    tail = pltpu.roll(next8, SUBLANES - k, 0)
    row8 = lax.broadcasted_iota(jnp.int32, (SUBLANES, cur.shape[1]), 0)
    last = jnp.where(row8 >= SUBLANES - k, tail, rolled[t - SUBLANES:])
    return jnp.concatenate([rolled[:t - SUBLANES], last], axis=0)


def _scan_fwd(a, b):
    t = a.shape[0]
    row = lax.broadcasted_iota(jnp.int32, a.shape, 0)
    d = 1
    while d < t:
        keep = row >= d
        a_s = jnp.where(keep, pltpu.roll(a, d, 0), 1.0)
        b_s = jnp.where(keep, pltpu.roll(b, d, 0), 0.0)
        b = a * b_s + b
        a = a * a_s
        d *= 2
    return a, b


def _scan_bwd(a, g):
    t = a.shape[0]
    row = lax.broadcasted_iota(jnp.int32, a.shape, 0)
    d = 1
    while d < t:
        keep = row < t - d
        a_s = jnp.where(keep, pltpu.roll(a, t - d, 0), 1.0)
        g_s = jnp.where(keep, pltpu.roll(g, t - d, 0), 0.0)
        g = a * g_s + g
        a = a * a_s
        d *= 2
    return a, g


def _dot(a, b):
    return jnp.dot(a, b, preferred_element_type=F32)


def _dot_nt(a, b):
    return lax.dot_general(a, b, (((1,), (1,)), ((), ())), preferred_element_type=F32)


def _dot_tn(a, b):
    return lax.dot_general(a, b, (((0,), (0,)), ((), ())), preferred_element_type=F32)


def _rows(ts, cols, rev_of=None):
    if rev_of is None:
        return pl.BlockSpec((ts, cols), lambda i: (i, 0))
    return pl.BlockSpec((ts, cols), lambda i: (rev_of - 1 - i, 0))


def _halo_prev(ts, cols, halo, rev_of=None, col_block=0):
    per = ts // halo
    if rev_of is None:
        return pl.BlockSpec((halo, cols), lambda i: (jnp.maximum(i * per - 1, 0), col_block))
    return pl.BlockSpec((halo, cols), lambda i: (jnp.maximum((rev_of - 1 - i) * per - 1, 0), col_block))


def _full(shape):
    nd = len(shape)
    return pl.BlockSpec(shape, lambda *_: (0,) * nd)


_RESIDENT = pl.BlockSpec(memory_space=pltpu.VMEM)


def _params(sem):
    return pltpu.CompilerParams(dimension_semantics=sem, vmem_limit_bytes=VMEM_LIMIT_BYTES)


def _sds(shape, dtype):
    return jax.ShapeDtypeStruct(shape, dtype)


def _sub_tiles(ts):
    return [slice(r0, r0 + SUB_ROWS) for r0 in range(0, ts, SUB_ROWS)]


def _mix_in(x, sc, sh, g, w_in4, ts=512):
    s, d = x.shape

    def body(x_ref, sc_ref, sh_ref, g_ref, w_ref, z_ref, h_ref):
        for rs in _sub_tiles(ts):
            xv = x_ref[rs, :]
            h = (xv * _msq_rsqrt(xv) * g_ref[...]) * (1.0 + sc_ref[...]) + sh_ref[...]
            hb = h.astype(BF16)
            h_ref[rs, :] = hb
            for k in range(N_CHIPS):
                z_ref[rs, k * 512:(k + 1) * 512] = _dot(hb, w_ref[k])

    return pl.pallas_call(
        body, grid=(s // ts,), name="mix_in",
        in_specs=[_rows(ts, d), _full((1, d)), _full((1, d)), _full((1, d)), _full(w_in4.shape)],
        out_specs=[_rows(ts, 2048), _rows(ts, d)],
        out_shape=[_sds((s, 2048), F32), _sds((s, d), BF16)],
        compiler_params=_params(("parallel",)),
    )(x, sc, sh, g, w_in4)


N_STASH = 12
(ST_XC, ST_R, ST_IG, ST_A, ST_MULT, ST_GL, ST_DGL, ST_U, ST_DU, ST_Q, ST_VHAT, ST_SPB) = range(N_STASH)


def _seq_param_specs():
    return [_full((4, 512)), _full((1, 512)), _full((512, 512)), _full((512, 512)), _full((1, 512)),
            _full((1, 512)), _full((1, 512)), _full((1, 512)), _full((1, 512)), _full((4, 128, 128)),
            _full((128, 4))]


def _seqmix(z, seq_params, glo, ggo, ts=256):
    s = z.shape[0]
    nt = s // ts

    def body(z_ref, zprev_ref, cw_ref, cb_ref, bdr_ref, bdi_ref, br_ref, bi_ref, la_ref, ng_ref, nb_ref,
             ws_ref, bst_ref, glo_ref, ggo_ref, ycat_ref, hst_ref, st_ref, hcarry, sp_scr):
        i = pl.program_id(0)

        @pl.when(i == 0)
        def _():
            hcarry[...] = jnp.zeros_like(hcarry)

        lx = z_ref[:, 0:512]
        prev8 = jnp.where(i == 0, 0.0, zprev_ref[...])
        xc = (cw_ref[3:4, :] * lx + cw_ref[2:3, :] * _shift_down(lx, prev8, 1)
              + cw_ref[1:2, :] * _shift_down(lx, prev8, 2) + cw_ref[0:1, :] * _shift_down(lx, prev8, 3)
              + cb_ref[...])
        xcb = xc.astype(BF16)
        r = _sigmoid(_dot(xcb, bdr_ref[...]) + br_ref[...])
        ig = _sigmoid(_dot(xcb, bdi_ref[...]) + bi_ref[...])
        log_a = (-LRU_C) * r * _softplus(-la_ref[...])
        a = jnp.exp(log_a)
        mult = jnp.sqrt(_neg_expm1(2.0 * log_a))
        acum, hloc = _scan_fwd(a, mult * (ig * xc))
        h = hloc + acum * hcarry[...]
        hcarry[...] = h[ts - 1:ts, :]
        hst_ref[...] = h
        gl, dgl = _gelu_and_grad(z_ref[:, 512:1024])
        y_l = h * gl
        for slot, val in ((ST_XC, xc), (ST_R, r), (ST_IG, ig), (ST_A, a), (ST_MULT, mult), (ST_GL, gl),
                          (ST_DGL, dgl)):
            st_ref[slot] = val

        u, du = _gelu_and_grad(z_ref[:, 1024:1536])
        vg, dvg = _gelu_and_grad(z_ref[:, 1536:2048])
        vc = vg - jnp.mean(vg, axis=-1, keepdims=True)
        rstd = lax.rsqrt(jnp.mean(vc * vc, axis=-1, keepdims=True) + EPS)
        vhat = vc * rstd
        vb = (vhat * ng_ref[...] + nb_ref[...]).astype(BF16)
        for n in range(ts // GMLP_BLOCK):
            rs = slice(n * GMLP_BLOCK, (n + 1) * GMLP_BLOCK)
            for g in range(GMLP_GROUPS):
                cs = slice(g * 128, (g + 1) * 128)
                sp_scr[rs, cs] = _dot(ws_ref[g], vb[rs, cs]) + bst_ref[:, g:g + 1]
        spb = sp_scr[...]
        y_g = u * spb
        for slot, val in ((ST_U, u), (ST_DU, du), (ST_Q, rstd * dvg), (ST_VHAT, vhat), (ST_SPB, spb)):
            st_ref[slot] = val

        ycat_ref[:, 0:512] = (y_l * _msq_rsqrt(y_l) * glo_ref[...]).astype(BF16)
        ycat_ref[:, 512:1024] = (y_g * _msq_rsqrt(y_g) * ggo_ref[...]).astype(BF16)

    return pl.pallas_call(
        body, grid=(nt,), name="seqmix",
        in_specs=[_rows(ts, 2048), _halo_prev(ts, 512, SUBLANES)] + _seq_param_specs()
        + [_full((1, 512)), _full((1, 512))],
        out_specs=[_rows(ts, 1024), _rows(ts, 512), pl.BlockSpec((N_STASH, ts, 512), lambda i: (0, i, 0))],
        out_shape=[_sds((s, 1024), BF16), _sds((s, 512), F32), _sds((N_STASH, s, 512), F32)],
        scratch_shapes=[pltpu.VMEM((1, 512), F32), pltpu.VMEM((ts, 512), F32)],
        compiler_params=_params(("arbitrary",)),
    )(z, z, *seq_params, glo, ggo)


def _mix_out(ycat, x, w_out, gt_m, g_post, g_pre2, sc_f, sh_f, ts=512):
    s, d = x.shape

    def body(yc_ref, x_ref, w_ref, gt_ref, gp_ref, g2_ref, sc_ref, sh_ref, y_ref, x1_ref, h2_ref):
        for rs in _sub_tiles(ts):
            y = _dot(yc_ref[rs, :], w_ref[...])
            y_ref[rs, :] = y
            x1 = x_ref[rs, :] + gt_ref[...] * (y * _msq_rsqrt(y) * gp_ref[...])
            x1_ref[rs, :] = x1
            h2 = (x1 * _msq_rsqrt(x1) * g2_ref[...]) * (1.0 + sc_ref[...]) + sh_ref[...]
            h2_ref[rs, :] = h2.astype(BF16)

    vec = _full((1, d))
    return pl.pallas_call(
        body, grid=(s // ts,), name="mix_out",
        in_specs=[_rows(ts, d), _rows(ts, d), _full((d, d)), vec, vec, vec, vec, vec],
        out_specs=[_rows(ts, d), _rows(ts, d), _rows(ts, d)],
        out_shape=[_sds((s, d), F32), _sds((s, d), F32), _sds((s, d), BF16)],
        compiler_params=_params(("parallel",)),
    )(ycat, x, w_out, gt_m, g_post, g_pre2, sc_f, sh_f)


def _ffn_cols(j):
    per = (2 * D_FF // N_CHIPS) // FFN_CHUNK
    return j // per, (j % per) * FFN_CHUNK, j * FFN_CHUNK


def _ffn_fwd(h2, x1, tgt, w_up4, w_down, fw, fb, gt_f, g_post, ts=256):
    s, d = x1.shape
    nch = D_FF // FFN_CHUNK

    def body(h2_ref, x1_ref, tgt_ref, wup_ref, wdn_ref, fw_ref, fb_ref, gt_ref, gp_ref,
             up0_ref, pre_ref, act_ref, dy2_ref, dx2_ref, loss_ref, dgt_ref, dgp_ref, tail_ref):
        i = pl.program_id(0)

        @pl.when(i == 0)
        def _():
            tail_ref[...] = jnp.zeros_like(tail_ref)
            loss_ref[...] = jnp.zeros_like(loss_ref)
            dgt_ref[...] = jnp.zeros_like(dgt_ref)
            dgp_ref[...] = jnp.zeros_like(dgp_ref)

        hb = h2_ref[...]

        def up_project(j):
            sh_g, off, _ = _ffn_cols(j)
            return [_dot(hb, wup_ref[shard, :, off:off + FFN_CHUNK]).astype(BF16) for shard in (sh_g, sh_g + 2)]

        y2 = jnp.zeros((ts, d), F32)
        ahead = up_project(0)
        for j in range(nch):
            _, _, col = _ffn_cols(j)
            ubs = ahead
            if j + 1 < nch:
                ahead = up_project(j + 1)
            halves = []
            for ub, c0 in zip(ubs, (col, D_FF + col)):
                cs = slice(c0, c0 + FFN_CHUNK)
                up0_ref[:, cs] = ub
                u = ub.astype(F32)
                prev8 = tail_ref[:, cs]
                tail_ref[:, cs] = u[ts - SUBLANES:, :]
                halves.append(fw_ref[2:3, cs] * u + fw_ref[1:2, cs] * _shift_down(u, prev8, 1)
                              + fw_ref[0:1, cs] * _shift_down(u, prev8, 2) + fb_ref[:, cs])
                pre_ref[:, cs] = halves[-1].astype(BF16)
            act = (_gelu(halves[0]) * halves[1]).astype(BF16)
            act_ref[:, col:col + FFN_CHUNK] = act
            y2 = y2 + _dot(act, wdn_ref[col:col + FFN_CHUNK, :])
        r2 = _msq_rsqrt(y2)
        yn = y2 * r2
        yng = yn * gp_ref[...]
        e = x1_ref[...] + gt_ref[...] * yng - tgt_ref[...]
        loss_ref[...] += jnp.sum(e * e) * (0.5 / d)
        dx2 = e * (1.0 / d)
        dx2_ref[...] = dx2
        dgt_ref[...] += _colsum(dx2 * yng)
        dyng = dx2 * gt_ref[...]
        dgp_ref[...] += _colsum(dyng * yn)
        dy2_ref[...] = _rms_bwd(dyng * gp_ref[...], yn, r2).astype(BF16)

    vec = _full((1, d))
    return pl.pallas_call(
        body, grid=(s // ts,), name="ffn_fwd",
        in_specs=[_rows(ts, d), _rows(ts, d), _rows(ts, d), _RESIDENT, _RESIDENT,
                  _full((3, 2 * D_FF)), _full((1, 2 * D_FF)), vec, vec],
        out_specs=[_rows(ts, 2 * D_FF), _rows(ts, 2 * D_FF), _rows(ts, D_FF), _rows(ts, d), _rows(ts, d),
                   _full((1, 128)), vec, vec],
        out_shape=[_sds((s, 2 * D_FF), BF16), _sds((s, 2 * D_FF), BF16), _sds((s, D_FF), BF16), _sds((s, d), BF16),
                   _sds((s, d), F32), _sds((1, 128), F32), _sds((1, d), F32), _sds((1, d), F32)],
        scratch_shapes=[pltpu.VMEM((SUBLANES, 2 * D_FF), F32)],
        compiler_params=_params(("arbitrary",)),
    )(h2, x1, tgt, w_up4, w_down, fw, fb, gt_f, g_post)


def _shift_up_mxu(vb, up_mat, next8, k):
    t = vb.shape[0]
    main = _dot(up_mat, vb)
    tail = pltpu.roll(next8, SUBLANES - k, 0)
    row8 = lax.broadcasted_iota(jnp.int32, next8.shape, 0)
    last = main[t - SUBLANES:] + jnp.where(row8 >= SUBLANES - k, tail, 0.0)
    return jnp.concatenate([main[:t - SUBLANES], last], axis=0)


def _ffn_bwd_a(dy2, pre, up0, w_down, fw, ts=256):
    s, d = dy2.shape
    nt = s // ts
    nch = D_FF // FFN_CHUNK
    wide = 2 * D_FF
    up_mats = jnp.stack([jnp.eye(ts, k=1, dtype=BF16), jnp.eye(ts, k=2, dtype=BF16)])

    def body(dy2_ref, pre_ref, up0_ref, wdn_ref, fw_ref, um_ref, dup0_ref, dfw_ref, dfb_ref, next_ref):
        i = pl.program_id(0)

        @pl.when(i == 0)
        def _():
            next_ref[...] = jnp.zeros_like(next_ref)
            dfw_ref[...] = jnp.zeros_like(dfw_ref)
            dfb_ref[...] = jnp.zeros_like(dfb_ref)

        dyb = dy2_ref[...]
        for j in range(nch):
            _, _, col = _ffn_cols(j)
            dact = _dot_nt(dyb, wdn_ref[col:col + FFN_CHUNK, :])
            gl, dgl = _gelu_and_grad(pre_ref[:, col:col + FFN_CHUNK].astype(F32))
            dpre = (dact * pre_ref[:, D_FF + col:D_FF + col + FFN_CHUNK].astype(F32) * dgl, dact * gl)
            for half, c0 in enumerate((col, D_FF + col)):
                cs = slice(c0, c0 + FFN_CHUNK)
                dp = dpre[half]
                dpb = dp.astype(BF16)
                nxt = next_ref[:, cs]
                next_ref[:, cs] = dpb.astype(F32)[0:SUBLANES, :]
                su1 = _shift_up_mxu(dpb, um_ref[0], nxt, 1)
                su2 = _shift_up_mxu(dpb, um_ref[1], nxt, 2)
                u = up0_ref[:, cs].astype(F32)
                dfb_ref[:, cs] += _colsum(dp)
                dfw_ref[2:3, cs] += _colsum(dp * u)
                dfw_ref[1:2, cs] += _colsum(su1 * u)
                dfw_ref[0:1, cs] += _colsum(su2 * u)
                dup0 = fw_ref[2:3, cs] * dp + fw_ref[1:2, cs] * su1 + fw_ref[0:1, cs] * su2
                dup0_ref[:, cs] = dup0.astype(BF16)

    return pl.pallas_call(
        body, grid=(nt,), name="ffn_bwd_a",
        in_specs=[_rows(ts, d, nt), _rows(ts, wide, nt), _rows(ts, wide, nt), _RESIDENT,
                  _full((3, wide)), _full((2, ts, ts))],
        out_specs=[_rows(ts, wide, nt), _full((3, wide)), _full((1, wide))],
        out_shape=[_sds((s, wide), BF16), _sds((3, wide), F32), _sds((1, wide), F32)],
        scratch_shapes=[pltpu.VMEM((SUBLANES, wide), F32)],
        compiler_params=_params(("arbitrary",)),
    )(dy2, pre, up0, w_down, fw, up_mats)


def _ffn_bwd_b(dup0, x1, y, dx2, w_up4, g_pre2, sc_f, sh_f, gt_m, g_post_m, ts=512):
    s, d = x1.shape
    shard_cols = 2 * D_FF // N_CHIPS

    def body(dup_ref, x1_ref, y_ref, dx2_ref, wup_ref, g2_ref, sc_ref, sh_ref, gt_ref, gp_ref,
             dx1_ref, dy_ref, dsh_ref, dsc_ref, dg2_ref, dgt_ref, dgp_ref):
        i = pl.program_id(0)

        @pl.when(i == 0)
        def _():
            for ref in (dsh_ref, dsc_ref, dg2_ref, dgt_ref, dgp_ref):
                ref[...] = jnp.zeros_like(ref)

        for rs in _sub_tiles(ts):
            dh2 = jnp.zeros((SUB_ROWS, d), F32)
            for k in range(N_CHIPS):
                dh2 = dh2 + _dot_nt(dup_ref[rs, k * shard_cols:(k + 1) * shard_cols], wup_ref[k])
            x1v = x1_ref[rs, :]
            r2 = _msq_rsqrt(x1v)
            xn = x1v * r2
            hn = xn * g2_ref[...]
            dsh_ref[...] += _colsum(dh2)
            dsc_ref[...] += _colsum(dh2 * hn)
            dhn = dh2 * (1.0 + sc_ref[...])
            dg2_ref[...] += _colsum(dhn * xn)
            dx1 = dx2_ref[rs, :] + _rms_bwd(dhn * g2_ref[...], xn, r2)
            dx1_ref[rs, :] = dx1
            yv = y_ref[rs, :]
            ry = _msq_rsqrt(yv)
            yn = yv * ry
            dgt_ref[...] += _colsum(dx1 * (yn * gp_ref[...]))
            dyng = dx1 * gt_ref[...]
            dgp_ref[...] += _colsum(dyng * yn)
            dy_ref[rs, :] = _rms_bwd(dyng * gp_ref[...], yn, ry).astype(BF16)

    vec = _full((1, d))
    return pl.pallas_call(
        body, grid=(s // ts,), name="ffn_bwd_b",
        in_specs=[_rows(ts, 2 * D_FF), _rows(ts, d), _rows(ts, d), _rows(ts, d), _RESIDENT,
                  vec, vec, vec, vec, vec],
        out_specs=[_rows(ts, d), _rows(ts, d), vec, vec, vec, vec, vec],
        out_shape=[_sds((s, d), F32), _sds((s, d), BF16)] + [_sds((1, d), F32)] * 5,
        compiler_params=_params(("arbitrary",)),
    )(dup0, x1, y, dx2, w_up4, g_pre2, sc_f, sh_f, gt_m, g_post_m)


def _seqmix_bwd(z, hst, stash, dy, w_out, seq_params, ws_t, glo, ggo, ts=256):
    s = z.shape[0]
    nt = s // ts
    small_shapes = [(4, 512), (1, 512), (512, 512), (512, 512), (1, 512), (1, 512), (1, 512),
                    (1, 512), (1, 512), (4, 128, 128), (128, 4), (1, 512), (1, 512)]

    def body(lx_ref, hst_ref, hprev_ref, st_ref, dy_ref, wout_ref, cw_ref, cb_ref, bdr_ref, bdi_ref, br_ref,
             bi_ref, la_ref, ng_ref, nb_ref, ws_ref, bst_ref, wst_ref, glo_ref, ggo_ref, dz_ref, *rest):
        small_refs = rest[:13]
        (dcw_ref, dcb_ref, dwr_ref, dwi_ref, dbr_ref, dbi_ref, dspa_ref, dng_ref, dnb_ref, dws_ref, dbs_ref,
         dglo_ref, dggo_ref) = small_refs
        gcarry, anext, dxcnext, dv_scr = rest[13:]
        i = pl.program_id(0)

        @pl.when(i == 0)
        def _():
            for ref in small_refs:
                ref[...] = jnp.zeros_like(ref)
            gcarry[...] = jnp.zeros_like(gcarry)
            anext[...] = jnp.ones_like(anext)
            dxcnext[...] = jnp.zeros_like(dxcnext)

        first_tile = i == nt - 1
        xc, r, ig, a, mult = st_ref[ST_XC], st_ref[ST_R], st_ref[ST_IG], st_ref[ST_A], st_ref[ST_MULT]
        gl, u, spb, vhat = st_ref[ST_GL], st_ref[ST_U], st_ref[ST_SPB], st_ref[ST_VHAT]
        lx = lx_ref[...]
        h = hst_ref[...]
        hprev = _shift_down(h, jnp.where(first_tile, 0.0, hprev_ref[...]), 1)
        y_l = h * gl
        y_g = u * spb

        dycat = _dot_nt(dy_ref[...], wout_ref[...])
        rl = _msq_rsqrt(y_l)
        yln = y_l * rl
        dyl = dycat[:, 0:512]
        dglo_ref[...] += _colsum(dyl * yln)
        dy_l = _rms_bwd(dyl * glo_ref[...], yln, rl)
        rg = _msq_rsqrt(y_g)
        ygn = y_g * rg
        dyg = dycat[:, 512:1024]
        dggo_ref[...] += _colsum(dyg * ygn)
        dy_g = _rms_bwd(dyg * ggo_ref[...], ygn, rg)

        dz_ref[:, 512:1024] = (dy_l * h * st_ref[ST_DGL]).astype(BF16)
        a_up = _shift_up(a, anext[...], 1)
        acum, gloc = _scan_bwd(a_up, dy_l * gl)
        gg = gloc + acum * gcarry[...]
        gcarry[...] = gg[0:1, :]
        anext[...] = a[0:SUBLANES, :]
        da = gg * hprev
        t1 = gg * mult
        di = t1 * xc
        dxc = t1 * ig
        dmult = gg * ig * xc
        dla = da * a - dmult * (a * a / mult)
        dspa_ref[...] += _colsum(dla * r) * (-LRU_C)
        dpr = dla * ((-LRU_C) * _softplus(-la_ref[...])) * r * (1.0 - r)
        dpi = di * ig * (1.0 - ig)
        dbr_ref[...] += _colsum(dpr)
        dbi_ref[...] += _colsum(dpi)
        dprb = dpr.astype(BF16)
        dpib = dpi.astype(BF16)
        xcb = xc.astype(BF16)
        dwr_ref[...] += _dot_tn(xcb, dprb)
        dwi_ref[...] += _dot_tn(xcb, dpib)
        dxc = dxc + _dot_nt(dprb, bdr_ref[...]) + _dot_nt(dpib, bdi_ref[...])
        nxt = dxcnext[...]
        dxcnext[...] = dxc[0:SUBLANES, :]
        up1, up2, up3 = _shift_up(dxc, nxt, 1), _shift_up(dxc, nxt, 2), _shift_up(dxc, nxt, 3)
        dcb_ref[...] += _colsum(dxc)
        dcw_ref[3:4, :] += _colsum(dxc * lx)
        dcw_ref[2:3, :] += _colsum(up1 * lx)
        dcw_ref[1:2, :] += _colsum(up2 * lx)
        dcw_ref[0:1, :] += _colsum(up3 * lx)
        dlx = cw_ref[3:4, :] * dxc + cw_ref[2:3, :] * up1 + cw_ref[1:2, :] * up2 + cw_ref[0:1, :] * up3
        dz_ref[:, 0:512] = dlx.astype(BF16)

        dz_ref[:, 1024:1536] = (dy_g * spb * st_ref[ST_DU]).astype(BF16)
        dsp = dy_g * u
        vb = (vhat * ng_ref[...] + nb_ref[...]).astype(BF16)
        for n in range(ts // GMLP_BLOCK):
            rs = slice(n * GMLP_BLOCK, (n + 1) * GMLP_BLOCK)
            for g in range(GMLP_GROUPS):
                cs = slice(g * 128, (g + 1) * 128)
                dbs_ref[:, g:g + 1] += jnp.sum(dsp[rs, cs], axis=1, keepdims=True)
                blk = dsp[rs, cs].astype(BF16)
                dws_ref[g] += _dot_nt(blk, vb[rs, cs])
                dv_scr[rs, cs] = _dot(wst_ref[g], blk)
        dv = dv_scr[...]
        dng_ref[...] += _colsum(dv * vhat)
        dnb_ref[...] += _colsum(dv)
        dvh = dv * ng_ref[...]
        dvg = dvh - jnp.mean(dvh, axis=-1, keepdims=True) - vhat * jnp.mean(dvh * vhat, axis=-1, keepdims=True)
        dz_ref[:, 1536:2048] = (dvg * st_ref[ST_Q]).astype(BF16)

        @pl.when(i == nt - 1)
        def _():
            pos = lax.broadcasted_iota(jnp.int32, (GMLP_BLOCK, GMLP_BLOCK), 0) // CHUNK
            src = lax.broadcasted_iota(jnp.int32, (GMLP_BLOCK, GMLP_BLOCK), 1) // CHUNK
            for g in range(GMLP_GROUPS):
                dws_ref[g] = jnp.where(src <= pos, dws_ref[g], 0.0)
            dspa_ref[...] = dspa_ref[...] * (-_sigmoid(-la_ref[...]))

    in_specs = ([_rows(ts, 512, nt), _rows(ts, 512, nt), _halo_prev(ts, 512, SUBLANES, nt),
                 pl.BlockSpec((N_STASH, ts, 512), lambda i: (0, nt - 1 - i, 0)), _rows(ts, 1024, nt),
                 _full((1024, 1024))]
                + _seq_param_specs() + [_full((4, 128, 128)), _full((1, 512)), _full((1, 512))])
    return pl.pallas_call(
        body, grid=(nt,), name="seqmix_bwd",
        in_specs=in_specs,
        out_specs=[_rows(ts, 2048, nt)] + [_full(sh) for sh in small_shapes],
        out_shape=[_sds((s, 2048), BF16)] + [_sds(sh, F32) for sh in small_shapes],
        scratch_shapes=[pltpu.VMEM((1, 512), F32), pltpu.VMEM((SUBLANES, 512), F32),
                        pltpu.VMEM((SUBLANES, 512), F32), pltpu.VMEM((ts, 512), F32)],
        compiler_params=_params(("arbitrary",)),
    )(z, hst, hst, stash, dy, w_out, *seq_params, ws_t, glo, ggo)


def _seqmix_bwd_recomputing_unused(z, hst, dy, w_out, seq_params, ws_t, glo, ggo, ts=256):
    s = z.shape[0]
    nt = s // ts
    small_shapes = [(4, 512), (1, 512), (512, 512), (512, 512), (1, 512), (1, 512), (1, 512),
                    (1, 512), (1, 512), (4, 128, 128), (128, 4), (1, 512), (1, 512)]

    def body(z_ref, zprev_ref, hst_ref, hprev_ref, dy_ref, wout_ref, *rest):
        p = rest[:11]
        wst_ref, glo_ref, ggo_ref = rest[11:14]
        dz_ref = rest[14]
        (dcw_ref, dcb_ref, dwr_ref, dwi_ref, dbr_ref, dbi_ref, dspa_ref, dng_ref, dnb_ref, dws_ref, dbs_ref,
         dglo_ref, dggo_ref) = rest[15:28]
        gcarry, anext, dxcnext, sp_scr, dv_scr = rest[28:]
        i = pl.program_id(0)

        @pl.when(i == 0)
        def _():
            for ref in rest[15:28]:
                ref[...] = jnp.zeros_like(ref)
            gcarry[...] = jnp.zeros_like(gcarry)
            anext[...] = jnp.ones_like(anext)
            dxcnext[...] = jnp.zeros_like(dxcnext)

        first_tile = i == nt - 1
        f = _seq_recompute(z_ref, zprev_ref, first_tile, p)
        xc, r, ig, a, mult, lx = f["xc"], f["r"], f["ig"], f["a"], f["mult"], f["lx"]
        h = hst_ref[...]
        hprev = _shift_down(h, jnp.where(first_tile, 0.0, hprev_ref[...]), 1)
        gl, dgl = _gelu_and_grad(f["lg"])
        y_l = h * gl
        gm = _gmlp_fwd(f["gu"], f["gv"], p[7], p[8], p[9], p[10], sp_scr)
        y_g = gm["y_g"]

        dycat = _dot_nt(dy_ref[...], wout_ref[...])
        rl = _msq_rsqrt(y_l)
        yln = y_l * rl
        dyl = dycat[:, 0:512]
        dglo_ref[...] += _colsum(dyl * yln)
        dy_l = _rms_bwd(dyl * glo_ref[...], yln, rl)
        rg = _msq_rsqrt(y_g)
        ygn = y_g * rg
        dyg = dycat[:, 512:1024]
        dggo_ref[...] += _colsum(dyg * ygn)
        dy_g = _rms_bwd(dyg * ggo_ref[...], ygn, rg)

        dz_ref[:, 512:1024] = (dy_l * h * dgl).astype(BF16)
        a_up = _shift_up(a, anext[...], 1)
        acum, gloc = _scan_bwd(a_up, dy_l * gl)
        gg = gloc + acum * gcarry[...]
        gcarry[...] = gg[0:1, :]
        anext[...] = a[0:SUBLANES, :]
        da = gg * hprev
        t1 = gg * mult
        di = t1 * xc
        dxc = t1 * ig
        dmult = gg * ig * xc
        dla = da * a - dmult * (a * a / mult)
        spa = f["spa"]
        dspa_ref[...] += _colsum(dla * r) * (-LRU_C)
        dpr = dla * ((-LRU_C) * spa) * r * (1.0 - r)
        dpi = di * ig * (1.0 - ig)
        dbr_ref[...] += _colsum(dpr)
        dbi_ref[...] += _colsum(dpi)
        dprb = dpr.astype(BF16)
        dpib = dpi.astype(BF16)
        dwr_ref[...] += _dot_tn(f["xcb"], dprb)
        dwi_ref[...] += _dot_tn(f["xcb"], dpib)
        dxc = dxc + _dot_nt(dprb, p[2][...]) + _dot_nt(dpib, p[3][...])
        dcb_ref[...] += _colsum(dxc)
        dcw_ref[3:4, :] += _colsum(dxc * lx)
        dcw_ref[2:3, :] += _colsum(dxc * f["s1"])
        dcw_ref[1:2, :] += _colsum(dxc * f["s2"])
        dcw_ref[0:1, :] += _colsum(dxc * f["s3"])
        nxt = dxcnext[...]
        dxcnext[...] = dxc[0:SUBLANES, :]
        cw_ref = p[0]
        dlx = (cw_ref[3:4, :] * dxc + cw_ref[2:3, :] * _shift_up(dxc, nxt, 1)
               + cw_ref[1:2, :] * _shift_up(dxc, nxt, 2) + cw_ref[0:1, :] * _shift_up(dxc, nxt, 3))
        dz_ref[:, 0:512] = dlx.astype(BF16)

        dz_ref[:, 1024:1536] = (dy_g * gm["spb"] * gm["du"]).astype(BF16)
        dsp = dy_g * gm["u"]
        vb = gm["vb"]
        for n in range(ts // GMLP_BLOCK):
            rs = slice(n * GMLP_BLOCK, (n + 1) * GMLP_BLOCK)
            for g in range(GMLP_GROUPS):
                cs = slice(g * 128, (g + 1) * 128)
                dbs_ref[:, g:g + 1] += jnp.sum(dsp[rs, cs], axis=1, keepdims=True)
                blk = dsp[rs, cs].astype(BF16)
                dws_ref[g] += _dot_nt(blk, vb[rs, cs])
                dv_scr[rs, cs] = _dot(wst_ref[g], blk)
        dv = dv_scr[...]
        vhat = gm["vhat"]
        dng_ref[...] += _colsum(dv * vhat)
        dnb_ref[...] += _colsum(dv)
        dvh = dv * p[7][...]
        dvg = gm["rstd"] * (dvh - jnp.mean(dvh, axis=-1, keepdims=True)
                            - vhat * jnp.mean(dvh * vhat, axis=-1, keepdims=True))
        dz_ref[:, 1536:2048] = (dvg * gm["dvg"]).astype(BF16)

        @pl.when(i == nt - 1)
        def _():
            pos = lax.broadcasted_iota(jnp.int32, (GMLP_BLOCK, GMLP_BLOCK), 0) // CHUNK
            src = lax.broadcasted_iota(jnp.int32, (GMLP_BLOCK, GMLP_BLOCK), 1) // CHUNK
            for g in range(GMLP_GROUPS):
                dws_ref[g] = jnp.where(src <= pos, dws_ref[g], 0.0)
            dspa_ref[...] = dspa_ref[...] * (-_sigmoid(-p[6][...]))

    in_specs = (_seq_specs(ts, nt, True)
                + [_rows(ts, 512, nt), _halo_prev(ts, 512, SUBLANES, nt), _rows(ts, 1024, nt), _full((1024, 1024))]
                + _seq_param_specs() + [_full((4, 128, 128)), _full((1, 512)), _full((1, 512))])
    return pl.pallas_call(
        body, grid=(nt,), name="seqmix_bwd",
        in_specs=in_specs,
        out_specs=[_rows(ts, 2048, nt)] + [_full(sh) for sh in small_shapes],
        out_shape=[_sds((s, 2048), BF16)] + [_sds(sh, F32) for sh in small_shapes],
        scratch_shapes=[pltpu.VMEM((1, 512), F32), pltpu.VMEM((SUBLANES, 512), F32),
                        pltpu.VMEM((SUBLANES, 512), F32), pltpu.VMEM((ts, 512), F32), pltpu.VMEM((ts, 512), F32)],
        compiler_params=_params(("arbitrary",)),
    )(z, z, hst, hst, dy, w_out, *seq_params, ws_t, glo, ggo)


def _mix_in_bwd(x, dz, dx1, w_in4, g, sc, ts=512):
    s, d = x.shape

    def body(x_ref, dz_ref, dx1_ref, w_ref, g_ref, sc_ref, gx_ref, dsh_ref, dsc_ref, dg_ref):
        i = pl.program_id(0)

        @pl.when(i == 0)
        def _():
            for ref in (dsh_ref, dsc_ref, dg_ref):
                ref[...] = jnp.zeros_like(ref)

        for rs in _sub_tiles(ts):
            dh = jnp.zeros((SUB_ROWS, d), F32)
            for k in range(N_CHIPS):
                dh = dh + _dot_nt(dz_ref[rs, k * 512:(k + 1) * 512], w_ref[k])
            xv = x_ref[rs, :]
            r = _msq_rsqrt(xv)
            xn = xv * r
            dsh_ref[...] += _colsum(dh)
            dsc_ref[...] += _colsum(dh * (xn * g_ref[...]))
            dhn = dh * (1.0 + sc_ref[...])
            dg_ref[...] += _colsum(dhn * xn)
            gx_ref[rs, :] = dx1_ref[rs, :] + _rms_bwd(dhn * g_ref[...], xn, r)

    vec = _full((1, d))
    return pl.pallas_call(
        body, grid=(s // ts,), name="mix_in_bwd",
        in_specs=[_rows(ts, d), _rows(ts, 2048), _rows(ts, d), _full(w_in4.shape), vec, vec],
        out_specs=[_rows(ts, d), vec, vec, vec],
        out_shape=[_sds((s, d), F32)] + [_sds((1, d), F32)] * 3,
        compiler_params=_params(("arbitrary",)),
    )(x, dz, dx1, w_in4, g, sc)


def _wgrad(a, b, n_chunks, name, chunk_major, ts=2048):
    s, m = a.shape
    n = b.shape[1]
    nc = n // n_chunks
    nt = s // ts

    def body(a_ref, b_ref, o_ref, acc):
        i = pl.program_id(1)

        @pl.when(i == 0)
        def _():
            acc[...] = jnp.zeros_like(acc)

        acc[...] += _dot_tn(a_ref[...], b_ref[...])

        @pl.when(i == nt - 1)
        def _():
            if chunk_major:
                o_ref[0] = acc[...].astype(BF16)
            else:
                o_ref[...] = acc[...].astype(BF16)

    if chunk_major:
        out_spec, out_shape = pl.BlockSpec((1, m, nc), lambda c, i: (c, 0, 0)), _sds((n_chunks, m, nc), BF16)
    else:
        out_spec, out_shape = pl.BlockSpec((m, nc), lambda c, i: (0, c)), _sds((m, n), BF16)
    return pl.pallas_call(
        body, grid=(n_chunks, nt), name=name,
        in_specs=[pl.BlockSpec((ts, m), lambda c, i: (i, 0)), pl.BlockSpec((ts, nc), lambda c, i: (i, c))],
        out_specs=out_spec,
        out_shape=out_shape,
        scratch_shapes=[pltpu.VMEM((m, nc), F32)],
        compiler_params=_params(("parallel", "arbitrary")),
    )(a, b)


def _block_diag(w):
    heads, hd, _ = w.shape
    eye = jnp.eye(heads, dtype=w.dtype)
    return (eye[:, None, :, None] * w[:, :, None, :]).reshape(heads * hd, heads * hd)


def _diag_blocks(m):
    hd = LRU_WIDTH // LRU_HEADS
    m4 = m.reshape(LRU_HEADS, hd, LRU_HEADS, hd)
    return jnp.stack([m4[k, :, k, :] for k in range(LRU_HEADS)])


def _seq_params(small):
    row = lambda v: v.reshape(1, -1)
    pos = jnp.arange(GMLP_BLOCK)
    mask = (pos[None, :] // CHUNK) <= (pos[:, None] // CHUNK)
    ws = jnp.where(mask[None], small["w_spatial"], 0.0)
    seq_params = (small["conv_w"], row(small["conv_b"]),
                  _block_diag(small["w_rgate"]).astype(BF16), _block_diag(small["w_igate"]).astype(BF16),
                  row(small["b_rgate"]), row(small["b_igate"]), row(small["lru_a"]),
                  row(small["v_norm_g"]), row(small["v_norm_b"]), ws.astype(BF16), small["b_spatial"].T)
    return seq_params, jnp.swapaxes(ws, 1, 2).astype(BF16)


_ANY = pl.BlockSpec(memory_space=pl.ANY)
_CHIP_FLIPS = ((1, 0), (0, 1), (1, 1))


def _position():
    return lax.axis_index("x"), lax.axis_index("y"), lax.axis_index("c")


def _flip(v, f):
    return 1 - v if f else v


def _remote(src, dst, send_sem, recv_sem, peer):
    return pltpu.make_async_remote_copy(src_ref=src, dst_ref=dst, send_sem=send_sem, recv_sem=recv_sem,
                                        device_id=peer, device_id_type=MESH)


def _allgather8(block, name, reduce):
    r, n = block.shape

    def body(x_ref, out_ref, *scratch):
        if reduce:
            gath, send_sems, recv_sems, loc_sem = scratch
        else:
            gath = out_ref
            send_sems, recv_sems, loc_sem = scratch
        x, y, c = _position()
        me = 4 * x + 2 * y + c
        loc = pltpu.make_async_copy(x_ref, gath.at[me], loc_sem)
        loc.start()
        peers = []
        for k in range(1, N_DEV):
            px, py, pc = _flip(x, k & 4), _flip(y, k & 2), _flip(c, k & 1)
            peers.append((px, py, pc))
            _remote(x_ref, gath.at[me], send_sems.at[k - 1], recv_sems.at[k - 1], (px, py, pc)).start()
        for k, (px, py, pc) in enumerate(peers):
            src = 4 * px + 2 * py + pc
            _remote(x_ref, gath.at[src], send_sems.at[k], recv_sems.at[k], (px, py, pc)).wait_recv()
        for k, peer in enumerate(peers):
            _remote(x_ref, gath.at[me], send_sems.at[k], recv_sems.at[k], peer).wait_send()
        loc.wait()
        if reduce:
            acc = gath[0]
            for k in range(1, N_DEV):
                acc = acc + gath[k]
            out_ref[...] = acc

    sems = [pltpu.SemaphoreType.DMA((N_DEV - 1,)), pltpu.SemaphoreType.DMA((N_DEV - 1,)), pltpu.SemaphoreType.DMA]
    if reduce:
        out_shape = _sds((r, n), F32)
        scratch = [pltpu.VMEM((N_DEV, r, n), F32)] + sems
    else:
        out_shape = _sds((N_DEV, r, n), F32)
        scratch = sems
    return pl.pallas_call(
        body, name=name, out_shape=out_shape,
        in_specs=[pl.BlockSpec(memory_space=pltpu.VMEM)], out_specs=pl.BlockSpec(memory_space=pltpu.VMEM),
        scratch_shapes=scratch,
        compiler_params=pltpu.CompilerParams(vmem_limit_bytes=VMEM_LIMIT_BYTES),
    )(block)


def _half(ref, c, rows):
    hr = rows // 2
    return ref.at[pl.ds(pl.multiple_of(c * hr, BF16_SUBLANES), hr), :]


def _gather_weights(shards):
    na = len(shards)

    def body(*refs):
        ins, outs = refs[:na], refs[na:2 * na]
        ici_send, ici_recv, d2d_send, d2d_recv, loc_sem = refs[2 * na:]
        x, y, c = _position()
        chip = 2 * x + y
        sibling = (x, y, 1 - c)
        local = []
        for a in range(na):
            local.append(pltpu.make_async_copy(ins[a], outs[a].at[chip], loc_sem.at[a]))
            local[-1].start()
        sends = []
        for a in range(na):
            rows = shards[a].shape[0]
            for j, (fx, fy) in enumerate(_CHIP_FLIPS):
                peer = (_flip(x, fx), _flip(y, fy), c)
                sends.append(_remote(_half(ins[a], c, rows), _half(outs[a].at[chip], c, rows),
                                     ici_send.at[a * 3 + j], ici_recv.at[a * 3 + j], peer))
                sends[-1].start()
        for a in range(na):
            rows = shards[a].shape[0]
            for j, (fx, fy) in enumerate(_CHIP_FLIPS):
                src_chip = 2 * _flip(x, fx) + _flip(y, fy)
                landed = _half(outs[a].at[src_chip], c, rows)
                _remote(landed, landed, ici_send.at[a * 3 + j], ici_recv.at[a * 3 + j], sibling).wait_recv()
                sends.append(_remote(landed, landed, d2d_send.at[a * 3 + j], d2d_recv.at[a * 3 + j], sibling))
                sends[-1].start()
        for a in range(na):
            rows = shards[a].shape[0]
            for j, (fx, fy) in enumerate(_CHIP_FLIPS):
                src_chip = 2 * _flip(x, fx) + _flip(y, fy)
                other = _half(outs[a].at[src_chip], 1 - c, rows)
                _remote(other, other, d2d_send.at[a * 3 + j], d2d_recv.at[a * 3 + j], sibling).wait_recv()
        for cp in sends:
            cp.wait_send()
        for cp in local:
            cp.wait()

    return pl.pallas_call(
        body, name="gather_weights",
        out_shape=[_sds((N_CHIPS,) + w.shape, w.dtype) for w in shards],
        in_specs=[_ANY] * na, out_specs=[_ANY] * na,
        scratch_shapes=[pltpu.SemaphoreType.DMA((3 * na,))] * 4 + [pltpu.SemaphoreType.DMA((na,))],
    )(*shards)


def _swap_halves(parts, name):
    na = len(parts)

    def body(*refs):
        ins, outs = refs[:na], refs[na:2 * na]
        send_sems, recv_sems = refs[2 * na:]
        x, y, c = _position()
        sibling = (x, y, 1 - c)
        cps = []
        for a in range(na):
            hr = parts[a].shape[1] // 2
            src = ins[a].at[:, pl.ds(pl.multiple_of((1 - c) * hr, BF16_SUBLANES), hr), :]
            cps.append(_remote(src, outs[a], send_sems.at[a], recv_sems.at[a], sibling))
            cps[-1].start()
        for cp in cps:
            cp.wait()

    return pl.pallas_call(
        body, name=name,
        out_shape=[_sds((N_CHIPS, p.shape[1] // 2, p.shape[2]), p.dtype) for p in parts],
        in_specs=[_ANY] * na, out_specs=[_ANY] * na,
        scratch_shapes=[pltpu.SemaphoreType.DMA((na,))] * 2,
    )(*parts)


def _chip_sum(part, recv, pos_arr, name):
    _, rows, cols = part.shape
    hr = rows // 2

    def body(pos_ref, p_ref, r_ref, o_ref, g_ref):
        total = (p_ref[...].astype(F32) + r_ref[...].astype(F32)).astype(BF16)
        o_ref[...] = total

        @pl.when(pl.program_id(0) == pos_ref[1])
        def _():
            g_ref[0] = total

    grid_spec = pltpu.PrefetchScalarGridSpec(
        num_scalar_prefetch=1, grid=(N_CHIPS,),
        in_specs=[pl.BlockSpec((1, hr, cols), lambda k, pos: (k, pos[0], 0)),
                  pl.BlockSpec((1, hr, cols), lambda k, pos: (k, 0, 0))],
        out_specs=[pl.BlockSpec((1, hr, cols), lambda k, pos: (k, 0, 0)),
                   pl.BlockSpec((1, 1, hr, cols), lambda k, pos: (0, pos[1], 0, 0))])
    return pl.pallas_call(
        body, name=name, grid_spec=grid_spec,
        out_shape=[_sds((N_CHIPS, hr, cols), BF16), _sds((2, N_CHIPS, hr, cols), BF16)],
        compiler_params=_params(("arbitrary",)),
    )(pos_arr, part, recv)


def _exchange_chips(sums):
    na = len(sums)

    def body(*refs):
        ins, outs = refs[:na], refs[na:2 * na]
        send_sems, recv_sems, loc_sem = refs[2 * na:]
        x, y, c = _position()
        chip = 2 * x + y
        local = []
        for a in range(na):
            local.append(pltpu.make_async_copy(ins[a].at[chip], outs[a].at[chip], loc_sem.at[a]))
            local[-1].start()
        cps = []
        for a in range(na):
            for j, (fx, fy) in enumerate(_CHIP_FLIPS):
                px, py = _flip(x, fx), _flip(y, fy)
                cps.append(_remote(ins[a].at[2 * px + py], outs[a].at[chip],
                                   send_sems.at[a * 3 + j], recv_sems.at[a * 3 + j], (px, py, c)))
                cps[-1].start()
        for a in range(na):
            for j, (fx, fy) in enumerate(_CHIP_FLIPS):
                src_chip = 2 * _flip(x, fx) + _flip(y, fy)
                landed = outs[a].at[src_chip]
                _remote(landed, landed, send_sems.at[a * 3 + j], recv_sems.at[a * 3 + j], (x, y, c)).wait_recv()
        for cp in cps:
            cp.wait_send()
        for cp in local:
            cp.wait()

    return pl.pallas_call(
        body, name="exchange_chips",
        out_shape=[_sds(s.shape, s.dtype) for s in sums],
        in_specs=[_ANY] * na, out_specs=[_ANY] * na,
        scratch_shapes=[pltpu.SemaphoreType.DMA((3 * na,))] * 2 + [pltpu.SemaphoreType.DMA((na,))],
    )(*sums)


def _sum_chips(gath, name, tr=128):
    _, hr, cols = gath.shape
    tr = min(tr, hr)

    def body(g_ref, o_ref):
        acc = g_ref[0].astype(F32)
        for k in range(1, N_CHIPS):
            acc = acc + g_ref[k].astype(F32)
        o_ref[...] = acc

    return pl.pallas_call(
        body, name=name, grid=(hr // tr,),
        in_specs=[pl.BlockSpec((N_CHIPS, tr, cols), lambda i: (0, i, 0))],
        out_specs=pl.BlockSpec((tr, cols), lambda i: (i, 0)),
        out_shape=_sds((hr, cols), F32),
        compiler_params=_params(("parallel",)),
    )(gath)


def _join_halves(halves):
    na = len(halves)

    def body(*refs):
        ins, outs = refs[:na], refs[na:2 * na]
        send_sems, recv_sems, loc_sem = refs[2 * na:]
        x, y, c = _position()
        sibling = (x, y, 1 - c)
        cps, local = [], []
        for a in range(na):
            rows = 2 * halves[a].shape[0]
            mine = _half(outs[a], c, rows)
            local.append(pltpu.make_async_copy(ins[a], mine, loc_sem.at[a]))
            local[-1].start()
            cps.append(_remote(ins[a], mine, send_sems.at[a], recv_sems.at[a], sibling))
            cps[-1].start()
        for a in range(na):
            rows = 2 * halves[a].shape[0]
            other = _half(outs[a], 1 - c, rows)
            _remote(ins[a], other, send_sems.at[a], recv_sems.at[a], sibling).wait_recv()
        for cp in cps:
            cp.wait_send()
        for cp in local:
            cp.wait()

    return pl.pallas_call(
        body, name="join_halves",
        out_shape=[_sds((2 * h.shape[0], h.shape[1]), h.dtype) for h in halves],
        in_specs=[_ANY] * na, out_specs=[_ANY] * na,
        scratch_shapes=[pltpu.SemaphoreType.DMA((na,))] * 3,
    )(*halves)


_HBM = pl.BlockSpec(memory_space=pltpu.HBM)
_SEM = pl.BlockSpec(memory_space=pltpu.SEMAPHORE)
_EFFECT = pltpu.SideEffectType.DATAFLOW_SIDE_EFFECTING


def _in_hbm(a):
    return pltpu.with_memory_space_constraint(a, pltpu.HBM)


def _split_start(srcs, lands, plan, n_copies, after, name):
    ns, nl = len(srcs), len(lands)
    bufs = list(srcs) + list(lands)

    def body(*refs):
        send_sems, recv_sems = refs[ns + nl + 1], refs[ns + nl + 2]
        token = refs[-1]
        for k, (src, dst, peer) in enumerate(plan(refs[:ns], refs[ns:ns + nl])):
            _remote(src, dst, send_sems.at[k], recv_sems.at[k], peer).start()
        token[...] = jnp.zeros_like(token)

    out = pl.pallas_call(
        body, name=name,
        out_shape=(pltpu.SemaphoreType.DMA((n_copies,)), pltpu.SemaphoreType.DMA((n_copies,)),
                   *[pltpu.HBM(b.shape, b.dtype) for b in bufs], _sds((SUBLANES, 128), F32)),
        in_specs=[_HBM] * (ns + nl) + [_ANY],
        out_specs=(_SEM, _SEM, *[_HBM] * (ns + nl), pl.BlockSpec(memory_space=pltpu.VMEM)),
        input_output_aliases={i: 2 + i for i in range(ns + nl)},
        compiler_params=pltpu.CompilerParams(has_side_effects=_EFFECT),
    )(*[_in_hbm(b) for b in bufs], after)
    return out[0], out[1], list(out[2:2 + ns]), list(out[2 + ns:2 + ns + nl]), out[-1]


def _split_wait(send_sems, recv_sems, srcs, lands, plan, after, name):
    ns, nl = len(srcs), len(lands)
    bufs = list(srcs) + list(lands)

    def body(*refs):
        send_ref, recv_ref = refs[ns + nl], refs[ns + nl + 1]
        me = _position()
        for k, src, dst in plan(refs[:ns], refs[ns:ns + nl]):
            cp = _remote(src, dst, send_ref.at[k], recv_ref.at[k], me)
            cp.wait_send()
            cp.wait_recv()

    out = pl.pallas_call(
        body, name=name,
        out_shape=[pltpu.HBM(b.shape, b.dtype) for b in bufs],
        in_specs=[_HBM] * (ns + nl) + [_SEM, _SEM, _ANY],
        out_specs=[_HBM] * (ns + nl),
        input_output_aliases={i: i for i in range(ns + nl)},
        compiler_params=pltpu.CompilerParams(has_side_effects=_EFFECT),
    )(*bufs, send_sems, recv_sems, after)
    return list(out[:ns]), list(out[ns:])


def _gather_plan(rows_of):
    def start(src_refs, land_refs):
        x, y, c = _position()
        chip = 2 * x + y
        out = []
        for a, rows in enumerate(rows_of):
            mine = _half(land_refs[a].at[chip], c, rows)
            out.extend((mine, mine, (_flip(x, fx), _flip(y, fy), c)) for fx, fy in _CHIP_FLIPS)
        return out

    def wait(src_refs, land_refs):
        x, y, c = _position()
        chip = 2 * x + y
        out = []
        for a, rows in enumerate(rows_of):
            for j, (fx, fy) in enumerate(_CHIP_FLIPS):
                src_chip = 2 * _flip(x, fx) + _flip(y, fy)
                out.append((3 * a + j, _half(land_refs[a].at[chip], c, rows),
                            _half(land_refs[a].at[src_chip], c, rows)))
        return out

    return start, wait


def _exchange_plan(n_arrays):
    def start(src_refs, land_refs):
        x, y, c = _position()
        chip = 2 * x + y
        out = []
        for a in range(n_arrays):
            for fx, fy in _CHIP_FLIPS:
                px, py = _flip(x, fx), _flip(y, fy)
                out.append((src_refs[a].at[2 * px + py], land_refs[a].at[0, chip], (px, py, c)))
        return out

    def wait(src_refs, land_refs):
        x, y, c = _position()
        out = []
        for a in range(n_arrays):
            for j, (fx, fy) in enumerate(_CHIP_FLIPS):
                src_chip = 2 * _flip(x, fx) + _flip(y, fy)
                out.append((3 * a + j, src_refs[a].at[src_chip], land_refs[a].at[0, src_chip]))
        return out

    return start, wait


def _forward_to_sibling(lands, name):
    na = len(lands)

    def body(*refs):
        land_refs = refs[na:2 * na]
        send_sems, recv_sems = refs[2 * na:]
        x, y, c = _position()
        sibling = (x, y, 1 - c)
        sends = []
        for a in range(na):
            rows = lands[a].shape[1]
            for j, (fx, fy) in enumerate(_CHIP_FLIPS):
                landed = _half(land_refs[a].at[2 * _flip(x, fx) + _flip(y, fy)], c, rows)
                sends.append(_remote(landed, landed, send_sems.at[3 * a + j], recv_sems.at[3 * a + j], sibling))
                sends[-1].start()
        for a in range(na):
            rows = lands[a].shape[1]
            for j, (fx, fy) in enumerate(_CHIP_FLIPS):
                other = _half(land_refs[a].at[2 * _flip(x, fx) + _flip(y, fy)], 1 - c, rows)
                _remote(other, other, send_sems.at[3 * a + j], recv_sems.at[3 * a + j], sibling).wait_recv()
        for cp in sends:
            cp.wait_send()

    return pl.pallas_call(
        body, name=name,
        out_shape=[_sds(l.shape, l.dtype) for l in lands],
        in_specs=[_ANY] * na, out_specs=[_ANY] * na,
        input_output_aliases={a: a for a in range(na)},
        scratch_shapes=[pltpu.SemaphoreType.DMA((3 * na,))] * 2,
    )(*lands)


def _swap_gathered(gath, name):
    na = len(gath)

    def body(*refs):
        gath_refs = refs[na:2 * na]
        send_sems, recv_sems = refs[2 * na:]
        x, y, c = _position()
        cps = [_remote(gath_refs[a].at[0], gath_refs[a].at[1], send_sems.at[a], recv_sems.at[a], (x, y, 1 - c))
               for a in range(na)]
        for cp in cps:
            cp.start()
        for cp in cps:
            cp.wait()

    return pl.pallas_call(
        body, name=name,
        out_shape=[_sds(g.shape, g.dtype) for g in gath],
        in_specs=[_ANY] * na, out_specs=[_ANY] * na,
        input_output_aliases={a: a for a in range(na)},
        scratch_shapes=[pltpu.SemaphoreType.DMA((na,))] * 2,
    )(*gath)


def _adam_gathered(w, gath, m, v, c_arr, name, tr=128):
    rows, cols = w.shape
    hr = rows // 2
    per = hr // tr

    def body(c_ref, w_ref, g_ref, m_ref, v_ref, go_ref, d_ref, nm_ref, nv_ref):
        g = g_ref[0, 0].astype(F32)
        for k in range(1, N_CHIPS):
            g = g + g_ref[0, k].astype(F32)
        go_ref[...] = g
        d_ref[...], nm_ref[...], nv_ref[...] = _adam_math(w_ref[...], g, m_ref[...], v_ref[...])

    def rows_of(h, i, c_ref):
        c = c_ref[0]
        return ((c + h - 2 * c * h) * per + i, 0)

    blk = pl.BlockSpec((tr, cols), rows_of)
    grid_spec = pltpu.PrefetchScalarGridSpec(
        num_scalar_prefetch=1, grid=(2, per),
        in_specs=[blk, pl.BlockSpec((1, N_CHIPS, tr, cols), lambda h, i, c_ref: (h, 0, i, 0)), blk, blk],
        out_specs=[blk] * 4)
    return pl.pallas_call(
        body, name=name, grid_spec=grid_spec, out_shape=[_sds(w.shape, F32)] * 4,
        compiler_params=_params(("arbitrary", "arbitrary")),
    )(c_arr, w, gath, m, v)


def _allreduce_small(block, name):
    r, n = block.shape
    hr = r // 2

    def body(x_ref, out_ref, sib, chipsum, gath, d2d_send, d2d_recv, ici_send, ici_recv):
        x, y, c = _position()
        chip = 2 * x + y
        sibling = (x, y, 1 - c)
        first = _remote(x_ref, sib, d2d_send.at[0], d2d_recv.at[0], sibling)
        first.start()
        first.wait()
        chipsum[...] = x_ref[...] + sib[...]
        mine = pl.ds(c * hr, hr)
        theirs = pl.ds((1 - c) * hr, hr)
        sends = []
        for j, (fx, fy) in enumerate(_CHIP_FLIPS):
            sends.append(_remote(chipsum.at[mine, :], gath.at[chip], ici_send.at[j], ici_recv.at[j],
                                 (_flip(x, fx), _flip(y, fy), c)))
            sends[-1].start()
        gath[chip] = chipsum[mine, :]
        for j, (fx, fy) in enumerate(_CHIP_FLIPS):
            landed = gath.at[2 * _flip(x, fx) + _flip(y, fy)]
            _remote(landed, landed, ici_send.at[j], ici_recv.at[j], sibling).wait_recv()
        for cp in sends:
            cp.wait_send()
        total = gath[0]
        for k in range(1, N_CHIPS):
            total = total + gath[k]
        out_ref[mine, :] = total
        last = _remote(out_ref.at[mine, :], out_ref.at[mine, :], d2d_send.at[1], d2d_recv.at[1], sibling)
        last.start()
        _remote(out_ref.at[theirs, :], out_ref.at[theirs, :], d2d_send.at[1], d2d_recv.at[1], sibling).wait_recv()
        last.wait_send()

    vmem = pl.BlockSpec(memory_space=pltpu.VMEM)
    return pl.pallas_call(
        body, name=name, out_shape=_sds((r, n), F32), in_specs=[vmem], out_specs=vmem,
        scratch_shapes=[pltpu.VMEM((r, n), F32), pltpu.VMEM((r, n), F32), pltpu.VMEM((N_CHIPS, hr, n), F32),
                        pltpu.SemaphoreType.DMA((2,)), pltpu.SemaphoreType.DMA((2,)),
                        pltpu.SemaphoreType.DMA((3,)), pltpu.SemaphoreType.DMA((3,))],
        compiler_params=pltpu.CompilerParams(vmem_limit_bytes=VMEM_LIMIT_BYTES),
    )(block)


def _cast_place(shards, chip_arr):
    na = len(shards)
    steps = 4

    def body(chip_ref, *refs):
        for a in range(na):
            refs[na + a][0] = refs[a][...].astype(BF16)

    grid_spec = pltpu.PrefetchScalarGridSpec(
        num_scalar_prefetch=1, grid=(steps,),
        in_specs=[pl.BlockSpec((s.shape[0] // steps, s.shape[1]), lambda i, ch: (i, 0)) for s in shards],
        out_specs=[pl.BlockSpec((1, s.shape[0] // steps, s.shape[1]), lambda i, ch: (ch[0], i, 0)) for s in shards])
    return pl.pallas_call(
        body, name="cast_place", grid_spec=grid_spec,
        out_shape=[_sds((N_CHIPS,) + s.shape, BF16) for s in shards],
        compiler_params=_params(("arbitrary",)),
    )(chip_arr, *shards)


def _silu(v):
    return v * _sigmoid(v)


def _ada_fwd(c8, w_ada):
    def body(c_ref, w_ref, o_ref):
        o_ref[...] = jnp.dot(_silu(c_ref[...]), w_ref[...], preferred_element_type=F32,
                             precision=lax.Precision.HIGHEST)

    return pl.pallas_call(
        body, name="ada_fwd", out_shape=_sds((N_DEV, w_ada.shape[1]), F32),
        compiler_params=pltpu.CompilerParams(vmem_limit_bytes=VMEM_LIMIT_BYTES),
    )(c8, w_ada)


def _mod_select(parts, b_ada, me_arr, after):
    cols = parts.shape[2]

    def body(me_ref, p_ref, b_ref, after_ref, o_ref):
        me = me_ref[0]
        for k in range(N_CHIPS):
            cs = slice(k * cols, (k + 1) * cols)
            o_ref[:, cs] = p_ref[2 * k, pl.ds(me, 1), :] + b_ref[:, cs]

    grid_spec = pltpu.PrefetchScalarGridSpec(
        num_scalar_prefetch=1, grid=(1,),
        in_specs=[pl.BlockSpec(parts.shape, lambda i, m: (0, 0, 0)), pl.BlockSpec(b_ada.shape, lambda i, m: (0, 0)),
                  _ANY],
        out_specs=pl.BlockSpec(b_ada.shape, lambda i, m: (0, 0)))
    return pl.pallas_call(body, name="mod_select", grid_spec=grid_spec, out_shape=_sds(b_ada.shape, F32))(
        me_arr, parts, b_ada, after)


def _ada_bwd(c8, dmod8, chip_arr, w, m, v, tr=256):
    d = c8.shape[1]
    cols = dmod8.shape[1] // N_CHIPS

    def body(chip_ref, c_ref, dm_ref, dmall_ref, w_ref, m_ref, v_ref, gw_ref, d_ref, nm_ref, nv_ref, gb_ref):
        g = lax.dot_general(_silu(c_ref[...]), dm_ref[...], (((0,), (0,)), ((), ())),
                            preferred_element_type=F32, precision=lax.Precision.HIGHEST)
        gw_ref[...] = g
        d_ref[...], nm_ref[...], nv_ref[...] = _adam_math(w_ref[...], g, m_ref[...], v_ref[...])
        acc = dmall_ref[0:1, :]
        for k in range(1, N_DEV):
            acc = acc + dmall_ref[k:k + 1, :]
        gb_ref[...] = acc

    rows = pl.BlockSpec((tr, cols), lambda i, ch: (i, 0))
    grid_spec = pltpu.PrefetchScalarGridSpec(
        num_scalar_prefetch=1, grid=(d // tr,),
        in_specs=[pl.BlockSpec((N_DEV, tr), lambda i, ch: (0, i)),
                  pl.BlockSpec((N_DEV, cols), lambda i, ch: (0, ch[0])),
                  pl.BlockSpec(dmod8.shape, lambda i, ch: (0, 0)), rows, rows, rows],
        out_specs=[rows] * 4 + [pl.BlockSpec((1, dmod8.shape[1]), lambda i, ch: (0, 0))])
    return pl.pallas_call(
        body, name="ada_bwd", grid_spec=grid_spec,
        out_shape=[_sds((d, cols), F32)] * 4 + [_sds((1, dmod8.shape[1]), F32)],
        compiler_params=_params(("arbitrary",)),
    )(chip_arr, c8, dmod8, dmod8, w, m, v)


def _adam_math(w, g, m, v):
    m = ADAM_B1 * m + (1.0 - ADAM_B1) * g
    v = ADAM_B2 * v + (1.0 - ADAM_B2) * (g * g)
    m_hat = m / (1.0 - ADAM_B1 ** ADAM_STEP)
    v_hat = v / (1.0 - ADAM_B2 ** ADAM_STEP)
    delta = -ADAM_LR * (m_hat / (jnp.sqrt(v_hat) + ADAM_EPS) + ADAM_WD * w)
    return delta, m, v


def _adam(w, g, m, v, name, tr=256):
    rows, cols = w.shape
    if rows % tr:
        tr = rows

    def body(w_ref, g_ref, m_ref, v_ref, d_ref, nm_ref, nv_ref):
        d_ref[...], nm_ref[...], nv_ref[...] = _adam_math(w_ref[...], g_ref[...], m_ref[...], v_ref[...])

    spec = pl.BlockSpec((tr, cols), lambda i: (i, 0))
    return pl.pallas_call(
        body, name=name, grid=(rows // tr,), in_specs=[spec] * 4, out_specs=[spec] * 3,
        out_shape=[_sds(w.shape, F32)] * 3, compiler_params=_params(("parallel",)),
    )(w, g, m, v)


def _adam_cols(w, g_full, m, v, chip_arr, name):
    rows, cols = w.shape

    def body(chip_ref, w_ref, g_ref, m_ref, v_ref, gs_ref, d_ref, nm_ref, nv_ref):
        g = g_ref[...]
        gs_ref[...] = g
        d_ref[...], nm_ref[...], nv_ref[...] = _adam_math(w_ref[...], g, m_ref[...], v_ref[...])

    own = pl.BlockSpec((rows, cols), lambda i, ch: (0, 0))
    grid_spec = pltpu.PrefetchScalarGridSpec(
        num_scalar_prefetch=1, grid=(1,),
        in_specs=[own, pl.BlockSpec((rows, cols), lambda i, ch: (0, ch[0])), own, own],
        out_specs=[own] * 4)
    return pl.pallas_call(body, name=name, grid_spec=grid_spec, out_shape=[_sds(w.shape, F32)] * 4)(
        chip_arr, w, g_full, m, v)


PACK_COLS = 512
SMALL_REPLICATED = ("g_mix_pre", "g_mix_post", "conv_b", "w_rgate", "b_rgate", "w_igate", "b_igate", "lru_a",
                    "v_norm_g", "v_norm_b", "w_spatial", "b_spatial", "g_lru_out", "g_gmlp_out", "g_ffn_pre",
                    "g_ffn_post", "ffn_conv_b")
SMALL_COLUMN_SHARDED = ("conv_w", "ffn_conv_w")


def _pack(arrays):
    flat = jnp.concatenate([a.reshape(1, -1) for a in arrays], axis=1)
    pad = (-flat.shape[1]) % (2 * LANES)
    if pad:
        flat = jnp.pad(flat, ((0, 0), (0, pad)))
    return flat.reshape(2, -1)


def _unpack(packed, shapes):
    flat = packed.reshape(1, -1)
    out, col = [], 0
    for shape in shapes:
        n = math.prod(shape)
        out.append(flat[:, col:col + n].reshape(shape))
        col += n
    return out


def kernel(x, c, w_ada, b_ada, g_mix_pre, g_mix_post, w_in, conv_w, conv_b, w_rgate, b_rgate, w_igate, b_igate, lru_a, v_norm_g, v_norm_b, w_spatial, b_spatial, g_lru_out, g_gmlp_out, w_out, g_ffn_pre, g_ffn_post, w_up, ffn_conv_w, ffn_conv_b, w_down, loss_target, m_w_ada, m_b_ada, m_g_mix_pre, m_g_mix_post, m_w_in, m_conv_w, m_conv_b, m_w_rgate, m_b_rgate, m_w_igate, m_b_igate, m_lru_a, m_v_norm_g, m_v_norm_b, m_w_spatial, m_b_spatial, m_g_lru_out, m_g_gmlp_out, m_w_out, m_g_ffn_pre, m_g_ffn_post, m_w_up, m_ffn_conv_w, m_ffn_conv_b, m_w_down, v_w_ada, v_b_ada, v_g_mix_pre, v_g_mix_post, v_w_in, v_conv_w, v_conv_b, v_w_rgate, v_b_rgate, v_w_igate, v_b_igate, v_lru_a, v_v_norm_g, v_v_norm_b, v_w_spatial, v_b_spatial, v_g_lru_out, v_g_gmlp_out, v_w_out, v_g_ffn_pre, v_g_ffn_post, v_w_up, v_ffn_conv_w, v_ffn_conv_b, v_w_down):
    args = dict(locals())
    names = ("w_ada", "b_ada", "g_mix_pre", "g_mix_post", "w_in", "conv_w", "conv_b", "w_rgate", "b_rgate",
             "w_igate", "b_igate", "lru_a", "v_norm_g", "v_norm_b", "w_spatial", "b_spatial", "g_lru_out",
             "g_gmlp_out", "w_out", "g_ffn_pre", "g_ffn_post", "w_up", "ffn_conv_w", "ffn_conv_b", "w_down")
    drop = lambda a: a if a.ndim == 2 else a[0]
    w = {n: drop(args[n]) for n in names}
    m = {n: drop(args["m_" + n]) for n in names}
    v = {n: drop(args["v_" + n]) for n in names}
    xi, yi, ci = _position()
    me_arr = jnp.reshape(4 * xi + 2 * yi + ci, (1,)).astype(jnp.int32)
    chip_arr = jnp.reshape(2 * xi + yi, (1,)).astype(jnp.int32)
    c_arr = jnp.reshape(ci, (1,)).astype(jnp.int32)
    pos_arr = jnp.stack([ci, 2 * xi + yi]).astype(jnp.int32)

    big = ("w_in", "w_out", "w_up", "w_down")
    lands = _cast_place([w[n] for n in big], chip_arr)
    start_a, wait_a = _gather_plan([w[n].shape[0] for n in big[:2]])
    start_b, wait_b = _gather_plan([w[n].shape[0] for n in big[2:]])

    row0 = jnp.concatenate([c, w["conv_w"].reshape(1, -1), w["ffn_conv_w"].reshape(1, -1)], axis=1)
    g0 = _allgather8(row0, "gather_cond", False)[:, 0, :]
    c8 = g0[:, :D_MODEL]
    per_chip = g0[0::2]
    conv_w_full = per_chip[:, D_MODEL:D_MODEL + 512].reshape(N_CHIPS, 4, 128).transpose(1, 0, 2).reshape(4, 512)
    ffn_conv_w_full = per_chip[:, D_MODEL + 512:].reshape(N_CHIPS, 3, 1536).transpose(1, 0, 2).reshape(3, 2 * D_FF)
    mod_parts = _allgather8(_ada_fwd(c8, w["w_ada"]), "gather_mod", False)
    send_a, recv_a, _, lands_a, token_a = _split_start([], lands[:2], start_a, 6, mod_parts, "gather_start_a")
    send_b, recv_b, _, lands_b, token_b = _split_start([], lands[2:], start_b, 6, token_a, "gather_start_b")
    mod = _mod_select(mod_parts, w["b_ada"].reshape(1, -1), me_arr, token_b).reshape(N_MOD, D_MODEL)
    sh_m, sc_m, gt_m, sh_f, sc_f, gt_f = [mod[k:k + 1] for k in range(N_MOD)]

    small = {n: w[n] for n in SMALL_REPLICATED}
    small["conv_w"] = conv_w_full
    small["ffn_conv_w"] = ffn_conv_w_full
    row = lambda a: a.reshape(1, -1)
    seq_params, ws_t = _seq_params(small)
    glo, ggo = row(small["g_lru_out"]), row(small["g_gmlp_out"])
    g_pre, g_post = row(small["g_mix_pre"]), row(small["g_mix_post"])
    g_pre2, g_post2 = row(small["g_ffn_pre"]), row(small["g_ffn_post"])
    fw, fb = small["ffn_conv_w"], row(small["ffn_conv_b"])
    xs, tgt = x[0], loss_target[0]

    _, lands_a = _split_wait(send_a, recv_a, [], lands_a, wait_a, mod, "gather_wait_a")
    w_in4, w_out4 = _forward_to_sibling(lands_a, "forward_a")
    w_out_b = w_out4.reshape(D_MODEL, D_MODEL)
    z, h = _mix_in(xs, sc_m, sh_m, g_pre, w_in4)
    ycat, hst, stash = _seqmix(z, seq_params, glo, ggo)
    y, x1, h2 = _mix_out(ycat, xs, w_out_b, gt_m, g_post, g_pre2, sc_f, sh_f)
    _, lands_b = _split_wait(send_b, recv_b, [], lands_b, wait_b, h2, "gather_wait_b")
    w_up4, w_down4 = _forward_to_sibling(lands_b, "forward_b")
    w_down_b = w_down4.reshape(D_FF, D_MODEL)
    up0, pre, act, dy2, dx2, loss, dgt_f, dg_post2 = _ffn_fwd(h2, x1, tgt, w_up4, w_down_b, fw, fb, gt_f, g_post2)

    dup0, dfw, dfb = _ffn_bwd_a(dy2, pre, up0, w_down_b, fw)
    gw_up = _wgrad(h2, dup0, N_CHIPS, "wgrad_up", True)
    gw_down = _wgrad(act, dy2, 2, "wgrad_down", False)
    ex_start, ex_wait = _exchange_plan(2)

    def reduce_start(parts, tags, name):
        recv = _swap_halves(parts, "swap_halves_" + name)
        both = [_chip_sum(p, r, pos_arr, "chip_sum_" + t) for p, r, t in zip(parts, recv, tags)]
        sums, gath = [b[0] for b in both], [b[1] for b in both]
        return _split_start(sums, gath, ex_start, 3 * len(parts), pos_arr, "exchange_start_" + name)

    e_send_b, e_recv_b, sums_b, gath_b, token_b = reduce_start(
        [gw_up, gw_down.reshape(N_CHIPS, -1, D_MODEL)], ("w_up", "w_down"), "b")

    dx1, dy, dsh_f, dsc_f, dg_pre2, dgt_m, dg_post = _ffn_bwd_b(
        dup0, x1, y, dx2, w_up4, g_pre2, sc_f + token_b[0:1, 0:1], sh_f, gt_m, g_post)
    (dz, dcw, dcb, dwr, dwi, dbr, dbi, dspa, dng, dnb, dws, dbs_t, dglo, dggo) = _seqmix_bwd(
        z, hst, stash, dy, w_out_b, seq_params, ws_t, glo, ggo)
    grad_x, dsh_m, dsc_m, dg_pre = _mix_in_bwd(xs, dz, dx1, w_in4, g_pre, sc_m)
    gw_in = _wgrad(h, dz, N_CHIPS, "wgrad_in", True)
    gw_out = _wgrad(ycat, dy, 1, "wgrad_out", False)
    e_send_a, e_recv_a, sums_a, gath_a, token_a = reduce_start(
        [gw_in, gw_out.reshape(N_CHIPS, -1, D_MODEL)], ("w_in", "w_out"), "a")

    grads, deltas, new_m, new_v = {}, {}, {}, {}

    def reduce_finish(send, recv, sums, gath, tags, after, name):
        sums, gath = _split_wait(send, recv, sums, gath, ex_wait, after, "exchange_wait_" + name)
        gath = _swap_gathered(gath, "swap_gathered_" + name)
        for g, t in zip(gath, tags):
            grads[t], deltas[t], new_m[t], new_v[t] = _adam_gathered(w[t], g, m[t], v[t], c_arr, "adam_" + t)

    reduce_finish(e_send_b, e_recv_b, sums_b, gath_b, ("w_up", "w_down"), token_a, "b")
    reduce_finish(e_send_a, e_recv_a, sums_a, gath_a, ("w_in", "w_out"), deltas["w_down"], "a")

    dmod = jnp.concatenate([dsh_m, dsc_m, dgt_m, dsh_f, dsc_f, dgt_f], axis=1)
    dmod8 = _allgather8(dmod, "gather_dmod", False)[:, 0, :]
    grads["w_ada"], deltas["w_ada"], new_m["w_ada"], new_v["w_ada"], g_b_ada = _ada_bwd(
        c8, dmod8, chip_arr, w["w_ada"], m["w_ada"], v["w_ada"])

    small_grads = dict(
        g_mix_pre=dg_pre, g_mix_post=dg_post, conv_w=dcw, conv_b=dcb,
        w_rgate=_diag_blocks(dwr), b_rgate=dbr.reshape(LRU_HEADS, -1),
        w_igate=_diag_blocks(dwi), b_igate=dbi.reshape(LRU_HEADS, -1), lru_a=dspa,
        v_norm_g=dng, v_norm_b=dnb, w_spatial=dws, b_spatial=dbs_t.T,
        g_lru_out=dglo, g_gmlp_out=dggo, g_ffn_pre=dg_pre2, g_ffn_post=dg_post2,
        ffn_conv_w=dfw, ffn_conv_b=dfb)
    packed_names = SMALL_REPLICATED + SMALL_COLUMN_SHARDED
    g_small = _allreduce_small(_pack([small_grads[n] for n in packed_names] + [loss]), "reduce_small")
    g_small_list = _unpack(g_small, [small_grads[n].shape for n in packed_names] + [loss.shape])
    g_rep = dict(zip(packed_names, g_small_list))
    total = g_small_list[-1][0, 0]

    rep = SMALL_REPLICATED
    d_p, m_p, v_p = _adam(_pack([w[n] for n in rep]), _pack([g_rep[n] for n in rep]),
                          _pack([m[n] for n in rep]), _pack([v[n] for n in rep]), "adam_small")
    shapes = [w[n].shape for n in rep]
    for n, dd, mm, vv in zip(rep, _unpack(d_p, shapes), _unpack(m_p, shapes), _unpack(v_p, shapes)):
        grads[n], deltas[n], new_m[n], new_v[n] = g_rep[n], dd, mm, vv
    for n in SMALL_COLUMN_SHARDED:
        grads[n], deltas[n], new_m[n], new_v[n] = _adam_cols(w[n], g_rep[n], m[n], v[n], chip_arr, "adam_" + n)
    d_b, m_b, v_b = _adam(w["b_ada"], g_b_ada, m["b_ada"], v["b_ada"], "adam_b_ada")
    grads["b_ada"], deltas["b_ada"], new_m["b_ada"], new_v["b_ada"] = g_b_ada, d_b, m_b, v_b

    outs = [total, grad_x[None]]
    for group in (grads, deltas, new_m, new_v):
        outs.extend(group[n].reshape(args[n].shape) for n in names)
    return tuple(outs)
```

```python
import functools
import math

import jax
import jax.numpy as jnp
from jax import lax
from jax.experimental import pallas as pl
from jax.experimental.pallas import tpu as pltpu

F32 = jnp.float32
BF16 = jnp.bfloat16
MESH = pl.DeviceIdType.MESH

D_MODEL = 1024
LRU_WIDTH = 512
LRU_HEADS = 8
GMLP_WIDTH = 512
GMLP_GROUPS = 4
GMLP_BLOCK = 128
CHUNK = 64
D_FF = 3072
N_MOD = 6
EPS = 1e-6
LRU_C = 8.0
N_CHIPS = 4
N_DEV = 8

ADAM_LR = 0.001
ADAM_B1 = 0.9
ADAM_B2 = 0.999
ADAM_EPS = 1e-08
ADAM_WD = 0.01
ADAM_STEP = 10

GELU_C0 = math.sqrt(2.0 / math.pi)
GELU_C1 = 0.044715

VMEM_LIMIT_BYTES = 56 * 1024 * 1024
SUBLANES = 8
LANES = 128
BF16_SUBLANES = 16
FFN_CHUNK = 768
SUB_ROWS = 256


def _gelu_gate(x):
    x2 = x * x
    z = x * ((2.0 * GELU_C0 * GELU_C1) * x2 + 2.0 * GELU_C0)
    return 1.0 / (1.0 + jnp.exp(-z)), x2


def _gelu(x):
    t = jnp.tanh(GELU_C0 * (x + GELU_C1 * x * x * x))
    return 0.5 * x * (1.0 + t)


def _gelu_and_grad(x):
    s, x2 = _gelu_gate(x)
    g = x * s
    dz = (6.0 * GELU_C0 * GELU_C1) * x2 + 2.0 * GELU_C0
    return g, s + g * (1.0 - s) * dz


def _sigmoid(x):
    return 1.0 / (1.0 + jnp.exp(-x))


def _log1p(u):
    w = 1.0 + u
    return jnp.where(w == 1.0, u, jnp.log(w) * (u / (w - 1.0)))


def _softplus(x):
    return jnp.maximum(x, 0.0) + _log1p(jnp.exp(-jnp.abs(x)))


def _neg_expm1(x):
    u = jnp.exp(x)
    um1 = u - 1.0
    tiny = um1 == 0.0
    small = um1 * (x / jnp.log(jnp.where(tiny, 2.0, jnp.maximum(u, 0.25))))
    return -jnp.where(tiny, x, jnp.where(x < -1.0, um1, small))


def _msq_rsqrt(v):
    return lax.rsqrt(jnp.mean(v * v, axis=-1, keepdims=True) + EPS)


def _rms_bwd(dyn, yn, r):
    return r * (dyn - yn * jnp.mean(dyn * yn, axis=-1, keepdims=True))


def _colsum(v):
    return jnp.sum(v, axis=0, keepdims=True)


def _shift_down(cur, prev8, k):
    rolled = pltpu.roll(cur, k, 0)
    head = pltpu.roll(prev8, k, 0)
    row8 = lax.broadcasted_iota(jnp.int32, (SUBLANES, cur.shape[1]), 0)
    first = jnp.where(row8 < k, head, rolled[0:SUBLANES])
    return jnp.concatenate([first, rolled[SUBLANES:]], axis=0)


def _shift_up(cur, next8, k):
    t = cur.shape[0]
    rolled = pltpu.roll(cur, t - k, 0)
    tail = pltpu.roll(next8, SUBLANES - k, 0)
    row8 = lax.broadcasted_iota(jnp.int32, (SUBLANES, cur.shape[1]), 0)
    last = jnp.where(row8 >= SUBLANES - k, tail, rolled[t - SUBLANES:])
    return jnp.concatenate([rolled[:t - SUBLANES], last], axis=0)


def _scan_fwd(a, b):
    t = a.shape[0]
    row = lax.broadcasted_iota(jnp.int32, a.shape, 0)
    d = 1
    while d < t:
        keep = row >= d
        a_s = jnp.where(keep, pltpu.roll(a, d, 0), 1.0)
        b_s = jnp.where(keep, pltpu.roll(b, d, 0), 0.0)
        b = a * b_s + b
        a = a * a_s
        d *= 2
    return a, b


def _scan_bwd(a, g):
    t = a.shape[0]
    row = lax.broadcasted_iota(jnp.int32, a.shape, 0)
    d = 1
    while d < t:
        keep = row < t - d
        a_s = jnp.where(keep, pltpu.roll(a, t - d, 0), 1.0)
        g_s = jnp.where(keep, pltpu.roll(g, t - d, 0), 0.0)
        g = a * g_s + g
        a = a * a_s
        d *= 2
    return a, g


def _dot(a, b):
    return jnp.dot(a, b, preferred_element_type=F32)


def _dot_nt(a, b):
    return lax.dot_general(a, b, (((1,), (1,)), ((), ())), preferred_element_type=F32)


def _dot_tn(a, b):
    return lax.dot_general(a, b, (((0,), (0,)), ((), ())), preferred_element_type=F32)


def _rows(ts, cols, rev_of=None):
    if rev_of is None:
        return pl.BlockSpec((ts, cols), lambda i: (i, 0))
    return pl.BlockSpec((ts, cols), lambda i: (rev_of - 1 - i, 0))


def _halo_prev(ts, cols, halo, rev_of=None, col_block=0):
    per = ts // halo
    if rev_of is None:
        return pl.BlockSpec((halo, cols), lambda i: (jnp.maximum(i * per - 1, 0), col_block))
    return pl.BlockSpec((halo, cols), lambda i: (jnp.maximum((rev_of - 1 - i) * per - 1, 0), col_block))


def _full(shape):
    nd = len(shape)
    return pl.BlockSpec(shape, lambda *_: (0,) * nd)


_RESIDENT = pl.BlockSpec(memory_space=pltpu.VMEM)


def _params(sem):
    return pltpu.CompilerParams(dimension_semantics=sem, vmem_limit_bytes=VMEM_LIMIT_BYTES)


def _sds(shape, dtype):
    return jax.ShapeDtypeStruct(shape, dtype)


def _sub_tiles(ts):
    return [slice(r0, r0 + SUB_ROWS) for r0 in range(0, ts, SUB_ROWS)]


def _mix_in(x, sc, sh, g, w_in4, ts=512):
    s, d = x.shape

    def body(x_ref, sc_ref, sh_ref, g_ref, w_ref, z_ref, h_ref):
        for rs in _sub_tiles(ts):
            xv = x_ref[rs, :]
            h = (xv * _msq_rsqrt(xv) * g_ref[...]) * (1.0 + sc_ref[...]) + sh_ref[...]
            hb = h.astype(BF16)
            h_ref[rs, :] = hb
            for k in range(N_CHIPS):
                z_ref[rs, k * 512:(k + 1) * 512] = _dot(hb, w_ref[k])

    return pl.pallas_call(
        body, grid=(s // ts,), name="mix_in",
        in_specs=[_rows(ts, d), _full((1, d)), _full((1, d)), _full((1, d)), _full(w_in4.shape)],
        out_specs=[_rows(ts, 2048), _rows(ts, d)],
        out_shape=[_sds((s, 2048), F32), _sds((s, d), BF16)],
        compiler_params=_params(("parallel",)),
    )(x, sc, sh, g, w_in4)


N_STASH = 12
(ST_XC, ST_R, ST_IG, ST_A, ST_MULT, ST_GL, ST_DGL, ST_U, ST_DU, ST_Q, ST_VHAT, ST_SPB) = range(N_STASH)


def _seq_param_specs():
    return [_full((4, 512)), _full((1, 512)), _full((512, 512)), _full((512, 512)), _full((1, 512)),
            _full((1, 512)), _full((1, 512)), _full((1, 512)), _full((1, 512)), _full((4, 128, 128)),
            _full((128, 4))]


def _seqmix(z, seq_params, glo, ggo, ts=256):
    s = z.shape[0]
    nt = s // ts

    def body(z_ref, zprev_ref, cw_ref, cb_ref, bdr_ref, bdi_ref, br_ref, bi_ref, la_ref, ng_ref, nb_ref,
             ws_ref, bst_ref, glo_ref, ggo_ref, ycat_ref, hst_ref, st_ref, hcarry, sp_scr):
        i = pl.program_id(0)

        @pl.when(i == 0)
        def _():
            hcarry[...] = jnp.zeros_like(hcarry)

        lx = z_ref[:, 0:512]
        prev8 = jnp.where(i == 0, 0.0, zprev_ref[...])
        xc = (cw_ref[3:4, :] * lx + cw_ref[2:3, :] * _shift_down(lx, prev8, 1)
              + cw_ref[1:2, :] * _shift_down(lx, prev8, 2) + cw_ref[0:1, :] * _shift_down(lx, prev8, 3)
              + cb_ref[...])
        xcb = xc.astype(BF16)
        r = _sigmoid(_dot(xcb, bdr_ref[...]) + br_ref[...])
        ig = _sigmoid(_dot(xcb, bdi_ref[...]) + bi_ref[...])
        log_a = (-LRU_C) * r * _softplus(-la_ref[...])
        a = jnp.exp(log_a)
        mult = jnp.sqrt(_neg_expm1(2.0 * log_a))
        acum, hloc = _scan_fwd(a, mult * (ig * xc))
        h = hloc + acum * hcarry[...]
        hcarry[...] = h[ts - 1:ts, :]
        hst_ref[...] = h
        gl, dgl = _gelu_and_grad(z_ref[:, 512:1024])
        y_l = h * gl
        for slot, val in ((ST_XC, xc), (ST_R, r), (ST_IG, ig), (ST_A, a), (ST_MULT, mult), (ST_GL, gl),
                          (ST_DGL, dgl)):
            st_ref[slot] = val

        u, du = _gelu_and_grad(z_ref[:, 1024:1536])
        vg, dvg = _gelu_and_grad(z_ref[:, 1536:2048])
        vc = vg - jnp.mean(vg, axis=-1, keepdims=True)
        rstd = lax.rsqrt(jnp.mean(vc * vc, axis=-1, keepdims=True) + EPS)
        vhat = vc * rstd
        vb = (vhat * ng_ref[...] + nb_ref[...]).astype(BF16)
        for n in range(ts // GMLP_BLOCK):
            rs = slice(n * GMLP_BLOCK, (n + 1) * GMLP_BLOCK)
            for g in range(GMLP_GROUPS):
                cs = slice(g * 128, (g + 1) * 128)
                sp_scr[rs, cs] = _dot(ws_ref[g], vb[rs, cs]) + bst_ref[:, g:g + 1]
        spb = sp_scr[...]
        y_g = u * spb
        for slot, val in ((ST_U, u), (ST_DU, du), (ST_Q, rstd * dvg), (ST_VHAT, vhat), (ST_SPB, spb)):
            st_ref[slot] = val

        ycat_ref[:, 0:512] = (y_l * _msq_rsqrt(y_l) * glo_ref[...]).astype(BF16)
        ycat_ref[:, 512:1024] = (y_g * _msq_rsqrt(y_g) * ggo_ref[...]).astype(BF16)

    return pl.pallas_call(
        body, grid=(nt,), name="seqmix",
        in_specs=[_rows(ts, 2048), _halo_prev(ts, 512, SUBLANES)] + _seq_param_specs()
        + [_full((1, 512)), _full((1, 512))],
        out_specs=[_rows(ts, 1024), _rows(ts, 512), pl.BlockSpec((N_STASH, ts, 512), lambda i: (0, i, 0))],
        out_shape=[_sds((s, 1024), BF16), _sds((s, 512), F32), _sds((N_STASH, s, 512), F32)],
        scratch_shapes=[pltpu.VMEM((1, 512), F32), pltpu.VMEM((ts, 512), F32)],
        compiler_params=_params(("arbitrary",)),
    )(z, z, *seq_params, glo, ggo)


def _mix_out(ycat, x, w_out, gt_m, g_post, g_pre2, sc_f, sh_f, ts=512):
    s, d = x.shape

    def body(yc_ref, x_ref, w_ref, gt_ref, gp_ref, g2_ref, sc_ref, sh_ref, y_ref, x1_ref, h2_ref):
        for rs in _sub_tiles(ts):
            y = _dot(yc_ref[rs, :], w_ref[...])
            y_ref[rs, :] = y
            x1 = x_ref[rs, :] + gt_ref[...] * (y * _msq_rsqrt(y) * gp_ref[...])
            x1_ref[rs, :] = x1
            h2 = (x1 * _msq_rsqrt(x1) * g2_ref[...]) * (1.0 + sc_ref[...]) + sh_ref[...]
            h2_ref[rs, :] = h2.astype(BF16)

    vec = _full((1, d))
    return pl.pallas_call(
        body, grid=(s // ts,), name="mix_out",
        in_specs=[_rows(ts, d), _rows(ts, d), _full((d, d)), vec, vec, vec, vec, vec],
        out_specs=[_rows(ts, d), _rows(ts, d), _rows(ts, d)],
        out_shape=[_sds((s, d), F32), _sds((s, d), F32), _sds((s, d), BF16)],
        compiler_params=_params(("parallel",)),
    )(ycat, x, w_out, gt_m, g_post, g_pre2, sc_f, sh_f)


def _ffn_cols(j):
    per = (2 * D_FF // N_CHIPS) // FFN_CHUNK
    return j // per, (j % per) * FFN_CHUNK, j * FFN_CHUNK


def _ffn_fwd(h2, x1, tgt, w_up4, w_down, fw, fb, gt_f, g_post, ts=256):
    s, d = x1.shape
    nch = D_FF // FFN_CHUNK

    def body(h2_ref, x1_ref, tgt_ref, wup_ref, wdn_ref, fw_ref, fb_ref, gt_ref, gp_ref,
             up0_ref, pre_ref, act_ref, dy2_ref, dx2_ref, loss_ref, dgt_ref, dgp_ref, tail_ref):
        i = pl.program_id(0)

        @pl.when(i == 0)
        def _():
            tail_ref[...] = jnp.zeros_like(tail_ref)
            loss_ref[...] = jnp.zeros_like(loss_ref)
            dgt_ref[...] = jnp.zeros_like(dgt_ref)
            dgp_ref[...] = jnp.zeros_like(dgp_ref)

        hb = h2_ref[...]

        def up_project(j):
            sh_g, off, _ = _ffn_cols(j)
            return [_dot(hb, wup_ref[shard, :, off:off + FFN_CHUNK]).astype(BF16) for shard in (sh_g, sh_g + 2)]

        y2 = jnp.zeros((ts, d), F32)
        ahead = up_project(0)
        for j in range(nch):
            _, _, col = _ffn_cols(j)
            ubs = ahead
            if j + 1 < nch:
                ahead = up_project(j + 1)
            halves = []
            for ub, c0 in zip(ubs, (col, D_FF + col)):
                cs = slice(c0, c0 + FFN_CHUNK)
                up0_ref[:, cs] = ub
                u = ub.astype(F32)
                prev8 = tail_ref[:, cs]
                tail_ref[:, cs] = u[ts - SUBLANES:, :]
                halves.append(fw_ref[2:3, cs] * u + fw_ref[1:2, cs] * _shift_down(u, prev8, 1)
                              + fw_ref[0:1, cs] * _shift_down(u, prev8, 2) + fb_ref[:, cs])
                pre_ref[:, cs] = halves[-1].astype(BF16)
            act = (_gelu(halves[0]) * halves[1]).astype(BF16)
            act_ref[:, col:col + FFN_CHUNK] = act
            y2 = y2 + _dot(act, wdn_ref[col:col + FFN_CHUNK, :])
        r2 = _msq_rsqrt(y2)
        yn = y2 * r2
        yng = yn * gp_ref[...]
        e = x1_ref[...] + gt_ref[...] * yng - tgt_ref[...]
        loss_ref[...] += jnp.sum(e * e) * (0.5 / d)
        dx2 = e * (1.0 / d)
        dx2_ref[...] = dx2
        dgt_ref[...] += _colsum(dx2 * yng)
        dyng = dx2 * gt_ref[...]
        dgp_ref[...] += _colsum(dyng * yn)
        dy2_ref[...] = _rms_bwd(dyng * gp_ref[...], yn, r2).astype(BF16)

    vec = _full((1, d))
    return pl.pallas_call(
        body, grid=(s // ts,), name="ffn_fwd",
        in_specs=[_rows(ts, d), _rows(ts, d), _rows(ts, d), _RESIDENT, _RESIDENT,
                  _full((3, 2 * D_FF)), _full((1, 2 * D_FF)), vec, vec],
        out_specs=[_rows(ts, 2 * D_FF), _rows(ts, 2 * D_FF), _rows(ts, D_FF), _rows(ts, d), _rows(ts, d),
                   _full((1, 128)), vec, vec],
        out_shape=[_sds((s, 2 * D_FF), BF16), _sds((s, 2 * D_FF), BF16), _sds((s, D_FF), BF16), _sds((s, d), BF16),
                   _sds((s, d), F32), _sds((1, 128), F32), _sds((1, d), F32), _sds((1, d), F32)],
        scratch_shapes=[pltpu.VMEM((SUBLANES, 2 * D_FF), F32)],
        compiler_params=_params(("arbitrary",)),
    )(h2, x1, tgt, w_up4, w_down, fw, fb, gt_f, g_post)


def _shift_up_mxu(vb, up_mat, next8, k):
    t = vb.shape[0]
    main = _dot(up_mat, vb)
    tail = pltpu.roll(next8, SUBLANES - k, 0)
    row8 = lax.broadcasted_iota(jnp.int32, next8.shape, 0)
    last = main[t - SUBLANES:] + jnp.where(row8 >= SUBLANES - k, tail, 0.0)
    return jnp.concatenate([main[:t - SUBLANES], last], axis=0)


def _ffn_bwd_a(dy2, pre, up0, w_down, fw, ts=256):
    s, d = dy2.shape
    nt = s // ts
    nch = D_FF // FFN_CHUNK
    wide = 2 * D_FF
    up_mats = jnp.stack([jnp.eye(ts, k=1, dtype=BF16), jnp.eye(ts, k=2, dtype=BF16)])

    def body(dy2_ref, pre_ref, up0_ref, wdn_ref, fw_ref, um_ref, dup0_ref, dfw_ref, dfb_ref, next_ref):
        i = pl.program_id(0)

        @pl.when(i == 0)
        def _():
            next_ref[...] = jnp.zeros_like(next_ref)
            dfw_ref[...] = jnp.zeros_like(dfw_ref)
            dfb_ref[...] = jnp.zeros_like(dfb_ref)

        dyb = dy2_ref[...]
        for j in range(nch):
            _, _, col = _ffn_cols(j)
            dact = _dot_nt(dyb, wdn_ref[col:col + FFN_CHUNK, :])
            gl, dgl = _gelu_and_grad(pre_ref[:, col:col + FFN_CHUNK].astype(F32))
            dpre = (dact * pre_ref[:, D_FF + col:D_FF + col + FFN_CHUNK].astype(F32) * dgl, dact * gl)
            for half, c0 in enumerate((col, D_FF + col)):
                cs = slice(c0, c0 + FFN_CHUNK)
                dp = dpre[half]
                dpb = dp.astype(BF16)
                nxt = next_ref[:, cs]
                next_ref[:, cs] = dpb.astype(F32)[0:SUBLANES, :]
                su1 = _shift_up_mxu(dpb, um_ref[0], nxt, 1)
                su2 = _shift_up_mxu(dpb, um_ref[1], nxt, 2)
                u = up0_ref[:, cs].astype(F32)
                dfb_ref[:, cs] += _colsum(dp)
                dfw_ref[2:3, cs] += _colsum(dp * u)
                dfw_ref[1:2, cs] += _colsum(su1 * u)
                dfw_ref[0:1, cs] += _colsum(su2 * u)
                dup0 = fw_ref[2:3, cs] * dp + fw_ref[1:2, cs] * su1 + fw_ref[0:1, cs] * su2
                dup0_ref[:, cs] = dup0.astype(BF16)

    return pl.pallas_call(
        body, grid=(nt,), name="ffn_bwd_a",
        in_specs=[_rows(ts, d, nt), _rows(ts, wide, nt), _rows(ts, wide, nt), _RESIDENT,
                  _full((3, wide)), _full((2, ts, ts))],
        out_specs=[_rows(ts, wide, nt), _full((3, wide)), _full((1, wide))],
        out_shape=[_sds((s, wide), BF16), _sds((3, wide), F32), _sds((1, wide), F32)],
        scratch_shapes=[pltpu.VMEM((SUBLANES, wide), F32)],
        compiler_params=_params(("arbitrary",)),
    )(dy2, pre, up0, w_down, fw, up_mats)


def _ffn_bwd_b(dup0, x1, y, dx2, w_up4, g_pre2, sc_f, sh_f, gt_m, g_post_m, ts=512):
    s, d = x1.shape
    shard_cols = 2 * D_FF // N_CHIPS

    def body(dup_ref, x1_ref, y_ref, dx2_ref, wup_ref, g2_ref, sc_ref, sh_ref, gt_ref, gp_ref,
             dx1_ref, dy_ref, dsh_ref, dsc_ref, dg2_ref, dgt_ref, dgp_ref):
        i = pl.program_id(0)

        @pl.when(i == 0)
        def _():
            for ref in (dsh_ref, dsc_ref, dg2_ref, dgt_ref, dgp_ref):
                ref[...] = jnp.zeros_like(ref)

        for rs in _sub_tiles(ts):
            dh2 = jnp.zeros((SUB_ROWS, d), F32)
            for k in range(N_CHIPS):
                dh2 = dh2 + _dot_nt(dup_ref[rs, k * shard_cols:(k + 1) * shard_cols], wup_ref[k])
            x1v = x1_ref[rs, :]
            r2 = _msq_rsqrt(x1v)
            xn = x1v * r2
            hn = xn * g2_ref[...]
            dsh_ref[...] += _colsum(dh2)
            dsc_ref[...] += _colsum(dh2 * hn)
            dhn = dh2 * (1.0 + sc_ref[...])
            dg2_ref[...] += _colsum(dhn * xn)
            dx1 = dx2_ref[rs, :] + _rms_bwd(dhn * g2_ref[...], xn, r2)
            dx1_ref[rs, :] = dx1
            yv = y_ref[rs, :]
            ry = _msq_rsqrt(yv)
            yn = yv * ry
            dgt_ref[...] += _colsum(dx1 * (yn * gp_ref[...]))
            dyng = dx1 * gt_ref[...]
            dgp_ref[...] += _colsum(dyng * yn)
            dy_ref[rs, :] = _rms_bwd(dyng * gp_ref[...], yn, ry).astype(BF16)

    vec = _full((1, d))
    return pl.pallas_call(
        body, grid=(s // ts,), name="ffn_bwd_b",
        in_specs=[_rows(ts, 2 * D_FF), _rows(ts, d), _rows(ts, d), _rows(ts, d), _RESIDENT,
                  vec, vec, vec, vec, vec],
        out_specs=[_rows(ts, d), _rows(ts, d), vec, vec, vec, vec, vec],
        out_shape=[_sds((s, d), F32), _sds((s, d), BF16)] + [_sds((1, d), F32)] * 5,
        compiler_params=_params(("arbitrary",)),
    )(dup0, x1, y, dx2, w_up4, g_pre2, sc_f, sh_f, gt_m, g_post_m)


def _seqmix_bwd(z, hst, stash, dy, w_out, seq_params, ws_t, glo, ggo, ts=256):
    s = z.shape[0]
    nt = s // ts
    small_shapes = [(4, 512), (1, 512), (512, 512), (512, 512), (1, 512), (1, 512), (1, 512),
                    (1, 512), (1, 512), (4, 128, 128), (128, 4), (1, 512), (1, 512)]

    def body(lx_ref, hst_ref, hprev_ref, st_ref, dy_ref, wout_ref, cw_ref, cb_ref, bdr_ref, bdi_ref, br_ref,
             bi_ref, la_ref, ng_ref, nb_ref, ws_ref, bst_ref, wst_ref, glo_ref, ggo_ref, dz_ref, *rest):
        small_refs = rest[:13]
        (dcw_ref, dcb_ref, dwr_ref, dwi_ref, dbr_ref, dbi_ref, dspa_ref, dng_ref, dnb_ref, dws_ref, dbs_ref,
         dglo_ref, dggo_ref) = small_refs
        gcarry, anext, dxcnext, dv_scr = rest[13:]
        i = pl.program_id(0)

        @pl.when(i == 0)
        def _():
            for ref in small_refs:
                ref[...] = jnp.zeros_like(ref)
            gcarry[...] = jnp.zeros_like(gcarry)
            anext[...] = jnp.ones_like(anext)
            dxcnext[...] = jnp.zeros_like(dxcnext)

        first_tile = i == nt - 1
        xc, r, ig, a, mult = st_ref[ST_XC], st_ref[ST_R], st_ref[ST_IG], st_ref[ST_A], st_ref[ST_MULT]
        gl, u, spb, vhat = st_ref[ST_GL], st_ref[ST_U], st_ref[ST_SPB], st_ref[ST_VHAT]
        lx = lx_ref[...]
        h = hst_ref[...]
        hprev = _shift_down(h, jnp.where(first_tile, 0.0, hprev_ref[...]), 1)
        y_l = h * gl
        y_g = u * spb

        dycat = _dot_nt(dy_ref[...], wout_ref[...])
        rl = _msq_rsqrt(y_l)
        yln = y_l * rl
        dyl = dycat[:, 0:512]
        dglo_ref[...] += _colsum(dyl * yln)
        dy_l = _rms_bwd(dyl * glo_ref[...], yln, rl)
        rg = _msq_rsqrt(y_g)
        ygn = y_g * rg
        dyg = dycat[:, 512:1024]
        dggo_ref[...] += _colsum(dyg * ygn)
        dy_g = _rms_bwd(dyg * ggo_ref[...], ygn, rg)

        dz_ref[:, 512:1024] = (dy_l * h * st_ref[ST_DGL]).astype(BF16)
        a_up = _shift_up(a, anext[...], 1)
        acum, gloc = _scan_bwd(a_up, dy_l * gl)
        gg = gloc + acum * gcarry[...]
        gcarry[...] = gg[0:1, :]
        anext[...] = a[0:SUBLANES, :]
        da = gg * hprev
        t1 = gg * mult
        di = t1 * xc
        dxc = t1 * ig
        dmult = gg * ig * xc
        dla = da * a - dmult * (a * a / mult)
        dspa_ref[...] += _colsum(dla * r) * (-LRU_C)
        dpr = dla * ((-LRU_C) * _softplus(-la_ref[...])) * r * (1.0 - r)
        dpi = di * ig * (1.0 - ig)
        dbr_ref[...] += _colsum(dpr)
        dbi_ref[...] += _colsum(dpi)
        dprb = dpr.astype(BF16)
        dpib = dpi.astype(BF16)
        xcb = xc.astype(BF16)
        dwr_ref[...] += _dot_tn(xcb, dprb)
        dwi_ref[...] += _dot_tn(xcb, dpib)
        dxc = dxc + _dot_nt(dprb, bdr_ref[...]) + _dot_nt(dpib, bdi_ref[...])
        nxt = dxcnext[...]
        dxcnext[...] = dxc[0:SUBLANES, :]
        up1, up2, up3 = _shift_up(dxc, nxt, 1), _shift_up(dxc, nxt, 2), _shift_up(dxc, nxt, 3)
        dcb_ref[...] += _colsum(dxc)
        dcw_ref[3:4, :] += _colsum(dxc * lx)
        dcw_ref[2:3, :] += _colsum(up1 * lx)
        dcw_ref[1:2, :] += _colsum(up2 * lx)
        dcw_ref[0:1, :] += _colsum(up3 * lx)
        dlx = cw_ref[3:4, :] * dxc + cw_ref[2:3, :] * up1 + cw_ref[1:2, :] * up2 + cw_ref[0:1, :] * up3
        dz_ref[:, 0:512] = dlx.astype(BF16)

        dz_ref[:, 1024:1536] = (dy_g * spb * st_ref[ST_DU]).astype(BF16)
        dsp = dy_g * u
        vb = (vhat * ng_ref[...] + nb_ref[...]).astype(BF16)
        for n in range(ts // GMLP_BLOCK):
            rs = slice(n * GMLP_BLOCK, (n + 1) * GMLP_BLOCK)
            for g in range(GMLP_GROUPS):
                cs = slice(g * 128, (g + 1) * 128)
                dbs_ref[:, g:g + 1] += jnp.sum(dsp[rs, cs], axis=1, keepdims=True)
                blk = dsp[rs, cs].astype(BF16)
                dws_ref[g] += _dot_nt(blk, vb[rs, cs])
                dv_scr[rs, cs] = _dot(wst_ref[g], blk)
        dv = dv_scr[...]
        dng_ref[...] += _colsum(dv * vhat)
        dnb_ref[...] += _colsum(dv)
        dvh = dv * ng_ref[...]
        dvg = dvh - jnp.mean(dvh, axis=-1, keepdims=True) - vhat * jnp.mean(dvh * vhat, axis=-1, keepdims=True)
        dz_ref[:, 1536:2048] = (dvg * st_ref[ST_Q]).astype(BF16)

        @pl.when(i == nt - 1)
        def _():
            pos = lax.broadcasted_iota(jnp.int32, (GMLP_BLOCK, GMLP_BLOCK), 0) // CHUNK
            src = lax.broadcasted_iota(jnp.int32, (GMLP_BLOCK, GMLP_BLOCK), 1) // CHUNK
            for g in range(GMLP_GROUPS):
                dws_ref[g] = jnp.where(src <= pos, dws_ref[g], 0.0)
            dspa_ref[...] = dspa_ref[...] * (-_sigmoid(-la_ref[...]))

    in_specs = ([_rows(ts, 512, nt), _rows(ts, 512, nt), _halo_prev(ts, 512, SUBLANES, nt),
                 pl.BlockSpec((N_STASH, ts, 512), lambda i: (0, nt - 1 - i, 0)), _rows(ts, 1024, nt),
                 _full((1024, 1024))]
                + _seq_param_specs() + [_full((4, 128, 128)), _full((1, 512)), _full((1, 512))])
    return pl.pallas_call(
        body, grid=(nt,), name="seqmix_bwd",
        in_specs=in_specs,
        out_specs=[_rows(ts, 2048, nt)] + [_full(sh) for sh in small_shapes],
        out_shape=[_sds((s, 2048), BF16)] + [_sds(sh, F32) for sh in small_shapes],
        scratch_shapes=[pltpu.VMEM((1, 512), F32), pltpu.VMEM((SUBLANES, 512), F32),
                        pltpu.VMEM((SUBLANES, 512), F32), pltpu.VMEM((ts, 512), F32)],
        compiler_params=_params(("arbitrary",)),
    )(z, hst, hst, stash, dy, w_out, *seq_params, ws_t, glo, ggo)


def _seqmix_bwd_recomputing_unused(z, hst, dy, w_out, seq_params, ws_t, glo, ggo, ts=256):
    s = z.shape[0]
    nt = s // ts
    small_shapes = [(4, 512), (1, 512), (512, 512), (512, 512), (1, 512), (1, 512), (1, 512),
                    (1, 512), (1, 512), (4, 128, 128), (128, 4), (1, 512), (1, 512)]

    def body(z_ref, zprev_ref, hst_ref, hprev_ref, dy_ref, wout_ref, *rest):
        p = rest[:11]
        wst_ref, glo_ref, ggo_ref = rest[11:14]
        dz_ref = rest[14]
        (dcw_ref, dcb_ref, dwr_ref, dwi_ref, dbr_ref, dbi_ref, dspa_ref, dng_ref, dnb_ref, dws_ref, dbs_ref,
         dglo_ref, dggo_ref) = rest[15:28]
        gcarry, anext, dxcnext, sp_scr, dv_scr = rest[28:]
        i = pl.program_id(0)

        @pl.when(i == 0)
        def _():
            for ref in rest[15:28]:
                ref[...] = jnp.zeros_like(ref)
            gcarry[...] = jnp.zeros_like(gcarry)
            anext[...] = jnp.ones_like(anext)
            dxcnext[...] = jnp.zeros_like(dxcnext)

        first_tile = i == nt - 1
        f = _seq_recompute(z_ref, zprev_ref, first_tile, p)
        xc, r, ig, a, mult, lx = f["xc"], f["r"], f["ig"], f["a"], f["mult"], f["lx"]
        h = hst_ref[...]
        hprev = _shift_down(h, jnp.where(first_tile, 0.0, hprev_ref[...]), 1)
        gl, dgl = _gelu_and_grad(f["lg"])
        y_l = h * gl
        gm = _gmlp_fwd(f["gu"], f["gv"], p[7], p[8], p[9], p[10], sp_scr)
        y_g = gm["y_g"]

        dycat = _dot_nt(dy_ref[...], wout_ref[...])
        rl = _msq_rsqrt(y_l)
        yln = y_l * rl
        dyl = dycat[:, 0:512]
        dglo_ref[...] += _colsum(dyl * yln)
        dy_l = _rms_bwd(dyl * glo_ref[...], yln, rl)
        rg = _msq_rsqrt(y_g)
        ygn = y_g * rg
        dyg = dycat[:, 512:1024]
        dggo_ref[...] += _colsum(dyg * ygn)
        dy_g = _rms_bwd(dyg * ggo_ref[...], ygn, rg)

        dz_ref[:, 512:1024] = (dy_l * h * dgl).astype(BF16)
        a_up = _shift_up(a, anext[...], 1)
        acum, gloc = _scan_bwd(a_up, dy_l * gl)
        gg = gloc + acum * gcarry[...]
        gcarry[...] = gg[0:1, :]
        anext[...] = a[0:SUBLANES, :]
        da = gg * hprev
        t1 = gg * mult
        di = t1 * xc
        dxc = t1 * ig
        dmult = gg * ig * xc
        dla = da * a - dmult * (a * a / mult)
        spa = f["spa"]
        dspa_ref[...] += _colsum(dla * r) * (-LRU_C)
        dpr = dla * ((-LRU_C) * spa) * r * (1.0 - r)
        dpi = di * ig * (1.0 - ig)
        dbr_ref[...] += _colsum(dpr)
        dbi_ref[...] += _colsum(dpi)
        dprb = dpr.astype(BF16)
        dpib = dpi.astype(BF16)
        dwr_ref[...] += _dot_tn(f["xcb"], dprb)
        dwi_ref[...] += _dot_tn(f["xcb"], dpib)
        dxc = dxc + _dot_nt(dprb, p[2][...]) + _dot_nt(dpib, p[3][...])
        dcb_ref[...] += _colsum(dxc)
        dcw_ref[3:4, :] += _colsum(dxc * lx)
        dcw_ref[2:3, :] += _colsum(dxc * f["s1"])
        dcw_ref[1:2, :] += _colsum(dxc * f["s2"])
        dcw_ref[0:1, :] += _colsum(dxc * f["s3"])
        nxt = dxcnext[...]
        dxcnext[...] = dxc[0:SUBLANES, :]
        cw_ref = p[0]
        dlx = (cw_ref[3:4, :] * dxc + cw_ref[2:3, :] * _shift_up(dxc, nxt, 1)
               + cw_ref[1:2, :] * _shift_up(dxc, nxt, 2) + cw_ref[0:1, :] * _shift_up(dxc, nxt, 3))
        dz_ref[:, 0:512] = dlx.astype(BF16)

        dz_ref[:, 1024:1536] = (dy_g * gm["spb"] * gm["du"]).astype(BF16)
        dsp = dy_g * gm["u"]
        vb = gm["vb"]
        for n in range(ts // GMLP_BLOCK):
            rs = slice(n * GMLP_BLOCK, (n + 1) * GMLP_BLOCK)
            for g in range(GMLP_GROUPS):
                cs = slice(g * 128, (g + 1) * 128)
                dbs_ref[:, g:g + 1] += jnp.sum(dsp[rs, cs], axis=1, keepdims=True)
                blk = dsp[rs, cs].astype(BF16)
                dws_ref[g] += _dot_nt(blk, vb[rs, cs])
                dv_scr[rs, cs] = _dot(wst_ref[g], blk)
        dv = dv_scr[...]
        vhat = gm["vhat"]
        dng_ref[...] += _colsum(dv * vhat)
        dnb_ref[...] += _colsum(dv)
        dvh = dv * p[7][...]
        dvg = gm["rstd"] * (dvh - jnp.mean(dvh, axis=-1, keepdims=True)
                            - vhat * jnp.mean(dvh * vhat, axis=-1, keepdims=True))
        dz_ref[:, 1536:2048] = (dvg * gm["dvg"]).astype(BF16)

        @pl.when(i == nt - 1)
        def _():
            pos = lax.broadcasted_iota(jnp.int32, (GMLP_BLOCK, GMLP_BLOCK), 0) // CHUNK
            src = lax.broadcasted_iota(jnp.int32, (GMLP_BLOCK, GMLP_BLOCK), 1) // CHUNK
            for g in range(GMLP_GROUPS):
                dws_ref[g] = jnp.where(src <= pos, dws_ref[g], 0.0)
            dspa_ref[...] = dspa_ref[...] * (-_sigmoid(-p[6][...]))

    in_specs = (_seq_specs(ts, nt, True)
                + [_rows(ts, 512, nt), _halo_prev(ts, 512, SUBLANES, nt), _rows(ts, 1024, nt), _full((1024, 1024))]
                + _seq_param_specs() + [_full((4, 128, 128)), _full((1, 512)), _full((1, 512))])
    return pl.pallas_call(
        body, grid=(nt,), name="seqmix_bwd",
        in_specs=in_specs,
        out_specs=[_rows(ts, 2048, nt)] + [_full(sh) for sh in small_shapes],
        out_shape=[_sds((s, 2048), BF16)] + [_sds(sh, F32) for sh in small_shapes],
        scratch_shapes=[pltpu.VMEM((1, 512), F32), pltpu.VMEM((SUBLANES, 512), F32),
                        pltpu.VMEM((SUBLANES, 512), F32), pltpu.VMEM((ts, 512), F32), pltpu.VMEM((ts, 512), F32)],
        compiler_params=_params(("arbitrary",)),
    )(z, z, hst, hst, dy, w_out, *seq_params, ws_t, glo, ggo)


def _mix_in_bwd(x, dz, dx1, w_in4, g, sc, ts=512):
    s, d = x.shape

    def body(x_ref, dz_ref, dx1_ref, w_ref, g_ref, sc_ref, gx_ref, dsh_ref, dsc_ref, dg_ref):
        i = pl.program_id(0)

        @pl.when(i == 0)
        def _():
            for ref in (dsh_ref, dsc_ref, dg_ref):
                ref[...] = jnp.zeros_like(ref)

        for rs in _sub_tiles(ts):
            dh = jnp.zeros((SUB_ROWS, d), F32)
            for k in range(N_CHIPS):
                dh = dh + _dot_nt(dz_ref[rs, k * 512:(k + 1) * 512], w_ref[k])
            xv = x_ref[rs, :]
            r = _msq_rsqrt(xv)
            xn = xv * r
            dsh_ref[...] += _colsum(dh)
            dsc_ref[...] += _colsum(dh * (xn * g_ref[...]))
            dhn = dh * (1.0 + sc_ref[...])
            dg_ref[...] += _colsum(dhn * xn)
            gx_ref[rs, :] = dx1_ref[rs, :] + _rms_bwd(dhn * g_ref[...], xn, r)

    vec = _full((1, d))
    return pl.pallas_call(
        body, grid=(s // ts,), name="mix_in_bwd",
        in_specs=[_rows(ts, d), _rows(ts, 2048), _rows(ts, d), _full(w_in4.shape), vec, vec],
        out_specs=[_rows(ts, d), vec, vec, vec],
        out_shape=[_sds((s, d), F32)] + [_sds((1, d), F32)] * 3,
        compiler_params=_params(("arbitrary",)),
    )(x, dz, dx1, w_in4, g, sc)


def _wgrad(a, b, n_chunks, name, chunk_major, ts=2048):
    s, m = a.shape
    n = b.shape[1]
    nc = n // n_chunks
    nt = s // ts

    def body(a_ref, b_ref, o_ref, acc):
        i = pl.program_id(1)

        @pl.when(i == 0)
        def _():
            acc[...] = jnp.zeros_like(acc)

        acc[...] += _dot_tn(a_ref[...], b_ref[...])

        @pl.when(i == nt - 1)
        def _():
            if chunk_major:
                o_ref[0] = acc[...].astype(BF16)
            else:
                o_ref[...] = acc[...].astype(BF16)

    if chunk_major:
        out_spec, out_shape = pl.BlockSpec((1, m, nc), lambda c, i: (c, 0, 0)), _sds((n_chunks, m, nc), BF16)
    else:
        out_spec, out_shape = pl.BlockSpec((m, nc), lambda c, i: (0, c)), _sds((m, n), BF16)
    return pl.pallas_call(
        body, grid=(n_chunks, nt), name=name,
        in_specs=[pl.BlockSpec((ts, m), lambda c, i: (i, 0)), pl.BlockSpec((ts, nc), lambda c, i: (i, c))],
        out_specs=out_spec,
        out_shape=out_shape,
        scratch_shapes=[pltpu.VMEM((m, nc), F32)],
        compiler_params=_params(("parallel", "arbitrary")),
    )(a, b)


def _block_diag(w):
    heads, hd, _ = w.shape
    eye = jnp.eye(heads, dtype=w.dtype)
    return (eye[:, None, :, None] * w[:, :, None, :]).reshape(heads * hd, heads * hd)


def _diag_blocks(m):
    hd = LRU_WIDTH // LRU_HEADS
    m4 = m.reshape(LRU_HEADS, hd, LRU_HEADS, hd)
    return jnp.stack([m4[k, :, k, :] for k in range(LRU_HEADS)])


def _seq_params(small):
    row = lambda v: v.reshape(1, -1)
    pos = jnp.arange(GMLP_BLOCK)
    mask = (pos[None, :] // CHUNK) <= (pos[:, None] // CHUNK)
    ws = jnp.where(mask[None], small["w_spatial"], 0.0)
    seq_params = (small["conv_w"], row(small["conv_b"]),
                  _block_diag(small["w_rgate"]).astype(BF16), _block_diag(small["w_igate"]).astype(BF16),
                  row(small["b_rgate"]), row(small["b_igate"]), row(small["lru_a"]),
                  row(small["v_norm_g"]), row(small["v_norm_b"]), ws.astype(BF16), small["b_spatial"].T)
    return seq_params, jnp.swapaxes(ws, 1, 2).astype(BF16)


_ANY = pl.BlockSpec(memory_space=pl.ANY)
_CHIP_FLIPS = ((1, 0), (0, 1), (1, 1))


def _position():
    return lax.axis_index("x"), lax.axis_index("y"), lax.axis_index("c")


def _flip(v, f):
    return 1 - v if f else v


def _remote(src, dst, send_sem, recv_sem, peer):
    return pltpu.make_async_remote_copy(src_ref=src, dst_ref=dst, send_sem=send_sem, recv_sem=recv_sem,
                                        device_id=peer, device_id_type=MESH)


def _allgather8(block, name, reduce):
    r, n = block.shape

    def body(x_ref, out_ref, *scratch):
        if reduce:
            gath, send_sems, recv_sems, loc_sem = scratch
        else:
            gath = out_ref
            send_sems, recv_sems, loc_sem = scratch
        x, y, c = _position()
        me = 4 * x + 2 * y + c
        loc = pltpu.make_async_copy(x_ref, gath.at[me], loc_sem)
        loc.start()
        peers = []
        for k in range(1, N_DEV):
            px, py, pc = _flip(x, k & 4), _flip(y, k & 2), _flip(c, k & 1)
            peers.append((px, py, pc))
            _remote(x_ref, gath.at[me], send_sems.at[k - 1], recv_sems.at[k - 1], (px, py, pc)).start()
        for k, (px, py, pc) in enumerate(peers):
            src = 4 * px + 2 * py + pc
            _remote(x_ref, gath.at[src], send_sems.at[k], recv_sems.at[k], (px, py, pc)).wait_recv()
        for k, peer in enumerate(peers):
            _remote(x_ref, gath.at[me], send_sems.at[k], recv_sems.at[k], peer).wait_send()
        loc.wait()
        if reduce:
            acc = gath[0]
            for k in range(1, N_DEV):
                acc = acc + gath[k]
            out_ref[...] = acc

    sems = [pltpu.SemaphoreType.DMA((N_DEV - 1,)), pltpu.SemaphoreType.DMA((N_DEV - 1,)), pltpu.SemaphoreType.DMA]
    if reduce:
        out_shape = _sds((r, n), F32)
        scratch = [pltpu.VMEM((N_DEV, r, n), F32)] + sems
    else:
        out_shape = _sds((N_DEV, r, n), F32)
        scratch = sems
    return pl.pallas_call(
        body, name=name, out_shape=out_shape,
        in_specs=[pl.BlockSpec(memory_space=pltpu.VMEM)], out_specs=pl.BlockSpec(memory_space=pltpu.VMEM),
        scratch_shapes=scratch,
        compiler_params=pltpu.CompilerParams(vmem_limit_bytes=VMEM_LIMIT_BYTES),
    )(block)


def _half(ref, c, rows):
    hr = rows // 2
    return ref.at[pl.ds(pl.multiple_of(c * hr, BF16_SUBLANES), hr), :]


def _gather_weights(shards):
    na = len(shards)

    def body(*refs):
        ins, outs = refs[:na], refs[na:2 * na]
        ici_send, ici_recv, d2d_send, d2d_recv, loc_sem = refs[2 * na:]
        x, y, c = _position()
        chip = 2 * x + y
        sibling = (x, y, 1 - c)
        local = []
        for a in range(na):
            local.append(pltpu.make_async_copy(ins[a], outs[a].at[chip], loc_sem.at[a]))
            local[-1].start()
        sends = []
        for a in range(na):
            rows = shards[a].shape[0]
            for j, (fx, fy) in enumerate(_CHIP_FLIPS):
                peer = (_flip(x, fx), _flip(y, fy), c)
                sends.append(_remote(_half(ins[a], c, rows), _half(outs[a].at[chip], c, rows),
                                     ici_send.at[a * 3 + j], ici_recv.at[a * 3 + j], peer))
                sends[-1].start()
        for a in range(na):
            rows = shards[a].shape[0]
            for j, (fx, fy) in enumerate(_CHIP_FLIPS):
                src_chip = 2 * _flip(x, fx) + _flip(y, fy)
                landed = _half(outs[a].at[src_chip], c, rows)
                _remote(landed, landed, ici_send.at[a * 3 + j], ici_recv.at[a * 3 + j], sibling).wait_recv()
                sends.append(_remote(landed, landed, d2d_send.at[a * 3 + j], d2d_recv.at[a * 3 + j], sibling))
                sends[-1].start()
        for a in range(na):
            rows = shards[a].shape[0]
            for j, (fx, fy) in enumerate(_CHIP_FLIPS):
                src_chip = 2 * _flip(x, fx) + _flip(y, fy)
                other = _half(outs[a].at[src_chip], 1 - c, rows)
                _remote(other, other, d2d_send.at[a * 3 + j], d2d_recv.at[a * 3 + j], sibling).wait_recv()
        for cp in sends:
            cp.wait_send()
        for cp in local:
            cp.wait()

    return pl.pallas_call(
        body, name="gather_weights",
        out_shape=[_sds((N_CHIPS,) + w.shape, w.dtype) for w in shards],
        in_specs=[_ANY] * na, out_specs=[_ANY] * na,
        scratch_shapes=[pltpu.SemaphoreType.DMA((3 * na,))] * 4 + [pltpu.SemaphoreType.DMA((na,))],
    )(*shards)


def _swap_halves(parts, name):
    na = len(parts)

    def body(*refs):
        ins, outs = refs[:na], refs[na:2 * na]
        send_sems, recv_sems = refs[2 * na:]
        x, y, c = _position()
        sibling = (x, y, 1 - c)
        cps = []
        for a in range(na):
            hr = parts[a].shape[1] // 2
            src = ins[a].at[:, pl.ds(pl.multiple_of((1 - c) * hr, BF16_SUBLANES), hr), :]
            cps.append(_remote(src, outs[a], send_sems.at[a], recv_sems.at[a], sibling))
            cps[-1].start()
        for cp in cps:
            cp.wait()

    return pl.pallas_call(
        body, name=name,
        out_shape=[_sds((N_CHIPS, p.shape[1] // 2, p.shape[2]), p.dtype) for p in parts],
        in_specs=[_ANY] * na, out_specs=[_ANY] * na,
        scratch_shapes=[pltpu.SemaphoreType.DMA((na,))] * 2,
    )(*parts)


def _chip_sum(part, recv, pos_arr, name):
    _, rows, cols = part.shape
    hr = rows // 2

    def body(pos_ref, p_ref, r_ref, o_ref, g_ref):
        total = (p_ref[...].astype(F32) + r_ref[...].astype(F32)).astype(BF16)
        o_ref[...] = total

        @pl.when(pl.program_id(0) == pos_ref[1])
        def _():
            g_ref[0] = total

    grid_spec = pltpu.PrefetchScalarGridSpec(
        num_scalar_prefetch=1, grid=(N_CHIPS,),
        in_specs=[pl.BlockSpec((1, hr, cols), lambda k, pos: (k, pos[0], 0)),
                  pl.BlockSpec((1, hr, cols), lambda k, pos: (k, 0, 0))],
        out_specs=[pl.BlockSpec((1, hr, cols), lambda k, pos: (k, 0, 0)),
                   pl.BlockSpec((1, 1, hr, cols), lambda k, pos: (0, pos[1], 0, 0))])
    return pl.pallas_call(
        body, name=name, grid_spec=grid_spec,
        out_shape=[_sds((N_CHIPS, hr, cols), BF16), _sds((2, N_CHIPS, hr, cols), BF16)],
        compiler_params=_params(("arbitrary",)),
    )(pos_arr, part, recv)


def _exchange_chips(sums):
    na = len(sums)

    def body(*refs):
        ins, outs = refs[:na], refs[na:2 * na]
        send_sems, recv_sems, loc_sem = refs[2 * na:]
        x, y, c = _position()
        chip = 2 * x + y
        local = []
        for a in range(na):
            local.append(pltpu.make_async_copy(ins[a].at[chip], outs[a].at[chip], loc_sem.at[a]))
            local[-1].start()
        cps = []
        for a in range(na):
            for j, (fx, fy) in enumerate(_CHIP_FLIPS):
                px, py = _flip(x, fx), _flip(y, fy)
                cps.append(_remote(ins[a].at[2 * px + py], outs[a].at[chip],
                                   send_sems.at[a * 3 + j], recv_sems.at[a * 3 + j], (px, py, c)))
                cps[-1].start()
        for a in range(na):
            for j, (fx, fy) in enumerate(_CHIP_FLIPS):
                src_chip = 2 * _flip(x, fx) + _flip(y, fy)
                landed = outs[a].at[src_chip]
                _remote(landed, landed, send_sems.at[a * 3 + j], recv_sems.at[a * 3 + j], (x, y, c)).wait_recv()
        for cp in cps:
            cp.wait_send()
        for cp in local:
            cp.wait()

    return pl.pallas_call(
        body, name="exchange_chips",
        out_shape=[_sds(s.shape, s.dtype) for s in sums],
        in_specs=[_ANY] * na, out_specs=[_ANY] * na,
        scratch_shapes=[pltpu.SemaphoreType.DMA((3 * na,))] * 2 + [pltpu.SemaphoreType.DMA((na,))],
    )(*sums)


def _sum_chips(gath, name, tr=128):
    _, hr, cols = gath.shape
    tr = min(tr, hr)

    def body(g_ref, o_ref):
        acc = g_ref[0].astype(F32)
        for k in range(1, N_CHIPS):
            acc = acc + g_ref[k].astype(F32)
        o_ref[...] = acc

    return pl.pallas_call(
        body, name=name, grid=(hr // tr,),
        in_specs=[pl.BlockSpec((N_CHIPS, tr, cols), lambda i: (0, i, 0))],
        out_specs=pl.BlockSpec((tr, cols), lambda i: (i, 0)),
        out_shape=_sds((hr, cols), F32),
        compiler_params=_params(("parallel",)),
    )(gath)


def _join_halves(halves):
    na = len(halves)

    def body(*refs):
        ins, outs = refs[:na], refs[na:2 * na]
        send_sems, recv_sems, loc_sem = refs[2 * na:]
        x, y, c = _position()
        sibling = (x, y, 1 - c)
        cps, local = [], []
        for a in range(na):
            rows = 2 * halves[a].shape[0]
            mine = _half(outs[a], c, rows)
            local.append(pltpu.make_async_copy(ins[a], mine, loc_sem.at[a]))
            local[-1].start()
            cps.append(_remote(ins[a], mine, send_sems.at[a], recv_sems.at[a], sibling))
            cps[-1].start()
        for a in range(na):
            rows = 2 * halves[a].shape[0]
            other = _half(outs[a], 1 - c, rows)
            _remote(ins[a], other, send_sems.at[a], recv_sems.at[a], sibling).wait_recv()
        for cp in cps:
            cp.wait_send()
        for cp in local:
            cp.wait()

    return pl.pallas_call(
        body, name="join_halves",
        out_shape=[_sds((2 * h.shape[0], h.shape[1]), h.dtype) for h in halves],
        in_specs=[_ANY] * na, out_specs=[_ANY] * na,
        scratch_shapes=[pltpu.SemaphoreType.DMA((na,))] * 3,
    )(*halves)


_HBM = pl.BlockSpec(memory_space=pltpu.HBM)
_SEM = pl.BlockSpec(memory_space=pltpu.SEMAPHORE)
_EFFECT = pltpu.SideEffectType.DATAFLOW_SIDE_EFFECTING


def _in_hbm(a):
    return pltpu.with_memory_space_constraint(a, pltpu.HBM)


def _split_start(srcs, lands, plan, n_copies, after, name):
    ns, nl = len(srcs), len(lands)
    bufs = list(srcs) + list(lands)

    def body(*refs):
        send_sems, recv_sems = refs[ns + nl + 1], refs[ns + nl + 2]
        token = refs[-1]
        for k, (src, dst, peer) in enumerate(plan(refs[:ns], refs[ns:ns + nl])):
            _remote(src, dst, send_sems.at[k], recv_sems.at[k], peer).start()
        token[...] = jnp.zeros_like(token)

    out = pl.pallas_call(
        body, name=name,
        out_shape=(pltpu.SemaphoreType.DMA((n_copies,)), pltpu.SemaphoreType.DMA((n_copies,)),
                   *[pltpu.HBM(b.shape, b.dtype) for b in bufs], _sds((SUBLANES, 128), F32)),
        in_specs=[_HBM] * (ns + nl) + [_ANY],
        out_specs=(_SEM, _SEM, *[_HBM] * (ns + nl), pl.BlockSpec(memory_space=pltpu.VMEM)),
        input_output_aliases={i: 2 + i for i in range(ns + nl)},
        compiler_params=pltpu.CompilerParams(has_side_effects=_EFFECT),
    )(*[_in_hbm(b) for b in bufs], after)
    return out[0], out[1], list(out[2:2 + ns]), list(out[2 + ns:2 + ns + nl]), out[-1]


def _split_wait(send_sems, recv_sems, srcs, lands, plan, after, name):
    ns, nl = len(srcs), len(lands)
    bufs = list(srcs) + list(lands)

    def body(*refs):
        send_ref, recv_ref = refs[ns + nl], refs[ns + nl + 1]
        me = _position()
        for k, src, dst in plan(refs[:ns], refs[ns:ns + nl]):
            cp = _remote(src, dst, send_ref.at[k], recv_ref.at[k], me)
            cp.wait_send()
            cp.wait_recv()

    out = pl.pallas_call(
        body, name=name,
        out_shape=[pltpu.HBM(b.shape, b.dtype) for b in bufs],
        in_specs=[_HBM] * (ns + nl) + [_SEM, _SEM, _ANY],
        out_specs=[_HBM] * (ns + nl),
        input_output_aliases={i: i for i in range(ns + nl)},
        compiler_params=pltpu.CompilerParams(has_side_effects=_EFFECT),
    )(*bufs, send_sems, recv_sems, after)
    return list(out[:ns]), list(out[ns:])


def _gather_plan(rows_of):
    def start(src_refs, land_refs):
        x, y, c = _position()
        chip = 2 * x + y
        out = []
        for a, rows in enumerate(rows_of):
            mine = _half(land_refs[a].at[chip], c, rows)
            out.extend((mine, mine, (_flip(x, fx), _flip(y, fy), c)) for fx, fy in _CHIP_FLIPS)
        return out

    def wait(src_refs, land_refs):
        x, y, c = _position()
        chip = 2 * x + y
        out = []
        for a, rows in enumerate(rows_of):
            for j, (fx, fy) in enumerate(_CHIP_FLIPS):
                src_chip = 2 * _flip(x, fx) + _flip(y, fy)
                out.append((3 * a + j, _half(land_refs[a].at[chip], c, rows),
                            _half(land_refs[a].at[src_chip], c, rows)))
        return out

    return start, wait


def _swap_halves_plan(half_rows):
    def slices(src_refs, c):
        return [src_refs[a].at[:, pl.ds(pl.multiple_of((1 - c) * hr, BF16_SUBLANES), hr), :]
                for a, hr in enumerate(half_rows)]

    def start(src_refs, land_refs):
        x, y, c = _position()
        return [(src, land_refs[a], (x, y, 1 - c)) for a, src in enumerate(slices(src_refs, c))]

    def wait(src_refs, land_refs):
        _, _, c = _position()
        return [(a, src, land_refs[a]) for a, src in enumerate(slices(src_refs, c))]

    return start, wait


def _swap_gathered_plan(n_arrays):
    def start(src_refs, land_refs):
        x, y, c = _position()
        return [(land_refs[a].at[0], land_refs[a].at[1], (x, y, 1 - c)) for a in range(n_arrays)]

    def wait(src_refs, land_refs):
        return [(a, land_refs[a].at[0], land_refs[a].at[1]) for a in range(n_arrays)]

    return start, wait


def _exchange_plan(n_arrays):
    def start(src_refs, land_refs):
        x, y, c = _position()
        chip = 2 * x + y
        out = []
        for a in range(n_arrays):
            for fx, fy in _CHIP_FLIPS:
                px, py = _flip(x, fx), _flip(y, fy)
                out.append((src_refs[a].at[2 * px + py], land_refs[a].at[0, chip], (px, py, c)))
        return out

    def wait(src_refs, land_refs):
        x, y, c = _position()
        out = []
        for a in range(n_arrays):
            for j, (fx, fy) in enumerate(_CHIP_FLIPS):
                src_chip = 2 * _flip(x, fx) + _flip(y, fy)
                out.append((3 * a + j, src_refs[a].at[src_chip], land_refs[a].at[0, src_chip]))
        return out

    return start, wait


def _forward_to_sibling(lands, name):
    na = len(lands)

    def body(*refs):
        land_refs = refs[na:2 * na]
        send_sems, recv_sems = refs[2 * na:]
        x, y, c = _position()
        sibling = (x, y, 1 - c)
        sends = []
        for a in range(na):
            rows = lands[a].shape[1]
            for j, (fx, fy) in enumerate(_CHIP_FLIPS):
                landed = _half(land_refs[a].at[2 * _flip(x, fx) + _flip(y, fy)], c, rows)
                sends.append(_remote(landed, landed, send_sems.at[3 * a + j], recv_sems.at[3 * a + j], sibling))
                sends[-1].start()
        for a in range(na):
            rows = lands[a].shape[1]
            for j, (fx, fy) in enumerate(_CHIP_FLIPS):
                other = _half(land_refs[a].at[2 * _flip(x, fx) + _flip(y, fy)], 1 - c, rows)
                _remote(other, other, send_sems.at[3 * a + j], recv_sems.at[3 * a + j], sibling).wait_recv()
        for cp in sends:
            cp.wait_send()

    return pl.pallas_call(
        body, name=name,
        out_shape=[_sds(l.shape, l.dtype) for l in lands],
        in_specs=[_ANY] * na, out_specs=[_ANY] * na,
        input_output_aliases={a: a for a in range(na)},
        scratch_shapes=[pltpu.SemaphoreType.DMA((3 * na,))] * 2,
    )(*lands)


def _swap_gathered(gath, name):
    na = len(gath)

    def body(*refs):
        gath_refs = refs[na:2 * na]
        send_sems, recv_sems = refs[2 * na:]
        x, y, c = _position()
        cps = [_remote(gath_refs[a].at[0], gath_refs[a].at[1], send_sems.at[a], recv_sems.at[a], (x, y, 1 - c))
               for a in range(na)]
        for cp in cps:
            cp.start()
        for cp in cps:
            cp.wait()

    return pl.pallas_call(
        body, name=name,
        out_shape=[_sds(g.shape, g.dtype) for g in gath],
        in_specs=[_ANY] * na, out_specs=[_ANY] * na,
        input_output_aliases={a: a for a in range(na)},
        scratch_shapes=[pltpu.SemaphoreType.DMA((na,))] * 2,
    )(*gath)


def _adam_gathered(w, gath, m, v, c_arr, name, tr=128):
    rows, cols = w.shape
    hr = rows // 2
    per = hr // tr

    def body(c_ref, w_ref, g_ref, m_ref, v_ref, go_ref, d_ref, nm_ref, nv_ref):
        g = g_ref[0, 0].astype(F32)
        for k in range(1, N_CHIPS):
            g = g + g_ref[0, k].astype(F32)
        go_ref[...] = g
        d_ref[...], nm_ref[...], nv_ref[...] = _adam_math(w_ref[...], g, m_ref[...], v_ref[...])

    def rows_of(h, i, c_ref):
        c = c_ref[0]
        return ((c + h - 2 * c * h) * per + i, 0)

    blk = pl.BlockSpec((tr, cols), rows_of)
    grid_spec = pltpu.PrefetchScalarGridSpec(
        num_scalar_prefetch=1, grid=(2, per),
        in_specs=[blk, pl.BlockSpec((1, N_CHIPS, tr, cols), lambda h, i, c_ref: (h, 0, i, 0)), blk, blk],
        out_specs=[blk] * 4)
    return pl.pallas_call(
        body, name=name, grid_spec=grid_spec, out_shape=[_sds(w.shape, F32)] * 4,
        compiler_params=_params(("arbitrary", "arbitrary")),
    )(c_arr, w, gath, m, v)


def _allreduce_small(block, name):
    r, n = block.shape
    hr = r // 2

    def body(x_ref, out_ref, sib, chipsum, gath, d2d_send, d2d_recv, ici_send, ici_recv):
        x, y, c = _position()
        chip = 2 * x + y
        sibling = (x, y, 1 - c)
        first = _remote(x_ref, sib, d2d_send.at[0], d2d_recv.at[0], sibling)
        first.start()
        first.wait()
        chipsum[...] = x_ref[...] + sib[...]
        mine = pl.ds(c * hr, hr)
        theirs = pl.ds((1 - c) * hr, hr)
        sends = []
        for j, (fx, fy) in enumerate(_CHIP_FLIPS):
            sends.append(_remote(chipsum.at[mine, :], gath.at[chip], ici_send.at[j], ici_recv.at[j],
                                 (_flip(x, fx), _flip(y, fy), c)))
            sends[-1].start()
        gath[chip] = chipsum[mine, :]
        for j, (fx, fy) in enumerate(_CHIP_FLIPS):
            landed = gath.at[2 * _flip(x, fx) + _flip(y, fy)]
            _remote(landed, landed, ici_send.at[j], ici_recv.at[j], sibling).wait_recv()
        for cp in sends:
            cp.wait_send()
        total = gath[0]
        for k in range(1, N_CHIPS):
            total = total + gath[k]
        out_ref[mine, :] = total
        last = _remote(out_ref.at[mine, :], out_ref.at[mine, :], d2d_send.at[1], d2d_recv.at[1], sibling)
        last.start()
        _remote(out_ref.at[theirs, :], out_ref.at[theirs, :], d2d_send.at[1], d2d_recv.at[1], sibling).wait_recv()
        last.wait_send()

    vmem = pl.BlockSpec(memory_space=pltpu.VMEM)
    return pl.pallas_call(
        body, name=name, out_shape=_sds((r, n), F32), in_specs=[vmem], out_specs=vmem,
        scratch_shapes=[pltpu.VMEM((r, n), F32), pltpu.VMEM((r, n), F32), pltpu.VMEM((N_CHIPS, hr, n), F32),
                        pltpu.SemaphoreType.DMA((2,)), pltpu.SemaphoreType.DMA((2,)),
                        pltpu.SemaphoreType.DMA((3,)), pltpu.SemaphoreType.DMA((3,))],
        compiler_params=pltpu.CompilerParams(vmem_limit_bytes=VMEM_LIMIT_BYTES),
    )(block)


def _cast_place(shards, chip_arr):
    na = len(shards)
    steps = 4

    def body(chip_ref, *refs):
        for a in range(na):
            refs[na + a][0] = refs[a][...].astype(BF16)

    grid_spec = pltpu.PrefetchScalarGridSpec(
        num_scalar_prefetch=1, grid=(steps,),
        in_specs=[pl.BlockSpec((s.shape[0] // steps, s.shape[1]), lambda i, ch: (i, 0)) for s in shards],
        out_specs=[pl.BlockSpec((1, s.shape[0] // steps, s.shape[1]), lambda i, ch: (ch[0], i, 0)) for s in shards])
    return pl.pallas_call(
        body, name="cast_place", grid_spec=grid_spec,
        out_shape=[_sds((N_CHIPS,) + s.shape, BF16) for s in shards],
        compiler_params=_params(("arbitrary",)),
    )(chip_arr, *shards)


def _silu(v):
    return v * _sigmoid(v)


def _ada_fwd(c8, w_ada):
    def body(c_ref, w_ref, o_ref):
        o_ref[...] = jnp.dot(_silu(c_ref[...]), w_ref[...], preferred_element_type=F32,
                             precision=lax.Precision.HIGHEST)

    return pl.pallas_call(
        body, name="ada_fwd", out_shape=_sds((N_DEV, w_ada.shape[1]), F32),
        compiler_params=pltpu.CompilerParams(vmem_limit_bytes=VMEM_LIMIT_BYTES),
    )(c8, w_ada)


def _mod_select(parts, b_ada, me_arr, after):
    cols = parts.shape[2]

    def body(me_ref, p_ref, b_ref, after_ref, o_ref):
        me = me_ref[0]
        for k in range(N_CHIPS):
            cs = slice(k * cols, (k + 1) * cols)
            o_ref[:, cs] = p_ref[2 * k, pl.ds(me, 1), :] + b_ref[:, cs]

    grid_spec = pltpu.PrefetchScalarGridSpec(
        num_scalar_prefetch=1, grid=(1,),
        in_specs=[pl.BlockSpec(parts.shape, lambda i, m: (0, 0, 0)), pl.BlockSpec(b_ada.shape, lambda i, m: (0, 0)),
                  _ANY],
        out_specs=pl.BlockSpec(b_ada.shape, lambda i, m: (0, 0)))
    return pl.pallas_call(body, name="mod_select", grid_spec=grid_spec, out_shape=_sds(b_ada.shape, F32))(
        me_arr, parts, b_ada, after)


def _ada_bwd(c8, dmod8, chip_arr, w, m, v, tr=256):
    d = c8.shape[1]
    cols = dmod8.shape[1] // N_CHIPS

    def body(chip_ref, c_ref, dm_ref, dmall_ref, w_ref, m_ref, v_ref, gw_ref, d_ref, nm_ref, nv_ref, gb_ref):
        g = lax.dot_general(_silu(c_ref[...]), dm_ref[...], (((0,), (0,)), ((), ())),
                            preferred_element_type=F32, precision=lax.Precision.HIGHEST)
        gw_ref[...] = g
        d_ref[...], nm_ref[...], nv_ref[...] = _adam_math(w_ref[...], g, m_ref[...], v_ref[...])
        acc = dmall_ref[0:1, :]
        for k in range(1, N_DEV):
            acc = acc + dmall_ref[k:k + 1, :]
        gb_ref[...] = acc

    rows = pl.BlockSpec((tr, cols), lambda i, ch: (i, 0))
    grid_spec = pltpu.PrefetchScalarGridSpec(
        num_scalar_prefetch=1, grid=(d // tr,),
        in_specs=[pl.BlockSpec((N_DEV, tr), lambda i, ch: (0, i)),
                  pl.BlockSpec((N_DEV, cols), lambda i, ch: (0, ch[0])),
                  pl.BlockSpec(dmod8.shape, lambda i, ch: (0, 0)), rows, rows, rows],
        out_specs=[rows] * 4 + [pl.BlockSpec((1, dmod8.shape[1]), lambda i, ch: (0, 0))])
    return pl.pallas_call(
        body, name="ada_bwd", grid_spec=grid_spec,
        out_shape=[_sds((d, cols), F32)] * 4 + [_sds((1, dmod8.shape[1]), F32)],
        compiler_params=_params(("arbitrary",)),
    )(chip_arr, c8, dmod8, dmod8, w, m, v)


def _adam_math(w, g, m, v):
    m = ADAM_B1 * m + (1.0 - ADAM_B1) * g
    v = ADAM_B2 * v + (1.0 - ADAM_B2) * (g * g)
    m_hat = m / (1.0 - ADAM_B1 ** ADAM_STEP)
    v_hat = v / (1.0 - ADAM_B2 ** ADAM_STEP)
    delta = -ADAM_LR * (m_hat / (jnp.sqrt(v_hat) + ADAM_EPS) + ADAM_WD * w)
    return delta, m, v


def _adam(w, g, m, v, name, tr=256):
    rows, cols = w.shape
    if rows % tr:
        tr = rows

    def body(w_ref, g_ref, m_ref, v_ref, d_ref, nm_ref, nv_ref):
        d_ref[...], nm_ref[...], nv_ref[...] = _adam_math(w_ref[...], g_ref[...], m_ref[...], v_ref[...])

    spec = pl.BlockSpec((tr, cols), lambda i: (i, 0))
    return pl.pallas_call(
        body, name=name, grid=(rows // tr,), in_specs=[spec] * 4, out_specs=[spec] * 3,
        out_shape=[_sds(w.shape, F32)] * 3, compiler_params=_params(("parallel",)),
    )(w, g, m, v)


def _adam_cols(w, g_full, m, v, chip_arr, name):
    rows, cols = w.shape

    def body(chip_ref, w_ref, g_ref, m_ref, v_ref, gs_ref, d_ref, nm_ref, nv_ref):
        g = g_ref[...]
        gs_ref[...] = g
        d_ref[...], nm_ref[...], nv_ref[...] = _adam_math(w_ref[...], g, m_ref[...], v_ref[...])

    own = pl.BlockSpec((rows, cols), lambda i, ch: (0, 0))
    grid_spec = pltpu.PrefetchScalarGridSpec(
        num_scalar_prefetch=1, grid=(1,),
        in_specs=[own, pl.BlockSpec((rows, cols), lambda i, ch: (0, ch[0])), own, own],
        out_specs=[own] * 4)
    return pl.pallas_call(body, name=name, grid_spec=grid_spec, out_shape=[_sds(w.shape, F32)] * 4)(
        chip_arr, w, g_full, m, v)


PACK_COLS = 512
SMALL_REPLICATED = ("g_mix_pre", "g_mix_post", "conv_b", "w_rgate", "b_rgate", "w_igate", "b_igate", "lru_a",
                    "v_norm_g", "v_norm_b", "w_spatial", "b_spatial", "g_lru_out", "g_gmlp_out", "g_ffn_pre",
                    "g_ffn_post", "ffn_conv_b")
SMALL_COLUMN_SHARDED = ("conv_w", "ffn_conv_w")


def _pack(arrays):
    flat = jnp.concatenate([a.reshape(1, -1) for a in arrays], axis=1)
    pad = (-flat.shape[1]) % (2 * LANES)
    if pad:
        flat = jnp.pad(flat, ((0, 0), (0, pad)))
    return flat.reshape(2, -1)


def _unpack(packed, shapes):
    flat = packed.reshape(1, -1)
    out, col = [], 0
    for shape in shapes:
        n = math.prod(shape)
        out.append(flat[:, col:col + n].reshape(shape))
        col += n
    return out


def kernel(x, c, w_ada, b_ada, g_mix_pre, g_mix_post, w_in, conv_w, conv_b, w_rgate, b_rgate, w_igate, b_igate, lru_a, v_norm_g, v_norm_b, w_spatial, b_spatial, g_lru_out, g_gmlp_out, w_out, g_ffn_pre, g_ffn_post, w_up, ffn_conv_w, ffn_conv_b, w_down, loss_target, m_w_ada, m_b_ada, m_g_mix_pre, m_g_mix_post, m_w_in, m_conv_w, m_conv_b, m_w_rgate, m_b_rgate, m_w_igate, m_b_igate, m_lru_a, m_v_norm_g, m_v_norm_b, m_w_spatial, m_b_spatial, m_g_lru_out, m_g_gmlp_out, m_w_out, m_g_ffn_pre, m_g_ffn_post, m_w_up, m_ffn_conv_w, m_ffn_conv_b, m_w_down, v_w_ada, v_b_ada, v_g_mix_pre, v_g_mix_post, v_w_in, v_conv_w, v_conv_b, v_w_rgate, v_b_rgate, v_w_igate, v_b_igate, v_lru_a, v_v_norm_g, v_v_norm_b, v_w_spatial, v_b_spatial, v_g_lru_out, v_g_gmlp_out, v_w_out, v_g_ffn_pre, v_g_ffn_post, v_w_up, v_ffn_conv_w, v_ffn_conv_b, v_w_down):
    args = dict(locals())
    names = ("w_ada", "b_ada", "g_mix_pre", "g_mix_post", "w_in", "conv_w", "conv_b", "w_rgate", "b_rgate",
             "w_igate", "b_igate", "lru_a", "v_norm_g", "v_norm_b", "w_spatial", "b_spatial", "g_lru_out",
             "g_gmlp_out", "w_out", "g_ffn_pre", "g_ffn_post", "w_up", "ffn_conv_w", "ffn_conv_b", "w_down")
    drop = lambda a: a if a.ndim == 2 else a[0]
    w = {n: drop(args[n]) for n in names}
    m = {n: drop(args["m_" + n]) for n in names}
    v = {n: drop(args["v_" + n]) for n in names}
    xi, yi, ci = _position()
    me_arr = jnp.reshape(4 * xi + 2 * yi + ci, (1,)).astype(jnp.int32)
    chip_arr = jnp.reshape(2 * xi + yi, (1,)).astype(jnp.int32)
    c_arr = jnp.reshape(ci, (1,)).astype(jnp.int32)
    pos_arr = jnp.stack([ci, 2 * xi + yi]).astype(jnp.int32)

    big = ("w_in", "w_out", "w_up", "w_down")
    lands = _cast_place([w[n] for n in big], chip_arr)
    start_a, wait_a = _gather_plan([w[n].shape[0] for n in big[:2]])
    start_b, wait_b = _gather_plan([w[n].shape[0] for n in big[2:]])

    row0 = jnp.concatenate([c, w["conv_w"].reshape(1, -1), w["ffn_conv_w"].reshape(1, -1)], axis=1)
    g0 = _allgather8(row0, "gather_cond", False)[:, 0, :]
    c8 = g0[:, :D_MODEL]
    per_chip = g0[0::2]
    conv_w_full = per_chip[:, D_MODEL:D_MODEL + 512].reshape(N_CHIPS, 4, 128).transpose(1, 0, 2).reshape(4, 512)
    ffn_conv_w_full = per_chip[:, D_MODEL + 512:].reshape(N_CHIPS, 3, 1536).transpose(1, 0, 2).reshape(3, 2 * D_FF)
    mod_parts = _allgather8(_ada_fwd(c8, w["w_ada"]), "gather_mod", False)
    send_a, recv_a, _, lands_a, token_a = _split_start([], lands[:2], start_a, 6, mod_parts, "gather_start_a")
    send_b, recv_b, _, lands_b, token_b = _split_start([], lands[2:], start_b, 6, token_a, "gather_start_b")
    mod = _mod_select(mod_parts, w["b_ada"].reshape(1, -1), me_arr, token_b).reshape(N_MOD, D_MODEL)
    sh_m, sc_m, gt_m, sh_f, sc_f, gt_f = [mod[k:k + 1] for k in range(N_MOD)]

    small = {n: w[n] for n in SMALL_REPLICATED}
    small["conv_w"] = conv_w_full
    small["ffn_conv_w"] = ffn_conv_w_full
    row = lambda a: a.reshape(1, -1)
    seq_params, ws_t = _seq_params(small)
    glo, ggo = row(small["g_lru_out"]), row(small["g_gmlp_out"])
    g_pre, g_post = row(small["g_mix_pre"]), row(small["g_mix_post"])
    g_pre2, g_post2 = row(small["g_ffn_pre"]), row(small["g_ffn_post"])
    fw, fb = small["ffn_conv_w"], row(small["ffn_conv_b"])
    xs, tgt = x[0], loss_target[0]

    _, lands_a = _split_wait(send_a, recv_a, [], lands_a, wait_a, mod, "gather_wait_a")
    w_in4, w_out4 = _forward_to_sibling(lands_a, "forward_a")
    w_out_b = w_out4.reshape(D_MODEL, D_MODEL)
    z, h = _mix_in(xs, sc_m, sh_m, g_pre, w_in4)
    ycat, hst, stash = _seqmix(z, seq_params, glo, ggo)
    y, x1, h2 = _mix_out(ycat, xs, w_out_b, gt_m, g_post, g_pre2, sc_f, sh_f)
    _, lands_b = _split_wait(send_b, recv_b, [], lands_b, wait_b, h2, "gather_wait_b")
    w_up4, w_down4 = _forward_to_sibling(lands_b, "forward_b")
    w_down_b = w_down4.reshape(D_FF, D_MODEL)
    up0, pre, act, dy2, dx2, loss, dgt_f, dg_post2 = _ffn_fwd(h2, x1, tgt, w_up4, w_down_b, fw, fb, gt_f, g_post2)

    dup0, dfw, dfb = _ffn_bwd_a(dy2, pre, up0, w_down_b, fw)
    gw_up = _wgrad(h2, dup0, N_CHIPS, "wgrad_up", True)
    gw_down = _wgrad(act, dy2, 2, "wgrad_down", False)
    ex_start, ex_wait = _exchange_plan(2)
    sg_start, sg_wait = _swap_gathered_plan(2)
    grads, deltas, new_m, new_v = {}, {}, {}, {}

    def swap_start(parts, name):
        sw_start, sw_wait = _swap_halves_plan([p.shape[1] // 2 for p in parts])
        recv = [lax.empty((N_CHIPS, p.shape[1] // 2, p.shape[2]), BF16) for p in parts]
        send_s, recv_s, parts, recv, token = _split_start(parts, recv, sw_start, len(parts), pos_arr,
                                                           "swap_start_" + name)
        return (send_s, recv_s, parts, recv, sw_wait), token

    def exchange_start(swap, tags, after, name):
        send_s, recv_s, parts, recv, sw_wait = swap
        parts, recv = _split_wait(send_s, recv_s, parts, recv, sw_wait, after, "swap_wait_" + name)
        both = [_chip_sum(p, r, pos_arr, "chip_sum_" + t) for p, r, t in zip(parts, recv, tags)]
        sums, gath = [b[0] for b in both], [b[1] for b in both]
        return _split_start(sums, gath, ex_start, 3 * len(parts), pos_arr, "exchange_start_" + name)

    def gathered_start(exchange, after, name):
        send_s, recv_s, sums, gath, _ = exchange
        _, gath = _split_wait(send_s, recv_s, sums, gath, ex_wait, after, "exchange_wait_" + name)
        send_s, recv_s, _, gath, token = _split_start([], gath, sg_start, len(gath), pos_arr,
                                                      "gathered_start_" + name)
        return (send_s, recv_s, gath), token

    def finish(gathered, tags, after, name):
        send_s, recv_s, gath = gathered
        _, gath = _split_wait(send_s, recv_s, [], gath, sg_wait, after, "gathered_wait_" + name)
        for g, t in zip(gath, tags):
            grads[t], deltas[t], new_m[t], new_v[t] = _adam_gathered(w[t], g, m[t], v[t], c_arr, "adam_" + t)

    def behind(value, token):
        return value + token[0:1, 0:1]

    tags_b, tags_a = ("w_up", "w_down"), ("w_in", "w_out")
    swap_b, tok = swap_start([gw_up, gw_down.reshape(N_CHIPS, -1, D_MODEL)], "b")
    dx1, dy, dsh_f, dsc_f, dg_pre2, dgt_m, dg_post = _ffn_bwd_b(
        dup0, x1, y, dx2, w_up4, g_pre2, behind(sc_f, tok), sh_f, gt_m, g_post)
    exchange_b = exchange_start(swap_b, tags_b, dg_post, "b")
    (dz, dcw, dcb, dwr, dwi, dbr, dbi, dspa, dng, dnb, dws, dbs_t, dglo, dggo) = _seqmix_bwd(
        z, hst, stash, dy, w_out_b, seq_params, ws_t, behind(glo, exchange_b[4]), ggo)
    grad_x, dsh_m, dsc_m, dg_pre = _mix_in_bwd(xs, dz, dx1, w_in4, g_pre, sc_m)
    gw_in = _wgrad(h, dz, N_CHIPS, "wgrad_in", True)
    gw_out = _wgrad(ycat, dy, 1, "wgrad_out", False)
    swap_a, tok = swap_start([gw_in, gw_out.reshape(N_CHIPS, -1, D_MODEL)], "a")

    dmod = jnp.concatenate([behind(dsh_m, tok), dsc_m, dgt_m, dsh_f, dsc_f, dgt_f], axis=1)
    dmod8 = _allgather8(dmod, "gather_dmod", False)[:, 0, :]
    grads["w_ada"], deltas["w_ada"], new_m["w_ada"], new_v["w_ada"], g_b_ada = _ada_bwd(
        c8, dmod8, chip_arr, w["w_ada"], m["w_ada"], v["w_ada"])
    exchange_a = exchange_start(swap_a, tags_a, g_b_ada, "a")
    gathered_b, tok = gathered_start(exchange_b, exchange_a[4], "b")

    small_grads = dict(
        g_mix_pre=dg_pre, g_mix_post=dg_post, conv_w=dcw, conv_b=dcb,
        w_rgate=_diag_blocks(dwr), b_rgate=dbr.reshape(LRU_HEADS, -1),
        w_igate=_diag_blocks(dwi), b_igate=dbi.reshape(LRU_HEADS, -1), lru_a=dspa,
        v_norm_g=dng, v_norm_b=dnb, w_spatial=dws, b_spatial=dbs_t.T,
        g_lru_out=dglo, g_gmlp_out=dggo, g_ffn_pre=dg_pre2, g_ffn_post=dg_post2,
        ffn_conv_w=dfw, ffn_conv_b=dfb)
    packed_names = SMALL_REPLICATED + SMALL_COLUMN_SHARDED
    packed_shapes = [small_grads[n].shape for n in packed_names] + [loss.shape]
    g_small = _allreduce_small(_pack([small_grads[n] for n in packed_names] + [behind(loss, tok)]), "reduce_small")
    g_rep = dict(zip(packed_names + ("loss",), _unpack(g_small, packed_shapes)))
    total = g_rep["loss"][0, 0]

    rep = SMALL_REPLICATED
    fill = [jnp.zeros(s, F32) for s in packed_shapes[len(rep):]]
    d_p, m_p, v_p = _adam(_pack([w[n] for n in rep] + fill), g_small, _pack([m[n] for n in rep] + fill),
                          _pack([v[n] for n in rep] + fill), "adam_small")
    for n, dd, mm, vv in zip(rep, _unpack(d_p, packed_shapes), _unpack(m_p, packed_shapes),
                             _unpack(v_p, packed_shapes)):
        grads[n], deltas[n], new_m[n], new_v[n] = g_rep[n], dd, mm, vv
    finish(gathered_b, tags_b, d_p, "b")

    gathered_a, tok = gathered_start(exchange_a, deltas["w_down"], "a")
    for n in SMALL_COLUMN_SHARDED:
        grads[n], deltas[n], new_m[n], new_v[n] = _adam_cols(w[n], g_rep[n], m[n], behind(v[n], tok), chip_arr,
                                                             "adam_" + n)
    d_b, m_b, v_b = _adam(w["b_ada"], g_b_ada, m["b_ada"], behind(v["b_ada"], tok), "adam_b_ada")
    grads["b_ada"], deltas["b_ada"], new_m["b_ada"], new_v["b_ada"] = g_b_ada, d_b, m_b, v_b
    finish(gathered_a, tags_a, d_b, "a")

    outs = [total, grad_x[None]]
    for group in (grads, deltas, new_m, new_v):
        outs.extend(group[n].reshape(args[n].shape) for n in names)
    return tuple(outs)
```

```python
import functools
import math

import jax
import jax.numpy as jnp
from jax import lax
from jax.experimental import pallas as pl
from jax.experimental.pallas import tpu as pltpu

F32 = jnp.float32
BF16 = jnp.bfloat16
MESH = pl.DeviceIdType.MESH

D_MODEL = 1024
LRU_WIDTH = 512
LRU_HEADS = 8
GMLP_WIDTH = 512
GMLP_GROUPS = 4
GMLP_BLOCK = 128
CHUNK = 64
D_FF = 3072
N_MOD = 6
EPS = 1e-6
LRU_C = 8.0
N_CHIPS = 4
N_DEV = 8

ADAM_LR = 0.001
ADAM_B1 = 0.9
ADAM_B2 = 0.999
ADAM_EPS = 1e-08
ADAM_WD = 0.01
ADAM_STEP = 10

GELU_C0 = math.sqrt(2.0 / math.pi)
GELU_C1 = 0.044715

VMEM_LIMIT_BYTES = 56 * 1024 * 1024
SUBLANES = 8
LANES = 128
BF16_SUBLANES = 16
FFN_CHUNK = 768
SUB_ROWS = 256


def _gelu_gate(x):
    x2 = x * x
    z = x * ((2.0 * GELU_C0 * GELU_C1) * x2 + 2.0 * GELU_C0)
    return 1.0 / (1.0 + jnp.exp(-z)), x2


def _gelu(x):
    t = jnp.tanh(GELU_C0 * (x + GELU_C1 * x * x * x))
    return 0.5 * x * (1.0 + t)


def _gelu_and_grad(x):
    s, x2 = _gelu_gate(x)
    g = x * s
    dz = (6.0 * GELU_C0 * GELU_C1) * x2 + 2.0 * GELU_C0
    return g, s + g * (1.0 - s) * dz


def _sigmoid(x):
    return 1.0 / (1.0 + jnp.exp(-x))


def _log1p(u):
    w = 1.0 + u
    return jnp.where(w == 1.0, u, jnp.log(w) * (u / (w - 1.0)))


def _softplus(x):
    return jnp.maximum(x, 0.0) + _log1p(jnp.exp(-jnp.abs(x)))


def _neg_expm1(x):
    u = jnp.exp(x)
    um1 = u - 1.0
    tiny = um1 == 0.0
    small = um1 * (x / jnp.log(jnp.where(tiny, 2.0, jnp.maximum(u, 0.25))))
    return -jnp.where(tiny, x, jnp.where(x < -1.0, um1, small))


def _msq_rsqrt(v):
    return lax.rsqrt(jnp.mean(v * v, axis=-1, keepdims=True) + EPS)


def _rms_bwd(dyn, yn, r):
    return r * (dyn - yn * jnp.mean(dyn * yn, axis=-1, keepdims=True))


def _colsum(v):
    return jnp.sum(v, axis=0, keepdims=True)


def _shift_down(cur, prev8, k):
    rolled = pltpu.roll(cur, k, 0)
    head = pltpu.roll(prev8, k, 0)
    row8 = lax.broadcasted_iota(jnp.int32, (SUBLANES, cur.shape[1]), 0)
    first = jnp.where(row8 < k, head, rolled[0:SUBLANES])
    return jnp.concatenate([first, rolled[SUBLANES:]], axis=0)


def _shift_up(cur, next8, k):
    t = cur.shape[0]
    rolled = pltpu.roll(cur, t - k, 0)
    tail = pltpu.roll(next8, SUBLANES - k, 0)
    row8 = lax.broadcasted_iota(jnp.int32, (SUBLANES, cur.shape[1]), 0)
    last = jnp.where(row8 >= SUBLANES - k, tail, rolled[t - SUBLANES:])
    return jnp.concatenate([rolled[:t - SUBLANES], last], axis=0)


def _scan_fwd(a, b):
    t = a.shape[0]
    row = lax.broadcasted_iota(jnp.int32, a.shape, 0)
    d = 1
    while d < t:
        keep = row >= d
        a_s = jnp.where(keep, pltpu.roll(a, d, 0), 1.0)
        b_s = jnp.where(keep, pltpu.roll(b, d, 0), 0.0)
        b = a * b_s + b
        a = a * a_s
        d *= 2
    return a, b


def _scan_bwd(a, g):
    t = a.shape[0]
    row = lax.broadcasted_iota(jnp.int32, a.shape, 0)
    d = 1
    while d < t:
        keep = row < t - d
        a_s = jnp.where(keep, pltpu.roll(a, t - d, 0), 1.0)
        g_s = jnp.where(keep, pltpu.roll(g, t - d, 0), 0.0)
        g = a * g_s + g
        a = a * a_s
        d *= 2
    return a, g


def _dot(a, b):
    return jnp.dot(a, b, preferred_element_type=F32)


def _dot_nt(a, b):
    return lax.dot_general(a, b, (((1,), (1,)), ((), ())), preferred_element_type=F32)


def _dot_tn(a, b):
    return lax.dot_general(a, b, (((0,), (0,)), ((), ())), preferred_element_type=F32)


def _rows(ts, cols, rev_of=None):
    if rev_of is None:
        return pl.BlockSpec((ts, cols), lambda i: (i, 0))
    return pl.BlockSpec((ts, cols), lambda i: (rev_of - 1 - i, 0))


def _halo_prev(ts, cols, halo, rev_of=None, col_block=0):
    per = ts // halo
    if rev_of is None:
        return pl.BlockSpec((halo, cols), lambda i: (jnp.maximum(i * per - 1, 0), col_block))
    return pl.BlockSpec((halo, cols), lambda i: (jnp.maximum((rev_of - 1 - i) * per - 1, 0), col_block))


def _full(shape):
    nd = len(shape)
    return pl.BlockSpec(shape, lambda *_: (0,) * nd)


_RESIDENT = pl.BlockSpec(memory_space=pltpu.VMEM)


def _params(sem):
    return pltpu.CompilerParams(dimension_semantics=sem, vmem_limit_bytes=VMEM_LIMIT_BYTES)


def _sds(shape, dtype):
    return jax.ShapeDtypeStruct(shape, dtype)


def _sub_tiles(ts):
    return [slice(r0, r0 + SUB_ROWS) for r0 in range(0, ts, SUB_ROWS)]


def _mix_in(x, sc, sh, g, w_in4, ts=512):
    s, d = x.shape

    def body(x_ref, sc_ref, sh_ref, g_ref, w_ref, z_ref, h_ref):
        for rs in _sub_tiles(ts):
            xv = x_ref[rs, :]
            h = (xv * _msq_rsqrt(xv) * g_ref[...]) * (1.0 + sc_ref[...]) + sh_ref[...]
            hb = h.astype(BF16)
            h_ref[rs, :] = hb
            for k in range(N_CHIPS):
                z_ref[rs, k * 512:(k + 1) * 512] = _dot(hb, w_ref[k])

    return pl.pallas_call(
        body, grid=(s // ts,), name="mix_in",
        in_specs=[_rows(ts, d), _full((1, d)), _full((1, d)), _full((1, d)), _full(w_in4.shape)],
        out_specs=[_rows(ts, 2048), _rows(ts, d)],
        out_shape=[_sds((s, 2048), F32), _sds((s, d), BF16)],
        compiler_params=_params(("parallel",)),
    )(x, sc, sh, g, w_in4)


N_STASH = 12
(ST_XC, ST_R, ST_IG, ST_A, ST_MULT, ST_GL, ST_DGL, ST_U, ST_DU, ST_Q, ST_VHAT, ST_SPB) = range(N_STASH)


def _seq_param_specs():
    return [_full((4, 512)), _full((1, 512)), _full((512, 512)), _full((512, 512)), _full((1, 512)),
            _full((1, 512)), _full((1, 512)), _full((1, 512)), _full((1, 512)), _full((4, 128, 128)),
            _full((128, 4))]


def _seqmix(z, seq_params, glo, ggo, ts=256):
    s = z.shape[0]
    nt = s // ts

    def body(z_ref, zprev_ref, cw_ref, cb_ref, bdr_ref, bdi_ref, br_ref, bi_ref, la_ref, ng_ref, nb_ref,
             ws_ref, bst_ref, glo_ref, ggo_ref, ycat_ref, hst_ref, st_ref, hcarry, sp_scr):
        i = pl.program_id(0)

        @pl.when(i == 0)
        def _():
            hcarry[...] = jnp.zeros_like(hcarry)

        lx = z_ref[:, 0:512]
        prev8 = jnp.where(i == 0, 0.0, zprev_ref[...])
        xc = (cw_ref[3:4, :] * lx + cw_ref[2:3, :] * _shift_down(lx, prev8, 1)
              + cw_ref[1:2, :] * _shift_down(lx, prev8, 2) + cw_ref[0:1, :] * _shift_down(lx, prev8, 3)
              + cb_ref[...])
        xcb = xc.astype(BF16)
        r = _sigmoid(_dot(xcb, bdr_ref[...]) + br_ref[...])
        ig = _sigmoid(_dot(xcb, bdi_ref[...]) + bi_ref[...])
        log_a = (-LRU_C) * r * _softplus(-la_ref[...])
        a = jnp.exp(log_a)
        mult = jnp.sqrt(_neg_expm1(2.0 * log_a))
        acum, hloc = _scan_fwd(a, mult * (ig * xc))
        h = hloc + acum * hcarry[...]
        hcarry[...] = h[ts - 1:ts, :]
        hst_ref[...] = h
        gl, dgl = _gelu_and_grad(z_ref[:, 512:1024])
        y_l = h * gl
        for slot, val in ((ST_XC, xc), (ST_R, r), (ST_IG, ig), (ST_A, a), (ST_MULT, mult), (ST_GL, gl),
                          (ST_DGL, dgl)):
            st_ref[slot] = val

        u, du = _gelu_and_grad(z_ref[:, 1024:1536])
        vg, dvg = _gelu_and_grad(z_ref[:, 1536:2048])
        vc = vg - jnp.mean(vg, axis=-1, keepdims=True)
        rstd = lax.rsqrt(jnp.mean(vc * vc, axis=-1, keepdims=True) + EPS)
        vhat = vc * rstd
        vb = (vhat * ng_ref[...] + nb_ref[...]).astype(BF16)
        for n in range(ts // GMLP_BLOCK):
            rs = slice(n * GMLP_BLOCK, (n + 1) * GMLP_BLOCK)
            for g in range(GMLP_GROUPS):
                cs = slice(g * 128, (g + 1) * 128)
                sp_scr[rs, cs] = _dot(ws_ref[g], vb[rs, cs]) + bst_ref[:, g:g + 1]
        spb = sp_scr[...]
        y_g = u * spb
        for slot, val in ((ST_U, u), (ST_DU, du), (ST_Q, rstd * dvg), (ST_VHAT, vhat), (ST_SPB, spb)):
            st_ref[slot] = val

        ycat_ref[:, 0:512] = (y_l * _msq_rsqrt(y_l) * glo_ref[...]).astype(BF16)
        ycat_ref[:, 512:1024] = (y_g * _msq_rsqrt(y_g) * ggo_ref[...]).astype(BF16)

    return pl.pallas_call(
        body, grid=(nt,), name="seqmix",
        in_specs=[_rows(ts, 2048), _halo_prev(ts, 512, SUBLANES)] + _seq_param_specs()
        + [_full((1, 512)), _full((1, 512))],
        out_specs=[_rows(ts, 1024), _rows(ts, 512), pl.BlockSpec((N_STASH, ts, 512), lambda i: (0, i, 0))],
        out_shape=[_sds((s, 1024), BF16), _sds((s, 512), F32), _sds((N_STASH, s, 512), F32)],
        scratch_shapes=[pltpu.VMEM((1, 512), F32), pltpu.VMEM((ts, 512), F32)],
        compiler_params=_params(("arbitrary",)),
    )(z, z, *seq_params, glo, ggo)


def _mix_out(ycat, x, w_out, gt_m, g_post, g_pre2, sc_f, sh_f, ts=512):
    s, d = x.shape

    def body(yc_ref, x_ref, w_ref, gt_ref, gp_ref, g2_ref, sc_ref, sh_ref, y_ref, x1_ref, h2_ref):
        for rs in _sub_tiles(ts):
            y = _dot(yc_ref[rs, :], w_ref[...])
            y_ref[rs, :] = y
            x1 = x_ref[rs, :] + gt_ref[...] * (y * _msq_rsqrt(y) * gp_ref[...])
            x1_ref[rs, :] = x1
            h2 = (x1 * _msq_rsqrt(x1) * g2_ref[...]) * (1.0 + sc_ref[...]) + sh_ref[...]
            h2_ref[rs, :] = h2.astype(BF16)

    vec = _full((1, d))
    return pl.pallas_call(
        body, grid=(s // ts,), name="mix_out",
        in_specs=[_rows(ts, d), _rows(ts, d), _full((d, d)), vec, vec, vec, vec, vec],
        out_specs=[_rows(ts, d), _rows(ts, d), _rows(ts, d)],
        out_shape=[_sds((s, d), F32), _sds((s, d), F32), _sds((s, d), BF16)],
        compiler_params=_params(("parallel",)),
    )(ycat, x, w_out, gt_m, g_post, g_pre2, sc_f, sh_f)


def _ffn_cols(j):
    per = (2 * D_FF // N_CHIPS) // FFN_CHUNK
    return j // per, (j % per) * FFN_CHUNK, j * FFN_CHUNK


def _ffn_fwd(h2, x1, tgt, w_up4, w_down, fw, fb, gt_f, g_post, ts=256):
    s, d = x1.shape
    nch = D_FF // FFN_CHUNK

    def body(h2_ref, x1_ref, tgt_ref, wup_ref, wdn_ref, fw_ref, fb_ref, gt_ref, gp_ref,
             up0_ref, pre_ref, act_ref, dy2_ref, dx2_ref, loss_ref, dgt_ref, dgp_ref, tail_ref):
        i = pl.program_id(0)

        @pl.when(i == 0)
        def _():
            tail_ref[...] = jnp.zeros_like(tail_ref)
            loss_ref[...] = jnp.zeros_like(loss_ref)
            dgt_ref[...] = jnp.zeros_like(dgt_ref)
            dgp_ref[...] = jnp.zeros_like(dgp_ref)

        hb = h2_ref[...]

        def up_project(j):
            sh_g, off, _ = _ffn_cols(j)
            return [_dot(hb, wup_ref[shard, :, off:off + FFN_CHUNK]).astype(BF16) for shard in (sh_g, sh_g + 2)]

        y2 = jnp.zeros((ts, d), F32)
        ahead = up_project(0)
        for j in range(nch):
            _, _, col = _ffn_cols(j)
            ubs = ahead
            if j + 1 < nch:
                ahead = up_project(j + 1)
            halves = []
            for ub, c0 in zip(ubs, (col, D_FF + col)):
                cs = slice(c0, c0 + FFN_CHUNK)
                up0_ref[:, cs] = ub
                u = ub.astype(F32)
                prev8 = tail_ref[:, cs]
                tail_ref[:, cs] = u[ts - SUBLANES:, :]
                halves.append(fw_ref[2:3, cs] * u + fw_ref[1:2, cs] * _shift_down(u, prev8, 1)
                              + fw_ref[0:1, cs] * _shift_down(u, prev8, 2) + fb_ref[:, cs])
                pre_ref[:, cs] = halves[-1].astype(BF16)
            act = (_gelu(halves[0]) * halves[1]).astype(BF16)
            act_ref[:, col:col + FFN_CHUNK] = act
            y2 = y2 + _dot(act, wdn_ref[col:col + FFN_CHUNK, :])
        r2 = _msq_rsqrt(y2)
        yn = y2 * r2
        yng = yn * gp_ref[...]
        e = x1_ref[...] + gt_ref[...] * yng - tgt_ref[...]
        loss_ref[...] += jnp.sum(e * e) * (0.5 / d)
        dx2 = e * (1.0 / d)
        dx2_ref[...] = dx2
        dgt_ref[...] += _colsum(dx2 * yng)
        dyng = dx2 * gt_ref[...]
        dgp_ref[...] += _colsum(dyng * yn)
        dy2_ref[...] = _rms_bwd(dyng * gp_ref[...], yn, r2).astype(BF16)

    vec = _full((1, d))
    return pl.pallas_call(
        body, grid=(s // ts,), name="ffn_fwd",
        in_specs=[_rows(ts, d), _rows(ts, d), _rows(ts, d), _RESIDENT, _RESIDENT,
                  _full((3, 2 * D_FF)), _full((1, 2 * D_FF)), vec, vec],
        out_specs=[_rows(ts, 2 * D_FF), _rows(ts, 2 * D_FF), _rows(ts, D_FF), _rows(ts, d), _rows(ts, d),
                   _full((1, 128)), vec, vec],
        out_shape=[_sds((s, 2 * D_FF), BF16), _sds((s, 2 * D_FF), BF16), _sds((s, D_FF), BF16), _sds((s, d), BF16),
                   _sds((s, d), F32), _sds((1, 128), F32), _sds((1, d), F32), _sds((1, d), F32)],
        scratch_shapes=[pltpu.VMEM((SUBLANES, 2 * D_FF), F32)],
        compiler_params=_params(("arbitrary",)),
    )(h2, x1, tgt, w_up4, w_down, fw, fb, gt_f, g_post)


def _shift_up_mxu(vb, up_mat, next8, k):
    t = vb.shape[0]
    main = _dot(up_mat, vb)
    tail = pltpu.roll(next8, SUBLANES - k, 0)
    row8 = lax.broadcasted_iota(jnp.int32, next8.shape, 0)
    last = main[t - SUBLANES:] + jnp.where(row8 >= SUBLANES - k, tail, 0.0)
    return jnp.concatenate([main[:t - SUBLANES], last], axis=0)


def _ffn_bwd_a(dy2, pre, up0, w_down, fw, ts=256):
    s, d = dy2.shape
    nt = s // ts
    nch = D_FF // FFN_CHUNK
    wide = 2 * D_FF
    up_mats = jnp.stack([jnp.eye(ts, k=1, dtype=BF16), jnp.eye(ts, k=2, dtype=BF16)])

    def body(dy2_ref, pre_ref, up0_ref, wdn_ref, fw_ref, um_ref, dup0_ref, dfw_ref, dfb_ref, next_ref):
        i = pl.program_id(0)

        @pl.when(i == 0)
        def _():
            next_ref[...] = jnp.zeros_like(next_ref)
            dfw_ref[...] = jnp.zeros_like(dfw_ref)
            dfb_ref[...] = jnp.zeros_like(dfb_ref)

        dyb = dy2_ref[...]
        for j in range(nch):
            _, _, col = _ffn_cols(j)
            dact = _dot_nt(dyb, wdn_ref[col:col + FFN_CHUNK, :])
            gl, dgl = _gelu_and_grad(pre_ref[:, col:col + FFN_CHUNK].astype(F32))
            dpre = (dact * pre_ref[:, D_FF + col:D_FF + col + FFN_CHUNK].astype(F32) * dgl, dact * gl)
            for half, c0 in enumerate((col, D_FF + col)):
                cs = slice(c0, c0 + FFN_CHUNK)
                dp = dpre[half]
                dpb = dp.astype(BF16)
                nxt = next_ref[:, cs]
                next_ref[:, cs] = dpb.astype(F32)[0:SUBLANES, :]
                su1 = _shift_up_mxu(dpb, um_ref[0], nxt, 1)
                su2 = _shift_up_mxu(dpb, um_ref[1], nxt, 2)
                u = up0_ref[:, cs].astype(F32)
                dfb_ref[:, cs] += _colsum(dp)
                dfw_ref[2:3, cs] += _colsum(dp * u)
                dfw_ref[1:2, cs] += _colsum(su1 * u)
                dfw_ref[0:1, cs] += _colsum(su2 * u)
                dup0 = fw_ref[2:3, cs] * dp + fw_ref[1:2, cs] * su1 + fw_ref[0:1, cs] * su2
                dup0_ref[:, cs] = dup0.astype(BF16)

    return pl.pallas_call(
        body, grid=(nt,), name="ffn_bwd_a",
        in_specs=[_rows(ts, d, nt), _rows(ts, wide, nt), _rows(ts, wide, nt), _RESIDENT,
                  _full((3, wide)), _full((2, ts, ts))],
        out_specs=[_rows(ts, wide, nt), _full((3, wide)), _full((1, wide))],
        out_shape=[_sds((s, wide), BF16), _sds((3, wide), F32), _sds((1, wide), F32)],
        scratch_shapes=[pltpu.VMEM((SUBLANES, wide), F32)],
        compiler_params=_params(("arbitrary",)),
    )(dy2, pre, up0, w_down, fw, up_mats)


def _ffn_bwd_b(dup0, x1, y, dx2, w_up4, g_pre2, sc_f, sh_f, gt_m, g_post_m, ts=512):
    s, d = x1.shape
    shard_cols = 2 * D_FF // N_CHIPS

    def body(dup_ref, x1_ref, y_ref, dx2_ref, wup_ref, g2_ref, sc_ref, sh_ref, gt_ref, gp_ref,
             dx1_ref, dy_ref, dsh_ref, dsc_ref, dg2_ref, dgt_ref, dgp_ref):
        i = pl.program_id(0)

        @pl.when(i == 0)
        def _():
            for ref in (dsh_ref, dsc_ref, dg2_ref, dgt_ref, dgp_ref):
                ref[...] = jnp.zeros_like(ref)

        for rs in _sub_tiles(ts):
            dh2 = jnp.zeros((SUB_ROWS, d), F32)
            for k in range(N_CHIPS):
                dh2 = dh2 + _dot_nt(dup_ref[rs, k * shard_cols:(k + 1) * shard_cols], wup_ref[k])
            x1v = x1_ref[rs, :]
            r2 = _msq_rsqrt(x1v)
            xn = x1v * r2
            hn = xn * g2_ref[...]
            dsh_ref[...] += _colsum(dh2)
            dsc_ref[...] += _colsum(dh2 * hn)
            dhn = dh2 * (1.0 + sc_ref[...])
            dg2_ref[...] += _colsum(dhn * xn)
            dx1 = dx2_ref[rs, :] + _rms_bwd(dhn * g2_ref[...], xn, r2)
            dx1_ref[rs, :] = dx1
            yv = y_ref[rs, :]
            ry = _msq_rsqrt(yv)
            yn = yv * ry
            dgt_ref[...] += _colsum(dx1 * (yn * gp_ref[...]))
            dyng = dx1 * gt_ref[...]
            dgp_ref[...] += _colsum(dyng * yn)
            dy_ref[rs, :] = _rms_bwd(dyng * gp_ref[...], yn, ry).astype(BF16)

    vec = _full((1, d))
    return pl.pallas_call(
        body, grid=(s // ts,), name="ffn_bwd_b",
        in_specs=[_rows(ts, 2 * D_FF), _rows(ts, d), _rows(ts, d), _rows(ts, d), _RESIDENT,
                  vec, vec, vec, vec, vec],
        out_specs=[_rows(ts, d), _rows(ts, d), vec, vec, vec, vec, vec],
        out_shape=[_sds((s, d), F32), _sds((s, d), BF16)] + [_sds((1, d), F32)] * 5,
        compiler_params=_params(("arbitrary",)),
    )(dup0, x1, y, dx2, w_up4, g_pre2, sc_f, sh_f, gt_m, g_post_m)


def _seqmix_bwd(z, hst, stash, dy, w_out, seq_params, ws_t, glo, ggo, ts=256):
    s = z.shape[0]
    nt = s // ts
    small_shapes = [(4, 512), (1, 512), (512, 512), (512, 512), (1, 512), (1, 512), (1, 512),
                    (1, 512), (1, 512), (4, 128, 128), (128, 4), (1, 512), (1, 512)]

    def body(lx_ref, hst_ref, hprev_ref, st_ref, dy_ref, wout_ref, cw_ref, cb_ref, bdr_ref, bdi_ref, br_ref,
             bi_ref, la_ref, ng_ref, nb_ref, ws_ref, bst_ref, wst_ref, glo_ref, ggo_ref, dz_ref, *rest):
        small_refs = rest[:13]
        (dcw_ref, dcb_ref, dwr_ref, dwi_ref, dbr_ref, dbi_ref, dspa_ref, dng_ref, dnb_ref, dws_ref, dbs_ref,
         dglo_ref, dggo_ref) = small_refs
        gcarry, anext, dxcnext, dv_scr = rest[13:]
        i = pl.program_id(0)

        @pl.when(i == 0)
        def _():
            for ref in small_refs:
                ref[...] = jnp.zeros_like(ref)
            gcarry[...] = jnp.zeros_like(gcarry)
            anext[...] = jnp.ones_like(anext)
            dxcnext[...] = jnp.zeros_like(dxcnext)

        first_tile = i == nt - 1
        xc, r, ig, a, mult = st_ref[ST_XC], st_ref[ST_R], st_ref[ST_IG], st_ref[ST_A], st_ref[ST_MULT]
        gl, u, spb, vhat = st_ref[ST_GL], st_ref[ST_U], st_ref[ST_SPB], st_ref[ST_VHAT]
        lx = lx_ref[...]
        h = hst_ref[...]
        hprev = _shift_down(h, jnp.where(first_tile, 0.0, hprev_ref[...]), 1)
        y_l = h * gl
        y_g = u * spb

        dycat = _dot_nt(dy_ref[...], wout_ref[...])
        rl = _msq_rsqrt(y_l)
        yln = y_l * rl
        dyl = dycat[:, 0:512]
        dglo_ref[...] += _colsum(dyl * yln)
        dy_l = _rms_bwd(dyl * glo_ref[...], yln, rl)
        rg = _msq_rsqrt(y_g)
        ygn = y_g * rg
        dyg = dycat[:, 512:1024]
        dggo_ref[...] += _colsum(dyg * ygn)
        dy_g = _rms_bwd(dyg * ggo_ref[...], ygn, rg)

        dz_ref[:, 512:1024] = (dy_l * h * st_ref[ST_DGL]).astype(BF16)
        a_up = _shift_up(a, anext[...], 1)
        acum, gloc = _scan_bwd(a_up, dy_l * gl)
        gg = gloc + acum * gcarry[...]
        gcarry[...] = gg[0:1, :]
        anext[...] = a[0:SUBLANES, :]
        da = gg * hprev
        t1 = gg * mult
        di = t1 * xc
        dxc = t1 * ig
        dmult = gg * ig * xc
        dla = da * a - dmult * (a * a / mult)
        dspa_ref[...] += _colsum(dla * r) * (-LRU_C)
        dpr = dla * ((-LRU_C) * _softplus(-la_ref[...])) * r * (1.0 - r)
        dpi = di * ig * (1.0 - ig)
        dbr_ref[...] += _colsum(dpr)
        dbi_ref[...] += _colsum(dpi)
        dprb = dpr.astype(BF16)
        dpib = dpi.astype(BF16)
        xcb = xc.astype(BF16)
        dwr_ref[...] += _dot_tn(xcb, dprb)
        dwi_ref[...] += _dot_tn(xcb, dpib)
        dxc = dxc + _dot_nt(dprb, bdr_ref[...]) + _dot_nt(dpib, bdi_ref[...])
        nxt = dxcnext[...]
        dxcnext[...] = dxc[0:SUBLANES, :]
        up1, up2, up3 = _shift_up(dxc, nxt, 1), _shift_up(dxc, nxt, 2), _shift_up(dxc, nxt, 3)
        dcb_ref[...] += _colsum(dxc)
        dcw_ref[3:4, :] += _colsum(dxc * lx)
        dcw_ref[2:3, :] += _colsum(up1 * lx)
        dcw_ref[1:2, :] += _colsum(up2 * lx)
        dcw_ref[0:1, :] += _colsum(up3 * lx)
        dlx = cw_ref[3:4, :] * dxc + cw_ref[2:3, :] * up1 + cw_ref[1:2, :] * up2 + cw_ref[0:1, :] * up3
        dz_ref[:, 0:512] = dlx.astype(BF16)

        dz_ref[:, 1024:1536] = (dy_g * spb * st_ref[ST_DU]).astype(BF16)
        dsp = dy_g * u
        vb = (vhat * ng_ref[...] + nb_ref[...]).astype(BF16)
        for n in range(ts // GMLP_BLOCK):
            rs = slice(n * GMLP_BLOCK, (n + 1) * GMLP_BLOCK)
            for g in range(GMLP_GROUPS):
                cs = slice(g * 128, (g + 1) * 128)
                dbs_ref[:, g:g + 1] += jnp.sum(dsp[rs, cs], axis=1, keepdims=True)
                blk = dsp[rs, cs].astype(BF16)
                dws_ref[g] += _dot_nt(blk, vb[rs, cs])
                dv_scr[rs, cs] = _dot(wst_ref[g], blk)
        dv = dv_scr[...]
        dng_ref[...] += _colsum(dv * vhat)
        dnb_ref[...] += _colsum(dv)
        dvh = dv * ng_ref[...]
        dvg = dvh - jnp.mean(dvh, axis=-1, keepdims=True) - vhat * jnp.mean(dvh * vhat, axis=-1, keepdims=True)
        dz_ref[:, 1536:2048] = (dvg * st_ref[ST_Q]).astype(BF16)

        @pl.when(i == nt - 1)
        def _():
            pos = lax.broadcasted_iota(jnp.int32, (GMLP_BLOCK, GMLP_BLOCK), 0) // CHUNK
            src = lax.broadcasted_iota(jnp.int32, (GMLP_BLOCK, GMLP_BLOCK), 1) // CHUNK
            for g in range(GMLP_GROUPS):
                dws_ref[g] = jnp.where(src <= pos, dws_ref[g], 0.0)
            dspa_ref[...] = dspa_ref[...] * (-_sigmoid(-la_ref[...]))

    in_specs = ([_rows(ts, 512, nt), _rows(ts, 512, nt), _halo_prev(ts, 512, SUBLANES, nt),
                 pl.BlockSpec((N_STASH, ts, 512), lambda i: (0, nt - 1 - i, 0)), _rows(ts, 1024, nt),
                 _full((1024, 1024))]
                + _seq_param_specs() + [_full((4, 128, 128)), _full((1, 512)), _full((1, 512))])
    return pl.pallas_call(
        body, grid=(nt,), name="seqmix_bwd",
        in_specs=in_specs,
        out_specs=[_rows(ts, 2048, nt)] + [_full(sh) for sh in small_shapes],
        out_shape=[_sds((s, 2048), BF16)] + [_sds(sh, F32) for sh in small_shapes],
        scratch_shapes=[pltpu.VMEM((1, 512), F32), pltpu.VMEM((SUBLANES, 512), F32),
                        pltpu.VMEM((SUBLANES, 512), F32), pltpu.VMEM((ts, 512), F32)],
        compiler_params=_params(("arbitrary",)),
    )(z, hst, hst, stash, dy, w_out, *seq_params, ws_t, glo, ggo)


def _seqmix_bwd_recomputing_unused(z, hst, dy, w_out, seq_params, ws_t, glo, ggo, ts=256):
    s = z.shape[0]
    nt = s // ts
    small_shapes = [(4, 512), (1, 512), (512, 512), (512, 512), (1, 512), (1, 512), (1, 512),
                    (1, 512), (1, 512), (4, 128, 128), (128, 4), (1, 512), (1, 512)]

    def body(z_ref, zprev_ref, hst_ref, hprev_ref, dy_ref, wout_ref, *rest):
        p = rest[:11]
        wst_ref, glo_ref, ggo_ref = rest[11:14]
        dz_ref = rest[14]
        (dcw_ref, dcb_ref, dwr_ref, dwi_ref, dbr_ref, dbi_ref, dspa_ref, dng_ref, dnb_ref, dws_ref, dbs_ref,
         dglo_ref, dggo_ref) = rest[15:28]
        gcarry, anext, dxcnext, sp_scr, dv_scr = rest[28:]
        i = pl.program_id(0)

        @pl.when(i == 0)
        def _():
            for ref in rest[15:28]:
                ref[...] = jnp.zeros_like(ref)
            gcarry[...] = jnp.zeros_like(gcarry)
            anext[...] = jnp.ones_like(anext)
            dxcnext[...] = jnp.zeros_like(dxcnext)

        first_tile = i == nt - 1
        f = _seq_recompute(z_ref, zprev_ref, first_tile, p)
        xc, r, ig, a, mult, lx = f["xc"], f["r"], f["ig"], f["a"], f["mult"], f["lx"]
        h = hst_ref[...]
        hprev = _shift_down(h, jnp.where(first_tile, 0.0, hprev_ref[...]), 1)
        gl, dgl = _gelu_and_grad(f["lg"])
        y_l = h * gl
        gm = _gmlp_fwd(f["gu"], f["gv"], p[7], p[8], p[9], p[10], sp_scr)
        y_g = gm["y_g"]

        dycat = _dot_nt(dy_ref[...], wout_ref[...])
        rl = _msq_rsqrt(y_l)
        yln = y_l * rl
        dyl = dycat[:, 0:512]
        dglo_ref[...] += _colsum(dyl * yln)
        dy_l = _rms_bwd(dyl * glo_ref[...], yln, rl)
        rg = _msq_rsqrt(y_g)
        ygn = y_g * rg
        dyg = dycat[:, 512:1024]
        dggo_ref[...] += _colsum(dyg * ygn)
        dy_g = _rms_bwd(dyg * ggo_ref[...], ygn, rg)

        dz_ref[:, 512:1024] = (dy_l * h * dgl).astype(BF16)
        a_up = _shift_up(a, anext[...], 1)
        acum, gloc = _scan_bwd(a_up, dy_l * gl)
        gg = gloc + acum * gcarry[...]
        gcarry[...] = gg[0:1, :]
        anext[...] = a[0:SUBLANES, :]
        da = gg * hprev
        t1 = gg * mult
        di = t1 * xc
        dxc = t1 * ig
        dmult = gg * ig * xc
        dla = da * a - dmult * (a * a / mult)
        spa = f["spa"]
        dspa_ref[...] += _colsum(dla * r) * (-LRU_C)
        dpr = dla * ((-LRU_C) * spa) * r * (1.0 - r)
        dpi = di * ig * (1.0 - ig)
        dbr_ref[...] += _colsum(dpr)
        dbi_ref[...] += _colsum(dpi)
        dprb = dpr.astype(BF16)
        dpib = dpi.astype(BF16)
        dwr_ref[...] += _dot_tn(f["xcb"], dprb)
        dwi_ref[...] += _dot_tn(f["xcb"], dpib)
        dxc = dxc + _dot_nt(dprb, p[2][...]) + _dot_nt(dpib, p[3][...])
        dcb_ref[...] += _colsum(dxc)
        dcw_ref[3:4, :] += _colsum(dxc * lx)
        dcw_ref[2:3, :] += _colsum(dxc * f["s1"])
        dcw_ref[1:2, :] += _colsum(dxc * f["s2"])
        dcw_ref[0:1, :] += _colsum(dxc * f["s3"])
        nxt = dxcnext[...]
        dxcnext[...] = dxc[0:SUBLANES, :]
        cw_ref = p[0]
        dlx = (cw_ref[3:4, :] * dxc + cw_ref[2:3, :] * _shift_up(dxc, nxt, 1)
               + cw_ref[1:2, :] * _shift_up(dxc, nxt, 2) + cw_ref[0:1, :] * _shift_up(dxc, nxt, 3))
        dz_ref[:, 0:512] = dlx.astype(BF16)

        dz_ref[:, 1024:1536] = (dy_g * gm["spb"] * gm["du"]).astype(BF16)
        dsp = dy_g * gm["u"]
        vb = gm["vb"]
        for n in range(ts // GMLP_BLOCK):
            rs = slice(n * GMLP_BLOCK, (n + 1) * GMLP_BLOCK)
            for g in range(GMLP_GROUPS):
                cs = slice(g * 128, (g + 1) * 128)
                dbs_ref[:, g:g + 1] += jnp.sum(dsp[rs, cs], axis=1, keepdims=True)
                blk = dsp[rs, cs].astype(BF16)
                dws_ref[g] += _dot_nt(blk, vb[rs, cs])
                dv_scr[rs, cs] = _dot(wst_ref[g], blk)
        dv = dv_scr[...]
        vhat = gm["vhat"]
        dng_ref[...] += _colsum(dv * vhat)
        dnb_ref[...] += _colsum(dv)
        dvh = dv * p[7][...]
        dvg = gm["rstd"] * (dvh - jnp.mean(dvh, axis=-1, keepdims=True)
                            - vhat * jnp.mean(dvh * vhat, axis=-1, keepdims=True))
        dz_ref[:, 1536:2048] = (dvg * gm["dvg"]).astype(BF16)

        @pl.when(i == nt - 1)
        def _():
            pos = lax.broadcasted_iota(jnp.int32, (GMLP_BLOCK, GMLP_BLOCK), 0) // CHUNK
            src = lax.broadcasted_iota(jnp.int32, (GMLP_BLOCK, GMLP_BLOCK), 1) // CHUNK
            for g in range(GMLP_GROUPS):
                dws_ref[g] = jnp.where(src <= pos, dws_ref[g], 0.0)
            dspa_ref[...] = dspa_ref[...] * (-_sigmoid(-p[6][...]))

    in_specs = (_seq_specs(ts, nt, True)
                + [_rows(ts, 512, nt), _halo_prev(ts, 512, SUBLANES, nt), _rows(ts, 1024, nt), _full((1024, 1024))]
                + _seq_param_specs() + [_full((4, 128, 128)), _full((1, 512)), _full((1, 512))])
    return pl.pallas_call(
        body, grid=(nt,), name="seqmix_bwd",
        in_specs=in_specs,
        out_specs=[_rows(ts, 2048, nt)] + [_full(sh) for sh in small_shapes],
        out_shape=[_sds((s, 2048), BF16)] + [_sds(sh, F32) for sh in small_shapes],
        scratch_shapes=[pltpu.VMEM((1, 512), F32), pltpu.VMEM((SUBLANES, 512), F32),
                        pltpu.VMEM((SUBLANES, 512), F32), pltpu.VMEM((ts, 512), F32), pltpu.VMEM((ts, 512), F32)],
        compiler_params=_params(("arbitrary",)),
    )(z, z, hst, hst, dy, w_out, *seq_params, ws_t, glo, ggo)


def _mix_in_bwd(x, dz, dx1, w_in4, g, sc, ts=512):
    s, d = x.shape

    def body(x_ref, dz_ref, dx1_ref, w_ref, g_ref, sc_ref, gx_ref, dsh_ref, dsc_ref, dg_ref):
        i = pl.program_id(0)

        @pl.when(i == 0)
        def _():
            for ref in (dsh_ref, dsc_ref, dg_ref):
                ref[...] = jnp.zeros_like(ref)

        for rs in _sub_tiles(ts):
            dh = jnp.zeros((SUB_ROWS, d), F32)
            for k in range(N_CHIPS):
                dh = dh + _dot_nt(dz_ref[rs, k * 512:(k + 1) * 512], w_ref[k])
            xv = x_ref[rs, :]
            r = _msq_rsqrt(xv)
            xn = xv * r
            dsh_ref[...] += _colsum(dh)
            dsc_ref[...] += _colsum(dh * (xn * g_ref[...]))
            dhn = dh * (1.0 + sc_ref[...])
            dg_ref[...] += _colsum(dhn * xn)
            gx_ref[rs, :] = dx1_ref[rs, :] + _rms_bwd(dhn * g_ref[...], xn, r)

    vec = _full((1, d))
    return pl.pallas_call(
        body, grid=(s // ts,), name="mix_in_bwd",
        in_specs=[_rows(ts, d), _rows(ts, 2048), _rows(ts, d), _full(w_in4.shape), vec, vec],
        out_specs=[_rows(ts, d), vec, vec, vec],
        out_shape=[_sds((s, d), F32)] + [_sds((1, d), F32)] * 3,
        compiler_params=_params(("arbitrary",)),
    )(x, dz, dx1, w_in4, g, sc)


def _wgrad(a, b, n_chunks, name, chunk_major, ts=2048):
    s, m = a.shape
    n = b.shape[1]
    nc = n // n_chunks
    nt = s // ts

    def body(a_ref, b_ref, o_ref, acc):
        i = pl.program_id(1)

        @pl.when(i == 0)
        def _():
            acc[...] = jnp.zeros_like(acc)

        acc[...] += _dot_tn(a_ref[...], b_ref[...])

        @pl.when(i == nt - 1)
        def _():
            if chunk_major:
                o_ref[0] = acc[...].astype(BF16)
            else:
                o_ref[...] = acc[...].astype(BF16)

    if chunk_major:
        out_spec, out_shape = pl.BlockSpec((1, m, nc), lambda c, i: (c, 0, 0)), _sds((n_chunks, m, nc), BF16)
    else:
        out_spec, out_shape = pl.BlockSpec((m, nc), lambda c, i: (0, c)), _sds((m, n), BF16)
    return pl.pallas_call(
        body, grid=(n_chunks, nt), name=name,
        in_specs=[pl.BlockSpec((ts, m), lambda c, i: (i, 0)), pl.BlockSpec((ts, nc), lambda c, i: (i, c))],
        out_specs=out_spec,
        out_shape=out_shape,
        scratch_shapes=[pltpu.VMEM((m, nc), F32)],
        compiler_params=_params(("parallel", "arbitrary")),
    )(a, b)


def _block_diag(w):
    heads, hd, _ = w.shape
    eye = jnp.eye(heads, dtype=w.dtype)
    return (eye[:, None, :, None] * w[:, :, None, :]).reshape(heads * hd, heads * hd)


def _diag_blocks(m):
    hd = LRU_WIDTH // LRU_HEADS
    m4 = m.reshape(LRU_HEADS, hd, LRU_HEADS, hd)
    return jnp.stack([m4[k, :, k, :] for k in range(LRU_HEADS)])


def _seq_params(small):
    row = lambda v: v.reshape(1, -1)
    pos = jnp.arange(GMLP_BLOCK)
    mask = (pos[None, :] // CHUNK) <= (pos[:, None] // CHUNK)
    ws = jnp.where(mask[None], small["w_spatial"], 0.0)
    seq_params = (small["conv_w"], row(small["conv_b"]),
                  _block_diag(small["w_rgate"]).astype(BF16), _block_diag(small["w_igate"]).astype(BF16),
                  row(small["b_rgate"]), row(small["b_igate"]), row(small["lru_a"]),
                  row(small["v_norm_g"]), row(small["v_norm_b"]), ws.astype(BF16), small["b_spatial"].T)
    return seq_params, jnp.swapaxes(ws, 1, 2).astype(BF16)


_ANY = pl.BlockSpec(memory_space=pl.ANY)
_CHIP_FLIPS = ((1, 0), (0, 1), (1, 1))


def _position():
    return lax.axis_index("x"), lax.axis_index("y"), lax.axis_index("c")


def _flip(v, f):
    return 1 - v if f else v


def _remote(src, dst, send_sem, recv_sem, peer):
    return pltpu.make_async_remote_copy(src_ref=src, dst_ref=dst, send_sem=send_sem, recv_sem=recv_sem,
                                        device_id=peer, device_id_type=MESH)


def _allgather8(block, name, reduce):
    r, n = block.shape

    def body(x_ref, out_ref, *scratch):
        if reduce:
            gath, send_sems, recv_sems, loc_sem = scratch
        else:
            gath = out_ref
            send_sems, recv_sems, loc_sem = scratch
        x, y, c = _position()
        me = 4 * x + 2 * y + c
        loc = pltpu.make_async_copy(x_ref, gath.at[me], loc_sem)
        loc.start()
        peers = []
        for k in range(1, N_DEV):
            px, py, pc = _flip(x, k & 4), _flip(y, k & 2), _flip(c, k & 1)
            peers.append((px, py, pc))
            _remote(x_ref, gath.at[me], send_sems.at[k - 1], recv_sems.at[k - 1], (px, py, pc)).start()
        for k, (px, py, pc) in enumerate(peers):
            src = 4 * px + 2 * py + pc
            _remote(x_ref, gath.at[src], send_sems.at[k], recv_sems.at[k], (px, py, pc)).wait_recv()
        for k, peer in enumerate(peers):
            _remote(x_ref, gath.at[me], send_sems.at[k], recv_sems.at[k], peer).wait_send()
        loc.wait()
        if reduce:
            acc = gath[0]
            for k in range(1, N_DEV):
                acc = acc + gath[k]
            out_ref[...] = acc

    sems = [pltpu.SemaphoreType.DMA((N_DEV - 1,)), pltpu.SemaphoreType.DMA((N_DEV - 1,)), pltpu.SemaphoreType.DMA]
    if reduce:
        out_shape = _sds((r, n), F32)
        scratch = [pltpu.VMEM((N_DEV, r, n), F32)] + sems
    else:
        out_shape = _sds((N_DEV, r, n), F32)
        scratch = sems
    return pl.pallas_call(
        body, name=name, out_shape=out_shape,
        in_specs=[pl.BlockSpec(memory_space=pltpu.VMEM)], out_specs=pl.BlockSpec(memory_space=pltpu.VMEM),
        scratch_shapes=scratch,
        compiler_params=pltpu.CompilerParams(vmem_limit_bytes=VMEM_LIMIT_BYTES),
    )(block)


def _half(ref, c, rows):
    hr = rows // 2
    return ref.at[pl.ds(pl.multiple_of(c * hr, BF16_SUBLANES), hr), :]


def _gather_weights(shards):
    na = len(shards)

    def body(*refs):
        ins, outs = refs[:na], refs[na:2 * na]
        ici_send, ici_recv, d2d_send, d2d_recv, loc_sem = refs[2 * na:]
        x, y, c = _position()
        chip = 2 * x + y
        sibling = (x, y, 1 - c)
        local = []
        for a in range(na):
            local.append(pltpu.make_async_copy(ins[a], outs[a].at[chip], loc_sem.at[a]))
            local[-1].start()
        sends = []
        for a in range(na):
            rows = shards[a].shape[0]
            for j, (fx, fy) in enumerate(_CHIP_FLIPS):
                peer = (_flip(x, fx), _flip(y, fy), c)
                sends.append(_remote(_half(ins[a], c, rows), _half(outs[a].at[chip], c, rows),
                                     ici_send.at[a * 3 + j], ici_recv.at[a * 3 + j], peer))
                sends[-1].start()
        for a in range(na):
            rows = shards[a].shape[0]
            for j, (fx, fy) in enumerate(_CHIP_FLIPS):
                src_chip = 2 * _flip(x, fx) + _flip(y, fy)
                landed = _half(outs[a].at[src_chip], c, rows)
                _remote(landed, landed, ici_send.at[a * 3 + j], ici_recv.at[a * 3 + j], sibling).wait_recv()
                sends.append(_remote(landed, landed, d2d_send.at[a * 3 + j], d2d_recv.at[a * 3 + j], sibling))
                sends[-1].start()
        for a in range(na):
            rows = shards[a].shape[0]
            for j, (fx, fy) in enumerate(_CHIP_FLIPS):
                src_chip = 2 * _flip(x, fx) + _flip(y, fy)
                other = _half(outs[a].at[src_chip], 1 - c, rows)
                _remote(other, other, d2d_send.at[a * 3 + j], d2d_recv.at[a * 3 + j], sibling).wait_recv()
        for cp in sends:
            cp.wait_send()
        for cp in local:
            cp.wait()

    return pl.pallas_call(
        body, name="gather_weights",
        out_shape=[_sds((N_CHIPS,) + w.shape, w.dtype) for w in shards],
        in_specs=[_ANY] * na, out_specs=[_ANY] * na,
        scratch_shapes=[pltpu.SemaphoreType.DMA((3 * na,))] * 4 + [pltpu.SemaphoreType.DMA((na,))],
    )(*shards)


def _swap_halves(parts, name):
    na = len(parts)

    def body(*refs):
        ins, outs = refs[:na], refs[na:2 * na]
        send_sems, recv_sems = refs[2 * na:]
        x, y, c = _position()
        sibling = (x, y, 1 - c)
        cps = []
        for a in range(na):
            hr = parts[a].shape[1] // 2
            src = ins[a].at[:, pl.ds(pl.multiple_of((1 - c) * hr, BF16_SUBLANES), hr), :]
            cps.append(_remote(src, outs[a], send_sems.at[a], recv_sems.at[a], sibling))
            cps[-1].start()
        for cp in cps:
            cp.wait()

    return pl.pallas_call(
        body, name=name,
        out_shape=[_sds((N_CHIPS, p.shape[1] // 2, p.shape[2]), p.dtype) for p in parts],
        in_specs=[_ANY] * na, out_specs=[_ANY] * na,
        scratch_shapes=[pltpu.SemaphoreType.DMA((na,))] * 2,
    )(*parts)


def _chip_sum(part, recv, pos_arr, name):
    _, rows, cols = part.shape
    hr = rows // 2

    def body(pos_ref, p_ref, r_ref, o_ref, g_ref):
        total = (p_ref[...].astype(F32) + r_ref[...].astype(F32)).astype(BF16)
        o_ref[...] = total

        @pl.when(pl.program_id(0) == pos_ref[1])
        def _():
            g_ref[0] = total

    grid_spec = pltpu.PrefetchScalarGridSpec(
        num_scalar_prefetch=1, grid=(N_CHIPS,),
        in_specs=[pl.BlockSpec((1, hr, cols), lambda k, pos: (k, pos[0], 0)),
                  pl.BlockSpec((1, hr, cols), lambda k, pos: (k, 0, 0))],
        out_specs=[pl.BlockSpec((1, hr, cols), lambda k, pos: (k, 0, 0)),
                   pl.BlockSpec((1, 1, hr, cols), lambda k, pos: (0, pos[1], 0, 0))])
    return pl.pallas_call(
        body, name=name, grid_spec=grid_spec,
        out_shape=[_sds((N_CHIPS, hr, cols), BF16), _sds((2, N_CHIPS, hr, cols), BF16)],
        compiler_params=_params(("arbitrary",)),
    )(pos_arr, part, recv)


def _exchange_chips(sums):
    na = len(sums)

    def body(*refs):
        ins, outs = refs[:na], refs[na:2 * na]
        send_sems, recv_sems, loc_sem = refs[2 * na:]
        x, y, c = _position()
        chip = 2 * x + y
        local = []
        for a in range(na):
            local.append(pltpu.make_async_copy(ins[a].at[chip], outs[a].at[chip], loc_sem.at[a]))
            local[-1].start()
        cps = []
        for a in range(na):
            for j, (fx, fy) in enumerate(_CHIP_FLIPS):
                px, py = _flip(x, fx), _flip(y, fy)
                cps.append(_remote(ins[a].at[2 * px + py], outs[a].at[chip],
                                   send_sems.at[a * 3 + j], recv_sems.at[a * 3 + j], (px, py, c)))
                cps[-1].start()
        for a in range(na):
            for j, (fx, fy) in enumerate(_CHIP_FLIPS):
                src_chip = 2 * _flip(x, fx) + _flip(y, fy)
                landed = outs[a].at[src_chip]
                _remote(landed, landed, send_sems.at[a * 3 + j], recv_sems.at[a * 3 + j], (x, y, c)).wait_recv()
        for cp in cps:
            cp.wait_send()
        for cp in local:
            cp.wait()

    return pl.pallas_call(
        body, name="exchange_chips",
        out_shape=[_sds(s.shape, s.dtype) for s in sums],
        in_specs=[_ANY] * na, out_specs=[_ANY] * na,
        scratch_shapes=[pltpu.SemaphoreType.DMA((3 * na,))] * 2 + [pltpu.SemaphoreType.DMA((na,))],
    )(*sums)


def _sum_chips(gath, name, tr=128):
    _, hr, cols = gath.shape
    tr = min(tr, hr)

    def body(g_ref, o_ref):
        acc = g_ref[0].astype(F32)
        for k in range(1, N_CHIPS):
            acc = acc + g_ref[k].astype(F32)
        o_ref[...] = acc

    return pl.pallas_call(
        body, name=name, grid=(hr // tr,),
        in_specs=[pl.BlockSpec((N_CHIPS, tr, cols), lambda i: (0, i, 0))],
        out_specs=pl.BlockSpec((tr, cols), lambda i: (i, 0)),
        out_shape=_sds((hr, cols), F32),
        compiler_params=_params(("parallel",)),
    )(gath)


def _join_halves(halves):
    na = len(halves)

    def body(*refs):
        ins, outs = refs[:na], refs[na:2 * na]
        send_sems, recv_sems, loc_sem = refs[2 * na:]
        x, y, c = _position()
        sibling = (x, y, 1 - c)
        cps, local = [], []
        for a in range(na):
            rows = 2 * halves[a].shape[0]
            mine = _half(outs[a], c, rows)
            local.append(pltpu.make_async_copy(ins[a], mine, loc_sem.at[a]))
            local[-1].start()
            cps.append(_remote(ins[a], mine, send_sems.at[a], recv_sems.at[a], sibling))
            cps[-1].start()
        for a in range(na):
            rows = 2 * halves[a].shape[0]
            other = _half(outs[a], 1 - c, rows)
            _remote(ins[a], other, send_sems.at[a], recv_sems.at[a], sibling).wait_recv()
        for cp in cps:
            cp.wait_send()
        for cp in local:
            cp.wait()

    return pl.pallas_call(
        body, name="join_halves",
        out_shape=[_sds((2 * h.shape[0], h.shape[1]), h.dtype) for h in halves],
        in_specs=[_ANY] * na, out_specs=[_ANY] * na,
        scratch_shapes=[pltpu.SemaphoreType.DMA((na,))] * 3,
    )(*halves)


_HBM = pl.BlockSpec(memory_space=pltpu.HBM)
_SEM = pl.BlockSpec(memory_space=pltpu.SEMAPHORE)
_EFFECT = pltpu.SideEffectType.DATAFLOW_SIDE_EFFECTING


def _in_hbm(a):
    return pltpu.with_memory_space_constraint(a, pltpu.HBM)


def _split_start(srcs, lands, plan, n_copies, after, name):
    ns, nl = len(srcs), len(lands)
    bufs = list(srcs) + list(lands)

    def body(*refs):
        send_sems, recv_sems = refs[ns + nl + 1], refs[ns + nl + 2]
        token = refs[-1]
        for k, (src, dst, peer) in enumerate(plan(refs[:ns], refs[ns:ns + nl])):
            _remote(src, dst, send_sems.at[k], recv_sems.at[k], peer).start()
        token[...] = jnp.zeros_like(token)

    out = pl.pallas_call(
        body, name=name,
        out_shape=(pltpu.SemaphoreType.DMA((n_copies,)), pltpu.SemaphoreType.DMA((n_copies,)),
                   *[pltpu.HBM(b.shape, b.dtype) for b in bufs], _sds((SUBLANES, 128), F32)),
        in_specs=[_HBM] * (ns + nl) + [_ANY],
        out_specs=(_SEM, _SEM, *[_HBM] * (ns + nl), pl.BlockSpec(memory_space=pltpu.VMEM)),
        input_output_aliases={i: 2 + i for i in range(ns + nl)},
        compiler_params=pltpu.CompilerParams(has_side_effects=_EFFECT),
    )(*[_in_hbm(b) for b in bufs], after)
    return out[0], out[1], list(out[2:2 + ns]), list(out[2 + ns:2 + ns + nl]), out[-1]


def _split_wait(send_sems, recv_sems, srcs, lands, plan, after, name):
    ns, nl = len(srcs), len(lands)
    bufs = list(srcs) + list(lands)

    def body(*refs):
        send_ref, recv_ref = refs[ns + nl], refs[ns + nl + 1]
        me = _position()
        for k, src, dst in plan(refs[:ns], refs[ns:ns + nl]):
            cp = _remote(src, dst, send_ref.at[k], recv_ref.at[k], me)
            cp.wait_send()
            cp.wait_recv()

    out = pl.pallas_call(
        body, name=name,
        out_shape=[pltpu.HBM(b.shape, b.dtype) for b in bufs],
        in_specs=[_HBM] * (ns + nl) + [_SEM, _SEM, _ANY],
        out_specs=[_HBM] * (ns + nl),
        input_output_aliases={i: i for i in range(ns + nl)},
        compiler_params=pltpu.CompilerParams(has_side_effects=_EFFECT),
    )(*bufs, send_sems, recv_sems, after)
    return list(out[:ns]), list(out[ns:])


def _gather_plan(rows_of):
    def start(src_refs, land_refs):
        x, y, c = _position()
        chip = 2 * x + y
        out = []
        for a, rows in enumerate(rows_of):
            mine = _half(land_refs[a].at[chip], c, rows)
            out.extend((mine, mine, (_flip(x, fx), _flip(y, fy), c)) for fx, fy in _CHIP_FLIPS)
        return out

    def wait(src_refs, land_refs):
        x, y, c = _position()
        chip = 2 * x + y
        out = []
        for a, rows in enumerate(rows_of):
            for j, (fx, fy) in enumerate(_CHIP_FLIPS):
                src_chip = 2 * _flip(x, fx) + _flip(y, fy)
                out.append((3 * a + j, _half(land_refs[a].at[chip], c, rows),
                            _half(land_refs[a].at[src_chip], c, rows)))
        return out

    return start, wait


def _swap_halves_plan(half_rows):
    def slices(src_refs, c):
        return [src_refs[a].at[:, pl.ds(pl.multiple_of((1 - c) * hr, BF16_SUBLANES), hr), :]
                for a, hr in enumerate(half_rows)]

    def start(src_refs, land_refs):
        x, y, c = _position()
        return [(src, land_refs[a], (x, y, 1 - c)) for a, src in enumerate(slices(src_refs, c))]

    def wait(src_refs, land_refs):
        _, _, c = _position()
        return [(a, src, land_refs[a]) for a, src in enumerate(slices(src_refs, c))]

    return start, wait


def _swap_gathered_plan(n_arrays):
    def start(src_refs, land_refs):
        x, y, c = _position()
        return [(land_refs[a].at[0], land_refs[a].at[1], (x, y, 1 - c)) for a in range(n_arrays)]

    def wait(src_refs, land_refs):
        return [(a, land_refs[a].at[0], land_refs[a].at[1]) for a in range(n_arrays)]

    return start, wait


def _exchange_plan(n_arrays):
    def start(src_refs, land_refs):
        x, y, c = _position()
        chip = 2 * x + y
        out = []
        for a in range(n_arrays):
            for fx, fy in _CHIP_FLIPS:
                px, py = _flip(x, fx), _flip(y, fy)
                out.append((src_refs[a].at[2 * px + py], land_refs[a].at[0, chip], (px, py, c)))
        return out

    def wait(src_refs, land_refs):
        x, y, c = _position()
        out = []
        for a in range(n_arrays):
            for j, (fx, fy) in enumerate(_CHIP_FLIPS):
                src_chip = 2 * _flip(x, fx) + _flip(y, fy)
                out.append((3 * a + j, src_refs[a].at[src_chip], land_refs[a].at[0, src_chip]))
        return out

    return start, wait


def _forward_to_sibling(lands, name):
    na = len(lands)

    def body(*refs):
        land_refs = refs[na:2 * na]
        send_sems, recv_sems = refs[2 * na:]
        x, y, c = _position()
        sibling = (x, y, 1 - c)
        sends = []
        for a in range(na):
            rows = lands[a].shape[1]
            for j, (fx, fy) in enumerate(_CHIP_FLIPS):
                landed = _half(land_refs[a].at[2 * _flip(x, fx) + _flip(y, fy)], c, rows)
                sends.append(_remote(landed, landed, send_sems.at[3 * a + j], recv_sems.at[3 * a + j], sibling))
                sends[-1].start()
        for a in range(na):
            rows = lands[a].shape[1]
            for j, (fx, fy) in enumerate(_CHIP_FLIPS):
                other = _half(land_refs[a].at[2 * _flip(x, fx) + _flip(y, fy)], 1 - c, rows)
                _remote(other, other, send_sems.at[3 * a + j], recv_sems.at[3 * a + j], sibling).wait_recv()
        for cp in sends:
            cp.wait_send()

    return pl.pallas_call(
        body, name=name,
        out_shape=[_sds(l.shape, l.dtype) for l in lands],
        in_specs=[_ANY] * na, out_specs=[_ANY] * na,
        input_output_aliases={a: a for a in range(na)},
        scratch_shapes=[pltpu.SemaphoreType.DMA((3 * na,))] * 2,
    )(*lands)


def _swap_gathered(gath, name):
    na = len(gath)

    def body(*refs):
        gath_refs = refs[na:2 * na]
        send_sems, recv_sems = refs[2 * na:]
        x, y, c = _position()
        cps = [_remote(gath_refs[a].at[0], gath_refs[a].at[1], send_sems.at[a], recv_sems.at[a], (x, y, 1 - c))
               for a in range(na)]
        for cp in cps:
            cp.start()
        for cp in cps:
            cp.wait()

    return pl.pallas_call(
        body, name=name,
        out_shape=[_sds(g.shape, g.dtype) for g in gath],
        in_specs=[_ANY] * na, out_specs=[_ANY] * na,
        input_output_aliases={a: a for a in range(na)},
        scratch_shapes=[pltpu.SemaphoreType.DMA((na,))] * 2,
    )(*gath)


def _adam_gathered(w, gath, m, v, c_arr, name, tr=128):
    rows, cols = w.shape
    hr = rows // 2
    per = hr // tr

    def body(c_ref, w_ref, g_ref, m_ref, v_ref, go_ref, d_ref, nm_ref, nv_ref):
        g = g_ref[0, 0].astype(F32)
        for k in range(1, N_CHIPS):
            g = g + g_ref[0, k].astype(F32)
        go_ref[...] = g
        d_ref[...], nm_ref[...], nv_ref[...] = _adam_math(w_ref[...], g, m_ref[...], v_ref[...])

    def rows_of(h, i, c_ref):
        c = c_ref[0]
        return ((c + h - 2 * c * h) * per + i, 0)

    blk = pl.BlockSpec((tr, cols), rows_of)
    grid_spec = pltpu.PrefetchScalarGridSpec(
        num_scalar_prefetch=1, grid=(2, per),
        in_specs=[blk, pl.BlockSpec((1, N_CHIPS, tr, cols), lambda h, i, c_ref: (h, 0, i, 0)), blk, blk],
        out_specs=[blk] * 4)
    return pl.pallas_call(
        body, name=name, grid_spec=grid_spec, out_shape=[_sds(w.shape, F32)] * 4,
        compiler_params=_params(("arbitrary", "arbitrary")),
    )(c_arr, w, gath, m, v)


def _allreduce_small(block, name):
    r, n = block.shape
    hr = r // 2

    def body(x_ref, out_ref, sib, chipsum, gath, d2d_send, d2d_recv, ici_send, ici_recv):
        x, y, c = _position()
        chip = 2 * x + y
        sibling = (x, y, 1 - c)
        first = _remote(x_ref, sib, d2d_send.at[0], d2d_recv.at[0], sibling)
        first.start()
        first.wait()
        chipsum[...] = x_ref[...] + sib[...]
        mine = pl.ds(c * hr, hr)
        theirs = pl.ds((1 - c) * hr, hr)
        sends = []
        for j, (fx, fy) in enumerate(_CHIP_FLIPS):
            sends.append(_remote(chipsum.at[mine, :], gath.at[chip], ici_send.at[j], ici_recv.at[j],
                                 (_flip(x, fx), _flip(y, fy), c)))
            sends[-1].start()
        gath[chip] = chipsum[mine, :]
        for j, (fx, fy) in enumerate(_CHIP_FLIPS):
            landed = gath.at[2 * _flip(x, fx) + _flip(y, fy)]
            _remote(landed, landed, ici_send.at[j], ici_recv.at[j], sibling).wait_recv()
        for cp in sends:
            cp.wait_send()
        total = gath[0]
        for k in range(1, N_CHIPS):
            total = total + gath[k]
        out_ref[mine, :] = total
        last = _remote(out_ref.at[mine, :], out_ref.at[mine, :], d2d_send.at[1], d2d_recv.at[1], sibling)
        last.start()
        _remote(out_ref.at[theirs, :], out_ref.at[theirs, :], d2d_send.at[1], d2d_recv.at[1], sibling).wait_recv()
        last.wait_send()

    vmem = pl.BlockSpec(memory_space=pltpu.VMEM)
    return pl.pallas_call(
        body, name=name, out_shape=_sds((r, n), F32), in_specs=[vmem], out_specs=vmem,
        scratch_shapes=[pltpu.VMEM((r, n), F32), pltpu.VMEM((r, n), F32), pltpu.VMEM((N_CHIPS, hr, n), F32),
                        pltpu.SemaphoreType.DMA((2,)), pltpu.SemaphoreType.DMA((2,)),
                        pltpu.SemaphoreType.DMA((3,)), pltpu.SemaphoreType.DMA((3,))],
        compiler_params=pltpu.CompilerParams(vmem_limit_bytes=VMEM_LIMIT_BYTES),
    )(block)


def _cast_place(shards, chip_arr):
    na = len(shards)
    steps = 4

    def body(chip_ref, *refs):
        for a in range(na):
            refs[na + a][0] = refs[a][...].astype(BF16)

    grid_spec = pltpu.PrefetchScalarGridSpec(
        num_scalar_prefetch=1, grid=(steps,),
        in_specs=[pl.BlockSpec((s.shape[0] // steps, s.shape[1]), lambda i, ch: (i, 0)) for s in shards],
        out_specs=[pl.BlockSpec((1, s.shape[0] // steps, s.shape[1]), lambda i, ch: (ch[0], i, 0)) for s in shards])
    return pl.pallas_call(
        body, name="cast_place", grid_spec=grid_spec,
        out_shape=[_sds((N_CHIPS,) + s.shape, BF16) for s in shards],
        compiler_params=_params(("arbitrary",)),
    )(chip_arr, *shards)


def _silu(v):
    return v * _sigmoid(v)


def _ada_fwd(c8, w_ada):
    def body(c_ref, w_ref, o_ref):
        o_ref[...] = jnp.dot(_silu(c_ref[...]), w_ref[...], preferred_element_type=F32,
                             precision=lax.Precision.HIGHEST)

    return pl.pallas_call(
        body, name="ada_fwd", out_shape=_sds((N_DEV, w_ada.shape[1]), F32),
        compiler_params=pltpu.CompilerParams(vmem_limit_bytes=VMEM_LIMIT_BYTES),
    )(c8, w_ada)


def _mod_select(parts, b_ada, me_arr, after):
    cols = parts.shape[2]

    def body(me_ref, p_ref, b_ref, after_ref, o_ref):
        me = me_ref[0]
        for k in range(N_CHIPS):
            cs = slice(k * cols, (k + 1) * cols)
            o_ref[:, cs] = p_ref[2 * k, pl.ds(me, 1), :] + b_ref[:, cs]

    grid_spec = pltpu.PrefetchScalarGridSpec(
        num_scalar_prefetch=1, grid=(1,),
        in_specs=[pl.BlockSpec(parts.shape, lambda i, m: (0, 0, 0)), pl.BlockSpec(b_ada.shape, lambda i, m: (0, 0)),
                  _ANY],
        out_specs=pl.BlockSpec(b_ada.shape, lambda i, m: (0, 0)))
    return pl.pallas_call(body, name="mod_select", grid_spec=grid_spec, out_shape=_sds(b_ada.shape, F32))(
        me_arr, parts, b_ada, after)


def _ada_bwd(c8, dmod8, chip_arr, w, m, v, tr=256):
    d = c8.shape[1]
    cols = dmod8.shape[1] // N_CHIPS

    def body(chip_ref, c_ref, dm_ref, dmall_ref, w_ref, m_ref, v_ref, gw_ref, d_ref, nm_ref, nv_ref, gb_ref):
        g = lax.dot_general(_silu(c_ref[...]), dm_ref[...], (((0,), (0,)), ((), ())),
                            preferred_element_type=F32, precision=lax.Precision.HIGHEST)
        gw_ref[...] = g
        d_ref[...], nm_ref[...], nv_ref[...] = _adam_math(w_ref[...], g, m_ref[...], v_ref[...])
        acc = dmall_ref[0:1, :]
        for k in range(1, N_DEV):
            acc = acc + dmall_ref[k:k + 1, :]
        gb_ref[...] = acc

    rows = pl.BlockSpec((tr, cols), lambda i, ch: (i, 0))
    grid_spec = pltpu.PrefetchScalarGridSpec(
        num_scalar_prefetch=1, grid=(d // tr,),
        in_specs=[pl.BlockSpec((N_DEV, tr), lambda i, ch: (0, i)),
                  pl.BlockSpec((N_DEV, cols), lambda i, ch: (0, ch[0])),
                  pl.BlockSpec(dmod8.shape, lambda i, ch: (0, 0)), rows, rows, rows],
        out_specs=[rows] * 4 + [pl.BlockSpec((1, dmod8.shape[1]), lambda i, ch: (0, 0))])
    return pl.pallas_call(
        body, name="ada_bwd", grid_spec=grid_spec,
        out_shape=[_sds((d, cols), F32)] * 4 + [_sds((1, dmod8.shape[1]), F32)],
        compiler_params=_params(("arbitrary",)),
    )(chip_arr, c8, dmod8, dmod8, w, m, v)


def _adam_math(w, g, m, v):
    m = ADAM_B1 * m + (1.0 - ADAM_B1) * g
    v = ADAM_B2 * v + (1.0 - ADAM_B2) * (g * g)
    m_hat = m / (1.0 - ADAM_B1 ** ADAM_STEP)
    v_hat = v / (1.0 - ADAM_B2 ** ADAM_STEP)
    delta = -ADAM_LR * (m_hat / (jnp.sqrt(v_hat) + ADAM_EPS) + ADAM_WD * w)
    return delta, m, v


def _adam(w, g, m, v, name, tr=256):
    rows, cols = w.shape
    if rows % tr:
        tr = rows

    def body(w_ref, g_ref, m_ref, v_ref, d_ref, nm_ref, nv_ref):
        d_ref[...], nm_ref[...], nv_ref[...] = _adam_math(w_ref[...], g_ref[...], m_ref[...], v_ref[...])

    spec = pl.BlockSpec((tr, cols), lambda i: (i, 0))
    return pl.pallas_call(
        body, name=name, grid=(rows // tr,), in_specs=[spec] * 4, out_specs=[spec] * 3,
        out_shape=[_sds(w.shape, F32)] * 3, compiler_params=_params(("parallel",)),
    )(w, g, m, v)


def _adam_cols(w, g_full, m, v, chip_arr, name):
    rows, cols = w.shape

    def body(chip_ref, w_ref, g_ref, m_ref, v_ref, gs_ref, d_ref, nm_ref, nv_ref):
        g = g_ref[...]
        gs_ref[...] = g
        d_ref[...], nm_ref[...], nv_ref[...] = _adam_math(w_ref[...], g, m_ref[...], v_ref[...])

    own = pl.BlockSpec((rows, cols), lambda i, ch: (0, 0))
    grid_spec = pltpu.PrefetchScalarGridSpec(
        num_scalar_prefetch=1, grid=(1,),
        in_specs=[own, pl.BlockSpec((rows, cols), lambda i, ch: (0, ch[0])), own, own],
        out_specs=[own] * 4)
    return pl.pallas_call(body, name=name, grid_spec=grid_spec, out_shape=[_sds(w.shape, F32)] * 4)(
        chip_arr, w, g_full, m, v)


PACK_COLS = 512
SMALL_REPLICATED = ("g_mix_pre", "g_mix_post", "conv_b", "w_rgate", "b_rgate", "w_igate", "b_igate", "lru_a",
                    "v_norm_g", "v_norm_b", "w_spatial", "b_spatial", "g_lru_out", "g_gmlp_out", "g_ffn_pre",
                    "g_ffn_post", "ffn_conv_b")
SMALL_COLUMN_SHARDED = ("conv_w", "ffn_conv_w")


def _pack(arrays):
    flat = jnp.concatenate([a.reshape(1, -1) for a in arrays], axis=1)
    pad = (-flat.shape[1]) % (2 * LANES)
    if pad:
        flat = jnp.pad(flat, ((0, 0), (0, pad)))
    return flat.reshape(2, -1)


def _unpack(packed, shapes):
    flat = packed.reshape(1, -1)
    out, col = [], 0
    for shape in shapes:
        n = math.prod(shape)
        out.append(flat[:, col:col + n].reshape(shape))
        col += n
    return out


SMALL_ROW_LEN = 86016
_SMALL_ROWS = (
    (("ffn_conv_w", 18432), ("conv_w", 2048), ("w_spatial", 65536)),
    (("w_rgate", 32768), ("w_igate", 32768), ("ffn_conv_b", 6144), ("g_mix_pre", 1024), ("g_mix_post", 1024),
     ("g_ffn_pre", 1024), ("g_ffn_post", 1024), ("conv_b", 512), ("b_rgate", 512), ("b_igate", 512),
     ("lru_a", 512), ("v_norm_g", 512), ("v_norm_b", 512), ("b_spatial", 512), ("g_lru_out", 512),
     ("g_gmlp_out", 512), ("loss", 128)),
)


def _small_slots():
    slots = {}
    for row, entries in enumerate(_SMALL_ROWS):
        off = 0
        for name, size in entries:
            slots[name] = (row, off)
            off += size
        assert off <= SMALL_ROW_LEN
    return slots


SMALL_SLOT = _small_slots()
ROW_VECTORS = ("ffn_conv_b", "g_mix_pre", "g_mix_post", "g_ffn_pre", "g_ffn_post", "conv_b", "lru_a", "v_norm_g",
               "v_norm_b", "g_lru_out", "g_gmlp_out")
HEAD_DIM = LRU_WIDTH // LRU_HEADS


def _pack_small(g):
    order = ("ffn_conv_w", "conv_w", "w_spatial", "w_rgate", "w_igate", "b_rgate", "b_igate", "b_spatial", "loss") \
        + ROW_VECTORS
    vmem = pl.BlockSpec(memory_space=pltpu.VMEM)

    def body(*refs):
        src = dict(zip(order, refs))
        out_ref = refs[len(order)]
        out_ref[...] = jnp.zeros_like(out_ref)

        def put(name, lane, val):
            row, off = SMALL_SLOT[name]
            out_ref[row:row + 1, off + lane:off + lane + val.shape[1]] = val

        for name in ROW_VECTORS + ("b_rgate", "b_igate", "loss"):
            put(name, 0, src[name][...])
        for name in ("ffn_conv_w", "conv_w"):
            k_taps, n = src[name].shape
            for k in range(k_taps):
                put(name, k * n, src[name][k:k + 1, :])
        for g_idx in range(GMLP_GROUPS):
            for i in range(GMLP_BLOCK):
                put("w_spatial", (g_idx * GMLP_BLOCK + i) * GMLP_BLOCK, src["w_spatial"][g_idx, i:i + 1, :])
        for name in ("w_rgate", "w_igate"):
            for h in range(LRU_HEADS):
                for i in range(HEAD_DIM):
                    r = h * HEAD_DIM + i
                    put(name, r * HEAD_DIM, src[name][r:r + 1, h * HEAD_DIM:(h + 1) * HEAD_DIM])
        eye = (lax.broadcasted_iota(jnp.int32, (GMLP_BLOCK, GMLP_BLOCK), 0)
               == lax.broadcasted_iota(jnp.int32, (GMLP_BLOCK, GMLP_BLOCK), 1))
        for g_idx in range(GMLP_GROUPS):
            col = src["b_spatial"][:, g_idx:g_idx + 1]
            put("b_spatial", g_idx * GMLP_BLOCK, _colsum(jnp.where(eye, col, 0.0)))

    return pl.pallas_call(
        body, name="pack_small", out_shape=_sds((2, SMALL_ROW_LEN), F32),
        in_specs=[vmem] * len(order), out_specs=vmem,
        compiler_params=pltpu.CompilerParams(vmem_limit_bytes=VMEM_LIMIT_BYTES),
    )(*[g[n] for n in order])


def _adam_small(g_small, w, m, v):
    vmem = pl.BlockSpec(memory_space=pltpu.VMEM)
    n_p = len(SMALL_REPLICATED)

    def body(g_ref, *refs):
        w_refs, m_refs, v_refs = refs[:n_p], refs[n_p:2 * n_p], refs[2 * n_p:3 * n_p]
        outs = refs[3 * n_p:]
        go, do, mo, vo = outs[:n_p], outs[n_p:2 * n_p], outs[2 * n_p:3 * n_p], outs[3 * n_p:]
        for k, name in enumerate(SMALL_REPLICATED):
            row, off = SMALL_SLOT[name]

            def take(lane, n, row=row, off=off):
                return g_ref[row:row + 1, off + lane:off + lane + n]

            shape = w_refs[k].shape
            if name in ROW_VECTORS:
                go[k][...] = take(0, shape[1])
            elif name in ("b_rgate", "b_igate"):
                for h in range(LRU_HEADS):
                    go[k][0, h:h + 1, :] = take(h * HEAD_DIM, HEAD_DIM)
            elif name == "b_spatial":
                for g_idx in range(GMLP_GROUPS):
                    go[k][0, g_idx:g_idx + 1, :] = take(g_idx * GMLP_BLOCK, GMLP_BLOCK)
            elif name == "w_spatial":
                for g_idx in range(GMLP_GROUPS):
                    for i in range(GMLP_BLOCK):
                        go[k][0, g_idx, i:i + 1, :] = take((g_idx * GMLP_BLOCK + i) * GMLP_BLOCK, GMLP_BLOCK)
            else:
                for h in range(LRU_HEADS):
                    for i in range(HEAD_DIM):
                        go[k][0, h, i:i + 1, :] = take((h * HEAD_DIM + i) * HEAD_DIM, HEAD_DIM)
            do[k][...], mo[k][...], vo[k][...] = _adam_math(w_refs[k][...], go[k][...], m_refs[k][...],
                                                             v_refs[k][...])

    names = SMALL_REPLICATED
    out_shape = [_sds(w[n].shape, F32) for n in names] * 4
    res = pl.pallas_call(
        body, name="adam_small", out_shape=out_shape,
        in_specs=[vmem] * (1 + 3 * n_p), out_specs=[vmem] * (4 * n_p),
        compiler_params=pltpu.CompilerParams(vmem_limit_bytes=VMEM_LIMIT_BYTES),
    )(g_small, *[w[n] for n in names], *[m[n] for n in names], *[v[n] for n in names])
    return [dict(zip(names, res[k * n_p:(k + 1) * n_p])) for k in range(4)]


def _adam_cols(name, g_small, w, m, v, chip_arr):
    _, k_taps, n = w.shape
    row, off = SMALL_SLOT[name]
    first = off // n

    def body(chip_ref, *refs):
        g_refs = refs[:k_taps]
        w_ref, m_ref, v_ref, go_ref, d_ref, nm_ref, nv_ref = refs[k_taps:]
        for k in range(k_taps):
            tap = (0, slice(k, k + 1), slice(None))
            g = g_refs[k][row:row + 1, :]
            go_ref[tap] = g
            d_ref[tap], nm_ref[tap], nv_ref[tap] = _adam_math(w_ref[tap], g, m_ref[tap], v_ref[tap])

    whole = pl.BlockSpec(w.shape, lambda i, ch: (0, 0, 0))
    taps = [pl.BlockSpec((2, n), functools.partial(lambda i, ch, k: (0, first + N_CHIPS * k + ch[0]), k=k))
            for k in range(k_taps)]
    grid_spec = pltpu.PrefetchScalarGridSpec(
        num_scalar_prefetch=1, grid=(1,), in_specs=taps + [whole] * 3, out_specs=[whole] * 4)
    return pl.pallas_call(body, name="adam_" + name, grid_spec=grid_spec, out_shape=[_sds(w.shape, F32)] * 4)(
        chip_arr, *[g_small] * k_taps, w, m, v)


def kernel(x, c, w_ada, b_ada, g_mix_pre, g_mix_post, w_in, conv_w, conv_b, w_rgate, b_rgate, w_igate, b_igate, lru_a, v_norm_g, v_norm_b, w_spatial, b_spatial, g_lru_out, g_gmlp_out, w_out, g_ffn_pre, g_ffn_post, w_up, ffn_conv_w, ffn_conv_b, w_down, loss_target, m_w_ada, m_b_ada, m_g_mix_pre, m_g_mix_post, m_w_in, m_conv_w, m_conv_b, m_w_rgate, m_b_rgate, m_w_igate, m_b_igate, m_lru_a, m_v_norm_g, m_v_norm_b, m_w_spatial, m_b_spatial, m_g_lru_out, m_g_gmlp_out, m_w_out, m_g_ffn_pre, m_g_ffn_post, m_w_up, m_ffn_conv_w, m_ffn_conv_b, m_w_down, v_w_ada, v_b_ada, v_g_mix_pre, v_g_mix_post, v_w_in, v_conv_w, v_conv_b, v_w_rgate, v_b_rgate, v_w_igate, v_b_igate, v_lru_a, v_v_norm_g, v_v_norm_b, v_w_spatial, v_b_spatial, v_g_lru_out, v_g_gmlp_out, v_w_out, v_g_ffn_pre, v_g_ffn_post, v_w_up, v_ffn_conv_w, v_ffn_conv_b, v_w_down):
    args = dict(locals())
    names = ("w_ada", "b_ada", "g_mix_pre", "g_mix_post", "w_in", "conv_w", "conv_b", "w_rgate", "b_rgate",
             "w_igate", "b_igate", "lru_a", "v_norm_g", "v_norm_b", "w_spatial", "b_spatial", "g_lru_out",
             "g_gmlp_out", "w_out", "g_ffn_pre", "g_ffn_post", "w_up", "ffn_conv_w", "ffn_conv_b", "w_down")
    drop = lambda a: a if a.ndim == 2 else a[0]
    w = {n: drop(args[n]) for n in names}
    m = {n: drop(args["m_" + n]) for n in names}
    v = {n: drop(args["v_" + n]) for n in names}
    xi, yi, ci = _position()
    me_arr = jnp.reshape(4 * xi + 2 * yi + ci, (1,)).astype(jnp.int32)
    chip_arr = jnp.reshape(2 * xi + yi, (1,)).astype(jnp.int32)
    c_arr = jnp.reshape(ci, (1,)).astype(jnp.int32)
    pos_arr = jnp.stack([ci, 2 * xi + yi]).astype(jnp.int32)

    big = ("w_in", "w_out", "w_up", "w_down")
    lands = _cast_place([w[n] for n in big], chip_arr)
    start_a, wait_a = _gather_plan([w[n].shape[0] for n in big[:2]])
    start_b, wait_b = _gather_plan([w[n].shape[0] for n in big[2:]])

    row0 = jnp.concatenate([c, w["conv_w"].reshape(1, -1), w["ffn_conv_w"].reshape(1, -1)], axis=1)
    g0 = _allgather8(row0, "gather_cond", False)[:, 0, :]
    c8 = g0[:, :D_MODEL]
    per_chip = g0[0::2]
    conv_w_full = per_chip[:, D_MODEL:D_MODEL + 512].reshape(N_CHIPS, 4, 128).transpose(1, 0, 2).reshape(4, 512)
    ffn_conv_w_full = per_chip[:, D_MODEL + 512:].reshape(N_CHIPS, 3, 1536).transpose(1, 0, 2).reshape(3, 2 * D_FF)
    mod_parts = _allgather8(_ada_fwd(c8, w["w_ada"]), "gather_mod", False)
    send_a, recv_a, _, lands_a, token_a = _split_start([], lands[:2], start_a, 6, mod_parts, "gather_start_a")
    send_b, recv_b, _, lands_b, token_b = _split_start([], lands[2:], start_b, 6, token_a, "gather_start_b")
    mod = _mod_select(mod_parts, w["b_ada"].reshape(1, -1), me_arr, token_b).reshape(N_MOD, D_MODEL)
    sh_m, sc_m, gt_m, sh_f, sc_f, gt_f = [mod[k:k + 1] for k in range(N_MOD)]

    small = {n: w[n] for n in SMALL_REPLICATED}
    small["conv_w"] = conv_w_full
    small["ffn_conv_w"] = ffn_conv_w_full
    row = lambda a: a.reshape(1, -1)
    seq_params, ws_t = _seq_params(small)
    glo, ggo = row(small["g_lru_out"]), row(small["g_gmlp_out"])
    g_pre, g_post = row(small["g_mix_pre"]), row(small["g_mix_post"])
    g_pre2, g_post2 = row(small["g_ffn_pre"]), row(small["g_ffn_post"])
    fw, fb = small["ffn_conv_w"], row(small["ffn_conv_b"])
    xs, tgt = x[0], loss_target[0]

    _, lands_a = _split_wait(send_a, recv_a, [], lands_a, wait_a, mod, "gather_wait_a")
    w_in4, w_out4 = _forward_to_sibling(lands_a, "forward_a")
    w_out_b = w_out4.reshape(D_MODEL, D_MODEL)
    z, h = _mix_in(xs, sc_m, sh_m, g_pre, w_in4)
    ycat, hst, stash = _seqmix(z, seq_params, glo, ggo)
    y, x1, h2 = _mix_out(ycat, xs, w_out_b, gt_m, g_post, g_pre2, sc_f, sh_f)
    _, lands_b = _split_wait(send_b, recv_b, [], lands_b, wait_b, h2, "gather_wait_b")
    w_up4, w_down4 = _forward_to_sibling(lands_b, "forward_b")
    w_down_b = w_down4.reshape(D_FF, D_MODEL)
    up0, pre, act, dy2, dx2, loss, dgt_f, dg_post2 = _ffn_fwd(h2, x1, tgt, w_up4, w_down_b, fw, fb, gt_f, g_post2)

    dup0, dfw, dfb = _ffn_bwd_a(dy2, pre, up0, w_down_b, fw)
    gw_up = _wgrad(h2, dup0, N_CHIPS, "wgrad_up", True)
    gw_down = _wgrad(act, dy2, 2, "wgrad_down", False)
    ex_start, ex_wait = _exchange_plan(2)
    sg_start, sg_wait = _swap_gathered_plan(2)
    grads, deltas, new_m, new_v = {}, {}, {}, {}

    def swap_start(parts, name):
        sw_start, sw_wait = _swap_halves_plan([p.shape[1] // 2 for p in parts])
        recv = [lax.empty((N_CHIPS, p.shape[1] // 2, p.shape[2]), BF16) for p in parts]
        send_s, recv_s, parts, recv, token = _split_start(parts, recv, sw_start, len(parts), pos_arr,
                                                           "swap_start_" + name)
        return (send_s, recv_s, parts, recv, sw_wait), token

    def exchange_start(swap, tags, after, name):
        send_s, recv_s, parts, recv, sw_wait = swap
        parts, recv = _split_wait(send_s, recv_s, parts, recv, sw_wait, after, "swap_wait_" + name)
        both = [_chip_sum(p, r, pos_arr, "chip_sum_" + t) for p, r, t in zip(parts, recv, tags)]
        sums, gath = [b[0] for b in both], [b[1] for b in both]
        return _split_start(sums, gath, ex_start, 3 * len(parts), pos_arr, "exchange_start_" + name)

    def gathered_start(exchange, after, name):
        send_s, recv_s, sums, gath, _ = exchange
        _, gath = _split_wait(send_s, recv_s, sums, gath, ex_wait, after, "exchange_wait_" + name)
        send_s, recv_s, _, gath, token = _split_start([], gath, sg_start, len(gath), pos_arr,
                                                      "gathered_start_" + name)
        return (send_s, recv_s, gath), token

    def finish(gathered, tags, after, name):
        send_s, recv_s, gath = gathered
        _, gath = _split_wait(send_s, recv_s, [], gath, sg_wait, after, "gathered_wait_" + name)
        for g, t in zip(gath, tags):
            grads[t], deltas[t], new_m[t], new_v[t] = _adam_gathered(w[t], g, m[t], v[t], c_arr, "adam_" + t)

    def behind(value, token):
        return value + token[0:1, 0:1]

    tags_b, tags_a = ("w_up", "w_down"), ("w_in", "w_out")
    swap_b, tok = swap_start([gw_up, gw_down.reshape(N_CHIPS, -1, D_MODEL)], "b")
    dx1, dy, dsh_f, dsc_f, dg_pre2, dgt_m, dg_post = _ffn_bwd_b(
        dup0, x1, y, dx2, w_up4, g_pre2, behind(sc_f, tok), sh_f, gt_m, g_post)
    exchange_b = exchange_start(swap_b, tags_b, dg_post, "b")
    (dz, dcw, dcb, dwr, dwi, dbr, dbi, dspa, dng, dnb, dws, dbs_t, dglo, dggo) = _seqmix_bwd(
        z, hst, stash, dy, w_out_b, seq_params, ws_t, behind(glo, exchange_b[4]), ggo)
    grad_x, dsh_m, dsc_m, dg_pre = _mix_in_bwd(xs, dz, dx1, w_in4, g_pre, sc_m)
    gw_in = _wgrad(h, dz, N_CHIPS, "wgrad_in", True)
    gw_out = _wgrad(ycat, dy, 1, "wgrad_out", False)
    swap_a, tok = swap_start([gw_in, gw_out.reshape(N_CHIPS, -1, D_MODEL)], "a")

    dmod = jnp.concatenate([behind(dsh_m, tok), dsc_m, dgt_m, dsh_f, dsc_f, dgt_f], axis=1)
    dmod8 = _allgather8(dmod, "gather_dmod", False)[:, 0, :]
    grads["w_ada"], deltas["w_ada"], new_m["w_ada"], new_v["w_ada"], g_b_ada = _ada_bwd(
        c8, dmod8, chip_arr, w["w_ada"], m["w_ada"], v["w_ada"])
    exchange_a = exchange_start(swap_a, tags_a, g_b_ada, "a")
    gathered_b, tok = gathered_start(exchange_b, exchange_a[4], "b")

    small_grads = dict(
        g_mix_pre=dg_pre, g_mix_post=dg_post, conv_w=dcw, conv_b=dcb, w_rgate=dwr, b_rgate=dbr, w_igate=dwi,
        b_igate=dbi, lru_a=dspa, v_norm_g=dng, v_norm_b=dnb, w_spatial=dws, b_spatial=dbs_t, g_lru_out=dglo,
        g_gmlp_out=dggo, g_ffn_pre=dg_pre2, g_ffn_post=dg_post2, ffn_conv_w=dfw, ffn_conv_b=dfb,
        loss=behind(loss, tok))
    g_small = _allreduce_small(_pack_small(small_grads), "reduce_small")
    total = g_small[SMALL_SLOT["loss"][0], SMALL_SLOT["loss"][1]]

    rep = SMALL_REPLICATED
    small_out = _adam_small(g_small, {n: args[n] for n in rep}, {n: args["m_" + n] for n in rep},
                            {n: args["v_" + n] for n in rep})
    for n in rep:
        grads[n], deltas[n], new_m[n], new_v[n] = [group[n] for group in small_out]
    finish(gathered_b, tags_b, deltas[rep[0]], "b")

    gathered_a, tok = gathered_start(exchange_a, deltas["w_down"], "a")
    for n in SMALL_COLUMN_SHARDED:
        grads[n], deltas[n], new_m[n], new_v[n] = _adam_cols(n, g_small, args[n], args["m_" + n],
                                                             behind(args["v_" + n], tok), chip_arr)
    d_b, m_b, v_b = _adam(w["b_ada"], g_b_ada, m["b_ada"], behind(v["b_ada"], tok), "adam_b_ada")
    grads["b_ada"], deltas["b_ada"], new_m["b_ada"], new_v["b_ada"] = g_b_ada, d_b, m_b, v_b
    finish(gathered_a, tags_a, d_b, "a")

    outs = [total, grad_x[None]]
    for group in (grads, deltas, new_m, new_v):
        outs.extend(group[n].reshape(args[n].shape) for n in names)
    return tuple(outs)
```

```python
import functools
import math

import jax
import jax.numpy as jnp
from jax import lax
from jax.experimental import pallas as pl
from jax.experimental.pallas import tpu as pltpu

F32 = jnp.float32
BF16 = jnp.bfloat16
MESH = pl.DeviceIdType.MESH

D_MODEL = 1024
LRU_WIDTH = 512
LRU_HEADS = 8
GMLP_WIDTH = 512
GMLP_GROUPS = 4
GMLP_BLOCK = 128
CHUNK = 64
D_FF = 3072
N_MOD = 6
EPS = 1e-6
LRU_C = 8.0
N_CHIPS = 4
N_DEV = 8

ADAM_LR = 0.001
ADAM_B1 = 0.9
ADAM_B2 = 0.999
ADAM_EPS = 1e-08
ADAM_WD = 0.01
ADAM_STEP = 10

GELU_C0 = math.sqrt(2.0 / math.pi)
GELU_C1 = 0.044715

VMEM_LIMIT_BYTES = 56 * 1024 * 1024
SUBLANES = 8
LANES = 128
BF16_SUBLANES = 16
FFN_CHUNK = 768
SUB_ROWS = 256


def _gelu_gate(x):
    x2 = x * x
    z = x * ((2.0 * GELU_C0 * GELU_C1) * x2 + 2.0 * GELU_C0)
    return 1.0 / (1.0 + jnp.exp(-z)), x2


def _gelu(x):
    t = jnp.tanh(GELU_C0 * (x + GELU_C1 * x * x * x))
    return 0.5 * x * (1.0 + t)


def _gelu_and_grad(x):
    s, x2 = _gelu_gate(x)
    g = x * s
    dz = (6.0 * GELU_C0 * GELU_C1) * x2 + 2.0 * GELU_C0
    return g, s + g * (1.0 - s) * dz


def _sigmoid(x):
    return 1.0 / (1.0 + jnp.exp(-x))


def _log1p(u):
    w = 1.0 + u
    return jnp.where(w == 1.0, u, jnp.log(w) * (u / (w - 1.0)))


def _softplus(x):
    return jnp.maximum(x, 0.0) + _log1p(jnp.exp(-jnp.abs(x)))


def _neg_expm1(x):
    u = jnp.exp(x)
    um1 = u - 1.0
    tiny = um1 == 0.0
    small = um1 * (x / jnp.log(jnp.where(tiny, 2.0, jnp.maximum(u, 0.25))))
    return -jnp.where(tiny, x, jnp.where(x < -1.0, um1, small))


def _msq_rsqrt(v):
    return lax.rsqrt(jnp.mean(v * v, axis=-1, keepdims=True) + EPS)


def _rms_bwd(dyn, yn, r):
    return r * (dyn - yn * jnp.mean(dyn * yn, axis=-1, keepdims=True))


def _colsum(v):
    return jnp.sum(v, axis=0, keepdims=True)


def _shift_down(cur, prev8, k):
    rolled = pltpu.roll(cur, k, 0)
    head = pltpu.roll(prev8, k, 0)
    row8 = lax.broadcasted_iota(jnp.int32, (SUBLANES, cur.shape[1]), 0)
    first = jnp.where(row8 < k, head, rolled[0:SUBLANES])
    return jnp.concatenate([first, rolled[SUBLANES:]], axis=0)


def _shift_up(cur, next8, k):
    t = cur.shape[0]
    rolled = pltpu.roll(cur, t - k, 0)
    tail = pltpu.roll(next8, SUBLANES - k, 0)
    row8 = lax.broadcasted_iota(jnp.int32, (SUBLANES, cur.shape[1]), 0)
    last = jnp.where(row8 >= SUBLANES - k, tail, rolled[t - SUBLANES:])
    return jnp.concatenate([rolled[:t - SUBLANES], last], axis=0)


def _scan_fwd(a, b):
    t = a.shape[0]
    row = lax.broadcasted_iota(jnp.int32, a.shape, 0)
    d = 1
    while d < t:
        keep = row >= d
        a_s = jnp.where(keep, pltpu.roll(a, d, 0), 1.0)
        b_s = jnp.where(keep, pltpu.roll(b, d, 0), 0.0)
        b = a * b_s + b
        a = a * a_s
        d *= 2
    return a, b


def _scan_bwd(a, g):
    t = a.shape[0]
    row = lax.broadcasted_iota(jnp.int32, a.shape, 0)
    d = 1
    while d < t:
        keep = row < t - d
        a_s = jnp.where(keep, pltpu.roll(a, t - d, 0), 1.0)
        g_s = jnp.where(keep, pltpu.roll(g, t - d, 0), 0.0)
        g = a * g_s + g
        a = a * a_s
        d *= 2
    return a, g


def _dot(a, b):
    return jnp.dot(a, b, preferred_element_type=F32)


def _dot_nt(a, b):
    return lax.dot_general(a, b, (((1,), (1,)), ((), ())), preferred_element_type=F32)


def _dot_tn(a, b):
    return lax.dot_general(a, b, (((0,), (0,)), ((), ())), preferred_element_type=F32)


def _rows(ts, cols, rev_of=None):
    if rev_of is None:
        return pl.BlockSpec((ts, cols), lambda i: (i, 0))
    return pl.BlockSpec((ts, cols), lambda i: (rev_of - 1 - i, 0))


def _halo_prev(ts, cols, halo, rev_of=None, col_block=0):
    per = ts // halo
    if rev_of is None:
        return pl.BlockSpec((halo, cols), lambda i: (jnp.maximum(i * per - 1, 0), col_block))
    return pl.BlockSpec((halo, cols), lambda i: (jnp.maximum((rev_of - 1 - i) * per - 1, 0), col_block))


def _full(shape):
    nd = len(shape)
    return pl.BlockSpec(shape, lambda *_: (0,) * nd)


_RESIDENT = pl.BlockSpec(memory_space=pltpu.VMEM)


def _params(sem):
    return pltpu.CompilerParams(dimension_semantics=sem, vmem_limit_bytes=VMEM_LIMIT_BYTES)


def _sds(shape, dtype):
    return jax.ShapeDtypeStruct(shape, dtype)


def _sub_tiles(ts):
    return [slice(r0, r0 + SUB_ROWS) for r0 in range(0, ts, SUB_ROWS)]


def _mix_in(x, sc, sh, g, w_in4, ts=512):
    s, d = x.shape

    def body(x_ref, sc_ref, sh_ref, g_ref, w_ref, z_ref, h_ref):
        for rs in _sub_tiles(ts):
            xv = x_ref[rs, :]
            h = (xv * _msq_rsqrt(xv) * g_ref[...]) * (1.0 + sc_ref[...]) + sh_ref[...]
            hb = h.astype(BF16)
            h_ref[rs, :] = hb
            for k in range(N_CHIPS):
                z_ref[rs, k * 512:(k + 1) * 512] = _dot(hb, w_ref[k])

    return pl.pallas_call(
        body, grid=(s // ts,), name="mix_in",
        in_specs=[_rows(ts, d), _full((1, d)), _full((1, d)), _full((1, d)), _full(w_in4.shape)],
        out_specs=[_rows(ts, 2048), _rows(ts, d)],
        out_shape=[_sds((s, 2048), F32), _sds((s, d), BF16)],
        compiler_params=_params(("parallel",)),
    )(x, sc, sh, g, w_in4)


N_STASH = 12
(ST_XC, ST_R, ST_IG, ST_A, ST_MULT, ST_GL, ST_DGL, ST_U, ST_DU, ST_Q, ST_VHAT, ST_SPB) = range(N_STASH)


def _seq_param_specs():
    return [_full((4, 512)), _full((1, 512)), _full((512, 512)), _full((512, 512)), _full((1, 512)),
            _full((1, 512)), _full((1, 512)), _full((1, 512)), _full((1, 512)), _full((4, 128, 128)),
            _full((128, 4))]


def _seqmix(z, seq_params, glo, ggo, ts=256):
    s = z.shape[0]
    nt = s // ts

    def body(z_ref, zprev_ref, cw_ref, cb_ref, bdr_ref, bdi_ref, br_ref, bi_ref, la_ref, ng_ref, nb_ref,
             ws_ref, bst_ref, glo_ref, ggo_ref, ycat_ref, hst_ref, st_ref, hcarry, sp_scr):
        i = pl.program_id(0)

        @pl.when(i == 0)
        def _():
            hcarry[...] = jnp.zeros_like(hcarry)

        lx = z_ref[:, 0:512]
        prev8 = jnp.where(i == 0, 0.0, zprev_ref[...])
        xc = (cw_ref[3:4, :] * lx + cw_ref[2:3, :] * _shift_down(lx, prev8, 1)
              + cw_ref[1:2, :] * _shift_down(lx, prev8, 2) + cw_ref[0:1, :] * _shift_down(lx, prev8, 3)
              + cb_ref[...])
        xcb = xc.astype(BF16)
        r = _sigmoid(_dot(xcb, bdr_ref[...]) + br_ref[...])
        ig = _sigmoid(_dot(xcb, bdi_ref[...]) + bi_ref[...])
        log_a = (-LRU_C) * r * _softplus(-la_ref[...])
        a = jnp.exp(log_a)
        mult = jnp.sqrt(_neg_expm1(2.0 * log_a))
        acum, hloc = _scan_fwd(a, mult * (ig * xc))
        h = hloc + acum * hcarry[...]
        hcarry[...] = h[ts - 1:ts, :]
        hst_ref[...] = h
        gl, dgl = _gelu_and_grad(z_ref[:, 512:1024])
        y_l = h * gl
        for slot, val in ((ST_XC, xc), (ST_R, r), (ST_IG, ig), (ST_A, a), (ST_MULT, mult), (ST_GL, gl),
                          (ST_DGL, dgl)):
            st_ref[slot] = val

        u, du = _gelu_and_grad(z_ref[:, 1024:1536])
        vg, dvg = _gelu_and_grad(z_ref[:, 1536:2048])
        vc = vg - jnp.mean(vg, axis=-1, keepdims=True)
        rstd = lax.rsqrt(jnp.mean(vc * vc, axis=-1, keepdims=True) + EPS)
        vhat = vc * rstd
        vb = (vhat * ng_ref[...] + nb_ref[...]).astype(BF16)
        for n in range(ts // GMLP_BLOCK):
            rs = slice(n * GMLP_BLOCK, (n + 1) * GMLP_BLOCK)
            for g in range(GMLP_GROUPS):
                cs = slice(g * 128, (g + 1) * 128)
                sp_scr[rs, cs] = _dot(ws_ref[g], vb[rs, cs]) + bst_ref[:, g:g + 1]
        spb = sp_scr[...]
        y_g = u * spb
        for slot, val in ((ST_U, u), (ST_DU, du), (ST_Q, rstd * dvg), (ST_VHAT, vhat), (ST_SPB, spb)):
            st_ref[slot] = val

        ycat_ref[:, 0:512] = (y_l * _msq_rsqrt(y_l) * glo_ref[...]).astype(BF16)
        ycat_ref[:, 512:1024] = (y_g * _msq_rsqrt(y_g) * ggo_ref[...]).astype(BF16)

    return pl.pallas_call(
        body, grid=(nt,), name="seqmix",
        in_specs=[_rows(ts, 2048), _halo_prev(ts, 512, SUBLANES)] + _seq_param_specs()
        + [_full((1, 512)), _full((1, 512))],
        out_specs=[_rows(ts, 1024), _rows(ts, 512), pl.BlockSpec((N_STASH, ts, 512), lambda i: (0, i, 0))],
        out_shape=[_sds((s, 1024), BF16), _sds((s, 512), F32), _sds((N_STASH, s, 512), F32)],
        scratch_shapes=[pltpu.VMEM((1, 512), F32), pltpu.VMEM((ts, 512), F32)],
        compiler_params=_params(("arbitrary",)),
    )(z, z, *seq_params, glo, ggo)


def _mix_out(ycat, x, w_out, gt_m, g_post, g_pre2, sc_f, sh_f, ts=512):
    s, d = x.shape

    def body(yc_ref, x_ref, w_ref, gt_ref, gp_ref, g2_ref, sc_ref, sh_ref, y_ref, x1_ref, h2_ref):
        for rs in _sub_tiles(ts):
            y = _dot(yc_ref[rs, :], w_ref[...])
            y_ref[rs, :] = y
            x1 = x_ref[rs, :] + gt_ref[...] * (y * _msq_rsqrt(y) * gp_ref[...])
            x1_ref[rs, :] = x1
            h2 = (x1 * _msq_rsqrt(x1) * g2_ref[...]) * (1.0 + sc_ref[...]) + sh_ref[...]
            h2_ref[rs, :] = h2.astype(BF16)

    vec = _full((1, d))
    return pl.pallas_call(
        body, grid=(s // ts,), name="mix_out",
        in_specs=[_rows(ts, d), _rows(ts, d), _full((d, d)), vec, vec, vec, vec, vec],
        out_specs=[_rows(ts, d), _rows(ts, d), _rows(ts, d)],
        out_shape=[_sds((s, d), F32), _sds((s, d), F32), _sds((s, d), BF16)],
        compiler_params=_params(("parallel",)),
    )(ycat, x, w_out, gt_m, g_post, g_pre2, sc_f, sh_f)


def _ffn_cols(j):
    per = (2 * D_FF // N_CHIPS) // FFN_CHUNK
    return j // per, (j % per) * FFN_CHUNK, j * FFN_CHUNK


def _ffn_fwd(h2, x1, tgt, w_up4, w_down, fw, fb, gt_f, g_post, ts=256):
    s, d = x1.shape
    nch = D_FF // FFN_CHUNK

    def body(h2_ref, x1_ref, tgt_ref, wup_ref, wdn_ref, fw_ref, fb_ref, gt_ref, gp_ref,
             up0_ref, pre_ref, act_ref, dy2_ref, dx2_ref, loss_ref, dgt_ref, dgp_ref, tail_ref):
        i = pl.program_id(0)

        @pl.when(i == 0)
        def _():
            tail_ref[...] = jnp.zeros_like(tail_ref)
            loss_ref[...] = jnp.zeros_like(loss_ref)
            dgt_ref[...] = jnp.zeros_like(dgt_ref)
            dgp_ref[...] = jnp.zeros_like(dgp_ref)

        hb = h2_ref[...]

        def up_project(j):
            sh_g, off, _ = _ffn_cols(j)
            return [_dot(hb, wup_ref[shard, :, off:off + FFN_CHUNK]).astype(BF16) for shard in (sh_g, sh_g + 2)]

        y2 = jnp.zeros((ts, d), F32)
        ahead = up_project(0)
        for j in range(nch):
            _, _, col = _ffn_cols(j)
            ubs = ahead
            if j + 1 < nch:
                ahead = up_project(j + 1)
            halves = []
            for ub, c0 in zip(ubs, (col, D_FF + col)):
                cs = slice(c0, c0 + FFN_CHUNK)
                up0_ref[:, cs] = ub
                u = ub.astype(F32)
                prev8 = tail_ref[:, cs]
                tail_ref[:, cs] = u[ts - SUBLANES:, :]
                halves.append(fw_ref[2:3, cs] * u + fw_ref[1:2, cs] * _shift_down(u, prev8, 1)
                              + fw_ref[0:1, cs] * _shift_down(u, prev8, 2) + fb_ref[:, cs])
                pre_ref[:, cs] = halves[-1].astype(BF16)
            act = (_gelu(halves[0]) * halves[1]).astype(BF16)
            act_ref[:, col:col + FFN_CHUNK] = act
            y2 = y2 + _dot(act, wdn_ref[col:col + FFN_CHUNK, :])
        r2 = _msq_rsqrt(y2)
        yn = y2 * r2
        yng = yn * gp_ref[...]
        e = x1_ref[...] + gt_ref[...] * yng - tgt_ref[...]
        loss_ref[...] += jnp.sum(e * e) * (0.5 / d)
        dx2 = e * (1.0 / d)
        dx2_ref[...] = dx2
        dgt_ref[...] += _colsum(dx2 * yng)
        dyng = dx2 * gt_ref[...]
        dgp_ref[...] += _colsum(dyng * yn)
        dy2_ref[...] = _rms_bwd(dyng * gp_ref[...], yn, r2).astype(BF16)

    vec = _full((1, d))
    return pl.pallas_call(
        body, grid=(s // ts,), name="ffn_fwd",
        in_specs=[_rows(ts, d), _rows(ts, d), _rows(ts, d), _RESIDENT, _RESIDENT,
                  _full((3, 2 * D_FF)), _full((1, 2 * D_FF)), vec, vec],
        out_specs=[_rows(ts, 2 * D_FF), _rows(ts, 2 * D_FF), _rows(ts, D_FF), _rows(ts, d), _rows(ts, d),
                   _full((1, 128)), vec, vec],
        out_shape=[_sds((s, 2 * D_FF), BF16), _sds((s, 2 * D_FF), BF16), _sds((s, D_FF), BF16), _sds((s, d), BF16),
                   _sds((s, d), F32), _sds((1, 128), F32), _sds((1, d), F32), _sds((1, d), F32)],
        scratch_shapes=[pltpu.VMEM((SUBLANES, 2 * D_FF), F32)],
        compiler_params=_params(("arbitrary",)),
    )(h2, x1, tgt, w_up4, w_down, fw, fb, gt_f, g_post)


def _shift_up_mxu(vb, up_mat, next8, k):
    t = vb.shape[0]
    main = _dot(up_mat, vb)
    tail = pltpu.roll(next8, SUBLANES - k, 0)
    row8 = lax.broadcasted_iota(jnp.int32, next8.shape, 0)
    last = main[t - SUBLANES:] + jnp.where(row8 >= SUBLANES - k, tail, 0.0)
    return jnp.concatenate([main[:t - SUBLANES], last], axis=0)


def _ffn_bwd_a(dy2, pre, up0, w_down, fw, ts=256):
    s, d = dy2.shape
    nt = s // ts
    nch = D_FF // FFN_CHUNK
    wide = 2 * D_FF
    up_mats = jnp.stack([jnp.eye(ts, k=1, dtype=BF16), jnp.eye(ts, k=2, dtype=BF16)])

    def body(dy2_ref, pre_ref, up0_ref, wdn_ref, fw_ref, um_ref, dup0_ref, dfw_ref, dfb_ref, next_ref):
        i = pl.program_id(0)

        @pl.when(i == 0)
        def _():
            next_ref[...] = jnp.zeros_like(next_ref)
            dfw_ref[...] = jnp.zeros_like(dfw_ref)
            dfb_ref[...] = jnp.zeros_like(dfb_ref)

        dyb = dy2_ref[...]
        for j in range(nch):
            _, _, col = _ffn_cols(j)
            dact = _dot_nt(dyb, wdn_ref[col:col + FFN_CHUNK, :])
            gl, dgl = _gelu_and_grad(pre_ref[:, col:col + FFN_CHUNK].astype(F32))
            dpre = (dact * pre_ref[:, D_FF + col:D_FF + col + FFN_CHUNK].astype(F32) * dgl, dact * gl)
            for half, c0 in enumerate((col, D_FF + col)):
                cs = slice(c0, c0 + FFN_CHUNK)
                dp = dpre[half]
                dpb = dp.astype(BF16)
                nxt = next_ref[:, cs]
                next_ref[:, cs] = dpb.astype(F32)[0:SUBLANES, :]
                su1 = _shift_up_mxu(dpb, um_ref[0], nxt, 1)
                su2 = _shift_up_mxu(dpb, um_ref[1], nxt, 2)
                u = up0_ref[:, cs].astype(F32)
                dfb_ref[:, cs] += _colsum(dp)
                dfw_ref[2:3, cs] += _colsum(dp * u)
                dfw_ref[1:2, cs] += _colsum(su1 * u)
                dfw_ref[0:1, cs] += _colsum(su2 * u)
                dup0 = fw_ref[2:3, cs] * dp + fw_ref[1:2, cs] * su1 + fw_ref[0:1, cs] * su2
                dup0_ref[:, cs] = dup0.astype(BF16)

    return pl.pallas_call(
        body, grid=(nt,), name="ffn_bwd_a",
        in_specs=[_rows(ts, d, nt), _rows(ts, wide, nt), _rows(ts, wide, nt), _RESIDENT,
                  _full((3, wide)), _full((2, ts, ts))],
        out_specs=[_rows(ts, wide, nt), _full((3, wide)), _full((1, wide))],
        out_shape=[_sds((s, wide), BF16), _sds((3, wide), F32), _sds((1, wide), F32)],
        scratch_shapes=[pltpu.VMEM((SUBLANES, wide), F32)],
        compiler_params=_params(("arbitrary",)),
    )(dy2, pre, up0, w_down, fw, up_mats)


def _ffn_bwd_b(dup0, x1, y, dx2, w_up4, g_pre2, sc_f, sh_f, gt_m, g_post_m, ts=512):
    s, d = x1.shape
    shard_cols = 2 * D_FF // N_CHIPS

    def body(dup_ref, x1_ref, y_ref, dx2_ref, wup_ref, g2_ref, sc_ref, sh_ref, gt_ref, gp_ref,
             dx1_ref, dy_ref, dsh_ref, dsc_ref, dg2_ref, dgt_ref, dgp_ref):
        i = pl.program_id(0)

        @pl.when(i == 0)
        def _():
            for ref in (dsh_ref, dsc_ref, dg2_ref, dgt_ref, dgp_ref):
                ref[...] = jnp.zeros_like(ref)

        for rs in _sub_tiles(ts):
            dh2 = jnp.zeros((SUB_ROWS, d), F32)
            for k in range(N_CHIPS):
                dh2 = dh2 + _dot_nt(dup_ref[rs, k * shard_cols:(k + 1) * shard_cols], wup_ref[k])
            x1v = x1_ref[rs, :]
            r2 = _msq_rsqrt(x1v)
            xn = x1v * r2
            hn = xn * g2_ref[...]
            dsh_ref[...] += _colsum(dh2)
            dsc_ref[...] += _colsum(dh2 * hn)
            dhn = dh2 * (1.0 + sc_ref[...])
            dg2_ref[...] += _colsum(dhn * xn)
            dx1 = dx2_ref[rs, :] + _rms_bwd(dhn * g2_ref[...], xn, r2)
            dx1_ref[rs, :] = dx1
            yv = y_ref[rs, :]
            ry = _msq_rsqrt(yv)
            yn = yv * ry
            dgt_ref[...] += _colsum(dx1 * (yn * gp_ref[...]))
            dyng = dx1 * gt_ref[...]
            dgp_ref[...] += _colsum(dyng * yn)
            dy_ref[rs, :] = _rms_bwd(dyng * gp_ref[...], yn, ry).astype(BF16)

    vec = _full((1, d))
    return pl.pallas_call(
        body, grid=(s // ts,), name="ffn_bwd_b",
        in_specs=[_rows(ts, 2 * D_FF), _rows(ts, d), _rows(ts, d), _rows(ts, d), _RESIDENT,
                  vec, vec, vec, vec, vec],
        out_specs=[_rows(ts, d), _rows(ts, d), vec, vec, vec, vec, vec],
        out_shape=[_sds((s, d), F32), _sds((s, d), BF16)] + [_sds((1, d), F32)] * 5,
        compiler_params=_params(("arbitrary",)),
    )(dup0, x1, y, dx2, w_up4, g_pre2, sc_f, sh_f, gt_m, g_post_m)


def _seqmix_bwd(z, hst, stash, dy, w_out, seq_params, ws_t, glo, ggo, ts=256):
    s = z.shape[0]
    nt = s // ts
    small_shapes = [(4, 512), (1, 512), (512, 512), (512, 512), (1, 512), (1, 512), (1, 512),
                    (1, 512), (1, 512), (4, 128, 128), (128, 4), (1, 512), (1, 512)]

    def body(lx_ref, hst_ref, hprev_ref, st_ref, dy_ref, wout_ref, cw_ref, cb_ref, bdr_ref, bdi_ref, br_ref,
             bi_ref, la_ref, ng_ref, nb_ref, ws_ref, bst_ref, wst_ref, glo_ref, ggo_ref, dz_ref, *rest):
        small_refs = rest[:13]
        (dcw_ref, dcb_ref, dwr_ref, dwi_ref, dbr_ref, dbi_ref, dspa_ref, dng_ref, dnb_ref, dws_ref, dbs_ref,
         dglo_ref, dggo_ref) = small_refs
        gcarry, anext, dxcnext, dv_scr = rest[13:]
        i = pl.program_id(0)

        @pl.when(i == 0)
        def _():
            for ref in small_refs:
                ref[...] = jnp.zeros_like(ref)
            gcarry[...] = jnp.zeros_like(gcarry)
            anext[...] = jnp.ones_like(anext)
            dxcnext[...] = jnp.zeros_like(dxcnext)

        first_tile = i == nt - 1
        xc, r, ig, a, mult = st_ref[ST_XC], st_ref[ST_R], st_ref[ST_IG], st_ref[ST_A], st_ref[ST_MULT]
        gl, u, spb, vhat = st_ref[ST_GL], st_ref[ST_U], st_ref[ST_SPB], st_ref[ST_VHAT]
        lx = lx_ref[...]
        h = hst_ref[...]
        hprev = _shift_down(h, jnp.where(first_tile, 0.0, hprev_ref[...]), 1)
        y_l = h * gl
        y_g = u * spb

        dycat = _dot_nt(dy_ref[...], wout_ref[...])
        rl = _msq_rsqrt(y_l)
        yln = y_l * rl
        dyl = dycat[:, 0:512]
        dglo_ref[...] += _colsum(dyl * yln)
        dy_l = _rms_bwd(dyl * glo_ref[...], yln, rl)
        rg = _msq_rsqrt(y_g)
        ygn = y_g * rg
        dyg = dycat[:, 512:1024]
        dggo_ref[...] += _colsum(dyg * ygn)
        dy_g = _rms_bwd(dyg * ggo_ref[...], ygn, rg)

        dz_ref[:, 512:1024] = (dy_l * h * st_ref[ST_DGL]).astype(BF16)
        a_up = _shift_up(a, anext[...], 1)
        acum, gloc = _scan_bwd(a_up, dy_l * gl)
        gg = gloc + acum * gcarry[...]
        gcarry[...] = gg[0:1, :]
        anext[...] = a[0:SUBLANES, :]
        da = gg * hprev
        t1 = gg * mult
        di = t1 * xc
        dxc = t1 * ig
        dmult = gg * ig * xc
        dla = da * a - dmult * (a * a / mult)
        dspa_ref[...] += _colsum(dla * r) * (-LRU_C)
        dpr = dla * ((-LRU_C) * _softplus(-la_ref[...])) * r * (1.0 - r)
        dpi = di * ig * (1.0 - ig)
        dbr_ref[...] += _colsum(dpr)
        dbi_ref[...] += _colsum(dpi)
        dprb = dpr.astype(BF16)
        dpib = dpi.astype(BF16)
        xcb = xc.astype(BF16)
        dwr_ref[...] += _dot_tn(xcb, dprb)
        dwi_ref[...] += _dot_tn(xcb, dpib)
        dxc = dxc + _dot_nt(dprb, bdr_ref[...]) + _dot_nt(dpib, bdi_ref[...])
        nxt = dxcnext[...]
        dxcnext[...] = dxc[0:SUBLANES, :]
        up1, up2, up3 = _shift_up(dxc, nxt, 1), _shift_up(dxc, nxt, 2), _shift_up(dxc, nxt, 3)
        dcb_ref[...] += _colsum(dxc)
        dcw_ref[3:4, :] += _colsum(dxc * lx)
        dcw_ref[2:3, :] += _colsum(up1 * lx)
        dcw_ref[1:2, :] += _colsum(up2 * lx)
        dcw_ref[0:1, :] += _colsum(up3 * lx)
        dlx = cw_ref[3:4, :] * dxc + cw_ref[2:3, :] * up1 + cw_ref[1:2, :] * up2 + cw_ref[0:1, :] * up3
        dz_ref[:, 0:512] = dlx.astype(BF16)

        dz_ref[:, 1024:1536] = (dy_g * spb * st_ref[ST_DU]).astype(BF16)
        dsp = dy_g * u
        vb = (vhat * ng_ref[...] + nb_ref[...]).astype(BF16)
        for n in range(ts // GMLP_BLOCK):
            rs = slice(n * GMLP_BLOCK, (n + 1) * GMLP_BLOCK)
            for g in range(GMLP_GROUPS):
                cs = slice(g * 128, (g + 1) * 128)
                dbs_ref[:, g:g + 1] += jnp.sum(dsp[rs, cs], axis=1, keepdims=True)
                blk = dsp[rs, cs].astype(BF16)
                dws_ref[g] += _dot_nt(blk, vb[rs, cs])
                dv_scr[rs, cs] = _dot(wst_ref[g], blk)
        dv = dv_scr[...]
        dng_ref[...] += _colsum(dv * vhat)
        dnb_ref[...] += _colsum(dv)
        dvh = dv * ng_ref[...]
        dvg = dvh - jnp.mean(dvh, axis=-1, keepdims=True) - vhat * jnp.mean(dvh * vhat, axis=-1, keepdims=True)
        dz_ref[:, 1536:2048] = (dvg * st_ref[ST_Q]).astype(BF16)

        @pl.when(i == nt - 1)
        def _():
            pos = lax.broadcasted_iota(jnp.int32, (GMLP_BLOCK, GMLP_BLOCK), 0) // CHUNK
            src = lax.broadcasted_iota(jnp.int32, (GMLP_BLOCK, GMLP_BLOCK), 1) // CHUNK
            for g in range(GMLP_GROUPS):
                dws_ref[g] = jnp.where(src <= pos, dws_ref[g], 0.0)
            dspa_ref[...] = dspa_ref[...] * (-_sigmoid(-la_ref[...]))

    in_specs = ([_rows(ts, 512, nt), _rows(ts, 512, nt), _halo_prev(ts, 512, SUBLANES, nt),
                 pl.BlockSpec((N_STASH, ts, 512), lambda i: (0, nt - 1 - i, 0)), _rows(ts, 1024, nt),
                 _full((1024, 1024))]
                + _seq_param_specs() + [_full((4, 128, 128)), _full((1, 512)), _full((1, 512))])
    return pl.pallas_call(
        body, grid=(nt,), name="seqmix_bwd",
        in_specs=in_specs,
        out_specs=[_rows(ts, 2048, nt)] + [_full(sh) for sh in small_shapes],
        out_shape=[_sds((s, 2048), BF16)] + [_sds(sh, F32) for sh in small_shapes],
        scratch_shapes=[pltpu.VMEM((1, 512), F32), pltpu.VMEM((SUBLANES, 512), F32),
                        pltpu.VMEM((SUBLANES, 512), F32), pltpu.VMEM((ts, 512), F32)],
        compiler_params=_params(("arbitrary",)),
    )(z, hst, hst, stash, dy, w_out, *seq_params, ws_t, glo, ggo)


def _seqmix_bwd_recomputing_unused(z, hst, dy, w_out, seq_params, ws_t, glo, ggo, ts=256):
    s = z.shape[0]
    nt = s // ts
    small_shapes = [(4, 512), (1, 512), (512, 512), (512, 512), (1, 512), (1, 512), (1, 512),
                    (1, 512), (1, 512), (4, 128, 128), (128, 4), (1, 512), (1, 512)]

    def body(z_ref, zprev_ref, hst_ref, hprev_ref, dy_ref, wout_ref, *rest):
        p = rest[:11]
        wst_ref, glo_ref, ggo_ref = rest[11:14]
        dz_ref = rest[14]
        (dcw_ref, dcb_ref, dwr_ref, dwi_ref, dbr_ref, dbi_ref, dspa_ref, dng_ref, dnb_ref, dws_ref, dbs_ref,
         dglo_ref, dggo_ref) = rest[15:28]
        gcarry, anext, dxcnext, sp_scr, dv_scr = rest[28:]
        i = pl.program_id(0)

        @pl.when(i == 0)
        def _():
            for ref in rest[15:28]:
                ref[...] = jnp.zeros_like(ref)
            gcarry[...] = jnp.zeros_like(gcarry)
            anext[...] = jnp.ones_like(anext)
            dxcnext[...] = jnp.zeros_like(dxcnext)

        first_tile = i == nt - 1
        f = _seq_recompute(z_ref, zprev_ref, first_tile, p)
        xc, r, ig, a, mult, lx = f["xc"], f["r"], f["ig"], f["a"], f["mult"], f["lx"]
        h = hst_ref[...]
        hprev = _shift_down(h, jnp.where(first_tile, 0.0, hprev_ref[...]), 1)
        gl, dgl = _gelu_and_grad(f["lg"])
        y_l = h * gl
        gm = _gmlp_fwd(f["gu"], f["gv"], p[7], p[8], p[9], p[10], sp_scr)
        y_g = gm["y_g"]

        dycat = _dot_nt(dy_ref[...], wout_ref[...])
        rl = _msq_rsqrt(y_l)
        yln = y_l * rl
        dyl = dycat[:, 0:512]
        dglo_ref[...] += _colsum(dyl * yln)
        dy_l = _rms_bwd(dyl * glo_ref[...], yln, rl)
        rg = _msq_rsqrt(y_g)
        ygn = y_g * rg
        dyg = dycat[:, 512:1024]
        dggo_ref[...] += _colsum(dyg * ygn)
        dy_g = _rms_bwd(dyg * ggo_ref[...], ygn, rg)

        dz_ref[:, 512:1024] = (dy_l * h * dgl).astype(BF16)
        a_up = _shift_up(a, anext[...], 1)
        acum, gloc = _scan_bwd(a_up, dy_l * gl)
        gg = gloc + acum * gcarry[...]
        gcarry[...] = gg[0:1, :]
        anext[...] = a[0:SUBLANES, :]
        da = gg * hprev
        t1 = gg * mult
        di = t1 * xc
        dxc = t1 * ig
        dmult = gg * ig * xc
        dla = da * a - dmult * (a * a / mult)
        spa = f["spa"]
        dspa_ref[...] += _colsum(dla * r) * (-LRU_C)
        dpr = dla * ((-LRU_C) * spa) * r * (1.0 - r)
        dpi = di * ig * (1.0 - ig)
        dbr_ref[...] += _colsum(dpr)
        dbi_ref[...] += _colsum(dpi)
        dprb = dpr.astype(BF16)
        dpib = dpi.astype(BF16)
        dwr_ref[...] += _dot_tn(f["xcb"], dprb)
        dwi_ref[...] += _dot_tn(f["xcb"], dpib)
        dxc = dxc + _dot_nt(dprb, p[2][...]) + _dot_nt(dpib, p[3][...])
        dcb_ref[...] += _colsum(dxc)
        dcw_ref[3:4, :] += _colsum(dxc * lx)
        dcw_ref[2:3, :] += _colsum(dxc * f["s1"])
        dcw_ref[1:2, :] += _colsum(dxc * f["s2"])
        dcw_ref[0:1, :] += _colsum(dxc * f["s3"])
        nxt = dxcnext[...]
        dxcnext[...] = dxc[0:SUBLANES, :]
        cw_ref = p[0]
        dlx = (cw_ref[3:4, :] * dxc + cw_ref[2:3, :] * _shift_up(dxc, nxt, 1)
               + cw_ref[1:2, :] * _shift_up(dxc, nxt, 2) + cw_ref[0:1, :] * _shift_up(dxc, nxt, 3))
        dz_ref[:, 0:512] = dlx.astype(BF16)

        dz_ref[:, 1024:1536] = (dy_g * gm["spb"] * gm["du"]).astype(BF16)
        dsp = dy_g * gm["u"]
        vb = gm["vb"]
        for n in range(ts // GMLP_BLOCK):
            rs = slice(n * GMLP_BLOCK, (n + 1) * GMLP_BLOCK)
            for g in range(GMLP_GROUPS):
                cs = slice(g * 128, (g + 1) * 128)
                dbs_ref[:, g:g + 1] += jnp.sum(dsp[rs, cs], axis=1, keepdims=True)
                blk = dsp[rs, cs].astype(BF16)
                dws_ref[g] += _dot_nt(blk, vb[rs, cs])
                dv_scr[rs, cs] = _dot(wst_ref[g], blk)
        dv = dv_scr[...]
        vhat = gm["vhat"]
        dng_ref[...] += _colsum(dv * vhat)
        dnb_ref[...] += _colsum(dv)
        dvh = dv * p[7][...]
        dvg = gm["rstd"] * (dvh - jnp.mean(dvh, axis=-1, keepdims=True)
                            - vhat * jnp.mean(dvh * vhat, axis=-1, keepdims=True))
        dz_ref[:, 1536:2048] = (dvg * gm["dvg"]).astype(BF16)

        @pl.when(i == nt - 1)
        def _():
            pos = lax.broadcasted_iota(jnp.int32, (GMLP_BLOCK, GMLP_BLOCK), 0) // CHUNK
            src = lax.broadcasted_iota(jnp.int32, (GMLP_BLOCK, GMLP_BLOCK), 1) // CHUNK
            for g in range(GMLP_GROUPS):
                dws_ref[g] = jnp.where(src <= pos, dws_ref[g], 0.0)
            dspa_ref[...] = dspa_ref[...] * (-_sigmoid(-p[6][...]))

    in_specs = (_seq_specs(ts, nt, True)
                + [_rows(ts, 512, nt), _halo_prev(ts, 512, SUBLANES, nt), _rows(ts, 1024, nt), _full((1024, 1024))]
                + _seq_param_specs() + [_full((4, 128, 128)), _full((1, 512)), _full((1, 512))])
    return pl.pallas_call(
        body, grid=(nt,), name="seqmix_bwd",
        in_specs=in_specs,
        out_specs=[_rows(ts, 2048, nt)] + [_full(sh) for sh in small_shapes],
        out_shape=[_sds((s, 2048), BF16)] + [_sds(sh, F32) for sh in small_shapes],
        scratch_shapes=[pltpu.VMEM((1, 512), F32), pltpu.VMEM((SUBLANES, 512), F32),
                        pltpu.VMEM((SUBLANES, 512), F32), pltpu.VMEM((ts, 512), F32), pltpu.VMEM((ts, 512), F32)],
        compiler_params=_params(("arbitrary",)),
    )(z, z, hst, hst, dy, w_out, *seq_params, ws_t, glo, ggo)


def _mix_in_bwd(x, dz, dx1, w_in4, g, sc, ts=512):
    s, d = x.shape

    def body(x_ref, dz_ref, dx1_ref, w_ref, g_ref, sc_ref, gx_ref, dsh_ref, dsc_ref, dg_ref):
        i = pl.program_id(0)

        @pl.when(i == 0)
        def _():
            for ref in (dsh_ref, dsc_ref, dg_ref):
                ref[...] = jnp.zeros_like(ref)

        for rs in _sub_tiles(ts):
            dh = jnp.zeros((SUB_ROWS, d), F32)
            for k in range(N_CHIPS):
                dh = dh + _dot_nt(dz_ref[rs, k * 512:(k + 1) * 512], w_ref[k])
            xv = x_ref[rs, :]
            r = _msq_rsqrt(xv)
            xn = xv * r
            dsh_ref[...] += _colsum(dh)
            dsc_ref[...] += _colsum(dh * (xn * g_ref[...]))
            dhn = dh * (1.0 + sc_ref[...])
            dg_ref[...] += _colsum(dhn * xn)
            gx_ref[rs, :] = dx1_ref[rs, :] + _rms_bwd(dhn * g_ref[...], xn, r)

    vec = _full((1, d))
    return pl.pallas_call(
        body, grid=(s // ts,), name="mix_in_bwd",
        in_specs=[_rows(ts, d), _rows(ts, 2048), _rows(ts, d), _full(w_in4.shape), vec, vec],
        out_specs=[_rows(ts, d), vec, vec, vec],
        out_shape=[_sds((s, d), F32)] + [_sds((1, d), F32)] * 3,
        compiler_params=_params(("arbitrary",)),
    )(x, dz, dx1, w_in4, g, sc)


def _wgrad(a, b, n_chunks, name, chunk_major, ts=2048):
    s, m = a.shape
    n = b.shape[1]
    nc = n // n_chunks
    nt = s // ts

    def body(a_ref, b_ref, o_ref, acc):
        i = pl.program_id(1)

        @pl.when(i == 0)
        def _():
            acc[...] = jnp.zeros_like(acc)

        acc[...] += _dot_tn(a_ref[...], b_ref[...])

        @pl.when(i == nt - 1)
        def _():
            if chunk_major:
                o_ref[0] = acc[...].astype(BF16)
            else:
                o_ref[...] = acc[...].astype(BF16)

    if chunk_major:
        out_spec, out_shape = pl.BlockSpec((1, m, nc), lambda c, i: (c, 0, 0)), _sds((n_chunks, m, nc), BF16)
    else:
        out_spec, out_shape = pl.BlockSpec((m, nc), lambda c, i: (0, c)), _sds((m, n), BF16)
    return pl.pallas_call(
        body, grid=(n_chunks, nt), name=name,
        in_specs=[pl.BlockSpec((ts, m), lambda c, i: (i, 0)), pl.BlockSpec((ts, nc), lambda c, i: (i, c))],
        out_specs=out_spec,
        out_shape=out_shape,
        scratch_shapes=[pltpu.VMEM((m, nc), F32)],
        compiler_params=_params(("parallel", "arbitrary")),
    )(a, b)


def _block_diag(w):
    heads, hd, _ = w.shape
    eye = jnp.eye(heads, dtype=w.dtype)
    return (eye[:, None, :, None] * w[:, :, None, :]).reshape(heads * hd, heads * hd)


def _diag_blocks(m):
    hd = LRU_WIDTH // LRU_HEADS
    m4 = m.reshape(LRU_HEADS, hd, LRU_HEADS, hd)
    return jnp.stack([m4[k, :, k, :] for k in range(LRU_HEADS)])


def _seq_params(small):
    row = lambda v: v.reshape(1, -1)
    pos = jnp.arange(GMLP_BLOCK)
    mask = (pos[None, :] // CHUNK) <= (pos[:, None] // CHUNK)
    ws = jnp.where(mask[None], small["w_spatial"], 0.0)
    seq_params = (small["conv_w"], row(small["conv_b"]),
                  _block_diag(small["w_rgate"]).astype(BF16), _block_diag(small["w_igate"]).astype(BF16),
                  row(small["b_rgate"]), row(small["b_igate"]), row(small["lru_a"]),
                  row(small["v_norm_g"]), row(small["v_norm_b"]), ws.astype(BF16), small["b_spatial"].T)
    return seq_params, jnp.swapaxes(ws, 1, 2).astype(BF16)


_ANY = pl.BlockSpec(memory_space=pl.ANY)
_CHIP_FLIPS = ((1, 0), (0, 1), (1, 1))


def _position():
    return lax.axis_index("x"), lax.axis_index("y"), lax.axis_index("c")


def _flip(v, f):
    return 1 - v if f else v


def _remote(src, dst, send_sem, recv_sem, peer):
    return pltpu.make_async_remote_copy(src_ref=src, dst_ref=dst, send_sem=send_sem, recv_sem=recv_sem,
                                        device_id=peer, device_id_type=MESH)


def _allgather8(block, name, reduce):
    r, n = block.shape

    def body(x_ref, out_ref, *scratch):
        if reduce:
            gath, send_sems, recv_sems, loc_sem = scratch
        else:
            gath = out_ref
            send_sems, recv_sems, loc_sem = scratch
        x, y, c = _position()
        me = 4 * x + 2 * y + c
        loc = pltpu.make_async_copy(x_ref, gath.at[me], loc_sem)
        loc.start()
        peers = []
        for k in range(1, N_DEV):
            px, py, pc = _flip(x, k & 4), _flip(y, k & 2), _flip(c, k & 1)
            peers.append((px, py, pc))
            _remote(x_ref, gath.at[me], send_sems.at[k - 1], recv_sems.at[k - 1], (px, py, pc)).start()
        for k, (px, py, pc) in enumerate(peers):
            src = 4 * px + 2 * py + pc
            _remote(x_ref, gath.at[src], send_sems.at[k], recv_sems.at[k], (px, py, pc)).wait_recv()
        for k, peer in enumerate(peers):
            _remote(x_ref, gath.at[me], send_sems.at[k], recv_sems.at[k], peer).wait_send()
        loc.wait()
        if reduce:
            acc = gath[0]
            for k in range(1, N_DEV):
                acc = acc + gath[k]
            out_ref[...] = acc

    sems = [pltpu.SemaphoreType.DMA((N_DEV - 1,)), pltpu.SemaphoreType.DMA((N_DEV - 1,)), pltpu.SemaphoreType.DMA]
    if reduce:
        out_shape = _sds((r, n), F32)
        scratch = [pltpu.VMEM((N_DEV, r, n), F32)] + sems
    else:
        out_shape = _sds((N_DEV, r, n), F32)
        scratch = sems
    return pl.pallas_call(
        body, name=name, out_shape=out_shape,
        in_specs=[pl.BlockSpec(memory_space=pltpu.VMEM)], out_specs=pl.BlockSpec(memory_space=pltpu.VMEM),
        scratch_shapes=scratch,
        compiler_params=pltpu.CompilerParams(vmem_limit_bytes=VMEM_LIMIT_BYTES),
    )(block)


def _half(ref, c, rows):
    hr = rows // 2
    return ref.at[pl.ds(pl.multiple_of(c * hr, BF16_SUBLANES), hr), :]


def _gather_weights(shards):
    na = len(shards)

    def body(*refs):
        ins, outs = refs[:na], refs[na:2 * na]
        ici_send, ici_recv, d2d_send, d2d_recv, loc_sem = refs[2 * na:]
        x, y, c = _position()
        chip = 2 * x + y
        sibling = (x, y, 1 - c)
        local = []
        for a in range(na):
            local.append(pltpu.make_async_copy(ins[a], outs[a].at[chip], loc_sem.at[a]))
            local[-1].start()
        sends = []
        for a in range(na):
            rows = shards[a].shape[0]
            for j, (fx, fy) in enumerate(_CHIP_FLIPS):
                peer = (_flip(x, fx), _flip(y, fy), c)
                sends.append(_remote(_half(ins[a], c, rows), _half(outs[a].at[chip], c, rows),
                                     ici_send.at[a * 3 + j], ici_recv.at[a * 3 + j], peer))
                sends[-1].start()
        for a in range(na):
            rows = shards[a].shape[0]
            for j, (fx, fy) in enumerate(_CHIP_FLIPS):
                src_chip = 2 * _flip(x, fx) + _flip(y, fy)
                landed = _half(outs[a].at[src_chip], c, rows)
                _remote(landed, landed, ici_send.at[a * 3 + j], ici_recv.at[a * 3 + j], sibling).wait_recv()
                sends.append(_remote(landed, landed, d2d_send.at[a * 3 + j], d2d_recv.at[a * 3 + j], sibling))
                sends[-1].start()
        for a in range(na):
            rows = shards[a].shape[0]
            for j, (fx, fy) in enumerate(_CHIP_FLIPS):
                src_chip = 2 * _flip(x, fx) + _flip(y, fy)
                other = _half(outs[a].at[src_chip], 1 - c, rows)
                _remote(other, other, d2d_send.at[a * 3 + j], d2d_recv.at[a * 3 + j], sibling).wait_recv()
        for cp in sends:
            cp.wait_send()
        for cp in local:
            cp.wait()

    return pl.pallas_call(
        body, name="gather_weights",
        out_shape=[_sds((N_CHIPS,) + w.shape, w.dtype) for w in shards],
        in_specs=[_ANY] * na, out_specs=[_ANY] * na,
        scratch_shapes=[pltpu.SemaphoreType.DMA((3 * na,))] * 4 + [pltpu.SemaphoreType.DMA((na,))],
    )(*shards)


def _swap_halves(parts, name):
    na = len(parts)

    def body(*refs):
        ins, outs = refs[:na], refs[na:2 * na]
        send_sems, recv_sems = refs[2 * na:]
        x, y, c = _position()
        sibling = (x, y, 1 - c)
        cps = []
        for a in range(na):
            hr = parts[a].shape[1] // 2
            src = ins[a].at[:, pl.ds(pl.multiple_of((1 - c) * hr, BF16_SUBLANES), hr), :]
            cps.append(_remote(src, outs[a], send_sems.at[a], recv_sems.at[a], sibling))
            cps[-1].start()
        for cp in cps:
            cp.wait()

    return pl.pallas_call(
        body, name=name,
        out_shape=[_sds((N_CHIPS, p.shape[1] // 2, p.shape[2]), p.dtype) for p in parts],
        in_specs=[_ANY] * na, out_specs=[_ANY] * na,
        scratch_shapes=[pltpu.SemaphoreType.DMA((na,))] * 2,
    )(*parts)


def _chip_sum(part, recv, pos_arr, name):
    _, rows, cols = part.shape
    hr = rows // 2

    def body(pos_ref, p_ref, r_ref, o_ref, g_ref):
        total = (p_ref[...].astype(F32) + r_ref[...].astype(F32)).astype(BF16)
        o_ref[...] = total

        @pl.when(pl.program_id(0) == pos_ref[1])
        def _():
            g_ref[0] = total

    grid_spec = pltpu.PrefetchScalarGridSpec(
        num_scalar_prefetch=1, grid=(N_CHIPS,),
        in_specs=[pl.BlockSpec((1, hr, cols), lambda k, pos: (k, pos[0], 0)),
                  pl.BlockSpec((1, hr, cols), lambda k, pos: (k, 0, 0))],
        out_specs=[pl.BlockSpec((1, hr, cols), lambda k, pos: (k, 0, 0)),
                   pl.BlockSpec((1, 1, hr, cols), lambda k, pos: (0, pos[1], 0, 0))])
    return pl.pallas_call(
        body, name=name, grid_spec=grid_spec,
        out_shape=[_sds((N_CHIPS, hr, cols), BF16), _sds((2, N_CHIPS, hr, cols), BF16)],
        compiler_params=_params(("arbitrary",)),
    )(pos_arr, part, recv)


def _exchange_chips(sums):
    na = len(sums)

    def body(*refs):
        ins, outs = refs[:na], refs[na:2 * na]
        send_sems, recv_sems, loc_sem = refs[2 * na:]
        x, y, c = _position()
        chip = 2 * x + y
        local = []
        for a in range(na):
            local.append(pltpu.make_async_copy(ins[a].at[chip], outs[a].at[chip], loc_sem.at[a]))
            local[-1].start()
        cps = []
        for a in range(na):
            for j, (fx, fy) in enumerate(_CHIP_FLIPS):
                px, py = _flip(x, fx), _flip(y, fy)
                cps.append(_remote(ins[a].at[2 * px + py], outs[a].at[chip],
                                   send_sems.at[a * 3 + j], recv_sems.at[a * 3 + j], (px, py, c)))
                cps[-1].start()
        for a in range(na):
            for j, (fx, fy) in enumerate(_CHIP_FLIPS):
                src_chip = 2 * _flip(x, fx) + _flip(y, fy)
                landed = outs[a].at[src_chip]
                _remote(landed, landed, send_sems.at[a * 3 + j], recv_sems.at[a * 3 + j], (x, y, c)).wait_recv()
        for cp in cps:
            cp.wait_send()
        for cp in local:
            cp.wait()

    return pl.pallas_call(
        body, name="exchange_chips",
        out_shape=[_sds(s.shape, s.dtype) for s in sums],
        in_specs=[_ANY] * na, out_specs=[_ANY] * na,
        scratch_shapes=[pltpu.SemaphoreType.DMA((3 * na,))] * 2 + [pltpu.SemaphoreType.DMA((na,))],
    )(*sums)


def _sum_chips(gath, name, tr=128):
    _, hr, cols = gath.shape
    tr = min(tr, hr)

    def body(g_ref, o_ref):
        acc = g_ref[0].astype(F32)
        for k in range(1, N_CHIPS):
            acc = acc + g_ref[k].astype(F32)
        o_ref[...] = acc

    return pl.pallas_call(
        body, name=name, grid=(hr // tr,),
        in_specs=[pl.BlockSpec((N_CHIPS, tr, cols), lambda i: (0, i, 0))],
        out_specs=pl.BlockSpec((tr, cols), lambda i: (i, 0)),
        out_shape=_sds((hr, cols), F32),
        compiler_params=_params(("parallel",)),
    )(gath)


def _join_halves(halves):
    na = len(halves)

    def body(*refs):
        ins, outs = refs[:na], refs[na:2 * na]
        send_sems, recv_sems, loc_sem = refs[2 * na:]
        x, y, c = _position()
        sibling = (x, y, 1 - c)
        cps, local = [], []
        for a in range(na):
            rows = 2 * halves[a].shape[0]
            mine = _half(outs[a], c, rows)
            local.append(pltpu.make_async_copy(ins[a], mine, loc_sem.at[a]))
            local[-1].start()
            cps.append(_remote(ins[a], mine, send_sems.at[a], recv_sems.at[a], sibling))
            cps[-1].start()
        for a in range(na):
            rows = 2 * halves[a].shape[0]
            other = _half(outs[a], 1 - c, rows)
            _remote(ins[a], other, send_sems.at[a], recv_sems.at[a], sibling).wait_recv()
        for cp in cps:
            cp.wait_send()
        for cp in local:
            cp.wait()

    return pl.pallas_call(
        body, name="join_halves",
        out_shape=[_sds((2 * h.shape[0], h.shape[1]), h.dtype) for h in halves],
        in_specs=[_ANY] * na, out_specs=[_ANY] * na,
        scratch_shapes=[pltpu.SemaphoreType.DMA((na,))] * 3,
    )(*halves)


_HBM = pl.BlockSpec(memory_space=pltpu.HBM)
_SEM = pl.BlockSpec(memory_space=pltpu.SEMAPHORE)
_EFFECT = pltpu.SideEffectType.DATAFLOW_SIDE_EFFECTING


def _in_hbm(a):
    return pltpu.with_memory_space_constraint(a, pltpu.HBM)


def _split_start(srcs, lands, plan, n_copies, after, name):
    ns, nl = len(srcs), len(lands)
    bufs = list(srcs) + list(lands)

    def body(*refs):
        send_sems, recv_sems = refs[ns + nl + 1], refs[ns + nl + 2]
        token = refs[-1]
        for k, (src, dst, peer) in enumerate(plan(refs[:ns], refs[ns:ns + nl])):
            _remote(src, dst, send_sems.at[k], recv_sems.at[k], peer).start()
        token[...] = jnp.zeros_like(token)

    out = pl.pallas_call(
        body, name=name,
        out_shape=(pltpu.SemaphoreType.DMA((n_copies,)), pltpu.SemaphoreType.DMA((n_copies,)),
                   *[pltpu.HBM(b.shape, b.dtype) for b in bufs], _sds((SUBLANES, 128), F32)),
        in_specs=[_HBM] * (ns + nl) + [_ANY],
        out_specs=(_SEM, _SEM, *[_HBM] * (ns + nl), pl.BlockSpec(memory_space=pltpu.VMEM)),
        input_output_aliases={i: 2 + i for i in range(ns + nl)},
        compiler_params=pltpu.CompilerParams(has_side_effects=_EFFECT),
    )(*[_in_hbm(b) for b in bufs], after)
    return out[0], out[1], list(out[2:2 + ns]), list(out[2 + ns:2 + ns + nl]), out[-1]


def _split_wait(send_sems, recv_sems, srcs, lands, plan, after, name):
    ns, nl = len(srcs), len(lands)
    bufs = list(srcs) + list(lands)

    def body(*refs):
        send_ref, recv_ref = refs[ns + nl], refs[ns + nl + 1]
        me = _position()
        for k, src, dst in plan(refs[:ns], refs[ns:ns + nl]):
            cp = _remote(src, dst, send_ref.at[k], recv_ref.at[k], me)
            cp.wait_send()
            cp.wait_recv()

    out = pl.pallas_call(
        body, name=name,
        out_shape=[pltpu.HBM(b.shape, b.dtype) for b in bufs],
        in_specs=[_HBM] * (ns + nl) + [_SEM, _SEM, _ANY],
        out_specs=[_HBM] * (ns + nl),
        input_output_aliases={i: i for i in range(ns + nl)},
        compiler_params=pltpu.CompilerParams(has_side_effects=_EFFECT),
    )(*bufs, send_sems, recv_sems, after)
    return list(out[:ns]), list(out[ns:])


def _gather_plan(rows_of):
    def start(src_refs, land_refs):
        x, y, c = _position()
        chip = 2 * x + y
        out = []
        for a, rows in enumerate(rows_of):
            mine = _half(land_refs[a].at[chip], c, rows)
            out.extend((mine, mine, (_flip(x, fx), _flip(y, fy), c)) for fx, fy in _CHIP_FLIPS)
        return out

    def wait(src_refs, land_refs):
        x, y, c = _position()
        chip = 2 * x + y
        out = []
        for a, rows in enumerate(rows_of):
            for j, (fx, fy) in enumerate(_CHIP_FLIPS):
                src_chip = 2 * _flip(x, fx) + _flip(y, fy)
                out.append((3 * a + j, _half(land_refs[a].at[chip], c, rows),
                            _half(land_refs[a].at[src_chip], c, rows)))
        return out

    return start, wait


def _forward_plan(rows_of):
    def pieces(land_refs, half):
        x, y, _ = _position()
        return [_half(land_refs[a].at[2 * _flip(x, fx) + _flip(y, fy)], half, rows)
                for a, rows in enumerate(rows_of) for fx, fy in _CHIP_FLIPS]

    def start(src_refs, land_refs):
        x, y, c = _position()
        return [(p, p, (x, y, 1 - c)) for p in pieces(land_refs, c)]

    def wait(src_refs, land_refs):
        _, _, c = _position()
        return [(k, mine, theirs)
                for k, (mine, theirs) in enumerate(zip(pieces(land_refs, c), pieces(land_refs, 1 - c)))]

    return start, wait


def _swap_halves_plan(half_rows):
    def slices(src_refs, c):
        return [src_refs[a].at[:, pl.ds(pl.multiple_of((1 - c) * hr, BF16_SUBLANES), hr), :]
                for a, hr in enumerate(half_rows)]

    def start(src_refs, land_refs):
        x, y, c = _position()
        return [(src, land_refs[a], (x, y, 1 - c)) for a, src in enumerate(slices(src_refs, c))]

    def wait(src_refs, land_refs):
        _, _, c = _position()
        return [(a, src, land_refs[a]) for a, src in enumerate(slices(src_refs, c))]

    return start, wait


def _swap_gathered_plan(n_arrays):
    def start(src_refs, land_refs):
        x, y, c = _position()
        return [(land_refs[a].at[0], land_refs[a].at[1], (x, y, 1 - c)) for a in range(n_arrays)]

    def wait(src_refs, land_refs):
        return [(a, land_refs[a].at[0], land_refs[a].at[1]) for a in range(n_arrays)]

    return start, wait


def _exchange_plan(n_arrays):
    def start(src_refs, land_refs):
        x, y, c = _position()
        chip = 2 * x + y
        out = []
        for a in range(n_arrays):
            for fx, fy in _CHIP_FLIPS:
                px, py = _flip(x, fx), _flip(y, fy)
                out.append((src_refs[a].at[2 * px + py], land_refs[a].at[0, chip], (px, py, c)))
        return out

    def wait(src_refs, land_refs):
        x, y, c = _position()
        out = []
        for a in range(n_arrays):
            for j, (fx, fy) in enumerate(_CHIP_FLIPS):
                src_chip = 2 * _flip(x, fx) + _flip(y, fy)
                out.append((3 * a + j, src_refs[a].at[src_chip], land_refs[a].at[0, src_chip]))
        return out

    return start, wait


def _forward_to_sibling(lands, name):
    na = len(lands)

    def body(*refs):
        land_refs = refs[na:2 * na]
        send_sems, recv_sems = refs[2 * na:]
        x, y, c = _position()
        sibling = (x, y, 1 - c)
        sends = []
        for a in range(na):
            rows = lands[a].shape[1]
            for j, (fx, fy) in enumerate(_CHIP_FLIPS):
                landed = _half(land_refs[a].at[2 * _flip(x, fx) + _flip(y, fy)], c, rows)
                sends.append(_remote(landed, landed, send_sems.at[3 * a + j], recv_sems.at[3 * a + j], sibling))
                sends[-1].start()
        for a in range(na):
            rows = lands[a].shape[1]
            for j, (fx, fy) in enumerate(_CHIP_FLIPS):
                other = _half(land_refs[a].at[2 * _flip(x, fx) + _flip(y, fy)], 1 - c, rows)
                _remote(other, other, send_sems.at[3 * a + j], recv_sems.at[3 * a + j], sibling).wait_recv()
        for cp in sends:
            cp.wait_send()

    return pl.pallas_call(
        body, name=name,
        out_shape=[_sds(l.shape, l.dtype) for l in lands],
        in_specs=[_ANY] * na, out_specs=[_ANY] * na,
        input_output_aliases={a: a for a in range(na)},
        scratch_shapes=[pltpu.SemaphoreType.DMA((3 * na,))] * 2,
    )(*lands)


def _swap_gathered(gath, name):
    na = len(gath)

    def body(*refs):
        gath_refs = refs[na:2 * na]
        send_sems, recv_sems = refs[2 * na:]
        x, y, c = _position()
        cps = [_remote(gath_refs[a].at[0], gath_refs[a].at[1], send_sems.at[a], recv_sems.at[a], (x, y, 1 - c))
               for a in range(na)]
        for cp in cps:
            cp.start()
        for cp in cps:
            cp.wait()

    return pl.pallas_call(
        body, name=name,
        out_shape=[_sds(g.shape, g.dtype) for g in gath],
        in_specs=[_ANY] * na, out_specs=[_ANY] * na,
        input_output_aliases={a: a for a in range(na)},
        scratch_shapes=[pltpu.SemaphoreType.DMA((na,))] * 2,
    )(*gath)


def _adam_gathered(w, gath, m, v, c_arr, name, tr=128):
    rows, cols = w.shape
    hr = rows // 2
    if hr % (2 * tr) == 0:
        tr = 2 * tr
    per = hr // tr

    def body(c_ref, w_ref, g_ref, m_ref, v_ref, go_ref, d_ref, nm_ref, nv_ref):
        g = g_ref[0, 0].astype(F32)
        for k in range(1, N_CHIPS):
            g = g + g_ref[0, k].astype(F32)
        go_ref[...] = g
        d_ref[...], nm_ref[...], nv_ref[...] = _adam_math(w_ref[...], g, m_ref[...], v_ref[...])

    def rows_of(h, i, c_ref):
        c = c_ref[0]
        return ((c + h - 2 * c * h) * per + i, 0)

    blk = pl.BlockSpec((tr, cols), rows_of)
    grid_spec = pltpu.PrefetchScalarGridSpec(
        num_scalar_prefetch=1, grid=(2, per),
        in_specs=[blk, pl.BlockSpec((1, N_CHIPS, tr, cols), lambda h, i, c_ref: (h, 0, i, 0)), blk, blk],
        out_specs=[blk] * 4)
    return pl.pallas_call(
        body, name=name, grid_spec=grid_spec, out_shape=[_sds(w.shape, F32)] * 4,
        compiler_params=_params(("arbitrary", "arbitrary")),
    )(c_arr, w, gath, m, v)


def _allreduce_small(block, name):
    r, n = block.shape
    hr = r // 2

    def body(x_ref, out_ref, sib, chipsum, gath, d2d_send, d2d_recv, ici_send, ici_recv):
        x, y, c = _position()
        chip = 2 * x + y
        sibling = (x, y, 1 - c)
        first = _remote(x_ref, sib, d2d_send.at[0], d2d_recv.at[0], sibling)
        first.start()
        first.wait()
        chipsum[...] = x_ref[...] + sib[...]
        mine = pl.ds(c * hr, hr)
        theirs = pl.ds((1 - c) * hr, hr)
        sends = []
        for j, (fx, fy) in enumerate(_CHIP_FLIPS):
            sends.append(_remote(chipsum.at[mine, :], gath.at[chip], ici_send.at[j], ici_recv.at[j],
                                 (_flip(x, fx), _flip(y, fy), c)))
            sends[-1].start()
        gath[chip] = chipsum[mine, :]
        for j, (fx, fy) in enumerate(_CHIP_FLIPS):
            landed = gath.at[2 * _flip(x, fx) + _flip(y, fy)]
            _remote(landed, landed, ici_send.at[j], ici_recv.at[j], sibling).wait_recv()
        for cp in sends:
            cp.wait_send()
        total = gath[0]
        for k in range(1, N_CHIPS):
            total = total + gath[k]
        out_ref[mine, :] = total
        last = _remote(out_ref.at[mine, :], out_ref.at[mine, :], d2d_send.at[1], d2d_recv.at[1], sibling)
        last.start()
        _remote(out_ref.at[theirs, :], out_ref.at[theirs, :], d2d_send.at[1], d2d_recv.at[1], sibling).wait_recv()
        last.wait_send()

    vmem = pl.BlockSpec(memory_space=pltpu.VMEM)
    return pl.pallas_call(
        body, name=name, out_shape=_sds((r, n), F32), in_specs=[vmem], out_specs=vmem,
        scratch_shapes=[pltpu.VMEM((r, n), F32), pltpu.VMEM((r, n), F32), pltpu.VMEM((N_CHIPS, hr, n), F32),
                        pltpu.SemaphoreType.DMA((2,)), pltpu.SemaphoreType.DMA((2,)),
                        pltpu.SemaphoreType.DMA((3,)), pltpu.SemaphoreType.DMA((3,))],
        compiler_params=pltpu.CompilerParams(vmem_limit_bytes=VMEM_LIMIT_BYTES),
    )(block)


def _cast_place(shards, chip_arr):
    na = len(shards)
    steps = 4

    def body(chip_ref, *refs):
        for a in range(na):
            refs[na + a][0] = refs[a][...].astype(BF16)

    grid_spec = pltpu.PrefetchScalarGridSpec(
        num_scalar_prefetch=1, grid=(steps,),
        in_specs=[pl.BlockSpec((s.shape[0] // steps, s.shape[1]), lambda i, ch: (i, 0)) for s in shards],
        out_specs=[pl.BlockSpec((1, s.shape[0] // steps, s.shape[1]), lambda i, ch: (ch[0], i, 0)) for s in shards])
    return pl.pallas_call(
        body, name="cast_place", grid_spec=grid_spec,
        out_shape=[_sds((N_CHIPS,) + s.shape, BF16) for s in shards],
        compiler_params=_params(("arbitrary",)),
    )(chip_arr, *shards)


def _silu(v):
    return v * _sigmoid(v)


def _ada_fwd(c8, w_ada):
    def body(c_ref, w_ref, o_ref):
        o_ref[...] = jnp.dot(_silu(c_ref[...]), w_ref[...], preferred_element_type=F32,
                             precision=lax.Precision.HIGHEST)

    return pl.pallas_call(
        body, name="ada_fwd", out_shape=_sds((N_DEV, w_ada.shape[1]), F32),
        compiler_params=pltpu.CompilerParams(vmem_limit_bytes=VMEM_LIMIT_BYTES),
    )(c8, w_ada)


def _mod_select(parts, b_ada, me_arr, after):
    cols = parts.shape[2]

    def body(me_ref, p_ref, b_ref, after_ref, o_ref):
        me = me_ref[0]
        for k in range(N_CHIPS):
            cs = slice(k * cols, (k + 1) * cols)
            o_ref[:, cs] = p_ref[2 * k, pl.ds(me, 1), :] + b_ref[:, cs]

    grid_spec = pltpu.PrefetchScalarGridSpec(
        num_scalar_prefetch=1, grid=(1,),
        in_specs=[pl.BlockSpec(parts.shape, lambda i, m: (0, 0, 0)), pl.BlockSpec(b_ada.shape, lambda i, m: (0, 0)),
                  _ANY],
        out_specs=pl.BlockSpec(b_ada.shape, lambda i, m: (0, 0)))
    return pl.pallas_call(body, name="mod_select", grid_spec=grid_spec, out_shape=_sds(b_ada.shape, F32))(
        me_arr, parts, b_ada, after)


def _ada_bwd(c8, dmod8, chip_arr, w, m, v, tr=256):
    d = c8.shape[1]
    cols = dmod8.shape[1] // N_CHIPS

    def body(chip_ref, c_ref, dm_ref, dmall_ref, w_ref, m_ref, v_ref, gw_ref, d_ref, nm_ref, nv_ref, gb_ref):
        g = lax.dot_general(_silu(c_ref[...]), dm_ref[...], (((0,), (0,)), ((), ())),
                            preferred_element_type=F32, precision=lax.Precision.HIGHEST)
        gw_ref[...] = g
        d_ref[...], nm_ref[...], nv_ref[...] = _adam_math(w_ref[...], g, m_ref[...], v_ref[...])
        acc = dmall_ref[0:1, :]
        for k in range(1, N_DEV):
            acc = acc + dmall_ref[k:k + 1, :]
        gb_ref[...] = acc

    rows = pl.BlockSpec((tr, cols), lambda i, ch: (i, 0))
    grid_spec = pltpu.PrefetchScalarGridSpec(
        num_scalar_prefetch=1, grid=(d // tr,),
        in_specs=[pl.BlockSpec((N_DEV, tr), lambda i, ch: (0, i)),
                  pl.BlockSpec((N_DEV, cols), lambda i, ch: (0, ch[0])),
                  pl.BlockSpec(dmod8.shape, lambda i, ch: (0, 0)), rows, rows, rows],
        out_specs=[rows] * 4 + [pl.BlockSpec((1, dmod8.shape[1]), lambda i, ch: (0, 0))])
    return pl.pallas_call(
        body, name="ada_bwd", grid_spec=grid_spec,
        out_shape=[_sds((d, cols), F32)] * 4 + [_sds((1, dmod8.shape[1]), F32)],
        compiler_params=_params(("arbitrary",)),
    )(chip_arr, c8, dmod8, dmod8, w, m, v)


def _adam_math(w, g, m, v):
    m = ADAM_B1 * m + (1.0 - ADAM_B1) * g
    v = ADAM_B2 * v + (1.0 - ADAM_B2) * (g * g)
    m_hat = m / (1.0 - ADAM_B1 ** ADAM_STEP)
    v_hat = v / (1.0 - ADAM_B2 ** ADAM_STEP)
    delta = -ADAM_LR * (m_hat / (jnp.sqrt(v_hat) + ADAM_EPS) + ADAM_WD * w)
    return delta, m, v


def _adam(w, g, m, v, name, tr=256):
    rows, cols = w.shape
    if rows % tr:
        tr = rows

    def body(w_ref, g_ref, m_ref, v_ref, d_ref, nm_ref, nv_ref):
        d_ref[...], nm_ref[...], nv_ref[...] = _adam_math(w_ref[...], g_ref[...], m_ref[...], v_ref[...])

    spec = pl.BlockSpec((tr, cols), lambda i: (i, 0))
    return pl.pallas_call(
        body, name=name, grid=(rows // tr,), in_specs=[spec] * 4, out_specs=[spec] * 3,
        out_shape=[_sds(w.shape, F32)] * 3, compiler_params=_params(("parallel",)),
    )(w, g, m, v)


def _adam_cols(w, g_full, m, v, chip_arr, name):
    rows, cols = w.shape

    def body(chip_ref, w_ref, g_ref, m_ref, v_ref, gs_ref, d_ref, nm_ref, nv_ref):
        g = g_ref[...]
        gs_ref[...] = g
        d_ref[...], nm_ref[...], nv_ref[...] = _adam_math(w_ref[...], g, m_ref[...], v_ref[...])

    own = pl.BlockSpec((rows, cols), lambda i, ch: (0, 0))
    grid_spec = pltpu.PrefetchScalarGridSpec(
        num_scalar_prefetch=1, grid=(1,),
        in_specs=[own, pl.BlockSpec((rows, cols), lambda i, ch: (0, ch[0])), own, own],
        out_specs=[own] * 4)
    return pl.pallas_call(body, name=name, grid_spec=grid_spec, out_shape=[_sds(w.shape, F32)] * 4)(
        chip_arr, w, g_full, m, v)


PACK_COLS = 512
SMALL_REPLICATED = ("g_mix_pre", "g_mix_post", "conv_b", "w_rgate", "b_rgate", "w_igate", "b_igate", "lru_a",
                    "v_norm_g", "v_norm_b", "w_spatial", "b_spatial", "g_lru_out", "g_gmlp_out", "g_ffn_pre",
                    "g_ffn_post", "ffn_conv_b")
SMALL_COLUMN_SHARDED = ("conv_w", "ffn_conv_w")


def _pack(arrays):
    flat = jnp.concatenate([a.reshape(1, -1) for a in arrays], axis=1)
    pad = (-flat.shape[1]) % (2 * LANES)
    if pad:
        flat = jnp.pad(flat, ((0, 0), (0, pad)))
    return flat.reshape(2, -1)


def _unpack(packed, shapes):
    flat = packed.reshape(1, -1)
    out, col = [], 0
    for shape in shapes:
        n = math.prod(shape)
        out.append(flat[:, col:col + n].reshape(shape))
        col += n
    return out


SMALL_ROW_LEN = 86016
_SMALL_ROWS = (
    (("ffn_conv_w", 18432), ("conv_w", 2048), ("w_spatial", 65536)),
    (("w_rgate", 32768), ("w_igate", 32768), ("ffn_conv_b", 6144), ("g_mix_pre", 1024), ("g_mix_post", 1024),
     ("g_ffn_pre", 1024), ("g_ffn_post", 1024), ("conv_b", 512), ("b_rgate", 512), ("b_igate", 512),
     ("lru_a", 512), ("v_norm_g", 512), ("v_norm_b", 512), ("b_spatial", 512), ("g_lru_out", 512),
     ("g_gmlp_out", 512), ("loss", 128)),
)


def _small_slots():
    slots = {}
    for row, entries in enumerate(_SMALL_ROWS):
        off = 0
        for name, size in entries:
            slots[name] = (row, off)
            off += size
        assert off <= SMALL_ROW_LEN
    return slots


SMALL_SLOT = _small_slots()
ROW_VECTORS = ("ffn_conv_b", "g_mix_pre", "g_mix_post", "g_ffn_pre", "g_ffn_post", "conv_b", "lru_a", "v_norm_g",
               "v_norm_b", "g_lru_out", "g_gmlp_out")
HEAD_DIM = LRU_WIDTH // LRU_HEADS


def _pack_small(g):
    order = ("ffn_conv_w", "conv_w", "w_spatial", "w_rgate", "w_igate", "b_rgate", "b_igate", "b_spatial", "loss") \
        + ROW_VECTORS
    vmem = pl.BlockSpec(memory_space=pltpu.VMEM)

    def body(*refs):
        src = dict(zip(order, refs))
        out_ref = refs[len(order)]
        out_ref[...] = jnp.zeros_like(out_ref)

        def put(name, lane, val):
            row, off = SMALL_SLOT[name]
            out_ref[row:row + 1, off + lane:off + lane + val.shape[1]] = val

        for name in ROW_VECTORS + ("b_rgate", "b_igate", "loss"):
            put(name, 0, src[name][...])
        for name in ("ffn_conv_w", "conv_w"):
            k_taps, n = src[name].shape
            for k in range(k_taps):
                put(name, k * n, src[name][k:k + 1, :])
        for g_idx in range(GMLP_GROUPS):
            for i in range(GMLP_BLOCK):
                put("w_spatial", (g_idx * GMLP_BLOCK + i) * GMLP_BLOCK, src["w_spatial"][g_idx, i:i + 1, :])
        for name in ("w_rgate", "w_igate"):
            for h in range(LRU_HEADS):
                for i in range(HEAD_DIM):
                    r = h * HEAD_DIM + i
                    put(name, r * HEAD_DIM, src[name][r:r + 1, h * HEAD_DIM:(h + 1) * HEAD_DIM])
        eye = (lax.broadcasted_iota(jnp.int32, (GMLP_BLOCK, GMLP_BLOCK), 0)
               == lax.broadcasted_iota(jnp.int32, (GMLP_BLOCK, GMLP_BLOCK), 1))
        for g_idx in range(GMLP_GROUPS):
            col = src["b_spatial"][:, g_idx:g_idx + 1]
            put("b_spatial", g_idx * GMLP_BLOCK, _colsum(jnp.where(eye, col, 0.0)))

    return pl.pallas_call(
        body, name="pack_small", out_shape=_sds((2, SMALL_ROW_LEN), F32),
        in_specs=[vmem] * len(order), out_specs=vmem,
        compiler_params=pltpu.CompilerParams(vmem_limit_bytes=VMEM_LIMIT_BYTES),
    )(*[g[n] for n in order])


def _adam_small(g_small, w, m, v):
    vmem = pl.BlockSpec(memory_space=pltpu.VMEM)
    n_p = len(SMALL_REPLICATED)

    def body(g_ref, *refs):
        w_refs, m_refs, v_refs = refs[:n_p], refs[n_p:2 * n_p], refs[2 * n_p:3 * n_p]
        outs = refs[3 * n_p:]
        go, do, mo, vo = outs[:n_p], outs[n_p:2 * n_p], outs[2 * n_p:3 * n_p], outs[3 * n_p:]
        for k, name in enumerate(SMALL_REPLICATED):
            row, off = SMALL_SLOT[name]

            def take(lane, n, row=row, off=off):
                return g_ref[row:row + 1, off + lane:off + lane + n]

            shape = w_refs[k].shape
            if name in ROW_VECTORS:
                go[k][...] = take(0, shape[1])
            elif name in ("b_rgate", "b_igate"):
                for h in range(LRU_HEADS):
                    go[k][0, h:h + 1, :] = take(h * HEAD_DIM, HEAD_DIM)
            elif name == "b_spatial":
                for g_idx in range(GMLP_GROUPS):
                    go[k][0, g_idx:g_idx + 1, :] = take(g_idx * GMLP_BLOCK, GMLP_BLOCK)
            elif name == "w_spatial":
                for g_idx in range(GMLP_GROUPS):
                    for i in range(GMLP_BLOCK):
                        go[k][0, g_idx, i:i + 1, :] = take((g_idx * GMLP_BLOCK + i) * GMLP_BLOCK, GMLP_BLOCK)
            else:
                for h in range(LRU_HEADS):
                    for i in range(HEAD_DIM):
                        go[k][0, h, i:i + 1, :] = take((h * HEAD_DIM + i) * HEAD_DIM, HEAD_DIM)
            do[k][...], mo[k][...], vo[k][...] = _adam_math(w_refs[k][...], go[k][...], m_refs[k][...],
                                                             v_refs[k][...])

    names = SMALL_REPLICATED
    out_shape = [_sds(w[n].shape, F32) for n in names] * 4
    res = pl.pallas_call(
        body, name="adam_small", out_shape=out_shape,
        in_specs=[vmem] * (1 + 3 * n_p), out_specs=[vmem] * (4 * n_p),
        compiler_params=pltpu.CompilerParams(vmem_limit_bytes=VMEM_LIMIT_BYTES),
    )(g_small, *[w[n] for n in names], *[m[n] for n in names], *[v[n] for n in names])
    return [dict(zip(names, res[k * n_p:(k + 1) * n_p])) for k in range(4)]


def _adam_cols(name, g_small, w, m, v, chip_arr):
    _, k_taps, n = w.shape
    row, off = SMALL_SLOT[name]
    first = off // n

    def body(chip_ref, *refs):
        g_refs = refs[:k_taps]
        w_ref, m_ref, v_ref, go_ref, d_ref, nm_ref, nv_ref = refs[k_taps:]
        for k in range(k_taps):
            tap = (0, slice(k, k + 1), slice(None))
            g = g_refs[k][row:row + 1, :]
            go_ref[tap] = g
            d_ref[tap], nm_ref[tap], nv_ref[tap] = _adam_math(w_ref[tap], g, m_ref[tap], v_ref[tap])

    whole = pl.BlockSpec(w.shape, lambda i, ch: (0, 0, 0))
    taps = [pl.BlockSpec((2, n), functools.partial(lambda i, ch, k: (0, first + N_CHIPS * k + ch[0]), k=k))
            for k in range(k_taps)]
    grid_spec = pltpu.PrefetchScalarGridSpec(
        num_scalar_prefetch=1, grid=(1,), in_specs=taps + [whole] * 3, out_specs=[whole] * 4)
    return pl.pallas_call(body, name="adam_" + name, grid_spec=grid_spec, out_shape=[_sds(w.shape, F32)] * 4)(
        chip_arr, *[g_small] * k_taps, w, m, v)


def kernel(x, c, w_ada, b_ada, g_mix_pre, g_mix_post, w_in, conv_w, conv_b, w_rgate, b_rgate, w_igate, b_igate, lru_a, v_norm_g, v_norm_b, w_spatial, b_spatial, g_lru_out, g_gmlp_out, w_out, g_ffn_pre, g_ffn_post, w_up, ffn_conv_w, ffn_conv_b, w_down, loss_target, m_w_ada, m_b_ada, m_g_mix_pre, m_g_mix_post, m_w_in, m_conv_w, m_conv_b, m_w_rgate, m_b_rgate, m_w_igate, m_b_igate, m_lru_a, m_v_norm_g, m_v_norm_b, m_w_spatial, m_b_spatial, m_g_lru_out, m_g_gmlp_out, m_w_out, m_g_ffn_pre, m_g_ffn_post, m_w_up, m_ffn_conv_w, m_ffn_conv_b, m_w_down, v_w_ada, v_b_ada, v_g_mix_pre, v_g_mix_post, v_w_in, v_conv_w, v_conv_b, v_w_rgate, v_b_rgate, v_w_igate, v_b_igate, v_lru_a, v_v_norm_g, v_v_norm_b, v_w_spatial, v_b_spatial, v_g_lru_out, v_g_gmlp_out, v_w_out, v_g_ffn_pre, v_g_ffn_post, v_w_up, v_ffn_conv_w, v_ffn_conv_b, v_w_down):
    args = dict(locals())
    names = ("w_ada", "b_ada", "g_mix_pre", "g_mix_post", "w_in", "conv_w", "conv_b", "w_rgate", "b_rgate",
             "w_igate", "b_igate", "lru_a", "v_norm_g", "v_norm_b", "w_spatial", "b_spatial", "g_lru_out",
             "g_gmlp_out", "w_out", "g_ffn_pre", "g_ffn_post", "w_up", "ffn_conv_w", "ffn_conv_b", "w_down")
    drop = lambda a: a if a.ndim == 2 else a[0]
    w = {n: drop(args[n]) for n in names}
    m = {n: drop(args["m_" + n]) for n in names}
    v = {n: drop(args["v_" + n]) for n in names}
    xi, yi, ci = _position()
    me_arr = jnp.reshape(4 * xi + 2 * yi + ci, (1,)).astype(jnp.int32)
    chip_arr = jnp.reshape(2 * xi + yi, (1,)).astype(jnp.int32)
    c_arr = jnp.reshape(ci, (1,)).astype(jnp.int32)
    pos_arr = jnp.stack([ci, 2 * xi + yi]).astype(jnp.int32)

    big = ("w_in", "w_out", "w_up", "w_down")
    lands = _cast_place([w[n] for n in big], chip_arr)
    start_a, wait_a = _gather_plan([w[n].shape[0] for n in big[:2]])
    start_b, wait_b = _gather_plan([w[n].shape[0] for n in big[2:]])

    row0 = jnp.concatenate([c, w["conv_w"].reshape(1, -1), w["ffn_conv_w"].reshape(1, -1)], axis=1)
    g0 = _allgather8(row0, "gather_cond", False)[:, 0, :]
    c8 = g0[:, :D_MODEL]
    per_chip = g0[0::2]
    conv_w_full = per_chip[:, D_MODEL:D_MODEL + 512].reshape(N_CHIPS, 4, 128).transpose(1, 0, 2).reshape(4, 512)
    ffn_conv_w_full = per_chip[:, D_MODEL + 512:].reshape(N_CHIPS, 3, 1536).transpose(1, 0, 2).reshape(3, 2 * D_FF)
    mod_parts = _allgather8(_ada_fwd(c8, w["w_ada"]), "gather_mod", False)
    send_a, recv_a, _, lands_a, token_a = _split_start([], lands[:2], start_a, 6, mod_parts, "gather_start_a")
    send_b, recv_b, _, lands_b, token_b = _split_start([], lands[2:], start_b, 6, token_a, "gather_start_b")
    mod = _mod_select(mod_parts, w["b_ada"].reshape(1, -1), me_arr, token_b).reshape(N_MOD, D_MODEL)
    sh_m, sc_m, gt_m, sh_f, sc_f, gt_f = [mod[k:k + 1] for k in range(N_MOD)]

    small = {n: w[n] for n in SMALL_REPLICATED}
    small["conv_w"] = conv_w_full
    small["ffn_conv_w"] = ffn_conv_w_full
    row = lambda a: a.reshape(1, -1)
    seq_params, ws_t = _seq_params(small)
    glo, ggo = row(small["g_lru_out"]), row(small["g_gmlp_out"])
    g_pre, g_post = row(small["g_mix_pre"]), row(small["g_mix_post"])
    g_pre2, g_post2 = row(small["g_ffn_pre"]), row(small["g_ffn_post"])
    fw, fb = small["ffn_conv_w"], row(small["ffn_conv_b"])
    xs, tgt = x[0], loss_target[0]

    _, lands_a = _split_wait(send_a, recv_a, [], lands_a, wait_a, mod, "gather_wait_a")
    w_in4, w_out4 = _forward_to_sibling(lands_a, "forward_a")
    w_out_b = w_out4.reshape(D_MODEL, D_MODEL)
    z, h = _mix_in(xs, sc_m, sh_m, g_pre, w_in4)
    ycat, hst, stash = _seqmix(z, seq_params, glo, ggo)
    _, lands_b = _split_wait(send_b, recv_b, [], lands_b, wait_b, ycat, "gather_wait_b")
    fwd_start, fwd_wait = _forward_plan([w[n].shape[0] for n in big[2:]])
    fwd_send, fwd_recv, _, lands_b, tok = _split_start([], lands_b, fwd_start, 6, pos_arr, "forward_start_b")
    y, x1, h2 = _mix_out(ycat, xs, w_out_b, gt_m + tok[0:1, 0:1], g_post, g_pre2, sc_f, sh_f)
    _, (w_up4, w_down4) = _split_wait(fwd_send, fwd_recv, [], lands_b, fwd_wait, h2, "forward_wait_b")
    w_down_b = w_down4.reshape(D_FF, D_MODEL)
    up0, pre, act, dy2, dx2, loss, dgt_f, dg_post2 = _ffn_fwd(h2, x1, tgt, w_up4, w_down_b, fw, fb, gt_f, g_post2)

    dup0, dfw, dfb = _ffn_bwd_a(dy2, pre, up0, w_down_b, fw)
    gw_up = _wgrad(h2, dup0, N_CHIPS, "wgrad_up", True)
    gw_down = _wgrad(act, dy2, 2, "wgrad_down", False)
    ex_start, ex_wait = _exchange_plan(2)
    sg_start, sg_wait = _swap_gathered_plan(2)
    grads, deltas, new_m, new_v = {}, {}, {}, {}

    def swap_start(parts, name):
        sw_start, sw_wait = _swap_halves_plan([p.shape[1] // 2 for p in parts])
        recv = [lax.empty((N_CHIPS, p.shape[1] // 2, p.shape[2]), BF16) for p in parts]
        send_s, recv_s, parts, recv, token = _split_start(parts, recv, sw_start, len(parts), pos_arr,
                                                           "swap_start_" + name)
        return (send_s, recv_s, parts, recv, sw_wait), token

    def exchange_start(swap, tags, after, name):
        send_s, recv_s, parts, recv, sw_wait = swap
        parts, recv = _split_wait(send_s, recv_s, parts, recv, sw_wait, after, "swap_wait_" + name)
        both = [_chip_sum(p, r, pos_arr, "chip_sum_" + t) for p, r, t in zip(parts, recv, tags)]
        sums, gath = [b[0] for b in both], [b[1] for b in both]
        return _split_start(sums, gath, ex_start, 3 * len(parts), pos_arr, "exchange_start_" + name)

    def gathered_start(exchange, after, name):
        send_s, recv_s, sums, gath, _ = exchange
        _, gath = _split_wait(send_s, recv_s, sums, gath, ex_wait, after, "exchange_wait_" + name)
        send_s, recv_s, _, gath, token = _split_start([], gath, sg_start, len(gath), pos_arr,
                                                      "gathered_start_" + name)
        return (send_s, recv_s, gath), token

    def finish(gathered, tags, after, name):
        send_s, recv_s, gath = gathered
        _, gath = _split_wait(send_s, recv_s, [], gath, sg_wait, after, "gathered_wait_" + name)
        for g, t in zip(gath, tags):
            grads[t], deltas[t], new_m[t], new_v[t] = _adam_gathered(w[t], g, m[t], v[t], c_arr, "adam_" + t)

    def behind(value, token):
        return value + token[0:1, 0:1]

    tags_b, tags_a = ("w_up", "w_down"), ("w_in", "w_out")
    swap_b, tok = swap_start([gw_up, gw_down.reshape(N_CHIPS, -1, D_MODEL)], "b")
    dx1, dy, dsh_f, dsc_f, dg_pre2, dgt_m, dg_post = _ffn_bwd_b(
        dup0, x1, y, dx2, w_up4, g_pre2, behind(sc_f, tok), sh_f, gt_m, g_post)
    exchange_b = exchange_start(swap_b, tags_b, dg_post, "b")
    (dz, dcw, dcb, dwr, dwi, dbr, dbi, dspa, dng, dnb, dws, dbs_t, dglo, dggo) = _seqmix_bwd(
        z, hst, stash, dy, w_out_b, seq_params, ws_t, behind(glo, exchange_b[4]), ggo)
    grad_x, dsh_m, dsc_m, dg_pre = _mix_in_bwd(xs, dz, dx1, w_in4, g_pre, sc_m)
    gw_in = _wgrad(h, dz, N_CHIPS, "wgrad_in", True)
    gw_out = _wgrad(ycat, dy, 1, "wgrad_out", False)
    swap_a, tok = swap_start([gw_in, gw_out.reshape(N_CHIPS, -1, D_MODEL)], "a")

    dmod = jnp.concatenate([behind(dsh_m, tok), dsc_m, dgt_m, dsh_f, dsc_f, dgt_f], axis=1)
    dmod8 = _allgather8(dmod, "gather_dmod", False)[:, 0, :]
    grads["w_ada"], deltas["w_ada"], new_m["w_ada"], new_v["w_ada"], g_b_ada = _ada_bwd(
        c8, dmod8, chip_arr, w["w_ada"], m["w_ada"], v["w_ada"])
    exchange_a = exchange_start(swap_a, tags_a, g_b_ada, "a")
    gathered_b, tok = gathered_start(exchange_b, exchange_a[4], "b")

    small_grads = dict(
        g_mix_pre=dg_pre, g_mix_post=dg_post, conv_w=dcw, conv_b=dcb, w_rgate=dwr, b_rgate=dbr, w_igate=dwi,
        b_igate=dbi, lru_a=dspa, v_norm_g=dng, v_norm_b=dnb, w_spatial=dws, b_spatial=dbs_t, g_lru_out=dglo,
        g_gmlp_out=dggo, g_ffn_pre=dg_pre2, g_ffn_post=dg_post2, ffn_conv_w=dfw, ffn_conv_b=dfb,
        loss=behind(loss, tok))
    g_small = _allreduce_small(_pack_small(small_grads), "reduce_small")
    total = g_small[SMALL_SLOT["loss"][0], SMALL_SLOT["loss"][1]]

    rep = SMALL_REPLICATED
    small_out = _adam_small(g_small, {n: args[n] for n in rep}, {n: args["m_" + n] for n in rep},
                            {n: args["v_" + n] for n in rep})
    for n in rep:
        grads[n], deltas[n], new_m[n], new_v[n] = [group[n] for group in small_out]
    finish(gathered_b, tags_b, deltas[rep[0]], "b")

    gathered_a, tok = gathered_start(exchange_a, deltas["w_down"], "a")
    for n in SMALL_COLUMN_SHARDED:
        grads[n], deltas[n], new_m[n], new_v[n] = _adam_cols(n, g_small, args[n], args["m_" + n],
                                                             behind(args["v_" + n], tok), chip_arr)
    d_b, m_b, v_b = _adam(w["b_ada"], g_b_ada, m["b_ada"], behind(v["b_ada"], tok), "adam_b_ada")
    grads["b_ada"], deltas["b_ada"], new_m["b_ada"], new_v["b_ada"] = g_b_ada, d_b, m_b, v_b
    finish(gathered_a, tags_a, d_b, "a")

    outs = [total, grad_x[None]]
    for group in (grads, deltas, new_m, new_v):
        outs.extend(group[n].reshape(args[n].shape) for n in names)
    return tuple(outs)
```

```python
import functools
import math

import jax
import jax.numpy as jnp
from jax import lax
from jax.experimental import pallas as pl
from jax.experimental.pallas import tpu as pltpu

F32 = jnp.float32
BF16 = jnp.bfloat16
MESH = pl.DeviceIdType.MESH

D_MODEL = 1024
LRU_WIDTH = 512
LRU_HEADS = 8
GMLP_WIDTH = 512
GMLP_GROUPS = 4
GMLP_BLOCK = 128
CHUNK = 64
D_FF = 3072
N_MOD = 6
EPS = 1e-6
LRU_C = 8.0
N_CHIPS = 4
N_DEV = 8

ADAM_LR = 0.001
ADAM_B1 = 0.9
ADAM_B2 = 0.999
ADAM_EPS = 1e-08
ADAM_WD = 0.01
ADAM_STEP = 10

GELU_C0 = math.sqrt(2.0 / math.pi)
GELU_C1 = 0.044715

VMEM_LIMIT_BYTES = 56 * 1024 * 1024
SUBLANES = 8
LANES = 128
BF16_SUBLANES = 16
FFN_CHUNK = 768
SUB_ROWS = 256


def _gelu_gate(x):
    x2 = x * x
    z = x * ((2.0 * GELU_C0 * GELU_C1) * x2 + 2.0 * GELU_C0)
    return 1.0 / (1.0 + jnp.exp(-z)), x2


def _gelu(x):
    t = jnp.tanh(GELU_C0 * (x + GELU_C1 * x * x * x))
    return 0.5 * x * (1.0 + t)


def _gelu_and_grad(x):
    s, x2 = _gelu_gate(x)
    g = x * s
    dz = (6.0 * GELU_C0 * GELU_C1) * x2 + 2.0 * GELU_C0
    return g, s + g * (1.0 - s) * dz


def _sigmoid(x):
    return 1.0 / (1.0 + jnp.exp(-x))


def _log1p(u):
    w = 1.0 + u
    return jnp.where(w == 1.0, u, jnp.log(w) * (u / (w - 1.0)))


def _softplus(x):
    return jnp.maximum(x, 0.0) + _log1p(jnp.exp(-jnp.abs(x)))


def _neg_expm1(x):
    u = jnp.exp(x)
    um1 = u - 1.0
    tiny = um1 == 0.0
    small = um1 * (x / jnp.log(jnp.where(tiny, 2.0, jnp.maximum(u, 0.25))))
    return -jnp.where(tiny, x, jnp.where(x < -1.0, um1, small))


def _msq_rsqrt(v):
    return lax.rsqrt(jnp.mean(v * v, axis=-1, keepdims=True) + EPS)


def _rms_bwd(dyn, yn, r):
    return r * (dyn - yn * jnp.mean(dyn * yn, axis=-1, keepdims=True))


def _colsum(v):
    return jnp.sum(v, axis=0, keepdims=True)


def _shift_down(cur, prev8, k):
    rolled = pltpu.roll(cur, k, 0)
    head = pltpu.roll(prev8, k, 0)
    row8 = lax.broadcasted_iota(jnp.int32, (SUBLANES, cur.shape[1]), 0)
    first = jnp.where(row8 < k, head, rolled[0:SUBLANES])
    return jnp.concatenate([first, rolled[SUBLANES:]], axis=0)


def _shift_up(cur, next8, k):
    t = cur.shape[0]
    rolled = pltpu.roll(cur, t - k, 0)
    tail = pltpu.roll(next8, SUBLANES - k, 0)
    row8 = lax.broadcasted_iota(jnp.int32, (SUBLANES, cur.shape[1]), 0)
    last = jnp.where(row8 >= SUBLANES - k, tail, rolled[t - SUBLANES:])
    return jnp.concatenate([rolled[:t - SUBLANES], last], axis=0)


def _scan_fwd(a, b):
    t = a.shape[0]
    row = lax.broadcasted_iota(jnp.int32, a.shape, 0)
    d = 1
    while d < t:
        keep = row >= d
        a_s = jnp.where(keep, pltpu.roll(a, d, 0), 1.0)
        b_s = jnp.where(keep, pltpu.roll(b, d, 0), 0.0)
        b = a * b_s + b
        a = a * a_s
        d *= 2
    return a, b


def _scan_bwd(a, g):
    t = a.shape[0]
    row = lax.broadcasted_iota(jnp.int32, a.shape, 0)
    d = 1
    while d < t:
        keep = row < t - d
        a_s = jnp.where(keep, pltpu.roll(a, t - d, 0), 1.0)
        g_s = jnp.where(keep, pltpu.roll(g, t - d, 0), 0.0)
        g = a * g_s + g
        a = a * a_s
        d *= 2
    return a, g


def _dot(a, b):
    return jnp.dot(a, b, preferred_element_type=F32)


def _dot_nt(a, b):
    return lax.dot_general(a, b, (((1,), (1,)), ((), ())), preferred_element_type=F32)


def _dot_tn(a, b):
    return lax.dot_general(a, b, (((0,), (0,)), ((), ())), preferred_element_type=F32)


def _rows(ts, cols, rev_of=None):
    if rev_of is None:
        return pl.BlockSpec((ts, cols), lambda i: (i, 0))
    return pl.BlockSpec((ts, cols), lambda i: (rev_of - 1 - i, 0))


def _halo_prev(ts, cols, halo, rev_of=None, col_block=0):
    per = ts // halo
    if rev_of is None:
        return pl.BlockSpec((halo, cols), lambda i: (jnp.maximum(i * per - 1, 0), col_block))
    return pl.BlockSpec((halo, cols), lambda i: (jnp.maximum((rev_of - 1 - i) * per - 1, 0), col_block))


def _full(shape):
    nd = len(shape)
    return pl.BlockSpec(shape, lambda *_: (0,) * nd)


_RESIDENT = pl.BlockSpec(memory_space=pltpu.VMEM)


def _params(sem):
    return pltpu.CompilerParams(dimension_semantics=sem, vmem_limit_bytes=VMEM_LIMIT_BYTES)


def _sds(shape, dtype):
    return jax.ShapeDtypeStruct(shape, dtype)


def _sub_tiles(ts):
    return [slice(r0, r0 + SUB_ROWS) for r0 in range(0, ts, SUB_ROWS)]


def _mix_in(x, sc, sh, g, w_in4, ts=512):
    s, d = x.shape

    def body(x_ref, sc_ref, sh_ref, g_ref, w_ref, z_ref, h_ref):
        for rs in _sub_tiles(ts):
            xv = x_ref[rs, :]
            h = (xv * _msq_rsqrt(xv) * g_ref[...]) * (1.0 + sc_ref[...]) + sh_ref[...]
            hb = h.astype(BF16)
            h_ref[rs, :] = hb
            for k in range(N_CHIPS):
                z_ref[rs, k * 512:(k + 1) * 512] = _dot(hb, w_ref[k])

    return pl.pallas_call(
        body, grid=(s // ts,), name="mix_in",
        in_specs=[_rows(ts, d), _full((1, d)), _full((1, d)), _full((1, d)), _full(w_in4.shape)],
        out_specs=[_rows(ts, 2048), _rows(ts, d)],
        out_shape=[_sds((s, 2048), F32), _sds((s, d), BF16)],
        compiler_params=_params(("parallel",)),
    )(x, sc, sh, g, w_in4)


N_STASH = 12
(ST_XC, ST_R, ST_IG, ST_A, ST_MULT, ST_GL, ST_DGL, ST_U, ST_DU, ST_Q, ST_VHAT, ST_SPB) = range(N_STASH)


def _seq_param_specs():
    return [_full((4, 512)), _full((1, 512)), _full((512, 512)), _full((512, 512)), _full((1, 512)),
            _full((1, 512)), _full((1, 512)), _full((1, 512)), _full((1, 512)), _full((4, 128, 128)),
            _full((128, 4))]


def _seqmix(z, seq_params, glo, ggo, ts=256):
    s = z.shape[0]
    nt = s // ts

    def body(z_ref, zprev_ref, cw_ref, cb_ref, bdr_ref, bdi_ref, br_ref, bi_ref, la_ref, ng_ref, nb_ref,
             ws_ref, bst_ref, glo_ref, ggo_ref, ycat_ref, hst_ref, st_ref, hcarry, sp_scr):
        i = pl.program_id(0)

        @pl.when(i == 0)
        def _():
            hcarry[...] = jnp.zeros_like(hcarry)

        lx = z_ref[:, 0:512]
        prev8 = jnp.where(i == 0, 0.0, zprev_ref[...])
        xc = (cw_ref[3:4, :] * lx + cw_ref[2:3, :] * _shift_down(lx, prev8, 1)
              + cw_ref[1:2, :] * _shift_down(lx, prev8, 2) + cw_ref[0:1, :] * _shift_down(lx, prev8, 3)
              + cb_ref[...])
        xcb = xc.astype(BF16)
        r = _sigmoid(_dot(xcb, bdr_ref[...]) + br_ref[...])
        ig = _sigmoid(_dot(xcb, bdi_ref[...]) + bi_ref[...])
        log_a = (-LRU_C) * r * _softplus(-la_ref[...])
        a = jnp.exp(log_a)
        mult = jnp.sqrt(_neg_expm1(2.0 * log_a))
        acum, hloc = _scan_fwd(a, mult * (ig * xc))
        h = hloc + acum * hcarry[...]
        hcarry[...] = h[ts - 1:ts, :]
        hst_ref[...] = h
        gl, dgl = _gelu_and_grad(z_ref[:, 512:1024])
        y_l = h * gl
        for slot, val in ((ST_XC, xc), (ST_R, r), (ST_IG, ig), (ST_A, a), (ST_MULT, mult), (ST_GL, gl),
                          (ST_DGL, dgl)):
            st_ref[slot] = val

        u, du = _gelu_and_grad(z_ref[:, 1024:1536])
        vg, dvg = _gelu_and_grad(z_ref[:, 1536:2048])
        vc = vg - jnp.mean(vg, axis=-1, keepdims=True)
        rstd = lax.rsqrt(jnp.mean(vc * vc, axis=-1, keepdims=True) + EPS)
        vhat = vc * rstd
        vb = (vhat * ng_ref[...] + nb_ref[...]).astype(BF16)
        for n in range(ts // GMLP_BLOCK):
            rs = slice(n * GMLP_BLOCK, (n + 1) * GMLP_BLOCK)
            for g in range(GMLP_GROUPS):
                cs = slice(g * 128, (g + 1) * 128)
                sp_scr[rs, cs] = _dot(ws_ref[g], vb[rs, cs]) + bst_ref[:, g:g + 1]
        spb = sp_scr[...]
        y_g = u * spb
        for slot, val in ((ST_U, u), (ST_DU, du), (ST_Q, rstd * dvg), (ST_VHAT, vhat), (ST_SPB, spb)):
            st_ref[slot] = val

        ycat_ref[:, 0:512] = (y_l * _msq_rsqrt(y_l) * glo_ref[...]).astype(BF16)
        ycat_ref[:, 512:1024] = (y_g * _msq_rsqrt(y_g) * ggo_ref[...]).astype(BF16)

    return pl.pallas_call(
        body, grid=(nt,), name="seqmix",
        in_specs=[_rows(ts, 2048), _halo_prev(ts, 512, SUBLANES)] + _seq_param_specs()
        + [_full((1, 512)), _full((1, 512))],
        out_specs=[_rows(ts, 1024), _rows(ts, 512), pl.BlockSpec((N_STASH, ts, 512), lambda i: (0, i, 0))],
        out_shape=[_sds((s, 1024), BF16), _sds((s, 512), F32), _sds((N_STASH, s, 512), F32)],
        scratch_shapes=[pltpu.VMEM((1, 512), F32), pltpu.VMEM((ts, 512), F32)],
        compiler_params=_params(("arbitrary",)),
    )(z, z, *seq_params, glo, ggo)


def _mix_out(ycat, x, w_out, gt_m, g_post, g_pre2, sc_f, sh_f, ts=512):
    s, d = x.shape

    def body(yc_ref, x_ref, w_ref, gt_ref, gp_ref, g2_ref, sc_ref, sh_ref, y_ref, x1_ref, h2_ref):
        for rs in _sub_tiles(ts):
            y = _dot(yc_ref[rs, :], w_ref[...])
            y_ref[rs, :] = y
            x1 = x_ref[rs, :] + gt_ref[...] * (y * _msq_rsqrt(y) * gp_ref[...])
            x1_ref[rs, :] = x1
            h2 = (x1 * _msq_rsqrt(x1) * g2_ref[...]) * (1.0 + sc_ref[...]) + sh_ref[...]
            h2_ref[rs, :] = h2.astype(BF16)

    vec = _full((1, d))
    return pl.pallas_call(
        body, grid=(s // ts,), name="mix_out",
        in_specs=[_rows(ts, d), _rows(ts, d), _full((d, d)), vec, vec, vec, vec, vec],
        out_specs=[_rows(ts, d), _rows(ts, d), _rows(ts, d)],
        out_shape=[_sds((s, d), F32), _sds((s, d), F32), _sds((s, d), BF16)],
        compiler_params=_params(("parallel",)),
    )(ycat, x, w_out, gt_m, g_post, g_pre2, sc_f, sh_f)


def _ffn_cols(j):
    per = (2 * D_FF // N_CHIPS) // FFN_CHUNK
    return j // per, (j % per) * FFN_CHUNK, j * FFN_CHUNK


def _ffn_fwd(h2, x1, tgt, w_up4, w_down, fw, fb, gt_f, g_post, ts=256):
    s, d = x1.shape
    nch = D_FF // FFN_CHUNK

    def body(h2_ref, x1_ref, tgt_ref, wup_ref, wdn_ref, fw_ref, fb_ref, gt_ref, gp_ref,
             up0_ref, pre_ref, act_ref, dy2_ref, dx2_ref, loss_ref, dgt_ref, dgp_ref, tail_ref):
        i = pl.program_id(0)

        @pl.when(i == 0)
        def _():
            tail_ref[...] = jnp.zeros_like(tail_ref)
            loss_ref[...] = jnp.zeros_like(loss_ref)
            dgt_ref[...] = jnp.zeros_like(dgt_ref)
            dgp_ref[...] = jnp.zeros_like(dgp_ref)

        hb = h2_ref[...]

        def up_project(j):
            sh_g, off, _ = _ffn_cols(j)
            return [_dot(hb, wup_ref[shard, :, off:off + FFN_CHUNK]).astype(BF16) for shard in (sh_g, sh_g + 2)]

        y2 = jnp.zeros((ts, d), F32)
        ahead = up_project(0)
        for j in range(nch):
            _, _, col = _ffn_cols(j)
            ubs = ahead
            if j + 1 < nch:
                ahead = up_project(j + 1)
            halves = []
            for ub, c0 in zip(ubs, (col, D_FF + col)):
                cs = slice(c0, c0 + FFN_CHUNK)
                up0_ref[:, cs] = ub
                u = ub.astype(F32)
                prev8 = tail_ref[:, cs]
                tail_ref[:, cs] = u[ts - SUBLANES:, :]
                halves.append(fw_ref[2:3, cs] * u + fw_ref[1:2, cs] * _shift_down(u, prev8, 1)
                              + fw_ref[0:1, cs] * _shift_down(u, prev8, 2) + fb_ref[:, cs])
                pre_ref[:, cs] = halves[-1].astype(BF16)
            act = (_gelu(halves[0]) * halves[1]).astype(BF16)
            act_ref[:, col:col + FFN_CHUNK] = act
            y2 = y2 + _dot(act, wdn_ref[col:col + FFN_CHUNK, :])
        r2 = _msq_rsqrt(y2)
        yn = y2 * r2
        yng = yn * gp_ref[...]
        e = x1_ref[...] + gt_ref[...] * yng - tgt_ref[...]
        loss_ref[...] += jnp.sum(e * e) * (0.5 / d)
        dx2 = e * (1.0 / d)
        dx2_ref[...] = dx2
        dgt_ref[...] += _colsum(dx2 * yng)
        dyng = dx2 * gt_ref[...]
        dgp_ref[...] += _colsum(dyng * yn)
        dy2_ref[...] = _rms_bwd(dyng * gp_ref[...], yn, r2).astype(BF16)

    vec = _full((1, d))
    return pl.pallas_call(
        body, grid=(s // ts,), name="ffn_fwd",
        in_specs=[_rows(ts, d), _rows(ts, d), _rows(ts, d), _RESIDENT, _RESIDENT,
                  _full((3, 2 * D_FF)), _full((1, 2 * D_FF)), vec, vec],
        out_specs=[_rows(ts, 2 * D_FF), _rows(ts, 2 * D_FF), _rows(ts, D_FF), _rows(ts, d), _rows(ts, d),
                   _full((1, 128)), vec, vec],
        out_shape=[_sds((s, 2 * D_FF), BF16), _sds((s, 2 * D_FF), BF16), _sds((s, D_FF), BF16), _sds((s, d), BF16),
                   _sds((s, d), F32), _sds((1, 128), F32), _sds((1, d), F32), _sds((1, d), F32)],
        scratch_shapes=[pltpu.VMEM((SUBLANES, 2 * D_FF), F32)],
        compiler_params=_params(("arbitrary",)),
    )(h2, x1, tgt, w_up4, w_down, fw, fb, gt_f, g_post)


def _shift_up_mxu(vb, up_mat, next8, k):
    t = vb.shape[0]
    main = _dot(up_mat, vb)
    tail = pltpu.roll(next8, SUBLANES - k, 0)
    row8 = lax.broadcasted_iota(jnp.int32, next8.shape, 0)
    last = main[t - SUBLANES:] + jnp.where(row8 >= SUBLANES - k, tail, 0.0)
    return jnp.concatenate([main[:t - SUBLANES], last], axis=0)


def _ffn_bwd_a(dy2, pre, up0, w_down, fw, ts=256):
    s, d = dy2.shape
    nt = s // ts
    nch = D_FF // FFN_CHUNK
    wide = 2 * D_FF
    up_mats = jnp.stack([jnp.eye(ts, k=1, dtype=BF16), jnp.eye(ts, k=2, dtype=BF16)])

    def body(dy2_ref, pre_ref, up0_ref, wdn_ref, fw_ref, um_ref, dup0_ref, dfw_ref, dfb_ref, next_ref):
        i = pl.program_id(0)

        @pl.when(i == 0)
        def _():
            next_ref[...] = jnp.zeros_like(next_ref)
            dfw_ref[...] = jnp.zeros_like(dfw_ref)
            dfb_ref[...] = jnp.zeros_like(dfb_ref)

        dyb = dy2_ref[...]
        for j in range(nch):
            _, _, col = _ffn_cols(j)
            dact = _dot_nt(dyb, wdn_ref[col:col + FFN_CHUNK, :])
            gl, dgl = _gelu_and_grad(pre_ref[:, col:col + FFN_CHUNK].astype(F32))
            dpre = (dact * pre_ref[:, D_FF + col:D_FF + col + FFN_CHUNK].astype(F32) * dgl, dact * gl)
            for half, c0 in enumerate((col, D_FF + col)):
                cs = slice(c0, c0 + FFN_CHUNK)
                dp = dpre[half]
                dpb = dp.astype(BF16)
                nxt = next_ref[:, cs]
                next_ref[:, cs] = dpb.astype(F32)[0:SUBLANES, :]
                su1 = _shift_up_mxu(dpb, um_ref[0], nxt, 1)
                su2 = _shift_up_mxu(dpb, um_ref[1], nxt, 2)
                u = up0_ref[:, cs].astype(F32)
                dfb_ref[:, cs] += _colsum(dp)
                dfw_ref[2:3, cs] += _colsum(dp * u)
                dfw_ref[1:2, cs] += _colsum(su1 * u)
                dfw_ref[0:1, cs] += _colsum(su2 * u)
                dup0 = fw_ref[2:3, cs] * dp + fw_ref[1:2, cs] * su1 + fw_ref[0:1, cs] * su2
                dup0_ref[:, cs] = dup0.astype(BF16)

    return pl.pallas_call(
        body, grid=(nt,), name="ffn_bwd_a",
        in_specs=[_rows(ts, d, nt), _rows(ts, wide, nt), _rows(ts, wide, nt), _RESIDENT,
                  _full((3, wide)), _full((2, ts, ts))],
        out_specs=[_rows(ts, wide, nt), _full((3, wide)), _full((1, wide))],
        out_shape=[_sds((s, wide), BF16), _sds((3, wide), F32), _sds((1, wide), F32)],
        scratch_shapes=[pltpu.VMEM((SUBLANES, wide), F32)],
        compiler_params=_params(("arbitrary",)),
    )(dy2, pre, up0, w_down, fw, up_mats)


def _ffn_bwd_b(dup0, x1, y, dx2, w_up4, g_pre2, sc_f, sh_f, gt_m, g_post_m, ts=512):
    s, d = x1.shape
    shard_cols = 2 * D_FF // N_CHIPS

    def body(dup_ref, x1_ref, y_ref, dx2_ref, wup_ref, g2_ref, sc_ref, sh_ref, gt_ref, gp_ref,
             dx1_ref, dy_ref, dsh_ref, dsc_ref, dg2_ref, dgt_ref, dgp_ref):
        i = pl.program_id(0)

        @pl.when(i == 0)
        def _():
            for ref in (dsh_ref, dsc_ref, dg2_ref, dgt_ref, dgp_ref):
                ref[...] = jnp.zeros_like(ref)

        for rs in _sub_tiles(ts):
            dh2 = jnp.zeros((SUB_ROWS, d), F32)
            for k in range(N_CHIPS):
                dh2 = dh2 + _dot_nt(dup_ref[rs, k * shard_cols:(k + 1) * shard_cols], wup_ref[k])
            x1v = x1_ref[rs, :]
            r2 = _msq_rsqrt(x1v)
            xn = x1v * r2
            hn = xn * g2_ref[...]
            dsh_ref[...] += _colsum(dh2)
            dsc_ref[...] += _colsum(dh2 * hn)
            dhn = dh2 * (1.0 + sc_ref[...])
            dg2_ref[...] += _colsum(dhn * xn)
            dx1 = dx2_ref[rs, :] + _rms_bwd(dhn * g2_ref[...], xn, r2)
            dx1_ref[rs, :] = dx1
            yv = y_ref[rs, :]
            ry = _msq_rsqrt(yv)
            yn = yv * ry
            dgt_ref[...] += _colsum(dx1 * (yn * gp_ref[...]))
            dyng = dx1 * gt_ref[...]
            dgp_ref[...] += _colsum(dyng * yn)
            dy_ref[rs, :] = _rms_bwd(dyng * gp_ref[...], yn, ry).astype(BF16)

    vec = _full((1, d))
    return pl.pallas_call(
        body, grid=(s // ts,), name="ffn_bwd_b",
        in_specs=[_rows(ts, 2 * D_FF), _rows(ts, d), _rows(ts, d), _rows(ts, d), _RESIDENT,
                  vec, vec, vec, vec, vec],
        out_specs=[_rows(ts, d), _rows(ts, d), vec, vec, vec, vec, vec],
        out_shape=[_sds((s, d), F32), _sds((s, d), BF16)] + [_sds((1, d), F32)] * 5,
        compiler_params=_params(("arbitrary",)),
    )(dup0, x1, y, dx2, w_up4, g_pre2, sc_f, sh_f, gt_m, g_post_m)


def _seqmix_bwd(z, hst, stash, dy, w_out, seq_params, ws_t, glo, ggo, ts=256):
    s = z.shape[0]
    nt = s // ts
    small_shapes = [(4, 512), (1, 512), (512, 512), (512, 512), (1, 512), (1, 512), (1, 512),
                    (1, 512), (1, 512), (4, 128, 128), (128, 4), (1, 512), (1, 512)]

    def body(lx_ref, hst_ref, hprev_ref, st_ref, dy_ref, wout_ref, cw_ref, cb_ref, bdr_ref, bdi_ref, br_ref,
             bi_ref, la_ref, ng_ref, nb_ref, ws_ref, bst_ref, wst_ref, glo_ref, ggo_ref, dz_ref, *rest):
        small_refs = rest[:13]
        (dcw_ref, dcb_ref, dwr_ref, dwi_ref, dbr_ref, dbi_ref, dspa_ref, dng_ref, dnb_ref, dws_ref, dbs_ref,
         dglo_ref, dggo_ref) = small_refs
        gcarry, anext, dxcnext, dv_scr = rest[13:]
        i = pl.program_id(0)

        @pl.when(i == 0)
        def _():
            for ref in small_refs:
                ref[...] = jnp.zeros_like(ref)
            gcarry[...] = jnp.zeros_like(gcarry)
            anext[...] = jnp.ones_like(anext)
            dxcnext[...] = jnp.zeros_like(dxcnext)

        first_tile = i == nt - 1
        xc, r, ig, a, mult = st_ref[ST_XC], st_ref[ST_R], st_ref[ST_IG], st_ref[ST_A], st_ref[ST_MULT]
        gl, u, spb, vhat = st_ref[ST_GL], st_ref[ST_U], st_ref[ST_SPB], st_ref[ST_VHAT]
        lx = lx_ref[...]
        h = hst_ref[...]
        hprev = _shift_down(h, jnp.where(first_tile, 0.0, hprev_ref[...]), 1)
        y_l = h * gl
        y_g = u * spb

        dycat = _dot_nt(dy_ref[...], wout_ref[...])
        rl = _msq_rsqrt(y_l)
        yln = y_l * rl
        dyl = dycat[:, 0:512]
        dglo_ref[...] += _colsum(dyl * yln)
        dy_l = _rms_bwd(dyl * glo_ref[...], yln, rl)
        rg = _msq_rsqrt(y_g)
        ygn = y_g * rg
        dyg = dycat[:, 512:1024]
        dggo_ref[...] += _colsum(dyg * ygn)
        dy_g = _rms_bwd(dyg * ggo_ref[...], ygn, rg)

        dz_ref[:, 512:1024] = (dy_l * h * st_ref[ST_DGL]).astype(BF16)
        a_up = _shift_up(a, anext[...], 1)
        acum, gloc = _scan_bwd(a_up, dy_l * gl)
        gg = gloc + acum * gcarry[...]
        gcarry[...] = gg[0:1, :]
        anext[...] = a[0:SUBLANES, :]
        da = gg * hprev
        t1 = gg * mult
        di = t1 * xc
        dxc = t1 * ig
        dmult = gg * ig * xc
        dla = da * a - dmult * (a * a / mult)
        dspa_ref[...] += _colsum(dla * r) * (-LRU_C)
        dpr = dla * ((-LRU_C) * _softplus(-la_ref[...])) * r * (1.0 - r)
        dpi = di * ig * (1.0 - ig)
        dbr_ref[...] += _colsum(dpr)
        dbi_ref[...] += _colsum(dpi)
        dprb = dpr.astype(BF16)
        dpib = dpi.astype(BF16)
        xcb = xc.astype(BF16)
        dwr_ref[...] += _dot_tn(xcb, dprb)
        dwi_ref[...] += _dot_tn(xcb, dpib)
        dxc = dxc + _dot_nt(dprb, bdr_ref[...]) + _dot_nt(dpib, bdi_ref[...])
        nxt = dxcnext[...]
        dxcnext[...] = dxc[0:SUBLANES, :]
        up1, up2, up3 = _shift_up(dxc, nxt, 1), _shift_up(dxc, nxt, 2), _shift_up(dxc, nxt, 3)
        dcb_ref[...] += _colsum(dxc)
        dcw_ref[3:4, :] += _colsum(dxc * lx)
        dcw_ref[2:3, :] += _colsum(up1 * lx)
        dcw_ref[1:2, :] += _colsum(up2 * lx)
        dcw_ref[0:1, :] += _colsum(up3 * lx)
        dlx = cw_ref[3:4, :] * dxc + cw_ref[2:3, :] * up1 + cw_ref[1:2, :] * up2 + cw_ref[0:1, :] * up3
        dz_ref[:, 0:512] = dlx.astype(BF16)

        dz_ref[:, 1024:1536] = (dy_g * spb * st_ref[ST_DU]).astype(BF16)
        dsp = dy_g * u
        vb = (vhat * ng_ref[...] + nb_ref[...]).astype(BF16)
        for n in range(ts // GMLP_BLOCK):
            rs = slice(n * GMLP_BLOCK, (n + 1) * GMLP_BLOCK)
            for g in range(GMLP_GROUPS):
                cs = slice(g * 128, (g + 1) * 128)
                dbs_ref[:, g:g + 1] += jnp.sum(dsp[rs, cs], axis=1, keepdims=True)
                blk = dsp[rs, cs].astype(BF16)
                dws_ref[g] += _dot_nt(blk, vb[rs, cs])
                dv_scr[rs, cs] = _dot(wst_ref[g], blk)
        dv = dv_scr[...]
        dng_ref[...] += _colsum(dv * vhat)
        dnb_ref[...] += _colsum(dv)
        dvh = dv * ng_ref[...]
        dvg = dvh - jnp.mean(dvh, axis=-1, keepdims=True) - vhat * jnp.mean(dvh * vhat, axis=-1, keepdims=True)
        dz_ref[:, 1536:2048] = (dvg * st_ref[ST_Q]).astype(BF16)

        @pl.when(i == nt - 1)
        def _():
            pos = lax.broadcasted_iota(jnp.int32, (GMLP_BLOCK, GMLP_BLOCK), 0) // CHUNK
            src = lax.broadcasted_iota(jnp.int32, (GMLP_BLOCK, GMLP_BLOCK), 1) // CHUNK
            for g in range(GMLP_GROUPS):
                dws_ref[g] = jnp.where(src <= pos, dws_ref[g], 0.0)
            dspa_ref[...] = dspa_ref[...] * (-_sigmoid(-la_ref[...]))

    in_specs = ([_rows(ts, 512, nt), _rows(ts, 512, nt), _halo_prev(ts, 512, SUBLANES, nt),
                 pl.BlockSpec((N_STASH, ts, 512), lambda i: (0, nt - 1 - i, 0)), _rows(ts, 1024, nt),
                 _full((1024, 1024))]
                + _seq_param_specs() + [_full((4, 128, 128)), _full((1, 512)), _full((1, 512))])
    return pl.pallas_call(
        body, grid=(nt,), name="seqmix_bwd",
        in_specs=in_specs,
        out_specs=[_rows(ts, 2048, nt)] + [_full(sh) for sh in small_shapes],
        out_shape=[_sds((s, 2048), BF16)] + [_sds(sh, F32) for sh in small_shapes],
        scratch_shapes=[pltpu.VMEM((1, 512), F32), pltpu.VMEM((SUBLANES, 512), F32),
                        pltpu.VMEM((SUBLANES, 512), F32), pltpu.VMEM((ts, 512), F32)],
        compiler_params=_params(("arbitrary",)),
    )(z, hst, hst, stash, dy, w_out, *seq_params, ws_t, glo, ggo)


def _seqmix_bwd_recomputing_unused(z, hst, dy, w_out, seq_params, ws_t, glo, ggo, ts=256):
    s = z.shape[0]
    nt = s // ts
    small_shapes = [(4, 512), (1, 512), (512, 512), (512, 512), (1, 512), (1, 512), (1, 512),
                    (1, 512), (1, 512), (4, 128, 128), (128, 4), (1, 512), (1, 512)]

    def body(z_ref, zprev_ref, hst_ref, hprev_ref, dy_ref, wout_ref, *rest):
        p = rest[:11]
        wst_ref, glo_ref, ggo_ref = rest[11:14]
        dz_ref = rest[14]
        (dcw_ref, dcb_ref, dwr_ref, dwi_ref, dbr_ref, dbi_ref, dspa_ref, dng_ref, dnb_ref, dws_ref, dbs_ref,
         dglo_ref, dggo_ref) = rest[15:28]
        gcarry, anext, dxcnext, sp_scr, dv_scr = rest[28:]
        i = pl.program_id(0)

        @pl.when(i == 0)
        def _():
            for ref in rest[15:28]:
                ref[...] = jnp.zeros_like(ref)
            gcarry[...] = jnp.zeros_like(gcarry)
            anext[...] = jnp.ones_like(anext)
            dxcnext[...] = jnp.zeros_like(dxcnext)

        first_tile = i == nt - 1
        f = _seq_recompute(z_ref, zprev_ref, first_tile, p)
        xc, r, ig, a, mult, lx = f["xc"], f["r"], f["ig"], f["a"], f["mult"], f["lx"]
        h = hst_ref[...]
        hprev = _shift_down(h, jnp.where(first_tile, 0.0, hprev_ref[...]), 1)
        gl, dgl = _gelu_and_grad(f["lg"])
        y_l = h * gl
        gm = _gmlp_fwd(f["gu"], f["gv"], p[7], p[8], p[9], p[10], sp_scr)
        y_g = gm["y_g"]

        dycat = _dot_nt(dy_ref[...], wout_ref[...])
        rl = _msq_rsqrt(y_l)
        yln = y_l * rl
        dyl = dycat[:, 0:512]
        dglo_ref[...] += _colsum(dyl * yln)
        dy_l = _rms_bwd(dyl * glo_ref[...], yln, rl)
        rg = _msq_rsqrt(y_g)
        ygn = y_g * rg
        dyg = dycat[:, 512:1024]
        dggo_ref[...] += _colsum(dyg * ygn)
        dy_g = _rms_bwd(dyg * ggo_ref[...], ygn, rg)

        dz_ref[:, 512:1024] = (dy_l * h * dgl).astype(BF16)
        a_up = _shift_up(a, anext[...], 1)
        acum, gloc = _scan_bwd(a_up, dy_l * gl)
        gg = gloc + acum * gcarry[...]
        gcarry[...] = gg[0:1, :]
        anext[...] = a[0:SUBLANES, :]
        da = gg * hprev
        t1 = gg * mult
        di = t1 * xc
        dxc = t1 * ig
        dmult = gg * ig * xc
        dla = da * a - dmult * (a * a / mult)
        spa = f["spa"]
        dspa_ref[...] += _colsum(dla * r) * (-LRU_C)
        dpr = dla * ((-LRU_C) * spa) * r * (1.0 - r)
        dpi = di * ig * (1.0 - ig)
        dbr_ref[...] += _colsum(dpr)
        dbi_ref[...] += _colsum(dpi)
        dprb = dpr.astype(BF16)
        dpib = dpi.astype(BF16)
        dwr_ref[...] += _dot_tn(f["xcb"], dprb)
        dwi_ref[...] += _dot_tn(f["xcb"], dpib)
        dxc = dxc + _dot_nt(dprb, p[2][...]) + _dot_nt(dpib, p[3][...])
        dcb_ref[...] += _colsum(dxc)
        dcw_ref[3:4, :] += _colsum(dxc * lx)
        dcw_ref[2:3, :] += _colsum(dxc * f["s1"])
        dcw_ref[1:2, :] += _colsum(dxc * f["s2"])
        dcw_ref[0:1, :] += _colsum(dxc * f["s3"])
        nxt = dxcnext[...]
        dxcnext[...] = dxc[0:SUBLANES, :]
        cw_ref = p[0]
        dlx = (cw_ref[3:4, :] * dxc + cw_ref[2:3, :] * _shift_up(dxc, nxt, 1)
               + cw_ref[1:2, :] * _shift_up(dxc, nxt, 2) + cw_ref[0:1, :] * _shift_up(dxc, nxt, 3))
        dz_ref[:, 0:512] = dlx.astype(BF16)

        dz_ref[:, 1024:1536] = (dy_g * gm["spb"] * gm["du"]).astype(BF16)
        dsp = dy_g * gm["u"]
        vb = gm["vb"]
        for n in range(ts // GMLP_BLOCK):
            rs = slice(n * GMLP_BLOCK, (n + 1) * GMLP_BLOCK)
            for g in range(GMLP_GROUPS):
                cs = slice(g * 128, (g + 1) * 128)
                dbs_ref[:, g:g + 1] += jnp.sum(dsp[rs, cs], axis=1, keepdims=True)
                blk = dsp[rs, cs].astype(BF16)
                dws_ref[g] += _dot_nt(blk, vb[rs, cs])
                dv_scr[rs, cs] = _dot(wst_ref[g], blk)
        dv = dv_scr[...]
        vhat = gm["vhat"]
        dng_ref[...] += _colsum(dv * vhat)
        dnb_ref[...] += _colsum(dv)
        dvh = dv * p[7][...]
        dvg = gm["rstd"] * (dvh - jnp.mean(dvh, axis=-1, keepdims=True)
                            - vhat * jnp.mean(dvh * vhat, axis=-1, keepdims=True))
        dz_ref[:, 1536:2048] = (dvg * gm["dvg"]).astype(BF16)

        @pl.when(i == nt - 1)
        def _():
            pos = lax.broadcasted_iota(jnp.int32, (GMLP_BLOCK, GMLP_BLOCK), 0) // CHUNK
            src = lax.broadcasted_iota(jnp.int32, (GMLP_BLOCK, GMLP_BLOCK), 1) // CHUNK
            for g in range(GMLP_GROUPS):
                dws_ref[g] = jnp.where(src <= pos, dws_ref[g], 0.0)
            dspa_ref[...] = dspa_ref[...] * (-_sigmoid(-p[6][...]))

    in_specs = (_seq_specs(ts, nt, True)
                + [_rows(ts, 512, nt), _halo_prev(ts, 512, SUBLANES, nt), _rows(ts, 1024, nt), _full((1024, 1024))]
                + _seq_param_specs() + [_full((4, 128, 128)), _full((1, 512)), _full((1, 512))])
    return pl.pallas_call(
        body, grid=(nt,), name="seqmix_bwd",
        in_specs=in_specs,
        out_specs=[_rows(ts, 2048, nt)] + [_full(sh) for sh in small_shapes],
        out_shape=[_sds((s, 2048), BF16)] + [_sds(sh, F32) for sh in small_shapes],
        scratch_shapes=[pltpu.VMEM((1, 512), F32), pltpu.VMEM((SUBLANES, 512), F32),
                        pltpu.VMEM((SUBLANES, 512), F32), pltpu.VMEM((ts, 512), F32), pltpu.VMEM((ts, 512), F32)],
        compiler_params=_params(("arbitrary",)),
    )(z, z, hst, hst, dy, w_out, *seq_params, ws_t, glo, ggo)


def _mix_in_bwd(x, dz, dx1, w_in4, g, sc, ts=512):
    s, d = x.shape

    def body(x_ref, dz_ref, dx1_ref, w_ref, g_ref, sc_ref, gx_ref, dsh_ref, dsc_ref, dg_ref):
        i = pl.program_id(0)

        @pl.when(i == 0)
        def _():
            for ref in (dsh_ref, dsc_ref, dg_ref):
                ref[...] = jnp.zeros_like(ref)

        for rs in _sub_tiles(ts):
            dh = jnp.zeros((SUB_ROWS, d), F32)
            for k in range(N_CHIPS):
                dh = dh + _dot_nt(dz_ref[rs, k * 512:(k + 1) * 512], w_ref[k])
            xv = x_ref[rs, :]
            r = _msq_rsqrt(xv)
            xn = xv * r
            dsh_ref[...] += _colsum(dh)
            dsc_ref[...] += _colsum(dh * (xn * g_ref[...]))
            dhn = dh * (1.0 + sc_ref[...])
            dg_ref[...] += _colsum(dhn * xn)
            gx_ref[rs, :] = dx1_ref[rs, :] + _rms_bwd(dhn * g_ref[...], xn, r)

    vec = _full((1, d))
    return pl.pallas_call(
        body, grid=(s // ts,), name="mix_in_bwd",
        in_specs=[_rows(ts, d), _rows(ts, 2048), _rows(ts, d), _full(w_in4.shape), vec, vec],
        out_specs=[_rows(ts, d), vec, vec, vec],
        out_shape=[_sds((s, d), F32)] + [_sds((1, d), F32)] * 3,
        compiler_params=_params(("arbitrary",)),
    )(x, dz, dx1, w_in4, g, sc)


def _wgrad(a, b, n_chunks, name, chunk_major, ts=2048):
    s, m = a.shape
    n = b.shape[1]
    nc = n // n_chunks
    nt = s // ts

    def body(a_ref, b_ref, o_ref, acc):
        i = pl.program_id(1)

        @pl.when(i == 0)
        def _():
            acc[...] = jnp.zeros_like(acc)

        acc[...] += _dot_tn(a_ref[...], b_ref[...])

        @pl.when(i == nt - 1)
        def _():
            if chunk_major:
                o_ref[0] = acc[...].astype(BF16)
            else:
                o_ref[...] = acc[...].astype(BF16)

    if chunk_major:
        out_spec, out_shape = pl.BlockSpec((1, m, nc), lambda c, i: (c, 0, 0)), _sds((n_chunks, m, nc), BF16)
    else:
        out_spec, out_shape = pl.BlockSpec((m, nc), lambda c, i: (0, c)), _sds((m, n), BF16)
    return pl.pallas_call(
        body, grid=(n_chunks, nt), name=name,
        in_specs=[pl.BlockSpec((ts, m), lambda c, i: (i, 0)), pl.BlockSpec((ts, nc), lambda c, i: (i, c))],
        out_specs=out_spec,
        out_shape=out_shape,
        scratch_shapes=[pltpu.VMEM((m, nc), F32)],
        compiler_params=_params(("parallel", "arbitrary")),
    )(a, b)


def _block_diag(w):
    heads, hd, _ = w.shape
    eye = jnp.eye(heads, dtype=w.dtype)
    return (eye[:, None, :, None] * w[:, :, None, :]).reshape(heads * hd, heads * hd)


def _diag_blocks(m):
    hd = LRU_WIDTH // LRU_HEADS
    m4 = m.reshape(LRU_HEADS, hd, LRU_HEADS, hd)
    return jnp.stack([m4[k, :, k, :] for k in range(LRU_HEADS)])


def _seq_params(small):
    row = lambda v: v.reshape(1, -1)
    pos = jnp.arange(GMLP_BLOCK)
    mask = (pos[None, :] // CHUNK) <= (pos[:, None] // CHUNK)
    ws = jnp.where(mask[None], small["w_spatial"], 0.0)
    seq_params = (small["conv_w"], row(small["conv_b"]),
                  _block_diag(small["w_rgate"]).astype(BF16), _block_diag(small["w_igate"]).astype(BF16),
                  row(small["b_rgate"]), row(small["b_igate"]), row(small["lru_a"]),
                  row(small["v_norm_g"]), row(small["v_norm_b"]), ws.astype(BF16), small["b_spatial"].T)
    return seq_params, jnp.swapaxes(ws, 1, 2).astype(BF16)


_ANY = pl.BlockSpec(memory_space=pl.ANY)
_CHIP_FLIPS = ((1, 0), (0, 1), (1, 1))


def _position():
    return lax.axis_index("x"), lax.axis_index("y"), lax.axis_index("c")


def _flip(v, f):
    return 1 - v if f else v


def _remote(src, dst, send_sem, recv_sem, peer):
    return pltpu.make_async_remote_copy(src_ref=src, dst_ref=dst, send_sem=send_sem, recv_sem=recv_sem,
                                        device_id=peer, device_id_type=MESH)


def _allgather8(block, name, reduce):
    r, n = block.shape

    def body(x_ref, out_ref, *scratch):
        if reduce:
            gath, send_sems, recv_sems, loc_sem = scratch
        else:
            gath = out_ref
            send_sems, recv_sems, loc_sem = scratch
        x, y, c = _position()
        me = 4 * x + 2 * y + c
        loc = pltpu.make_async_copy(x_ref, gath.at[me], loc_sem)
        loc.start()
        peers = []
        for k in range(1, N_DEV):
            px, py, pc = _flip(x, k & 4), _flip(y, k & 2), _flip(c, k & 1)
            peers.append((px, py, pc))
            _remote(x_ref, gath.at[me], send_sems.at[k - 1], recv_sems.at[k - 1], (px, py, pc)).start()
        for k, (px, py, pc) in enumerate(peers):
            src = 4 * px + 2 * py + pc
            _remote(x_ref, gath.at[src], send_sems.at[k], recv_sems.at[k], (px, py, pc)).wait_recv()
        for k, peer in enumerate(peers):
            _remote(x_ref, gath.at[me], send_sems.at[k], recv_sems.at[k], peer).wait_send()
        loc.wait()
        if reduce:
            acc = gath[0]
            for k in range(1, N_DEV):
                acc = acc + gath[k]
            out_ref[...] = acc

    sems = [pltpu.SemaphoreType.DMA((N_DEV - 1,)), pltpu.SemaphoreType.DMA((N_DEV - 1,)), pltpu.SemaphoreType.DMA]
    if reduce:
        out_shape = _sds((r, n), F32)
        scratch = [pltpu.VMEM((N_DEV, r, n), F32)] + sems
    else:
        out_shape = _sds((N_DEV, r, n), F32)
        scratch = sems
    return pl.pallas_call(
        body, name=name, out_shape=out_shape,
        in_specs=[pl.BlockSpec(memory_space=pltpu.VMEM)], out_specs=pl.BlockSpec(memory_space=pltpu.VMEM),
        scratch_shapes=scratch,
        compiler_params=pltpu.CompilerParams(vmem_limit_bytes=VMEM_LIMIT_BYTES),
    )(block)


def _half(ref, c, rows):
    hr = rows // 2
    return ref.at[pl.ds(pl.multiple_of(c * hr, BF16_SUBLANES), hr), :]


def _gather_weights(shards):
    na = len(shards)

    def body(*refs):
        ins, outs = refs[:na], refs[na:2 * na]
        ici_send, ici_recv, d2d_send, d2d_recv, loc_sem = refs[2 * na:]
        x, y, c = _position()
        chip = 2 * x + y
        sibling = (x, y, 1 - c)
        local = []
        for a in range(na):
            local.append(pltpu.make_async_copy(ins[a], outs[a].at[chip], loc_sem.at[a]))
            local[-1].start()
        sends = []
        for a in range(na):
            rows = shards[a].shape[0]
            for j, (fx, fy) in enumerate(_CHIP_FLIPS):
                peer = (_flip(x, fx), _flip(y, fy), c)
                sends.append(_remote(_half(ins[a], c, rows), _half(outs[a].at[chip], c, rows),
                                     ici_send.at[a * 3 + j], ici_recv.at[a * 3 + j], peer))
                sends[-1].start()
        for a in range(na):
            rows = shards[a].shape[0]
            for j, (fx, fy) in enumerate(_CHIP_FLIPS):
                src_chip = 2 * _flip(x, fx) + _flip(y, fy)
                landed = _half(outs[a].at[src_chip], c, rows)
                _remote(landed, landed, ici_send.at[a * 3 + j], ici_recv.at[a * 3 + j], sibling).wait_recv()
                sends.append(_remote(landed, landed, d2d_send.at[a * 3 + j], d2d_recv.at[a * 3 + j], sibling))
                sends[-1].start()
        for a in range(na):
            rows = shards[a].shape[0]
            for j, (fx, fy) in enumerate(_CHIP_FLIPS):
                src_chip = 2 * _flip(x, fx) + _flip(y, fy)
                other = _half(outs[a].at[src_chip], 1 - c, rows)
                _remote(other, other, d2d_send.at[a * 3 + j], d2d_recv.at[a * 3 + j], sibling).wait_recv()
        for cp in sends:
            cp.wait_send()
        for cp in local:
            cp.wait()

    return pl.pallas_call(
        body, name="gather_weights",
        out_shape=[_sds((N_CHIPS,) + w.shape, w.dtype) for w in shards],
        in_specs=[_ANY] * na, out_specs=[_ANY] * na,
        scratch_shapes=[pltpu.SemaphoreType.DMA((3 * na,))] * 4 + [pltpu.SemaphoreType.DMA((na,))],
    )(*shards)


def _swap_halves(parts, name):
    na = len(parts)

    def body(*refs):
        ins, outs = refs[:na], refs[na:2 * na]
        send_sems, recv_sems = refs[2 * na:]
        x, y, c = _position()
        sibling = (x, y, 1 - c)
        cps = []
        for a in range(na):
            hr = parts[a].shape[1] // 2
            src = ins[a].at[:, pl.ds(pl.multiple_of((1 - c) * hr, BF16_SUBLANES), hr), :]
            cps.append(_remote(src, outs[a], send_sems.at[a], recv_sems.at[a], sibling))
            cps[-1].start()
        for cp in cps:
            cp.wait()

    return pl.pallas_call(
        body, name=name,
        out_shape=[_sds((N_CHIPS, p.shape[1] // 2, p.shape[2]), p.dtype) for p in parts],
        in_specs=[_ANY] * na, out_specs=[_ANY] * na,
        scratch_shapes=[pltpu.SemaphoreType.DMA((na,))] * 2,
    )(*parts)


def _chip_sum(part, recv, pos_arr, name):
    _, rows, cols = part.shape
    hr = rows // 2

    def body(pos_ref, p_ref, r_ref, o_ref, g_ref):
        total = (p_ref[...].astype(F32) + r_ref[...].astype(F32)).astype(BF16)
        o_ref[...] = total

        @pl.when(pl.program_id(0) == pos_ref[1])
        def _():
            g_ref[0] = total

    grid_spec = pltpu.PrefetchScalarGridSpec(
        num_scalar_prefetch=1, grid=(N_CHIPS,),
        in_specs=[pl.BlockSpec((1, hr, cols), lambda k, pos: (k, pos[0], 0)),
                  pl.BlockSpec((1, hr, cols), lambda k, pos: (k, 0, 0))],
        out_specs=[pl.BlockSpec((1, hr, cols), lambda k, pos: (k, 0, 0)),
                   pl.BlockSpec((1, 1, hr, cols), lambda k, pos: (0, pos[1], 0, 0))])
    return pl.pallas_call(
        body, name=name, grid_spec=grid_spec,
        out_shape=[_sds((N_CHIPS, hr, cols), BF16), _sds((2, N_CHIPS, hr, cols), BF16)],
        compiler_params=_params(("arbitrary",)),
    )(pos_arr, part, recv)


def _exchange_chips(sums):
    na = len(sums)

    def body(*refs):
        ins, outs = refs[:na], refs[na:2 * na]
        send_sems, recv_sems, loc_sem = refs[2 * na:]
        x, y, c = _position()
        chip = 2 * x + y
        local = []
        for a in range(na):
            local.append(pltpu.make_async_copy(ins[a].at[chip], outs[a].at[chip], loc_sem.at[a]))
            local[-1].start()
        cps = []
        for a in range(na):
            for j, (fx, fy) in enumerate(_CHIP_FLIPS):
                px, py = _flip(x, fx), _flip(y, fy)
                cps.append(_remote(ins[a].at[2 * px + py], outs[a].at[chip],
                                   send_sems.at[a * 3 + j], recv_sems.at[a * 3 + j], (px, py, c)))
                cps[-1].start()
        for a in range(na):
            for j, (fx, fy) in enumerate(_CHIP_FLIPS):
                src_chip = 2 * _flip(x, fx) + _flip(y, fy)
                landed = outs[a].at[src_chip]
                _remote(landed, landed, send_sems.at[a * 3 + j], recv_sems.at[a * 3 + j], (x, y, c)).wait_recv()
        for cp in cps:
            cp.wait_send()
        for cp in local:
            cp.wait()

    return pl.pallas_call(
        body, name="exchange_chips",
        out_shape=[_sds(s.shape, s.dtype) for s in sums],
        in_specs=[_ANY] * na, out_specs=[_ANY] * na,
        scratch_shapes=[pltpu.SemaphoreType.DMA((3 * na,))] * 2 + [pltpu.SemaphoreType.DMA((na,))],
    )(*sums)


def _sum_chips(gath, name, tr=128):
    _, hr, cols = gath.shape
    tr = min(tr, hr)

    def body(g_ref, o_ref):
        acc = g_ref[0].astype(F32)
        for k in range(1, N_CHIPS):
            acc = acc + g_ref[k].astype(F32)
        o_ref[...] = acc

    return pl.pallas_call(
        body, name=name, grid=(hr // tr,),
        in_specs=[pl.BlockSpec((N_CHIPS, tr, cols), lambda i: (0, i, 0))],
        out_specs=pl.BlockSpec((tr, cols), lambda i: (i, 0)),
        out_shape=_sds((hr, cols), F32),
        compiler_params=_params(("parallel",)),
    )(gath)


def _join_halves(halves):
    na = len(halves)

    def body(*refs):
        ins, outs = refs[:na], refs[na:2 * na]
        send_sems, recv_sems, loc_sem = refs[2 * na:]
        x, y, c = _position()
        sibling = (x, y, 1 - c)
        cps, local = [], []
        for a in range(na):
            rows = 2 * halves[a].shape[0]
            mine = _half(outs[a], c, rows)
            local.append(pltpu.make_async_copy(ins[a], mine, loc_sem.at[a]))
            local[-1].start()
            cps.append(_remote(ins[a], mine, send_sems.at[a], recv_sems.at[a], sibling))
            cps[-1].start()
        for a in range(na):
            rows = 2 * halves[a].shape[0]
            other = _half(outs[a], 1 - c, rows)
            _remote(ins[a], other, send_sems.at[a], recv_sems.at[a], sibling).wait_recv()
        for cp in cps:
            cp.wait_send()
        for cp in local:
            cp.wait()

    return pl.pallas_call(
        body, name="join_halves",
        out_shape=[_sds((2 * h.shape[0], h.shape[1]), h.dtype) for h in halves],
        in_specs=[_ANY] * na, out_specs=[_ANY] * na,
        scratch_shapes=[pltpu.SemaphoreType.DMA((na,))] * 3,
    )(*halves)


_HBM = pl.BlockSpec(memory_space=pltpu.HBM)
_SEM = pl.BlockSpec(memory_space=pltpu.SEMAPHORE)
_EFFECT = pltpu.SideEffectType.DATAFLOW_SIDE_EFFECTING


def _in_hbm(a):
    return pltpu.with_memory_space_constraint(a, pltpu.HBM)


def _split_start(srcs, lands, plan, n_copies, after, name):
    ns, nl = len(srcs), len(lands)
    bufs = list(srcs) + list(lands)

    def body(*refs):
        send_sems, recv_sems = refs[ns + nl + 1], refs[ns + nl + 2]
        token = refs[-1]
        for k, (src, dst, peer) in enumerate(plan(refs[:ns], refs[ns:ns + nl])):
            _remote(src, dst, send_sems.at[k], recv_sems.at[k], peer).start()
        token[...] = jnp.zeros_like(token)

    out = pl.pallas_call(
        body, name=name,
        out_shape=(pltpu.SemaphoreType.DMA((n_copies,)), pltpu.SemaphoreType.DMA((n_copies,)),
                   *[pltpu.HBM(b.shape, b.dtype) for b in bufs], _sds((SUBLANES, 128), F32)),
        in_specs=[_HBM] * (ns + nl) + [_ANY],
        out_specs=(_SEM, _SEM, *[_HBM] * (ns + nl), pl.BlockSpec(memory_space=pltpu.VMEM)),
        input_output_aliases={i: 2 + i for i in range(ns + nl)},
        compiler_params=pltpu.CompilerParams(has_side_effects=_EFFECT),
    )(*[_in_hbm(b) for b in bufs], after)
    return out[0], out[1], list(out[2:2 + ns]), list(out[2 + ns:2 + ns + nl]), out[-1]


def _split_wait(send_sems, recv_sems, srcs, lands, plan, after, name):
    ns, nl = len(srcs), len(lands)
    bufs = list(srcs) + list(lands)

    def body(*refs):
        send_ref, recv_ref = refs[ns + nl], refs[ns + nl + 1]
        me = _position()
        for k, src, dst in plan(refs[:ns], refs[ns:ns + nl]):
            cp = _remote(src, dst, send_ref.at[k], recv_ref.at[k], me)
            cp.wait_send()
            cp.wait_recv()

    out = pl.pallas_call(
        body, name=name,
        out_shape=[pltpu.HBM(b.shape, b.dtype) for b in bufs],
        in_specs=[_HBM] * (ns + nl) + [_SEM, _SEM, _ANY],
        out_specs=[_HBM] * (ns + nl),
        input_output_aliases={i: i for i in range(ns + nl)},
        compiler_params=pltpu.CompilerParams(has_side_effects=_EFFECT),
    )(*bufs, send_sems, recv_sems, after)
    return list(out[:ns]), list(out[ns:])


def _gather_plan(rows_of):
    def start(src_refs, land_refs):
        x, y, c = _position()
        chip = 2 * x + y
        out = []
        for a, rows in enumerate(rows_of):
            mine = _half(land_refs[a].at[chip], c, rows)
            out.extend((mine, mine, (_flip(x, fx), _flip(y, fy), c)) for fx, fy in _CHIP_FLIPS)
        return out

    def wait(src_refs, land_refs):
        x, y, c = _position()
        chip = 2 * x + y
        out = []
        for a, rows in enumerate(rows_of):
            for j, (fx, fy) in enumerate(_CHIP_FLIPS):
                src_chip = 2 * _flip(x, fx) + _flip(y, fy)
                out.append((3 * a + j, _half(land_refs[a].at[chip], c, rows),
                            _half(land_refs[a].at[src_chip], c, rows)))
        return out

    return start, wait


def _forward_plan(rows_of):
    def pieces(land_refs, half):
        x, y, _ = _position()
        return [_half(land_refs[a].at[2 * _flip(x, fx) + _flip(y, fy)], half, rows)
                for a, rows in enumerate(rows_of) for fx, fy in _CHIP_FLIPS]

    def start(src_refs, land_refs):
        x, y, c = _position()
        return [(p, p, (x, y, 1 - c)) for p in pieces(land_refs, c)]

    def wait(src_refs, land_refs):
        _, _, c = _position()
        return [(k, mine, theirs)
                for k, (mine, theirs) in enumerate(zip(pieces(land_refs, c), pieces(land_refs, 1 - c)))]

    return start, wait


def _swap_halves_plan(half_rows):
    def slices(src_refs, c):
        return [src_refs[a].at[:, pl.ds(pl.multiple_of((1 - c) * hr, BF16_SUBLANES), hr), :]
                for a, hr in enumerate(half_rows)]

    def start(src_refs, land_refs):
        x, y, c = _position()
        return [(src, land_refs[a], (x, y, 1 - c)) for a, src in enumerate(slices(src_refs, c))]

    def wait(src_refs, land_refs):
        _, _, c = _position()
        return [(a, src, land_refs[a]) for a, src in enumerate(slices(src_refs, c))]

    return start, wait


def _swap_gathered_plan(n_arrays):
    def start(src_refs, land_refs):
        x, y, c = _position()
        return [(land_refs[a].at[0], land_refs[a].at[1], (x, y, 1 - c)) for a in range(n_arrays)]

    def wait(src_refs, land_refs):
        return [(a, land_refs[a].at[0], land_refs[a].at[1]) for a in range(n_arrays)]

    return start, wait


def _exchange_plan(n_arrays):
    def start(src_refs, land_refs):
        x, y, c = _position()
        chip = 2 * x + y
        out = []
        for a in range(n_arrays):
            for fx, fy in _CHIP_FLIPS:
                px, py = _flip(x, fx), _flip(y, fy)
                out.append((src_refs[a].at[2 * px + py], land_refs[a].at[0, chip], (px, py, c)))
        return out

    def wait(src_refs, land_refs):
        x, y, c = _position()
        out = []
        for a in range(n_arrays):
            for j, (fx, fy) in enumerate(_CHIP_FLIPS):
                src_chip = 2 * _flip(x, fx) + _flip(y, fy)
                out.append((3 * a + j, src_refs[a].at[src_chip], land_refs[a].at[0, src_chip]))
        return out

    return start, wait


def _forward_to_sibling(lands, name):
    na = len(lands)

    def body(*refs):
        land_refs = refs[na:2 * na]
        send_sems, recv_sems = refs[2 * na:]
        x, y, c = _position()
        sibling = (x, y, 1 - c)
        sends = []
        for a in range(na):
            rows = lands[a].shape[1]
            for j, (fx, fy) in enumerate(_CHIP_FLIPS):
                landed = _half(land_refs[a].at[2 * _flip(x, fx) + _flip(y, fy)], c, rows)
                sends.append(_remote(landed, landed, send_sems.at[3 * a + j], recv_sems.at[3 * a + j], sibling))
                sends[-1].start()
        for a in range(na):
            rows = lands[a].shape[1]
            for j, (fx, fy) in enumerate(_CHIP_FLIPS):
                other = _half(land_refs[a].at[2 * _flip(x, fx) + _flip(y, fy)], 1 - c, rows)
                _remote(other, other, send_sems.at[3 * a + j], recv_sems.at[3 * a + j], sibling).wait_recv()
        for cp in sends:
            cp.wait_send()

    return pl.pallas_call(
        body, name=name,
        out_shape=[_sds(l.shape, l.dtype) for l in lands],
        in_specs=[_ANY] * na, out_specs=[_ANY] * na,
        input_output_aliases={a: a for a in range(na)},
        scratch_shapes=[pltpu.SemaphoreType.DMA((3 * na,))] * 2,
    )(*lands)


def _swap_gathered(gath, name):
    na = len(gath)

    def body(*refs):
        gath_refs = refs[na:2 * na]
        send_sems, recv_sems = refs[2 * na:]
        x, y, c = _position()
        cps = [_remote(gath_refs[a].at[0], gath_refs[a].at[1], send_sems.at[a], recv_sems.at[a], (x, y, 1 - c))
               for a in range(na)]
        for cp in cps:
            cp.start()
        for cp in cps:
            cp.wait()

    return pl.pallas_call(
        body, name=name,
        out_shape=[_sds(g.shape, g.dtype) for g in gath],
        in_specs=[_ANY] * na, out_specs=[_ANY] * na,
        input_output_aliases={a: a for a in range(na)},
        scratch_shapes=[pltpu.SemaphoreType.DMA((na,))] * 2,
    )(*gath)


def _adam_gathered(w, gath, m, v, c_arr, name, tr=128):
    rows, cols = w.shape
    hr = rows // 2
    if hr % (2 * tr) == 0:
        tr = 2 * tr
    per = hr // tr

    def body(c_ref, w_ref, g_ref, m_ref, v_ref, go_ref, d_ref, nm_ref, nv_ref):
        g = g_ref[0, 0].astype(F32)
        for k in range(1, N_CHIPS):
            g = g + g_ref[0, k].astype(F32)
        go_ref[...] = g
        d_ref[...], nm_ref[...], nv_ref[...] = _adam_math(w_ref[...], g, m_ref[...], v_ref[...])

    def rows_of(h, i, c_ref):
        c = c_ref[0]
        return ((c + h - 2 * c * h) * per + i, 0)

    blk = pl.BlockSpec((tr, cols), rows_of)
    grid_spec = pltpu.PrefetchScalarGridSpec(
        num_scalar_prefetch=1, grid=(2, per),
        in_specs=[blk, pl.BlockSpec((1, N_CHIPS, tr, cols), lambda h, i, c_ref: (h, 0, i, 0)), blk, blk],
        out_specs=[blk] * 4)
    return pl.pallas_call(
        body, name=name, grid_spec=grid_spec, out_shape=[_sds(w.shape, F32)] * 4,
        compiler_params=_params(("arbitrary", "arbitrary")),
    )(c_arr, w, gath, m, v)


def _allreduce_small(block, name):
    two, r, n = block.shape
    assert two == 2

    def body(x_ref, out_ref, sib, chipsum, gath, d2d_send, d2d_recv, ici_send, ici_recv):
        x, y, c = _position()
        chip = 2 * x + y
        sibling = (x, y, 1 - c)
        first = _remote(x_ref, sib, d2d_send.at[0], d2d_recv.at[0], sibling)
        first.start()
        first.wait()
        chipsum[...] = x_ref[...] + sib[...]
        sends = []
        for j, (fx, fy) in enumerate(_CHIP_FLIPS):
            sends.append(_remote(chipsum.at[c], gath.at[chip], ici_send.at[j], ici_recv.at[j],
                                 (_flip(x, fx), _flip(y, fy), c)))
            sends[-1].start()
        gath[chip] = chipsum[c]
        for j, (fx, fy) in enumerate(_CHIP_FLIPS):
            landed = gath.at[2 * _flip(x, fx) + _flip(y, fy)]
            _remote(landed, landed, ici_send.at[j], ici_recv.at[j], sibling).wait_recv()
        for cp in sends:
            cp.wait_send()
        total = gath[0]
        for k in range(1, N_CHIPS):
            total = total + gath[k]
        out_ref[c] = total
        last = _remote(out_ref.at[c], out_ref.at[c], d2d_send.at[1], d2d_recv.at[1], sibling)
        last.start()
        _remote(out_ref.at[1 - c], out_ref.at[1 - c], d2d_send.at[1], d2d_recv.at[1], sibling).wait_recv()
        last.wait_send()

    vmem = pl.BlockSpec(memory_space=pltpu.VMEM)
    return pl.pallas_call(
        body, name=name, out_shape=_sds(block.shape, F32), in_specs=[vmem], out_specs=vmem,
        scratch_shapes=[pltpu.VMEM(block.shape, F32), pltpu.VMEM(block.shape, F32), pltpu.VMEM((N_CHIPS, r, n), F32),
                        pltpu.SemaphoreType.DMA((2,)), pltpu.SemaphoreType.DMA((2,)),
                        pltpu.SemaphoreType.DMA((3,)), pltpu.SemaphoreType.DMA((3,))],
        compiler_params=pltpu.CompilerParams(vmem_limit_bytes=VMEM_LIMIT_BYTES),
    )(block)


def _cast_place(shards, chip_arr):
    na = len(shards)
    steps = 4

    def body(chip_ref, *refs):
        for a in range(na):
            refs[na + a][0] = refs[a][...].astype(BF16)

    grid_spec = pltpu.PrefetchScalarGridSpec(
        num_scalar_prefetch=1, grid=(steps,),
        in_specs=[pl.BlockSpec((s.shape[0] // steps, s.shape[1]), lambda i, ch: (i, 0)) for s in shards],
        out_specs=[pl.BlockSpec((1, s.shape[0] // steps, s.shape[1]), lambda i, ch: (ch[0], i, 0)) for s in shards])
    return pl.pallas_call(
        body, name="cast_place", grid_spec=grid_spec,
        out_shape=[_sds((N_CHIPS,) + s.shape, BF16) for s in shards],
        compiler_params=_params(("arbitrary",)),
    )(chip_arr, *shards)


def _silu(v):
    return v * _sigmoid(v)


def _ada_fwd(c8, w_ada):
    def body(c_ref, w_ref, o_ref):
        o_ref[...] = jnp.dot(_silu(c_ref[...]), w_ref[...], preferred_element_type=F32,
                             precision=lax.Precision.HIGHEST)

    return pl.pallas_call(
        body, name="ada_fwd", out_shape=_sds((N_DEV, w_ada.shape[1]), F32),
        compiler_params=pltpu.CompilerParams(vmem_limit_bytes=VMEM_LIMIT_BYTES),
    )(c8, w_ada)


def _mod_select(parts, b_ada, me_arr, after):
    cols = parts.shape[2]

    def body(me_ref, p_ref, b_ref, after_ref, o_ref):
        me = me_ref[0]
        for k in range(N_CHIPS):
            cs = slice(k * cols, (k + 1) * cols)
            o_ref[:, cs] = p_ref[2 * k, pl.ds(me, 1), :] + b_ref[:, cs]

    grid_spec = pltpu.PrefetchScalarGridSpec(
        num_scalar_prefetch=1, grid=(1,),
        in_specs=[pl.BlockSpec(parts.shape, lambda i, m: (0, 0, 0)), pl.BlockSpec(b_ada.shape, lambda i, m: (0, 0)),
                  _ANY],
        out_specs=pl.BlockSpec(b_ada.shape, lambda i, m: (0, 0)))
    return pl.pallas_call(body, name="mod_select", grid_spec=grid_spec, out_shape=_sds(b_ada.shape, F32))(
        me_arr, parts, b_ada, after)


def _ada_bwd(c8, dmod8, chip_arr, w, m, v, tr=256):
    d = c8.shape[1]
    cols = dmod8.shape[1] // N_CHIPS

    def body(chip_ref, c_ref, dm_ref, dmall_ref, w_ref, m_ref, v_ref, gw_ref, d_ref, nm_ref, nv_ref, gb_ref):
        g = lax.dot_general(_silu(c_ref[...]), dm_ref[...], (((0,), (0,)), ((), ())),
                            preferred_element_type=F32, precision=lax.Precision.HIGHEST)
        gw_ref[...] = g
        d_ref[...], nm_ref[...], nv_ref[...] = _adam_math(w_ref[...], g, m_ref[...], v_ref[...])
        acc = dmall_ref[0:1, :]
        for k in range(1, N_DEV):
            acc = acc + dmall_ref[k:k + 1, :]
        gb_ref[...] = acc

    rows = pl.BlockSpec((tr, cols), lambda i, ch: (i, 0))
    grid_spec = pltpu.PrefetchScalarGridSpec(
        num_scalar_prefetch=1, grid=(d // tr,),
        in_specs=[pl.BlockSpec((N_DEV, tr), lambda i, ch: (0, i)),
                  pl.BlockSpec((N_DEV, cols), lambda i, ch: (0, ch[0])),
                  pl.BlockSpec(dmod8.shape, lambda i, ch: (0, 0)), rows, rows, rows],
        out_specs=[rows] * 4 + [pl.BlockSpec((1, dmod8.shape[1]), lambda i, ch: (0, 0))])
    return pl.pallas_call(
        body, name="ada_bwd", grid_spec=grid_spec,
        out_shape=[_sds((d, cols), F32)] * 4 + [_sds((1, dmod8.shape[1]), F32)],
        compiler_params=_params(("arbitrary",)),
    )(chip_arr, c8, dmod8, dmod8, w, m, v)


def _adam_math(w, g, m, v):
    m = ADAM_B1 * m + (1.0 - ADAM_B1) * g
    v = ADAM_B2 * v + (1.0 - ADAM_B2) * (g * g)
    m_hat = m / (1.0 - ADAM_B1 ** ADAM_STEP)
    v_hat = v / (1.0 - ADAM_B2 ** ADAM_STEP)
    delta = -ADAM_LR * (m_hat / (jnp.sqrt(v_hat) + ADAM_EPS) + ADAM_WD * w)
    return delta, m, v


def _adam(w, g, m, v, name, tr=256):
    rows, cols = w.shape
    if rows % tr:
        tr = rows

    def body(w_ref, g_ref, m_ref, v_ref, d_ref, nm_ref, nv_ref):
        d_ref[...], nm_ref[...], nv_ref[...] = _adam_math(w_ref[...], g_ref[...], m_ref[...], v_ref[...])

    spec = pl.BlockSpec((tr, cols), lambda i: (i, 0))
    return pl.pallas_call(
        body, name=name, grid=(rows // tr,), in_specs=[spec] * 4, out_specs=[spec] * 3,
        out_shape=[_sds(w.shape, F32)] * 3, compiler_params=_params(("parallel",)),
    )(w, g, m, v)


def _adam_cols(w, g_full, m, v, chip_arr, name):
    rows, cols = w.shape

    def body(chip_ref, w_ref, g_ref, m_ref, v_ref, gs_ref, d_ref, nm_ref, nv_ref):
        g = g_ref[...]
        gs_ref[...] = g
        d_ref[...], nm_ref[...], nv_ref[...] = _adam_math(w_ref[...], g, m_ref[...], v_ref[...])

    own = pl.BlockSpec((rows, cols), lambda i, ch: (0, 0))
    grid_spec = pltpu.PrefetchScalarGridSpec(
        num_scalar_prefetch=1, grid=(1,),
        in_specs=[own, pl.BlockSpec((rows, cols), lambda i, ch: (0, ch[0])), own, own],
        out_specs=[own] * 4)
    return pl.pallas_call(body, name=name, grid_spec=grid_spec, out_shape=[_sds(w.shape, F32)] * 4)(
        chip_arr, w, g_full, m, v)


PACK_COLS = 512
SMALL_REPLICATED = ("g_mix_pre", "g_mix_post", "conv_b", "w_rgate", "b_rgate", "w_igate", "b_igate", "lru_a",
                    "v_norm_g", "v_norm_b", "w_spatial", "b_spatial", "g_lru_out", "g_gmlp_out", "g_ffn_pre",
                    "g_ffn_post", "ffn_conv_b")
SMALL_COLUMN_SHARDED = ("conv_w", "ffn_conv_w")


def _pack(arrays):
    flat = jnp.concatenate([a.reshape(1, -1) for a in arrays], axis=1)
    pad = (-flat.shape[1]) % (2 * LANES)
    if pad:
        flat = jnp.pad(flat, ((0, 0), (0, pad)))
    return flat.reshape(2, -1)


def _unpack(packed, shapes):
    flat = packed.reshape(1, -1)
    out, col = [], 0
    for shape in shapes:
        n = math.prod(shape)
        out.append(flat[:, col:col + n].reshape(shape))
        col += n
    return out


SMALL_ROW_LEN = 86016
_SMALL_ROWS = (
    (("ffn_conv_w", 18432), ("conv_w", 2048), ("w_spatial", 65536)),
    (("w_rgate", 32768), ("w_igate", 32768), ("ffn_conv_b", 6144), ("g_mix_pre", 1024), ("g_mix_post", 1024),
     ("g_ffn_pre", 1024), ("g_ffn_post", 1024), ("conv_b", 512), ("b_rgate", 512), ("b_igate", 512),
     ("lru_a", 512), ("v_norm_g", 512), ("v_norm_b", 512), ("b_spatial", 512), ("g_lru_out", 512),
     ("g_gmlp_out", 512), ("loss", 128)),
)


def _small_slots():
    slots = {}
    for row, entries in enumerate(_SMALL_ROWS):
        off = 0
        for name, size in entries:
            slots[name] = (row, off)
            off += size
        assert off <= SMALL_ROW_LEN
    return slots


SMALL_SLOT = _small_slots()
SMALL_LANES = SMALL_ROW_LEN // SUBLANES


def _small_pieces(name, first, count):
    row, off = SMALL_SLOT[name]
    pos, pieces = off + first, []
    while count:
        sub, lane = divmod(pos, SMALL_LANES)
        n = min(count, SMALL_LANES - lane)
        pieces.append((row, sub, lane, n))
        pos, count = pos + n, count - n
    return pieces
ROW_VECTORS = ("ffn_conv_b", "g_mix_pre", "g_mix_post", "g_ffn_pre", "g_ffn_post", "conv_b", "lru_a", "v_norm_g",
               "v_norm_b", "g_lru_out", "g_gmlp_out")
HEAD_DIM = LRU_WIDTH // LRU_HEADS


def _pack_small(g):
    order = ("ffn_conv_w", "conv_w", "w_spatial", "w_rgate", "w_igate", "b_rgate", "b_igate", "b_spatial", "loss") \
        + ROW_VECTORS
    vmem = pl.BlockSpec(memory_space=pltpu.VMEM)

    def body(*refs):
        src = dict(zip(order, refs))
        out_ref = refs[len(order)]
        out_ref[...] = jnp.zeros_like(out_ref)

        def put(name, first, val):
            col = 0
            for row, sub, lane, n in _small_pieces(name, first, val.shape[1]):
                out_ref[row, sub:sub + 1, lane:lane + n] = val[:, col:col + n]
                col += n

        for name in ROW_VECTORS + ("b_rgate", "b_igate", "loss"):
            put(name, 0, src[name][...])
        for name in ("ffn_conv_w", "conv_w"):
            k_taps, n = src[name].shape
            for k in range(k_taps):
                put(name, k * n, src[name][k:k + 1, :])
        for g_idx in range(GMLP_GROUPS):
            for i in range(GMLP_BLOCK):
                put("w_spatial", (g_idx * GMLP_BLOCK + i) * GMLP_BLOCK, src["w_spatial"][g_idx, i:i + 1, :])
        for name in ("w_rgate", "w_igate"):
            for h in range(LRU_HEADS):
                for i in range(HEAD_DIM):
                    r = h * HEAD_DIM + i
                    put(name, r * HEAD_DIM, src[name][r:r + 1, h * HEAD_DIM:(h + 1) * HEAD_DIM])
        eye = (lax.broadcasted_iota(jnp.int32, (GMLP_BLOCK, GMLP_BLOCK), 0)
               == lax.broadcasted_iota(jnp.int32, (GMLP_BLOCK, GMLP_BLOCK), 1))
        for g_idx in range(GMLP_GROUPS):
            col = src["b_spatial"][:, g_idx:g_idx + 1]
            put("b_spatial", g_idx * GMLP_BLOCK, _colsum(jnp.where(eye, col, 0.0)))

    return pl.pallas_call(
        body, name="pack_small", out_shape=_sds((2, SUBLANES, SMALL_LANES), F32),
        in_specs=[vmem] * len(order), out_specs=vmem,
        compiler_params=pltpu.CompilerParams(vmem_limit_bytes=VMEM_LIMIT_BYTES),
    )(*[g[n] for n in order])


def _adam_small(g_small, w, m, v):
    vmem = pl.BlockSpec(memory_space=pltpu.VMEM)
    n_p = len(SMALL_REPLICATED)

    def body(g_ref, *refs):
        w_refs, m_refs, v_refs = refs[:n_p], refs[n_p:2 * n_p], refs[2 * n_p:3 * n_p]
        outs = refs[3 * n_p:]
        go, do, mo, vo = outs[:n_p], outs[n_p:2 * n_p], outs[2 * n_p:3 * n_p], outs[3 * n_p:]
        for k, name in enumerate(SMALL_REPLICATED):
            def take(first, count, name=name):
                parts = [g_ref[row, sub:sub + 1, lane:lane + n]
                         for row, sub, lane, n in _small_pieces(name, first, count)]
                return parts[0] if len(parts) == 1 else jnp.concatenate(parts, axis=1)

            shape = w_refs[k].shape
            if name in ROW_VECTORS:
                go[k][...] = take(0, shape[1])
            elif name in ("b_rgate", "b_igate"):
                for h in range(LRU_HEADS):
                    go[k][0, h:h + 1, :] = take(h * HEAD_DIM, HEAD_DIM)
            elif name == "b_spatial":
                for g_idx in range(GMLP_GROUPS):
                    go[k][0, g_idx:g_idx + 1, :] = take(g_idx * GMLP_BLOCK, GMLP_BLOCK)
            elif name == "w_spatial":
                for g_idx in range(GMLP_GROUPS):
                    for i in range(GMLP_BLOCK):
                        go[k][0, g_idx, i:i + 1, :] = take((g_idx * GMLP_BLOCK + i) * GMLP_BLOCK, GMLP_BLOCK)
            else:
                for h in range(LRU_HEADS):
                    for i in range(HEAD_DIM):
                        go[k][0, h, i:i + 1, :] = take((h * HEAD_DIM + i) * HEAD_DIM, HEAD_DIM)
            do[k][...], mo[k][...], vo[k][...] = _adam_math(w_refs[k][...], go[k][...], m_refs[k][...],
                                                             v_refs[k][...])

    names = SMALL_REPLICATED
    out_shape = [_sds(w[n].shape, F32) for n in names] * 4
    res = pl.pallas_call(
        body, name="adam_small", out_shape=out_shape,
        in_specs=[vmem] * (1 + 3 * n_p), out_specs=[vmem] * (4 * n_p),
        compiler_params=pltpu.CompilerParams(vmem_limit_bytes=VMEM_LIMIT_BYTES),
    )(g_small, *[w[n] for n in names], *[m[n] for n in names], *[v[n] for n in names])
    return [dict(zip(names, res[k * n_p:(k + 1) * n_p])) for k in range(4)]


def _adam_cols(name, g_small, w, m, v, chip_arr):
    _, k_taps, n = w.shape
    row, off = SMALL_SLOT[name]
    first = off // n
    per_sub = SMALL_LANES // n

    def body(chip_ref, *refs):
        g_refs = refs[:k_taps]
        w_ref, m_ref, v_ref, go_ref, d_ref, nm_ref, nv_ref = refs[k_taps:]
        for k in range(k_taps):
            tap = (0, slice(k, k + 1), slice(None))
            sub = (first + N_CHIPS * k + chip_ref[0]) // per_sub
            g = g_refs[k][row, pl.ds(sub, 1), :]
            go_ref[tap] = g
            d_ref[tap], nm_ref[tap], nv_ref[tap] = _adam_math(w_ref[tap], g, m_ref[tap], v_ref[tap])

    whole = pl.BlockSpec(w.shape, lambda i, ch: (0, 0, 0))
    taps = [pl.BlockSpec((2, SUBLANES, n),
                         functools.partial(lambda i, ch, k: (0, 0, (first + N_CHIPS * k + ch[0]) % per_sub), k=k))
            for k in range(k_taps)]
    grid_spec = pltpu.PrefetchScalarGridSpec(
        num_scalar_prefetch=1, grid=(1,), in_specs=taps + [whole] * 3, out_specs=[whole] * 4)
    return pl.pallas_call(body, name="adam_" + name, grid_spec=grid_spec, out_shape=[_sds(w.shape, F32)] * 4)(
        chip_arr, *[g_small] * k_taps, w, m, v)


def kernel(x, c, w_ada, b_ada, g_mix_pre, g_mix_post, w_in, conv_w, conv_b, w_rgate, b_rgate, w_igate, b_igate, lru_a, v_norm_g, v_norm_b, w_spatial, b_spatial, g_lru_out, g_gmlp_out, w_out, g_ffn_pre, g_ffn_post, w_up, ffn_conv_w, ffn_conv_b, w_down, loss_target, m_w_ada, m_b_ada, m_g_mix_pre, m_g_mix_post, m_w_in, m_conv_w, m_conv_b, m_w_rgate, m_b_rgate, m_w_igate, m_b_igate, m_lru_a, m_v_norm_g, m_v_norm_b, m_w_spatial, m_b_spatial, m_g_lru_out, m_g_gmlp_out, m_w_out, m_g_ffn_pre, m_g_ffn_post, m_w_up, m_ffn_conv_w, m_ffn_conv_b, m_w_down, v_w_ada, v_b_ada, v_g_mix_pre, v_g_mix_post, v_w_in, v_conv_w, v_conv_b, v_w_rgate, v_b_rgate, v_w_igate, v_b_igate, v_lru_a, v_v_norm_g, v_v_norm_b, v_w_spatial, v_b_spatial, v_g_lru_out, v_g_gmlp_out, v_w_out, v_g_ffn_pre, v_g_ffn_post, v_w_up, v_ffn_conv_w, v_ffn_conv_b, v_w_down):
    args = dict(locals())
    names = ("w_ada", "b_ada", "g_mix_pre", "g_mix_post", "w_in", "conv_w", "conv_b", "w_rgate", "b_rgate",
             "w_igate", "b_igate", "lru_a", "v_norm_g", "v_norm_b", "w_spatial", "b_spatial", "g_lru_out",
             "g_gmlp_out", "w_out", "g_ffn_pre", "g_ffn_post", "w_up", "ffn_conv_w", "ffn_conv_b", "w_down")
    drop = lambda a: a if a.ndim == 2 else a[0]
    w = {n: drop(args[n]) for n in names}
    m = {n: drop(args["m_" + n]) for n in names}
    v = {n: drop(args["v_" + n]) for n in names}
    xi, yi, ci = _position()
    me_arr = jnp.reshape(4 * xi + 2 * yi + ci, (1,)).astype(jnp.int32)
    chip_arr = jnp.reshape(2 * xi + yi, (1,)).astype(jnp.int32)
    c_arr = jnp.reshape(ci, (1,)).astype(jnp.int32)
    pos_arr = jnp.stack([ci, 2 * xi + yi]).astype(jnp.int32)

    big = ("w_in", "w_out", "w_up", "w_down")
    lands = _cast_place([w[n] for n in big], chip_arr)
    start_a, wait_a = _gather_plan([w[n].shape[0] for n in big[:2]])
    start_b, wait_b = _gather_plan([w[n].shape[0] for n in big[2:]])

    row0 = jnp.concatenate([c, w["conv_w"].reshape(1, -1), w["ffn_conv_w"].reshape(1, -1)], axis=1)
    g0 = _allgather8(row0, "gather_cond", False)[:, 0, :]
    c8 = g0[:, :D_MODEL]
    per_chip = g0[0::2]
    conv_w_full = per_chip[:, D_MODEL:D_MODEL + 512].reshape(N_CHIPS, 4, 128).transpose(1, 0, 2).reshape(4, 512)
    ffn_conv_w_full = per_chip[:, D_MODEL + 512:].reshape(N_CHIPS, 3, 1536).transpose(1, 0, 2).reshape(3, 2 * D_FF)
    mod_parts = _allgather8(_ada_fwd(c8, w["w_ada"]), "gather_mod", False)
    send_a, recv_a, _, lands_a, token_a = _split_start([], lands[:2], start_a, 6, mod_parts, "gather_start_a")
    send_b, recv_b, _, lands_b, token_b = _split_start([], lands[2:], start_b, 6, token_a, "gather_start_b")
    mod = _mod_select(mod_parts, w["b_ada"].reshape(1, -1), me_arr, token_b).reshape(N_MOD, D_MODEL)
    sh_m, sc_m, gt_m, sh_f, sc_f, gt_f = [mod[k:k + 1] for k in range(N_MOD)]

    small = {n: w[n] for n in SMALL_REPLICATED}
    small["conv_w"] = conv_w_full
    small["ffn_conv_w"] = ffn_conv_w_full
    row = lambda a: a.reshape(1, -1)
    seq_params, ws_t = _seq_params(small)
    glo, ggo = row(small["g_lru_out"]), row(small["g_gmlp_out"])
    g_pre, g_post = row(small["g_mix_pre"]), row(small["g_mix_post"])
    g_pre2, g_post2 = row(small["g_ffn_pre"]), row(small["g_ffn_post"])
    fw, fb = small["ffn_conv_w"], row(small["ffn_conv_b"])
    xs, tgt = x[0], loss_target[0]

    _, lands_a = _split_wait(send_a, recv_a, [], lands_a, wait_a, mod, "gather_wait_a")
    w_in4, w_out4 = _forward_to_sibling(lands_a, "forward_a")
    w_out_b = w_out4.reshape(D_MODEL, D_MODEL)
    z, h = _mix_in(xs, sc_m, sh_m, g_pre, w_in4)
    ycat, hst, stash = _seqmix(z, seq_params, glo, ggo)
    _, lands_b = _split_wait(send_b, recv_b, [], lands_b, wait_b, ycat, "gather_wait_b")
    fwd_start, fwd_wait = _forward_plan([w[n].shape[0] for n in big[2:]])
    fwd_send, fwd_recv, _, lands_b, tok = _split_start([], lands_b, fwd_start, 6, pos_arr, "forward_start_b")
    y, x1, h2 = _mix_out(ycat, xs, w_out_b, gt_m + tok[0:1, 0:1], g_post, g_pre2, sc_f, sh_f)
    _, (w_up4, w_down4) = _split_wait(fwd_send, fwd_recv, [], lands_b, fwd_wait, h2, "forward_wait_b")
    w_down_b = w_down4.reshape(D_FF, D_MODEL)
    up0, pre, act, dy2, dx2, loss, dgt_f, dg_post2 = _ffn_fwd(h2, x1, tgt, w_up4, w_down_b, fw, fb, gt_f, g_post2)

    dup0, dfw, dfb = _ffn_bwd_a(dy2, pre, up0, w_down_b, fw)
    gw_up = _wgrad(h2, dup0, N_CHIPS, "wgrad_up", True)
    gw_down = _wgrad(act, dy2, 2, "wgrad_down", False)
    ex_start, ex_wait = _exchange_plan(2)
    sg_start, sg_wait = _swap_gathered_plan(2)
    grads, deltas, new_m, new_v = {}, {}, {}, {}

    def swap_start(parts, name):
        sw_start, sw_wait = _swap_halves_plan([p.shape[1] // 2 for p in parts])
        recv = [lax.empty((N_CHIPS, p.shape[1] // 2, p.shape[2]), BF16) for p in parts]
        send_s, recv_s, parts, recv, token = _split_start(parts, recv, sw_start, len(parts), pos_arr,
                                                           "swap_start_" + name)
        return (send_s, recv_s, parts, recv, sw_wait), token

    def exchange_start(swap, tags, after, name):
        send_s, recv_s, parts, recv, sw_wait = swap
        parts, recv = _split_wait(send_s, recv_s, parts, recv, sw_wait, after, "swap_wait_" + name)
        both = [_chip_sum(p, r, pos_arr, "chip_sum_" + t) for p, r, t in zip(parts, recv, tags)]
        sums, gath = [b[0] for b in both], [b[1] for b in both]
        return _split_start(sums, gath, ex_start, 3 * len(parts), pos_arr, "exchange_start_" + name)

    def gathered_start(exchange, after, name):
        send_s, recv_s, sums, gath, _ = exchange
        _, gath = _split_wait(send_s, recv_s, sums, gath, ex_wait, after, "exchange_wait_" + name)
        send_s, recv_s, _, gath, token = _split_start([], gath, sg_start, len(gath), pos_arr,
                                                      "gathered_start_" + name)
        return (send_s, recv_s, gath), token

    def finish(gathered, tags, after, name):
        send_s, recv_s, gath = gathered
        _, gath = _split_wait(send_s, recv_s, [], gath, sg_wait, after, "gathered_wait_" + name)
        for g, t in zip(gath, tags):
            grads[t], deltas[t], new_m[t], new_v[t] = _adam_gathered(w[t], g, m[t], v[t], c_arr, "adam_" + t)

    def behind(value, token):
        return value + token[0:1, 0:1]

    tags_b, tags_a = ("w_up", "w_down"), ("w_in", "w_out")
    swap_b, tok = swap_start([gw_up, gw_down.reshape(N_CHIPS, -1, D_MODEL)], "b")
    dx1, dy, dsh_f, dsc_f, dg_pre2, dgt_m, dg_post = _ffn_bwd_b(
        dup0, x1, y, dx2, w_up4, g_pre2, behind(sc_f, tok), sh_f, gt_m, g_post)
    exchange_b = exchange_start(swap_b, tags_b, dg_post, "b")
    (dz, dcw, dcb, dwr, dwi, dbr, dbi, dspa, dng, dnb, dws, dbs_t, dglo, dggo) = _seqmix_bwd(
        z, hst, stash, dy, w_out_b, seq_params, ws_t, behind(glo, exchange_b[4]), ggo)
    grad_x, dsh_m, dsc_m, dg_pre = _mix_in_bwd(xs, dz, dx1, w_in4, g_pre, sc_m)
    gw_in = _wgrad(h, dz, N_CHIPS, "wgrad_in", True)
    gw_out = _wgrad(ycat, dy, 1, "wgrad_out", False)
    swap_a, tok = swap_start([gw_in, gw_out.reshape(N_CHIPS, -1, D_MODEL)], "a")

    dmod = jnp.concatenate([behind(dsh_m, tok), dsc_m, dgt_m, dsh_f, dsc_f, dgt_f], axis=1)
    dmod8 = _allgather8(dmod, "gather_dmod", False)[:, 0, :]
    grads["w_ada"], deltas["w_ada"], new_m["w_ada"], new_v["w_ada"], g_b_ada = _ada_bwd(
        c8, dmod8, chip_arr, w["w_ada"], m["w_ada"], v["w_ada"])
    exchange_a = exchange_start(swap_a, tags_a, g_b_ada, "a")
    gathered_b, tok = gathered_start(exchange_b, exchange_a[4], "b")

    small_grads = dict(
        g_mix_pre=dg_pre, g_mix_post=dg_post, conv_w=dcw, conv_b=dcb, w_rgate=dwr, b_rgate=dbr, w_igate=dwi,
        b_igate=dbi, lru_a=dspa, v_norm_g=dng, v_norm_b=dnb, w_spatial=dws, b_spatial=dbs_t, g_lru_out=dglo,
        g_gmlp_out=dggo, g_ffn_pre=dg_pre2, g_ffn_post=dg_post2, ffn_conv_w=dfw, ffn_conv_b=dfb,
        loss=behind(loss, tok))
    g_small = _allreduce_small(_pack_small(small_grads), "reduce_small")
    total = g_small[_small_pieces("loss", 0, 1)[0][:3]]

    rep = SMALL_REPLICATED
    small_out = _adam_small(g_small, {n: args[n] for n in rep}, {n: args["m_" + n] for n in rep},
                            {n: args["v_" + n] for n in rep})
    for n in rep:
        grads[n], deltas[n], new_m[n], new_v[n] = [group[n] for group in small_out]
    finish(gathered_b, tags_b, deltas[rep[0]], "b")

    gathered_a, tok = gathered_start(exchange_a, deltas["w_down"], "a")
    for n in SMALL_COLUMN_SHARDED:
        grads[n], deltas[n], new_m[n], new_v[n] = _adam_cols(n, g_small, args[n], args["m_" + n],
                                                             behind(args["v_" + n], tok), chip_arr)
    d_b, m_b, v_b = _adam(w["b_ada"], g_b_ada, m["b_ada"], behind(v["b_ada"], tok), "adam_b_ada")
    grads["b_ada"], deltas["b_ada"], new_m["b_ada"], new_v["b_ada"] = g_b_ada, d_b, m_b, v_b
    finish(gathered_a, tags_a, d_b, "a")

    outs = [total, grad_x[None]]
    for group in (grads, deltas, new_m, new_v):
        outs.extend(group[n].reshape(args[n].shape) for n in names)
    return tuple(outs)
```

```python
import functools
import math

import jax
import jax.numpy as jnp
from jax import lax
from jax.experimental import pallas as pl
from jax.experimental.pallas import tpu as pltpu

F32 = jnp.float32
BF16 = jnp.bfloat16
MESH = pl.DeviceIdType.MESH

D_MODEL = 1024
LRU_WIDTH = 512
LRU_HEADS = 8
GMLP_WIDTH = 512
GMLP_GROUPS = 4
GMLP_BLOCK = 128
CHUNK = 64
D_FF = 3072
N_MOD = 6
EPS = 1e-6
LRU_C = 8.0
N_CHIPS = 4
N_DEV = 8

ADAM_LR = 0.001
ADAM_B1 = 0.9
ADAM_B2 = 0.999
ADAM_EPS = 1e-08
ADAM_WD = 0.01
ADAM_STEP = 10

GELU_C0 = math.sqrt(2.0 / math.pi)
GELU_C1 = 0.044715

VMEM_LIMIT_BYTES = 56 * 1024 * 1024
SUBLANES = 8
LANES = 128
BF16_SUBLANES = 16
FFN_CHUNK = 768
SUB_ROWS = 256


def _gelu_gate(x):
    x2 = x * x
    z = x * ((2.0 * GELU_C0 * GELU_C1) * x2 + 2.0 * GELU_C0)
    return 1.0 / (1.0 + jnp.exp(-z)), x2


def _gelu(x):
    t = jnp.tanh(GELU_C0 * (x + GELU_C1 * x * x * x))
    return 0.5 * x * (1.0 + t)


def _gelu_and_grad(x):
    s, x2 = _gelu_gate(x)
    g = x * s
    dz = (6.0 * GELU_C0 * GELU_C1) * x2 + 2.0 * GELU_C0
    return g, s + g * (1.0 - s) * dz


def _sigmoid(x):
    return 1.0 / (1.0 + jnp.exp(-x))


def _log1p(u):
    w = 1.0 + u
    return jnp.where(w == 1.0, u, jnp.log(w) * (u / (w - 1.0)))


def _softplus(x):
    return jnp.maximum(x, 0.0) + _log1p(jnp.exp(-jnp.abs(x)))


def _neg_expm1(x):
    u = jnp.exp(x)
    um1 = u - 1.0
    tiny = um1 == 0.0
    small = um1 * (x / jnp.log(jnp.where(tiny, 2.0, jnp.maximum(u, 0.25))))
    return -jnp.where(tiny, x, jnp.where(x < -1.0, um1, small))


def _msq_rsqrt(v):
    return lax.rsqrt(jnp.mean(v * v, axis=-1, keepdims=True) + EPS)


def _rms_bwd(dyn, yn, r):
    return r * (dyn - yn * jnp.mean(dyn * yn, axis=-1, keepdims=True))


def _colsum(v):
    return jnp.sum(v, axis=0, keepdims=True)


def _shift_down(cur, prev8, k):
    rolled = pltpu.roll(cur, k, 0)
    head = pltpu.roll(prev8, k, 0)
    row8 = lax.broadcasted_iota(jnp.int32, (SUBLANES, cur.shape[1]), 0)
    first = jnp.where(row8 < k, head, rolled[0:SUBLANES])
    return jnp.concatenate([first, rolled[SUBLANES:]], axis=0)


def _shift_up(cur, next8, k):
    t = cur.shape[0]
    rolled = pltpu.roll(cur, t - k, 0)
    tail = pltpu.roll(next8, SUBLANES - k, 0)
    row8 = lax.broadcasted_iota(jnp.int32, (SUBLANES, cur.shape[1]), 0)
    last = jnp.where(row8 >= SUBLANES - k, tail, rolled[t - SUBLANES:])
    return jnp.concatenate([rolled[:t - SUBLANES], last], axis=0)


def _scan_fwd(a, b):
    t = a.shape[0]
    row = lax.broadcasted_iota(jnp.int32, a.shape, 0)
    d = 1
    while d < t:
        keep = row >= d
        a_s = jnp.where(keep, pltpu.roll(a, d, 0), 1.0)
        b_s = jnp.where(keep, pltpu.roll(b, d, 0), 0.0)
        b = a * b_s + b
        a = a * a_s
        d *= 2
    return a, b


def _scan_bwd(a, g):
    t = a.shape[0]
    row = lax.broadcasted_iota(jnp.int32, a.shape, 0)
    d = 1
    while d < t:
        keep = row < t - d
        a_s = jnp.where(keep, pltpu.roll(a, t - d, 0), 1.0)
        g_s = jnp.where(keep, pltpu.roll(g, t - d, 0), 0.0)
        g = a * g_s + g
        a = a * a_s
        d *= 2
    return a, g


def _dot(a, b):
    return jnp.dot(a, b, preferred_element_type=F32)


def _dot_nt(a, b):
    return lax.dot_general(a, b, (((1,), (1,)), ((), ())), preferred_element_type=F32)


def _dot_tn(a, b):
    return lax.dot_general(a, b, (((0,), (0,)), ((), ())), preferred_element_type=F32)


def _rows(ts, cols, rev_of=None):
    if rev_of is None:
        return pl.BlockSpec((ts, cols), lambda i: (i, 0))
    return pl.BlockSpec((ts, cols), lambda i: (rev_of - 1 - i, 0))


def _halo_prev(ts, cols, halo, rev_of=None, col_block=0):
    per = ts // halo
    if rev_of is None:
        return pl.BlockSpec((halo, cols), lambda i: (jnp.maximum(i * per - 1, 0), col_block))
    return pl.BlockSpec((halo, cols), lambda i: (jnp.maximum((rev_of - 1 - i) * per - 1, 0), col_block))


def _full(shape):
    nd = len(shape)
    return pl.BlockSpec(shape, lambda *_: (0,) * nd)


_RESIDENT = pl.BlockSpec(memory_space=pltpu.VMEM)


def _params(sem):
    return pltpu.CompilerParams(dimension_semantics=sem, vmem_limit_bytes=VMEM_LIMIT_BYTES)


def _sds(shape, dtype):
    return jax.ShapeDtypeStruct(shape, dtype)


def _sub_tiles(ts):
    return [slice(r0, r0 + SUB_ROWS) for r0 in range(0, ts, SUB_ROWS)]


def _mix_in(x, sc, sh, g, w_in4, ts=512):
    s, d = x.shape

    def body(x_ref, sc_ref, sh_ref, g_ref, w_ref, z_ref, h_ref):
        for rs in _sub_tiles(ts):
            xv = x_ref[rs, :]
            h = (xv * _msq_rsqrt(xv) * g_ref[...]) * (1.0 + sc_ref[...]) + sh_ref[...]
            hb = h.astype(BF16)
            h_ref[rs, :] = hb
            for k in range(N_CHIPS):
                z_ref[rs, k * 512:(k + 1) * 512] = _dot(hb, w_ref[k])

    return pl.pallas_call(
        body, grid=(s // ts,), name="mix_in",
        in_specs=[_rows(ts, d), _full((1, d)), _full((1, d)), _full((1, d)), _full(w_in4.shape)],
        out_specs=[_rows(ts, 2048), _rows(ts, d)],
        out_shape=[_sds((s, 2048), F32), _sds((s, d), BF16)],
        compiler_params=_params(("parallel",)),
    )(x, sc, sh, g, w_in4)


N_STASH = 12
(ST_XC, ST_R, ST_IG, ST_A, ST_MULT, ST_GL, ST_DGL, ST_U, ST_DU, ST_Q, ST_VHAT, ST_SPB) = range(N_STASH)


def _seq_param_specs():
    return [_full((4, 512)), _full((1, 512)), _full((512, 512)), _full((512, 512)), _full((1, 512)),
            _full((1, 512)), _full((1, 512)), _full((1, 512)), _full((1, 512)), _full((4, 128, 128)),
            _full((128, 4))]


def _seqmix(z, seq_params, glo, ggo, ts=256):
    s = z.shape[0]
    nt = s // ts

    def body(z_ref, zprev_ref, cw_ref, cb_ref, bdr_ref, bdi_ref, br_ref, bi_ref, la_ref, ng_ref, nb_ref,
             ws_ref, bst_ref, glo_ref, ggo_ref, ycat_ref, hst_ref, st_ref, hcarry, sp_scr):
        i = pl.program_id(0)

        @pl.when(i == 0)
        def _():
            hcarry[...] = jnp.zeros_like(hcarry)

        lx = z_ref[:, 0:512]
        prev8 = jnp.where(i == 0, 0.0, zprev_ref[...])
        xc = (cw_ref[3:4, :] * lx + cw_ref[2:3, :] * _shift_down(lx, prev8, 1)
              + cw_ref[1:2, :] * _shift_down(lx, prev8, 2) + cw_ref[0:1, :] * _shift_down(lx, prev8, 3)
              + cb_ref[...])
        xcb = xc.astype(BF16)
        r = _sigmoid(_dot(xcb, bdr_ref[...]) + br_ref[...])
        ig = _sigmoid(_dot(xcb, bdi_ref[...]) + bi_ref[...])
        log_a = (-LRU_C) * r * _softplus(-la_ref[...])
        a = jnp.exp(log_a)
        mult = jnp.sqrt(_neg_expm1(2.0 * log_a))
        acum, hloc = _scan_fwd(a, mult * (ig * xc))
        h = hloc + acum * hcarry[...]
        hcarry[...] = h[ts - 1:ts, :]
        hst_ref[...] = h
        gl, dgl = _gelu_and_grad(z_ref[:, 512:1024])
        y_l = h * gl
        for slot, val in ((ST_XC, xc), (ST_R, r), (ST_IG, ig), (ST_A, a), (ST_MULT, mult), (ST_GL, gl),
                          (ST_DGL, dgl)):
            st_ref[slot] = val

        u, du = _gelu_and_grad(z_ref[:, 1024:1536])
        vg, dvg = _gelu_and_grad(z_ref[:, 1536:2048])
        vc = vg - jnp.mean(vg, axis=-1, keepdims=True)
        rstd = lax.rsqrt(jnp.mean(vc * vc, axis=-1, keepdims=True) + EPS)
        vhat = vc * rstd
        vb = (vhat * ng_ref[...] + nb_ref[...]).astype(BF16)
        for n in range(ts // GMLP_BLOCK):
            rs = slice(n * GMLP_BLOCK, (n + 1) * GMLP_BLOCK)
            for g in range(GMLP_GROUPS):
                cs = slice(g * 128, (g + 1) * 128)
                sp_scr[rs, cs] = _dot(ws_ref[g], vb[rs, cs]) + bst_ref[:, g:g + 1]
        spb = sp_scr[...]
        y_g = u * spb
        for slot, val in ((ST_U, u), (ST_DU, du), (ST_Q, rstd * dvg), (ST_VHAT, vhat), (ST_SPB, spb)):
            st_ref[slot] = val

        ycat_ref[:, 0:512] = (y_l * _msq_rsqrt(y_l) * glo_ref[...]).astype(BF16)
        ycat_ref[:, 512:1024] = (y_g * _msq_rsqrt(y_g) * ggo_ref[...]).astype(BF16)

    return pl.pallas_call(
        body, grid=(nt,), name="seqmix",
        in_specs=[_rows(ts, 2048), _halo_prev(ts, 512, SUBLANES)] + _seq_param_specs()
        + [_full((1, 512)), _full((1, 512))],
        out_specs=[_rows(ts, 1024), _rows(ts, 512), pl.BlockSpec((N_STASH, ts, 512), lambda i: (0, i, 0))],
        out_shape=[_sds((s, 1024), BF16), _sds((s, 512), F32), _sds((N_STASH, s, 512), F32)],
        scratch_shapes=[pltpu.VMEM((1, 512), F32), pltpu.VMEM((ts, 512), F32)],
        compiler_params=_params(("arbitrary",)),
    )(z, z, *seq_params, glo, ggo)


def _mix_out(ycat, x, w_out, gt_m, g_post, g_pre2, sc_f, sh_f, ts=512):
    s, d = x.shape

    def body(yc_ref, x_ref, w_ref, gt_ref, gp_ref, g2_ref, sc_ref, sh_ref, y_ref, x1_ref, h2_ref):
        for rs in _sub_tiles(ts):
            y = _dot(yc_ref[rs, :], w_ref[...])
            y_ref[rs, :] = y
            x1 = x_ref[rs, :] + gt_ref[...] * (y * _msq_rsqrt(y) * gp_ref[...])
            x1_ref[rs, :] = x1
            h2 = (x1 * _msq_rsqrt(x1) * g2_ref[...]) * (1.0 + sc_ref[...]) + sh_ref[...]
            h2_ref[rs, :] = h2.astype(BF16)

    vec = _full((1, d))
    return pl.pallas_call(
        body, grid=(s // ts,), name="mix_out",
        in_specs=[_rows(ts, d), _rows(ts, d), _full((d, d)), vec, vec, vec, vec, vec],
        out_specs=[_rows(ts, d), _rows(ts, d), _rows(ts, d)],
        out_shape=[_sds((s, d), F32), _sds((s, d), F32), _sds((s, d), BF16)],
        compiler_params=_params(("parallel",)),
    )(ycat, x, w_out, gt_m, g_post, g_pre2, sc_f, sh_f)


def _ffn_cols(j):
    per = (2 * D_FF // N_CHIPS) // FFN_CHUNK
    return j // per, (j % per) * FFN_CHUNK, j * FFN_CHUNK


def _ffn_fwd(h2, x1, tgt, w_up4, w_down, fw, fb, gt_f, g_post, ts=256):
    s, d = x1.shape
    nch = D_FF // FFN_CHUNK

    def body(h2_ref, x1_ref, tgt_ref, wup_ref, wdn_ref, fw_ref, fb_ref, gt_ref, gp_ref,
             up0_ref, pre_ref, act_ref, dy2_ref, dx2_ref, loss_ref, dgt_ref, dgp_ref, tail_ref):
        i = pl.program_id(0)

        @pl.when(i == 0)
        def _():
            tail_ref[...] = jnp.zeros_like(tail_ref)
            loss_ref[...] = jnp.zeros_like(loss_ref)
            dgt_ref[...] = jnp.zeros_like(dgt_ref)
            dgp_ref[...] = jnp.zeros_like(dgp_ref)

        hb = h2_ref[...]

        def up_project(j):
            sh_g, off, _ = _ffn_cols(j)
            return [_dot(hb, wup_ref[shard, :, off:off + FFN_CHUNK]).astype(BF16) for shard in (sh_g, sh_g + 2)]

        y2 = jnp.zeros((ts, d), F32)
        ahead = up_project(0)
        for j in range(nch):
            _, _, col = _ffn_cols(j)
            ubs = ahead
            if j + 1 < nch:
                ahead = up_project(j + 1)
            halves = []
            for ub, c0 in zip(ubs, (col, D_FF + col)):
                cs = slice(c0, c0 + FFN_CHUNK)
                up0_ref[:, cs] = ub
                u = ub.astype(F32)
                prev8 = tail_ref[:, cs]
                tail_ref[:, cs] = u[ts - SUBLANES:, :]
                halves.append(fw_ref[2:3, cs] * u + fw_ref[1:2, cs] * _shift_down(u, prev8, 1)
                              + fw_ref[0:1, cs] * _shift_down(u, prev8, 2) + fb_ref[:, cs])
                pre_ref[:, cs] = halves[-1].astype(BF16)
            act = (_gelu(halves[0]) * halves[1]).astype(BF16)
            act_ref[:, col:col + FFN_CHUNK] = act
            y2 = y2 + _dot(act, wdn_ref[col:col + FFN_CHUNK, :])
        r2 = _msq_rsqrt(y2)
        yn = y2 * r2
        yng = yn * gp_ref[...]
        e = x1_ref[...] + gt_ref[...] * yng - tgt_ref[...]
        loss_ref[...] += jnp.sum(e * e) * (0.5 / d)
        dx2 = e * (1.0 / d)
        dx2_ref[...] = dx2
        dgt_ref[...] += _colsum(dx2 * yng)
        dyng = dx2 * gt_ref[...]
        dgp_ref[...] += _colsum(dyng * yn)
        dy2_ref[...] = _rms_bwd(dyng * gp_ref[...], yn, r2).astype(BF16)

    vec = _full((1, d))
    return pl.pallas_call(
        body, grid=(s // ts,), name="ffn_fwd",
        in_specs=[_rows(ts, d), _rows(ts, d), _rows(ts, d), _RESIDENT, _RESIDENT,
                  _full((3, 2 * D_FF)), _full((1, 2 * D_FF)), vec, vec],
        out_specs=[_rows(ts, 2 * D_FF), _rows(ts, 2 * D_FF), _rows(ts, D_FF), _rows(ts, d), _rows(ts, d),
                   _full((1, 128)), vec, vec],
        out_shape=[_sds((s, 2 * D_FF), BF16), _sds((s, 2 * D_FF), BF16), _sds((s, D_FF), BF16), _sds((s, d), BF16),
                   _sds((s, d), F32), _sds((1, 128), F32), _sds((1, d), F32), _sds((1, d), F32)],
        scratch_shapes=[pltpu.VMEM((SUBLANES, 2 * D_FF), F32)],
        compiler_params=_params(("arbitrary",)),
    )(h2, x1, tgt, w_up4, w_down, fw, fb, gt_f, g_post)


def _shift_up_mxu(vb, up_mat, next8, k):
    t = vb.shape[0]
    main = _dot(up_mat, vb)
    tail = pltpu.roll(next8, SUBLANES - k, 0)
    row8 = lax.broadcasted_iota(jnp.int32, next8.shape, 0)
    last = main[t - SUBLANES:] + jnp.where(row8 >= SUBLANES - k, tail, 0.0)
    return jnp.concatenate([main[:t - SUBLANES], last], axis=0)


def _ffn_bwd_a(dy2, pre, up0, w_down, fw, ts=256):
    s, d = dy2.shape
    nt = s // ts
    nch = D_FF // FFN_CHUNK
    wide = 2 * D_FF
    up_mats = jnp.stack([jnp.eye(ts, k=1, dtype=BF16), jnp.eye(ts, k=2, dtype=BF16)])

    def body(dy2_ref, pre_ref, up0_ref, wdn_ref, fw_ref, um_ref, dup0_ref, dfw_ref, dfb_ref, next_ref):
        i = pl.program_id(0)

        @pl.when(i == 0)
        def _():
            next_ref[...] = jnp.zeros_like(next_ref)
            dfw_ref[...] = jnp.zeros_like(dfw_ref)
            dfb_ref[...] = jnp.zeros_like(dfb_ref)

        dyb = dy2_ref[...]
        for j in range(nch):
            _, _, col = _ffn_cols(j)
            dact = _dot_nt(dyb, wdn_ref[col:col + FFN_CHUNK, :])
            gl, dgl = _gelu_and_grad(pre_ref[:, col:col + FFN_CHUNK].astype(F32))
            dpre = (dact * pre_ref[:, D_FF + col:D_FF + col + FFN_CHUNK].astype(F32) * dgl, dact * gl)
            for half, c0 in enumerate((col, D_FF + col)):
                cs = slice(c0, c0 + FFN_CHUNK)
                dp = dpre[half]
                dpb = dp.astype(BF16)
                nxt = next_ref[:, cs]
                next_ref[:, cs] = dpb.astype(F32)[0:SUBLANES, :]
                su1 = _shift_up_mxu(dpb, um_ref[0], nxt, 1)
                su2 = _shift_up_mxu(dpb, um_ref[1], nxt, 2)
                u = up0_ref[:, cs].astype(F32)
                dfb_ref[:, cs] += _colsum(dp)
                dfw_ref[2:3, cs] += _colsum(dp * u)
                dfw_ref[1:2, cs] += _colsum(su1 * u)
                dfw_ref[0:1, cs] += _colsum(su2 * u)
                dup0 = fw_ref[2:3, cs] * dp + fw_ref[1:2, cs] * su1 + fw_ref[0:1, cs] * su2
                dup0_ref[:, cs] = dup0.astype(BF16)

    return pl.pallas_call(
        body, grid=(nt,), name="ffn_bwd_a",
        in_specs=[_rows(ts, d, nt), _rows(ts, wide, nt), _rows(ts, wide, nt), _RESIDENT,
                  _full((3, wide)), _full((2, ts, ts))],
        out_specs=[_rows(ts, wide, nt), _full((3, wide)), _full((1, wide))],
        out_shape=[_sds((s, wide), BF16), _sds((3, wide), F32), _sds((1, wide), F32)],
        scratch_shapes=[pltpu.VMEM((SUBLANES, wide), F32)],
        compiler_params=_params(("arbitrary",)),
    )(dy2, pre, up0, w_down, fw, up_mats)


def _ffn_bwd_b(dup0, x1, y, dx2, w_up4, g_pre2, sc_f, sh_f, gt_m, g_post_m, ts=512):
    s, d = x1.shape
    shard_cols = 2 * D_FF // N_CHIPS

    def body(dup_ref, x1_ref, y_ref, dx2_ref, wup_ref, g2_ref, sc_ref, sh_ref, gt_ref, gp_ref,
             dx1_ref, dy_ref, dsh_ref, dsc_ref, dg2_ref, dgt_ref, dgp_ref):
        i = pl.program_id(0)

        @pl.when(i == 0)
        def _():
            for ref in (dsh_ref, dsc_ref, dg2_ref, dgt_ref, dgp_ref):
                ref[...] = jnp.zeros_like(ref)

        for rs in _sub_tiles(ts):
            dh2 = jnp.zeros((SUB_ROWS, d), F32)
            for k in range(N_CHIPS):
                dh2 = dh2 + _dot_nt(dup_ref[rs, k * shard_cols:(k + 1) * shard_cols], wup_ref[k])
            x1v = x1_ref[rs, :]
            r2 = _msq_rsqrt(x1v)
            xn = x1v * r2
            hn = xn * g2_ref[...]
            dsh_ref[...] += _colsum(dh2)
            dsc_ref[...] += _colsum(dh2 * hn)
            dhn = dh2 * (1.0 + sc_ref[...])
            dg2_ref[...] += _colsum(dhn * xn)
            dx1 = dx2_ref[rs, :] + _rms_bwd(dhn * g2_ref[...], xn, r2)
            dx1_ref[rs, :] = dx1
            yv = y_ref[rs, :]
            ry = _msq_rsqrt(yv)
            yn = yv * ry
            dgt_ref[...] += _colsum(dx1 * (yn * gp_ref[...]))
            dyng = dx1 * gt_ref[...]
            dgp_ref[...] += _colsum(dyng * yn)
            dy_ref[rs, :] = _rms_bwd(dyng * gp_ref[...], yn, ry).astype(BF16)

    vec = _full((1, d))
    return pl.pallas_call(
        body, grid=(s // ts,), name="ffn_bwd_b",
        in_specs=[_rows(ts, 2 * D_FF), _rows(ts, d), _rows(ts, d), _rows(ts, d), _RESIDENT,
                  vec, vec, vec, vec, vec],
        out_specs=[_rows(ts, d), _rows(ts, d), vec, vec, vec, vec, vec],
        out_shape=[_sds((s, d), F32), _sds((s, d), BF16)] + [_sds((1, d), F32)] * 5,
        compiler_params=_params(("arbitrary",)),
    )(dup0, x1, y, dx2, w_up4, g_pre2, sc_f, sh_f, gt_m, g_post_m)


def _seqmix_bwd(z, hst, stash, dy, w_out, seq_params, ws_t, glo, ggo, ts=256):
    s = z.shape[0]
    nt = s // ts
    small_shapes = [(4, 512), (1, 512), (512, 512), (512, 512), (1, 512), (1, 512), (1, 512),
                    (1, 512), (1, 512), (4, 128, 128), (128, 4), (1, 512), (1, 512)]

    def body(lx_ref, hst_ref, hprev_ref, st_ref, dy_ref, wout_ref, cw_ref, cb_ref, bdr_ref, bdi_ref, br_ref,
             bi_ref, la_ref, ng_ref, nb_ref, ws_ref, bst_ref, wst_ref, glo_ref, ggo_ref, dz_ref, *rest):
        small_refs = rest[:13]
        (dcw_ref, dcb_ref, dwr_ref, dwi_ref, dbr_ref, dbi_ref, dspa_ref, dng_ref, dnb_ref, dws_ref, dbs_ref,
         dglo_ref, dggo_ref) = small_refs
        gcarry, anext, dxcnext, dv_scr = rest[13:]
        i = pl.program_id(0)

        @pl.when(i == 0)
        def _():
            for ref in small_refs:
                ref[...] = jnp.zeros_like(ref)
            gcarry[...] = jnp.zeros_like(gcarry)
            anext[...] = jnp.ones_like(anext)
            dxcnext[...] = jnp.zeros_like(dxcnext)

        first_tile = i == nt - 1
        xc, r, ig, a, mult = st_ref[ST_XC], st_ref[ST_R], st_ref[ST_IG], st_ref[ST_A], st_ref[ST_MULT]
        gl, u, spb, vhat = st_ref[ST_GL], st_ref[ST_U], st_ref[ST_SPB], st_ref[ST_VHAT]
        lx = lx_ref[...]
        h = hst_ref[...]
        hprev = _shift_down(h, jnp.where(first_tile, 0.0, hprev_ref[...]), 1)
        y_l = h * gl
        y_g = u * spb

        dycat = _dot_nt(dy_ref[...], wout_ref[...])
        rl = _msq_rsqrt(y_l)
        yln = y_l * rl
        dyl = dycat[:, 0:512]
        dglo_ref[...] += _colsum(dyl * yln)
        dy_l = _rms_bwd(dyl * glo_ref[...], yln, rl)
        rg = _msq_rsqrt(y_g)
        ygn = y_g * rg
        dyg = dycat[:, 512:1024]
        dggo_ref[...] += _colsum(dyg * ygn)
        dy_g = _rms_bwd(dyg * ggo_ref[...], ygn, rg)

        dz_ref[:, 512:1024] = (dy_l * h * st_ref[ST_DGL]).astype(BF16)
        a_up = _shift_up(a, anext[...], 1)
        acum, gloc = _scan_bwd(a_up, dy_l * gl)
        gg = gloc + acum * gcarry[...]
        gcarry[...] = gg[0:1, :]
        anext[...] = a[0:SUBLANES, :]
        da = gg * hprev
        t1 = gg * mult
        di = t1 * xc
        dxc = t1 * ig
        dmult = gg * ig * xc
        dla = da * a - dmult * (a * a / mult)
        dspa_ref[...] += _colsum(dla * r) * (-LRU_C)
        dpr = dla * ((-LRU_C) * _softplus(-la_ref[...])) * r * (1.0 - r)
        dpi = di * ig * (1.0 - ig)
        dbr_ref[...] += _colsum(dpr)
        dbi_ref[...] += _colsum(dpi)
        dprb = dpr.astype(BF16)
        dpib = dpi.astype(BF16)
        xcb = xc.astype(BF16)
        dwr_ref[...] += _dot_tn(xcb, dprb)
        dwi_ref[...] += _dot_tn(xcb, dpib)
        dxc = dxc + _dot_nt(dprb, bdr_ref[...]) + _dot_nt(dpib, bdi_ref[...])
        nxt = dxcnext[...]
        dxcnext[...] = dxc[0:SUBLANES, :]
        up1, up2, up3 = _shift_up(dxc, nxt, 1), _shift_up(dxc, nxt, 2), _shift_up(dxc, nxt, 3)
        dcb_ref[...] += _colsum(dxc)
        dcw_ref[3:4, :] += _colsum(dxc * lx)
        dcw_ref[2:3, :] += _colsum(up1 * lx)
        dcw_ref[1:2, :] += _colsum(up2 * lx)
        dcw_ref[0:1, :] += _colsum(up3 * lx)
        dlx = cw_ref[3:4, :] * dxc + cw_ref[2:3, :] * up1 + cw_ref[1:2, :] * up2 + cw_ref[0:1, :] * up3
        dz_ref[:, 0:512] = dlx.astype(BF16)

        dz_ref[:, 1024:1536] = (dy_g * spb * st_ref[ST_DU]).astype(BF16)
        dsp = dy_g * u
        vb = (vhat * ng_ref[...] + nb_ref[...]).astype(BF16)
        for n in range(ts // GMLP_BLOCK):
            rs = slice(n * GMLP_BLOCK, (n + 1) * GMLP_BLOCK)
            for g in range(GMLP_GROUPS):
                cs = slice(g * 128, (g + 1) * 128)
                dbs_ref[:, g:g + 1] += jnp.sum(dsp[rs, cs], axis=1, keepdims=True)
                blk = dsp[rs, cs].astype(BF16)
                dws_ref[g] += _dot_nt(blk, vb[rs, cs])
                dv_scr[rs, cs] = _dot(wst_ref[g], blk)
        dv = dv_scr[...]
        dng_ref[...] += _colsum(dv * vhat)
        dnb_ref[...] += _colsum(dv)
        dvh = dv * ng_ref[...]
        dvg = dvh - jnp.mean(dvh, axis=-1, keepdims=True) - vhat * jnp.mean(dvh * vhat, axis=-1, keepdims=True)
        dz_ref[:, 1536:2048] = (dvg * st_ref[ST_Q]).astype(BF16)

        @pl.when(i == nt - 1)
        def _():
            pos = lax.broadcasted_iota(jnp.int32, (GMLP_BLOCK, GMLP_BLOCK), 0) // CHUNK
            src = lax.broadcasted_iota(jnp.int32, (GMLP_BLOCK, GMLP_BLOCK), 1) // CHUNK
            for g in range(GMLP_GROUPS):
                dws_ref[g] = jnp.where(src <= pos, dws_ref[g], 0.0)
            dspa_ref[...] = dspa_ref[...] * (-_sigmoid(-la_ref[...]))

    in_specs = ([_rows(ts, 512, nt), _rows(ts, 512, nt), _halo_prev(ts, 512, SUBLANES, nt),
                 pl.BlockSpec((N_STASH, ts, 512), lambda i: (0, nt - 1 - i, 0)), _rows(ts, 1024, nt),
                 _full((1024, 1024))]
                + _seq_param_specs() + [_full((4, 128, 128)), _full((1, 512)), _full((1, 512))])
    return pl.pallas_call(
        body, grid=(nt,), name="seqmix_bwd",
        in_specs=in_specs,
        out_specs=[_rows(ts, 2048, nt)] + [_full(sh) for sh in small_shapes],
        out_shape=[_sds((s, 2048), BF16)] + [_sds(sh, F32) for sh in small_shapes],
        scratch_shapes=[pltpu.VMEM((1, 512), F32), pltpu.VMEM((SUBLANES, 512), F32),
                        pltpu.VMEM((SUBLANES, 512), F32), pltpu.VMEM((ts, 512), F32)],
        compiler_params=_params(("arbitrary",)),
    )(z, hst, hst, stash, dy, w_out, *seq_params, ws_t, glo, ggo)


def _seqmix_bwd_recomputing_unused(z, hst, dy, w_out, seq_params, ws_t, glo, ggo, ts=256):
    s = z.shape[0]
    nt = s // ts
    small_shapes = [(4, 512), (1, 512), (512, 512), (512, 512), (1, 512), (1, 512), (1, 512),
                    (1, 512), (1, 512), (4, 128, 128), (128, 4), (1, 512), (1, 512)]

    def body(z_ref, zprev_ref, hst_ref, hprev_ref, dy_ref, wout_ref, *rest):
        p = rest[:11]
        wst_ref, glo_ref, ggo_ref = rest[11:14]
        dz_ref = rest[14]
        (dcw_ref, dcb_ref, dwr_ref, dwi_ref, dbr_ref, dbi_ref, dspa_ref, dng_ref, dnb_ref, dws_ref, dbs_ref,
         dglo_ref, dggo_ref) = rest[15:28]
        gcarry, anext, dxcnext, sp_scr, dv_scr = rest[28:]
        i = pl.program_id(0)

        @pl.when(i == 0)
        def _():
            for ref in rest[15:28]:
                ref[...] = jnp.zeros_like(ref)
            gcarry[...] = jnp.zeros_like(gcarry)
            anext[...] = jnp.ones_like(anext)
            dxcnext[...] = jnp.zeros_like(dxcnext)

        first_tile = i == nt - 1
        f = _seq_recompute(z_ref, zprev_ref, first_tile, p)
        xc, r, ig, a, mult, lx = f["xc"], f["r"], f["ig"], f["a"], f["mult"], f["lx"]
        h = hst_ref[...]
        hprev = _shift_down(h, jnp.where(first_tile, 0.0, hprev_ref[...]), 1)
        gl, dgl = _gelu_and_grad(f["lg"])
        y_l = h * gl
        gm = _gmlp_fwd(f["gu"], f["gv"], p[7], p[8], p[9], p[10], sp_scr)
        y_g = gm["y_g"]

        dycat = _dot_nt(dy_ref[...], wout_ref[...])
        rl = _msq_rsqrt(y_l)
        yln = y_l * rl
        dyl = dycat[:, 0:512]
        dglo_ref[...] += _colsum(dyl * yln)
        dy_l = _rms_bwd(dyl * glo_ref[...], yln, rl)
        rg = _msq_rsqrt(y_g)
        ygn = y_g * rg
        dyg = dycat[:, 512:1024]
        dggo_ref[...] += _colsum(dyg * ygn)
        dy_g = _rms_bwd(dyg * ggo_ref[...], ygn, rg)

        dz_ref[:, 512:1024] = (dy_l * h * dgl).astype(BF16)
        a_up = _shift_up(a, anext[...], 1)
        acum, gloc = _scan_bwd(a_up, dy_l * gl)
        gg = gloc + acum * gcarry[...]
        gcarry[...] = gg[0:1, :]
        anext[...] = a[0:SUBLANES, :]
        da = gg * hprev
        t1 = gg * mult
        di = t1 * xc
        dxc = t1 * ig
        dmult = gg * ig * xc
        dla = da * a - dmult * (a * a / mult)
        spa = f["spa"]
        dspa_ref[...] += _colsum(dla * r) * (-LRU_C)
        dpr = dla * ((-LRU_C) * spa) * r * (1.0 - r)
        dpi = di * ig * (1.0 - ig)
        dbr_ref[...] += _colsum(dpr)
        dbi_ref[...] += _colsum(dpi)
        dprb = dpr.astype(BF16)
        dpib = dpi.astype(BF16)
        dwr_ref[...] += _dot_tn(f["xcb"], dprb)
        dwi_ref[...] += _dot_tn(f["xcb"], dpib)
        dxc = dxc + _dot_nt(dprb, p[2][...]) + _dot_nt(dpib, p[3][...])
        dcb_ref[...] += _colsum(dxc)
        dcw_ref[3:4, :] += _colsum(dxc * lx)
        dcw_ref[2:3, :] += _colsum(dxc * f["s1"])
        dcw_ref[1:2, :] += _colsum(dxc * f["s2"])
        dcw_ref[0:1, :] += _colsum(dxc * f["s3"])
        nxt = dxcnext[...]
        dxcnext[...] = dxc[0:SUBLANES, :]
        cw_ref = p[0]
        dlx = (cw_ref[3:4, :] * dxc + cw_ref[2:3, :] * _shift_up(dxc, nxt, 1)
               + cw_ref[1:2, :] * _shift_up(dxc, nxt, 2) + cw_ref[0:1, :] * _shift_up(dxc, nxt, 3))
        dz_ref[:, 0:512] = dlx.astype(BF16)

        dz_ref[:, 1024:1536] = (dy_g * gm["spb"] * gm["du"]).astype(BF16)
        dsp = dy_g * gm["u"]
        vb = gm["vb"]
        for n in range(ts // GMLP_BLOCK):
            rs = slice(n * GMLP_BLOCK, (n + 1) * GMLP_BLOCK)
            for g in range(GMLP_GROUPS):
                cs = slice(g * 128, (g + 1) * 128)
                dbs_ref[:, g:g + 1] += jnp.sum(dsp[rs, cs], axis=1, keepdims=True)
                blk = dsp[rs, cs].astype(BF16)
                dws_ref[g] += _dot_nt(blk, vb[rs, cs])
                dv_scr[rs, cs] = _dot(wst_ref[g], blk)
        dv = dv_scr[...]
        vhat = gm["vhat"]
        dng_ref[...] += _colsum(dv * vhat)
        dnb_ref[...] += _colsum(dv)
        dvh = dv * p[7][...]
        dvg = gm["rstd"] * (dvh - jnp.mean(dvh, axis=-1, keepdims=True)
                            - vhat * jnp.mean(dvh * vhat, axis=-1, keepdims=True))
        dz_ref[:, 1536:2048] = (dvg * gm["dvg"]).astype(BF16)

        @pl.when(i == nt - 1)
        def _():
            pos = lax.broadcasted_iota(jnp.int32, (GMLP_BLOCK, GMLP_BLOCK), 0) // CHUNK
            src = lax.broadcasted_iota(jnp.int32, (GMLP_BLOCK, GMLP_BLOCK), 1) // CHUNK
            for g in range(GMLP_GROUPS):
                dws_ref[g] = jnp.where(src <= pos, dws_ref[g], 0.0)
            dspa_ref[...] = dspa_ref[...] * (-_sigmoid(-p[6][...]))

    in_specs = (_seq_specs(ts, nt, True)
                + [_rows(ts, 512, nt), _halo_prev(ts, 512, SUBLANES, nt), _rows(ts, 1024, nt), _full((1024, 1024))]
                + _seq_param_specs() + [_full((4, 128, 128)), _full((1, 512)), _full((1, 512))])
    return pl.pallas_call(
        body, grid=(nt,), name="seqmix_bwd",
        in_specs=in_specs,
        out_specs=[_rows(ts, 2048, nt)] + [_full(sh) for sh in small_shapes],
        out_shape=[_sds((s, 2048), BF16)] + [_sds(sh, F32) for sh in small_shapes],
        scratch_shapes=[pltpu.VMEM((1, 512), F32), pltpu.VMEM((SUBLANES, 512), F32),
                        pltpu.VMEM((SUBLANES, 512), F32), pltpu.VMEM((ts, 512), F32), pltpu.VMEM((ts, 512), F32)],
        compiler_params=_params(("arbitrary",)),
    )(z, z, hst, hst, dy, w_out, *seq_params, ws_t, glo, ggo)


def _mix_in_bwd(x, dz, dx1, w_in4, g, sc, ts=512):
    s, d = x.shape

    def body(x_ref, dz_ref, dx1_ref, w_ref, g_ref, sc_ref, gx_ref, dsh_ref, dsc_ref, dg_ref):
        i = pl.program_id(0)

        @pl.when(i == 0)
        def _():
            for ref in (dsh_ref, dsc_ref, dg_ref):
                ref[...] = jnp.zeros_like(ref)

        for rs in _sub_tiles(ts):
            dh = jnp.zeros((SUB_ROWS, d), F32)
            for k in range(N_CHIPS):
                dh = dh + _dot_nt(dz_ref[rs, k * 512:(k + 1) * 512], w_ref[k])
            xv = x_ref[rs, :]
            r = _msq_rsqrt(xv)
            xn = xv * r
            dsh_ref[...] += _colsum(dh)
            dsc_ref[...] += _colsum(dh * (xn * g_ref[...]))
            dhn = dh * (1.0 + sc_ref[...])
            dg_ref[...] += _colsum(dhn * xn)
            gx_ref[rs, :] = dx1_ref[rs, :] + _rms_bwd(dhn * g_ref[...], xn, r)

    vec = _full((1, d))
    return pl.pallas_call(
        body, grid=(s // ts,), name="mix_in_bwd",
        in_specs=[_rows(ts, d), _rows(ts, 2048), _rows(ts, d), _full(w_in4.shape), vec, vec],
        out_specs=[_rows(ts, d), vec, vec, vec],
        out_shape=[_sds((s, d), F32)] + [_sds((1, d), F32)] * 3,
        compiler_params=_params(("arbitrary",)),
    )(x, dz, dx1, w_in4, g, sc)


def _wgrad(a, b, n_chunks, name, chunk_major, ts=2048):
    s, m = a.shape
    n = b.shape[1]
    nc = n // n_chunks
    nt = s // ts

    def body(a_ref, b_ref, o_ref, acc):
        i = pl.program_id(1)

        @pl.when(i == 0)
        def _():
            acc[...] = jnp.zeros_like(acc)

        acc[...] += _dot_tn(a_ref[...], b_ref[...])

        @pl.when(i == nt - 1)
        def _():
            if chunk_major:
                o_ref[0] = acc[...].astype(BF16)
            else:
                o_ref[...] = acc[...].astype(BF16)

    if chunk_major:
        out_spec, out_shape = pl.BlockSpec((1, m, nc), lambda c, i: (c, 0, 0)), _sds((n_chunks, m, nc), BF16)
    else:
        out_spec, out_shape = pl.BlockSpec((m, nc), lambda c, i: (0, c)), _sds((m, n), BF16)
    return pl.pallas_call(
        body, grid=(n_chunks, nt), name=name,
        in_specs=[pl.BlockSpec((ts, m), lambda c, i: (i, 0)), pl.BlockSpec((ts, nc), lambda c, i: (i, c))],
        out_specs=out_spec,
        out_shape=out_shape,
        scratch_shapes=[pltpu.VMEM((m, nc), F32)],
        compiler_params=_params(("parallel", "arbitrary")),
    )(a, b)


def _block_diag(w):
    heads, hd, _ = w.shape
    eye = jnp.eye(heads, dtype=w.dtype)
    return (eye[:, None, :, None] * w[:, :, None, :]).reshape(heads * hd, heads * hd)


def _diag_blocks(m):
    hd = LRU_WIDTH // LRU_HEADS
    m4 = m.reshape(LRU_HEADS, hd, LRU_HEADS, hd)
    return jnp.stack([m4[k, :, k, :] for k in range(LRU_HEADS)])


def _seq_params(small):
    row = lambda v: v.reshape(1, -1)
    pos = jnp.arange(GMLP_BLOCK)
    mask = (pos[None, :] // CHUNK) <= (pos[:, None] // CHUNK)
    ws = jnp.where(mask[None], small["w_spatial"], 0.0)
    seq_params = (small["conv_w"], row(small["conv_b"]),
                  _block_diag(small["w_rgate"]).astype(BF16), _block_diag(small["w_igate"]).astype(BF16),
                  row(small["b_rgate"]), row(small["b_igate"]), row(small["lru_a"]),
                  row(small["v_norm_g"]), row(small["v_norm_b"]), ws.astype(BF16), small["b_spatial"].T)
    return seq_params, jnp.swapaxes(ws, 1, 2).astype(BF16)


_ANY = pl.BlockSpec(memory_space=pl.ANY)
_CHIP_FLIPS = ((1, 0), (0, 1), (1, 1))


def _position():
    return lax.axis_index("x"), lax.axis_index("y"), lax.axis_index("c")


def _flip(v, f):
    return 1 - v if f else v


def _remote(src, dst, send_sem, recv_sem, peer):
    return pltpu.make_async_remote_copy(src_ref=src, dst_ref=dst, send_sem=send_sem, recv_sem=recv_sem,
                                        device_id=peer, device_id_type=MESH)


def _allgather8(block, name, reduce):
    r, n = block.shape

    def body(x_ref, out_ref, *scratch):
        if reduce:
            gath, send_sems, recv_sems, loc_sem = scratch
        else:
            gath = out_ref
            send_sems, recv_sems, loc_sem = scratch
        x, y, c = _position()
        me = 4 * x + 2 * y + c
        loc = pltpu.make_async_copy(x_ref, gath.at[me], loc_sem)
        loc.start()
        peers = []
        for k in range(1, N_DEV):
            px, py, pc = _flip(x, k & 4), _flip(y, k & 2), _flip(c, k & 1)
            peers.append((px, py, pc))
            _remote(x_ref, gath.at[me], send_sems.at[k - 1], recv_sems.at[k - 1], (px, py, pc)).start()
        for k, (px, py, pc) in enumerate(peers):
            src = 4 * px + 2 * py + pc
            _remote(x_ref, gath.at[src], send_sems.at[k], recv_sems.at[k], (px, py, pc)).wait_recv()
        for k, peer in enumerate(peers):
            _remote(x_ref, gath.at[me], send_sems.at[k], recv_sems.at[k], peer).wait_send()
        loc.wait()
        if reduce:
            acc = gath[0]
            for k in range(1, N_DEV):
                acc = acc + gath[k]
            out_ref[...] = acc

    sems = [pltpu.SemaphoreType.DMA((N_DEV - 1,)), pltpu.SemaphoreType.DMA((N_DEV - 1,)), pltpu.SemaphoreType.DMA]
    if reduce:
        out_shape = _sds((r, n), F32)
        scratch = [pltpu.VMEM((N_DEV, r, n), F32)] + sems
    else:
        out_shape = _sds((N_DEV, r, n), F32)
        scratch = sems
    return pl.pallas_call(
        body, name=name, out_shape=out_shape,
        in_specs=[pl.BlockSpec(memory_space=pltpu.VMEM)], out_specs=pl.BlockSpec(memory_space=pltpu.VMEM),
        scratch_shapes=scratch,
        compiler_params=pltpu.CompilerParams(vmem_limit_bytes=VMEM_LIMIT_BYTES),
    )(block)


def _half(ref, c, rows):
    hr = rows // 2
    return ref.at[pl.ds(pl.multiple_of(c * hr, BF16_SUBLANES), hr), :]


def _gather_weights(shards):
    na = len(shards)

    def body(*refs):
        ins, outs = refs[:na], refs[na:2 * na]
        ici_send, ici_recv, d2d_send, d2d_recv, loc_sem = refs[2 * na:]
        x, y, c = _position()
        chip = 2 * x + y
        sibling = (x, y, 1 - c)
        local = []
        for a in range(na):
            local.append(pltpu.make_async_copy(ins[a], outs[a].at[chip], loc_sem.at[a]))
            local[-1].start()
        sends = []
        for a in range(na):
            rows = shards[a].shape[0]
            for j, (fx, fy) in enumerate(_CHIP_FLIPS):
                peer = (_flip(x, fx), _flip(y, fy), c)
                sends.append(_remote(_half(ins[a], c, rows), _half(outs[a].at[chip], c, rows),
                                     ici_send.at[a * 3 + j], ici_recv.at[a * 3 + j], peer))
                sends[-1].start()
        for a in range(na):
            rows = shards[a].shape[0]
            for j, (fx, fy) in enumerate(_CHIP_FLIPS):
                src_chip = 2 * _flip(x, fx) + _flip(y, fy)
                landed = _half(outs[a].at[src_chip], c, rows)
                _remote(landed, landed, ici_send.at[a * 3 + j], ici_recv.at[a * 3 + j], sibling).wait_recv()
                sends.append(_remote(landed, landed, d2d_send.at[a * 3 + j], d2d_recv.at[a * 3 + j], sibling))
                sends[-1].start()
        for a in range(na):
            rows = shards[a].shape[0]
            for j, (fx, fy) in enumerate(_CHIP_FLIPS):
                src_chip = 2 * _flip(x, fx) + _flip(y, fy)
                other = _half(outs[a].at[src_chip], 1 - c, rows)
                _remote(other, other, d2d_send.at[a * 3 + j], d2d_recv.at[a * 3 + j], sibling).wait_recv()
        for cp in sends:
            cp.wait_send()
        for cp in local:
            cp.wait()

    return pl.pallas_call(
        body, name="gather_weights",
        out_shape=[_sds((N_CHIPS,) + w.shape, w.dtype) for w in shards],
        in_specs=[_ANY] * na, out_specs=[_ANY] * na,
        scratch_shapes=[pltpu.SemaphoreType.DMA((3 * na,))] * 4 + [pltpu.SemaphoreType.DMA((na,))],
    )(*shards)


def _swap_halves(parts, name):
    na = len(parts)

    def body(*refs):
        ins, outs = refs[:na], refs[na:2 * na]
        send_sems, recv_sems = refs[2 * na:]
        x, y, c = _position()
        sibling = (x, y, 1 - c)
        cps = []
        for a in range(na):
            hr = parts[a].shape[1] // 2
            src = ins[a].at[:, pl.ds(pl.multiple_of((1 - c) * hr, BF16_SUBLANES), hr), :]
            cps.append(_remote(src, outs[a], send_sems.at[a], recv_sems.at[a], sibling))
            cps[-1].start()
        for cp in cps:
            cp.wait()

    return pl.pallas_call(
        body, name=name,
        out_shape=[_sds((N_CHIPS, p.shape[1] // 2, p.shape[2]), p.dtype) for p in parts],
        in_specs=[_ANY] * na, out_specs=[_ANY] * na,
        scratch_shapes=[pltpu.SemaphoreType.DMA((na,))] * 2,
    )(*parts)


def _chip_sum(part, recv, pos_arr, name):
    _, rows, cols = part.shape
    hr = rows // 2

    def body(pos_ref, p_ref, r_ref, o_ref, g_ref):
        total = (p_ref[...].astype(F32) + r_ref[...].astype(F32)).astype(BF16)
        o_ref[...] = total

        @pl.when(pl.program_id(0) == pos_ref[1])
        def _():
            g_ref[0] = total

    grid_spec = pltpu.PrefetchScalarGridSpec(
        num_scalar_prefetch=1, grid=(N_CHIPS,),
        in_specs=[pl.BlockSpec((1, hr, cols), lambda k, pos: (k, pos[0], 0)),
                  pl.BlockSpec((1, hr, cols), lambda k, pos: (k, 0, 0))],
        out_specs=[pl.BlockSpec((1, hr, cols), lambda k, pos: (k, 0, 0)),
                   pl.BlockSpec((1, 1, hr, cols), lambda k, pos: (0, pos[1], 0, 0))])
    return pl.pallas_call(
        body, name=name, grid_spec=grid_spec,
        out_shape=[_sds((N_CHIPS, hr, cols), BF16), _sds((2, N_CHIPS, hr, cols), BF16)],
        compiler_params=_params(("arbitrary",)),
    )(pos_arr, part, recv)


def _exchange_chips(sums):
    na = len(sums)

    def body(*refs):
        ins, outs = refs[:na], refs[na:2 * na]
        send_sems, recv_sems, loc_sem = refs[2 * na:]
        x, y, c = _position()
        chip = 2 * x + y
        local = []
        for a in range(na):
            local.append(pltpu.make_async_copy(ins[a].at[chip], outs[a].at[chip], loc_sem.at[a]))
            local[-1].start()
        cps = []
        for a in range(na):
            for j, (fx, fy) in enumerate(_CHIP_FLIPS):
                px, py = _flip(x, fx), _flip(y, fy)
                cps.append(_remote(ins[a].at[2 * px + py], outs[a].at[chip],
                                   send_sems.at[a * 3 + j], recv_sems.at[a * 3 + j], (px, py, c)))
                cps[-1].start()
        for a in range(na):
            for j, (fx, fy) in enumerate(_CHIP_FLIPS):
                src_chip = 2 * _flip(x, fx) + _flip(y, fy)
                landed = outs[a].at[src_chip]
                _remote(landed, landed, send_sems.at[a * 3 + j], recv_sems.at[a * 3 + j], (x, y, c)).wait_recv()
        for cp in cps:
            cp.wait_send()
        for cp in local:
            cp.wait()

    return pl.pallas_call(
        body, name="exchange_chips",
        out_shape=[_sds(s.shape, s.dtype) for s in sums],
        in_specs=[_ANY] * na, out_specs=[_ANY] * na,
        scratch_shapes=[pltpu.SemaphoreType.DMA((3 * na,))] * 2 + [pltpu.SemaphoreType.DMA((na,))],
    )(*sums)


def _sum_chips(gath, name, tr=128):
    _, hr, cols = gath.shape
    tr = min(tr, hr)

    def body(g_ref, o_ref):
        acc = g_ref[0].astype(F32)
        for k in range(1, N_CHIPS):
            acc = acc + g_ref[k].astype(F32)
        o_ref[...] = acc

    return pl.pallas_call(
        body, name=name, grid=(hr // tr,),
        in_specs=[pl.BlockSpec((N_CHIPS, tr, cols), lambda i: (0, i, 0))],
        out_specs=pl.BlockSpec((tr, cols), lambda i: (i, 0)),
        out_shape=_sds((hr, cols), F32),
        compiler_params=_params(("parallel",)),
    )(gath)


def _join_halves(halves):
    na = len(halves)

    def body(*refs):
        ins, outs = refs[:na], refs[na:2 * na]
        send_sems, recv_sems, loc_sem = refs[2 * na:]
        x, y, c = _position()
        sibling = (x, y, 1 - c)
        cps, local = [], []
        for a in range(na):
            rows = 2 * halves[a].shape[0]
            mine = _half(outs[a], c, rows)
            local.append(pltpu.make_async_copy(ins[a], mine, loc_sem.at[a]))
            local[-1].start()
            cps.append(_remote(ins[a], mine, send_sems.at[a], recv_sems.at[a], sibling))
            cps[-1].start()
        for a in range(na):
            rows = 2 * halves[a].shape[0]
            other = _half(outs[a], 1 - c, rows)
            _remote(ins[a], other, send_sems.at[a], recv_sems.at[a], sibling).wait_recv()
        for cp in cps:
            cp.wait_send()
        for cp in local:
            cp.wait()

    return pl.pallas_call(
        body, name="join_halves",
        out_shape=[_sds((2 * h.shape[0], h.shape[1]), h.dtype) for h in halves],
        in_specs=[_ANY] * na, out_specs=[_ANY] * na,
        scratch_shapes=[pltpu.SemaphoreType.DMA((na,))] * 3,
    )(*halves)


_HBM = pl.BlockSpec(memory_space=pltpu.HBM)
_SEM = pl.BlockSpec(memory_space=pltpu.SEMAPHORE)
_EFFECT = pltpu.SideEffectType.DATAFLOW_SIDE_EFFECTING


def _in_hbm(a):
    return pltpu.with_memory_space_constraint(a, pltpu.HBM)


def _split_start(srcs, lands, plan, n_copies, after, name):
    ns, nl = len(srcs), len(lands)
    bufs = list(srcs) + list(lands)

    def body(*refs):
        send_sems, recv_sems = refs[ns + nl + 1], refs[ns + nl + 2]
        token = refs[-1]
        for k, (src, dst, peer) in enumerate(plan(refs[:ns], refs[ns:ns + nl])):
            _remote(src, dst, send_sems.at[k], recv_sems.at[k], peer).start()
        token[...] = jnp.zeros_like(token)

    out = pl.pallas_call(
        body, name=name,
        out_shape=(pltpu.SemaphoreType.DMA((n_copies,)), pltpu.SemaphoreType.DMA((n_copies,)),
                   *[pltpu.HBM(b.shape, b.dtype) for b in bufs], _sds((SUBLANES, 128), F32)),
        in_specs=[_HBM] * (ns + nl) + [_ANY],
        out_specs=(_SEM, _SEM, *[_HBM] * (ns + nl), pl.BlockSpec(memory_space=pltpu.VMEM)),
        input_output_aliases={i: 2 + i for i in range(ns + nl)},
        compiler_params=pltpu.CompilerParams(has_side_effects=_EFFECT),
    )(*[_in_hbm(b) for b in bufs], after)
    return out[0], out[1], list(out[2:2 + ns]), list(out[2 + ns:2 + ns + nl]), out[-1]


def _split_wait(send_sems, recv_sems, srcs, lands, plan, after, name):
    ns, nl = len(srcs), len(lands)
    bufs = list(srcs) + list(lands)

    def body(*refs):
        send_ref, recv_ref = refs[ns + nl], refs[ns + nl + 1]
        me = _position()
        for k, src, dst in plan(refs[:ns], refs[ns:ns + nl]):
            cp = _remote(src, dst, send_ref.at[k], recv_ref.at[k], me)
            cp.wait_send()
            cp.wait_recv()

    out = pl.pallas_call(
        body, name=name,
        out_shape=[pltpu.HBM(b.shape, b.dtype) for b in bufs],
        in_specs=[_HBM] * (ns + nl) + [_SEM, _SEM, _ANY],
        out_specs=[_HBM] * (ns + nl),
        input_output_aliases={i: i for i in range(ns + nl)},
        compiler_params=pltpu.CompilerParams(has_side_effects=_EFFECT),
    )(*bufs, send_sems, recv_sems, after)
    return list(out[:ns]), list(out[ns:])


def _gather_plan(rows_of):
    def start(src_refs, land_refs):
        x, y, c = _position()
        chip = 2 * x + y
        out = []
        for a, rows in enumerate(rows_of):
            mine = _half(land_refs[a].at[chip], c, rows)
            out.extend((mine, mine, (_flip(x, fx), _flip(y, fy), c)) for fx, fy in _CHIP_FLIPS)
        return out

    def wait(src_refs, land_refs):
        x, y, c = _position()
        chip = 2 * x + y
        out = []
        for a, rows in enumerate(rows_of):
            for j, (fx, fy) in enumerate(_CHIP_FLIPS):
                src_chip = 2 * _flip(x, fx) + _flip(y, fy)
                out.append((3 * a + j, _half(land_refs[a].at[chip], c, rows),
                            _half(land_refs[a].at[src_chip], c, rows)))
        return out

    return start, wait


def _forward_plan(rows_of):
    def pieces(land_refs, half):
        x, y, _ = _position()
        return [_half(land_refs[a].at[2 * _flip(x, fx) + _flip(y, fy)], half, rows)
                for a, rows in enumerate(rows_of) for fx, fy in _CHIP_FLIPS]

    def start(src_refs, land_refs):
        x, y, c = _position()
        return [(p, p, (x, y, 1 - c)) for p in pieces(land_refs, c)]

    def wait(src_refs, land_refs):
        _, _, c = _position()
        return [(k, mine, theirs)
                for k, (mine, theirs) in enumerate(zip(pieces(land_refs, c), pieces(land_refs, 1 - c)))]

    return start, wait


def _swap_halves_plan(half_rows):
    def slices(src_refs, c):
        return [src_refs[a].at[:, pl.ds(pl.multiple_of((1 - c) * hr, BF16_SUBLANES), hr), :]
                for a, hr in enumerate(half_rows)]

    def start(src_refs, land_refs):
        x, y, c = _position()
        return [(src, land_refs[a], (x, y, 1 - c)) for a, src in enumerate(slices(src_refs, c))]

    def wait(src_refs, land_refs):
        _, _, c = _position()
        return [(a, src, land_refs[a]) for a, src in enumerate(slices(src_refs, c))]

    return start, wait


def _swap_gathered_plan(n_arrays):
    def start(src_refs, land_refs):
        x, y, c = _position()
        return [(land_refs[a].at[0], land_refs[a].at[1], (x, y, 1 - c)) for a in range(n_arrays)]

    def wait(src_refs, land_refs):
        return [(a, land_refs[a].at[0], land_refs[a].at[1]) for a in range(n_arrays)]

    return start, wait


def _exchange_plan(n_arrays):
    def start(src_refs, land_refs):
        x, y, c = _position()
        chip = 2 * x + y
        out = []
        for a in range(n_arrays):
            for fx, fy in _CHIP_FLIPS:
                px, py = _flip(x, fx), _flip(y, fy)
                out.append((src_refs[a].at[2 * px + py], land_refs[a].at[0, chip], (px, py, c)))
        return out

    def wait(src_refs, land_refs):
        x, y, c = _position()
        out = []
        for a in range(n_arrays):
            for j, (fx, fy) in enumerate(_CHIP_FLIPS):
                src_chip = 2 * _flip(x, fx) + _flip(y, fy)
                out.append((3 * a + j, src_refs[a].at[src_chip], land_refs[a].at[0, src_chip]))
        return out

    return start, wait


def _forward_to_sibling(lands, name):
    na = len(lands)

    def body(*refs):
        land_refs = refs[na:2 * na]
        send_sems, recv_sems = refs[2 * na:]
        x, y, c = _position()
        sibling = (x, y, 1 - c)
        sends = []
        for a in range(na):
            rows = lands[a].shape[1]
            for j, (fx, fy) in enumerate(_CHIP_FLIPS):
                landed = _half(land_refs[a].at[2 * _flip(x, fx) + _flip(y, fy)], c, rows)
                sends.append(_remote(landed, landed, send_sems.at[3 * a + j], recv_sems.at[3 * a + j], sibling))
                sends[-1].start()
        for a in range(na):
            rows = lands[a].shape[1]
            for j, (fx, fy) in enumerate(_CHIP_FLIPS):
                other = _half(land_refs[a].at[2 * _flip(x, fx) + _flip(y, fy)], 1 - c, rows)
                _remote(other, other, send_sems.at[3 * a + j], recv_sems.at[3 * a + j], sibling).wait_recv()
        for cp in sends:
            cp.wait_send()

    return pl.pallas_call(
        body, name=name,
        out_shape=[_sds(l.shape, l.dtype) for l in lands],
        in_specs=[_ANY] * na, out_specs=[_ANY] * na,
        input_output_aliases={a: a for a in range(na)},
        scratch_shapes=[pltpu.SemaphoreType.DMA((3 * na,))] * 2,
    )(*lands)


def _swap_gathered(gath, name):
    na = len(gath)

    def body(*refs):
        gath_refs = refs[na:2 * na]
        send_sems, recv_sems = refs[2 * na:]
        x, y, c = _position()
        cps = [_remote(gath_refs[a].at[0], gath_refs[a].at[1], send_sems.at[a], recv_sems.at[a], (x, y, 1 - c))
               for a in range(na)]
        for cp in cps:
            cp.start()
        for cp in cps:
            cp.wait()

    return pl.pallas_call(
        body, name=name,
        out_shape=[_sds(g.shape, g.dtype) for g in gath],
        in_specs=[_ANY] * na, out_specs=[_ANY] * na,
        input_output_aliases={a: a for a in range(na)},
        scratch_shapes=[pltpu.SemaphoreType.DMA((na,))] * 2,
    )(*gath)


def _adam_gathered(w, gath, m, v, c_arr, name, tr=128):
    rows, cols = w.shape
    hr = rows // 2
    if hr % (2 * tr) == 0:
        tr = 2 * tr
    per = hr // tr

    def body(c_ref, w_ref, g_ref, m_ref, v_ref, go_ref, d_ref, nm_ref, nv_ref):
        g = g_ref[0, 0].astype(F32)
        for k in range(1, N_CHIPS):
            g = g + g_ref[0, k].astype(F32)
        go_ref[...] = g
        d_ref[...], nm_ref[...], nv_ref[...] = _adam_math(w_ref[...], g, m_ref[...], v_ref[...])

    def rows_of(h, i, c_ref):
        c = c_ref[0]
        return ((c + h - 2 * c * h) * per + i, 0)

    blk = pl.BlockSpec((tr, cols), rows_of)
    grid_spec = pltpu.PrefetchScalarGridSpec(
        num_scalar_prefetch=1, grid=(2, per),
        in_specs=[blk, pl.BlockSpec((1, N_CHIPS, tr, cols), lambda h, i, c_ref: (h, 0, i, 0)), blk, blk],
        out_specs=[blk] * 4)
    return pl.pallas_call(
        body, name=name, grid_spec=grid_spec, out_shape=[_sds(w.shape, F32)] * 4,
        compiler_params=_params(("arbitrary", "arbitrary")),
    )(c_arr, w, gath, m, v)


def _allreduce_small(block, name):
    two, r, n = block.shape
    assert two == 2

    def body(x_ref, out_ref, sib, chipsum, gath, d2d_send, d2d_recv, ici_send, ici_recv):
        x, y, c = _position()
        chip = 2 * x + y
        sibling = (x, y, 1 - c)
        first = _remote(x_ref, sib, d2d_send.at[0], d2d_recv.at[0], sibling)
        first.start()
        first.wait()
        chipsum[...] = x_ref[...] + sib[...]
        sends = []
        for j, (fx, fy) in enumerate(_CHIP_FLIPS):
            sends.append(_remote(chipsum.at[c], gath.at[chip], ici_send.at[j], ici_recv.at[j],
                                 (_flip(x, fx), _flip(y, fy), c)))
            sends[-1].start()
        gath[chip] = chipsum[c]
        for j, (fx, fy) in enumerate(_CHIP_FLIPS):
            landed = gath.at[2 * _flip(x, fx) + _flip(y, fy)]
            _remote(landed, landed, ici_send.at[j], ici_recv.at[j], sibling).wait_recv()
        for cp in sends:
            cp.wait_send()
        total = gath[0]
        for k in range(1, N_CHIPS):
            total = total + gath[k]
        out_ref[c] = total
        last = _remote(out_ref.at[c], out_ref.at[c], d2d_send.at[1], d2d_recv.at[1], sibling)
        last.start()
        _remote(out_ref.at[1 - c], out_ref.at[1 - c], d2d_send.at[1], d2d_recv.at[1], sibling).wait_recv()
        last.wait_send()

    vmem = pl.BlockSpec(memory_space=pltpu.VMEM)
    return pl.pallas_call(
        body, name=name, out_shape=_sds(block.shape, F32), in_specs=[vmem], out_specs=vmem,
        scratch_shapes=[pltpu.VMEM(block.shape, F32), pltpu.VMEM(block.shape, F32), pltpu.VMEM((N_CHIPS, r, n), F32),
                        pltpu.SemaphoreType.DMA((2,)), pltpu.SemaphoreType.DMA((2,)),
                        pltpu.SemaphoreType.DMA((3,)), pltpu.SemaphoreType.DMA((3,))],
        compiler_params=pltpu.CompilerParams(vmem_limit_bytes=VMEM_LIMIT_BYTES),
    )(block)


def _cast_place(shards, chip_arr):
    na = len(shards)
    steps = 4

    def body(chip_ref, *refs):
        for a in range(na):
            refs[na + a][0] = refs[a][...].astype(BF16)

    grid_spec = pltpu.PrefetchScalarGridSpec(
        num_scalar_prefetch=1, grid=(steps,),
        in_specs=[pl.BlockSpec((s.shape[0] // steps, s.shape[1]), lambda i, ch: (i, 0)) for s in shards],
        out_specs=[pl.BlockSpec((1, s.shape[0] // steps, s.shape[1]), lambda i, ch: (ch[0], i, 0)) for s in shards])
    return pl.pallas_call(
        body, name="cast_place", grid_spec=grid_spec,
        out_shape=[_sds((N_CHIPS,) + s.shape, BF16) for s in shards],
        compiler_params=_params(("arbitrary",)),
    )(chip_arr, *shards)


def _silu(v):
    return v * _sigmoid(v)


def _ada_fwd(c8, w_ada):
    def body(c_ref, w_ref, o_ref):
        o_ref[...] = jnp.dot(_silu(c_ref[...]), w_ref[...], preferred_element_type=F32,
                             precision=lax.Precision.HIGHEST)

    return pl.pallas_call(
        body, name="ada_fwd", out_shape=_sds((N_DEV, w_ada.shape[1]), F32),
        compiler_params=pltpu.CompilerParams(vmem_limit_bytes=VMEM_LIMIT_BYTES),
    )(c8, w_ada)


def _mod_select(parts, b_ada, me_arr, after):
    cols = parts.shape[2]

    def body(me_ref, p_ref, b_ref, after_ref, o_ref):
        me = me_ref[0]
        for k in range(N_CHIPS):
            cs = slice(k * cols, (k + 1) * cols)
            o_ref[:, cs] = p_ref[2 * k, pl.ds(me, 1), :] + b_ref[:, cs]

    grid_spec = pltpu.PrefetchScalarGridSpec(
        num_scalar_prefetch=1, grid=(1,),
        in_specs=[pl.BlockSpec(parts.shape, lambda i, m: (0, 0, 0)), pl.BlockSpec(b_ada.shape, lambda i, m: (0, 0)),
                  _ANY],
        out_specs=pl.BlockSpec(b_ada.shape, lambda i, m: (0, 0)))
    return pl.pallas_call(body, name="mod_select", grid_spec=grid_spec, out_shape=_sds(b_ada.shape, F32))(
        me_arr, parts, b_ada, after)


def _ada_bwd(c8, dmod8, chip_arr, w, m, v, tr=256):
    d = c8.shape[1]
    cols = dmod8.shape[1] // N_CHIPS

    def body(chip_ref, c_ref, dm_ref, dmall_ref, w_ref, m_ref, v_ref, gw_ref, d_ref, nm_ref, nv_ref, gb_ref):
        g = lax.dot_general(_silu(c_ref[...]), dm_ref[...], (((0,), (0,)), ((), ())),
                            preferred_element_type=F32, precision=lax.Precision.HIGHEST)
        gw_ref[...] = g
        d_ref[...], nm_ref[...], nv_ref[...] = _adam_math(w_ref[...], g, m_ref[...], v_ref[...])
        acc = dmall_ref[0:1, :]
        for k in range(1, N_DEV):
            acc = acc + dmall_ref[k:k + 1, :]
        gb_ref[...] = acc

    rows = pl.BlockSpec((tr, cols), lambda i, ch: (i, 0))
    grid_spec = pltpu.PrefetchScalarGridSpec(
        num_scalar_prefetch=1, grid=(d // tr,),
        in_specs=[pl.BlockSpec((N_DEV, tr), lambda i, ch: (0, i)),
                  pl.BlockSpec((N_DEV, cols), lambda i, ch: (0, ch[0])),
                  pl.BlockSpec(dmod8.shape, lambda i, ch: (0, 0)), rows, rows, rows],
        out_specs=[rows] * 4 + [pl.BlockSpec((1, dmod8.shape[1]), lambda i, ch: (0, 0))])
    return pl.pallas_call(
        body, name="ada_bwd", grid_spec=grid_spec,
        out_shape=[_sds((d, cols), F32)] * 4 + [_sds((1, dmod8.shape[1]), F32)],
        compiler_params=_params(("arbitrary",)),
    )(chip_arr, c8, dmod8, dmod8, w, m, v)


def _adam_math(w, g, m, v):
    m = ADAM_B1 * m + (1.0 - ADAM_B1) * g
    v = ADAM_B2 * v + (1.0 - ADAM_B2) * (g * g)
    m_hat = m / (1.0 - ADAM_B1 ** ADAM_STEP)
    v_hat = v / (1.0 - ADAM_B2 ** ADAM_STEP)
    delta = -ADAM_LR * (m_hat / (jnp.sqrt(v_hat) + ADAM_EPS) + ADAM_WD * w)
    return delta, m, v


def _adam(w, g, m, v, name, tr=256):
    rows, cols = w.shape
    if rows % tr:
        tr = rows

    def body(w_ref, g_ref, m_ref, v_ref, d_ref, nm_ref, nv_ref):
        d_ref[...], nm_ref[...], nv_ref[...] = _adam_math(w_ref[...], g_ref[...], m_ref[...], v_ref[...])

    spec = pl.BlockSpec((tr, cols), lambda i: (i, 0))
    return pl.pallas_call(
        body, name=name, grid=(rows // tr,), in_specs=[spec] * 4, out_specs=[spec] * 3,
        out_shape=[_sds(w.shape, F32)] * 3, compiler_params=_params(("parallel",)),
    )(w, g, m, v)


def _adam_cols(w, g_full, m, v, chip_arr, name):
    rows, cols = w.shape

    def body(chip_ref, w_ref, g_ref, m_ref, v_ref, gs_ref, d_ref, nm_ref, nv_ref):
        g = g_ref[...]
        gs_ref[...] = g
        d_ref[...], nm_ref[...], nv_ref[...] = _adam_math(w_ref[...], g, m_ref[...], v_ref[...])

    own = pl.BlockSpec((rows, cols), lambda i, ch: (0, 0))
    grid_spec = pltpu.PrefetchScalarGridSpec(
        num_scalar_prefetch=1, grid=(1,),
        in_specs=[own, pl.BlockSpec((rows, cols), lambda i, ch: (0, ch[0])), own, own],
        out_specs=[own] * 4)
    return pl.pallas_call(body, name=name, grid_spec=grid_spec, out_shape=[_sds(w.shape, F32)] * 4)(
        chip_arr, w, g_full, m, v)


PACK_COLS = 512
SMALL_REPLICATED = ("g_mix_pre", "g_mix_post", "conv_b", "w_rgate", "b_rgate", "w_igate", "b_igate", "lru_a",
                    "v_norm_g", "v_norm_b", "w_spatial", "b_spatial", "g_lru_out", "g_gmlp_out", "g_ffn_pre",
                    "g_ffn_post", "ffn_conv_b")
SMALL_COLUMN_SHARDED = ("conv_w", "ffn_conv_w")


def _pack(arrays):
    flat = jnp.concatenate([a.reshape(1, -1) for a in arrays], axis=1)
    pad = (-flat.shape[1]) % (2 * LANES)
    if pad:
        flat = jnp.pad(flat, ((0, 0), (0, pad)))
    return flat.reshape(2, -1)


def _unpack(packed, shapes):
    flat = packed.reshape(1, -1)
    out, col = [], 0
    for shape in shapes:
        n = math.prod(shape)
        out.append(flat[:, col:col + n].reshape(shape))
        col += n
    return out


SMALL_ROW_LEN = 86016
_SMALL_ROWS = (
    (("ffn_conv_w", 18432), ("conv_w", 2048), ("w_spatial", 65536)),
    (("w_rgate", 32768), ("w_igate", 32768), ("ffn_conv_b", 6144), ("g_mix_pre", 1024), ("g_mix_post", 1024),
     ("g_ffn_pre", 1024), ("g_ffn_post", 1024), ("conv_b", 512), ("b_rgate", 512), ("b_igate", 512),
     ("lru_a", 512), ("v_norm_g", 512), ("v_norm_b", 512), ("b_spatial", 512), ("g_lru_out", 512),
     ("g_gmlp_out", 512), ("loss", 128)),
)


def _small_slots():
    slots = {}
    for row, entries in enumerate(_SMALL_ROWS):
        off = 0
        for name, size in entries:
            slots[name] = (row, off)
            off += size
        assert off <= SMALL_ROW_LEN
    return slots


SMALL_SLOT = _small_slots()
SMALL_LANES = SMALL_ROW_LEN // SUBLANES


def _small_pieces(name, first, count):
    row, off = SMALL_SLOT[name]
    pos, pieces = off + first, []
    while count:
        sub, lane = divmod(pos, SMALL_LANES)
        n = min(count, SMALL_LANES - lane)
        pieces.append((row, sub, lane, n))
        pos, count = pos + n, count - n
    return pieces
ROW_VECTORS = ("ffn_conv_b", "g_mix_pre", "g_mix_post", "g_ffn_pre", "g_ffn_post", "conv_b", "lru_a", "v_norm_g",
               "v_norm_b", "g_lru_out", "g_gmlp_out")
HEAD_DIM = LRU_WIDTH // LRU_HEADS


def _pack_small(g, after):
    order = ("ffn_conv_w", "conv_w", "w_spatial", "w_rgate", "w_igate", "b_rgate", "b_igate", "b_spatial", "loss") \
        + ROW_VECTORS
    vmem = pl.BlockSpec(memory_space=pltpu.VMEM)

    def body(*refs):
        src = dict(zip(order, refs))
        out_ref = refs[len(order) + 1]
        out_ref[...] = jnp.zeros_like(out_ref)

        def put(name, first, val):
            col = 0
            for row, sub, lane, n in _small_pieces(name, first, val.shape[1]):
                out_ref[row, sub:sub + 1, lane:lane + n] = val[:, col:col + n]
                col += n

        for name in ROW_VECTORS + ("b_rgate", "b_igate", "loss"):
            put(name, 0, src[name][...])
        for name in ("ffn_conv_w", "conv_w"):
            k_taps, n = src[name].shape
            for k in range(k_taps):
                put(name, k * n, src[name][k:k + 1, :])
        for g_idx in range(GMLP_GROUPS):
            for i in range(GMLP_BLOCK):
                put("w_spatial", (g_idx * GMLP_BLOCK + i) * GMLP_BLOCK, src["w_spatial"][g_idx, i:i + 1, :])
        for name in ("w_rgate", "w_igate"):
            for h in range(LRU_HEADS):
                for i in range(HEAD_DIM):
                    r = h * HEAD_DIM + i
                    put(name, r * HEAD_DIM, src[name][r:r + 1, h * HEAD_DIM:(h + 1) * HEAD_DIM])
        eye = (lax.broadcasted_iota(jnp.int32, (GMLP_BLOCK, GMLP_BLOCK), 0)
               == lax.broadcasted_iota(jnp.int32, (GMLP_BLOCK, GMLP_BLOCK), 1))
        for g_idx in range(GMLP_GROUPS):
            col = src["b_spatial"][:, g_idx:g_idx + 1]
            put("b_spatial", g_idx * GMLP_BLOCK, _colsum(jnp.where(eye, col, 0.0)))

    return pl.pallas_call(
        body, name="pack_small", out_shape=_sds((2, SUBLANES, SMALL_LANES), F32),
        in_specs=[vmem] * len(order) + [_ANY], out_specs=vmem,
        compiler_params=pltpu.CompilerParams(vmem_limit_bytes=VMEM_LIMIT_BYTES),
    )(*[g[n] for n in order], after)


def _adam_small(g_small, w, m, v):
    vmem = pl.BlockSpec(memory_space=pltpu.VMEM)
    n_p = len(SMALL_REPLICATED)

    def body(g_ref, *refs):
        w_refs, m_refs, v_refs = refs[:n_p], refs[n_p:2 * n_p], refs[2 * n_p:3 * n_p]
        outs = refs[3 * n_p:]
        go, do, mo, vo = outs[:n_p], outs[n_p:2 * n_p], outs[2 * n_p:3 * n_p], outs[3 * n_p:]
        for k, name in enumerate(SMALL_REPLICATED):
            def take(first, count, name=name):
                parts = [g_ref[row, sub:sub + 1, lane:lane + n]
                         for row, sub, lane, n in _small_pieces(name, first, count)]
                return parts[0] if len(parts) == 1 else jnp.concatenate(parts, axis=1)

            shape = w_refs[k].shape
            if name in ROW_VECTORS:
                go[k][...] = take(0, shape[1])
            elif name in ("b_rgate", "b_igate"):
                for h in range(LRU_HEADS):
                    go[k][0, h:h + 1, :] = take(h * HEAD_DIM, HEAD_DIM)
            elif name == "b_spatial":
                for g_idx in range(GMLP_GROUPS):
                    go[k][0, g_idx:g_idx + 1, :] = take(g_idx * GMLP_BLOCK, GMLP_BLOCK)
            elif name == "w_spatial":
                for g_idx in range(GMLP_GROUPS):
                    for i in range(GMLP_BLOCK):
                        go[k][0, g_idx, i:i + 1, :] = take((g_idx * GMLP_BLOCK + i) * GMLP_BLOCK, GMLP_BLOCK)
            else:
                for h in range(LRU_HEADS):
                    for i in range(HEAD_DIM):
                        go[k][0, h, i:i + 1, :] = take((h * HEAD_DIM + i) * HEAD_DIM, HEAD_DIM)
            do[k][...], mo[k][...], vo[k][...] = _adam_math(w_refs[k][...], go[k][...], m_refs[k][...],
                                                             v_refs[k][...])

    names = SMALL_REPLICATED
    out_shape = [_sds(w[n].shape, F32) for n in names] * 4
    res = pl.pallas_call(
        body, name="adam_small", out_shape=out_shape,
        in_specs=[vmem] * (1 + 3 * n_p), out_specs=[vmem] * (4 * n_p),
        compiler_params=pltpu.CompilerParams(vmem_limit_bytes=VMEM_LIMIT_BYTES),
    )(g_small, *[w[n] for n in names], *[m[n] for n in names], *[v[n] for n in names])
    return [dict(zip(names, res[k * n_p:(k + 1) * n_p])) for k in range(4)]


def _adam_cols(name, g_small, w, m, v, chip_arr):
    _, k_taps, n = w.shape
    row, off = SMALL_SLOT[name]
    first = off // n
    per_sub = SMALL_LANES // n

    def body(chip_ref, *refs):
        g_refs = refs[:k_taps]
        w_ref, m_ref, v_ref, go_ref, d_ref, nm_ref, nv_ref = refs[k_taps:]
        for k in range(k_taps):
            tap = (0, slice(k, k + 1), slice(None))
            sub = (first + N_CHIPS * k + chip_ref[0]) // per_sub
            g = g_refs[k][row, pl.ds(sub, 1), :]
            go_ref[tap] = g
            d_ref[tap], nm_ref[tap], nv_ref[tap] = _adam_math(w_ref[tap], g, m_ref[tap], v_ref[tap])

    whole = pl.BlockSpec(w.shape, lambda i, ch: (0, 0, 0))
    taps = [pl.BlockSpec((2, SUBLANES, n),
                         functools.partial(lambda i, ch, k: (0, 0, (first + N_CHIPS * k + ch[0]) % per_sub), k=k))
            for k in range(k_taps)]
    grid_spec = pltpu.PrefetchScalarGridSpec(
        num_scalar_prefetch=1, grid=(1,), in_specs=taps + [whole] * 3, out_specs=[whole] * 4)
    return pl.pallas_call(body, name="adam_" + name, grid_spec=grid_spec, out_shape=[_sds(w.shape, F32)] * 4)(
        chip_arr, *[g_small] * k_taps, w, m, v)


def kernel(x, c, w_ada, b_ada, g_mix_pre, g_mix_post, w_in, conv_w, conv_b, w_rgate, b_rgate, w_igate, b_igate, lru_a, v_norm_g, v_norm_b, w_spatial, b_spatial, g_lru_out, g_gmlp_out, w_out, g_ffn_pre, g_ffn_post, w_up, ffn_conv_w, ffn_conv_b, w_down, loss_target, m_w_ada, m_b_ada, m_g_mix_pre, m_g_mix_post, m_w_in, m_conv_w, m_conv_b, m_w_rgate, m_b_rgate, m_w_igate, m_b_igate, m_lru_a, m_v_norm_g, m_v_norm_b, m_w_spatial, m_b_spatial, m_g_lru_out, m_g_gmlp_out, m_w_out, m_g_ffn_pre, m_g_ffn_post, m_w_up, m_ffn_conv_w, m_ffn_conv_b, m_w_down, v_w_ada, v_b_ada, v_g_mix_pre, v_g_mix_post, v_w_in, v_conv_w, v_conv_b, v_w_rgate, v_b_rgate, v_w_igate, v_b_igate, v_lru_a, v_v_norm_g, v_v_norm_b, v_w_spatial, v_b_spatial, v_g_lru_out, v_g_gmlp_out, v_w_out, v_g_ffn_pre, v_g_ffn_post, v_w_up, v_ffn_conv_w, v_ffn_conv_b, v_w_down):
    args = dict(locals())
    names = ("w_ada", "b_ada", "g_mix_pre", "g_mix_post", "w_in", "conv_w", "conv_b", "w_rgate", "b_rgate",
             "w_igate", "b_igate", "lru_a", "v_norm_g", "v_norm_b", "w_spatial", "b_spatial", "g_lru_out",
             "g_gmlp_out", "w_out", "g_ffn_pre", "g_ffn_post", "w_up", "ffn_conv_w", "ffn_conv_b", "w_down")
    drop = lambda a: a if a.ndim == 2 else a[0]
    w = {n: drop(args[n]) for n in names}
    m = {n: drop(args["m_" + n]) for n in names}
    v = {n: drop(args["v_" + n]) for n in names}
    xi, yi, ci = _position()
    me_arr = jnp.reshape(4 * xi + 2 * yi + ci, (1,)).astype(jnp.int32)
    chip_arr = jnp.reshape(2 * xi + yi, (1,)).astype(jnp.int32)
    c_arr = jnp.reshape(ci, (1,)).astype(jnp.int32)
    pos_arr = jnp.stack([ci, 2 * xi + yi]).astype(jnp.int32)

    big = ("w_in", "w_out", "w_up", "w_down")
    lands = _cast_place([w[n] for n in big], chip_arr)
    start_a, wait_a = _gather_plan([w[n].shape[0] for n in big[:2]])
    start_b, wait_b = _gather_plan([w[n].shape[0] for n in big[2:]])

    row0 = jnp.concatenate([c, w["conv_w"].reshape(1, -1), w["ffn_conv_w"].reshape(1, -1)], axis=1)
    g0 = _allgather8(row0, "gather_cond", False)[:, 0, :]
    c8 = g0[:, :D_MODEL]
    per_chip = g0[0::2]
    conv_w_full = per_chip[:, D_MODEL:D_MODEL + 512].reshape(N_CHIPS, 4, 128).transpose(1, 0, 2).reshape(4, 512)
    ffn_conv_w_full = per_chip[:, D_MODEL + 512:].reshape(N_CHIPS, 3, 1536).transpose(1, 0, 2).reshape(3, 2 * D_FF)
    mod_parts = _allgather8(_ada_fwd(c8, w["w_ada"]), "gather_mod", False)
    send_a, recv_a, _, lands_a, token_a = _split_start([], lands[:2], start_a, 6, mod_parts, "gather_start_a")
    send_b, recv_b, _, lands_b, token_b = _split_start([], lands[2:], start_b, 6, token_a, "gather_start_b")
    mod = _mod_select(mod_parts, w["b_ada"].reshape(1, -1), me_arr, token_b).reshape(N_MOD, D_MODEL)
    sh_m, sc_m, gt_m, sh_f, sc_f, gt_f = [mod[k:k + 1] for k in range(N_MOD)]

    small = {n: w[n] for n in SMALL_REPLICATED}
    small["conv_w"] = conv_w_full
    small["ffn_conv_w"] = ffn_conv_w_full
    row = lambda a: a.reshape(1, -1)
    seq_params, ws_t = _seq_params(small)
    glo, ggo = row(small["g_lru_out"]), row(small["g_gmlp_out"])
    g_pre, g_post = row(small["g_mix_pre"]), row(small["g_mix_post"])
    g_pre2, g_post2 = row(small["g_ffn_pre"]), row(small["g_ffn_post"])
    fw, fb = small["ffn_conv_w"], row(small["ffn_conv_b"])
    xs, tgt = x[0], loss_target[0]

    _, lands_a = _split_wait(send_a, recv_a, [], lands_a, wait_a, mod, "gather_wait_a")
    w_in4, w_out4 = _forward_to_sibling(lands_a, "forward_a")
    w_out_b = w_out4.reshape(D_MODEL, D_MODEL)
    z, h = _mix_in(xs, sc_m, sh_m, g_pre, w_in4)
    ycat, hst, stash = _seqmix(z, seq_params, glo, ggo)
    _, lands_b = _split_wait(send_b, recv_b, [], lands_b, wait_b, ycat, "gather_wait_b")
    fwd_start, fwd_wait = _forward_plan([w[n].shape[0] for n in big[2:]])
    fwd_send, fwd_recv, _, lands_b, tok = _split_start([], lands_b, fwd_start, 6, pos_arr, "forward_start_b")
    y, x1, h2 = _mix_out(ycat, xs, w_out_b, gt_m + tok[0:1, 0:1], g_post, g_pre2, sc_f, sh_f)
    _, (w_up4, w_down4) = _split_wait(fwd_send, fwd_recv, [], lands_b, fwd_wait, h2, "forward_wait_b")
    w_down_b = w_down4.reshape(D_FF, D_MODEL)
    up0, pre, act, dy2, dx2, loss, dgt_f, dg_post2 = _ffn_fwd(h2, x1, tgt, w_up4, w_down_b, fw, fb, gt_f, g_post2)

    dup0, dfw, dfb = _ffn_bwd_a(dy2, pre, up0, w_down_b, fw)
    gw_up = _wgrad(h2, dup0, N_CHIPS, "wgrad_up", True)
    gw_down = _wgrad(act, dy2, 2, "wgrad_down", False)
    ex_start, ex_wait = _exchange_plan(2)
    sg_start, sg_wait = _swap_gathered_plan(2)
    grads, deltas, new_m, new_v = {}, {}, {}, {}

    def swap_start(parts, name):
        sw_start, sw_wait = _swap_halves_plan([p.shape[1] // 2 for p in parts])
        recv = [lax.empty((N_CHIPS, p.shape[1] // 2, p.shape[2]), BF16) for p in parts]
        send_s, recv_s, parts, recv, token = _split_start(parts, recv, sw_start, len(parts), pos_arr,
                                                           "swap_start_" + name)
        return (send_s, recv_s, parts, recv, sw_wait), token

    def exchange_start(swap, tags, after, name):
        send_s, recv_s, parts, recv, sw_wait = swap
        parts, recv = _split_wait(send_s, recv_s, parts, recv, sw_wait, after, "swap_wait_" + name)
        both = [_chip_sum(p, r, pos_arr, "chip_sum_" + t) for p, r, t in zip(parts, recv, tags)]
        sums, gath = [b[0] for b in both], [b[1] for b in both]
        return _split_start(sums, gath, ex_start, 3 * len(parts), pos_arr, "exchange_start_" + name)

    def gathered_start(exchange, after, name):
        send_s, recv_s, sums, gath, _ = exchange
        _, gath = _split_wait(send_s, recv_s, sums, gath, ex_wait, after, "exchange_wait_" + name)
        send_s, recv_s, _, gath, token = _split_start([], gath, sg_start, len(gath), pos_arr,
                                                      "gathered_start_" + name)
        return (send_s, recv_s, gath), token

    def finish(gathered, tags, after, name):
        send_s, recv_s, gath = gathered
        _, gath = _split_wait(send_s, recv_s, [], gath, sg_wait, after, "gathered_wait_" + name)
        for g, t in zip(gath, tags):
            grads[t], deltas[t], new_m[t], new_v[t] = _adam_gathered(w[t], g, m[t], v[t], c_arr, "adam_" + t)

    def behind(value, token):
        return value + token[0:1, 0:1]

    tags_b, tags_a = ("w_up", "w_down"), ("w_in", "w_out")
    swap_b, tok = swap_start([gw_up, gw_down.reshape(N_CHIPS, -1, D_MODEL)], "b")
    dx1, dy, dsh_f, dsc_f, dg_pre2, dgt_m, dg_post = _ffn_bwd_b(
        dup0, x1, y, dx2, w_up4, g_pre2, behind(sc_f, tok), sh_f, gt_m, g_post)
    exchange_b = exchange_start(swap_b, tags_b, dg_post, "b")
    (dz, dcw, dcb, dwr, dwi, dbr, dbi, dspa, dng, dnb, dws, dbs_t, dglo, dggo) = _seqmix_bwd(
        z, hst, stash, dy, w_out_b, seq_params, ws_t, behind(glo, exchange_b[4]), ggo)
    grad_x, dsh_m, dsc_m, dg_pre = _mix_in_bwd(xs, dz, dx1, w_in4, g_pre, sc_m)
    gw_in = _wgrad(h, dz, N_CHIPS, "wgrad_in", True)
    gw_out = _wgrad(ycat, dy, 1, "wgrad_out", False)
    swap_a, tok = swap_start([gw_in, gw_out.reshape(N_CHIPS, -1, D_MODEL)], "a")

    dmod = jnp.concatenate([behind(dsh_m, tok), dsc_m, dgt_m, dsh_f, dsc_f, dgt_f], axis=1)
    dmod8 = _allgather8(dmod, "gather_dmod", False)[:, 0, :]
    small_grads = dict(
        g_mix_pre=dg_pre, g_mix_post=dg_post, conv_w=dcw, conv_b=dcb, w_rgate=dwr, b_rgate=dbr, w_igate=dwi,
        b_igate=dbi, lru_a=dspa, v_norm_g=dng, v_norm_b=dnb, w_spatial=dws, b_spatial=dbs_t, g_lru_out=dglo,
        g_gmlp_out=dggo, g_ffn_pre=dg_pre2, g_ffn_post=dg_post2, ffn_conv_w=dfw, ffn_conv_b=dfb,
        loss=loss)
    g_small = _allreduce_small(_pack_small(small_grads, dmod8), "reduce_small")
    total = g_small[_small_pieces("loss", 0, 1)[0][:3]]
    exchange_a = exchange_start(swap_a, tags_a, g_small, "a")
    gathered_b, tok = gathered_start(exchange_b, exchange_a[4], "b")

    grads["w_ada"], deltas["w_ada"], new_m["w_ada"], new_v["w_ada"], g_b_ada = _ada_bwd(
        c8, behind(dmod8, tok), chip_arr, w["w_ada"], m["w_ada"], v["w_ada"])
    rep = SMALL_REPLICATED
    small_out = _adam_small(g_small, {n: args[n] for n in rep}, {n: args["m_" + n] for n in rep},
                            {n: args["v_" + n] for n in rep})
    for n in rep:
        grads[n], deltas[n], new_m[n], new_v[n] = [group[n] for group in small_out]
    finish(gathered_b, tags_b, deltas[rep[0]], "b")

    gathered_a, tok = gathered_start(exchange_a, deltas["w_down"], "a")
    for n in SMALL_COLUMN_SHARDED:
        grads[n], deltas[n], new_m[n], new_v[n] = _adam_cols(n, g_small, args[n], args["m_" + n],
                                                             behind(args["v_" + n], tok), chip_arr)
    d_b, m_b, v_b = _adam(w["b_ada"], g_b_ada, m["b_ada"], behind(v["b_ada"], tok), "adam_b_ada")
    grads["b_ada"], deltas["b_ada"], new_m["b_ada"], new_v["b_ada"] = g_b_ada, d_b, m_b, v_b
    finish(gathered_a, tags_a, d_b, "a")

    outs = [total, grad_x[None]]
    for group in (grads, deltas, new_m, new_v):
        outs.extend(group[n].reshape(args[n].shape) for n in names)
    return tuple(outs)
```

```python
import functools
import math

import jax
import jax.numpy as jnp
from jax import lax
from jax.experimental import pallas as pl
from jax.experimental.pallas import tpu as pltpu

F32 = jnp.float32
BF16 = jnp.bfloat16
MESH = pl.DeviceIdType.MESH

D_MODEL = 1024
LRU_WIDTH = 512
LRU_HEADS = 8
GMLP_WIDTH = 512
GMLP_GROUPS = 4
GMLP_BLOCK = 128
CHUNK = 64
D_FF = 3072
N_MOD = 6
EPS = 1e-6
LRU_C = 8.0
N_CHIPS = 4
N_DEV = 8

ADAM_LR = 0.001
ADAM_B1 = 0.9
ADAM_B2 = 0.999
ADAM_EPS = 1e-08
ADAM_WD = 0.01
ADAM_STEP = 10

GELU_C0 = math.sqrt(2.0 / math.pi)
GELU_C1 = 0.044715

VMEM_LIMIT_BYTES = 56 * 1024 * 1024
SUBLANES = 8
LANES = 128
BF16_SUBLANES = 16
FFN_CHUNK = 768
SUB_ROWS = 256


def _gelu_gate(x):
    x2 = x * x
    z = x * ((2.0 * GELU_C0 * GELU_C1) * x2 + 2.0 * GELU_C0)
    return 1.0 / (1.0 + jnp.exp(-z)), x2


def _gelu(x):
    t = jnp.tanh(GELU_C0 * (x + GELU_C1 * x * x * x))
    return 0.5 * x * (1.0 + t)


def _gelu_and_grad(x):
    s, x2 = _gelu_gate(x)
    g = x * s
    dz = (6.0 * GELU_C0 * GELU_C1) * x2 + 2.0 * GELU_C0
    return g, s + g * (1.0 - s) * dz


def _sigmoid(x):
    return 1.0 / (1.0 + jnp.exp(-x))


def _log1p(u):
    w = 1.0 + u
    return jnp.where(w == 1.0, u, jnp.log(w) * (u / (w - 1.0)))


def _softplus(x):
    return jnp.maximum(x, 0.0) + _log1p(jnp.exp(-jnp.abs(x)))


def _neg_expm1(x):
    u = jnp.exp(x)
    um1 = u - 1.0
    tiny = um1 == 0.0
    small = um1 * (x / jnp.log(jnp.where(tiny, 2.0, jnp.maximum(u, 0.25))))
    return -jnp.where(tiny, x, jnp.where(x < -1.0, um1, small))


def _msq_rsqrt(v):
    return lax.rsqrt(jnp.mean(v * v, axis=-1, keepdims=True) + EPS)


def _rms_bwd(dyn, yn, r):
    return r * (dyn - yn * jnp.mean(dyn * yn, axis=-1, keepdims=True))


def _colsum(v):
    return jnp.sum(v, axis=0, keepdims=True)


def _shift_down(cur, prev8, k):
    rolled = pltpu.roll(cur, k, 0)
    head = pltpu.roll(prev8, k, 0)
    row8 = lax.broadcasted_iota(jnp.int32, (SUBLANES, cur.shape[1]), 0)
    first = jnp.where(row8 < k, head, rolled[0:SUBLANES])
    return jnp.concatenate([first, rolled[SUBLANES:]], axis=0)


def _shift_up(cur, next8, k):
    t = cur.shape[0]
    rolled = pltpu.roll(cur, t - k, 0)
    tail = pltpu.roll(next8, SUBLANES - k, 0)
    row8 = lax.broadcasted_iota(jnp.int32, (SUBLANES, cur.shape[1]), 0)
    last = jnp.where(row8 >= SUBLANES - k, tail, rolled[t - SUBLANES:])
    return jnp.concatenate([rolled[:t - SUBLANES], last], axis=0)


def _scan_fwd(a, b):
    t = a.shape[0]
    row = lax.broadcasted_iota(jnp.int32, a.shape, 0)
    d = 1
    while d < t:
        keep = row >= d
        a_s = jnp.where(keep, pltpu.roll(a, d, 0), 1.0)
        b_s = jnp.where(keep, pltpu.roll(b, d, 0), 0.0)
        b = a * b_s + b
        a = a * a_s
        d *= 2
    return a, b


def _scan_bwd(a, g):
    t = a.shape[0]
    row = lax.broadcasted_iota(jnp.int32, a.shape, 0)
    d = 1
    while d < t:
        keep = row < t - d
        a_s = jnp.where(keep, pltpu.roll(a, t - d, 0), 1.0)
        g_s = jnp.where(keep, pltpu.roll(g, t - d, 0), 0.0)
        g = a * g_s + g
        a = a * a_s
        d *= 2
    return a, g


def _dot(a, b):
    return jnp.dot(a, b, preferred_element_type=F32)


def _dot_nt(a, b):
    return lax.dot_general(a, b, (((1,), (1,)), ((), ())), preferred_element_type=F32)


def _dot_tn(a, b):
    return lax.dot_general(a, b, (((0,), (0,)), ((), ())), preferred_element_type=F32)


def _rows(ts, cols, rev_of=None):
    if rev_of is None:
        return pl.BlockSpec((ts, cols), lambda i: (i, 0))
    return pl.BlockSpec((ts, cols), lambda i: (rev_of - 1 - i, 0))


def _halo_prev(ts, cols, halo, rev_of=None, col_block=0):
    per = ts // halo
    if rev_of is None:
        return pl.BlockSpec((halo, cols), lambda i: (jnp.maximum(i * per - 1, 0), col_block))
    return pl.BlockSpec((halo, cols), lambda i: (jnp.maximum((rev_of - 1 - i) * per - 1, 0), col_block))


def _full(shape):
    nd = len(shape)
    return pl.BlockSpec(shape, lambda *_: (0,) * nd)


_RESIDENT = pl.BlockSpec(memory_space=pltpu.VMEM)


def _params(sem):
    return pltpu.CompilerParams(dimension_semantics=sem, vmem_limit_bytes=VMEM_LIMIT_BYTES)


def _sds(shape, dtype):
    return jax.ShapeDtypeStruct(shape, dtype)


def _sub_tiles(ts):
    return [slice(r0, r0 + SUB_ROWS) for r0 in range(0, ts, SUB_ROWS)]


def _mix_in(x, sc, sh, g, w_in4, ts=512):
    s, d = x.shape

    def body(x_ref, sc_ref, sh_ref, g_ref, w_ref, z_ref, h_ref):
        for rs in _sub_tiles(ts):
            xv = x_ref[rs, :]
            h = (xv * _msq_rsqrt(xv) * g_ref[...]) * (1.0 + sc_ref[...]) + sh_ref[...]
            hb = h.astype(BF16)
            h_ref[rs, :] = hb
            for k in range(N_CHIPS):
                z_ref[rs, k * 512:(k + 1) * 512] = _dot(hb, w_ref[k])

    return pl.pallas_call(
        body, grid=(s // ts,), name="mix_in",
        in_specs=[_rows(ts, d), _full((1, d)), _full((1, d)), _full((1, d)), _full(w_in4.shape)],
        out_specs=[_rows(ts, 2048), _rows(ts, d)],
        out_shape=[_sds((s, 2048), F32), _sds((s, d), BF16)],
        compiler_params=_params(("parallel",)),
    )(x, sc, sh, g, w_in4)


N_STASH = 12
(ST_XC, ST_R, ST_IG, ST_A, ST_MULT, ST_GL, ST_DGL, ST_U, ST_DU, ST_Q, ST_VHAT, ST_SPB) = range(N_STASH)


def _seq_param_specs():
    return [_full((4, 512)), _full((1, 512)), _full((512, 512)), _full((512, 512)), _full((1, 512)),
            _full((1, 512)), _full((1, 512)), _full((1, 512)), _full((1, 512)), _full((4, 128, 128)),
            _full((128, 4))]


def _seqmix(z, seq_params, glo, ggo, ts=256):
    s = z.shape[0]
    nt = s // ts

    def body(z_ref, zprev_ref, cw_ref, cb_ref, bdr_ref, bdi_ref, br_ref, bi_ref, la_ref, ng_ref, nb_ref,
             ws_ref, bst_ref, glo_ref, ggo_ref, ycat_ref, hst_ref, st_ref, hcarry, sp_scr):
        i = pl.program_id(0)

        @pl.when(i == 0)
        def _():
            hcarry[...] = jnp.zeros_like(hcarry)

        lx = z_ref[:, 0:512]
        prev8 = jnp.where(i == 0, 0.0, zprev_ref[...])
        xc = (cw_ref[3:4, :] * lx + cw_ref[2:3, :] * _shift_down(lx, prev8, 1)
              + cw_ref[1:2, :] * _shift_down(lx, prev8, 2) + cw_ref[0:1, :] * _shift_down(lx, prev8, 3)
              + cb_ref[...])
        xcb = xc.astype(BF16)
        r = _sigmoid(_dot(xcb, bdr_ref[...]) + br_ref[...])
        ig = _sigmoid(_dot(xcb, bdi_ref[...]) + bi_ref[...])
        log_a = (-LRU_C) * r * _softplus(-la_ref[...])
        a = jnp.exp(log_a)
        mult = jnp.sqrt(_neg_expm1(2.0 * log_a))
        acum, hloc = _scan_fwd(a, mult * (ig * xc))
        h = hloc + acum * hcarry[...]
        hcarry[...] = h[ts - 1:ts, :]
        hst_ref[...] = h
        gl, dgl = _gelu_and_grad(z_ref[:, 512:1024])
        y_l = h * gl
        for slot, val in ((ST_XC, xc), (ST_R, r), (ST_IG, ig), (ST_A, a), (ST_MULT, mult), (ST_GL, gl),
                          (ST_DGL, dgl)):
            st_ref[slot] = val

        u, du = _gelu_and_grad(z_ref[:, 1024:1536])
        vg, dvg = _gelu_and_grad(z_ref[:, 1536:2048])
        vc = vg - jnp.mean(vg, axis=-1, keepdims=True)
        rstd = lax.rsqrt(jnp.mean(vc * vc, axis=-1, keepdims=True) + EPS)
        vhat = vc * rstd
        vb = (vhat * ng_ref[...] + nb_ref[...]).astype(BF16)
        for n in range(ts // GMLP_BLOCK):
            rs = slice(n * GMLP_BLOCK, (n + 1) * GMLP_BLOCK)
            for g in range(GMLP_GROUPS):
                cs = slice(g * 128, (g + 1) * 128)
                sp_scr[rs, cs] = _dot(ws_ref[g], vb[rs, cs]) + bst_ref[:, g:g + 1]
        spb = sp_scr[...]
        y_g = u * spb
        for slot, val in ((ST_U, u), (ST_DU, du), (ST_Q, rstd * dvg), (ST_VHAT, vhat), (ST_SPB, spb)):
            st_ref[slot] = val

        ycat_ref[:, 0:512] = (y_l * _msq_rsqrt(y_l) * glo_ref[...]).astype(BF16)
        ycat_ref[:, 512:1024] = (y_g * _msq_rsqrt(y_g) * ggo_ref[...]).astype(BF16)

    return pl.pallas_call(
        body, grid=(nt,), name="seqmix",
        in_specs=[_rows(ts, 2048), _halo_prev(ts, 512, SUBLANES)] + _seq_param_specs()
        + [_full((1, 512)), _full((1, 512))],
        out_specs=[_rows(ts, 1024), _rows(ts, 512), pl.BlockSpec((N_STASH, ts, 512), lambda i: (0, i, 0))],
        out_shape=[_sds((s, 1024), BF16), _sds((s, 512), F32), _sds((N_STASH, s, 512), F32)],
        scratch_shapes=[pltpu.VMEM((1, 512), F32), pltpu.VMEM((ts, 512), F32)],
        compiler_params=_params(("arbitrary",)),
    )(z, z, *seq_params, glo, ggo)


def _mix_out(ycat, x, w_out, gt_m, g_post, g_pre2, sc_f, sh_f, ts=512):
    s, d = x.shape

    def body(yc_ref, x_ref, w_ref, gt_ref, gp_ref, g2_ref, sc_ref, sh_ref, y_ref, x1_ref, h2_ref):
        for rs in _sub_tiles(ts):
            y = _dot(yc_ref[rs, :], w_ref[...])
            y_ref[rs, :] = y
            x1 = x_ref[rs, :] + gt_ref[...] * (y * _msq_rsqrt(y) * gp_ref[...])
            x1_ref[rs, :] = x1
            h2 = (x1 * _msq_rsqrt(x1) * g2_ref[...]) * (1.0 + sc_ref[...]) + sh_ref[...]
            h2_ref[rs, :] = h2.astype(BF16)

    vec = _full((1, d))
    return pl.pallas_call(
        body, grid=(s // ts,), name="mix_out",
        in_specs=[_rows(ts, d), _rows(ts, d), _full((d, d)), vec, vec, vec, vec, vec],
        out_specs=[_rows(ts, d), _rows(ts, d), _rows(ts, d)],
        out_shape=[_sds((s, d), F32), _sds((s, d), F32), _sds((s, d), BF16)],
        compiler_params=_params(("parallel",)),
    )(ycat, x, w_out, gt_m, g_post, g_pre2, sc_f, sh_f)


def _ffn_cols(j):
    per = (2 * D_FF // N_CHIPS) // FFN_CHUNK
    return j // per, (j % per) * FFN_CHUNK, j * FFN_CHUNK


def _ffn_fwd(h2, x1, tgt, w_up4, w_down, fw, fb, gt_f, g_post, ts=256):
    s, d = x1.shape
    nch = D_FF // FFN_CHUNK

    def body(h2_ref, x1_ref, tgt_ref, wup_ref, wdn_ref, fw_ref, fb_ref, gt_ref, gp_ref,
             up0_ref, pre_ref, act_ref, dy2_ref, dx2_ref, loss_ref, dgt_ref, dgp_ref, tail_ref):
        i = pl.program_id(0)

        @pl.when(i == 0)
        def _():
            tail_ref[...] = jnp.zeros_like(tail_ref)
            loss_ref[...] = jnp.zeros_like(loss_ref)
            dgt_ref[...] = jnp.zeros_like(dgt_ref)
            dgp_ref[...] = jnp.zeros_like(dgp_ref)

        hb = h2_ref[...]

        def up_project(j):
            sh_g, off, _ = _ffn_cols(j)
            return [_dot(hb, wup_ref[shard, :, off:off + FFN_CHUNK]).astype(BF16) for shard in (sh_g, sh_g + 2)]

        y2 = jnp.zeros((ts, d), F32)
        ahead = up_project(0)
        for j in range(nch):
            _, _, col = _ffn_cols(j)
            ubs = ahead
            if j + 1 < nch:
                ahead = up_project(j + 1)
            halves = []
            for ub, c0 in zip(ubs, (col, D_FF + col)):
                cs = slice(c0, c0 + FFN_CHUNK)
                up0_ref[:, cs] = ub
                u = ub.astype(F32)
                prev8 = tail_ref[:, cs]
                tail_ref[:, cs] = u[ts - SUBLANES:, :]
                halves.append(fw_ref[2:3, cs] * u + fw_ref[1:2, cs] * _shift_down(u, prev8, 1)
                              + fw_ref[0:1, cs] * _shift_down(u, prev8, 2) + fb_ref[:, cs])
                pre_ref[:, cs] = halves[-1].astype(BF16)
            act = (_gelu(halves[0]) * halves[1]).astype(BF16)
            act_ref[:, col:col + FFN_CHUNK] = act
            y2 = y2 + _dot(act, wdn_ref[col:col + FFN_CHUNK, :])
        r2 = _msq_rsqrt(y2)
        yn = y2 * r2
        yng = yn * gp_ref[...]
        e = x1_ref[...] + gt_ref[...] * yng - tgt_ref[...]
        loss_ref[...] += jnp.sum(e * e) * (0.5 / d)
        dx2 = e * (1.0 / d)
        dx2_ref[...] = dx2
        dgt_ref[...] += _colsum(dx2 * yng)
        dyng = dx2 * gt_ref[...]
        dgp_ref[...] += _colsum(dyng * yn)
        dy2_ref[...] = _rms_bwd(dyng * gp_ref[...], yn, r2).astype(BF16)

    vec = _full((1, d))
    return pl.pallas_call(
        body, grid=(s // ts,), name="ffn_fwd",
        in_specs=[_rows(ts, d), _rows(ts, d), _rows(ts, d), _RESIDENT, _RESIDENT,
                  _full((3, 2 * D_FF)), _full((1, 2 * D_FF)), vec, vec],
        out_specs=[_rows(ts, 2 * D_FF), _rows(ts, 2 * D_FF), _rows(ts, D_FF), _rows(ts, d), _rows(ts, d),
                   _full((1, 128)), vec, vec],
        out_shape=[_sds((s, 2 * D_FF), BF16), _sds((s, 2 * D_FF), BF16), _sds((s, D_FF), BF16), _sds((s, d), BF16),
                   _sds((s, d), F32), _sds((1, 128), F32), _sds((1, d), F32), _sds((1, d), F32)],
        scratch_shapes=[pltpu.VMEM((SUBLANES, 2 * D_FF), F32)],
        compiler_params=_params(("arbitrary",)),
    )(h2, x1, tgt, w_up4, w_down, fw, fb, gt_f, g_post)


def _shift_up_mxu(vb, up_mat, next8, k):
    t = vb.shape[0]
    main = _dot(up_mat, vb)
    tail = pltpu.roll(next8, SUBLANES - k, 0)
    row8 = lax.broadcasted_iota(jnp.int32, next8.shape, 0)
    last = main[t - SUBLANES:] + jnp.where(row8 >= SUBLANES - k, tail, 0.0)
    return jnp.concatenate([main[:t - SUBLANES], last], axis=0)


def _ffn_bwd_a(dy2, pre, up0, w_down, fw, ts=256):
    s, d = dy2.shape
    nt = s // ts
    nch = D_FF // FFN_CHUNK
    wide = 2 * D_FF
    up_mats = jnp.stack([jnp.eye(ts, k=1, dtype=BF16), jnp.eye(ts, k=2, dtype=BF16)])

    def body(dy2_ref, pre_ref, up0_ref, wdn_ref, fw_ref, um_ref, dup0_ref, dfw_ref, dfb_ref, next_ref):
        i = pl.program_id(0)

        @pl.when(i == 0)
        def _():
            next_ref[...] = jnp.zeros_like(next_ref)
            dfw_ref[...] = jnp.zeros_like(dfw_ref)
            dfb_ref[...] = jnp.zeros_like(dfb_ref)

        dyb = dy2_ref[...]
        for j in range(nch):
            _, _, col = _ffn_cols(j)
            dact = _dot_nt(dyb, wdn_ref[col:col + FFN_CHUNK, :])
            gl, dgl = _gelu_and_grad(pre_ref[:, col:col + FFN_CHUNK].astype(F32))
            dpre = (dact * pre_ref[:, D_FF + col:D_FF + col + FFN_CHUNK].astype(F32) * dgl, dact * gl)
            for half, c0 in enumerate((col, D_FF + col)):
                cs = slice(c0, c0 + FFN_CHUNK)
                dp = dpre[half]
                dpb = dp.astype(BF16)
                nxt = next_ref[:, cs]
                next_ref[:, cs] = dpb.astype(F32)[0:SUBLANES, :]
                su1 = _shift_up_mxu(dpb, um_ref[0], nxt, 1)
                su2 = _shift_up_mxu(dpb, um_ref[1], nxt, 2)
                u = up0_ref[:, cs].astype(F32)
                dfb_ref[:, cs] += _colsum(dp)
                dfw_ref[2:3, cs] += _colsum(dp * u)
                dfw_ref[1:2, cs] += _colsum(su1 * u)
                dfw_ref[0:1, cs] += _colsum(su2 * u)
                dup0 = fw_ref[2:3, cs] * dp + fw_ref[1:2, cs] * su1 + fw_ref[0:1, cs] * su2
                dup0_ref[:, cs] = dup0.astype(BF16)

    return pl.pallas_call(
        body, grid=(nt,), name="ffn_bwd_a",
        in_specs=[_rows(ts, d, nt), _rows(ts, wide, nt), _rows(ts, wide, nt), _RESIDENT,
                  _full((3, wide)), _full((2, ts, ts))],
        out_specs=[_rows(ts, wide, nt), _full((3, wide)), _full((1, wide))],
        out_shape=[_sds((s, wide), BF16), _sds((3, wide), F32), _sds((1, wide), F32)],
        scratch_shapes=[pltpu.VMEM((SUBLANES, wide), F32)],
        compiler_params=_params(("arbitrary",)),
    )(dy2, pre, up0, w_down, fw, up_mats)


def _ffn_bwd_b(dup0, x1, y, dx2, w_up4, g_pre2, sc_f, sh_f, gt_m, g_post_m, ts=512):
    s, d = x1.shape
    shard_cols = 2 * D_FF // N_CHIPS

    def body(dup_ref, x1_ref, y_ref, dx2_ref, wup_ref, g2_ref, sc_ref, sh_ref, gt_ref, gp_ref,
             dx1_ref, dy_ref, dsh_ref, dsc_ref, dg2_ref, dgt_ref, dgp_ref):
        i = pl.program_id(0)

        @pl.when(i == 0)
        def _():
            for ref in (dsh_ref, dsc_ref, dg2_ref, dgt_ref, dgp_ref):
                ref[...] = jnp.zeros_like(ref)

        for rs in _sub_tiles(ts):
            dh2 = jnp.zeros((SUB_ROWS, d), F32)
            for k in range(N_CHIPS):
                dh2 = dh2 + _dot_nt(dup_ref[rs, k * shard_cols:(k + 1) * shard_cols], wup_ref[k])
            x1v = x1_ref[rs, :]
            r2 = _msq_rsqrt(x1v)
            xn = x1v * r2
            hn = xn * g2_ref[...]
            dsh_ref[...] += _colsum(dh2)
            dsc_ref[...] += _colsum(dh2 * hn)
            dhn = dh2 * (1.0 + sc_ref[...])
            dg2_ref[...] += _colsum(dhn * xn)
            dx1 = dx2_ref[rs, :] + _rms_bwd(dhn * g2_ref[...], xn, r2)
            dx1_ref[rs, :] = dx1
            yv = y_ref[rs, :]
            ry = _msq_rsqrt(yv)
            yn = yv * ry
            dgt_ref[...] += _colsum(dx1 * (yn * gp_ref[...]))
            dyng = dx1 * gt_ref[...]
            dgp_ref[...] += _colsum(dyng * yn)
            dy_ref[rs, :] = _rms_bwd(dyng * gp_ref[...], yn, ry).astype(BF16)

    vec = _full((1, d))
    return pl.pallas_call(
        body, grid=(s // ts,), name="ffn_bwd_b",
        in_specs=[_rows(ts, 2 * D_FF), _rows(ts, d), _rows(ts, d), _rows(ts, d), _RESIDENT,
                  vec, vec, vec, vec, vec],
        out_specs=[_rows(ts, d), _rows(ts, d), vec, vec, vec, vec, vec],
        out_shape=[_sds((s, d), F32), _sds((s, d), BF16)] + [_sds((1, d), F32)] * 5,
        compiler_params=_params(("arbitrary",)),
    )(dup0, x1, y, dx2, w_up4, g_pre2, sc_f, sh_f, gt_m, g_post_m)


def _seqmix_bwd(z, hst, stash, dy, w_out, seq_params, ws_t, glo, ggo, ts=256):
    s = z.shape[0]
    nt = s // ts
    small_shapes = [(4, 512), (1, 512), (512, 512), (512, 512), (1, 512), (1, 512), (1, 512),
                    (1, 512), (1, 512), (4, 128, 128), (128, 4), (1, 512), (1, 512)]

    def body(lx_ref, hst_ref, hprev_ref, st_ref, dy_ref, wout_ref, cw_ref, cb_ref, bdr_ref, bdi_ref, br_ref,
             bi_ref, la_ref, ng_ref, nb_ref, ws_ref, bst_ref, wst_ref, glo_ref, ggo_ref, dz_ref, *rest):
        small_refs = rest[:13]
        (dcw_ref, dcb_ref, dwr_ref, dwi_ref, dbr_ref, dbi_ref, dspa_ref, dng_ref, dnb_ref, dws_ref, dbs_ref,
         dglo_ref, dggo_ref) = small_refs
        gcarry, anext, dxcnext, dv_scr = rest[13:]
        i = pl.program_id(0)

        @pl.when(i == 0)
        def _():
            for ref in small_refs:
                ref[...] = jnp.zeros_like(ref)
            gcarry[...] = jnp.zeros_like(gcarry)
            anext[...] = jnp.ones_like(anext)
            dxcnext[...] = jnp.zeros_like(dxcnext)

        first_tile = i == nt - 1
        xc, r, ig, a, mult = st_ref[ST_XC], st_ref[ST_R], st_ref[ST_IG], st_ref[ST_A], st_ref[ST_MULT]
        gl, u, spb, vhat = st_ref[ST_GL], st_ref[ST_U], st_ref[ST_SPB], st_ref[ST_VHAT]
        lx = lx_ref[...]
        h = hst_ref[...]
        hprev = _shift_down(h, jnp.where(first_tile, 0.0, hprev_ref[...]), 1)
        y_l = h * gl
        y_g = u * spb

        dycat = _dot_nt(dy_ref[...], wout_ref[...])
        rl = _msq_rsqrt(y_l)
        yln = y_l * rl
        dyl = dycat[:, 0:512]
        dglo_ref[...] += _colsum(dyl * yln)
        dy_l = _rms_bwd(dyl * glo_ref[...], yln, rl)
        rg = _msq_rsqrt(y_g)
        ygn = y_g * rg
        dyg = dycat[:, 512:1024]
        dggo_ref[...] += _colsum(dyg * ygn)
        dy_g = _rms_bwd(dyg * ggo_ref[...], ygn, rg)

        dz_ref[:, 512:1024] = (dy_l * h * st_ref[ST_DGL]).astype(BF16)
        a_up = _shift_up(a, anext[...], 1)
        acum, gloc = _scan_bwd(a_up, dy_l * gl)
        gg = gloc + acum * gcarry[...]
        gcarry[...] = gg[0:1, :]
        anext[...] = a[0:SUBLANES, :]
        da = gg * hprev
        t1 = gg * mult
        di = t1 * xc
        dxc = t1 * ig
        dmult = gg * ig * xc
        dla = da * a - dmult * (a * a / mult)
        dspa_ref[...] += _colsum(dla * r) * (-LRU_C)
        dpr = dla * ((-LRU_C) * _softplus(-la_ref[...])) * r * (1.0 - r)
        dpi = di * ig * (1.0 - ig)
        dbr_ref[...] += _colsum(dpr)
        dbi_ref[...] += _colsum(dpi)
        dprb = dpr.astype(BF16)
        dpib = dpi.astype(BF16)
        xcb = xc.astype(BF16)
        dwr_ref[...] += _dot_tn(xcb, dprb)
        dwi_ref[...] += _dot_tn(xcb, dpib)
        dxc = dxc + _dot_nt(dprb, bdr_ref[...]) + _dot_nt(dpib, bdi_ref[...])
        nxt = dxcnext[...]
        dxcnext[...] = dxc[0:SUBLANES, :]
        up1, up2, up3 = _shift_up(dxc, nxt, 1), _shift_up(dxc, nxt, 2), _shift_up(dxc, nxt, 3)
        dcb_ref[...] += _colsum(dxc)
        dcw_ref[3:4, :] += _colsum(dxc * lx)
        dcw_ref[2:3, :] += _colsum(up1 * lx)
        dcw_ref[1:2, :] += _colsum(up2 * lx)
        dcw_ref[0:1, :] += _colsum(up3 * lx)
        dlx = cw_ref[3:4, :] * dxc + cw_ref[2:3, :] * up1 + cw_ref[1:2, :] * up2 + cw_ref[0:1, :] * up3
        dz_ref[:, 0:512] = dlx.astype(BF16)

        dz_ref[:, 1024:1536] = (dy_g * spb * st_ref[ST_DU]).astype(BF16)
        dsp = dy_g * u
        vb = (vhat * ng_ref[...] + nb_ref[...]).astype(BF16)
        for n in range(ts // GMLP_BLOCK):
            rs = slice(n * GMLP_BLOCK, (n + 1) * GMLP_BLOCK)
            for g in range(GMLP_GROUPS):
                cs = slice(g * 128, (g + 1) * 128)
                dbs_ref[:, g:g + 1] += jnp.sum(dsp[rs, cs], axis=1, keepdims=True)
                blk = dsp[rs, cs].astype(BF16)
                dws_ref[g] += _dot_nt(blk, vb[rs, cs])
                dv_scr[rs, cs] = _dot(wst_ref[g], blk)
        dv = dv_scr[...]
        dng_ref[...] += _colsum(dv * vhat)
        dnb_ref[...] += _colsum(dv)
        dvh = dv * ng_ref[...]
        dvg = dvh - jnp.mean(dvh, axis=-1, keepdims=True) - vhat * jnp.mean(dvh * vhat, axis=-1, keepdims=True)
        dz_ref[:, 1536:2048] = (dvg * st_ref[ST_Q]).astype(BF16)

        @pl.when(i == nt - 1)
        def _():
            pos = lax.broadcasted_iota(jnp.int32, (GMLP_BLOCK, GMLP_BLOCK), 0) // CHUNK
            src = lax.broadcasted_iota(jnp.int32, (GMLP_BLOCK, GMLP_BLOCK), 1) // CHUNK
            for g in range(GMLP_GROUPS):
                dws_ref[g] = jnp.where(src <= pos, dws_ref[g], 0.0)
            dspa_ref[...] = dspa_ref[...] * (-_sigmoid(-la_ref[...]))

    in_specs = ([_rows(ts, 512, nt), _rows(ts, 512, nt), _halo_prev(ts, 512, SUBLANES, nt),
                 pl.BlockSpec((N_STASH, ts, 512), lambda i: (0, nt - 1 - i, 0)), _rows(ts, 1024, nt),
                 _full((1024, 1024))]
                + _seq_param_specs() + [_full((4, 128, 128)), _full((1, 512)), _full((1, 512))])
    return pl.pallas_call(
        body, grid=(nt,), name="seqmix_bwd",
        in_specs=in_specs,
        out_specs=[_rows(ts, 2048, nt)] + [_full(sh) for sh in small_shapes],
        out_shape=[_sds((s, 2048), BF16)] + [_sds(sh, F32) for sh in small_shapes],
        scratch_shapes=[pltpu.VMEM((1, 512), F32), pltpu.VMEM((SUBLANES, 512), F32),
                        pltpu.VMEM((SUBLANES, 512), F32), pltpu.VMEM((ts, 512), F32)],
        compiler_params=_params(("arbitrary",)),
    )(z, hst, hst, stash, dy, w_out, *seq_params, ws_t, glo, ggo)


def _seqmix_bwd_recomputing_unused(z, hst, dy, w_out, seq_params, ws_t, glo, ggo, ts=256):
    s = z.shape[0]
    nt = s // ts
    small_shapes = [(4, 512), (1, 512), (512, 512), (512, 512), (1, 512), (1, 512), (1, 512),
                    (1, 512), (1, 512), (4, 128, 128), (128, 4), (1, 512), (1, 512)]

    def body(z_ref, zprev_ref, hst_ref, hprev_ref, dy_ref, wout_ref, *rest):
        p = rest[:11]
        wst_ref, glo_ref, ggo_ref = rest[11:14]
        dz_ref = rest[14]
        (dcw_ref, dcb_ref, dwr_ref, dwi_ref, dbr_ref, dbi_ref, dspa_ref, dng_ref, dnb_ref, dws_ref, dbs_ref,
         dglo_ref, dggo_ref) = rest[15:28]
        gcarry, anext, dxcnext, sp_scr, dv_scr = rest[28:]
        i = pl.program_id(0)

        @pl.when(i == 0)
        def _():
            for ref in rest[15:28]:
                ref[...] = jnp.zeros_like(ref)
            gcarry[...] = jnp.zeros_like(gcarry)
            anext[...] = jnp.ones_like(anext)
            dxcnext[...] = jnp.zeros_like(dxcnext)

        first_tile = i == nt - 1
        f = _seq_recompute(z_ref, zprev_ref, first_tile, p)
        xc, r, ig, a, mult, lx = f["xc"], f["r"], f["ig"], f["a"], f["mult"], f["lx"]
        h = hst_ref[...]
        hprev = _shift_down(h, jnp.where(first_tile, 0.0, hprev_ref[...]), 1)
        gl, dgl = _gelu_and_grad(f["lg"])
        y_l = h * gl
        gm = _gmlp_fwd(f["gu"], f["gv"], p[7], p[8], p[9], p[10], sp_scr)
        y_g = gm["y_g"]

        dycat = _dot_nt(dy_ref[...], wout_ref[...])
        rl = _msq_rsqrt(y_l)
        yln = y_l * rl
        dyl = dycat[:, 0:512]
        dglo_ref[...] += _colsum(dyl * yln)
        dy_l = _rms_bwd(dyl * glo_ref[...], yln, rl)
        rg = _msq_rsqrt(y_g)
        ygn = y_g * rg
        dyg = dycat[:, 512:1024]
        dggo_ref[...] += _colsum(dyg * ygn)
        dy_g = _rms_bwd(dyg * ggo_ref[...], ygn, rg)

        dz_ref[:, 512:1024] = (dy_l * h * dgl).astype(BF16)
        a_up = _shift_up(a, anext[...], 1)
        acum, gloc = _scan_bwd(a_up, dy_l * gl)
        gg = gloc + acum * gcarry[...]
        gcarry[...] = gg[0:1, :]
        anext[...] = a[0:SUBLANES, :]
        da = gg * hprev
        t1 = gg * mult
        di = t1 * xc
        dxc = t1 * ig
        dmult = gg * ig * xc
        dla = da * a - dmult * (a * a / mult)
        spa = f["spa"]
        dspa_ref[...] += _colsum(dla * r) * (-LRU_C)
        dpr = dla * ((-LRU_C) * spa) * r * (1.0 - r)
        dpi = di * ig * (1.0 - ig)
        dbr_ref[...] += _colsum(dpr)
        dbi_ref[...] += _colsum(dpi)
        dprb = dpr.astype(BF16)
        dpib = dpi.astype(BF16)
        dwr_ref[...] += _dot_tn(f["xcb"], dprb)
        dwi_ref[...] += _dot_tn(f["xcb"], dpib)
        dxc = dxc + _dot_nt(dprb, p[2][...]) + _dot_nt(dpib, p[3][...])
        dcb_ref[...] += _colsum(dxc)
        dcw_ref[3:4, :] += _colsum(dxc * lx)
        dcw_ref[2:3, :] += _colsum(dxc * f["s1"])
        dcw_ref[1:2, :] += _colsum(dxc * f["s2"])
        dcw_ref[0:1, :] += _colsum(dxc * f["s3"])
        nxt = dxcnext[...]
        dxcnext[...] = dxc[0:SUBLANES, :]
        cw_ref = p[0]
        dlx = (cw_ref[3:4, :] * dxc + cw_ref[2:3, :] * _shift_up(dxc, nxt, 1)
               + cw_ref[1:2, :] * _shift_up(dxc, nxt, 2) + cw_ref[0:1, :] * _shift_up(dxc, nxt, 3))
        dz_ref[:, 0:512] = dlx.astype(BF16)

        dz_ref[:, 1024:1536] = (dy_g * gm["spb"] * gm["du"]).astype(BF16)
        dsp = dy_g * gm["u"]
        vb = gm["vb"]
        for n in range(ts // GMLP_BLOCK):
            rs = slice(n * GMLP_BLOCK, (n + 1) * GMLP_BLOCK)
            for g in range(GMLP_GROUPS):
                cs = slice(g * 128, (g + 1) * 128)
                dbs_ref[:, g:g + 1] += jnp.sum(dsp[rs, cs], axis=1, keepdims=True)
                blk = dsp[rs, cs].astype(BF16)
                dws_ref[g] += _dot_nt(blk, vb[rs, cs])
                dv_scr[rs, cs] = _dot(wst_ref[g], blk)
        dv = dv_scr[...]
        vhat = gm["vhat"]
        dng_ref[...] += _colsum(dv * vhat)
        dnb_ref[...] += _colsum(dv)
        dvh = dv * p[7][...]
        dvg = gm["rstd"] * (dvh - jnp.mean(dvh, axis=-1, keepdims=True)
                            - vhat * jnp.mean(dvh * vhat, axis=-1, keepdims=True))
        dz_ref[:, 1536:2048] = (dvg * gm["dvg"]).astype(BF16)

        @pl.when(i == nt - 1)
        def _():
            pos = lax.broadcasted_iota(jnp.int32, (GMLP_BLOCK, GMLP_BLOCK), 0) // CHUNK
            src = lax.broadcasted_iota(jnp.int32, (GMLP_BLOCK, GMLP_BLOCK), 1) // CHUNK
            for g in range(GMLP_GROUPS):
                dws_ref[g] = jnp.where(src <= pos, dws_ref[g], 0.0)
            dspa_ref[...] = dspa_ref[...] * (-_sigmoid(-p[6][...]))

    in_specs = (_seq_specs(ts, nt, True)
                + [_rows(ts, 512, nt), _halo_prev(ts, 512, SUBLANES, nt), _rows(ts, 1024, nt), _full((1024, 1024))]
                + _seq_param_specs() + [_full((4, 128, 128)), _full((1, 512)), _full((1, 512))])
    return pl.pallas_call(
        body, grid=(nt,), name="seqmix_bwd",
        in_specs=in_specs,
        out_specs=[_rows(ts, 2048, nt)] + [_full(sh) for sh in small_shapes],
        out_shape=[_sds((s, 2048), BF16)] + [_sds(sh, F32) for sh in small_shapes],
        scratch_shapes=[pltpu.VMEM((1, 512), F32), pltpu.VMEM((SUBLANES, 512), F32),
                        pltpu.VMEM((SUBLANES, 512), F32), pltpu.VMEM((ts, 512), F32), pltpu.VMEM((ts, 512), F32)],
        compiler_params=_params(("arbitrary",)),
    )(z, z, hst, hst, dy, w_out, *seq_params, ws_t, glo, ggo)


def _mix_in_bwd(x, dz, dx1, w_in4, g, sc, ts=512):
    s, d = x.shape

    def body(x_ref, dz_ref, dx1_ref, w_ref, g_ref, sc_ref, gx_ref, dsh_ref, dsc_ref, dg_ref):
        i = pl.program_id(0)

        @pl.when(i == 0)
        def _():
            for ref in (dsh_ref, dsc_ref, dg_ref):
                ref[...] = jnp.zeros_like(ref)

        for rs in _sub_tiles(ts):
            dh = jnp.zeros((SUB_ROWS, d), F32)
            for k in range(N_CHIPS):
                dh = dh + _dot_nt(dz_ref[rs, k * 512:(k + 1) * 512], w_ref[k])
            xv = x_ref[rs, :]
            r = _msq_rsqrt(xv)
            xn = xv * r
            dsh_ref[...] += _colsum(dh)
            dsc_ref[...] += _colsum(dh * (xn * g_ref[...]))
            dhn = dh * (1.0 + sc_ref[...])
            dg_ref[...] += _colsum(dhn * xn)
            gx_ref[rs, :] = dx1_ref[rs, :] + _rms_bwd(dhn * g_ref[...], xn, r)

    vec = _full((1, d))
    return pl.pallas_call(
        body, grid=(s // ts,), name="mix_in_bwd",
        in_specs=[_rows(ts, d), _rows(ts, 2048), _rows(ts, d), _full(w_in4.shape), vec, vec],
        out_specs=[_rows(ts, d), vec, vec, vec],
        out_shape=[_sds((s, d), F32)] + [_sds((1, d), F32)] * 3,
        compiler_params=_params(("arbitrary",)),
    )(x, dz, dx1, w_in4, g, sc)


def _wgrad(a, b, n_chunks, name, chunk_major, ts=2048):
    s, m = a.shape
    n = b.shape[1]
    nc = n // n_chunks
    nt = s // ts

    def body(a_ref, b_ref, o_ref, acc):
        i = pl.program_id(1)

        @pl.when(i == 0)
        def _():
            acc[...] = jnp.zeros_like(acc)

        acc[...] += _dot_tn(a_ref[...], b_ref[...])

        @pl.when(i == nt - 1)
        def _():
            if chunk_major:
                o_ref[0] = acc[...].astype(BF16)
            else:
                o_ref[...] = acc[...].astype(BF16)

    if chunk_major:
        out_spec, out_shape = pl.BlockSpec((1, m, nc), lambda c, i: (c, 0, 0)), _sds((n_chunks, m, nc), BF16)
    else:
        out_spec, out_shape = pl.BlockSpec((m, nc), lambda c, i: (0, c)), _sds((m, n), BF16)
    return pl.pallas_call(
        body, grid=(n_chunks, nt), name=name,
        in_specs=[pl.BlockSpec((ts, m), lambda c, i: (i, 0)), pl.BlockSpec((ts, nc), lambda c, i: (i, c))],
        out_specs=out_spec,
        out_shape=out_shape,
        scratch_shapes=[pltpu.VMEM((m, nc), F32)],
        compiler_params=_params(("parallel", "arbitrary")),
    )(a, b)


def _block_diag(w):
    heads, hd, _ = w.shape
    eye = jnp.eye(heads, dtype=w.dtype)
    return (eye[:, None, :, None] * w[:, :, None, :]).reshape(heads * hd, heads * hd)


def _diag_blocks(m):
    hd = LRU_WIDTH // LRU_HEADS
    m4 = m.reshape(LRU_HEADS, hd, LRU_HEADS, hd)
    return jnp.stack([m4[k, :, k, :] for k in range(LRU_HEADS)])


def _seq_params(small):
    row = lambda v: v.reshape(1, -1)
    pos = jnp.arange(GMLP_BLOCK)
    mask = (pos[None, :] // CHUNK) <= (pos[:, None] // CHUNK)
    ws = jnp.where(mask[None], small["w_spatial"], 0.0)
    seq_params = (small["conv_w"], row(small["conv_b"]),
                  _block_diag(small["w_rgate"]).astype(BF16), _block_diag(small["w_igate"]).astype(BF16),
                  row(small["b_rgate"]), row(small["b_igate"]), row(small["lru_a"]),
                  row(small["v_norm_g"]), row(small["v_norm_b"]), ws.astype(BF16), small["b_spatial"].T)
    return seq_params, jnp.swapaxes(ws, 1, 2).astype(BF16)


_ANY = pl.BlockSpec(memory_space=pl.ANY)
_CHIP_FLIPS = ((1, 0), (0, 1), (1, 1))


def _position():
    return lax.axis_index("x"), lax.axis_index("y"), lax.axis_index("c")


def _flip(v, f):
    return 1 - v if f else v


def _remote(src, dst, send_sem, recv_sem, peer):
    return pltpu.make_async_remote_copy(src_ref=src, dst_ref=dst, send_sem=send_sem, recv_sem=recv_sem,
                                        device_id=peer, device_id_type=MESH)


def _allgather8(block, name, reduce):
    r, n = block.shape

    def body(x_ref, out_ref, *scratch):
        if reduce:
            gath, send_sems, recv_sems, loc_sem = scratch
        else:
            gath = out_ref
            send_sems, recv_sems, loc_sem = scratch
        x, y, c = _position()
        me = 4 * x + 2 * y + c
        loc = pltpu.make_async_copy(x_ref, gath.at[me], loc_sem)
        loc.start()
        peers = []
        for k in range(1, N_DEV):
            px, py, pc = _flip(x, k & 4), _flip(y, k & 2), _flip(c, k & 1)
            peers.append((px, py, pc))
            _remote(x_ref, gath.at[me], send_sems.at[k - 1], recv_sems.at[k - 1], (px, py, pc)).start()
        for k, (px, py, pc) in enumerate(peers):
            src = 4 * px + 2 * py + pc
            _remote(x_ref, gath.at[src], send_sems.at[k], recv_sems.at[k], (px, py, pc)).wait_recv()
        for k, peer in enumerate(peers):
            _remote(x_ref, gath.at[me], send_sems.at[k], recv_sems.at[k], peer).wait_send()
        loc.wait()
        if reduce:
            acc = gath[0]
            for k in range(1, N_DEV):
                acc = acc + gath[k]
            out_ref[...] = acc

    sems = [pltpu.SemaphoreType.DMA((N_DEV - 1,)), pltpu.SemaphoreType.DMA((N_DEV - 1,)), pltpu.SemaphoreType.DMA]
    if reduce:
        out_shape = _sds((r, n), F32)
        scratch = [pltpu.VMEM((N_DEV, r, n), F32)] + sems
    else:
        out_shape = _sds((N_DEV, r, n), F32)
        scratch = sems
    return pl.pallas_call(
        body, name=name, out_shape=out_shape,
        in_specs=[pl.BlockSpec(memory_space=pltpu.VMEM)], out_specs=pl.BlockSpec(memory_space=pltpu.VMEM),
        scratch_shapes=scratch,
        compiler_params=pltpu.CompilerParams(vmem_limit_bytes=VMEM_LIMIT_BYTES),
    )(block)


def _half(ref, c, rows):
    hr = rows // 2
    return ref.at[pl.ds(pl.multiple_of(c * hr, BF16_SUBLANES), hr), :]


def _gather_weights(shards):
    na = len(shards)

    def body(*refs):
        ins, outs = refs[:na], refs[na:2 * na]
        ici_send, ici_recv, d2d_send, d2d_recv, loc_sem = refs[2 * na:]
        x, y, c = _position()
        chip = 2 * x + y
        sibling = (x, y, 1 - c)
        local = []
        for a in range(na):
            local.append(pltpu.make_async_copy(ins[a], outs[a].at[chip], loc_sem.at[a]))
            local[-1].start()
        sends = []
        for a in range(na):
            rows = shards[a].shape[0]
            for j, (fx, fy) in enumerate(_CHIP_FLIPS):
                peer = (_flip(x, fx), _flip(y, fy), c)
                sends.append(_remote(_half(ins[a], c, rows), _half(outs[a].at[chip], c, rows),
                                     ici_send.at[a * 3 + j], ici_recv.at[a * 3 + j], peer))
                sends[-1].start()
        for a in range(na):
            rows = shards[a].shape[0]
            for j, (fx, fy) in enumerate(_CHIP_FLIPS):
                src_chip = 2 * _flip(x, fx) + _flip(y, fy)
                landed = _half(outs[a].at[src_chip], c, rows)
                _remote(landed, landed, ici_send.at[a * 3 + j], ici_recv.at[a * 3 + j], sibling).wait_recv()
                sends.append(_remote(landed, landed, d2d_send.at[a * 3 + j], d2d_recv.at[a * 3 + j], sibling))
                sends[-1].start()
        for a in range(na):
            rows = shards[a].shape[0]
            for j, (fx, fy) in enumerate(_CHIP_FLIPS):
                src_chip = 2 * _flip(x, fx) + _flip(y, fy)
                other = _half(outs[a].at[src_chip], 1 - c, rows)
                _remote(other, other, d2d_send.at[a * 3 + j], d2d_recv.at[a * 3 + j], sibling).wait_recv()
        for cp in sends:
            cp.wait_send()
        for cp in local:
            cp.wait()

    return pl.pallas_call(
        body, name="gather_weights",
        out_shape=[_sds((N_CHIPS,) + w.shape, w.dtype) for w in shards],
        in_specs=[_ANY] * na, out_specs=[_ANY] * na,
        scratch_shapes=[pltpu.SemaphoreType.DMA((3 * na,))] * 4 + [pltpu.SemaphoreType.DMA((na,))],
    )(*shards)


def _swap_halves(parts, name):
    na = len(parts)

    def body(*refs):
        ins, outs = refs[:na], refs[na:2 * na]
        send_sems, recv_sems = refs[2 * na:]
        x, y, c = _position()
        sibling = (x, y, 1 - c)
        cps = []
        for a in range(na):
            hr = parts[a].shape[1] // 2
            src = ins[a].at[:, pl.ds(pl.multiple_of((1 - c) * hr, BF16_SUBLANES), hr), :]
            cps.append(_remote(src, outs[a], send_sems.at[a], recv_sems.at[a], sibling))
            cps[-1].start()
        for cp in cps:
            cp.wait()

    return pl.pallas_call(
        body, name=name,
        out_shape=[_sds((N_CHIPS, p.shape[1] // 2, p.shape[2]), p.dtype) for p in parts],
        in_specs=[_ANY] * na, out_specs=[_ANY] * na,
        scratch_shapes=[pltpu.SemaphoreType.DMA((na,))] * 2,
    )(*parts)


def _chip_sum(part, recv, pos_arr, name):
    _, rows, cols = part.shape
    hr = rows // 2

    def body(pos_ref, p_ref, r_ref, o_ref, g_ref):
        total = (p_ref[...].astype(F32) + r_ref[...].astype(F32)).astype(BF16)
        o_ref[...] = total

        @pl.when(pl.program_id(0) == pos_ref[1])
        def _():
            g_ref[0] = total

    grid_spec = pltpu.PrefetchScalarGridSpec(
        num_scalar_prefetch=1, grid=(N_CHIPS,),
        in_specs=[pl.BlockSpec((1, hr, cols), lambda k, pos: (k, pos[0], 0)),
                  pl.BlockSpec((1, hr, cols), lambda k, pos: (k, 0, 0))],
        out_specs=[pl.BlockSpec((1, hr, cols), lambda k, pos: (k, 0, 0)),
                   pl.BlockSpec((1, 1, hr, cols), lambda k, pos: (0, pos[1], 0, 0))])
    return pl.pallas_call(
        body, name=name, grid_spec=grid_spec,
        out_shape=[_sds((N_CHIPS, hr, cols), BF16), _sds((2, N_CHIPS, hr, cols), BF16)],
        compiler_params=_params(("arbitrary",)),
    )(pos_arr, part, recv)


def _exchange_chips(sums):
    na = len(sums)

    def body(*refs):
        ins, outs = refs[:na], refs[na:2 * na]
        send_sems, recv_sems, loc_sem = refs[2 * na:]
        x, y, c = _position()
        chip = 2 * x + y
        local = []
        for a in range(na):
            local.append(pltpu.make_async_copy(ins[a].at[chip], outs[a].at[chip], loc_sem.at[a]))
            local[-1].start()
        cps = []
        for a in range(na):
            for j, (fx, fy) in enumerate(_CHIP_FLIPS):
                px, py = _flip(x, fx), _flip(y, fy)
                cps.append(_remote(ins[a].at[2 * px + py], outs[a].at[chip],
                                   send_sems.at[a * 3 + j], recv_sems.at[a * 3 + j], (px, py, c)))
                cps[-1].start()
        for a in range(na):
            for j, (fx, fy) in enumerate(_CHIP_FLIPS):
                src_chip = 2 * _flip(x, fx) + _flip(y, fy)
                landed = outs[a].at[src_chip]
                _remote(landed, landed, send_sems.at[a * 3 + j], recv_sems.at[a * 3 + j], (x, y, c)).wait_recv()
        for cp in cps:
            cp.wait_send()
        for cp in local:
            cp.wait()

    return pl.pallas_call(
        body, name="exchange_chips",
        out_shape=[_sds(s.shape, s.dtype) for s in sums],
        in_specs=[_ANY] * na, out_specs=[_ANY] * na,
        scratch_shapes=[pltpu.SemaphoreType.DMA((3 * na,))] * 2 + [pltpu.SemaphoreType.DMA((na,))],
    )(*sums)


def _sum_chips(gath, name, tr=128):
    _, hr, cols = gath.shape
    tr = min(tr, hr)

    def body(g_ref, o_ref):
        acc = g_ref[0].astype(F32)
        for k in range(1, N_CHIPS):
            acc = acc + g_ref[k].astype(F32)
        o_ref[...] = acc

    return pl.pallas_call(
        body, name=name, grid=(hr // tr,),
        in_specs=[pl.BlockSpec((N_CHIPS, tr, cols), lambda i: (0, i, 0))],
        out_specs=pl.BlockSpec((tr, cols), lambda i: (i, 0)),
        out_shape=_sds((hr, cols), F32),
        compiler_params=_params(("parallel",)),
    )(gath)


def _join_halves(halves):
    na = len(halves)

    def body(*refs):
        ins, outs = refs[:na], refs[na:2 * na]
        send_sems, recv_sems, loc_sem = refs[2 * na:]
        x, y, c = _position()
        sibling = (x, y, 1 - c)
        cps, local = [], []
        for a in range(na):
            rows = 2 * halves[a].shape[0]
            mine = _half(outs[a], c, rows)
            local.append(pltpu.make_async_copy(ins[a], mine, loc_sem.at[a]))
            local[-1].start()
            cps.append(_remote(ins[a], mine, send_sems.at[a], recv_sems.at[a], sibling))
            cps[-1].start()
        for a in range(na):
            rows = 2 * halves[a].shape[0]
            other = _half(outs[a], 1 - c, rows)
            _remote(ins[a], other, send_sems.at[a], recv_sems.at[a], sibling).wait_recv()
        for cp in cps:
            cp.wait_send()
        for cp in local:
            cp.wait()

    return pl.pallas_call(
        body, name="join_halves",
        out_shape=[_sds((2 * h.shape[0], h.shape[1]), h.dtype) for h in halves],
        in_specs=[_ANY] * na, out_specs=[_ANY] * na,
        scratch_shapes=[pltpu.SemaphoreType.DMA((na,))] * 3,
    )(*halves)


_HBM = pl.BlockSpec(memory_space=pltpu.HBM)
_SEM = pl.BlockSpec(memory_space=pltpu.SEMAPHORE)
_EFFECT = pltpu.SideEffectType.DATAFLOW_SIDE_EFFECTING


def _in_hbm(a):
    return pltpu.with_memory_space_constraint(a, pltpu.HBM)


def _split_start(srcs, lands, plan, n_copies, after, name):
    ns, nl = len(srcs), len(lands)
    bufs = list(srcs) + list(lands)

    def body(*refs):
        send_sems, recv_sems = refs[ns + nl + 1], refs[ns + nl + 2]
        token = refs[-1]
        for k, (src, dst, peer) in enumerate(plan(refs[:ns], refs[ns:ns + nl])):
            _remote(src, dst, send_sems.at[k], recv_sems.at[k], peer).start()
        token[...] = jnp.zeros_like(token)

    out = pl.pallas_call(
        body, name=name,
        out_shape=(pltpu.SemaphoreType.DMA((n_copies,)), pltpu.SemaphoreType.DMA((n_copies,)),
                   *[pltpu.HBM(b.shape, b.dtype) for b in bufs], _sds((SUBLANES, 128), F32)),
        in_specs=[_HBM] * (ns + nl) + [_ANY],
        out_specs=(_SEM, _SEM, *[_HBM] * (ns + nl), pl.BlockSpec(memory_space=pltpu.VMEM)),
        input_output_aliases={i: 2 + i for i in range(ns + nl)},
        compiler_params=pltpu.CompilerParams(has_side_effects=_EFFECT),
    )(*[_in_hbm(b) for b in bufs], after)
    return out[0], out[1], list(out[2:2 + ns]), list(out[2 + ns:2 + ns + nl]), out[-1]


def _split_wait(send_sems, recv_sems, srcs, lands, plan, after, name):
    ns, nl = len(srcs), len(lands)
    bufs = list(srcs) + list(lands)

    def body(*refs):
        send_ref, recv_ref = refs[ns + nl], refs[ns + nl + 1]
        me = _position()
        for k, src, dst in plan(refs[:ns], refs[ns:ns + nl]):
            cp = _remote(src, dst, send_ref.at[k], recv_ref.at[k], me)
            cp.wait_send()
            cp.wait_recv()

    out = pl.pallas_call(
        body, name=name,
        out_shape=[pltpu.HBM(b.shape, b.dtype) for b in bufs],
        in_specs=[_HBM] * (ns + nl) + [_SEM, _SEM, _ANY],
        out_specs=[_HBM] * (ns + nl),
        input_output_aliases={i: i for i in range(ns + nl)},
        compiler_params=pltpu.CompilerParams(has_side_effects=_EFFECT),
    )(*bufs, send_sems, recv_sems, after)
    return list(out[:ns]), list(out[ns:])


def _gather_plan(rows_of):
    def start(src_refs, land_refs):
        x, y, c = _position()
        chip = 2 * x + y
        out = []
        for a, rows in enumerate(rows_of):
            mine = _half(land_refs[a].at[chip], c, rows)
            out.extend((mine, mine, (_flip(x, fx), _flip(y, fy), c)) for fx, fy in _CHIP_FLIPS)
        return out

    def wait(src_refs, land_refs):
        x, y, c = _position()
        chip = 2 * x + y
        out = []
        for a, rows in enumerate(rows_of):
            for j, (fx, fy) in enumerate(_CHIP_FLIPS):
                src_chip = 2 * _flip(x, fx) + _flip(y, fy)
                out.append((3 * a + j, _half(land_refs[a].at[chip], c, rows),
                            _half(land_refs[a].at[src_chip], c, rows)))
        return out

    return start, wait


def _forward_plan(rows_of):
    def pieces(land_refs, half):
        x, y, _ = _position()
        return [_half(land_refs[a].at[2 * _flip(x, fx) + _flip(y, fy)], half, rows)
                for a, rows in enumerate(rows_of) for fx, fy in _CHIP_FLIPS]

    def start(src_refs, land_refs):
        x, y, c = _position()
        return [(p, p, (x, y, 1 - c)) for p in pieces(land_refs, c)]

    def wait(src_refs, land_refs):
        _, _, c = _position()
        return [(k, mine, theirs)
                for k, (mine, theirs) in enumerate(zip(pieces(land_refs, c), pieces(land_refs, 1 - c)))]

    return start, wait


def _swap_halves_plan(half_rows):
    def slices(src_refs, c):
        return [src_refs[a].at[:, pl.ds(pl.multiple_of((1 - c) * hr, BF16_SUBLANES), hr), :]
                for a, hr in enumerate(half_rows)]

    def start(src_refs, land_refs):
        x, y, c = _position()
        return [(src, land_refs[a], (x, y, 1 - c)) for a, src in enumerate(slices(src_refs, c))]

    def wait(src_refs, land_refs):
        _, _, c = _position()
        return [(a, src, land_refs[a]) for a, src in enumerate(slices(src_refs, c))]

    return start, wait


def _swap_gathered_plan(n_arrays):
    def start(src_refs, land_refs):
        x, y, c = _position()
        return [(land_refs[a].at[0], land_refs[a].at[1], (x, y, 1 - c)) for a in range(n_arrays)]

    def wait(src_refs, land_refs):
        return [(a, land_refs[a].at[0], land_refs[a].at[1]) for a in range(n_arrays)]

    return start, wait


def _exchange_plan(n_arrays):
    def start(src_refs, land_refs):
        x, y, c = _position()
        chip = 2 * x + y
        out = []
        for a in range(n_arrays):
            for fx, fy in _CHIP_FLIPS:
                px, py = _flip(x, fx), _flip(y, fy)
                out.append((src_refs[a].at[2 * px + py], land_refs[a].at[0, chip], (px, py, c)))
        return out

    def wait(src_refs, land_refs):
        x, y, c = _position()
        out = []
        for a in range(n_arrays):
            for j, (fx, fy) in enumerate(_CHIP_FLIPS):
                src_chip = 2 * _flip(x, fx) + _flip(y, fy)
                out.append((3 * a + j, src_refs[a].at[src_chip], land_refs[a].at[0, src_chip]))
        return out

    return start, wait


def _forward_to_sibling(lands, name):
    na = len(lands)

    def body(*refs):
        land_refs = refs[na:2 * na]
        send_sems, recv_sems = refs[2 * na:]
        x, y, c = _position()
        sibling = (x, y, 1 - c)
        sends = []
        for a in range(na):
            rows = lands[a].shape[1]
            for j, (fx, fy) in enumerate(_CHIP_FLIPS):
                landed = _half(land_refs[a].at[2 * _flip(x, fx) + _flip(y, fy)], c, rows)
                sends.append(_remote(landed, landed, send_sems.at[3 * a + j], recv_sems.at[3 * a + j], sibling))
                sends[-1].start()
        for a in range(na):
            rows = lands[a].shape[1]
            for j, (fx, fy) in enumerate(_CHIP_FLIPS):
                other = _half(land_refs[a].at[2 * _flip(x, fx) + _flip(y, fy)], 1 - c, rows)
                _remote(other, other, send_sems.at[3 * a + j], recv_sems.at[3 * a + j], sibling).wait_recv()
        for cp in sends:
            cp.wait_send()

    return pl.pallas_call(
        body, name=name,
        out_shape=[_sds(l.shape, l.dtype) for l in lands],
        in_specs=[_ANY] * na, out_specs=[_ANY] * na,
        input_output_aliases={a: a for a in range(na)},
        scratch_shapes=[pltpu.SemaphoreType.DMA((3 * na,))] * 2,
    )(*lands)


def _swap_gathered(gath, name):
    na = len(gath)

    def body(*refs):
        gath_refs = refs[na:2 * na]
        send_sems, recv_sems = refs[2 * na:]
        x, y, c = _position()
        cps = [_remote(gath_refs[a].at[0], gath_refs[a].at[1], send_sems.at[a], recv_sems.at[a], (x, y, 1 - c))
               for a in range(na)]
        for cp in cps:
            cp.start()
        for cp in cps:
            cp.wait()

    return pl.pallas_call(
        body, name=name,
        out_shape=[_sds(g.shape, g.dtype) for g in gath],
        in_specs=[_ANY] * na, out_specs=[_ANY] * na,
        input_output_aliases={a: a for a in range(na)},
        scratch_shapes=[pltpu.SemaphoreType.DMA((na,))] * 2,
    )(*gath)


def _adam_gathered(w, gath, m, v, c_arr, after, name, tr=128):
    rows, cols = w.shape
    hr = rows // 2
    if hr % (2 * tr) == 0:
        tr = 2 * tr
    per = hr // tr

    def body(c_ref, w_ref, g_ref, m_ref, v_ref, after_ref, go_ref, d_ref, nm_ref, nv_ref):
        g = g_ref[0, 0].astype(F32)
        for k in range(1, N_CHIPS):
            g = g + g_ref[0, k].astype(F32)
        go_ref[...] = g
        d_ref[...], nm_ref[...], nv_ref[...] = _adam_math(w_ref[...], g, m_ref[...], v_ref[...])

    def rows_of(h, i, c_ref):
        c = c_ref[0]
        return ((c + h - 2 * c * h) * per + i, 0)

    blk = pl.BlockSpec((tr, cols), rows_of)
    grid_spec = pltpu.PrefetchScalarGridSpec(
        num_scalar_prefetch=1, grid=(2, per),
        in_specs=[blk, pl.BlockSpec((1, N_CHIPS, tr, cols), lambda h, i, c_ref: (h, 0, i, 0)), blk, blk, _ANY],
        out_specs=[blk] * 4)
    return pl.pallas_call(
        body, name=name, grid_spec=grid_spec, out_shape=[_sds(w.shape, F32)] * 4,
        compiler_params=_params(("arbitrary", "arbitrary")),
    )(c_arr, w, gath, m, v, after)


def _allreduce_small(block, name):
    two, r, n = block.shape
    assert two == 2

    def body(x_ref, out_ref, sib, chipsum, gath, d2d_send, d2d_recv, ici_send, ici_recv):
        x, y, c = _position()
        chip = 2 * x + y
        sibling = (x, y, 1 - c)
        first = _remote(x_ref, sib, d2d_send.at[0], d2d_recv.at[0], sibling)
        first.start()
        first.wait()
        chipsum[...] = x_ref[...] + sib[...]
        sends = []
        for j, (fx, fy) in enumerate(_CHIP_FLIPS):
            sends.append(_remote(chipsum.at[c], gath.at[chip], ici_send.at[j], ici_recv.at[j],
                                 (_flip(x, fx), _flip(y, fy), c)))
            sends[-1].start()
        gath[chip] = chipsum[c]
        for j, (fx, fy) in enumerate(_CHIP_FLIPS):
            landed = gath.at[2 * _flip(x, fx) + _flip(y, fy)]
            _remote(landed, landed, ici_send.at[j], ici_recv.at[j], sibling).wait_recv()
        for cp in sends:
            cp.wait_send()
        total = gath[0]
        for k in range(1, N_CHIPS):
            total = total + gath[k]
        out_ref[c] = total
        last = _remote(out_ref.at[c], out_ref.at[c], d2d_send.at[1], d2d_recv.at[1], sibling)
        last.start()
        _remote(out_ref.at[1 - c], out_ref.at[1 - c], d2d_send.at[1], d2d_recv.at[1], sibling).wait_recv()
        last.wait_send()

    vmem = pl.BlockSpec(memory_space=pltpu.VMEM)
    return pl.pallas_call(
        body, name=name, out_shape=_sds(block.shape, F32), in_specs=[vmem], out_specs=vmem,
        scratch_shapes=[pltpu.VMEM(block.shape, F32), pltpu.VMEM(block.shape, F32), pltpu.VMEM((N_CHIPS, r, n), F32),
                        pltpu.SemaphoreType.DMA((2,)), pltpu.SemaphoreType.DMA((2,)),
                        pltpu.SemaphoreType.DMA((3,)), pltpu.SemaphoreType.DMA((3,))],
        compiler_params=pltpu.CompilerParams(vmem_limit_bytes=VMEM_LIMIT_BYTES),
    )(block)


def _cast_place(shards, chip_arr):
    na = len(shards)
    steps = 4

    def body(chip_ref, *refs):
        for a in range(na):
            refs[na + a][0] = refs[a][...].astype(BF16)

    grid_spec = pltpu.PrefetchScalarGridSpec(
        num_scalar_prefetch=1, grid=(steps,),
        in_specs=[pl.BlockSpec((s.shape[0] // steps, s.shape[1]), lambda i, ch: (i, 0)) for s in shards],
        out_specs=[pl.BlockSpec((1, s.shape[0] // steps, s.shape[1]), lambda i, ch: (ch[0], i, 0)) for s in shards])
    return pl.pallas_call(
        body, name="cast_place", grid_spec=grid_spec,
        out_shape=[_sds((N_CHIPS,) + s.shape, BF16) for s in shards],
        compiler_params=_params(("arbitrary",)),
    )(chip_arr, *shards)


def _silu(v):
    return v * _sigmoid(v)


def _ada_fwd(c8, w_ada):
    def body(c_ref, w_ref, o_ref):
        o_ref[...] = jnp.dot(_silu(c_ref[...]), w_ref[...], preferred_element_type=F32,
                             precision=lax.Precision.HIGHEST)

    return pl.pallas_call(
        body, name="ada_fwd", out_shape=_sds((N_DEV, w_ada.shape[1]), F32),
        compiler_params=pltpu.CompilerParams(vmem_limit_bytes=VMEM_LIMIT_BYTES),
    )(c8, w_ada)


def _mod_select(parts, b_ada, me_arr, after):
    cols = parts.shape[2]

    def body(me_ref, p_ref, b_ref, after_ref, o_ref):
        me = me_ref[0]
        for k in range(N_CHIPS):
            cs = slice(k * cols, (k + 1) * cols)
            o_ref[:, cs] = p_ref[2 * k, pl.ds(me, 1), :] + b_ref[:, cs]

    grid_spec = pltpu.PrefetchScalarGridSpec(
        num_scalar_prefetch=1, grid=(1,),
        in_specs=[pl.BlockSpec(parts.shape, lambda i, m: (0, 0, 0)), pl.BlockSpec(b_ada.shape, lambda i, m: (0, 0)),
                  _ANY],
        out_specs=pl.BlockSpec(b_ada.shape, lambda i, m: (0, 0)))
    return pl.pallas_call(body, name="mod_select", grid_spec=grid_spec, out_shape=_sds(b_ada.shape, F32))(
        me_arr, parts, b_ada, after)


def _ada_bwd(c8, dmod8, chip_arr, w, m, v, tr=256):
    d = c8.shape[1]
    cols = dmod8.shape[1] // N_CHIPS

    def body(chip_ref, c_ref, dm_ref, dmall_ref, w_ref, m_ref, v_ref, gw_ref, d_ref, nm_ref, nv_ref, gb_ref):
        g = lax.dot_general(_silu(c_ref[...]), dm_ref[...], (((0,), (0,)), ((), ())),
                            preferred_element_type=F32, precision=lax.Precision.HIGHEST)
        gw_ref[...] = g
        d_ref[...], nm_ref[...], nv_ref[...] = _adam_math(w_ref[...], g, m_ref[...], v_ref[...])
        acc = dmall_ref[0:1, :]
        for k in range(1, N_DEV):
            acc = acc + dmall_ref[k:k + 1, :]
        gb_ref[...] = acc

    rows = pl.BlockSpec((tr, cols), lambda i, ch: (i, 0))
    grid_spec = pltpu.PrefetchScalarGridSpec(
        num_scalar_prefetch=1, grid=(d // tr,),
        in_specs=[pl.BlockSpec((N_DEV, tr), lambda i, ch: (0, i)),
                  pl.BlockSpec((N_DEV, cols), lambda i, ch: (0, ch[0])),
                  pl.BlockSpec(dmod8.shape, lambda i, ch: (0, 0)), rows, rows, rows],
        out_specs=[rows] * 4 + [pl.BlockSpec((1, dmod8.shape[1]), lambda i, ch: (0, 0))])
    return pl.pallas_call(
        body, name="ada_bwd", grid_spec=grid_spec,
        out_shape=[_sds((d, cols), F32)] * 4 + [_sds((1, dmod8.shape[1]), F32)],
        compiler_params=_params(("arbitrary",)),
    )(chip_arr, c8, dmod8, dmod8, w, m, v)


def _adam_math(w, g, m, v):
    m = ADAM_B1 * m + (1.0 - ADAM_B1) * g
    v = ADAM_B2 * v + (1.0 - ADAM_B2) * (g * g)
    m_hat = m / (1.0 - ADAM_B1 ** ADAM_STEP)
    v_hat = v / (1.0 - ADAM_B2 ** ADAM_STEP)
    delta = -ADAM_LR * (m_hat / (jnp.sqrt(v_hat) + ADAM_EPS) + ADAM_WD * w)
    return delta, m, v


def _adam(w, g, m, v, name, tr=256):
    rows, cols = w.shape
    if rows % tr:
        tr = rows

    def body(w_ref, g_ref, m_ref, v_ref, d_ref, nm_ref, nv_ref):
        d_ref[...], nm_ref[...], nv_ref[...] = _adam_math(w_ref[...], g_ref[...], m_ref[...], v_ref[...])

    spec = pl.BlockSpec((tr, cols), lambda i: (i, 0))
    return pl.pallas_call(
        body, name=name, grid=(rows // tr,), in_specs=[spec] * 4, out_specs=[spec] * 3,
        out_shape=[_sds(w.shape, F32)] * 3, compiler_params=_params(("parallel",)),
    )(w, g, m, v)


def _adam_cols(w, g_full, m, v, chip_arr, name):
    rows, cols = w.shape

    def body(chip_ref, w_ref, g_ref, m_ref, v_ref, gs_ref, d_ref, nm_ref, nv_ref):
        g = g_ref[...]
        gs_ref[...] = g
        d_ref[...], nm_ref[...], nv_ref[...] = _adam_math(w_ref[...], g, m_ref[...], v_ref[...])

    own = pl.BlockSpec((rows, cols), lambda i, ch: (0, 0))
    grid_spec = pltpu.PrefetchScalarGridSpec(
        num_scalar_prefetch=1, grid=(1,),
        in_specs=[own, pl.BlockSpec((rows, cols), lambda i, ch: (0, ch[0])), own, own],
        out_specs=[own] * 4)
    return pl.pallas_call(body, name=name, grid_spec=grid_spec, out_shape=[_sds(w.shape, F32)] * 4)(
        chip_arr, w, g_full, m, v)


PACK_COLS = 512
SMALL_REPLICATED = ("g_mix_pre", "g_mix_post", "conv_b", "w_rgate", "b_rgate", "w_igate", "b_igate", "lru_a",
                    "v_norm_g", "v_norm_b", "w_spatial", "b_spatial", "g_lru_out", "g_gmlp_out", "g_ffn_pre",
                    "g_ffn_post", "ffn_conv_b")
SMALL_COLUMN_SHARDED = ("conv_w", "ffn_conv_w")


def _pack(arrays):
    flat = jnp.concatenate([a.reshape(1, -1) for a in arrays], axis=1)
    pad = (-flat.shape[1]) % (2 * LANES)
    if pad:
        flat = jnp.pad(flat, ((0, 0), (0, pad)))
    return flat.reshape(2, -1)


def _unpack(packed, shapes):
    flat = packed.reshape(1, -1)
    out, col = [], 0
    for shape in shapes:
        n = math.prod(shape)
        out.append(flat[:, col:col + n].reshape(shape))
        col += n
    return out


SMALL_ROW_LEN = 86016
_SMALL_ROWS = (
    (("ffn_conv_w", 18432), ("conv_w", 2048), ("w_spatial", 65536)),
    (("w_rgate", 32768), ("w_igate", 32768), ("ffn_conv_b", 6144), ("g_mix_pre", 1024), ("g_mix_post", 1024),
     ("g_ffn_pre", 1024), ("g_ffn_post", 1024), ("conv_b", 512), ("b_rgate", 512), ("b_igate", 512),
     ("lru_a", 512), ("v_norm_g", 512), ("v_norm_b", 512), ("b_spatial", 512), ("g_lru_out", 512),
     ("g_gmlp_out", 512), ("loss", 128)),
)


def _small_slots():
    slots = {}
    for row, entries in enumerate(_SMALL_ROWS):
        off = 0
        for name, size in entries:
            slots[name] = (row, off)
            off += size
        assert off <= SMALL_ROW_LEN
    return slots


SMALL_SLOT = _small_slots()
SMALL_LANES = SMALL_ROW_LEN // SUBLANES


def _small_pieces(name, first, count):
    row, off = SMALL_SLOT[name]
    pos, pieces = off + first, []
    while count:
        sub, lane = divmod(pos, SMALL_LANES)
        n = min(count, SMALL_LANES - lane)
        pieces.append((row, sub, lane, n))
        pos, count = pos + n, count - n
    return pieces
ROW_VECTORS = ("ffn_conv_b", "g_mix_pre", "g_mix_post", "g_ffn_pre", "g_ffn_post", "conv_b", "lru_a", "v_norm_g",
               "v_norm_b", "g_lru_out", "g_gmlp_out")
HEAD_DIM = LRU_WIDTH // LRU_HEADS


def _pack_small(g, after):
    order = ("ffn_conv_w", "conv_w", "w_spatial", "w_rgate", "w_igate", "b_rgate", "b_igate", "b_spatial", "loss") \
        + ROW_VECTORS
    vmem = pl.BlockSpec(memory_space=pltpu.VMEM)

    def body(*refs):
        src = dict(zip(order, refs))
        out_ref = refs[len(order) + 1]
        out_ref[...] = jnp.zeros_like(out_ref)

        def put(name, first, val):
            col = 0
            for row, sub, lane, n in _small_pieces(name, first, val.shape[1]):
                out_ref[row, sub:sub + 1, lane:lane + n] = val[:, col:col + n]
                col += n

        for name in ROW_VECTORS + ("b_rgate", "b_igate", "loss"):
            put(name, 0, src[name][...])
        for name in ("ffn_conv_w", "conv_w"):
            k_taps, n = src[name].shape
            for k in range(k_taps):
                put(name, k * n, src[name][k:k + 1, :])
        for g_idx in range(GMLP_GROUPS):
            for i in range(GMLP_BLOCK):
                put("w_spatial", (g_idx * GMLP_BLOCK + i) * GMLP_BLOCK, src["w_spatial"][g_idx, i:i + 1, :])
        for name in ("w_rgate", "w_igate"):
            for h in range(LRU_HEADS):
                for i in range(HEAD_DIM):
                    r = h * HEAD_DIM + i
                    put(name, r * HEAD_DIM, src[name][r:r + 1, h * HEAD_DIM:(h + 1) * HEAD_DIM])
        eye = (lax.broadcasted_iota(jnp.int32, (GMLP_BLOCK, GMLP_BLOCK), 0)
               == lax.broadcasted_iota(jnp.int32, (GMLP_BLOCK, GMLP_BLOCK), 1))
        for g_idx in range(GMLP_GROUPS):
            col = src["b_spatial"][:, g_idx:g_idx + 1]
            put("b_spatial", g_idx * GMLP_BLOCK, _colsum(jnp.where(eye, col, 0.0)))

    return pl.pallas_call(
        body, name="pack_small", out_shape=_sds((2, SUBLANES, SMALL_LANES), F32),
        in_specs=[vmem] * len(order) + [_ANY], out_specs=vmem,
        compiler_params=pltpu.CompilerParams(vmem_limit_bytes=VMEM_LIMIT_BYTES),
    )(*[g[n] for n in order], after)


def _adam_small(g_small, w, m, v):
    vmem = pl.BlockSpec(memory_space=pltpu.VMEM)
    n_p = len(SMALL_REPLICATED)

    def body(g_ref, *refs):
        w_refs, m_refs, v_refs = refs[:n_p], refs[n_p:2 * n_p], refs[2 * n_p:3 * n_p]
        outs = refs[3 * n_p:]
        go, do, mo, vo = outs[:n_p], outs[n_p:2 * n_p], outs[2 * n_p:3 * n_p], outs[3 * n_p:]
        for k, name in enumerate(SMALL_REPLICATED):
            def take(first, count, name=name):
                parts = [g_ref[row, sub:sub + 1, lane:lane + n]
                         for row, sub, lane, n in _small_pieces(name, first, count)]
                return parts[0] if len(parts) == 1 else jnp.concatenate(parts, axis=1)

            shape = w_refs[k].shape
            if name in ROW_VECTORS:
                go[k][...] = take(0, shape[1])
            elif name in ("b_rgate", "b_igate"):
                for h in range(LRU_HEADS):
                    go[k][0, h:h + 1, :] = take(h * HEAD_DIM, HEAD_DIM)
            elif name == "b_spatial":
                for g_idx in range(GMLP_GROUPS):
                    go[k][0, g_idx:g_idx + 1, :] = take(g_idx * GMLP_BLOCK, GMLP_BLOCK)
            elif name == "w_spatial":
                for g_idx in range(GMLP_GROUPS):
                    for i in range(GMLP_BLOCK):
                        go[k][0, g_idx, i:i + 1, :] = take((g_idx * GMLP_BLOCK + i) * GMLP_BLOCK, GMLP_BLOCK)
            else:
                for h in range(LRU_HEADS):
                    for i in range(HEAD_DIM):
                        go[k][0, h, i:i + 1, :] = take((h * HEAD_DIM + i) * HEAD_DIM, HEAD_DIM)
            do[k][...], mo[k][...], vo[k][...] = _adam_math(w_refs[k][...], go[k][...], m_refs[k][...],
                                                             v_refs[k][...])

    names = SMALL_REPLICATED
    out_shape = [_sds(w[n].shape, F32) for n in names] * 4
    res = pl.pallas_call(
        body, name="adam_small", out_shape=out_shape,
        in_specs=[vmem] * (1 + 3 * n_p), out_specs=[vmem] * (4 * n_p),
        compiler_params=pltpu.CompilerParams(vmem_limit_bytes=VMEM_LIMIT_BYTES),
    )(g_small, *[w[n] for n in names], *[m[n] for n in names], *[v[n] for n in names])
    return [dict(zip(names, res[k * n_p:(k + 1) * n_p])) for k in range(4)]


def _adam_cols(name, g_small, w, m, v, chip_arr):
    _, k_taps, n = w.shape
    row, off = SMALL_SLOT[name]
    first = off // n
    per_sub = SMALL_LANES // n

    def body(chip_ref, *refs):
        g_refs = refs[:k_taps]
        w_ref, m_ref, v_ref, go_ref, d_ref, nm_ref, nv_ref = refs[k_taps:]
        for k in range(k_taps):
            tap = (0, slice(k, k + 1), slice(None))
            sub = (first + N_CHIPS * k + chip_ref[0]) // per_sub
            g = g_refs[k][row, pl.ds(sub, 1), :]
            go_ref[tap] = g
            d_ref[tap], nm_ref[tap], nv_ref[tap] = _adam_math(w_ref[tap], g, m_ref[tap], v_ref[tap])

    whole = pl.BlockSpec(w.shape, lambda i, ch: (0, 0, 0))
    taps = [pl.BlockSpec((2, SUBLANES, n),
                         functools.partial(lambda i, ch, k: (0, 0, (first + N_CHIPS * k + ch[0]) % per_sub), k=k))
            for k in range(k_taps)]
    grid_spec = pltpu.PrefetchScalarGridSpec(
        num_scalar_prefetch=1, grid=(1,), in_specs=taps + [whole] * 3, out_specs=[whole] * 4)
    return pl.pallas_call(body, name="adam_" + name, grid_spec=grid_spec, out_shape=[_sds(w.shape, F32)] * 4)(
        chip_arr, *[g_small] * k_taps, w, m, v)


def kernel(x, c, w_ada, b_ada, g_mix_pre, g_mix_post, w_in, conv_w, conv_b, w_rgate, b_rgate, w_igate, b_igate, lru_a, v_norm_g, v_norm_b, w_spatial, b_spatial, g_lru_out, g_gmlp_out, w_out, g_ffn_pre, g_ffn_post, w_up, ffn_conv_w, ffn_conv_b, w_down, loss_target, m_w_ada, m_b_ada, m_g_mix_pre, m_g_mix_post, m_w_in, m_conv_w, m_conv_b, m_w_rgate, m_b_rgate, m_w_igate, m_b_igate, m_lru_a, m_v_norm_g, m_v_norm_b, m_w_spatial, m_b_spatial, m_g_lru_out, m_g_gmlp_out, m_w_out, m_g_ffn_pre, m_g_ffn_post, m_w_up, m_ffn_conv_w, m_ffn_conv_b, m_w_down, v_w_ada, v_b_ada, v_g_mix_pre, v_g_mix_post, v_w_in, v_conv_w, v_conv_b, v_w_rgate, v_b_rgate, v_w_igate, v_b_igate, v_lru_a, v_v_norm_g, v_v_norm_b, v_w_spatial, v_b_spatial, v_g_lru_out, v_g_gmlp_out, v_w_out, v_g_ffn_pre, v_g_ffn_post, v_w_up, v_ffn_conv_w, v_ffn_conv_b, v_w_down):
    args = dict(locals())
    names = ("w_ada", "b_ada", "g_mix_pre", "g_mix_post", "w_in", "conv_w", "conv_b", "w_rgate", "b_rgate",
             "w_igate", "b_igate", "lru_a", "v_norm_g", "v_norm_b", "w_spatial", "b_spatial", "g_lru_out",
             "g_gmlp_out", "w_out", "g_ffn_pre", "g_ffn_post", "w_up", "ffn_conv_w", "ffn_conv_b", "w_down")
    drop = lambda a: a if a.ndim == 2 else a[0]
    w = {n: drop(args[n]) for n in names}
    m = {n: drop(args["m_" + n]) for n in names}
    v = {n: drop(args["v_" + n]) for n in names}
    xi, yi, ci = _position()
    me_arr = jnp.reshape(4 * xi + 2 * yi + ci, (1,)).astype(jnp.int32)
    chip_arr = jnp.reshape(2 * xi + yi, (1,)).astype(jnp.int32)
    c_arr = jnp.reshape(ci, (1,)).astype(jnp.int32)
    pos_arr = jnp.stack([ci, 2 * xi + yi]).astype(jnp.int32)

    big = ("w_in", "w_out", "w_up", "w_down")
    lands = _cast_place([w[n] for n in big], chip_arr)
    start_a, wait_a = _gather_plan([w[n].shape[0] for n in big[:2]])
    start_b, wait_b = _gather_plan([w[n].shape[0] for n in big[2:]])

    row0 = jnp.concatenate([c, w["conv_w"].reshape(1, -1), w["ffn_conv_w"].reshape(1, -1)], axis=1)
    g0 = _allgather8(row0, "gather_cond", False)[:, 0, :]
    c8 = g0[:, :D_MODEL]
    per_chip = g0[0::2]
    conv_w_full = per_chip[:, D_MODEL:D_MODEL + 512].reshape(N_CHIPS, 4, 128).transpose(1, 0, 2).reshape(4, 512)
    ffn_conv_w_full = per_chip[:, D_MODEL + 512:].reshape(N_CHIPS, 3, 1536).transpose(1, 0, 2).reshape(3, 2 * D_FF)
    mod_parts = _allgather8(_ada_fwd(c8, w["w_ada"]), "gather_mod", False)
    send_a, recv_a, _, lands_a, token_a = _split_start([], lands[:2], start_a, 6, mod_parts, "gather_start_a")
    send_b, recv_b, _, lands_b, token_b = _split_start([], lands[2:], start_b, 6, token_a, "gather_start_b")
    mod = _mod_select(mod_parts, w["b_ada"].reshape(1, -1), me_arr, token_b).reshape(N_MOD, D_MODEL)
    sh_m, sc_m, gt_m, sh_f, sc_f, gt_f = [mod[k:k + 1] for k in range(N_MOD)]

    small = {n: w[n] for n in SMALL_REPLICATED}
    small["conv_w"] = conv_w_full
    small["ffn_conv_w"] = ffn_conv_w_full
    row = lambda a: a.reshape(1, -1)
    seq_params, ws_t = _seq_params(small)
    glo, ggo = row(small["g_lru_out"]), row(small["g_gmlp_out"])
    g_pre, g_post = row(small["g_mix_pre"]), row(small["g_mix_post"])
    g_pre2, g_post2 = row(small["g_ffn_pre"]), row(small["g_ffn_post"])
    fw, fb = small["ffn_conv_w"], row(small["ffn_conv_b"])
    xs, tgt = x[0], loss_target[0]

    _, lands_a = _split_wait(send_a, recv_a, [], lands_a, wait_a, mod, "gather_wait_a")
    w_in4, w_out4 = _forward_to_sibling(lands_a, "forward_a")
    w_out_b = w_out4.reshape(D_MODEL, D_MODEL)
    z, h = _mix_in(xs, sc_m, sh_m, g_pre, w_in4)
    ycat, hst, stash = _seqmix(z, seq_params, glo, ggo)
    _, lands_b = _split_wait(send_b, recv_b, [], lands_b, wait_b, ycat, "gather_wait_b")
    fwd_start, fwd_wait = _forward_plan([w[n].shape[0] for n in big[2:]])
    fwd_send, fwd_recv, _, lands_b, tok = _split_start([], lands_b, fwd_start, 6, pos_arr, "forward_start_b")
    y, x1, h2 = _mix_out(ycat, xs, w_out_b, gt_m + tok[0:1, 0:1], g_post, g_pre2, sc_f, sh_f)
    _, (w_up4, w_down4) = _split_wait(fwd_send, fwd_recv, [], lands_b, fwd_wait, h2, "forward_wait_b")
    w_down_b = w_down4.reshape(D_FF, D_MODEL)
    up0, pre, act, dy2, dx2, loss, dgt_f, dg_post2 = _ffn_fwd(h2, x1, tgt, w_up4, w_down_b, fw, fb, gt_f, g_post2)

    dup0, dfw, dfb = _ffn_bwd_a(dy2, pre, up0, w_down_b, fw)
    gw_up = _wgrad(h2, dup0, N_CHIPS, "wgrad_up", True)
    gw_down = _wgrad(act, dy2, 2, "wgrad_down", False)
    ex_start, ex_wait = _exchange_plan(2)
    sg_start, sg_wait = _swap_gathered_plan(2)
    grads, deltas, new_m, new_v = {}, {}, {}, {}

    def swap_start(parts, name):
        sw_start, sw_wait = _swap_halves_plan([p.shape[1] // 2 for p in parts])
        recv = [lax.empty((N_CHIPS, p.shape[1] // 2, p.shape[2]), BF16) for p in parts]
        send_s, recv_s, parts, recv, token = _split_start(parts, recv, sw_start, len(parts), pos_arr,
                                                           "swap_start_" + name)
        return (send_s, recv_s, parts, recv, sw_wait), token

    def exchange_start(swap, tags, after, name):
        send_s, recv_s, parts, recv, sw_wait = swap
        parts, recv = _split_wait(send_s, recv_s, parts, recv, sw_wait, after, "swap_wait_" + name)
        both = [_chip_sum(p, r, pos_arr, "chip_sum_" + t) for p, r, t in zip(parts, recv, tags)]
        sums, gath = [b[0] for b in both], [b[1] for b in both]
        return _split_start(sums, gath, ex_start, 3 * len(parts), pos_arr, "exchange_start_" + name)

    def gathered_start(exchange, after, name):
        send_s, recv_s, sums, gath, _ = exchange
        _, gath = _split_wait(send_s, recv_s, sums, gath, ex_wait, after, "exchange_wait_" + name)
        send_s, recv_s, _, gath, token = _split_start([], gath, sg_start, len(gath), pos_arr,
                                                      "gathered_start_" + name)
        return (send_s, recv_s, gath), token

    def gathered_wait(gathered, after, name):
        send_s, recv_s, gath = gathered
        return _split_wait(send_s, recv_s, [], gath, sg_wait, after, "gathered_wait_" + name)[1]

    def adam_big(t, gath, after):
        grads[t], deltas[t], new_m[t], new_v[t] = _adam_gathered(w[t], gath, m[t], v[t], c_arr, after, "adam_" + t)

    def behind(value, token):
        return value + token[0:1, 0:1]

    tags_b, tags_a = ("w_up", "w_down"), ("w_in", "w_out")
    swap_b, tok = swap_start([gw_up, gw_down.reshape(N_CHIPS, -1, D_MODEL)], "b")
    dx1, dy, dsh_f, dsc_f, dg_pre2, dgt_m, dg_post = _ffn_bwd_b(
        dup0, x1, y, dx2, w_up4, g_pre2, behind(sc_f, tok), sh_f, gt_m, g_post)
    exchange_b = exchange_start(swap_b, tags_b, dg_post, "b")
    (dz, dcw, dcb, dwr, dwi, dbr, dbi, dspa, dng, dnb, dws, dbs_t, dglo, dggo) = _seqmix_bwd(
        z, hst, stash, dy, w_out_b, seq_params, ws_t, behind(glo, exchange_b[4]), ggo)
    grad_x, dsh_m, dsc_m, dg_pre = _mix_in_bwd(xs, dz, dx1, w_in4, g_pre, sc_m)
    gw_in = _wgrad(h, dz, N_CHIPS, "wgrad_in", True)
    gw_out = _wgrad(ycat, dy, 1, "wgrad_out", False)
    swap_a, tok = swap_start([gw_in, gw_out.reshape(N_CHIPS, -1, D_MODEL)], "a")

    dmod = jnp.concatenate([behind(dsh_m, tok), dsc_m, dgt_m, dsh_f, dsc_f, dgt_f], axis=1)
    dmod8 = _allgather8(dmod, "gather_dmod", False)[:, 0, :]
    small_grads = dict(
        g_mix_pre=dg_pre, g_mix_post=dg_post, conv_w=dcw, conv_b=dcb, w_rgate=dwr, b_rgate=dbr, w_igate=dwi,
        b_igate=dbi, lru_a=dspa, v_norm_g=dng, v_norm_b=dnb, w_spatial=dws, b_spatial=dbs_t, g_lru_out=dglo,
        g_gmlp_out=dggo, g_ffn_pre=dg_pre2, g_ffn_post=dg_post2, ffn_conv_w=dfw, ffn_conv_b=dfb,
        loss=loss)
    g_small = _allreduce_small(_pack_small(small_grads, dmod8), "reduce_small")
    total = g_small[_small_pieces("loss", 0, 1)[0][:3]]
    exchange_a = exchange_start(swap_a, tags_a, g_small, "a")
    gathered_b, tok = gathered_start(exchange_b, exchange_a[4], "b")

    grads["w_ada"], deltas["w_ada"], new_m["w_ada"], new_v["w_ada"], g_b_ada = _ada_bwd(
        c8, behind(dmod8, tok), chip_arr, w["w_ada"], m["w_ada"], v["w_ada"])
    rep = SMALL_REPLICATED
    small_out = _adam_small(g_small, {n: args[n] for n in rep}, {n: args["m_" + n] for n in rep},
                            {n: args["v_" + n] for n in rep})
    for n in rep:
        grads[n], deltas[n], new_m[n], new_v[n] = [group[n] for group in small_out]
    for n in SMALL_COLUMN_SHARDED:
        grads[n], deltas[n], new_m[n], new_v[n] = _adam_cols(n, g_small, args[n], args["m_" + n],
                                                             args["v_" + n], chip_arr)
    d_b, m_b, v_b = _adam(w["b_ada"], g_b_ada, m["b_ada"], v["b_ada"], "adam_b_ada")
    grads["b_ada"], deltas["b_ada"], new_m["b_ada"], new_v["b_ada"] = g_b_ada, d_b, m_b, v_b

    gath_up, gath_down = gathered_wait(gathered_b, d_b, "b")
    adam_big("w_down", gath_down, pos_arr)
    gathered_a, tok = gathered_start(exchange_a, deltas["w_down"], "a")
    adam_big("w_up", gath_up, tok)
    gath_in, gath_out = gathered_wait(gathered_a, deltas["w_up"], "a")
    adam_big("w_in", gath_in, pos_arr)
    adam_big("w_out", gath_out, pos_arr)

    outs = [total, grad_x[None]]
    for group in (grads, deltas, new_m, new_v):
        outs.extend(group[n].reshape(args[n].shape) for n in names)
    return tuple(outs)
```

```python
import functools
import math

import jax
import jax.numpy as jnp
from jax import lax
from jax.experimental import pallas as pl
from jax.experimental.pallas import tpu as pltpu

F32 = jnp.float32
BF16 = jnp.bfloat16
MESH = pl.DeviceIdType.MESH

D_MODEL = 1024
LRU_WIDTH = 512
LRU_HEADS = 8
GMLP_GROUPS = 4
GMLP_BLOCK = 128
CHUNK = 64
D_FF = 3072
N_MOD = 6
EPS = 1e-6
LRU_C = 8.0
N_CHIPS = 4
N_DEV = 8

ADAM_LR = 0.001
ADAM_B1 = 0.9
ADAM_B2 = 0.999
ADAM_EPS = 1e-08
ADAM_WD = 0.01
ADAM_STEP = 10

GELU_C0 = math.sqrt(2.0 / math.pi)
GELU_C1 = 0.044715

VMEM_LIMIT_BYTES = 56 * 1024 * 1024
SUBLANES = 8
BF16_SUBLANES = 16
FFN_CHUNK = 768
SUB_ROWS = 256


def _gelu_gate(x):
    x2 = x * x
    z = x * ((2.0 * GELU_C0 * GELU_C1) * x2 + 2.0 * GELU_C0)
    return 1.0 / (1.0 + jnp.exp(-z)), x2


def _gelu(x):
    t = jnp.tanh(GELU_C0 * (x + GELU_C1 * x * x * x))
    return 0.5 * x * (1.0 + t)


def _gelu_and_grad(x):
    s, x2 = _gelu_gate(x)
    g = x * s
    dz = (6.0 * GELU_C0 * GELU_C1) * x2 + 2.0 * GELU_C0
    return g, s + g * (1.0 - s) * dz


def _sigmoid(x):
    return 1.0 / (1.0 + jnp.exp(-x))


def _log1p(u):
    w = 1.0 + u
    return jnp.where(w == 1.0, u, jnp.log(w) * (u / (w - 1.0)))


def _softplus(x):
    return jnp.maximum(x, 0.0) + _log1p(jnp.exp(-jnp.abs(x)))


def _neg_expm1(x):
    u = jnp.exp(x)
    um1 = u - 1.0
    tiny = um1 == 0.0
    small = um1 * (x / jnp.log(jnp.where(tiny, 2.0, jnp.maximum(u, 0.25))))
    return -jnp.where(tiny, x, jnp.where(x < -1.0, um1, small))


def _msq_rsqrt(v):
    return lax.rsqrt(jnp.mean(v * v, axis=-1, keepdims=True) + EPS)


def _rms_bwd(dyn, yn, r):
    return r * (dyn - yn * jnp.mean(dyn * yn, axis=-1, keepdims=True))


def _colsum(v):
    return jnp.sum(v, axis=0, keepdims=True)


def _shift_down(cur, prev8, k):
    rolled = pltpu.roll(cur, k, 0)
    head = pltpu.roll(prev8, k, 0)
    row8 = lax.broadcasted_iota(jnp.int32, (SUBLANES, cur.shape[1]), 0)
    first = jnp.where(row8 < k, head, rolled[0:SUBLANES])
    return jnp.concatenate([first, rolled[SUBLANES:]], axis=0)


def _shift_up(cur, next8, k):
    t = cur.shape[0]
    rolled = pltpu.roll(cur, t - k, 0)
    tail = pltpu.roll(next8, SUBLANES - k, 0)
    row8 = lax.broadcasted_iota(jnp.int32, (SUBLANES, cur.shape[1]), 0)
    last = jnp.where(row8 >= SUBLANES - k, tail, rolled[t - SUBLANES:])
    return jnp.concatenate([rolled[:t - SUBLANES], last], axis=0)


def _scan_fwd(a, b):
    t = a.shape[0]
    row = lax.broadcasted_iota(jnp.int32, a.shape, 0)
    d = 1
    while d < t:
        keep = row >= d
        a_s = jnp.where(keep, pltpu.roll(a, d, 0), 1.0)
        b_s = jnp.where(keep, pltpu.roll(b, d, 0), 0.0)
        b = a * b_s + b
        a = a * a_s
        d *= 2
    return a, b


def _scan_bwd(a, g):
    t = a.shape[0]
    row = lax.broadcasted_iota(jnp.int32, a.shape, 0)
    d = 1
    while d < t:
        keep = row < t - d
        a_s = jnp.where(keep, pltpu.roll(a, t - d, 0), 1.0)
        g_s = jnp.where(keep, pltpu.roll(g, t - d, 0), 0.0)
        g = a * g_s + g
        a = a * a_s
        d *= 2
    return a, g


def _dot(a, b):
    return jnp.dot(a, b, preferred_element_type=F32)


def _dot_nt(a, b):
    return lax.dot_general(a, b, (((1,), (1,)), ((), ())), preferred_element_type=F32)


def _dot_tn(a, b):
    return lax.dot_general(a, b, (((0,), (0,)), ((), ())), preferred_element_type=F32)


def _rows(ts, cols, rev_of=None):
    if rev_of is None:
        return pl.BlockSpec((ts, cols), lambda i: (i, 0))
    return pl.BlockSpec((ts, cols), lambda i: (rev_of - 1 - i, 0))


def _halo_prev(ts, cols, halo, rev_of=None, col_block=0):
    per = ts // halo
    if rev_of is None:
        return pl.BlockSpec((halo, cols), lambda i: (jnp.maximum(i * per - 1, 0), col_block))
    return pl.BlockSpec((halo, cols), lambda i: (jnp.maximum((rev_of - 1 - i) * per - 1, 0), col_block))


def _full(shape):
    nd = len(shape)
    return pl.BlockSpec(shape, lambda *_: (0,) * nd)


_RESIDENT = pl.BlockSpec(memory_space=pltpu.VMEM)


def _params(sem):
    return pltpu.CompilerParams(dimension_semantics=sem, vmem_limit_bytes=VMEM_LIMIT_BYTES)


def _sds(shape, dtype):
    return jax.ShapeDtypeStruct(shape, dtype)


def _sub_tiles(ts):
    return [slice(r0, r0 + SUB_ROWS) for r0 in range(0, ts, SUB_ROWS)]


def _mix_in(x, sc, sh, g, w_in4, ts=512):
    s, d = x.shape

    def body(x_ref, sc_ref, sh_ref, g_ref, w_ref, z_ref, h_ref):
        for rs in _sub_tiles(ts):
            xv = x_ref[rs, :]
            h = (xv * _msq_rsqrt(xv) * g_ref[...]) * (1.0 + sc_ref[...]) + sh_ref[...]
            hb = h.astype(BF16)
            h_ref[rs, :] = hb
            for k in range(N_CHIPS):
                z_ref[rs, k * 512:(k + 1) * 512] = _dot(hb, w_ref[k])

    return pl.pallas_call(
        body, grid=(s // ts,), name="mix_in",
        in_specs=[_rows(ts, d), _full((1, d)), _full((1, d)), _full((1, d)), _full(w_in4.shape)],
        out_specs=[_rows(ts, 2048), _rows(ts, d)],
        out_shape=[_sds((s, 2048), F32), _sds((s, d), BF16)],
        compiler_params=_params(("parallel",)),
    )(x, sc, sh, g, w_in4)


N_STASH = 12
(ST_XC, ST_R, ST_IG, ST_A, ST_MULT, ST_GL, ST_DGL, ST_U, ST_DU, ST_Q, ST_VHAT, ST_SPB) = range(N_STASH)


def _seq_param_specs():
    return [_full((4, 512)), _full((1, 512)), _full((512, 512)), _full((512, 512)), _full((1, 512)),
            _full((1, 512)), _full((1, 512)), _full((1, 512)), _full((1, 512)), _full((4, 128, 128)),
            _full((128, 4))]


def _seqmix(z, seq_params, glo, ggo, ts=256):
    s = z.shape[0]
    nt = s // ts

    def body(z_ref, zprev_ref, cw_ref, cb_ref, bdr_ref, bdi_ref, br_ref, bi_ref, la_ref, ng_ref, nb_ref,
             ws_ref, bst_ref, glo_ref, ggo_ref, ycat_ref, hst_ref, st_ref, hcarry, sp_scr):
        i = pl.program_id(0)

        @pl.when(i == 0)
        def _():
            hcarry[...] = jnp.zeros_like(hcarry)

        lx = z_ref[:, 0:512]
        prev8 = jnp.where(i == 0, 0.0, zprev_ref[...])
        xc = (cw_ref[3:4, :] * lx + cw_ref[2:3, :] * _shift_down(lx, prev8, 1)
              + cw_ref[1:2, :] * _shift_down(lx, prev8, 2) + cw_ref[0:1, :] * _shift_down(lx, prev8, 3)
              + cb_ref[...])
        xcb = xc.astype(BF16)
        r = _sigmoid(_dot(xcb, bdr_ref[...]) + br_ref[...])
        ig = _sigmoid(_dot(xcb, bdi_ref[...]) + bi_ref[...])
        log_a = (-LRU_C) * r * _softplus(-la_ref[...])
        a = jnp.exp(log_a)
        mult = jnp.sqrt(_neg_expm1(2.0 * log_a))
        acum, hloc = _scan_fwd(a, mult * (ig * xc))
        h = hloc + acum * hcarry[...]
        hcarry[...] = h[ts - 1:ts, :]
        hst_ref[...] = h
        gl, dgl = _gelu_and_grad(z_ref[:, 512:1024])
        y_l = h * gl
        for slot, val in ((ST_XC, xc), (ST_R, r), (ST_IG, ig), (ST_A, a), (ST_MULT, mult), (ST_GL, gl),
                          (ST_DGL, dgl)):
            st_ref[slot] = val

        u, du = _gelu_and_grad(z_ref[:, 1024:1536])
        vg, dvg = _gelu_and_grad(z_ref[:, 1536:2048])
        vc = vg - jnp.mean(vg, axis=-1, keepdims=True)
        rstd = lax.rsqrt(jnp.mean(vc * vc, axis=-1, keepdims=True) + EPS)
        vhat = vc * rstd
        vb = (vhat * ng_ref[...] + nb_ref[...]).astype(BF16)
        for n in range(ts // GMLP_BLOCK):
            rs = slice(n * GMLP_BLOCK, (n + 1) * GMLP_BLOCK)
            for g in range(GMLP_GROUPS):
                cs = slice(g * 128, (g + 1) * 128)
                sp_scr[rs, cs] = _dot(ws_ref[g], vb[rs, cs]) + bst_ref[:, g:g + 1]
        spb = sp_scr[...]
        y_g = u * spb
        for slot, val in ((ST_U, u), (ST_DU, du), (ST_Q, rstd * dvg), (ST_VHAT, vhat), (ST_SPB, spb)):
            st_ref[slot] = val

        ycat_ref[:, 0:512] = (y_l * _msq_rsqrt(y_l) * glo_ref[...]).astype(BF16)
        ycat_ref[:, 512:1024] = (y_g * _msq_rsqrt(y_g) * ggo_ref[...]).astype(BF16)

    return pl.pallas_call(
        body, grid=(nt,), name="seqmix",
        in_specs=[_rows(ts, 2048), _halo_prev(ts, 512, SUBLANES)] + _seq_param_specs()
        + [_full((1, 512)), _full((1, 512))],
        out_specs=[_rows(ts, 1024), _rows(ts, 512), pl.BlockSpec((N_STASH, ts, 512), lambda i: (0, i, 0))],
        out_shape=[_sds((s, 1024), BF16), _sds((s, 512), F32), _sds((N_STASH, s, 512), F32)],
        scratch_shapes=[pltpu.VMEM((1, 512), F32), pltpu.VMEM((ts, 512), F32)],
        compiler_params=_params(("arbitrary",)),
    )(z, z, *seq_params, glo, ggo)


def _mix_out(ycat, x, w_out, gt_m, g_post, g_pre2, sc_f, sh_f, ts=512):
    s, d = x.shape

    def body(yc_ref, x_ref, w_ref, gt_ref, gp_ref, g2_ref, sc_ref, sh_ref, y_ref, x1_ref, h2_ref):
        for rs in _sub_tiles(ts):
            y = _dot(yc_ref[rs, :], w_ref[...])
            y_ref[rs, :] = y
            x1 = x_ref[rs, :] + gt_ref[...] * (y * _msq_rsqrt(y) * gp_ref[...])
            x1_ref[rs, :] = x1
            h2 = (x1 * _msq_rsqrt(x1) * g2_ref[...]) * (1.0 + sc_ref[...]) + sh_ref[...]
            h2_ref[rs, :] = h2.astype(BF16)

    vec = _full((1, d))
    return pl.pallas_call(
        body, grid=(s // ts,), name="mix_out",
        in_specs=[_rows(ts, d), _rows(ts, d), _full((d, d)), vec, vec, vec, vec, vec],
        out_specs=[_rows(ts, d), _rows(ts, d), _rows(ts, d)],
        out_shape=[_sds((s, d), F32), _sds((s, d), F32), _sds((s, d), BF16)],
        compiler_params=_params(("parallel",)),
    )(ycat, x, w_out, gt_m, g_post, g_pre2, sc_f, sh_f)


def _ffn_cols(j):
    per = (2 * D_FF // N_CHIPS) // FFN_CHUNK
    return j // per, (j % per) * FFN_CHUNK, j * FFN_CHUNK


def _ffn_fwd(h2, x1, tgt, w_up4, w_down, fw, fb, gt_f, g_post, ts=256):
    s, d = x1.shape
    nch = D_FF // FFN_CHUNK

    def body(h2_ref, x1_ref, tgt_ref, wup_ref, wdn_ref, fw_ref, fb_ref, gt_ref, gp_ref,
             up0_ref, pre_ref, act_ref, dy2_ref, dx2_ref, loss_ref, dgt_ref, dgp_ref, tail_ref):
        i = pl.program_id(0)

        @pl.when(i == 0)
        def _():
            tail_ref[...] = jnp.zeros_like(tail_ref)
            loss_ref[...] = jnp.zeros_like(loss_ref)
            dgt_ref[...] = jnp.zeros_like(dgt_ref)
            dgp_ref[...] = jnp.zeros_like(dgp_ref)

        hb = h2_ref[...]

        def up_project(j):
            sh_g, off, _ = _ffn_cols(j)
            return [_dot(hb, wup_ref[shard, :, off:off + FFN_CHUNK]).astype(BF16) for shard in (sh_g, sh_g + 2)]

        y2 = jnp.zeros((ts, d), F32)
        ahead = up_project(0)
        for j in range(nch):
            _, _, col = _ffn_cols(j)
            ubs = ahead
            if j + 1 < nch:
                ahead = up_project(j + 1)
            halves = []
            for ub, c0 in zip(ubs, (col, D_FF + col)):
                cs = slice(c0, c0 + FFN_CHUNK)
                up0_ref[:, cs] = ub
                u = ub.astype(F32)
                prev8 = tail_ref[:, cs]
                tail_ref[:, cs] = u[ts - SUBLANES:, :]
                halves.append(fw_ref[2:3, cs] * u + fw_ref[1:2, cs] * _shift_down(u, prev8, 1)
                              + fw_ref[0:1, cs] * _shift_down(u, prev8, 2) + fb_ref[:, cs])
                pre_ref[:, cs] = halves[-1].astype(BF16)
            act = (_gelu(halves[0]) * halves[1]).astype(BF16)
            act_ref[:, col:col + FFN_CHUNK] = act
            y2 = y2 + _dot(act, wdn_ref[col:col + FFN_CHUNK, :])
        r2 = _msq_rsqrt(y2)
        yn = y2 * r2
        yng = yn * gp_ref[...]
        e = x1_ref[...] + gt_ref[...] * yng - tgt_ref[...]
        loss_ref[...] += jnp.sum(e * e) * (0.5 / d)
        dx2 = e * (1.0 / d)
        dx2_ref[...] = dx2
        dgt_ref[...] += _colsum(dx2 * yng)
        dyng = dx2 * gt_ref[...]
        dgp_ref[...] += _colsum(dyng * yn)
        dy2_ref[...] = _rms_bwd(dyng * gp_ref[...], yn, r2).astype(BF16)

    vec = _full((1, d))
    return pl.pallas_call(
        body, grid=(s // ts,), name="ffn_fwd",
        in_specs=[_rows(ts, d), _rows(ts, d), _rows(ts, d), _RESIDENT, _RESIDENT,
                  _full((3, 2 * D_FF)), _full((1, 2 * D_FF)), vec, vec],
        out_specs=[_rows(ts, 2 * D_FF), _rows(ts, 2 * D_FF), _rows(ts, D_FF), _rows(ts, d), _rows(ts, d),
                   _full((1, 128)), vec, vec],
        out_shape=[_sds((s, 2 * D_FF), BF16), _sds((s, 2 * D_FF), BF16), _sds((s, D_FF), BF16), _sds((s, d), BF16),
                   _sds((s, d), F32), _sds((1, 128), F32), _sds((1, d), F32), _sds((1, d), F32)],
        scratch_shapes=[pltpu.VMEM((SUBLANES, 2 * D_FF), F32)],
        compiler_params=_params(("arbitrary",)),
    )(h2, x1, tgt, w_up4, w_down, fw, fb, gt_f, g_post)


def _shift_up_mxu(vb, up_mat, next8, k):
    t = vb.shape[0]
    main = _dot(up_mat, vb)
    tail = pltpu.roll(next8, SUBLANES - k, 0)
    row8 = lax.broadcasted_iota(jnp.int32, next8.shape, 0)
    last = main[t - SUBLANES:] + jnp.where(row8 >= SUBLANES - k, tail, 0.0)
    return jnp.concatenate([main[:t - SUBLANES], last], axis=0)


def _ffn_bwd_a(dy2, pre, up0, w_down, fw, ts=256):
    s, d = dy2.shape
    nt = s // ts
    nch = D_FF // FFN_CHUNK
    wide = 2 * D_FF
    up_mats = jnp.stack([jnp.eye(ts, k=1, dtype=BF16), jnp.eye(ts, k=2, dtype=BF16)])

    def body(dy2_ref, pre_ref, up0_ref, wdn_ref, fw_ref, um_ref, dup0_ref, dfw_ref, dfb_ref, next_ref):
        i = pl.program_id(0)

        @pl.when(i == 0)
        def _():
            next_ref[...] = jnp.zeros_like(next_ref)
            dfw_ref[...] = jnp.zeros_like(dfw_ref)
            dfb_ref[...] = jnp.zeros_like(dfb_ref)

        dyb = dy2_ref[...]
        for j in range(nch):
            _, _, col = _ffn_cols(j)
            dact = _dot_nt(dyb, wdn_ref[col:col + FFN_CHUNK, :])
            gl, dgl = _gelu_and_grad(pre_ref[:, col:col + FFN_CHUNK].astype(F32))
            dpre = (dact * pre_ref[:, D_FF + col:D_FF + col + FFN_CHUNK].astype(F32) * dgl, dact * gl)
            for half, c0 in enumerate((col, D_FF + col)):
                cs = slice(c0, c0 + FFN_CHUNK)
                dp = dpre[half]
                dpb = dp.astype(BF16)
                nxt = next_ref[:, cs]
                next_ref[:, cs] = dpb.astype(F32)[0:SUBLANES, :]
                su1 = _shift_up_mxu(dpb, um_ref[0], nxt, 1)
                su2 = _shift_up_mxu(dpb, um_ref[1], nxt, 2)
                u = up0_ref[:, cs].astype(F32)
                dfb_ref[:, cs] += _colsum(dp)
                dfw_ref[2:3, cs] += _colsum(dp * u)
                dfw_ref[1:2, cs] += _colsum(su1 * u)
                dfw_ref[0:1, cs] += _colsum(su2 * u)
                dup0 = fw_ref[2:3, cs] * dp + fw_ref[1:2, cs] * su1 + fw_ref[0:1, cs] * su2
                dup0_ref[:, cs] = dup0.astype(BF16)

    return pl.pallas_call(
        body, grid=(nt,), name="ffn_bwd_a",
        in_specs=[_rows(ts, d, nt), _rows(ts, wide, nt), _rows(ts, wide, nt), _RESIDENT,
                  _full((3, wide)), _full((2, ts, ts))],
        out_specs=[_rows(ts, wide, nt), _full((3, wide)), _full((1, wide))],
        out_shape=[_sds((s, wide), BF16), _sds((3, wide), F32), _sds((1, wide), F32)],
        scratch_shapes=[pltpu.VMEM((SUBLANES, wide), F32)],
        compiler_params=_params(("arbitrary",)),
    )(dy2, pre, up0, w_down, fw, up_mats)


def _ffn_bwd_b(dup0, x1, y, dx2, w_up4, g_pre2, sc_f, sh_f, gt_m, g_post_m, ts=512):
    s, d = x1.shape
    shard_cols = 2 * D_FF // N_CHIPS

    def body(dup_ref, x1_ref, y_ref, dx2_ref, wup_ref, g2_ref, sc_ref, sh_ref, gt_ref, gp_ref,
             dx1_ref, dy_ref, dsh_ref, dsc_ref, dg2_ref, dgt_ref, dgp_ref):
        i = pl.program_id(0)

        @pl.when(i == 0)
        def _():
            for ref in (dsh_ref, dsc_ref, dg2_ref, dgt_ref, dgp_ref):
                ref[...] = jnp.zeros_like(ref)

        for rs in _sub_tiles(ts):
            dh2 = jnp.zeros((SUB_ROWS, d), F32)
            for k in range(N_CHIPS):
                dh2 = dh2 + _dot_nt(dup_ref[rs, k * shard_cols:(k + 1) * shard_cols], wup_ref[k])
            x1v = x1_ref[rs, :]
            r2 = _msq_rsqrt(x1v)
            xn = x1v * r2
            hn = xn * g2_ref[...]
            dsh_ref[...] += _colsum(dh2)
            dsc_ref[...] += _colsum(dh2 * hn)
            dhn = dh2 * (1.0 + sc_ref[...])
            dg2_ref[...] += _colsum(dhn * xn)
            dx1 = dx2_ref[rs, :] + _rms_bwd(dhn * g2_ref[...], xn, r2)
            dx1_ref[rs, :] = dx1
            yv = y_ref[rs, :]
            ry = _msq_rsqrt(yv)
            yn = yv * ry
            dgt_ref[...] += _colsum(dx1 * (yn * gp_ref[...]))
            dyng = dx1 * gt_ref[...]
            dgp_ref[...] += _colsum(dyng * yn)
            dy_ref[rs, :] = _rms_bwd(dyng * gp_ref[...], yn, ry).astype(BF16)

    vec = _full((1, d))
    return pl.pallas_call(
        body, grid=(s // ts,), name="ffn_bwd_b",
        in_specs=[_rows(ts, 2 * D_FF), _rows(ts, d), _rows(ts, d), _rows(ts, d), _RESIDENT,
                  vec, vec, vec, vec, vec],
        out_specs=[_rows(ts, d), _rows(ts, d), vec, vec, vec, vec, vec],
        out_shape=[_sds((s, d), F32), _sds((s, d), BF16)] + [_sds((1, d), F32)] * 5,
        compiler_params=_params(("arbitrary",)),
    )(dup0, x1, y, dx2, w_up4, g_pre2, sc_f, sh_f, gt_m, g_post_m)


def _seqmix_bwd(z, hst, stash, dy, w_out, seq_params, ws_t, glo, ggo, ts=256):
    s = z.shape[0]
    nt = s // ts
    small_shapes = [(4, 512), (1, 512), (512, 512), (512, 512), (1, 512), (1, 512), (1, 512),
                    (1, 512), (1, 512), (4, 128, 128), (128, 4), (1, 512), (1, 512)]

    def body(lx_ref, hst_ref, hprev_ref, st_ref, dy_ref, wout_ref, cw_ref, cb_ref, bdr_ref, bdi_ref, br_ref,
             bi_ref, la_ref, ng_ref, nb_ref, ws_ref, bst_ref, wst_ref, glo_ref, ggo_ref, dz_ref, *rest):
        small_refs = rest[:13]
        (dcw_ref, dcb_ref, dwr_ref, dwi_ref, dbr_ref, dbi_ref, dspa_ref, dng_ref, dnb_ref, dws_ref, dbs_ref,
         dglo_ref, dggo_ref) = small_refs
        gcarry, anext, dxcnext, dv_scr = rest[13:]
        i = pl.program_id(0)

        @pl.when(i == 0)
        def _():
            for ref in small_refs:
                ref[...] = jnp.zeros_like(ref)
            gcarry[...] = jnp.zeros_like(gcarry)
            anext[...] = jnp.ones_like(anext)
            dxcnext[...] = jnp.zeros_like(dxcnext)

        first_tile = i == nt - 1
        xc, r, ig, a, mult = st_ref[ST_XC], st_ref[ST_R], st_ref[ST_IG], st_ref[ST_A], st_ref[ST_MULT]
        gl, u, spb, vhat = st_ref[ST_GL], st_ref[ST_U], st_ref[ST_SPB], st_ref[ST_VHAT]
        lx = lx_ref[...]
        h = hst_ref[...]
        hprev = _shift_down(h, jnp.where(first_tile, 0.0, hprev_ref[...]), 1)
        y_l = h * gl
        y_g = u * spb

        dycat = _dot_nt(dy_ref[...], wout_ref[...])

        def emit_dz(k, val):
            dz_ref[:, k * 512:(k + 1) * 512] = val.astype(BF16)

        rl = _msq_rsqrt(y_l)
        yln = y_l * rl
        dyl = dycat[:, 0:512]
        dglo_ref[...] += _colsum(dyl * yln)
        dy_l = _rms_bwd(dyl * glo_ref[...], yln, rl)
        rg = _msq_rsqrt(y_g)
        ygn = y_g * rg
        dyg = dycat[:, 512:1024]
        dggo_ref[...] += _colsum(dyg * ygn)
        dy_g = _rms_bwd(dyg * ggo_ref[...], ygn, rg)

        emit_dz(1, dy_l * h * st_ref[ST_DGL])
        a_up = _shift_up(a, anext[...], 1)
        acum, gloc = _scan_bwd(a_up, dy_l * gl)
        gg = gloc + acum * gcarry[...]
        gcarry[...] = gg[0:1, :]
        anext[...] = a[0:SUBLANES, :]
        da = gg * hprev
        t1 = gg * mult
        di = t1 * xc
        dxc = t1 * ig
        dmult = gg * ig * xc
        dla = da * a - dmult * (a * a / mult)
        dspa_ref[...] += _colsum(dla * r) * (-LRU_C)
        dpr = dla * ((-LRU_C) * _softplus(-la_ref[...])) * r * (1.0 - r)
        dpi = di * ig * (1.0 - ig)
        dbr_ref[...] += _colsum(dpr)
        dbi_ref[...] += _colsum(dpi)
        dprb = dpr.astype(BF16)
        dpib = dpi.astype(BF16)
        xcb = xc.astype(BF16)
        dwr_ref[...] += _dot_tn(xcb, dprb)
        dwi_ref[...] += _dot_tn(xcb, dpib)
        dxc = dxc + _dot_nt(dprb, bdr_ref[...]) + _dot_nt(dpib, bdi_ref[...])
        nxt = dxcnext[...]
        dxcnext[...] = dxc[0:SUBLANES, :]
        up1, up2, up3 = _shift_up(dxc, nxt, 1), _shift_up(dxc, nxt, 2), _shift_up(dxc, nxt, 3)
        dcb_ref[...] += _colsum(dxc)
        dcw_ref[3:4, :] += _colsum(dxc * lx)
        dcw_ref[2:3, :] += _colsum(up1 * lx)
        dcw_ref[1:2, :] += _colsum(up2 * lx)
        dcw_ref[0:1, :] += _colsum(up3 * lx)
        dlx = cw_ref[3:4, :] * dxc + cw_ref[2:3, :] * up1 + cw_ref[1:2, :] * up2 + cw_ref[0:1, :] * up3
        emit_dz(0, dlx)

        emit_dz(2, dy_g * spb * st_ref[ST_DU])
        dsp = dy_g * u
        vb = (vhat * ng_ref[...] + nb_ref[...]).astype(BF16)
        for n in range(ts // GMLP_BLOCK):
            rs = slice(n * GMLP_BLOCK, (n + 1) * GMLP_BLOCK)
            for g in range(GMLP_GROUPS):
                cs = slice(g * 128, (g + 1) * 128)
                dbs_ref[:, g:g + 1] += jnp.sum(dsp[rs, cs], axis=1, keepdims=True)
                blk = dsp[rs, cs].astype(BF16)
                dws_ref[g] += _dot_nt(blk, vb[rs, cs])
                dv_scr[rs, cs] = _dot(wst_ref[g], blk)
        dv = dv_scr[...]
        dng_ref[...] += _colsum(dv * vhat)
        dnb_ref[...] += _colsum(dv)
        dvh = dv * ng_ref[...]
        dvg = dvh - jnp.mean(dvh, axis=-1, keepdims=True) - vhat * jnp.mean(dvh * vhat, axis=-1, keepdims=True)
        emit_dz(3, dvg * st_ref[ST_Q])

        @pl.when(i == nt - 1)
        def _():
            pos = lax.broadcasted_iota(jnp.int32, (GMLP_BLOCK, GMLP_BLOCK), 0) // CHUNK
            src = lax.broadcasted_iota(jnp.int32, (GMLP_BLOCK, GMLP_BLOCK), 1) // CHUNK
            for g in range(GMLP_GROUPS):
                dws_ref[g] = jnp.where(src <= pos, dws_ref[g], 0.0)
            dspa_ref[...] = dspa_ref[...] * (-_sigmoid(-la_ref[...]))

    in_specs = ([_rows(ts, 512, nt), _rows(ts, 512, nt), _halo_prev(ts, 512, SUBLANES, nt),
                 pl.BlockSpec((N_STASH, ts, 512), lambda i: (0, nt - 1 - i, 0)), _rows(ts, 1024, nt),
                 _full((1024, 1024))]
                + _seq_param_specs() + [_full((4, 128, 128)), _full((1, 512)), _full((1, 512))])
    return pl.pallas_call(
        body, grid=(nt,), name="seqmix_bwd",
        in_specs=in_specs,
        out_specs=[_rows(ts, 2048, nt)] + [_full(sh) for sh in small_shapes],
        out_shape=[_sds((s, 2048), BF16)] + [_sds(sh, F32) for sh in small_shapes],
        scratch_shapes=[pltpu.VMEM((1, 512), F32), pltpu.VMEM((SUBLANES, 512), F32),
                        pltpu.VMEM((SUBLANES, 512), F32), pltpu.VMEM((ts, 512), F32)],
        compiler_params=_params(("arbitrary",)),
    )(z, hst, hst, stash, dy, w_out, *seq_params, ws_t, glo, ggo)


def _mix_in_bwd(x, dz, dx1, w_in4, g, sc, ts=512):
    s, d = x.shape

    def body(x_ref, dz_ref, dx1_ref, w_ref, g_ref, sc_ref, gx_ref, dsh_ref, dsc_ref, dg_ref):
        i = pl.program_id(0)

        @pl.when(i == 0)
        def _():
            for ref in (dsh_ref, dsc_ref, dg_ref):
                ref[...] = jnp.zeros_like(ref)

        for rs in _sub_tiles(ts):
            dh = jnp.zeros((SUB_ROWS, d), F32)
            for k in range(N_CHIPS):
                dh = dh + _dot_nt(dz_ref[rs, k * 512:(k + 1) * 512], w_ref[k])
            xv = x_ref[rs, :]
            r = _msq_rsqrt(xv)
            xn = xv * r
            dsh_ref[...] += _colsum(dh)
            dsc_ref[...] += _colsum(dh * (xn * g_ref[...]))
            dhn = dh * (1.0 + sc_ref[...])
            dg_ref[...] += _colsum(dhn * xn)
            gx_ref[rs, :] = dx1_ref[rs, :] + _rms_bwd(dhn * g_ref[...], xn, r)

    vec = _full((1, d))
    return pl.pallas_call(
        body, grid=(s // ts,), name="mix_in_bwd",
        in_specs=[_rows(ts, d), _rows(ts, 2048), _rows(ts, d), _full(w_in4.shape), vec, vec],
        out_specs=[_rows(ts, d), vec, vec, vec],
        out_shape=[_sds((s, d), F32)] + [_sds((1, d), F32)] * 3,
        compiler_params=_params(("arbitrary",)),
    )(x, dz, dx1, w_in4, g, sc)


def _wgrad(a, b, n_chunks, name, chunk_major, ts=2048):
    s, m = a.shape
    n = b.shape[1]
    nc = n // n_chunks
    nt = s // ts

    def body(a_ref, b_ref, o_ref, acc):
        i = pl.program_id(1)

        @pl.when(i == 0)
        def _():
            acc[...] = jnp.zeros_like(acc)

        acc[...] += _dot_tn(a_ref[...], b_ref[...])

        @pl.when(i == nt - 1)
        def _():
            if chunk_major:
                o_ref[0] = acc[...].astype(BF16)
            else:
                o_ref[...] = acc[...].astype(BF16)

    if chunk_major:
        out_spec, out_shape = pl.BlockSpec((1, m, nc), lambda c, i: (c, 0, 0)), _sds((n_chunks, m, nc), BF16)
    else:
        out_spec, out_shape = pl.BlockSpec((m, nc), lambda c, i: (0, c)), _sds((m, n), BF16)
    return pl.pallas_call(
        body, grid=(n_chunks, nt), name=name,
        in_specs=[pl.BlockSpec((ts, m), lambda c, i: (i, 0)), pl.BlockSpec((ts, nc), lambda c, i: (i, c))],
        out_specs=out_spec,
        out_shape=out_shape,
        scratch_shapes=[pltpu.VMEM((m, nc), F32)],
        compiler_params=_params(("parallel", "arbitrary")),
    )(a, b)


def _block_diag(w):
    heads, hd, _ = w.shape
    eye = jnp.eye(heads, dtype=w.dtype)
    return (eye[:, None, :, None] * w[:, :, None, :]).reshape(heads * hd, heads * hd)


def _seq_params(small):
    row = lambda v: v.reshape(1, -1)
    pos = jnp.arange(GMLP_BLOCK)
    mask = (pos[None, :] // CHUNK) <= (pos[:, None] // CHUNK)
    ws = jnp.where(mask[None], small["w_spatial"], 0.0)
    seq_params = (small["conv_w"], row(small["conv_b"]),
                  _block_diag(small["w_rgate"]).astype(BF16), _block_diag(small["w_igate"]).astype(BF16),
                  row(small["b_rgate"]), row(small["b_igate"]), row(small["lru_a"]),
                  row(small["v_norm_g"]), row(small["v_norm_b"]), ws.astype(BF16), small["b_spatial"].T)
    return seq_params, jnp.swapaxes(ws, 1, 2).astype(BF16)


_ANY = pl.BlockSpec(memory_space=pl.ANY)
_CHIP_FLIPS = ((1, 0), (0, 1), (1, 1))


def _position():
    return lax.axis_index("x"), lax.axis_index("y"), lax.axis_index("c")


def _flip(v, f):
    return 1 - v if f else v


def _remote(src, dst, send_sem, recv_sem, peer):
    return pltpu.make_async_remote_copy(src_ref=src, dst_ref=dst, send_sem=send_sem, recv_sem=recv_sem,
                                        device_id=peer, device_id_type=MESH)


def _allgather8(block, name):
    r, n = block.shape

    def body(x_ref, gath, send_sems, recv_sems, loc_sem):
        x, y, c = _position()
        me = 4 * x + 2 * y + c
        loc = pltpu.make_async_copy(x_ref, gath.at[me], loc_sem)
        loc.start()
        peers = []
        for k in range(1, N_DEV):
            px, py, pc = _flip(x, k & 4), _flip(y, k & 2), _flip(c, k & 1)
            peers.append((px, py, pc))
            _remote(x_ref, gath.at[me], send_sems.at[k - 1], recv_sems.at[k - 1], (px, py, pc)).start()
        for k, (px, py, pc) in enumerate(peers):
            src = 4 * px + 2 * py + pc
            _remote(x_ref, gath.at[src], send_sems.at[k], recv_sems.at[k], (px, py, pc)).wait_recv()
        for k, peer in enumerate(peers):
            _remote(x_ref, gath.at[me], send_sems.at[k], recv_sems.at[k], peer).wait_send()
        loc.wait()

    return pl.pallas_call(
        body, name=name, out_shape=_sds((N_DEV, r, n), F32),
        in_specs=[pl.BlockSpec(memory_space=pltpu.VMEM)], out_specs=pl.BlockSpec(memory_space=pltpu.VMEM),
        scratch_shapes=[pltpu.SemaphoreType.DMA((N_DEV - 1,)), pltpu.SemaphoreType.DMA((N_DEV - 1,)),
                        pltpu.SemaphoreType.DMA],
        compiler_params=pltpu.CompilerParams(vmem_limit_bytes=VMEM_LIMIT_BYTES),
    )(block)


def _half(ref, c, rows):
    hr = rows // 2
    return ref.at[pl.ds(pl.multiple_of(c * hr, BF16_SUBLANES), hr), :]


def _chip_sum(part, recv, pos_arr, name):
    _, rows, cols = part.shape
    hr = rows // 2

    def body(pos_ref, p_ref, r_ref, o_ref, g_ref):
        total = (p_ref[...].astype(F32) + r_ref[...].astype(F32)).astype(BF16)
        o_ref[...] = total

        @pl.when(pl.program_id(0) == pos_ref[1])
        def _():
            g_ref[0] = total

    grid_spec = pltpu.PrefetchScalarGridSpec(
        num_scalar_prefetch=1, grid=(N_CHIPS,),
        in_specs=[pl.BlockSpec((1, hr, cols), lambda k, pos: (k, pos[0], 0)),
                  pl.BlockSpec((1, hr, cols), lambda k, pos: (k, 0, 0))],
        out_specs=[pl.BlockSpec((1, hr, cols), lambda k, pos: (k, 0, 0)),
                   pl.BlockSpec((1, 1, hr, cols), lambda k, pos: (0, pos[1], 0, 0))])
    return pl.pallas_call(
        body, name=name, grid_spec=grid_spec,
        out_shape=[_sds((N_CHIPS, hr, cols), BF16), _sds((2, N_CHIPS, hr, cols), BF16)],
        compiler_params=_params(("arbitrary",)),
    )(pos_arr, part, recv)


_HBM = pl.BlockSpec(memory_space=pltpu.HBM)
_SEM = pl.BlockSpec(memory_space=pltpu.SEMAPHORE)
_EFFECT = pltpu.SideEffectType.DATAFLOW_SIDE_EFFECTING


def _in_hbm(a):
    return pltpu.with_memory_space_constraint(a, pltpu.HBM)


def _split_start(srcs, lands, plan, n_copies, after, name):
    ns, nl = len(srcs), len(lands)
    bufs = list(srcs) + list(lands)

    def body(*refs):
        send_sems, recv_sems = refs[ns + nl + 1], refs[ns + nl + 2]
        token = refs[-1]
        for k, (src, dst, peer) in enumerate(plan(refs[:ns], refs[ns:ns + nl])):
            _remote(src, dst, send_sems.at[k], recv_sems.at[k], peer).start()
        token[...] = jnp.zeros_like(token)

    out = pl.pallas_call(
        body, name=name,
        out_shape=(pltpu.SemaphoreType.DMA((n_copies,)), pltpu.SemaphoreType.DMA((n_copies,)),
                   *[pltpu.HBM(b.shape, b.dtype) for b in bufs], _sds((SUBLANES, 128), F32)),
        in_specs=[_HBM] * (ns + nl) + [_ANY],
        out_specs=(_SEM, _SEM, *[_HBM] * (ns + nl), pl.BlockSpec(memory_space=pltpu.VMEM)),
        input_output_aliases={i: 2 + i for i in range(ns + nl)},
        compiler_params=pltpu.CompilerParams(has_side_effects=_EFFECT),
    )(*[_in_hbm(b) for b in bufs], after)
    return out[0], out[1], list(out[2:2 + ns]), list(out[2 + ns:2 + ns + nl]), out[-1]


def _split_wait(send_sems, recv_sems, srcs, lands, plan, after, name):
    ns, nl = len(srcs), len(lands)
    bufs = list(srcs) + list(lands)

    def body(*refs):
        send_ref, recv_ref = refs[ns + nl], refs[ns + nl + 1]
        me = _position()
        for k, src, dst in plan(refs[:ns], refs[ns:ns + nl]):
            cp = _remote(src, dst, send_ref.at[k], recv_ref.at[k], me)
            cp.wait_send()
            cp.wait_recv()

    out = pl.pallas_call(
        body, name=name,
        out_shape=[pltpu.HBM(b.shape, b.dtype) for b in bufs],
        in_specs=[_HBM] * (ns + nl) + [_SEM, _SEM, _ANY],
        out_specs=[_HBM] * (ns + nl),
        input_output_aliases={i: i for i in range(ns + nl)},
        compiler_params=pltpu.CompilerParams(has_side_effects=_EFFECT),
    )(*bufs, send_sems, recv_sems, after)
    return list(out[:ns]), list(out[ns:])


def _gather_plan(rows_of):
    def start(src_refs, land_refs):
        x, y, c = _position()
        chip = 2 * x + y
        out = []
        for a, rows in enumerate(rows_of):
            mine = _half(land_refs[a].at[chip], c, rows)
            out.extend((mine, mine, (_flip(x, fx), _flip(y, fy), c)) for fx, fy in _CHIP_FLIPS)
        return out

    def wait(src_refs, land_refs):
        x, y, c = _position()
        chip = 2 * x + y
        out = []
        for a, rows in enumerate(rows_of):
            for j, (fx, fy) in enumerate(_CHIP_FLIPS):
                src_chip = 2 * _flip(x, fx) + _flip(y, fy)
                out.append((3 * a + j, _half(land_refs[a].at[chip], c, rows),
                            _half(land_refs[a].at[src_chip], c, rows)))
        return out

    return start, wait


def _forward_plan(rows_of):
    def pieces(land_refs, half):
        x, y, _ = _position()
        return [_half(land_refs[a].at[2 * _flip(x, fx) + _flip(y, fy)], half, rows)
                for a, rows in enumerate(rows_of) for fx, fy in _CHIP_FLIPS]

    def start(src_refs, land_refs):
        x, y, c = _position()
        return [(p, p, (x, y, 1 - c)) for p in pieces(land_refs, c)]

    def wait(src_refs, land_refs):
        _, _, c = _position()
        return [(k, mine, theirs)
                for k, (mine, theirs) in enumerate(zip(pieces(land_refs, c), pieces(land_refs, 1 - c)))]

    return start, wait


def _swap_halves_plan(half_rows):
    def slices(src_refs, c):
        return [src_refs[a].at[:, pl.ds(pl.multiple_of((1 - c) * hr, BF16_SUBLANES), hr), :]
                for a, hr in enumerate(half_rows)]

    def start(src_refs, land_refs):
        x, y, c = _position()
        return [(src, land_refs[a], (x, y, 1 - c)) for a, src in enumerate(slices(src_refs, c))]

    def wait(src_refs, land_refs):
        _, _, c = _position()
        return [(a, src, land_refs[a]) for a, src in enumerate(slices(src_refs, c))]

    return start, wait


def _swap_gathered_plan(n_arrays):
    def start(src_refs, land_refs):
        x, y, c = _position()
        return [(land_refs[a].at[0], land_refs[a].at[1], (x, y, 1 - c)) for a in range(n_arrays)]

    def wait(src_refs, land_refs):
        return [(a, land_refs[a].at[0], land_refs[a].at[1]) for a in range(n_arrays)]

    return start, wait


def _exchange_plan(n_arrays):
    def start(src_refs, land_refs):
        x, y, c = _position()
        chip = 2 * x + y
        out = []
        for a in range(n_arrays):
            for fx, fy in _CHIP_FLIPS:
                px, py = _flip(x, fx), _flip(y, fy)
                out.append((src_refs[a].at[2 * px + py], land_refs[a].at[0, chip], (px, py, c)))
        return out

    def wait(src_refs, land_refs):
        x, y, c = _position()
        out = []
        for a in range(n_arrays):
            for j, (fx, fy) in enumerate(_CHIP_FLIPS):
                src_chip = 2 * _flip(x, fx) + _flip(y, fy)
                out.append((3 * a + j, src_refs[a].at[src_chip], land_refs[a].at[0, src_chip]))
        return out

    return start, wait


def _forward_to_sibling(lands, name):
    na = len(lands)

    def body(*refs):
        land_refs = refs[na:2 * na]
        send_sems, recv_sems = refs[2 * na:]
        x, y, c = _position()
        sibling = (x, y, 1 - c)
        sends = []
        for a in range(na):
            rows = lands[a].shape[1]
            for j, (fx, fy) in enumerate(_CHIP_FLIPS):
                landed = _half(land_refs[a].at[2 * _flip(x, fx) + _flip(y, fy)], c, rows)
                sends.append(_remote(landed, landed, send_sems.at[3 * a + j], recv_sems.at[3 * a + j], sibling))
                sends[-1].start()
        for a in range(na):
            rows = lands[a].shape[1]
            for j, (fx, fy) in enumerate(_CHIP_FLIPS):
                other = _half(land_refs[a].at[2 * _flip(x, fx) + _flip(y, fy)], 1 - c, rows)
                _remote(other, other, send_sems.at[3 * a + j], recv_sems.at[3 * a + j], sibling).wait_recv()
        for cp in sends:
            cp.wait_send()

    return pl.pallas_call(
        body, name=name,
        out_shape=[_sds(l.shape, l.dtype) for l in lands],
        in_specs=[_ANY] * na, out_specs=[_ANY] * na,
        input_output_aliases={a: a for a in range(na)},
        scratch_shapes=[pltpu.SemaphoreType.DMA((3 * na,))] * 2,
    )(*lands)


def _adam_gathered(w, gath, m, v, c_arr, after, name, tr=128):
    rows, cols = w.shape
    hr = rows // 2
    if hr % (2 * tr) == 0:
        tr = 2 * tr
    per = hr // tr

    def body(c_ref, w_ref, g_ref, m_ref, v_ref, after_ref, go_ref, d_ref, nm_ref, nv_ref):
        g = g_ref[0, 0].astype(F32)
        for k in range(1, N_CHIPS):
            g = g + g_ref[0, k].astype(F32)
        go_ref[...] = g
        d_ref[...], nm_ref[...], nv_ref[...] = _adam_math(w_ref[...], g, m_ref[...], v_ref[...])

    def rows_of(h, i, c_ref):
        c = c_ref[0]
        return ((c + h - 2 * c * h) * per + i, 0)

    blk = pl.BlockSpec((tr, cols), rows_of)
    grid_spec = pltpu.PrefetchScalarGridSpec(
        num_scalar_prefetch=1, grid=(2, per),
        in_specs=[blk, pl.BlockSpec((1, N_CHIPS, tr, cols), lambda h, i, c_ref: (h, 0, i, 0)), blk, blk, _ANY],
        out_specs=[blk] * 4)
    return pl.pallas_call(
        body, name=name, grid_spec=grid_spec, out_shape=[_sds(w.shape, F32)] * 4,
        compiler_params=_params(("arbitrary", "arbitrary")),
    )(c_arr, w, gath, m, v, after)


def _allreduce_small(block, name):
    two, r, n = block.shape
    assert two == 2

    def body(x_ref, out_ref, sib, chipsum, gath, d2d_send, d2d_recv, ici_send, ici_recv):
        x, y, c = _position()
        chip = 2 * x + y
        sibling = (x, y, 1 - c)
        first = _remote(x_ref, sib, d2d_send.at[0], d2d_recv.at[0], sibling)
        first.start()
        first.wait()
        chipsum[...] = x_ref[...] + sib[...]
        sends = []
        for j, (fx, fy) in enumerate(_CHIP_FLIPS):
            sends.append(_remote(chipsum.at[c], gath.at[chip], ici_send.at[j], ici_recv.at[j],
                                 (_flip(x, fx), _flip(y, fy), c)))
            sends[-1].start()
        gath[chip] = chipsum[c]
        for j, (fx, fy) in enumerate(_CHIP_FLIPS):
            landed = gath.at[2 * _flip(x, fx) + _flip(y, fy)]
            _remote(landed, landed, ici_send.at[j], ici_recv.at[j], sibling).wait_recv()
        for cp in sends:
            cp.wait_send()
        total = gath[0]
        for k in range(1, N_CHIPS):
            total = total + gath[k]
        out_ref[c] = total
        last = _remote(out_ref.at[c], out_ref.at[c], d2d_send.at[1], d2d_recv.at[1], sibling)
        last.start()
        _remote(out_ref.at[1 - c], out_ref.at[1 - c], d2d_send.at[1], d2d_recv.at[1], sibling).wait_recv()
        last.wait_send()

    vmem = pl.BlockSpec(memory_space=pltpu.VMEM)
    return pl.pallas_call(
        body, name=name, out_shape=_sds(block.shape, F32), in_specs=[vmem], out_specs=vmem,
        scratch_shapes=[pltpu.VMEM(block.shape, F32), pltpu.VMEM(block.shape, F32), pltpu.VMEM((N_CHIPS, r, n), F32),
                        pltpu.SemaphoreType.DMA((2,)), pltpu.SemaphoreType.DMA((2,)),
                        pltpu.SemaphoreType.DMA((3,)), pltpu.SemaphoreType.DMA((3,))],
        compiler_params=pltpu.CompilerParams(vmem_limit_bytes=VMEM_LIMIT_BYTES),
    )(block)


def _cast_place(shards, chip_arr):
    na = len(shards)
    steps = 4

    def body(chip_ref, *refs):
        for a in range(na):
            refs[na + a][0] = refs[a][...].astype(BF16)

    grid_spec = pltpu.PrefetchScalarGridSpec(
        num_scalar_prefetch=1, grid=(steps,),
        in_specs=[pl.BlockSpec((s.shape[0] // steps, s.shape[1]), lambda i, ch: (i, 0)) for s in shards],
        out_specs=[pl.BlockSpec((1, s.shape[0] // steps, s.shape[1]), lambda i, ch: (ch[0], i, 0)) for s in shards])
    return pl.pallas_call(
        body, name="cast_place", grid_spec=grid_spec,
        out_shape=[_sds((N_CHIPS,) + s.shape, BF16) for s in shards],
        compiler_params=_params(("arbitrary",)),
    )(chip_arr, *shards)


def _silu(v):
    return v * _sigmoid(v)


def _ada_fwd(c8, w_ada):
    def body(c_ref, w_ref, o_ref):
        o_ref[...] = jnp.dot(_silu(c_ref[...]), w_ref[...], preferred_element_type=F32,
                             precision=lax.Precision.HIGHEST)

    return pl.pallas_call(
        body, name="ada_fwd", out_shape=_sds((N_DEV, w_ada.shape[1]), F32),
        compiler_params=pltpu.CompilerParams(vmem_limit_bytes=VMEM_LIMIT_BYTES),
    )(c8, w_ada)


def _mod_select(parts, b_ada, me_arr, after):
    cols = parts.shape[2]

    def body(me_ref, p_ref, b_ref, after_ref, o_ref):
        me = me_ref[0]
        for k in range(N_CHIPS):
            cs = slice(k * cols, (k + 1) * cols)
            o_ref[:, cs] = p_ref[2 * k, pl.ds(me, 1), :] + b_ref[:, cs]

    grid_spec = pltpu.PrefetchScalarGridSpec(
        num_scalar_prefetch=1, grid=(1,),
        in_specs=[pl.BlockSpec(parts.shape, lambda i, m: (0, 0, 0)), pl.BlockSpec(b_ada.shape, lambda i, m: (0, 0)),
                  _ANY],
        out_specs=pl.BlockSpec(b_ada.shape, lambda i, m: (0, 0)))
    return pl.pallas_call(body, name="mod_select", grid_spec=grid_spec, out_shape=_sds(b_ada.shape, F32))(
        me_arr, parts, b_ada, after)


def _ada_bwd(c8, dmod8, chip_arr, w, m, v, tr=256):
    d = c8.shape[1]
    cols = dmod8.shape[1] // N_CHIPS

    def body(chip_ref, c_ref, dm_ref, dmall_ref, w_ref, m_ref, v_ref, gw_ref, d_ref, nm_ref, nv_ref, gb_ref):
        g = lax.dot_general(_silu(c_ref[...]), dm_ref[...], (((0,), (0,)), ((), ())),
                            preferred_element_type=F32, precision=lax.Precision.HIGHEST)
        gw_ref[...] = g
        d_ref[...], nm_ref[...], nv_ref[...] = _adam_math(w_ref[...], g, m_ref[...], v_ref[...])
        acc = dmall_ref[0:1, :]
        for k in range(1, N_DEV):
            acc = acc + dmall_ref[k:k + 1, :]
        gb_ref[...] = acc

    rows = pl.BlockSpec((tr, cols), lambda i, ch: (i, 0))
    grid_spec = pltpu.PrefetchScalarGridSpec(
        num_scalar_prefetch=1, grid=(d // tr,),
        in_specs=[pl.BlockSpec((N_DEV, tr), lambda i, ch: (0, i)),
                  pl.BlockSpec((N_DEV, cols), lambda i, ch: (0, ch[0])),
                  pl.BlockSpec(dmod8.shape, lambda i, ch: (0, 0)), rows, rows, rows],
        out_specs=[rows] * 4 + [pl.BlockSpec((1, dmod8.shape[1]), lambda i, ch: (0, 0))])
    return pl.pallas_call(
        body, name="ada_bwd", grid_spec=grid_spec,
        out_shape=[_sds((d, cols), F32)] * 4 + [_sds((1, dmod8.shape[1]), F32)],
        compiler_params=_params(("arbitrary",)),
    )(chip_arr, c8, dmod8, dmod8, w, m, v)


def _adam_math(w, g, m, v):
    m = ADAM_B1 * m + (1.0 - ADAM_B1) * g
    v = ADAM_B2 * v + (1.0 - ADAM_B2) * (g * g)
    m_hat = m / (1.0 - ADAM_B1 ** ADAM_STEP)
    v_hat = v / (1.0 - ADAM_B2 ** ADAM_STEP)
    delta = -ADAM_LR * (m_hat / (jnp.sqrt(v_hat) + ADAM_EPS) + ADAM_WD * w)
    return delta, m, v


def _adam(w, g, m, v, name, tr=256):
    rows, cols = w.shape
    if rows % tr:
        tr = rows

    def body(w_ref, g_ref, m_ref, v_ref, d_ref, nm_ref, nv_ref):
        d_ref[...], nm_ref[...], nv_ref[...] = _adam_math(w_ref[...], g_ref[...], m_ref[...], v_ref[...])

    spec = pl.BlockSpec((tr, cols), lambda i: (i, 0))
    return pl.pallas_call(
        body, name=name, grid=(rows // tr,), in_specs=[spec] * 4, out_specs=[spec] * 3,
        out_shape=[_sds(w.shape, F32)] * 3, compiler_params=_params(("parallel",)),
    )(w, g, m, v)


SMALL_REPLICATED = ("g_mix_pre", "g_mix_post", "conv_b", "w_rgate", "b_rgate", "w_igate", "b_igate", "lru_a",
                    "v_norm_g", "v_norm_b", "w_spatial", "b_spatial", "g_lru_out", "g_gmlp_out", "g_ffn_pre",
                    "g_ffn_post", "ffn_conv_b")
SMALL_COLUMN_SHARDED = ("conv_w", "ffn_conv_w")

SMALL_ROW_LEN = 86016
_SMALL_ROWS = (
    (("ffn_conv_w", 18432), ("conv_w", 2048), ("w_spatial", 65536)),
    (("w_rgate", 32768), ("w_igate", 32768), ("ffn_conv_b", 6144), ("g_mix_pre", 1024), ("g_mix_post", 1024),
     ("g_ffn_pre", 1024), ("g_ffn_post", 1024), ("conv_b", 512), ("b_rgate", 512), ("b_igate", 512),
     ("lru_a", 512), ("v_norm_g", 512), ("v_norm_b", 512), ("b_spatial", 512), ("g_lru_out", 512),
     ("g_gmlp_out", 512), ("loss", 128)),
)


def _small_slots():
    slots = {}
    for row, entries in enumerate(_SMALL_ROWS):
        off = 0
        for name, size in entries:
            slots[name] = (row, off)
            off += size
        assert off <= SMALL_ROW_LEN
    return slots


SMALL_SLOT = _small_slots()
SMALL_LANES = SMALL_ROW_LEN // SUBLANES


def _small_pieces(name, first, count):
    row, off = SMALL_SLOT[name]
    pos, pieces = off + first, []
    while count:
        sub, lane = divmod(pos, SMALL_LANES)
        n = min(count, SMALL_LANES - lane)
        pieces.append((row, sub, lane, n))
        pos, count = pos + n, count - n
    return pieces
ROW_VECTORS = ("ffn_conv_b", "g_mix_pre", "g_mix_post", "g_ffn_pre", "g_ffn_post", "conv_b", "lru_a", "v_norm_g",
               "v_norm_b", "g_lru_out", "g_gmlp_out")
HEAD_DIM = LRU_WIDTH // LRU_HEADS


def _pack_small(g, after):
    order = ("ffn_conv_w", "conv_w", "w_spatial", "w_rgate", "w_igate", "b_rgate", "b_igate", "b_spatial", "loss") \
        + ROW_VECTORS
    vmem = pl.BlockSpec(memory_space=pltpu.VMEM)

    def body(*refs):
        src = dict(zip(order, refs))
        out_ref = refs[len(order) + 1]
        out_ref[...] = jnp.zeros_like(out_ref)

        def put(name, first, val):
            col = 0
            for row, sub, lane, n in _small_pieces(name, first, val.shape[1]):
                out_ref[row, sub:sub + 1, lane:lane + n] = val[:, col:col + n]
                col += n

        for name in ROW_VECTORS + ("b_rgate", "b_igate", "loss"):
            put(name, 0, src[name][...])
        for name in ("ffn_conv_w", "conv_w"):
            k_taps, n = src[name].shape
            for k in range(k_taps):
                put(name, k * n, src[name][k:k + 1, :])
        for g_idx in range(GMLP_GROUPS):
            for i in range(GMLP_BLOCK):
                put("w_spatial", (g_idx * GMLP_BLOCK + i) * GMLP_BLOCK, src["w_spatial"][g_idx, i:i + 1, :])
        for name in ("w_rgate", "w_igate"):
            for h in range(LRU_HEADS):
                for i in range(HEAD_DIM):
                    r = h * HEAD_DIM + i
                    put(name, r * HEAD_DIM, src[name][r:r + 1, h * HEAD_DIM:(h + 1) * HEAD_DIM])
        eye = (lax.broadcasted_iota(jnp.int32, (GMLP_BLOCK, GMLP_BLOCK), 0)
               == lax.broadcasted_iota(jnp.int32, (GMLP_BLOCK, GMLP_BLOCK), 1))
        for g_idx in range(GMLP_GROUPS):
            col = src["b_spatial"][:, g_idx:g_idx + 1]
            put("b_spatial", g_idx * GMLP_BLOCK, _colsum(jnp.where(eye, col, 0.0)))

    return pl.pallas_call(
        body, name="pack_small", out_shape=_sds((2, SUBLANES, SMALL_LANES), F32),
        in_specs=[vmem] * len(order) + [_ANY], out_specs=vmem,
        compiler_params=pltpu.CompilerParams(vmem_limit_bytes=VMEM_LIMIT_BYTES),
    )(*[g[n] for n in order], after)


def _adam_small(g_small, w, m, v):
    vmem = pl.BlockSpec(memory_space=pltpu.VMEM)
    n_p = len(SMALL_REPLICATED)

    def body(g_ref, *refs):
        w_refs, m_refs, v_refs = refs[:n_p], refs[n_p:2 * n_p], refs[2 * n_p:3 * n_p]
        outs = refs[3 * n_p:]
        go, do, mo, vo = outs[:n_p], outs[n_p:2 * n_p], outs[2 * n_p:3 * n_p], outs[3 * n_p:]
        for k, name in enumerate(SMALL_REPLICATED):
            def take(first, count, name=name):
                parts = [g_ref[row, sub:sub + 1, lane:lane + n]
                         for row, sub, lane, n in _small_pieces(name, first, count)]
                return parts[0] if len(parts) == 1 else jnp.concatenate(parts, axis=1)

            shape = w_refs[k].shape
            if name in ROW_VECTORS:
                go[k][...] = take(0, shape[1])
            elif name in ("b_rgate", "b_igate"):
                for h in range(LRU_HEADS):
                    go[k][0, h:h + 1, :] = take(h * HEAD_DIM, HEAD_DIM)
            elif name == "b_spatial":
                for g_idx in range(GMLP_GROUPS):
                    go[k][0, g_idx:g_idx + 1, :] = take(g_idx * GMLP_BLOCK, GMLP_BLOCK)
            elif name == "w_spatial":
                for g_idx in range(GMLP_GROUPS):
                    for i in range(GMLP_BLOCK):
                        go[k][0, g_idx, i:i + 1, :] = take((g_idx * GMLP_BLOCK + i) * GMLP_BLOCK, GMLP_BLOCK)
            else:
                for h in range(LRU_HEADS):
                    for i in range(HEAD_DIM):
                        go[k][0, h, i:i + 1, :] = take((h * HEAD_DIM + i) * HEAD_DIM, HEAD_DIM)
            do[k][...], mo[k][...], vo[k][...] = _adam_math(w_refs[k][...], go[k][...], m_refs[k][...],
                                                             v_refs[k][...])

    names = SMALL_REPLICATED
    out_shape = [_sds(w[n].shape, F32) for n in names] * 4
    res = pl.pallas_call(
        body, name="adam_small", out_shape=out_shape,
        in_specs=[vmem] * (1 + 3 * n_p), out_specs=[vmem] * (4 * n_p),
        compiler_params=pltpu.CompilerParams(vmem_limit_bytes=VMEM_LIMIT_BYTES),
    )(g_small, *[w[n] for n in names], *[m[n] for n in names], *[v[n] for n in names])
    return [dict(zip(names, res[k * n_p:(k + 1) * n_p])) for k in range(4)]


def _adam_cols(name, g_small, w, m, v, chip_arr):
    _, k_taps, n = w.shape
    row, off = SMALL_SLOT[name]
    first = off // n
    per_sub = SMALL_LANES // n

    def body(chip_ref, *refs):
        g_refs = refs[:k_taps]
        w_ref, m_ref, v_ref, go_ref, d_ref, nm_ref, nv_ref = refs[k_taps:]
        for k in range(k_taps):
            tap = (0, slice(k, k + 1), slice(None))
            sub = (first + N_CHIPS * k + chip_ref[0]) // per_sub
            g = g_refs[k][row, pl.ds(sub, 1), :]
            go_ref[tap] = g
            d_ref[tap], nm_ref[tap], nv_ref[tap] = _adam_math(w_ref[tap], g, m_ref[tap], v_ref[tap])

    whole = pl.BlockSpec(w.shape, lambda i, ch: (0, 0, 0))
    taps = [pl.BlockSpec((2, SUBLANES, n),
                         functools.partial(lambda i, ch, k: (0, 0, (first + N_CHIPS * k + ch[0]) % per_sub), k=k))
            for k in range(k_taps)]
    grid_spec = pltpu.PrefetchScalarGridSpec(
        num_scalar_prefetch=1, grid=(1,), in_specs=taps + [whole] * 3, out_specs=[whole] * 4)
    return pl.pallas_call(body, name="adam_" + name, grid_spec=grid_spec, out_shape=[_sds(w.shape, F32)] * 4)(
        chip_arr, *[g_small] * k_taps, w, m, v)


def kernel(x, c, w_ada, b_ada, g_mix_pre, g_mix_post, w_in, conv_w, conv_b, w_rgate, b_rgate, w_igate, b_igate, lru_a, v_norm_g, v_norm_b, w_spatial, b_spatial, g_lru_out, g_gmlp_out, w_out, g_ffn_pre, g_ffn_post, w_up, ffn_conv_w, ffn_conv_b, w_down, loss_target, m_w_ada, m_b_ada, m_g_mix_pre, m_g_mix_post, m_w_in, m_conv_w, m_conv_b, m_w_rgate, m_b_rgate, m_w_igate, m_b_igate, m_lru_a, m_v_norm_g, m_v_norm_b, m_w_spatial, m_b_spatial, m_g_lru_out, m_g_gmlp_out, m_w_out, m_g_ffn_pre, m_g_ffn_post, m_w_up, m_ffn_conv_w, m_ffn_conv_b, m_w_down, v_w_ada, v_b_ada, v_g_mix_pre, v_g_mix_post, v_w_in, v_conv_w, v_conv_b, v_w_rgate, v_b_rgate, v_w_igate, v_b_igate, v_lru_a, v_v_norm_g, v_v_norm_b, v_w_spatial, v_b_spatial, v_g_lru_out, v_g_gmlp_out, v_w_out, v_g_ffn_pre, v_g_ffn_post, v_w_up, v_ffn_conv_w, v_ffn_conv_b, v_w_down):
    args = dict(locals())
    names = ("w_ada", "b_ada", "g_mix_pre", "g_mix_post", "w_in", "conv_w", "conv_b", "w_rgate", "b_rgate",
             "w_igate", "b_igate", "lru_a", "v_norm_g", "v_norm_b", "w_spatial", "b_spatial", "g_lru_out",
             "g_gmlp_out", "w_out", "g_ffn_pre", "g_ffn_post", "w_up", "ffn_conv_w", "ffn_conv_b", "w_down")
    drop = lambda a: a if a.ndim == 2 else a[0]
    w = {n: drop(args[n]) for n in names}
    m = {n: drop(args["m_" + n]) for n in names}
    v = {n: drop(args["v_" + n]) for n in names}
    xi, yi, ci = _position()
    me_arr = jnp.reshape(4 * xi + 2 * yi + ci, (1,)).astype(jnp.int32)
    chip_arr = jnp.reshape(2 * xi + yi, (1,)).astype(jnp.int32)
    c_arr = jnp.reshape(ci, (1,)).astype(jnp.int32)
    pos_arr = jnp.stack([ci, 2 * xi + yi]).astype(jnp.int32)

    big = ("w_in", "w_out", "w_up", "w_down")
    lands = _cast_place([w[n] for n in big], chip_arr)
    start_a, wait_a = _gather_plan([w[n].shape[0] for n in big[:2]])
    start_b, wait_b = _gather_plan([w[n].shape[0] for n in big[2:]])

    row0 = jnp.concatenate([c, w["conv_w"].reshape(1, -1), w["ffn_conv_w"].reshape(1, -1)], axis=1)
    g0 = _allgather8(row0, "gather_cond")[:, 0, :]
    c8 = g0[:, :D_MODEL]
    per_chip = g0[0::2]
    conv_w_full = per_chip[:, D_MODEL:D_MODEL + 512].reshape(N_CHIPS, 4, 128).transpose(1, 0, 2).reshape(4, 512)
    ffn_conv_w_full = per_chip[:, D_MODEL + 512:].reshape(N_CHIPS, 3, 1536).transpose(1, 0, 2).reshape(3, 2 * D_FF)
    mod_parts = _allgather8(_ada_fwd(c8, w["w_ada"]), "gather_mod")
    send_a, recv_a, _, lands_a, token_a = _split_start([], lands[:2], start_a, 6, mod_parts, "gather_start_a")
    send_b, recv_b, _, lands_b, token_b = _split_start([], lands[2:], start_b, 6, token_a, "gather_start_b")
    mod = _mod_select(mod_parts, w["b_ada"].reshape(1, -1), me_arr, token_b).reshape(N_MOD, D_MODEL)
    sh_m, sc_m, gt_m, sh_f, sc_f, gt_f = [mod[k:k + 1] for k in range(N_MOD)]

    small = {n: w[n] for n in SMALL_REPLICATED}
    small["conv_w"] = conv_w_full
    small["ffn_conv_w"] = ffn_conv_w_full
    row = lambda a: a.reshape(1, -1)
    seq_params, ws_t = _seq_params(small)
    glo, ggo = row(small["g_lru_out"]), row(small["g_gmlp_out"])
    g_pre, g_post = row(small["g_mix_pre"]), row(small["g_mix_post"])
    g_pre2, g_post2 = row(small["g_ffn_pre"]), row(small["g_ffn_post"])
    fw, fb = small["ffn_conv_w"], row(small["ffn_conv_b"])
    xs, tgt = x[0], loss_target[0]

    _, lands_a = _split_wait(send_a, recv_a, [], lands_a, wait_a, mod, "gather_wait_a")
    w_in4, w_out4 = _forward_to_sibling(lands_a, "forward_a")
    w_out_b = w_out4.reshape(D_MODEL, D_MODEL)
    z, h = _mix_in(xs, sc_m, sh_m, g_pre, w_in4)
    ycat, hst, stash = _seqmix(z, seq_params, glo, ggo)
    _, lands_b = _split_wait(send_b, recv_b, [], lands_b, wait_b, ycat, "gather_wait_b")
    fwd_start, fwd_wait = _forward_plan([w[n].shape[0] for n in big[2:]])
    fwd_send, fwd_recv, _, lands_b, tok = _split_start([], lands_b, fwd_start, 6, pos_arr, "forward_start_b")
    y, x1, h2 = _mix_out(ycat, xs, w_out_b, gt_m + tok[0:1, 0:1], g_post, g_pre2, sc_f, sh_f)
    _, (w_up4, w_down4) = _split_wait(fwd_send, fwd_recv, [], lands_b, fwd_wait, h2, "forward_wait_b")
    w_down_b = w_down4.reshape(D_FF, D_MODEL)
    up0, pre, act, dy2, dx2, loss, dgt_f, dg_post2 = _ffn_fwd(h2, x1, tgt, w_up4, w_down_b, fw, fb, gt_f, g_post2)

    dup0, dfw, dfb = _ffn_bwd_a(dy2, pre, up0, w_down_b, fw)
    gw_up = _wgrad(h2, dup0, N_CHIPS, "wgrad_up", True)
    gw_down = _wgrad(act, dy2, 2, "wgrad_down", False)
    ex_start, ex_wait = _exchange_plan(2)
    sg_start, sg_wait = _swap_gathered_plan(2)
    grads, deltas, new_m, new_v = {}, {}, {}, {}

    def swap_start(parts, name):
        sw_start, sw_wait = _swap_halves_plan([p.shape[1] // 2 for p in parts])
        recv = [lax.empty((N_CHIPS, p.shape[1] // 2, p.shape[2]), BF16) for p in parts]
        send_s, recv_s, parts, recv, token = _split_start(parts, recv, sw_start, len(parts), pos_arr,
                                                           "swap_start_" + name)
        return (send_s, recv_s, parts, recv, sw_wait), token

    def exchange_start(swap, tags, after, name):
        send_s, recv_s, parts, recv, sw_wait = swap
        parts, recv = _split_wait(send_s, recv_s, parts, recv, sw_wait, after, "swap_wait_" + name)
        both = [_chip_sum(p, r, pos_arr, "chip_sum_" + t) for p, r, t in zip(parts, recv, tags)]
        sums, gath = [b[0] for b in both], [b[1] for b in both]
        return _split_start(sums, gath, ex_start, 3 * len(parts), pos_arr, "exchange_start_" + name)

    def gathered_start(exchange, after, name):
        send_s, recv_s, sums, gath, _ = exchange
        _, gath = _split_wait(send_s, recv_s, sums, gath, ex_wait, after, "exchange_wait_" + name)
        send_s, recv_s, _, gath, token = _split_start([], gath, sg_start, len(gath), pos_arr,
                                                      "gathered_start_" + name)
        return (send_s, recv_s, gath), token

    def gathered_wait(gathered, after, name):
        send_s, recv_s, gath = gathered
        return _split_wait(send_s, recv_s, [], gath, sg_wait, after, "gathered_wait_" + name)[1]

    def adam_big(t, gath, after):
        grads[t], deltas[t], new_m[t], new_v[t] = _adam_gathered(w[t], gath, m[t], v[t], c_arr, after, "adam_" + t)

    def behind(value, token):
        return value + token[0:1, 0:1]

    tags_b, tags_a = ("w_up", "w_down"), ("w_in", "w_out")
    swap_b, tok = swap_start([gw_up, gw_down.reshape(N_CHIPS, -1, D_MODEL)], "b")
    dx1, dy, dsh_f, dsc_f, dg_pre2, dgt_m, dg_post = _ffn_bwd_b(
        dup0, x1, y, dx2, w_up4, g_pre2, behind(sc_f, tok), sh_f, gt_m, g_post)
    exchange_b = exchange_start(swap_b, tags_b, dg_post, "b")
    (dz, dcw, dcb, dwr, dwi, dbr, dbi, dspa, dng, dnb, dws, dbs_t, dglo, dggo) = _seqmix_bwd(
        z, hst, stash, dy, w_out_b, seq_params, ws_t, behind(glo, exchange_b[4]), ggo)
    gw_in = _wgrad(h, dz, N_CHIPS, "wgrad_in", True)
    gw_out = _wgrad(ycat, dy, 1, "wgrad_out", False)
    swap_a, tok = swap_start([gw_in, gw_out.reshape(N_CHIPS, -1, D_MODEL)], "a")
    grad_x, dsh_m, dsc_m, dg_pre = _mix_in_bwd(xs, dz, dx1, w_in4, g_pre, behind(sc_m, tok))

    dmod = jnp.concatenate([behind(dsh_m, tok), dsc_m, dgt_m, dsh_f, dsc_f, dgt_f], axis=1)
    dmod8 = _allgather8(dmod, "gather_dmod")[:, 0, :]
    small_grads = dict(
        g_mix_pre=dg_pre, g_mix_post=dg_post, conv_w=dcw, conv_b=dcb, w_rgate=dwr, b_rgate=dbr, w_igate=dwi,
        b_igate=dbi, lru_a=dspa, v_norm_g=dng, v_norm_b=dnb, w_spatial=dws, b_spatial=dbs_t, g_lru_out=dglo,
        g_gmlp_out=dggo, g_ffn_pre=dg_pre2, g_ffn_post=dg_post2, ffn_conv_w=dfw, ffn_conv_b=dfb,
        loss=loss)
    g_small = _allreduce_small(_pack_small(small_grads, dmod8), "reduce_small")
    total = g_small[_small_pieces("loss", 0, 1)[0][:3]]
    exchange_a = exchange_start(swap_a, tags_a, g_small, "a")
    gathered_b, tok = gathered_start(exchange_b, exchange_a[4], "b")

    grads["w_ada"], deltas["w_ada"], new_m["w_ada"], new_v["w_ada"], g_b_ada = _ada_bwd(
        c8, behind(dmod8, tok), chip_arr, w["w_ada"], m["w_ada"], v["w_ada"])
    rep = SMALL_REPLICATED
    small_out = _adam_small(g_small, {n: args[n] for n in rep}, {n: args["m_" + n] for n in rep},
                            {n: args["v_" + n] for n in rep})
    for n in rep:
        grads[n], deltas[n], new_m[n], new_v[n] = [group[n] for group in small_out]
    for n in SMALL_COLUMN_SHARDED:
        grads[n], deltas[n], new_m[n], new_v[n] = _adam_cols(n, g_small, args[n], args["m_" + n],
                                                             args["v_" + n], chip_arr)
    d_b, m_b, v_b = _adam(w["b_ada"], g_b_ada, m["b_ada"], v["b_ada"], "adam_b_ada")
    grads["b_ada"], deltas["b_ada"], new_m["b_ada"], new_v["b_ada"] = g_b_ada, d_b, m_b, v_b

    gath_up, gath_down = gathered_wait(gathered_b, d_b, "b")
    adam_big("w_down", gath_down, pos_arr)
    gathered_a, tok = gathered_start(exchange_a, deltas["w_down"], "a")
    adam_big("w_up", gath_up, tok)
    gath_in, gath_out = gathered_wait(gathered_a, deltas["w_up"], "a")
    adam_big("w_in", gath_in, pos_arr)
    adam_big("w_out", gath_out, pos_arr)

    outs = [total, grad_x[None]]
    for group in (grads, deltas, new_m, new_v):
        outs.extend(group[n].reshape(args[n].shape) for n in names)
    return tuple(outs)
```

```python
import functools
import math

import jax
import jax.numpy as jnp
from jax import lax
from jax.experimental import pallas as pl
from jax.experimental.pallas import tpu as pltpu

F32 = jnp.float32
BF16 = jnp.bfloat16
MESH = pl.DeviceIdType.MESH

D_MODEL = 1024
LRU_WIDTH = 512
LRU_HEADS = 8
GMLP_GROUPS = 4
GMLP_BLOCK = 128
CHUNK = 64
D_FF = 3072
N_MOD = 6
EPS = 1e-6
LRU_C = 8.0
N_CHIPS = 4
N_DEV = 8

ADAM_LR = 0.001
ADAM_B1 = 0.9
ADAM_B2 = 0.999
ADAM_EPS = 1e-08
ADAM_WD = 0.01
ADAM_STEP = 10

GELU_C0 = math.sqrt(2.0 / math.pi)
GELU_C1 = 0.044715

VMEM_LIMIT_BYTES = 56 * 1024 * 1024
SUBLANES = 8
BF16_SUBLANES = 16
FFN_CHUNK = 768
SUB_ROWS = 256


def _gelu_gate(x):
    x2 = x * x
    z = x * ((2.0 * GELU_C0 * GELU_C1) * x2 + 2.0 * GELU_C0)
    return 1.0 / (1.0 + jnp.exp(-z)), x2


def _gelu(x):
    t = jnp.tanh(GELU_C0 * (x + GELU_C1 * x * x * x))
    return 0.5 * x * (1.0 + t)


def _gelu_and_grad(x):
    s, x2 = _gelu_gate(x)
    g = x * s
    dz = (6.0 * GELU_C0 * GELU_C1) * x2 + 2.0 * GELU_C0
    return g, s + g * (1.0 - s) * dz


def _sigmoid(x):
    return 1.0 / (1.0 + jnp.exp(-x))


def _log1p(u):
    w = 1.0 + u
    return jnp.where(w == 1.0, u, jnp.log(w) * (u / (w - 1.0)))


def _softplus(x):
    return jnp.maximum(x, 0.0) + _log1p(jnp.exp(-jnp.abs(x)))


def _neg_expm1(x):
    u = jnp.exp(x)
    um1 = u - 1.0
    tiny = um1 == 0.0
    small = um1 * (x / jnp.log(jnp.where(tiny, 2.0, jnp.maximum(u, 0.25))))
    return -jnp.where(tiny, x, jnp.where(x < -1.0, um1, small))


def _msq_rsqrt(v):
    return lax.rsqrt(jnp.mean(v * v, axis=-1, keepdims=True) + EPS)


def _rms_bwd(dyn, yn, r):
    return r * (dyn - yn * jnp.mean(dyn * yn, axis=-1, keepdims=True))


def _colsum(v):
    return jnp.sum(v, axis=0, keepdims=True)


def _shift_down(cur, prev8, k):
    rolled = pltpu.roll(cur, k, 0)
    head = pltpu.roll(prev8, k, 0)
    row8 = lax.broadcasted_iota(jnp.int32, (SUBLANES, cur.shape[1]), 0)
    first = jnp.where(row8 < k, head, rolled[0:SUBLANES])
    return jnp.concatenate([first, rolled[SUBLANES:]], axis=0)


def _shift_up(cur, next8, k):
    t = cur.shape[0]
    rolled = pltpu.roll(cur, t - k, 0)
    tail = pltpu.roll(next8, SUBLANES - k, 0)
    row8 = lax.broadcasted_iota(jnp.int32, (SUBLANES, cur.shape[1]), 0)
    last = jnp.where(row8 >= SUBLANES - k, tail, rolled[t - SUBLANES:])
    return jnp.concatenate([rolled[:t - SUBLANES], last], axis=0)


def _scan_fwd(a, b):
    t = a.shape[0]
    row = lax.broadcasted_iota(jnp.int32, a.shape, 0)
    d = 1
    while d < t:
        keep = row >= d
        a_s = jnp.where(keep, pltpu.roll(a, d, 0), 1.0)
        b_s = jnp.where(keep, pltpu.roll(b, d, 0), 0.0)
        b = a * b_s + b
        a = a * a_s
        d *= 2
    return a, b


def _scan_bwd(a, g):
    t = a.shape[0]
    row = lax.broadcasted_iota(jnp.int32, a.shape, 0)
    d = 1
    while d < t:
        keep = row < t - d
        a_s = jnp.where(keep, pltpu.roll(a, t - d, 0), 1.0)
        g_s = jnp.where(keep, pltpu.roll(g, t - d, 0), 0.0)
        g = a * g_s + g
        a = a * a_s
        d *= 2
    return a, g


def _dot(a, b):
    return jnp.dot(a, b, preferred_element_type=F32)


def _dot_nt(a, b):
    return lax.dot_general(a, b, (((1,), (1,)), ((), ())), preferred_element_type=F32)


def _dot_tn(a, b):
    return lax.dot_general(a, b, (((0,), (0,)), ((), ())), preferred_element_type=F32)


def _rows(ts, cols, rev_of=None):
    if rev_of is None:
        return pl.BlockSpec((ts, cols), lambda i: (i, 0))
    return pl.BlockSpec((ts, cols), lambda i: (rev_of - 1 - i, 0))


def _halo_prev(ts, cols, halo, rev_of=None, col_block=0):
    per = ts // halo
    if rev_of is None:
        return pl.BlockSpec((halo, cols), lambda i: (jnp.maximum(i * per - 1, 0), col_block))
    return pl.BlockSpec((halo, cols), lambda i: (jnp.maximum((rev_of - 1 - i) * per - 1, 0), col_block))


def _full(shape):
    nd = len(shape)
    return pl.BlockSpec(shape, lambda *_: (0,) * nd)


_RESIDENT = pl.BlockSpec(memory_space=pltpu.VMEM)


def _params(sem):
    return pltpu.CompilerParams(dimension_semantics=sem, vmem_limit_bytes=VMEM_LIMIT_BYTES)


def _sds(shape, dtype):
    return jax.ShapeDtypeStruct(shape, dtype)


def _sub_tiles(ts):
    return [slice(r0, r0 + SUB_ROWS) for r0 in range(0, ts, SUB_ROWS)]


def _mix_in(x, sc, sh, g, w_in4, ts=512):
    s, d = x.shape

    def body(x_ref, sc_ref, sh_ref, g_ref, w_ref, z_ref, h_ref):
        for rs in _sub_tiles(ts):
            xv = x_ref[rs, :]
            h = (xv * _msq_rsqrt(xv) * g_ref[...]) * (1.0 + sc_ref[...]) + sh_ref[...]
            hb = h.astype(BF16)
            h_ref[rs, :] = hb
            for k in range(N_CHIPS):
                z_ref[rs, k * 512:(k + 1) * 512] = _dot(hb, w_ref[k])

    return pl.pallas_call(
        body, grid=(s // ts,), name="mix_in",
        in_specs=[_rows(ts, d), _full((1, d)), _full((1, d)), _full((1, d)), _full(w_in4.shape)],
        out_specs=[_rows(ts, 2048), _rows(ts, d)],
        out_shape=[_sds((s, 2048), F32), _sds((s, d), BF16)],
        compiler_params=_params(("parallel",)),
    )(x, sc, sh, g, w_in4)


N_STASH = 12
(ST_XC, ST_R, ST_IG, ST_A, ST_MULT, ST_GL, ST_DGL, ST_U, ST_DU, ST_Q, ST_VHAT, ST_SPB) = range(N_STASH)


def _seq_param_specs():
    return [_full((4, 512)), _full((1, 512)), _full((512, 512)), _full((512, 512)), _full((1, 512)),
            _full((1, 512)), _full((1, 512)), _full((1, 512)), _full((1, 512)), _full((4, 128, 128)),
            _full((128, 4))]


def _seqmix(z, seq_params, glo, ggo, ts=256):
    s = z.shape[0]
    nt = s // ts

    def body(z_ref, zprev_ref, cw_ref, cb_ref, bdr_ref, bdi_ref, br_ref, bi_ref, la_ref, ng_ref, nb_ref,
             ws_ref, bst_ref, glo_ref, ggo_ref, ycat_ref, hst_ref, st_ref, hcarry, sp_scr):
        i = pl.program_id(0)

        @pl.when(i == 0)
        def _():
            hcarry[...] = jnp.zeros_like(hcarry)

        lx = z_ref[:, 0:512]
        prev8 = jnp.where(i == 0, 0.0, zprev_ref[...])
        xc = (cw_ref[3:4, :] * lx + cw_ref[2:3, :] * _shift_down(lx, prev8, 1)
              + cw_ref[1:2, :] * _shift_down(lx, prev8, 2) + cw_ref[0:1, :] * _shift_down(lx, prev8, 3)
              + cb_ref[...])
        xcb = xc.astype(BF16)
        r = _sigmoid(_dot(xcb, bdr_ref[...]) + br_ref[...])
        ig = _sigmoid(_dot(xcb, bdi_ref[...]) + bi_ref[...])
        log_a = (-LRU_C) * r * _softplus(-la_ref[...])
        a = jnp.exp(log_a)
        mult = jnp.sqrt(_neg_expm1(2.0 * log_a))
        acum, hloc = _scan_fwd(a, mult * (ig * xc))
        h = hloc + acum * hcarry[...]
        hcarry[...] = h[ts - 1:ts, :]
        hst_ref[...] = h
        gl, dgl = _gelu_and_grad(z_ref[:, 512:1024])
        y_l = h * gl
        for slot, val in ((ST_XC, xc), (ST_R, r), (ST_IG, ig), (ST_A, a), (ST_MULT, mult), (ST_GL, gl),
                          (ST_DGL, dgl)):
            st_ref[slot] = val

        u, du = _gelu_and_grad(z_ref[:, 1024:1536])
        vg, dvg = _gelu_and_grad(z_ref[:, 1536:2048])
        vc = vg - jnp.mean(vg, axis=-1, keepdims=True)
        rstd = lax.rsqrt(jnp.mean(vc * vc, axis=-1, keepdims=True) + EPS)
        vhat = vc * rstd
        vb = (vhat * ng_ref[...] + nb_ref[...]).astype(BF16)
        for n in range(ts // GMLP_BLOCK):
            rs = slice(n * GMLP_BLOCK, (n + 1) * GMLP_BLOCK)
            for g in range(GMLP_GROUPS):
                cs = slice(g * 128, (g + 1) * 128)
                sp_scr[rs, cs] = _dot(ws_ref[g], vb[rs, cs]) + bst_ref[:, g:g + 1]
        spb = sp_scr[...]
        y_g = u * spb
        for slot, val in ((ST_U, u), (ST_DU, du), (ST_Q, rstd * dvg), (ST_VHAT, vhat), (ST_SPB, spb)):
            st_ref[slot] = val

        ycat_ref[:, 0:512] = (y_l * _msq_rsqrt(y_l) * glo_ref[...]).astype(BF16)
        ycat_ref[:, 512:1024] = (y_g * _msq_rsqrt(y_g) * ggo_ref[...]).astype(BF16)

    return pl.pallas_call(
        body, grid=(nt,), name="seqmix",
        in_specs=[_rows(ts, 2048), _halo_prev(ts, 512, SUBLANES)] + _seq_param_specs()
        + [_full((1, 512)), _full((1, 512))],
        out_specs=[_rows(ts, 1024), _rows(ts, 512), pl.BlockSpec((N_STASH, ts, 512), lambda i: (0, i, 0))],
        out_shape=[_sds((s, 1024), BF16), _sds((s, 512), F32), _sds((N_STASH, s, 512), F32)],
        scratch_shapes=[pltpu.VMEM((1, 512), F32), pltpu.VMEM((ts, 512), F32)],
        compiler_params=_params(("arbitrary",)),
    )(z, z, *seq_params, glo, ggo)


def _mix_out(ycat, x, w_out, gt_m, g_post, g_pre2, sc_f, sh_f, ts=512):
    s, d = x.shape

    def body(yc_ref, x_ref, w_ref, gt_ref, gp_ref, g2_ref, sc_ref, sh_ref, y_ref, x1_ref, h2_ref):
        for rs in _sub_tiles(ts):
            y = _dot(yc_ref[rs, :], w_ref[...])
            y_ref[rs, :] = y
            x1 = x_ref[rs, :] + gt_ref[...] * (y * _msq_rsqrt(y) * gp_ref[...])
            x1_ref[rs, :] = x1
            h2 = (x1 * _msq_rsqrt(x1) * g2_ref[...]) * (1.0 + sc_ref[...]) + sh_ref[...]
            h2_ref[rs, :] = h2.astype(BF16)

    vec = _full((1, d))
    return pl.pallas_call(
        body, grid=(s // ts,), name="mix_out",
        in_specs=[_rows(ts, d), _rows(ts, d), _full((d, d)), vec, vec, vec, vec, vec],
        out_specs=[_rows(ts, d), _rows(ts, d), _rows(ts, d)],
        out_shape=[_sds((s, d), F32), _sds((s, d), F32), _sds((s, d), BF16)],
        compiler_params=_params(("parallel",)),
    )(ycat, x, w_out, gt_m, g_post, g_pre2, sc_f, sh_f)


def _ffn_cols(j):
    per = (2 * D_FF // N_CHIPS) // FFN_CHUNK
    return j // per, (j % per) * FFN_CHUNK, j * FFN_CHUNK


def _ffn_fwd(h2, x1, tgt, w_up4, w_down, fw, fb, gt_f, g_post, ts=256):
    s, d = x1.shape
    nch = D_FF // FFN_CHUNK

    def body(h2_ref, x1_ref, tgt_ref, wup_ref, wdn_ref, fw_ref, fb_ref, gt_ref, gp_ref,
             up0_ref, pre_ref, act_ref, dy2_ref, dx2_ref, loss_ref, dgt_ref, dgp_ref, tail_ref):
        i = pl.program_id(0)

        @pl.when(i == 0)
        def _():
            tail_ref[...] = jnp.zeros_like(tail_ref)
            loss_ref[...] = jnp.zeros_like(loss_ref)
            dgt_ref[...] = jnp.zeros_like(dgt_ref)
            dgp_ref[...] = jnp.zeros_like(dgp_ref)

        hb = h2_ref[...]

        def up_project(j):
            sh_g, off, _ = _ffn_cols(j)
            return [_dot(hb, wup_ref[shard, :, off:off + FFN_CHUNK]).astype(BF16) for shard in (sh_g, sh_g + 2)]

        y2 = jnp.zeros((ts, d), F32)
        ahead = up_project(0)
        for j in range(nch):
            _, _, col = _ffn_cols(j)
            ubs = ahead
            if j + 1 < nch:
                ahead = up_project(j + 1)
            halves = []
            for ub, c0 in zip(ubs, (col, D_FF + col)):
                cs = slice(c0, c0 + FFN_CHUNK)
                up0_ref[:, cs] = ub
                u = ub.astype(F32)
                prev8 = tail_ref[:, cs]
                tail_ref[:, cs] = u[ts - SUBLANES:, :]
                halves.append(fw_ref[2:3, cs] * u + fw_ref[1:2, cs] * _shift_down(u, prev8, 1)
                              + fw_ref[0:1, cs] * _shift_down(u, prev8, 2) + fb_ref[:, cs])
                pre_ref[:, cs] = halves[-1].astype(BF16)
            act = (_gelu(halves[0]) * halves[1]).astype(BF16)
            act_ref[:, col:col + FFN_CHUNK] = act
            y2 = y2 + _dot(act, wdn_ref[col:col + FFN_CHUNK, :])
        r2 = _msq_rsqrt(y2)
        yn = y2 * r2
        yng = yn * gp_ref[...]
        e = x1_ref[...] + gt_ref[...] * yng - tgt_ref[...]
        loss_ref[...] += jnp.sum(e * e) * (0.5 / d)
        dx2 = e * (1.0 / d)
        dx2_ref[...] = dx2
        dgt_ref[...] += _colsum(dx2 * yng)
        dyng = dx2 * gt_ref[...]
        dgp_ref[...] += _colsum(dyng * yn)
        dy2_ref[...] = _rms_bwd(dyng * gp_ref[...], yn, r2).astype(BF16)

    vec = _full((1, d))
    return pl.pallas_call(
        body, grid=(s // ts,), name="ffn_fwd",
        in_specs=[_rows(ts, d), _rows(ts, d), _rows(ts, d), _RESIDENT, _RESIDENT,
                  _full((3, 2 * D_FF)), _full((1, 2 * D_FF)), vec, vec],
        out_specs=[_rows(ts, 2 * D_FF), _rows(ts, 2 * D_FF), _rows(ts, D_FF), _rows(ts, d), _rows(ts, d),
                   _full((1, 128)), vec, vec],
        out_shape=[_sds((s, 2 * D_FF), BF16), _sds((s, 2 * D_FF), BF16), _sds((s, D_FF), BF16), _sds((s, d), BF16),
                   _sds((s, d), F32), _sds((1, 128), F32), _sds((1, d), F32), _sds((1, d), F32)],
        scratch_shapes=[pltpu.VMEM((SUBLANES, 2 * D_FF), F32)],
        compiler_params=_params(("arbitrary",)),
    )(h2, x1, tgt, w_up4, w_down, fw, fb, gt_f, g_post)


def _shift_up_mxu(vb, up_mat, next8, k):
    t = vb.shape[0]
    main = _dot(up_mat, vb)
    tail = pltpu.roll(next8, SUBLANES - k, 0)
    row8 = lax.broadcasted_iota(jnp.int32, next8.shape, 0)
    last = main[t - SUBLANES:] + jnp.where(row8 >= SUBLANES - k, tail, 0.0)
    return jnp.concatenate([main[:t - SUBLANES], last], axis=0)


def _ffn_bwd_a(dy2, pre, up0, w_down, fw, ts=256):
    s, d = dy2.shape
    nt = s // ts
    nch = D_FF // FFN_CHUNK
    wide = 2 * D_FF
    up_mats = jnp.stack([jnp.eye(ts, k=1, dtype=BF16), jnp.eye(ts, k=2, dtype=BF16)])

    def body(dy2_ref, pre_ref, up0_ref, wdn_ref, fw_ref, um_ref, dup0_ref, dfw_ref, dfb_ref, next_ref):
        i = pl.program_id(0)

        @pl.when(i == 0)
        def _():
            next_ref[...] = jnp.zeros_like(next_ref)
            dfw_ref[...] = jnp.zeros_like(dfw_ref)
            dfb_ref[...] = jnp.zeros_like(dfb_ref)

        dyb = dy2_ref[...]
        for j in range(nch):
            _, _, col = _ffn_cols(j)
            dact = _dot_nt(dyb, wdn_ref[col:col + FFN_CHUNK, :])
            gl, dgl = _gelu_and_grad(pre_ref[:, col:col + FFN_CHUNK].astype(F32))
            dpre = (dact * pre_ref[:, D_FF + col:D_FF + col + FFN_CHUNK].astype(F32) * dgl, dact * gl)
            for half, c0 in enumerate((col, D_FF + col)):
                cs = slice(c0, c0 + FFN_CHUNK)
                dp = dpre[half]
                dpb = dp.astype(BF16)
                nxt = next_ref[:, cs]
                next_ref[:, cs] = dpb.astype(F32)[0:SUBLANES, :]
                su1 = _shift_up_mxu(dpb, um_ref[0], nxt, 1)
                su2 = _shift_up_mxu(dpb, um_ref[1], nxt, 2)
                u = up0_ref[:, cs].astype(F32)
                dfb_ref[:, cs] += _colsum(dp)
                dfw_ref[2:3, cs] += _colsum(dp * u)
                dfw_ref[1:2, cs] += _colsum(su1 * u)
                dfw_ref[0:1, cs] += _colsum(su2 * u)
                dup0 = fw_ref[2:3, cs] * dp + fw_ref[1:2, cs] * su1 + fw_ref[0:1, cs] * su2
                dup0_ref[:, cs] = dup0.astype(BF16)

    return pl.pallas_call(
        body, grid=(nt,), name="ffn_bwd_a",
        in_specs=[_rows(ts, d, nt), _rows(ts, wide, nt), _rows(ts, wide, nt), _RESIDENT,
                  _full((3, wide)), _full((2, ts, ts))],
        out_specs=[_rows(ts, wide, nt), _full((3, wide)), _full((1, wide))],
        out_shape=[_sds((s, wide), BF16), _sds((3, wide), F32), _sds((1, wide), F32)],
        scratch_shapes=[pltpu.VMEM((SUBLANES, wide), F32)],
        compiler_params=_params(("arbitrary",)),
    )(dy2, pre, up0, w_down, fw, up_mats)


def _ffn_bwd_b(dup0, x1, y, dx2, w_up4, g_pre2, sc_f, sh_f, gt_m, g_post_m, ts=512):
    s, d = x1.shape
    shard_cols = 2 * D_FF // N_CHIPS

    def body(dup_ref, x1_ref, y_ref, dx2_ref, wup_ref, g2_ref, sc_ref, sh_ref, gt_ref, gp_ref,
             dx1_ref, dy_ref, dsh_ref, dsc_ref, dg2_ref, dgt_ref, dgp_ref):
        i = pl.program_id(0)

        @pl.when(i == 0)
        def _():
            for ref in (dsh_ref, dsc_ref, dg2_ref, dgt_ref, dgp_ref):
                ref[...] = jnp.zeros_like(ref)

        for rs in _sub_tiles(ts):
            dh2 = jnp.zeros((SUB_ROWS, d), F32)
            for k in range(N_CHIPS):
                dh2 = dh2 + _dot_nt(dup_ref[rs, k * shard_cols:(k + 1) * shard_cols], wup_ref[k])
            x1v = x1_ref[rs, :]
            r2 = _msq_rsqrt(x1v)
            xn = x1v * r2
            hn = xn * g2_ref[...]
            dsh_ref[...] += _colsum(dh2)
            dsc_ref[...] += _colsum(dh2 * hn)
            dhn = dh2 * (1.0 + sc_ref[...])
            dg2_ref[...] += _colsum(dhn * xn)
            dx1 = dx2_ref[rs, :] + _rms_bwd(dhn * g2_ref[...], xn, r2)
            dx1_ref[rs, :] = dx1
            yv = y_ref[rs, :]
            ry = _msq_rsqrt(yv)
            yn = yv * ry
            dgt_ref[...] += _colsum(dx1 * (yn * gp_ref[...]))
            dyng = dx1 * gt_ref[...]
            dgp_ref[...] += _colsum(dyng * yn)
            dy_ref[rs, :] = _rms_bwd(dyng * gp_ref[...], yn, ry).astype(BF16)

    vec = _full((1, d))
    return pl.pallas_call(
        body, grid=(s // ts,), name="ffn_bwd_b",
        in_specs=[_rows(ts, 2 * D_FF), _rows(ts, d), _rows(ts, d), _rows(ts, d), _RESIDENT,
                  vec, vec, vec, vec, vec],
        out_specs=[_rows(ts, d), _rows(ts, d), vec, vec, vec, vec, vec],
        out_shape=[_sds((s, d), F32), _sds((s, d), BF16)] + [_sds((1, d), F32)] * 5,
        compiler_params=_params(("arbitrary",)),
    )(dup0, x1, y, dx2, w_up4, g_pre2, sc_f, sh_f, gt_m, g_post_m)


def _seqmix_bwd(z, hst, stash, dy, w_out, seq_params, ws_t, glo, ggo, ts=256):
    s = z.shape[0]
    nt = s // ts
    small_shapes = [(4, 512), (1, 512), (512, 512), (512, 512), (1, 512), (1, 512), (1, 512),
                    (1, 512), (1, 512), (4, 128, 128), (128, 4), (1, 512), (1, 512)]

    def body(lx_ref, hst_ref, hprev_ref, st_ref, dy_ref, wout_ref, cw_ref, cb_ref, bdr_ref, bdi_ref, br_ref,
             bi_ref, la_ref, ng_ref, nb_ref, ws_ref, bst_ref, wst_ref, glo_ref, ggo_ref, dz_ref, *rest):
        small_refs = rest[:13]
        (dcw_ref, dcb_ref, dwr_ref, dwi_ref, dbr_ref, dbi_ref, dspa_ref, dng_ref, dnb_ref, dws_ref, dbs_ref,
         dglo_ref, dggo_ref) = small_refs
        gcarry, anext, dxcnext, dv_scr = rest[13:]
        i = pl.program_id(0)

        @pl.when(i == 0)
        def _():
            for ref in small_refs:
                ref[...] = jnp.zeros_like(ref)
            gcarry[...] = jnp.zeros_like(gcarry)
            anext[...] = jnp.ones_like(anext)
            dxcnext[...] = jnp.zeros_like(dxcnext)

        first_tile = i == nt - 1
        xc, r, ig, a, mult = st_ref[ST_XC], st_ref[ST_R], st_ref[ST_IG], st_ref[ST_A], st_ref[ST_MULT]
        gl, u, spb, vhat = st_ref[ST_GL], st_ref[ST_U], st_ref[ST_SPB], st_ref[ST_VHAT]
        lx = lx_ref[...]
        h = hst_ref[...]
        hprev = _shift_down(h, jnp.where(first_tile, 0.0, hprev_ref[...]), 1)
        y_l = h * gl
        y_g = u * spb

        dycat = _dot_nt(dy_ref[...], wout_ref[...])

        def emit_dz(k, val):
            dz_ref[:, k * 512:(k + 1) * 512] = val.astype(BF16)

        rl = _msq_rsqrt(y_l)
        yln = y_l * rl
        dyl = dycat[:, 0:512]
        dglo_ref[...] += _colsum(dyl * yln)
        dy_l = _rms_bwd(dyl * glo_ref[...], yln, rl)
        rg = _msq_rsqrt(y_g)
        ygn = y_g * rg
        dyg = dycat[:, 512:1024]
        dggo_ref[...] += _colsum(dyg * ygn)
        dy_g = _rms_bwd(dyg * ggo_ref[...], ygn, rg)

        emit_dz(1, dy_l * h * st_ref[ST_DGL])
        a_up = _shift_up(a, anext[...], 1)
        acum, gloc = _scan_bwd(a_up, dy_l * gl)
        gg = gloc + acum * gcarry[...]
        gcarry[...] = gg[0:1, :]
        anext[...] = a[0:SUBLANES, :]
        da = gg * hprev
        t1 = gg * mult
        di = t1 * xc
        dxc = t1 * ig
        dmult = gg * ig * xc
        dla = da * a - dmult * (a * a / mult)
        dspa_ref[...] += _colsum(dla * r) * (-LRU_C)
        dpr = dla * ((-LRU_C) * _softplus(-la_ref[...])) * r * (1.0 - r)
        dpi = di * ig * (1.0 - ig)
        dbr_ref[...] += _colsum(dpr)
        dbi_ref[...] += _colsum(dpi)
        dprb = dpr.astype(BF16)
        dpib = dpi.astype(BF16)
        xcb = xc.astype(BF16)
        dwr_ref[...] += _dot_tn(xcb, dprb)
        dwi_ref[...] += _dot_tn(xcb, dpib)
        dxc = dxc + _dot_nt(dprb, bdr_ref[...]) + _dot_nt(dpib, bdi_ref[...])
        nxt = dxcnext[...]
        dxcnext[...] = dxc[0:SUBLANES, :]
        up1, up2, up3 = _shift_up(dxc, nxt, 1), _shift_up(dxc, nxt, 2), _shift_up(dxc, nxt, 3)
        dcb_ref[...] += _colsum(dxc)
        dcw_ref[3:4, :] += _colsum(dxc * lx)
        dcw_ref[2:3, :] += _colsum(up1 * lx)
        dcw_ref[1:2, :] += _colsum(up2 * lx)
        dcw_ref[0:1, :] += _colsum(up3 * lx)
        dlx = cw_ref[3:4, :] * dxc + cw_ref[2:3, :] * up1 + cw_ref[1:2, :] * up2 + cw_ref[0:1, :] * up3
        emit_dz(0, dlx)

        emit_dz(2, dy_g * spb * st_ref[ST_DU])
        dsp = dy_g * u
        vb = (vhat * ng_ref[...] + nb_ref[...]).astype(BF16)
        for n in range(ts // GMLP_BLOCK):
            rs = slice(n * GMLP_BLOCK, (n + 1) * GMLP_BLOCK)
            for g in range(GMLP_GROUPS):
                cs = slice(g * 128, (g + 1) * 128)
                dbs_ref[:, g:g + 1] += jnp.sum(dsp[rs, cs], axis=1, keepdims=True)
                blk = dsp[rs, cs].astype(BF16)
                dws_ref[g] += _dot_nt(blk, vb[rs, cs])
                dv_scr[rs, cs] = _dot(wst_ref[g], blk)
        dv = dv_scr[...]
        dng_ref[...] += _colsum(dv * vhat)
        dnb_ref[...] += _colsum(dv)
        dvh = dv * ng_ref[...]
        dvg = dvh - jnp.mean(dvh, axis=-1, keepdims=True) - vhat * jnp.mean(dvh * vhat, axis=-1, keepdims=True)
        emit_dz(3, dvg * st_ref[ST_Q])

        @pl.when(i == nt - 1)
        def _():
            pos = lax.broadcasted_iota(jnp.int32, (GMLP_BLOCK, GMLP_BLOCK), 0) // CHUNK
            src = lax.broadcasted_iota(jnp.int32, (GMLP_BLOCK, GMLP_BLOCK), 1) // CHUNK
            for g in range(GMLP_GROUPS):
                dws_ref[g] = jnp.where(src <= pos, dws_ref[g], 0.0)
            dspa_ref[...] = dspa_ref[...] * (-_sigmoid(-la_ref[...]))

    in_specs = ([_rows(ts, 512, nt), _rows(ts, 512, nt), _halo_prev(ts, 512, SUBLANES, nt),
                 pl.BlockSpec((N_STASH, ts, 512), lambda i: (0, nt - 1 - i, 0)), _rows(ts, 1024, nt),
                 _full((1024, 1024))]
                + _seq_param_specs() + [_full((4, 128, 128)), _full((1, 512)), _full((1, 512))])
    return pl.pallas_call(
        body, grid=(nt,), name="seqmix_bwd",
        in_specs=in_specs,
        out_specs=[_rows(ts, 2048, nt)] + [_full(sh) for sh in small_shapes],
        out_shape=[_sds((s, 2048), BF16)] + [_sds(sh, F32) for sh in small_shapes],
        scratch_shapes=[pltpu.VMEM((1, 512), F32), pltpu.VMEM((SUBLANES, 512), F32),
                        pltpu.VMEM((SUBLANES, 512), F32), pltpu.VMEM((ts, 512), F32)],
        compiler_params=_params(("arbitrary",)),
    )(z, hst, hst, stash, dy, w_out, *seq_params, ws_t, glo, ggo)


def _mix_in_bwd(x, dz, dx1, w_in4, g, sc, ts=512):
    s, d = x.shape

    def body(x_ref, dz_ref, dx1_ref, w_ref, g_ref, sc_ref, gx_ref, dsh_ref, dsc_ref, dg_ref):
        i = pl.program_id(0)

        @pl.when(i == 0)
        def _():
            for ref in (dsh_ref, dsc_ref, dg_ref):
                ref[...] = jnp.zeros_like(ref)

        for rs in _sub_tiles(ts):
            dh = jnp.zeros((SUB_ROWS, d), F32)
            for k in range(N_CHIPS):
                dh = dh + _dot_nt(dz_ref[rs, k * 512:(k + 1) * 512], w_ref[k])
            xv = x_ref[rs, :]
            r = _msq_rsqrt(xv)
            xn = xv * r
            dsh_ref[...] += _colsum(dh)
            dsc_ref[...] += _colsum(dh * (xn * g_ref[...]))
            dhn = dh * (1.0 + sc_ref[...])
            dg_ref[...] += _colsum(dhn * xn)
            gx_ref[rs, :] = dx1_ref[rs, :] + _rms_bwd(dhn * g_ref[...], xn, r)

    vec = _full((1, d))
    return pl.pallas_call(
        body, grid=(s // ts,), name="mix_in_bwd",
        in_specs=[_rows(ts, d), _rows(ts, 2048), _rows(ts, d), _full(w_in4.shape), vec, vec],
        out_specs=[_rows(ts, d), vec, vec, vec],
        out_shape=[_sds((s, d), F32)] + [_sds((1, d), F32)] * 3,
        compiler_params=_params(("arbitrary",)),
    )(x, dz, dx1, w_in4, g, sc)


def _wgrad(a, b, n_chunks, name, chunk_major, ts=2048):
    s, m = a.shape
    n = b.shape[1]
    nc = n // n_chunks
    nt = s // ts

    def body(a_ref, b_ref, o_ref, acc):
        i = pl.program_id(1)

        @pl.when(i == 0)
        def _():
            acc[...] = jnp.zeros_like(acc)

        acc[...] += _dot_tn(a_ref[...], b_ref[...])

        @pl.when(i == nt - 1)
        def _():
            if chunk_major:
                o_ref[0] = acc[...].astype(BF16)
            else:
                o_ref[...] = acc[...].astype(BF16)

    if chunk_major:
        out_spec, out_shape = pl.BlockSpec((1, m, nc), lambda c, i: (c, 0, 0)), _sds((n_chunks, m, nc), BF16)
    else:
        out_spec, out_shape = pl.BlockSpec((m, nc), lambda c, i: (0, c)), _sds((m, n), BF16)
    return pl.pallas_call(
        body, grid=(n_chunks, nt), name=name,
        in_specs=[pl.BlockSpec((ts, m), lambda c, i: (i, 0)), pl.BlockSpec((ts, nc), lambda c, i: (i, c))],
        out_specs=out_spec,
        out_shape=out_shape,
        scratch_shapes=[pltpu.VMEM((m, nc), F32)],
        compiler_params=_params(("parallel", "arbitrary")),
    )(a, b)


def _block_diag(w):
    heads, hd, _ = w.shape
    eye = jnp.eye(heads, dtype=w.dtype)
    return (eye[:, None, :, None] * w[:, :, None, :]).reshape(heads * hd, heads * hd)


def _seq_params(small):
    row = lambda v: v.reshape(1, -1)
    pos = jnp.arange(GMLP_BLOCK)
    mask = (pos[None, :] // CHUNK) <= (pos[:, None] // CHUNK)
    ws = jnp.where(mask[None], small["w_spatial"], 0.0)
    seq_params = (small["conv_w"], row(small["conv_b"]),
                  _block_diag(small["w_rgate"]).astype(BF16), _block_diag(small["w_igate"]).astype(BF16),
                  row(small["b_rgate"]), row(small["b_igate"]), row(small["lru_a"]),
                  row(small["v_norm_g"]), row(small["v_norm_b"]), ws.astype(BF16), small["b_spatial"].T)
    return seq_params, jnp.swapaxes(ws, 1, 2).astype(BF16)


_ANY = pl.BlockSpec(memory_space=pl.ANY)
_CHIP_FLIPS = ((1, 0), (0, 1), (1, 1))


def _position():
    return lax.axis_index("x"), lax.axis_index("y"), lax.axis_index("c")


def _flip(v, f):
    return 1 - v if f else v


def _remote(src, dst, send_sem, recv_sem, peer):
    return pltpu.make_async_remote_copy(src_ref=src, dst_ref=dst, send_sem=send_sem, recv_sem=recv_sem,
                                        device_id=peer, device_id_type=MESH)


def _allgather8(block, name):
    r, n = block.shape

    def body(x_ref, gath, send_sems, recv_sems, loc_sem):
        x, y, c = _position()
        me = 4 * x + 2 * y + c
        loc = pltpu.make_async_copy(x_ref, gath.at[me], loc_sem)
        loc.start()
        peers = []
        for k in range(1, N_DEV):
            px, py, pc = _flip(x, k & 4), _flip(y, k & 2), _flip(c, k & 1)
            peers.append((px, py, pc))
            _remote(x_ref, gath.at[me], send_sems.at[k - 1], recv_sems.at[k - 1], (px, py, pc)).start()
        for k, (px, py, pc) in enumerate(peers):
            src = 4 * px + 2 * py + pc
            _remote(x_ref, gath.at[src], send_sems.at[k], recv_sems.at[k], (px, py, pc)).wait_recv()
        for k, peer in enumerate(peers):
            _remote(x_ref, gath.at[me], send_sems.at[k], recv_sems.at[k], peer).wait_send()
        loc.wait()

    return pl.pallas_call(
        body, name=name, out_shape=_sds((N_DEV, r, n), F32),
        in_specs=[pl.BlockSpec(memory_space=pltpu.VMEM)], out_specs=pl.BlockSpec(memory_space=pltpu.VMEM),
        scratch_shapes=[pltpu.SemaphoreType.DMA((N_DEV - 1,)), pltpu.SemaphoreType.DMA((N_DEV - 1,)),
                        pltpu.SemaphoreType.DMA],
        compiler_params=pltpu.CompilerParams(vmem_limit_bytes=VMEM_LIMIT_BYTES),
    )(block)


def _half(ref, c, rows):
    hr = rows // 2
    return ref.at[pl.ds(pl.multiple_of(c * hr, BF16_SUBLANES), hr), :]


def _chip_sum(part, recv, pos_arr, name):
    _, rows, cols = part.shape
    hr = rows // 2

    def body(pos_ref, p_ref, r_ref, o_ref, g_ref):
        total = (p_ref[...].astype(F32) + r_ref[...].astype(F32)).astype(BF16)
        o_ref[...] = total

        @pl.when(pl.program_id(0) == pos_ref[1])
        def _():
            g_ref[0] = total

    grid_spec = pltpu.PrefetchScalarGridSpec(
        num_scalar_prefetch=1, grid=(N_CHIPS,),
        in_specs=[pl.BlockSpec((1, hr, cols), lambda k, pos: (k, pos[0], 0)),
                  pl.BlockSpec((1, hr, cols), lambda k, pos: (k, 0, 0))],
        out_specs=[pl.BlockSpec((1, hr, cols), lambda k, pos: (k, 0, 0)),
                   pl.BlockSpec((1, 1, hr, cols), lambda k, pos: (0, pos[1], 0, 0))])
    return pl.pallas_call(
        body, name=name, grid_spec=grid_spec,
        out_shape=[_sds((N_CHIPS, hr, cols), BF16), _sds((2, N_CHIPS, hr, cols), BF16)],
        compiler_params=_params(("arbitrary",)),
    )(pos_arr, part, recv)


_HBM = pl.BlockSpec(memory_space=pltpu.HBM)
_SEM = pl.BlockSpec(memory_space=pltpu.SEMAPHORE)
_EFFECT = pltpu.SideEffectType.DATAFLOW_SIDE_EFFECTING


def _in_hbm(a):
    return pltpu.with_memory_space_constraint(a, pltpu.HBM)


def _split_start(srcs, lands, plan, n_copies, after, name):
    ns, nl = len(srcs), len(lands)
    bufs = list(srcs) + list(lands)

    def body(*refs):
        send_sems, recv_sems = refs[ns + nl + 1], refs[ns + nl + 2]
        token = refs[-1]
        for k, (src, dst, peer) in enumerate(plan(refs[:ns], refs[ns:ns + nl])):
            _remote(src, dst, send_sems.at[k], recv_sems.at[k], peer).start()
        token[...] = jnp.zeros_like(token)

    out = pl.pallas_call(
        body, name=name,
        out_shape=(pltpu.SemaphoreType.DMA((n_copies,)), pltpu.SemaphoreType.DMA((n_copies,)),
                   *[pltpu.HBM(b.shape, b.dtype) for b in bufs], _sds((SUBLANES, 128), F32)),
        in_specs=[_HBM] * (ns + nl) + [_ANY],
        out_specs=(_SEM, _SEM, *[_HBM] * (ns + nl), pl.BlockSpec(memory_space=pltpu.VMEM)),
        input_output_aliases={i: 2 + i for i in range(ns + nl)},
        compiler_params=pltpu.CompilerParams(has_side_effects=_EFFECT),
    )(*[_in_hbm(b) for b in bufs], after)
    return out[0], out[1], list(out[2:2 + ns]), list(out[2 + ns:2 + ns + nl]), out[-1]


def _split_wait(send_sems, recv_sems, srcs, lands, plan, after, name):
    ns, nl = len(srcs), len(lands)
    bufs = list(srcs) + list(lands)

    def body(*refs):
        send_ref, recv_ref = refs[ns + nl], refs[ns + nl + 1]
        me = _position()
        for k, src, dst in plan(refs[:ns], refs[ns:ns + nl]):
            cp = _remote(src, dst, send_ref.at[k], recv_ref.at[k], me)
            cp.wait_send()
            cp.wait_recv()

    out = pl.pallas_call(
        body, name=name,
        out_shape=[pltpu.HBM(b.shape, b.dtype) for b in bufs],
        in_specs=[_HBM] * (ns + nl) + [_SEM, _SEM, _ANY],
        out_specs=[_HBM] * (ns + nl),
        input_output_aliases={i: i for i in range(ns + nl)},
        compiler_params=pltpu.CompilerParams(has_side_effects=_EFFECT),
    )(*bufs, send_sems, recv_sems, after)
    return list(out[:ns]), list(out[ns:])


def _gather_plan(rows_of):
    def start(src_refs, land_refs):
        x, y, c = _position()
        chip = 2 * x + y
        out = []
        for a, rows in enumerate(rows_of):
            mine = _half(land_refs[a].at[chip], c, rows)
            out.extend((mine, mine, (_flip(x, fx), _flip(y, fy), c)) for fx, fy in _CHIP_FLIPS)
        return out

    def wait(src_refs, land_refs):
        x, y, c = _position()
        chip = 2 * x + y
        out = []
        for a, rows in enumerate(rows_of):
            for j, (fx, fy) in enumerate(_CHIP_FLIPS):
                src_chip = 2 * _flip(x, fx) + _flip(y, fy)
                out.append((3 * a + j, _half(land_refs[a].at[chip], c, rows),
                            _half(land_refs[a].at[src_chip], c, rows)))
        return out

    return start, wait


def _forward_plan(rows_of):
    def pieces(land_refs, half):
        x, y, _ = _position()
        return [_half(land_refs[a].at[2 * _flip(x, fx) + _flip(y, fy)], half, rows)
                for a, rows in enumerate(rows_of) for fx, fy in _CHIP_FLIPS]

    def start(src_refs, land_refs):
        x, y, c = _position()
        return [(p, p, (x, y, 1 - c)) for p in pieces(land_refs, c)]

    def wait(src_refs, land_refs):
        _, _, c = _position()
        return [(k, mine, theirs)
                for k, (mine, theirs) in enumerate(zip(pieces(land_refs, c), pieces(land_refs, 1 - c)))]

    return start, wait


def _swap_halves_plan(half_rows):
    def slices(src_refs, c):
        return [src_refs[a].at[:, pl.ds(pl.multiple_of((1 - c) * hr, BF16_SUBLANES), hr), :]
                for a, hr in enumerate(half_rows)]

    def start(src_refs, land_refs):
        x, y, c = _position()
        return [(src, land_refs[a], (x, y, 1 - c)) for a, src in enumerate(slices(src_refs, c))]

    def wait(src_refs, land_refs):
        _, _, c = _position()
        return [(a, src, land_refs[a]) for a, src in enumerate(slices(src_refs, c))]

    return start, wait


def _swap_gathered_plan(n_arrays):
    def start(src_refs, land_refs):
        x, y, c = _position()
        return [(land_refs[a].at[0], land_refs[a].at[1], (x, y, 1 - c)) for a in range(n_arrays)]

    def wait(src_refs, land_refs):
        return [(a, land_refs[a].at[0], land_refs[a].at[1]) for a in range(n_arrays)]

    return start, wait


def _exchange_plan(n_arrays):
    def start(src_refs, land_refs):
        x, y, c = _position()
        chip = 2 * x + y
        out = []
        for a in range(n_arrays):
            for fx, fy in _CHIP_FLIPS:
                px, py = _flip(x, fx), _flip(y, fy)
                out.append((src_refs[a].at[2 * px + py], land_refs[a].at[0, chip], (px, py, c)))
        return out

    def wait(src_refs, land_refs):
        x, y, c = _position()
        out = []
        for a in range(n_arrays):
            for j, (fx, fy) in enumerate(_CHIP_FLIPS):
                src_chip = 2 * _flip(x, fx) + _flip(y, fy)
                out.append((3 * a + j, src_refs[a].at[src_chip], land_refs[a].at[0, src_chip]))
        return out

    return start, wait


def _forward_to_sibling(lands, name):
    na = len(lands)

    def body(*refs):
        land_refs = refs[na:2 * na]
        send_sems, recv_sems = refs[2 * na:]
        x, y, c = _position()
        sibling = (x, y, 1 - c)
        sends = []
        for a in range(na):
            rows = lands[a].shape[1]
            for j, (fx, fy) in enumerate(_CHIP_FLIPS):
                landed = _half(land_refs[a].at[2 * _flip(x, fx) + _flip(y, fy)], c, rows)
                sends.append(_remote(landed, landed, send_sems.at[3 * a + j], recv_sems.at[3 * a + j], sibling))
                sends[-1].start()
        for a in range(na):
            rows = lands[a].shape[1]
            for j, (fx, fy) in enumerate(_CHIP_FLIPS):
                other = _half(land_refs[a].at[2 * _flip(x, fx) + _flip(y, fy)], 1 - c, rows)
                _remote(other, other, send_sems.at[3 * a + j], recv_sems.at[3 * a + j], sibling).wait_recv()
        for cp in sends:
            cp.wait_send()

    return pl.pallas_call(
        body, name=name,
        out_shape=[_sds(l.shape, l.dtype) for l in lands],
        in_specs=[_ANY] * na, out_specs=[_ANY] * na,
        input_output_aliases={a: a for a in range(na)},
        scratch_shapes=[pltpu.SemaphoreType.DMA((3 * na,))] * 2,
    )(*lands)


def _adam_gathered(w, gath, m, v, c_arr, after, name, tr=128):
    rows, cols = w.shape
    hr = rows // 2
    if hr % (2 * tr) == 0:
        tr = 2 * tr
    per = hr // tr

    def body(c_ref, w_ref, g_ref, m_ref, v_ref, after_ref, go_ref, d_ref, nm_ref, nv_ref):
        g = g_ref[0, 0].astype(F32)
        for k in range(1, N_CHIPS):
            g = g + g_ref[0, k].astype(F32)
        go_ref[...] = g
        d_ref[...], nm_ref[...], nv_ref[...] = _adam_math(w_ref[...], g, m_ref[...], v_ref[...])

    def rows_of(h, i, c_ref):
        c = c_ref[0]
        return ((c + h - 2 * c * h) * per + i, 0)

    blk = pl.BlockSpec((tr, cols), rows_of)
    grid_spec = pltpu.PrefetchScalarGridSpec(
        num_scalar_prefetch=1, grid=(2, per),
        in_specs=[blk, pl.BlockSpec((1, N_CHIPS, tr, cols), lambda h, i, c_ref: (h, 0, i, 0)), blk, blk, _ANY],
        out_specs=[blk] * 4)
    return pl.pallas_call(
        body, name=name, grid_spec=grid_spec, out_shape=[_sds(w.shape, F32)] * 4,
        compiler_params=_params(("arbitrary", "arbitrary")),
    )(c_arr, w, gath, m, v, after)


def _allreduce_small(block, name):
    two, r, n = block.shape
    assert two == 2

    def body(x_ref, out_ref, sib, chipsum, gath, d2d_send, d2d_recv, ici_send, ici_recv):
        x, y, c = _position()
        chip = 2 * x + y
        sibling = (x, y, 1 - c)
        first = _remote(x_ref, sib, d2d_send.at[0], d2d_recv.at[0], sibling)
        first.start()
        first.wait()
        chipsum[...] = x_ref[...] + sib[...]
        sends = []
        for j, (fx, fy) in enumerate(_CHIP_FLIPS):
            sends.append(_remote(chipsum.at[c], gath.at[chip], ici_send.at[j], ici_recv.at[j],
                                 (_flip(x, fx), _flip(y, fy), c)))
            sends[-1].start()
        gath[chip] = chipsum[c]
        for j, (fx, fy) in enumerate(_CHIP_FLIPS):
            landed = gath.at[2 * _flip(x, fx) + _flip(y, fy)]
            _remote(landed, landed, ici_send.at[j], ici_recv.at[j], sibling).wait_recv()
        for cp in sends:
            cp.wait_send()
        total = gath[0]
        for k in range(1, N_CHIPS):
            total = total + gath[k]
        out_ref[c] = total
        last = _remote(out_ref.at[c], out_ref.at[c], d2d_send.at[1], d2d_recv.at[1], sibling)
        last.start()
        _remote(out_ref.at[1 - c], out_ref.at[1 - c], d2d_send.at[1], d2d_recv.at[1], sibling).wait_recv()
        last.wait_send()

    vmem = pl.BlockSpec(memory_space=pltpu.VMEM)
    return pl.pallas_call(
        body, name=name, out_shape=_sds(block.shape, F32), in_specs=[vmem], out_specs=vmem,
        scratch_shapes=[pltpu.VMEM(block.shape, F32), pltpu.VMEM(block.shape, F32), pltpu.VMEM((N_CHIPS, r, n), F32),
                        pltpu.SemaphoreType.DMA((2,)), pltpu.SemaphoreType.DMA((2,)),
                        pltpu.SemaphoreType.DMA((3,)), pltpu.SemaphoreType.DMA((3,))],
        compiler_params=pltpu.CompilerParams(vmem_limit_bytes=VMEM_LIMIT_BYTES),
    )(block)


def _cast_place(shards, chip_arr):
    na = len(shards)
    steps = 4

    def body(chip_ref, *refs):
        for a in range(na):
            refs[na + a][0] = refs[a][...].astype(BF16)

    grid_spec = pltpu.PrefetchScalarGridSpec(
        num_scalar_prefetch=1, grid=(steps,),
        in_specs=[pl.BlockSpec((s.shape[0] // steps, s.shape[1]), lambda i, ch: (i, 0)) for s in shards],
        out_specs=[pl.BlockSpec((1, s.shape[0] // steps, s.shape[1]), lambda i, ch: (ch[0], i, 0)) for s in shards])
    return pl.pallas_call(
        body, name="cast_place", grid_spec=grid_spec,
        out_shape=[_sds((N_CHIPS,) + s.shape, BF16) for s in shards],
        compiler_params=_params(("arbitrary",)),
    )(chip_arr, *shards)


def _silu(v):
    return v * _sigmoid(v)


def _ada_fwd(c8, w_ada):
    def body(c_ref, w_ref, o_ref):
        o_ref[...] = jnp.dot(_silu(c_ref[...]), w_ref[...], preferred_element_type=F32,
                             precision=lax.Precision.HIGHEST)

    return pl.pallas_call(
        body, name="ada_fwd", out_shape=_sds((N_DEV, w_ada.shape[1]), F32),
        compiler_params=pltpu.CompilerParams(vmem_limit_bytes=VMEM_LIMIT_BYTES),
    )(c8, w_ada)


def _mod_select(parts, b_ada, me_arr, after):
    cols = parts.shape[2]

    def body(me_ref, p_ref, b_ref, after_ref, o_ref):
        me = me_ref[0]
        for k in range(N_CHIPS):
            cs = slice(k * cols, (k + 1) * cols)
            o_ref[:, cs] = p_ref[2 * k, pl.ds(me, 1), :] + b_ref[:, cs]

    grid_spec = pltpu.PrefetchScalarGridSpec(
        num_scalar_prefetch=1, grid=(1,),
        in_specs=[pl.BlockSpec(parts.shape, lambda i, m: (0, 0, 0)), pl.BlockSpec(b_ada.shape, lambda i, m: (0, 0)),
                  _ANY],
        out_specs=pl.BlockSpec(b_ada.shape, lambda i, m: (0, 0)))
    return pl.pallas_call(body, name="mod_select", grid_spec=grid_spec, out_shape=_sds(b_ada.shape, F32))(
        me_arr, parts, b_ada, after)


def _ada_bwd(c8, dmod8, chip_arr, w, m, v, tr=256):
    d = c8.shape[1]
    cols = dmod8.shape[1] // N_CHIPS

    def body(chip_ref, c_ref, dm_ref, dmall_ref, w_ref, m_ref, v_ref, gw_ref, d_ref, nm_ref, nv_ref, gb_ref):
        g = lax.dot_general(_silu(c_ref[...]), dm_ref[...], (((0,), (0,)), ((), ())),
                            preferred_element_type=F32, precision=lax.Precision.HIGHEST)
        gw_ref[...] = g
        d_ref[...], nm_ref[...], nv_ref[...] = _adam_math(w_ref[...], g, m_ref[...], v_ref[...])
        acc = dmall_ref[0:1, :]
        for k in range(1, N_DEV):
            acc = acc + dmall_ref[k:k + 1, :]
        gb_ref[...] = acc

    rows = pl.BlockSpec((tr, cols), lambda i, ch: (i, 0))
    grid_spec = pltpu.PrefetchScalarGridSpec(
        num_scalar_prefetch=1, grid=(d // tr,),
        in_specs=[pl.BlockSpec((N_DEV, tr), lambda i, ch: (0, i)),
                  pl.BlockSpec((N_DEV, cols), lambda i, ch: (0, ch[0])),
                  pl.BlockSpec(dmod8.shape, lambda i, ch: (0, 0)), rows, rows, rows],
        out_specs=[rows] * 4 + [pl.BlockSpec((1, dmod8.shape[1]), lambda i, ch: (0, 0))])
    return pl.pallas_call(
        body, name="ada_bwd", grid_spec=grid_spec,
        out_shape=[_sds((d, cols), F32)] * 4 + [_sds((1, dmod8.shape[1]), F32)],
        compiler_params=_params(("arbitrary",)),
    )(chip_arr, c8, dmod8, dmod8, w, m, v)


def _adam_math(w, g, m, v):
    m = ADAM_B1 * m + (1.0 - ADAM_B1) * g
    v = ADAM_B2 * v + (1.0 - ADAM_B2) * (g * g)
    m_hat = m / (1.0 - ADAM_B1 ** ADAM_STEP)
    v_hat = v / (1.0 - ADAM_B2 ** ADAM_STEP)
    delta = -ADAM_LR * (m_hat / (jnp.sqrt(v_hat) + ADAM_EPS) + ADAM_WD * w)
    return delta, m, v


def _adam(w, g, m, v, name, tr=256):
    rows, cols = w.shape
    if rows % tr:
        tr = rows

    def body(w_ref, g_ref, m_ref, v_ref, d_ref, nm_ref, nv_ref):
        d_ref[...], nm_ref[...], nv_ref[...] = _adam_math(w_ref[...], g_ref[...], m_ref[...], v_ref[...])

    spec = pl.BlockSpec((tr, cols), lambda i: (i, 0))
    return pl.pallas_call(
        body, name=name, grid=(rows // tr,), in_specs=[spec] * 4, out_specs=[spec] * 3,
        out_shape=[_sds(w.shape, F32)] * 3, compiler_params=_params(("parallel",)),
    )(w, g, m, v)


SMALL_REPLICATED = ("g_mix_pre", "g_mix_post", "conv_b", "w_rgate", "b_rgate", "w_igate", "b_igate", "lru_a",
                    "v_norm_g", "v_norm_b", "w_spatial", "b_spatial", "g_lru_out", "g_gmlp_out", "g_ffn_pre",
                    "g_ffn_post", "ffn_conv_b")
SMALL_COLUMN_SHARDED = ("conv_w", "ffn_conv_w")

SMALL_ROW_LEN = 86016
_SMALL_ROWS = (
    (("ffn_conv_w", 18432), ("conv_w", 2048), ("w_spatial", 65536)),
    (("w_rgate", 32768), ("w_igate", 32768), ("ffn_conv_b", 6144), ("g_mix_pre", 1024), ("g_mix_post", 1024),
     ("g_ffn_pre", 1024), ("g_ffn_post", 1024), ("conv_b", 512), ("b_rgate", 512), ("b_igate", 512),
     ("lru_a", 512), ("v_norm_g", 512), ("v_norm_b", 512), ("b_spatial", 512), ("g_lru_out", 512),
     ("g_gmlp_out", 512), ("loss", 128)),
)


def _small_slots():
    slots = {}
    for row, entries in enumerate(_SMALL_ROWS):
        off = 0
        for name, size in entries:
            slots[name] = (row, off)
            off += size
        assert off <= SMALL_ROW_LEN
    return slots


SMALL_SLOT = _small_slots()
SMALL_LANES = SMALL_ROW_LEN // SUBLANES


def _small_pieces(name, first, count):
    row, off = SMALL_SLOT[name]
    pos, pieces = off + first, []
    while count:
        sub, lane = divmod(pos, SMALL_LANES)
        n = min(count, SMALL_LANES - lane)
        pieces.append((row, sub, lane, n))
        pos, count = pos + n, count - n
    return pieces
ROW_VECTORS = ("ffn_conv_b", "g_mix_pre", "g_mix_post", "g_ffn_pre", "g_ffn_post", "conv_b", "lru_a", "v_norm_g",
               "v_norm_b", "g_lru_out", "g_gmlp_out")
HEAD_DIM = LRU_WIDTH // LRU_HEADS


def _pack_small(g, after):
    order = ("ffn_conv_w", "conv_w", "w_spatial", "w_rgate", "w_igate", "b_rgate", "b_igate", "b_spatial", "loss") \
        + ROW_VECTORS
    vmem = pl.BlockSpec(memory_space=pltpu.VMEM)

    def body(*refs):
        src = dict(zip(order, refs))
        out_ref = refs[len(order) + 1]
        out_ref[...] = jnp.zeros_like(out_ref)

        def put(name, first, val):
            col = 0
            for row, sub, lane, n in _small_pieces(name, first, val.shape[1]):
                out_ref[row, sub:sub + 1, lane:lane + n] = val[:, col:col + n]
                col += n

        for name in ROW_VECTORS + ("b_rgate", "b_igate", "loss"):
            put(name, 0, src[name][...])
        for name in ("ffn_conv_w", "conv_w"):
            k_taps, n = src[name].shape
            for k in range(k_taps):
                put(name, k * n, src[name][k:k + 1, :])
        for g_idx in range(GMLP_GROUPS):
            for i in range(GMLP_BLOCK):
                put("w_spatial", (g_idx * GMLP_BLOCK + i) * GMLP_BLOCK, src["w_spatial"][g_idx, i:i + 1, :])
        for name in ("w_rgate", "w_igate"):
            for h in range(LRU_HEADS):
                for i in range(HEAD_DIM):
                    r = h * HEAD_DIM + i
                    put(name, r * HEAD_DIM, src[name][r:r + 1, h * HEAD_DIM:(h + 1) * HEAD_DIM])
        eye = (lax.broadcasted_iota(jnp.int32, (GMLP_BLOCK, GMLP_BLOCK), 0)
               == lax.broadcasted_iota(jnp.int32, (GMLP_BLOCK, GMLP_BLOCK), 1))
        for g_idx in range(GMLP_GROUPS):
            col = src["b_spatial"][:, g_idx:g_idx + 1]
            put("b_spatial", g_idx * GMLP_BLOCK, _colsum(jnp.where(eye, col, 0.0)))

    return pl.pallas_call(
        body, name="pack_small", out_shape=_sds((2, SUBLANES, SMALL_LANES), F32),
        in_specs=[vmem] * len(order) + [_ANY], out_specs=vmem,
        compiler_params=pltpu.CompilerParams(vmem_limit_bytes=VMEM_LIMIT_BYTES),
    )(*[g[n] for n in order], after)


def _adam_small(g_small, w, m, v):
    vmem = pl.BlockSpec(memory_space=pltpu.VMEM)
    n_p = len(SMALL_REPLICATED)

    def body(g_ref, *refs):
        w_refs, m_refs, v_refs = refs[:n_p], refs[n_p:2 * n_p], refs[2 * n_p:3 * n_p]
        outs = refs[3 * n_p:]
        go, do, mo, vo = outs[:n_p], outs[n_p:2 * n_p], outs[2 * n_p:3 * n_p], outs[3 * n_p:]
        for k, name in enumerate(SMALL_REPLICATED):
            def take(first, count, name=name):
                parts = [g_ref[row, sub:sub + 1, lane:lane + n]
                         for row, sub, lane, n in _small_pieces(name, first, count)]
                return parts[0] if len(parts) == 1 else jnp.concatenate(parts, axis=1)

            shape = w_refs[k].shape
            if name in ROW_VECTORS:
                go[k][...] = take(0, shape[1])
            elif name in ("b_rgate", "b_igate"):
                for h in range(LRU_HEADS):
                    go[k][0, h:h + 1, :] = take(h * HEAD_DIM, HEAD_DIM)
            elif name == "b_spatial":
                for g_idx in range(GMLP_GROUPS):
                    go[k][0, g_idx:g_idx + 1, :] = take(g_idx * GMLP_BLOCK, GMLP_BLOCK)
            elif name == "w_spatial":
                for g_idx in range(GMLP_GROUPS):
                    for i in range(GMLP_BLOCK):
                        go[k][0, g_idx, i:i + 1, :] = take((g_idx * GMLP_BLOCK + i) * GMLP_BLOCK, GMLP_BLOCK)
            else:
                for h in range(LRU_HEADS):
                    for i in range(HEAD_DIM):
                        go[k][0, h, i:i + 1, :] = take((h * HEAD_DIM + i) * HEAD_DIM, HEAD_DIM)
            do[k][...], mo[k][...], vo[k][...] = _adam_math(w_refs[k][...], go[k][...], m_refs[k][...],
                                                             v_refs[k][...])

    names = SMALL_REPLICATED
    out_shape = [_sds(w[n].shape, F32) for n in names] * 4
    res = pl.pallas_call(
        body, name="adam_small", out_shape=out_shape,
        in_specs=[vmem] * (1 + 3 * n_p), out_specs=[vmem] * (4 * n_p),
        compiler_params=pltpu.CompilerParams(vmem_limit_bytes=VMEM_LIMIT_BYTES),
    )(g_small, *[w[n] for n in names], *[m[n] for n in names], *[v[n] for n in names])
    return [dict(zip(names, res[k * n_p:(k + 1) * n_p])) for k in range(4)]


def _adam_cols(name, g_small, w, m, v, chip_arr):
    _, k_taps, n = w.shape
    row, off = SMALL_SLOT[name]
    first = off // n
    per_sub = SMALL_LANES // n

    def body(chip_ref, *refs):
        g_refs = refs[:k_taps]
        w_ref, m_ref, v_ref, go_ref, d_ref, nm_ref, nv_ref = refs[k_taps:]
        for k in range(k_taps):
            tap = (0, slice(k, k + 1), slice(None))
            sub = (first + N_CHIPS * k + chip_ref[0]) // per_sub
            g = g_refs[k][row, pl.ds(sub, 1), :]
            go_ref[tap] = g
            d_ref[tap], nm_ref[tap], nv_ref[tap] = _adam_math(w_ref[tap], g, m_ref[tap], v_ref[tap])

    whole = pl.BlockSpec(w.shape, lambda i, ch: (0, 0, 0))
    taps = [pl.BlockSpec((2, SUBLANES, n),
                         functools.partial(lambda i, ch, k: (0, 0, (first + N_CHIPS * k + ch[0]) % per_sub), k=k))
            for k in range(k_taps)]
    grid_spec = pltpu.PrefetchScalarGridSpec(
        num_scalar_prefetch=1, grid=(1,), in_specs=taps + [whole] * 3, out_specs=[whole] * 4)
    return pl.pallas_call(body, name="adam_" + name, grid_spec=grid_spec, out_shape=[_sds(w.shape, F32)] * 4)(
        chip_arr, *[g_small] * k_taps, w, m, v)


def kernel(x, c, w_ada, b_ada, g_mix_pre, g_mix_post, w_in, conv_w, conv_b, w_rgate, b_rgate, w_igate, b_igate, lru_a, v_norm_g, v_norm_b, w_spatial, b_spatial, g_lru_out, g_gmlp_out, w_out, g_ffn_pre, g_ffn_post, w_up, ffn_conv_w, ffn_conv_b, w_down, loss_target, m_w_ada, m_b_ada, m_g_mix_pre, m_g_mix_post, m_w_in, m_conv_w, m_conv_b, m_w_rgate, m_b_rgate, m_w_igate, m_b_igate, m_lru_a, m_v_norm_g, m_v_norm_b, m_w_spatial, m_b_spatial, m_g_lru_out, m_g_gmlp_out, m_w_out, m_g_ffn_pre, m_g_ffn_post, m_w_up, m_ffn_conv_w, m_ffn_conv_b, m_w_down, v_w_ada, v_b_ada, v_g_mix_pre, v_g_mix_post, v_w_in, v_conv_w, v_conv_b, v_w_rgate, v_b_rgate, v_w_igate, v_b_igate, v_lru_a, v_v_norm_g, v_v_norm_b, v_w_spatial, v_b_spatial, v_g_lru_out, v_g_gmlp_out, v_w_out, v_g_ffn_pre, v_g_ffn_post, v_w_up, v_ffn_conv_w, v_ffn_conv_b, v_w_down):
    args = dict(locals())
    names = ("w_ada", "b_ada", "g_mix_pre", "g_mix_post", "w_in", "conv_w", "conv_b", "w_rgate", "b_rgate",
             "w_igate", "b_igate", "lru_a", "v_norm_g", "v_norm_b", "w_spatial", "b_spatial", "g_lru_out",
             "g_gmlp_out", "w_out", "g_ffn_pre", "g_ffn_post", "w_up", "ffn_conv_w", "ffn_conv_b", "w_down")
    drop = lambda a: a if a.ndim == 2 else a[0]
    w = {n: drop(args[n]) for n in names}
    m = {n: drop(args["m_" + n]) for n in names}
    v = {n: drop(args["v_" + n]) for n in names}
    xi, yi, ci = _position()
    me_arr = jnp.reshape(4 * xi + 2 * yi + ci, (1,)).astype(jnp.int32)
    chip_arr = jnp.reshape(2 * xi + yi, (1,)).astype(jnp.int32)
    c_arr = jnp.reshape(ci, (1,)).astype(jnp.int32)
    pos_arr = jnp.stack([ci, 2 * xi + yi]).astype(jnp.int32)

    big = ("w_in", "w_out", "w_up", "w_down")
    lands = _cast_place([w[n] for n in big], chip_arr)
    start_a, wait_a = _gather_plan([w[n].shape[0] for n in big[:2]])
    start_b, wait_b = _gather_plan([w[n].shape[0] for n in big[2:]])

    send_a, recv_a, _, lands_a, token_a = _split_start([], lands[:2], start_a, 6, pos_arr, "gather_start_a")

    row0 = jnp.concatenate([c + token_a[0:1, 0:1], w["conv_w"].reshape(1, -1), w["ffn_conv_w"].reshape(1, -1)],
                           axis=1)
    g0 = _allgather8(row0, "gather_cond")[:, 0, :]
    c8 = g0[:, :D_MODEL]
    per_chip = g0[0::2]
    conv_w_full = per_chip[:, D_MODEL:D_MODEL + 512].reshape(N_CHIPS, 4, 128).transpose(1, 0, 2).reshape(4, 512)
    ffn_conv_w_full = per_chip[:, D_MODEL + 512:].reshape(N_CHIPS, 3, 1536).transpose(1, 0, 2).reshape(3, 2 * D_FF)
    mod_parts = _allgather8(_ada_fwd(c8, w["w_ada"]), "gather_mod")
    send_b, recv_b, _, lands_b, token_b = _split_start([], lands[2:], start_b, 6, mod_parts, "gather_start_b")
    mod = _mod_select(mod_parts, w["b_ada"].reshape(1, -1), me_arr, token_b).reshape(N_MOD, D_MODEL)
    sh_m, sc_m, gt_m, sh_f, sc_f, gt_f = [mod[k:k + 1] for k in range(N_MOD)]

    small = {n: w[n] for n in SMALL_REPLICATED}
    small["conv_w"] = conv_w_full
    small["ffn_conv_w"] = ffn_conv_w_full
    row = lambda a: a.reshape(1, -1)
    seq_params, ws_t = _seq_params(small)
    glo, ggo = row(small["g_lru_out"]), row(small["g_gmlp_out"])
    g_pre, g_post = row(small["g_mix_pre"]), row(small["g_mix_post"])
    g_pre2, g_post2 = row(small["g_ffn_pre"]), row(small["g_ffn_post"])
    fw, fb = small["ffn_conv_w"], row(small["ffn_conv_b"])
    xs, tgt = x[0], loss_target[0]

    _, lands_a = _split_wait(send_a, recv_a, [], lands_a, wait_a, mod, "gather_wait_a")
    w_in4, w_out4 = _forward_to_sibling(lands_a, "forward_a")
    w_out_b = w_out4.reshape(D_MODEL, D_MODEL)
    z, h = _mix_in(xs, sc_m, sh_m, g_pre, w_in4)
    ycat, hst, stash = _seqmix(z, seq_params, glo, ggo)
    _, lands_b = _split_wait(send_b, recv_b, [], lands_b, wait_b, ycat, "gather_wait_b")
    fwd_start, fwd_wait = _forward_plan([w[n].shape[0] for n in big[2:]])
    fwd_send, fwd_recv, _, lands_b, tok = _split_start([], lands_b, fwd_start, 6, pos_arr, "forward_start_b")
    y, x1, h2 = _mix_out(ycat, xs, w_out_b, gt_m + tok[0:1, 0:1], g_post, g_pre2, sc_f, sh_f)
    _, (w_up4, w_down4) = _split_wait(fwd_send, fwd_recv, [], lands_b, fwd_wait, h2, "forward_wait_b")
    w_down_b = w_down4.reshape(D_FF, D_MODEL)
    up0, pre, act, dy2, dx2, loss, dgt_f, dg_post2 = _ffn_fwd(h2, x1, tgt, w_up4, w_down_b, fw, fb, gt_f, g_post2)

    dup0, dfw, dfb = _ffn_bwd_a(dy2, pre, up0, w_down_b, fw)
    gw_up = _wgrad(h2, dup0, N_CHIPS, "wgrad_up", True)
    gw_down = _wgrad(act, dy2, 2, "wgrad_down", False)
    ex_start, ex_wait = _exchange_plan(2)
    sg_start, sg_wait = _swap_gathered_plan(2)
    grads, deltas, new_m, new_v = {}, {}, {}, {}

    def swap_start(parts, name):
        sw_start, sw_wait = _swap_halves_plan([p.shape[1] // 2 for p in parts])
        recv = [lax.empty((N_CHIPS, p.shape[1] // 2, p.shape[2]), BF16) for p in parts]
        send_s, recv_s, parts, recv, token = _split_start(parts, recv, sw_start, len(parts), pos_arr,
                                                           "swap_start_" + name)
        return (send_s, recv_s, parts, recv, sw_wait), token

    def exchange_start(swap, tags, after, name):
        send_s, recv_s, parts, recv, sw_wait = swap
        parts, recv = _split_wait(send_s, recv_s, parts, recv, sw_wait, after, "swap_wait_" + name)
        both = [_chip_sum(p, r, pos_arr, "chip_sum_" + t) for p, r, t in zip(parts, recv, tags)]
        sums, gath = [b[0] for b in both], [b[1] for b in both]
        return _split_start(sums, gath, ex_start, 3 * len(parts), pos_arr, "exchange_start_" + name)

    def gathered_start(exchange, after, name):
        send_s, recv_s, sums, gath, _ = exchange
        _, gath = _split_wait(send_s, recv_s, sums, gath, ex_wait, after, "exchange_wait_" + name)
        send_s, recv_s, _, gath, token = _split_start([], gath, sg_start, len(gath), pos_arr,
                                                      "gathered_start_" + name)
        return (send_s, recv_s, gath), token

    def gathered_wait(gathered, after, name):
        send_s, recv_s, gath = gathered
        return _split_wait(send_s, recv_s, [], gath, sg_wait, after, "gathered_wait_" + name)[1]

    def adam_big(t, gath, after):
        grads[t], deltas[t], new_m[t], new_v[t] = _adam_gathered(w[t], gath, m[t], v[t], c_arr, after, "adam_" + t)

    def behind(value, token):
        return value + token[0:1, 0:1]

    tags_b, tags_a = ("w_up", "w_down"), ("w_in", "w_out")
    swap_b, tok = swap_start([gw_up, gw_down.reshape(N_CHIPS, -1, D_MODEL)], "b")
    dx1, dy, dsh_f, dsc_f, dg_pre2, dgt_m, dg_post = _ffn_bwd_b(
        dup0, x1, y, dx2, w_up4, g_pre2, behind(sc_f, tok), sh_f, gt_m, g_post)
    exchange_b = exchange_start(swap_b, tags_b, dg_post, "b")
    (dz, dcw, dcb, dwr, dwi, dbr, dbi, dspa, dng, dnb, dws, dbs_t, dglo, dggo) = _seqmix_bwd(
        z, hst, stash, dy, w_out_b, seq_params, ws_t, behind(glo, exchange_b[4]), ggo)
    gw_in = _wgrad(h, dz, N_CHIPS, "wgrad_in", True)
    gw_out = _wgrad(ycat, dy, 1, "wgrad_out", False)
    swap_a, tok = swap_start([gw_in, gw_out.reshape(N_CHIPS, -1, D_MODEL)], "a")
    grad_x, dsh_m, dsc_m, dg_pre = _mix_in_bwd(xs, dz, dx1, w_in4, g_pre, behind(sc_m, tok))

    dmod = jnp.concatenate([behind(dsh_m, tok), dsc_m, dgt_m, dsh_f, dsc_f, dgt_f], axis=1)
    dmod8 = _allgather8(dmod, "gather_dmod")[:, 0, :]
    small_grads = dict(
        g_mix_pre=dg_pre, g_mix_post=dg_post, conv_w=dcw, conv_b=dcb, w_rgate=dwr, b_rgate=dbr, w_igate=dwi,
        b_igate=dbi, lru_a=dspa, v_norm_g=dng, v_norm_b=dnb, w_spatial=dws, b_spatial=dbs_t, g_lru_out=dglo,
        g_gmlp_out=dggo, g_ffn_pre=dg_pre2, g_ffn_post=dg_post2, ffn_conv_w=dfw, ffn_conv_b=dfb,
        loss=loss)
    g_small = _allreduce_small(_pack_small(small_grads, dmod8), "reduce_small")
    total = g_small[_small_pieces("loss", 0, 1)[0][:3]]
    exchange_a = exchange_start(swap_a, tags_a, g_small, "a")
    gathered_b, tok = gathered_start(exchange_b, exchange_a[4], "b")

    grads["w_ada"], deltas["w_ada"], new_m["w_ada"], new_v["w_ada"], g_b_ada = _ada_bwd(
        c8, behind(dmod8, tok), chip_arr, w["w_ada"], m["w_ada"], v["w_ada"])
    rep = SMALL_REPLICATED
    small_out = _adam_small(g_small, {n: args[n] for n in rep}, {n: args["m_" + n] for n in rep},
                            {n: args["v_" + n] for n in rep})
    for n in rep:
        grads[n], deltas[n], new_m[n], new_v[n] = [group[n] for group in small_out]
    for n in SMALL_COLUMN_SHARDED:
        grads[n], deltas[n], new_m[n], new_v[n] = _adam_cols(n, g_small, args[n], args["m_" + n],
                                                             args["v_" + n], chip_arr)
    d_b, m_b, v_b = _adam(w["b_ada"], g_b_ada, m["b_ada"], v["b_ada"], "adam_b_ada")
    grads["b_ada"], deltas["b_ada"], new_m["b_ada"], new_v["b_ada"] = g_b_ada, d_b, m_b, v_b

    gath_up, gath_down = gathered_wait(gathered_b, d_b, "b")
    adam_big("w_down", gath_down, pos_arr)
    gathered_a, tok = gathered_start(exchange_a, deltas["w_down"], "a")
    adam_big("w_up", gath_up, tok)
    gath_in, gath_out = gathered_wait(gathered_a, deltas["w_up"], "a")
    adam_big("w_in", gath_in, pos_arr)
    adam_big("w_out", gath_out, pos_arr)

    outs = [total, grad_x[None]]
    for group in (grads, deltas, new_m, new_v):
        outs.extend(group[n].reshape(args[n].shape) for n in names)
    return tuple(outs)
```

```python
import functools
import math

import jax
import jax.numpy as jnp
from jax import lax
from jax.experimental import pallas as pl
from jax.experimental.pallas import tpu as pltpu

F32 = jnp.float32
BF16 = jnp.bfloat16
MESH = pl.DeviceIdType.MESH

D_MODEL = 1024
LRU_WIDTH = 512
LRU_HEADS = 8
GMLP_GROUPS = 4
GMLP_BLOCK = 128
CHUNK = 64
D_FF = 3072
N_MOD = 6
EPS = 1e-6
LRU_C = 8.0
N_CHIPS = 4
N_DEV = 8

ADAM_LR = 0.001
ADAM_B1 = 0.9
ADAM_B2 = 0.999
ADAM_EPS = 1e-08
ADAM_WD = 0.01
ADAM_STEP = 10

GELU_C0 = math.sqrt(2.0 / math.pi)
GELU_C1 = 0.044715

VMEM_LIMIT_BYTES = 56 * 1024 * 1024
SUBLANES = 8
BF16_SUBLANES = 16
FFN_CHUNK = 768
SUB_ROWS = 256


def _gelu_gate(x):
    x2 = x * x
    z = x * ((2.0 * GELU_C0 * GELU_C1) * x2 + 2.0 * GELU_C0)
    return 1.0 / (1.0 + jnp.exp(-z)), x2


def _gelu(x):
    t = jnp.tanh(GELU_C0 * (x + GELU_C1 * x * x * x))
    return 0.5 * x * (1.0 + t)


def _gelu_and_grad(x):
    s, x2 = _gelu_gate(x)
    g = x * s
    dz = (6.0 * GELU_C0 * GELU_C1) * x2 + 2.0 * GELU_C0
    return g, s + g * (1.0 - s) * dz


def _sigmoid(x):
    return 1.0 / (1.0 + jnp.exp(-x))


def _log1p(u):
    w = 1.0 + u
    return jnp.where(w == 1.0, u, jnp.log(w) * (u / (w - 1.0)))


def _softplus(x):
    return jnp.maximum(x, 0.0) + _log1p(jnp.exp(-jnp.abs(x)))


def _neg_expm1(x):
    u = jnp.exp(x)
    um1 = u - 1.0
    tiny = um1 == 0.0
    small = um1 * (x / jnp.log(jnp.where(tiny, 2.0, jnp.maximum(u, 0.25))))
    return -jnp.where(tiny, x, jnp.where(x < -1.0, um1, small))


def _msq_rsqrt(v):
    return lax.rsqrt(jnp.mean(v * v, axis=-1, keepdims=True) + EPS)


def _rms_bwd(dyn, yn, r):
    return r * (dyn - yn * jnp.mean(dyn * yn, axis=-1, keepdims=True))


def _colsum(v):
    return jnp.sum(v, axis=0, keepdims=True)


def _shift_down(cur, prev8, k):
    rolled = pltpu.roll(cur, k, 0)
    head = pltpu.roll(prev8, k, 0)
    row8 = lax.broadcasted_iota(jnp.int32, (SUBLANES, cur.shape[1]), 0)
    first = jnp.where(row8 < k, head, rolled[0:SUBLANES])
    return jnp.concatenate([first, rolled[SUBLANES:]], axis=0)


def _shift_up(cur, next8, k):
    t = cur.shape[0]
    rolled = pltpu.roll(cur, t - k, 0)
    tail = pltpu.roll(next8, SUBLANES - k, 0)
    row8 = lax.broadcasted_iota(jnp.int32, (SUBLANES, cur.shape[1]), 0)
    last = jnp.where(row8 >= SUBLANES - k, tail, rolled[t - SUBLANES:])
    return jnp.concatenate([rolled[:t - SUBLANES], last], axis=0)


def _scan_fwd(a, b):
    t = a.shape[0]
    row = lax.broadcasted_iota(jnp.int32, a.shape, 0)
    d = 1
    while d < t:
        keep = row >= d
        a_s = jnp.where(keep, pltpu.roll(a, d, 0), 1.0)
        b_s = jnp.where(keep, pltpu.roll(b, d, 0), 0.0)
        b = a * b_s + b
        a = a * a_s
        d *= 2
    return a, b


def _scan_bwd(a, g):
    t = a.shape[0]
    row = lax.broadcasted_iota(jnp.int32, a.shape, 0)
    d = 1
    while d < t:
        keep = row < t - d
        a_s = jnp.where(keep, pltpu.roll(a, t - d, 0), 1.0)
        g_s = jnp.where(keep, pltpu.roll(g, t - d, 0), 0.0)
        g = a * g_s + g
        a = a * a_s
        d *= 2
    return a, g


def _dot(a, b):
    return jnp.dot(a, b, preferred_element_type=F32)


def _dot_nt(a, b):
    return lax.dot_general(a, b, (((1,), (1,)), ((), ())), preferred_element_type=F32)


def _dot_tn(a, b):
    return lax.dot_general(a, b, (((0,), (0,)), ((), ())), preferred_element_type=F32)


def _rows(ts, cols, rev_of=None):
    if rev_of is None:
        return pl.BlockSpec((ts, cols), lambda i: (i, 0))
    return pl.BlockSpec((ts, cols), lambda i: (rev_of - 1 - i, 0))


def _halo_prev(ts, cols, halo, rev_of=None, col_block=0):
    per = ts // halo
    if rev_of is None:
        return pl.BlockSpec((halo, cols), lambda i: (jnp.maximum(i * per - 1, 0), col_block))
    return pl.BlockSpec((halo, cols), lambda i: (jnp.maximum((rev_of - 1 - i) * per - 1, 0), col_block))


def _full(shape):
    nd = len(shape)
    return pl.BlockSpec(shape, lambda *_: (0,) * nd)


_RESIDENT = pl.BlockSpec(memory_space=pltpu.VMEM)


def _params(sem):
    return pltpu.CompilerParams(dimension_semantics=sem, vmem_limit_bytes=VMEM_LIMIT_BYTES)


def _sds(shape, dtype):
    return jax.ShapeDtypeStruct(shape, dtype)


def _sub_tiles(ts):
    return [slice(r0, r0 + SUB_ROWS) for r0 in range(0, ts, SUB_ROWS)]


def _mix_in(x, sc, sh, g, w_in4, ts=512):
    s, d = x.shape

    def body(x_ref, sc_ref, sh_ref, g_ref, w_ref, z_ref, h_ref):
        for rs in _sub_tiles(ts):
            xv = x_ref[rs, :]
            h = (xv * _msq_rsqrt(xv) * g_ref[...]) * (1.0 + sc_ref[...]) + sh_ref[...]
            hb = h.astype(BF16)
            h_ref[rs, :] = hb
            for k in range(N_CHIPS):
                z_ref[rs, k * 512:(k + 1) * 512] = _dot(hb, w_ref[k])

    return pl.pallas_call(
        body, grid=(s // ts,), name="mix_in",
        in_specs=[_rows(ts, d), _full((1, d)), _full((1, d)), _full((1, d)), _full(w_in4.shape)],
        out_specs=[_rows(ts, 2048), _rows(ts, d)],
        out_shape=[_sds((s, 2048), F32), _sds((s, d), BF16)],
        compiler_params=_params(("parallel",)),
    )(x, sc, sh, g, w_in4)


N_STASH = 12
(ST_XC, ST_R, ST_IG, ST_A, ST_MULT, ST_GL, ST_DGL, ST_U, ST_DU, ST_Q, ST_VHAT, ST_SPB) = range(N_STASH)


def _seq_param_specs():
    return [_full((4, 512)), _full((1, 512)), _full((512, 512)), _full((512, 512)), _full((1, 512)),
            _full((1, 512)), _full((1, 512)), _full((1, 512)), _full((1, 512)), _full((4, 128, 128)),
            _full((128, 4))]


def _seqmix(z, seq_params, glo, ggo, ts=256):
    s = z.shape[0]
    nt = s // ts

    def body(z_ref, zprev_ref, cw_ref, cb_ref, bdr_ref, bdi_ref, br_ref, bi_ref, la_ref, ng_ref, nb_ref,
             ws_ref, bst_ref, glo_ref, ggo_ref, ycat_ref, hst_ref, st_ref, hcarry, sp_scr):
        i = pl.program_id(0)

        @pl.when(i == 0)
        def _():
            hcarry[...] = jnp.zeros_like(hcarry)

        lx = z_ref[:, 0:512]
        prev8 = jnp.where(i == 0, 0.0, zprev_ref[...])
        xc = (cw_ref[3:4, :] * lx + cw_ref[2:3, :] * _shift_down(lx, prev8, 1)
              + cw_ref[1:2, :] * _shift_down(lx, prev8, 2) + cw_ref[0:1, :] * _shift_down(lx, prev8, 3)
              + cb_ref[...])
        xcb = xc.astype(BF16)
        r = _sigmoid(_dot(xcb, bdr_ref[...]) + br_ref[...])
        ig = _sigmoid(_dot(xcb, bdi_ref[...]) + bi_ref[...])
        log_a = (-LRU_C) * r * _softplus(-la_ref[...])
        a = jnp.exp(log_a)
        mult = jnp.sqrt(_neg_expm1(2.0 * log_a))
        acum, hloc = _scan_fwd(a, mult * (ig * xc))
        h = hloc + acum * hcarry[...]
        hcarry[...] = h[ts - 1:ts, :]
        hst_ref[...] = h
        gl, dgl = _gelu_and_grad(z_ref[:, 512:1024])
        y_l = h * gl
        for slot, val in ((ST_XC, xc), (ST_R, r), (ST_IG, ig), (ST_A, a), (ST_MULT, mult), (ST_GL, gl),
                          (ST_DGL, dgl)):
            st_ref[slot] = val

        u, du = _gelu_and_grad(z_ref[:, 1024:1536])
        vg, dvg = _gelu_and_grad(z_ref[:, 1536:2048])
        vc = vg - jnp.mean(vg, axis=-1, keepdims=True)
        rstd = lax.rsqrt(jnp.mean(vc * vc, axis=-1, keepdims=True) + EPS)
        vhat = vc * rstd
        vb = (vhat * ng_ref[...] + nb_ref[...]).astype(BF16)
        for n in range(ts // GMLP_BLOCK):
            rs = slice(n * GMLP_BLOCK, (n + 1) * GMLP_BLOCK)
            for g in range(GMLP_GROUPS):
                cs = slice(g * 128, (g + 1) * 128)
                sp_scr[rs, cs] = _dot(ws_ref[g], vb[rs, cs]) + bst_ref[:, g:g + 1]
        spb = sp_scr[...]
        y_g = u * spb
        for slot, val in ((ST_U, u), (ST_DU, du), (ST_Q, rstd * dvg), (ST_VHAT, vhat), (ST_SPB, spb)):
            st_ref[slot] = val

        ycat_ref[:, 0:512] = (y_l * _msq_rsqrt(y_l) * glo_ref[...]).astype(BF16)
        ycat_ref[:, 512:1024] = (y_g * _msq_rsqrt(y_g) * ggo_ref[...]).astype(BF16)

    return pl.pallas_call(
        body, grid=(nt,), name="seqmix",
        in_specs=[_rows(ts, 2048), _halo_prev(ts, 512, SUBLANES)] + _seq_param_specs()
        + [_full((1, 512)), _full((1, 512))],
        out_specs=[_rows(ts, 1024), _rows(ts, 512), pl.BlockSpec((N_STASH, ts, 512), lambda i: (0, i, 0))],
        out_shape=[_sds((s, 1024), BF16), _sds((s, 512), F32), _sds((N_STASH, s, 512), F32)],
        scratch_shapes=[pltpu.VMEM((1, 512), F32), pltpu.VMEM((ts, 512), F32)],
        compiler_params=_params(("arbitrary",)),
    )(z, z, *seq_params, glo, ggo)


def _mix_out(ycat, x, w_out, gt_m, g_post, g_pre2, sc_f, sh_f, ts=512):
    s, d = x.shape

    def body(yc_ref, x_ref, w_ref, gt_ref, gp_ref, g2_ref, sc_ref, sh_ref, y_ref, x1_ref, h2_ref):
        for rs in _sub_tiles(ts):
            y = _dot(yc_ref[rs, :], w_ref[...])
            y_ref[rs, :] = y
            x1 = x_ref[rs, :] + gt_ref[...] * (y * _msq_rsqrt(y) * gp_ref[...])
            x1_ref[rs, :] = x1
            h2 = (x1 * _msq_rsqrt(x1) * g2_ref[...]) * (1.0 + sc_ref[...]) + sh_ref[...]
            h2_ref[rs, :] = h2.astype(BF16)

    vec = _full((1, d))
    return pl.pallas_call(
        body, grid=(s // ts,), name="mix_out",
        in_specs=[_rows(ts, d), _rows(ts, d), _full((d, d)), vec, vec, vec, vec, vec],
        out_specs=[_rows(ts, d), _rows(ts, d), _rows(ts, d)],
        out_shape=[_sds((s, d), F32), _sds((s, d), F32), _sds((s, d), BF16)],
        compiler_params=_params(("parallel",)),
    )(ycat, x, w_out, gt_m, g_post, g_pre2, sc_f, sh_f)


def _ffn_cols(j):
    per = (2 * D_FF // N_CHIPS) // FFN_CHUNK
    return j // per, (j % per) * FFN_CHUNK, j * FFN_CHUNK


def _ffn_fwd(h2, x1, tgt, w_up4, w_down, fw, fb, gt_f, g_post, ts=256):
    s, d = x1.shape
    nch = D_FF // FFN_CHUNK

    def body(h2_ref, x1_ref, tgt_ref, wup_ref, wdn_ref, fw_ref, fb_ref, gt_ref, gp_ref,
             up0_ref, pre_ref, act_ref, dy2_ref, dx2_ref, loss_ref, dgt_ref, dgp_ref, tail_ref):
        i = pl.program_id(0)

        @pl.when(i == 0)
        def _():
            tail_ref[...] = jnp.zeros_like(tail_ref)
            loss_ref[...] = jnp.zeros_like(loss_ref)
            dgt_ref[...] = jnp.zeros_like(dgt_ref)
            dgp_ref[...] = jnp.zeros_like(dgp_ref)

        hb = h2_ref[...]

        def up_project(j):
            sh_g, off, _ = _ffn_cols(j)
            return [_dot(hb, wup_ref[shard, :, off:off + FFN_CHUNK]).astype(BF16) for shard in (sh_g, sh_g + 2)]

        y2 = jnp.zeros((ts, d), F32)
        ahead = up_project(0)
        for j in range(nch):
            _, _, col = _ffn_cols(j)
            ubs = ahead
            if j + 1 < nch:
                ahead = up_project(j + 1)
            halves = []
            for ub, c0 in zip(ubs, (col, D_FF + col)):
                cs = slice(c0, c0 + FFN_CHUNK)
                up0_ref[:, cs] = ub
                u = ub.astype(F32)
                prev8 = tail_ref[:, cs]
                tail_ref[:, cs] = u[ts - SUBLANES:, :]
                halves.append(fw_ref[2:3, cs] * u + fw_ref[1:2, cs] * _shift_down(u, prev8, 1)
                              + fw_ref[0:1, cs] * _shift_down(u, prev8, 2) + fb_ref[:, cs])
                pre_ref[:, cs] = halves[-1].astype(BF16)
            act = (_gelu(halves[0]) * halves[1]).astype(BF16)
            act_ref[:, col:col + FFN_CHUNK] = act
            y2 = y2 + _dot(act, wdn_ref[col:col + FFN_CHUNK, :])
        r2 = _msq_rsqrt(y2)
        yn = y2 * r2
        yng = yn * gp_ref[...]
        e = x1_ref[...] + gt_ref[...] * yng - tgt_ref[...]
        loss_ref[...] += jnp.sum(e * e) * (0.5 / d)
        dx2 = e * (1.0 / d)
        dx2_ref[...] = dx2
        dgt_ref[...] += _colsum(dx2 * yng)
        dyng = dx2 * gt_ref[...]
        dgp_ref[...] += _colsum(dyng * yn)
        dy2_ref[...] = _rms_bwd(dyng * gp_ref[...], yn, r2).astype(BF16)

    vec = _full((1, d))
    return pl.pallas_call(
        body, grid=(s // ts,), name="ffn_fwd",
        in_specs=[_rows(ts, d), _rows(ts, d), _rows(ts, d), _RESIDENT, _RESIDENT,
                  _full((3, 2 * D_FF)), _full((1, 2 * D_FF)), vec, vec],
        out_specs=[_rows(ts, 2 * D_FF), _rows(ts, 2 * D_FF), _rows(ts, D_FF), _rows(ts, d), _rows(ts, d),
                   _full((1, 128)), vec, vec],
        out_shape=[_sds((s, 2 * D_FF), BF16), _sds((s, 2 * D_FF), BF16), _sds((s, D_FF), BF16), _sds((s, d), BF16),
                   _sds((s, d), F32), _sds((1, 128), F32), _sds((1, d), F32), _sds((1, d), F32)],
        scratch_shapes=[pltpu.VMEM((SUBLANES, 2 * D_FF), F32)],
        compiler_params=_params(("arbitrary",)),
    )(h2, x1, tgt, w_up4, w_down, fw, fb, gt_f, g_post)


def _shift_up_mxu(vb, up_mat, next8, k):
    t = vb.shape[0]
    main = _dot(up_mat, vb)
    tail = pltpu.roll(next8, SUBLANES - k, 0)
    row8 = lax.broadcasted_iota(jnp.int32, next8.shape, 0)
    last = main[t - SUBLANES:] + jnp.where(row8 >= SUBLANES - k, tail, 0.0)
    return jnp.concatenate([main[:t - SUBLANES], last], axis=0)


def _ffn_bwd_a(dy2, pre, up0, w_down, fw, ts=256):
    s, d = dy2.shape
    nt = s // ts
    nch = D_FF // FFN_CHUNK
    wide = 2 * D_FF
    up_mats = jnp.stack([jnp.eye(ts, k=1, dtype=BF16), jnp.eye(ts, k=2, dtype=BF16)])

    def body(dy2_ref, pre_ref, up0_ref, wdn_ref, fw_ref, um_ref, dup0_ref, dfw_ref, dfb_ref, next_ref):
        i = pl.program_id(0)

        @pl.when(i == 0)
        def _():
            next_ref[...] = jnp.zeros_like(next_ref)
            dfw_ref[...] = jnp.zeros_like(dfw_ref)
            dfb_ref[...] = jnp.zeros_like(dfb_ref)

        dyb = dy2_ref[...]
        for j in range(nch):
            _, _, col = _ffn_cols(j)
            dact = _dot_nt(dyb, wdn_ref[col:col + FFN_CHUNK, :])
            gl, dgl = _gelu_and_grad(pre_ref[:, col:col + FFN_CHUNK].astype(F32))
            dpre = (dact * pre_ref[:, D_FF + col:D_FF + col + FFN_CHUNK].astype(F32) * dgl, dact * gl)
            for half, c0 in enumerate((col, D_FF + col)):
                cs = slice(c0, c0 + FFN_CHUNK)
                dp = dpre[half]
                dpb = dp.astype(BF16)
                nxt = next_ref[:, cs]
                next_ref[:, cs] = dpb.astype(F32)[0:SUBLANES, :]
                su1 = _shift_up_mxu(dpb, um_ref[0], nxt, 1)
                su2 = _shift_up_mxu(dpb, um_ref[1], nxt, 2)
                u = up0_ref[:, cs].astype(F32)
                dfb_ref[:, cs] += _colsum(dp)
                dfw_ref[2:3, cs] += _colsum(dp * u)
                dfw_ref[1:2, cs] += _colsum(su1 * u)
                dfw_ref[0:1, cs] += _colsum(su2 * u)
                dup0 = fw_ref[2:3, cs] * dp + fw_ref[1:2, cs] * su1 + fw_ref[0:1, cs] * su2
                dup0_ref[:, cs] = dup0.astype(BF16)

    return pl.pallas_call(
        body, grid=(nt,), name="ffn_bwd_a",
        in_specs=[_rows(ts, d, nt), _rows(ts, wide, nt), _rows(ts, wide, nt), _RESIDENT,
                  _full((3, wide)), _full((2, ts, ts))],
        out_specs=[_rows(ts, wide, nt), _full((3, wide)), _full((1, wide))],
        out_shape=[_sds((s, wide), BF16), _sds((3, wide), F32), _sds((1, wide), F32)],
        scratch_shapes=[pltpu.VMEM((SUBLANES, wide), F32)],
        compiler_params=_params(("arbitrary",)),
    )(dy2, pre, up0, w_down, fw, up_mats)


def _ffn_bwd_b(dup0, x1, y, dx2, w_up4, g_pre2, sc_f, sh_f, gt_m, g_post_m, ts=512):
    s, d = x1.shape
    shard_cols = 2 * D_FF // N_CHIPS

    def body(dup_ref, x1_ref, y_ref, dx2_ref, wup_ref, g2_ref, sc_ref, sh_ref, gt_ref, gp_ref,
             dx1_ref, dy_ref, dsh_ref, dsc_ref, dg2_ref, dgt_ref, dgp_ref):
        i = pl.program_id(0)

        @pl.when(i == 0)
        def _():
            for ref in (dsh_ref, dsc_ref, dg2_ref, dgt_ref, dgp_ref):
                ref[...] = jnp.zeros_like(ref)

        for rs in _sub_tiles(ts):
            dh2 = jnp.zeros((SUB_ROWS, d), F32)
            for k in range(N_CHIPS):
                dh2 = dh2 + _dot_nt(dup_ref[rs, k * shard_cols:(k + 1) * shard_cols], wup_ref[k])
            x1v = x1_ref[rs, :]
            r2 = _msq_rsqrt(x1v)
            xn = x1v * r2
            hn = xn * g2_ref[...]
            dsh_ref[...] += _colsum(dh2)
            dsc_ref[...] += _colsum(dh2 * hn)
            dhn = dh2 * (1.0 + sc_ref[...])
            dg2_ref[...] += _colsum(dhn * xn)
            dx1 = dx2_ref[rs, :] + _rms_bwd(dhn * g2_ref[...], xn, r2)
            dx1_ref[rs, :] = dx1
            yv = y_ref[rs, :]
            ry = _msq_rsqrt(yv)
            yn = yv * ry
            dgt_ref[...] += _colsum(dx1 * (yn * gp_ref[...]))
            dyng = dx1 * gt_ref[...]
            dgp_ref[...] += _colsum(dyng * yn)
            dy_ref[rs, :] = _rms_bwd(dyng * gp_ref[...], yn, ry).astype(BF16)

    vec = _full((1, d))
    return pl.pallas_call(
        body, grid=(s // ts,), name="ffn_bwd_b",
        in_specs=[_rows(ts, 2 * D_FF), _rows(ts, d), _rows(ts, d), _rows(ts, d), _RESIDENT,
                  vec, vec, vec, vec, vec],
        out_specs=[_rows(ts, d), _rows(ts, d), vec, vec, vec, vec, vec],
        out_shape=[_sds((s, d), F32), _sds((s, d), BF16)] + [_sds((1, d), F32)] * 5,
        compiler_params=_params(("arbitrary",)),
    )(dup0, x1, y, dx2, w_up4, g_pre2, sc_f, sh_f, gt_m, g_post_m)


def _seqmix_bwd(z, hst, stash, dy, w_out, seq_params, ws_t, glo, ggo, ts=256):
    s = z.shape[0]
    nt = s // ts
    small_shapes = [(4, 512), (1, 512), (512, 512), (512, 512), (1, 512), (1, 512), (1, 512),
                    (1, 512), (1, 512), (4, 128, 128), (128, 4), (1, 512), (1, 512)]

    def body(lx_ref, hst_ref, hprev_ref, st_ref, dy_ref, wout_ref, cw_ref, cb_ref, bdr_ref, bdi_ref, br_ref,
             bi_ref, la_ref, ng_ref, nb_ref, ws_ref, bst_ref, wst_ref, glo_ref, ggo_ref, dz_ref, *rest):
        small_refs = rest[:13]
        (dcw_ref, dcb_ref, dwr_ref, dwi_ref, dbr_ref, dbi_ref, dspa_ref, dng_ref, dnb_ref, dws_ref, dbs_ref,
         dglo_ref, dggo_ref) = small_refs
        gcarry, anext, dxcnext, dv_scr = rest[13:]
        i = pl.program_id(0)

        @pl.when(i == 0)
        def _():
            for ref in small_refs:
                ref[...] = jnp.zeros_like(ref)
            gcarry[...] = jnp.zeros_like(gcarry)
            anext[...] = jnp.ones_like(anext)
            dxcnext[...] = jnp.zeros_like(dxcnext)

        first_tile = i == nt - 1
        xc, r, ig, a, mult = st_ref[ST_XC], st_ref[ST_R], st_ref[ST_IG], st_ref[ST_A], st_ref[ST_MULT]
        gl, u, spb, vhat = st_ref[ST_GL], st_ref[ST_U], st_ref[ST_SPB], st_ref[ST_VHAT]
        lx = lx_ref[...]
        h = hst_ref[...]
        hprev = _shift_down(h, jnp.where(first_tile, 0.0, hprev_ref[...]), 1)
        y_l = h * gl
        y_g = u * spb

        dycat = _dot_nt(dy_ref[...], wout_ref[...])

        def emit_dz(k, val):
            dz_ref[:, k * 512:(k + 1) * 512] = val.astype(BF16)

        rl = _msq_rsqrt(y_l)
        yln = y_l * rl
        dyl = dycat[:, 0:512]
        dglo_ref[...] += _colsum(dyl * yln)
        dy_l = _rms_bwd(dyl * glo_ref[...], yln, rl)
        rg = _msq_rsqrt(y_g)
        ygn = y_g * rg
        dyg = dycat[:, 512:1024]
        dggo_ref[...] += _colsum(dyg * ygn)
        dy_g = _rms_bwd(dyg * ggo_ref[...], ygn, rg)

        emit_dz(1, dy_l * h * st_ref[ST_DGL])
        a_up = _shift_up(a, anext[...], 1)
        acum, gloc = _scan_bwd(a_up, dy_l * gl)
        gg = gloc + acum * gcarry[...]
        gcarry[...] = gg[0:1, :]
        anext[...] = a[0:SUBLANES, :]
        da = gg * hprev
        t1 = gg * mult
        di = t1 * xc
        dxc = t1 * ig
        dmult = gg * ig * xc
        dla = da * a - dmult * (a * a / mult)
        dspa_ref[...] += _colsum(dla * r) * (-LRU_C)
        dpr = dla * ((-LRU_C) * _softplus(-la_ref[...])) * r * (1.0 - r)
        dpi = di * ig * (1.0 - ig)
        dbr_ref[...] += _colsum(dpr)
        dbi_ref[...] += _colsum(dpi)
        dprb = dpr.astype(BF16)
        dpib = dpi.astype(BF16)
        xcb = xc.astype(BF16)
        dwr_ref[...] += _dot_tn(xcb, dprb)
        dwi_ref[...] += _dot_tn(xcb, dpib)
        dxc = dxc + _dot_nt(dprb, bdr_ref[...]) + _dot_nt(dpib, bdi_ref[...])
        nxt = dxcnext[...]
        dxcnext[...] = dxc[0:SUBLANES, :]
        up1, up2, up3 = _shift_up(dxc, nxt, 1), _shift_up(dxc, nxt, 2), _shift_up(dxc, nxt, 3)
        dcb_ref[...] += _colsum(dxc)
        dcw_ref[3:4, :] += _colsum(dxc * lx)
        dcw_ref[2:3, :] += _colsum(up1 * lx)
        dcw_ref[1:2, :] += _colsum(up2 * lx)
        dcw_ref[0:1, :] += _colsum(up3 * lx)
        dlx = cw_ref[3:4, :] * dxc + cw_ref[2:3, :] * up1 + cw_ref[1:2, :] * up2 + cw_ref[0:1, :] * up3
        emit_dz(0, dlx)

        emit_dz(2, dy_g * spb * st_ref[ST_DU])
        dsp = dy_g * u
        vb = (vhat * ng_ref[...] + nb_ref[...]).astype(BF16)
        for n in range(ts // GMLP_BLOCK):
            rs = slice(n * GMLP_BLOCK, (n + 1) * GMLP_BLOCK)
            for g in range(GMLP_GROUPS):
                cs = slice(g * 128, (g + 1) * 128)
                dbs_ref[:, g:g + 1] += jnp.sum(dsp[rs, cs], axis=1, keepdims=True)
                blk = dsp[rs, cs].astype(BF16)
                dws_ref[g] += _dot_nt(blk, vb[rs, cs])
                dv_scr[rs, cs] = _dot(wst_ref[g], blk)
        dv = dv_scr[...]
        dng_ref[...] += _colsum(dv * vhat)
        dnb_ref[...] += _colsum(dv)
        dvh = dv * ng_ref[...]
        dvg = dvh - jnp.mean(dvh, axis=-1, keepdims=True) - vhat * jnp.mean(dvh * vhat, axis=-1, keepdims=True)
        emit_dz(3, dvg * st_ref[ST_Q])

        @pl.when(i == nt - 1)
        def _():
            pos = lax.broadcasted_iota(jnp.int32, (GMLP_BLOCK, GMLP_BLOCK), 0) // CHUNK
            src = lax.broadcasted_iota(jnp.int32, (GMLP_BLOCK, GMLP_BLOCK), 1) // CHUNK
            for g in range(GMLP_GROUPS):
                dws_ref[g] = jnp.where(src <= pos, dws_ref[g], 0.0)
            dspa_ref[...] = dspa_ref[...] * (-_sigmoid(-la_ref[...]))

    in_specs = ([_rows(ts, 512, nt), _rows(ts, 512, nt), _halo_prev(ts, 512, SUBLANES, nt),
                 pl.BlockSpec((N_STASH, ts, 512), lambda i: (0, nt - 1 - i, 0)), _rows(ts, 1024, nt),
                 _full((1024, 1024))]
                + _seq_param_specs() + [_full((4, 128, 128)), _full((1, 512)), _full((1, 512))])
    return pl.pallas_call(
        body, grid=(nt,), name="seqmix_bwd",
        in_specs=in_specs,
        out_specs=[_rows(ts, 2048, nt)] + [_full(sh) for sh in small_shapes],
        out_shape=[_sds((s, 2048), BF16)] + [_sds(sh, F32) for sh in small_shapes],
        scratch_shapes=[pltpu.VMEM((1, 512), F32), pltpu.VMEM((SUBLANES, 512), F32),
                        pltpu.VMEM((SUBLANES, 512), F32), pltpu.VMEM((ts, 512), F32)],
        compiler_params=_params(("arbitrary",)),
    )(z, hst, hst, stash, dy, w_out, *seq_params, ws_t, glo, ggo)


def _mix_in_bwd(x, dz, dx1, w_in4, g, sc, ts=512):
    s, d = x.shape

    def body(x_ref, dz_ref, dx1_ref, w_ref, g_ref, sc_ref, gx_ref, dsh_ref, dsc_ref, dg_ref):
        i = pl.program_id(0)

        @pl.when(i == 0)
        def _():
            for ref in (dsh_ref, dsc_ref, dg_ref):
                ref[...] = jnp.zeros_like(ref)

        for rs in _sub_tiles(ts):
            dh = jnp.zeros((SUB_ROWS, d), F32)
            for k in range(N_CHIPS):
                dh = dh + _dot_nt(dz_ref[rs, k * 512:(k + 1) * 512], w_ref[k])
            xv = x_ref[rs, :]
            r = _msq_rsqrt(xv)
            xn = xv * r
            dsh_ref[...] += _colsum(dh)
            dsc_ref[...] += _colsum(dh * (xn * g_ref[...]))
            dhn = dh * (1.0 + sc_ref[...])
            dg_ref[...] += _colsum(dhn * xn)
            gx_ref[rs, :] = dx1_ref[rs, :] + _rms_bwd(dhn * g_ref[...], xn, r)

    vec = _full((1, d))
    return pl.pallas_call(
        body, grid=(s // ts,), name="mix_in_bwd",
        in_specs=[_rows(ts, d), _rows(ts, 2048), _rows(ts, d), _full(w_in4.shape), vec, vec],
        out_specs=[_rows(ts, d), vec, vec, vec],
        out_shape=[_sds((s, d), F32)] + [_sds((1, d), F32)] * 3,
        compiler_params=_params(("arbitrary",)),
    )(x, dz, dx1, w_in4, g, sc)


def _wgrad(a, b, n_chunks, name, chunk_major, ts=2048):
    s, m = a.shape
    n = b.shape[1]
    nc = n // n_chunks
    nt = s // ts

    def body(a_ref, b_ref, o_ref, acc):
        i = pl.program_id(1)

        @pl.when(i == 0)
        def _():
            acc[...] = jnp.zeros_like(acc)

        acc[...] += _dot_tn(a_ref[...], b_ref[...])

        @pl.when(i == nt - 1)
        def _():
            if chunk_major:
                o_ref[0] = acc[...].astype(BF16)
            else:
                o_ref[...] = acc[...].astype(BF16)

    if chunk_major:
        out_spec, out_shape = pl.BlockSpec((1, m, nc), lambda c, i: (c, 0, 0)), _sds((n_chunks, m, nc), BF16)
    else:
        out_spec, out_shape = pl.BlockSpec((m, nc), lambda c, i: (0, c)), _sds((m, n), BF16)
    return pl.pallas_call(
        body, grid=(n_chunks, nt), name=name,
        in_specs=[pl.BlockSpec((ts, m), lambda c, i: (i, 0)), pl.BlockSpec((ts, nc), lambda c, i: (i, c))],
        out_specs=out_spec,
        out_shape=out_shape,
        scratch_shapes=[pltpu.VMEM((m, nc), F32)],
        compiler_params=_params(("parallel", "arbitrary")),
    )(a, b)


def _block_diag(w):
    heads, hd, _ = w.shape
    eye = jnp.eye(heads, dtype=w.dtype)
    return (eye[:, None, :, None] * w[:, :, None, :]).reshape(heads * hd, heads * hd)


def _seq_params(small):
    row = lambda v: v.reshape(1, -1)
    pos = jnp.arange(GMLP_BLOCK)
    mask = (pos[None, :] // CHUNK) <= (pos[:, None] // CHUNK)
    ws = jnp.where(mask[None], small["w_spatial"], 0.0)
    seq_params = (small["conv_w"], row(small["conv_b"]),
                  _block_diag(small["w_rgate"]).astype(BF16), _block_diag(small["w_igate"]).astype(BF16),
                  row(small["b_rgate"]), row(small["b_igate"]), row(small["lru_a"]),
                  row(small["v_norm_g"]), row(small["v_norm_b"]), ws.astype(BF16), small["b_spatial"].T)
    return seq_params, jnp.swapaxes(ws, 1, 2).astype(BF16)


_ANY = pl.BlockSpec(memory_space=pl.ANY)
_CHIP_FLIPS = ((1, 0), (0, 1), (1, 1))


def _position():
    return lax.axis_index("x"), lax.axis_index("y"), lax.axis_index("c")


def _flip(v, f):
    return 1 - v if f else v


def _remote(src, dst, send_sem, recv_sem, peer):
    return pltpu.make_async_remote_copy(src_ref=src, dst_ref=dst, send_sem=send_sem, recv_sem=recv_sem,
                                        device_id=peer, device_id_type=MESH)


def _allgather8(block, name):
    r, n = block.shape

    def body(x_ref, gath, send_sems, recv_sems, loc_sem):
        x, y, c = _position()
        me = 4 * x + 2 * y + c
        loc = pltpu.make_async_copy(x_ref, gath.at[me], loc_sem)
        loc.start()
        peers = []
        for k in range(1, N_DEV):
            px, py, pc = _flip(x, k & 4), _flip(y, k & 2), _flip(c, k & 1)
            peers.append((px, py, pc))
            _remote(x_ref, gath.at[me], send_sems.at[k - 1], recv_sems.at[k - 1], (px, py, pc)).start()
        for k, (px, py, pc) in enumerate(peers):
            src = 4 * px + 2 * py + pc
            _remote(x_ref, gath.at[src], send_sems.at[k], recv_sems.at[k], (px, py, pc)).wait_recv()
        for k, peer in enumerate(peers):
            _remote(x_ref, gath.at[me], send_sems.at[k], recv_sems.at[k], peer).wait_send()
        loc.wait()

    return pl.pallas_call(
        body, name=name, out_shape=_sds((N_DEV, r, n), F32),
        in_specs=[pl.BlockSpec(memory_space=pltpu.VMEM)], out_specs=pl.BlockSpec(memory_space=pltpu.VMEM),
        scratch_shapes=[pltpu.SemaphoreType.DMA((N_DEV - 1,)), pltpu.SemaphoreType.DMA((N_DEV - 1,)),
                        pltpu.SemaphoreType.DMA],
        compiler_params=pltpu.CompilerParams(vmem_limit_bytes=VMEM_LIMIT_BYTES),
    )(block)


def _half(ref, c, rows):
    hr = rows // 2
    return ref.at[pl.ds(pl.multiple_of(c * hr, BF16_SUBLANES), hr), :]


def _chip_sum(part, recv, pos_arr, name):
    _, rows, cols = part.shape
    hr = rows // 2

    def body(pos_ref, p_ref, r_ref, o_ref, g_ref):
        total = (p_ref[...].astype(F32) + r_ref[...].astype(F32)).astype(BF16)
        o_ref[...] = total

        @pl.when(pl.program_id(0) == pos_ref[1])
        def _():
            g_ref[0] = total

    grid_spec = pltpu.PrefetchScalarGridSpec(
        num_scalar_prefetch=1, grid=(N_CHIPS,),
        in_specs=[pl.BlockSpec((1, hr, cols), lambda k, pos: (k, pos[0], 0)),
                  pl.BlockSpec((1, hr, cols), lambda k, pos: (k, 0, 0))],
        out_specs=[pl.BlockSpec((1, hr, cols), lambda k, pos: (k, 0, 0)),
                   pl.BlockSpec((1, 1, hr, cols), lambda k, pos: (0, pos[1], 0, 0))])
    return pl.pallas_call(
        body, name=name, grid_spec=grid_spec,
        out_shape=[_sds((N_CHIPS, hr, cols), BF16), _sds((2, N_CHIPS, hr, cols), BF16)],
        compiler_params=_params(("arbitrary",)),
    )(pos_arr, part, recv)


_HBM = pl.BlockSpec(memory_space=pltpu.HBM)
_SEM = pl.BlockSpec(memory_space=pltpu.SEMAPHORE)
_EFFECT = pltpu.SideEffectType.DATAFLOW_SIDE_EFFECTING


def _in_hbm(a):
    return pltpu.with_memory_space_constraint(a, pltpu.HBM)


def _split_start(srcs, lands, plan, n_copies, after, name):
    ns, nl = len(srcs), len(lands)
    bufs = list(srcs) + list(lands)

    def body(*refs):
        send_sems, recv_sems = refs[ns + nl + 1], refs[ns + nl + 2]
        token = refs[-1]
        for k, (src, dst, peer) in enumerate(plan(refs[:ns], refs[ns:ns + nl])):
            _remote(src, dst, send_sems.at[k], recv_sems.at[k], peer).start()
        token[...] = jnp.zeros_like(token)

    out = pl.pallas_call(
        body, name=name,
        out_shape=(pltpu.SemaphoreType.DMA((n_copies,)), pltpu.SemaphoreType.DMA((n_copies,)),
                   *[pltpu.HBM(b.shape, b.dtype) for b in bufs], _sds((SUBLANES, 128), F32)),
        in_specs=[_HBM] * (ns + nl) + [_ANY],
        out_specs=(_SEM, _SEM, *[_HBM] * (ns + nl), pl.BlockSpec(memory_space=pltpu.VMEM)),
        input_output_aliases={i: 2 + i for i in range(ns + nl)},
        compiler_params=pltpu.CompilerParams(has_side_effects=_EFFECT),
    )(*[_in_hbm(b) for b in bufs], after)
    return out[0], out[1], list(out[2:2 + ns]), list(out[2 + ns:2 + ns + nl]), out[-1]


def _split_wait(send_sems, recv_sems, srcs, lands, plan, after, name):
    ns, nl = len(srcs), len(lands)
    bufs = list(srcs) + list(lands)

    def body(*refs):
        send_ref, recv_ref = refs[ns + nl], refs[ns + nl + 1]
        me = _position()
        for k, src, dst in plan(refs[:ns], refs[ns:ns + nl]):
            cp = _remote(src, dst, send_ref.at[k], recv_ref.at[k], me)
            cp.wait_send()
            cp.wait_recv()

    out = pl.pallas_call(
        body, name=name,
        out_shape=[pltpu.HBM(b.shape, b.dtype) for b in bufs],
        in_specs=[_HBM] * (ns + nl) + [_SEM, _SEM, _ANY],
        out_specs=[_HBM] * (ns + nl),
        input_output_aliases={i: i for i in range(ns + nl)},
        compiler_params=pltpu.CompilerParams(has_side_effects=_EFFECT),
    )(*bufs, send_sems, recv_sems, after)
    return list(out[:ns]), list(out[ns:])


def _gather_plan(rows_of):
    def start(src_refs, land_refs):
        x, y, c = _position()
        chip = 2 * x + y
        out = []
        for a, rows in enumerate(rows_of):
            mine = _half(land_refs[a].at[chip], c, rows)
            out.extend((mine, mine, (_flip(x, fx), _flip(y, fy), c)) for fx, fy in _CHIP_FLIPS)
        return out

    def wait(src_refs, land_refs):
        x, y, c = _position()
        chip = 2 * x + y
        out = []
        for a, rows in enumerate(rows_of):
            for j, (fx, fy) in enumerate(_CHIP_FLIPS):
                src_chip = 2 * _flip(x, fx) + _flip(y, fy)
                out.append((3 * a + j, _half(land_refs[a].at[chip], c, rows),
                            _half(land_refs[a].at[src_chip], c, rows)))
        return out

    return start, wait


def _forward_plan(rows_of):
    def pieces(land_refs, half):
        x, y, _ = _position()
        return [_half(land_refs[a].at[2 * _flip(x, fx) + _flip(y, fy)], half, rows)
                for a, rows in enumerate(rows_of) for fx, fy in _CHIP_FLIPS]

    def start(src_refs, land_refs):
        x, y, c = _position()
        return [(p, p, (x, y, 1 - c)) for p in pieces(land_refs, c)]

    def wait(src_refs, land_refs):
        _, _, c = _position()
        return [(k, mine, theirs)
                for k, (mine, theirs) in enumerate(zip(pieces(land_refs, c), pieces(land_refs, 1 - c)))]

    return start, wait


def _swap_halves_plan(half_rows):
    def slices(src_refs, c):
        return [src_refs[a].at[:, pl.ds(pl.multiple_of((1 - c) * hr, BF16_SUBLANES), hr), :]
                for a, hr in enumerate(half_rows)]

    def start(src_refs, land_refs):
        x, y, c = _position()
        return [(src, land_refs[a], (x, y, 1 - c)) for a, src in enumerate(slices(src_refs, c))]

    def wait(src_refs, land_refs):
        _, _, c = _position()
        return [(a, src, land_refs[a]) for a, src in enumerate(slices(src_refs, c))]

    return start, wait


def _swap_gathered_plan(n_arrays):
    def start(src_refs, land_refs):
        x, y, c = _position()
        return [(land_refs[a].at[0], land_refs[a].at[1], (x, y, 1 - c)) for a in range(n_arrays)]

    def wait(src_refs, land_refs):
        return [(a, land_refs[a].at[0], land_refs[a].at[1]) for a in range(n_arrays)]

    return start, wait


def _exchange_plan(n_arrays):
    def start(src_refs, land_refs):
        x, y, c = _position()
        chip = 2 * x + y
        out = []
        for a in range(n_arrays):
            for fx, fy in _CHIP_FLIPS:
                px, py = _flip(x, fx), _flip(y, fy)
                out.append((src_refs[a].at[2 * px + py], land_refs[a].at[0, chip], (px, py, c)))
        return out

    def wait(src_refs, land_refs):
        x, y, c = _position()
        out = []
        for a in range(n_arrays):
            for j, (fx, fy) in enumerate(_CHIP_FLIPS):
                src_chip = 2 * _flip(x, fx) + _flip(y, fy)
                out.append((3 * a + j, src_refs[a].at[src_chip], land_refs[a].at[0, src_chip]))
        return out

    return start, wait


def _forward_to_sibling(lands, name):
    na = len(lands)

    def body(*refs):
        land_refs = refs[na:2 * na]
        send_sems, recv_sems = refs[2 * na:]
        x, y, c = _position()
        sibling = (x, y, 1 - c)
        sends = []
        for a in range(na):
            rows = lands[a].shape[1]
            for j, (fx, fy) in enumerate(_CHIP_FLIPS):
                landed = _half(land_refs[a].at[2 * _flip(x, fx) + _flip(y, fy)], c, rows)
                sends.append(_remote(landed, landed, send_sems.at[3 * a + j], recv_sems.at[3 * a + j], sibling))
                sends[-1].start()
        for a in range(na):
            rows = lands[a].shape[1]
            for j, (fx, fy) in enumerate(_CHIP_FLIPS):
                other = _half(land_refs[a].at[2 * _flip(x, fx) + _flip(y, fy)], 1 - c, rows)
                _remote(other, other, send_sems.at[3 * a + j], recv_sems.at[3 * a + j], sibling).wait_recv()
        for cp in sends:
            cp.wait_send()

    return pl.pallas_call(
        body, name=name,
        out_shape=[_sds(l.shape, l.dtype) for l in lands],
        in_specs=[_ANY] * na, out_specs=[_ANY] * na,
        input_output_aliases={a: a for a in range(na)},
        scratch_shapes=[pltpu.SemaphoreType.DMA((3 * na,))] * 2,
    )(*lands)


def _adam_gathered(w, gath, m, v, c_arr, after, name, tr=128):
    rows, cols = w.shape
    hr = rows // 2
    if hr % (2 * tr) == 0:
        tr = 2 * tr
    per = hr // tr

    def body(c_ref, w_ref, g_ref, m_ref, v_ref, after_ref, go_ref, d_ref, nm_ref, nv_ref):
        g = g_ref[0, 0].astype(F32)
        for k in range(1, N_CHIPS):
            g = g + g_ref[0, k].astype(F32)
        go_ref[...] = g
        d_ref[...], nm_ref[...], nv_ref[...] = _adam_math(w_ref[...], g, m_ref[...], v_ref[...])

    def rows_of(h, i, c_ref):
        c = c_ref[0]
        return ((c + h - 2 * c * h) * per + i, 0)

    blk = pl.BlockSpec((tr, cols), rows_of)
    grid_spec = pltpu.PrefetchScalarGridSpec(
        num_scalar_prefetch=1, grid=(2, per),
        in_specs=[blk, pl.BlockSpec((1, N_CHIPS, tr, cols), lambda h, i, c_ref: (h, 0, i, 0)), blk, blk, _ANY],
        out_specs=[blk] * 4)
    return pl.pallas_call(
        body, name=name, grid_spec=grid_spec, out_shape=[_sds(w.shape, F32)] * 4,
        compiler_params=_params(("arbitrary", "arbitrary")),
    )(c_arr, w, gath, m, v, after)


def _allreduce_small(block, name):
    two, r, n = block.shape
    assert two == 2

    def body(x_ref, out_ref, sib, chipsum, gath, d2d_send, d2d_recv, ici_send, ici_recv):
        x, y, c = _position()
        chip = 2 * x + y
        sibling = (x, y, 1 - c)
        first = _remote(x_ref, sib, d2d_send.at[0], d2d_recv.at[0], sibling)
        first.start()
        first.wait()
        chipsum[...] = x_ref[...] + sib[...]
        sends = []
        for j, (fx, fy) in enumerate(_CHIP_FLIPS):
            sends.append(_remote(chipsum.at[c], gath.at[chip], ici_send.at[j], ici_recv.at[j],
                                 (_flip(x, fx), _flip(y, fy), c)))
            sends[-1].start()
        gath[chip] = chipsum[c]
        for j, (fx, fy) in enumerate(_CHIP_FLIPS):
            landed = gath.at[2 * _flip(x, fx) + _flip(y, fy)]
            _remote(landed, landed, ici_send.at[j], ici_recv.at[j], sibling).wait_recv()
        for cp in sends:
            cp.wait_send()
        total = gath[0]
        for k in range(1, N_CHIPS):
            total = total + gath[k]
        out_ref[c] = total
        last = _remote(out_ref.at[c], out_ref.at[c], d2d_send.at[1], d2d_recv.at[1], sibling)
        last.start()
        _remote(out_ref.at[1 - c], out_ref.at[1 - c], d2d_send.at[1], d2d_recv.at[1], sibling).wait_recv()
        last.wait_send()

    vmem = pl.BlockSpec(memory_space=pltpu.VMEM)
    return pl.pallas_call(
        body, name=name, out_shape=_sds(block.shape, F32), in_specs=[vmem], out_specs=vmem,
        scratch_shapes=[pltpu.VMEM(block.shape, F32), pltpu.VMEM(block.shape, F32), pltpu.VMEM((N_CHIPS, r, n), F32),
                        pltpu.SemaphoreType.DMA((2,)), pltpu.SemaphoreType.DMA((2,)),
                        pltpu.SemaphoreType.DMA((3,)), pltpu.SemaphoreType.DMA((3,))],
        compiler_params=pltpu.CompilerParams(vmem_limit_bytes=VMEM_LIMIT_BYTES),
    )(block)


def _cast_place(shards, chip_arr):
    na = len(shards)
    steps = 4

    def body(chip_ref, *refs):
        for a in range(na):
            refs[na + a][0] = refs[a][...].astype(BF16)

    grid_spec = pltpu.PrefetchScalarGridSpec(
        num_scalar_prefetch=1, grid=(steps,),
        in_specs=[pl.BlockSpec((s.shape[0] // steps, s.shape[1]), lambda i, ch: (i, 0)) for s in shards],
        out_specs=[pl.BlockSpec((1, s.shape[0] // steps, s.shape[1]), lambda i, ch: (ch[0], i, 0)) for s in shards])
    return pl.pallas_call(
        body, name="cast_place", grid_spec=grid_spec,
        out_shape=[_sds((N_CHIPS,) + s.shape, BF16) for s in shards],
        compiler_params=_params(("arbitrary",)),
    )(chip_arr, *shards)


def _silu(v):
    return v * _sigmoid(v)


def _ada_fwd(c8, w_ada):
    def body(c_ref, w_ref, o_ref):
        o_ref[...] = jnp.dot(_silu(c_ref[...]), w_ref[...], preferred_element_type=F32,
                             precision=lax.Precision.HIGHEST)

    return pl.pallas_call(
        body, name="ada_fwd", out_shape=_sds((N_DEV, w_ada.shape[1]), F32),
        compiler_params=pltpu.CompilerParams(vmem_limit_bytes=VMEM_LIMIT_BYTES),
    )(c8, w_ada)


def _mod_select(parts, b_ada, me_arr, after):
    cols = parts.shape[2]

    def body(me_ref, p_ref, b_ref, after_ref, o_ref):
        me = me_ref[0]
        for k in range(N_CHIPS):
            cs = slice(k * cols, (k + 1) * cols)
            o_ref[:, cs] = p_ref[2 * k, pl.ds(me, 1), :] + b_ref[:, cs]

    grid_spec = pltpu.PrefetchScalarGridSpec(
        num_scalar_prefetch=1, grid=(1,),
        in_specs=[pl.BlockSpec(parts.shape, lambda i, m: (0, 0, 0)), pl.BlockSpec(b_ada.shape, lambda i, m: (0, 0)),
                  _ANY],
        out_specs=pl.BlockSpec(b_ada.shape, lambda i, m: (0, 0)))
    return pl.pallas_call(body, name="mod_select", grid_spec=grid_spec, out_shape=_sds(b_ada.shape, F32))(
        me_arr, parts, b_ada, after)


def _ada_bwd(c8, dmod8, chip_arr, w, m, v, tr=256):
    d = c8.shape[1]
    cols = dmod8.shape[1] // N_CHIPS

    def body(chip_ref, c_ref, dm_ref, dmall_ref, w_ref, m_ref, v_ref, gw_ref, d_ref, nm_ref, nv_ref, gb_ref):
        g = lax.dot_general(_silu(c_ref[...]), dm_ref[...], (((0,), (0,)), ((), ())),
                            preferred_element_type=F32, precision=lax.Precision.HIGHEST)
        gw_ref[...] = g
        d_ref[...], nm_ref[...], nv_ref[...] = _adam_math(w_ref[...], g, m_ref[...], v_ref[...])
        acc = dmall_ref[0:1, :]
        for k in range(1, N_DEV):
            acc = acc + dmall_ref[k:k + 1, :]
        gb_ref[...] = acc

    rows = pl.BlockSpec((tr, cols), lambda i, ch: (i, 0))
    grid_spec = pltpu.PrefetchScalarGridSpec(
        num_scalar_prefetch=1, grid=(d // tr,),
        in_specs=[pl.BlockSpec((N_DEV, tr), lambda i, ch: (0, i)),
                  pl.BlockSpec((N_DEV, cols), lambda i, ch: (0, ch[0])),
                  pl.BlockSpec(dmod8.shape, lambda i, ch: (0, 0)), rows, rows, rows],
        out_specs=[rows] * 4 + [pl.BlockSpec((1, dmod8.shape[1]), lambda i, ch: (0, 0))])
    return pl.pallas_call(
        body, name="ada_bwd", grid_spec=grid_spec,
        out_shape=[_sds((d, cols), F32)] * 4 + [_sds((1, dmod8.shape[1]), F32)],
        compiler_params=_params(("arbitrary",)),
    )(chip_arr, c8, dmod8, dmod8, w, m, v)


def _adam_math(w, g, m, v):
    m = ADAM_B1 * m + (1.0 - ADAM_B1) * g
    v = ADAM_B2 * v + (1.0 - ADAM_B2) * (g * g)
    m_hat = m / (1.0 - ADAM_B1 ** ADAM_STEP)
    v_hat = v / (1.0 - ADAM_B2 ** ADAM_STEP)
    delta = -ADAM_LR * (m_hat / (jnp.sqrt(v_hat) + ADAM_EPS) + ADAM_WD * w)
    return delta, m, v


def _adam(w, g, m, v, name, tr=256):
    rows, cols = w.shape
    if rows % tr:
        tr = rows

    def body(w_ref, g_ref, m_ref, v_ref, d_ref, nm_ref, nv_ref):
        d_ref[...], nm_ref[...], nv_ref[...] = _adam_math(w_ref[...], g_ref[...], m_ref[...], v_ref[...])

    spec = pl.BlockSpec((tr, cols), lambda i: (i, 0))
    return pl.pallas_call(
        body, name=name, grid=(rows // tr,), in_specs=[spec] * 4, out_specs=[spec] * 3,
        out_shape=[_sds(w.shape, F32)] * 3, compiler_params=_params(("parallel",)),
    )(w, g, m, v)


SMALL_REPLICATED = ("g_mix_pre", "g_mix_post", "conv_b", "w_rgate", "b_rgate", "w_igate", "b_igate", "lru_a",
                    "v_norm_g", "v_norm_b", "w_spatial", "b_spatial", "g_lru_out", "g_gmlp_out", "g_ffn_pre",
                    "g_ffn_post", "ffn_conv_b")
SMALL_COLUMN_SHARDED = ("conv_w", "ffn_conv_w")

SMALL_ROW_LEN = 86016
_SMALL_ROWS = (
    (("ffn_conv_w", 18432), ("conv_w", 2048), ("w_spatial", 65536)),
    (("w_rgate", 32768), ("w_igate", 32768), ("ffn_conv_b", 6144), ("g_mix_pre", 1024), ("g_mix_post", 1024),
     ("g_ffn_pre", 1024), ("g_ffn_post", 1024), ("conv_b", 512), ("b_rgate", 512), ("b_igate", 512),
     ("lru_a", 512), ("v_norm_g", 512), ("v_norm_b", 512), ("b_spatial", 512), ("g_lru_out", 512),
     ("g_gmlp_out", 512), ("loss", 128)),
)


def _small_slots():
    slots = {}
    for row, entries in enumerate(_SMALL_ROWS):
        off = 0
        for name, size in entries:
            slots[name] = (row, off)
            off += size
        assert off <= SMALL_ROW_LEN
    return slots


SMALL_SLOT = _small_slots()
SMALL_LANES = SMALL_ROW_LEN // SUBLANES


def _small_pieces(name, first, count):
    row, off = SMALL_SLOT[name]
    pos, pieces = off + first, []
    while count:
        sub, lane = divmod(pos, SMALL_LANES)
        n = min(count, SMALL_LANES - lane)
        pieces.append((row, sub, lane, n))
        pos, count = pos + n, count - n
    return pieces
ROW_VECTORS = ("ffn_conv_b", "g_mix_pre", "g_mix_post", "g_ffn_pre", "g_ffn_post", "conv_b", "lru_a", "v_norm_g",
               "v_norm_b", "g_lru_out", "g_gmlp_out")
HEAD_DIM = LRU_WIDTH // LRU_HEADS


def _pack_small(g, after):
    order = ("ffn_conv_w", "conv_w", "w_spatial", "w_rgate", "w_igate", "b_rgate", "b_igate", "b_spatial", "loss") \
        + ROW_VECTORS
    vmem = pl.BlockSpec(memory_space=pltpu.VMEM)

    def body(*refs):
        src = dict(zip(order, refs))
        out_ref = refs[len(order) + 1]
        out_ref[...] = jnp.zeros_like(out_ref)

        def put(name, first, val):
            col = 0
            for row, sub, lane, n in _small_pieces(name, first, val.shape[1]):
                out_ref[row, sub:sub + 1, lane:lane + n] = val[:, col:col + n]
                col += n

        for name in ROW_VECTORS + ("b_rgate", "b_igate", "loss"):
            put(name, 0, src[name][...])
        for name in ("ffn_conv_w", "conv_w"):
            k_taps, n = src[name].shape
            for k in range(k_taps):
                put(name, k * n, src[name][k:k + 1, :])
        for g_idx in range(GMLP_GROUPS):
            for i in range(GMLP_BLOCK):
                put("w_spatial", (g_idx * GMLP_BLOCK + i) * GMLP_BLOCK, src["w_spatial"][g_idx, i:i + 1, :])
        for name in ("w_rgate", "w_igate"):
            for h in range(LRU_HEADS):
                for i in range(HEAD_DIM):
                    r = h * HEAD_DIM + i
                    put(name, r * HEAD_DIM, src[name][r:r + 1, h * HEAD_DIM:(h + 1) * HEAD_DIM])
        eye = (lax.broadcasted_iota(jnp.int32, (GMLP_BLOCK, GMLP_BLOCK), 0)
               == lax.broadcasted_iota(jnp.int32, (GMLP_BLOCK, GMLP_BLOCK), 1))
        for g_idx in range(GMLP_GROUPS):
            col = src["b_spatial"][:, g_idx:g_idx + 1]
            put("b_spatial", g_idx * GMLP_BLOCK, _colsum(jnp.where(eye, col, 0.0)))

    return pl.pallas_call(
        body, name="pack_small", out_shape=_sds((2, SUBLANES, SMALL_LANES), F32),
        in_specs=[vmem] * len(order) + [_ANY], out_specs=vmem,
        compiler_params=pltpu.CompilerParams(vmem_limit_bytes=VMEM_LIMIT_BYTES),
    )(*[g[n] for n in order], after)


def _adam_small(g_small, w, m, v):
    vmem = pl.BlockSpec(memory_space=pltpu.VMEM)
    n_p = len(SMALL_REPLICATED)

    def body(g_ref, *refs):
        w_refs, m_refs, v_refs = refs[:n_p], refs[n_p:2 * n_p], refs[2 * n_p:3 * n_p]
        outs = refs[3 * n_p:]
        go, do, mo, vo = outs[:n_p], outs[n_p:2 * n_p], outs[2 * n_p:3 * n_p], outs[3 * n_p:]
        for k, name in enumerate(SMALL_REPLICATED):
            def take(first, count, name=name):
                parts = [g_ref[row, sub:sub + 1, lane:lane + n]
                         for row, sub, lane, n in _small_pieces(name, first, count)]
                return parts[0] if len(parts) == 1 else jnp.concatenate(parts, axis=1)

            shape = w_refs[k].shape
            if name in ROW_VECTORS:
                go[k][...] = take(0, shape[1])
            elif name in ("b_rgate", "b_igate"):
                for h in range(LRU_HEADS):
                    go[k][0, h:h + 1, :] = take(h * HEAD_DIM, HEAD_DIM)
            elif name == "b_spatial":
                for g_idx in range(GMLP_GROUPS):
                    go[k][0, g_idx:g_idx + 1, :] = take(g_idx * GMLP_BLOCK, GMLP_BLOCK)
            elif name == "w_spatial":
                for g_idx in range(GMLP_GROUPS):
                    for i in range(GMLP_BLOCK):
                        go[k][0, g_idx, i:i + 1, :] = take((g_idx * GMLP_BLOCK + i) * GMLP_BLOCK, GMLP_BLOCK)
            else:
                for h in range(LRU_HEADS):
                    for i in range(HEAD_DIM):
                        go[k][0, h, i:i + 1, :] = take((h * HEAD_DIM + i) * HEAD_DIM, HEAD_DIM)
            do[k][...], mo[k][...], vo[k][...] = _adam_math(w_refs[k][...], go[k][...], m_refs[k][...],
                                                             v_refs[k][...])

    names = SMALL_REPLICATED
    out_shape = [_sds(w[n].shape, F32) for n in names] * 4
    res = pl.pallas_call(
        body, name="adam_small", out_shape=out_shape,
        in_specs=[vmem] * (1 + 3 * n_p), out_specs=[vmem] * (4 * n_p),
        compiler_params=pltpu.CompilerParams(vmem_limit_bytes=VMEM_LIMIT_BYTES),
    )(g_small, *[w[n] for n in names], *[m[n] for n in names], *[v[n] for n in names])
    return [dict(zip(names, res[k * n_p:(k + 1) * n_p])) for k in range(4)]


def _adam_cols(name, g_small, w, m, v, chip_arr):
    _, k_taps, n = w.shape
    row, off = SMALL_SLOT[name]
    first = off // n
    per_sub = SMALL_LANES // n

    def body(chip_ref, *refs):
        g_refs = refs[:k_taps]
        w_ref, m_ref, v_ref, go_ref, d_ref, nm_ref, nv_ref = refs[k_taps:]
        for k in range(k_taps):
            tap = (0, slice(k, k + 1), slice(None))
            sub = (first + N_CHIPS * k + chip_ref[0]) // per_sub
            g = g_refs[k][row, pl.ds(sub, 1), :]
            go_ref[tap] = g
            d_ref[tap], nm_ref[tap], nv_ref[tap] = _adam_math(w_ref[tap], g, m_ref[tap], v_ref[tap])

    whole = pl.BlockSpec(w.shape, lambda i, ch: (0, 0, 0))
    taps = [pl.BlockSpec((2, SUBLANES, n),
                         functools.partial(lambda i, ch, k: (0, 0, (first + N_CHIPS * k + ch[0]) % per_sub), k=k))
            for k in range(k_taps)]
    grid_spec = pltpu.PrefetchScalarGridSpec(
        num_scalar_prefetch=1, grid=(1,), in_specs=taps + [whole] * 3, out_specs=[whole] * 4)
    return pl.pallas_call(body, name="adam_" + name, grid_spec=grid_spec, out_shape=[_sds(w.shape, F32)] * 4)(
        chip_arr, *[g_small] * k_taps, w, m, v)


def kernel(x, c, w_ada, b_ada, g_mix_pre, g_mix_post, w_in, conv_w, conv_b, w_rgate, b_rgate, w_igate, b_igate, lru_a, v_norm_g, v_norm_b, w_spatial, b_spatial, g_lru_out, g_gmlp_out, w_out, g_ffn_pre, g_ffn_post, w_up, ffn_conv_w, ffn_conv_b, w_down, loss_target, m_w_ada, m_b_ada, m_g_mix_pre, m_g_mix_post, m_w_in, m_conv_w, m_conv_b, m_w_rgate, m_b_rgate, m_w_igate, m_b_igate, m_lru_a, m_v_norm_g, m_v_norm_b, m_w_spatial, m_b_spatial, m_g_lru_out, m_g_gmlp_out, m_w_out, m_g_ffn_pre, m_g_ffn_post, m_w_up, m_ffn_conv_w, m_ffn_conv_b, m_w_down, v_w_ada, v_b_ada, v_g_mix_pre, v_g_mix_post, v_w_in, v_conv_w, v_conv_b, v_w_rgate, v_b_rgate, v_w_igate, v_b_igate, v_lru_a, v_v_norm_g, v_v_norm_b, v_w_spatial, v_b_spatial, v_g_lru_out, v_g_gmlp_out, v_w_out, v_g_ffn_pre, v_g_ffn_post, v_w_up, v_ffn_conv_w, v_ffn_conv_b, v_w_down):
    args = dict(locals())
    names = ("w_ada", "b_ada", "g_mix_pre", "g_mix_post", "w_in", "conv_w", "conv_b", "w_rgate", "b_rgate",
             "w_igate", "b_igate", "lru_a", "v_norm_g", "v_norm_b", "w_spatial", "b_spatial", "g_lru_out",
             "g_gmlp_out", "w_out", "g_ffn_pre", "g_ffn_post", "w_up", "ffn_conv_w", "ffn_conv_b", "w_down")
    drop = lambda a: a if a.ndim == 2 else a[0]
    w = {n: drop(args[n]) for n in names}
    m = {n: drop(args["m_" + n]) for n in names}
    v = {n: drop(args["v_" + n]) for n in names}
    xi, yi, ci = _position()
    me_arr = jnp.reshape(4 * xi + 2 * yi + ci, (1,)).astype(jnp.int32)
    chip_arr = jnp.reshape(2 * xi + yi, (1,)).astype(jnp.int32)
    c_arr = jnp.reshape(ci, (1,)).astype(jnp.int32)
    pos_arr = jnp.stack([ci, 2 * xi + yi]).astype(jnp.int32)

    big = ("w_in", "w_out", "w_up", "w_down")
    lands = _cast_place([w[n] for n in big], chip_arr)
    start_a, wait_a = _gather_plan([w[n].shape[0] for n in big[:2]])
    start_b, wait_b = _gather_plan([w[n].shape[0] for n in big[2:]])

    row0 = jnp.concatenate([c, w["conv_w"].reshape(1, -1), w["ffn_conv_w"].reshape(1, -1)], axis=1)
    g0 = _allgather8(row0, "gather_cond")[:, 0, :]
    send_a, recv_a, _, lands_a, token_a = _split_start([], lands[:2], start_a, 6, g0, "gather_start_a")
    c8 = g0[:, :D_MODEL]
    per_chip = g0[0::2]
    conv_w_full = per_chip[:, D_MODEL:D_MODEL + 512].reshape(N_CHIPS, 4, 128).transpose(1, 0, 2).reshape(4, 512)
    ffn_conv_w_full = per_chip[:, D_MODEL + 512:].reshape(N_CHIPS, 3, 1536).transpose(1, 0, 2).reshape(3, 2 * D_FF)
    mod_parts = _allgather8(_ada_fwd(c8 + token_a[0:1, 0:1], w["w_ada"]), "gather_mod")
    send_b, recv_b, _, lands_b, token_b = _split_start([], lands[2:], start_b, 6, mod_parts, "gather_start_b")
    mod = _mod_select(mod_parts, w["b_ada"].reshape(1, -1), me_arr, token_b).reshape(N_MOD, D_MODEL)
    sh_m, sc_m, gt_m, sh_f, sc_f, gt_f = [mod[k:k + 1] for k in range(N_MOD)]

    small = {n: w[n] for n in SMALL_REPLICATED}
    small["conv_w"] = conv_w_full
    small["ffn_conv_w"] = ffn_conv_w_full
    row = lambda a: a.reshape(1, -1)
    seq_params, ws_t = _seq_params(small)
    glo, ggo = row(small["g_lru_out"]), row(small["g_gmlp_out"])
    g_pre, g_post = row(small["g_mix_pre"]), row(small["g_mix_post"])
    g_pre2, g_post2 = row(small["g_ffn_pre"]), row(small["g_ffn_post"])
    fw, fb = small["ffn_conv_w"], row(small["ffn_conv_b"])
    xs, tgt = x[0], loss_target[0]

    _, lands_a = _split_wait(send_a, recv_a, [], lands_a, wait_a, mod, "gather_wait_a")
    w_in4, w_out4 = _forward_to_sibling(lands_a, "forward_a")
    w_out_b = w_out4.reshape(D_MODEL, D_MODEL)
    z, h = _mix_in(xs, sc_m, sh_m, g_pre, w_in4)
    ycat, hst, stash = _seqmix(z, seq_params, glo, ggo)
    _, lands_b = _split_wait(send_b, recv_b, [], lands_b, wait_b, ycat, "gather_wait_b")
    fwd_start, fwd_wait = _forward_plan([w[n].shape[0] for n in big[2:]])
    fwd_send, fwd_recv, _, lands_b, tok = _split_start([], lands_b, fwd_start, 6, pos_arr, "forward_start_b")
    y, x1, h2 = _mix_out(ycat, xs, w_out_b, gt_m + tok[0:1, 0:1], g_post, g_pre2, sc_f, sh_f)
    _, (w_up4, w_down4) = _split_wait(fwd_send, fwd_recv, [], lands_b, fwd_wait, h2, "forward_wait_b")
    w_down_b = w_down4.reshape(D_FF, D_MODEL)
    up0, pre, act, dy2, dx2, loss, dgt_f, dg_post2 = _ffn_fwd(h2, x1, tgt, w_up4, w_down_b, fw, fb, gt_f, g_post2)

    dup0, dfw, dfb = _ffn_bwd_a(dy2, pre, up0, w_down_b, fw)
    gw_up = _wgrad(h2, dup0, N_CHIPS, "wgrad_up", True)
    gw_down = _wgrad(act, dy2, 2, "wgrad_down", False)
    ex_start, ex_wait = _exchange_plan(2)
    sg_start, sg_wait = _swap_gathered_plan(2)
    grads, deltas, new_m, new_v = {}, {}, {}, {}

    def swap_start(parts, name):
        sw_start, sw_wait = _swap_halves_plan([p.shape[1] // 2 for p in parts])
        recv = [lax.empty((N_CHIPS, p.shape[1] // 2, p.shape[2]), BF16) for p in parts]
        send_s, recv_s, parts, recv, token = _split_start(parts, recv, sw_start, len(parts), pos_arr,
                                                           "swap_start_" + name)
        return (send_s, recv_s, parts, recv, sw_wait), token

    def exchange_start(swap, tags, after, name):
        send_s, recv_s, parts, recv, sw_wait = swap
        parts, recv = _split_wait(send_s, recv_s, parts, recv, sw_wait, after, "swap_wait_" + name)
        both = [_chip_sum(p, r, pos_arr, "chip_sum_" + t) for p, r, t in zip(parts, recv, tags)]
        sums, gath = [b[0] for b in both], [b[1] for b in both]
        return _split_start(sums, gath, ex_start, 3 * len(parts), pos_arr, "exchange_start_" + name)

    def gathered_start(exchange, after, name):
        send_s, recv_s, sums, gath, _ = exchange
        _, gath = _split_wait(send_s, recv_s, sums, gath, ex_wait, after, "exchange_wait_" + name)
        send_s, recv_s, _, gath, token = _split_start([], gath, sg_start, len(gath), pos_arr,
                                                      "gathered_start_" + name)
        return (send_s, recv_s, gath), token

    def gathered_wait(gathered, after, name):
        send_s, recv_s, gath = gathered
        return _split_wait(send_s, recv_s, [], gath, sg_wait, after, "gathered_wait_" + name)[1]

    def adam_big(t, gath, after):
        grads[t], deltas[t], new_m[t], new_v[t] = _adam_gathered(w[t], gath, m[t], v[t], c_arr, after, "adam_" + t)

    def behind(value, token):
        return value + token[0:1, 0:1]

    tags_b, tags_a = ("w_up", "w_down"), ("w_in", "w_out")
    swap_b, tok = swap_start([gw_up, gw_down.reshape(N_CHIPS, -1, D_MODEL)], "b")
    dx1, dy, dsh_f, dsc_f, dg_pre2, dgt_m, dg_post = _ffn_bwd_b(
        dup0, x1, y, dx2, w_up4, g_pre2, behind(sc_f, tok), sh_f, gt_m, g_post)
    exchange_b = exchange_start(swap_b, tags_b, dg_post, "b")
    (dz, dcw, dcb, dwr, dwi, dbr, dbi, dspa, dng, dnb, dws, dbs_t, dglo, dggo) = _seqmix_bwd(
        z, hst, stash, dy, w_out_b, seq_params, ws_t, behind(glo, exchange_b[4]), ggo)
    gw_in = _wgrad(h, dz, N_CHIPS, "wgrad_in", True)
    gw_out = _wgrad(ycat, dy, 1, "wgrad_out", False)
    swap_a, tok = swap_start([gw_in, gw_out.reshape(N_CHIPS, -1, D_MODEL)], "a")
    grad_x, dsh_m, dsc_m, dg_pre = _mix_in_bwd(xs, dz, dx1, w_in4, g_pre, behind(sc_m, tok))

    dmod = jnp.concatenate([behind(dsh_m, tok), dsc_m, dgt_m, dsh_f, dsc_f, dgt_f], axis=1)
    dmod8 = _allgather8(dmod, "gather_dmod")[:, 0, :]
    small_grads = dict(
        g_mix_pre=dg_pre, g_mix_post=dg_post, conv_w=dcw, conv_b=dcb, w_rgate=dwr, b_rgate=dbr, w_igate=dwi,
        b_igate=dbi, lru_a=dspa, v_norm_g=dng, v_norm_b=dnb, w_spatial=dws, b_spatial=dbs_t, g_lru_out=dglo,
        g_gmlp_out=dggo, g_ffn_pre=dg_pre2, g_ffn_post=dg_post2, ffn_conv_w=dfw, ffn_conv_b=dfb,
        loss=loss)
    g_small = _allreduce_small(_pack_small(small_grads, dmod8), "reduce_small")
    total = g_small[_small_pieces("loss", 0, 1)[0][:3]]
    exchange_a = exchange_start(swap_a, tags_a, g_small, "a")
    gathered_b, tok = gathered_start(exchange_b, exchange_a[4], "b")

    grads["w_ada"], deltas["w_ada"], new_m["w_ada"], new_v["w_ada"], g_b_ada = _ada_bwd(
        c8, behind(dmod8, tok), chip_arr, w["w_ada"], m["w_ada"], v["w_ada"])
    rep = SMALL_REPLICATED
    small_out = _adam_small(g_small, {n: args[n] for n in rep}, {n: args["m_" + n] for n in rep},
                            {n: args["v_" + n] for n in rep})
    for n in rep:
        grads[n], deltas[n], new_m[n], new_v[n] = [group[n] for group in small_out]
    for n in SMALL_COLUMN_SHARDED:
        grads[n], deltas[n], new_m[n], new_v[n] = _adam_cols(n, g_small, args[n], args["m_" + n],
                                                             args["v_" + n], chip_arr)
    d_b, m_b, v_b = _adam(w["b_ada"], g_b_ada, m["b_ada"], v["b_ada"], "adam_b_ada")
    grads["b_ada"], deltas["b_ada"], new_m["b_ada"], new_v["b_ada"] = g_b_ada, d_b, m_b, v_b

    gath_up, gath_down = gathered_wait(gathered_b, d_b, "b")
    adam_big("w_down", gath_down, pos_arr)
    gathered_a, tok = gathered_start(exchange_a, deltas["w_down"], "a")
    adam_big("w_up", gath_up, tok)
    gath_in, gath_out = gathered_wait(gathered_a, deltas["w_up"], "a")
    adam_big("w_in", gath_in, pos_arr)
    adam_big("w_out", gath_out, pos_arr)

    outs = [total, grad_x[None]]
    for group in (grads, deltas, new_m, new_v):
        outs.extend(group[n].reshape(args[n].shape) for n in names)
    return tuple(outs)
```

```python
import functools
import math

import jax
import jax.numpy as jnp
from jax import lax
from jax.experimental import pallas as pl
from jax.experimental.pallas import tpu as pltpu

F32 = jnp.float32
BF16 = jnp.bfloat16
MESH = pl.DeviceIdType.MESH

D_MODEL = 1024
LRU_WIDTH = 512
LRU_HEADS = 8
GMLP_GROUPS = 4
GMLP_BLOCK = 128
CHUNK = 64
D_FF = 3072
N_MOD = 6
EPS = 1e-6
LRU_C = 8.0
N_CHIPS = 4
N_DEV = 8

ADAM_LR = 0.001
ADAM_B1 = 0.9
ADAM_B2 = 0.999
ADAM_EPS = 1e-08
ADAM_WD = 0.01
ADAM_STEP = 10

GELU_C0 = math.sqrt(2.0 / math.pi)
GELU_C1 = 0.044715

VMEM_LIMIT_BYTES = 56 * 1024 * 1024
SUBLANES = 8
BF16_SUBLANES = 16
FFN_CHUNK = 768
SUB_ROWS = 256


def _gelu_gate(x):
    x2 = x * x
    z = x * ((2.0 * GELU_C0 * GELU_C1) * x2 + 2.0 * GELU_C0)
    return 1.0 / (1.0 + jnp.exp(-z)), x2


def _gelu(x):
    t = jnp.tanh(GELU_C0 * (x + GELU_C1 * x * x * x))
    return 0.5 * x * (1.0 + t)


def _gelu_and_grad(x):
    s, x2 = _gelu_gate(x)
    g = x * s
    dz = (6.0 * GELU_C0 * GELU_C1) * x2 + 2.0 * GELU_C0
    return g, s + g * (1.0 - s) * dz


def _sigmoid(x):
    return 1.0 / (1.0 + jnp.exp(-x))


def _log1p(u):
    w = 1.0 + u
    return jnp.where(w == 1.0, u, jnp.log(w) * (u / (w - 1.0)))


def _softplus(x):
    return jnp.maximum(x, 0.0) + _log1p(jnp.exp(-jnp.abs(x)))


def _neg_expm1(x):
    u = jnp.exp(x)
    um1 = u - 1.0
    tiny = um1 == 0.0
    small = um1 * (x / jnp.log(jnp.where(tiny, 2.0, jnp.maximum(u, 0.25))))
    return -jnp.where(tiny, x, jnp.where(x < -1.0, um1, small))


def _msq_rsqrt(v):
    return lax.rsqrt(jnp.mean(v * v, axis=-1, keepdims=True) + EPS)


def _rms_bwd(dyn, yn, r):
    return r * (dyn - yn * jnp.mean(dyn * yn, axis=-1, keepdims=True))


def _colsum(v):
    return jnp.sum(v, axis=0, keepdims=True)


def _shift_down(cur, prev8, k):
    rolled = pltpu.roll(cur, k, 0)
    head = pltpu.roll(prev8, k, 0)
    row8 = lax.broadcasted_iota(jnp.int32, (SUBLANES, cur.shape[1]), 0)
    first = jnp.where(row8 < k, head, rolled[0:SUBLANES])
    return jnp.concatenate([first, rolled[SUBLANES:]], axis=0)


def _shift_up(cur, next8, k):
    t = cur.shape[0]
    rolled = pltpu.roll(cur, t - k, 0)
    tail = pltpu.roll(next8, SUBLANES - k, 0)
    row8 = lax.broadcasted_iota(jnp.int32, (SUBLANES, cur.shape[1]), 0)
    last = jnp.where(row8 >= SUBLANES - k, tail, rolled[t - SUBLANES:])
    return jnp.concatenate([rolled[:t - SUBLANES], last], axis=0)


def _scan_fwd(a, b):
    t = a.shape[0]
    row = lax.broadcasted_iota(jnp.int32, a.shape, 0)
    d = 1
    while d < t:
        keep = row >= d
        a_s = jnp.where(keep, pltpu.roll(a, d, 0), 1.0)
        b_s = jnp.where(keep, pltpu.roll(b, d, 0), 0.0)
        b = a * b_s + b
        a = a * a_s
        d *= 2
    return a, b


def _scan_bwd(a, g):
    t = a.shape[0]
    row = lax.broadcasted_iota(jnp.int32, a.shape, 0)
    d = 1
    while d < t:
        keep = row < t - d
        a_s = jnp.where(keep, pltpu.roll(a, t - d, 0), 1.0)
        g_s = jnp.where(keep, pltpu.roll(g, t - d, 0), 0.0)
        g = a * g_s + g
        a = a * a_s
        d *= 2
    return a, g


def _dot(a, b):
    return jnp.dot(a, b, preferred_element_type=F32)


def _dot_nt(a, b):
    return lax.dot_general(a, b, (((1,), (1,)), ((), ())), preferred_element_type=F32)


def _dot_tn(a, b):
    return lax.dot_general(a, b, (((0,), (0,)), ((), ())), preferred_element_type=F32)


def _rows(ts, cols, rev_of=None):
    if rev_of is None:
        return pl.BlockSpec((ts, cols), lambda i: (i, 0))
    return pl.BlockSpec((ts, cols), lambda i: (rev_of - 1 - i, 0))


def _halo_prev(ts, cols, halo, rev_of=None, col_block=0):
    per = ts // halo
    if rev_of is None:
        return pl.BlockSpec((halo, cols), lambda i: (jnp.maximum(i * per - 1, 0), col_block))
    return pl.BlockSpec((halo, cols), lambda i: (jnp.maximum((rev_of - 1 - i) * per - 1, 0), col_block))


def _full(shape):
    nd = len(shape)
    return pl.BlockSpec(shape, lambda *_: (0,) * nd)


_RESIDENT = pl.BlockSpec(memory_space=pltpu.VMEM)


def _params(sem):
    return pltpu.CompilerParams(dimension_semantics=sem, vmem_limit_bytes=VMEM_LIMIT_BYTES)


def _sds(shape, dtype):
    return jax.ShapeDtypeStruct(shape, dtype)


def _sub_tiles(ts):
    return [slice(r0, r0 + SUB_ROWS) for r0 in range(0, ts, SUB_ROWS)]


def _mix_in(x, sc, sh, g, w_in4, ts=512):
    s, d = x.shape

    def body(x_ref, sc_ref, sh_ref, g_ref, w_ref, z_ref, h_ref):
        for rs in _sub_tiles(ts):
            xv = x_ref[rs, :]
            h = (xv * _msq_rsqrt(xv) * g_ref[...]) * (1.0 + sc_ref[...]) + sh_ref[...]
            hb = h.astype(BF16)
            h_ref[rs, :] = hb
            for k in range(N_CHIPS):
                z_ref[rs, k * 512:(k + 1) * 512] = _dot(hb, w_ref[k])

    return pl.pallas_call(
        body, grid=(s // ts,), name="mix_in",
        in_specs=[_rows(ts, d), _full((1, d)), _full((1, d)), _full((1, d)), _full(w_in4.shape)],
        out_specs=[_rows(ts, 2048), _rows(ts, d)],
        out_shape=[_sds((s, 2048), F32), _sds((s, d), BF16)],
        compiler_params=_params(("parallel",)),
    )(x, sc, sh, g, w_in4)


N_STASH = 12
(ST_XC, ST_R, ST_IG, ST_A, ST_MULT, ST_GL, ST_DGL, ST_U, ST_DU, ST_Q, ST_VHAT, ST_SPB) = range(N_STASH)


def _seq_param_specs():
    return [_full((4, 512)), _full((1, 512)), _full((512, 512)), _full((512, 512)), _full((1, 512)),
            _full((1, 512)), _full((1, 512)), _full((1, 512)), _full((1, 512)), _full((4, 128, 128)),
            _full((128, 4))]


def _seqmix(z, seq_params, glo, ggo, ts=256):
    s = z.shape[0]
    nt = s // ts

    def body(z_ref, zprev_ref, cw_ref, cb_ref, bdr_ref, bdi_ref, br_ref, bi_ref, la_ref, ng_ref, nb_ref,
             ws_ref, bst_ref, glo_ref, ggo_ref, ycat_ref, hst_ref, st_ref, hcarry, sp_scr):
        i = pl.program_id(0)

        @pl.when(i == 0)
        def _():
            hcarry[...] = jnp.zeros_like(hcarry)

        lx = z_ref[:, 0:512]
        prev8 = jnp.where(i == 0, 0.0, zprev_ref[...])
        xc = (cw_ref[3:4, :] * lx + cw_ref[2:3, :] * _shift_down(lx, prev8, 1)
              + cw_ref[1:2, :] * _shift_down(lx, prev8, 2) + cw_ref[0:1, :] * _shift_down(lx, prev8, 3)
              + cb_ref[...])
        xcb = xc.astype(BF16)
        r = _sigmoid(_dot(xcb, bdr_ref[...]) + br_ref[...])
        ig = _sigmoid(_dot(xcb, bdi_ref[...]) + bi_ref[...])
        log_a = (-LRU_C) * r * _softplus(-la_ref[...])
        a = jnp.exp(log_a)
        mult = jnp.sqrt(_neg_expm1(2.0 * log_a))
        acum, hloc = _scan_fwd(a, mult * (ig * xc))
        h = hloc + acum * hcarry[...]
        hcarry[...] = h[ts - 1:ts, :]
        hst_ref[...] = h
        gl, dgl = _gelu_and_grad(z_ref[:, 512:1024])
        y_l = h * gl
        for slot, val in ((ST_XC, xc), (ST_R, r), (ST_IG, ig), (ST_A, a), (ST_MULT, mult), (ST_GL, gl),
                          (ST_DGL, dgl)):
            st_ref[slot] = val

        u, du = _gelu_and_grad(z_ref[:, 1024:1536])
        vg, dvg = _gelu_and_grad(z_ref[:, 1536:2048])
        vc = vg - jnp.mean(vg, axis=-1, keepdims=True)
        rstd = lax.rsqrt(jnp.mean(vc * vc, axis=-1, keepdims=True) + EPS)
        vhat = vc * rstd
        vb = (vhat * ng_ref[...] + nb_ref[...]).astype(BF16)
        for n in range(ts // GMLP_BLOCK):
            rs = slice(n * GMLP_BLOCK, (n + 1) * GMLP_BLOCK)
            for g in range(GMLP_GROUPS):
                cs = slice(g * 128, (g + 1) * 128)
                sp_scr[rs, cs] = _dot(ws_ref[g], vb[rs, cs]) + bst_ref[:, g:g + 1]
        spb = sp_scr[...]
        y_g = u * spb
        for slot, val in ((ST_U, u), (ST_DU, du), (ST_Q, rstd * dvg), (ST_VHAT, vhat), (ST_SPB, spb)):
            st_ref[slot] = val

        ycat_ref[:, 0:512] = (y_l * _msq_rsqrt(y_l) * glo_ref[...]).astype(BF16)
        ycat_ref[:, 512:1024] = (y_g * _msq_rsqrt(y_g) * ggo_ref[...]).astype(BF16)

    return pl.pallas_call(
        body, grid=(nt,), name="seqmix",
        in_specs=[_rows(ts, 2048), _halo_prev(ts, 512, SUBLANES)] + _seq_param_specs()
        + [_full((1, 512)), _full((1, 512))],
        out_specs=[_rows(ts, 1024), _rows(ts, 512), pl.BlockSpec((N_STASH, ts, 512), lambda i: (0, i, 0))],
        out_shape=[_sds((s, 1024), BF16), _sds((s, 512), F32), _sds((N_STASH, s, 512), F32)],
        scratch_shapes=[pltpu.VMEM((1, 512), F32), pltpu.VMEM((ts, 512), F32)],
        compiler_params=_params(("arbitrary",)),
    )(z, z, *seq_params, glo, ggo)


def _mix_out(ycat, x, w_out, gt_m, g_post, g_pre2, sc_f, sh_f, ts=512):
    s, d = x.shape

    def body(yc_ref, x_ref, w_ref, gt_ref, gp_ref, g2_ref, sc_ref, sh_ref, y_ref, x1_ref, h2_ref):
        for rs in _sub_tiles(ts):
            y = _dot(yc_ref[rs, :], w_ref[...])
            y_ref[rs, :] = y
            x1 = x_ref[rs, :] + gt_ref[...] * (y * _msq_rsqrt(y) * gp_ref[...])
            x1_ref[rs, :] = x1
            h2 = (x1 * _msq_rsqrt(x1) * g2_ref[...]) * (1.0 + sc_ref[...]) + sh_ref[...]
            h2_ref[rs, :] = h2.astype(BF16)

    vec = _full((1, d))
    return pl.pallas_call(
        body, grid=(s // ts,), name="mix_out",
        in_specs=[_rows(ts, d), _rows(ts, d), _full((d, d)), vec, vec, vec, vec, vec],
        out_specs=[_rows(ts, d), _rows(ts, d), _rows(ts, d)],
        out_shape=[_sds((s, d), F32), _sds((s, d), F32), _sds((s, d), BF16)],
        compiler_params=_params(("parallel",)),
    )(ycat, x, w_out, gt_m, g_post, g_pre2, sc_f, sh_f)


def _ffn_cols(j):
    per = (2 * D_FF // N_CHIPS) // FFN_CHUNK
    return j // per, (j % per) * FFN_CHUNK, j * FFN_CHUNK


def _ffn_fwd(h2, x1, tgt, w_up4, w_down, fw, fb, gt_f, g_post, ts=256):
    s, d = x1.shape
    nch = D_FF // FFN_CHUNK

    def body(h2_ref, x1_ref, tgt_ref, wup_ref, wdn_ref, fw_ref, fb_ref, gt_ref, gp_ref,
             up0_ref, pre_ref, act_ref, dy2_ref, dx2_ref, loss_ref, dgt_ref, dgp_ref, tail_ref):
        i = pl.program_id(0)

        @pl.when(i == 0)
        def _():
            tail_ref[...] = jnp.zeros_like(tail_ref)
            loss_ref[...] = jnp.zeros_like(loss_ref)
            dgt_ref[...] = jnp.zeros_like(dgt_ref)
            dgp_ref[...] = jnp.zeros_like(dgp_ref)

        hb = h2_ref[...]

        def up_project(j):
            sh_g, off, _ = _ffn_cols(j)
            return [_dot(hb, wup_ref[shard, :, off:off + FFN_CHUNK]).astype(BF16) for shard in (sh_g, sh_g + 2)]

        y2 = jnp.zeros((ts, d), F32)
        ahead = up_project(0)
        for j in range(nch):
            _, _, col = _ffn_cols(j)
            ubs = ahead
            if j + 1 < nch:
                ahead = up_project(j + 1)
            halves = []
            for ub, c0 in zip(ubs, (col, D_FF + col)):
                cs = slice(c0, c0 + FFN_CHUNK)
                up0_ref[:, cs] = ub
                u = ub.astype(F32)
                prev8 = tail_ref[:, cs]
                tail_ref[:, cs] = u[ts - SUBLANES:, :]
                halves.append(fw_ref[2:3, cs] * u + fw_ref[1:2, cs] * _shift_down(u, prev8, 1)
                              + fw_ref[0:1, cs] * _shift_down(u, prev8, 2) + fb_ref[:, cs])
                pre_ref[:, cs] = halves[-1].astype(BF16)
            act = (_gelu(halves[0]) * halves[1]).astype(BF16)
            act_ref[:, col:col + FFN_CHUNK] = act
            y2 = y2 + _dot(act, wdn_ref[col:col + FFN_CHUNK, :])
        r2 = _msq_rsqrt(y2)
        yn = y2 * r2
        yng = yn * gp_ref[...]
        e = x1_ref[...] + gt_ref[...] * yng - tgt_ref[...]
        loss_ref[...] += jnp.sum(e * e) * (0.5 / d)
        dx2 = e * (1.0 / d)
        dx2_ref[...] = dx2
        dgt_ref[...] += _colsum(dx2 * yng)
        dyng = dx2 * gt_ref[...]
        dgp_ref[...] += _colsum(dyng * yn)
        dy2_ref[...] = _rms_bwd(dyng * gp_ref[...], yn, r2).astype(BF16)

    vec = _full((1, d))
    return pl.pallas_call(
        body, grid=(s // ts,), name="ffn_fwd",
        in_specs=[_rows(ts, d), _rows(ts, d), _rows(ts, d), _RESIDENT, _RESIDENT,
                  _full((3, 2 * D_FF)), _full((1, 2 * D_FF)), vec, vec],
        out_specs=[_rows(ts, 2 * D_FF), _rows(ts, 2 * D_FF), _rows(ts, D_FF), _rows(ts, d), _rows(ts, d),
                   _full((1, 128)), vec, vec],
        out_shape=[_sds((s, 2 * D_FF), BF16), _sds((s, 2 * D_FF), BF16), _sds((s, D_FF), BF16), _sds((s, d), BF16),
                   _sds((s, d), F32), _sds((1, 128), F32), _sds((1, d), F32), _sds((1, d), F32)],
        scratch_shapes=[pltpu.VMEM((SUBLANES, 2 * D_FF), F32)],
        compiler_params=_params(("arbitrary",)),
    )(h2, x1, tgt, w_up4, w_down, fw, fb, gt_f, g_post)


def _shift_up_mxu(vb, up_mat, next8, k):
    t = vb.shape[0]
    main = _dot(up_mat, vb)
    tail = pltpu.roll(next8, SUBLANES - k, 0)
    row8 = lax.broadcasted_iota(jnp.int32, next8.shape, 0)
    last = main[t - SUBLANES:] + jnp.where(row8 >= SUBLANES - k, tail, 0.0)
    return jnp.concatenate([main[:t - SUBLANES], last], axis=0)


def _ffn_bwd_a(dy2, pre, up0, w_down, fw, ts=256):
    s, d = dy2.shape
    nt = s // ts
    nch = D_FF // FFN_CHUNK
    wide = 2 * D_FF
    up_mats = jnp.stack([jnp.eye(ts, k=1, dtype=BF16), jnp.eye(ts, k=2, dtype=BF16)])

    def body(dy2_ref, pre_ref, up0_ref, wdn_ref, fw_ref, um_ref, dup0_ref, dfw_ref, dfb_ref, next_ref):
        i = pl.program_id(0)

        @pl.when(i == 0)
        def _():
            next_ref[...] = jnp.zeros_like(next_ref)
            dfw_ref[...] = jnp.zeros_like(dfw_ref)
            dfb_ref[...] = jnp.zeros_like(dfb_ref)

        dyb = dy2_ref[...]
        for j in range(nch):
            _, _, col = _ffn_cols(j)
            dact = _dot_nt(dyb, wdn_ref[col:col + FFN_CHUNK, :])
            gl, dgl = _gelu_and_grad(pre_ref[:, col:col + FFN_CHUNK].astype(F32))
            dpre = (dact * pre_ref[:, D_FF + col:D_FF + col + FFN_CHUNK].astype(F32) * dgl, dact * gl)
            for half, c0 in enumerate((col, D_FF + col)):
                cs = slice(c0, c0 + FFN_CHUNK)
                dp = dpre[half]
                dpb = dp.astype(BF16)
                nxt = next_ref[:, cs]
                next_ref[:, cs] = dpb.astype(F32)[0:SUBLANES, :]
                su1 = _shift_up_mxu(dpb, um_ref[0], nxt, 1)
                su2 = _shift_up_mxu(dpb, um_ref[1], nxt, 2)
                u = up0_ref[:, cs].astype(F32)
                dfb_ref[:, cs] += _colsum(dp)
                dfw_ref[2:3, cs] += _colsum(dp * u)
                dfw_ref[1:2, cs] += _colsum(su1 * u)
                dfw_ref[0:1, cs] += _colsum(su2 * u)
                dup0 = fw_ref[2:3, cs] * dp + fw_ref[1:2, cs] * su1 + fw_ref[0:1, cs] * su2
                dup0_ref[:, cs] = dup0.astype(BF16)

    return pl.pallas_call(
        body, grid=(nt,), name="ffn_bwd_a",
        in_specs=[_rows(ts, d, nt), _rows(ts, wide, nt), _rows(ts, wide, nt), _RESIDENT,
                  _full((3, wide)), _full((2, ts, ts))],
        out_specs=[_rows(ts, wide, nt), _full((3, wide)), _full((1, wide))],
        out_shape=[_sds((s, wide), BF16), _sds((3, wide), F32), _sds((1, wide), F32)],
        scratch_shapes=[pltpu.VMEM((SUBLANES, wide), F32)],
        compiler_params=_params(("arbitrary",)),
    )(dy2, pre, up0, w_down, fw, up_mats)


def _ffn_bwd_b(dup0, x1, y, dx2, w_up4, g_pre2, sc_f, sh_f, gt_m, g_post_m, ts=512):
    s, d = x1.shape
    shard_cols = 2 * D_FF // N_CHIPS

    def body(dup_ref, x1_ref, y_ref, dx2_ref, wup_ref, g2_ref, sc_ref, sh_ref, gt_ref, gp_ref,
             dx1_ref, dy_ref, dsh_ref, dsc_ref, dg2_ref, dgt_ref, dgp_ref):
        i = pl.program_id(0)

        @pl.when(i == 0)
        def _():
            for ref in (dsh_ref, dsc_ref, dg2_ref, dgt_ref, dgp_ref):
                ref[...] = jnp.zeros_like(ref)

        for rs in _sub_tiles(ts):
            dh2 = jnp.zeros((SUB_ROWS, d), F32)
            for k in range(N_CHIPS):
                dh2 = dh2 + _dot_nt(dup_ref[rs, k * shard_cols:(k + 1) * shard_cols], wup_ref[k])
            x1v = x1_ref[rs, :]
            r2 = _msq_rsqrt(x1v)
            xn = x1v * r2
            hn = xn * g2_ref[...]
            dsh_ref[...] += _colsum(dh2)
            dsc_ref[...] += _colsum(dh2 * hn)
            dhn = dh2 * (1.0 + sc_ref[...])
            dg2_ref[...] += _colsum(dhn * xn)
            dx1 = dx2_ref[rs, :] + _rms_bwd(dhn * g2_ref[...], xn, r2)
            dx1_ref[rs, :] = dx1
            yv = y_ref[rs, :]
            ry = _msq_rsqrt(yv)
            yn = yv * ry
            dgt_ref[...] += _colsum(dx1 * (yn * gp_ref[...]))
            dyng = dx1 * gt_ref[...]
            dgp_ref[...] += _colsum(dyng * yn)
            dy_ref[rs, :] = _rms_bwd(dyng * gp_ref[...], yn, ry).astype(BF16)

    vec = _full((1, d))
    return pl.pallas_call(
        body, grid=(s // ts,), name="ffn_bwd_b",
        in_specs=[_rows(ts, 2 * D_FF), _rows(ts, d), _rows(ts, d), _rows(ts, d), _RESIDENT,
                  vec, vec, vec, vec, vec],
        out_specs=[_rows(ts, d), _rows(ts, d), vec, vec, vec, vec, vec],
        out_shape=[_sds((s, d), F32), _sds((s, d), BF16)] + [_sds((1, d), F32)] * 5,
        compiler_params=_params(("arbitrary",)),
    )(dup0, x1, y, dx2, w_up4, g_pre2, sc_f, sh_f, gt_m, g_post_m)


def _seqmix_bwd(z, hst, stash, dy, w_out, seq_params, ws_t, glo, ggo, ts=256):
    s = z.shape[0]
    nt = s // ts
    small_shapes = [(4, 512), (1, 512), (512, 512), (512, 512), (1, 512), (1, 512), (1, 512),
                    (1, 512), (1, 512), (4, 128, 128), (128, 4), (1, 512), (1, 512)]

    def body(lx_ref, hst_ref, hprev_ref, st_ref, dy_ref, wout_ref, cw_ref, cb_ref, bdr_ref, bdi_ref, br_ref,
             bi_ref, la_ref, ng_ref, nb_ref, ws_ref, bst_ref, wst_ref, glo_ref, ggo_ref, dz_ref, *rest):
        small_refs = rest[:13]
        (dcw_ref, dcb_ref, dwr_ref, dwi_ref, dbr_ref, dbi_ref, dspa_ref, dng_ref, dnb_ref, dws_ref, dbs_ref,
         dglo_ref, dggo_ref) = small_refs
        gcarry, anext, dxcnext, dv_scr = rest[13:]
        i = pl.program_id(0)

        @pl.when(i == 0)
        def _():
            for ref in small_refs:
                ref[...] = jnp.zeros_like(ref)
            gcarry[...] = jnp.zeros_like(gcarry)
            anext[...] = jnp.ones_like(anext)
            dxcnext[...] = jnp.zeros_like(dxcnext)

        first_tile = i == nt - 1
        xc, r, ig, a, mult = st_ref[ST_XC], st_ref[ST_R], st_ref[ST_IG], st_ref[ST_A], st_ref[ST_MULT]
        gl, u, spb, vhat = st_ref[ST_GL], st_ref[ST_U], st_ref[ST_SPB], st_ref[ST_VHAT]
        lx = lx_ref[...]
        h = hst_ref[...]
        hprev = _shift_down(h, jnp.where(first_tile, 0.0, hprev_ref[...]), 1)
        y_l = h * gl
        y_g = u * spb

        dycat = _dot_nt(dy_ref[...], wout_ref[...])

        def emit_dz(k, val):
            dz_ref[:, k * 512:(k + 1) * 512] = val.astype(BF16)

        rl = _msq_rsqrt(y_l)
        yln = y_l * rl
        dyl = dycat[:, 0:512]
        dglo_ref[...] += _colsum(dyl * yln)
        dy_l = _rms_bwd(dyl * glo_ref[...], yln, rl)
        rg = _msq_rsqrt(y_g)
        ygn = y_g * rg
        dyg = dycat[:, 512:1024]
        dggo_ref[...] += _colsum(dyg * ygn)
        dy_g = _rms_bwd(dyg * ggo_ref[...], ygn, rg)

        emit_dz(1, dy_l * h * st_ref[ST_DGL])
        a_up = _shift_up(a, anext[...], 1)
        acum, gloc = _scan_bwd(a_up, dy_l * gl)
        gg = gloc + acum * gcarry[...]
        gcarry[...] = gg[0:1, :]
        anext[...] = a[0:SUBLANES, :]
        da = gg * hprev
        t1 = gg * mult
        di = t1 * xc
        dxc = t1 * ig
        dmult = gg * ig * xc
        dla = da * a - dmult * (a * a / mult)
        dspa_ref[...] += _colsum(dla * r) * (-LRU_C)
        dpr = dla * ((-LRU_C) * _softplus(-la_ref[...])) * r * (1.0 - r)
        dpi = di * ig * (1.0 - ig)
        dbr_ref[...] += _colsum(dpr)
        dbi_ref[...] += _colsum(dpi)
        dprb = dpr.astype(BF16)
        dpib = dpi.astype(BF16)
        xcb = xc.astype(BF16)
        dwr_ref[...] += _dot_tn(xcb, dprb)
        dwi_ref[...] += _dot_tn(xcb, dpib)
        dxc = dxc + _dot_nt(dprb, bdr_ref[...]) + _dot_nt(dpib, bdi_ref[...])
        nxt = dxcnext[...]
        dxcnext[...] = dxc[0:SUBLANES, :]
        up1, up2, up3 = _shift_up(dxc, nxt, 1), _shift_up(dxc, nxt, 2), _shift_up(dxc, nxt, 3)
        dcb_ref[...] += _colsum(dxc)
        dcw_ref[3:4, :] += _colsum(dxc * lx)
        dcw_ref[2:3, :] += _colsum(up1 * lx)
        dcw_ref[1:2, :] += _colsum(up2 * lx)
        dcw_ref[0:1, :] += _colsum(up3 * lx)
        dlx = cw_ref[3:4, :] * dxc + cw_ref[2:3, :] * up1 + cw_ref[1:2, :] * up2 + cw_ref[0:1, :] * up3
        emit_dz(0, dlx)

        emit_dz(2, dy_g * spb * st_ref[ST_DU])
        dsp = dy_g * u
        vb = (vhat * ng_ref[...] + nb_ref[...]).astype(BF16)
        for n in range(ts // GMLP_BLOCK):
            rs = slice(n * GMLP_BLOCK, (n + 1) * GMLP_BLOCK)
            for g in range(GMLP_GROUPS):
                cs = slice(g * 128, (g + 1) * 128)
                dbs_ref[:, g:g + 1] += jnp.sum(dsp[rs, cs], axis=1, keepdims=True)
                blk = dsp[rs, cs].astype(BF16)
                dws_ref[g] += _dot_nt(blk, vb[rs, cs])
                dv_scr[rs, cs] = _dot(wst_ref[g], blk)
        dv = dv_scr[...]
        dng_ref[...] += _colsum(dv * vhat)
        dnb_ref[...] += _colsum(dv)
        dvh = dv * ng_ref[...]
        dvg = dvh - jnp.mean(dvh, axis=-1, keepdims=True) - vhat * jnp.mean(dvh * vhat, axis=-1, keepdims=True)
        emit_dz(3, dvg * st_ref[ST_Q])

        @pl.when(i == nt - 1)
        def _():
            pos = lax.broadcasted_iota(jnp.int32, (GMLP_BLOCK, GMLP_BLOCK), 0) // CHUNK
            src = lax.broadcasted_iota(jnp.int32, (GMLP_BLOCK, GMLP_BLOCK), 1) // CHUNK
            for g in range(GMLP_GROUPS):
                dws_ref[g] = jnp.where(src <= pos, dws_ref[g], 0.0)
            dspa_ref[...] = dspa_ref[...] * (-_sigmoid(-la_ref[...]))

    in_specs = ([_rows(ts, 512, nt), _rows(ts, 512, nt), _halo_prev(ts, 512, SUBLANES, nt),
                 pl.BlockSpec((N_STASH, ts, 512), lambda i: (0, nt - 1 - i, 0)), _rows(ts, 1024, nt),
                 _full((1024, 1024))]
                + _seq_param_specs() + [_full((4, 128, 128)), _full((1, 512)), _full((1, 512))])
    return pl.pallas_call(
        body, grid=(nt,), name="seqmix_bwd",
        in_specs=in_specs,
        out_specs=[_rows(ts, 2048, nt)] + [_full(sh) for sh in small_shapes],
        out_shape=[_sds((s, 2048), BF16)] + [_sds(sh, F32) for sh in small_shapes],
        scratch_shapes=[pltpu.VMEM((1, 512), F32), pltpu.VMEM((SUBLANES, 512), F32),
                        pltpu.VMEM((SUBLANES, 512), F32), pltpu.VMEM((ts, 512), F32)],
        compiler_params=_params(("arbitrary",)),
    )(z, hst, hst, stash, dy, w_out, *seq_params, ws_t, glo, ggo)


def _mix_in_bwd(x, dz, dx1, w_in4, g, sc, ts=512):
    s, d = x.shape

    def body(x_ref, dz_ref, dx1_ref, w_ref, g_ref, sc_ref, gx_ref, dsh_ref, dsc_ref, dg_ref):
        i = pl.program_id(0)

        @pl.when(i == 0)
        def _():
            for ref in (dsh_ref, dsc_ref, dg_ref):
                ref[...] = jnp.zeros_like(ref)

        for rs in _sub_tiles(ts):
            dh = jnp.zeros((SUB_ROWS, d), F32)
            for k in range(N_CHIPS):
                dh = dh + _dot_nt(dz_ref[rs, k * 512:(k + 1) * 512], w_ref[k])
            xv = x_ref[rs, :]
            r = _msq_rsqrt(xv)
            xn = xv * r
            dsh_ref[...] += _colsum(dh)
            dsc_ref[...] += _colsum(dh * (xn * g_ref[...]))
            dhn = dh * (1.0 + sc_ref[...])
            dg_ref[...] += _colsum(dhn * xn)
            gx_ref[rs, :] = dx1_ref[rs, :] + _rms_bwd(dhn * g_ref[...], xn, r)

    vec = _full((1, d))
    return pl.pallas_call(
        body, grid=(s // ts,), name="mix_in_bwd",
        in_specs=[_rows(ts, d), _rows(ts, 2048), _rows(ts, d), _full(w_in4.shape), vec, vec],
        out_specs=[_rows(ts, d), vec, vec, vec],
        out_shape=[_sds((s, d), F32)] + [_sds((1, d), F32)] * 3,
        compiler_params=_params(("arbitrary",)),
    )(x, dz, dx1, w_in4, g, sc)


def _wgrad(a, b, n_chunks, name, chunk_major, ts=2048):
    s, m = a.shape
    n = b.shape[1]
    nc = n // n_chunks
    nt = s // ts

    def body(a_ref, b_ref, o_ref, acc):
        i = pl.program_id(1)

        @pl.when(i == 0)
        def _():
            acc[...] = jnp.zeros_like(acc)

        acc[...] += _dot_tn(a_ref[...], b_ref[...])

        @pl.when(i == nt - 1)
        def _():
            if chunk_major:
                o_ref[0] = acc[...].astype(BF16)
            else:
                o_ref[...] = acc[...].astype(BF16)

    if chunk_major:
        out_spec, out_shape = pl.BlockSpec((1, m, nc), lambda c, i: (c, 0, 0)), _sds((n_chunks, m, nc), BF16)
    else:
        out_spec, out_shape = pl.BlockSpec((m, nc), lambda c, i: (0, c)), _sds((m, n), BF16)
    return pl.pallas_call(
        body, grid=(n_chunks, nt), name=name,
        in_specs=[pl.BlockSpec((ts, m), lambda c, i: (i, 0)), pl.BlockSpec((ts, nc), lambda c, i: (i, c))],
        out_specs=out_spec,
        out_shape=out_shape,
        scratch_shapes=[pltpu.VMEM((m, nc), F32)],
        compiler_params=_params(("parallel", "arbitrary")),
    )(a, b)


def _block_diag(w):
    heads, hd, _ = w.shape
    eye = jnp.eye(heads, dtype=w.dtype)
    return (eye[:, None, :, None] * w[:, :, None, :]).reshape(heads * hd, heads * hd)


def _seq_params(small):
    row = lambda v: v.reshape(1, -1)
    pos = jnp.arange(GMLP_BLOCK)
    mask = (pos[None, :] // CHUNK) <= (pos[:, None] // CHUNK)
    ws = jnp.where(mask[None], small["w_spatial"], 0.0)
    seq_params = (small["conv_w"], row(small["conv_b"]),
                  _block_diag(small["w_rgate"]).astype(BF16), _block_diag(small["w_igate"]).astype(BF16),
                  row(small["b_rgate"]), row(small["b_igate"]), row(small["lru_a"]),
                  row(small["v_norm_g"]), row(small["v_norm_b"]), ws.astype(BF16), small["b_spatial"].T)
    return seq_params, jnp.swapaxes(ws, 1, 2).astype(BF16)


_ANY = pl.BlockSpec(memory_space=pl.ANY)
_CHIP_FLIPS = ((1, 0), (0, 1), (1, 1))


def _position():
    return lax.axis_index("x"), lax.axis_index("y"), lax.axis_index("c")


def _flip(v, f):
    return 1 - v if f else v


def _remote(src, dst, send_sem, recv_sem, peer):
    return pltpu.make_async_remote_copy(src_ref=src, dst_ref=dst, send_sem=send_sem, recv_sem=recv_sem,
                                        device_id=peer, device_id_type=MESH)


def _allgather8(block, name):
    r, n = block.shape

    def body(x_ref, gath, send_sems, recv_sems, loc_sem):
        x, y, c = _position()
        me = 4 * x + 2 * y + c
        loc = pltpu.make_async_copy(x_ref, gath.at[me], loc_sem)
        loc.start()
        peers = []
        for k in range(1, N_DEV):
            px, py, pc = _flip(x, k & 4), _flip(y, k & 2), _flip(c, k & 1)
            peers.append((px, py, pc))
            _remote(x_ref, gath.at[me], send_sems.at[k - 1], recv_sems.at[k - 1], (px, py, pc)).start()
        for k, (px, py, pc) in enumerate(peers):
            src = 4 * px + 2 * py + pc
            _remote(x_ref, gath.at[src], send_sems.at[k], recv_sems.at[k], (px, py, pc)).wait_recv()
        for k, peer in enumerate(peers):
            _remote(x_ref, gath.at[me], send_sems.at[k], recv_sems.at[k], peer).wait_send()
        loc.wait()

    return pl.pallas_call(
        body, name=name, out_shape=_sds((N_DEV, r, n), F32),
        in_specs=[pl.BlockSpec(memory_space=pltpu.VMEM)], out_specs=pl.BlockSpec(memory_space=pltpu.VMEM),
        scratch_shapes=[pltpu.SemaphoreType.DMA((N_DEV - 1,)), pltpu.SemaphoreType.DMA((N_DEV - 1,)),
                        pltpu.SemaphoreType.DMA],
        compiler_params=pltpu.CompilerParams(vmem_limit_bytes=VMEM_LIMIT_BYTES),
    )(block)


def _half(ref, c, rows):
    hr = rows // 2
    return ref.at[pl.ds(pl.multiple_of(c * hr, BF16_SUBLANES), hr), :]


def _chip_sum(part, recv, pos_arr, name):
    _, rows, cols = part.shape
    hr = rows // 2

    def body(pos_ref, p_ref, r_ref, o_ref, g_ref):
        total = (p_ref[...].astype(F32) + r_ref[...].astype(F32)).astype(BF16)
        o_ref[...] = total

        @pl.when(pl.program_id(0) == pos_ref[1])
        def _():
            g_ref[0] = total

    grid_spec = pltpu.PrefetchScalarGridSpec(
        num_scalar_prefetch=1, grid=(N_CHIPS,),
        in_specs=[pl.BlockSpec((1, hr, cols), lambda k, pos: (k, pos[0], 0)),
                  pl.BlockSpec((1, hr, cols), lambda k, pos: (k, 0, 0))],
        out_specs=[pl.BlockSpec((1, hr, cols), lambda k, pos: (k, 0, 0)),
                   pl.BlockSpec((1, 1, hr, cols), lambda k, pos: (0, pos[1], 0, 0))])
    return pl.pallas_call(
        body, name=name, grid_spec=grid_spec,
        out_shape=[_sds((N_CHIPS, hr, cols), BF16), _sds((2, N_CHIPS, hr, cols), BF16)],
        compiler_params=_params(("arbitrary",)),
    )(pos_arr, part, recv)


_HBM = pl.BlockSpec(memory_space=pltpu.HBM)
_SEM = pl.BlockSpec(memory_space=pltpu.SEMAPHORE)
_EFFECT = pltpu.SideEffectType.DATAFLOW_SIDE_EFFECTING


def _in_hbm(a):
    return pltpu.with_memory_space_constraint(a, pltpu.HBM)


def _split_start(srcs, lands, plan, n_copies, after, name):
    ns, nl = len(srcs), len(lands)
    bufs = list(srcs) + list(lands)

    def body(*refs):
        send_sems, recv_sems = refs[ns + nl + 1], refs[ns + nl + 2]
        token = refs[-1]
        for k, (src, dst, peer) in enumerate(plan(refs[:ns], refs[ns:ns + nl])):
            _remote(src, dst, send_sems.at[k], recv_sems.at[k], peer).start()
        token[...] = jnp.zeros_like(token)

    out = pl.pallas_call(
        body, name=name,
        out_shape=(pltpu.SemaphoreType.DMA((n_copies,)), pltpu.SemaphoreType.DMA((n_copies,)),
                   *[pltpu.HBM(b.shape, b.dtype) for b in bufs], _sds((SUBLANES, 128), F32)),
        in_specs=[_HBM] * (ns + nl) + [_ANY],
        out_specs=(_SEM, _SEM, *[_HBM] * (ns + nl), pl.BlockSpec(memory_space=pltpu.VMEM)),
        input_output_aliases={i: 2 + i for i in range(ns + nl)},
        compiler_params=pltpu.CompilerParams(has_side_effects=_EFFECT),
    )(*[_in_hbm(b) for b in bufs], after)
    return out[0], out[1], list(out[2:2 + ns]), list(out[2 + ns:2 + ns + nl]), out[-1]


def _split_wait(send_sems, recv_sems, srcs, lands, plan, after, name):
    ns, nl = len(srcs), len(lands)
    bufs = list(srcs) + list(lands)

    def body(*refs):
        send_ref, recv_ref = refs[ns + nl], refs[ns + nl + 1]
        me = _position()
        for k, src, dst in plan(refs[:ns], refs[ns:ns + nl]):
            cp = _remote(src, dst, send_ref.at[k], recv_ref.at[k], me)
            cp.wait_send()
            cp.wait_recv()

    out = pl.pallas_call(
        body, name=name,
        out_shape=[pltpu.HBM(b.shape, b.dtype) for b in bufs],
        in_specs=[_HBM] * (ns + nl) + [_SEM, _SEM, _ANY],
        out_specs=[_HBM] * (ns + nl),
        input_output_aliases={i: i for i in range(ns + nl)},
        compiler_params=pltpu.CompilerParams(has_side_effects=_EFFECT),
    )(*bufs, send_sems, recv_sems, after)
    return list(out[:ns]), list(out[ns:])


def _gather_plan(rows_of):
    def start(src_refs, land_refs):
        x, y, c = _position()
        chip = 2 * x + y
        out = []
        for a, rows in enumerate(rows_of):
            mine = _half(land_refs[a].at[chip], c, rows)
            out.extend((mine, mine, (_flip(x, fx), _flip(y, fy), c)) for fx, fy in _CHIP_FLIPS)
        return out

    def wait(src_refs, land_refs):
        x, y, c = _position()
        chip = 2 * x + y
        out = []
        for a, rows in enumerate(rows_of):
            for j, (fx, fy) in enumerate(_CHIP_FLIPS):
                src_chip = 2 * _flip(x, fx) + _flip(y, fy)
                out.append((3 * a + j, _half(land_refs[a].at[chip], c, rows),
                            _half(land_refs[a].at[src_chip], c, rows)))
        return out

    return start, wait


def _forward_plan(rows_of):
    def pieces(land_refs, half):
        x, y, _ = _position()
        return [_half(land_refs[a].at[2 * _flip(x, fx) + _flip(y, fy)], half, rows)
                for a, rows in enumerate(rows_of) for fx, fy in _CHIP_FLIPS]

    def start(src_refs, land_refs):
        x, y, c = _position()
        return [(p, p, (x, y, 1 - c)) for p in pieces(land_refs, c)]

    def wait(src_refs, land_refs):
        _, _, c = _position()
        return [(k, mine, theirs)
                for k, (mine, theirs) in enumerate(zip(pieces(land_refs, c), pieces(land_refs, 1 - c)))]

    return start, wait


def _swap_halves_plan(half_rows):
    def slices(src_refs, c):
        return [src_refs[a].at[:, pl.ds(pl.multiple_of((1 - c) * hr, BF16_SUBLANES), hr), :]
                for a, hr in enumerate(half_rows)]

    def start(src_refs, land_refs):
        x, y, c = _position()
        return [(src, land_refs[a], (x, y, 1 - c)) for a, src in enumerate(slices(src_refs, c))]

    def wait(src_refs, land_refs):
        _, _, c = _position()
        return [(a, src, land_refs[a]) for a, src in enumerate(slices(src_refs, c))]

    return start, wait


def _swap_gathered_plan(n_arrays):
    def start(src_refs, land_refs):
        x, y, c = _position()
        return [(land_refs[a].at[0], land_refs[a].at[1], (x, y, 1 - c)) for a in range(n_arrays)]

    def wait(src_refs, land_refs):
        return [(a, land_refs[a].at[0], land_refs[a].at[1]) for a in range(n_arrays)]

    return start, wait


def _exchange_plan(n_arrays):
    def start(src_refs, land_refs):
        x, y, c = _position()
        chip = 2 * x + y
        out = []
        for a in range(n_arrays):
            for fx, fy in _CHIP_FLIPS:
                px, py = _flip(x, fx), _flip(y, fy)
                out.append((src_refs[a].at[2 * px + py], land_refs[a].at[0, chip], (px, py, c)))
        return out

    def wait(src_refs, land_refs):
        x, y, c = _position()
        out = []
        for a in range(n_arrays):
            for j, (fx, fy) in enumerate(_CHIP_FLIPS):
                src_chip = 2 * _flip(x, fx) + _flip(y, fy)
                out.append((3 * a + j, src_refs[a].at[src_chip], land_refs[a].at[0, src_chip]))
        return out

    return start, wait


def _forward_to_sibling(lands, name):
    na = len(lands)

    def body(*refs):
        land_refs = refs[na:2 * na]
        send_sems, recv_sems = refs[2 * na:]
        x, y, c = _position()
        sibling = (x, y, 1 - c)
        sends = []
        for a in range(na):
            rows = lands[a].shape[1]
            for j, (fx, fy) in enumerate(_CHIP_FLIPS):
                landed = _half(land_refs[a].at[2 * _flip(x, fx) + _flip(y, fy)], c, rows)
                sends.append(_remote(landed, landed, send_sems.at[3 * a + j], recv_sems.at[3 * a + j], sibling))
                sends[-1].start()
        for a in range(na):
            rows = lands[a].shape[1]
            for j, (fx, fy) in enumerate(_CHIP_FLIPS):
                other = _half(land_refs[a].at[2 * _flip(x, fx) + _flip(y, fy)], 1 - c, rows)
                _remote(other, other, send_sems.at[3 * a + j], recv_sems.at[3 * a + j], sibling).wait_recv()
        for cp in sends:
            cp.wait_send()

    return pl.pallas_call(
        body, name=name,
        out_shape=[_sds(l.shape, l.dtype) for l in lands],
        in_specs=[_ANY] * na, out_specs=[_ANY] * na,
        input_output_aliases={a: a for a in range(na)},
        scratch_shapes=[pltpu.SemaphoreType.DMA((3 * na,))] * 2,
    )(*lands)


def _adam_gathered(w, gath, m, v, c_arr, after, name, tr=128):
    rows, cols = w.shape
    hr = rows // 2
    if hr % (2 * tr) == 0:
        tr = 2 * tr
    per = hr // tr

    def body(c_ref, w_ref, g_ref, m_ref, v_ref, after_ref, go_ref, d_ref, nm_ref, nv_ref):
        g = g_ref[0, 0].astype(F32)
        for k in range(1, N_CHIPS):
            g = g + g_ref[0, k].astype(F32)
        go_ref[...] = g
        d_ref[...], nm_ref[...], nv_ref[...] = _adam_math(w_ref[...], g, m_ref[...], v_ref[...])

    def rows_of(h, i, c_ref):
        c = c_ref[0]
        return ((c + h - 2 * c * h) * per + i, 0)

    blk = pl.BlockSpec((tr, cols), rows_of)
    grid_spec = pltpu.PrefetchScalarGridSpec(
        num_scalar_prefetch=1, grid=(2, per),
        in_specs=[blk, pl.BlockSpec((1, N_CHIPS, tr, cols), lambda h, i, c_ref: (h, 0, i, 0)), blk, blk, _ANY],
        out_specs=[blk] * 4)
    return pl.pallas_call(
        body, name=name, grid_spec=grid_spec, out_shape=[_sds(w.shape, F32)] * 4,
        compiler_params=_params(("arbitrary", "arbitrary")),
    )(c_arr, w, gath, m, v, after)


def _allreduce_small(block, name):
    two, r, n = block.shape
    assert two == 2

    def body(x_ref, out_ref, sib, chipsum, gath, d2d_send, d2d_recv, ici_send, ici_recv):
        x, y, c = _position()
        chip = 2 * x + y
        sibling = (x, y, 1 - c)
        first = _remote(x_ref, sib, d2d_send.at[0], d2d_recv.at[0], sibling)
        first.start()
        first.wait()
        chipsum[...] = x_ref[...] + sib[...]
        sends = []
        for j, (fx, fy) in enumerate(_CHIP_FLIPS):
            sends.append(_remote(chipsum.at[c], gath.at[chip], ici_send.at[j], ici_recv.at[j],
                                 (_flip(x, fx), _flip(y, fy), c)))
            sends[-1].start()
        gath[chip] = chipsum[c]
        for j, (fx, fy) in enumerate(_CHIP_FLIPS):
            landed = gath.at[2 * _flip(x, fx) + _flip(y, fy)]
            _remote(landed, landed, ici_send.at[j], ici_recv.at[j], sibling).wait_recv()
        for cp in sends:
            cp.wait_send()
        total = gath[0]
        for k in range(1, N_CHIPS):
            total = total + gath[k]
        out_ref[c] = total
        last = _remote(out_ref.at[c], out_ref.at[c], d2d_send.at[1], d2d_recv.at[1], sibling)
        last.start()
        _remote(out_ref.at[1 - c], out_ref.at[1 - c], d2d_send.at[1], d2d_recv.at[1], sibling).wait_recv()
        last.wait_send()

    vmem = pl.BlockSpec(memory_space=pltpu.VMEM)
    return pl.pallas_call(
        body, name=name, out_shape=_sds(block.shape, F32), in_specs=[vmem], out_specs=vmem,
        scratch_shapes=[pltpu.VMEM(block.shape, F32), pltpu.VMEM(block.shape, F32), pltpu.VMEM((N_CHIPS, r, n), F32),
                        pltpu.SemaphoreType.DMA((2,)), pltpu.SemaphoreType.DMA((2,)),
                        pltpu.SemaphoreType.DMA((3,)), pltpu.SemaphoreType.DMA((3,))],
        compiler_params=pltpu.CompilerParams(vmem_limit_bytes=VMEM_LIMIT_BYTES),
    )(block)


def _cast_place(shards, chip_arr, name):
    na = len(shards)
    steps = 4

    def body(chip_ref, *refs):
        for a in range(na):
            refs[na + a][0] = refs[a][...].astype(BF16)

    grid_spec = pltpu.PrefetchScalarGridSpec(
        num_scalar_prefetch=1, grid=(steps,),
        in_specs=[pl.BlockSpec((s.shape[0] // steps, s.shape[1]), lambda i, ch: (i, 0)) for s in shards],
        out_specs=[pl.BlockSpec((1, s.shape[0] // steps, s.shape[1]), lambda i, ch: (ch[0], i, 0)) for s in shards])
    return pl.pallas_call(
        body, name=name, grid_spec=grid_spec,
        out_shape=[_sds((N_CHIPS,) + s.shape, BF16) for s in shards],
        compiler_params=_params(("arbitrary",)),
    )(chip_arr, *shards)


def _silu(v):
    return v * _sigmoid(v)


def _ada_fwd(c8, w_ada):
    def body(c_ref, w_ref, o_ref):
        o_ref[...] = jnp.dot(_silu(c_ref[...]), w_ref[...], preferred_element_type=F32,
                             precision=lax.Precision.HIGHEST)

    return pl.pallas_call(
        body, name="ada_fwd", out_shape=_sds((N_DEV, w_ada.shape[1]), F32),
        compiler_params=pltpu.CompilerParams(vmem_limit_bytes=VMEM_LIMIT_BYTES),
    )(c8, w_ada)


def _mod_select(parts, b_ada, me_arr, after):
    cols = parts.shape[2]

    def body(me_ref, p_ref, b_ref, after_ref, o_ref):
        me = me_ref[0]
        for k in range(N_CHIPS):
            cs = slice(k * cols, (k + 1) * cols)
            o_ref[:, cs] = p_ref[2 * k, pl.ds(me, 1), :] + b_ref[:, cs]

    grid_spec = pltpu.PrefetchScalarGridSpec(
        num_scalar_prefetch=1, grid=(1,),
        in_specs=[pl.BlockSpec(parts.shape, lambda i, m: (0, 0, 0)), pl.BlockSpec(b_ada.shape, lambda i, m: (0, 0)),
                  _ANY],
        out_specs=pl.BlockSpec(b_ada.shape, lambda i, m: (0, 0)))
    return pl.pallas_call(body, name="mod_select", grid_spec=grid_spec, out_shape=_sds(b_ada.shape, F32))(
        me_arr, parts, b_ada, after)


def _ada_bwd(c8, dmod8, chip_arr, w, m, v, tr=256):
    d = c8.shape[1]
    cols = dmod8.shape[1] // N_CHIPS

    def body(chip_ref, c_ref, dm_ref, dmall_ref, w_ref, m_ref, v_ref, gw_ref, d_ref, nm_ref, nv_ref, gb_ref):
        g = lax.dot_general(_silu(c_ref[...]), dm_ref[...], (((0,), (0,)), ((), ())),
                            preferred_element_type=F32, precision=lax.Precision.HIGHEST)
        gw_ref[...] = g
        d_ref[...], nm_ref[...], nv_ref[...] = _adam_math(w_ref[...], g, m_ref[...], v_ref[...])
        acc = dmall_ref[0:1, :]
        for k in range(1, N_DEV):
            acc = acc + dmall_ref[k:k + 1, :]
        gb_ref[...] = acc

    rows = pl.BlockSpec((tr, cols), lambda i, ch: (i, 0))
    grid_spec = pltpu.PrefetchScalarGridSpec(
        num_scalar_prefetch=1, grid=(d // tr,),
        in_specs=[pl.BlockSpec((N_DEV, tr), lambda i, ch: (0, i)),
                  pl.BlockSpec((N_DEV, cols), lambda i, ch: (0, ch[0])),
                  pl.BlockSpec(dmod8.shape, lambda i, ch: (0, 0)), rows, rows, rows],
        out_specs=[rows] * 4 + [pl.BlockSpec((1, dmod8.shape[1]), lambda i, ch: (0, 0))])
    return pl.pallas_call(
        body, name="ada_bwd", grid_spec=grid_spec,
        out_shape=[_sds((d, cols), F32)] * 4 + [_sds((1, dmod8.shape[1]), F32)],
        compiler_params=_params(("arbitrary",)),
    )(chip_arr, c8, dmod8, dmod8, w, m, v)


def _adam_math(w, g, m, v):
    m = ADAM_B1 * m + (1.0 - ADAM_B1) * g
    v = ADAM_B2 * v + (1.0 - ADAM_B2) * (g * g)
    m_hat = m / (1.0 - ADAM_B1 ** ADAM_STEP)
    v_hat = v / (1.0 - ADAM_B2 ** ADAM_STEP)
    delta = -ADAM_LR * (m_hat / (jnp.sqrt(v_hat) + ADAM_EPS) + ADAM_WD * w)
    return delta, m, v


def _adam(w, g, m, v, name, tr=256):
    rows, cols = w.shape
    if rows % tr:
        tr = rows

    def body(w_ref, g_ref, m_ref, v_ref, d_ref, nm_ref, nv_ref):
        d_ref[...], nm_ref[...], nv_ref[...] = _adam_math(w_ref[...], g_ref[...], m_ref[...], v_ref[...])

    spec = pl.BlockSpec((tr, cols), lambda i: (i, 0))
    return pl.pallas_call(
        body, name=name, grid=(rows // tr,), in_specs=[spec] * 4, out_specs=[spec] * 3,
        out_shape=[_sds(w.shape, F32)] * 3, compiler_params=_params(("parallel",)),
    )(w, g, m, v)


SMALL_REPLICATED = ("g_mix_pre", "g_mix_post", "conv_b", "w_rgate", "b_rgate", "w_igate", "b_igate", "lru_a",
                    "v_norm_g", "v_norm_b", "w_spatial", "b_spatial", "g_lru_out", "g_gmlp_out", "g_ffn_pre",
                    "g_ffn_post", "ffn_conv_b")
SMALL_COLUMN_SHARDED = ("conv_w", "ffn_conv_w")

SMALL_ROW_LEN = 86016
_SMALL_ROWS = (
    (("ffn_conv_w", 18432), ("conv_w", 2048), ("w_spatial", 65536)),
    (("w_rgate", 32768), ("w_igate", 32768), ("ffn_conv_b", 6144), ("g_mix_pre", 1024), ("g_mix_post", 1024),
     ("g_ffn_pre", 1024), ("g_ffn_post", 1024), ("conv_b", 512), ("b_rgate", 512), ("b_igate", 512),
     ("lru_a", 512), ("v_norm_g", 512), ("v_norm_b", 512), ("b_spatial", 512), ("g_lru_out", 512),
     ("g_gmlp_out", 512), ("loss", 128)),
)


def _small_slots():
    slots = {}
    for row, entries in enumerate(_SMALL_ROWS):
        off = 0
        for name, size in entries:
            slots[name] = (row, off)
            off += size
        assert off <= SMALL_ROW_LEN
    return slots


SMALL_SLOT = _small_slots()
SMALL_LANES = SMALL_ROW_LEN // SUBLANES


def _small_pieces(name, first, count):
    row, off = SMALL_SLOT[name]
    pos, pieces = off + first, []
    while count:
        sub, lane = divmod(pos, SMALL_LANES)
        n = min(count, SMALL_LANES - lane)
        pieces.append((row, sub, lane, n))
        pos, count = pos + n, count - n
    return pieces
ROW_VECTORS = ("ffn_conv_b", "g_mix_pre", "g_mix_post", "g_ffn_pre", "g_ffn_post", "conv_b", "lru_a", "v_norm_g",
               "v_norm_b", "g_lru_out", "g_gmlp_out")
HEAD_DIM = LRU_WIDTH // LRU_HEADS


def _pack_small(g, after):
    order = ("ffn_conv_w", "conv_w", "w_spatial", "w_rgate", "w_igate", "b_rgate", "b_igate", "b_spatial", "loss") \
        + ROW_VECTORS
    vmem = pl.BlockSpec(memory_space=pltpu.VMEM)

    def body(*refs):
        src = dict(zip(order, refs))
        out_ref = refs[len(order) + 1]
        out_ref[...] = jnp.zeros_like(out_ref)

        def put(name, first, val):
            col = 0
            for row, sub, lane, n in _small_pieces(name, first, val.shape[1]):
                out_ref[row, sub:sub + 1, lane:lane + n] = val[:, col:col + n]
                col += n

        for name in ROW_VECTORS + ("b_rgate", "b_igate", "loss"):
            put(name, 0, src[name][...])
        for name in ("ffn_conv_w", "conv_w"):
            k_taps, n = src[name].shape
            for k in range(k_taps):
                put(name, k * n, src[name][k:k + 1, :])
        for g_idx in range(GMLP_GROUPS):
            for i in range(GMLP_BLOCK):
                put("w_spatial", (g_idx * GMLP_BLOCK + i) * GMLP_BLOCK, src["w_spatial"][g_idx, i:i + 1, :])
        for name in ("w_rgate", "w_igate"):
            for h in range(LRU_HEADS):
                for i in range(HEAD_DIM):
                    r = h * HEAD_DIM + i
                    put(name, r * HEAD_DIM, src[name][r:r + 1, h * HEAD_DIM:(h + 1) * HEAD_DIM])
        eye = (lax.broadcasted_iota(jnp.int32, (GMLP_BLOCK, GMLP_BLOCK), 0)
               == lax.broadcasted_iota(jnp.int32, (GMLP_BLOCK, GMLP_BLOCK), 1))
        for g_idx in range(GMLP_GROUPS):
            col = src["b_spatial"][:, g_idx:g_idx + 1]
            put("b_spatial", g_idx * GMLP_BLOCK, _colsum(jnp.where(eye, col, 0.0)))

    return pl.pallas_call(
        body, name="pack_small", out_shape=_sds((2, SUBLANES, SMALL_LANES), F32),
        in_specs=[vmem] * len(order) + [_ANY], out_specs=vmem,
        compiler_params=pltpu.CompilerParams(vmem_limit_bytes=VMEM_LIMIT_BYTES),
    )(*[g[n] for n in order], after)


def _adam_small(g_small, w, m, v):
    vmem = pl.BlockSpec(memory_space=pltpu.VMEM)
    n_p = len(SMALL_REPLICATED)

    def body(g_ref, *refs):
        w_refs, m_refs, v_refs = refs[:n_p], refs[n_p:2 * n_p], refs[2 * n_p:3 * n_p]
        outs = refs[3 * n_p:]
        go, do, mo, vo = outs[:n_p], outs[n_p:2 * n_p], outs[2 * n_p:3 * n_p], outs[3 * n_p:]
        for k, name in enumerate(SMALL_REPLICATED):
            def take(first, count, name=name):
                parts = [g_ref[row, sub:sub + 1, lane:lane + n]
                         for row, sub, lane, n in _small_pieces(name, first, count)]
                return parts[0] if len(parts) == 1 else jnp.concatenate(parts, axis=1)

            shape = w_refs[k].shape
            if name in ROW_VECTORS:
                go[k][...] = take(0, shape[1])
            elif name in ("b_rgate", "b_igate"):
                for h in range(LRU_HEADS):
                    go[k][0, h:h + 1, :] = take(h * HEAD_DIM, HEAD_DIM)
            elif name == "b_spatial":
                for g_idx in range(GMLP_GROUPS):
                    go[k][0, g_idx:g_idx + 1, :] = take(g_idx * GMLP_BLOCK, GMLP_BLOCK)
            elif name == "w_spatial":
                for g_idx in range(GMLP_GROUPS):
                    for i in range(GMLP_BLOCK):
                        go[k][0, g_idx, i:i + 1, :] = take((g_idx * GMLP_BLOCK + i) * GMLP_BLOCK, GMLP_BLOCK)
            else:
                for h in range(LRU_HEADS):
                    for i in range(HEAD_DIM):
                        go[k][0, h, i:i + 1, :] = take((h * HEAD_DIM + i) * HEAD_DIM, HEAD_DIM)
            do[k][...], mo[k][...], vo[k][...] = _adam_math(w_refs[k][...], go[k][...], m_refs[k][...],
                                                             v_refs[k][...])

    names = SMALL_REPLICATED
    out_shape = [_sds(w[n].shape, F32) for n in names] * 4
    res = pl.pallas_call(
        body, name="adam_small", out_shape=out_shape,
        in_specs=[vmem] * (1 + 3 * n_p), out_specs=[vmem] * (4 * n_p),
        compiler_params=pltpu.CompilerParams(vmem_limit_bytes=VMEM_LIMIT_BYTES),
    )(g_small, *[w[n] for n in names], *[m[n] for n in names], *[v[n] for n in names])
    return [dict(zip(names, res[k * n_p:(k + 1) * n_p])) for k in range(4)]


def _adam_cols(name, g_small, w, m, v, chip_arr):
    _, k_taps, n = w.shape
    row, off = SMALL_SLOT[name]
    first = off // n
    per_sub = SMALL_LANES // n

    def body(chip_ref, *refs):
        g_refs = refs[:k_taps]
        w_ref, m_ref, v_ref, go_ref, d_ref, nm_ref, nv_ref = refs[k_taps:]
        for k in range(k_taps):
            tap = (0, slice(k, k + 1), slice(None))
            sub = (first + N_CHIPS * k + chip_ref[0]) // per_sub
            g = g_refs[k][row, pl.ds(sub, 1), :]
            go_ref[tap] = g
            d_ref[tap], nm_ref[tap], nv_ref[tap] = _adam_math(w_ref[tap], g, m_ref[tap], v_ref[tap])

    whole = pl.BlockSpec(w.shape, lambda i, ch: (0, 0, 0))
    taps = [pl.BlockSpec((2, SUBLANES, n),
                         functools.partial(lambda i, ch, k: (0, 0, (first + N_CHIPS * k + ch[0]) % per_sub), k=k))
            for k in range(k_taps)]
    grid_spec = pltpu.PrefetchScalarGridSpec(
        num_scalar_prefetch=1, grid=(1,), in_specs=taps + [whole] * 3, out_specs=[whole] * 4)
    return pl.pallas_call(body, name="adam_" + name, grid_spec=grid_spec, out_shape=[_sds(w.shape, F32)] * 4)(
        chip_arr, *[g_small] * k_taps, w, m, v)


def kernel(x, c, w_ada, b_ada, g_mix_pre, g_mix_post, w_in, conv_w, conv_b, w_rgate, b_rgate, w_igate, b_igate, lru_a, v_norm_g, v_norm_b, w_spatial, b_spatial, g_lru_out, g_gmlp_out, w_out, g_ffn_pre, g_ffn_post, w_up, ffn_conv_w, ffn_conv_b, w_down, loss_target, m_w_ada, m_b_ada, m_g_mix_pre, m_g_mix_post, m_w_in, m_conv_w, m_conv_b, m_w_rgate, m_b_rgate, m_w_igate, m_b_igate, m_lru_a, m_v_norm_g, m_v_norm_b, m_w_spatial, m_b_spatial, m_g_lru_out, m_g_gmlp_out, m_w_out, m_g_ffn_pre, m_g_ffn_post, m_w_up, m_ffn_conv_w, m_ffn_conv_b, m_w_down, v_w_ada, v_b_ada, v_g_mix_pre, v_g_mix_post, v_w_in, v_conv_w, v_conv_b, v_w_rgate, v_b_rgate, v_w_igate, v_b_igate, v_lru_a, v_v_norm_g, v_v_norm_b, v_w_spatial, v_b_spatial, v_g_lru_out, v_g_gmlp_out, v_w_out, v_g_ffn_pre, v_g_ffn_post, v_w_up, v_ffn_conv_w, v_ffn_conv_b, v_w_down):
    args = dict(locals())
    names = ("w_ada", "b_ada", "g_mix_pre", "g_mix_post", "w_in", "conv_w", "conv_b", "w_rgate", "b_rgate",
             "w_igate", "b_igate", "lru_a", "v_norm_g", "v_norm_b", "w_spatial", "b_spatial", "g_lru_out",
             "g_gmlp_out", "w_out", "g_ffn_pre", "g_ffn_post", "w_up", "ffn_conv_w", "ffn_conv_b", "w_down")
    drop = lambda a: a if a.ndim == 2 else a[0]
    w = {n: drop(args[n]) for n in names}
    m = {n: drop(args["m_" + n]) for n in names}
    v = {n: drop(args["v_" + n]) for n in names}
    xi, yi, ci = _position()
    me_arr = jnp.reshape(4 * xi + 2 * yi + ci, (1,)).astype(jnp.int32)
    chip_arr = jnp.reshape(2 * xi + yi, (1,)).astype(jnp.int32)
    c_arr = jnp.reshape(ci, (1,)).astype(jnp.int32)
    pos_arr = jnp.stack([ci, 2 * xi + yi]).astype(jnp.int32)

    big = ("w_in", "w_out", "w_up", "w_down")
    lands_a = _cast_place([w[n] for n in big[:2]], chip_arr, "cast_place_a")
    start_a, wait_a = _gather_plan([w[n].shape[0] for n in big[:2]])
    start_b, wait_b = _gather_plan([w[n].shape[0] for n in big[2:]])

    row0 = jnp.concatenate([c, w["conv_w"].reshape(1, -1), w["ffn_conv_w"].reshape(1, -1)], axis=1)
    g0 = _allgather8(row0, "gather_cond")[:, 0, :]
    send_a, recv_a, _, lands_a, token_a = _split_start([], lands_a, start_a, 6, g0, "gather_start_a")
    lands_b = _cast_place([w[n] for n in big[2:]], chip_arr + token_a[0, 0].astype(jnp.int32), "cast_place_b")
    c8 = g0[:, :D_MODEL]
    per_chip = g0[0::2]
    conv_w_full = per_chip[:, D_MODEL:D_MODEL + 512].reshape(N_CHIPS, 4, 128).transpose(1, 0, 2).reshape(4, 512)
    ffn_conv_w_full = per_chip[:, D_MODEL + 512:].reshape(N_CHIPS, 3, 1536).transpose(1, 0, 2).reshape(3, 2 * D_FF)
    mod_parts = _allgather8(_ada_fwd(c8 + token_a[0:1, 0:1], w["w_ada"]), "gather_mod")
    send_b, recv_b, _, lands_b, token_b = _split_start([], lands_b, start_b, 6, mod_parts, "gather_start_b")
    mod = _mod_select(mod_parts, w["b_ada"].reshape(1, -1), me_arr, token_b).reshape(N_MOD, D_MODEL)
    sh_m, sc_m, gt_m, sh_f, sc_f, gt_f = [mod[k:k + 1] for k in range(N_MOD)]

    small = {n: w[n] for n in SMALL_REPLICATED}
    small["conv_w"] = conv_w_full
    small["ffn_conv_w"] = ffn_conv_w_full
    row = lambda a: a.reshape(1, -1)
    seq_params, ws_t = _seq_params(small)
    glo, ggo = row(small["g_lru_out"]), row(small["g_gmlp_out"])
    g_pre, g_post = row(small["g_mix_pre"]), row(small["g_mix_post"])
    g_pre2, g_post2 = row(small["g_ffn_pre"]), row(small["g_ffn_post"])
    fw, fb = small["ffn_conv_w"], row(small["ffn_conv_b"])
    xs, tgt = x[0], loss_target[0]

    _, lands_a = _split_wait(send_a, recv_a, [], lands_a, wait_a, mod, "gather_wait_a")
    w_in4, w_out4 = _forward_to_sibling(lands_a, "forward_a")
    w_out_b = w_out4.reshape(D_MODEL, D_MODEL)
    z, h = _mix_in(xs, sc_m, sh_m, g_pre, w_in4)
    ycat, hst, stash = _seqmix(z, seq_params, glo, ggo)
    _, lands_b = _split_wait(send_b, recv_b, [], lands_b, wait_b, ycat, "gather_wait_b")
    fwd_start, fwd_wait = _forward_plan([w[n].shape[0] for n in big[2:]])
    fwd_send, fwd_recv, _, lands_b, tok = _split_start([], lands_b, fwd_start, 6, pos_arr, "forward_start_b")
    y, x1, h2 = _mix_out(ycat, xs, w_out_b, gt_m + tok[0:1, 0:1], g_post, g_pre2, sc_f, sh_f)
    _, (w_up4, w_down4) = _split_wait(fwd_send, fwd_recv, [], lands_b, fwd_wait, h2, "forward_wait_b")
    w_down_b = w_down4.reshape(D_FF, D_MODEL)
    up0, pre, act, dy2, dx2, loss, dgt_f, dg_post2 = _ffn_fwd(h2, x1, tgt, w_up4, w_down_b, fw, fb, gt_f, g_post2)

    dup0, dfw, dfb = _ffn_bwd_a(dy2, pre, up0, w_down_b, fw)
    gw_up = _wgrad(h2, dup0, N_CHIPS, "wgrad_up", True)
    gw_down = _wgrad(act, dy2, 2, "wgrad_down", False)
    ex_start, ex_wait = _exchange_plan(2)
    sg_start, sg_wait = _swap_gathered_plan(2)
    grads, deltas, new_m, new_v = {}, {}, {}, {}

    def swap_start(parts, name):
        sw_start, sw_wait = _swap_halves_plan([p.shape[1] // 2 for p in parts])
        recv = [lax.empty((N_CHIPS, p.shape[1] // 2, p.shape[2]), BF16) for p in parts]
        send_s, recv_s, parts, recv, token = _split_start(parts, recv, sw_start, len(parts), pos_arr,
                                                           "swap_start_" + name)
        return (send_s, recv_s, parts, recv, sw_wait), token

    def exchange_start(swap, tags, after, name):
        send_s, recv_s, parts, recv, sw_wait = swap
        parts, recv = _split_wait(send_s, recv_s, parts, recv, sw_wait, after, "swap_wait_" + name)
        both = [_chip_sum(p, r, pos_arr, "chip_sum_" + t) for p, r, t in zip(parts, recv, tags)]
        sums, gath = [b[0] for b in both], [b[1] for b in both]
        return _split_start(sums, gath, ex_start, 3 * len(parts), pos_arr, "exchange_start_" + name)

    def gathered_start(exchange, after, name):
        send_s, recv_s, sums, gath, _ = exchange
        _, gath = _split_wait(send_s, recv_s, sums, gath, ex_wait, after, "exchange_wait_" + name)
        send_s, recv_s, _, gath, token = _split_start([], gath, sg_start, len(gath), pos_arr,
                                                      "gathered_start_" + name)
        return (send_s, recv_s, gath), token

    def gathered_wait(gathered, after, name):
        send_s, recv_s, gath = gathered
        return _split_wait(send_s, recv_s, [], gath, sg_wait, after, "gathered_wait_" + name)[1]

    def adam_big(t, gath, after):
        grads[t], deltas[t], new_m[t], new_v[t] = _adam_gathered(w[t], gath, m[t], v[t], c_arr, after, "adam_" + t)

    def behind(value, token):
        return value + token[0:1, 0:1]

    tags_b, tags_a = ("w_up", "w_down"), ("w_in", "w_out")
    swap_b, tok = swap_start([gw_up, gw_down.reshape(N_CHIPS, -1, D_MODEL)], "b")
    dx1, dy, dsh_f, dsc_f, dg_pre2, dgt_m, dg_post = _ffn_bwd_b(
        dup0, x1, y, dx2, w_up4, g_pre2, behind(sc_f, tok), sh_f, gt_m, g_post)
    exchange_b = exchange_start(swap_b, tags_b, dg_post, "b")
    (dz, dcw, dcb, dwr, dwi, dbr, dbi, dspa, dng, dnb, dws, dbs_t, dglo, dggo) = _seqmix_bwd(
        z, hst, stash, dy, w_out_b, seq_params, ws_t, behind(glo, exchange_b[4]), ggo)
    gw_in = _wgrad(h, dz, N_CHIPS, "wgrad_in", True)
    gw_out = _wgrad(ycat, dy, 1, "wgrad_out", False)
    swap_a, tok = swap_start([gw_in, gw_out.reshape(N_CHIPS, -1, D_MODEL)], "a")
    grad_x, dsh_m, dsc_m, dg_pre = _mix_in_bwd(xs, dz, dx1, w_in4, g_pre, behind(sc_m, tok))

    dmod = jnp.concatenate([behind(dsh_m, tok), dsc_m, dgt_m, dsh_f, dsc_f, dgt_f], axis=1)
    dmod8 = _allgather8(dmod, "gather_dmod")[:, 0, :]
    small_grads = dict(
        g_mix_pre=dg_pre, g_mix_post=dg_post, conv_w=dcw, conv_b=dcb, w_rgate=dwr, b_rgate=dbr, w_igate=dwi,
        b_igate=dbi, lru_a=dspa, v_norm_g=dng, v_norm_b=dnb, w_spatial=dws, b_spatial=dbs_t, g_lru_out=dglo,
        g_gmlp_out=dggo, g_ffn_pre=dg_pre2, g_ffn_post=dg_post2, ffn_conv_w=dfw, ffn_conv_b=dfb,
        loss=loss)
    g_small = _allreduce_small(_pack_small(small_grads, dmod8), "reduce_small")
    total = g_small[_small_pieces("loss", 0, 1)[0][:3]]
    exchange_a = exchange_start(swap_a, tags_a, g_small, "a")
    gathered_b, tok = gathered_start(exchange_b, exchange_a[4], "b")

    grads["w_ada"], deltas["w_ada"], new_m["w_ada"], new_v["w_ada"], g_b_ada = _ada_bwd(
        c8, behind(dmod8, tok), chip_arr, w["w_ada"], m["w_ada"], v["w_ada"])
    rep = SMALL_REPLICATED
    small_out = _adam_small(g_small, {n: args[n] for n in rep}, {n: args["m_" + n] for n in rep},
                            {n: args["v_" + n] for n in rep})
    for n in rep:
        grads[n], deltas[n], new_m[n], new_v[n] = [group[n] for group in small_out]
    for n in SMALL_COLUMN_SHARDED:
        grads[n], deltas[n], new_m[n], new_v[n] = _adam_cols(n, g_small, args[n], args["m_" + n],
                                                             args["v_" + n], chip_arr)
    d_b, m_b, v_b = _adam(w["b_ada"], g_b_ada, m["b_ada"], v["b_ada"], "adam_b_ada")
    grads["b_ada"], deltas["b_ada"], new_m["b_ada"], new_v["b_ada"] = g_b_ada, d_b, m_b, v_b

    gath_up, gath_down = gathered_wait(gathered_b, d_b, "b")
    adam_big("w_down", gath_down, pos_arr)
    gathered_a, tok = gathered_start(exchange_a, deltas["w_down"], "a")
    adam_big("w_up", gath_up, tok)
    gath_in, gath_out = gathered_wait(gathered_a, deltas["w_up"], "a")
    adam_big("w_in", gath_in, pos_arr)
    adam_big("w_out", gath_out, pos_arr)

    outs = [total, grad_x[None]]
    for group in (grads, deltas, new_m, new_v):
        outs.extend(group[n].reshape(args[n].shape) for n in names)
    return tuple(outs)
```

```python
import functools
import math

import jax
import jax.numpy as jnp
from jax import lax
from jax.experimental import pallas as pl
from jax.experimental.pallas import tpu as pltpu

F32 = jnp.float32
BF16 = jnp.bfloat16
MESH = pl.DeviceIdType.MESH

D_MODEL = 1024
LRU_WIDTH = 512
LRU_HEADS = 8
GMLP_GROUPS = 4
GMLP_BLOCK = 128
CHUNK = 64
D_FF = 3072
N_MOD = 6
EPS = 1e-6
LRU_C = 8.0
N_CHIPS = 4
N_DEV = 8

ADAM_LR = 0.001
ADAM_B1 = 0.9
ADAM_B2 = 0.999
ADAM_EPS = 1e-08
ADAM_WD = 0.01
ADAM_STEP = 10

GELU_C0 = math.sqrt(2.0 / math.pi)
GELU_C1 = 0.044715

VMEM_LIMIT_BYTES = 56 * 1024 * 1024
SUBLANES = 8
BF16_SUBLANES = 16
FFN_CHUNK = 768
SUB_ROWS = 256


def _gelu_gate(x):
    x2 = x * x
    z = x * ((2.0 * GELU_C0 * GELU_C1) * x2 + 2.0 * GELU_C0)
    return 1.0 / (1.0 + jnp.exp(-z)), x2


def _gelu(x):
    t = jnp.tanh(GELU_C0 * (x + GELU_C1 * x * x * x))
    return 0.5 * x * (1.0 + t)


def _gelu_and_grad(x):
    s, x2 = _gelu_gate(x)
    g = x * s
    dz = (6.0 * GELU_C0 * GELU_C1) * x2 + 2.0 * GELU_C0
    return g, s + g * (1.0 - s) * dz


def _sigmoid(x):
    return 1.0 / (1.0 + jnp.exp(-x))


def _log1p(u):
    w = 1.0 + u
    return jnp.where(w == 1.0, u, jnp.log(w) * (u / (w - 1.0)))


def _softplus(x):
    return jnp.maximum(x, 0.0) + _log1p(jnp.exp(-jnp.abs(x)))


def _neg_expm1(x):
    u = jnp.exp(x)
    um1 = u - 1.0
    tiny = um1 == 0.0
    small = um1 * (x / jnp.log(jnp.where(tiny, 2.0, jnp.maximum(u, 0.25))))
    return -jnp.where(tiny, x, jnp.where(x < -1.0, um1, small))


def _msq_rsqrt(v):
    return lax.rsqrt(jnp.mean(v * v, axis=-1, keepdims=True) + EPS)


def _rms_bwd(dyn, yn, r):
    return r * (dyn - yn * jnp.mean(dyn * yn, axis=-1, keepdims=True))


def _colsum(v):
    return jnp.sum(v, axis=0, keepdims=True)


def _shift_down(cur, prev8, k):
    rolled = pltpu.roll(cur, k, 0)
    head = pltpu.roll(prev8, k, 0)
    row8 = lax.broadcasted_iota(jnp.int32, (SUBLANES, cur.shape[1]), 0)
    first = jnp.where(row8 < k, head, rolled[0:SUBLANES])
    return jnp.concatenate([first, rolled[SUBLANES:]], axis=0)


def _shift_up(cur, next8, k):
    t = cur.shape[0]
    rolled = pltpu.roll(cur, t - k, 0)
    tail = pltpu.roll(next8, SUBLANES - k, 0)
    row8 = lax.broadcasted_iota(jnp.int32, (SUBLANES, cur.shape[1]), 0)
    last = jnp.where(row8 >= SUBLANES - k, tail, rolled[t - SUBLANES:])
    return jnp.concatenate([rolled[:t - SUBLANES], last], axis=0)


def _scan_fwd(a, b):
    t = a.shape[0]
    row = lax.broadcasted_iota(jnp.int32, a.shape, 0)
    d = 1
    while d < t:
        keep = row >= d
        a_s = jnp.where(keep, pltpu.roll(a, d, 0), 1.0)
        b_s = jnp.where(keep, pltpu.roll(b, d, 0), 0.0)
        b = a * b_s + b
        a = a * a_s
        d *= 2
    return a, b


def _scan_bwd(a, g):
    t = a.shape[0]
    row = lax.broadcasted_iota(jnp.int32, a.shape, 0)
    d = 1
    while d < t:
        keep = row < t - d
        a_s = jnp.where(keep, pltpu.roll(a, t - d, 0), 1.0)
        g_s = jnp.where(keep, pltpu.roll(g, t - d, 0), 0.0)
        g = a * g_s + g
        a = a * a_s
        d *= 2
    return a, g


def _dot(a, b):
    return jnp.dot(a, b, preferred_element_type=F32)


def _dot_nt(a, b):
    return lax.dot_general(a, b, (((1,), (1,)), ((), ())), preferred_element_type=F32)


def _dot_tn(a, b):
    return lax.dot_general(a, b, (((0,), (0,)), ((), ())), preferred_element_type=F32)


def _rows(ts, cols, rev_of=None):
    if rev_of is None:
        return pl.BlockSpec((ts, cols), lambda i: (i, 0))
    return pl.BlockSpec((ts, cols), lambda i: (rev_of - 1 - i, 0))


def _halo_prev(ts, cols, halo, rev_of=None, col_block=0):
    per = ts // halo
    if rev_of is None:
        return pl.BlockSpec((halo, cols), lambda i: (jnp.maximum(i * per - 1, 0), col_block))
    return pl.BlockSpec((halo, cols), lambda i: (jnp.maximum((rev_of - 1 - i) * per - 1, 0), col_block))


def _full(shape):
    nd = len(shape)
    return pl.BlockSpec(shape, lambda *_: (0,) * nd)


_RESIDENT = pl.BlockSpec(memory_space=pltpu.VMEM)


def _params(sem):
    return pltpu.CompilerParams(dimension_semantics=sem, vmem_limit_bytes=VMEM_LIMIT_BYTES)


def _sds(shape, dtype):
    return jax.ShapeDtypeStruct(shape, dtype)


def _sub_tiles(ts):
    return [slice(r0, r0 + SUB_ROWS) for r0 in range(0, ts, SUB_ROWS)]


def _mix_in(x, sc, sh, g, w_in4, ts=512):
    s, d = x.shape

    def body(x_ref, sc_ref, sh_ref, g_ref, w_ref, z_ref, h_ref):
        for rs in _sub_tiles(ts):
            xv = x_ref[rs, :]
            h = (xv * _msq_rsqrt(xv) * g_ref[...]) * (1.0 + sc_ref[...]) + sh_ref[...]
            hb = h.astype(BF16)
            h_ref[rs, :] = hb
            for k in range(N_CHIPS):
                z_ref[rs, k * 512:(k + 1) * 512] = _dot(hb, w_ref[k])

    return pl.pallas_call(
        body, grid=(s // ts,), name="mix_in",
        in_specs=[_rows(ts, d), _full((1, d)), _full((1, d)), _full((1, d)), _full(w_in4.shape)],
        out_specs=[_rows(ts, 2048), _rows(ts, d)],
        out_shape=[_sds((s, 2048), F32), _sds((s, d), BF16)],
        compiler_params=_params(("parallel",)),
    )(x, sc, sh, g, w_in4)


N_STASH = 12
(ST_XC, ST_R, ST_IG, ST_A, ST_MULT, ST_GL, ST_DGL, ST_U, ST_DU, ST_Q, ST_VHAT, ST_SPB) = range(N_STASH)


def _seq_param_specs():
    return [_full((4, 512)), _full((1, 512)), _full((512, 512)), _full((512, 512)), _full((1, 512)),
            _full((1, 512)), _full((1, 512)), _full((1, 512)), _full((1, 512)), _full((4, 128, 128)),
            _full((128, 4))]


def _seqmix(z, seq_params, glo, ggo, ts=256):
    s = z.shape[0]
    nt = s // ts

    def body(z_ref, zprev_ref, cw_ref, cb_ref, bdr_ref, bdi_ref, br_ref, bi_ref, la_ref, ng_ref, nb_ref,
             ws_ref, bst_ref, glo_ref, ggo_ref, ycat_ref, hst_ref, st_ref, hcarry, sp_scr):
        i = pl.program_id(0)

        @pl.when(i == 0)
        def _():
            hcarry[...] = jnp.zeros_like(hcarry)

        lx = z_ref[:, 0:512]
        prev8 = jnp.where(i == 0, 0.0, zprev_ref[...])
        xc = (cw_ref[3:4, :] * lx + cw_ref[2:3, :] * _shift_down(lx, prev8, 1)
              + cw_ref[1:2, :] * _shift_down(lx, prev8, 2) + cw_ref[0:1, :] * _shift_down(lx, prev8, 3)
              + cb_ref[...])
        xcb = xc.astype(BF16)
        r = _sigmoid(_dot(xcb, bdr_ref[...]) + br_ref[...])
        ig = _sigmoid(_dot(xcb, bdi_ref[...]) + bi_ref[...])
        log_a = (-LRU_C) * r * _softplus(-la_ref[...])
        a = jnp.exp(log_a)
        mult = jnp.sqrt(_neg_expm1(2.0 * log_a))
        acum, hloc = _scan_fwd(a, mult * (ig * xc))
        h = hloc + acum * hcarry[...]
        hcarry[...] = h[ts - 1:ts, :]
        hst_ref[...] = h
        gl, dgl = _gelu_and_grad(z_ref[:, 512:1024])
        y_l = h * gl
        for slot, val in ((ST_XC, xc), (ST_R, r), (ST_IG, ig), (ST_A, a), (ST_MULT, mult), (ST_GL, gl),
                          (ST_DGL, dgl)):
            st_ref[slot] = val

        u, du = _gelu_and_grad(z_ref[:, 1024:1536])
        vg, dvg = _gelu_and_grad(z_ref[:, 1536:2048])
        vc = vg - jnp.mean(vg, axis=-1, keepdims=True)
        rstd = lax.rsqrt(jnp.mean(vc * vc, axis=-1, keepdims=True) + EPS)
        vhat = vc * rstd
        vb = (vhat * ng_ref[...] + nb_ref[...]).astype(BF16)
        for n in range(ts // GMLP_BLOCK):
            rs = slice(n * GMLP_BLOCK, (n + 1) * GMLP_BLOCK)
            for g in range(GMLP_GROUPS):
                cs = slice(g * 128, (g + 1) * 128)
                sp_scr[rs, cs] = _dot(ws_ref[g], vb[rs, cs]) + bst_ref[:, g:g + 1]
        spb = sp_scr[...]
        y_g = u * spb
        for slot, val in ((ST_U, u), (ST_DU, du), (ST_Q, rstd * dvg), (ST_VHAT, vhat), (ST_SPB, spb)):
            st_ref[slot] = val

        ycat_ref[:, 0:512] = (y_l * _msq_rsqrt(y_l) * glo_ref[...]).astype(BF16)
        ycat_ref[:, 512:1024] = (y_g * _msq_rsqrt(y_g) * ggo_ref[...]).astype(BF16)

    return pl.pallas_call(
        body, grid=(nt,), name="seqmix",
        in_specs=[_rows(ts, 2048), _halo_prev(ts, 512, SUBLANES)] + _seq_param_specs()
        + [_full((1, 512)), _full((1, 512))],
        out_specs=[_rows(ts, 1024), _rows(ts, 512), pl.BlockSpec((N_STASH, ts, 512), lambda i: (0, i, 0))],
        out_shape=[_sds((s, 1024), BF16), _sds((s, 512), F32), _sds((N_STASH, s, 512), F32)],
        scratch_shapes=[pltpu.VMEM((1, 512), F32), pltpu.VMEM((ts, 512), F32)],
        compiler_params=_params(("arbitrary",)),
    )(z, z, *seq_params, glo, ggo)


def _mix_out(ycat, x, w_out, gt_m, g_post, g_pre2, sc_f, sh_f, ts=512):
    s, d = x.shape

    def body(yc_ref, x_ref, w_ref, gt_ref, gp_ref, g2_ref, sc_ref, sh_ref, y_ref, x1_ref, h2_ref):
        for rs in _sub_tiles(ts):
            y = _dot(yc_ref[rs, :], w_ref[...])
            y_ref[rs, :] = y
            x1 = x_ref[rs, :] + gt_ref[...] * (y * _msq_rsqrt(y) * gp_ref[...])
            x1_ref[rs, :] = x1
            h2 = (x1 * _msq_rsqrt(x1) * g2_ref[...]) * (1.0 + sc_ref[...]) + sh_ref[...]
            h2_ref[rs, :] = h2.astype(BF16)

    vec = _full((1, d))
    return pl.pallas_call(
        body, grid=(s // ts,), name="mix_out",
        in_specs=[_rows(ts, d), _rows(ts, d), _full((d, d)), vec, vec, vec, vec, vec],
        out_specs=[_rows(ts, d), _rows(ts, d), _rows(ts, d)],
        out_shape=[_sds((s, d), F32), _sds((s, d), F32), _sds((s, d), BF16)],
        compiler_params=_params(("parallel",)),
    )(ycat, x, w_out, gt_m, g_post, g_pre2, sc_f, sh_f)


def _ffn_cols(j):
    per = (2 * D_FF // N_CHIPS) // FFN_CHUNK
    return j // per, (j % per) * FFN_CHUNK, j * FFN_CHUNK


def _ffn_fwd(h2, x1, tgt, w_up4, w_down, fw, fb, gt_f, g_post, ts=256):
    s, d = x1.shape
    nch = D_FF // FFN_CHUNK

    def body(h2_ref, x1_ref, tgt_ref, wup_ref, wdn_ref, fw_ref, fb_ref, gt_ref, gp_ref,
             up0_ref, pre_ref, act_ref, dy2_ref, dx2_ref, loss_ref, dgt_ref, dgp_ref, tail_ref):
        i = pl.program_id(0)

        @pl.when(i == 0)
        def _():
            tail_ref[...] = jnp.zeros_like(tail_ref)
            loss_ref[...] = jnp.zeros_like(loss_ref)
            dgt_ref[...] = jnp.zeros_like(dgt_ref)
            dgp_ref[...] = jnp.zeros_like(dgp_ref)

        hb = h2_ref[...]

        def up_project(j):
            sh_g, off, _ = _ffn_cols(j)
            return [_dot(hb, wup_ref[shard, :, off:off + FFN_CHUNK]).astype(BF16) for shard in (sh_g, sh_g + 2)]

        y2 = jnp.zeros((ts, d), F32)
        ahead = up_project(0)
        for j in range(nch):
            _, _, col = _ffn_cols(j)
            ubs = ahead
            if j + 1 < nch:
                ahead = up_project(j + 1)
            halves = []
            for ub, c0 in zip(ubs, (col, D_FF + col)):
                cs = slice(c0, c0 + FFN_CHUNK)
                up0_ref[:, cs] = ub
                u = ub.astype(F32)
                prev8 = tail_ref[:, cs]
                tail_ref[:, cs] = u[ts - SUBLANES:, :]
                halves.append(fw_ref[2:3, cs] * u + fw_ref[1:2, cs] * _shift_down(u, prev8, 1)
                              + fw_ref[0:1, cs] * _shift_down(u, prev8, 2) + fb_ref[:, cs])
                pre_ref[:, cs] = halves[-1].astype(BF16)
            act = (_gelu(halves[0]) * halves[1]).astype(BF16)
            act_ref[:, col:col + FFN_CHUNK] = act
            y2 = y2 + _dot(act, wdn_ref[col:col + FFN_CHUNK, :])
        r2 = _msq_rsqrt(y2)
        yn = y2 * r2
        yng = yn * gp_ref[...]
        e = x1_ref[...] + gt_ref[...] * yng - tgt_ref[...]
        loss_ref[...] += jnp.sum(e * e) * (0.5 / d)
        dx2 = e * (1.0 / d)
        dx2_ref[...] = dx2
        dgt_ref[...] += _colsum(dx2 * yng)
        dyng = dx2 * gt_ref[...]
        dgp_ref[...] += _colsum(dyng * yn)
        dy2_ref[...] = _rms_bwd(dyng * gp_ref[...], yn, r2).astype(BF16)

    vec = _full((1, d))
    return pl.pallas_call(
        body, grid=(s // ts,), name="ffn_fwd",
        in_specs=[_rows(ts, d), _rows(ts, d), _rows(ts, d), _RESIDENT, _RESIDENT,
                  _full((3, 2 * D_FF)), _full((1, 2 * D_FF)), vec, vec],
        out_specs=[_rows(ts, 2 * D_FF), _rows(ts, 2 * D_FF), _rows(ts, D_FF), _rows(ts, d), _rows(ts, d),
                   _full((1, 128)), vec, vec],
        out_shape=[_sds((s, 2 * D_FF), BF16), _sds((s, 2 * D_FF), BF16), _sds((s, D_FF), BF16), _sds((s, d), BF16),
                   _sds((s, d), F32), _sds((1, 128), F32), _sds((1, d), F32), _sds((1, d), F32)],
        scratch_shapes=[pltpu.VMEM((SUBLANES, 2 * D_FF), F32)],
        compiler_params=_params(("arbitrary",)),
    )(h2, x1, tgt, w_up4, w_down, fw, fb, gt_f, g_post)


def _shift_up_mxu(vb, up_mat, next8, k):
    t = vb.shape[0]
    main = _dot(up_mat, vb)
    tail = pltpu.roll(next8, SUBLANES - k, 0)
    row8 = lax.broadcasted_iota(jnp.int32, next8.shape, 0)
    last = main[t - SUBLANES:] + jnp.where(row8 >= SUBLANES - k, tail, 0.0)
    return jnp.concatenate([main[:t - SUBLANES], last], axis=0)


def _ffn_bwd_a(dy2, pre, up0, w_down, fw, ts=256):
    s, d = dy2.shape
    nt = s // ts
    nch = D_FF // FFN_CHUNK
    wide = 2 * D_FF
    up_mats = jnp.stack([jnp.eye(ts, k=1, dtype=BF16), jnp.eye(ts, k=2, dtype=BF16)])

    def body(dy2_ref, pre_ref, up0_ref, wdn_ref, fw_ref, um_ref, dup0_ref, dfw_ref, dfb_ref, next_ref):
        i = pl.program_id(0)

        @pl.when(i == 0)
        def _():
            next_ref[...] = jnp.zeros_like(next_ref)
            dfw_ref[...] = jnp.zeros_like(dfw_ref)
            dfb_ref[...] = jnp.zeros_like(dfb_ref)

        dyb = dy2_ref[...]
        for j in range(nch):
            _, _, col = _ffn_cols(j)
            dact = _dot_nt(dyb, wdn_ref[col:col + FFN_CHUNK, :])
            gl, dgl = _gelu_and_grad(pre_ref[:, col:col + FFN_CHUNK].astype(F32))
            dpre = (dact * pre_ref[:, D_FF + col:D_FF + col + FFN_CHUNK].astype(F32) * dgl, dact * gl)
            for half, c0 in enumerate((col, D_FF + col)):
                cs = slice(c0, c0 + FFN_CHUNK)
                dp = dpre[half]
                dpb = dp.astype(BF16)
                nxt = next_ref[:, cs]
                next_ref[:, cs] = dpb.astype(F32)[0:SUBLANES, :]
                su1 = _shift_up_mxu(dpb, um_ref[0], nxt, 1)
                su2 = _shift_up_mxu(dpb, um_ref[1], nxt, 2)
                u = up0_ref[:, cs].astype(F32)
                dfb_ref[:, cs] += _colsum(dp)
                dfw_ref[2:3, cs] += _colsum(dp * u)
                dfw_ref[1:2, cs] += _colsum(su1 * u)
                dfw_ref[0:1, cs] += _colsum(su2 * u)
                dup0 = fw_ref[2:3, cs] * dp + fw_ref[1:2, cs] * su1 + fw_ref[0:1, cs] * su2
                dup0_ref[:, cs] = dup0.astype(BF16)

    return pl.pallas_call(
        body, grid=(nt,), name="ffn_bwd_a",
        in_specs=[_rows(ts, d, nt), _rows(ts, wide, nt), _rows(ts, wide, nt), _RESIDENT,
                  _full((3, wide)), _full((2, ts, ts))],
        out_specs=[_rows(ts, wide, nt), _full((3, wide)), _full((1, wide))],
        out_shape=[_sds((s, wide), BF16), _sds((3, wide), F32), _sds((1, wide), F32)],
        scratch_shapes=[pltpu.VMEM((SUBLANES, wide), F32)],
        compiler_params=_params(("arbitrary",)),
    )(dy2, pre, up0, w_down, fw, up_mats)


def _ffn_bwd_b(dup0, x1, y, dx2, w_up4, g_pre2, sc_f, sh_f, gt_m, g_post_m, ts=512):
    s, d = x1.shape
    shard_cols = 2 * D_FF // N_CHIPS

    def body(dup_ref, x1_ref, y_ref, dx2_ref, wup_ref, g2_ref, sc_ref, sh_ref, gt_ref, gp_ref,
             dx1_ref, dy_ref, dsh_ref, dsc_ref, dg2_ref, dgt_ref, dgp_ref):
        i = pl.program_id(0)

        @pl.when(i == 0)
        def _():
            for ref in (dsh_ref, dsc_ref, dg2_ref, dgt_ref, dgp_ref):
                ref[...] = jnp.zeros_like(ref)

        for rs in _sub_tiles(ts):
            dh2 = jnp.zeros((SUB_ROWS, d), F32)
            for k in range(N_CHIPS):
                dh2 = dh2 + _dot_nt(dup_ref[rs, k * shard_cols:(k + 1) * shard_cols], wup_ref[k])
            x1v = x1_ref[rs, :]
            r2 = _msq_rsqrt(x1v)
            xn = x1v * r2
            hn = xn * g2_ref[...]
            dsh_ref[...] += _colsum(dh2)
            dsc_ref[...] += _colsum(dh2 * hn)
            dhn = dh2 * (1.0 + sc_ref[...])
            dg2_ref[...] += _colsum(dhn * xn)
            dx1 = dx2_ref[rs, :] + _rms_bwd(dhn * g2_ref[...], xn, r2)
            dx1_ref[rs, :] = dx1
            yv = y_ref[rs, :]
            ry = _msq_rsqrt(yv)
            yn = yv * ry
            dgt_ref[...] += _colsum(dx1 * (yn * gp_ref[...]))
            dyng = dx1 * gt_ref[...]
            dgp_ref[...] += _colsum(dyng * yn)
            dy_ref[rs, :] = _rms_bwd(dyng * gp_ref[...], yn, ry).astype(BF16)

    vec = _full((1, d))
    return pl.pallas_call(
        body, grid=(s // ts,), name="ffn_bwd_b",
        in_specs=[_rows(ts, 2 * D_FF), _rows(ts, d), _rows(ts, d), _rows(ts, d), _RESIDENT,
                  vec, vec, vec, vec, vec],
        out_specs=[_rows(ts, d), _rows(ts, d), vec, vec, vec, vec, vec],
        out_shape=[_sds((s, d), F32), _sds((s, d), BF16)] + [_sds((1, d), F32)] * 5,
        compiler_params=_params(("arbitrary",)),
    )(dup0, x1, y, dx2, w_up4, g_pre2, sc_f, sh_f, gt_m, g_post_m)


def _seqmix_bwd(z, hst, stash, dy, w_out, seq_params, ws_t, glo, ggo, x, dx1, w_in4, g_pre, sc_m, ts=256):
    s, d = x.shape
    nt = s // ts
    small_shapes = [(4, 512), (1, 512), (512, 512), (512, 512), (1, 512), (1, 512), (1, 512),
                    (1, 512), (1, 512), (4, 128, 128), (128, 4), (1, 512), (1, 512),
                    (1, d), (1, d), (1, d)]

    def body(lx_ref, hst_ref, hprev_ref, st_ref, dy_ref, wout_ref, cw_ref, cb_ref, bdr_ref, bdi_ref, br_ref,
             bi_ref, la_ref, ng_ref, nb_ref, ws_ref, bst_ref, wst_ref, glo_ref, ggo_ref, x_ref, dx1_ref, win_ref,
             gpre_ref, scm_ref, dz_ref, gx_ref, *rest):
        small_refs = rest[:16]
        (dcw_ref, dcb_ref, dwr_ref, dwi_ref, dbr_ref, dbi_ref, dspa_ref, dng_ref, dnb_ref, dws_ref, dbs_ref,
         dglo_ref, dggo_ref, dsh_ref, dsc_ref, dgpre_ref) = small_refs
        gcarry, anext, dxcnext, dv_scr = rest[16:]
        i = pl.program_id(0)

        @pl.when(i == 0)
        def _():
            for ref in small_refs:
                ref[...] = jnp.zeros_like(ref)
            gcarry[...] = jnp.zeros_like(gcarry)
            anext[...] = jnp.ones_like(anext)
            dxcnext[...] = jnp.zeros_like(dxcnext)

        first_tile = i == nt - 1
        xc, r, ig, a, mult = st_ref[ST_XC], st_ref[ST_R], st_ref[ST_IG], st_ref[ST_A], st_ref[ST_MULT]
        gl, u, spb, vhat = st_ref[ST_GL], st_ref[ST_U], st_ref[ST_SPB], st_ref[ST_VHAT]
        lx = lx_ref[...]
        h = hst_ref[...]
        hprev = _shift_down(h, jnp.where(first_tile, 0.0, hprev_ref[...]), 1)
        y_l = h * gl
        y_g = u * spb

        dycat = _dot_nt(dy_ref[...], wout_ref[...])

        dz_parts = {}

        def emit_dz(k, val):
            dz_parts[k] = val.astype(BF16)
            dz_ref[:, k * 512:(k + 1) * 512] = dz_parts[k]

        rl = _msq_rsqrt(y_l)
        yln = y_l * rl
        dyl = dycat[:, 0:512]
        dglo_ref[...] += _colsum(dyl * yln)
        dy_l = _rms_bwd(dyl * glo_ref[...], yln, rl)
        rg = _msq_rsqrt(y_g)
        ygn = y_g * rg
        dyg = dycat[:, 512:1024]
        dggo_ref[...] += _colsum(dyg * ygn)
        dy_g = _rms_bwd(dyg * ggo_ref[...], ygn, rg)

        emit_dz(1, dy_l * h * st_ref[ST_DGL])
        a_up = _shift_up(a, anext[...], 1)
        acum, gloc = _scan_bwd(a_up, dy_l * gl)
        gg = gloc + acum * gcarry[...]
        gcarry[...] = gg[0:1, :]
        anext[...] = a[0:SUBLANES, :]
        da = gg * hprev
        t1 = gg * mult
        di = t1 * xc
        dxc = t1 * ig
        dmult = gg * ig * xc
        dla = da * a - dmult * (a * a / mult)
        dspa_ref[...] += _colsum(dla * r) * (-LRU_C)
        dpr = dla * ((-LRU_C) * _softplus(-la_ref[...])) * r * (1.0 - r)
        dpi = di * ig * (1.0 - ig)
        dbr_ref[...] += _colsum(dpr)
        dbi_ref[...] += _colsum(dpi)
        dprb = dpr.astype(BF16)
        dpib = dpi.astype(BF16)
        xcb = xc.astype(BF16)
        dwr_ref[...] += _dot_tn(xcb, dprb)
        dwi_ref[...] += _dot_tn(xcb, dpib)
        dxc = dxc + _dot_nt(dprb, bdr_ref[...]) + _dot_nt(dpib, bdi_ref[...])
        nxt = dxcnext[...]
        dxcnext[...] = dxc[0:SUBLANES, :]
        up1, up2, up3 = _shift_up(dxc, nxt, 1), _shift_up(dxc, nxt, 2), _shift_up(dxc, nxt, 3)
        dcb_ref[...] += _colsum(dxc)
        dcw_ref[3:4, :] += _colsum(dxc * lx)
        dcw_ref[2:3, :] += _colsum(up1 * lx)
        dcw_ref[1:2, :] += _colsum(up2 * lx)
        dcw_ref[0:1, :] += _colsum(up3 * lx)
        dlx = cw_ref[3:4, :] * dxc + cw_ref[2:3, :] * up1 + cw_ref[1:2, :] * up2 + cw_ref[0:1, :] * up3
        emit_dz(0, dlx)

        emit_dz(2, dy_g * spb * st_ref[ST_DU])
        dsp = dy_g * u
        vb = (vhat * ng_ref[...] + nb_ref[...]).astype(BF16)
        for n in range(ts // GMLP_BLOCK):
            rs = slice(n * GMLP_BLOCK, (n + 1) * GMLP_BLOCK)
            for g in range(GMLP_GROUPS):
                cs = slice(g * 128, (g + 1) * 128)
                dbs_ref[:, g:g + 1] += jnp.sum(dsp[rs, cs], axis=1, keepdims=True)
                blk = dsp[rs, cs].astype(BF16)
                dws_ref[g] += _dot_nt(blk, vb[rs, cs])
                dv_scr[rs, cs] = _dot(wst_ref[g], blk)
        dv = dv_scr[...]
        dng_ref[...] += _colsum(dv * vhat)
        dnb_ref[...] += _colsum(dv)
        dvh = dv * ng_ref[...]
        dvg = dvh - jnp.mean(dvh, axis=-1, keepdims=True) - vhat * jnp.mean(dvh * vhat, axis=-1, keepdims=True)
        emit_dz(3, dvg * st_ref[ST_Q])

        dh = _dot_nt(dz_parts[0], win_ref[0])
        for k in range(1, N_CHIPS):
            dh = dh + _dot_nt(dz_parts[k], win_ref[k])
        xv = x_ref[...]
        rx = _msq_rsqrt(xv)
        xn = xv * rx
        dsh_ref[...] += _colsum(dh)
        dsc_ref[...] += _colsum(dh * (xn * gpre_ref[...]))
        dhn = dh * (1.0 + scm_ref[...])
        dgpre_ref[...] += _colsum(dhn * xn)
        gx_ref[...] = dx1_ref[...] + _rms_bwd(dhn * gpre_ref[...], xn, rx)

        @pl.when(i == nt - 1)
        def _():
            pos = lax.broadcasted_iota(jnp.int32, (GMLP_BLOCK, GMLP_BLOCK), 0) // CHUNK
            src = lax.broadcasted_iota(jnp.int32, (GMLP_BLOCK, GMLP_BLOCK), 1) // CHUNK
            for g in range(GMLP_GROUPS):
                dws_ref[g] = jnp.where(src <= pos, dws_ref[g], 0.0)
            dspa_ref[...] = dspa_ref[...] * (-_sigmoid(-la_ref[...]))

    vec = _full((1, d))
    in_specs = ([_rows(ts, 512, nt), _rows(ts, 512, nt), _halo_prev(ts, 512, SUBLANES, nt),
                 pl.BlockSpec((N_STASH, ts, 512), lambda i: (0, nt - 1 - i, 0)), _rows(ts, d, nt),
                 _full((d, d))]
                + _seq_param_specs() + [_full((4, 128, 128)), _full((1, 512)), _full((1, 512))]
                + [_rows(ts, d, nt), _rows(ts, d, nt), _full(w_in4.shape), vec, vec])
    return pl.pallas_call(
        body, grid=(nt,), name="seqmix_bwd",
        in_specs=in_specs,
        out_specs=[_rows(ts, 2048, nt), _rows(ts, d, nt)] + [_full(sh) for sh in small_shapes],
        out_shape=[_sds((s, 2048), BF16), _sds((s, d), F32)] + [_sds(sh, F32) for sh in small_shapes],
        scratch_shapes=[pltpu.VMEM((1, 512), F32), pltpu.VMEM((SUBLANES, 512), F32),
                        pltpu.VMEM((SUBLANES, 512), F32), pltpu.VMEM((ts, 512), F32)],
        compiler_params=_params(("arbitrary",)),
    )(z, hst, hst, stash, dy, w_out, *seq_params, ws_t, glo, ggo, x, dx1, w_in4, g_pre, sc_m)


def _wgrad(a, b, n_chunks, name, chunk_major, ts=2048):
    s, m = a.shape
    n = b.shape[1]
    nc = n // n_chunks
    nt = s // ts

    def body(a_ref, b_ref, o_ref, acc):
        i = pl.program_id(1)

        @pl.when(i == 0)
        def _():
            acc[...] = jnp.zeros_like(acc)

        acc[...] += _dot_tn(a_ref[...], b_ref[...])

        @pl.when(i == nt - 1)
        def _():
            if chunk_major:
                o_ref[0] = acc[...].astype(BF16)
            else:
                o_ref[...] = acc[...].astype(BF16)

    if chunk_major:
        out_spec, out_shape = pl.BlockSpec((1, m, nc), lambda c, i: (c, 0, 0)), _sds((n_chunks, m, nc), BF16)
    else:
        out_spec, out_shape = pl.BlockSpec((m, nc), lambda c, i: (0, c)), _sds((m, n), BF16)
    return pl.pallas_call(
        body, grid=(n_chunks, nt), name=name,
        in_specs=[pl.BlockSpec((ts, m), lambda c, i: (i, 0)), pl.BlockSpec((ts, nc), lambda c, i: (i, c))],
        out_specs=out_spec,
        out_shape=out_shape,
        scratch_shapes=[pltpu.VMEM((m, nc), F32)],
        compiler_params=_params(("parallel", "arbitrary")),
    )(a, b)


def _block_diag(w):
    heads, hd, _ = w.shape
    eye = jnp.eye(heads, dtype=w.dtype)
    return (eye[:, None, :, None] * w[:, :, None, :]).reshape(heads * hd, heads * hd)


def _seq_params(small):
    row = lambda v: v.reshape(1, -1)
    pos = jnp.arange(GMLP_BLOCK)
    mask = (pos[None, :] // CHUNK) <= (pos[:, None] // CHUNK)
    ws = jnp.where(mask[None], small["w_spatial"], 0.0)
    seq_params = (small["conv_w"], row(small["conv_b"]),
                  _block_diag(small["w_rgate"]).astype(BF16), _block_diag(small["w_igate"]).astype(BF16),
                  row(small["b_rgate"]), row(small["b_igate"]), row(small["lru_a"]),
                  row(small["v_norm_g"]), row(small["v_norm_b"]), ws.astype(BF16), small["b_spatial"].T)
    return seq_params, jnp.swapaxes(ws, 1, 2).astype(BF16)


_ANY = pl.BlockSpec(memory_space=pl.ANY)
_CHIP_FLIPS = ((1, 0), (0, 1), (1, 1))


def _position():
    return lax.axis_index("x"), lax.axis_index("y"), lax.axis_index("c")


def _flip(v, f):
    return 1 - v if f else v


def _remote(src, dst, send_sem, recv_sem, peer):
    return pltpu.make_async_remote_copy(src_ref=src, dst_ref=dst, send_sem=send_sem, recv_sem=recv_sem,
                                        device_id=peer, device_id_type=MESH)


def _allgather8(block, name):
    r, n = block.shape

    def body(x_ref, gath, send_sems, recv_sems, loc_sem):
        x, y, c = _position()
        me = 4 * x + 2 * y + c
        loc = pltpu.make_async_copy(x_ref, gath.at[me], loc_sem)
        loc.start()
        peers = []
        for k in range(1, N_DEV):
            px, py, pc = _flip(x, k & 4), _flip(y, k & 2), _flip(c, k & 1)
            peers.append((px, py, pc))
            _remote(x_ref, gath.at[me], send_sems.at[k - 1], recv_sems.at[k - 1], (px, py, pc)).start()
        for k, (px, py, pc) in enumerate(peers):
            src = 4 * px + 2 * py + pc
            _remote(x_ref, gath.at[src], send_sems.at[k], recv_sems.at[k], (px, py, pc)).wait_recv()
        for k, peer in enumerate(peers):
            _remote(x_ref, gath.at[me], send_sems.at[k], recv_sems.at[k], peer).wait_send()
        loc.wait()

    return pl.pallas_call(
        body, name=name, out_shape=_sds((N_DEV, r, n), F32),
        in_specs=[pl.BlockSpec(memory_space=pltpu.VMEM)], out_specs=pl.BlockSpec(memory_space=pltpu.VMEM),
        scratch_shapes=[pltpu.SemaphoreType.DMA((N_DEV - 1,)), pltpu.SemaphoreType.DMA((N_DEV - 1,)),
                        pltpu.SemaphoreType.DMA],
        compiler_params=pltpu.CompilerParams(vmem_limit_bytes=VMEM_LIMIT_BYTES),
    )(block)


def _half(ref, c, rows):
    hr = rows // 2
    return ref.at[pl.ds(pl.multiple_of(c * hr, BF16_SUBLANES), hr), :]


def _chip_sum(part, recv, pos_arr, name):
    _, rows, cols = part.shape
    hr = rows // 2

    def body(pos_ref, p_ref, r_ref, o_ref, g_ref):
        total = (p_ref[...].astype(F32) + r_ref[...].astype(F32)).astype(BF16)
        o_ref[...] = total

        @pl.when(pl.program_id(0) == pos_ref[1])
        def _():
            g_ref[0] = total

    grid_spec = pltpu.PrefetchScalarGridSpec(
        num_scalar_prefetch=1, grid=(N_CHIPS,),
        in_specs=[pl.BlockSpec((1, hr, cols), lambda k, pos: (k, pos[0], 0)),
                  pl.BlockSpec((1, hr, cols), lambda k, pos: (k, 0, 0))],
        out_specs=[pl.BlockSpec((1, hr, cols), lambda k, pos: (k, 0, 0)),
                   pl.BlockSpec((1, 1, hr, cols), lambda k, pos: (0, pos[1], 0, 0))])
    return pl.pallas_call(
        body, name=name, grid_spec=grid_spec,
        out_shape=[_sds((N_CHIPS, hr, cols), BF16), _sds((2, N_CHIPS, hr, cols), BF16)],
        compiler_params=_params(("arbitrary",)),
    )(pos_arr, part, recv)


_HBM = pl.BlockSpec(memory_space=pltpu.HBM)
_SEM = pl.BlockSpec(memory_space=pltpu.SEMAPHORE)
_EFFECT = pltpu.SideEffectType.DATAFLOW_SIDE_EFFECTING


def _in_hbm(a):
    return pltpu.with_memory_space_constraint(a, pltpu.HBM)


def _split_start(srcs, lands, plan, n_copies, after, name):
    ns, nl = len(srcs), len(lands)
    bufs = list(srcs) + list(lands)

    def body(*refs):
        send_sems, recv_sems = refs[ns + nl + 1], refs[ns + nl + 2]
        token = refs[-1]
        for k, (src, dst, peer) in enumerate(plan(refs[:ns], refs[ns:ns + nl])):
            _remote(src, dst, send_sems.at[k], recv_sems.at[k], peer).start()
        token[...] = jnp.zeros_like(token)

    out = pl.pallas_call(
        body, name=name,
        out_shape=(pltpu.SemaphoreType.DMA((n_copies,)), pltpu.SemaphoreType.DMA((n_copies,)),
                   *[pltpu.HBM(b.shape, b.dtype) for b in bufs], _sds((SUBLANES, 128), F32)),
        in_specs=[_HBM] * (ns + nl) + [_ANY],
        out_specs=(_SEM, _SEM, *[_HBM] * (ns + nl), pl.BlockSpec(memory_space=pltpu.VMEM)),
        input_output_aliases={i: 2 + i for i in range(ns + nl)},
        compiler_params=pltpu.CompilerParams(has_side_effects=_EFFECT),
    )(*[_in_hbm(b) for b in bufs], after)
    return out[0], out[1], list(out[2:2 + ns]), list(out[2 + ns:2 + ns + nl]), out[-1]


def _split_wait(send_sems, recv_sems, srcs, lands, plan, after, name):
    ns, nl = len(srcs), len(lands)
    bufs = list(srcs) + list(lands)

    def body(*refs):
        send_ref, recv_ref = refs[ns + nl], refs[ns + nl + 1]
        me = _position()
        for k, src, dst in plan(refs[:ns], refs[ns:ns + nl]):
            cp = _remote(src, dst, send_ref.at[k], recv_ref.at[k], me)
            cp.wait_send()
            cp.wait_recv()

    out = pl.pallas_call(
        body, name=name,
        out_shape=[pltpu.HBM(b.shape, b.dtype) for b in bufs],
        in_specs=[_HBM] * (ns + nl) + [_SEM, _SEM, _ANY],
        out_specs=[_HBM] * (ns + nl),
        input_output_aliases={i: i for i in range(ns + nl)},
        compiler_params=pltpu.CompilerParams(has_side_effects=_EFFECT),
    )(*bufs, send_sems, recv_sems, after)
    return list(out[:ns]), list(out[ns:])


def _gather_plan(rows_of):
    def start(src_refs, land_refs):
        x, y, c = _position()
        chip = 2 * x + y
        out = []
        for a, rows in enumerate(rows_of):
            mine = _half(land_refs[a].at[chip], c, rows)
            out.extend((mine, mine, (_flip(x, fx), _flip(y, fy), c)) for fx, fy in _CHIP_FLIPS)
        return out

    def wait(src_refs, land_refs):
        x, y, c = _position()
        chip = 2 * x + y
        out = []
        for a, rows in enumerate(rows_of):
            for j, (fx, fy) in enumerate(_CHIP_FLIPS):
                src_chip = 2 * _flip(x, fx) + _flip(y, fy)
                out.append((3 * a + j, _half(land_refs[a].at[chip], c, rows),
                            _half(land_refs[a].at[src_chip], c, rows)))
        return out

    return start, wait


def _forward_plan(rows_of):
    def pieces(land_refs, half):
        x, y, _ = _position()
        return [_half(land_refs[a].at[2 * _flip(x, fx) + _flip(y, fy)], half, rows)
                for a, rows in enumerate(rows_of) for fx, fy in _CHIP_FLIPS]

    def start(src_refs, land_refs):
        x, y, c = _position()
        return [(p, p, (x, y, 1 - c)) for p in pieces(land_refs, c)]

    def wait(src_refs, land_refs):
        _, _, c = _position()
        return [(k, mine, theirs)
                for k, (mine, theirs) in enumerate(zip(pieces(land_refs, c), pieces(land_refs, 1 - c)))]

    return start, wait


def _swap_halves_plan(half_rows):
    def slices(src_refs, c):
        return [src_refs[a].at[:, pl.ds(pl.multiple_of((1 - c) * hr, BF16_SUBLANES), hr), :]
                for a, hr in enumerate(half_rows)]

    def start(src_refs, land_refs):
        x, y, c = _position()
        return [(src, land_refs[a], (x, y, 1 - c)) for a, src in enumerate(slices(src_refs, c))]

    def wait(src_refs, land_refs):
        _, _, c = _position()
        return [(a, src, land_refs[a]) for a, src in enumerate(slices(src_refs, c))]

    return start, wait


def _swap_gathered_plan(n_arrays):
    def start(src_refs, land_refs):
        x, y, c = _position()
        return [(land_refs[a].at[0], land_refs[a].at[1], (x, y, 1 - c)) for a in range(n_arrays)]

    def wait(src_refs, land_refs):
        return [(a, land_refs[a].at[0], land_refs[a].at[1]) for a in range(n_arrays)]

    return start, wait


def _exchange_plan(n_arrays):
    def start(src_refs, land_refs):
        x, y, c = _position()
        chip = 2 * x + y
        out = []
        for a in range(n_arrays):
            for fx, fy in _CHIP_FLIPS:
                px, py = _flip(x, fx), _flip(y, fy)
                out.append((src_refs[a].at[2 * px + py], land_refs[a].at[0, chip], (px, py, c)))
        return out

    def wait(src_refs, land_refs):
        x, y, c = _position()
        out = []
        for a in range(n_arrays):
            for j, (fx, fy) in enumerate(_CHIP_FLIPS):
                src_chip = 2 * _flip(x, fx) + _flip(y, fy)
                out.append((3 * a + j, src_refs[a].at[src_chip], land_refs[a].at[0, src_chip]))
        return out

    return start, wait


def _forward_to_sibling(lands, name):
    na = len(lands)

    def body(*refs):
        land_refs = refs[na:2 * na]
        send_sems, recv_sems = refs[2 * na:]
        x, y, c = _position()
        sibling = (x, y, 1 - c)
        sends = []
        for a in range(na):
            rows = lands[a].shape[1]
            for j, (fx, fy) in enumerate(_CHIP_FLIPS):
                landed = _half(land_refs[a].at[2 * _flip(x, fx) + _flip(y, fy)], c, rows)
                sends.append(_remote(landed, landed, send_sems.at[3 * a + j], recv_sems.at[3 * a + j], sibling))
                sends[-1].start()
        for a in range(na):
            rows = lands[a].shape[1]
            for j, (fx, fy) in enumerate(_CHIP_FLIPS):
                other = _half(land_refs[a].at[2 * _flip(x, fx) + _flip(y, fy)], 1 - c, rows)
                _remote(other, other, send_sems.at[3 * a + j], recv_sems.at[3 * a + j], sibling).wait_recv()
        for cp in sends:
            cp.wait_send()

    return pl.pallas_call(
        body, name=name,
        out_shape=[_sds(l.shape, l.dtype) for l in lands],
        in_specs=[_ANY] * na, out_specs=[_ANY] * na,
        input_output_aliases={a: a for a in range(na)},
        scratch_shapes=[pltpu.SemaphoreType.DMA((3 * na,))] * 2,
    )(*lands)


def _adam_gathered(w, gath, m, v, c_arr, after, name, tr=128):
    rows, cols = w.shape
    hr = rows // 2
    if hr % (2 * tr) == 0:
        tr = 2 * tr
    per = hr // tr

    def body(c_ref, w_ref, g_ref, m_ref, v_ref, after_ref, go_ref, d_ref, nm_ref, nv_ref):
        g = g_ref[0, 0].astype(F32)
        for k in range(1, N_CHIPS):
            g = g + g_ref[0, k].astype(F32)
        go_ref[...] = g
        d_ref[...], nm_ref[...], nv_ref[...] = _adam_math(w_ref[...], g, m_ref[...], v_ref[...])

    def rows_of(h, i, c_ref):
        c = c_ref[0]
        return ((c + h - 2 * c * h) * per + i, 0)

    blk = pl.BlockSpec((tr, cols), rows_of)
    grid_spec = pltpu.PrefetchScalarGridSpec(
        num_scalar_prefetch=1, grid=(2, per),
        in_specs=[blk, pl.BlockSpec((1, N_CHIPS, tr, cols), lambda h, i, c_ref: (h, 0, i, 0)), blk, blk, _ANY],
        out_specs=[blk] * 4)
    return pl.pallas_call(
        body, name=name, grid_spec=grid_spec, out_shape=[_sds(w.shape, F32)] * 4,
        compiler_params=_params(("arbitrary", "arbitrary")),
    )(c_arr, w, gath, m, v, after)


def _allreduce_small(block, name):
    two, r, n = block.shape
    assert two == 2

    def body(x_ref, out_ref, sib, chipsum, gath, d2d_send, d2d_recv, ici_send, ici_recv):
        x, y, c = _position()
        chip = 2 * x + y
        sibling = (x, y, 1 - c)
        first = _remote(x_ref, sib, d2d_send.at[0], d2d_recv.at[0], sibling)
        first.start()
        first.wait()
        chipsum[...] = x_ref[...] + sib[...]
        sends = []
        for j, (fx, fy) in enumerate(_CHIP_FLIPS):
            sends.append(_remote(chipsum.at[c], gath.at[chip], ici_send.at[j], ici_recv.at[j],
                                 (_flip(x, fx), _flip(y, fy), c)))
            sends[-1].start()
        gath[chip] = chipsum[c]
        for j, (fx, fy) in enumerate(_CHIP_FLIPS):
            landed = gath.at[2 * _flip(x, fx) + _flip(y, fy)]
            _remote(landed, landed, ici_send.at[j], ici_recv.at[j], sibling).wait_recv()
        for cp in sends:
            cp.wait_send()
        total = gath[0]
        for k in range(1, N_CHIPS):
            total = total + gath[k]
        out_ref[c] = total
        last = _remote(out_ref.at[c], out_ref.at[c], d2d_send.at[1], d2d_recv.at[1], sibling)
        last.start()
        _remote(out_ref.at[1 - c], out_ref.at[1 - c], d2d_send.at[1], d2d_recv.at[1], sibling).wait_recv()
        last.wait_send()

    vmem = pl.BlockSpec(memory_space=pltpu.VMEM)
    return pl.pallas_call(
        body, name=name, out_shape=_sds(block.shape, F32), in_specs=[vmem], out_specs=vmem,
        scratch_shapes=[pltpu.VMEM(block.shape, F32), pltpu.VMEM(block.shape, F32), pltpu.VMEM((N_CHIPS, r, n), F32),
                        pltpu.SemaphoreType.DMA((2,)), pltpu.SemaphoreType.DMA((2,)),
                        pltpu.SemaphoreType.DMA((3,)), pltpu.SemaphoreType.DMA((3,))],
        compiler_params=pltpu.CompilerParams(vmem_limit_bytes=VMEM_LIMIT_BYTES),
    )(block)


def _cast_place(shards, chip_arr, name):
    na = len(shards)
    steps = 4

    def body(chip_ref, *refs):
        for a in range(na):
            refs[na + a][0] = refs[a][...].astype(BF16)

    grid_spec = pltpu.PrefetchScalarGridSpec(
        num_scalar_prefetch=1, grid=(steps,),
        in_specs=[pl.BlockSpec((s.shape[0] // steps, s.shape[1]), lambda i, ch: (i, 0)) for s in shards],
        out_specs=[pl.BlockSpec((1, s.shape[0] // steps, s.shape[1]), lambda i, ch: (ch[0], i, 0)) for s in shards])
    return pl.pallas_call(
        body, name=name, grid_spec=grid_spec,
        out_shape=[_sds((N_CHIPS,) + s.shape, BF16) for s in shards],
        compiler_params=_params(("arbitrary",)),
    )(chip_arr, *shards)


def _silu(v):
    return v * _sigmoid(v)


def _ada_fwd(c8, w_ada):
    def body(c_ref, w_ref, o_ref):
        o_ref[...] = jnp.dot(_silu(c_ref[...]), w_ref[...], preferred_element_type=F32,
                             precision=lax.Precision.HIGHEST)

    return pl.pallas_call(
        body, name="ada_fwd", out_shape=_sds((N_DEV, w_ada.shape[1]), F32),
        compiler_params=pltpu.CompilerParams(vmem_limit_bytes=VMEM_LIMIT_BYTES),
    )(c8, w_ada)


def _mod_select(parts, b_ada, me_arr, after):
    cols = parts.shape[2]

    def body(me_ref, p_ref, b_ref, after_ref, o_ref):
        me = me_ref[0]
        for k in range(N_CHIPS):
            cs = slice(k * cols, (k + 1) * cols)
            o_ref[:, cs] = p_ref[2 * k, pl.ds(me, 1), :] + b_ref[:, cs]

    grid_spec = pltpu.PrefetchScalarGridSpec(
        num_scalar_prefetch=1, grid=(1,),
        in_specs=[pl.BlockSpec(parts.shape, lambda i, m: (0, 0, 0)), pl.BlockSpec(b_ada.shape, lambda i, m: (0, 0)),
                  _ANY],
        out_specs=pl.BlockSpec(b_ada.shape, lambda i, m: (0, 0)))
    return pl.pallas_call(body, name="mod_select", grid_spec=grid_spec, out_shape=_sds(b_ada.shape, F32))(
        me_arr, parts, b_ada, after)


def _ada_bwd(c8, dmod8, chip_arr, w, m, v, tr=256):
    d = c8.shape[1]
    cols = dmod8.shape[1] // N_CHIPS

    def body(chip_ref, c_ref, dm_ref, dmall_ref, w_ref, m_ref, v_ref, gw_ref, d_ref, nm_ref, nv_ref, gb_ref):
        g = lax.dot_general(_silu(c_ref[...]), dm_ref[...], (((0,), (0,)), ((), ())),
                            preferred_element_type=F32, precision=lax.Precision.HIGHEST)
        gw_ref[...] = g
        d_ref[...], nm_ref[...], nv_ref[...] = _adam_math(w_ref[...], g, m_ref[...], v_ref[...])
        acc = dmall_ref[0:1, :]
        for k in range(1, N_DEV):
            acc = acc + dmall_ref[k:k + 1, :]
        gb_ref[...] = acc

    rows = pl.BlockSpec((tr, cols), lambda i, ch: (i, 0))
    grid_spec = pltpu.PrefetchScalarGridSpec(
        num_scalar_prefetch=1, grid=(d // tr,),
        in_specs=[pl.BlockSpec((N_DEV, tr), lambda i, ch: (0, i)),
                  pl.BlockSpec((N_DEV, cols), lambda i, ch: (0, ch[0])),
                  pl.BlockSpec(dmod8.shape, lambda i, ch: (0, 0)), rows, rows, rows],
        out_specs=[rows] * 4 + [pl.BlockSpec((1, dmod8.shape[1]), lambda i, ch: (0, 0))])
    return pl.pallas_call(
        body, name="ada_bwd", grid_spec=grid_spec,
        out_shape=[_sds((d, cols), F32)] * 4 + [_sds((1, dmod8.shape[1]), F32)],
        compiler_params=_params(("arbitrary",)),
    )(chip_arr, c8, dmod8, dmod8, w, m, v)


def _adam_math(w, g, m, v):
    m = ADAM_B1 * m + (1.0 - ADAM_B1) * g
    v = ADAM_B2 * v + (1.0 - ADAM_B2) * (g * g)
    m_hat = m / (1.0 - ADAM_B1 ** ADAM_STEP)
    v_hat = v / (1.0 - ADAM_B2 ** ADAM_STEP)
    delta = -ADAM_LR * (m_hat / (jnp.sqrt(v_hat) + ADAM_EPS) + ADAM_WD * w)
    return delta, m, v


def _adam(w, g, m, v, name, tr=256):
    rows, cols = w.shape
    if rows % tr:
        tr = rows

    def body(w_ref, g_ref, m_ref, v_ref, d_ref, nm_ref, nv_ref):
        d_ref[...], nm_ref[...], nv_ref[...] = _adam_math(w_ref[...], g_ref[...], m_ref[...], v_ref[...])

    spec = pl.BlockSpec((tr, cols), lambda i: (i, 0))
    return pl.pallas_call(
        body, name=name, grid=(rows // tr,), in_specs=[spec] * 4, out_specs=[spec] * 3,
        out_shape=[_sds(w.shape, F32)] * 3, compiler_params=_params(("parallel",)),
    )(w, g, m, v)


SMALL_REPLICATED = ("g_mix_pre", "g_mix_post", "conv_b", "w_rgate", "b_rgate", "w_igate", "b_igate", "lru_a",
                    "v_norm_g", "v_norm_b", "w_spatial", "b_spatial", "g_lru_out", "g_gmlp_out", "g_ffn_pre",
                    "g_ffn_post", "ffn_conv_b")
SMALL_COLUMN_SHARDED = ("conv_w", "ffn_conv_w")

SMALL_ROW_LEN = 86016
_SMALL_ROWS = (
    (("ffn_conv_w", 18432), ("conv_w", 2048), ("w_spatial", 65536)),
    (("w_rgate", 32768), ("w_igate", 32768), ("ffn_conv_b", 6144), ("g_mix_pre", 1024), ("g_mix_post", 1024),
     ("g_ffn_pre", 1024), ("g_ffn_post", 1024), ("conv_b", 512), ("b_rgate", 512), ("b_igate", 512),
     ("lru_a", 512), ("v_norm_g", 512), ("v_norm_b", 512), ("b_spatial", 512), ("g_lru_out", 512),
     ("g_gmlp_out", 512), ("loss", 128)),
)


def _small_slots():
    slots = {}
    for row, entries in enumerate(_SMALL_ROWS):
        off = 0
        for name, size in entries:
            slots[name] = (row, off)
            off += size
        assert off <= SMALL_ROW_LEN
    return slots


SMALL_SLOT = _small_slots()
SMALL_LANES = SMALL_ROW_LEN // SUBLANES


def _small_pieces(name, first, count):
    row, off = SMALL_SLOT[name]
    pos, pieces = off + first, []
    while count:
        sub, lane = divmod(pos, SMALL_LANES)
        n = min(count, SMALL_LANES - lane)
        pieces.append((row, sub, lane, n))
        pos, count = pos + n, count - n
    return pieces
ROW_VECTORS = ("ffn_conv_b", "g_mix_pre", "g_mix_post", "g_ffn_pre", "g_ffn_post", "conv_b", "lru_a", "v_norm_g",
               "v_norm_b", "g_lru_out", "g_gmlp_out")
HEAD_DIM = LRU_WIDTH // LRU_HEADS


def _pack_small(g, after):
    order = ("ffn_conv_w", "conv_w", "w_spatial", "w_rgate", "w_igate", "b_rgate", "b_igate", "b_spatial", "loss") \
        + ROW_VECTORS
    vmem = pl.BlockSpec(memory_space=pltpu.VMEM)

    def body(*refs):
        src = dict(zip(order, refs))
        out_ref = refs[len(order) + 1]
        out_ref[...] = jnp.zeros_like(out_ref)

        def put(name, first, val):
            col = 0
            for row, sub, lane, n in _small_pieces(name, first, val.shape[1]):
                out_ref[row, sub:sub + 1, lane:lane + n] = val[:, col:col + n]
                col += n

        for name in ROW_VECTORS + ("b_rgate", "b_igate", "loss"):
            put(name, 0, src[name][...])
        for name in ("ffn_conv_w", "conv_w"):
            k_taps, n = src[name].shape
            for k in range(k_taps):
                put(name, k * n, src[name][k:k + 1, :])
        for g_idx in range(GMLP_GROUPS):
            for i in range(GMLP_BLOCK):
                put("w_spatial", (g_idx * GMLP_BLOCK + i) * GMLP_BLOCK, src["w_spatial"][g_idx, i:i + 1, :])
        for name in ("w_rgate", "w_igate"):
            for h in range(LRU_HEADS):
                for i in range(HEAD_DIM):
                    r = h * HEAD_DIM + i
                    put(name, r * HEAD_DIM, src[name][r:r + 1, h * HEAD_DIM:(h + 1) * HEAD_DIM])
        eye = (lax.broadcasted_iota(jnp.int32, (GMLP_BLOCK, GMLP_BLOCK), 0)
               == lax.broadcasted_iota(jnp.int32, (GMLP_BLOCK, GMLP_BLOCK), 1))
        for g_idx in range(GMLP_GROUPS):
            col = src["b_spatial"][:, g_idx:g_idx + 1]
            put("b_spatial", g_idx * GMLP_BLOCK, _colsum(jnp.where(eye, col, 0.0)))

    return pl.pallas_call(
        body, name="pack_small", out_shape=_sds((2, SUBLANES, SMALL_LANES), F32),
        in_specs=[vmem] * len(order) + [_ANY], out_specs=vmem,
        compiler_params=pltpu.CompilerParams(vmem_limit_bytes=VMEM_LIMIT_BYTES),
    )(*[g[n] for n in order], after)


def _adam_small(g_small, w, m, v):
    vmem = pl.BlockSpec(memory_space=pltpu.VMEM)
    n_p = len(SMALL_REPLICATED)

    def body(g_ref, *refs):
        w_refs, m_refs, v_refs = refs[:n_p], refs[n_p:2 * n_p], refs[2 * n_p:3 * n_p]
        outs = refs[3 * n_p:]
        go, do, mo, vo = outs[:n_p], outs[n_p:2 * n_p], outs[2 * n_p:3 * n_p], outs[3 * n_p:]
        for k, name in enumerate(SMALL_REPLICATED):
            def take(first, count, name=name):
                parts = [g_ref[row, sub:sub + 1, lane:lane + n]
                         for row, sub, lane, n in _small_pieces(name, first, count)]
                return parts[0] if len(parts) == 1 else jnp.concatenate(parts, axis=1)

            shape = w_refs[k].shape
            if name in ROW_VECTORS:
                go[k][...] = take(0, shape[1])
            elif name in ("b_rgate", "b_igate"):
                for h in range(LRU_HEADS):
                    go[k][0, h:h + 1, :] = take(h * HEAD_DIM, HEAD_DIM)
            elif name == "b_spatial":
                for g_idx in range(GMLP_GROUPS):
                    go[k][0, g_idx:g_idx + 1, :] = take(g_idx * GMLP_BLOCK, GMLP_BLOCK)
            elif name == "w_spatial":
                for g_idx in range(GMLP_GROUPS):
                    for i in range(GMLP_BLOCK):
                        go[k][0, g_idx, i:i + 1, :] = take((g_idx * GMLP_BLOCK + i) * GMLP_BLOCK, GMLP_BLOCK)
            else:
                for h in range(LRU_HEADS):
                    for i in range(HEAD_DIM):
                        go[k][0, h, i:i + 1, :] = take((h * HEAD_DIM + i) * HEAD_DIM, HEAD_DIM)
            do[k][...], mo[k][...], vo[k][...] = _adam_math(w_refs[k][...], go[k][...], m_refs[k][...],
                                                             v_refs[k][...])

    names = SMALL_REPLICATED
    out_shape = [_sds(w[n].shape, F32) for n in names] * 4
    res = pl.pallas_call(
        body, name="adam_small", out_shape=out_shape,
        in_specs=[vmem] * (1 + 3 * n_p), out_specs=[vmem] * (4 * n_p),
        compiler_params=pltpu.CompilerParams(vmem_limit_bytes=VMEM_LIMIT_BYTES),
    )(g_small, *[w[n] for n in names], *[m[n] for n in names], *[v[n] for n in names])
    return [dict(zip(names, res[k * n_p:(k + 1) * n_p])) for k in range(4)]


def _adam_cols(name, g_small, w, m, v, chip_arr):
    _, k_taps, n = w.shape
    row, off = SMALL_SLOT[name]
    first = off // n
    per_sub = SMALL_LANES // n

    def body(chip_ref, *refs):
        g_refs = refs[:k_taps]
        w_ref, m_ref, v_ref, go_ref, d_ref, nm_ref, nv_ref = refs[k_taps:]
        for k in range(k_taps):
            tap = (0, slice(k, k + 1), slice(None))
            sub = (first + N_CHIPS * k + chip_ref[0]) // per_sub
            g = g_refs[k][row, pl.ds(sub, 1), :]
            go_ref[tap] = g
            d_ref[tap], nm_ref[tap], nv_ref[tap] = _adam_math(w_ref[tap], g, m_ref[tap], v_ref[tap])

    whole = pl.BlockSpec(w.shape, lambda i, ch: (0, 0, 0))
    taps = [pl.BlockSpec((2, SUBLANES, n),
                         functools.partial(lambda i, ch, k: (0, 0, (first + N_CHIPS * k + ch[0]) % per_sub), k=k))
            for k in range(k_taps)]
    grid_spec = pltpu.PrefetchScalarGridSpec(
        num_scalar_prefetch=1, grid=(1,), in_specs=taps + [whole] * 3, out_specs=[whole] * 4)
    return pl.pallas_call(body, name="adam_" + name, grid_spec=grid_spec, out_shape=[_sds(w.shape, F32)] * 4)(
        chip_arr, *[g_small] * k_taps, w, m, v)


def kernel(x, c, w_ada, b_ada, g_mix_pre, g_mix_post, w_in, conv_w, conv_b, w_rgate, b_rgate, w_igate, b_igate, lru_a, v_norm_g, v_norm_b, w_spatial, b_spatial, g_lru_out, g_gmlp_out, w_out, g_ffn_pre, g_ffn_post, w_up, ffn_conv_w, ffn_conv_b, w_down, loss_target, m_w_ada, m_b_ada, m_g_mix_pre, m_g_mix_post, m_w_in, m_conv_w, m_conv_b, m_w_rgate, m_b_rgate, m_w_igate, m_b_igate, m_lru_a, m_v_norm_g, m_v_norm_b, m_w_spatial, m_b_spatial, m_g_lru_out, m_g_gmlp_out, m_w_out, m_g_ffn_pre, m_g_ffn_post, m_w_up, m_ffn_conv_w, m_ffn_conv_b, m_w_down, v_w_ada, v_b_ada, v_g_mix_pre, v_g_mix_post, v_w_in, v_conv_w, v_conv_b, v_w_rgate, v_b_rgate, v_w_igate, v_b_igate, v_lru_a, v_v_norm_g, v_v_norm_b, v_w_spatial, v_b_spatial, v_g_lru_out, v_g_gmlp_out, v_w_out, v_g_ffn_pre, v_g_ffn_post, v_w_up, v_ffn_conv_w, v_ffn_conv_b, v_w_down):
    args = dict(locals())
    names = ("w_ada", "b_ada", "g_mix_pre", "g_mix_post", "w_in", "conv_w", "conv_b", "w_rgate", "b_rgate",
             "w_igate", "b_igate", "lru_a", "v_norm_g", "v_norm_b", "w_spatial", "b_spatial", "g_lru_out",
             "g_gmlp_out", "w_out", "g_ffn_pre", "g_ffn_post", "w_up", "ffn_conv_w", "ffn_conv_b", "w_down")
    drop = lambda a: a if a.ndim == 2 else a[0]
    w = {n: drop(args[n]) for n in names}
    m = {n: drop(args["m_" + n]) for n in names}
    v = {n: drop(args["v_" + n]) for n in names}
    xi, yi, ci = _position()
    me_arr = jnp.reshape(4 * xi + 2 * yi + ci, (1,)).astype(jnp.int32)
    chip_arr = jnp.reshape(2 * xi + yi, (1,)).astype(jnp.int32)
    c_arr = jnp.reshape(ci, (1,)).astype(jnp.int32)
    pos_arr = jnp.stack([ci, 2 * xi + yi]).astype(jnp.int32)

    big = ("w_in", "w_out", "w_up", "w_down")
    lands_a = _cast_place([w[n] for n in big[:2]], chip_arr, "cast_place_a")
    start_a, wait_a = _gather_plan([w[n].shape[0] for n in big[:2]])
    start_b, wait_b = _gather_plan([w[n].shape[0] for n in big[2:]])

    row0 = jnp.concatenate([c, w["conv_w"].reshape(1, -1), w["ffn_conv_w"].reshape(1, -1)], axis=1)
    g0 = _allgather8(row0, "gather_cond")[:, 0, :]
    send_a, recv_a, _, lands_a, token_a = _split_start([], lands_a, start_a, 6, g0, "gather_start_a")
    lands_b = _cast_place([w[n] for n in big[2:]], chip_arr + token_a[0, 0].astype(jnp.int32), "cast_place_b")
    c8 = g0[:, :D_MODEL]
    per_chip = g0[0::2]
    conv_w_full = per_chip[:, D_MODEL:D_MODEL + 512].reshape(N_CHIPS, 4, 128).transpose(1, 0, 2).reshape(4, 512)
    ffn_conv_w_full = per_chip[:, D_MODEL + 512:].reshape(N_CHIPS, 3, 1536).transpose(1, 0, 2).reshape(3, 2 * D_FF)
    mod_parts = _allgather8(_ada_fwd(c8 + token_a[0:1, 0:1], w["w_ada"]), "gather_mod")
    send_b, recv_b, _, lands_b, token_b = _split_start([], lands_b, start_b, 6, mod_parts, "gather_start_b")
    mod = _mod_select(mod_parts, w["b_ada"].reshape(1, -1), me_arr, token_b).reshape(N_MOD, D_MODEL)
    sh_m, sc_m, gt_m, sh_f, sc_f, gt_f = [mod[k:k + 1] for k in range(N_MOD)]

    small = {n: w[n] for n in SMALL_REPLICATED}
    small["conv_w"] = conv_w_full
    small["ffn_conv_w"] = ffn_conv_w_full
    row = lambda a: a.reshape(1, -1)
    seq_params, ws_t = _seq_params(small)
    glo, ggo = row(small["g_lru_out"]), row(small["g_gmlp_out"])
    g_pre, g_post = row(small["g_mix_pre"]), row(small["g_mix_post"])
    g_pre2, g_post2 = row(small["g_ffn_pre"]), row(small["g_ffn_post"])
    fw, fb = small["ffn_conv_w"], row(small["ffn_conv_b"])
    xs, tgt = x[0], loss_target[0]

    _, lands_a = _split_wait(send_a, recv_a, [], lands_a, wait_a, mod, "gather_wait_a")
    w_in4, w_out4 = _forward_to_sibling(lands_a, "forward_a")
    w_out_b = w_out4.reshape(D_MODEL, D_MODEL)
    z, h = _mix_in(xs, sc_m, sh_m, g_pre, w_in4)
    ycat, hst, stash = _seqmix(z, seq_params, glo, ggo)
    _, lands_b = _split_wait(send_b, recv_b, [], lands_b, wait_b, ycat, "gather_wait_b")
    fwd_start, fwd_wait = _forward_plan([w[n].shape[0] for n in big[2:]])
    fwd_send, fwd_recv, _, lands_b, tok = _split_start([], lands_b, fwd_start, 6, pos_arr, "forward_start_b")
    y, x1, h2 = _mix_out(ycat, xs, w_out_b, gt_m + tok[0:1, 0:1], g_post, g_pre2, sc_f, sh_f)
    _, (w_up4, w_down4) = _split_wait(fwd_send, fwd_recv, [], lands_b, fwd_wait, h2, "forward_wait_b")
    w_down_b = w_down4.reshape(D_FF, D_MODEL)
    up0, pre, act, dy2, dx2, loss, dgt_f, dg_post2 = _ffn_fwd(h2, x1, tgt, w_up4, w_down_b, fw, fb, gt_f, g_post2)

    dup0, dfw, dfb = _ffn_bwd_a(dy2, pre, up0, w_down_b, fw)
    gw_up = _wgrad(h2, dup0, N_CHIPS, "wgrad_up", True)
    gw_down = _wgrad(act, dy2, 2, "wgrad_down", False)
    ex_start, ex_wait = _exchange_plan(2)
    sg_start, sg_wait = _swap_gathered_plan(2)
    grads, deltas, new_m, new_v = {}, {}, {}, {}

    def swap_start(parts, name):
        sw_start, sw_wait = _swap_halves_plan([p.shape[1] // 2 for p in parts])
        recv = [lax.empty((N_CHIPS, p.shape[1] // 2, p.shape[2]), BF16) for p in parts]
        send_s, recv_s, parts, recv, token = _split_start(parts, recv, sw_start, len(parts), pos_arr,
                                                           "swap_start_" + name)
        return (send_s, recv_s, parts, recv, sw_wait), token

    def exchange_start(swap, tags, after, name):
        send_s, recv_s, parts, recv, sw_wait = swap
        parts, recv = _split_wait(send_s, recv_s, parts, recv, sw_wait, after, "swap_wait_" + name)
        both = [_chip_sum(p, r, pos_arr, "chip_sum_" + t) for p, r, t in zip(parts, recv, tags)]
        sums, gath = [b[0] for b in both], [b[1] for b in both]
        return _split_start(sums, gath, ex_start, 3 * len(parts), pos_arr, "exchange_start_" + name)

    def gathered_start(exchange, after, name):
        send_s, recv_s, sums, gath, _ = exchange
        _, gath = _split_wait(send_s, recv_s, sums, gath, ex_wait, after, "exchange_wait_" + name)
        send_s, recv_s, _, gath, token = _split_start([], gath, sg_start, len(gath), pos_arr,
                                                      "gathered_start_" + name)
        return (send_s, recv_s, gath), token

    def gathered_wait(gathered, after, name):
        send_s, recv_s, gath = gathered
        return _split_wait(send_s, recv_s, [], gath, sg_wait, after, "gathered_wait_" + name)[1]

    def adam_big(t, gath, after):
        grads[t], deltas[t], new_m[t], new_v[t] = _adam_gathered(w[t], gath, m[t], v[t], c_arr, after, "adam_" + t)

    def behind(value, token):
        return value + token[0:1, 0:1]

    tags_b, tags_a = ("w_up", "w_down"), ("w_in", "w_out")
    swap_b, tok = swap_start([gw_up, gw_down.reshape(N_CHIPS, -1, D_MODEL)], "b")
    dx1, dy, dsh_f, dsc_f, dg_pre2, dgt_m, dg_post = _ffn_bwd_b(
        dup0, x1, y, dx2, w_up4, g_pre2, behind(sc_f, tok), sh_f, gt_m, g_post)
    exchange_b = exchange_start(swap_b, tags_b, dg_post, "b")
    (dz, grad_x, dcw, dcb, dwr, dwi, dbr, dbi, dspa, dng, dnb, dws, dbs_t, dglo, dggo, dsh_m, dsc_m,
     dg_pre) = _seqmix_bwd(z, hst, stash, dy, w_out_b, seq_params, ws_t, behind(glo, exchange_b[4]), ggo,
                           xs, dx1, w_in4, g_pre, sc_m)
    gw_in = _wgrad(h, dz, N_CHIPS, "wgrad_in", True)
    gw_out = _wgrad(ycat, dy, 1, "wgrad_out", False)
    swap_a, tok = swap_start([gw_in, gw_out.reshape(N_CHIPS, -1, D_MODEL)], "a")

    dmod = jnp.concatenate([behind(dsh_m, tok), dsc_m, dgt_m, dsh_f, dsc_f, dgt_f], axis=1)
    dmod8 = _allgather8(dmod, "gather_dmod")[:, 0, :]
    small_grads = dict(
        g_mix_pre=dg_pre, g_mix_post=dg_post, conv_w=dcw, conv_b=dcb, w_rgate=dwr, b_rgate=dbr, w_igate=dwi,
        b_igate=dbi, lru_a=dspa, v_norm_g=dng, v_norm_b=dnb, w_spatial=dws, b_spatial=dbs_t, g_lru_out=dglo,
        g_gmlp_out=dggo, g_ffn_pre=dg_pre2, g_ffn_post=dg_post2, ffn_conv_w=dfw, ffn_conv_b=dfb,
        loss=loss)
    g_small = _allreduce_small(_pack_small(small_grads, dmod8), "reduce_small")
    total = g_small[_small_pieces("loss", 0, 1)[0][:3]]
    exchange_a = exchange_start(swap_a, tags_a, g_small, "a")
    gathered_b, tok = gathered_start(exchange_b, exchange_a[4], "b")

    grads["w_ada"], deltas["w_ada"], new_m["w_ada"], new_v["w_ada"], g_b_ada = _ada_bwd(
        c8, behind(dmod8, tok), chip_arr, w["w_ada"], m["w_ada"], v["w_ada"])
    rep = SMALL_REPLICATED
    small_out = _adam_small(g_small, {n: args[n] for n in rep}, {n: args["m_" + n] for n in rep},
                            {n: args["v_" + n] for n in rep})
    for n in rep:
        grads[n], deltas[n], new_m[n], new_v[n] = [group[n] for group in small_out]
    for n in SMALL_COLUMN_SHARDED:
        grads[n], deltas[n], new_m[n], new_v[n] = _adam_cols(n, g_small, args[n], args["m_" + n],
                                                             args["v_" + n], chip_arr)
    d_b, m_b, v_b = _adam(w["b_ada"], g_b_ada, m["b_ada"], v["b_ada"], "adam_b_ada")
    grads["b_ada"], deltas["b_ada"], new_m["b_ada"], new_v["b_ada"] = g_b_ada, d_b, m_b, v_b

    gath_up, gath_down = gathered_wait(gathered_b, d_b, "b")
    adam_big("w_down", gath_down, pos_arr)
    gathered_a, tok = gathered_start(exchange_a, deltas["w_down"], "a")
    adam_big("w_up", gath_up, tok)
    gath_in, gath_out = gathered_wait(gathered_a, deltas["w_up"], "a")
    adam_big("w_in", gath_in, pos_arr)
    adam_big("w_out", gath_out, pos_arr)

    outs = [total, grad_x[None]]
    for group in (grads, deltas, new_m, new_v):
        outs.extend(group[n].reshape(args[n].shape) for n in names)
    return tuple(outs)
```

```python
import functools
import math

import jax
import jax.numpy as jnp
from jax import lax
from jax.experimental import pallas as pl
from jax.experimental.pallas import tpu as pltpu

F32 = jnp.float32
BF16 = jnp.bfloat16
MESH = pl.DeviceIdType.MESH

D_MODEL = 1024
LRU_WIDTH = 512
LRU_HEADS = 8
GMLP_GROUPS = 4
GMLP_BLOCK = 128
CHUNK = 64
D_FF = 3072
N_MOD = 6
EPS = 1e-6
LRU_C = 8.0
N_CHIPS = 4
N_DEV = 8

ADAM_LR = 0.001
ADAM_B1 = 0.9
ADAM_B2 = 0.999
ADAM_EPS = 1e-08
ADAM_WD = 0.01
ADAM_STEP = 10

GELU_C0 = math.sqrt(2.0 / math.pi)
GELU_C1 = 0.044715

VMEM_LIMIT_BYTES = 56 * 1024 * 1024
SUBLANES = 8
BF16_SUBLANES = 16
FFN_CHUNK = 768
SUB_ROWS = 256


def _gelu_gate(x):
    x2 = x * x
    z = x * ((2.0 * GELU_C0 * GELU_C1) * x2 + 2.0 * GELU_C0)
    return 1.0 / (1.0 + jnp.exp(-z)), x2


def _gelu(x):
    t = jnp.tanh(GELU_C0 * (x + GELU_C1 * x * x * x))
    return 0.5 * x * (1.0 + t)


def _gelu_and_grad(x):
    s, x2 = _gelu_gate(x)
    g = x * s
    dz = (6.0 * GELU_C0 * GELU_C1) * x2 + 2.0 * GELU_C0
    return g, s + g * (1.0 - s) * dz


def _sigmoid(x):
    return 1.0 / (1.0 + jnp.exp(-x))


def _log1p(u):
    w = 1.0 + u
    return jnp.where(w == 1.0, u, jnp.log(w) * (u / (w - 1.0)))


def _softplus(x):
    return jnp.maximum(x, 0.0) + _log1p(jnp.exp(-jnp.abs(x)))


def _neg_expm1(x):
    u = jnp.exp(x)
    um1 = u - 1.0
    tiny = um1 == 0.0
    small = um1 * (x / jnp.log(jnp.where(tiny, 2.0, jnp.maximum(u, 0.25))))
    return -jnp.where(tiny, x, jnp.where(x < -1.0, um1, small))


def _msq_rsqrt(v):
    return lax.rsqrt(jnp.mean(v * v, axis=-1, keepdims=True) + EPS)


def _rms_bwd(dyn, yn, r):
    return r * (dyn - yn * jnp.mean(dyn * yn, axis=-1, keepdims=True))


def _colsum(v):
    return jnp.sum(v, axis=0, keepdims=True)


def _shift_down(cur, prev8, k):
    rolled = pltpu.roll(cur, k, 0)
    head = pltpu.roll(prev8, k, 0)
    row8 = lax.broadcasted_iota(jnp.int32, (SUBLANES, cur.shape[1]), 0)
    first = jnp.where(row8 < k, head, rolled[0:SUBLANES])
    return jnp.concatenate([first, rolled[SUBLANES:]], axis=0)


def _shift_up(cur, next8, k):
    t = cur.shape[0]
    rolled = pltpu.roll(cur, t - k, 0)
    tail = pltpu.roll(next8, SUBLANES - k, 0)
    row8 = lax.broadcasted_iota(jnp.int32, (SUBLANES, cur.shape[1]), 0)
    last = jnp.where(row8 >= SUBLANES - k, tail, rolled[t - SUBLANES:])
    return jnp.concatenate([rolled[:t - SUBLANES], last], axis=0)


def _scan_fwd(a, b):
    t = a.shape[0]
    row = lax.broadcasted_iota(jnp.int32, a.shape, 0)
    d = 1
    while d < t:
        keep = row >= d
        a_s = jnp.where(keep, pltpu.roll(a, d, 0), 1.0)
        b_s = jnp.where(keep, pltpu.roll(b, d, 0), 0.0)
        b = a * b_s + b
        a = a * a_s
        d *= 2
    return a, b


def _scan_bwd(a, g):
    t = a.shape[0]
    row = lax.broadcasted_iota(jnp.int32, a.shape, 0)
    d = 1
    while d < t:
        keep = row < t - d
        a_s = jnp.where(keep, pltpu.roll(a, t - d, 0), 1.0)
        g_s = jnp.where(keep, pltpu.roll(g, t - d, 0), 0.0)
        g = a * g_s + g
        a = a * a_s
        d *= 2
    return a, g


def _dot(a, b):
    return jnp.dot(a, b, preferred_element_type=F32)


def _dot_nt(a, b):
    return lax.dot_general(a, b, (((1,), (1,)), ((), ())), preferred_element_type=F32)


def _dot_tn(a, b):
    return lax.dot_general(a, b, (((0,), (0,)), ((), ())), preferred_element_type=F32)


def _rows(ts, cols, rev_of=None):
    if rev_of is None:
        return pl.BlockSpec((ts, cols), lambda i: (i, 0))
    return pl.BlockSpec((ts, cols), lambda i: (rev_of - 1 - i, 0))


def _halo_prev(ts, cols, halo, rev_of=None, col_block=0):
    per = ts // halo
    if rev_of is None:
        return pl.BlockSpec((halo, cols), lambda i: (jnp.maximum(i * per - 1, 0), col_block))
    return pl.BlockSpec((halo, cols), lambda i: (jnp.maximum((rev_of - 1 - i) * per - 1, 0), col_block))


def _full(shape):
    nd = len(shape)
    return pl.BlockSpec(shape, lambda *_: (0,) * nd)


_RESIDENT = pl.BlockSpec(memory_space=pltpu.VMEM)


def _params(sem):
    return pltpu.CompilerParams(dimension_semantics=sem, vmem_limit_bytes=VMEM_LIMIT_BYTES)


def _sds(shape, dtype):
    return jax.ShapeDtypeStruct(shape, dtype)


def _sub_tiles(ts):
    return [slice(r0, r0 + SUB_ROWS) for r0 in range(0, ts, SUB_ROWS)]


N_STASH = 12
(ST_XC, ST_R, ST_IG, ST_A, ST_MULT, ST_GL, ST_DGL, ST_U, ST_DU, ST_Q, ST_VHAT, ST_SPB) = range(N_STASH)


def _seq_param_specs():
    return [_full((4, 512)), _full((1, 512)), _full((512, 512)), _full((512, 512)), _full((1, 512)),
            _full((1, 512)), _full((1, 512)), _full((1, 512)), _full((1, 512)), _full((4, 128, 128)),
            _full((128, 4))]


def _seqmix(x, sc, sh, g_pre, w_in4, seq_params, glo, ggo, ts=256):
    s, d = x.shape
    nt = s // ts

    def body(x_ref, sc_ref, sh_ref, gpre_ref, win_ref, cw_ref, cb_ref, bdr_ref, bdi_ref, br_ref, bi_ref, la_ref,
             ng_ref, nb_ref, ws_ref, bst_ref, glo_ref, ggo_ref, hin_ref, lx_ref, ycat_ref, hst_ref, st_ref,
             hcarry, lxprev, sp_scr):
        i = pl.program_id(0)

        @pl.when(i == 0)
        def _():
            hcarry[...] = jnp.zeros_like(hcarry)
            lxprev[...] = jnp.zeros_like(lxprev)

        xv = x_ref[...]
        hin = ((xv * _msq_rsqrt(xv) * gpre_ref[...]) * (1.0 + sc_ref[...]) + sh_ref[...]).astype(BF16)
        hin_ref[...] = hin
        z = [_dot(hin, win_ref[k]) for k in range(N_CHIPS)]

        lx = z[0]
        lx_ref[...] = lx
        prev8 = lxprev[...]
        lxprev[...] = lx[ts - SUBLANES:, :]
        xc = (cw_ref[3:4, :] * lx + cw_ref[2:3, :] * _shift_down(lx, prev8, 1)
              + cw_ref[1:2, :] * _shift_down(lx, prev8, 2) + cw_ref[0:1, :] * _shift_down(lx, prev8, 3)
              + cb_ref[...])
        xcb = xc.astype(BF16)
        r = _sigmoid(_dot(xcb, bdr_ref[...]) + br_ref[...])
        ig = _sigmoid(_dot(xcb, bdi_ref[...]) + bi_ref[...])
        log_a = (-LRU_C) * r * _softplus(-la_ref[...])
        a = jnp.exp(log_a)
        mult = jnp.sqrt(_neg_expm1(2.0 * log_a))
        acum, hloc = _scan_fwd(a, mult * (ig * xc))
        h = hloc + acum * hcarry[...]
        hcarry[...] = h[ts - 1:ts, :]
        hst_ref[...] = h
        gl, dgl = _gelu_and_grad(z[1])
        y_l = h * gl
        for slot, val in ((ST_XC, xc), (ST_R, r), (ST_IG, ig), (ST_A, a), (ST_MULT, mult), (ST_GL, gl),
                          (ST_DGL, dgl)):
            st_ref[slot] = val

        u, du = _gelu_and_grad(z[2])
        vg, dvg = _gelu_and_grad(z[3])
        vc = vg - jnp.mean(vg, axis=-1, keepdims=True)
        rstd = lax.rsqrt(jnp.mean(vc * vc, axis=-1, keepdims=True) + EPS)
        vhat = vc * rstd
        vb = (vhat * ng_ref[...] + nb_ref[...]).astype(BF16)
        for n in range(ts // GMLP_BLOCK):
            rs = slice(n * GMLP_BLOCK, (n + 1) * GMLP_BLOCK)
            for g in range(GMLP_GROUPS):
                cs = slice(g * 128, (g + 1) * 128)
                sp_scr[rs, cs] = _dot(ws_ref[g], vb[rs, cs]) + bst_ref[:, g:g + 1]
        spb = sp_scr[...]
        y_g = u * spb
        for slot, val in ((ST_U, u), (ST_DU, du), (ST_Q, rstd * dvg), (ST_VHAT, vhat), (ST_SPB, spb)):
            st_ref[slot] = val

        ycat_ref[:, 0:512] = (y_l * _msq_rsqrt(y_l) * glo_ref[...]).astype(BF16)
        ycat_ref[:, 512:1024] = (y_g * _msq_rsqrt(y_g) * ggo_ref[...]).astype(BF16)

    vec = _full((1, d))
    return pl.pallas_call(
        body, grid=(nt,), name="seqmix",
        in_specs=[_rows(ts, d), vec, vec, vec, _full(w_in4.shape)] + _seq_param_specs()
        + [_full((1, 512)), _full((1, 512))],
        out_specs=[_rows(ts, d), _rows(ts, 512), _rows(ts, d), _rows(ts, 512),
                   pl.BlockSpec((N_STASH, ts, 512), lambda i: (0, i, 0))],
        out_shape=[_sds((s, d), BF16), _sds((s, 512), F32), _sds((s, d), BF16), _sds((s, 512), F32),
                   _sds((N_STASH, s, 512), F32)],
        scratch_shapes=[pltpu.VMEM((1, 512), F32), pltpu.VMEM((SUBLANES, 512), F32), pltpu.VMEM((ts, 512), F32)],
        compiler_params=_params(("arbitrary",)),
    )(x, sc, sh, g_pre, w_in4, *seq_params, glo, ggo)


def _mix_out(ycat, x, w_out, gt_m, g_post, g_pre2, sc_f, sh_f, ts=512):
    s, d = x.shape

    def body(yc_ref, x_ref, w_ref, gt_ref, gp_ref, g2_ref, sc_ref, sh_ref, y_ref, x1_ref, h2_ref):
        for rs in _sub_tiles(ts):
            y = _dot(yc_ref[rs, :], w_ref[...])
            y_ref[rs, :] = y
            x1 = x_ref[rs, :] + gt_ref[...] * (y * _msq_rsqrt(y) * gp_ref[...])
            x1_ref[rs, :] = x1
            h2 = (x1 * _msq_rsqrt(x1) * g2_ref[...]) * (1.0 + sc_ref[...]) + sh_ref[...]
            h2_ref[rs, :] = h2.astype(BF16)

    vec = _full((1, d))
    return pl.pallas_call(
        body, grid=(s // ts,), name="mix_out",
        in_specs=[_rows(ts, d), _rows(ts, d), _full((d, d)), vec, vec, vec, vec, vec],
        out_specs=[_rows(ts, d), _rows(ts, d), _rows(ts, d)],
        out_shape=[_sds((s, d), F32), _sds((s, d), F32), _sds((s, d), BF16)],
        compiler_params=_params(("parallel",)),
    )(ycat, x, w_out, gt_m, g_post, g_pre2, sc_f, sh_f)


def _ffn_cols(j):
    per = (2 * D_FF // N_CHIPS) // FFN_CHUNK
    return j // per, (j % per) * FFN_CHUNK, j * FFN_CHUNK


def _ffn_fwd(h2, x1, tgt, w_up4, w_down, fw, fb, gt_f, g_post, ts=256):
    s, d = x1.shape
    nch = D_FF // FFN_CHUNK

    def body(h2_ref, x1_ref, tgt_ref, wup_ref, wdn_ref, fw_ref, fb_ref, gt_ref, gp_ref,
             up0_ref, pre_ref, act_ref, dy2_ref, dx2_ref, loss_ref, dgt_ref, dgp_ref, tail_ref):
        i = pl.program_id(0)

        @pl.when(i == 0)
        def _():
            tail_ref[...] = jnp.zeros_like(tail_ref)
            loss_ref[...] = jnp.zeros_like(loss_ref)
            dgt_ref[...] = jnp.zeros_like(dgt_ref)
            dgp_ref[...] = jnp.zeros_like(dgp_ref)

        hb = h2_ref[...]

        def up_project(j):
            sh_g, off, _ = _ffn_cols(j)
            return [_dot(hb, wup_ref[shard, :, off:off + FFN_CHUNK]).astype(BF16) for shard in (sh_g, sh_g + 2)]

        y2 = jnp.zeros((ts, d), F32)
        ahead = up_project(0)
        for j in range(nch):
            _, _, col = _ffn_cols(j)
            ubs = ahead
            if j + 1 < nch:
                ahead = up_project(j + 1)
            halves = []
            for ub, c0 in zip(ubs, (col, D_FF + col)):
                cs = slice(c0, c0 + FFN_CHUNK)
                up0_ref[:, cs] = ub
                u = ub.astype(F32)
                prev8 = tail_ref[:, cs]
                tail_ref[:, cs] = u[ts - SUBLANES:, :]
                halves.append(fw_ref[2:3, cs] * u + fw_ref[1:2, cs] * _shift_down(u, prev8, 1)
                              + fw_ref[0:1, cs] * _shift_down(u, prev8, 2) + fb_ref[:, cs])
                pre_ref[:, cs] = halves[-1].astype(BF16)
            act = (_gelu(halves[0]) * halves[1]).astype(BF16)
            act_ref[:, col:col + FFN_CHUNK] = act
            y2 = y2 + _dot(act, wdn_ref[col:col + FFN_CHUNK, :])
        r2 = _msq_rsqrt(y2)
        yn = y2 * r2
        yng = yn * gp_ref[...]
        e = x1_ref[...] + gt_ref[...] * yng - tgt_ref[...]
        loss_ref[...] += jnp.sum(e * e) * (0.5 / d)
        dx2 = e * (1.0 / d)
        dx2_ref[...] = dx2
        dgt_ref[...] += _colsum(dx2 * yng)
        dyng = dx2 * gt_ref[...]
        dgp_ref[...] += _colsum(dyng * yn)
        dy2_ref[...] = _rms_bwd(dyng * gp_ref[...], yn, r2).astype(BF16)

    vec = _full((1, d))
    return pl.pallas_call(
        body, grid=(s // ts,), name="ffn_fwd",
        in_specs=[_rows(ts, d), _rows(ts, d), _rows(ts, d), _RESIDENT, _RESIDENT,
                  _full((3, 2 * D_FF)), _full((1, 2 * D_FF)), vec, vec],
        out_specs=[_rows(ts, 2 * D_FF), _rows(ts, 2 * D_FF), _rows(ts, D_FF), _rows(ts, d), _rows(ts, d),
                   _full((1, 128)), vec, vec],
        out_shape=[_sds((s, 2 * D_FF), BF16), _sds((s, 2 * D_FF), BF16), _sds((s, D_FF), BF16), _sds((s, d), BF16),
                   _sds((s, d), F32), _sds((1, 128), F32), _sds((1, d), F32), _sds((1, d), F32)],
        scratch_shapes=[pltpu.VMEM((SUBLANES, 2 * D_FF), F32)],
        compiler_params=_params(("arbitrary",)),
    )(h2, x1, tgt, w_up4, w_down, fw, fb, gt_f, g_post)


def _shift_up_mxu(vb, up_mat, next8, k):
    t = vb.shape[0]
    main = _dot(up_mat, vb)
    tail = pltpu.roll(next8, SUBLANES - k, 0)
    row8 = lax.broadcasted_iota(jnp.int32, next8.shape, 0)
    last = main[t - SUBLANES:] + jnp.where(row8 >= SUBLANES - k, tail, 0.0)
    return jnp.concatenate([main[:t - SUBLANES], last], axis=0)


def _ffn_bwd_a(dy2, pre, up0, w_down, fw, ts=256):
    s, d = dy2.shape
    nt = s // ts
    nch = D_FF // FFN_CHUNK
    wide = 2 * D_FF
    up_mats = jnp.stack([jnp.eye(ts, k=1, dtype=BF16), jnp.eye(ts, k=2, dtype=BF16)])

    def body(dy2_ref, pre_ref, up0_ref, wdn_ref, fw_ref, um_ref, dup0_ref, dfw_ref, dfb_ref, next_ref):
        i = pl.program_id(0)

        @pl.when(i == 0)
        def _():
            next_ref[...] = jnp.zeros_like(next_ref)
            dfw_ref[...] = jnp.zeros_like(dfw_ref)
            dfb_ref[...] = jnp.zeros_like(dfb_ref)

        dyb = dy2_ref[...]
        for j in range(nch):
            _, _, col = _ffn_cols(j)
            dact = _dot_nt(dyb, wdn_ref[col:col + FFN_CHUNK, :])
            gl, dgl = _gelu_and_grad(pre_ref[:, col:col + FFN_CHUNK].astype(F32))
            dpre = (dact * pre_ref[:, D_FF + col:D_FF + col + FFN_CHUNK].astype(F32) * dgl, dact * gl)
            for half, c0 in enumerate((col, D_FF + col)):
                cs = slice(c0, c0 + FFN_CHUNK)
                dp = dpre[half]
                dpb = dp.astype(BF16)
                nxt = next_ref[:, cs]
                next_ref[:, cs] = dpb.astype(F32)[0:SUBLANES, :]
                su1 = _shift_up_mxu(dpb, um_ref[0], nxt, 1)
                su2 = _shift_up_mxu(dpb, um_ref[1], nxt, 2)
                u = up0_ref[:, cs].astype(F32)
                dfb_ref[:, cs] += _colsum(dp)
                dfw_ref[2:3, cs] += _colsum(dp * u)
                dfw_ref[1:2, cs] += _colsum(su1 * u)
                dfw_ref[0:1, cs] += _colsum(su2 * u)
                dup0 = fw_ref[2:3, cs] * dp + fw_ref[1:2, cs] * su1 + fw_ref[0:1, cs] * su2
                dup0_ref[:, cs] = dup0.astype(BF16)

    return pl.pallas_call(
        body, grid=(nt,), name="ffn_bwd_a",
        in_specs=[_rows(ts, d, nt), _rows(ts, wide, nt), _rows(ts, wide, nt), _RESIDENT,
                  _full((3, wide)), _full((2, ts, ts))],
        out_specs=[_rows(ts, wide, nt), _full((3, wide)), _full((1, wide))],
        out_shape=[_sds((s, wide), BF16), _sds((3, wide), F32), _sds((1, wide), F32)],
        scratch_shapes=[pltpu.VMEM((SUBLANES, wide), F32)],
        compiler_params=_params(("arbitrary",)),
    )(dy2, pre, up0, w_down, fw, up_mats)


def _ffn_bwd_b(dup0, x1, y, dx2, w_up4, g_pre2, sc_f, sh_f, gt_m, g_post_m, ts=512):
    s, d = x1.shape
    shard_cols = 2 * D_FF // N_CHIPS

    def body(dup_ref, x1_ref, y_ref, dx2_ref, wup_ref, g2_ref, sc_ref, sh_ref, gt_ref, gp_ref,
             dx1_ref, dy_ref, dsh_ref, dsc_ref, dg2_ref, dgt_ref, dgp_ref):
        i = pl.program_id(0)

        @pl.when(i == 0)
        def _():
            for ref in (dsh_ref, dsc_ref, dg2_ref, dgt_ref, dgp_ref):
                ref[...] = jnp.zeros_like(ref)

        for rs in _sub_tiles(ts):
            dh2 = jnp.zeros((SUB_ROWS, d), F32)
            for k in range(N_CHIPS):
                dh2 = dh2 + _dot_nt(dup_ref[rs, k * shard_cols:(k + 1) * shard_cols], wup_ref[k])
            x1v = x1_ref[rs, :]
            r2 = _msq_rsqrt(x1v)
            xn = x1v * r2
            hn = xn * g2_ref[...]
            dsh_ref[...] += _colsum(dh2)
            dsc_ref[...] += _colsum(dh2 * hn)
            dhn = dh2 * (1.0 + sc_ref[...])
            dg2_ref[...] += _colsum(dhn * xn)
            dx1 = dx2_ref[rs, :] + _rms_bwd(dhn * g2_ref[...], xn, r2)
            dx1_ref[rs, :] = dx1
            yv = y_ref[rs, :]
            ry = _msq_rsqrt(yv)
            yn = yv * ry
            dgt_ref[...] += _colsum(dx1 * (yn * gp_ref[...]))
            dyng = dx1 * gt_ref[...]
            dgp_ref[...] += _colsum(dyng * yn)
            dy_ref[rs, :] = _rms_bwd(dyng * gp_ref[...], yn, ry).astype(BF16)

    vec = _full((1, d))
    return pl.pallas_call(
        body, grid=(s // ts,), name="ffn_bwd_b",
        in_specs=[_rows(ts, 2 * D_FF), _rows(ts, d), _rows(ts, d), _rows(ts, d), _RESIDENT,
                  vec, vec, vec, vec, vec],
        out_specs=[_rows(ts, d), _rows(ts, d), vec, vec, vec, vec, vec],
        out_shape=[_sds((s, d), F32), _sds((s, d), BF16)] + [_sds((1, d), F32)] * 5,
        compiler_params=_params(("arbitrary",)),
    )(dup0, x1, y, dx2, w_up4, g_pre2, sc_f, sh_f, gt_m, g_post_m)


def _seqmix_bwd(lru_x, hst, stash, dy, w_out, seq_params, ws_t, glo, ggo, x, dx1, w_in4, g_pre, sc_m, ts=256):
    s, d = x.shape
    nt = s // ts
    small_shapes = [(4, 512), (1, 512), (512, 512), (512, 512), (1, 512), (1, 512), (1, 512),
                    (1, 512), (1, 512), (4, 128, 128), (128, 4), (1, 512), (1, 512),
                    (1, d), (1, d), (1, d)]

    def body(lx_ref, hst_ref, hprev_ref, st_ref, dy_ref, wout_ref, cw_ref, cb_ref, bdr_ref, bdi_ref, br_ref,
             bi_ref, la_ref, ng_ref, nb_ref, ws_ref, bst_ref, wst_ref, glo_ref, ggo_ref, x_ref, dx1_ref, win_ref,
             gpre_ref, scm_ref, dz_ref, gx_ref, *rest):
        small_refs = rest[:16]
        (dcw_ref, dcb_ref, dwr_ref, dwi_ref, dbr_ref, dbi_ref, dspa_ref, dng_ref, dnb_ref, dws_ref, dbs_ref,
         dglo_ref, dggo_ref, dsh_ref, dsc_ref, dgpre_ref) = small_refs
        gcarry, anext, dxcnext, dv_scr = rest[16:]
        i = pl.program_id(0)

        @pl.when(i == 0)
        def _():
            for ref in small_refs:
                ref[...] = jnp.zeros_like(ref)
            gcarry[...] = jnp.zeros_like(gcarry)
            anext[...] = jnp.ones_like(anext)
            dxcnext[...] = jnp.zeros_like(dxcnext)

        first_tile = i == nt - 1
        xc, r, ig, a, mult = st_ref[ST_XC], st_ref[ST_R], st_ref[ST_IG], st_ref[ST_A], st_ref[ST_MULT]
        gl, u, spb, vhat = st_ref[ST_GL], st_ref[ST_U], st_ref[ST_SPB], st_ref[ST_VHAT]
        lx = lx_ref[...]
        h = hst_ref[...]
        hprev = _shift_down(h, jnp.where(first_tile, 0.0, hprev_ref[...]), 1)
        y_l = h * gl
        y_g = u * spb

        dycat = _dot_nt(dy_ref[...], wout_ref[...])

        dz_parts = {}

        def emit_dz(k, val):
            dz_parts[k] = val.astype(BF16)
            dz_ref[:, k * 512:(k + 1) * 512] = dz_parts[k]

        rl = _msq_rsqrt(y_l)
        yln = y_l * rl
        dyl = dycat[:, 0:512]
        dglo_ref[...] += _colsum(dyl * yln)
        dy_l = _rms_bwd(dyl * glo_ref[...], yln, rl)
        rg = _msq_rsqrt(y_g)
        ygn = y_g * rg
        dyg = dycat[:, 512:1024]
        dggo_ref[...] += _colsum(dyg * ygn)
        dy_g = _rms_bwd(dyg * ggo_ref[...], ygn, rg)

        emit_dz(1, dy_l * h * st_ref[ST_DGL])
        a_up = _shift_up(a, anext[...], 1)
        acum, gloc = _scan_bwd(a_up, dy_l * gl)
        gg = gloc + acum * gcarry[...]
        gcarry[...] = gg[0:1, :]
        anext[...] = a[0:SUBLANES, :]
        da = gg * hprev
        t1 = gg * mult
        di = t1 * xc
        dxc = t1 * ig
        dmult = gg * ig * xc
        dla = da * a - dmult * (a * a / mult)
        dspa_ref[...] += _colsum(dla * r) * (-LRU_C)
        dpr = dla * ((-LRU_C) * _softplus(-la_ref[...])) * r * (1.0 - r)
        dpi = di * ig * (1.0 - ig)
        dbr_ref[...] += _colsum(dpr)
        dbi_ref[...] += _colsum(dpi)
        dprb = dpr.astype(BF16)
        dpib = dpi.astype(BF16)
        xcb = xc.astype(BF16)
        dwr_ref[...] += _dot_tn(xcb, dprb)
        dwi_ref[...] += _dot_tn(xcb, dpib)
        dxc = dxc + _dot_nt(dprb, bdr_ref[...]) + _dot_nt(dpib, bdi_ref[...])
        nxt = dxcnext[...]
        dxcnext[...] = dxc[0:SUBLANES, :]
        up1, up2, up3 = _shift_up(dxc, nxt, 1), _shift_up(dxc, nxt, 2), _shift_up(dxc, nxt, 3)
        dcb_ref[...] += _colsum(dxc)
        dcw_ref[3:4, :] += _colsum(dxc * lx)
        dcw_ref[2:3, :] += _colsum(up1 * lx)
        dcw_ref[1:2, :] += _colsum(up2 * lx)
        dcw_ref[0:1, :] += _colsum(up3 * lx)
        dlx = cw_ref[3:4, :] * dxc + cw_ref[2:3, :] * up1 + cw_ref[1:2, :] * up2 + cw_ref[0:1, :] * up3
        emit_dz(0, dlx)

        emit_dz(2, dy_g * spb * st_ref[ST_DU])
        dsp = dy_g * u
        vb = (vhat * ng_ref[...] + nb_ref[...]).astype(BF16)
        for n in range(ts // GMLP_BLOCK):
            rs = slice(n * GMLP_BLOCK, (n + 1) * GMLP_BLOCK)
            for g in range(GMLP_GROUPS):
                cs = slice(g * 128, (g + 1) * 128)
                dbs_ref[:, g:g + 1] += jnp.sum(dsp[rs, cs], axis=1, keepdims=True)
                blk = dsp[rs, cs].astype(BF16)
                dws_ref[g] += _dot_nt(blk, vb[rs, cs])
                dv_scr[rs, cs] = _dot(wst_ref[g], blk)
        dv = dv_scr[...]
        dng_ref[...] += _colsum(dv * vhat)
        dnb_ref[...] += _colsum(dv)
        dvh = dv * ng_ref[...]
        dvg = dvh - jnp.mean(dvh, axis=-1, keepdims=True) - vhat * jnp.mean(dvh * vhat, axis=-1, keepdims=True)
        emit_dz(3, dvg * st_ref[ST_Q])

        dh = _dot_nt(dz_parts[0], win_ref[0])
        for k in range(1, N_CHIPS):
            dh = dh + _dot_nt(dz_parts[k], win_ref[k])
        xv = x_ref[...]
        rx = _msq_rsqrt(xv)
        xn = xv * rx
        dsh_ref[...] += _colsum(dh)
        dsc_ref[...] += _colsum(dh * (xn * gpre_ref[...]))
        dhn = dh * (1.0 + scm_ref[...])
        dgpre_ref[...] += _colsum(dhn * xn)
        gx_ref[...] = dx1_ref[...] + _rms_bwd(dhn * gpre_ref[...], xn, rx)

        @pl.when(i == nt - 1)
        def _():
            pos = lax.broadcasted_iota(jnp.int32, (GMLP_BLOCK, GMLP_BLOCK), 0) // CHUNK
            src = lax.broadcasted_iota(jnp.int32, (GMLP_BLOCK, GMLP_BLOCK), 1) // CHUNK
            for g in range(GMLP_GROUPS):
                dws_ref[g] = jnp.where(src <= pos, dws_ref[g], 0.0)
            dspa_ref[...] = dspa_ref[...] * (-_sigmoid(-la_ref[...]))

    vec = _full((1, d))
    in_specs = ([_rows(ts, 512, nt), _rows(ts, 512, nt), _halo_prev(ts, 512, SUBLANES, nt),
                 pl.BlockSpec((N_STASH, ts, 512), lambda i: (0, nt - 1 - i, 0)), _rows(ts, d, nt),
                 _full((d, d))]
                + _seq_param_specs() + [_full((4, 128, 128)), _full((1, 512)), _full((1, 512))]
                + [_rows(ts, d, nt), _rows(ts, d, nt), _full(w_in4.shape), vec, vec])
    return pl.pallas_call(
        body, grid=(nt,), name="seqmix_bwd",
        in_specs=in_specs,
        out_specs=[_rows(ts, 2048, nt), _rows(ts, d, nt)] + [_full(sh) for sh in small_shapes],
        out_shape=[_sds((s, 2048), BF16), _sds((s, d), F32)] + [_sds(sh, F32) for sh in small_shapes],
        scratch_shapes=[pltpu.VMEM((1, 512), F32), pltpu.VMEM((SUBLANES, 512), F32),
                        pltpu.VMEM((SUBLANES, 512), F32), pltpu.VMEM((ts, 512), F32)],
        compiler_params=_params(("arbitrary",)),
    )(lru_x, hst, hst, stash, dy, w_out, *seq_params, ws_t, glo, ggo, x, dx1, w_in4, g_pre, sc_m)


def _wgrad(a, b, n_chunks, name, chunk_major, ts=2048):
    s, m = a.shape
    n = b.shape[1]
    nc = n // n_chunks
    nt = s // ts

    def body(a_ref, b_ref, o_ref, acc):
        i = pl.program_id(1)

        @pl.when(i == 0)
        def _():
            acc[...] = jnp.zeros_like(acc)

        acc[...] += _dot_tn(a_ref[...], b_ref[...])

        @pl.when(i == nt - 1)
        def _():
            if chunk_major:
                o_ref[0] = acc[...].astype(BF16)
            else:
                o_ref[...] = acc[...].astype(BF16)

    if chunk_major:
        out_spec, out_shape = pl.BlockSpec((1, m, nc), lambda c, i: (c, 0, 0)), _sds((n_chunks, m, nc), BF16)
    else:
        out_spec, out_shape = pl.BlockSpec((m, nc), lambda c, i: (0, c)), _sds((m, n), BF16)
    return pl.pallas_call(
        body, grid=(n_chunks, nt), name=name,
        in_specs=[pl.BlockSpec((ts, m), lambda c, i: (i, 0)), pl.BlockSpec((ts, nc), lambda c, i: (i, c))],
        out_specs=out_spec,
        out_shape=out_shape,
        scratch_shapes=[pltpu.VMEM((m, nc), F32)],
        compiler_params=_params(("parallel", "arbitrary")),
    )(a, b)


def _block_diag(w):
    heads, hd, _ = w.shape
    eye = jnp.eye(heads, dtype=w.dtype)
    return (eye[:, None, :, None] * w[:, :, None, :]).reshape(heads * hd, heads * hd)


def _seq_params(small):
    row = lambda v: v.reshape(1, -1)
    pos = jnp.arange(GMLP_BLOCK)
    mask = (pos[None, :] // CHUNK) <= (pos[:, None] // CHUNK)
    ws = jnp.where(mask[None], small["w_spatial"], 0.0)
    seq_params = (small["conv_w"], row(small["conv_b"]),
                  _block_diag(small["w_rgate"]).astype(BF16), _block_diag(small["w_igate"]).astype(BF16),
                  row(small["b_rgate"]), row(small["b_igate"]), row(small["lru_a"]),
                  row(small["v_norm_g"]), row(small["v_norm_b"]), ws.astype(BF16), small["b_spatial"].T)
    return seq_params, jnp.swapaxes(ws, 1, 2).astype(BF16)


_ANY = pl.BlockSpec(memory_space=pl.ANY)
_CHIP_FLIPS = ((1, 0), (0, 1), (1, 1))


def _position():
    return lax.axis_index("x"), lax.axis_index("y"), lax.axis_index("c")


def _flip(v, f):
    return 1 - v if f else v


def _remote(src, dst, send_sem, recv_sem, peer):
    return pltpu.make_async_remote_copy(src_ref=src, dst_ref=dst, send_sem=send_sem, recv_sem=recv_sem,
                                        device_id=peer, device_id_type=MESH)


def _allgather8(block, name):
    r, n = block.shape

    def body(x_ref, gath, send_sems, recv_sems, loc_sem):
        x, y, c = _position()
        me = 4 * x + 2 * y + c
        loc = pltpu.make_async_copy(x_ref, gath.at[me], loc_sem)
        loc.start()
        peers = []
        for k in range(1, N_DEV):
            px, py, pc = _flip(x, k & 4), _flip(y, k & 2), _flip(c, k & 1)
            peers.append((px, py, pc))
            _remote(x_ref, gath.at[me], send_sems.at[k - 1], recv_sems.at[k - 1], (px, py, pc)).start()
        for k, (px, py, pc) in enumerate(peers):
            src = 4 * px + 2 * py + pc
            _remote(x_ref, gath.at[src], send_sems.at[k], recv_sems.at[k], (px, py, pc)).wait_recv()
        for k, peer in enumerate(peers):
            _remote(x_ref, gath.at[me], send_sems.at[k], recv_sems.at[k], peer).wait_send()
        loc.wait()

    return pl.pallas_call(
        body, name=name, out_shape=_sds((N_DEV, r, n), F32),
        in_specs=[pl.BlockSpec(memory_space=pltpu.VMEM)], out_specs=pl.BlockSpec(memory_space=pltpu.VMEM),
        scratch_shapes=[pltpu.SemaphoreType.DMA((N_DEV - 1,)), pltpu.SemaphoreType.DMA((N_DEV - 1,)),
                        pltpu.SemaphoreType.DMA],
        compiler_params=pltpu.CompilerParams(vmem_limit_bytes=VMEM_LIMIT_BYTES),
    )(block)


def _half(ref, c, rows):
    hr = rows // 2
    return ref.at[pl.ds(pl.multiple_of(c * hr, BF16_SUBLANES), hr), :]


def _chip_sum(part, recv, pos_arr, name):
    _, rows, cols = part.shape
    hr = rows // 2

    def body(pos_ref, p_ref, r_ref, o_ref, g_ref):
        total = (p_ref[...].astype(F32) + r_ref[...].astype(F32)).astype(BF16)
        o_ref[...] = total

        @pl.when(pl.program_id(0) == pos_ref[1])
        def _():
            g_ref[0] = total

    grid_spec = pltpu.PrefetchScalarGridSpec(
        num_scalar_prefetch=1, grid=(N_CHIPS,),
        in_specs=[pl.BlockSpec((1, hr, cols), lambda k, pos: (k, pos[0], 0)),
                  pl.BlockSpec((1, hr, cols), lambda k, pos: (k, 0, 0))],
        out_specs=[pl.BlockSpec((1, hr, cols), lambda k, pos: (k, 0, 0)),
                   pl.BlockSpec((1, 1, hr, cols), lambda k, pos: (0, pos[1], 0, 0))])
    return pl.pallas_call(
        body, name=name, grid_spec=grid_spec,
        out_shape=[_sds((N_CHIPS, hr, cols), BF16), _sds((2, N_CHIPS, hr, cols), BF16)],
        compiler_params=_params(("arbitrary",)),
    )(pos_arr, part, recv)


_HBM = pl.BlockSpec(memory_space=pltpu.HBM)
_SEM = pl.BlockSpec(memory_space=pltpu.SEMAPHORE)
_EFFECT = pltpu.SideEffectType.DATAFLOW_SIDE_EFFECTING


def _in_hbm(a):
    return pltpu.with_memory_space_constraint(a, pltpu.HBM)


def _split_start(srcs, lands, plan, n_copies, after, name):
    ns, nl = len(srcs), len(lands)
    bufs = list(srcs) + list(lands)

    def body(*refs):
        send_sems, recv_sems = refs[ns + nl + 1], refs[ns + nl + 2]
        token = refs[-1]
        for k, (src, dst, peer) in enumerate(plan(refs[:ns], refs[ns:ns + nl])):
            _remote(src, dst, send_sems.at[k], recv_sems.at[k], peer).start()
        token[...] = jnp.zeros_like(token)

    out = pl.pallas_call(
        body, name=name,
        out_shape=(pltpu.SemaphoreType.DMA((n_copies,)), pltpu.SemaphoreType.DMA((n_copies,)),
                   *[pltpu.HBM(b.shape, b.dtype) for b in bufs], _sds((SUBLANES, 128), F32)),
        in_specs=[_HBM] * (ns + nl) + [_ANY],
        out_specs=(_SEM, _SEM, *[_HBM] * (ns + nl), pl.BlockSpec(memory_space=pltpu.VMEM)),
        input_output_aliases={i: 2 + i for i in range(ns + nl)},
        compiler_params=pltpu.CompilerParams(has_side_effects=_EFFECT),
    )(*[_in_hbm(b) for b in bufs], after)
    return out[0], out[1], list(out[2:2 + ns]), list(out[2 + ns:2 + ns + nl]), out[-1]


def _split_wait(send_sems, recv_sems, srcs, lands, plan, after, name):
    ns, nl = len(srcs), len(lands)
    bufs = list(srcs) + list(lands)

    def body(*refs):
        send_ref, recv_ref = refs[ns + nl], refs[ns + nl + 1]
        me = _position()
        for k, src, dst in plan(refs[:ns], refs[ns:ns + nl]):
            cp = _remote(src, dst, send_ref.at[k], recv_ref.at[k], me)
            cp.wait_send()
            cp.wait_recv()

    out = pl.pallas_call(
        body, name=name,
        out_shape=[pltpu.HBM(b.shape, b.dtype) for b in bufs],
        in_specs=[_HBM] * (ns + nl) + [_SEM, _SEM, _ANY],
        out_specs=[_HBM] * (ns + nl),
        input_output_aliases={i: i for i in range(ns + nl)},
        compiler_params=pltpu.CompilerParams(has_side_effects=_EFFECT),
    )(*bufs, send_sems, recv_sems, after)
    return list(out[:ns]), list(out[ns:])


def _gather_plan(rows_of):
    def start(src_refs, land_refs):
        x, y, c = _position()
        chip = 2 * x + y
        out = []
        for a, rows in enumerate(rows_of):
            mine = _half(land_refs[a].at[chip], c, rows)
            out.extend((mine, mine, (_flip(x, fx), _flip(y, fy), c)) for fx, fy in _CHIP_FLIPS)
        return out

    def wait(src_refs, land_refs):
        x, y, c = _position()
        chip = 2 * x + y
        out = []
        for a, rows in enumerate(rows_of):
            for j, (fx, fy) in enumerate(_CHIP_FLIPS):
                src_chip = 2 * _flip(x, fx) + _flip(y, fy)
                out.append((3 * a + j, _half(land_refs[a].at[chip], c, rows),
                            _half(land_refs[a].at[src_chip], c, rows)))
        return out

    return start, wait


def _forward_plan(rows_of):
    def pieces(land_refs, half):
        x, y, _ = _position()
        return [_half(land_refs[a].at[2 * _flip(x, fx) + _flip(y, fy)], half, rows)
                for a, rows in enumerate(rows_of) for fx, fy in _CHIP_FLIPS]

    def start(src_refs, land_refs):
        x, y, c = _position()
        return [(p, p, (x, y, 1 - c)) for p in pieces(land_refs, c)]

    def wait(src_refs, land_refs):
        _, _, c = _position()
        return [(k, mine, theirs)
                for k, (mine, theirs) in enumerate(zip(pieces(land_refs, c), pieces(land_refs, 1 - c)))]

    return start, wait


def _swap_halves_plan(half_rows):
    def slices(src_refs, c):
        return [src_refs[a].at[:, pl.ds(pl.multiple_of((1 - c) * hr, BF16_SUBLANES), hr), :]
                for a, hr in enumerate(half_rows)]

    def start(src_refs, land_refs):
        x, y, c = _position()
        return [(src, land_refs[a], (x, y, 1 - c)) for a, src in enumerate(slices(src_refs, c))]

    def wait(src_refs, land_refs):
        _, _, c = _position()
        return [(a, src, land_refs[a]) for a, src in enumerate(slices(src_refs, c))]

    return start, wait


def _swap_gathered_plan(n_arrays):
    def start(src_refs, land_refs):
        x, y, c = _position()
        return [(land_refs[a].at[0], land_refs[a].at[1], (x, y, 1 - c)) for a in range(n_arrays)]

    def wait(src_refs, land_refs):
        return [(a, land_refs[a].at[0], land_refs[a].at[1]) for a in range(n_arrays)]

    return start, wait


def _exchange_plan(n_arrays):
    def start(src_refs, land_refs):
        x, y, c = _position()
        chip = 2 * x + y
        out = []
        for a in range(n_arrays):
            for fx, fy in _CHIP_FLIPS:
                px, py = _flip(x, fx), _flip(y, fy)
                out.append((src_refs[a].at[2 * px + py], land_refs[a].at[0, chip], (px, py, c)))
        return out

    def wait(src_refs, land_refs):
        x, y, c = _position()
        out = []
        for a in range(n_arrays):
            for j, (fx, fy) in enumerate(_CHIP_FLIPS):
                src_chip = 2 * _flip(x, fx) + _flip(y, fy)
                out.append((3 * a + j, src_refs[a].at[src_chip], land_refs[a].at[0, src_chip]))
        return out

    return start, wait


def _forward_to_sibling(lands, name):
    na = len(lands)

    def body(*refs):
        land_refs = refs[na:2 * na]
        send_sems, recv_sems = refs[2 * na:]
        x, y, c = _position()
        sibling = (x, y, 1 - c)
        sends = []
        for a in range(na):
            rows = lands[a].shape[1]
            for j, (fx, fy) in enumerate(_CHIP_FLIPS):
                landed = _half(land_refs[a].at[2 * _flip(x, fx) + _flip(y, fy)], c, rows)
                sends.append(_remote(landed, landed, send_sems.at[3 * a + j], recv_sems.at[3 * a + j], sibling))
                sends[-1].start()
        for a in range(na):
            rows = lands[a].shape[1]
            for j, (fx, fy) in enumerate(_CHIP_FLIPS):
                other = _half(land_refs[a].at[2 * _flip(x, fx) + _flip(y, fy)], 1 - c, rows)
                _remote(other, other, send_sems.at[3 * a + j], recv_sems.at[3 * a + j], sibling).wait_recv()
        for cp in sends:
            cp.wait_send()

    return pl.pallas_call(
        body, name=name,
        out_shape=[_sds(l.shape, l.dtype) for l in lands],
        in_specs=[_ANY] * na, out_specs=[_ANY] * na,
        input_output_aliases={a: a for a in range(na)},
        scratch_shapes=[pltpu.SemaphoreType.DMA((3 * na,))] * 2,
    )(*lands)


def _adam_gathered(w, gath, m, v, c_arr, after, name, tr=128):
    rows, cols = w.shape
    hr = rows // 2
    if hr % (2 * tr) == 0:
        tr = 2 * tr
    per = hr // tr

    def body(c_ref, w_ref, g_ref, m_ref, v_ref, after_ref, go_ref, d_ref, nm_ref, nv_ref):
        g = g_ref[0, 0].astype(F32)
        for k in range(1, N_CHIPS):
            g = g + g_ref[0, k].astype(F32)
        go_ref[...] = g
        d_ref[...], nm_ref[...], nv_ref[...] = _adam_math(w_ref[...], g, m_ref[...], v_ref[...])

    def rows_of(h, i, c_ref):
        c = c_ref[0]
        return ((c + h - 2 * c * h) * per + i, 0)

    blk = pl.BlockSpec((tr, cols), rows_of)
    grid_spec = pltpu.PrefetchScalarGridSpec(
        num_scalar_prefetch=1, grid=(2, per),
        in_specs=[blk, pl.BlockSpec((1, N_CHIPS, tr, cols), lambda h, i, c_ref: (h, 0, i, 0)), blk, blk, _ANY],
        out_specs=[blk] * 4)
    return pl.pallas_call(
        body, name=name, grid_spec=grid_spec, out_shape=[_sds(w.shape, F32)] * 4,
        compiler_params=_params(("arbitrary", "arbitrary")),
    )(c_arr, w, gath, m, v, after)


def _allreduce_small(block, name):
    two, r, n = block.shape
    assert two == 2

    def body(x_ref, out_ref, sib, chipsum, gath, d2d_send, d2d_recv, ici_send, ici_recv):
        x, y, c = _position()
        chip = 2 * x + y
        sibling = (x, y, 1 - c)
        first = _remote(x_ref, sib, d2d_send.at[0], d2d_recv.at[0], sibling)
        first.start()
        first.wait()
        chipsum[...] = x_ref[...] + sib[...]
        sends = []
        for j, (fx, fy) in enumerate(_CHIP_FLIPS):
            sends.append(_remote(chipsum.at[c], gath.at[chip], ici_send.at[j], ici_recv.at[j],
                                 (_flip(x, fx), _flip(y, fy), c)))
            sends[-1].start()
        gath[chip] = chipsum[c]
        for j, (fx, fy) in enumerate(_CHIP_FLIPS):
            landed = gath.at[2 * _flip(x, fx) + _flip(y, fy)]
            _remote(landed, landed, ici_send.at[j], ici_recv.at[j], sibling).wait_recv()
        for cp in sends:
            cp.wait_send()
        total = gath[0]
        for k in range(1, N_CHIPS):
            total = total + gath[k]
        out_ref[c] = total
        last = _remote(out_ref.at[c], out_ref.at[c], d2d_send.at[1], d2d_recv.at[1], sibling)
        last.start()
        _remote(out_ref.at[1 - c], out_ref.at[1 - c], d2d_send.at[1], d2d_recv.at[1], sibling).wait_recv()
        last.wait_send()

    vmem = pl.BlockSpec(memory_space=pltpu.VMEM)
    return pl.pallas_call(
        body, name=name, out_shape=_sds(block.shape, F32), in_specs=[vmem], out_specs=vmem,
        scratch_shapes=[pltpu.VMEM(block.shape, F32), pltpu.VMEM(block.shape, F32), pltpu.VMEM((N_CHIPS, r, n), F32),
                        pltpu.SemaphoreType.DMA((2,)), pltpu.SemaphoreType.DMA((2,)),
                        pltpu.SemaphoreType.DMA((3,)), pltpu.SemaphoreType.DMA((3,))],
        compiler_params=pltpu.CompilerParams(vmem_limit_bytes=VMEM_LIMIT_BYTES),
    )(block)


def _cast_place(shards, chip_arr, name):
    na = len(shards)
    steps = 4

    def body(chip_ref, *refs):
        for a in range(na):
            refs[na + a][0] = refs[a][...].astype(BF16)

    grid_spec = pltpu.PrefetchScalarGridSpec(
        num_scalar_prefetch=1, grid=(steps,),
        in_specs=[pl.BlockSpec((s.shape[0] // steps, s.shape[1]), lambda i, ch: (i, 0)) for s in shards],
        out_specs=[pl.BlockSpec((1, s.shape[0] // steps, s.shape[1]), lambda i, ch: (ch[0], i, 0)) for s in shards])
    return pl.pallas_call(
        body, name=name, grid_spec=grid_spec,
        out_shape=[_sds((N_CHIPS,) + s.shape, BF16) for s in shards],
        compiler_params=_params(("arbitrary",)),
    )(chip_arr, *shards)


def _silu(v):
    return v * _sigmoid(v)


def _ada_fwd(c8, w_ada):
    def body(c_ref, w_ref, o_ref):
        o_ref[...] = jnp.dot(_silu(c_ref[...]), w_ref[...], preferred_element_type=F32,
                             precision=lax.Precision.HIGHEST)

    return pl.pallas_call(
        body, name="ada_fwd", out_shape=_sds((N_DEV, w_ada.shape[1]), F32),
        compiler_params=pltpu.CompilerParams(vmem_limit_bytes=VMEM_LIMIT_BYTES),
    )(c8, w_ada)


def _mod_select(parts, b_ada, me_arr, after):
    cols = parts.shape[2]

    def body(me_ref, p_ref, b_ref, after_ref, o_ref):
        me = me_ref[0]
        for k in range(N_CHIPS):
            cs = slice(k * cols, (k + 1) * cols)
            o_ref[:, cs] = p_ref[2 * k, pl.ds(me, 1), :] + b_ref[:, cs]

    grid_spec = pltpu.PrefetchScalarGridSpec(
        num_scalar_prefetch=1, grid=(1,),
        in_specs=[pl.BlockSpec(parts.shape, lambda i, m: (0, 0, 0)), pl.BlockSpec(b_ada.shape, lambda i, m: (0, 0)),
                  _ANY],
        out_specs=pl.BlockSpec(b_ada.shape, lambda i, m: (0, 0)))
    return pl.pallas_call(body, name="mod_select", grid_spec=grid_spec, out_shape=_sds(b_ada.shape, F32))(
        me_arr, parts, b_ada, after)


def _ada_bwd(c8, dmod8, chip_arr, w, m, v, tr=256):
    d = c8.shape[1]
    cols = dmod8.shape[1] // N_CHIPS

    def body(chip_ref, c_ref, dm_ref, dmall_ref, w_ref, m_ref, v_ref, gw_ref, d_ref, nm_ref, nv_ref, gb_ref):
        g = lax.dot_general(_silu(c_ref[...]), dm_ref[...], (((0,), (0,)), ((), ())),
                            preferred_element_type=F32, precision=lax.Precision.HIGHEST)
        gw_ref[...] = g
        d_ref[...], nm_ref[...], nv_ref[...] = _adam_math(w_ref[...], g, m_ref[...], v_ref[...])
        acc = dmall_ref[0:1, :]
        for k in range(1, N_DEV):
            acc = acc + dmall_ref[k:k + 1, :]
        gb_ref[...] = acc

    rows = pl.BlockSpec((tr, cols), lambda i, ch: (i, 0))
    grid_spec = pltpu.PrefetchScalarGridSpec(
        num_scalar_prefetch=1, grid=(d // tr,),
        in_specs=[pl.BlockSpec((N_DEV, tr), lambda i, ch: (0, i)),
                  pl.BlockSpec((N_DEV, cols), lambda i, ch: (0, ch[0])),
                  pl.BlockSpec(dmod8.shape, lambda i, ch: (0, 0)), rows, rows, rows],
        out_specs=[rows] * 4 + [pl.BlockSpec((1, dmod8.shape[1]), lambda i, ch: (0, 0))])
    return pl.pallas_call(
        body, name="ada_bwd", grid_spec=grid_spec,
        out_shape=[_sds((d, cols), F32)] * 4 + [_sds((1, dmod8.shape[1]), F32)],
        compiler_params=_params(("arbitrary",)),
    )(chip_arr, c8, dmod8, dmod8, w, m, v)


def _adam_math(w, g, m, v):
    m = ADAM_B1 * m + (1.0 - ADAM_B1) * g
    v = ADAM_B2 * v + (1.0 - ADAM_B2) * (g * g)
    m_hat = m / (1.0 - ADAM_B1 ** ADAM_STEP)
    v_hat = v / (1.0 - ADAM_B2 ** ADAM_STEP)
    delta = -ADAM_LR * (m_hat / (jnp.sqrt(v_hat) + ADAM_EPS) + ADAM_WD * w)
    return delta, m, v


def _adam(w, g, m, v, name, tr=256):
    rows, cols = w.shape
    if rows % tr:
        tr = rows

    def body(w_ref, g_ref, m_ref, v_ref, d_ref, nm_ref, nv_ref):
        d_ref[...], nm_ref[...], nv_ref[...] = _adam_math(w_ref[...], g_ref[...], m_ref[...], v_ref[...])

    spec = pl.BlockSpec((tr, cols), lambda i: (i, 0))
    return pl.pallas_call(
        body, name=name, grid=(rows // tr,), in_specs=[spec] * 4, out_specs=[spec] * 3,
        out_shape=[_sds(w.shape, F32)] * 3, compiler_params=_params(("parallel",)),
    )(w, g, m, v)


SMALL_REPLICATED = ("g_mix_pre", "g_mix_post", "conv_b", "w_rgate", "b_rgate", "w_igate", "b_igate", "lru_a",
                    "v_norm_g", "v_norm_b", "w_spatial", "b_spatial", "g_lru_out", "g_gmlp_out", "g_ffn_pre",
                    "g_ffn_post", "ffn_conv_b")
SMALL_COLUMN_SHARDED = ("conv_w", "ffn_conv_w")

SMALL_ROW_LEN = 86016
_SMALL_ROWS = (
    (("ffn_conv_w", 18432), ("conv_w", 2048), ("w_spatial", 65536)),
    (("w_rgate", 32768), ("w_igate", 32768), ("ffn_conv_b", 6144), ("g_mix_pre", 1024), ("g_mix_post", 1024),
     ("g_ffn_pre", 1024), ("g_ffn_post", 1024), ("conv_b", 512), ("b_rgate", 512), ("b_igate", 512),
     ("lru_a", 512), ("v_norm_g", 512), ("v_norm_b", 512), ("b_spatial", 512), ("g_lru_out", 512),
     ("g_gmlp_out", 512), ("loss", 128)),
)


def _small_slots():
    slots = {}
    for row, entries in enumerate(_SMALL_ROWS):
        off = 0
        for name, size in entries:
            slots[name] = (row, off)
            off += size
        assert off <= SMALL_ROW_LEN
    return slots


SMALL_SLOT = _small_slots()
SMALL_LANES = SMALL_ROW_LEN // SUBLANES


def _small_pieces(name, first, count):
    row, off = SMALL_SLOT[name]
    pos, pieces = off + first, []
    while count:
        sub, lane = divmod(pos, SMALL_LANES)
        n = min(count, SMALL_LANES - lane)
        pieces.append((row, sub, lane, n))
        pos, count = pos + n, count - n
    return pieces
ROW_VECTORS = ("ffn_conv_b", "g_mix_pre", "g_mix_post", "g_ffn_pre", "g_ffn_post", "conv_b", "lru_a", "v_norm_g",
               "v_norm_b", "g_lru_out", "g_gmlp_out")
HEAD_DIM = LRU_WIDTH // LRU_HEADS


def _pack_small(g, after):
    order = ("ffn_conv_w", "conv_w", "w_spatial", "w_rgate", "w_igate", "b_rgate", "b_igate", "b_spatial", "loss") \
        + ROW_VECTORS
    vmem = pl.BlockSpec(memory_space=pltpu.VMEM)

    def body(*refs):
        src = dict(zip(order, refs))
        out_ref = refs[len(order) + 1]
        out_ref[...] = jnp.zeros_like(out_ref)

        def put(name, first, val):
            col = 0
            for row, sub, lane, n in _small_pieces(name, first, val.shape[1]):
                out_ref[row, sub:sub + 1, lane:lane + n] = val[:, col:col + n]
                col += n

        for name in ROW_VECTORS + ("b_rgate", "b_igate", "loss"):
            put(name, 0, src[name][...])
        for name in ("ffn_conv_w", "conv_w"):
            k_taps, n = src[name].shape
            for k in range(k_taps):
                put(name, k * n, src[name][k:k + 1, :])
        for g_idx in range(GMLP_GROUPS):
            for i in range(GMLP_BLOCK):
                put("w_spatial", (g_idx * GMLP_BLOCK + i) * GMLP_BLOCK, src["w_spatial"][g_idx, i:i + 1, :])
        for name in ("w_rgate", "w_igate"):
            for h in range(LRU_HEADS):
                for i in range(HEAD_DIM):
                    r = h * HEAD_DIM + i
                    put(name, r * HEAD_DIM, src[name][r:r + 1, h * HEAD_DIM:(h + 1) * HEAD_DIM])
        eye = (lax.broadcasted_iota(jnp.int32, (GMLP_BLOCK, GMLP_BLOCK), 0)
               == lax.broadcasted_iota(jnp.int32, (GMLP_BLOCK, GMLP_BLOCK), 1))
        for g_idx in range(GMLP_GROUPS):
            col = src["b_spatial"][:, g_idx:g_idx + 1]
            put("b_spatial", g_idx * GMLP_BLOCK, _colsum(jnp.where(eye, col, 0.0)))

    return pl.pallas_call(
        body, name="pack_small", out_shape=_sds((2, SUBLANES, SMALL_LANES), F32),
        in_specs=[vmem] * len(order) + [_ANY], out_specs=vmem,
        compiler_params=pltpu.CompilerParams(vmem_limit_bytes=VMEM_LIMIT_BYTES),
    )(*[g[n] for n in order], after)


def _adam_small(g_small, w, m, v):
    vmem = pl.BlockSpec(memory_space=pltpu.VMEM)
    n_p = len(SMALL_REPLICATED)

    def body(g_ref, *refs):
        w_refs, m_refs, v_refs = refs[:n_p], refs[n_p:2 * n_p], refs[2 * n_p:3 * n_p]
        outs = refs[3 * n_p:]
        go, do, mo, vo = outs[:n_p], outs[n_p:2 * n_p], outs[2 * n_p:3 * n_p], outs[3 * n_p:]
        for k, name in enumerate(SMALL_REPLICATED):
            def take(first, count, name=name):
                parts = [g_ref[row, sub:sub + 1, lane:lane + n]
                         for row, sub, lane, n in _small_pieces(name, first, count)]
                return parts[0] if len(parts) == 1 else jnp.concatenate(parts, axis=1)

            shape = w_refs[k].shape
            if name in ROW_VECTORS:
                go[k][...] = take(0, shape[1])
            elif name in ("b_rgate", "b_igate"):
                for h in range(LRU_HEADS):
                    go[k][0, h:h + 1, :] = take(h * HEAD_DIM, HEAD_DIM)
            elif name == "b_spatial":
                for g_idx in range(GMLP_GROUPS):
                    go[k][0, g_idx:g_idx + 1, :] = take(g_idx * GMLP_BLOCK, GMLP_BLOCK)
            elif name == "w_spatial":
                for g_idx in range(GMLP_GROUPS):
                    for i in range(GMLP_BLOCK):
                        go[k][0, g_idx, i:i + 1, :] = take((g_idx * GMLP_BLOCK + i) * GMLP_BLOCK, GMLP_BLOCK)
            else:
                for h in range(LRU_HEADS):
                    for i in range(HEAD_DIM):
                        go[k][0, h, i:i + 1, :] = take((h * HEAD_DIM + i) * HEAD_DIM, HEAD_DIM)
            do[k][...], mo[k][...], vo[k][...] = _adam_math(w_refs[k][...], go[k][...], m_refs[k][...],
                                                             v_refs[k][...])

    names = SMALL_REPLICATED
    out_shape = [_sds(w[n].shape, F32) for n in names] * 4
    res = pl.pallas_call(
        body, name="adam_small", out_shape=out_shape,
        in_specs=[vmem] * (1 + 3 * n_p), out_specs=[vmem] * (4 * n_p),
        compiler_params=pltpu.CompilerParams(vmem_limit_bytes=VMEM_LIMIT_BYTES),
    )(g_small, *[w[n] for n in names], *[m[n] for n in names], *[v[n] for n in names])
    return [dict(zip(names, res[k * n_p:(k + 1) * n_p])) for k in range(4)]


def _adam_cols(name, g_small, w, m, v, chip_arr):
    _, k_taps, n = w.shape
    row, off = SMALL_SLOT[name]
    first = off // n
    per_sub = SMALL_LANES // n

    def body(chip_ref, *refs):
        g_refs = refs[:k_taps]
        w_ref, m_ref, v_ref, go_ref, d_ref, nm_ref, nv_ref = refs[k_taps:]
        for k in range(k_taps):
            tap = (0, slice(k, k + 1), slice(None))
            sub = (first + N_CHIPS * k + chip_ref[0]) // per_sub
            g = g_refs[k][row, pl.ds(sub, 1), :]
            go_ref[tap] = g
            d_ref[tap], nm_ref[tap], nv_ref[tap] = _adam_math(w_ref[tap], g, m_ref[tap], v_ref[tap])

    whole = pl.BlockSpec(w.shape, lambda i, ch: (0, 0, 0))
    taps = [pl.BlockSpec((2, SUBLANES, n),
                         functools.partial(lambda i, ch, k: (0, 0, (first + N_CHIPS * k + ch[0]) % per_sub), k=k))
            for k in range(k_taps)]
    grid_spec = pltpu.PrefetchScalarGridSpec(
        num_scalar_prefetch=1, grid=(1,), in_specs=taps + [whole] * 3, out_specs=[whole] * 4)
    return pl.pallas_call(body, name="adam_" + name, grid_spec=grid_spec, out_shape=[_sds(w.shape, F32)] * 4)(
        chip_arr, *[g_small] * k_taps, w, m, v)


def kernel(x, c, w_ada, b_ada, g_mix_pre, g_mix_post, w_in, conv_w, conv_b, w_rgate, b_rgate, w_igate, b_igate, lru_a, v_norm_g, v_norm_b, w_spatial, b_spatial, g_lru_out, g_gmlp_out, w_out, g_ffn_pre, g_ffn_post, w_up, ffn_conv_w, ffn_conv_b, w_down, loss_target, m_w_ada, m_b_ada, m_g_mix_pre, m_g_mix_post, m_w_in, m_conv_w, m_conv_b, m_w_rgate, m_b_rgate, m_w_igate, m_b_igate, m_lru_a, m_v_norm_g, m_v_norm_b, m_w_spatial, m_b_spatial, m_g_lru_out, m_g_gmlp_out, m_w_out, m_g_ffn_pre, m_g_ffn_post, m_w_up, m_ffn_conv_w, m_ffn_conv_b, m_w_down, v_w_ada, v_b_ada, v_g_mix_pre, v_g_mix_post, v_w_in, v_conv_w, v_conv_b, v_w_rgate, v_b_rgate, v_w_igate, v_b_igate, v_lru_a, v_v_norm_g, v_v_norm_b, v_w_spatial, v_b_spatial, v_g_lru_out, v_g_gmlp_out, v_w_out, v_g_ffn_pre, v_g_ffn_post, v_w_up, v_ffn_conv_w, v_ffn_conv_b, v_w_down):
    args = dict(locals())
    names = ("w_ada", "b_ada", "g_mix_pre", "g_mix_post", "w_in", "conv_w", "conv_b", "w_rgate", "b_rgate",
             "w_igate", "b_igate", "lru_a", "v_norm_g", "v_norm_b", "w_spatial", "b_spatial", "g_lru_out",
             "g_gmlp_out", "w_out", "g_ffn_pre", "g_ffn_post", "w_up", "ffn_conv_w", "ffn_conv_b", "w_down")
    drop = lambda a: a if a.ndim == 2 else a[0]
    w = {n: drop(args[n]) for n in names}
    m = {n: drop(args["m_" + n]) for n in names}
    v = {n: drop(args["v_" + n]) for n in names}
    xi, yi, ci = _position()
    me_arr = jnp.reshape(4 * xi + 2 * yi + ci, (1,)).astype(jnp.int32)
    chip_arr = jnp.reshape(2 * xi + yi, (1,)).astype(jnp.int32)
    c_arr = jnp.reshape(ci, (1,)).astype(jnp.int32)
    pos_arr = jnp.stack([ci, 2 * xi + yi]).astype(jnp.int32)

    big = ("w_in", "w_out", "w_up", "w_down")
    lands_a = _cast_place([w[n] for n in big[:2]], chip_arr, "cast_place_a")
    start_a, wait_a = _gather_plan([w[n].shape[0] for n in big[:2]])
    start_b, wait_b = _gather_plan([w[n].shape[0] for n in big[2:]])

    row0 = jnp.concatenate([c, w["conv_w"].reshape(1, -1), w["ffn_conv_w"].reshape(1, -1)], axis=1)
    g0 = _allgather8(row0, "gather_cond")[:, 0, :]
    send_a, recv_a, _, lands_a, token_a = _split_start([], lands_a, start_a, 6, g0, "gather_start_a")
    lands_b = _cast_place([w[n] for n in big[2:]], chip_arr + token_a[0, 0].astype(jnp.int32), "cast_place_b")
    c8 = g0[:, :D_MODEL]
    per_chip = g0[0::2]
    conv_w_full = per_chip[:, D_MODEL:D_MODEL + 512].reshape(N_CHIPS, 4, 128).transpose(1, 0, 2).reshape(4, 512)
    ffn_conv_w_full = per_chip[:, D_MODEL + 512:].reshape(N_CHIPS, 3, 1536).transpose(1, 0, 2).reshape(3, 2 * D_FF)
    mod_parts = _allgather8(_ada_fwd(c8 + token_a[0:1, 0:1], w["w_ada"]), "gather_mod")
    send_b, recv_b, _, lands_b, token_b = _split_start([], lands_b, start_b, 6, mod_parts, "gather_start_b")
    mod = _mod_select(mod_parts, w["b_ada"].reshape(1, -1), me_arr, token_b).reshape(N_MOD, D_MODEL)
    sh_m, sc_m, gt_m, sh_f, sc_f, gt_f = [mod[k:k + 1] for k in range(N_MOD)]

    small = {n: w[n] for n in SMALL_REPLICATED}
    small["conv_w"] = conv_w_full
    small["ffn_conv_w"] = ffn_conv_w_full
    row = lambda a: a.reshape(1, -1)
    seq_params, ws_t = _seq_params(small)
    glo, ggo = row(small["g_lru_out"]), row(small["g_gmlp_out"])
    g_pre, g_post = row(small["g_mix_pre"]), row(small["g_mix_post"])
    g_pre2, g_post2 = row(small["g_ffn_pre"]), row(small["g_ffn_post"])
    fw, fb = small["ffn_conv_w"], row(small["ffn_conv_b"])
    xs, tgt = x[0], loss_target[0]

    _, lands_a = _split_wait(send_a, recv_a, [], lands_a, wait_a, mod, "gather_wait_a")
    w_in4, w_out4 = _forward_to_sibling(lands_a, "forward_a")
    w_out_b = w_out4.reshape(D_MODEL, D_MODEL)
    h, lx, ycat, hst, stash = _seqmix(xs, sc_m, sh_m, g_pre, w_in4, seq_params, glo, ggo)
    _, lands_b = _split_wait(send_b, recv_b, [], lands_b, wait_b, ycat, "gather_wait_b")
    fwd_start, fwd_wait = _forward_plan([w[n].shape[0] for n in big[2:]])
    fwd_send, fwd_recv, _, lands_b, tok = _split_start([], lands_b, fwd_start, 6, pos_arr, "forward_start_b")
    y, x1, h2 = _mix_out(ycat, xs, w_out_b, gt_m + tok[0:1, 0:1], g_post, g_pre2, sc_f, sh_f)
    _, (w_up4, w_down4) = _split_wait(fwd_send, fwd_recv, [], lands_b, fwd_wait, h2, "forward_wait_b")
    w_down_b = w_down4.reshape(D_FF, D_MODEL)
    up0, pre, act, dy2, dx2, loss, dgt_f, dg_post2 = _ffn_fwd(h2, x1, tgt, w_up4, w_down_b, fw, fb, gt_f, g_post2)

    dup0, dfw, dfb = _ffn_bwd_a(dy2, pre, up0, w_down_b, fw)
    gw_up = _wgrad(h2, dup0, N_CHIPS, "wgrad_up", True)
    gw_down = _wgrad(act, dy2, 2, "wgrad_down", False)
    ex_start, ex_wait = _exchange_plan(2)
    sg_start, sg_wait = _swap_gathered_plan(2)
    grads, deltas, new_m, new_v = {}, {}, {}, {}

    def swap_start(parts, name):
        sw_start, sw_wait = _swap_halves_plan([p.shape[1] // 2 for p in parts])
        recv = [lax.empty((N_CHIPS, p.shape[1] // 2, p.shape[2]), BF16) for p in parts]
        send_s, recv_s, parts, recv, token = _split_start(parts, recv, sw_start, len(parts), pos_arr,
                                                           "swap_start_" + name)
        return (send_s, recv_s, parts, recv, sw_wait), token

    def exchange_start(swap, tags, after, name):
        send_s, recv_s, parts, recv, sw_wait = swap
        parts, recv = _split_wait(send_s, recv_s, parts, recv, sw_wait, after, "swap_wait_" + name)
        both = [_chip_sum(p, r, pos_arr, "chip_sum_" + t) for p, r, t in zip(parts, recv, tags)]
        sums, gath = [b[0] for b in both], [b[1] for b in both]
        return _split_start(sums, gath, ex_start, 3 * len(parts), pos_arr, "exchange_start_" + name)

    def gathered_start(exchange, after, name):
        send_s, recv_s, sums, gath, _ = exchange
        _, gath = _split_wait(send_s, recv_s, sums, gath, ex_wait, after, "exchange_wait_" + name)
        send_s, recv_s, _, gath, token = _split_start([], gath, sg_start, len(gath), pos_arr,
                                                      "gathered_start_" + name)
        return (send_s, recv_s, gath), token

    def gathered_wait(gathered, after, name):
        send_s, recv_s, gath = gathered
        return _split_wait(send_s, recv_s, [], gath, sg_wait, after, "gathered_wait_" + name)[1]

    def adam_big(t, gath, after):
        grads[t], deltas[t], new_m[t], new_v[t] = _adam_gathered(w[t], gath, m[t], v[t], c_arr, after, "adam_" + t)

    def behind(value, token):
        return value + token[0:1, 0:1]

    tags_b, tags_a = ("w_up", "w_down"), ("w_in", "w_out")
    swap_b, tok = swap_start([gw_up, gw_down.reshape(N_CHIPS, -1, D_MODEL)], "b")
    dx1, dy, dsh_f, dsc_f, dg_pre2, dgt_m, dg_post = _ffn_bwd_b(
        dup0, x1, y, dx2, w_up4, g_pre2, behind(sc_f, tok), sh_f, gt_m, g_post)
    exchange_b = exchange_start(swap_b, tags_b, dg_post, "b")
    (dz, grad_x, dcw, dcb, dwr, dwi, dbr, dbi, dspa, dng, dnb, dws, dbs_t, dglo, dggo, dsh_m, dsc_m,
     dg_pre) = _seqmix_bwd(lx, hst, stash, dy, w_out_b, seq_params, ws_t, behind(glo, exchange_b[4]), ggo,
                           xs, dx1, w_in4, g_pre, sc_m)
    gw_in = _wgrad(h, dz, N_CHIPS, "wgrad_in", True)
    gw_out = _wgrad(ycat, dy, 1, "wgrad_out", False)
    swap_a, tok = swap_start([gw_in, gw_out.reshape(N_CHIPS, -1, D_MODEL)], "a")

    dmod = jnp.concatenate([behind(dsh_m, tok), dsc_m, dgt_m, dsh_f, dsc_f, dgt_f], axis=1)
    dmod8 = _allgather8(dmod, "gather_dmod")[:, 0, :]
    small_grads = dict(
        g_mix_pre=dg_pre, g_mix_post=dg_post, conv_w=dcw, conv_b=dcb, w_rgate=dwr, b_rgate=dbr, w_igate=dwi,
        b_igate=dbi, lru_a=dspa, v_norm_g=dng, v_norm_b=dnb, w_spatial=dws, b_spatial=dbs_t, g_lru_out=dglo,
        g_gmlp_out=dggo, g_ffn_pre=dg_pre2, g_ffn_post=dg_post2, ffn_conv_w=dfw, ffn_conv_b=dfb,
        loss=loss)
    g_small = _allreduce_small(_pack_small(small_grads, dmod8), "reduce_small")
    total = g_small[_small_pieces("loss", 0, 1)[0][:3]]
    exchange_a = exchange_start(swap_a, tags_a, g_small, "a")
    gathered_b, tok = gathered_start(exchange_b, exchange_a[4], "b")

    grads["w_ada"], deltas["w_ada"], new_m["w_ada"], new_v["w_ada"], g_b_ada = _ada_bwd(
        c8, behind(dmod8, tok), chip_arr, w["w_ada"], m["w_ada"], v["w_ada"])
    rep = SMALL_REPLICATED
    small_out = _adam_small(g_small, {n: args[n] for n in rep}, {n: args["m_" + n] for n in rep},
                            {n: args["v_" + n] for n in rep})
    for n in rep:
        grads[n], deltas[n], new_m[n], new_v[n] = [group[n] for group in small_out]
    for n in SMALL_COLUMN_SHARDED:
        grads[n], deltas[n], new_m[n], new_v[n] = _adam_cols(n, g_small, args[n], args["m_" + n],
                                                             args["v_" + n], chip_arr)
    d_b, m_b, v_b = _adam(w["b_ada"], g_b_ada, m["b_ada"], v["b_ada"], "adam_b_ada")
    grads["b_ada"], deltas["b_ada"], new_m["b_ada"], new_v["b_ada"] = g_b_ada, d_b, m_b, v_b

    gath_up, gath_down = gathered_wait(gathered_b, d_b, "b")
    adam_big("w_down", gath_down, pos_arr)
    gathered_a, tok = gathered_start(exchange_a, deltas["w_down"], "a")
    adam_big("w_up", gath_up, tok)
    gath_in, gath_out = gathered_wait(gathered_a, deltas["w_up"], "a")
    adam_big("w_in", gath_in, pos_arr)
    adam_big("w_out", gath_out, pos_arr)

    outs = [total, grad_x[None]]
    for group in (grads, deltas, new_m, new_v):
        outs.extend(group[n].reshape(args[n].shape) for n in names)
    return tuple(outs)
```

```python
import functools
import math

import jax
import jax.numpy as jnp
from jax import lax
from jax.experimental import pallas as pl
from jax.experimental.pallas import tpu as pltpu

F32 = jnp.float32
BF16 = jnp.bfloat16
MESH = pl.DeviceIdType.MESH

D_MODEL = 1024
LRU_WIDTH = 512
LRU_HEADS = 8
GMLP_GROUPS = 4
GMLP_BLOCK = 128
CHUNK = 64
D_FF = 3072
N_MOD = 6
EPS = 1e-6
LRU_C = 8.0
N_CHIPS = 4
N_DEV = 8

ADAM_LR = 0.001
ADAM_B1 = 0.9
ADAM_B2 = 0.999
ADAM_EPS = 1e-08
ADAM_WD = 0.01
ADAM_STEP = 10

GELU_C0 = math.sqrt(2.0 / math.pi)
GELU_C1 = 0.044715

VMEM_LIMIT_BYTES = 56 * 1024 * 1024
SUBLANES = 8
BF16_SUBLANES = 16
FFN_CHUNK = 768
SUB_ROWS = 256


def _gelu_gate(x):
    x2 = x * x
    z = x * ((2.0 * GELU_C0 * GELU_C1) * x2 + 2.0 * GELU_C0)
    return 1.0 / (1.0 + jnp.exp(-z)), x2


def _gelu(x):
    t = jnp.tanh(GELU_C0 * (x + GELU_C1 * x * x * x))
    return 0.5 * x * (1.0 + t)


def _gelu_and_grad(x):
    s, x2 = _gelu_gate(x)
    g = x * s
    dz = (6.0 * GELU_C0 * GELU_C1) * x2 + 2.0 * GELU_C0
    return g, s + g * (1.0 - s) * dz


def _sigmoid(x):
    return 1.0 / (1.0 + jnp.exp(-x))


def _log1p(u):
    w = 1.0 + u
    return jnp.where(w == 1.0, u, jnp.log(w) * (u / (w - 1.0)))


def _softplus(x):
    return jnp.maximum(x, 0.0) + _log1p(jnp.exp(-jnp.abs(x)))


def _neg_expm1(x):
    u = jnp.exp(x)
    um1 = u - 1.0
    tiny = um1 == 0.0
    small = um1 * (x / jnp.log(jnp.where(tiny, 2.0, jnp.maximum(u, 0.25))))
    return -jnp.where(tiny, x, jnp.where(x < -1.0, um1, small))


def _msq_rsqrt(v):
    return lax.rsqrt(jnp.mean(v * v, axis=-1, keepdims=True) + EPS)


def _rms_bwd(dyn, yn, r):
    return r * (dyn - yn * jnp.mean(dyn * yn, axis=-1, keepdims=True))


def _colsum(v):
    return jnp.sum(v, axis=0, keepdims=True)


def _shift_down(cur, prev8, k):
    rolled = pltpu.roll(cur, k, 0)
    head = pltpu.roll(prev8, k, 0)
    row8 = lax.broadcasted_iota(jnp.int32, (SUBLANES, cur.shape[1]), 0)
    first = jnp.where(row8 < k, head, rolled[0:SUBLANES])
    return jnp.concatenate([first, rolled[SUBLANES:]], axis=0)


def _shift_up(cur, next8, k):
    t = cur.shape[0]
    rolled = pltpu.roll(cur, t - k, 0)
    tail = pltpu.roll(next8, SUBLANES - k, 0)
    row8 = lax.broadcasted_iota(jnp.int32, (SUBLANES, cur.shape[1]), 0)
    last = jnp.where(row8 >= SUBLANES - k, tail, rolled[t - SUBLANES:])
    return jnp.concatenate([rolled[:t - SUBLANES], last], axis=0)


def _scan_fwd(a, b):
    t = a.shape[0]
    row = lax.broadcasted_iota(jnp.int32, a.shape, 0)
    d = 1
    while d < t:
        keep = row >= d
        a_s = jnp.where(keep, pltpu.roll(a, d, 0), 1.0)
        b_s = jnp.where(keep, pltpu.roll(b, d, 0), 0.0)
        b = a * b_s + b
        a = a * a_s
        d *= 2
    return a, b


def _scan_bwd(a, g):
    t = a.shape[0]
    row = lax.broadcasted_iota(jnp.int32, a.shape, 0)
    d = 1
    while d < t:
        keep = row < t - d
        a_s = jnp.where(keep, pltpu.roll(a, t - d, 0), 1.0)
        g_s = jnp.where(keep, pltpu.roll(g, t - d, 0), 0.0)
        g = a * g_s + g
        a = a * a_s
        d *= 2
    return a, g


def _dot(a, b):
    return jnp.dot(a, b, preferred_element_type=F32)


def _dot_nt(a, b):
    return lax.dot_general(a, b, (((1,), (1,)), ((), ())), preferred_element_type=F32)


def _dot_tn(a, b):
    return lax.dot_general(a, b, (((0,), (0,)), ((), ())), preferred_element_type=F32)


def _rows(ts, cols, rev_of=None):
    if rev_of is None:
        return pl.BlockSpec((ts, cols), lambda i: (i, 0))
    return pl.BlockSpec((ts, cols), lambda i: (rev_of - 1 - i, 0))


def _halo_prev(ts, cols, halo, rev_of=None, col_block=0):
    per = ts // halo
    if rev_of is None:
        return pl.BlockSpec((halo, cols), lambda i: (jnp.maximum(i * per - 1, 0), col_block))
    return pl.BlockSpec((halo, cols), lambda i: (jnp.maximum((rev_of - 1 - i) * per - 1, 0), col_block))


def _full(shape):
    nd = len(shape)
    return pl.BlockSpec(shape, lambda *_: (0,) * nd)


_RESIDENT = pl.BlockSpec(memory_space=pltpu.VMEM)


def _params(sem):
    return pltpu.CompilerParams(dimension_semantics=sem, vmem_limit_bytes=VMEM_LIMIT_BYTES)


def _sds(shape, dtype):
    return jax.ShapeDtypeStruct(shape, dtype)


def _sub_tiles(ts):
    return [slice(r0, r0 + SUB_ROWS) for r0 in range(0, ts, SUB_ROWS)]


N_STASH = 12
(ST_XC, ST_R, ST_IG, ST_A, ST_MULT, ST_GL, ST_DGL, ST_U, ST_DU, ST_Q, ST_VHAT, ST_SPB) = range(N_STASH)


def _seq_param_specs():
    return [_full((4, 512)), _full((1, 512)), _full((512, 512)), _full((512, 512)), _full((1, 512)),
            _full((1, 512)), _full((1, 512)), _full((1, 512)), _full((1, 512)), _full((4, 128, 128)),
            _full((128, 4))]


def _seqmix(x, sc, sh, g_pre, w_in4, seq_params, glo, ggo, ts=256):
    s, d = x.shape
    nt = s // ts

    def body(x_ref, sc_ref, sh_ref, gpre_ref, win_ref, cw_ref, cb_ref, bdr_ref, bdi_ref, br_ref, bi_ref, la_ref,
             ng_ref, nb_ref, ws_ref, bst_ref, glo_ref, ggo_ref, hin_ref, lx_ref, ycat_ref, hst_ref, st_ref,
             hcarry, lxprev, sp_scr):
        i = pl.program_id(0)

        @pl.when(i == 0)
        def _():
            hcarry[...] = jnp.zeros_like(hcarry)
            lxprev[...] = jnp.zeros_like(lxprev)

        xv = x_ref[...]
        hin = ((xv * _msq_rsqrt(xv) * gpre_ref[...]) * (1.0 + sc_ref[...]) + sh_ref[...]).astype(BF16)
        hin_ref[...] = hin
        z = [_dot(hin, win_ref[k]) for k in range(N_CHIPS)]

        lx = z[0]
        lx_ref[...] = lx
        prev8 = lxprev[...]
        lxprev[...] = lx[ts - SUBLANES:, :]
        xc = (cw_ref[3:4, :] * lx + cw_ref[2:3, :] * _shift_down(lx, prev8, 1)
              + cw_ref[1:2, :] * _shift_down(lx, prev8, 2) + cw_ref[0:1, :] * _shift_down(lx, prev8, 3)
              + cb_ref[...])
        xcb = xc.astype(BF16)
        r = _sigmoid(_dot(xcb, bdr_ref[...]) + br_ref[...])
        ig = _sigmoid(_dot(xcb, bdi_ref[...]) + bi_ref[...])
        log_a = (-LRU_C) * r * _softplus(-la_ref[...])
        a = jnp.exp(log_a)
        mult = jnp.sqrt(_neg_expm1(2.0 * log_a))
        acum, hloc = _scan_fwd(a, mult * (ig * xc))
        h = hloc + acum * hcarry[...]
        hcarry[...] = h[ts - 1:ts, :]
        hst_ref[...] = h
        gl, dgl = _gelu_and_grad(z[1])
        y_l = h * gl
        for slot, val in ((ST_XC, xc), (ST_R, r), (ST_IG, ig), (ST_A, a), (ST_MULT, mult), (ST_GL, gl),
                          (ST_DGL, dgl)):
            st_ref[slot] = val

        u, du = _gelu_and_grad(z[2])
        vg, dvg = _gelu_and_grad(z[3])
        vc = vg - jnp.mean(vg, axis=-1, keepdims=True)
        rstd = lax.rsqrt(jnp.mean(vc * vc, axis=-1, keepdims=True) + EPS)
        vhat = vc * rstd
        vb = (vhat * ng_ref[...] + nb_ref[...]).astype(BF16)
        for n in range(ts // GMLP_BLOCK):
            rs = slice(n * GMLP_BLOCK, (n + 1) * GMLP_BLOCK)
            for g in range(GMLP_GROUPS):
                cs = slice(g * 128, (g + 1) * 128)
                sp_scr[rs, cs] = _dot(ws_ref[g], vb[rs, cs]) + bst_ref[:, g:g + 1]
        spb = sp_scr[...]
        y_g = u * spb
        for slot, val in ((ST_U, u), (ST_DU, du), (ST_Q, rstd * dvg), (ST_VHAT, vhat), (ST_SPB, spb)):
            st_ref[slot] = val

        ycat_ref[:, 0:512] = (y_l * _msq_rsqrt(y_l) * glo_ref[...]).astype(BF16)
        ycat_ref[:, 512:1024] = (y_g * _msq_rsqrt(y_g) * ggo_ref[...]).astype(BF16)

    vec = _full((1, d))
    return pl.pallas_call(
        body, grid=(nt,), name="seqmix",
        in_specs=[_rows(ts, d), vec, vec, vec, _full(w_in4.shape)] + _seq_param_specs()
        + [_full((1, 512)), _full((1, 512))],
        out_specs=[_rows(ts, d), _rows(ts, 512), _rows(ts, d), _rows(ts, 512),
                   pl.BlockSpec((N_STASH, ts, 512), lambda i: (0, i, 0))],
        out_shape=[_sds((s, d), BF16), _sds((s, 512), F32), _sds((s, d), BF16), _sds((s, 512), F32),
                   _sds((N_STASH, s, 512), F32)],
        scratch_shapes=[pltpu.VMEM((1, 512), F32), pltpu.VMEM((SUBLANES, 512), F32), pltpu.VMEM((ts, 512), F32)],
        compiler_params=_params(("arbitrary",)),
    )(x, sc, sh, g_pre, w_in4, *seq_params, glo, ggo)


def _mix_out(ycat, x, w_out, gt_m, g_post, g_pre2, sc_f, sh_f, ts=512):
    s, d = x.shape

    def body(yc_ref, x_ref, w_ref, gt_ref, gp_ref, g2_ref, sc_ref, sh_ref, y_ref, x1_ref, h2_ref):
        for rs in _sub_tiles(ts):
            y = _dot(yc_ref[rs, :], w_ref[...])
            y_ref[rs, :] = y
            x1 = x_ref[rs, :] + gt_ref[...] * (y * _msq_rsqrt(y) * gp_ref[...])
            x1_ref[rs, :] = x1
            h2 = (x1 * _msq_rsqrt(x1) * g2_ref[...]) * (1.0 + sc_ref[...]) + sh_ref[...]
            h2_ref[rs, :] = h2.astype(BF16)

    vec = _full((1, d))
    return pl.pallas_call(
        body, grid=(s // ts,), name="mix_out",
        in_specs=[_rows(ts, d), _rows(ts, d), _full((d, d)), vec, vec, vec, vec, vec],
        out_specs=[_rows(ts, d), _rows(ts, d), _rows(ts, d)],
        out_shape=[_sds((s, d), F32), _sds((s, d), F32), _sds((s, d), BF16)],
        compiler_params=_params(("parallel",)),
    )(ycat, x, w_out, gt_m, g_post, g_pre2, sc_f, sh_f)


def _ffn_cols(j):
    per = (2 * D_FF // N_CHIPS) // FFN_CHUNK
    return j // per, (j % per) * FFN_CHUNK, j * FFN_CHUNK


def _ffn_fwd(h2, x1, tgt, w_up4, w_down, fw, fb, gt_f, g_post, ts=256):
    s, d = x1.shape
    nch = D_FF // FFN_CHUNK

    def body(h2_ref, x1_ref, tgt_ref, wup_ref, wdn_ref, fw_ref, fb_ref, gt_ref, gp_ref,
             up0_ref, pre_ref, act_ref, dy2_ref, dx2_ref, loss_ref, dgt_ref, dgp_ref, tail_ref):
        i = pl.program_id(0)

        @pl.when(i == 0)
        def _():
            tail_ref[...] = jnp.zeros_like(tail_ref)
            loss_ref[...] = jnp.zeros_like(loss_ref)
            dgt_ref[...] = jnp.zeros_like(dgt_ref)
            dgp_ref[...] = jnp.zeros_like(dgp_ref)

        hb = h2_ref[...]

        def up_project(j):
            sh_g, off, _ = _ffn_cols(j)
            return [_dot(hb, wup_ref[shard, :, off:off + FFN_CHUNK]) for shard in (sh_g, sh_g + 2)]

        y2 = jnp.zeros((ts, d), F32)
        ahead = up_project(0)
        for j in range(nch):
            _, _, col = _ffn_cols(j)
            ubs = ahead
            if j + 1 < nch:
                ahead = up_project(j + 1)
            halves = []
            for u, c0 in zip(ubs, (col, D_FF + col)):
                cs = slice(c0, c0 + FFN_CHUNK)
                up0_ref[:, cs] = u.astype(BF16)
                prev8 = tail_ref[:, cs]
                tail_ref[:, cs] = u[ts - SUBLANES:, :]
                halves.append(fw_ref[2:3, cs] * u + fw_ref[1:2, cs] * _shift_down(u, prev8, 1)
                              + fw_ref[0:1, cs] * _shift_down(u, prev8, 2) + fb_ref[:, cs])
                pre_ref[:, cs] = halves[-1].astype(BF16)
            act = (_gelu(halves[0]) * halves[1]).astype(BF16)
            act_ref[:, col:col + FFN_CHUNK] = act
            y2 = y2 + _dot(act, wdn_ref[col:col + FFN_CHUNK, :])
        r2 = _msq_rsqrt(y2)
        yn = y2 * r2
        yng = yn * gp_ref[...]
        e = x1_ref[...] + gt_ref[...] * yng - tgt_ref[...]
        loss_ref[...] += jnp.sum(e * e) * (0.5 / d)
        dx2 = e * (1.0 / d)
        dx2_ref[...] = dx2
        dgt_ref[...] += _colsum(dx2 * yng)
        dyng = dx2 * gt_ref[...]
        dgp_ref[...] += _colsum(dyng * yn)
        dy2_ref[...] = _rms_bwd(dyng * gp_ref[...], yn, r2).astype(BF16)

    vec = _full((1, d))
    return pl.pallas_call(
        body, grid=(s // ts,), name="ffn_fwd",
        in_specs=[_rows(ts, d), _rows(ts, d), _rows(ts, d), _RESIDENT, _RESIDENT,
                  _full((3, 2 * D_FF)), _full((1, 2 * D_FF)), vec, vec],
        out_specs=[_rows(ts, 2 * D_FF), _rows(ts, 2 * D_FF), _rows(ts, D_FF), _rows(ts, d), _rows(ts, d),
                   _full((1, 128)), vec, vec],
        out_shape=[_sds((s, 2 * D_FF), BF16), _sds((s, 2 * D_FF), BF16), _sds((s, D_FF), BF16), _sds((s, d), BF16),
                   _sds((s, d), F32), _sds((1, 128), F32), _sds((1, d), F32), _sds((1, d), F32)],
        scratch_shapes=[pltpu.VMEM((SUBLANES, 2 * D_FF), F32)],
        compiler_params=_params(("arbitrary",)),
    )(h2, x1, tgt, w_up4, w_down, fw, fb, gt_f, g_post)


def _shift_up_mxu(vb, up_mat, next8, k):
    t = vb.shape[0]
    main = _dot(up_mat, vb)
    tail = pltpu.roll(next8, SUBLANES - k, 0)
    row8 = lax.broadcasted_iota(jnp.int32, next8.shape, 0)
    last = main[t - SUBLANES:] + jnp.where(row8 >= SUBLANES - k, tail, 0.0)
    return jnp.concatenate([main[:t - SUBLANES], last], axis=0)


def _ffn_bwd_a(dy2, pre, up0, w_down, fw, ts=256):
    s, d = dy2.shape
    nt = s // ts
    nch = D_FF // FFN_CHUNK
    wide = 2 * D_FF
    up_mats = jnp.stack([jnp.eye(ts, k=1, dtype=BF16), jnp.eye(ts, k=2, dtype=BF16)])

    def body(dy2_ref, pre_ref, up0_ref, wdn_ref, fw_ref, um_ref, dup0_ref, dfw_ref, dfb_ref, next_ref):
        i = pl.program_id(0)

        @pl.when(i == 0)
        def _():
            next_ref[...] = jnp.zeros_like(next_ref)
            dfw_ref[...] = jnp.zeros_like(dfw_ref)
            dfb_ref[...] = jnp.zeros_like(dfb_ref)

        dyb = dy2_ref[...]
        for j in range(nch):
            _, _, col = _ffn_cols(j)
            dact = _dot_nt(dyb, wdn_ref[col:col + FFN_CHUNK, :])
            gl, dgl = _gelu_and_grad(pre_ref[:, col:col + FFN_CHUNK].astype(F32))
            dpre = (dact * pre_ref[:, D_FF + col:D_FF + col + FFN_CHUNK].astype(F32) * dgl, dact * gl)
            for half, c0 in enumerate((col, D_FF + col)):
                cs = slice(c0, c0 + FFN_CHUNK)
                dp = dpre[half]
                dpb = dp.astype(BF16)
                nxt = next_ref[:, cs]
                next_ref[:, cs] = dpb.astype(F32)[0:SUBLANES, :]
                su1 = _shift_up_mxu(dpb, um_ref[0], nxt, 1)
                su2 = _shift_up_mxu(dpb, um_ref[1], nxt, 2)
                u = up0_ref[:, cs].astype(F32)
                dfb_ref[:, cs] += _colsum(dp)
                dfw_ref[2:3, cs] += _colsum(dp * u)
                dfw_ref[1:2, cs] += _colsum(su1 * u)
                dfw_ref[0:1, cs] += _colsum(su2 * u)
                dup0 = fw_ref[2:3, cs] * dp + fw_ref[1:2, cs] * su1 + fw_ref[0:1, cs] * su2
                dup0_ref[:, cs] = dup0.astype(BF16)

    return pl.pallas_call(
        body, grid=(nt,), name="ffn_bwd_a",
        in_specs=[_rows(ts, d, nt), _rows(ts, wide, nt), _rows(ts, wide, nt), _RESIDENT,
                  _full((3, wide)), _full((2, ts, ts))],
        out_specs=[_rows(ts, wide, nt), _full((3, wide)), _full((1, wide))],
        out_shape=[_sds((s, wide), BF16), _sds((3, wide), F32), _sds((1, wide), F32)],
        scratch_shapes=[pltpu.VMEM((SUBLANES, wide), F32)],
        compiler_params=_params(("arbitrary",)),
    )(dy2, pre, up0, w_down, fw, up_mats)


def _ffn_bwd_b(dup0, x1, y, dx2, w_up4, g_pre2, sc_f, sh_f, gt_m, g_post_m, ts=512):
    s, d = x1.shape
    shard_cols = 2 * D_FF // N_CHIPS

    def body(dup_ref, x1_ref, y_ref, dx2_ref, wup_ref, g2_ref, sc_ref, sh_ref, gt_ref, gp_ref,
             dx1_ref, dy_ref, dsh_ref, dsc_ref, dg2_ref, dgt_ref, dgp_ref):
        i = pl.program_id(0)

        @pl.when(i == 0)
        def _():
            for ref in (dsh_ref, dsc_ref, dg2_ref, dgt_ref, dgp_ref):
                ref[...] = jnp.zeros_like(ref)

        for rs in _sub_tiles(ts):
            dh2 = jnp.zeros((SUB_ROWS, d), F32)
            for k in range(N_CHIPS):
                dh2 = dh2 + _dot_nt(dup_ref[rs, k * shard_cols:(k + 1) * shard_cols], wup_ref[k])
            x1v = x1_ref[rs, :]
            r2 = _msq_rsqrt(x1v)
            xn = x1v * r2
            hn = xn * g2_ref[...]
            dsh_ref[...] += _colsum(dh2)
            dsc_ref[...] += _colsum(dh2 * hn)
            dhn = dh2 * (1.0 + sc_ref[...])
            dg2_ref[...] += _colsum(dhn * xn)
            dx1 = dx2_ref[rs, :] + _rms_bwd(dhn * g2_ref[...], xn, r2)
            dx1_ref[rs, :] = dx1
            yv = y_ref[rs, :]
            ry = _msq_rsqrt(yv)
            yn = yv * ry
            dgt_ref[...] += _colsum(dx1 * (yn * gp_ref[...]))
            dyng = dx1 * gt_ref[...]
            dgp_ref[...] += _colsum(dyng * yn)
            dy_ref[rs, :] = _rms_bwd(dyng * gp_ref[...], yn, ry).astype(BF16)

    vec = _full((1, d))
    return pl.pallas_call(
        body, grid=(s // ts,), name="ffn_bwd_b",
        in_specs=[_rows(ts, 2 * D_FF), _rows(ts, d), _rows(ts, d), _rows(ts, d), _RESIDENT,
                  vec, vec, vec, vec, vec],
        out_specs=[_rows(ts, d), _rows(ts, d), vec, vec, vec, vec, vec],
        out_shape=[_sds((s, d), F32), _sds((s, d), BF16)] + [_sds((1, d), F32)] * 5,
        compiler_params=_params(("arbitrary",)),
    )(dup0, x1, y, dx2, w_up4, g_pre2, sc_f, sh_f, gt_m, g_post_m)


def _seqmix_bwd(lru_x, hst, stash, dy, w_out, seq_params, ws_t, glo, ggo, x, dx1, w_in4, g_pre, sc_m, ts=256):
    s, d = x.shape
    nt = s // ts
    small_shapes = [(4, 512), (1, 512), (512, 512), (512, 512), (1, 512), (1, 512), (1, 512),
                    (1, 512), (1, 512), (4, 128, 128), (128, 4), (1, 512), (1, 512),
                    (1, d), (1, d), (1, d)]

    def body(lx_ref, hst_ref, hprev_ref, st_ref, dy_ref, wout_ref, cw_ref, cb_ref, bdr_ref, bdi_ref, br_ref,
             bi_ref, la_ref, ng_ref, nb_ref, ws_ref, bst_ref, wst_ref, glo_ref, ggo_ref, x_ref, dx1_ref, win_ref,
             gpre_ref, scm_ref, dz_ref, gx_ref, *rest):
        small_refs = rest[:16]
        (dcw_ref, dcb_ref, dwr_ref, dwi_ref, dbr_ref, dbi_ref, dspa_ref, dng_ref, dnb_ref, dws_ref, dbs_ref,
         dglo_ref, dggo_ref, dsh_ref, dsc_ref, dgpre_ref) = small_refs
        gcarry, anext, dxcnext, dv_scr = rest[16:]
        i = pl.program_id(0)

        @pl.when(i == 0)
        def _():
            for ref in small_refs:
                ref[...] = jnp.zeros_like(ref)
            gcarry[...] = jnp.zeros_like(gcarry)
            anext[...] = jnp.ones_like(anext)
            dxcnext[...] = jnp.zeros_like(dxcnext)

        first_tile = i == nt - 1
        xc, r, ig, a, mult = st_ref[ST_XC], st_ref[ST_R], st_ref[ST_IG], st_ref[ST_A], st_ref[ST_MULT]
        gl, u, spb, vhat = st_ref[ST_GL], st_ref[ST_U], st_ref[ST_SPB], st_ref[ST_VHAT]
        lx = lx_ref[...]
        h = hst_ref[...]
        hprev = _shift_down(h, jnp.where(first_tile, 0.0, hprev_ref[...]), 1)
        y_l = h * gl
        y_g = u * spb

        dycat = _dot_nt(dy_ref[...], wout_ref[...])

        dz_parts = {}

        def emit_dz(k, val):
            dz_parts[k] = val.astype(BF16)
            dz_ref[:, k * 512:(k + 1) * 512] = dz_parts[k]

        rl = _msq_rsqrt(y_l)
        yln = y_l * rl
        dyl = dycat[:, 0:512]
        dglo_ref[...] += _colsum(dyl * yln)
        dy_l = _rms_bwd(dyl * glo_ref[...], yln, rl)
        rg = _msq_rsqrt(y_g)
        ygn = y_g * rg
        dyg = dycat[:, 512:1024]
        dggo_ref[...] += _colsum(dyg * ygn)
        dy_g = _rms_bwd(dyg * ggo_ref[...], ygn, rg)

        emit_dz(1, dy_l * h * st_ref[ST_DGL])
        a_up = _shift_up(a, anext[...], 1)
        acum, gloc = _scan_bwd(a_up, dy_l * gl)
        gg = gloc + acum * gcarry[...]
        gcarry[...] = gg[0:1, :]
        anext[...] = a[0:SUBLANES, :]
        da = gg * hprev
        t1 = gg * mult
        di = t1 * xc
        dxc = t1 * ig
        dmult = gg * ig * xc
        dla = da * a - dmult * (a * a / mult)
        dspa_ref[...] += _colsum(dla * r) * (-LRU_C)
        dpr = dla * ((-LRU_C) * _softplus(-la_ref[...])) * r * (1.0 - r)
        dpi = di * ig * (1.0 - ig)
        dbr_ref[...] += _colsum(dpr)
        dbi_ref[...] += _colsum(dpi)
        dprb = dpr.astype(BF16)
        dpib = dpi.astype(BF16)
        xcb = xc.astype(BF16)
        dwr_ref[...] += _dot_tn(xcb, dprb)
        dwi_ref[...] += _dot_tn(xcb, dpib)
        dxc = dxc + _dot_nt(dprb, bdr_ref[...]) + _dot_nt(dpib, bdi_ref[...])
        nxt = dxcnext[...]
        dxcnext[...] = dxc[0:SUBLANES, :]
        up1, up2, up3 = _shift_up(dxc, nxt, 1), _shift_up(dxc, nxt, 2), _shift_up(dxc, nxt, 3)
        dcb_ref[...] += _colsum(dxc)
        dcw_ref[3:4, :] += _colsum(dxc * lx)
        dcw_ref[2:3, :] += _colsum(up1 * lx)
        dcw_ref[1:2, :] += _colsum(up2 * lx)
        dcw_ref[0:1, :] += _colsum(up3 * lx)
        dlx = cw_ref[3:4, :] * dxc + cw_ref[2:3, :] * up1 + cw_ref[1:2, :] * up2 + cw_ref[0:1, :] * up3
        emit_dz(0, dlx)

        emit_dz(2, dy_g * spb * st_ref[ST_DU])
        dsp = dy_g * u
        vb = (vhat * ng_ref[...] + nb_ref[...]).astype(BF16)
        for n in range(ts // GMLP_BLOCK):
            rs = slice(n * GMLP_BLOCK, (n + 1) * GMLP_BLOCK)
            for g in range(GMLP_GROUPS):
                cs = slice(g * 128, (g + 1) * 128)
                dbs_ref[:, g:g + 1] += jnp.sum(dsp[rs, cs], axis=1, keepdims=True)
                blk = dsp[rs, cs].astype(BF16)
                dws_ref[g] += _dot_nt(blk, vb[rs, cs])
                dv_scr[rs, cs] = _dot(wst_ref[g], blk)
        dv = dv_scr[...]
        dng_ref[...] += _colsum(dv * vhat)
        dnb_ref[...] += _colsum(dv)
        dvh = dv * ng_ref[...]
        dvg = dvh - jnp.mean(dvh, axis=-1, keepdims=True) - vhat * jnp.mean(dvh * vhat, axis=-1, keepdims=True)
        emit_dz(3, dvg * st_ref[ST_Q])

        dh = _dot_nt(dz_parts[0], win_ref[0])
        for k in range(1, N_CHIPS):
            dh = dh + _dot_nt(dz_parts[k], win_ref[k])
        xv = x_ref[...]
        rx = _msq_rsqrt(xv)
        xn = xv * rx
        dsh_ref[...] += _colsum(dh)
        dsc_ref[...] += _colsum(dh * (xn * gpre_ref[...]))
        dhn = dh * (1.0 + scm_ref[...])
        dgpre_ref[...] += _colsum(dhn * xn)
        gx_ref[...] = dx1_ref[...] + _rms_bwd(dhn * gpre_ref[...], xn, rx)

        @pl.when(i == nt - 1)
        def _():
            pos = lax.broadcasted_iota(jnp.int32, (GMLP_BLOCK, GMLP_BLOCK), 0) // CHUNK
            src = lax.broadcasted_iota(jnp.int32, (GMLP_BLOCK, GMLP_BLOCK), 1) // CHUNK
            for g in range(GMLP_GROUPS):
                dws_ref[g] = jnp.where(src <= pos, dws_ref[g], 0.0)
            dspa_ref[...] = dspa_ref[...] * (-_sigmoid(-la_ref[...]))

    vec = _full((1, d))
    in_specs = ([_rows(ts, 512, nt), _rows(ts, 512, nt), _halo_prev(ts, 512, SUBLANES, nt),
                 pl.BlockSpec((N_STASH, ts, 512), lambda i: (0, nt - 1 - i, 0)), _rows(ts, d, nt),
                 _full((d, d))]
                + _seq_param_specs() + [_full((4, 128, 128)), _full((1, 512)), _full((1, 512))]
                + [_rows(ts, d, nt), _rows(ts, d, nt), _full(w_in4.shape), vec, vec])
    return pl.pallas_call(
        body, grid=(nt,), name="seqmix_bwd",
        in_specs=in_specs,
        out_specs=[_rows(ts, 2048, nt), _rows(ts, d, nt)] + [_full(sh) for sh in small_shapes],
        out_shape=[_sds((s, 2048), BF16), _sds((s, d), F32)] + [_sds(sh, F32) for sh in small_shapes],
        scratch_shapes=[pltpu.VMEM((1, 512), F32), pltpu.VMEM((SUBLANES, 512), F32),
                        pltpu.VMEM((SUBLANES, 512), F32), pltpu.VMEM((ts, 512), F32)],
        compiler_params=_params(("arbitrary",)),
    )(lru_x, hst, hst, stash, dy, w_out, *seq_params, ws_t, glo, ggo, x, dx1, w_in4, g_pre, sc_m)


def _wgrad(a, b, n_chunks, name, chunk_major, ts=2048):
    s, m = a.shape
    n = b.shape[1]
    nc = n // n_chunks
    nt = s // ts

    def body(a_ref, b_ref, o_ref, acc):
        i = pl.program_id(1)

        @pl.when(i == 0)
        def _():
            acc[...] = jnp.zeros_like(acc)

        acc[...] += _dot_tn(a_ref[...], b_ref[...])

        @pl.when(i == nt - 1)
        def _():
            if chunk_major:
                o_ref[0] = acc[...].astype(BF16)
            else:
                o_ref[...] = acc[...].astype(BF16)

    if chunk_major:
        out_spec, out_shape = pl.BlockSpec((1, m, nc), lambda c, i: (c, 0, 0)), _sds((n_chunks, m, nc), BF16)
    else:
        out_spec, out_shape = pl.BlockSpec((m, nc), lambda c, i: (0, c)), _sds((m, n), BF16)
    return pl.pallas_call(
        body, grid=(n_chunks, nt), name=name,
        in_specs=[pl.BlockSpec((ts, m), lambda c, i: (i, 0)), pl.BlockSpec((ts, nc), lambda c, i: (i, c))],
        out_specs=out_spec,
        out_shape=out_shape,
        scratch_shapes=[pltpu.VMEM((m, nc), F32)],
        compiler_params=_params(("parallel", "arbitrary")),
    )(a, b)


def _block_diag(w):
    heads, hd, _ = w.shape
    eye = jnp.eye(heads, dtype=w.dtype)
    return (eye[:, None, :, None] * w[:, :, None, :]).reshape(heads * hd, heads * hd)


def _seq_params(small):
    row = lambda v: v.reshape(1, -1)
    pos = jnp.arange(GMLP_BLOCK)
    mask = (pos[None, :] // CHUNK) <= (pos[:, None] // CHUNK)
    ws = jnp.where(mask[None], small["w_spatial"], 0.0)
    seq_params = (small["conv_w"], row(small["conv_b"]),
                  _block_diag(small["w_rgate"]).astype(BF16), _block_diag(small["w_igate"]).astype(BF16),
                  row(small["b_rgate"]), row(small["b_igate"]), row(small["lru_a"]),
                  row(small["v_norm_g"]), row(small["v_norm_b"]), ws.astype(BF16), small["b_spatial"].T)
    return seq_params, jnp.swapaxes(ws, 1, 2).astype(BF16)


_ANY = pl.BlockSpec(memory_space=pl.ANY)
_CHIP_FLIPS = ((1, 0), (0, 1), (1, 1))


def _position():
    return lax.axis_index("x"), lax.axis_index("y"), lax.axis_index("c")


def _flip(v, f):
    return 1 - v if f else v


def _remote(src, dst, send_sem, recv_sem, peer):
    return pltpu.make_async_remote_copy(src_ref=src, dst_ref=dst, send_sem=send_sem, recv_sem=recv_sem,
                                        device_id=peer, device_id_type=MESH)


def _allgather8(block, name):
    r, n = block.shape

    def body(x_ref, gath, send_sems, recv_sems, loc_sem):
        x, y, c = _position()
        me = 4 * x + 2 * y + c
        loc = pltpu.make_async_copy(x_ref, gath.at[me], loc_sem)
        loc.start()
        peers = []
        for k in range(1, N_DEV):
            px, py, pc = _flip(x, k & 4), _flip(y, k & 2), _flip(c, k & 1)
            peers.append((px, py, pc))
            _remote(x_ref, gath.at[me], send_sems.at[k - 1], recv_sems.at[k - 1], (px, py, pc)).start()
        for k, (px, py, pc) in enumerate(peers):
            src = 4 * px + 2 * py + pc
            _remote(x_ref, gath.at[src], send_sems.at[k], recv_sems.at[k], (px, py, pc)).wait_recv()
        for k, peer in enumerate(peers):
            _remote(x_ref, gath.at[me], send_sems.at[k], recv_sems.at[k], peer).wait_send()
        loc.wait()

    return pl.pallas_call(
        body, name=name, out_shape=_sds((N_DEV, r, n), F32),
        in_specs=[pl.BlockSpec(memory_space=pltpu.VMEM)], out_specs=pl.BlockSpec(memory_space=pltpu.VMEM),
        scratch_shapes=[pltpu.SemaphoreType.DMA((N_DEV - 1,)), pltpu.SemaphoreType.DMA((N_DEV - 1,)),
                        pltpu.SemaphoreType.DMA],
        compiler_params=pltpu.CompilerParams(vmem_limit_bytes=VMEM_LIMIT_BYTES),
    )(block)


def _half(ref, c, rows):
    hr = rows // 2
    return ref.at[pl.ds(pl.multiple_of(c * hr, BF16_SUBLANES), hr), :]


def _chip_sum(part, recv, pos_arr, name):
    _, rows, cols = part.shape
    hr = rows // 2

    def body(pos_ref, p_ref, r_ref, o_ref, g_ref):
        total = (p_ref[...].astype(F32) + r_ref[...].astype(F32)).astype(BF16)
        o_ref[...] = total

        @pl.when(pl.program_id(0) == pos_ref[1])
        def _():
            g_ref[0] = total

    grid_spec = pltpu.PrefetchScalarGridSpec(
        num_scalar_prefetch=1, grid=(N_CHIPS,),
        in_specs=[pl.BlockSpec((1, hr, cols), lambda k, pos: (k, pos[0], 0)),
                  pl.BlockSpec((1, hr, cols), lambda k, pos: (k, 0, 0))],
        out_specs=[pl.BlockSpec((1, hr, cols), lambda k, pos: (k, 0, 0)),
                   pl.BlockSpec((1, 1, hr, cols), lambda k, pos: (0, pos[1], 0, 0))])
    return pl.pallas_call(
        body, name=name, grid_spec=grid_spec,
        out_shape=[_sds((N_CHIPS, hr, cols), BF16), _sds((2, N_CHIPS, hr, cols), BF16)],
        compiler_params=_params(("arbitrary",)),
    )(pos_arr, part, recv)


_HBM = pl.BlockSpec(memory_space=pltpu.HBM)
_SEM = pl.BlockSpec(memory_space=pltpu.SEMAPHORE)
_EFFECT = pltpu.SideEffectType.DATAFLOW_SIDE_EFFECTING


def _in_hbm(a):
    return pltpu.with_memory_space_constraint(a, pltpu.HBM)


def _split_start(srcs, lands, plan, n_copies, after, name):
    ns, nl = len(srcs), len(lands)
    bufs = list(srcs) + list(lands)

    def body(*refs):
        send_sems, recv_sems = refs[ns + nl + 1], refs[ns + nl + 2]
        token = refs[-1]
        for k, (src, dst, peer) in enumerate(plan(refs[:ns], refs[ns:ns + nl])):
            _remote(src, dst, send_sems.at[k], recv_sems.at[k], peer).start()
        token[...] = jnp.zeros_like(token)

    out = pl.pallas_call(
        body, name=name,
        out_shape=(pltpu.SemaphoreType.DMA((n_copies,)), pltpu.SemaphoreType.DMA((n_copies,)),
                   *[pltpu.HBM(b.shape, b.dtype) for b in bufs], _sds((SUBLANES, 128), F32)),
        in_specs=[_HBM] * (ns + nl) + [_ANY],
        out_specs=(_SEM, _SEM, *[_HBM] * (ns + nl), pl.BlockSpec(memory_space=pltpu.VMEM)),
        input_output_aliases={i: 2 + i for i in range(ns + nl)},
        compiler_params=pltpu.CompilerParams(has_side_effects=_EFFECT),
    )(*[_in_hbm(b) for b in bufs], after)
    return out[0], out[1], list(out[2:2 + ns]), list(out[2 + ns:2 + ns + nl]), out[-1]


def _split_wait(send_sems, recv_sems, srcs, lands, plan, after, name):
    ns, nl = len(srcs), len(lands)
    bufs = list(srcs) + list(lands)

    def body(*refs):
        send_ref, recv_ref = refs[ns + nl], refs[ns + nl + 1]
        me = _position()
        for k, src, dst in plan(refs[:ns], refs[ns:ns + nl]):
            cp = _remote(src, dst, send_ref.at[k], recv_ref.at[k], me)
            cp.wait_send()
            cp.wait_recv()

    out = pl.pallas_call(
        body, name=name,
        out_shape=[pltpu.HBM(b.shape, b.dtype) for b in bufs],
        in_specs=[_HBM] * (ns + nl) + [_SEM, _SEM, _ANY],
        out_specs=[_HBM] * (ns + nl),
        input_output_aliases={i: i for i in range(ns + nl)},
        compiler_params=pltpu.CompilerParams(has_side_effects=_EFFECT),
    )(*bufs, send_sems, recv_sems, after)
    return list(out[:ns]), list(out[ns:])


def _gather_plan(rows_of):
    def start(src_refs, land_refs):
        x, y, c = _position()
        chip = 2 * x + y
        out = []
        for a, rows in enumerate(rows_of):
            mine = _half(land_refs[a].at[chip], c, rows)
            out.extend((mine, mine, (_flip(x, fx), _flip(y, fy), c)) for fx, fy in _CHIP_FLIPS)
        return out

    def wait(src_refs, land_refs):
        x, y, c = _position()
        chip = 2 * x + y
        out = []
        for a, rows in enumerate(rows_of):
            for j, (fx, fy) in enumerate(_CHIP_FLIPS):
                src_chip = 2 * _flip(x, fx) + _flip(y, fy)
                out.append((3 * a + j, _half(land_refs[a].at[chip], c, rows),
                            _half(land_refs[a].at[src_chip], c, rows)))
        return out

    return start, wait


def _forward_plan(rows_of):
    def pieces(land_refs, half):
        x, y, _ = _position()
        return [_half(land_refs[a].at[2 * _flip(x, fx) + _flip(y, fy)], half, rows)
                for a, rows in enumerate(rows_of) for fx, fy in _CHIP_FLIPS]

    def start(src_refs, land_refs):
        x, y, c = _position()
        return [(p, p, (x, y, 1 - c)) for p in pieces(land_refs, c)]

    def wait(src_refs, land_refs):
        _, _, c = _position()
        return [(k, mine, theirs)
                for k, (mine, theirs) in enumerate(zip(pieces(land_refs, c), pieces(land_refs, 1 - c)))]

    return start, wait


def _swap_halves_plan(half_rows):
    def slices(src_refs, c):
        return [src_refs[a].at[:, pl.ds(pl.multiple_of((1 - c) * hr, BF16_SUBLANES), hr), :]
                for a, hr in enumerate(half_rows)]

    def start(src_refs, land_refs):
        x, y, c = _position()
        return [(src, land_refs[a], (x, y, 1 - c)) for a, src in enumerate(slices(src_refs, c))]

    def wait(src_refs, land_refs):
        _, _, c = _position()
        return [(a, src, land_refs[a]) for a, src in enumerate(slices(src_refs, c))]

    return start, wait


def _swap_gathered_plan(n_arrays):
    def start(src_refs, land_refs):
        x, y, c = _position()
        return [(land_refs[a].at[0], land_refs[a].at[1], (x, y, 1 - c)) for a in range(n_arrays)]

    def wait(src_refs, land_refs):
        return [(a, land_refs[a].at[0], land_refs[a].at[1]) for a in range(n_arrays)]

    return start, wait


def _exchange_plan(n_arrays):
    def start(src_refs, land_refs):
        x, y, c = _position()
        chip = 2 * x + y
        out = []
        for a in range(n_arrays):
            for fx, fy in _CHIP_FLIPS:
                px, py = _flip(x, fx), _flip(y, fy)
                out.append((src_refs[a].at[2 * px + py], land_refs[a].at[0, chip], (px, py, c)))
        return out

    def wait(src_refs, land_refs):
        x, y, c = _position()
        out = []
        for a in range(n_arrays):
            for j, (fx, fy) in enumerate(_CHIP_FLIPS):
                src_chip = 2 * _flip(x, fx) + _flip(y, fy)
                out.append((3 * a + j, src_refs[a].at[src_chip], land_refs[a].at[0, src_chip]))
        return out

    return start, wait


def _forward_to_sibling(lands, name):
    na = len(lands)

    def body(*refs):
        land_refs = refs[na:2 * na]
        send_sems, recv_sems = refs[2 * na:]
        x, y, c = _position()
        sibling = (x, y, 1 - c)
        sends = []
        for a in range(na):
            rows = lands[a].shape[1]
            for j, (fx, fy) in enumerate(_CHIP_FLIPS):
                landed = _half(land_refs[a].at[2 * _flip(x, fx) + _flip(y, fy)], c, rows)
                sends.append(_remote(landed, landed, send_sems.at[3 * a + j], recv_sems.at[3 * a + j], sibling))
                sends[-1].start()
        for a in range(na):
            rows = lands[a].shape[1]
            for j, (fx, fy) in enumerate(_CHIP_FLIPS):
                other = _half(land_refs[a].at[2 * _flip(x, fx) + _flip(y, fy)], 1 - c, rows)
                _remote(other, other, send_sems.at[3 * a + j], recv_sems.at[3 * a + j], sibling).wait_recv()
        for cp in sends:
            cp.wait_send()

    return pl.pallas_call(
        body, name=name,
        out_shape=[_sds(l.shape, l.dtype) for l in lands],
        in_specs=[_ANY] * na, out_specs=[_ANY] * na,
        input_output_aliases={a: a for a in range(na)},
        scratch_shapes=[pltpu.SemaphoreType.DMA((3 * na,))] * 2,
    )(*lands)


def _adam_gathered(w, gath, m, v, c_arr, after, name, tr=128):
    rows, cols = w.shape
    hr = rows // 2
    if hr % (2 * tr) == 0:
        tr = 2 * tr
    per = hr // tr

    def body(c_ref, w_ref, g_ref, m_ref, v_ref, after_ref, go_ref, d_ref, nm_ref, nv_ref):
        g = g_ref[0, 0].astype(F32)
        for k in range(1, N_CHIPS):
            g = g + g_ref[0, k].astype(F32)
        go_ref[...] = g
        d_ref[...], nm_ref[...], nv_ref[...] = _adam_math(w_ref[...], g, m_ref[...], v_ref[...])

    def rows_of(h, i, c_ref):
        c = c_ref[0]
        return ((c + h - 2 * c * h) * per + i, 0)

    blk = pl.BlockSpec((tr, cols), rows_of)
    grid_spec = pltpu.PrefetchScalarGridSpec(
        num_scalar_prefetch=1, grid=(2, per),
        in_specs=[blk, pl.BlockSpec((1, N_CHIPS, tr, cols), lambda h, i, c_ref: (h, 0, i, 0)), blk, blk, _ANY],
        out_specs=[blk] * 4)
    return pl.pallas_call(
        body, name=name, grid_spec=grid_spec, out_shape=[_sds(w.shape, F32)] * 4,
        compiler_params=_params(("arbitrary", "arbitrary")),
    )(c_arr, w, gath, m, v, after)


def _allreduce_small(block, name):
    two, r, n = block.shape
    assert two == 2

    def body(x_ref, out_ref, sib, chipsum, gath, d2d_send, d2d_recv, ici_send, ici_recv):
        x, y, c = _position()
        chip = 2 * x + y
        sibling = (x, y, 1 - c)
        first = _remote(x_ref, sib, d2d_send.at[0], d2d_recv.at[0], sibling)
        first.start()
        first.wait()
        chipsum[...] = x_ref[...] + sib[...]
        sends = []
        for j, (fx, fy) in enumerate(_CHIP_FLIPS):
            sends.append(_remote(chipsum.at[c], gath.at[chip], ici_send.at[j], ici_recv.at[j],
                                 (_flip(x, fx), _flip(y, fy), c)))
            sends[-1].start()
        gath[chip] = chipsum[c]
        for j, (fx, fy) in enumerate(_CHIP_FLIPS):
            landed = gath.at[2 * _flip(x, fx) + _flip(y, fy)]
            _remote(landed, landed, ici_send.at[j], ici_recv.at[j], sibling).wait_recv()
        for cp in sends:
            cp.wait_send()
        total = gath[0]
        for k in range(1, N_CHIPS):
            total = total + gath[k]
        out_ref[c] = total
        last = _remote(out_ref.at[c], out_ref.at[c], d2d_send.at[1], d2d_recv.at[1], sibling)
        last.start()
        _remote(out_ref.at[1 - c], out_ref.at[1 - c], d2d_send.at[1], d2d_recv.at[1], sibling).wait_recv()
        last.wait_send()

    vmem = pl.BlockSpec(memory_space=pltpu.VMEM)
    return pl.pallas_call(
        body, name=name, out_shape=_sds(block.shape, F32), in_specs=[vmem], out_specs=vmem,
        scratch_shapes=[pltpu.VMEM(block.shape, F32), pltpu.VMEM(block.shape, F32), pltpu.VMEM((N_CHIPS, r, n), F32),
                        pltpu.SemaphoreType.DMA((2,)), pltpu.SemaphoreType.DMA((2,)),
                        pltpu.SemaphoreType.DMA((3,)), pltpu.SemaphoreType.DMA((3,))],
        compiler_params=pltpu.CompilerParams(vmem_limit_bytes=VMEM_LIMIT_BYTES),
    )(block)


def _cast_place(shards, chip_arr, name):
    na = len(shards)
    steps = 4

    def body(chip_ref, *refs):
        for a in range(na):
            refs[na + a][0] = refs[a][...].astype(BF16)

    grid_spec = pltpu.PrefetchScalarGridSpec(
        num_scalar_prefetch=1, grid=(steps,),
        in_specs=[pl.BlockSpec((s.shape[0] // steps, s.shape[1]), lambda i, ch: (i, 0)) for s in shards],
        out_specs=[pl.BlockSpec((1, s.shape[0] // steps, s.shape[1]), lambda i, ch: (ch[0], i, 0)) for s in shards])
    return pl.pallas_call(
        body, name=name, grid_spec=grid_spec,
        out_shape=[_sds((N_CHIPS,) + s.shape, BF16) for s in shards],
        compiler_params=_params(("arbitrary",)),
    )(chip_arr, *shards)


def _silu(v):
    return v * _sigmoid(v)


def _ada_fwd(c8, w_ada):
    def body(c_ref, w_ref, o_ref):
        o_ref[...] = jnp.dot(_silu(c_ref[...]), w_ref[...], preferred_element_type=F32,
                             precision=lax.Precision.HIGHEST)

    return pl.pallas_call(
        body, name="ada_fwd", out_shape=_sds((N_DEV, w_ada.shape[1]), F32),
        compiler_params=pltpu.CompilerParams(vmem_limit_bytes=VMEM_LIMIT_BYTES),
    )(c8, w_ada)


def _mod_select(parts, b_ada, me_arr, after):
    cols = parts.shape[2]

    def body(me_ref, p_ref, b_ref, after_ref, o_ref):
        me = me_ref[0]
        for k in range(N_CHIPS):
            cs = slice(k * cols, (k + 1) * cols)
            o_ref[:, cs] = p_ref[2 * k, pl.ds(me, 1), :] + b_ref[:, cs]

    grid_spec = pltpu.PrefetchScalarGridSpec(
        num_scalar_prefetch=1, grid=(1,),
        in_specs=[pl.BlockSpec(parts.shape, lambda i, m: (0, 0, 0)), pl.BlockSpec(b_ada.shape, lambda i, m: (0, 0)),
                  _ANY],
        out_specs=pl.BlockSpec(b_ada.shape, lambda i, m: (0, 0)))
    return pl.pallas_call(body, name="mod_select", grid_spec=grid_spec, out_shape=_sds(b_ada.shape, F32))(
        me_arr, parts, b_ada, after)


def _ada_bwd(c8, dmod8, chip_arr, w, m, v, tr=256):
    d = c8.shape[1]
    cols = dmod8.shape[1] // N_CHIPS

    def body(chip_ref, c_ref, dm_ref, dmall_ref, w_ref, m_ref, v_ref, gw_ref, d_ref, nm_ref, nv_ref, gb_ref):
        g = lax.dot_general(_silu(c_ref[...]), dm_ref[...], (((0,), (0,)), ((), ())),
                            preferred_element_type=F32, precision=lax.Precision.HIGHEST)
        gw_ref[...] = g
        d_ref[...], nm_ref[...], nv_ref[...] = _adam_math(w_ref[...], g, m_ref[...], v_ref[...])
        acc = dmall_ref[0:1, :]
        for k in range(1, N_DEV):
            acc = acc + dmall_ref[k:k + 1, :]
        gb_ref[...] = acc

    rows = pl.BlockSpec((tr, cols), lambda i, ch: (i, 0))
    grid_spec = pltpu.PrefetchScalarGridSpec(
        num_scalar_prefetch=1, grid=(d // tr,),
        in_specs=[pl.BlockSpec((N_DEV, tr), lambda i, ch: (0, i)),
                  pl.BlockSpec((N_DEV, cols), lambda i, ch: (0, ch[0])),
                  pl.BlockSpec(dmod8.shape, lambda i, ch: (0, 0)), rows, rows, rows],
        out_specs=[rows] * 4 + [pl.BlockSpec((1, dmod8.shape[1]), lambda i, ch: (0, 0))])
    return pl.pallas_call(
        body, name="ada_bwd", grid_spec=grid_spec,
        out_shape=[_sds((d, cols), F32)] * 4 + [_sds((1, dmod8.shape[1]), F32)],
        compiler_params=_params(("arbitrary",)),
    )(chip_arr, c8, dmod8, dmod8, w, m, v)


def _adam_math(w, g, m, v):
    m = ADAM_B1 * m + (1.0 - ADAM_B1) * g
    v = ADAM_B2 * v + (1.0 - ADAM_B2) * (g * g)
    m_hat = m / (1.0 - ADAM_B1 ** ADAM_STEP)
    v_hat = v / (1.0 - ADAM_B2 ** ADAM_STEP)
    delta = -ADAM_LR * (m_hat / (jnp.sqrt(v_hat) + ADAM_EPS) + ADAM_WD * w)
    return delta, m, v


def _adam(w, g, m, v, name, tr=256):
    rows, cols = w.shape
    if rows % tr:
        tr = rows

    def body(w_ref, g_ref, m_ref, v_ref, d_ref, nm_ref, nv_ref):
        d_ref[...], nm_ref[...], nv_ref[...] = _adam_math(w_ref[...], g_ref[...], m_ref[...], v_ref[...])

    spec = pl.BlockSpec((tr, cols), lambda i: (i, 0))
    return pl.pallas_call(
        body, name=name, grid=(rows // tr,), in_specs=[spec] * 4, out_specs=[spec] * 3,
        out_shape=[_sds(w.shape, F32)] * 3, compiler_params=_params(("parallel",)),
    )(w, g, m, v)


SMALL_REPLICATED = ("g_mix_pre", "g_mix_post", "conv_b", "w_rgate", "b_rgate", "w_igate", "b_igate", "lru_a",
                    "v_norm_g", "v_norm_b", "w_spatial", "b_spatial", "g_lru_out", "g_gmlp_out", "g_ffn_pre",
                    "g_ffn_post", "ffn_conv_b")
SMALL_COLUMN_SHARDED = ("conv_w", "ffn_conv_w")

SMALL_ROW_LEN = 86016
_SMALL_ROWS = (
    (("ffn_conv_w", 18432), ("conv_w", 2048), ("w_spatial", 65536)),
    (("w_rgate", 32768), ("w_igate", 32768), ("ffn_conv_b", 6144), ("g_mix_pre", 1024), ("g_mix_post", 1024),
     ("g_ffn_pre", 1024), ("g_ffn_post", 1024), ("conv_b", 512), ("b_rgate", 512), ("b_igate", 512),
     ("lru_a", 512), ("v_norm_g", 512), ("v_norm_b", 512), ("b_spatial", 512), ("g_lru_out", 512),
     ("g_gmlp_out", 512), ("loss", 128)),
)


def _small_slots():
    slots = {}
    for row, entries in enumerate(_SMALL_ROWS):
        off = 0
        for name, size in entries:
            slots[name] = (row, off)
            off += size
        assert off <= SMALL_ROW_LEN
    return slots


SMALL_SLOT = _small_slots()
SMALL_LANES = SMALL_ROW_LEN // SUBLANES


def _small_pieces(name, first, count):
    row, off = SMALL_SLOT[name]
    pos, pieces = off + first, []
    while count:
        sub, lane = divmod(pos, SMALL_LANES)
        n = min(count, SMALL_LANES - lane)
        pieces.append((row, sub, lane, n))
        pos, count = pos + n, count - n
    return pieces
ROW_VECTORS = ("ffn_conv_b", "g_mix_pre", "g_mix_post", "g_ffn_pre", "g_ffn_post", "conv_b", "lru_a", "v_norm_g",
               "v_norm_b", "g_lru_out", "g_gmlp_out")
HEAD_DIM = LRU_WIDTH // LRU_HEADS


def _pack_small(g, after):
    order = ("ffn_conv_w", "conv_w", "w_spatial", "w_rgate", "w_igate", "b_rgate", "b_igate", "b_spatial", "loss") \
        + ROW_VECTORS
    vmem = pl.BlockSpec(memory_space=pltpu.VMEM)

    def body(*refs):
        src = dict(zip(order, refs))
        out_ref = refs[len(order) + 1]
        out_ref[...] = jnp.zeros_like(out_ref)

        def put(name, first, val):
            col = 0
            for row, sub, lane, n in _small_pieces(name, first, val.shape[1]):
                out_ref[row, sub:sub + 1, lane:lane + n] = val[:, col:col + n]
                col += n

        for name in ROW_VECTORS + ("b_rgate", "b_igate", "loss"):
            put(name, 0, src[name][...])
        for name in ("ffn_conv_w", "conv_w"):
            k_taps, n = src[name].shape
            for k in range(k_taps):
                put(name, k * n, src[name][k:k + 1, :])
        for g_idx in range(GMLP_GROUPS):
            for i in range(GMLP_BLOCK):
                put("w_spatial", (g_idx * GMLP_BLOCK + i) * GMLP_BLOCK, src["w_spatial"][g_idx, i:i + 1, :])
        for name in ("w_rgate", "w_igate"):
            for h in range(LRU_HEADS):
                for i in range(HEAD_DIM):
                    r = h * HEAD_DIM + i
                    put(name, r * HEAD_DIM, src[name][r:r + 1, h * HEAD_DIM:(h + 1) * HEAD_DIM])
        eye = (lax.broadcasted_iota(jnp.int32, (GMLP_BLOCK, GMLP_BLOCK), 0)
               == lax.broadcasted_iota(jnp.int32, (GMLP_BLOCK, GMLP_BLOCK), 1))
        for g_idx in range(GMLP_GROUPS):
            col = src["b_spatial"][:, g_idx:g_idx + 1]
            put("b_spatial", g_idx * GMLP_BLOCK, _colsum(jnp.where(eye, col, 0.0)))

    return pl.pallas_call(
        body, name="pack_small", out_shape=_sds((2, SUBLANES, SMALL_LANES), F32),
        in_specs=[vmem] * len(order) + [_ANY], out_specs=vmem,
        compiler_params=pltpu.CompilerParams(vmem_limit_bytes=VMEM_LIMIT_BYTES),
    )(*[g[n] for n in order], after)


def _adam_small(g_small, w, m, v):
    vmem = pl.BlockSpec(memory_space=pltpu.VMEM)
    n_p = len(SMALL_REPLICATED)

    def body(g_ref, *refs):
        w_refs, m_refs, v_refs = refs[:n_p], refs[n_p:2 * n_p], refs[2 * n_p:3 * n_p]
        outs = refs[3 * n_p:]
        go, do, mo, vo = outs[:n_p], outs[n_p:2 * n_p], outs[2 * n_p:3 * n_p], outs[3 * n_p:]
        for k, name in enumerate(SMALL_REPLICATED):
            def take(first, count, name=name):
                parts = [g_ref[row, sub:sub + 1, lane:lane + n]
                         for row, sub, lane, n in _small_pieces(name, first, count)]
                return parts[0] if len(parts) == 1 else jnp.concatenate(parts, axis=1)

            shape = w_refs[k].shape
            if name in ROW_VECTORS:
                go[k][...] = take(0, shape[1])
            elif name in ("b_rgate", "b_igate"):
                for h in range(LRU_HEADS):
                    go[k][0, h:h + 1, :] = take(h * HEAD_DIM, HEAD_DIM)
            elif name == "b_spatial":
                for g_idx in range(GMLP_GROUPS):
                    go[k][0, g_idx:g_idx + 1, :] = take(g_idx * GMLP_BLOCK, GMLP_BLOCK)
            elif name == "w_spatial":
                for g_idx in range(GMLP_GROUPS):
                    for i in range(GMLP_BLOCK):
                        go[k][0, g_idx, i:i + 1, :] = take((g_idx * GMLP_BLOCK + i) * GMLP_BLOCK, GMLP_BLOCK)
            else:
                for h in range(LRU_HEADS):
                    for i in range(HEAD_DIM):
                        go[k][0, h, i:i + 1, :] = take((h * HEAD_DIM + i) * HEAD_DIM, HEAD_DIM)
            do[k][...], mo[k][...], vo[k][...] = _adam_math(w_refs[k][...], go[k][...], m_refs[k][...],
                                                             v_refs[k][...])

    names = SMALL_REPLICATED
    out_shape = [_sds(w[n].shape, F32) for n in names] * 4
    res = pl.pallas_call(
        body, name="adam_small", out_shape=out_shape,
        in_specs=[vmem] * (1 + 3 * n_p), out_specs=[vmem] * (4 * n_p),
        compiler_params=pltpu.CompilerParams(vmem_limit_bytes=VMEM_LIMIT_BYTES),
    )(g_small, *[w[n] for n in names], *[m[n] for n in names], *[v[n] for n in names])
    return [dict(zip(names, res[k * n_p:(k + 1) * n_p])) for k in range(4)]


def _adam_cols(name, g_small, w, m, v, chip_arr):
    _, k_taps, n = w.shape
    row, off = SMALL_SLOT[name]
    first = off // n
    per_sub = SMALL_LANES // n

    def body(chip_ref, *refs):
        g_refs = refs[:k_taps]
        w_ref, m_ref, v_ref, go_ref, d_ref, nm_ref, nv_ref = refs[k_taps:]
        for k in range(k_taps):
            tap = (0, slice(k, k + 1), slice(None))
            sub = (first + N_CHIPS * k + chip_ref[0]) // per_sub
            g = g_refs[k][row, pl.ds(sub, 1), :]
            go_ref[tap] = g
            d_ref[tap], nm_ref[tap], nv_ref[tap] = _adam_math(w_ref[tap], g, m_ref[tap], v_ref[tap])

    whole = pl.BlockSpec(w.shape, lambda i, ch: (0, 0, 0))
    taps = [pl.BlockSpec((2, SUBLANES, n),
                         functools.partial(lambda i, ch, k: (0, 0, (first + N_CHIPS * k + ch[0]) % per_sub), k=k))
            for k in range(k_taps)]
    grid_spec = pltpu.PrefetchScalarGridSpec(
        num_scalar_prefetch=1, grid=(1,), in_specs=taps + [whole] * 3, out_specs=[whole] * 4)
    return pl.pallas_call(body, name="adam_" + name, grid_spec=grid_spec, out_shape=[_sds(w.shape, F32)] * 4)(
        chip_arr, *[g_small] * k_taps, w, m, v)


def kernel(x, c, w_ada, b_ada, g_mix_pre, g_mix_post, w_in, conv_w, conv_b, w_rgate, b_rgate, w_igate, b_igate, lru_a, v_norm_g, v_norm_b, w_spatial, b_spatial, g_lru_out, g_gmlp_out, w_out, g_ffn_pre, g_ffn_post, w_up, ffn_conv_w, ffn_conv_b, w_down, loss_target, m_w_ada, m_b_ada, m_g_mix_pre, m_g_mix_post, m_w_in, m_conv_w, m_conv_b, m_w_rgate, m_b_rgate, m_w_igate, m_b_igate, m_lru_a, m_v_norm_g, m_v_norm_b, m_w_spatial, m_b_spatial, m_g_lru_out, m_g_gmlp_out, m_w_out, m_g_ffn_pre, m_g_ffn_post, m_w_up, m_ffn_conv_w, m_ffn_conv_b, m_w_down, v_w_ada, v_b_ada, v_g_mix_pre, v_g_mix_post, v_w_in, v_conv_w, v_conv_b, v_w_rgate, v_b_rgate, v_w_igate, v_b_igate, v_lru_a, v_v_norm_g, v_v_norm_b, v_w_spatial, v_b_spatial, v_g_lru_out, v_g_gmlp_out, v_w_out, v_g_ffn_pre, v_g_ffn_post, v_w_up, v_ffn_conv_w, v_ffn_conv_b, v_w_down):
    args = dict(locals())
    names = ("w_ada", "b_ada", "g_mix_pre", "g_mix_post", "w_in", "conv_w", "conv_b", "w_rgate", "b_rgate",
             "w_igate", "b_igate", "lru_a", "v_norm_g", "v_norm_b", "w_spatial", "b_spatial", "g_lru_out",
             "g_gmlp_out", "w_out", "g_ffn_pre", "g_ffn_post", "w_up", "ffn_conv_w", "ffn_conv_b", "w_down")
    drop = lambda a: a if a.ndim == 2 else a[0]
    w = {n: drop(args[n]) for n in names}
    m = {n: drop(args["m_" + n]) for n in names}
    v = {n: drop(args["v_" + n]) for n in names}
    xi, yi, ci = _position()
    me_arr = jnp.reshape(4 * xi + 2 * yi + ci, (1,)).astype(jnp.int32)
    chip_arr = jnp.reshape(2 * xi + yi, (1,)).astype(jnp.int32)
    c_arr = jnp.reshape(ci, (1,)).astype(jnp.int32)
    pos_arr = jnp.stack([ci, 2 * xi + yi]).astype(jnp.int32)

    big = ("w_in", "w_out", "w_up", "w_down")
    lands_a = _cast_place([w[n] for n in big[:2]], chip_arr, "cast_place_a")
    start_a, wait_a = _gather_plan([w[n].shape[0] for n in big[:2]])
    start_b, wait_b = _gather_plan([w[n].shape[0] for n in big[2:]])

    row0 = jnp.concatenate([c, w["conv_w"].reshape(1, -1), w["ffn_conv_w"].reshape(1, -1)], axis=1)
    g0 = _allgather8(row0, "gather_cond")[:, 0, :]
    send_a, recv_a, _, lands_a, token_a = _split_start([], lands_a, start_a, 6, g0, "gather_start_a")
    lands_b = _cast_place([w[n] for n in big[2:]], chip_arr + token_a[0, 0].astype(jnp.int32), "cast_place_b")
    c8 = g0[:, :D_MODEL]
    per_chip = g0[0::2]
    conv_w_full = per_chip[:, D_MODEL:D_MODEL + 512].reshape(N_CHIPS, 4, 128).transpose(1, 0, 2).reshape(4, 512)
    ffn_conv_w_full = per_chip[:, D_MODEL + 512:].reshape(N_CHIPS, 3, 1536).transpose(1, 0, 2).reshape(3, 2 * D_FF)
    mod_parts = _allgather8(_ada_fwd(c8 + token_a[0:1, 0:1], w["w_ada"]), "gather_mod")
    send_b, recv_b, _, lands_b, token_b = _split_start([], lands_b, start_b, 6, mod_parts, "gather_start_b")
    mod = _mod_select(mod_parts, w["b_ada"].reshape(1, -1), me_arr, token_b).reshape(N_MOD, D_MODEL)
    sh_m, sc_m, gt_m, sh_f, sc_f, gt_f = [mod[k:k + 1] for k in range(N_MOD)]

    small = {n: w[n] for n in SMALL_REPLICATED}
    small["conv_w"] = conv_w_full
    small["ffn_conv_w"] = ffn_conv_w_full
    row = lambda a: a.reshape(1, -1)
    seq_params, ws_t = _seq_params(small)
    glo, ggo = row(small["g_lru_out"]), row(small["g_gmlp_out"])
    g_pre, g_post = row(small["g_mix_pre"]), row(small["g_mix_post"])
    g_pre2, g_post2 = row(small["g_ffn_pre"]), row(small["g_ffn_post"])
    fw, fb = small["ffn_conv_w"], row(small["ffn_conv_b"])
    xs, tgt = x[0], loss_target[0]

    _, lands_a = _split_wait(send_a, recv_a, [], lands_a, wait_a, mod, "gather_wait_a")
    w_in4, w_out4 = _forward_to_sibling(lands_a, "forward_a")
    w_out_b = w_out4.reshape(D_MODEL, D_MODEL)
    h, lx, ycat, hst, stash = _seqmix(xs, sc_m, sh_m, g_pre, w_in4, seq_params, glo, ggo)
    _, lands_b = _split_wait(send_b, recv_b, [], lands_b, wait_b, ycat, "gather_wait_b")
    fwd_start, fwd_wait = _forward_plan([w[n].shape[0] for n in big[2:]])
    fwd_send, fwd_recv, _, lands_b, tok = _split_start([], lands_b, fwd_start, 6, pos_arr, "forward_start_b")
    y, x1, h2 = _mix_out(ycat, xs, w_out_b, gt_m + tok[0:1, 0:1], g_post, g_pre2, sc_f, sh_f)
    _, (w_up4, w_down4) = _split_wait(fwd_send, fwd_recv, [], lands_b, fwd_wait, h2, "forward_wait_b")
    w_down_b = w_down4.reshape(D_FF, D_MODEL)
    up0, pre, act, dy2, dx2, loss, dgt_f, dg_post2 = _ffn_fwd(h2, x1, tgt, w_up4, w_down_b, fw, fb, gt_f, g_post2)

    dup0, dfw, dfb = _ffn_bwd_a(dy2, pre, up0, w_down_b, fw)
    gw_up = _wgrad(h2, dup0, N_CHIPS, "wgrad_up", True)
    gw_down = _wgrad(act, dy2, 2, "wgrad_down", False)
    ex_start, ex_wait = _exchange_plan(2)
    sg_start, sg_wait = _swap_gathered_plan(2)
    grads, deltas, new_m, new_v = {}, {}, {}, {}

    def swap_start(parts, name):
        sw_start, sw_wait = _swap_halves_plan([p.shape[1] // 2 for p in parts])
        recv = [lax.empty((N_CHIPS, p.shape[1] // 2, p.shape[2]), BF16) for p in parts]
        send_s, recv_s, parts, recv, token = _split_start(parts, recv, sw_start, len(parts), pos_arr,
                                                           "swap_start_" + name)
        return (send_s, recv_s, parts, recv, sw_wait), token

    def exchange_start(swap, tags, after, name):
        send_s, recv_s, parts, recv, sw_wait = swap
        parts, recv = _split_wait(send_s, recv_s, parts, recv, sw_wait, after, "swap_wait_" + name)
        both = [_chip_sum(p, r, pos_arr, "chip_sum_" + t) for p, r, t in zip(parts, recv, tags)]
        sums, gath = [b[0] for b in both], [b[1] for b in both]
        return _split_start(sums, gath, ex_start, 3 * len(parts), pos_arr, "exchange_start_" + name)

    def gathered_start(exchange, after, name):
        send_s, recv_s, sums, gath, _ = exchange
        _, gath = _split_wait(send_s, recv_s, sums, gath, ex_wait, after, "exchange_wait_" + name)
        send_s, recv_s, _, gath, token = _split_start([], gath, sg_start, len(gath), pos_arr,
                                                      "gathered_start_" + name)
        return (send_s, recv_s, gath), token

    def gathered_wait(gathered, after, name):
        send_s, recv_s, gath = gathered
        return _split_wait(send_s, recv_s, [], gath, sg_wait, after, "gathered_wait_" + name)[1]

    def adam_big(t, gath, after):
        grads[t], deltas[t], new_m[t], new_v[t] = _adam_gathered(w[t], gath, m[t], v[t], c_arr, after, "adam_" + t)

    def behind(value, token):
        return value + token[0:1, 0:1]

    tags_b, tags_a = ("w_up", "w_down"), ("w_in", "w_out")
    swap_b, tok = swap_start([gw_up, gw_down.reshape(N_CHIPS, -1, D_MODEL)], "b")
    dx1, dy, dsh_f, dsc_f, dg_pre2, dgt_m, dg_post = _ffn_bwd_b(
        dup0, x1, y, dx2, w_up4, g_pre2, behind(sc_f, tok), sh_f, gt_m, g_post)
    exchange_b = exchange_start(swap_b, tags_b, dg_post, "b")
    (dz, grad_x, dcw, dcb, dwr, dwi, dbr, dbi, dspa, dng, dnb, dws, dbs_t, dglo, dggo, dsh_m, dsc_m,
     dg_pre) = _seqmix_bwd(lx, hst, stash, dy, w_out_b, seq_params, ws_t, behind(glo, exchange_b[4]), ggo,
                           xs, dx1, w_in4, g_pre, sc_m)
    gw_in = _wgrad(h, dz, N_CHIPS, "wgrad_in", True)
    gw_out = _wgrad(ycat, dy, 1, "wgrad_out", False)
    swap_a, tok = swap_start([gw_in, gw_out.reshape(N_CHIPS, -1, D_MODEL)], "a")

    dmod = jnp.concatenate([behind(dsh_m, tok), dsc_m, dgt_m, dsh_f, dsc_f, dgt_f], axis=1)
    dmod8 = _allgather8(dmod, "gather_dmod")[:, 0, :]
    small_grads = dict(
        g_mix_pre=dg_pre, g_mix_post=dg_post, conv_w=dcw, conv_b=dcb, w_rgate=dwr, b_rgate=dbr, w_igate=dwi,
        b_igate=dbi, lru_a=dspa, v_norm_g=dng, v_norm_b=dnb, w_spatial=dws, b_spatial=dbs_t, g_lru_out=dglo,
        g_gmlp_out=dggo, g_ffn_pre=dg_pre2, g_ffn_post=dg_post2, ffn_conv_w=dfw, ffn_conv_b=dfb,
        loss=loss)
    g_small = _allreduce_small(_pack_small(small_grads, dmod8), "reduce_small")
    total = g_small[_small_pieces("loss", 0, 1)[0][:3]]
    exchange_a = exchange_start(swap_a, tags_a, g_small, "a")
    gathered_b, tok = gathered_start(exchange_b, exchange_a[4], "b")

    grads["w_ada"], deltas["w_ada"], new_m["w_ada"], new_v["w_ada"], g_b_ada = _ada_bwd(
        c8, behind(dmod8, tok), chip_arr, w["w_ada"], m["w_ada"], v["w_ada"])
    rep = SMALL_REPLICATED
    small_out = _adam_small(g_small, {n: args[n] for n in rep}, {n: args["m_" + n] for n in rep},
                            {n: args["v_" + n] for n in rep})
    for n in rep:
        grads[n], deltas[n], new_m[n], new_v[n] = [group[n] for group in small_out]
    for n in SMALL_COLUMN_SHARDED:
        grads[n], deltas[n], new_m[n], new_v[n] = _adam_cols(n, g_small, args[n], args["m_" + n],
                                                             args["v_" + n], chip_arr)
    d_b, m_b, v_b = _adam(w["b_ada"], g_b_ada, m["b_ada"], v["b_ada"], "adam_b_ada")
    grads["b_ada"], deltas["b_ada"], new_m["b_ada"], new_v["b_ada"] = g_b_ada, d_b, m_b, v_b

    gath_up, gath_down = gathered_wait(gathered_b, d_b, "b")
    adam_big("w_down", gath_down, pos_arr)
    gathered_a, tok = gathered_start(exchange_a, deltas["w_down"], "a")
    adam_big("w_up", gath_up, tok)
    gath_in, gath_out = gathered_wait(gathered_a, deltas["w_up"], "a")
    adam_big("w_in", gath_in, pos_arr)
    adam_big("w_out", gath_out, pos_arr)

    outs = [total, grad_x[None]]
    for group in (grads, deltas, new_m, new_v):
        outs.extend(group[n].reshape(args[n].shape) for n in names)
    return tuple(outs)
```

```python
import functools
import math

import jax
import jax.numpy as jnp
from jax import lax
from jax.experimental import pallas as pl
from jax.experimental.pallas import tpu as pltpu

F32 = jnp.float32
BF16 = jnp.bfloat16
MESH = pl.DeviceIdType.MESH

D_MODEL = 1024
LRU_WIDTH = 512
LRU_HEADS = 8
GMLP_GROUPS = 4
GMLP_BLOCK = 128
CHUNK = 64
D_FF = 3072
N_MOD = 6
EPS = 1e-6
LRU_C = 8.0
N_CHIPS = 4
N_DEV = 8

ADAM_LR = 0.001
ADAM_B1 = 0.9
ADAM_B2 = 0.999
ADAM_EPS = 1e-08
ADAM_WD = 0.01
ADAM_STEP = 10

GELU_C0 = math.sqrt(2.0 / math.pi)
GELU_C1 = 0.044715

VMEM_LIMIT_BYTES = 56 * 1024 * 1024
SUBLANES = 8
BF16_SUBLANES = 16
FFN_CHUNK = 768
SUB_ROWS = 256


def _gelu_gate(x):
    x2 = x * x
    z = x * ((2.0 * GELU_C0 * GELU_C1) * x2 + 2.0 * GELU_C0)
    return 1.0 / (1.0 + jnp.exp(-z)), x2


def _gelu(x):
    t = jnp.tanh(GELU_C0 * (x + GELU_C1 * x * x * x))
    return 0.5 * x * (1.0 + t)


def _gelu_and_grad(x):
    s, x2 = _gelu_gate(x)
    g = x * s
    dz = (6.0 * GELU_C0 * GELU_C1) * x2 + 2.0 * GELU_C0
    return g, s + g * (1.0 - s) * dz


def _sigmoid(x):
    return 1.0 / (1.0 + jnp.exp(-x))


def _log1p(u):
    w = 1.0 + u
    return jnp.where(w == 1.0, u, jnp.log(w) * (u / (w - 1.0)))


def _softplus(x):
    return jnp.maximum(x, 0.0) + _log1p(jnp.exp(-jnp.abs(x)))


def _neg_expm1(x):
    u = jnp.exp(x)
    um1 = u - 1.0
    tiny = um1 == 0.0
    small = um1 * (x / jnp.log(jnp.where(tiny, 2.0, jnp.maximum(u, 0.25))))
    return -jnp.where(tiny, x, jnp.where(x < -1.0, um1, small))


def _msq_rsqrt(v):
    return lax.rsqrt(jnp.mean(v * v, axis=-1, keepdims=True) + EPS)


def _rms_bwd(dyn, yn, r):
    return r * (dyn - yn * jnp.mean(dyn * yn, axis=-1, keepdims=True))


def _colsum(v):
    return jnp.sum(v, axis=0, keepdims=True)


def _shift_down(cur, prev8, k):
    rolled = pltpu.roll(cur, k, 0)
    head = pltpu.roll(prev8, k, 0)
    row8 = lax.broadcasted_iota(jnp.int32, (SUBLANES, cur.shape[1]), 0)
    first = jnp.where(row8 < k, head, rolled[0:SUBLANES])
    return jnp.concatenate([first, rolled[SUBLANES:]], axis=0)


def _shift_up(cur, next8, k):
    t = cur.shape[0]
    rolled = pltpu.roll(cur, t - k, 0)
    tail = pltpu.roll(next8, SUBLANES - k, 0)
    row8 = lax.broadcasted_iota(jnp.int32, (SUBLANES, cur.shape[1]), 0)
    last = jnp.where(row8 >= SUBLANES - k, tail, rolled[t - SUBLANES:])
    return jnp.concatenate([rolled[:t - SUBLANES], last], axis=0)


def _scan_fwd(a, b):
    t = a.shape[0]
    row = lax.broadcasted_iota(jnp.int32, a.shape, 0)
    d = 1
    while d < t:
        keep = row >= d
        a_s = jnp.where(keep, pltpu.roll(a, d, 0), 1.0)
        b_s = jnp.where(keep, pltpu.roll(b, d, 0), 0.0)
        b = a * b_s + b
        a = a * a_s
        d *= 2
    return a, b


def _scan_bwd(a, g):
    t = a.shape[0]
    row = lax.broadcasted_iota(jnp.int32, a.shape, 0)
    d = 1
    while d < t:
        keep = row < t - d
        a_s = jnp.where(keep, pltpu.roll(a, t - d, 0), 1.0)
        g_s = jnp.where(keep, pltpu.roll(g, t - d, 0), 0.0)
        g = a * g_s + g
        a = a * a_s
        d *= 2
    return a, g


def _dot(a, b):
    return jnp.dot(a, b, preferred_element_type=F32)


def _dot_nt(a, b):
    return lax.dot_general(a, b, (((1,), (1,)), ((), ())), preferred_element_type=F32)


def _dot_tn(a, b):
    return lax.dot_general(a, b, (((0,), (0,)), ((), ())), preferred_element_type=F32)


def _rows(ts, cols, rev_of=None):
    if rev_of is None:
        return pl.BlockSpec((ts, cols), lambda i: (i, 0))
    return pl.BlockSpec((ts, cols), lambda i: (rev_of - 1 - i, 0))


def _halo_prev(ts, cols, halo, rev_of=None, col_block=0):
    per = ts // halo
    if rev_of is None:
        return pl.BlockSpec((halo, cols), lambda i: (jnp.maximum(i * per - 1, 0), col_block))
    return pl.BlockSpec((halo, cols), lambda i: (jnp.maximum((rev_of - 1 - i) * per - 1, 0), col_block))


def _full(shape):
    nd = len(shape)
    return pl.BlockSpec(shape, lambda *_: (0,) * nd)


_RESIDENT = pl.BlockSpec(memory_space=pltpu.VMEM)


def _params(sem):
    return pltpu.CompilerParams(dimension_semantics=sem, vmem_limit_bytes=VMEM_LIMIT_BYTES)


def _sds(shape, dtype):
    return jax.ShapeDtypeStruct(shape, dtype)


def _sub_tiles(ts):
    return [slice(r0, r0 + SUB_ROWS) for r0 in range(0, ts, SUB_ROWS)]


N_STASH = 12
(ST_XC, ST_R, ST_IG, ST_A, ST_MULT, ST_GL, ST_DGL, ST_U, ST_DU, ST_Q, ST_VHAT, ST_SPB) = range(N_STASH)


def _seq_param_specs():
    return [_full((4, 512)), _full((1, 512)), _full((512, 512)), _full((512, 512)), _full((1, 512)),
            _full((1, 512)), _full((1, 512)), _full((1, 512)), _full((1, 512)), _full((4, 128, 128)),
            _full((128, 4))]


def _seqmix(x, sc, sh, g_pre, w_in4, seq_params, glo, ggo, ts=256):
    s, d = x.shape
    nt = s // ts

    def body(x_ref, sc_ref, sh_ref, gpre_ref, win_ref, cw_ref, cb_ref, bdr_ref, bdi_ref, br_ref, bi_ref, la_ref,
             ng_ref, nb_ref, ws_ref, bst_ref, glo_ref, ggo_ref, hin_ref, lx_ref, ycat_ref, hst_ref, st_ref,
             hcarry, lxprev, sp_scr):
        i = pl.program_id(0)

        @pl.when(i == 0)
        def _():
            hcarry[...] = jnp.zeros_like(hcarry)
            lxprev[...] = jnp.zeros_like(lxprev)

        xv = x_ref[...]
        hin = ((xv * _msq_rsqrt(xv) * gpre_ref[...]) * (1.0 + sc_ref[...]) + sh_ref[...]).astype(BF16)
        hin_ref[...] = hin
        z = [_dot(hin, win_ref[k]) for k in range(N_CHIPS)]

        lx = z[0]
        lx_ref[...] = lx
        prev8 = lxprev[...]
        lxprev[...] = lx[ts - SUBLANES:, :]
        xc = (cw_ref[3:4, :] * lx + cw_ref[2:3, :] * _shift_down(lx, prev8, 1)
              + cw_ref[1:2, :] * _shift_down(lx, prev8, 2) + cw_ref[0:1, :] * _shift_down(lx, prev8, 3)
              + cb_ref[...])
        xcb = xc.astype(BF16)
        r = _sigmoid(_dot(xcb, bdr_ref[...]) + br_ref[...])
        ig = _sigmoid(_dot(xcb, bdi_ref[...]) + bi_ref[...])
        log_a = (-LRU_C) * r * _softplus(-la_ref[...])
        a = jnp.exp(log_a)
        mult = jnp.sqrt(_neg_expm1(2.0 * log_a))
        acum, hloc = _scan_fwd(a, mult * (ig * xc))
        h = hloc + acum * hcarry[...]
        hcarry[...] = h[ts - 1:ts, :]
        hst_ref[...] = h
        gl, dgl = _gelu_and_grad(z[1])
        y_l = h * gl
        for slot, val in ((ST_XC, xc), (ST_R, r), (ST_IG, ig), (ST_A, a), (ST_MULT, mult), (ST_GL, gl),
                          (ST_DGL, dgl)):
            st_ref[slot] = val

        u, du = _gelu_and_grad(z[2])
        vg, dvg = _gelu_and_grad(z[3])
        vc = vg - jnp.mean(vg, axis=-1, keepdims=True)
        rstd = lax.rsqrt(jnp.mean(vc * vc, axis=-1, keepdims=True) + EPS)
        vhat = vc * rstd
        vb = (vhat * ng_ref[...] + nb_ref[...]).astype(BF16)
        for n in range(ts // GMLP_BLOCK):
            rs = slice(n * GMLP_BLOCK, (n + 1) * GMLP_BLOCK)
            for g in range(GMLP_GROUPS):
                cs = slice(g * 128, (g + 1) * 128)
                sp_scr[rs, cs] = _dot(ws_ref[g], vb[rs, cs]) + bst_ref[:, g:g + 1]
        spb = sp_scr[...]
        y_g = u * spb
        for slot, val in ((ST_U, u), (ST_DU, du), (ST_Q, rstd * dvg), (ST_VHAT, vhat), (ST_SPB, spb)):
            st_ref[slot] = val

        ycat_ref[:, 0:512] = (y_l * _msq_rsqrt(y_l) * glo_ref[...]).astype(BF16)
        ycat_ref[:, 512:1024] = (y_g * _msq_rsqrt(y_g) * ggo_ref[...]).astype(BF16)

    vec = _full((1, d))
    return pl.pallas_call(
        body, grid=(nt,), name="seqmix",
        in_specs=[_rows(ts, d), vec, vec, vec, _full(w_in4.shape)] + _seq_param_specs()
        + [_full((1, 512)), _full((1, 512))],
        out_specs=[_rows(ts, d), _rows(ts, 512), _rows(ts, d), _rows(ts, 512),
                   pl.BlockSpec((N_STASH, ts, 512), lambda i: (0, i, 0))],
        out_shape=[_sds((s, d), BF16), _sds((s, 512), F32), _sds((s, d), BF16), _sds((s, 512), F32),
                   _sds((N_STASH, s, 512), F32)],
        scratch_shapes=[pltpu.VMEM((1, 512), F32), pltpu.VMEM((SUBLANES, 512), F32), pltpu.VMEM((ts, 512), F32)],
        compiler_params=_params(("arbitrary",)),
    )(x, sc, sh, g_pre, w_in4, *seq_params, glo, ggo)


def _mix_out(ycat, x, w_out, gt_m, g_post, g_pre2, sc_f, sh_f, ts=512):
    s, d = x.shape

    def body(yc_ref, x_ref, w_ref, gt_ref, gp_ref, g2_ref, sc_ref, sh_ref, y_ref, x1_ref, h2_ref):
        for rs in _sub_tiles(ts):
            y = _dot(yc_ref[rs, :], w_ref[...])
            y_ref[rs, :] = y
            x1 = x_ref[rs, :] + gt_ref[...] * (y * _msq_rsqrt(y) * gp_ref[...])
            x1_ref[rs, :] = x1
            h2 = (x1 * _msq_rsqrt(x1) * g2_ref[...]) * (1.0 + sc_ref[...]) + sh_ref[...]
            h2_ref[rs, :] = h2.astype(BF16)

    vec = _full((1, d))
    return pl.pallas_call(
        body, grid=(s // ts,), name="mix_out",
        in_specs=[_rows(ts, d), _rows(ts, d), _full((d, d)), vec, vec, vec, vec, vec],
        out_specs=[_rows(ts, d), _rows(ts, d), _rows(ts, d)],
        out_shape=[_sds((s, d), F32), _sds((s, d), F32), _sds((s, d), BF16)],
        compiler_params=_params(("parallel",)),
    )(ycat, x, w_out, gt_m, g_post, g_pre2, sc_f, sh_f)


def _ffn_cols(j):
    per = (2 * D_FF // N_CHIPS) // FFN_CHUNK
    return j // per, (j % per) * FFN_CHUNK, j * FFN_CHUNK


def _ffn_fwd(h2, x1, tgt, w_up4, w_down, fw, fb, gt_f, g_post, ts=256):
    s, d = x1.shape
    nch = D_FF // FFN_CHUNK

    def body(h2_ref, x1_ref, tgt_ref, wup_ref, wdn_ref, fw_ref, fb_ref, gt_ref, gp_ref,
             up0_ref, pre_ref, act_ref, dy2_ref, dx2_ref, loss_ref, dgt_ref, dgp_ref, tail_ref):
        i = pl.program_id(0)

        @pl.when(i == 0)
        def _():
            tail_ref[...] = jnp.zeros_like(tail_ref)
            loss_ref[...] = jnp.zeros_like(loss_ref)
            dgt_ref[...] = jnp.zeros_like(dgt_ref)
            dgp_ref[...] = jnp.zeros_like(dgp_ref)

        hb = h2_ref[...]

        def up_project(j):
            sh_g, off, _ = _ffn_cols(j)
            return [_dot(hb, wup_ref[shard, :, off:off + FFN_CHUNK]) for shard in (sh_g, sh_g + 2)]

        y2 = jnp.zeros((ts, d), F32)
        ahead = up_project(0)
        for j in range(nch):
            _, _, col = _ffn_cols(j)
            ubs = ahead
            if j + 1 < nch:
                ahead = up_project(j + 1)
            halves = []
            for u, c0 in zip(ubs, (col, D_FF + col)):
                cs = slice(c0, c0 + FFN_CHUNK)
                up0_ref[:, cs] = u.astype(BF16)
                prev8 = tail_ref[:, cs]
                tail_ref[:, cs] = u[ts - SUBLANES:, :]
                halves.append(fw_ref[2:3, cs] * u + fw_ref[1:2, cs] * _shift_down(u, prev8, 1)
                              + fw_ref[0:1, cs] * _shift_down(u, prev8, 2) + fb_ref[:, cs])
                pre_ref[:, cs] = halves[-1].astype(BF16)
            act = (_gelu(halves[0]) * halves[1]).astype(BF16)
            act_ref[:, col:col + FFN_CHUNK] = act
            y2 = y2 + _dot(act, wdn_ref[col:col + FFN_CHUNK, :])
        r2 = _msq_rsqrt(y2)
        yn = y2 * r2
        yng = yn * gp_ref[...]
        e = x1_ref[...] + gt_ref[...] * yng - tgt_ref[...]
        loss_ref[...] += jnp.sum(e * e) * (0.5 / d)
        dx2 = e * (1.0 / d)
        dx2_ref[...] = dx2
        dgt_ref[...] += _colsum(dx2 * yng)
        dyng = dx2 * gt_ref[...]
        dgp_ref[...] += _colsum(dyng * yn)
        dy2_ref[...] = _rms_bwd(dyng * gp_ref[...], yn, r2).astype(BF16)

    vec = _full((1, d))
    return pl.pallas_call(
        body, grid=(s // ts,), name="ffn_fwd",
        in_specs=[_rows(ts, d), _rows(ts, d), _rows(ts, d), _RESIDENT, _RESIDENT,
                  _full((3, 2 * D_FF)), _full((1, 2 * D_FF)), vec, vec],
        out_specs=[_rows(ts, 2 * D_FF), _rows(ts, 2 * D_FF), _rows(ts, D_FF), _rows(ts, d), _rows(ts, d),
                   _full((1, 128)), vec, vec],
        out_shape=[_sds((s, 2 * D_FF), BF16), _sds((s, 2 * D_FF), BF16), _sds((s, D_FF), BF16), _sds((s, d), BF16),
                   _sds((s, d), F32), _sds((1, 128), F32), _sds((1, d), F32), _sds((1, d), F32)],
        scratch_shapes=[pltpu.VMEM((SUBLANES, 2 * D_FF), F32)],
        compiler_params=_params(("arbitrary",)),
    )(h2, x1, tgt, w_up4, w_down, fw, fb, gt_f, g_post)


def _shift_up_mxu(vb, up_mat, next8, k):
    t = vb.shape[0]
    main = _dot(up_mat, vb)
    tail = pltpu.roll(next8, SUBLANES - k, 0)
    row8 = lax.broadcasted_iota(jnp.int32, next8.shape, 0)
    last = main[t - SUBLANES:] + jnp.where(row8 >= SUBLANES - k, tail, 0.0)
    return jnp.concatenate([main[:t - SUBLANES], last], axis=0)


def _ffn_bwd_a(dy2, pre, up0, w_down, fw, ts=256):
    s, d = dy2.shape
    nt = s // ts
    nch = D_FF // FFN_CHUNK
    wide = 2 * D_FF
    up_mats = jnp.stack([jnp.eye(ts, k=1, dtype=BF16), jnp.eye(ts, k=2, dtype=BF16)])

    def body(dy2_ref, pre_ref, up0_ref, wdn_ref, fw_ref, um_ref, dup0_ref, dfw_ref, dfb_ref, next_ref):
        i = pl.program_id(0)

        @pl.when(i == 0)
        def _():
            next_ref[...] = jnp.zeros_like(next_ref)
            dfw_ref[...] = jnp.zeros_like(dfw_ref)
            dfb_ref[...] = jnp.zeros_like(dfb_ref)

        dyb = dy2_ref[...]
        for j in range(nch):
            _, _, col = _ffn_cols(j)
            dact = _dot_nt(dyb, wdn_ref[col:col + FFN_CHUNK, :])
            gl, dgl = _gelu_and_grad(pre_ref[:, col:col + FFN_CHUNK].astype(F32))
            dpre = (dact * pre_ref[:, D_FF + col:D_FF + col + FFN_CHUNK].astype(F32) * dgl, dact * gl)
            for half, c0 in enumerate((col, D_FF + col)):
                cs = slice(c0, c0 + FFN_CHUNK)
                dp = dpre[half]
                dpb = dp.astype(BF16)
                nxt = next_ref[:, cs]
                next_ref[:, cs] = dpb.astype(F32)[0:SUBLANES, :]
                su1 = _shift_up_mxu(dpb, um_ref[0], nxt, 1)
                su2 = _shift_up_mxu(dpb, um_ref[1], nxt, 2)
                u = up0_ref[:, cs].astype(F32)
                dfb_ref[:, cs] += _colsum(dp)
                dfw_ref[2:3, cs] += _colsum(dp * u)
                dfw_ref[1:2, cs] += _colsum(su1 * u)
                dfw_ref[0:1, cs] += _colsum(su2 * u)
                dup0 = fw_ref[2:3, cs] * dp + fw_ref[1:2, cs] * su1 + fw_ref[0:1, cs] * su2
                dup0_ref[:, cs] = dup0.astype(BF16)

    return pl.pallas_call(
        body, grid=(nt,), name="ffn_bwd_a",
        in_specs=[_rows(ts, d, nt), _rows(ts, wide, nt), _rows(ts, wide, nt), _RESIDENT,
                  _full((3, wide)), _full((2, ts, ts))],
        out_specs=[_rows(ts, wide, nt), _full((3, wide)), _full((1, wide))],
        out_shape=[_sds((s, wide), BF16), _sds((3, wide), F32), _sds((1, wide), F32)],
        scratch_shapes=[pltpu.VMEM((SUBLANES, wide), F32)],
        compiler_params=_params(("arbitrary",)),
    )(dy2, pre, up0, w_down, fw, up_mats)


def _ffn_bwd_b(dup0, x1, y, dx2, w_up4, g_pre2, sc_f, sh_f, gt_m, g_post_m, ts=512):
    s, d = x1.shape
    shard_cols = 2 * D_FF // N_CHIPS

    def body(dup_ref, x1_ref, y_ref, dx2_ref, wup_ref, g2_ref, sc_ref, sh_ref, gt_ref, gp_ref,
             dx1_ref, dy_ref, dsh_ref, dsc_ref, dg2_ref, dgt_ref, dgp_ref):
        i = pl.program_id(0)

        @pl.when(i == 0)
        def _():
            for ref in (dsh_ref, dsc_ref, dg2_ref, dgt_ref, dgp_ref):
                ref[...] = jnp.zeros_like(ref)

        for rs in _sub_tiles(ts):
            dh2 = jnp.zeros((SUB_ROWS, d), F32)
            for k in range(N_CHIPS):
                dh2 = dh2 + _dot_nt(dup_ref[rs, k * shard_cols:(k + 1) * shard_cols], wup_ref[k])
            x1v = x1_ref[rs, :]
            r2 = _msq_rsqrt(x1v)
            xn = x1v * r2
            hn = xn * g2_ref[...]
            dsh_ref[...] += _colsum(dh2)
            dsc_ref[...] += _colsum(dh2 * hn)
            dhn = dh2 * (1.0 + sc_ref[...])
            dg2_ref[...] += _colsum(dhn * xn)
            dx1 = dx2_ref[rs, :] + _rms_bwd(dhn * g2_ref[...], xn, r2)
            dx1_ref[rs, :] = dx1
            yv = y_ref[rs, :]
            ry = _msq_rsqrt(yv)
            yn = yv * ry
            dgt_ref[...] += _colsum(dx1 * (yn * gp_ref[...]))
            dyng = dx1 * gt_ref[...]
            dgp_ref[...] += _colsum(dyng * yn)
            dy_ref[rs, :] = _rms_bwd(dyng * gp_ref[...], yn, ry).astype(BF16)

    vec = _full((1, d))
    return pl.pallas_call(
        body, grid=(s // ts,), name="ffn_bwd_b",
        in_specs=[_rows(ts, 2 * D_FF), _rows(ts, d), _rows(ts, d), _rows(ts, d), _RESIDENT,
                  vec, vec, vec, vec, vec],
        out_specs=[_rows(ts, d), _rows(ts, d), vec, vec, vec, vec, vec],
        out_shape=[_sds((s, d), F32), _sds((s, d), BF16)] + [_sds((1, d), F32)] * 5,
        compiler_params=_params(("arbitrary",)),
    )(dup0, x1, y, dx2, w_up4, g_pre2, sc_f, sh_f, gt_m, g_post_m)


def _seqmix_bwd(lru_x, hst, stash, dy, w_out, seq_params, ws_t, glo, ggo, x, dx1, w_in4, g_pre, sc_m, ts=256):
    s, d = x.shape
    nt = s // ts
    small_shapes = [(4, 512), (1, 512), (512, 512), (512, 512), (1, 512), (1, 512), (1, 512),
                    (1, 512), (1, 512), (4, 128, 128), (128, 4), (1, 512), (1, 512),
                    (1, d), (1, d), (1, d)]

    def body(lx_ref, hst_ref, hprev_ref, st_ref, dy_ref, wout_ref, cw_ref, cb_ref, bdr_ref, bdi_ref, br_ref,
             bi_ref, la_ref, ng_ref, nb_ref, ws_ref, bst_ref, wst_ref, glo_ref, ggo_ref, x_ref, dx1_ref, win_ref,
             gpre_ref, scm_ref, dz_ref, gx_ref, *rest):
        small_refs = rest[:16]
        (dcw_ref, dcb_ref, dwr_ref, dwi_ref, dbr_ref, dbi_ref, dspa_ref, dng_ref, dnb_ref, dws_ref, dbs_ref,
         dglo_ref, dggo_ref, dsh_ref, dsc_ref, dgpre_ref) = small_refs
        gcarry, anext, dxcnext, dv_scr = rest[16:]
        i = pl.program_id(0)

        @pl.when(i == 0)
        def _():
            for ref in small_refs:
                ref[...] = jnp.zeros_like(ref)
            gcarry[...] = jnp.zeros_like(gcarry)
            anext[...] = jnp.ones_like(anext)
            dxcnext[...] = jnp.zeros_like(dxcnext)

        first_tile = i == nt - 1
        xc, r, ig, a, mult = st_ref[ST_XC], st_ref[ST_R], st_ref[ST_IG], st_ref[ST_A], st_ref[ST_MULT]
        gl, u, spb, vhat = st_ref[ST_GL], st_ref[ST_U], st_ref[ST_SPB], st_ref[ST_VHAT]
        lx = lx_ref[...]
        h = hst_ref[...]
        hprev = _shift_down(h, jnp.where(first_tile, 0.0, hprev_ref[...]), 1)
        y_l = h * gl
        y_g = u * spb

        dycat = _dot_nt(dy_ref[...], wout_ref[...])

        dz_parts = {}

        def emit_dz(k, val):
            dz_parts[k] = val.astype(BF16)
            dz_ref[:, k * 512:(k + 1) * 512] = dz_parts[k]

        rl = _msq_rsqrt(y_l)
        yln = y_l * rl
        dyl = dycat[:, 0:512]
        dglo_ref[...] += _colsum(dyl * yln)
        dy_l = _rms_bwd(dyl * glo_ref[...], yln, rl)
        rg = _msq_rsqrt(y_g)
        ygn = y_g * rg
        dyg = dycat[:, 512:1024]
        dggo_ref[...] += _colsum(dyg * ygn)
        dy_g = _rms_bwd(dyg * ggo_ref[...], ygn, rg)

        emit_dz(1, dy_l * h * st_ref[ST_DGL])
        a_up = _shift_up(a, anext[...], 1)
        acum, gloc = _scan_bwd(a_up, dy_l * gl)
        gg = gloc + acum * gcarry[...]
        gcarry[...] = gg[0:1, :]
        anext[...] = a[0:SUBLANES, :]
        da = gg * hprev
        t1 = gg * mult
        di = t1 * xc
        dxc = t1 * ig
        dmult = gg * ig * xc
        dla = da * a - dmult * (a * a / mult)
        dspa_ref[...] += _colsum(dla * r) * (-LRU_C)
        dpr = dla * ((-LRU_C) * _softplus(-la_ref[...])) * r * (1.0 - r)
        dpi = di * ig * (1.0 - ig)
        dbr_ref[...] += _colsum(dpr)
        dbi_ref[...] += _colsum(dpi)
        dprb = dpr.astype(BF16)
        dpib = dpi.astype(BF16)
        xcb = xc.astype(BF16)
        dwr_ref[...] += _dot_tn(xcb, dprb)
        dwi_ref[...] += _dot_tn(xcb, dpib)
        dxc = dxc + _dot_nt(dprb, bdr_ref[...]) + _dot_nt(dpib, bdi_ref[...])
        nxt = dxcnext[...]
        dxcnext[...] = dxc[0:SUBLANES, :]
        up1, up2, up3 = _shift_up(dxc, nxt, 1), _shift_up(dxc, nxt, 2), _shift_up(dxc, nxt, 3)
        dcb_ref[...] += _colsum(dxc)
        dcw_ref[3:4, :] += _colsum(dxc * lx)
        dcw_ref[2:3, :] += _colsum(up1 * lx)
        dcw_ref[1:2, :] += _colsum(up2 * lx)
        dcw_ref[0:1, :] += _colsum(up3 * lx)
        dlx = cw_ref[3:4, :] * dxc + cw_ref[2:3, :] * up1 + cw_ref[1:2, :] * up2 + cw_ref[0:1, :] * up3
        emit_dz(0, dlx)

        emit_dz(2, dy_g * spb * st_ref[ST_DU])
        dsp = dy_g * u
        vb = (vhat * ng_ref[...] + nb_ref[...]).astype(BF16)
        for n in range(ts // GMLP_BLOCK):
            rs = slice(n * GMLP_BLOCK, (n + 1) * GMLP_BLOCK)
            for g in range(GMLP_GROUPS):
                cs = slice(g * 128, (g + 1) * 128)
                dbs_ref[:, g:g + 1] += jnp.sum(dsp[rs, cs], axis=1, keepdims=True)
                blk = dsp[rs, cs].astype(BF16)
                dws_ref[g] += _dot_nt(blk, vb[rs, cs])
                dv_scr[rs, cs] = _dot(wst_ref[g], blk)
        dv = dv_scr[...]
        dng_ref[...] += _colsum(dv * vhat)
        dnb_ref[...] += _colsum(dv)
        dvh = dv * ng_ref[...]
        dvg = dvh - jnp.mean(dvh, axis=-1, keepdims=True) - vhat * jnp.mean(dvh * vhat, axis=-1, keepdims=True)
        emit_dz(3, dvg * st_ref[ST_Q])

        dh = _dot_nt(dz_parts[0], win_ref[0])
        for k in range(1, N_CHIPS):
            dh = dh + _dot_nt(dz_parts[k], win_ref[k])
        xv = x_ref[...]
        rx = _msq_rsqrt(xv)
        xn = xv * rx
        dsh_ref[...] += _colsum(dh)
        dsc_ref[...] += _colsum(dh * (xn * gpre_ref[...]))
        dhn = dh * (1.0 + scm_ref[...])
        dgpre_ref[...] += _colsum(dhn * xn)
        gx_ref[...] = dx1_ref[...] + _rms_bwd(dhn * gpre_ref[...], xn, rx)

        @pl.when(i == nt - 1)
        def _():
            pos = lax.broadcasted_iota(jnp.int32, (GMLP_BLOCK, GMLP_BLOCK), 0) // CHUNK
            src = lax.broadcasted_iota(jnp.int32, (GMLP_BLOCK, GMLP_BLOCK), 1) // CHUNK
            for g in range(GMLP_GROUPS):
                dws_ref[g] = jnp.where(src <= pos, dws_ref[g], 0.0)
            dspa_ref[...] = dspa_ref[...] * (-_sigmoid(-la_ref[...]))

    vec = _full((1, d))
    in_specs = ([_rows(ts, 512, nt), _rows(ts, 512, nt), _halo_prev(ts, 512, SUBLANES, nt),
                 pl.BlockSpec((N_STASH, ts, 512), lambda i: (0, nt - 1 - i, 0)), _rows(ts, d, nt),
                 _full((d, d))]
                + _seq_param_specs() + [_full((4, 128, 128)), _full((1, 512)), _full((1, 512))]
                + [_rows(ts, d, nt), _rows(ts, d, nt), _full(w_in4.shape), vec, vec])
    return pl.pallas_call(
        body, grid=(nt,), name="seqmix_bwd",
        in_specs=in_specs,
        out_specs=[_rows(ts, 2048, nt), _rows(ts, d, nt)] + [_full(sh) for sh in small_shapes],
        out_shape=[_sds((s, 2048), BF16), _sds((s, d), F32)] + [_sds(sh, F32) for sh in small_shapes],
        scratch_shapes=[pltpu.VMEM((1, 512), F32), pltpu.VMEM((SUBLANES, 512), F32),
                        pltpu.VMEM((SUBLANES, 512), F32), pltpu.VMEM((ts, 512), F32)],
        compiler_params=_params(("arbitrary",)),
    )(lru_x, hst, hst, stash, dy, w_out, *seq_params, ws_t, glo, ggo, x, dx1, w_in4, g_pre, sc_m)


def _wgrad(a, b, n_chunks, name, chunk_major, ts=2048):
    s, m = a.shape
    n = b.shape[1]
    nc = n // n_chunks
    nt = s // ts

    def body(a_ref, b_ref, o_ref, acc):
        i = pl.program_id(1)

        @pl.when(i == 0)
        def _():
            acc[...] = jnp.zeros_like(acc)

        acc[...] += _dot_tn(a_ref[...], b_ref[...])

        @pl.when(i == nt - 1)
        def _():
            if chunk_major:
                o_ref[0] = acc[...].astype(BF16)
            else:
                o_ref[...] = acc[...].astype(BF16)

    if chunk_major:
        out_spec, out_shape = pl.BlockSpec((1, m, nc), lambda c, i: (c, 0, 0)), _sds((n_chunks, m, nc), BF16)
    else:
        out_spec, out_shape = pl.BlockSpec((m, nc), lambda c, i: (0, c)), _sds((m, n), BF16)
    return pl.pallas_call(
        body, grid=(n_chunks, nt), name=name,
        in_specs=[pl.BlockSpec((ts, m), lambda c, i: (i, 0)), pl.BlockSpec((ts, nc), lambda c, i: (i, c))],
        out_specs=out_spec,
        out_shape=out_shape,
        scratch_shapes=[pltpu.VMEM((m, nc), F32)],
        compiler_params=_params(("parallel", "arbitrary")),
    )(a, b)


def _block_diag(w):
    heads, hd, _ = w.shape
    eye = jnp.eye(heads, dtype=w.dtype)
    return (eye[:, None, :, None] * w[:, :, None, :]).reshape(heads * hd, heads * hd)


def _seq_params(small):
    row = lambda v: v.reshape(1, -1)
    pos = jnp.arange(GMLP_BLOCK)
    mask = (pos[None, :] // CHUNK) <= (pos[:, None] // CHUNK)
    ws = jnp.where(mask[None], small["w_spatial"], 0.0)
    seq_params = (small["conv_w"], row(small["conv_b"]),
                  _block_diag(small["w_rgate"]).astype(BF16), _block_diag(small["w_igate"]).astype(BF16),
                  row(small["b_rgate"]), row(small["b_igate"]), row(small["lru_a"]),
                  row(small["v_norm_g"]), row(small["v_norm_b"]), ws.astype(BF16), small["b_spatial"].T)
    return seq_params, jnp.swapaxes(ws, 1, 2).astype(BF16)


_ANY = pl.BlockSpec(memory_space=pl.ANY)
_CHIP_FLIPS = ((1, 0), (0, 1), (1, 1))


def _position():
    return lax.axis_index("x"), lax.axis_index("y"), lax.axis_index("c")


def _flip(v, f):
    return 1 - v if f else v


def _remote(src, dst, send_sem, recv_sem, peer):
    return pltpu.make_async_remote_copy(src_ref=src, dst_ref=dst, send_sem=send_sem, recv_sem=recv_sem,
                                        device_id=peer, device_id_type=MESH)


def _allgather8(block, name):
    r, n = block.shape

    def body(x_ref, gath, send_sems, recv_sems, loc_sem):
        x, y, c = _position()
        me = 4 * x + 2 * y + c
        loc = pltpu.make_async_copy(x_ref, gath.at[me], loc_sem)
        loc.start()
        peers = []
        for k in range(1, N_DEV):
            px, py, pc = _flip(x, k & 4), _flip(y, k & 2), _flip(c, k & 1)
            peers.append((px, py, pc))
            _remote(x_ref, gath.at[me], send_sems.at[k - 1], recv_sems.at[k - 1], (px, py, pc)).start()
        for k, (px, py, pc) in enumerate(peers):
            src = 4 * px + 2 * py + pc
            _remote(x_ref, gath.at[src], send_sems.at[k], recv_sems.at[k], (px, py, pc)).wait_recv()
        for k, peer in enumerate(peers):
            _remote(x_ref, gath.at[me], send_sems.at[k], recv_sems.at[k], peer).wait_send()
        loc.wait()

    return pl.pallas_call(
        body, name=name, out_shape=_sds((N_DEV, r, n), F32),
        in_specs=[pl.BlockSpec(memory_space=pltpu.VMEM)], out_specs=pl.BlockSpec(memory_space=pltpu.VMEM),
        scratch_shapes=[pltpu.SemaphoreType.DMA((N_DEV - 1,)), pltpu.SemaphoreType.DMA((N_DEV - 1,)),
                        pltpu.SemaphoreType.DMA],
        compiler_params=pltpu.CompilerParams(vmem_limit_bytes=VMEM_LIMIT_BYTES),
    )(block)


def _half(ref, c, rows):
    hr = rows // 2
    return ref.at[pl.ds(pl.multiple_of(c * hr, BF16_SUBLANES), hr), :]


def _chip_sum(part, recv, pos_arr, name):
    _, rows, cols = part.shape
    hr = rows // 2

    def body(pos_ref, p_ref, r_ref, o_ref, g_ref):
        total = (p_ref[...].astype(F32) + r_ref[...].astype(F32)).astype(BF16)
        o_ref[...] = total

        @pl.when(pl.program_id(0) == pos_ref[1])
        def _():
            g_ref[0] = total

    grid_spec = pltpu.PrefetchScalarGridSpec(
        num_scalar_prefetch=1, grid=(N_CHIPS,),
        in_specs=[pl.BlockSpec((1, hr, cols), lambda k, pos: (k, pos[0], 0)),
                  pl.BlockSpec((1, hr, cols), lambda k, pos: (k, 0, 0))],
        out_specs=[pl.BlockSpec((1, hr, cols), lambda k, pos: (k, 0, 0)),
                   pl.BlockSpec((1, 1, hr, cols), lambda k, pos: (0, pos[1], 0, 0))])
    return pl.pallas_call(
        body, name=name, grid_spec=grid_spec,
        out_shape=[_sds((N_CHIPS, hr, cols), BF16), _sds((2, N_CHIPS, hr, cols), BF16)],
        compiler_params=_params(("arbitrary",)),
    )(pos_arr, part, recv)


_HBM = pl.BlockSpec(memory_space=pltpu.HBM)
_SEM = pl.BlockSpec(memory_space=pltpu.SEMAPHORE)
_EFFECT = pltpu.SideEffectType.DATAFLOW_SIDE_EFFECTING


def _in_hbm(a):
    return pltpu.with_memory_space_constraint(a, pltpu.HBM)


def _split_start(srcs, lands, plan, n_copies, after, name):
    ns, nl = len(srcs), len(lands)
    bufs = list(srcs) + list(lands)

    def body(*refs):
        send_sems, recv_sems = refs[ns + nl + 1], refs[ns + nl + 2]
        token = refs[-1]
        for k, (src, dst, peer) in enumerate(plan(refs[:ns], refs[ns:ns + nl])):
            _remote(src, dst, send_sems.at[k], recv_sems.at[k], peer).start()
        token[...] = jnp.zeros_like(token)

    out = pl.pallas_call(
        body, name=name,
        out_shape=(pltpu.SemaphoreType.DMA((n_copies,)), pltpu.SemaphoreType.DMA((n_copies,)),
                   *[pltpu.HBM(b.shape, b.dtype) for b in bufs], _sds((SUBLANES, 128), F32)),
        in_specs=[_HBM] * (ns + nl) + [_ANY],
        out_specs=(_SEM, _SEM, *[_HBM] * (ns + nl), pl.BlockSpec(memory_space=pltpu.VMEM)),
        input_output_aliases={i: 2 + i for i in range(ns + nl)},
        compiler_params=pltpu.CompilerParams(has_side_effects=_EFFECT),
    )(*[_in_hbm(b) for b in bufs], after)
    return out[0], out[1], list(out[2:2 + ns]), list(out[2 + ns:2 + ns + nl]), out[-1]


def _split_wait(send_sems, recv_sems, srcs, lands, plan, after, name):
    ns, nl = len(srcs), len(lands)
    bufs = list(srcs) + list(lands)

    def body(*refs):
        send_ref, recv_ref = refs[ns + nl], refs[ns + nl + 1]
        me = _position()
        for k, src, dst in plan(refs[:ns], refs[ns:ns + nl]):
            cp = _remote(src, dst, send_ref.at[k], recv_ref.at[k], me)
            cp.wait_send()
            cp.wait_recv()

    out = pl.pallas_call(
        body, name=name,
        out_shape=[pltpu.HBM(b.shape, b.dtype) for b in bufs],
        in_specs=[_HBM] * (ns + nl) + [_SEM, _SEM, _ANY],
        out_specs=[_HBM] * (ns + nl),
        input_output_aliases={i: i for i in range(ns + nl)},
        compiler_params=pltpu.CompilerParams(has_side_effects=_EFFECT),
    )(*bufs, send_sems, recv_sems, after)
    return list(out[:ns]), list(out[ns:])


def _gather_plan(rows_of):
    def start(src_refs, land_refs):
        x, y, c = _position()
        chip = 2 * x + y
        out = []
        for a, rows in enumerate(rows_of):
            mine = _half(land_refs[a].at[chip], c, rows)
            out.extend((mine, mine, (_flip(x, fx), _flip(y, fy), c)) for fx, fy in _CHIP_FLIPS)
        return out

    def wait(src_refs, land_refs):
        x, y, c = _position()
        chip = 2 * x + y
        out = []
        for a, rows in enumerate(rows_of):
            for j, (fx, fy) in enumerate(_CHIP_FLIPS):
                src_chip = 2 * _flip(x, fx) + _flip(y, fy)
                out.append((3 * a + j, _half(land_refs[a].at[chip], c, rows),
                            _half(land_refs[a].at[src_chip], c, rows)))
        return out

    return start, wait


def _forward_plan(rows_of):
    def pieces(land_refs, half):
        x, y, _ = _position()
        return [_half(land_refs[a].at[2 * _flip(x, fx) + _flip(y, fy)], half, rows)
                for a, rows in enumerate(rows_of) for fx, fy in _CHIP_FLIPS]

    def start(src_refs, land_refs):
        x, y, c = _position()
        return [(p, p, (x, y, 1 - c)) for p in pieces(land_refs, c)]

    def wait(src_refs, land_refs):
        _, _, c = _position()
        return [(k, mine, theirs)
                for k, (mine, theirs) in enumerate(zip(pieces(land_refs, c), pieces(land_refs, 1 - c)))]

    return start, wait


def _swap_halves_plan(half_rows):
    def slices(src_refs, c):
        return [src_refs[a].at[:, pl.ds(pl.multiple_of((1 - c) * hr, BF16_SUBLANES), hr), :]
                for a, hr in enumerate(half_rows)]

    def start(src_refs, land_refs):
        x, y, c = _position()
        return [(src, land_refs[a], (x, y, 1 - c)) for a, src in enumerate(slices(src_refs, c))]

    def wait(src_refs, land_refs):
        _, _, c = _position()
        return [(a, src, land_refs[a]) for a, src in enumerate(slices(src_refs, c))]

    return start, wait


def _swap_gathered_plan(n_arrays):
    def start(src_refs, land_refs):
        x, y, c = _position()
        return [(land_refs[a].at[0], land_refs[a].at[1], (x, y, 1 - c)) for a in range(n_arrays)]

    def wait(src_refs, land_refs):
        return [(a, land_refs[a].at[0], land_refs[a].at[1]) for a in range(n_arrays)]

    return start, wait


def _exchange_plan(n_arrays):
    def start(src_refs, land_refs):
        x, y, c = _position()
        chip = 2 * x + y
        out = []
        for a in range(n_arrays):
            for fx, fy in _CHIP_FLIPS:
                px, py = _flip(x, fx), _flip(y, fy)
                out.append((src_refs[a].at[2 * px + py], land_refs[a].at[0, chip], (px, py, c)))
        return out

    def wait(src_refs, land_refs):
        x, y, c = _position()
        out = []
        for a in range(n_arrays):
            for j, (fx, fy) in enumerate(_CHIP_FLIPS):
                src_chip = 2 * _flip(x, fx) + _flip(y, fy)
                out.append((3 * a + j, src_refs[a].at[src_chip], land_refs[a].at[0, src_chip]))
        return out

    return start, wait


def _adam_gathered(w, gath, m, v, c_arr, after, name, tr=128):
    rows, cols = w.shape
    hr = rows // 2
    if hr % (2 * tr) == 0:
        tr = 2 * tr
    per = hr // tr

    def body(c_ref, w_ref, g_ref, m_ref, v_ref, after_ref, go_ref, d_ref, nm_ref, nv_ref):
        g = g_ref[0, 0].astype(F32)
        for k in range(1, N_CHIPS):
            g = g + g_ref[0, k].astype(F32)
        go_ref[...] = g
        d_ref[...], nm_ref[...], nv_ref[...] = _adam_math(w_ref[...], g, m_ref[...], v_ref[...])

    def rows_of(h, i, c_ref):
        c = c_ref[0]
        return ((c + h - 2 * c * h) * per + i, 0)

    blk = pl.BlockSpec((tr, cols), rows_of)
    grid_spec = pltpu.PrefetchScalarGridSpec(
        num_scalar_prefetch=1, grid=(2, per),
        in_specs=[blk, pl.BlockSpec((1, N_CHIPS, tr, cols), lambda h, i, c_ref: (h, 0, i, 0)), blk, blk, _ANY],
        out_specs=[blk] * 4)
    return pl.pallas_call(
        body, name=name, grid_spec=grid_spec, out_shape=[_sds(w.shape, F32)] * 4,
        compiler_params=_params(("arbitrary", "arbitrary")),
    )(c_arr, w, gath, m, v, after)


def _allreduce_small(block, name):
    two, r, n = block.shape
    assert two == 2

    def body(x_ref, out_ref, sib, chipsum, gath, d2d_send, d2d_recv, ici_send, ici_recv):
        x, y, c = _position()
        chip = 2 * x + y
        sibling = (x, y, 1 - c)
        first = _remote(x_ref, sib, d2d_send.at[0], d2d_recv.at[0], sibling)
        first.start()
        first.wait()
        chipsum[...] = x_ref[...] + sib[...]
        sends = []
        for j, (fx, fy) in enumerate(_CHIP_FLIPS):
            sends.append(_remote(chipsum.at[c], gath.at[chip], ici_send.at[j], ici_recv.at[j],
                                 (_flip(x, fx), _flip(y, fy), c)))
            sends[-1].start()
        gath[chip] = chipsum[c]
        for j, (fx, fy) in enumerate(_CHIP_FLIPS):
            landed = gath.at[2 * _flip(x, fx) + _flip(y, fy)]
            _remote(landed, landed, ici_send.at[j], ici_recv.at[j], sibling).wait_recv()
        for cp in sends:
            cp.wait_send()
        total = gath[0]
        for k in range(1, N_CHIPS):
            total = total + gath[k]
        out_ref[c] = total
        last = _remote(out_ref.at[c], out_ref.at[c], d2d_send.at[1], d2d_recv.at[1], sibling)
        last.start()
        _remote(out_ref.at[1 - c], out_ref.at[1 - c], d2d_send.at[1], d2d_recv.at[1], sibling).wait_recv()
        last.wait_send()

    vmem = pl.BlockSpec(memory_space=pltpu.VMEM)
    return pl.pallas_call(
        body, name=name, out_shape=_sds(block.shape, F32), in_specs=[vmem], out_specs=vmem,
        scratch_shapes=[pltpu.VMEM(block.shape, F32), pltpu.VMEM(block.shape, F32), pltpu.VMEM((N_CHIPS, r, n), F32),
                        pltpu.SemaphoreType.DMA((2,)), pltpu.SemaphoreType.DMA((2,)),
                        pltpu.SemaphoreType.DMA((3,)), pltpu.SemaphoreType.DMA((3,))],
        compiler_params=pltpu.CompilerParams(vmem_limit_bytes=VMEM_LIMIT_BYTES),
    )(block)


def _cast_place(shards, chip_arr, name):
    na = len(shards)
    steps = 4

    def body(chip_ref, *refs):
        for a in range(na):
            refs[na + a][0] = refs[a][...].astype(BF16)

    grid_spec = pltpu.PrefetchScalarGridSpec(
        num_scalar_prefetch=1, grid=(steps,),
        in_specs=[pl.BlockSpec((s.shape[0] // steps, s.shape[1]), lambda i, ch: (i, 0)) for s in shards],
        out_specs=[pl.BlockSpec((1, s.shape[0] // steps, s.shape[1]), lambda i, ch: (ch[0], i, 0)) for s in shards])
    return pl.pallas_call(
        body, name=name, grid_spec=grid_spec,
        out_shape=[_sds((N_CHIPS,) + s.shape, BF16) for s in shards],
        compiler_params=_params(("arbitrary",)),
    )(chip_arr, *shards)


def _silu(v):
    return v * _sigmoid(v)


def _ada_fwd(c8, w_ada):
    def body(c_ref, w_ref, o_ref):
        o_ref[...] = jnp.dot(_silu(c_ref[...]), w_ref[...], preferred_element_type=F32,
                             precision=lax.Precision.HIGHEST)

    return pl.pallas_call(
        body, name="ada_fwd", out_shape=_sds((N_DEV, w_ada.shape[1]), F32),
        compiler_params=pltpu.CompilerParams(vmem_limit_bytes=VMEM_LIMIT_BYTES),
    )(c8, w_ada)


def _mod_select(parts, b_ada, me_arr, after):
    cols = parts.shape[2]

    def body(me_ref, p_ref, b_ref, after_ref, o_ref):
        me = me_ref[0]
        for k in range(N_CHIPS):
            cs = slice(k * cols, (k + 1) * cols)
            o_ref[:, cs] = p_ref[2 * k, pl.ds(me, 1), :] + b_ref[:, cs]

    grid_spec = pltpu.PrefetchScalarGridSpec(
        num_scalar_prefetch=1, grid=(1,),
        in_specs=[pl.BlockSpec(parts.shape, lambda i, m: (0, 0, 0)), pl.BlockSpec(b_ada.shape, lambda i, m: (0, 0)),
                  _ANY],
        out_specs=pl.BlockSpec(b_ada.shape, lambda i, m: (0, 0)))
    return pl.pallas_call(body, name="mod_select", grid_spec=grid_spec, out_shape=_sds(b_ada.shape, F32))(
        me_arr, parts, b_ada, after)


def _ada_bwd(c8, dmod8, chip_arr, w, m, v, tr=512):
    d = c8.shape[1]
    cols = dmod8.shape[1] // N_CHIPS

    def body(chip_ref, c_ref, dm_ref, dmall_ref, w_ref, m_ref, v_ref, gw_ref, d_ref, nm_ref, nv_ref, gb_ref):
        g = lax.dot_general(_silu(c_ref[...]), dm_ref[...], (((0,), (0,)), ((), ())),
                            preferred_element_type=F32, precision=lax.Precision.HIGHEST)
        gw_ref[...] = g
        d_ref[...], nm_ref[...], nv_ref[...] = _adam_math(w_ref[...], g, m_ref[...], v_ref[...])
        acc = dmall_ref[0:1, :]
        for k in range(1, N_DEV):
            acc = acc + dmall_ref[k:k + 1, :]
        gb_ref[...] = acc

    rows = pl.BlockSpec((tr, cols), lambda i, ch: (i, 0))
    grid_spec = pltpu.PrefetchScalarGridSpec(
        num_scalar_prefetch=1, grid=(d // tr,),
        in_specs=[pl.BlockSpec((N_DEV, tr), lambda i, ch: (0, i)),
                  pl.BlockSpec((N_DEV, cols), lambda i, ch: (0, ch[0])),
                  pl.BlockSpec(dmod8.shape, lambda i, ch: (0, 0)), rows, rows, rows],
        out_specs=[rows] * 4 + [pl.BlockSpec((1, dmod8.shape[1]), lambda i, ch: (0, 0))])
    return pl.pallas_call(
        body, name="ada_bwd", grid_spec=grid_spec,
        out_shape=[_sds((d, cols), F32)] * 4 + [_sds((1, dmod8.shape[1]), F32)],
        compiler_params=_params(("arbitrary",)),
    )(chip_arr, c8, dmod8, dmod8, w, m, v)


def _adam_math(w, g, m, v):
    m = ADAM_B1 * m + (1.0 - ADAM_B1) * g
    v = ADAM_B2 * v + (1.0 - ADAM_B2) * (g * g)
    m_hat = m / (1.0 - ADAM_B1 ** ADAM_STEP)
    v_hat = v / (1.0 - ADAM_B2 ** ADAM_STEP)
    delta = -ADAM_LR * (m_hat / (jnp.sqrt(v_hat) + ADAM_EPS) + ADAM_WD * w)
    return delta, m, v


def _adam(w, g, m, v, name, tr=256):
    rows, cols = w.shape
    if rows % tr:
        tr = rows

    def body(w_ref, g_ref, m_ref, v_ref, d_ref, nm_ref, nv_ref):
        d_ref[...], nm_ref[...], nv_ref[...] = _adam_math(w_ref[...], g_ref[...], m_ref[...], v_ref[...])

    spec = pl.BlockSpec((tr, cols), lambda i: (i, 0))
    return pl.pallas_call(
        body, name=name, grid=(rows // tr,), in_specs=[spec] * 4, out_specs=[spec] * 3,
        out_shape=[_sds(w.shape, F32)] * 3, compiler_params=_params(("parallel",)),
    )(w, g, m, v)


SMALL_REPLICATED = ("g_mix_pre", "g_mix_post", "conv_b", "w_rgate", "b_rgate", "w_igate", "b_igate", "lru_a",
                    "v_norm_g", "v_norm_b", "w_spatial", "b_spatial", "g_lru_out", "g_gmlp_out", "g_ffn_pre",
                    "g_ffn_post", "ffn_conv_b")
SMALL_COLUMN_SHARDED = ("conv_w", "ffn_conv_w")

SMALL_ROW_LEN = 86016
_SMALL_ROWS = (
    (("ffn_conv_w", 18432), ("conv_w", 2048), ("w_spatial", 65536)),
    (("w_rgate", 32768), ("w_igate", 32768), ("ffn_conv_b", 6144), ("g_mix_pre", 1024), ("g_mix_post", 1024),
     ("g_ffn_pre", 1024), ("g_ffn_post", 1024), ("conv_b", 512), ("b_rgate", 512), ("b_igate", 512),
     ("lru_a", 512), ("v_norm_g", 512), ("v_norm_b", 512), ("b_spatial", 512), ("g_lru_out", 512),
     ("g_gmlp_out", 512), ("loss", 128)),
)


def _small_slots():
    slots = {}
    for row, entries in enumerate(_SMALL_ROWS):
        off = 0
        for name, size in entries:
            slots[name] = (row, off)
            off += size
        assert off <= SMALL_ROW_LEN
    return slots


SMALL_SLOT = _small_slots()
SMALL_LANES = SMALL_ROW_LEN // SUBLANES


def _small_pieces(name, first, count):
    row, off = SMALL_SLOT[name]
    pos, pieces = off + first, []
    while count:
        sub, lane = divmod(pos, SMALL_LANES)
        n = min(count, SMALL_LANES - lane)
        pieces.append((row, sub, lane, n))
        pos, count = pos + n, count - n
    return pieces
ROW_VECTORS = ("ffn_conv_b", "g_mix_pre", "g_mix_post", "g_ffn_pre", "g_ffn_post", "conv_b", "lru_a", "v_norm_g",
               "v_norm_b", "g_lru_out", "g_gmlp_out")
HEAD_DIM = LRU_WIDTH // LRU_HEADS


def _pack_small(g, after):
    order = ("ffn_conv_w", "conv_w", "w_spatial", "w_rgate", "w_igate", "b_rgate", "b_igate", "b_spatial", "loss") \
        + ROW_VECTORS
    vmem = pl.BlockSpec(memory_space=pltpu.VMEM)

    def body(*refs):
        src = dict(zip(order, refs))
        out_ref = refs[len(order) + 1]
        out_ref[...] = jnp.zeros_like(out_ref)

        def put(name, first, val):
            col = 0
            for row, sub, lane, n in _small_pieces(name, first, val.shape[1]):
                out_ref[row, sub:sub + 1, lane:lane + n] = val[:, col:col + n]
                col += n

        for name in ROW_VECTORS + ("b_rgate", "b_igate", "loss"):
            put(name, 0, src[name][...])
        for name in ("ffn_conv_w", "conv_w"):
            k_taps, n = src[name].shape
            for k in range(k_taps):
                put(name, k * n, src[name][k:k + 1, :])
        for g_idx in range(GMLP_GROUPS):
            for i in range(GMLP_BLOCK):
                put("w_spatial", (g_idx * GMLP_BLOCK + i) * GMLP_BLOCK, src["w_spatial"][g_idx, i:i + 1, :])
        for name in ("w_rgate", "w_igate"):
            for h in range(LRU_HEADS):
                for i in range(HEAD_DIM):
                    r = h * HEAD_DIM + i
                    put(name, r * HEAD_DIM, src[name][r:r + 1, h * HEAD_DIM:(h + 1) * HEAD_DIM])
        eye = (lax.broadcasted_iota(jnp.int32, (GMLP_BLOCK, GMLP_BLOCK), 0)
               == lax.broadcasted_iota(jnp.int32, (GMLP_BLOCK, GMLP_BLOCK), 1))
        for g_idx in range(GMLP_GROUPS):
            col = src["b_spatial"][:, g_idx:g_idx + 1]
            put("b_spatial", g_idx * GMLP_BLOCK, _colsum(jnp.where(eye, col, 0.0)))

    return pl.pallas_call(
        body, name="pack_small", out_shape=_sds((2, SUBLANES, SMALL_LANES), F32),
        in_specs=[vmem] * len(order) + [_ANY], out_specs=vmem,
        compiler_params=pltpu.CompilerParams(vmem_limit_bytes=VMEM_LIMIT_BYTES),
    )(*[g[n] for n in order], after)


def _adam_small(g_small, w, m, v):
    vmem = pl.BlockSpec(memory_space=pltpu.VMEM)
    n_p = len(SMALL_REPLICATED)

    def body(g_ref, *refs):
        w_refs, m_refs, v_refs = refs[:n_p], refs[n_p:2 * n_p], refs[2 * n_p:3 * n_p]
        outs = refs[3 * n_p:]
        go, do, mo, vo = outs[:n_p], outs[n_p:2 * n_p], outs[2 * n_p:3 * n_p], outs[3 * n_p:]
        for k, name in enumerate(SMALL_REPLICATED):
            def take(first, count, name=name):
                parts = [g_ref[row, sub:sub + 1, lane:lane + n]
                         for row, sub, lane, n in _small_pieces(name, first, count)]
                return parts[0] if len(parts) == 1 else jnp.concatenate(parts, axis=1)

            shape = w_refs[k].shape
            if name in ROW_VECTORS:
                go[k][...] = take(0, shape[1])
            elif name in ("b_rgate", "b_igate"):
                for h in range(LRU_HEADS):
                    go[k][0, h:h + 1, :] = take(h * HEAD_DIM, HEAD_DIM)
            elif name == "b_spatial":
                for g_idx in range(GMLP_GROUPS):
                    go[k][0, g_idx:g_idx + 1, :] = take(g_idx * GMLP_BLOCK, GMLP_BLOCK)
            elif name == "w_spatial":
                for g_idx in range(GMLP_GROUPS):
                    for i in range(GMLP_BLOCK):
                        go[k][0, g_idx, i:i + 1, :] = take((g_idx * GMLP_BLOCK + i) * GMLP_BLOCK, GMLP_BLOCK)
            else:
                for h in range(LRU_HEADS):
                    for i in range(HEAD_DIM):
                        go[k][0, h, i:i + 1, :] = take((h * HEAD_DIM + i) * HEAD_DIM, HEAD_DIM)
            do[k][...], mo[k][...], vo[k][...] = _adam_math(w_refs[k][...], go[k][...], m_refs[k][...],
                                                             v_refs[k][...])

    names = SMALL_REPLICATED
    out_shape = [_sds(w[n].shape, F32) for n in names] * 4
    res = pl.pallas_call(
        body, name="adam_small", out_shape=out_shape,
        in_specs=[vmem] * (1 + 3 * n_p), out_specs=[vmem] * (4 * n_p),
        compiler_params=pltpu.CompilerParams(vmem_limit_bytes=VMEM_LIMIT_BYTES),
    )(g_small, *[w[n] for n in names], *[m[n] for n in names], *[v[n] for n in names])
    return [dict(zip(names, res[k * n_p:(k + 1) * n_p])) for k in range(4)]


def _adam_cols(name, g_small, w, m, v, chip_arr):
    _, k_taps, n = w.shape
    row, off = SMALL_SLOT[name]
    first = off // n
    per_sub = SMALL_LANES // n

    def body(chip_ref, *refs):
        g_refs = refs[:k_taps]
        w_ref, m_ref, v_ref, go_ref, d_ref, nm_ref, nv_ref = refs[k_taps:]
        for k in range(k_taps):
            tap = (0, slice(k, k + 1), slice(None))
            sub = (first + N_CHIPS * k + chip_ref[0]) // per_sub
            g = g_refs[k][row, pl.ds(sub, 1), :]
            go_ref[tap] = g
            d_ref[tap], nm_ref[tap], nv_ref[tap] = _adam_math(w_ref[tap], g, m_ref[tap], v_ref[tap])

    whole = pl.BlockSpec(w.shape, lambda i, ch: (0, 0, 0))
    taps = [pl.BlockSpec((2, SUBLANES, n),
                         functools.partial(lambda i, ch, k: (0, 0, (first + N_CHIPS * k + ch[0]) % per_sub), k=k))
            for k in range(k_taps)]
    grid_spec = pltpu.PrefetchScalarGridSpec(
        num_scalar_prefetch=1, grid=(1,), in_specs=taps + [whole] * 3, out_specs=[whole] * 4)
    return pl.pallas_call(body, name="adam_" + name, grid_spec=grid_spec, out_shape=[_sds(w.shape, F32)] * 4)(
        chip_arr, *[g_small] * k_taps, w, m, v)


def kernel(x, c, w_ada, b_ada, g_mix_pre, g_mix_post, w_in, conv_w, conv_b, w_rgate, b_rgate, w_igate, b_igate, lru_a, v_norm_g, v_norm_b, w_spatial, b_spatial, g_lru_out, g_gmlp_out, w_out, g_ffn_pre, g_ffn_post, w_up, ffn_conv_w, ffn_conv_b, w_down, loss_target, m_w_ada, m_b_ada, m_g_mix_pre, m_g_mix_post, m_w_in, m_conv_w, m_conv_b, m_w_rgate, m_b_rgate, m_w_igate, m_b_igate, m_lru_a, m_v_norm_g, m_v_norm_b, m_w_spatial, m_b_spatial, m_g_lru_out, m_g_gmlp_out, m_w_out, m_g_ffn_pre, m_g_ffn_post, m_w_up, m_ffn_conv_w, m_ffn_conv_b, m_w_down, v_w_ada, v_b_ada, v_g_mix_pre, v_g_mix_post, v_w_in, v_conv_w, v_conv_b, v_w_rgate, v_b_rgate, v_w_igate, v_b_igate, v_lru_a, v_v_norm_g, v_v_norm_b, v_w_spatial, v_b_spatial, v_g_lru_out, v_g_gmlp_out, v_w_out, v_g_ffn_pre, v_g_ffn_post, v_w_up, v_ffn_conv_w, v_ffn_conv_b, v_w_down):
    args = dict(locals())
    names = ("w_ada", "b_ada", "g_mix_pre", "g_mix_post", "w_in", "conv_w", "conv_b", "w_rgate", "b_rgate",
             "w_igate", "b_igate", "lru_a", "v_norm_g", "v_norm_b", "w_spatial", "b_spatial", "g_lru_out",
             "g_gmlp_out", "w_out", "g_ffn_pre", "g_ffn_post", "w_up", "ffn_conv_w", "ffn_conv_b", "w_down")
    drop = lambda a: a if a.ndim == 2 else a[0]
    w = {n: drop(args[n]) for n in names}
    m = {n: drop(args["m_" + n]) for n in names}
    v = {n: drop(args["v_" + n]) for n in names}
    xi, yi, ci = _position()
    me_arr = jnp.reshape(4 * xi + 2 * yi + ci, (1,)).astype(jnp.int32)
    chip_arr = jnp.reshape(2 * xi + yi, (1,)).astype(jnp.int32)
    c_arr = jnp.reshape(ci, (1,)).astype(jnp.int32)
    pos_arr = jnp.stack([ci, 2 * xi + yi]).astype(jnp.int32)

    big = ("w_in", "w_out", "w_up", "w_down")
    lands_a = _cast_place([w[n] for n in big[:2]], chip_arr, "cast_place_a")
    start_a, wait_a = _gather_plan([w[n].shape[0] for n in big[:2]])
    start_b, wait_b = _gather_plan([w[n].shape[0] for n in big[2:]])

    row0 = jnp.concatenate([c, w["conv_w"].reshape(1, -1), w["ffn_conv_w"].reshape(1, -1)], axis=1)
    g0 = _allgather8(row0, "gather_cond")[:, 0, :]
    send_a, recv_a, _, lands_a, token_a = _split_start([], lands_a, start_a, 6, g0, "gather_start_a")
    lands_b = _cast_place([w[n] for n in big[2:]], chip_arr + token_a[0, 0].astype(jnp.int32), "cast_place_b")
    c8 = g0[:, :D_MODEL]
    per_chip = g0[0::2]
    conv_w_full = per_chip[:, D_MODEL:D_MODEL + 512].reshape(N_CHIPS, 4, 128).transpose(1, 0, 2).reshape(4, 512)
    ffn_conv_w_full = per_chip[:, D_MODEL + 512:].reshape(N_CHIPS, 3, 1536).transpose(1, 0, 2).reshape(3, 2 * D_FF)
    mod_parts = _allgather8(_ada_fwd(c8 + token_a[0:1, 0:1], w["w_ada"]), "gather_mod")
    send_b, recv_b, _, lands_b, token_b = _split_start([], lands_b, start_b, 6, mod_parts, "gather_start_b")
    _, lands_a = _split_wait(send_a, recv_a, [], lands_a, wait_a, token_b, "gather_wait_a")
    fwd_start, fwd_wait_a = _forward_plan([w[n].shape[0] for n in big[:2]])
    fwd_send_a, fwd_recv_a, _, lands_a, tok = _split_start([], lands_a, fwd_start, 6, pos_arr, "forward_start_a")
    mod = _mod_select(mod_parts, w["b_ada"].reshape(1, -1), me_arr, tok).reshape(N_MOD, D_MODEL)
    sh_m, sc_m, gt_m, sh_f, sc_f, gt_f = [mod[k:k + 1] for k in range(N_MOD)]

    small = {n: w[n] for n in SMALL_REPLICATED}
    small["conv_w"] = conv_w_full
    small["ffn_conv_w"] = ffn_conv_w_full
    row = lambda a: a.reshape(1, -1)
    seq_params, ws_t = _seq_params(small)
    glo, ggo = row(small["g_lru_out"]), row(small["g_gmlp_out"])
    g_pre, g_post = row(small["g_mix_pre"]), row(small["g_mix_post"])
    g_pre2, g_post2 = row(small["g_ffn_pre"]), row(small["g_ffn_post"])
    fw, fb = small["ffn_conv_w"], row(small["ffn_conv_b"])
    xs, tgt = x[0], loss_target[0]

    _, (w_in4, w_out4) = _split_wait(fwd_send_a, fwd_recv_a, [], lands_a, fwd_wait_a, mod, "forward_wait_a")
    w_out_b = w_out4.reshape(D_MODEL, D_MODEL)
    h, lx, ycat, hst, stash = _seqmix(xs, sc_m, sh_m, g_pre, w_in4, seq_params, glo, ggo)
    _, lands_b = _split_wait(send_b, recv_b, [], lands_b, wait_b, ycat, "gather_wait_b")
    fwd_start, fwd_wait = _forward_plan([w[n].shape[0] for n in big[2:]])
    fwd_send, fwd_recv, _, lands_b, tok = _split_start([], lands_b, fwd_start, 6, pos_arr, "forward_start_b")
    y, x1, h2 = _mix_out(ycat, xs, w_out_b, gt_m + tok[0:1, 0:1], g_post, g_pre2, sc_f, sh_f)
    _, (w_up4, w_down4) = _split_wait(fwd_send, fwd_recv, [], lands_b, fwd_wait, h2, "forward_wait_b")
    w_down_b = w_down4.reshape(D_FF, D_MODEL)
    up0, pre, act, dy2, dx2, loss, dgt_f, dg_post2 = _ffn_fwd(h2, x1, tgt, w_up4, w_down_b, fw, fb, gt_f, g_post2)

    dup0, dfw, dfb = _ffn_bwd_a(dy2, pre, up0, w_down_b, fw)
    gw_up = _wgrad(h2, dup0, N_CHIPS, "wgrad_up", True)
    gw_down = _wgrad(act, dy2, 2, "wgrad_down", False)
    ex_start, ex_wait = _exchange_plan(2)
    sg_start, sg_wait = _swap_gathered_plan(2)
    grads, deltas, new_m, new_v = {}, {}, {}, {}

    def swap_start(parts, name):
        sw_start, sw_wait = _swap_halves_plan([p.shape[1] // 2 for p in parts])
        recv = [lax.empty((N_CHIPS, p.shape[1] // 2, p.shape[2]), BF16) for p in parts]
        send_s, recv_s, parts, recv, token = _split_start(parts, recv, sw_start, len(parts), pos_arr,
                                                           "swap_start_" + name)
        return (send_s, recv_s, parts, recv, sw_wait), token

    def exchange_start(swap, tags, after, name):
        send_s, recv_s, parts, recv, sw_wait = swap
        parts, recv = _split_wait(send_s, recv_s, parts, recv, sw_wait, after, "swap_wait_" + name)
        both = [_chip_sum(p, r, pos_arr, "chip_sum_" + t) for p, r, t in zip(parts, recv, tags)]
        sums, gath = [b[0] for b in both], [b[1] for b in both]
        return _split_start(sums, gath, ex_start, 3 * len(parts), pos_arr, "exchange_start_" + name)

    def gathered_start(exchange, after, name):
        send_s, recv_s, sums, gath, _ = exchange
        _, gath = _split_wait(send_s, recv_s, sums, gath, ex_wait, after, "exchange_wait_" + name)
        send_s, recv_s, _, gath, token = _split_start([], gath, sg_start, len(gath), pos_arr,
                                                      "gathered_start_" + name)
        return (send_s, recv_s, gath), token

    def gathered_wait(gathered, after, name):
        send_s, recv_s, gath = gathered
        return _split_wait(send_s, recv_s, [], gath, sg_wait, after, "gathered_wait_" + name)[1]

    def adam_big(t, gath, after):
        grads[t], deltas[t], new_m[t], new_v[t] = _adam_gathered(w[t], gath, m[t], v[t], c_arr, after, "adam_" + t)

    def behind(value, token):
        return value + token[0:1, 0:1]

    tags_b, tags_a = ("w_up", "w_down"), ("w_in", "w_out")
    swap_b, tok = swap_start([gw_up, gw_down.reshape(N_CHIPS, -1, D_MODEL)], "b")
    dx1, dy, dsh_f, dsc_f, dg_pre2, dgt_m, dg_post = _ffn_bwd_b(
        dup0, x1, y, dx2, w_up4, g_pre2, behind(sc_f, tok), sh_f, gt_m, g_post)
    exchange_b = exchange_start(swap_b, tags_b, dg_post, "b")
    (dz, grad_x, dcw, dcb, dwr, dwi, dbr, dbi, dspa, dng, dnb, dws, dbs_t, dglo, dggo, dsh_m, dsc_m,
     dg_pre) = _seqmix_bwd(lx, hst, stash, dy, w_out_b, seq_params, ws_t, behind(glo, exchange_b[4]), ggo,
                           xs, dx1, w_in4, g_pre, sc_m)
    gw_in = _wgrad(h, dz, N_CHIPS, "wgrad_in", True)
    gw_out = _wgrad(ycat, dy, 1, "wgrad_out", False)
    swap_a, tok = swap_start([gw_in, gw_out.reshape(N_CHIPS, -1, D_MODEL)], "a")

    dmod = jnp.concatenate([behind(dsh_m, tok), dsc_m, dgt_m, dsh_f, dsc_f, dgt_f], axis=1)
    dmod8 = _allgather8(dmod, "gather_dmod")[:, 0, :]
    small_grads = dict(
        g_mix_pre=dg_pre, g_mix_post=dg_post, conv_w=dcw, conv_b=dcb, w_rgate=dwr, b_rgate=dbr, w_igate=dwi,
        b_igate=dbi, lru_a=dspa, v_norm_g=dng, v_norm_b=dnb, w_spatial=dws, b_spatial=dbs_t, g_lru_out=dglo,
        g_gmlp_out=dggo, g_ffn_pre=dg_pre2, g_ffn_post=dg_post2, ffn_conv_w=dfw, ffn_conv_b=dfb,
        loss=loss)
    g_small = _allreduce_small(_pack_small(small_grads, dmod8), "reduce_small")
    total = g_small[_small_pieces("loss", 0, 1)[0][:3]]
    exchange_a = exchange_start(swap_a, tags_a, g_small, "a")
    gathered_b, tok = gathered_start(exchange_b, exchange_a[4], "b")

    grads["w_ada"], deltas["w_ada"], new_m["w_ada"], new_v["w_ada"], g_b_ada = _ada_bwd(
        c8, behind(dmod8, tok), chip_arr, w["w_ada"], m["w_ada"], v["w_ada"])
    rep = SMALL_REPLICATED
    small_out = _adam_small(g_small, {n: args[n] for n in rep}, {n: args["m_" + n] for n in rep},
                            {n: args["v_" + n] for n in rep})
    for n in rep:
        grads[n], deltas[n], new_m[n], new_v[n] = [group[n] for group in small_out]
    for n in SMALL_COLUMN_SHARDED:
        grads[n], deltas[n], new_m[n], new_v[n] = _adam_cols(n, g_small, args[n], args["m_" + n],
                                                             args["v_" + n], chip_arr)
    d_b, m_b, v_b = _adam(w["b_ada"], g_b_ada, m["b_ada"], v["b_ada"], "adam_b_ada")
    grads["b_ada"], deltas["b_ada"], new_m["b_ada"], new_v["b_ada"] = g_b_ada, d_b, m_b, v_b

    gath_up, gath_down = gathered_wait(gathered_b, d_b, "b")
    adam_big("w_down", gath_down, pos_arr)
    gathered_a, tok = gathered_start(exchange_a, deltas["w_down"], "a")
    adam_big("w_up", gath_up, tok)
    gath_in, gath_out = gathered_wait(gathered_a, deltas["w_up"], "a")
    adam_big("w_in", gath_in, pos_arr)
    adam_big("w_out", gath_out, pos_arr)

    outs = [total, grad_x[None]]
    for group in (grads, deltas, new_m, new_v):
        outs.extend(group[n].reshape(args[n].shape) for n in names)
    return tuple(outs)
```

```python
import functools
import math

import jax
import jax.numpy as jnp
from jax import lax
from jax.experimental import pallas as pl
from jax.experimental.pallas import tpu as pltpu

F32 = jnp.float32
BF16 = jnp.bfloat16
MESH = pl.DeviceIdType.MESH

D_MODEL = 1024
LRU_WIDTH = 512
LRU_HEADS = 8
GMLP_GROUPS = 4
GMLP_BLOCK = 128
CHUNK = 64
D_FF = 3072
N_MOD = 6
EPS = 1e-6
LRU_C = 8.0
N_CHIPS = 4
N_DEV = 8

ADAM_LR = 0.001
ADAM_B1 = 0.9
ADAM_B2 = 0.999
ADAM_EPS = 1e-08
ADAM_WD = 0.01
ADAM_STEP = 10

GELU_C0 = math.sqrt(2.0 / math.pi)
GELU_C1 = 0.044715

VMEM_LIMIT_BYTES = 56 * 1024 * 1024
SUBLANES = 8
BF16_SUBLANES = 16
FFN_CHUNK = 768
SUB_ROWS = 256


def _gelu_gate(x):
    x2 = x * x
    z = x * ((2.0 * GELU_C0 * GELU_C1) * x2 + 2.0 * GELU_C0)
    return 1.0 / (1.0 + jnp.exp(-z)), x2


def _gelu(x):
    t = jnp.tanh(GELU_C0 * (x + GELU_C1 * x * x * x))
    return 0.5 * x * (1.0 + t)


def _gelu_and_grad(x):
    s, x2 = _gelu_gate(x)
    g = x * s
    dz = (6.0 * GELU_C0 * GELU_C1) * x2 + 2.0 * GELU_C0
    return g, s + g * (1.0 - s) * dz


def _sigmoid(x):
    return 1.0 / (1.0 + jnp.exp(-x))


def _log1p(u):
    w = 1.0 + u
    return jnp.where(w == 1.0, u, jnp.log(w) * (u / (w - 1.0)))


def _softplus(x):
    return jnp.maximum(x, 0.0) + _log1p(jnp.exp(-jnp.abs(x)))


def _neg_expm1(x):
    u = jnp.exp(x)
    um1 = u - 1.0
    tiny = um1 == 0.0
    small = um1 * (x / jnp.log(jnp.where(tiny, 2.0, jnp.maximum(u, 0.25))))
    return -jnp.where(tiny, x, jnp.where(x < -1.0, um1, small))


def _msq_rsqrt(v):
    return lax.rsqrt(jnp.mean(v * v, axis=-1, keepdims=True) + EPS)


def _rms_bwd(dyn, yn, r):
    return r * (dyn - yn * jnp.mean(dyn * yn, axis=-1, keepdims=True))


def _colsum(v):
    return jnp.sum(v, axis=0, keepdims=True)


def _shift_down(cur, prev8, k):
    rolled = pltpu.roll(cur, k, 0)
    head = pltpu.roll(prev8, k, 0)
    row8 = lax.broadcasted_iota(jnp.int32, (SUBLANES, cur.shape[1]), 0)
    first = jnp.where(row8 < k, head, rolled[0:SUBLANES])
    return jnp.concatenate([first, rolled[SUBLANES:]], axis=0)


def _shift_up(cur, next8, k):
    t = cur.shape[0]
    rolled = pltpu.roll(cur, t - k, 0)
    tail = pltpu.roll(next8, SUBLANES - k, 0)
    row8 = lax.broadcasted_iota(jnp.int32, (SUBLANES, cur.shape[1]), 0)
    last = jnp.where(row8 >= SUBLANES - k, tail, rolled[t - SUBLANES:])
    return jnp.concatenate([rolled[:t - SUBLANES], last], axis=0)


def _scan_fwd(a, b):
    t = a.shape[0]
    row = lax.broadcasted_iota(jnp.int32, a.shape, 0)
    d = 1
    while d < t:
        keep = row >= d
        a_s = jnp.where(keep, pltpu.roll(a, d, 0), 1.0)
        b_s = jnp.where(keep, pltpu.roll(b, d, 0), 0.0)
        b = a * b_s + b
        a = a * a_s
        d *= 2
    return a, b


def _scan_bwd(a, g):
    t = a.shape[0]
    row = lax.broadcasted_iota(jnp.int32, a.shape, 0)
    d = 1
    while d < t:
        keep = row < t - d
        a_s = jnp.where(keep, pltpu.roll(a, t - d, 0), 1.0)
        g_s = jnp.where(keep, pltpu.roll(g, t - d, 0), 0.0)
        g = a * g_s + g
        a = a * a_s
        d *= 2
    return a, g


def _dot(a, b):
    return jnp.dot(a, b, preferred_element_type=F32)


def _dot_nt(a, b):
    return lax.dot_general(a, b, (((1,), (1,)), ((), ())), preferred_element_type=F32)


def _dot_tn(a, b):
    return lax.dot_general(a, b, (((0,), (0,)), ((), ())), preferred_element_type=F32)


def _rows(ts, cols, rev_of=None):
    if rev_of is None:
        return pl.BlockSpec((ts, cols), lambda i: (i, 0))
    return pl.BlockSpec((ts, cols), lambda i: (rev_of - 1 - i, 0))


def _halo_prev(ts, cols, halo, rev_of=None, col_block=0):
    per = ts // halo
    if rev_of is None:
        return pl.BlockSpec((halo, cols), lambda i: (jnp.maximum(i * per - 1, 0), col_block))
    return pl.BlockSpec((halo, cols), lambda i: (jnp.maximum((rev_of - 1 - i) * per - 1, 0), col_block))


def _full(shape):
    nd = len(shape)
    return pl.BlockSpec(shape, lambda *_: (0,) * nd)


_RESIDENT = pl.BlockSpec(memory_space=pltpu.VMEM)


def _params(sem):
    return pltpu.CompilerParams(dimension_semantics=sem, vmem_limit_bytes=VMEM_LIMIT_BYTES)


def _sds(shape, dtype):
    return jax.ShapeDtypeStruct(shape, dtype)


def _sub_tiles(ts):
    return [slice(r0, r0 + SUB_ROWS) for r0 in range(0, ts, SUB_ROWS)]


N_STASH = 12
(ST_XC, ST_R, ST_IG, ST_A, ST_MULT, ST_GL, ST_DGL, ST_U, ST_DU, ST_Q, ST_VHAT, ST_SPB) = range(N_STASH)


def _seq_param_specs():
    return [_full((4, 512)), _full((1, 512)), _full((512, 512)), _full((512, 512)), _full((1, 512)),
            _full((1, 512)), _full((1, 512)), _full((1, 512)), _full((1, 512)), _full((4, 128, 128)),
            _full((128, 4))]


def _seqmix(x, sc, sh, g_pre, w_in4, seq_params, glo, ggo, ts=256):
    s, d = x.shape
    nt = s // ts

    def body(x_ref, sc_ref, sh_ref, gpre_ref, win_ref, cw_ref, cb_ref, bdr_ref, bdi_ref, br_ref, bi_ref, la_ref,
             ng_ref, nb_ref, ws_ref, bst_ref, glo_ref, ggo_ref, hin_ref, lx_ref, ycat_ref, hst_ref, st_ref,
             hcarry, lxprev, sp_scr):
        i = pl.program_id(0)

        @pl.when(i == 0)
        def _():
            hcarry[...] = jnp.zeros_like(hcarry)
            lxprev[...] = jnp.zeros_like(lxprev)

        xv = x_ref[...]
        hin = ((xv * _msq_rsqrt(xv) * gpre_ref[...]) * (1.0 + sc_ref[...]) + sh_ref[...]).astype(BF16)
        hin_ref[...] = hin
        z = [_dot(hin, win_ref[k]) for k in range(N_CHIPS)]

        lx = z[0]
        lx_ref[...] = lx
        prev8 = lxprev[...]
        lxprev[...] = lx[ts - SUBLANES:, :]
        xc = (cw_ref[3:4, :] * lx + cw_ref[2:3, :] * _shift_down(lx, prev8, 1)
              + cw_ref[1:2, :] * _shift_down(lx, prev8, 2) + cw_ref[0:1, :] * _shift_down(lx, prev8, 3)
              + cb_ref[...])
        xcb = xc.astype(BF16)
        r = _sigmoid(_dot(xcb, bdr_ref[...]) + br_ref[...])
        ig = _sigmoid(_dot(xcb, bdi_ref[...]) + bi_ref[...])
        log_a = (-LRU_C) * r * _softplus(-la_ref[...])
        a = jnp.exp(log_a)
        mult = jnp.sqrt(_neg_expm1(2.0 * log_a))
        acum, hloc = _scan_fwd(a, mult * (ig * xc))
        h = hloc + acum * hcarry[...]
        hcarry[...] = h[ts - 1:ts, :]
        hst_ref[...] = h
        gl, dgl = _gelu_and_grad(z[1])
        y_l = h * gl
        for slot, val in ((ST_XC, xc), (ST_R, r), (ST_IG, ig), (ST_A, a), (ST_MULT, mult), (ST_GL, gl),
                          (ST_DGL, dgl)):
            st_ref[slot] = val

        u, du = _gelu_and_grad(z[2])
        vg, dvg = _gelu_and_grad(z[3])
        vc = vg - jnp.mean(vg, axis=-1, keepdims=True)
        rstd = lax.rsqrt(jnp.mean(vc * vc, axis=-1, keepdims=True) + EPS)
        vhat = vc * rstd
        vb = (vhat * ng_ref[...] + nb_ref[...]).astype(BF16)
        for n in range(ts // GMLP_BLOCK):
            rs = slice(n * GMLP_BLOCK, (n + 1) * GMLP_BLOCK)
            for g in range(GMLP_GROUPS):
                cs = slice(g * 128, (g + 1) * 128)
                sp_scr[rs, cs] = _dot(ws_ref[g], vb[rs, cs]) + bst_ref[:, g:g + 1]
        spb = sp_scr[...]
        y_g = u * spb
        for slot, val in ((ST_U, u), (ST_DU, du), (ST_Q, rstd * dvg), (ST_VHAT, vhat), (ST_SPB, spb)):
            st_ref[slot] = val

        ycat_ref[:, 0:512] = (y_l * _msq_rsqrt(y_l) * glo_ref[...]).astype(BF16)
        ycat_ref[:, 512:1024] = (y_g * _msq_rsqrt(y_g) * ggo_ref[...]).astype(BF16)

    vec = _full((1, d))
    return pl.pallas_call(
        body, grid=(nt,), name="seqmix",
        in_specs=[_rows(ts, d), vec, vec, vec, _full(w_in4.shape)] + _seq_param_specs()
        + [_full((1, 512)), _full((1, 512))],
        out_specs=[_rows(ts, d), _rows(ts, 512), _rows(ts, d), _rows(ts, 512),
                   pl.BlockSpec((N_STASH, ts, 512), lambda i: (0, i, 0))],
        out_shape=[_sds((s, d), BF16), _sds((s, 512), F32), _sds((s, d), BF16), _sds((s, 512), F32),
                   _sds((N_STASH, s, 512), F32)],
        scratch_shapes=[pltpu.VMEM((1, 512), F32), pltpu.VMEM((SUBLANES, 512), F32), pltpu.VMEM((ts, 512), F32)],
        compiler_params=_params(("arbitrary",)),
    )(x, sc, sh, g_pre, w_in4, *seq_params, glo, ggo)


def _mix_out(ycat, x, w_out, gt_m, g_post, g_pre2, sc_f, sh_f, ts=512):
    s, d = x.shape

    def body(yc_ref, x_ref, w_ref, gt_ref, gp_ref, g2_ref, sc_ref, sh_ref, y_ref, x1_ref, h2_ref):
        for rs in _sub_tiles(ts):
            y = _dot(yc_ref[rs, :], w_ref[...])
            y_ref[rs, :] = y
            x1 = x_ref[rs, :] + gt_ref[...] * (y * _msq_rsqrt(y) * gp_ref[...])
            x1_ref[rs, :] = x1
            h2 = (x1 * _msq_rsqrt(x1) * g2_ref[...]) * (1.0 + sc_ref[...]) + sh_ref[...]
            h2_ref[rs, :] = h2.astype(BF16)

    vec = _full((1, d))
    return pl.pallas_call(
        body, grid=(s // ts,), name="mix_out",
        in_specs=[_rows(ts, d), _rows(ts, d), _full((d, d)), vec, vec, vec, vec, vec],
        out_specs=[_rows(ts, d), _rows(ts, d), _rows(ts, d)],
        out_shape=[_sds((s, d), F32), _sds((s, d), F32), _sds((s, d), BF16)],
        compiler_params=_params(("parallel",)),
    )(ycat, x, w_out, gt_m, g_post, g_pre2, sc_f, sh_f)


def _ffn_cols(j):
    per = (2 * D_FF // N_CHIPS) // FFN_CHUNK
    return j // per, (j % per) * FFN_CHUNK, j * FFN_CHUNK


def _ffn_fwd(h2, x1, tgt, w_up4, w_down, fw, fb, gt_f, g_post, ts=256):
    s, d = x1.shape
    nch = D_FF // FFN_CHUNK

    def body(h2_ref, x1_ref, tgt_ref, wup_ref, wdn_ref, fw_ref, fb_ref, gt_ref, gp_ref,
             up0_ref, pre_ref, act_ref, dy2_ref, dx2_ref, loss_ref, dgt_ref, dgp_ref, tail_ref):
        i = pl.program_id(0)

        @pl.when(i == 0)
        def _():
            tail_ref[...] = jnp.zeros_like(tail_ref)
            loss_ref[...] = jnp.zeros_like(loss_ref)
            dgt_ref[...] = jnp.zeros_like(dgt_ref)
            dgp_ref[...] = jnp.zeros_like(dgp_ref)

        hb = h2_ref[...]

        def up_project(j):
            sh_g, off, _ = _ffn_cols(j)
            return [_dot(hb, wup_ref[shard, :, off:off + FFN_CHUNK]) for shard in (sh_g, sh_g + 2)]

        y2 = jnp.zeros((ts, d), F32)
        ahead = up_project(0)
        for j in range(nch):
            _, _, col = _ffn_cols(j)
            ubs = ahead
            if j + 1 < nch:
                ahead = up_project(j + 1)
            halves = []
            for u, c0 in zip(ubs, (col, D_FF + col)):
                cs = slice(c0, c0 + FFN_CHUNK)
                up0_ref[:, cs] = u.astype(BF16)
                prev8 = tail_ref[:, cs]
                tail_ref[:, cs] = u[ts - SUBLANES:, :]
                halves.append(fw_ref[2:3, cs] * u + fw_ref[1:2, cs] * _shift_down(u, prev8, 1)
                              + fw_ref[0:1, cs] * _shift_down(u, prev8, 2) + fb_ref[:, cs])
                pre_ref[:, cs] = halves[-1].astype(BF16)
            act = (_gelu(halves[0]) * halves[1]).astype(BF16)
            act_ref[:, col:col + FFN_CHUNK] = act
            y2 = y2 + _dot(act, wdn_ref[col:col + FFN_CHUNK, :])
        r2 = _msq_rsqrt(y2)
        yn = y2 * r2
        yng = yn * gp_ref[...]
        e = x1_ref[...] + gt_ref[...] * yng - tgt_ref[...]
        loss_ref[...] += jnp.sum(e * e) * (0.5 / d)
        dx2 = e * (1.0 / d)
        dx2_ref[...] = dx2
        dgt_ref[...] += _colsum(dx2 * yng)
        dyng = dx2 * gt_ref[...]
        dgp_ref[...] += _colsum(dyng * yn)
        dy2_ref[...] = _rms_bwd(dyng * gp_ref[...], yn, r2).astype(BF16)

    vec = _full((1, d))
    return pl.pallas_call(
        body, grid=(s // ts,), name="ffn_fwd",
        in_specs=[_rows(ts, d), _rows(ts, d), _rows(ts, d), _RESIDENT, _RESIDENT,
                  _full((3, 2 * D_FF)), _full((1, 2 * D_FF)), vec, vec],
        out_specs=[_rows(ts, 2 * D_FF), _rows(ts, 2 * D_FF), _rows(ts, D_FF), _rows(ts, d), _rows(ts, d),
                   _full((1, 128)), vec, vec],
        out_shape=[_sds((s, 2 * D_FF), BF16), _sds((s, 2 * D_FF), BF16), _sds((s, D_FF), BF16), _sds((s, d), BF16),
                   _sds((s, d), F32), _sds((1, 128), F32), _sds((1, d), F32), _sds((1, d), F32)],
        scratch_shapes=[pltpu.VMEM((SUBLANES, 2 * D_FF), F32)],
        compiler_params=_params(("arbitrary",)),
    )(h2, x1, tgt, w_up4, w_down, fw, fb, gt_f, g_post)


def _shift_up_mxu(vb, up_mat, next8, k):
    t = vb.shape[0]
    main = _dot(up_mat, vb)
    tail = pltpu.roll(next8, SUBLANES - k, 0)
    row8 = lax.broadcasted_iota(jnp.int32, next8.shape, 0)
    last = main[t - SUBLANES:] + jnp.where(row8 >= SUBLANES - k, tail, 0.0)
    return jnp.concatenate([main[:t - SUBLANES], last], axis=0)


def _ffn_bwd_a(dy2, pre, up0, w_down, fw, ts=256):
    s, d = dy2.shape
    nt = s // ts
    nch = D_FF // FFN_CHUNK
    wide = 2 * D_FF
    up_mats = jnp.stack([jnp.eye(ts, k=1, dtype=BF16), jnp.eye(ts, k=2, dtype=BF16)])

    def body(dy2_ref, pre_ref, up0_ref, wdn_ref, fw_ref, um_ref, dup0_ref, dfw_ref, dfb_ref, next_ref):
        i = pl.program_id(0)

        @pl.when(i == 0)
        def _():
            next_ref[...] = jnp.zeros_like(next_ref)
            dfw_ref[...] = jnp.zeros_like(dfw_ref)
            dfb_ref[...] = jnp.zeros_like(dfb_ref)

        dyb = dy2_ref[...]
        for j in range(nch):
            _, _, col = _ffn_cols(j)
            dact = _dot_nt(dyb, wdn_ref[col:col + FFN_CHUNK, :])
            gl, dgl = _gelu_and_grad(pre_ref[:, col:col + FFN_CHUNK].astype(F32))
            dpre = (dact * pre_ref[:, D_FF + col:D_FF + col + FFN_CHUNK].astype(F32) * dgl, dact * gl)
            for half, c0 in enumerate((col, D_FF + col)):
                cs = slice(c0, c0 + FFN_CHUNK)
                dp = dpre[half]
                dpb = dp.astype(BF16)
                nxt = next_ref[:, cs]
                next_ref[:, cs] = dpb.astype(F32)[0:SUBLANES, :]
                su1 = _shift_up_mxu(dpb, um_ref[0], nxt, 1)
                su2 = _shift_up_mxu(dpb, um_ref[1], nxt, 2)
                u = up0_ref[:, cs].astype(F32)
                dfb_ref[:, cs] += _colsum(dp)
                dfw_ref[2:3, cs] += _colsum(dp * u)
                dfw_ref[1:2, cs] += _colsum(su1 * u)
                dfw_ref[0:1, cs] += _colsum(su2 * u)
                dup0 = fw_ref[2:3, cs] * dp + fw_ref[1:2, cs] * su1 + fw_ref[0:1, cs] * su2
                dup0_ref[:, cs] = dup0.astype(BF16)

    return pl.pallas_call(
        body, grid=(nt,), name="ffn_bwd_a",
        in_specs=[_rows(ts, d, nt), _rows(ts, wide, nt), _rows(ts, wide, nt), _RESIDENT,
                  _full((3, wide)), _full((2, ts, ts))],
        out_specs=[_rows(ts, wide, nt), _full((3, wide)), _full((1, wide))],
        out_shape=[_sds((s, wide), BF16), _sds((3, wide), F32), _sds((1, wide), F32)],
        scratch_shapes=[pltpu.VMEM((SUBLANES, wide), F32)],
        compiler_params=_params(("arbitrary",)),
    )(dy2, pre, up0, w_down, fw, up_mats)


def _ffn_bwd_b(dup0, x1, y, dx2, w_up4, g_pre2, sc_f, sh_f, gt_m, g_post_m, ts=512):
    s, d = x1.shape
    shard_cols = 2 * D_FF // N_CHIPS

    def body(dup_ref, x1_ref, y_ref, dx2_ref, wup_ref, g2_ref, sc_ref, sh_ref, gt_ref, gp_ref,
             dx1_ref, dy_ref, dsh_ref, dsc_ref, dg2_ref, dgt_ref, dgp_ref):
        i = pl.program_id(0)

        @pl.when(i == 0)
        def _():
            for ref in (dsh_ref, dsc_ref, dg2_ref, dgt_ref, dgp_ref):
                ref[...] = jnp.zeros_like(ref)

        for rs in _sub_tiles(ts):
            dh2 = jnp.zeros((SUB_ROWS, d), F32)
            for k in range(N_CHIPS):
                dh2 = dh2 + _dot_nt(dup_ref[rs, k * shard_cols:(k + 1) * shard_cols], wup_ref[k])
            x1v = x1_ref[rs, :]
            r2 = _msq_rsqrt(x1v)
            xn = x1v * r2
            hn = xn * g2_ref[...]
            dsh_ref[...] += _colsum(dh2)
            dsc_ref[...] += _colsum(dh2 * hn)
            dhn = dh2 * (1.0 + sc_ref[...])
            dg2_ref[...] += _colsum(dhn * xn)
            dx1 = dx2_ref[rs, :] + _rms_bwd(dhn * g2_ref[...], xn, r2)
            dx1_ref[rs, :] = dx1
            yv = y_ref[rs, :]
            ry = _msq_rsqrt(yv)
            yn = yv * ry
            dgt_ref[...] += _colsum(dx1 * (yn * gp_ref[...]))
            dyng = dx1 * gt_ref[...]
            dgp_ref[...] += _colsum(dyng * yn)
            dy_ref[rs, :] = _rms_bwd(dyng * gp_ref[...], yn, ry).astype(BF16)

    vec = _full((1, d))
    return pl.pallas_call(
        body, grid=(s // ts,), name="ffn_bwd_b",
        in_specs=[_rows(ts, 2 * D_FF), _rows(ts, d), _rows(ts, d), _rows(ts, d), _RESIDENT,
                  vec, vec, vec, vec, vec],
        out_specs=[_rows(ts, d), _rows(ts, d), vec, vec, vec, vec, vec],
        out_shape=[_sds((s, d), F32), _sds((s, d), BF16)] + [_sds((1, d), F32)] * 5,
        compiler_params=_params(("arbitrary",)),
    )(dup0, x1, y, dx2, w_up4, g_pre2, sc_f, sh_f, gt_m, g_post_m)


def _seqmix_bwd(lru_x, hst, stash, dy, w_out, seq_params, ws_t, glo, ggo, x, dx1, w_in4, g_pre, sc_m, ts=256):
    s, d = x.shape
    nt = s // ts
    small_shapes = [(4, 512), (1, 512), (512, 512), (512, 512), (1, 512), (1, 512), (1, 512),
                    (1, 512), (1, 512), (4, 128, 128), (128, 4), (1, 512), (1, 512),
                    (1, d), (1, d), (1, d)]

    def body(lx_ref, hst_ref, hprev_ref, st_ref, dy_ref, wout_ref, cw_ref, cb_ref, bdr_ref, bdi_ref, br_ref,
             bi_ref, la_ref, ng_ref, nb_ref, ws_ref, bst_ref, wst_ref, glo_ref, ggo_ref, x_ref, dx1_ref, win_ref,
             gpre_ref, scm_ref, dz_ref, gx_ref, *rest):
        small_refs = rest[:16]
        (dcw_ref, dcb_ref, dwr_ref, dwi_ref, dbr_ref, dbi_ref, dspa_ref, dng_ref, dnb_ref, dws_ref, dbs_ref,
         dglo_ref, dggo_ref, dsh_ref, dsc_ref, dgpre_ref) = small_refs
        gcarry, anext, dxcnext, dv_scr = rest[16:]
        i = pl.program_id(0)

        @pl.when(i == 0)
        def _():
            for ref in small_refs:
                ref[...] = jnp.zeros_like(ref)
            gcarry[...] = jnp.zeros_like(gcarry)
            anext[...] = jnp.ones_like(anext)
            dxcnext[...] = jnp.zeros_like(dxcnext)

        first_tile = i == nt - 1
        xc, r, ig, a, mult = st_ref[ST_XC], st_ref[ST_R], st_ref[ST_IG], st_ref[ST_A], st_ref[ST_MULT]
        gl, u, spb, vhat = st_ref[ST_GL], st_ref[ST_U], st_ref[ST_SPB], st_ref[ST_VHAT]
        lx = lx_ref[...]
        h = hst_ref[...]
        hprev = _shift_down(h, jnp.where(first_tile, 0.0, hprev_ref[...]), 1)
        y_l = h * gl
        y_g = u * spb

        dycat = _dot_nt(dy_ref[...], wout_ref[...])

        dz_parts = {}

        def emit_dz(k, val):
            dz_parts[k] = val.astype(BF16)
            dz_ref[:, k * 512:(k + 1) * 512] = dz_parts[k]

        rl = _msq_rsqrt(y_l)
        yln = y_l * rl
        dyl = dycat[:, 0:512]
        dglo_ref[...] += _colsum(dyl * yln)
        dy_l = _rms_bwd(dyl * glo_ref[...], yln, rl)
        rg = _msq_rsqrt(y_g)
        ygn = y_g * rg
        dyg = dycat[:, 512:1024]
        dggo_ref[...] += _colsum(dyg * ygn)
        dy_g = _rms_bwd(dyg * ggo_ref[...], ygn, rg)

        emit_dz(1, dy_l * h * st_ref[ST_DGL])
        a_up = _shift_up(a, anext[...], 1)
        acum, gloc = _scan_bwd(a_up, dy_l * gl)
        gg = gloc + acum * gcarry[...]
        gcarry[...] = gg[0:1, :]
        anext[...] = a[0:SUBLANES, :]
        da = gg * hprev
        t1 = gg * mult
        di = t1 * xc
        dxc = t1 * ig
        dmult = gg * ig * xc
        dla = da * a - dmult * (a * a / mult)
        dspa_ref[...] += _colsum(dla * r) * (-LRU_C)
        dpr = dla * ((-LRU_C) * _softplus(-la_ref[...])) * r * (1.0 - r)
        dpi = di * ig * (1.0 - ig)
        dbr_ref[...] += _colsum(dpr)
        dbi_ref[...] += _colsum(dpi)
        dprb = dpr.astype(BF16)
        dpib = dpi.astype(BF16)
        xcb = xc.astype(BF16)
        dwr_ref[...] += _dot_tn(xcb, dprb)
        dwi_ref[...] += _dot_tn(xcb, dpib)
        dxc = dxc + _dot_nt(dprb, bdr_ref[...]) + _dot_nt(dpib, bdi_ref[...])
        nxt = dxcnext[...]
        dxcnext[...] = dxc[0:SUBLANES, :]
        up1, up2, up3 = _shift_up(dxc, nxt, 1), _shift_up(dxc, nxt, 2), _shift_up(dxc, nxt, 3)
        dcb_ref[...] += _colsum(dxc)
        dcw_ref[3:4, :] += _colsum(dxc * lx)
        dcw_ref[2:3, :] += _colsum(up1 * lx)
        dcw_ref[1:2, :] += _colsum(up2 * lx)
        dcw_ref[0:1, :] += _colsum(up3 * lx)
        dlx = cw_ref[3:4, :] * dxc + cw_ref[2:3, :] * up1 + cw_ref[1:2, :] * up2 + cw_ref[0:1, :] * up3
        emit_dz(0, dlx)

        emit_dz(2, dy_g * spb * st_ref[ST_DU])
        dsp = dy_g * u
        vb = (vhat * ng_ref[...] + nb_ref[...]).astype(BF16)
        for n in range(ts // GMLP_BLOCK):
            rs = slice(n * GMLP_BLOCK, (n + 1) * GMLP_BLOCK)
            for g in range(GMLP_GROUPS):
                cs = slice(g * 128, (g + 1) * 128)
                dbs_ref[:, g:g + 1] += jnp.sum(dsp[rs, cs], axis=1, keepdims=True)
                blk = dsp[rs, cs].astype(BF16)
                dws_ref[g] += _dot_nt(blk, vb[rs, cs])
                dv_scr[rs, cs] = _dot(wst_ref[g], blk)
        dv = dv_scr[...]
        dng_ref[...] += _colsum(dv * vhat)
        dnb_ref[...] += _colsum(dv)
        dvh = dv * ng_ref[...]
        dvg = dvh - jnp.mean(dvh, axis=-1, keepdims=True) - vhat * jnp.mean(dvh * vhat, axis=-1, keepdims=True)
        emit_dz(3, dvg * st_ref[ST_Q])

        dh = _dot_nt(dz_parts[0], win_ref[0])
        for k in range(1, N_CHIPS):
            dh = dh + _dot_nt(dz_parts[k], win_ref[k])
        xv = x_ref[...]
        rx = _msq_rsqrt(xv)
        xn = xv * rx
        dsh_ref[...] += _colsum(dh)
        dsc_ref[...] += _colsum(dh * (xn * gpre_ref[...]))
        dhn = dh * (1.0 + scm_ref[...])
        dgpre_ref[...] += _colsum(dhn * xn)
        gx_ref[...] = dx1_ref[...] + _rms_bwd(dhn * gpre_ref[...], xn, rx)

        @pl.when(i == nt - 1)
        def _():
            pos = lax.broadcasted_iota(jnp.int32, (GMLP_BLOCK, GMLP_BLOCK), 0) // CHUNK
            src = lax.broadcasted_iota(jnp.int32, (GMLP_BLOCK, GMLP_BLOCK), 1) // CHUNK
            for g in range(GMLP_GROUPS):
                dws_ref[g] = jnp.where(src <= pos, dws_ref[g], 0.0)
            dspa_ref[...] = dspa_ref[...] * (-_sigmoid(-la_ref[...]))

    vec = _full((1, d))
    in_specs = ([_rows(ts, 512, nt), _rows(ts, 512, nt), _halo_prev(ts, 512, SUBLANES, nt),
                 pl.BlockSpec((N_STASH, ts, 512), lambda i: (0, nt - 1 - i, 0)), _rows(ts, d, nt),
                 _full((d, d))]
                + _seq_param_specs() + [_full((4, 128, 128)), _full((1, 512)), _full((1, 512))]
                + [_rows(ts, d, nt), _rows(ts, d, nt), _full(w_in4.shape), vec, vec])
    return pl.pallas_call(
        body, grid=(nt,), name="seqmix_bwd",
        in_specs=in_specs,
        out_specs=[_rows(ts, 2048, nt), _rows(ts, d, nt)] + [_full(sh) for sh in small_shapes],
        out_shape=[_sds((s, 2048), BF16), _sds((s, d), F32)] + [_sds(sh, F32) for sh in small_shapes],
        scratch_shapes=[pltpu.VMEM((1, 512), F32), pltpu.VMEM((SUBLANES, 512), F32),
                        pltpu.VMEM((SUBLANES, 512), F32), pltpu.VMEM((ts, 512), F32)],
        compiler_params=_params(("arbitrary",)),
    )(lru_x, hst, hst, stash, dy, w_out, *seq_params, ws_t, glo, ggo, x, dx1, w_in4, g_pre, sc_m)


def _wgrad(a, b, n_chunks, name, chunk_major, ts=2048):
    s, m = a.shape
    n = b.shape[1]
    nc = n // n_chunks
    nt = s // ts

    def body(a_ref, b_ref, o_ref, acc):
        i = pl.program_id(1)

        @pl.when(i == 0)
        def _():
            acc[...] = jnp.zeros_like(acc)

        acc[...] += _dot_tn(a_ref[...], b_ref[...])

        @pl.when(i == nt - 1)
        def _():
            if chunk_major:
                o_ref[0] = acc[...].astype(BF16)
            else:
                o_ref[...] = acc[...].astype(BF16)

    if chunk_major:
        out_spec, out_shape = pl.BlockSpec((1, m, nc), lambda c, i: (c, 0, 0)), _sds((n_chunks, m, nc), BF16)
    else:
        out_spec, out_shape = pl.BlockSpec((m, nc), lambda c, i: (0, c)), _sds((m, n), BF16)
    return pl.pallas_call(
        body, grid=(n_chunks, nt), name=name,
        in_specs=[pl.BlockSpec((ts, m), lambda c, i: (i, 0)), pl.BlockSpec((ts, nc), lambda c, i: (i, c))],
        out_specs=out_spec,
        out_shape=out_shape,
        scratch_shapes=[pltpu.VMEM((m, nc), F32)],
        compiler_params=_params(("parallel", "arbitrary")),
    )(a, b)


def _block_diag(w):
    heads, hd, _ = w.shape
    eye = jnp.eye(heads, dtype=w.dtype)
    return (eye[:, None, :, None] * w[:, :, None, :]).reshape(heads * hd, heads * hd)


def _seq_params(small):
    row = lambda v: v.reshape(1, -1)
    pos = jnp.arange(GMLP_BLOCK)
    mask = (pos[None, :] // CHUNK) <= (pos[:, None] // CHUNK)
    ws = jnp.where(mask[None], small["w_spatial"], 0.0)
    seq_params = (small["conv_w"], row(small["conv_b"]),
                  _block_diag(small["w_rgate"]).astype(BF16), _block_diag(small["w_igate"]).astype(BF16),
                  row(small["b_rgate"]), row(small["b_igate"]), row(small["lru_a"]),
                  row(small["v_norm_g"]), row(small["v_norm_b"]), ws.astype(BF16), small["b_spatial"].T)
    return seq_params, jnp.swapaxes(ws, 1, 2).astype(BF16)


_ANY = pl.BlockSpec(memory_space=pl.ANY)
_CHIP_FLIPS = ((1, 0), (0, 1), (1, 1))


def _position():
    return lax.axis_index("x"), lax.axis_index("y"), lax.axis_index("c")


def _flip(v, f):
    return 1 - v if f else v


def _remote(src, dst, send_sem, recv_sem, peer):
    return pltpu.make_async_remote_copy(src_ref=src, dst_ref=dst, send_sem=send_sem, recv_sem=recv_sem,
                                        device_id=peer, device_id_type=MESH)


def _allgather8(block, name, after=()):
    r, n = block.shape

    def body(x_ref, *refs):
        gath, send_sems, recv_sems, loc_sem = refs[len(after):]
        x, y, c = _position()
        me = 4 * x + 2 * y + c
        loc = pltpu.make_async_copy(x_ref, gath.at[me], loc_sem)
        loc.start()
        peers = []
        for k in range(1, N_DEV):
            px, py, pc = _flip(x, k & 4), _flip(y, k & 2), _flip(c, k & 1)
            peers.append((px, py, pc))
            _remote(x_ref, gath.at[me], send_sems.at[k - 1], recv_sems.at[k - 1], (px, py, pc)).start()
        for k, (px, py, pc) in enumerate(peers):
            src = 4 * px + 2 * py + pc
            _remote(x_ref, gath.at[src], send_sems.at[k], recv_sems.at[k], (px, py, pc)).wait_recv()
        for k, peer in enumerate(peers):
            _remote(x_ref, gath.at[me], send_sems.at[k], recv_sems.at[k], peer).wait_send()
        loc.wait()

    return pl.pallas_call(
        body, name=name, out_shape=_sds((N_DEV, r, n), F32),
        in_specs=[pl.BlockSpec(memory_space=pltpu.VMEM)] + [pl.BlockSpec(memory_space=pl.ANY)] * len(after),
        out_specs=pl.BlockSpec(memory_space=pltpu.VMEM),
        scratch_shapes=[pltpu.SemaphoreType.DMA((N_DEV - 1,)), pltpu.SemaphoreType.DMA((N_DEV - 1,)),
                        pltpu.SemaphoreType.DMA],
        compiler_params=pltpu.CompilerParams(vmem_limit_bytes=VMEM_LIMIT_BYTES),
    )(block, *after)


def _half(ref, c, rows):
    hr = rows // 2
    return ref.at[pl.ds(pl.multiple_of(c * hr, BF16_SUBLANES), hr), :]


def _chip_sum(part, recv, pos_arr, name):
    _, rows, cols = part.shape
    hr = rows // 2

    def body(pos_ref, p_ref, r_ref, o_ref, g_ref):
        total = (p_ref[...].astype(F32) + r_ref[...].astype(F32)).astype(BF16)
        o_ref[...] = total

        @pl.when(pl.program_id(0) == pos_ref[1])
        def _():
            g_ref[0] = total

    grid_spec = pltpu.PrefetchScalarGridSpec(
        num_scalar_prefetch=1, grid=(N_CHIPS,),
        in_specs=[pl.BlockSpec((1, hr, cols), lambda k, pos: (k, pos[0], 0)),
                  pl.BlockSpec((1, hr, cols), lambda k, pos: (k, 0, 0))],
        out_specs=[pl.BlockSpec((1, hr, cols), lambda k, pos: (k, 0, 0)),
                   pl.BlockSpec((1, 1, hr, cols), lambda k, pos: (0, pos[1], 0, 0))])
    return pl.pallas_call(
        body, name=name, grid_spec=grid_spec,
        out_shape=[_sds((N_CHIPS, hr, cols), BF16), _sds((2, N_CHIPS, hr, cols), BF16)],
        compiler_params=_params(("arbitrary",)),
    )(pos_arr, part, recv)


_HBM = pl.BlockSpec(memory_space=pltpu.HBM)
_SEM = pl.BlockSpec(memory_space=pltpu.SEMAPHORE)
_EFFECT = pltpu.SideEffectType.DATAFLOW_SIDE_EFFECTING


def _in_hbm(a):
    return pltpu.with_memory_space_constraint(a, pltpu.HBM)


def _split_start(srcs, lands, plan, n_copies, after, name):
    ns, nl = len(srcs), len(lands)
    bufs = list(srcs) + list(lands)

    def body(*refs):
        send_sems, recv_sems = refs[ns + nl + 1], refs[ns + nl + 2]
        token = refs[-1]
        for k, (src, dst, peer) in enumerate(plan(refs[:ns], refs[ns:ns + nl])):
            _remote(src, dst, send_sems.at[k], recv_sems.at[k], peer).start()
        token[...] = jnp.zeros_like(token)

    out = pl.pallas_call(
        body, name=name,
        out_shape=(pltpu.SemaphoreType.DMA((n_copies,)), pltpu.SemaphoreType.DMA((n_copies,)),
                   *[pltpu.HBM(b.shape, b.dtype) for b in bufs], _sds((SUBLANES, 128), F32)),
        in_specs=[_HBM] * (ns + nl) + [_ANY],
        out_specs=(_SEM, _SEM, *[_HBM] * (ns + nl), pl.BlockSpec(memory_space=pltpu.VMEM)),
        input_output_aliases={i: 2 + i for i in range(ns + nl)},
        compiler_params=pltpu.CompilerParams(has_side_effects=_EFFECT),
    )(*[_in_hbm(b) for b in bufs], after)
    return out[0], out[1], list(out[2:2 + ns]), list(out[2 + ns:2 + ns + nl]), out[-1]


def _split_wait(send_sems, recv_sems, srcs, lands, plan, after, name):
    ns, nl = len(srcs), len(lands)
    bufs = list(srcs) + list(lands)

    def body(*refs):
        send_ref, recv_ref = refs[ns + nl], refs[ns + nl + 1]
        me = _position()
        for k, src, dst in plan(refs[:ns], refs[ns:ns + nl]):
            cp = _remote(src, dst, send_ref.at[k], recv_ref.at[k], me)
            cp.wait_send()
            cp.wait_recv()

    out = pl.pallas_call(
        body, name=name,
        out_shape=[pltpu.HBM(b.shape, b.dtype) for b in bufs],
        in_specs=[_HBM] * (ns + nl) + [_SEM, _SEM, _ANY],
        out_specs=[_HBM] * (ns + nl),
        input_output_aliases={i: i for i in range(ns + nl)},
        compiler_params=pltpu.CompilerParams(has_side_effects=_EFFECT),
    )(*bufs, send_sems, recv_sems, after)
    return list(out[:ns]), list(out[ns:])


def _gather_plan(rows_of):
    def start(src_refs, land_refs):
        x, y, c = _position()
        chip = 2 * x + y
        out = []
        for a, rows in enumerate(rows_of):
            mine = _half(land_refs[a].at[chip], c, rows)
            out.extend((mine, mine, (_flip(x, fx), _flip(y, fy), c)) for fx, fy in _CHIP_FLIPS)
        return out

    def wait(src_refs, land_refs):
        x, y, c = _position()
        chip = 2 * x + y
        out = []
        for a, rows in enumerate(rows_of):
            for j, (fx, fy) in enumerate(_CHIP_FLIPS):
                src_chip = 2 * _flip(x, fx) + _flip(y, fy)
                out.append((3 * a + j, _half(land_refs[a].at[chip], c, rows),
                            _half(land_refs[a].at[src_chip], c, rows)))
        return out

    return start, wait


def _forward_plan(rows_of):
    def pieces(land_refs, half):
        x, y, _ = _position()
        return [_half(land_refs[a].at[2 * _flip(x, fx) + _flip(y, fy)], half, rows)
                for a, rows in enumerate(rows_of) for fx, fy in _CHIP_FLIPS]

    def start(src_refs, land_refs):
        x, y, c = _position()
        return [(p, p, (x, y, 1 - c)) for p in pieces(land_refs, c)]

    def wait(src_refs, land_refs):
        _, _, c = _position()
        return [(k, mine, theirs)
                for k, (mine, theirs) in enumerate(zip(pieces(land_refs, c), pieces(land_refs, 1 - c)))]

    return start, wait


def _swap_halves_plan(half_rows):
    def slices(src_refs, c):
        return [src_refs[a].at[:, pl.ds(pl.multiple_of((1 - c) * hr, BF16_SUBLANES), hr), :]
                for a, hr in enumerate(half_rows)]

    def start(src_refs, land_refs):
        x, y, c = _position()
        return [(src, land_refs[a], (x, y, 1 - c)) for a, src in enumerate(slices(src_refs, c))]

    def wait(src_refs, land_refs):
        _, _, c = _position()
        return [(a, src, land_refs[a]) for a, src in enumerate(slices(src_refs, c))]

    return start, wait


def _swap_gathered_plan(n_arrays):
    def start(src_refs, land_refs):
        x, y, c = _position()
        return [(land_refs[a].at[0], land_refs[a].at[1], (x, y, 1 - c)) for a in range(n_arrays)]

    def wait(src_refs, land_refs):
        return [(a, land_refs[a].at[0], land_refs[a].at[1]) for a in range(n_arrays)]

    return start, wait


def _exchange_plan(n_arrays):
    def start(src_refs, land_refs):
        x, y, c = _position()
        chip = 2 * x + y
        out = []
        for a in range(n_arrays):
            for fx, fy in _CHIP_FLIPS:
                px, py = _flip(x, fx), _flip(y, fy)
                out.append((src_refs[a].at[2 * px + py], land_refs[a].at[0, chip], (px, py, c)))
        return out

    def wait(src_refs, land_refs):
        x, y, c = _position()
        out = []
        for a in range(n_arrays):
            for j, (fx, fy) in enumerate(_CHIP_FLIPS):
                src_chip = 2 * _flip(x, fx) + _flip(y, fy)
                out.append((3 * a + j, src_refs[a].at[src_chip], land_refs[a].at[0, src_chip]))
        return out

    return start, wait


def _adam_gathered(w, gath, m, v, c_arr, after, name, tr=128):
    rows, cols = w.shape
    hr = rows // 2
    if hr % (2 * tr) == 0:
        tr = 2 * tr
    per = hr // tr

    def body(c_ref, w_ref, g_ref, m_ref, v_ref, after_ref, go_ref, d_ref, nm_ref, nv_ref):
        g = g_ref[0, 0].astype(F32)
        for k in range(1, N_CHIPS):
            g = g + g_ref[0, k].astype(F32)
        go_ref[...] = g
        d_ref[...], nm_ref[...], nv_ref[...] = _adam_math(w_ref[...], g, m_ref[...], v_ref[...])

    def rows_of(h, i, c_ref):
        c = c_ref[0]
        return ((c + h - 2 * c * h) * per + i, 0)

    blk = pl.BlockSpec((tr, cols), rows_of)
    grid_spec = pltpu.PrefetchScalarGridSpec(
        num_scalar_prefetch=1, grid=(2, per),
        in_specs=[blk, pl.BlockSpec((1, N_CHIPS, tr, cols), lambda h, i, c_ref: (h, 0, i, 0)), blk, blk, _ANY],
        out_specs=[blk] * 4)
    return pl.pallas_call(
        body, name=name, grid_spec=grid_spec, out_shape=[_sds(w.shape, F32)] * 4,
        compiler_params=_params(("arbitrary", "arbitrary")),
    )(c_arr, w, gath, m, v, after)


def _allreduce_small(block, name):
    two, r, n = block.shape
    assert two == 2

    def body(x_ref, out_ref, sib, chipsum, gath, d2d_send, d2d_recv, ici_send, ici_recv):
        x, y, c = _position()
        chip = 2 * x + y
        sibling = (x, y, 1 - c)
        first = _remote(x_ref, sib, d2d_send.at[0], d2d_recv.at[0], sibling)
        first.start()
        first.wait()
        chipsum[...] = x_ref[...] + sib[...]
        sends = []
        for j, (fx, fy) in enumerate(_CHIP_FLIPS):
            sends.append(_remote(chipsum.at[c], gath.at[chip], ici_send.at[j], ici_recv.at[j],
                                 (_flip(x, fx), _flip(y, fy), c)))
            sends[-1].start()
        gath[chip] = chipsum[c]
        for j, (fx, fy) in enumerate(_CHIP_FLIPS):
            landed = gath.at[2 * _flip(x, fx) + _flip(y, fy)]
            _remote(landed, landed, ici_send.at[j], ici_recv.at[j], sibling).wait_recv()
        for cp in sends:
            cp.wait_send()
        total = gath[0]
        for k in range(1, N_CHIPS):
            total = total + gath[k]
        out_ref[c] = total
        last = _remote(out_ref.at[c], out_ref.at[c], d2d_send.at[1], d2d_recv.at[1], sibling)
        last.start()
        _remote(out_ref.at[1 - c], out_ref.at[1 - c], d2d_send.at[1], d2d_recv.at[1], sibling).wait_recv()
        last.wait_send()

    vmem = pl.BlockSpec(memory_space=pltpu.VMEM)
    return pl.pallas_call(
        body, name=name, out_shape=_sds(block.shape, F32), in_specs=[vmem], out_specs=vmem,
        scratch_shapes=[pltpu.VMEM(block.shape, F32), pltpu.VMEM(block.shape, F32), pltpu.VMEM((N_CHIPS, r, n), F32),
                        pltpu.SemaphoreType.DMA((2,)), pltpu.SemaphoreType.DMA((2,)),
                        pltpu.SemaphoreType.DMA((3,)), pltpu.SemaphoreType.DMA((3,))],
        compiler_params=pltpu.CompilerParams(vmem_limit_bytes=VMEM_LIMIT_BYTES),
    )(block)


def _cast_place(shards, chip_arr, name):
    na = len(shards)
    steps = 4

    def body(chip_ref, *refs):
        for a in range(na):
            refs[na + a][0] = refs[a][...].astype(BF16)

    grid_spec = pltpu.PrefetchScalarGridSpec(
        num_scalar_prefetch=1, grid=(steps,),
        in_specs=[pl.BlockSpec((s.shape[0] // steps, s.shape[1]), lambda i, ch: (i, 0)) for s in shards],
        out_specs=[pl.BlockSpec((1, s.shape[0] // steps, s.shape[1]), lambda i, ch: (ch[0], i, 0)) for s in shards])
    return pl.pallas_call(
        body, name=name, grid_spec=grid_spec,
        out_shape=[_sds((N_CHIPS,) + s.shape, BF16) for s in shards],
        compiler_params=_params(("arbitrary",)),
    )(chip_arr, *shards)


def _silu(v):
    return v * _sigmoid(v)


def _ada_fwd(c8, w_ada):
    def body(c_ref, w_ref, o_ref):
        o_ref[...] = jnp.dot(_silu(c_ref[...]), w_ref[...], preferred_element_type=F32,
                             precision=lax.Precision.HIGHEST)

    return pl.pallas_call(
        body, name="ada_fwd", out_shape=_sds((N_DEV, w_ada.shape[1]), F32),
        compiler_params=pltpu.CompilerParams(vmem_limit_bytes=VMEM_LIMIT_BYTES),
    )(c8, w_ada)


def _mod_select(parts, b_ada, me_arr, after):
    cols = parts.shape[2]

    def body(me_ref, p_ref, b_ref, after_ref, o_ref):
        me = me_ref[0]
        for k in range(N_CHIPS):
            cs = slice(k * cols, (k + 1) * cols)
            o_ref[:, cs] = p_ref[2 * k, pl.ds(me, 1), :] + b_ref[:, cs]

    grid_spec = pltpu.PrefetchScalarGridSpec(
        num_scalar_prefetch=1, grid=(1,),
        in_specs=[pl.BlockSpec(parts.shape, lambda i, m: (0, 0, 0)), pl.BlockSpec(b_ada.shape, lambda i, m: (0, 0)),
                  _ANY],
        out_specs=pl.BlockSpec(b_ada.shape, lambda i, m: (0, 0)))
    return pl.pallas_call(body, name="mod_select", grid_spec=grid_spec, out_shape=_sds(b_ada.shape, F32))(
        me_arr, parts, b_ada, after)


def _ada_bwd(c8, dmod8, chip_arr, w, m, v, tr=512):
    d = c8.shape[1]
    cols = dmod8.shape[1] // N_CHIPS

    def body(chip_ref, c_ref, dm_ref, dmall_ref, w_ref, m_ref, v_ref, gw_ref, d_ref, nm_ref, nv_ref, gb_ref):
        g = lax.dot_general(_silu(c_ref[...]), dm_ref[...], (((0,), (0,)), ((), ())),
                            preferred_element_type=F32, precision=lax.Precision.HIGHEST)
        gw_ref[...] = g
        d_ref[...], nm_ref[...], nv_ref[...] = _adam_math(w_ref[...], g, m_ref[...], v_ref[...])
        acc = dmall_ref[0:1, :]
        for k in range(1, N_DEV):
            acc = acc + dmall_ref[k:k + 1, :]
        gb_ref[...] = acc

    rows = pl.BlockSpec((tr, cols), lambda i, ch: (i, 0))
    grid_spec = pltpu.PrefetchScalarGridSpec(
        num_scalar_prefetch=1, grid=(d // tr,),
        in_specs=[pl.BlockSpec((N_DEV, tr), lambda i, ch: (0, i)),
                  pl.BlockSpec((N_DEV, cols), lambda i, ch: (0, ch[0])),
                  pl.BlockSpec(dmod8.shape, lambda i, ch: (0, 0)), rows, rows, rows],
        out_specs=[rows] * 4 + [pl.BlockSpec((1, dmod8.shape[1]), lambda i, ch: (0, 0))])
    return pl.pallas_call(
        body, name="ada_bwd", grid_spec=grid_spec,
        out_shape=[_sds((d, cols), F32)] * 4 + [_sds((1, dmod8.shape[1]), F32)],
        compiler_params=_params(("arbitrary",)),
    )(chip_arr, c8, dmod8, dmod8, w, m, v)


def _adam_math(w, g, m, v):
    m = ADAM_B1 * m + (1.0 - ADAM_B1) * g
    v = ADAM_B2 * v + (1.0 - ADAM_B2) * (g * g)
    m_hat = m / (1.0 - ADAM_B1 ** ADAM_STEP)
    v_hat = v / (1.0 - ADAM_B2 ** ADAM_STEP)
    delta = -ADAM_LR * (m_hat / (jnp.sqrt(v_hat) + ADAM_EPS) + ADAM_WD * w)
    return delta, m, v


def _adam(w, g, m, v, name, tr=256):
    rows, cols = w.shape
    if rows % tr:
        tr = rows

    def body(w_ref, g_ref, m_ref, v_ref, d_ref, nm_ref, nv_ref):
        d_ref[...], nm_ref[...], nv_ref[...] = _adam_math(w_ref[...], g_ref[...], m_ref[...], v_ref[...])

    spec = pl.BlockSpec((tr, cols), lambda i: (i, 0))
    return pl.pallas_call(
        body, name=name, grid=(rows // tr,), in_specs=[spec] * 4, out_specs=[spec] * 3,
        out_shape=[_sds(w.shape, F32)] * 3, compiler_params=_params(("parallel",)),
    )(w, g, m, v)


SMALL_REPLICATED = ("g_mix_pre", "g_mix_post", "conv_b", "w_rgate", "b_rgate", "w_igate", "b_igate", "lru_a",
                    "v_norm_g", "v_norm_b", "w_spatial", "b_spatial", "g_lru_out", "g_gmlp_out", "g_ffn_pre",
                    "g_ffn_post", "ffn_conv_b")
SMALL_COLUMN_SHARDED = ("conv_w", "ffn_conv_w")

SMALL_ROW_LEN = 86016
_SMALL_ROWS = (
    (("ffn_conv_w", 18432), ("conv_w", 2048), ("w_spatial", 65536)),
    (("w_rgate", 32768), ("w_igate", 32768), ("ffn_conv_b", 6144), ("g_mix_pre", 1024), ("g_mix_post", 1024),
     ("g_ffn_pre", 1024), ("g_ffn_post", 1024), ("conv_b", 512), ("b_rgate", 512), ("b_igate", 512),
     ("lru_a", 512), ("v_norm_g", 512), ("v_norm_b", 512), ("b_spatial", 512), ("g_lru_out", 512),
     ("g_gmlp_out", 512), ("loss", 128)),
)


def _small_slots():
    slots = {}
    for row, entries in enumerate(_SMALL_ROWS):
        off = 0
        for name, size in entries:
            slots[name] = (row, off)
            off += size
        assert off <= SMALL_ROW_LEN
    return slots


SMALL_SLOT = _small_slots()
SMALL_LANES = SMALL_ROW_LEN // SUBLANES


def _small_pieces(name, first, count):
    row, off = SMALL_SLOT[name]
    pos, pieces = off + first, []
    while count:
        sub, lane = divmod(pos, SMALL_LANES)
        n = min(count, SMALL_LANES - lane)
        pieces.append((row, sub, lane, n))
        pos, count = pos + n, count - n
    return pieces
ROW_VECTORS = ("ffn_conv_b", "g_mix_pre", "g_mix_post", "g_ffn_pre", "g_ffn_post", "conv_b", "lru_a", "v_norm_g",
               "v_norm_b", "g_lru_out", "g_gmlp_out")
HEAD_DIM = LRU_WIDTH // LRU_HEADS


def _pack_small(g, after):
    order = ("ffn_conv_w", "conv_w", "w_spatial", "w_rgate", "w_igate", "b_rgate", "b_igate", "b_spatial", "loss") \
        + ROW_VECTORS
    vmem = pl.BlockSpec(memory_space=pltpu.VMEM)

    def body(*refs):
        src = dict(zip(order, refs))
        out_ref = refs[len(order) + 1]
        out_ref[...] = jnp.zeros_like(out_ref)

        def put(name, first, val):
            col = 0
            for row, sub, lane, n in _small_pieces(name, first, val.shape[1]):
                out_ref[row, sub:sub + 1, lane:lane + n] = val[:, col:col + n]
                col += n

        for name in ROW_VECTORS + ("b_rgate", "b_igate", "loss"):
            put(name, 0, src[name][...])
        for name in ("ffn_conv_w", "conv_w"):
            k_taps, n = src[name].shape
            for k in range(k_taps):
                put(name, k * n, src[name][k:k + 1, :])
        for g_idx in range(GMLP_GROUPS):
            for i in range(GMLP_BLOCK):
                put("w_spatial", (g_idx * GMLP_BLOCK + i) * GMLP_BLOCK, src["w_spatial"][g_idx, i:i + 1, :])
        for name in ("w_rgate", "w_igate"):
            for h in range(LRU_HEADS):
                for i in range(HEAD_DIM):
                    r = h * HEAD_DIM + i
                    put(name, r * HEAD_DIM, src[name][r:r + 1, h * HEAD_DIM:(h + 1) * HEAD_DIM])
        eye = (lax.broadcasted_iota(jnp.int32, (GMLP_BLOCK, GMLP_BLOCK), 0)
               == lax.broadcasted_iota(jnp.int32, (GMLP_BLOCK, GMLP_BLOCK), 1))
        for g_idx in range(GMLP_GROUPS):
            col = src["b_spatial"][:, g_idx:g_idx + 1]
            put("b_spatial", g_idx * GMLP_BLOCK, _colsum(jnp.where(eye, col, 0.0)))

    return pl.pallas_call(
        body, name="pack_small", out_shape=_sds((2, SUBLANES, SMALL_LANES), F32),
        in_specs=[vmem] * len(order) + [_ANY], out_specs=vmem,
        compiler_params=pltpu.CompilerParams(vmem_limit_bytes=VMEM_LIMIT_BYTES),
    )(*[g[n] for n in order], after)


def _adam_small(g_small, w, m, v):
    vmem = pl.BlockSpec(memory_space=pltpu.VMEM)
    n_p = len(SMALL_REPLICATED)

    def body(g_ref, *refs):
        w_refs, m_refs, v_refs = refs[:n_p], refs[n_p:2 * n_p], refs[2 * n_p:3 * n_p]
        outs = refs[3 * n_p:]
        go, do, mo, vo = outs[:n_p], outs[n_p:2 * n_p], outs[2 * n_p:3 * n_p], outs[3 * n_p:]
        for k, name in enumerate(SMALL_REPLICATED):
            def take(first, count, name=name):
                parts = [g_ref[row, sub:sub + 1, lane:lane + n]
                         for row, sub, lane, n in _small_pieces(name, first, count)]
                return parts[0] if len(parts) == 1 else jnp.concatenate(parts, axis=1)

            shape = w_refs[k].shape
            if name in ROW_VECTORS:
                go[k][...] = take(0, shape[1])
            elif name in ("b_rgate", "b_igate"):
                for h in range(LRU_HEADS):
                    go[k][0, h:h + 1, :] = take(h * HEAD_DIM, HEAD_DIM)
            elif name == "b_spatial":
                for g_idx in range(GMLP_GROUPS):
                    go[k][0, g_idx:g_idx + 1, :] = take(g_idx * GMLP_BLOCK, GMLP_BLOCK)
            elif name == "w_spatial":
                for g_idx in range(GMLP_GROUPS):
                    for i in range(GMLP_BLOCK):
                        go[k][0, g_idx, i:i + 1, :] = take((g_idx * GMLP_BLOCK + i) * GMLP_BLOCK, GMLP_BLOCK)
            else:
                for h in range(LRU_HEADS):
                    for i in range(HEAD_DIM):
                        go[k][0, h, i:i + 1, :] = take((h * HEAD_DIM + i) * HEAD_DIM, HEAD_DIM)
            do[k][...], mo[k][...], vo[k][...] = _adam_math(w_refs[k][...], go[k][...], m_refs[k][...],
                                                             v_refs[k][...])

    names = SMALL_REPLICATED
    out_shape = [_sds(w[n].shape, F32) for n in names] * 4
    res = pl.pallas_call(
        body, name="adam_small", out_shape=out_shape,
        in_specs=[vmem] * (1 + 3 * n_p), out_specs=[vmem] * (4 * n_p),
        compiler_params=pltpu.CompilerParams(vmem_limit_bytes=VMEM_LIMIT_BYTES),
    )(g_small, *[w[n] for n in names], *[m[n] for n in names], *[v[n] for n in names])
    return [dict(zip(names, res[k * n_p:(k + 1) * n_p])) for k in range(4)]


def _adam_cols(name, g_small, w, m, v, chip_arr):
    _, k_taps, n = w.shape
    row, off = SMALL_SLOT[name]
    first = off // n
    per_sub = SMALL_LANES // n

    def body(chip_ref, *refs):
        g_refs = refs[:k_taps]
        w_ref, m_ref, v_ref, go_ref, d_ref, nm_ref, nv_ref = refs[k_taps:]
        for k in range(k_taps):
            tap = (0, slice(k, k + 1), slice(None))
            sub = (first + N_CHIPS * k + chip_ref[0]) // per_sub
            g = g_refs[k][row, pl.ds(sub, 1), :]
            go_ref[tap] = g
            d_ref[tap], nm_ref[tap], nv_ref[tap] = _adam_math(w_ref[tap], g, m_ref[tap], v_ref[tap])

    whole = pl.BlockSpec(w.shape, lambda i, ch: (0, 0, 0))
    taps = [pl.BlockSpec((2, SUBLANES, n),
                         functools.partial(lambda i, ch, k: (0, 0, (first + N_CHIPS * k + ch[0]) % per_sub), k=k))
            for k in range(k_taps)]
    grid_spec = pltpu.PrefetchScalarGridSpec(
        num_scalar_prefetch=1, grid=(1,), in_specs=taps + [whole] * 3, out_specs=[whole] * 4)
    return pl.pallas_call(body, name="adam_" + name, grid_spec=grid_spec, out_shape=[_sds(w.shape, F32)] * 4)(
        chip_arr, *[g_small] * k_taps, w, m, v)


def kernel(x, c, w_ada, b_ada, g_mix_pre, g_mix_post, w_in, conv_w, conv_b, w_rgate, b_rgate, w_igate, b_igate, lru_a, v_norm_g, v_norm_b, w_spatial, b_spatial, g_lru_out, g_gmlp_out, w_out, g_ffn_pre, g_ffn_post, w_up, ffn_conv_w, ffn_conv_b, w_down, loss_target, m_w_ada, m_b_ada, m_g_mix_pre, m_g_mix_post, m_w_in, m_conv_w, m_conv_b, m_w_rgate, m_b_rgate, m_w_igate, m_b_igate, m_lru_a, m_v_norm_g, m_v_norm_b, m_w_spatial, m_b_spatial, m_g_lru_out, m_g_gmlp_out, m_w_out, m_g_ffn_pre, m_g_ffn_post, m_w_up, m_ffn_conv_w, m_ffn_conv_b, m_w_down, v_w_ada, v_b_ada, v_g_mix_pre, v_g_mix_post, v_w_in, v_conv_w, v_conv_b, v_w_rgate, v_b_rgate, v_w_igate, v_b_igate, v_lru_a, v_v_norm_g, v_v_norm_b, v_w_spatial, v_b_spatial, v_g_lru_out, v_g_gmlp_out, v_w_out, v_g_ffn_pre, v_g_ffn_post, v_w_up, v_ffn_conv_w, v_ffn_conv_b, v_w_down):
    args = dict(locals())
    names = ("w_ada", "b_ada", "g_mix_pre", "g_mix_post", "w_in", "conv_w", "conv_b", "w_rgate", "b_rgate",
             "w_igate", "b_igate", "lru_a", "v_norm_g", "v_norm_b", "w_spatial", "b_spatial", "g_lru_out",
             "g_gmlp_out", "w_out", "g_ffn_pre", "g_ffn_post", "w_up", "ffn_conv_w", "ffn_conv_b", "w_down")
    drop = lambda a: a if a.ndim == 2 else a[0]
    w = {n: drop(args[n]) for n in names}
    m = {n: drop(args["m_" + n]) for n in names}
    v = {n: drop(args["v_" + n]) for n in names}
    xi, yi, ci = _position()
    me_arr = jnp.reshape(4 * xi + 2 * yi + ci, (1,)).astype(jnp.int32)
    chip_arr = jnp.reshape(2 * xi + yi, (1,)).astype(jnp.int32)
    c_arr = jnp.reshape(ci, (1,)).astype(jnp.int32)
    pos_arr = jnp.stack([ci, 2 * xi + yi]).astype(jnp.int32)

    big = ("w_in", "w_out", "w_up", "w_down")
    lands_a = _cast_place([w[n] for n in big[:2]], chip_arr, "cast_place_a")
    start_a, wait_a = _gather_plan([w[n].shape[0] for n in big[:2]])
    start_b, wait_b = _gather_plan([w[n].shape[0] for n in big[2:]])

    row0 = jnp.concatenate([c, w["conv_w"].reshape(1, -1), w["ffn_conv_w"].reshape(1, -1)], axis=1)
    g0 = _allgather8(row0, "gather_cond")[:, 0, :]
    send_a, recv_a, _, lands_a, token_a = _split_start([], lands_a, start_a, 6, g0, "gather_start_a")
    lands_b = _cast_place([w[n] for n in big[2:]], chip_arr + token_a[0, 0].astype(jnp.int32), "cast_place_b")
    c8 = g0[:, :D_MODEL]
    per_chip = g0[0::2]
    conv_w_full = per_chip[:, D_MODEL:D_MODEL + 512].reshape(N_CHIPS, 4, 128).transpose(1, 0, 2).reshape(4, 512)
    ffn_conv_w_full = per_chip[:, D_MODEL + 512:].reshape(N_CHIPS, 3, 1536).transpose(1, 0, 2).reshape(3, 2 * D_FF)
    small = {n: w[n] for n in SMALL_REPLICATED}
    small["conv_w"] = conv_w_full
    small["ffn_conv_w"] = ffn_conv_w_full
    seq_params, ws_t = _seq_params(small)
    mod_parts = _allgather8(_ada_fwd(c8 + token_a[0:1, 0:1], w["w_ada"]), "gather_mod",
                            after=(seq_params[2], seq_params[3], seq_params[9], seq_params[10], ws_t))
    send_b, recv_b, _, lands_b, token_b = _split_start([], lands_b, start_b, 6, mod_parts, "gather_start_b")
    _, lands_a = _split_wait(send_a, recv_a, [], lands_a, wait_a, token_b, "gather_wait_a")
    fwd_start, fwd_wait_a = _forward_plan([w[n].shape[0] for n in big[:2]])
    fwd_send_a, fwd_recv_a, _, lands_a, tok = _split_start([], lands_a, fwd_start, 6, pos_arr, "forward_start_a")
    mod = _mod_select(mod_parts, w["b_ada"].reshape(1, -1), me_arr, tok).reshape(N_MOD, D_MODEL)
    sh_m, sc_m, gt_m, sh_f, sc_f, gt_f = [mod[k:k + 1] for k in range(N_MOD)]

    row = lambda a: a.reshape(1, -1)
    glo, ggo = row(small["g_lru_out"]), row(small["g_gmlp_out"])
    g_pre, g_post = row(small["g_mix_pre"]), row(small["g_mix_post"])
    g_pre2, g_post2 = row(small["g_ffn_pre"]), row(small["g_ffn_post"])
    fw, fb = small["ffn_conv_w"], row(small["ffn_conv_b"])
    xs, tgt = x[0], loss_target[0]

    _, (w_in4, w_out4) = _split_wait(fwd_send_a, fwd_recv_a, [], lands_a, fwd_wait_a, mod, "forward_wait_a")
    w_out_b = w_out4.reshape(D_MODEL, D_MODEL)
    h, lx, ycat, hst, stash = _seqmix(xs, sc_m, sh_m, g_pre, w_in4, seq_params, glo, ggo)
    _, lands_b = _split_wait(send_b, recv_b, [], lands_b, wait_b, ycat, "gather_wait_b")
    fwd_start, fwd_wait = _forward_plan([w[n].shape[0] for n in big[2:]])
    fwd_send, fwd_recv, _, lands_b, tok = _split_start([], lands_b, fwd_start, 6, pos_arr, "forward_start_b")
    y, x1, h2 = _mix_out(ycat, xs, w_out_b, gt_m + tok[0:1, 0:1], g_post, g_pre2, sc_f, sh_f)
    _, (w_up4, w_down4) = _split_wait(fwd_send, fwd_recv, [], lands_b, fwd_wait, h2, "forward_wait_b")
    w_down_b = w_down4.reshape(D_FF, D_MODEL)
    up0, pre, act, dy2, dx2, loss, dgt_f, dg_post2 = _ffn_fwd(h2, x1, tgt, w_up4, w_down_b, fw, fb, gt_f, g_post2)

    dup0, dfw, dfb = _ffn_bwd_a(dy2, pre, up0, w_down_b, fw)
    gw_up = _wgrad(h2, dup0, N_CHIPS, "wgrad_up", True)
    gw_down = _wgrad(act, dy2, 2, "wgrad_down", False)
    ex_start, ex_wait = _exchange_plan(2)
    sg_start, sg_wait = _swap_gathered_plan(2)
    grads, deltas, new_m, new_v = {}, {}, {}, {}

    def swap_start(parts, name):
        sw_start, sw_wait = _swap_halves_plan([p.shape[1] // 2 for p in parts])
        recv = [lax.empty((N_CHIPS, p.shape[1] // 2, p.shape[2]), BF16) for p in parts]
        send_s, recv_s, parts, recv, token = _split_start(parts, recv, sw_start, len(parts), pos_arr,
                                                           "swap_start_" + name)
        return (send_s, recv_s, parts, recv, sw_wait), token

    def exchange_start(swap, tags, after, name):
        send_s, recv_s, parts, recv, sw_wait = swap
        parts, recv = _split_wait(send_s, recv_s, parts, recv, sw_wait, after, "swap_wait_" + name)
        both = [_chip_sum(p, r, pos_arr, "chip_sum_" + t) for p, r, t in zip(parts, recv, tags)]
        sums, gath = [b[0] for b in both], [b[1] for b in both]
        return _split_start(sums, gath, ex_start, 3 * len(parts), pos_arr, "exchange_start_" + name)

    def gathered_start(exchange, after, name):
        send_s, recv_s, sums, gath, _ = exchange
        _, gath = _split_wait(send_s, recv_s, sums, gath, ex_wait, after, "exchange_wait_" + name)
        send_s, recv_s, _, gath, token = _split_start([], gath, sg_start, len(gath), pos_arr,
                                                      "gathered_start_" + name)
        return (send_s, recv_s, gath), token

    def gathered_wait(gathered, after, name):
        send_s, recv_s, gath = gathered
        return _split_wait(send_s, recv_s, [], gath, sg_wait, after, "gathered_wait_" + name)[1]

    def adam_big(t, gath, after):
        grads[t], deltas[t], new_m[t], new_v[t] = _adam_gathered(w[t], gath, m[t], v[t], c_arr, after, "adam_" + t)

    def behind(value, token):
        return value + token[0:1, 0:1]

    tags_b, tags_a = ("w_up", "w_down"), ("w_in", "w_out")
    swap_b, tok = swap_start([gw_up, gw_down.reshape(N_CHIPS, -1, D_MODEL)], "b")
    dx1, dy, dsh_f, dsc_f, dg_pre2, dgt_m, dg_post = _ffn_bwd_b(
        dup0, x1, y, dx2, w_up4, g_pre2, behind(sc_f, tok), sh_f, gt_m, g_post)
    exchange_b = exchange_start(swap_b, tags_b, dg_post, "b")
    (dz, grad_x, dcw, dcb, dwr, dwi, dbr, dbi, dspa, dng, dnb, dws, dbs_t, dglo, dggo, dsh_m, dsc_m,
     dg_pre) = _seqmix_bwd(lx, hst, stash, dy, w_out_b, seq_params, ws_t, behind(glo, exchange_b[4]), ggo,
                           xs, dx1, w_in4, g_pre, sc_m)
    gw_in = _wgrad(h, dz, N_CHIPS, "wgrad_in", True)
    gw_out = _wgrad(ycat, dy, 1, "wgrad_out", False)
    swap_a, tok = swap_start([gw_in, gw_out.reshape(N_CHIPS, -1, D_MODEL)], "a")

    dmod = jnp.concatenate([behind(dsh_m, tok), dsc_m, dgt_m, dsh_f, dsc_f, dgt_f], axis=1)
    dmod8 = _allgather8(dmod, "gather_dmod")[:, 0, :]
    small_grads = dict(
        g_mix_pre=dg_pre, g_mix_post=dg_post, conv_w=dcw, conv_b=dcb, w_rgate=dwr, b_rgate=dbr, w_igate=dwi,
        b_igate=dbi, lru_a=dspa, v_norm_g=dng, v_norm_b=dnb, w_spatial=dws, b_spatial=dbs_t, g_lru_out=dglo,
        g_gmlp_out=dggo, g_ffn_pre=dg_pre2, g_ffn_post=dg_post2, ffn_conv_w=dfw, ffn_conv_b=dfb,
        loss=loss)
    g_small = _allreduce_small(_pack_small(small_grads, dmod8), "reduce_small")
    total = g_small[_small_pieces("loss", 0, 1)[0][:3]]
    exchange_a = exchange_start(swap_a, tags_a, g_small, "a")
    gathered_b, tok = gathered_start(exchange_b, exchange_a[4], "b")

    grads["w_ada"], deltas["w_ada"], new_m["w_ada"], new_v["w_ada"], g_b_ada = _ada_bwd(
        c8, behind(dmod8, tok), chip_arr, w["w_ada"], m["w_ada"], v["w_ada"])
    rep = SMALL_REPLICATED
    small_out = _adam_small(g_small, {n: args[n] for n in rep}, {n: args["m_" + n] for n in rep},
                            {n: args["v_" + n] for n in rep})
    for n in rep:
        grads[n], deltas[n], new_m[n], new_v[n] = [group[n] for group in small_out]
    for n in SMALL_COLUMN_SHARDED:
        grads[n], deltas[n], new_m[n], new_v[n] = _adam_cols(n, g_small, args[n], args["m_" + n],
                                                             args["v_" + n], chip_arr)
    d_b, m_b, v_b = _adam(w["b_ada"], g_b_ada, m["b_ada"], v["b_ada"], "adam_b_ada")
    grads["b_ada"], deltas["b_ada"], new_m["b_ada"], new_v["b_ada"] = g_b_ada, d_b, m_b, v_b

    gath_up, gath_down = gathered_wait(gathered_b, d_b, "b")
    adam_big("w_down", gath_down, pos_arr)
    gathered_a, tok = gathered_start(exchange_a, deltas["w_down"], "a")
    adam_big("w_up", gath_up, tok)
    gath_in, gath_out = gathered_wait(gathered_a, deltas["w_up"], "a")
    adam_big("w_in", gath_in, pos_arr)
    adam_big("w_out", gath_out, pos_arr)

    outs = [total, grad_x[None]]
    for group in (grads, deltas, new_m, new_v):
        outs.extend(group[n].reshape(args[n].shape) for n in names)
    return tuple(outs)
```

```python
import functools
import math

import jax
import jax.numpy as jnp
from jax import lax
from jax.experimental import pallas as pl
from jax.experimental.pallas import tpu as pltpu

F32 = jnp.float32
BF16 = jnp.bfloat16
MESH = pl.DeviceIdType.MESH

D_MODEL = 1024
LRU_WIDTH = 512
LRU_HEADS = 8
GMLP_GROUPS = 4
GMLP_BLOCK = 128
CHUNK = 64
D_FF = 3072
N_MOD = 6
EPS = 1e-6
LRU_C = 8.0
N_CHIPS = 4
N_DEV = 8

ADAM_LR = 0.001
ADAM_B1 = 0.9
ADAM_B2 = 0.999
ADAM_EPS = 1e-08
ADAM_WD = 0.01
ADAM_STEP = 10

GELU_C0 = math.sqrt(2.0 / math.pi)
GELU_C1 = 0.044715

VMEM_LIMIT_BYTES = 56 * 1024 * 1024
SUBLANES = 8
BF16_SUBLANES = 16
FFN_CHUNK = 768
SUB_ROWS = 256


def _gelu_gate(x):
    x2 = x * x
    z = x * ((2.0 * GELU_C0 * GELU_C1) * x2 + 2.0 * GELU_C0)
    return 1.0 / (1.0 + jnp.exp(-z)), x2


def _gelu(x):
    t = jnp.tanh(GELU_C0 * (x + GELU_C1 * x * x * x))
    return 0.5 * x * (1.0 + t)


def _gelu_and_grad(x):
    s, x2 = _gelu_gate(x)
    g = x * s
    dz = (6.0 * GELU_C0 * GELU_C1) * x2 + 2.0 * GELU_C0
    return g, s + g * (1.0 - s) * dz


def _sigmoid(x):
    return 1.0 / (1.0 + jnp.exp(-x))


def _log1p(u):
    w = 1.0 + u
    return jnp.where(w == 1.0, u, jnp.log(w) * (u / (w - 1.0)))


def _softplus(x):
    return jnp.maximum(x, 0.0) + _log1p(jnp.exp(-jnp.abs(x)))


def _neg_expm1(x):
    u = jnp.exp(x)
    um1 = u - 1.0
    tiny = um1 == 0.0
    small = um1 * (x / jnp.log(jnp.where(tiny, 2.0, jnp.maximum(u, 0.25))))
    return -jnp.where(tiny, x, jnp.where(x < -1.0, um1, small))


def _msq_rsqrt(v):
    return lax.rsqrt(jnp.mean(v * v, axis=-1, keepdims=True) + EPS)


def _rms_bwd(dyn, yn, r):
    return r * (dyn - yn * jnp.mean(dyn * yn, axis=-1, keepdims=True))


def _colsum(v):
    return jnp.sum(v, axis=0, keepdims=True)


def _shift_down(cur, prev8, k):
    rolled = pltpu.roll(cur, k, 0)
    head = pltpu.roll(prev8, k, 0)
    row8 = lax.broadcasted_iota(jnp.int32, (SUBLANES, cur.shape[1]), 0)
    first = jnp.where(row8 < k, head, rolled[0:SUBLANES])
    return jnp.concatenate([first, rolled[SUBLANES:]], axis=0)


def _shift_up(cur, next8, k):
    t = cur.shape[0]
    rolled = pltpu.roll(cur, t - k, 0)
    tail = pltpu.roll(next8, SUBLANES - k, 0)
    row8 = lax.broadcasted_iota(jnp.int32, (SUBLANES, cur.shape[1]), 0)
    last = jnp.where(row8 >= SUBLANES - k, tail, rolled[t - SUBLANES:])
    return jnp.concatenate([rolled[:t - SUBLANES], last], axis=0)


def _scan_fwd(a, b):
    t = a.shape[0]
    row = lax.broadcasted_iota(jnp.int32, a.shape, 0)
    d = 1
    while d < t:
        keep = row >= d
        a_s = jnp.where(keep, pltpu.roll(a, d, 0), 1.0)
        b_s = jnp.where(keep, pltpu.roll(b, d, 0), 0.0)
        b = a * b_s + b
        a = a * a_s
        d *= 2
    return a, b


def _scan_bwd(a, g):
    t = a.shape[0]
    row = lax.broadcasted_iota(jnp.int32, a.shape, 0)
    d = 1
    while d < t:
        keep = row < t - d
        a_s = jnp.where(keep, pltpu.roll(a, t - d, 0), 1.0)
        g_s = jnp.where(keep, pltpu.roll(g, t - d, 0), 0.0)
        g = a * g_s + g
        a = a * a_s
        d *= 2
    return a, g


def _dot(a, b):
    return jnp.dot(a, b, preferred_element_type=F32)


def _dot_nt(a, b):
    return lax.dot_general(a, b, (((1,), (1,)), ((), ())), preferred_element_type=F32)


def _dot_tn(a, b):
    return lax.dot_general(a, b, (((0,), (0,)), ((), ())), preferred_element_type=F32)


def _rows(ts, cols, rev_of=None):
    if rev_of is None:
        return pl.BlockSpec((ts, cols), lambda i: (i, 0))
    return pl.BlockSpec((ts, cols), lambda i: (rev_of - 1 - i, 0))


def _halo_prev(ts, cols, halo, rev_of=None, col_block=0):
    per = ts // halo
    if rev_of is None:
        return pl.BlockSpec((halo, cols), lambda i: (jnp.maximum(i * per - 1, 0), col_block))
    return pl.BlockSpec((halo, cols), lambda i: (jnp.maximum((rev_of - 1 - i) * per - 1, 0), col_block))


def _full(shape):
    nd = len(shape)
    return pl.BlockSpec(shape, lambda *_: (0,) * nd)


_RESIDENT = pl.BlockSpec(memory_space=pltpu.VMEM)


def _params(sem):
    return pltpu.CompilerParams(dimension_semantics=sem, vmem_limit_bytes=VMEM_LIMIT_BYTES)


def _sds(shape, dtype):
    return jax.ShapeDtypeStruct(shape, dtype)


def _sub_tiles(ts):
    return [slice(r0, r0 + SUB_ROWS) for r0 in range(0, ts, SUB_ROWS)]


N_STASH = 12
(ST_XC, ST_R, ST_IG, ST_A, ST_MULT, ST_GL, ST_DGL, ST_U, ST_DU, ST_Q, ST_VHAT, ST_SPB) = range(N_STASH)


def _seq_param_specs():
    return [_full((4, 512)), _full((1, 512)), _full((512, 512)), _full((512, 512)), _full((1, 512)),
            _full((1, 512)), _full((1, 512)), _full((1, 512)), _full((1, 512)), _full((4, 128, 128)),
            _full((128, 4))]


def _seqmix(x, sc, sh, g_pre, w_in4, seq_params, glo, ggo, ts=256):
    s, d = x.shape
    nt = s // ts

    def body(x_ref, sc_ref, sh_ref, gpre_ref, win_ref, cw_ref, cb_ref, bdr_ref, bdi_ref, br_ref, bi_ref, la_ref,
             ng_ref, nb_ref, ws_ref, bst_ref, glo_ref, ggo_ref, hin_ref, lx_ref, ycat_ref, hst_ref, st_ref,
             hcarry, lxprev, sp_scr):
        i = pl.program_id(0)

        @pl.when(i == 0)
        def _():
            hcarry[...] = jnp.zeros_like(hcarry)
            lxprev[...] = jnp.zeros_like(lxprev)

        xv = x_ref[...]
        hin = ((xv * _msq_rsqrt(xv) * gpre_ref[...]) * (1.0 + sc_ref[...]) + sh_ref[...]).astype(BF16)
        hin_ref[...] = hin
        z = [_dot(hin, win_ref[k]) for k in range(N_CHIPS)]

        lx = z[0]
        lx_ref[...] = lx
        prev8 = lxprev[...]
        lxprev[...] = lx[ts - SUBLANES:, :]
        xc = (cw_ref[3:4, :] * lx + cw_ref[2:3, :] * _shift_down(lx, prev8, 1)
              + cw_ref[1:2, :] * _shift_down(lx, prev8, 2) + cw_ref[0:1, :] * _shift_down(lx, prev8, 3)
              + cb_ref[...])
        xcb = xc.astype(BF16)
        r = _sigmoid(_dot(xcb, bdr_ref[...]) + br_ref[...])
        ig = _sigmoid(_dot(xcb, bdi_ref[...]) + bi_ref[...])
        log_a = (-LRU_C) * r * _softplus(-la_ref[...])
        a = jnp.exp(log_a)
        mult = jnp.sqrt(_neg_expm1(2.0 * log_a))
        acum, hloc = _scan_fwd(a, mult * (ig * xc))
        h = hloc + acum * hcarry[...]
        hcarry[...] = h[ts - 1:ts, :]
        hst_ref[...] = h
        gl, dgl = _gelu_and_grad(z[1])
        y_l = h * gl
        for slot, val in ((ST_XC, xc), (ST_R, r), (ST_IG, ig), (ST_A, a), (ST_MULT, mult), (ST_GL, gl),
                          (ST_DGL, dgl)):
            st_ref[slot] = val

        u, du = _gelu_and_grad(z[2])
        vg, dvg = _gelu_and_grad(z[3])
        vc = vg - jnp.mean(vg, axis=-1, keepdims=True)
        rstd = lax.rsqrt(jnp.mean(vc * vc, axis=-1, keepdims=True) + EPS)
        vhat = vc * rstd
        vb = (vhat * ng_ref[...] + nb_ref[...]).astype(BF16)
        for n in range(ts // GMLP_BLOCK):
            rs = slice(n * GMLP_BLOCK, (n + 1) * GMLP_BLOCK)
            for g in range(GMLP_GROUPS):
                cs = slice(g * 128, (g + 1) * 128)
                sp_scr[rs, cs] = _dot(ws_ref[g], vb[rs, cs]) + bst_ref[:, g:g + 1]
        spb = sp_scr[...]
        y_g = u * spb
        for slot, val in ((ST_U, u), (ST_DU, du), (ST_Q, rstd * dvg), (ST_VHAT, vhat), (ST_SPB, spb)):
            st_ref[slot] = val

        ycat_ref[:, 0:512] = (y_l * _msq_rsqrt(y_l) * glo_ref[...]).astype(BF16)
        ycat_ref[:, 512:1024] = (y_g * _msq_rsqrt(y_g) * ggo_ref[...]).astype(BF16)

    vec = _full((1, d))
    return pl.pallas_call(
        body, grid=(nt,), name="seqmix",
        in_specs=[_rows(ts, d), vec, vec, vec, _full(w_in4.shape)] + _seq_param_specs()
        + [_full((1, 512)), _full((1, 512))],
        out_specs=[_rows(ts, d), _rows(ts, 512), _rows(ts, d), _rows(ts, 512),
                   pl.BlockSpec((N_STASH, ts, 512), lambda i: (0, i, 0))],
        out_shape=[_sds((s, d), BF16), _sds((s, 512), F32), _sds((s, d), BF16), _sds((s, 512), F32),
                   _sds((N_STASH, s, 512), F32)],
        scratch_shapes=[pltpu.VMEM((1, 512), F32), pltpu.VMEM((SUBLANES, 512), F32), pltpu.VMEM((ts, 512), F32)],
        compiler_params=_params(("arbitrary",)),
    )(x, sc, sh, g_pre, w_in4, *seq_params, glo, ggo)


def _mix_out(ycat, x, w_out, gt_m, g_post, g_pre2, sc_f, sh_f, ts=512):
    s, d = x.shape

    def body(yc_ref, x_ref, w_ref, gt_ref, gp_ref, g2_ref, sc_ref, sh_ref, y_ref, x1_ref, h2_ref):
        for rs in _sub_tiles(ts):
            y = _dot(yc_ref[rs, :], w_ref[...])
            y_ref[rs, :] = y
            x1 = x_ref[rs, :] + gt_ref[...] * (y * _msq_rsqrt(y) * gp_ref[...])
            x1_ref[rs, :] = x1
            h2 = (x1 * _msq_rsqrt(x1) * g2_ref[...]) * (1.0 + sc_ref[...]) + sh_ref[...]
            h2_ref[rs, :] = h2.astype(BF16)

    vec = _full((1, d))
    return pl.pallas_call(
        body, grid=(s // ts,), name="mix_out",
        in_specs=[_rows(ts, d), _rows(ts, d), _full((d, d)), vec, vec, vec, vec, vec],
        out_specs=[_rows(ts, d), _rows(ts, d), _rows(ts, d)],
        out_shape=[_sds((s, d), F32), _sds((s, d), F32), _sds((s, d), BF16)],
        compiler_params=_params(("parallel",)),
    )(ycat, x, w_out, gt_m, g_post, g_pre2, sc_f, sh_f)


def _ffn_cols(j):
    per = (2 * D_FF // N_CHIPS) // FFN_CHUNK
    return j // per, (j % per) * FFN_CHUNK, j * FFN_CHUNK


def _ffn_fwd(h2, x1, tgt, w_up4, w_down, fw, fb, gt_f, g_post, ts=256):
    s, d = x1.shape
    nch = D_FF // FFN_CHUNK

    def body(h2_ref, x1_ref, tgt_ref, wup_ref, wdn_ref, fw_ref, fb_ref, gt_ref, gp_ref,
             up0_ref, pre_ref, act_ref, dy2_ref, dx2_ref, loss_ref, dgt_ref, dgp_ref, tail_ref):
        i = pl.program_id(0)

        @pl.when(i == 0)
        def _():
            tail_ref[...] = jnp.zeros_like(tail_ref)
            loss_ref[...] = jnp.zeros_like(loss_ref)
            dgt_ref[...] = jnp.zeros_like(dgt_ref)
            dgp_ref[...] = jnp.zeros_like(dgp_ref)

        hb = h2_ref[...]

        def up_project(j):
            sh_g, off, _ = _ffn_cols(j)
            return [_dot(hb, wup_ref[shard, :, off:off + FFN_CHUNK]) for shard in (sh_g, sh_g + 2)]

        y2 = jnp.zeros((ts, d), F32)
        ahead = up_project(0)
        for j in range(nch):
            _, _, col = _ffn_cols(j)
            ubs = ahead
            if j + 1 < nch:
                ahead = up_project(j + 1)
            halves = []
            for u, c0 in zip(ubs, (col, D_FF + col)):
                cs = slice(c0, c0 + FFN_CHUNK)
                up0_ref[:, cs] = u.astype(BF16)
                prev8 = tail_ref[:, cs]
                tail_ref[:, cs] = u[ts - SUBLANES:, :]
                halves.append(fw_ref[2:3, cs] * u + fw_ref[1:2, cs] * _shift_down(u, prev8, 1)
                              + fw_ref[0:1, cs] * _shift_down(u, prev8, 2) + fb_ref[:, cs])
                pre_ref[:, cs] = halves[-1].astype(BF16)
            act = (_gelu(halves[0]) * halves[1]).astype(BF16)
            act_ref[:, col:col + FFN_CHUNK] = act
            y2 = y2 + _dot(act, wdn_ref[col:col + FFN_CHUNK, :])
        r2 = _msq_rsqrt(y2)
        yn = y2 * r2
        yng = yn * gp_ref[...]
        e = x1_ref[...] + gt_ref[...] * yng - tgt_ref[...]
        loss_ref[...] += jnp.sum(e * e) * (0.5 / d)
        dx2 = e * (1.0 / d)
        dx2_ref[...] = dx2
        dgt_ref[...] += _colsum(dx2 * yng)
        dyng = dx2 * gt_ref[...]
        dgp_ref[...] += _colsum(dyng * yn)
        dy2_ref[...] = _rms_bwd(dyng * gp_ref[...], yn, r2).astype(BF16)

    vec = _full((1, d))
    return pl.pallas_call(
        body, grid=(s // ts,), name="ffn_fwd",
        in_specs=[_rows(ts, d), _rows(ts, d), _rows(ts, d), _RESIDENT, _RESIDENT,
                  _full((3, 2 * D_FF)), _full((1, 2 * D_FF)), vec, vec],
        out_specs=[_rows(ts, 2 * D_FF), _rows(ts, 2 * D_FF), _rows(ts, D_FF), _rows(ts, d), _rows(ts, d),
                   _full((1, 128)), vec, vec],
        out_shape=[_sds((s, 2 * D_FF), BF16), _sds((s, 2 * D_FF), BF16), _sds((s, D_FF), BF16), _sds((s, d), BF16),
                   _sds((s, d), F32), _sds((1, 128), F32), _sds((1, d), F32), _sds((1, d), F32)],
        scratch_shapes=[pltpu.VMEM((SUBLANES, 2 * D_FF), F32)],
        compiler_params=_params(("arbitrary",)),
    )(h2, x1, tgt, w_up4, w_down, fw, fb, gt_f, g_post)


def _shift_up_mxu(vb, up_mat, next8, k):
    t = vb.shape[0]
    main = _dot(up_mat, vb)
    tail = pltpu.roll(next8, SUBLANES - k, 0)
    row8 = lax.broadcasted_iota(jnp.int32, next8.shape, 0)
    last = main[t - SUBLANES:] + jnp.where(row8 >= SUBLANES - k, tail, 0.0)
    return jnp.concatenate([main[:t - SUBLANES], last], axis=0)


def _ffn_bwd_a(dy2, pre, up0, w_down, fw, ts=256):
    s, d = dy2.shape
    nt = s // ts
    nch = D_FF // FFN_CHUNK
    wide = 2 * D_FF
    up_mats = jnp.stack([jnp.eye(ts, k=1, dtype=BF16), jnp.eye(ts, k=2, dtype=BF16)])

    def body(dy2_ref, pre_ref, up0_ref, wdn_ref, fw_ref, um_ref, dup0_ref, dfw_ref, dfb_ref, next_ref):
        i = pl.program_id(0)

        @pl.when(i == 0)
        def _():
            next_ref[...] = jnp.zeros_like(next_ref)
            dfw_ref[...] = jnp.zeros_like(dfw_ref)
            dfb_ref[...] = jnp.zeros_like(dfb_ref)

        dyb = dy2_ref[...]
        for j in range(nch):
            _, _, col = _ffn_cols(j)
            dact = _dot_nt(dyb, wdn_ref[col:col + FFN_CHUNK, :])
            gl, dgl = _gelu_and_grad(pre_ref[:, col:col + FFN_CHUNK].astype(F32))
            dpre = (dact * pre_ref[:, D_FF + col:D_FF + col + FFN_CHUNK].astype(F32) * dgl, dact * gl)
            for half, c0 in enumerate((col, D_FF + col)):
                cs = slice(c0, c0 + FFN_CHUNK)
                dp = dpre[half]
                dpb = dp.astype(BF16)
                nxt = next_ref[:, cs]
                next_ref[:, cs] = dpb.astype(F32)[0:SUBLANES, :]
                su1 = _shift_up_mxu(dpb, um_ref[0], nxt, 1)
                su2 = _shift_up_mxu(dpb, um_ref[1], nxt, 2)
                u = up0_ref[:, cs].astype(F32)
                dfb_ref[:, cs] += _colsum(dp)
                dfw_ref[2:3, cs] += _colsum(dp * u)
                dfw_ref[1:2, cs] += _colsum(su1 * u)
                dfw_ref[0:1, cs] += _colsum(su2 * u)
                dup0 = fw_ref[2:3, cs] * dp + fw_ref[1:2, cs] * su1 + fw_ref[0:1, cs] * su2
                dup0_ref[:, cs] = dup0.astype(BF16)

    return pl.pallas_call(
        body, grid=(nt,), name="ffn_bwd_a",
        in_specs=[_rows(ts, d, nt), _rows(ts, wide, nt), _rows(ts, wide, nt), _RESIDENT,
                  _full((3, wide)), _full((2, ts, ts))],
        out_specs=[_rows(ts, wide, nt), _full((3, wide)), _full((1, wide))],
        out_shape=[_sds((s, wide), BF16), _sds((3, wide), F32), _sds((1, wide), F32)],
        scratch_shapes=[pltpu.VMEM((SUBLANES, wide), F32)],
        compiler_params=_params(("arbitrary",)),
    )(dy2, pre, up0, w_down, fw, up_mats)


def _ffn_bwd_b(dup0, x1, y, dx2, w_up4, g_pre2, sc_f, sh_f, gt_m, g_post_m, ts=512):
    s, d = x1.shape
    shard_cols = 2 * D_FF // N_CHIPS

    def body(dup_ref, x1_ref, y_ref, dx2_ref, wup_ref, g2_ref, sc_ref, sh_ref, gt_ref, gp_ref,
             dx1_ref, dy_ref, dsh_ref, dsc_ref, dg2_ref, dgt_ref, dgp_ref):
        i = pl.program_id(0)

        @pl.when(i == 0)
        def _():
            for ref in (dsh_ref, dsc_ref, dg2_ref, dgt_ref, dgp_ref):
                ref[...] = jnp.zeros_like(ref)

        for rs in _sub_tiles(ts):
            dh2 = jnp.zeros((SUB_ROWS, d), F32)
            for k in range(N_CHIPS):
                dh2 = dh2 + _dot_nt(dup_ref[rs, k * shard_cols:(k + 1) * shard_cols], wup_ref[k])
            x1v = x1_ref[rs, :]
            r2 = _msq_rsqrt(x1v)
            xn = x1v * r2
            hn = xn * g2_ref[...]
            dsh_ref[...] += _colsum(dh2)
            dsc_ref[...] += _colsum(dh2 * hn)
            dhn = dh2 * (1.0 + sc_ref[...])
            dg2_ref[...] += _colsum(dhn * xn)
            dx1 = dx2_ref[rs, :] + _rms_bwd(dhn * g2_ref[...], xn, r2)
            dx1_ref[rs, :] = dx1
            yv = y_ref[rs, :]
            ry = _msq_rsqrt(yv)
            yn = yv * ry
            dgt_ref[...] += _colsum(dx1 * (yn * gp_ref[...]))
            dyng = dx1 * gt_ref[...]
            dgp_ref[...] += _colsum(dyng * yn)
            dy_ref[rs, :] = _rms_bwd(dyng * gp_ref[...], yn, ry).astype(BF16)

    vec = _full((1, d))
    return pl.pallas_call(
        body, grid=(s // ts,), name="ffn_bwd_b",
        in_specs=[_rows(ts, 2 * D_FF), _rows(ts, d), _rows(ts, d), _rows(ts, d), _RESIDENT,
                  vec, vec, vec, vec, vec],
        out_specs=[_rows(ts, d), _rows(ts, d), vec, vec, vec, vec, vec],
        out_shape=[_sds((s, d), F32), _sds((s, d), BF16)] + [_sds((1, d), F32)] * 5,
        compiler_params=_params(("arbitrary",)),
    )(dup0, x1, y, dx2, w_up4, g_pre2, sc_f, sh_f, gt_m, g_post_m)


def _seqmix_bwd(lru_x, hst, stash, dy, w_out, seq_params, ws_t, glo, ggo, x, dx1, w_in4, g_pre, sc_m, ts=256):
    s, d = x.shape
    nt = s // ts
    small_shapes = [(4, 512), (1, 512), (512, 512), (512, 512), (1, 512), (1, 512), (1, 512),
                    (1, 512), (1, 512), (4, 128, 128), (128, 4), (1, 512), (1, 512),
                    (1, d), (1, d), (1, d)]

    def body(lx_ref, hst_ref, hprev_ref, st_ref, dy_ref, wout_ref, cw_ref, cb_ref, bdr_ref, bdi_ref, br_ref,
             bi_ref, la_ref, ng_ref, nb_ref, ws_ref, bst_ref, wst_ref, glo_ref, ggo_ref, x_ref, dx1_ref, win_ref,
             gpre_ref, scm_ref, dz_ref, gx_ref, *rest):
        small_refs = rest[:16]
        (dcw_ref, dcb_ref, dwr_ref, dwi_ref, dbr_ref, dbi_ref, dspa_ref, dng_ref, dnb_ref, dws_ref, dbs_ref,
         dglo_ref, dggo_ref, dsh_ref, dsc_ref, dgpre_ref) = small_refs
        gcarry, anext, dxcnext, dv_scr = rest[16:]
        i = pl.program_id(0)

        @pl.when(i == 0)
        def _():
            for ref in small_refs:
                ref[...] = jnp.zeros_like(ref)
            gcarry[...] = jnp.zeros_like(gcarry)
            anext[...] = jnp.ones_like(anext)
            dxcnext[...] = jnp.zeros_like(dxcnext)

        first_tile = i == nt - 1
        xc, r, ig, a, mult = st_ref[ST_XC], st_ref[ST_R], st_ref[ST_IG], st_ref[ST_A], st_ref[ST_MULT]
        gl, u, spb, vhat = st_ref[ST_GL], st_ref[ST_U], st_ref[ST_SPB], st_ref[ST_VHAT]
        lx = lx_ref[...]
        h = hst_ref[...]
        hprev = _shift_down(h, jnp.where(first_tile, 0.0, hprev_ref[...]), 1)
        y_l = h * gl
        y_g = u * spb

        dycat = _dot_nt(dy_ref[...], wout_ref[...])

        dz_parts = {}

        def emit_dz(k, val):
            dz_parts[k] = val.astype(BF16)
            dz_ref[:, k * 512:(k + 1) * 512] = dz_parts[k]

        rl = _msq_rsqrt(y_l)
        yln = y_l * rl
        dyl = dycat[:, 0:512]
        dglo_ref[...] += _colsum(dyl * yln)
        dy_l = _rms_bwd(dyl * glo_ref[...], yln, rl)
        rg = _msq_rsqrt(y_g)
        ygn = y_g * rg
        dyg = dycat[:, 512:1024]
        dggo_ref[...] += _colsum(dyg * ygn)
        dy_g = _rms_bwd(dyg * ggo_ref[...], ygn, rg)

        emit_dz(1, dy_l * h * st_ref[ST_DGL])
        a_up = _shift_up(a, anext[...], 1)
        acum, gloc = _scan_bwd(a_up, dy_l * gl)
        gg = gloc + acum * gcarry[...]
        gcarry[...] = gg[0:1, :]
        anext[...] = a[0:SUBLANES, :]
        da = gg * hprev
        t1 = gg * mult
        di = t1 * xc
        dxc = t1 * ig
        dmult = gg * ig * xc
        dla = da * a - dmult * (a * a / mult)
        dspa_ref[...] += _colsum(dla * r) * (-LRU_C)
        dpr = dla * ((-LRU_C) * _softplus(-la_ref[...])) * r * (1.0 - r)
        dpi = di * ig * (1.0 - ig)
        dbr_ref[...] += _colsum(dpr)
        dbi_ref[...] += _colsum(dpi)
        dprb = dpr.astype(BF16)
        dpib = dpi.astype(BF16)
        xcb = xc.astype(BF16)
        dwr_ref[...] += _dot_tn(xcb, dprb)
        dwi_ref[...] += _dot_tn(xcb, dpib)
        dxc = dxc + _dot_nt(dprb, bdr_ref[...]) + _dot_nt(dpib, bdi_ref[...])
        nxt = dxcnext[...]
        dxcnext[...] = dxc[0:SUBLANES, :]
        up1, up2, up3 = _shift_up(dxc, nxt, 1), _shift_up(dxc, nxt, 2), _shift_up(dxc, nxt, 3)
        dcb_ref[...] += _colsum(dxc)
        dcw_ref[3:4, :] += _colsum(dxc * lx)
        dcw_ref[2:3, :] += _colsum(up1 * lx)
        dcw_ref[1:2, :] += _colsum(up2 * lx)
        dcw_ref[0:1, :] += _colsum(up3 * lx)
        dlx = cw_ref[3:4, :] * dxc + cw_ref[2:3, :] * up1 + cw_ref[1:2, :] * up2 + cw_ref[0:1, :] * up3
        emit_dz(0, dlx)

        emit_dz(2, dy_g * spb * st_ref[ST_DU])
        dsp = dy_g * u
        vb = (vhat * ng_ref[...] + nb_ref[...]).astype(BF16)
        for n in range(ts // GMLP_BLOCK):
            rs = slice(n * GMLP_BLOCK, (n + 1) * GMLP_BLOCK)
            for g in range(GMLP_GROUPS):
                cs = slice(g * 128, (g + 1) * 128)
                dbs_ref[:, g:g + 1] += jnp.sum(dsp[rs, cs], axis=1, keepdims=True)
                blk = dsp[rs, cs].astype(BF16)
                dws_ref[g] += _dot_nt(blk, vb[rs, cs])
                dv_scr[rs, cs] = _dot(wst_ref[g], blk)
        dv = dv_scr[...]
        dng_ref[...] += _colsum(dv * vhat)
        dnb_ref[...] += _colsum(dv)
        dvh = dv * ng_ref[...]
        dvg = dvh - jnp.mean(dvh, axis=-1, keepdims=True) - vhat * jnp.mean(dvh * vhat, axis=-1, keepdims=True)
        emit_dz(3, dvg * st_ref[ST_Q])

        dh = _dot_nt(dz_parts[0], win_ref[0])
        for k in range(1, N_CHIPS):
            dh = dh + _dot_nt(dz_parts[k], win_ref[k])
        xv = x_ref[...]
        rx = _msq_rsqrt(xv)
        xn = xv * rx
        dsh_ref[...] += _colsum(dh)
        dsc_ref[...] += _colsum(dh * (xn * gpre_ref[...]))
        dhn = dh * (1.0 + scm_ref[...])
        dgpre_ref[...] += _colsum(dhn * xn)
        gx_ref[...] = dx1_ref[...] + _rms_bwd(dhn * gpre_ref[...], xn, rx)

        @pl.when(i == nt - 1)
        def _():
            pos = lax.broadcasted_iota(jnp.int32, (GMLP_BLOCK, GMLP_BLOCK), 0) // CHUNK
            src = lax.broadcasted_iota(jnp.int32, (GMLP_BLOCK, GMLP_BLOCK), 1) // CHUNK
            for g in range(GMLP_GROUPS):
                dws_ref[g] = jnp.where(src <= pos, dws_ref[g], 0.0)
            dspa_ref[...] = dspa_ref[...] * (-_sigmoid(-la_ref[...]))

    vec = _full((1, d))
    in_specs = ([_rows(ts, 512, nt), _rows(ts, 512, nt), _halo_prev(ts, 512, SUBLANES, nt),
                 pl.BlockSpec((N_STASH, ts, 512), lambda i: (0, nt - 1 - i, 0)), _rows(ts, d, nt),
                 _full((d, d))]
                + _seq_param_specs() + [_full((4, 128, 128)), _full((1, 512)), _full((1, 512))]
                + [_rows(ts, d, nt), _rows(ts, d, nt), _full(w_in4.shape), vec, vec])
    return pl.pallas_call(
        body, grid=(nt,), name="seqmix_bwd",
        in_specs=in_specs,
        out_specs=[_rows(ts, 2048, nt), _rows(ts, d, nt)] + [_full(sh) for sh in small_shapes],
        out_shape=[_sds((s, 2048), BF16), _sds((s, d), F32)] + [_sds(sh, F32) for sh in small_shapes],
        scratch_shapes=[pltpu.VMEM((1, 512), F32), pltpu.VMEM((SUBLANES, 512), F32),
                        pltpu.VMEM((SUBLANES, 512), F32), pltpu.VMEM((ts, 512), F32)],
        compiler_params=_params(("arbitrary",)),
    )(lru_x, hst, hst, stash, dy, w_out, *seq_params, ws_t, glo, ggo, x, dx1, w_in4, g_pre, sc_m)


def _wgrad(a, b, n_chunks, name, chunk_major, ts=2048):
    s, m = a.shape
    n = b.shape[1]
    nc = n // n_chunks
    nt = s // ts

    def body(a_ref, b_ref, o_ref, acc):
        i = pl.program_id(1)

        @pl.when(i == 0)
        def _():
            acc[...] = jnp.zeros_like(acc)

        acc[...] += _dot_tn(a_ref[...], b_ref[...])

        @pl.when(i == nt - 1)
        def _():
            if chunk_major:
                o_ref[0] = acc[...].astype(BF16)
            else:
                o_ref[...] = acc[...].astype(BF16)

    if chunk_major:
        out_spec, out_shape = pl.BlockSpec((1, m, nc), lambda c, i: (c, 0, 0)), _sds((n_chunks, m, nc), BF16)
    else:
        out_spec, out_shape = pl.BlockSpec((m, nc), lambda c, i: (0, c)), _sds((m, n), BF16)
    return pl.pallas_call(
        body, grid=(n_chunks, nt), name=name,
        in_specs=[pl.BlockSpec((ts, m), lambda c, i: (i, 0)), pl.BlockSpec((ts, nc), lambda c, i: (i, c))],
        out_specs=out_spec,
        out_shape=out_shape,
        scratch_shapes=[pltpu.VMEM((m, nc), F32)],
        compiler_params=_params(("parallel", "arbitrary")),
    )(a, b)


def _block_diag(w):
    heads, hd, _ = w.shape
    eye = jnp.eye(heads, dtype=w.dtype)
    return (eye[:, None, :, None] * w[:, :, None, :]).reshape(heads * hd, heads * hd)


def _seq_params(small):
    row = lambda v: v.reshape(1, -1)
    pos = jnp.arange(GMLP_BLOCK)
    mask = (pos[None, :] // CHUNK) <= (pos[:, None] // CHUNK)
    ws = jnp.where(mask[None], small["w_spatial"], 0.0)
    seq_params = (small["conv_w"], row(small["conv_b"]),
                  _block_diag(small["w_rgate"]).astype(BF16), _block_diag(small["w_igate"]).astype(BF16),
                  row(small["b_rgate"]), row(small["b_igate"]), row(small["lru_a"]),
                  row(small["v_norm_g"]), row(small["v_norm_b"]), ws.astype(BF16), small["b_spatial"].T)
    return seq_params, jnp.swapaxes(ws, 1, 2).astype(BF16)


_ANY = pl.BlockSpec(memory_space=pl.ANY)
_CHIP_FLIPS = ((1, 0), (0, 1), (1, 1))


def _position():
    return lax.axis_index("x"), lax.axis_index("y"), lax.axis_index("c")


def _flip(v, f):
    return 1 - v if f else v


def _remote(src, dst, send_sem, recv_sem, peer):
    return pltpu.make_async_remote_copy(src_ref=src, dst_ref=dst, send_sem=send_sem, recv_sem=recv_sem,
                                        device_id=peer, device_id_type=MESH)


def _allgather8(block, name):
    r, n = block.shape

    def body(x_ref, gath, send_sems, recv_sems, loc_sem):
        x, y, c = _position()
        me = 4 * x + 2 * y + c
        loc = pltpu.make_async_copy(x_ref, gath.at[me], loc_sem)
        loc.start()
        peers = []
        for k in range(1, N_DEV):
            px, py, pc = _flip(x, k & 4), _flip(y, k & 2), _flip(c, k & 1)
            peers.append((px, py, pc))
            _remote(x_ref, gath.at[me], send_sems.at[k - 1], recv_sems.at[k - 1], (px, py, pc)).start()
        for k, (px, py, pc) in enumerate(peers):
            src = 4 * px + 2 * py + pc
            _remote(x_ref, gath.at[src], send_sems.at[k], recv_sems.at[k], (px, py, pc)).wait_recv()
        for k, peer in enumerate(peers):
            _remote(x_ref, gath.at[me], send_sems.at[k], recv_sems.at[k], peer).wait_send()
        loc.wait()

    return pl.pallas_call(
        body, name=name, out_shape=_sds((N_DEV, r, n), F32),
        in_specs=[pl.BlockSpec(memory_space=pltpu.VMEM)], out_specs=pl.BlockSpec(memory_space=pltpu.VMEM),
        scratch_shapes=[pltpu.SemaphoreType.DMA((N_DEV - 1,)), pltpu.SemaphoreType.DMA((N_DEV - 1,)),
                        pltpu.SemaphoreType.DMA],
        compiler_params=pltpu.CompilerParams(vmem_limit_bytes=VMEM_LIMIT_BYTES),
    )(block)


def _half(ref, c, rows):
    hr = rows // 2
    return ref.at[pl.ds(pl.multiple_of(c * hr, BF16_SUBLANES), hr), :]


def _chip_sum(part, recv, pos_arr, name):
    _, rows, cols = part.shape
    hr = rows // 2

    def body(pos_ref, p_ref, r_ref, o_ref, g_ref):
        total = (p_ref[...].astype(F32) + r_ref[...].astype(F32)).astype(BF16)
        o_ref[...] = total

        @pl.when(pl.program_id(0) == pos_ref[1])
        def _():
            g_ref[0] = total

    grid_spec = pltpu.PrefetchScalarGridSpec(
        num_scalar_prefetch=1, grid=(N_CHIPS,),
        in_specs=[pl.BlockSpec((1, hr, cols), lambda k, pos: (k, pos[0], 0)),
                  pl.BlockSpec((1, hr, cols), lambda k, pos: (k, 0, 0))],
        out_specs=[pl.BlockSpec((1, hr, cols), lambda k, pos: (k, 0, 0)),
                   pl.BlockSpec((1, 1, hr, cols), lambda k, pos: (0, pos[1], 0, 0))])
    return pl.pallas_call(
        body, name=name, grid_spec=grid_spec,
        out_shape=[_sds((N_CHIPS, hr, cols), BF16), _sds((2, N_CHIPS, hr, cols), BF16)],
        compiler_params=_params(("arbitrary",)),
    )(pos_arr, part, recv)


_HBM = pl.BlockSpec(memory_space=pltpu.HBM)
_SEM = pl.BlockSpec(memory_space=pltpu.SEMAPHORE)
_EFFECT = pltpu.SideEffectType.DATAFLOW_SIDE_EFFECTING


def _in_hbm(a):
    return pltpu.with_memory_space_constraint(a, pltpu.HBM)


def _split_start(srcs, lands, plan, n_copies, after, name):
    ns, nl = len(srcs), len(lands)
    bufs = list(srcs) + list(lands)

    def body(*refs):
        send_sems, recv_sems = refs[ns + nl + 1], refs[ns + nl + 2]
        token = refs[-1]
        for k, (src, dst, peer) in enumerate(plan(refs[:ns], refs[ns:ns + nl])):
            _remote(src, dst, send_sems.at[k], recv_sems.at[k], peer).start()
        token[...] = jnp.zeros_like(token)

    out = pl.pallas_call(
        body, name=name,
        out_shape=(pltpu.SemaphoreType.DMA((n_copies,)), pltpu.SemaphoreType.DMA((n_copies,)),
                   *[pltpu.HBM(b.shape, b.dtype) for b in bufs], _sds((SUBLANES, 128), F32)),
        in_specs=[_HBM] * (ns + nl) + [_ANY],
        out_specs=(_SEM, _SEM, *[_HBM] * (ns + nl), pl.BlockSpec(memory_space=pltpu.VMEM)),
        input_output_aliases={i: 2 + i for i in range(ns + nl)},
        compiler_params=pltpu.CompilerParams(has_side_effects=_EFFECT),
    )(*[_in_hbm(b) for b in bufs], after)
    return out[0], out[1], list(out[2:2 + ns]), list(out[2 + ns:2 + ns + nl]), out[-1]


def _split_wait(send_sems, recv_sems, srcs, lands, plan, after, name):
    ns, nl = len(srcs), len(lands)
    bufs = list(srcs) + list(lands)

    def body(*refs):
        send_ref, recv_ref = refs[ns + nl], refs[ns + nl + 1]
        me = _position()
        for k, src, dst in plan(refs[:ns], refs[ns:ns + nl]):
            cp = _remote(src, dst, send_ref.at[k], recv_ref.at[k], me)
            cp.wait_send()
            cp.wait_recv()

    out = pl.pallas_call(
        body, name=name,
        out_shape=[pltpu.HBM(b.shape, b.dtype) for b in bufs],
        in_specs=[_HBM] * (ns + nl) + [_SEM, _SEM, _ANY],
        out_specs=[_HBM] * (ns + nl),
        input_output_aliases={i: i for i in range(ns + nl)},
        compiler_params=pltpu.CompilerParams(has_side_effects=_EFFECT),
    )(*bufs, send_sems, recv_sems, after)
    return list(out[:ns]), list(out[ns:])


BULK_CORE = 1


def _static_half(ref, h, rows):
    hr = rows // 2
    return ref.at[pl.ds(h * hr, hr), :]


def _bulk_start(lands, rows_of, after, name):
    nl = len(lands)

    def body(*refs):
        land_refs = refs[:nl]
        send_sems, recv_sems = refs[nl + 1], refs[nl + 2]
        token = refs[-1]
        x, y, c = _position()
        chip = 2 * x + y

        @pl.when(c == BULK_CORE)
        def _():
            for h in range(2):
                for a, rows in enumerate(rows_of):
                    piece = _static_half(land_refs[a].at[chip], h, rows)
                    for j, (fx, fy) in enumerate(_CHIP_FLIPS):
                        _remote(piece, piece, send_sems.at[6 * h + 3 * a + j], recv_sems.at[3 * a + j],
                                (_flip(x, fx), _flip(y, fy), h)).start()

        token[...] = jnp.zeros_like(token)

    out = pl.pallas_call(
        body, name=name,
        out_shape=(pltpu.SemaphoreType.DMA((12,)), pltpu.SemaphoreType.DMA((6,)),
                   *[pltpu.HBM(b.shape, b.dtype) for b in lands], _sds((SUBLANES, 128), F32)),
        in_specs=[_HBM] * nl + [_ANY],
        out_specs=(_SEM, _SEM, *[_HBM] * nl, pl.BlockSpec(memory_space=pltpu.VMEM)),
        input_output_aliases={i: 2 + i for i in range(nl)},
        compiler_params=pltpu.CompilerParams(has_side_effects=_EFFECT),
    )(*[_in_hbm(b) for b in lands], after)
    return out[0], out[1], list(out[2:2 + nl]), out[-1]


def _bulk_wait(send_sems, recv_sems, lands, rows_of, after, name):
    nl = len(lands)

    def body(*refs):
        land_refs = refs[:nl]
        send_ref, recv_ref = refs[nl], refs[nl + 1]
        x, y, c = _position()
        chip = 2 * x + y
        me = (x, y, c)
        for a, rows in enumerate(rows_of):
            for j, (fx, fy) in enumerate(_CHIP_FLIPS):
                landed = _half(land_refs[a].at[2 * _flip(x, fx) + _flip(y, fy)], c, rows)
                _remote(landed, landed, send_ref.at[0], recv_ref.at[3 * a + j], me).wait_recv()

        @pl.when(c == BULK_CORE)
        def _():
            for h in range(2):
                for a, rows in enumerate(rows_of):
                    piece = _static_half(land_refs[a].at[chip], h, rows)
                    for j in range(3):
                        _remote(piece, piece, send_ref.at[6 * h + 3 * a + j], recv_ref.at[0], me).wait_send()

    out = pl.pallas_call(
        body, name=name,
        out_shape=[pltpu.HBM(b.shape, b.dtype) for b in lands],
        in_specs=[_HBM] * nl + [_SEM, _SEM, _ANY],
        out_specs=[_HBM] * nl,
        input_output_aliases={i: i for i in range(nl)},
        compiler_params=pltpu.CompilerParams(has_side_effects=_EFFECT),
    )(*lands, send_sems, recv_sems, after)
    return list(out)


def _allgather8_relay(block, name):
    r, n = block.shape

    def body(x_ref, out_ref, d2d_send, d2d_recv, ici_send, ici_recv):
        x, y, c = _position()
        chip = 2 * x + y
        sibling = (x, y, 1 - c)

        @pl.when(c == BULK_CORE)
        def _():
            mine = _remote(x_ref, out_ref.at[2 * chip + BULK_CORE], d2d_send.at[0], d2d_recv.at[0], sibling)
            mine.start()
            mine.wait_send()
            _remote(out_ref, out_ref, d2d_send.at[1], d2d_recv.at[1], sibling).wait_recv()

        @pl.when(c != BULK_CORE)
        def _():
            out_ref[2 * chip + 1 - BULK_CORE] = x_ref[...]
            theirs = out_ref.at[2 * chip + BULK_CORE]
            _remote(theirs, theirs, d2d_send.at[0], d2d_recv.at[0], sibling).wait_recv()
            pair = out_ref.at[pl.ds(2 * chip, 2)]
            sends = []
            for j, (fx, fy) in enumerate(_CHIP_FLIPS):
                sends.append(_remote(pair, pair, ici_send.at[j], ici_recv.at[j], (_flip(x, fx), _flip(y, fy), c)))
                sends[-1].start()
            for j, (fx, fy) in enumerate(_CHIP_FLIPS):
                landed = out_ref.at[pl.ds(2 * (2 * _flip(x, fx) + _flip(y, fy)), 2)]
                _remote(landed, landed, ici_send.at[j], ici_recv.at[j], sibling).wait_recv()
            for cp in sends:
                cp.wait_send()
            back = _remote(out_ref, out_ref, d2d_send.at[1], d2d_recv.at[1], sibling)
            back.start()
            back.wait_send()

    vmem = pl.BlockSpec(memory_space=pltpu.VMEM)
    return pl.pallas_call(
        body, name=name, out_shape=_sds((N_DEV, r, n), F32), in_specs=[vmem], out_specs=vmem,
        scratch_shapes=[pltpu.SemaphoreType.DMA((2,)), pltpu.SemaphoreType.DMA((2,)),
                        pltpu.SemaphoreType.DMA((3,)), pltpu.SemaphoreType.DMA((3,))],
        compiler_params=pltpu.CompilerParams(vmem_limit_bytes=VMEM_LIMIT_BYTES),
    )(block)


def _forward_plan(rows_of):
    def pieces(land_refs, half):
        x, y, _ = _position()
        return [_half(land_refs[a].at[2 * _flip(x, fx) + _flip(y, fy)], half, rows)
                for a, rows in enumerate(rows_of) for fx, fy in _CHIP_FLIPS]

    def start(src_refs, land_refs):
        x, y, c = _position()
        return [(p, p, (x, y, 1 - c)) for p in pieces(land_refs, c)]

    def wait(src_refs, land_refs):
        _, _, c = _position()
        return [(k, mine, theirs)
                for k, (mine, theirs) in enumerate(zip(pieces(land_refs, c), pieces(land_refs, 1 - c)))]

    return start, wait


def _swap_halves_plan(half_rows):
    def slices(src_refs, c):
        return [src_refs[a].at[:, pl.ds(pl.multiple_of((1 - c) * hr, BF16_SUBLANES), hr), :]
                for a, hr in enumerate(half_rows)]

    def start(src_refs, land_refs):
        x, y, c = _position()
        return [(src, land_refs[a], (x, y, 1 - c)) for a, src in enumerate(slices(src_refs, c))]

    def wait(src_refs, land_refs):
        _, _, c = _position()
        return [(a, src, land_refs[a]) for a, src in enumerate(slices(src_refs, c))]

    return start, wait


def _swap_gathered_plan(n_arrays):
    def start(src_refs, land_refs):
        x, y, c = _position()
        return [(land_refs[a].at[0], land_refs[a].at[1], (x, y, 1 - c)) for a in range(n_arrays)]

    def wait(src_refs, land_refs):
        return [(a, land_refs[a].at[0], land_refs[a].at[1]) for a in range(n_arrays)]

    return start, wait


def _exchange_plan(n_arrays):
    def start(src_refs, land_refs):
        x, y, c = _position()
        chip = 2 * x + y
        out = []
        for a in range(n_arrays):
            for fx, fy in _CHIP_FLIPS:
                px, py = _flip(x, fx), _flip(y, fy)
                out.append((src_refs[a].at[2 * px + py], land_refs[a].at[0, chip], (px, py, c)))
        return out

    def wait(src_refs, land_refs):
        x, y, c = _position()
        out = []
        for a in range(n_arrays):
            for j, (fx, fy) in enumerate(_CHIP_FLIPS):
                src_chip = 2 * _flip(x, fx) + _flip(y, fy)
                out.append((3 * a + j, src_refs[a].at[src_chip], land_refs[a].at[0, src_chip]))
        return out

    return start, wait


def _adam_gathered(w, gath, m, v, c_arr, after, name, tr=128):
    rows, cols = w.shape
    hr = rows // 2
    if hr % (2 * tr) == 0:
        tr = 2 * tr
    per = hr // tr

    def body(c_ref, w_ref, g_ref, m_ref, v_ref, after_ref, go_ref, d_ref, nm_ref, nv_ref):
        g = g_ref[0, 0].astype(F32)
        for k in range(1, N_CHIPS):
            g = g + g_ref[0, k].astype(F32)
        go_ref[...] = g
        d_ref[...], nm_ref[...], nv_ref[...] = _adam_math(w_ref[...], g, m_ref[...], v_ref[...])

    def rows_of(h, i, c_ref):
        c = c_ref[0]
        return ((c + h - 2 * c * h) * per + i, 0)

    blk = pl.BlockSpec((tr, cols), rows_of)
    grid_spec = pltpu.PrefetchScalarGridSpec(
        num_scalar_prefetch=1, grid=(2, per),
        in_specs=[blk, pl.BlockSpec((1, N_CHIPS, tr, cols), lambda h, i, c_ref: (h, 0, i, 0)), blk, blk, _ANY],
        out_specs=[blk] * 4)
    return pl.pallas_call(
        body, name=name, grid_spec=grid_spec, out_shape=[_sds(w.shape, F32)] * 4,
        compiler_params=_params(("arbitrary", "arbitrary")),
    )(c_arr, w, gath, m, v, after)


def _allreduce_small(block, name):
    two, r, n = block.shape
    assert two == 2

    def body(x_ref, out_ref, sib, chipsum, gath, d2d_send, d2d_recv, ici_send, ici_recv):
        x, y, c = _position()
        chip = 2 * x + y
        sibling = (x, y, 1 - c)
        first = _remote(x_ref, sib, d2d_send.at[0], d2d_recv.at[0], sibling)
        first.start()
        first.wait()
        chipsum[...] = x_ref[...] + sib[...]
        sends = []
        for j, (fx, fy) in enumerate(_CHIP_FLIPS):
            sends.append(_remote(chipsum.at[c], gath.at[chip], ici_send.at[j], ici_recv.at[j],
                                 (_flip(x, fx), _flip(y, fy), c)))
            sends[-1].start()
        gath[chip] = chipsum[c]
        for j, (fx, fy) in enumerate(_CHIP_FLIPS):
            landed = gath.at[2 * _flip(x, fx) + _flip(y, fy)]
            _remote(landed, landed, ici_send.at[j], ici_recv.at[j], sibling).wait_recv()
        for cp in sends:
            cp.wait_send()
        total = gath[0]
        for k in range(1, N_CHIPS):
            total = total + gath[k]
        out_ref[c] = total
        last = _remote(out_ref.at[c], out_ref.at[c], d2d_send.at[1], d2d_recv.at[1], sibling)
        last.start()
        _remote(out_ref.at[1 - c], out_ref.at[1 - c], d2d_send.at[1], d2d_recv.at[1], sibling).wait_recv()
        last.wait_send()

    vmem = pl.BlockSpec(memory_space=pltpu.VMEM)
    return pl.pallas_call(
        body, name=name, out_shape=_sds(block.shape, F32), in_specs=[vmem], out_specs=vmem,
        scratch_shapes=[pltpu.VMEM(block.shape, F32), pltpu.VMEM(block.shape, F32), pltpu.VMEM((N_CHIPS, r, n), F32),
                        pltpu.SemaphoreType.DMA((2,)), pltpu.SemaphoreType.DMA((2,)),
                        pltpu.SemaphoreType.DMA((3,)), pltpu.SemaphoreType.DMA((3,))],
        compiler_params=pltpu.CompilerParams(vmem_limit_bytes=VMEM_LIMIT_BYTES),
    )(block)


def _cast_place(shards, chip_arr, name):
    na = len(shards)
    steps = 4

    def body(chip_ref, *refs):
        for a in range(na):
            refs[na + a][0] = refs[a][...].astype(BF16)

    grid_spec = pltpu.PrefetchScalarGridSpec(
        num_scalar_prefetch=1, grid=(steps,),
        in_specs=[pl.BlockSpec((s.shape[0] // steps, s.shape[1]), lambda i, ch: (i, 0)) for s in shards],
        out_specs=[pl.BlockSpec((1, s.shape[0] // steps, s.shape[1]), lambda i, ch: (ch[0], i, 0)) for s in shards])
    return pl.pallas_call(
        body, name=name, grid_spec=grid_spec,
        out_shape=[_sds((N_CHIPS,) + s.shape, BF16) for s in shards],
        compiler_params=_params(("arbitrary",)),
    )(chip_arr, *shards)


def _silu(v):
    return v * _sigmoid(v)


def _ada_fwd(c8, w_ada):
    def body(c_ref, w_ref, o_ref):
        o_ref[...] = jnp.dot(_silu(c_ref[...]), w_ref[...], preferred_element_type=F32,
                             precision=lax.Precision.HIGHEST)

    return pl.pallas_call(
        body, name="ada_fwd", out_shape=_sds((N_DEV, w_ada.shape[1]), F32),
        compiler_params=pltpu.CompilerParams(vmem_limit_bytes=VMEM_LIMIT_BYTES),
    )(c8, w_ada)


def _mod_select(parts, b_ada, me_arr, after):
    cols = parts.shape[2]

    def body(me_ref, p_ref, b_ref, after_ref, o_ref):
        me = me_ref[0]
        for k in range(N_CHIPS):
            cs = slice(k * cols, (k + 1) * cols)
            o_ref[:, cs] = p_ref[2 * k, pl.ds(me, 1), :] + b_ref[:, cs]

    grid_spec = pltpu.PrefetchScalarGridSpec(
        num_scalar_prefetch=1, grid=(1,),
        in_specs=[pl.BlockSpec(parts.shape, lambda i, m: (0, 0, 0)), pl.BlockSpec(b_ada.shape, lambda i, m: (0, 0)),
                  _ANY],
        out_specs=pl.BlockSpec(b_ada.shape, lambda i, m: (0, 0)))
    return pl.pallas_call(body, name="mod_select", grid_spec=grid_spec, out_shape=_sds(b_ada.shape, F32))(
        me_arr, parts, b_ada, after)


def _ada_bwd(c8, dmod8, chip_arr, w, m, v, tr=512):
    d = c8.shape[1]
    cols = dmod8.shape[1] // N_CHIPS

    def body(chip_ref, c_ref, dm_ref, dmall_ref, w_ref, m_ref, v_ref, gw_ref, d_ref, nm_ref, nv_ref, gb_ref):
        g = lax.dot_general(_silu(c_ref[...]), dm_ref[...], (((0,), (0,)), ((), ())),
                            preferred_element_type=F32, precision=lax.Precision.HIGHEST)
        gw_ref[...] = g
        d_ref[...], nm_ref[...], nv_ref[...] = _adam_math(w_ref[...], g, m_ref[...], v_ref[...])
        acc = dmall_ref[0:1, :]
        for k in range(1, N_DEV):
            acc = acc + dmall_ref[k:k + 1, :]
        gb_ref[...] = acc

    rows = pl.BlockSpec((tr, cols), lambda i, ch: (i, 0))
    grid_spec = pltpu.PrefetchScalarGridSpec(
        num_scalar_prefetch=1, grid=(d // tr,),
        in_specs=[pl.BlockSpec((N_DEV, tr), lambda i, ch: (0, i)),
                  pl.BlockSpec((N_DEV, cols), lambda i, ch: (0, ch[0])),
                  pl.BlockSpec(dmod8.shape, lambda i, ch: (0, 0)), rows, rows, rows],
        out_specs=[rows] * 4 + [pl.BlockSpec((1, dmod8.shape[1]), lambda i, ch: (0, 0))])
    return pl.pallas_call(
        body, name="ada_bwd", grid_spec=grid_spec,
        out_shape=[_sds((d, cols), F32)] * 4 + [_sds((1, dmod8.shape[1]), F32)],
        compiler_params=_params(("arbitrary",)),
    )(chip_arr, c8, dmod8, dmod8, w, m, v)


def _adam_math(w, g, m, v):
    m = ADAM_B1 * m + (1.0 - ADAM_B1) * g
    v = ADAM_B2 * v + (1.0 - ADAM_B2) * (g * g)
    m_hat = m / (1.0 - ADAM_B1 ** ADAM_STEP)
    v_hat = v / (1.0 - ADAM_B2 ** ADAM_STEP)
    delta = -ADAM_LR * (m_hat / (jnp.sqrt(v_hat) + ADAM_EPS) + ADAM_WD * w)
    return delta, m, v


def _adam(w, g, m, v, name, tr=256):
    rows, cols = w.shape
    if rows % tr:
        tr = rows

    def body(w_ref, g_ref, m_ref, v_ref, d_ref, nm_ref, nv_ref):
        d_ref[...], nm_ref[...], nv_ref[...] = _adam_math(w_ref[...], g_ref[...], m_ref[...], v_ref[...])

    spec = pl.BlockSpec((tr, cols), lambda i: (i, 0))
    return pl.pallas_call(
        body, name=name, grid=(rows // tr,), in_specs=[spec] * 4, out_specs=[spec] * 3,
        out_shape=[_sds(w.shape, F32)] * 3, compiler_params=_params(("parallel",)),
    )(w, g, m, v)


SMALL_REPLICATED = ("g_mix_pre", "g_mix_post", "conv_b", "w_rgate", "b_rgate", "w_igate", "b_igate", "lru_a",
                    "v_norm_g", "v_norm_b", "w_spatial", "b_spatial", "g_lru_out", "g_gmlp_out", "g_ffn_pre",
                    "g_ffn_post", "ffn_conv_b")
SMALL_COLUMN_SHARDED = ("conv_w", "ffn_conv_w")

SMALL_ROW_LEN = 86016
_SMALL_ROWS = (
    (("ffn_conv_w", 18432), ("conv_w", 2048), ("w_spatial", 65536)),
    (("w_rgate", 32768), ("w_igate", 32768), ("ffn_conv_b", 6144), ("g_mix_pre", 1024), ("g_mix_post", 1024),
     ("g_ffn_pre", 1024), ("g_ffn_post", 1024), ("conv_b", 512), ("b_rgate", 512), ("b_igate", 512),
     ("lru_a", 512), ("v_norm_g", 512), ("v_norm_b", 512), ("b_spatial", 512), ("g_lru_out", 512),
     ("g_gmlp_out", 512), ("loss", 128)),
)


def _small_slots():
    slots = {}
    for row, entries in enumerate(_SMALL_ROWS):
        off = 0
        for name, size in entries:
            slots[name] = (row, off)
            off += size
        assert off <= SMALL_ROW_LEN
    return slots


SMALL_SLOT = _small_slots()
SMALL_LANES = SMALL_ROW_LEN // SUBLANES


def _small_pieces(name, first, count):
    row, off = SMALL_SLOT[name]
    pos, pieces = off + first, []
    while count:
        sub, lane = divmod(pos, SMALL_LANES)
        n = min(count, SMALL_LANES - lane)
        pieces.append((row, sub, lane, n))
        pos, count = pos + n, count - n
    return pieces
ROW_VECTORS = ("ffn_conv_b", "g_mix_pre", "g_mix_post", "g_ffn_pre", "g_ffn_post", "conv_b", "lru_a", "v_norm_g",
               "v_norm_b", "g_lru_out", "g_gmlp_out")
HEAD_DIM = LRU_WIDTH // LRU_HEADS


def _pack_small(g, after):
    order = ("ffn_conv_w", "conv_w", "w_spatial", "w_rgate", "w_igate", "b_rgate", "b_igate", "b_spatial", "loss") \
        + ROW_VECTORS
    vmem = pl.BlockSpec(memory_space=pltpu.VMEM)

    def body(*refs):
        src = dict(zip(order, refs))
        out_ref = refs[len(order) + 1]
        out_ref[...] = jnp.zeros_like(out_ref)

        def put(name, first, val):
            col = 0
            for row, sub, lane, n in _small_pieces(name, first, val.shape[1]):
                out_ref[row, sub:sub + 1, lane:lane + n] = val[:, col:col + n]
                col += n

        for name in ROW_VECTORS + ("b_rgate", "b_igate", "loss"):
            put(name, 0, src[name][...])
        for name in ("ffn_conv_w", "conv_w"):
            k_taps, n = src[name].shape
            for k in range(k_taps):
                put(name, k * n, src[name][k:k + 1, :])
        for g_idx in range(GMLP_GROUPS):
            for i in range(GMLP_BLOCK):
                put("w_spatial", (g_idx * GMLP_BLOCK + i) * GMLP_BLOCK, src["w_spatial"][g_idx, i:i + 1, :])
        for name in ("w_rgate", "w_igate"):
            for h in range(LRU_HEADS):
                for i in range(HEAD_DIM):
                    r = h * HEAD_DIM + i
                    put(name, r * HEAD_DIM, src[name][r:r + 1, h * HEAD_DIM:(h + 1) * HEAD_DIM])
        eye = (lax.broadcasted_iota(jnp.int32, (GMLP_BLOCK, GMLP_BLOCK), 0)
               == lax.broadcasted_iota(jnp.int32, (GMLP_BLOCK, GMLP_BLOCK), 1))
        for g_idx in range(GMLP_GROUPS):
            col = src["b_spatial"][:, g_idx:g_idx + 1]
            put("b_spatial", g_idx * GMLP_BLOCK, _colsum(jnp.where(eye, col, 0.0)))

    return pl.pallas_call(
        body, name="pack_small", out_shape=_sds((2, SUBLANES, SMALL_LANES), F32),
        in_specs=[vmem] * len(order) + [_ANY], out_specs=vmem,
        compiler_params=pltpu.CompilerParams(vmem_limit_bytes=VMEM_LIMIT_BYTES),
    )(*[g[n] for n in order], after)


def _adam_small(g_small, w, m, v):
    vmem = pl.BlockSpec(memory_space=pltpu.VMEM)
    n_p = len(SMALL_REPLICATED)

    def body(g_ref, *refs):
        w_refs, m_refs, v_refs = refs[:n_p], refs[n_p:2 * n_p], refs[2 * n_p:3 * n_p]
        outs = refs[3 * n_p:]
        go, do, mo, vo = outs[:n_p], outs[n_p:2 * n_p], outs[2 * n_p:3 * n_p], outs[3 * n_p:]
        for k, name in enumerate(SMALL_REPLICATED):
            def take(first, count, name=name):
                parts = [g_ref[row, sub:sub + 1, lane:lane + n]
                         for row, sub, lane, n in _small_pieces(name, first, count)]
                return parts[0] if len(parts) == 1 else jnp.concatenate(parts, axis=1)

            shape = w_refs[k].shape
            if name in ROW_VECTORS:
                go[k][...] = take(0, shape[1])
            elif name in ("b_rgate", "b_igate"):
                for h in range(LRU_HEADS):
                    go[k][0, h:h + 1, :] = take(h * HEAD_DIM, HEAD_DIM)
            elif name == "b_spatial":
                for g_idx in range(GMLP_GROUPS):
                    go[k][0, g_idx:g_idx + 1, :] = take(g_idx * GMLP_BLOCK, GMLP_BLOCK)
            elif name == "w_spatial":
                for g_idx in range(GMLP_GROUPS):
                    for i in range(GMLP_BLOCK):
                        go[k][0, g_idx, i:i + 1, :] = take((g_idx * GMLP_BLOCK + i) * GMLP_BLOCK, GMLP_BLOCK)
            else:
                for h in range(LRU_HEADS):
                    for i in range(HEAD_DIM):
                        go[k][0, h, i:i + 1, :] = take((h * HEAD_DIM + i) * HEAD_DIM, HEAD_DIM)
            do[k][...], mo[k][...], vo[k][...] = _adam_math(w_refs[k][...], go[k][...], m_refs[k][...],
                                                             v_refs[k][...])

    names = SMALL_REPLICATED
    out_shape = [_sds(w[n].shape, F32) for n in names] * 4
    res = pl.pallas_call(
        body, name="adam_small", out_shape=out_shape,
        in_specs=[vmem] * (1 + 3 * n_p), out_specs=[vmem] * (4 * n_p),
        compiler_params=pltpu.CompilerParams(vmem_limit_bytes=VMEM_LIMIT_BYTES),
    )(g_small, *[w[n] for n in names], *[m[n] for n in names], *[v[n] for n in names])
    return [dict(zip(names, res[k * n_p:(k + 1) * n_p])) for k in range(4)]


def _adam_cols(name, g_small, w, m, v, chip_arr):
    _, k_taps, n = w.shape
    row, off = SMALL_SLOT[name]
    first = off // n
    per_sub = SMALL_LANES // n

    def body(chip_ref, *refs):
        g_refs = refs[:k_taps]
        w_ref, m_ref, v_ref, go_ref, d_ref, nm_ref, nv_ref = refs[k_taps:]
        for k in range(k_taps):
            tap = (0, slice(k, k + 1), slice(None))
            sub = (first + N_CHIPS * k + chip_ref[0]) // per_sub
            g = g_refs[k][row, pl.ds(sub, 1), :]
            go_ref[tap] = g
            d_ref[tap], nm_ref[tap], nv_ref[tap] = _adam_math(w_ref[tap], g, m_ref[tap], v_ref[tap])

    whole = pl.BlockSpec(w.shape, lambda i, ch: (0, 0, 0))
    taps = [pl.BlockSpec((2, SUBLANES, n),
                         functools.partial(lambda i, ch, k: (0, 0, (first + N_CHIPS * k + ch[0]) % per_sub), k=k))
            for k in range(k_taps)]
    grid_spec = pltpu.PrefetchScalarGridSpec(
        num_scalar_prefetch=1, grid=(1,), in_specs=taps + [whole] * 3, out_specs=[whole] * 4)
    return pl.pallas_call(body, name="adam_" + name, grid_spec=grid_spec, out_shape=[_sds(w.shape, F32)] * 4)(
        chip_arr, *[g_small] * k_taps, w, m, v)


def kernel(x, c, w_ada, b_ada, g_mix_pre, g_mix_post, w_in, conv_w, conv_b, w_rgate, b_rgate, w_igate, b_igate, lru_a, v_norm_g, v_norm_b, w_spatial, b_spatial, g_lru_out, g_gmlp_out, w_out, g_ffn_pre, g_ffn_post, w_up, ffn_conv_w, ffn_conv_b, w_down, loss_target, m_w_ada, m_b_ada, m_g_mix_pre, m_g_mix_post, m_w_in, m_conv_w, m_conv_b, m_w_rgate, m_b_rgate, m_w_igate, m_b_igate, m_lru_a, m_v_norm_g, m_v_norm_b, m_w_spatial, m_b_spatial, m_g_lru_out, m_g_gmlp_out, m_w_out, m_g_ffn_pre, m_g_ffn_post, m_w_up, m_ffn_conv_w, m_ffn_conv_b, m_w_down, v_w_ada, v_b_ada, v_g_mix_pre, v_g_mix_post, v_w_in, v_conv_w, v_conv_b, v_w_rgate, v_b_rgate, v_w_igate, v_b_igate, v_lru_a, v_v_norm_g, v_v_norm_b, v_w_spatial, v_b_spatial, v_g_lru_out, v_g_gmlp_out, v_w_out, v_g_ffn_pre, v_g_ffn_post, v_w_up, v_ffn_conv_w, v_ffn_conv_b, v_w_down):
    args = dict(locals())
    names = ("w_ada", "b_ada", "g_mix_pre", "g_mix_post", "w_in", "conv_w", "conv_b", "w_rgate", "b_rgate",
             "w_igate", "b_igate", "lru_a", "v_norm_g", "v_norm_b", "w_spatial", "b_spatial", "g_lru_out",
             "g_gmlp_out", "w_out", "g_ffn_pre", "g_ffn_post", "w_up", "ffn_conv_w", "ffn_conv_b", "w_down")
    drop = lambda a: a if a.ndim == 2 else a[0]
    w = {n: drop(args[n]) for n in names}
    m = {n: drop(args["m_" + n]) for n in names}
    v = {n: drop(args["v_" + n]) for n in names}
    xi, yi, ci = _position()
    me_arr = jnp.reshape(4 * xi + 2 * yi + ci, (1,)).astype(jnp.int32)
    chip_arr = jnp.reshape(2 * xi + yi, (1,)).astype(jnp.int32)
    c_arr = jnp.reshape(ci, (1,)).astype(jnp.int32)
    pos_arr = jnp.stack([ci, 2 * xi + yi]).astype(jnp.int32)

    big = ("w_in", "w_out", "w_up", "w_down")
    rows_a, rows_b = [w[n].shape[0] for n in big[:2]], [w[n].shape[0] for n in big[2:]]
    lands_a = _cast_place([w[n] for n in big[:2]], chip_arr, "cast_place_a")
    send_a, recv_a, lands_a, token_a = _bulk_start(lands_a, rows_a, pos_arr, "gather_start_a")
    lands_b = _cast_place([w[n] for n in big[2:]], chip_arr + token_a[0, 0].astype(jnp.int32), "cast_place_b")
    send_b, recv_b, lands_b, token_b = _bulk_start(lands_b, rows_b, token_a, "gather_start_b")

    row0 = jnp.concatenate([c + token_b[0:1, 0:1], w["conv_w"].reshape(1, -1), w["ffn_conv_w"].reshape(1, -1)],
                           axis=1)
    g0 = _allgather8_relay(row0, "gather_cond")[:, 0, :]
    c8 = g0[:, :D_MODEL]
    per_chip = g0[0::2]
    conv_w_full = per_chip[:, D_MODEL:D_MODEL + 512].reshape(N_CHIPS, 4, 128).transpose(1, 0, 2).reshape(4, 512)
    ffn_conv_w_full = per_chip[:, D_MODEL + 512:].reshape(N_CHIPS, 3, 1536).transpose(1, 0, 2).reshape(3, 2 * D_FF)
    mod_parts = _allgather8_relay(_ada_fwd(c8, w["w_ada"]), "gather_mod")
    lands_a = _bulk_wait(send_a, recv_a, lands_a, rows_a, mod_parts, "gather_wait_a")
    fwd_start, fwd_wait_a = _forward_plan([w[n].shape[0] for n in big[:2]])
    fwd_send_a, fwd_recv_a, _, lands_a, tok = _split_start([], lands_a, fwd_start, 6, pos_arr, "forward_start_a")
    mod = _mod_select(mod_parts, w["b_ada"].reshape(1, -1), me_arr, tok).reshape(N_MOD, D_MODEL)
    sh_m, sc_m, gt_m, sh_f, sc_f, gt_f = [mod[k:k + 1] for k in range(N_MOD)]

    small = {n: w[n] for n in SMALL_REPLICATED}
    small["conv_w"] = conv_w_full
    small["ffn_conv_w"] = ffn_conv_w_full
    row = lambda a: a.reshape(1, -1)
    seq_params, ws_t = _seq_params(small)
    glo, ggo = row(small["g_lru_out"]), row(small["g_gmlp_out"])
    g_pre, g_post = row(small["g_mix_pre"]), row(small["g_mix_post"])
    g_pre2, g_post2 = row(small["g_ffn_pre"]), row(small["g_ffn_post"])
    fw, fb = small["ffn_conv_w"], row(small["ffn_conv_b"])
    xs, tgt = x[0], loss_target[0]

    _, (w_in4, w_out4) = _split_wait(fwd_send_a, fwd_recv_a, [], lands_a, fwd_wait_a, mod, "forward_wait_a")
    w_out_b = w_out4.reshape(D_MODEL, D_MODEL)
    h, lx, ycat, hst, stash = _seqmix(xs, sc_m, sh_m, g_pre, w_in4, seq_params, glo, ggo)
    lands_b = _bulk_wait(send_b, recv_b, lands_b, rows_b, ycat, "gather_wait_b")
    fwd_start, fwd_wait = _forward_plan([w[n].shape[0] for n in big[2:]])
    fwd_send, fwd_recv, _, lands_b, tok = _split_start([], lands_b, fwd_start, 6, pos_arr, "forward_start_b")
    y, x1, h2 = _mix_out(ycat, xs, w_out_b, gt_m + tok[0:1, 0:1], g_post, g_pre2, sc_f, sh_f)
    _, (w_up4, w_down4) = _split_wait(fwd_send, fwd_recv, [], lands_b, fwd_wait, h2, "forward_wait_b")
    w_down_b = w_down4.reshape(D_FF, D_MODEL)
    up0, pre, act, dy2, dx2, loss, dgt_f, dg_post2 = _ffn_fwd(h2, x1, tgt, w_up4, w_down_b, fw, fb, gt_f, g_post2)

    dup0, dfw, dfb = _ffn_bwd_a(dy2, pre, up0, w_down_b, fw)
    gw_up = _wgrad(h2, dup0, N_CHIPS, "wgrad_up", True)
    gw_down = _wgrad(act, dy2, 2, "wgrad_down", False)
    ex_start, ex_wait = _exchange_plan(2)
    sg_start, sg_wait = _swap_gathered_plan(2)
    grads, deltas, new_m, new_v = {}, {}, {}, {}

    def swap_start(parts, name):
        sw_start, sw_wait = _swap_halves_plan([p.shape[1] // 2 for p in parts])
        recv = [lax.empty((N_CHIPS, p.shape[1] // 2, p.shape[2]), BF16) for p in parts]
        send_s, recv_s, parts, recv, token = _split_start(parts, recv, sw_start, len(parts), pos_arr,
                                                           "swap_start_" + name)
        return (send_s, recv_s, parts, recv, sw_wait), token

    def exchange_start(swap, tags, after, name):
        send_s, recv_s, parts, recv, sw_wait = swap
        parts, recv = _split_wait(send_s, recv_s, parts, recv, sw_wait, after, "swap_wait_" + name)
        both = [_chip_sum(p, r, pos_arr, "chip_sum_" + t) for p, r, t in zip(parts, recv, tags)]
        sums, gath = [b[0] for b in both], [b[1] for b in both]
        return _split_start(sums, gath, ex_start, 3 * len(parts), pos_arr, "exchange_start_" + name)

    def gathered_start(exchange, after, name):
        send_s, recv_s, sums, gath, _ = exchange
        _, gath = _split_wait(send_s, recv_s, sums, gath, ex_wait, after, "exchange_wait_" + name)
        send_s, recv_s, _, gath, token = _split_start([], gath, sg_start, len(gath), pos_arr,
                                                      "gathered_start_" + name)
        return (send_s, recv_s, gath), token

    def gathered_wait(gathered, after, name):
        send_s, recv_s, gath = gathered
        return _split_wait(send_s, recv_s, [], gath, sg_wait, after, "gathered_wait_" + name)[1]

    def adam_big(t, gath, after):
        grads[t], deltas[t], new_m[t], new_v[t] = _adam_gathered(w[t], gath, m[t], v[t], c_arr, after, "adam_" + t)

    def behind(value, token):
        return value + token[0:1, 0:1]

    tags_b, tags_a = ("w_up", "w_down"), ("w_in", "w_out")
    swap_b, tok = swap_start([gw_up, gw_down.reshape(N_CHIPS, -1, D_MODEL)], "b")
    dx1, dy, dsh_f, dsc_f, dg_pre2, dgt_m, dg_post = _ffn_bwd_b(
        dup0, x1, y, dx2, w_up4, g_pre2, behind(sc_f, tok), sh_f, gt_m, g_post)
    exchange_b = exchange_start(swap_b, tags_b, dg_post, "b")
    (dz, grad_x, dcw, dcb, dwr, dwi, dbr, dbi, dspa, dng, dnb, dws, dbs_t, dglo, dggo, dsh_m, dsc_m,
     dg_pre) = _seqmix_bwd(lx, hst, stash, dy, w_out_b, seq_params, ws_t, behind(glo, exchange_b[4]), ggo,
                           xs, dx1, w_in4, g_pre, sc_m)
    gw_in = _wgrad(h, dz, N_CHIPS, "wgrad_in", True)
    gw_out = _wgrad(ycat, dy, 1, "wgrad_out", False)
    swap_a, tok = swap_start([gw_in, gw_out.reshape(N_CHIPS, -1, D_MODEL)], "a")

    dmod = jnp.concatenate([behind(dsh_m, tok), dsc_m, dgt_m, dsh_f, dsc_f, dgt_f], axis=1)
    dmod8 = _allgather8(dmod, "gather_dmod")[:, 0, :]
    small_grads = dict(
        g_mix_pre=dg_pre, g_mix_post=dg_post, conv_w=dcw, conv_b=dcb, w_rgate=dwr, b_rgate=dbr, w_igate=dwi,
        b_igate=dbi, lru_a=dspa, v_norm_g=dng, v_norm_b=dnb, w_spatial=dws, b_spatial=dbs_t, g_lru_out=dglo,
        g_gmlp_out=dggo, g_ffn_pre=dg_pre2, g_ffn_post=dg_post2, ffn_conv_w=dfw, ffn_conv_b=dfb,
        loss=loss)
    g_small = _allreduce_small(_pack_small(small_grads, dmod8), "reduce_small")
    total = g_small[_small_pieces("loss", 0, 1)[0][:3]]
    exchange_a = exchange_start(swap_a, tags_a, g_small, "a")
    gathered_b, tok = gathered_start(exchange_b, exchange_a[4], "b")

    grads["w_ada"], deltas["w_ada"], new_m["w_ada"], new_v["w_ada"], g_b_ada = _ada_bwd(
        c8, behind(dmod8, tok), chip_arr, w["w_ada"], m["w_ada"], v["w_ada"])
    rep = SMALL_REPLICATED
    small_out = _adam_small(g_small, {n: args[n] for n in rep}, {n: args["m_" + n] for n in rep},
                            {n: args["v_" + n] for n in rep})
    for n in rep:
        grads[n], deltas[n], new_m[n], new_v[n] = [group[n] for group in small_out]
    for n in SMALL_COLUMN_SHARDED:
        grads[n], deltas[n], new_m[n], new_v[n] = _adam_cols(n, g_small, args[n], args["m_" + n],
                                                             args["v_" + n], chip_arr)
    d_b, m_b, v_b = _adam(w["b_ada"], g_b_ada, m["b_ada"], v["b_ada"], "adam_b_ada")
    grads["b_ada"], deltas["b_ada"], new_m["b_ada"], new_v["b_ada"] = g_b_ada, d_b, m_b, v_b

    gath_up, gath_down = gathered_wait(gathered_b, d_b, "b")
    adam_big("w_down", gath_down, pos_arr)
    gathered_a, tok = gathered_start(exchange_a, deltas["w_down"], "a")
    adam_big("w_up", gath_up, tok)
    gath_in, gath_out = gathered_wait(gathered_a, deltas["w_up"], "a")
    adam_big("w_in", gath_in, pos_arr)
    adam_big("w_out", gath_out, pos_arr)

    outs = [total, grad_x[None]]
    for group in (grads, deltas, new_m, new_v):
        outs.extend(group[n].reshape(args[n].shape) for n in names)
    return tuple(outs)
```

```python
import functools
import math

import jax
import jax.numpy as jnp
from jax import lax
from jax.experimental import pallas as pl
from jax.experimental.pallas import tpu as pltpu

F32 = jnp.float32
BF16 = jnp.bfloat16
MESH = pl.DeviceIdType.MESH

D_MODEL = 1024
LRU_WIDTH = 512
LRU_HEADS = 8
GMLP_GROUPS = 4
GMLP_BLOCK = 128
CHUNK = 64
D_FF = 3072
N_MOD = 6
EPS = 1e-6
LRU_C = 8.0
N_CHIPS = 4
N_DEV = 8

ADAM_LR = 0.001
ADAM_B1 = 0.9
ADAM_B2 = 0.999
ADAM_EPS = 1e-08
ADAM_WD = 0.01
ADAM_STEP = 10

GELU_C0 = math.sqrt(2.0 / math.pi)
GELU_C1 = 0.044715

VMEM_LIMIT_BYTES = 56 * 1024 * 1024
SUBLANES = 8
BF16_SUBLANES = 16
FFN_CHUNK = 768
SUB_ROWS = 256


def _gelu_gate(x):
    x2 = x * x
    z = x * ((2.0 * GELU_C0 * GELU_C1) * x2 + 2.0 * GELU_C0)
    return 1.0 / (1.0 + jnp.exp(-z)), x2


def _gelu(x):
    t = jnp.tanh(GELU_C0 * (x + GELU_C1 * x * x * x))
    return 0.5 * x * (1.0 + t)


def _gelu_and_grad(x):
    s, x2 = _gelu_gate(x)
    g = x * s
    dz = (6.0 * GELU_C0 * GELU_C1) * x2 + 2.0 * GELU_C0
    return g, s + g * (1.0 - s) * dz


def _sigmoid(x):
    return 1.0 / (1.0 + jnp.exp(-x))


def _log1p(u):
    w = 1.0 + u
    return jnp.where(w == 1.0, u, jnp.log(w) * (u / (w - 1.0)))


def _softplus(x):
    return jnp.maximum(x, 0.0) + _log1p(jnp.exp(-jnp.abs(x)))


def _neg_expm1(x):
    u = jnp.exp(x)
    um1 = u - 1.0
    tiny = um1 == 0.0
    small = um1 * (x / jnp.log(jnp.where(tiny, 2.0, jnp.maximum(u, 0.25))))
    return -jnp.where(tiny, x, jnp.where(x < -1.0, um1, small))


def _msq_rsqrt(v):
    return lax.rsqrt(jnp.mean(v * v, axis=-1, keepdims=True) + EPS)


def _rms_bwd(dyn, yn, r):
    return r * (dyn - yn * jnp.mean(dyn * yn, axis=-1, keepdims=True))


def _colsum(v):
    return jnp.sum(v, axis=0, keepdims=True)


def _shift_down(cur, prev8, k):
    rolled = pltpu.roll(cur, k, 0)
    head = pltpu.roll(prev8, k, 0)
    row8 = lax.broadcasted_iota(jnp.int32, (SUBLANES, cur.shape[1]), 0)
    first = jnp.where(row8 < k, head, rolled[0:SUBLANES])
    return jnp.concatenate([first, rolled[SUBLANES:]], axis=0)


def _shift_up(cur, next8, k):
    t = cur.shape[0]
    rolled = pltpu.roll(cur, t - k, 0)
    tail = pltpu.roll(next8, SUBLANES - k, 0)
    row8 = lax.broadcasted_iota(jnp.int32, (SUBLANES, cur.shape[1]), 0)
    last = jnp.where(row8 >= SUBLANES - k, tail, rolled[t - SUBLANES:])
    return jnp.concatenate([rolled[:t - SUBLANES], last], axis=0)


def _scan_fwd(a, b):
    t = a.shape[0]
    row = lax.broadcasted_iota(jnp.int32, a.shape, 0)
    d = 1
    while d < t:
        keep = row >= d
        a_s = jnp.where(keep, pltpu.roll(a, d, 0), 1.0)
        b_s = jnp.where(keep, pltpu.roll(b, d, 0), 0.0)
        b = a * b_s + b
        a = a * a_s
        d *= 2
    return a, b


def _scan_bwd(a, g):
    t = a.shape[0]
    row = lax.broadcasted_iota(jnp.int32, a.shape, 0)
    d = 1
    while d < t:
        keep = row < t - d
        a_s = jnp.where(keep, pltpu.roll(a, t - d, 0), 1.0)
        g_s = jnp.where(keep, pltpu.roll(g, t - d, 0), 0.0)
        g = a * g_s + g
        a = a * a_s
        d *= 2
    return a, g


def _dot(a, b):
    return jnp.dot(a, b, preferred_element_type=F32)


def _dot_nt(a, b):
    return lax.dot_general(a, b, (((1,), (1,)), ((), ())), preferred_element_type=F32)


def _dot_tn(a, b):
    return lax.dot_general(a, b, (((0,), (0,)), ((), ())), preferred_element_type=F32)


def _rows(ts, cols, rev_of=None):
    if rev_of is None:
        return pl.BlockSpec((ts, cols), lambda i: (i, 0))
    return pl.BlockSpec((ts, cols), lambda i: (rev_of - 1 - i, 0))


def _halo_prev(ts, cols, halo, rev_of=None, col_block=0):
    per = ts // halo
    if rev_of is None:
        return pl.BlockSpec((halo, cols), lambda i: (jnp.maximum(i * per - 1, 0), col_block))
    return pl.BlockSpec((halo, cols), lambda i: (jnp.maximum((rev_of - 1 - i) * per - 1, 0), col_block))


def _full(shape):
    nd = len(shape)
    return pl.BlockSpec(shape, lambda *_: (0,) * nd)


_RESIDENT = pl.BlockSpec(memory_space=pltpu.VMEM)


def _params(sem):
    return pltpu.CompilerParams(dimension_semantics=sem, vmem_limit_bytes=VMEM_LIMIT_BYTES)


def _sds(shape, dtype):
    return jax.ShapeDtypeStruct(shape, dtype)


def _sub_tiles(ts):
    return [slice(r0, r0 + SUB_ROWS) for r0 in range(0, ts, SUB_ROWS)]


N_STASH = 12
(ST_XC, ST_R, ST_IG, ST_A, ST_MULT, ST_GL, ST_DGL, ST_U, ST_DU, ST_Q, ST_VHAT, ST_SPB) = range(N_STASH)


def _seq_param_specs():
    return [_full((4, 512)), _full((1, 512)), _full((512, 512)), _full((512, 512)), _full((1, 512)),
            _full((1, 512)), _full((1, 512)), _full((1, 512)), _full((1, 512)), _full((4, 128, 128)),
            _full((128, 4))]


def _seqmix(x, sc, sh, g_pre, w_in4, seq_params, glo, ggo, ts=256):
    s, d = x.shape
    nt = s // ts

    def body(x_ref, sc_ref, sh_ref, gpre_ref, win_ref, cw_ref, cb_ref, bdr_ref, bdi_ref, br_ref, bi_ref, la_ref,
             ng_ref, nb_ref, ws_ref, bst_ref, glo_ref, ggo_ref, hin_ref, lx_ref, ycat_ref, hst_ref, st_ref,
             hcarry, lxprev, sp_scr):
        i = pl.program_id(0)

        @pl.when(i == 0)
        def _():
            hcarry[...] = jnp.zeros_like(hcarry)
            lxprev[...] = jnp.zeros_like(lxprev)

        xv = x_ref[...]
        hin = ((xv * _msq_rsqrt(xv) * gpre_ref[...]) * (1.0 + sc_ref[...]) + sh_ref[...]).astype(BF16)
        hin_ref[...] = hin
        z = [_dot(hin, win_ref[k]) for k in range(N_CHIPS)]

        lx = z[0]
        lx_ref[...] = lx
        prev8 = lxprev[...]
        lxprev[...] = lx[ts - SUBLANES:, :]
        xc = (cw_ref[3:4, :] * lx + cw_ref[2:3, :] * _shift_down(lx, prev8, 1)
              + cw_ref[1:2, :] * _shift_down(lx, prev8, 2) + cw_ref[0:1, :] * _shift_down(lx, prev8, 3)
              + cb_ref[...])
        xcb = xc.astype(BF16)
        r = _sigmoid(_dot(xcb, bdr_ref[...]) + br_ref[...])
        ig = _sigmoid(_dot(xcb, bdi_ref[...]) + bi_ref[...])
        log_a = (-LRU_C) * r * _softplus(-la_ref[...])
        a = jnp.exp(log_a)
        mult = jnp.sqrt(_neg_expm1(2.0 * log_a))
        acum, hloc = _scan_fwd(a, mult * (ig * xc))
        h = hloc + acum * hcarry[...]
        hcarry[...] = h[ts - 1:ts, :]
        hst_ref[...] = h
        gl, dgl = _gelu_and_grad(z[1])
        y_l = h * gl
        for slot, val in ((ST_XC, xc), (ST_R, r), (ST_IG, ig), (ST_A, a), (ST_MULT, mult), (ST_GL, gl),
                          (ST_DGL, dgl)):
            st_ref[slot] = val

        u, du = _gelu_and_grad(z[2])
        vg, dvg = _gelu_and_grad(z[3])
        vc = vg - jnp.mean(vg, axis=-1, keepdims=True)
        rstd = lax.rsqrt(jnp.mean(vc * vc, axis=-1, keepdims=True) + EPS)
        vhat = vc * rstd
        vb = (vhat * ng_ref[...] + nb_ref[...]).astype(BF16)
        for n in range(ts // GMLP_BLOCK):
            rs = slice(n * GMLP_BLOCK, (n + 1) * GMLP_BLOCK)
            for g in range(GMLP_GROUPS):
                cs = slice(g * 128, (g + 1) * 128)
                sp_scr[rs, cs] = _dot(ws_ref[g], vb[rs, cs]) + bst_ref[:, g:g + 1]
        spb = sp_scr[...]
        y_g = u * spb
        for slot, val in ((ST_U, u), (ST_DU, du), (ST_Q, rstd * dvg), (ST_VHAT, vhat), (ST_SPB, spb)):
            st_ref[slot] = val

        ycat_ref[:, 0:512] = (y_l * _msq_rsqrt(y_l) * glo_ref[...]).astype(BF16)
        ycat_ref[:, 512:1024] = (y_g * _msq_rsqrt(y_g) * ggo_ref[...]).astype(BF16)

    vec = _full((1, d))
    return pl.pallas_call(
        body, grid=(nt,), name="seqmix",
        in_specs=[_rows(ts, d), vec, vec, vec, _full(w_in4.shape)] + _seq_param_specs()
        + [_full((1, 512)), _full((1, 512))],
        out_specs=[_rows(ts, d), _rows(ts, 512), _rows(ts, d), _rows(ts, 512),
                   pl.BlockSpec((N_STASH, ts, 512), lambda i: (0, i, 0))],
        out_shape=[_sds((s, d), BF16), _sds((s, 512), F32), _sds((s, d), BF16), _sds((s, 512), F32),
                   _sds((N_STASH, s, 512), F32)],
        scratch_shapes=[pltpu.VMEM((1, 512), F32), pltpu.VMEM((SUBLANES, 512), F32), pltpu.VMEM((ts, 512), F32)],
        compiler_params=_params(("arbitrary",)),
    )(x, sc, sh, g_pre, w_in4, *seq_params, glo, ggo)


def _mix_out(ycat, x, w_out, gt_m, g_post, g_pre2, sc_f, sh_f, ts=512):
    s, d = x.shape

    def body(yc_ref, x_ref, w_ref, gt_ref, gp_ref, g2_ref, sc_ref, sh_ref, y_ref, x1_ref, h2_ref):
        for rs in _sub_tiles(ts):
            y = _dot(yc_ref[rs, :], w_ref[...])
            y_ref[rs, :] = y
            x1 = x_ref[rs, :] + gt_ref[...] * (y * _msq_rsqrt(y) * gp_ref[...])
            x1_ref[rs, :] = x1
            h2 = (x1 * _msq_rsqrt(x1) * g2_ref[...]) * (1.0 + sc_ref[...]) + sh_ref[...]
            h2_ref[rs, :] = h2.astype(BF16)

    vec = _full((1, d))
    return pl.pallas_call(
        body, grid=(s // ts,), name="mix_out",
        in_specs=[_rows(ts, d), _rows(ts, d), _full((d, d)), vec, vec, vec, vec, vec],
        out_specs=[_rows(ts, d), _rows(ts, d), _rows(ts, d)],
        out_shape=[_sds((s, d), F32), _sds((s, d), F32), _sds((s, d), BF16)],
        compiler_params=_params(("parallel",)),
    )(ycat, x, w_out, gt_m, g_post, g_pre2, sc_f, sh_f)


def _ffn_cols(j):
    per = (2 * D_FF // N_CHIPS) // FFN_CHUNK
    return j // per, (j % per) * FFN_CHUNK, j * FFN_CHUNK


def _ffn_fwd(h2, x1, tgt, w_up4, w_down, fw, fb, gt_f, g_post, ts=256):
    s, d = x1.shape
    nch = D_FF // FFN_CHUNK

    def body(h2_ref, x1_ref, tgt_ref, wup_ref, wdn_ref, fw_ref, fb_ref, gt_ref, gp_ref,
             up0_ref, pre_ref, act_ref, dy2_ref, dx2_ref, loss_ref, dgt_ref, dgp_ref, tail_ref):
        i = pl.program_id(0)

        @pl.when(i == 0)
        def _():
            tail_ref[...] = jnp.zeros_like(tail_ref)
            loss_ref[...] = jnp.zeros_like(loss_ref)
            dgt_ref[...] = jnp.zeros_like(dgt_ref)
            dgp_ref[...] = jnp.zeros_like(dgp_ref)

        hb = h2_ref[...]

        def up_project(j):
            sh_g, off, _ = _ffn_cols(j)
            return [_dot(hb, wup_ref[shard, :, off:off + FFN_CHUNK]) for shard in (sh_g, sh_g + 2)]

        y2 = jnp.zeros((ts, d), F32)
        ahead = up_project(0)
        for j in range(nch):
            _, _, col = _ffn_cols(j)
            ubs = ahead
            if j + 1 < nch:
                ahead = up_project(j + 1)
            halves = []
            for u, c0 in zip(ubs, (col, D_FF + col)):
                cs = slice(c0, c0 + FFN_CHUNK)
                up0_ref[:, cs] = u.astype(BF16)
                prev8 = tail_ref[:, cs]
                tail_ref[:, cs] = u[ts - SUBLANES:, :]
                halves.append(fw_ref[2:3, cs] * u + fw_ref[1:2, cs] * _shift_down(u, prev8, 1)
                              + fw_ref[0:1, cs] * _shift_down(u, prev8, 2) + fb_ref[:, cs])
                pre_ref[:, cs] = halves[-1].astype(BF16)
            act = (_gelu(halves[0]) * halves[1]).astype(BF16)
            act_ref[:, col:col + FFN_CHUNK] = act
            y2 = y2 + _dot(act, wdn_ref[col:col + FFN_CHUNK, :])
        r2 = _msq_rsqrt(y2)
        yn = y2 * r2
        yng = yn * gp_ref[...]
        e = x1_ref[...] + gt_ref[...] * yng - tgt_ref[...]
        loss_ref[...] += jnp.sum(e * e) * (0.5 / d)
        dx2 = e * (1.0 / d)
        dx2_ref[...] = dx2
        dgt_ref[...] += _colsum(dx2 * yng)
        dyng = dx2 * gt_ref[...]
        dgp_ref[...] += _colsum(dyng * yn)
        dy2_ref[...] = _rms_bwd(dyng * gp_ref[...], yn, r2).astype(BF16)

    vec = _full((1, d))
    return pl.pallas_call(
        body, grid=(s // ts,), name="ffn_fwd",
        in_specs=[_rows(ts, d), _rows(ts, d), _rows(ts, d), _RESIDENT, _RESIDENT,
                  _full((3, 2 * D_FF)), _full((1, 2 * D_FF)), vec, vec],
        out_specs=[_rows(ts, 2 * D_FF), _rows(ts, 2 * D_FF), _rows(ts, D_FF), _rows(ts, d), _rows(ts, d),
                   _full((1, 128)), vec, vec],
        out_shape=[_sds((s, 2 * D_FF), BF16), _sds((s, 2 * D_FF), BF16), _sds((s, D_FF), BF16), _sds((s, d), BF16),
                   _sds((s, d), F32), _sds((1, 128), F32), _sds((1, d), F32), _sds((1, d), F32)],
        scratch_shapes=[pltpu.VMEM((SUBLANES, 2 * D_FF), F32)],
        compiler_params=_params(("arbitrary",)),
    )(h2, x1, tgt, w_up4, w_down, fw, fb, gt_f, g_post)


def _shift_up_mxu(vb, up_mat, next8, k):
    t = vb.shape[0]
    main = _dot(up_mat, vb)
    tail = pltpu.roll(next8, SUBLANES - k, 0)
    row8 = lax.broadcasted_iota(jnp.int32, next8.shape, 0)
    last = main[t - SUBLANES:] + jnp.where(row8 >= SUBLANES - k, tail, 0.0)
    return jnp.concatenate([main[:t - SUBLANES], last], axis=0)


def _ffn_bwd_a(dy2, pre, up0, w_down, fw, ts=256):
    s, d = dy2.shape
    nt = s // ts
    nch = D_FF // FFN_CHUNK
    wide = 2 * D_FF
    up_mats = jnp.stack([jnp.eye(ts, k=1, dtype=BF16), jnp.eye(ts, k=2, dtype=BF16)])

    def body(dy2_ref, pre_ref, up0_ref, wdn_ref, fw_ref, um_ref, dup0_ref, dfw_ref, dfb_ref, next_ref):
        i = pl.program_id(0)

        @pl.when(i == 0)
        def _():
            next_ref[...] = jnp.zeros_like(next_ref)
            dfw_ref[...] = jnp.zeros_like(dfw_ref)
            dfb_ref[...] = jnp.zeros_like(dfb_ref)

        dyb = dy2_ref[...]
        for j in range(nch):
            _, _, col = _ffn_cols(j)
            dact = _dot_nt(dyb, wdn_ref[col:col + FFN_CHUNK, :])
            gl, dgl = _gelu_and_grad(pre_ref[:, col:col + FFN_CHUNK].astype(F32))
            dpre = (dact * pre_ref[:, D_FF + col:D_FF + col + FFN_CHUNK].astype(F32) * dgl, dact * gl)
            for half, c0 in enumerate((col, D_FF + col)):
                cs = slice(c0, c0 + FFN_CHUNK)
                dp = dpre[half]
                dpb = dp.astype(BF16)
                nxt = next_ref[:, cs]
                next_ref[:, cs] = dpb.astype(F32)[0:SUBLANES, :]
                su1 = _shift_up_mxu(dpb, um_ref[0], nxt, 1)
                su2 = _shift_up_mxu(dpb, um_ref[1], nxt, 2)
                u = up0_ref[:, cs].astype(F32)
                dfb_ref[:, cs] += _colsum(dp)
                dfw_ref[2:3, cs] += _colsum(dp * u)
                dfw_ref[1:2, cs] += _colsum(su1 * u)
                dfw_ref[0:1, cs] += _colsum(su2 * u)
                dup0 = fw_ref[2:3, cs] * dp + fw_ref[1:2, cs] * su1 + fw_ref[0:1, cs] * su2
                dup0_ref[:, cs] = dup0.astype(BF16)

    return pl.pallas_call(
        body, grid=(nt,), name="ffn_bwd_a",
        in_specs=[_rows(ts, d, nt), _rows(ts, wide, nt), _rows(ts, wide, nt), _RESIDENT,
                  _full((3, wide)), _full((2, ts, ts))],
        out_specs=[_rows(ts, wide, nt), _full((3, wide)), _full((1, wide))],
        out_shape=[_sds((s, wide), BF16), _sds((3, wide), F32), _sds((1, wide), F32)],
        scratch_shapes=[pltpu.VMEM((SUBLANES, wide), F32)],
        compiler_params=_params(("arbitrary",)),
    )(dy2, pre, up0, w_down, fw, up_mats)


def _ffn_bwd_b(dup0, x1, y, dx2, w_up4, g_pre2, sc_f, sh_f, gt_m, g_post_m, ts=512):
    s, d = x1.shape
    shard_cols = 2 * D_FF // N_CHIPS

    def body(dup_ref, x1_ref, y_ref, dx2_ref, wup_ref, g2_ref, sc_ref, sh_ref, gt_ref, gp_ref,
             dx1_ref, dy_ref, dsh_ref, dsc_ref, dg2_ref, dgt_ref, dgp_ref):
        i = pl.program_id(0)

        @pl.when(i == 0)
        def _():
            for ref in (dsh_ref, dsc_ref, dg2_ref, dgt_ref, dgp_ref):
                ref[...] = jnp.zeros_like(ref)

        for rs in _sub_tiles(ts):
            dh2 = jnp.zeros((SUB_ROWS, d), F32)
            for k in range(N_CHIPS):
                dh2 = dh2 + _dot_nt(dup_ref[rs, k * shard_cols:(k + 1) * shard_cols], wup_ref[k])
            x1v = x1_ref[rs, :]
            r2 = _msq_rsqrt(x1v)
            xn = x1v * r2
            hn = xn * g2_ref[...]
            dsh_ref[...] += _colsum(dh2)
            dsc_ref[...] += _colsum(dh2 * hn)
            dhn = dh2 * (1.0 + sc_ref[...])
            dg2_ref[...] += _colsum(dhn * xn)
            dx1 = dx2_ref[rs, :] + _rms_bwd(dhn * g2_ref[...], xn, r2)
            dx1_ref[rs, :] = dx1
            yv = y_ref[rs, :]
            ry = _msq_rsqrt(yv)
            yn = yv * ry
            dgt_ref[...] += _colsum(dx1 * (yn * gp_ref[...]))
            dyng = dx1 * gt_ref[...]
            dgp_ref[...] += _colsum(dyng * yn)
            dy_ref[rs, :] = _rms_bwd(dyng * gp_ref[...], yn, ry).astype(BF16)

    vec = _full((1, d))
    return pl.pallas_call(
        body, grid=(s // ts,), name="ffn_bwd_b",
        in_specs=[_rows(ts, 2 * D_FF), _rows(ts, d), _rows(ts, d), _rows(ts, d), _RESIDENT,
                  vec, vec, vec, vec, vec],
        out_specs=[_rows(ts, d), _rows(ts, d), vec, vec, vec, vec, vec],
        out_shape=[_sds((s, d), F32), _sds((s, d), BF16)] + [_sds((1, d), F32)] * 5,
        compiler_params=_params(("arbitrary",)),
    )(dup0, x1, y, dx2, w_up4, g_pre2, sc_f, sh_f, gt_m, g_post_m)


def _seqmix_bwd(lru_x, hst, stash, dy, w_out, seq_params, ws_t, glo, ggo, x, dx1, w_in4, g_pre, sc_m, ts=256):
    s, d = x.shape
    nt = s // ts
    small_shapes = [(4, 512), (1, 512), (512, 512), (512, 512), (1, 512), (1, 512), (1, 512),
                    (1, 512), (1, 512), (4, 128, 128), (128, 4), (1, 512), (1, 512),
                    (1, d), (1, d), (1, d)]

    def body(lx_ref, hst_ref, hprev_ref, st_ref, dy_ref, wout_ref, cw_ref, cb_ref, bdr_ref, bdi_ref, br_ref,
             bi_ref, la_ref, ng_ref, nb_ref, ws_ref, bst_ref, wst_ref, glo_ref, ggo_ref, x_ref, dx1_ref, win_ref,
             gpre_ref, scm_ref, dz_ref, gx_ref, *rest):
        small_refs = rest[:16]
        (dcw_ref, dcb_ref, dwr_ref, dwi_ref, dbr_ref, dbi_ref, dspa_ref, dng_ref, dnb_ref, dws_ref, dbs_ref,
         dglo_ref, dggo_ref, dsh_ref, dsc_ref, dgpre_ref) = small_refs
        gcarry, anext, dxcnext, dv_scr = rest[16:]
        i = pl.program_id(0)

        @pl.when(i == 0)
        def _():
            for ref in small_refs:
                ref[...] = jnp.zeros_like(ref)
            gcarry[...] = jnp.zeros_like(gcarry)
            anext[...] = jnp.ones_like(anext)
            dxcnext[...] = jnp.zeros_like(dxcnext)

        first_tile = i == nt - 1
        xc, r, ig, a, mult = st_ref[ST_XC], st_ref[ST_R], st_ref[ST_IG], st_ref[ST_A], st_ref[ST_MULT]
        gl, u, spb, vhat = st_ref[ST_GL], st_ref[ST_U], st_ref[ST_SPB], st_ref[ST_VHAT]
        lx = lx_ref[...]
        h = hst_ref[...]
        hprev = _shift_down(h, jnp.where(first_tile, 0.0, hprev_ref[...]), 1)
        y_l = h * gl
        y_g = u * spb

        dycat = _dot_nt(dy_ref[...], wout_ref[...])

        dz_parts = {}

        def emit_dz(k, val):
            dz_parts[k] = val.astype(BF16)
            dz_ref[:, k * 512:(k + 1) * 512] = dz_parts[k]

        rl = _msq_rsqrt(y_l)
        yln = y_l * rl
        dyl = dycat[:, 0:512]
        dglo_ref[...] += _colsum(dyl * yln)
        dy_l = _rms_bwd(dyl * glo_ref[...], yln, rl)
        rg = _msq_rsqrt(y_g)
        ygn = y_g * rg
        dyg = dycat[:, 512:1024]
        dggo_ref[...] += _colsum(dyg * ygn)
        dy_g = _rms_bwd(dyg * ggo_ref[...], ygn, rg)

        emit_dz(1, dy_l * h * st_ref[ST_DGL])
        a_up = _shift_up(a, anext[...], 1)
        acum, gloc = _scan_bwd(a_up, dy_l * gl)
        gg = gloc + acum * gcarry[...]
        gcarry[...] = gg[0:1, :]
        anext[...] = a[0:SUBLANES, :]
        da = gg * hprev
        t1 = gg * mult
        di = t1 * xc
        dxc = t1 * ig
        dmult = gg * ig * xc
        dla = da * a - dmult * (a * a / mult)
        dspa_ref[...] += _colsum(dla * r) * (-LRU_C)
        dpr = dla * ((-LRU_C) * _softplus(-la_ref[...])) * r * (1.0 - r)
        dpi = di * ig * (1.0 - ig)
        dbr_ref[...] += _colsum(dpr)
        dbi_ref[...] += _colsum(dpi)
        dprb = dpr.astype(BF16)
        dpib = dpi.astype(BF16)
        xcb = xc.astype(BF16)
        dwr_ref[...] += _dot_tn(xcb, dprb)
        dwi_ref[...] += _dot_tn(xcb, dpib)
        dxc = dxc + _dot_nt(dprb, bdr_ref[...]) + _dot_nt(dpib, bdi_ref[...])
        nxt = dxcnext[...]
        dxcnext[...] = dxc[0:SUBLANES, :]
        up1, up2, up3 = _shift_up(dxc, nxt, 1), _shift_up(dxc, nxt, 2), _shift_up(dxc, nxt, 3)
        dcb_ref[...] += _colsum(dxc)
        dcw_ref[3:4, :] += _colsum(dxc * lx)
        dcw_ref[2:3, :] += _colsum(up1 * lx)
        dcw_ref[1:2, :] += _colsum(up2 * lx)
        dcw_ref[0:1, :] += _colsum(up3 * lx)
        dlx = cw_ref[3:4, :] * dxc + cw_ref[2:3, :] * up1 + cw_ref[1:2, :] * up2 + cw_ref[0:1, :] * up3
        emit_dz(0, dlx)

        emit_dz(2, dy_g * spb * st_ref[ST_DU])
        dsp = dy_g * u
        vb = (vhat * ng_ref[...] + nb_ref[...]).astype(BF16)
        for n in range(ts // GMLP_BLOCK):
            rs = slice(n * GMLP_BLOCK, (n + 1) * GMLP_BLOCK)
            for g in range(GMLP_GROUPS):
                cs = slice(g * 128, (g + 1) * 128)
                dbs_ref[:, g:g + 1] += jnp.sum(dsp[rs, cs], axis=1, keepdims=True)
                blk = dsp[rs, cs].astype(BF16)
                dws_ref[g] += _dot_nt(blk, vb[rs, cs])
                dv_scr[rs, cs] = _dot(wst_ref[g], blk)
        dv = dv_scr[...]
        dng_ref[...] += _colsum(dv * vhat)
        dnb_ref[...] += _colsum(dv)
        dvh = dv * ng_ref[...]
        dvg = dvh - jnp.mean(dvh, axis=-1, keepdims=True) - vhat * jnp.mean(dvh * vhat, axis=-1, keepdims=True)
        emit_dz(3, dvg * st_ref[ST_Q])

        dh = _dot_nt(dz_parts[0], win_ref[0])
        for k in range(1, N_CHIPS):
            dh = dh + _dot_nt(dz_parts[k], win_ref[k])
        xv = x_ref[...]
        rx = _msq_rsqrt(xv)
        xn = xv * rx
        dsh_ref[...] += _colsum(dh)
        dsc_ref[...] += _colsum(dh * (xn * gpre_ref[...]))
        dhn = dh * (1.0 + scm_ref[...])
        dgpre_ref[...] += _colsum(dhn * xn)
        gx_ref[...] = dx1_ref[...] + _rms_bwd(dhn * gpre_ref[...], xn, rx)

        @pl.when(i == nt - 1)
        def _():
            pos = lax.broadcasted_iota(jnp.int32, (GMLP_BLOCK, GMLP_BLOCK), 0) // CHUNK
            src = lax.broadcasted_iota(jnp.int32, (GMLP_BLOCK, GMLP_BLOCK), 1) // CHUNK
            for g in range(GMLP_GROUPS):
                dws_ref[g] = jnp.where(src <= pos, dws_ref[g], 0.0)
            dspa_ref[...] = dspa_ref[...] * (-_sigmoid(-la_ref[...]))

    vec = _full((1, d))
    in_specs = ([_rows(ts, 512, nt), _rows(ts, 512, nt), _halo_prev(ts, 512, SUBLANES, nt),
                 pl.BlockSpec((N_STASH, ts, 512), lambda i: (0, nt - 1 - i, 0)), _rows(ts, d, nt),
                 _full((d, d))]
                + _seq_param_specs() + [_full((4, 128, 128)), _full((1, 512)), _full((1, 512))]
                + [_rows(ts, d, nt), _rows(ts, d, nt), _full(w_in4.shape), vec, vec])
    return pl.pallas_call(
        body, grid=(nt,), name="seqmix_bwd",
        in_specs=in_specs,
        out_specs=[_rows(ts, 2048, nt), _rows(ts, d, nt)] + [_full(sh) for sh in small_shapes],
        out_shape=[_sds((s, 2048), BF16), _sds((s, d), F32)] + [_sds(sh, F32) for sh in small_shapes],
        scratch_shapes=[pltpu.VMEM((1, 512), F32), pltpu.VMEM((SUBLANES, 512), F32),
                        pltpu.VMEM((SUBLANES, 512), F32), pltpu.VMEM((ts, 512), F32)],
        compiler_params=_params(("arbitrary",)),
    )(lru_x, hst, hst, stash, dy, w_out, *seq_params, ws_t, glo, ggo, x, dx1, w_in4, g_pre, sc_m)


def _wgrad(a, b, n_chunks, name, chunk_major, ts=2048):
    s, m = a.shape
    n = b.shape[1]
    nc = n // n_chunks
    nt = s // ts

    def body(a_ref, b_ref, o_ref, acc):
        i = pl.program_id(1)

        @pl.when(i == 0)
        def _():
            acc[...] = jnp.zeros_like(acc)

        acc[...] += _dot_tn(a_ref[...], b_ref[...])

        @pl.when(i == nt - 1)
        def _():
            if chunk_major:
                o_ref[0] = acc[...].astype(BF16)
            else:
                o_ref[...] = acc[...].astype(BF16)

    if chunk_major:
        out_spec, out_shape = pl.BlockSpec((1, m, nc), lambda c, i: (c, 0, 0)), _sds((n_chunks, m, nc), BF16)
    else:
        out_spec, out_shape = pl.BlockSpec((m, nc), lambda c, i: (0, c)), _sds((m, n), BF16)
    return pl.pallas_call(
        body, grid=(n_chunks, nt), name=name,
        in_specs=[pl.BlockSpec((ts, m), lambda c, i: (i, 0)), pl.BlockSpec((ts, nc), lambda c, i: (i, c))],
        out_specs=out_spec,
        out_shape=out_shape,
        scratch_shapes=[pltpu.VMEM((m, nc), F32)],
        compiler_params=_params(("parallel", "arbitrary")),
    )(a, b)


def _block_diag(w):
    heads, hd, _ = w.shape
    eye = jnp.eye(heads, dtype=w.dtype)
    return (eye[:, None, :, None] * w[:, :, None, :]).reshape(heads * hd, heads * hd)


def _seq_params(small):
    row = lambda v: v.reshape(1, -1)
    pos = jnp.arange(GMLP_BLOCK)
    mask = (pos[None, :] // CHUNK) <= (pos[:, None] // CHUNK)
    ws = jnp.where(mask[None], small["w_spatial"], 0.0)
    seq_params = (small["conv_w"], row(small["conv_b"]),
                  _block_diag(small["w_rgate"]).astype(BF16), _block_diag(small["w_igate"]).astype(BF16),
                  row(small["b_rgate"]), row(small["b_igate"]), row(small["lru_a"]),
                  row(small["v_norm_g"]), row(small["v_norm_b"]), ws.astype(BF16), small["b_spatial"].T)
    return seq_params, jnp.swapaxes(ws, 1, 2).astype(BF16)


_ANY = pl.BlockSpec(memory_space=pl.ANY)
_CHIP_FLIPS = ((1, 0), (0, 1), (1, 1))


def _position():
    return lax.axis_index("x"), lax.axis_index("y"), lax.axis_index("c")


def _flip(v, f):
    return 1 - v if f else v


def _remote(src, dst, send_sem, recv_sem, peer):
    return pltpu.make_async_remote_copy(src_ref=src, dst_ref=dst, send_sem=send_sem, recv_sem=recv_sem,
                                        device_id=peer, device_id_type=MESH)


def _allgather8(block, name):
    r, n = block.shape

    def body(x_ref, gath, send_sems, recv_sems, loc_sem):
        x, y, c = _position()
        me = 4 * x + 2 * y + c
        loc = pltpu.make_async_copy(x_ref, gath.at[me], loc_sem)
        loc.start()
        peers = []
        for k in range(1, N_DEV):
            px, py, pc = _flip(x, k & 4), _flip(y, k & 2), _flip(c, k & 1)
            peers.append((px, py, pc))
            _remote(x_ref, gath.at[me], send_sems.at[k - 1], recv_sems.at[k - 1], (px, py, pc)).start()
        for k, (px, py, pc) in enumerate(peers):
            src = 4 * px + 2 * py + pc
            _remote(x_ref, gath.at[src], send_sems.at[k], recv_sems.at[k], (px, py, pc)).wait_recv()
        for k, peer in enumerate(peers):
            _remote(x_ref, gath.at[me], send_sems.at[k], recv_sems.at[k], peer).wait_send()
        loc.wait()

    return pl.pallas_call(
        body, name=name, out_shape=_sds((N_DEV, r, n), F32),
        in_specs=[pl.BlockSpec(memory_space=pltpu.VMEM)], out_specs=pl.BlockSpec(memory_space=pltpu.VMEM),
        scratch_shapes=[pltpu.SemaphoreType.DMA((N_DEV - 1,)), pltpu.SemaphoreType.DMA((N_DEV - 1,)),
                        pltpu.SemaphoreType.DMA],
        compiler_params=pltpu.CompilerParams(vmem_limit_bytes=VMEM_LIMIT_BYTES),
    )(block)


def _half(ref, c, rows):
    hr = rows // 2
    return ref.at[pl.ds(pl.multiple_of(c * hr, BF16_SUBLANES), hr), :]


def _chip_sum(part, recv, pos_arr, name):
    _, rows, cols = part.shape
    hr = rows // 2

    def body(pos_ref, p_ref, r_ref, o_ref, g_ref):
        total = (p_ref[...].astype(F32) + r_ref[...].astype(F32)).astype(BF16)
        o_ref[...] = total

        @pl.when(pl.program_id(0) == pos_ref[1])
        def _():
            g_ref[0] = total

    grid_spec = pltpu.PrefetchScalarGridSpec(
        num_scalar_prefetch=1, grid=(N_CHIPS,),
        in_specs=[pl.BlockSpec((1, hr, cols), lambda k, pos: (k, pos[0], 0)),
                  pl.BlockSpec((1, hr, cols), lambda k, pos: (k, 0, 0))],
        out_specs=[pl.BlockSpec((1, hr, cols), lambda k, pos: (k, 0, 0)),
                   pl.BlockSpec((1, 1, hr, cols), lambda k, pos: (0, pos[1], 0, 0))])
    return pl.pallas_call(
        body, name=name, grid_spec=grid_spec,
        out_shape=[_sds((N_CHIPS, hr, cols), BF16), _sds((2, N_CHIPS, hr, cols), BF16)],
        compiler_params=_params(("arbitrary",)),
    )(pos_arr, part, recv)


_HBM = pl.BlockSpec(memory_space=pltpu.HBM)
_SEM = pl.BlockSpec(memory_space=pltpu.SEMAPHORE)
_EFFECT = pltpu.SideEffectType.DATAFLOW_SIDE_EFFECTING


def _in_hbm(a):
    return pltpu.with_memory_space_constraint(a, pltpu.HBM)


def _split_start(srcs, lands, plan, n_copies, after, name):
    ns, nl = len(srcs), len(lands)
    bufs = list(srcs) + list(lands)

    def body(*refs):
        send_sems, recv_sems = refs[ns + nl + 1], refs[ns + nl + 2]
        token = refs[-1]
        for k, (src, dst, peer) in enumerate(plan(refs[:ns], refs[ns:ns + nl])):
            _remote(src, dst, send_sems.at[k], recv_sems.at[k], peer).start()
        token[...] = jnp.zeros_like(token)

    out = pl.pallas_call(
        body, name=name,
        out_shape=(pltpu.SemaphoreType.DMA((n_copies,)), pltpu.SemaphoreType.DMA((n_copies,)),
                   *[pltpu.HBM(b.shape, b.dtype) for b in bufs], _sds((SUBLANES, 128), F32)),
        in_specs=[_HBM] * (ns + nl) + [_ANY],
        out_specs=(_SEM, _SEM, *[_HBM] * (ns + nl), pl.BlockSpec(memory_space=pltpu.VMEM)),
        input_output_aliases={i: 2 + i for i in range(ns + nl)},
        compiler_params=pltpu.CompilerParams(has_side_effects=_EFFECT),
    )(*[_in_hbm(b) for b in bufs], after)
    return out[0], out[1], list(out[2:2 + ns]), list(out[2 + ns:2 + ns + nl]), out[-1]


def _split_wait(send_sems, recv_sems, srcs, lands, plan, after, name):
    ns, nl = len(srcs), len(lands)
    bufs = list(srcs) + list(lands)

    def body(*refs):
        send_ref, recv_ref = refs[ns + nl], refs[ns + nl + 1]
        me = _position()
        for k, src, dst in plan(refs[:ns], refs[ns:ns + nl]):
            cp = _remote(src, dst, send_ref.at[k], recv_ref.at[k], me)
            cp.wait_send()
            cp.wait_recv()

    out = pl.pallas_call(
        body, name=name,
        out_shape=[pltpu.HBM(b.shape, b.dtype) for b in bufs],
        in_specs=[_HBM] * (ns + nl) + [_SEM, _SEM, _ANY],
        out_specs=[_HBM] * (ns + nl),
        input_output_aliases={i: i for i in range(ns + nl)},
        compiler_params=pltpu.CompilerParams(has_side_effects=_EFFECT),
    )(*bufs, send_sems, recv_sems, after)
    return list(out[:ns]), list(out[ns:])


BULK_CORE = 1


def _static_half(ref, h, rows):
    hr = rows // 2
    return ref.at[pl.ds(h * hr, hr), :]


def _bulk_start(lands, rows_of, after, name):
    nl = len(lands)

    def body(*refs):
        land_refs = refs[:nl]
        send_sems, recv_sems = refs[nl + 1], refs[nl + 2]
        token = refs[-1]
        x, y, c = _position()
        chip = 2 * x + y

        @pl.when(c == BULK_CORE)
        def _():
            for first in range(0, nl, 2):
                for h in range(2):
                    for a in range(first, min(first + 2, nl)):
                        piece = _static_half(land_refs[a].at[chip], h, rows_of[a])
                        for j, (fx, fy) in enumerate(_CHIP_FLIPS):
                            _remote(piece, piece, send_sems.at[3 * nl * h + 3 * a + j], recv_sems.at[3 * a + j],
                                    (_flip(x, fx), _flip(y, fy), h)).start()

        token[...] = jnp.zeros_like(token)

    out = pl.pallas_call(
        body, name=name,
        out_shape=(pltpu.SemaphoreType.DMA((6 * nl,)), pltpu.SemaphoreType.DMA((3 * nl,)),
                   *[pltpu.HBM(b.shape, b.dtype) for b in lands], _sds((SUBLANES, 128), F32)),
        in_specs=[_HBM] * nl + [_ANY],
        out_specs=(_SEM, _SEM, *[_HBM] * nl, pl.BlockSpec(memory_space=pltpu.VMEM)),
        input_output_aliases={i: 2 + i for i in range(nl)},
        compiler_params=pltpu.CompilerParams(has_side_effects=_EFFECT),
    )(*[_in_hbm(b) for b in lands], after)
    return out[0], out[1], list(out[2:2 + nl]), out[-1]


def _bulk_wait(send_sems, recv_sems, lands, rows_of, first, total, after, name):
    nl = len(lands)

    def body(*refs):
        land_refs = refs[:nl]
        send_ref, recv_ref = refs[nl], refs[nl + 1]
        x, y, c = _position()
        chip = 2 * x + y
        me = (x, y, c)
        for a, rows in enumerate(rows_of):
            for j, (fx, fy) in enumerate(_CHIP_FLIPS):
                landed = _half(land_refs[a].at[2 * _flip(x, fx) + _flip(y, fy)], c, rows)
                _remote(landed, landed, send_ref.at[0], recv_ref.at[3 * (first + a) + j], me).wait_recv()

        @pl.when(c == BULK_CORE)
        def _():
            for h in range(2):
                for a, rows in enumerate(rows_of):
                    piece = _static_half(land_refs[a].at[chip], h, rows)
                    for j in range(3):
                        _remote(piece, piece, send_ref.at[3 * total * h + 3 * (first + a) + j], recv_ref.at[0],
                                me).wait_send()

    out = pl.pallas_call(
        body, name=name,
        out_shape=[pltpu.HBM(b.shape, b.dtype) for b in lands],
        in_specs=[_HBM] * nl + [_SEM, _SEM, _ANY],
        out_specs=[_HBM] * nl,
        input_output_aliases={i: i for i in range(nl)},
        compiler_params=pltpu.CompilerParams(has_side_effects=_EFFECT),
    )(*lands, send_sems, recv_sems, after)
    return list(out)


def _allgather8_relay(block, name):
    r, n = block.shape

    def body(x_ref, out_ref, d2d_send, d2d_recv, ici_send, ici_recv):
        x, y, c = _position()
        chip = 2 * x + y
        sibling = (x, y, 1 - c)

        @pl.when(c == BULK_CORE)
        def _():
            mine = _remote(x_ref, out_ref.at[2 * chip + BULK_CORE], d2d_send.at[0], d2d_recv.at[0], sibling)
            mine.start()
            mine.wait_send()
            _remote(out_ref, out_ref, d2d_send.at[1], d2d_recv.at[1], sibling).wait_recv()

        @pl.when(c != BULK_CORE)
        def _():
            out_ref[2 * chip + 1 - BULK_CORE] = x_ref[...]
            theirs = out_ref.at[2 * chip + BULK_CORE]
            _remote(theirs, theirs, d2d_send.at[0], d2d_recv.at[0], sibling).wait_recv()
            pair = out_ref.at[pl.ds(2 * chip, 2)]
            sends = []
            for j, (fx, fy) in enumerate(_CHIP_FLIPS):
                sends.append(_remote(pair, pair, ici_send.at[j], ici_recv.at[j], (_flip(x, fx), _flip(y, fy), c)))
                sends[-1].start()
            for j, (fx, fy) in enumerate(_CHIP_FLIPS):
                landed = out_ref.at[pl.ds(2 * (2 * _flip(x, fx) + _flip(y, fy)), 2)]
                _remote(landed, landed, ici_send.at[j], ici_recv.at[j], sibling).wait_recv()
            for cp in sends:
                cp.wait_send()
            back = _remote(out_ref, out_ref, d2d_send.at[1], d2d_recv.at[1], sibling)
            back.start()
            back.wait_send()

    vmem = pl.BlockSpec(memory_space=pltpu.VMEM)
    return pl.pallas_call(
        body, name=name, out_shape=_sds((N_DEV, r, n), F32), in_specs=[vmem], out_specs=vmem,
        scratch_shapes=[pltpu.SemaphoreType.DMA((2,)), pltpu.SemaphoreType.DMA((2,)),
                        pltpu.SemaphoreType.DMA((3,)), pltpu.SemaphoreType.DMA((3,))],
        compiler_params=pltpu.CompilerParams(vmem_limit_bytes=VMEM_LIMIT_BYTES),
    )(block)


def _forward_plan(rows_of):
    def pieces(land_refs, half):
        x, y, _ = _position()
        return [_half(land_refs[a].at[2 * _flip(x, fx) + _flip(y, fy)], half, rows)
                for a, rows in enumerate(rows_of) for fx, fy in _CHIP_FLIPS]

    def start(src_refs, land_refs):
        x, y, c = _position()
        return [(p, p, (x, y, 1 - c)) for p in pieces(land_refs, c)]

    def wait(src_refs, land_refs):
        _, _, c = _position()
        return [(k, mine, theirs)
                for k, (mine, theirs) in enumerate(zip(pieces(land_refs, c), pieces(land_refs, 1 - c)))]

    return start, wait


def _swap_halves_plan(half_rows):
    def slices(src_refs, c):
        return [src_refs[a].at[:, pl.ds(pl.multiple_of((1 - c) * hr, BF16_SUBLANES), hr), :]
                for a, hr in enumerate(half_rows)]

    def start(src_refs, land_refs):
        x, y, c = _position()
        return [(src, land_refs[a], (x, y, 1 - c)) for a, src in enumerate(slices(src_refs, c))]

    def wait(src_refs, land_refs):
        _, _, c = _position()
        return [(a, src, land_refs[a]) for a, src in enumerate(slices(src_refs, c))]

    return start, wait


def _swap_gathered_plan(n_arrays):
    def start(src_refs, land_refs):
        x, y, c = _position()
        return [(land_refs[a].at[0], land_refs[a].at[1], (x, y, 1 - c)) for a in range(n_arrays)]

    def wait(src_refs, land_refs):
        return [(a, land_refs[a].at[0], land_refs[a].at[1]) for a in range(n_arrays)]

    return start, wait


def _exchange_plan(n_arrays):
    def start(src_refs, land_refs):
        x, y, c = _position()
        chip = 2 * x + y
        out = []
        for a in range(n_arrays):
            for fx, fy in _CHIP_FLIPS:
                px, py = _flip(x, fx), _flip(y, fy)
                out.append((src_refs[a].at[2 * px + py], land_refs[a].at[0, chip], (px, py, c)))
        return out

    def wait(src_refs, land_refs):
        x, y, c = _position()
        out = []
        for a in range(n_arrays):
            for j, (fx, fy) in enumerate(_CHIP_FLIPS):
                src_chip = 2 * _flip(x, fx) + _flip(y, fy)
                out.append((3 * a + j, src_refs[a].at[src_chip], land_refs[a].at[0, src_chip]))
        return out

    return start, wait


def _adam_gathered(w, gath, m, v, c_arr, after, name, tr=128):
    rows, cols = w.shape
    hr = rows // 2
    if hr % (2 * tr) == 0:
        tr = 2 * tr
    per = hr // tr

    def body(c_ref, w_ref, g_ref, m_ref, v_ref, after_ref, go_ref, d_ref, nm_ref, nv_ref):
        g = g_ref[0, 0].astype(F32)
        for k in range(1, N_CHIPS):
            g = g + g_ref[0, k].astype(F32)
        go_ref[...] = g
        d_ref[...], nm_ref[...], nv_ref[...] = _adam_math(w_ref[...], g, m_ref[...], v_ref[...])

    def rows_of(h, i, c_ref):
        c = c_ref[0]
        return ((c + h - 2 * c * h) * per + i, 0)

    blk = pl.BlockSpec((tr, cols), rows_of)
    grid_spec = pltpu.PrefetchScalarGridSpec(
        num_scalar_prefetch=1, grid=(2, per),
        in_specs=[blk, pl.BlockSpec((1, N_CHIPS, tr, cols), lambda h, i, c_ref: (h, 0, i, 0)), blk, blk, _ANY],
        out_specs=[blk] * 4)
    return pl.pallas_call(
        body, name=name, grid_spec=grid_spec, out_shape=[_sds(w.shape, F32)] * 4,
        compiler_params=_params(("arbitrary", "arbitrary")),
    )(c_arr, w, gath, m, v, after)


def _allreduce_small(block, name):
    two, r, n = block.shape
    assert two == 2

    def body(x_ref, out_ref, sib, chipsum, gath, d2d_send, d2d_recv, ici_send, ici_recv):
        x, y, c = _position()
        chip = 2 * x + y
        sibling = (x, y, 1 - c)
        first = _remote(x_ref, sib, d2d_send.at[0], d2d_recv.at[0], sibling)
        first.start()
        first.wait()
        chipsum[...] = x_ref[...] + sib[...]
        sends = []
        for j, (fx, fy) in enumerate(_CHIP_FLIPS):
            sends.append(_remote(chipsum.at[c], gath.at[chip], ici_send.at[j], ici_recv.at[j],
                                 (_flip(x, fx), _flip(y, fy), c)))
            sends[-1].start()
        gath[chip] = chipsum[c]
        for j, (fx, fy) in enumerate(_CHIP_FLIPS):
            landed = gath.at[2 * _flip(x, fx) + _flip(y, fy)]
            _remote(landed, landed, ici_send.at[j], ici_recv.at[j], sibling).wait_recv()
        for cp in sends:
            cp.wait_send()
        total = gath[0]
        for k in range(1, N_CHIPS):
            total = total + gath[k]
        out_ref[c] = total
        last = _remote(out_ref.at[c], out_ref.at[c], d2d_send.at[1], d2d_recv.at[1], sibling)
        last.start()
        _remote(out_ref.at[1 - c], out_ref.at[1 - c], d2d_send.at[1], d2d_recv.at[1], sibling).wait_recv()
        last.wait_send()

    vmem = pl.BlockSpec(memory_space=pltpu.VMEM)
    return pl.pallas_call(
        body, name=name, out_shape=_sds(block.shape, F32), in_specs=[vmem], out_specs=vmem,
        scratch_shapes=[pltpu.VMEM(block.shape, F32), pltpu.VMEM(block.shape, F32), pltpu.VMEM((N_CHIPS, r, n), F32),
                        pltpu.SemaphoreType.DMA((2,)), pltpu.SemaphoreType.DMA((2,)),
                        pltpu.SemaphoreType.DMA((3,)), pltpu.SemaphoreType.DMA((3,))],
        compiler_params=pltpu.CompilerParams(vmem_limit_bytes=VMEM_LIMIT_BYTES),
    )(block)


def _cast_place(shards, chip_arr, name):
    na = len(shards)
    steps = 4

    def body(chip_ref, *refs):
        for a in range(na):
            refs[na + a][0] = refs[a][...].astype(BF16)

    grid_spec = pltpu.PrefetchScalarGridSpec(
        num_scalar_prefetch=1, grid=(steps,),
        in_specs=[pl.BlockSpec((s.shape[0] // steps, s.shape[1]), lambda i, ch: (i, 0)) for s in shards],
        out_specs=[pl.BlockSpec((1, s.shape[0] // steps, s.shape[1]), lambda i, ch: (ch[0], i, 0)) for s in shards])
    return pl.pallas_call(
        body, name=name, grid_spec=grid_spec,
        out_shape=[_sds((N_CHIPS,) + s.shape, BF16) for s in shards],
        compiler_params=_params(("arbitrary",)),
    )(chip_arr, *shards)


def _silu(v):
    return v * _sigmoid(v)


def _ada_fwd(c8, w_ada):
    def body(c_ref, w_ref, o_ref):
        o_ref[...] = jnp.dot(_silu(c_ref[...]), w_ref[...], preferred_element_type=F32,
                             precision=lax.Precision.HIGHEST)

    return pl.pallas_call(
        body, name="ada_fwd", out_shape=_sds((N_DEV, w_ada.shape[1]), F32),
        compiler_params=pltpu.CompilerParams(vmem_limit_bytes=VMEM_LIMIT_BYTES),
    )(c8, w_ada)


def _mod_select(parts, b_ada, me_arr, after):
    cols = parts.shape[2]

    def body(me_ref, p_ref, b_ref, after_ref, o_ref):
        me = me_ref[0]
        for k in range(N_CHIPS):
            cs = slice(k * cols, (k + 1) * cols)
            o_ref[:, cs] = p_ref[2 * k, pl.ds(me, 1), :] + b_ref[:, cs]

    grid_spec = pltpu.PrefetchScalarGridSpec(
        num_scalar_prefetch=1, grid=(1,),
        in_specs=[pl.BlockSpec(parts.shape, lambda i, m: (0, 0, 0)), pl.BlockSpec(b_ada.shape, lambda i, m: (0, 0)),
                  _ANY],
        out_specs=pl.BlockSpec(b_ada.shape, lambda i, m: (0, 0)))
    return pl.pallas_call(body, name="mod_select", grid_spec=grid_spec, out_shape=_sds(b_ada.shape, F32))(
        me_arr, parts, b_ada, after)


def _ada_bwd(c8, dmod8, chip_arr, w, m, v, tr=512):
    d = c8.shape[1]
    cols = dmod8.shape[1] // N_CHIPS

    def body(chip_ref, c_ref, dm_ref, dmall_ref, w_ref, m_ref, v_ref, gw_ref, d_ref, nm_ref, nv_ref, gb_ref):
        g = lax.dot_general(_silu(c_ref[...]), dm_ref[...], (((0,), (0,)), ((), ())),
                            preferred_element_type=F32, precision=lax.Precision.HIGHEST)
        gw_ref[...] = g
        d_ref[...], nm_ref[...], nv_ref[...] = _adam_math(w_ref[...], g, m_ref[...], v_ref[...])
        acc = dmall_ref[0:1, :]
        for k in range(1, N_DEV):
            acc = acc + dmall_ref[k:k + 1, :]
        gb_ref[...] = acc

    rows = pl.BlockSpec((tr, cols), lambda i, ch: (i, 0))
    grid_spec = pltpu.PrefetchScalarGridSpec(
        num_scalar_prefetch=1, grid=(d // tr,),
        in_specs=[pl.BlockSpec((N_DEV, tr), lambda i, ch: (0, i)),
                  pl.BlockSpec((N_DEV, cols), lambda i, ch: (0, ch[0])),
                  pl.BlockSpec(dmod8.shape, lambda i, ch: (0, 0)), rows, rows, rows],
        out_specs=[rows] * 4 + [pl.BlockSpec((1, dmod8.shape[1]), lambda i, ch: (0, 0))])
    return pl.pallas_call(
        body, name="ada_bwd", grid_spec=grid_spec,
        out_shape=[_sds((d, cols), F32)] * 4 + [_sds((1, dmod8.shape[1]), F32)],
        compiler_params=_params(("arbitrary",)),
    )(chip_arr, c8, dmod8, dmod8, w, m, v)


def _adam_math(w, g, m, v):
    m = ADAM_B1 * m + (1.0 - ADAM_B1) * g
    v = ADAM_B2 * v + (1.0 - ADAM_B2) * (g * g)
    m_hat = m / (1.0 - ADAM_B1 ** ADAM_STEP)
    v_hat = v / (1.0 - ADAM_B2 ** ADAM_STEP)
    delta = -ADAM_LR * (m_hat / (jnp.sqrt(v_hat) + ADAM_EPS) + ADAM_WD * w)
    return delta, m, v


def _adam(w, g, m, v, name, tr=256):
    rows, cols = w.shape
    if rows % tr:
        tr = rows

    def body(w_ref, g_ref, m_ref, v_ref, d_ref, nm_ref, nv_ref):
        d_ref[...], nm_ref[...], nv_ref[...] = _adam_math(w_ref[...], g_ref[...], m_ref[...], v_ref[...])

    spec = pl.BlockSpec((tr, cols), lambda i: (i, 0))
    return pl.pallas_call(
        body, name=name, grid=(rows // tr,), in_specs=[spec] * 4, out_specs=[spec] * 3,
        out_shape=[_sds(w.shape, F32)] * 3, compiler_params=_params(("parallel",)),
    )(w, g, m, v)


SMALL_REPLICATED = ("g_mix_pre", "g_mix_post", "conv_b", "w_rgate", "b_rgate", "w_igate", "b_igate", "lru_a",
                    "v_norm_g", "v_norm_b", "w_spatial", "b_spatial", "g_lru_out", "g_gmlp_out", "g_ffn_pre",
                    "g_ffn_post", "ffn_conv_b")
SMALL_COLUMN_SHARDED = ("conv_w", "ffn_conv_w")

SMALL_ROW_LEN = 86016
_SMALL_ROWS = (
    (("ffn_conv_w", 18432), ("conv_w", 2048), ("w_spatial", 65536)),
    (("w_rgate", 32768), ("w_igate", 32768), ("ffn_conv_b", 6144), ("g_mix_pre", 1024), ("g_mix_post", 1024),
     ("g_ffn_pre", 1024), ("g_ffn_post", 1024), ("conv_b", 512), ("b_rgate", 512), ("b_igate", 512),
     ("lru_a", 512), ("v_norm_g", 512), ("v_norm_b", 512), ("b_spatial", 512), ("g_lru_out", 512),
     ("g_gmlp_out", 512), ("loss", 128)),
)


def _small_slots():
    slots = {}
    for row, entries in enumerate(_SMALL_ROWS):
        off = 0
        for name, size in entries:
            slots[name] = (row, off)
            off += size
        assert off <= SMALL_ROW_LEN
    return slots


SMALL_SLOT = _small_slots()
SMALL_LANES = SMALL_ROW_LEN // SUBLANES


def _small_pieces(name, first, count):
    row, off = SMALL_SLOT[name]
    pos, pieces = off + first, []
    while count:
        sub, lane = divmod(pos, SMALL_LANES)
        n = min(count, SMALL_LANES - lane)
        pieces.append((row, sub, lane, n))
        pos, count = pos + n, count - n
    return pieces
ROW_VECTORS = ("ffn_conv_b", "g_mix_pre", "g_mix_post", "g_ffn_pre", "g_ffn_post", "conv_b", "lru_a", "v_norm_g",
               "v_norm_b", "g_lru_out", "g_gmlp_out")
HEAD_DIM = LRU_WIDTH // LRU_HEADS


def _pack_small(g, after):
    order = ("ffn_conv_w", "conv_w", "w_spatial", "w_rgate", "w_igate", "b_rgate", "b_igate", "b_spatial", "loss") \
        + ROW_VECTORS
    vmem = pl.BlockSpec(memory_space=pltpu.VMEM)

    def body(*refs):
        src = dict(zip(order, refs))
        out_ref = refs[len(order) + 1]
        out_ref[...] = jnp.zeros_like(out_ref)

        def put(name, first, val):
            col = 0
            for row, sub, lane, n in _small_pieces(name, first, val.shape[1]):
                out_ref[row, sub:sub + 1, lane:lane + n] = val[:, col:col + n]
                col += n

        for name in ROW_VECTORS + ("b_rgate", "b_igate", "loss"):
            put(name, 0, src[name][...])
        for name in ("ffn_conv_w", "conv_w"):
            k_taps, n = src[name].shape
            for k in range(k_taps):
                put(name, k * n, src[name][k:k + 1, :])
        for g_idx in range(GMLP_GROUPS):
            for i in range(GMLP_BLOCK):
                put("w_spatial", (g_idx * GMLP_BLOCK + i) * GMLP_BLOCK, src["w_spatial"][g_idx, i:i + 1, :])
        for name in ("w_rgate", "w_igate"):
            for h in range(LRU_HEADS):
                for i in range(HEAD_DIM):
                    r = h * HEAD_DIM + i
                    put(name, r * HEAD_DIM, src[name][r:r + 1, h * HEAD_DIM:(h + 1) * HEAD_DIM])
        eye = (lax.broadcasted_iota(jnp.int32, (GMLP_BLOCK, GMLP_BLOCK), 0)
               == lax.broadcasted_iota(jnp.int32, (GMLP_BLOCK, GMLP_BLOCK), 1))
        for g_idx in range(GMLP_GROUPS):
            col = src["b_spatial"][:, g_idx:g_idx + 1]
            put("b_spatial", g_idx * GMLP_BLOCK, _colsum(jnp.where(eye, col, 0.0)))

    return pl.pallas_call(
        body, name="pack_small", out_shape=_sds((2, SUBLANES, SMALL_LANES), F32),
        in_specs=[vmem] * len(order) + [_ANY], out_specs=vmem,
        compiler_params=pltpu.CompilerParams(vmem_limit_bytes=VMEM_LIMIT_BYTES),
    )(*[g[n] for n in order], after)


def _adam_small(g_small, w, m, v):
    vmem = pl.BlockSpec(memory_space=pltpu.VMEM)
    n_p = len(SMALL_REPLICATED)

    def body(g_ref, *refs):
        w_refs, m_refs, v_refs = refs[:n_p], refs[n_p:2 * n_p], refs[2 * n_p:3 * n_p]
        outs = refs[3 * n_p:]
        go, do, mo, vo = outs[:n_p], outs[n_p:2 * n_p], outs[2 * n_p:3 * n_p], outs[3 * n_p:]
        for k, name in enumerate(SMALL_REPLICATED):
            def take(first, count, name=name):
                parts = [g_ref[row, sub:sub + 1, lane:lane + n]
                         for row, sub, lane, n in _small_pieces(name, first, count)]
                return parts[0] if len(parts) == 1 else jnp.concatenate(parts, axis=1)

            shape = w_refs[k].shape
            if name in ROW_VECTORS:
                go[k][...] = take(0, shape[1])
            elif name in ("b_rgate", "b_igate"):
                for h in range(LRU_HEADS):
                    go[k][0, h:h + 1, :] = take(h * HEAD_DIM, HEAD_DIM)
            elif name == "b_spatial":
                for g_idx in range(GMLP_GROUPS):
                    go[k][0, g_idx:g_idx + 1, :] = take(g_idx * GMLP_BLOCK, GMLP_BLOCK)
            elif name == "w_spatial":
                for g_idx in range(GMLP_GROUPS):
                    for i in range(GMLP_BLOCK):
                        go[k][0, g_idx, i:i + 1, :] = take((g_idx * GMLP_BLOCK + i) * GMLP_BLOCK, GMLP_BLOCK)
            else:
                for h in range(LRU_HEADS):
                    for i in range(HEAD_DIM):
                        go[k][0, h, i:i + 1, :] = take((h * HEAD_DIM + i) * HEAD_DIM, HEAD_DIM)
            do[k][...], mo[k][...], vo[k][...] = _adam_math(w_refs[k][...], go[k][...], m_refs[k][...],
                                                             v_refs[k][...])

    names = SMALL_REPLICATED
    out_shape = [_sds(w[n].shape, F32) for n in names] * 4
    res = pl.pallas_call(
        body, name="adam_small", out_shape=out_shape,
        in_specs=[vmem] * (1 + 3 * n_p), out_specs=[vmem] * (4 * n_p),
        compiler_params=pltpu.CompilerParams(vmem_limit_bytes=VMEM_LIMIT_BYTES),
    )(g_small, *[w[n] for n in names], *[m[n] for n in names], *[v[n] for n in names])
    return [dict(zip(names, res[k * n_p:(k + 1) * n_p])) for k in range(4)]


def _adam_cols(name, g_small, w, m, v, chip_arr):
    _, k_taps, n = w.shape
    row, off = SMALL_SLOT[name]
    first = off // n
    per_sub = SMALL_LANES // n

    def body(chip_ref, *refs):
        g_refs = refs[:k_taps]
        w_ref, m_ref, v_ref, go_ref, d_ref, nm_ref, nv_ref = refs[k_taps:]
        for k in range(k_taps):
            tap = (0, slice(k, k + 1), slice(None))
            sub = (first + N_CHIPS * k + chip_ref[0]) // per_sub
            g = g_refs[k][row, pl.ds(sub, 1), :]
            go_ref[tap] = g
            d_ref[tap], nm_ref[tap], nv_ref[tap] = _adam_math(w_ref[tap], g, m_ref[tap], v_ref[tap])

    whole = pl.BlockSpec(w.shape, lambda i, ch: (0, 0, 0))
    taps = [pl.BlockSpec((2, SUBLANES, n),
                         functools.partial(lambda i, ch, k: (0, 0, (first + N_CHIPS * k + ch[0]) % per_sub), k=k))
            for k in range(k_taps)]
    grid_spec = pltpu.PrefetchScalarGridSpec(
        num_scalar_prefetch=1, grid=(1,), in_specs=taps + [whole] * 3, out_specs=[whole] * 4)
    return pl.pallas_call(body, name="adam_" + name, grid_spec=grid_spec, out_shape=[_sds(w.shape, F32)] * 4)(
        chip_arr, *[g_small] * k_taps, w, m, v)


def kernel(x, c, w_ada, b_ada, g_mix_pre, g_mix_post, w_in, conv_w, conv_b, w_rgate, b_rgate, w_igate, b_igate, lru_a, v_norm_g, v_norm_b, w_spatial, b_spatial, g_lru_out, g_gmlp_out, w_out, g_ffn_pre, g_ffn_post, w_up, ffn_conv_w, ffn_conv_b, w_down, loss_target, m_w_ada, m_b_ada, m_g_mix_pre, m_g_mix_post, m_w_in, m_conv_w, m_conv_b, m_w_rgate, m_b_rgate, m_w_igate, m_b_igate, m_lru_a, m_v_norm_g, m_v_norm_b, m_w_spatial, m_b_spatial, m_g_lru_out, m_g_gmlp_out, m_w_out, m_g_ffn_pre, m_g_ffn_post, m_w_up, m_ffn_conv_w, m_ffn_conv_b, m_w_down, v_w_ada, v_b_ada, v_g_mix_pre, v_g_mix_post, v_w_in, v_conv_w, v_conv_b, v_w_rgate, v_b_rgate, v_w_igate, v_b_igate, v_lru_a, v_v_norm_g, v_v_norm_b, v_w_spatial, v_b_spatial, v_g_lru_out, v_g_gmlp_out, v_w_out, v_g_ffn_pre, v_g_ffn_post, v_w_up, v_ffn_conv_w, v_ffn_conv_b, v_w_down):
    args = dict(locals())
    names = ("w_ada", "b_ada", "g_mix_pre", "g_mix_post", "w_in", "conv_w", "conv_b", "w_rgate", "b_rgate",
             "w_igate", "b_igate", "lru_a", "v_norm_g", "v_norm_b", "w_spatial", "b_spatial", "g_lru_out",
             "g_gmlp_out", "w_out", "g_ffn_pre", "g_ffn_post", "w_up", "ffn_conv_w", "ffn_conv_b", "w_down")
    drop = lambda a: a if a.ndim == 2 else a[0]
    w = {n: drop(args[n]) for n in names}
    m = {n: drop(args["m_" + n]) for n in names}
    v = {n: drop(args["v_" + n]) for n in names}
    xi, yi, ci = _position()
    me_arr = jnp.reshape(4 * xi + 2 * yi + ci, (1,)).astype(jnp.int32)
    chip_arr = jnp.reshape(2 * xi + yi, (1,)).astype(jnp.int32)
    c_arr = jnp.reshape(ci, (1,)).astype(jnp.int32)
    pos_arr = jnp.stack([ci, 2 * xi + yi]).astype(jnp.int32)

    big = ("w_in", "w_out", "w_up", "w_down")
    rows_a, rows_b = [w[n].shape[0] for n in big[:2]], [w[n].shape[0] for n in big[2:]]
    lands = _cast_place([w[n] for n in big], chip_arr, "cast_place")
    bulk_send, bulk_recv, lands, token_b = _bulk_start(lands, rows_a + rows_b, pos_arr, "gather_start")
    lands_a, lands_b = lands[:2], lands[2:]

    row0 = jnp.concatenate([c + token_b[0:1, 0:1], w["conv_w"].reshape(1, -1), w["ffn_conv_w"].reshape(1, -1)],
                           axis=1)
    g0 = _allgather8_relay(row0, "gather_cond")[:, 0, :]
    c8 = g0[:, :D_MODEL]
    per_chip = g0[0::2]
    conv_w_full = per_chip[:, D_MODEL:D_MODEL + 512].reshape(N_CHIPS, 4, 128).transpose(1, 0, 2).reshape(4, 512)
    ffn_conv_w_full = per_chip[:, D_MODEL + 512:].reshape(N_CHIPS, 3, 1536).transpose(1, 0, 2).reshape(3, 2 * D_FF)
    mod_parts = _allgather8_relay(_ada_fwd(c8, w["w_ada"]), "gather_mod")
    lands_a = _bulk_wait(bulk_send, bulk_recv, lands_a, rows_a, 0, 4, mod_parts, "gather_wait_a")
    fwd_start, fwd_wait_a = _forward_plan([w[n].shape[0] for n in big[:2]])
    fwd_send_a, fwd_recv_a, _, lands_a, tok = _split_start([], lands_a, fwd_start, 6, pos_arr, "forward_start_a")
    mod = _mod_select(mod_parts, w["b_ada"].reshape(1, -1), me_arr, tok).reshape(N_MOD, D_MODEL)
    sh_m, sc_m, gt_m, sh_f, sc_f, gt_f = [mod[k:k + 1] for k in range(N_MOD)]

    small = {n: w[n] for n in SMALL_REPLICATED}
    small["conv_w"] = conv_w_full
    small["ffn_conv_w"] = ffn_conv_w_full
    row = lambda a: a.reshape(1, -1)
    seq_params, ws_t = _seq_params(small)
    glo, ggo = row(small["g_lru_out"]), row(small["g_gmlp_out"])
    g_pre, g_post = row(small["g_mix_pre"]), row(small["g_mix_post"])
    g_pre2, g_post2 = row(small["g_ffn_pre"]), row(small["g_ffn_post"])
    fw, fb = small["ffn_conv_w"], row(small["ffn_conv_b"])
    xs, tgt = x[0], loss_target[0]

    _, (w_in4, w_out4) = _split_wait(fwd_send_a, fwd_recv_a, [], lands_a, fwd_wait_a, mod, "forward_wait_a")
    w_out_b = w_out4.reshape(D_MODEL, D_MODEL)
    h, lx, ycat, hst, stash = _seqmix(xs, sc_m, sh_m, g_pre, w_in4, seq_params, glo, ggo)
    lands_b = _bulk_wait(bulk_send, bulk_recv, lands_b, rows_b, 2, 4, ycat, "gather_wait_b")
    fwd_start, fwd_wait = _forward_plan([w[n].shape[0] for n in big[2:]])
    fwd_send, fwd_recv, _, lands_b, tok = _split_start([], lands_b, fwd_start, 6, pos_arr, "forward_start_b")
    y, x1, h2 = _mix_out(ycat, xs, w_out_b, gt_m + tok[0:1, 0:1], g_post, g_pre2, sc_f, sh_f)
    _, (w_up4, w_down4) = _split_wait(fwd_send, fwd_recv, [], lands_b, fwd_wait, h2, "forward_wait_b")
    w_down_b = w_down4.reshape(D_FF, D_MODEL)
    up0, pre, act, dy2, dx2, loss, dgt_f, dg_post2 = _ffn_fwd(h2, x1, tgt, w_up4, w_down_b, fw, fb, gt_f, g_post2)

    dup0, dfw, dfb = _ffn_bwd_a(dy2, pre, up0, w_down_b, fw)
    gw_up = _wgrad(h2, dup0, N_CHIPS, "wgrad_up", True)
    gw_down = _wgrad(act, dy2, 2, "wgrad_down", False)
    ex_start, ex_wait = _exchange_plan(2)
    sg_start, sg_wait = _swap_gathered_plan(2)
    grads, deltas, new_m, new_v = {}, {}, {}, {}

    def swap_start(parts, name):
        sw_start, sw_wait = _swap_halves_plan([p.shape[1] // 2 for p in parts])
        recv = [lax.empty((N_CHIPS, p.shape[1] // 2, p.shape[2]), BF16) for p in parts]
        send_s, recv_s, parts, recv, token = _split_start(parts, recv, sw_start, len(parts), pos_arr,
                                                           "swap_start_" + name)
        return (send_s, recv_s, parts, recv, sw_wait), token

    def exchange_start(swap, tags, after, name):
        send_s, recv_s, parts, recv, sw_wait = swap
        parts, recv = _split_wait(send_s, recv_s, parts, recv, sw_wait, after, "swap_wait_" + name)
        both = [_chip_sum(p, r, pos_arr, "chip_sum_" + t) for p, r, t in zip(parts, recv, tags)]
        sums, gath = [b[0] for b in both], [b[1] for b in both]
        return _split_start(sums, gath, ex_start, 3 * len(parts), pos_arr, "exchange_start_" + name)

    def gathered_start(exchange, after, name):
        send_s, recv_s, sums, gath, _ = exchange
        _, gath = _split_wait(send_s, recv_s, sums, gath, ex_wait, after, "exchange_wait_" + name)
        send_s, recv_s, _, gath, token = _split_start([], gath, sg_start, len(gath), pos_arr,
                                                      "gathered_start_" + name)
        return (send_s, recv_s, gath), token

    def gathered_wait(gathered, after, name):
        send_s, recv_s, gath = gathered
        return _split_wait(send_s, recv_s, [], gath, sg_wait, after, "gathered_wait_" + name)[1]

    def adam_big(t, gath, after):
        grads[t], deltas[t], new_m[t], new_v[t] = _adam_gathered(w[t], gath, m[t], v[t], c_arr, after, "adam_" + t)

    def behind(value, token):
        return value + token[0:1, 0:1]

    tags_b, tags_a = ("w_up", "w_down"), ("w_in", "w_out")
    swap_b, tok = swap_start([gw_up, gw_down.reshape(N_CHIPS, -1, D_MODEL)], "b")
    dx1, dy, dsh_f, dsc_f, dg_pre2, dgt_m, dg_post = _ffn_bwd_b(
        dup0, x1, y, dx2, w_up4, g_pre2, behind(sc_f, tok), sh_f, gt_m, g_post)
    exchange_b = exchange_start(swap_b, tags_b, dg_post, "b")
    (dz, grad_x, dcw, dcb, dwr, dwi, dbr, dbi, dspa, dng, dnb, dws, dbs_t, dglo, dggo, dsh_m, dsc_m,
     dg_pre) = _seqmix_bwd(lx, hst, stash, dy, w_out_b, seq_params, ws_t, behind(glo, exchange_b[4]), ggo,
                           xs, dx1, w_in4, g_pre, sc_m)
    gw_in = _wgrad(h, dz, N_CHIPS, "wgrad_in", True)
    gw_out = _wgrad(ycat, dy, 1, "wgrad_out", False)
    swap_a, tok = swap_start([gw_in, gw_out.reshape(N_CHIPS, -1, D_MODEL)], "a")

    dmod = jnp.concatenate([behind(dsh_m, tok), dsc_m, dgt_m, dsh_f, dsc_f, dgt_f], axis=1)
    dmod8 = _allgather8(dmod, "gather_dmod")[:, 0, :]
    small_grads = dict(
        g_mix_pre=dg_pre, g_mix_post=dg_post, conv_w=dcw, conv_b=dcb, w_rgate=dwr, b_rgate=dbr, w_igate=dwi,
        b_igate=dbi, lru_a=dspa, v_norm_g=dng, v_norm_b=dnb, w_spatial=dws, b_spatial=dbs_t, g_lru_out=dglo,
        g_gmlp_out=dggo, g_ffn_pre=dg_pre2, g_ffn_post=dg_post2, ffn_conv_w=dfw, ffn_conv_b=dfb,
        loss=loss)
    g_small = _allreduce_small(_pack_small(small_grads, dmod8), "reduce_small")
    total = g_small[_small_pieces("loss", 0, 1)[0][:3]]
    exchange_a = exchange_start(swap_a, tags_a, g_small, "a")
    gathered_b, tok = gathered_start(exchange_b, exchange_a[4], "b")

    grads["w_ada"], deltas["w_ada"], new_m["w_ada"], new_v["w_ada"], g_b_ada = _ada_bwd(
        c8, behind(dmod8, tok), chip_arr, w["w_ada"], m["w_ada"], v["w_ada"])
    rep = SMALL_REPLICATED
    small_out = _adam_small(g_small, {n: args[n] for n in rep}, {n: args["m_" + n] for n in rep},
                            {n: args["v_" + n] for n in rep})
    for n in rep:
        grads[n], deltas[n], new_m[n], new_v[n] = [group[n] for group in small_out]
    for n in SMALL_COLUMN_SHARDED:
        grads[n], deltas[n], new_m[n], new_v[n] = _adam_cols(n, g_small, args[n], args["m_" + n],
                                                             args["v_" + n], chip_arr)
    d_b, m_b, v_b = _adam(w["b_ada"], g_b_ada, m["b_ada"], v["b_ada"], "adam_b_ada")
    grads["b_ada"], deltas["b_ada"], new_m["b_ada"], new_v["b_ada"] = g_b_ada, d_b, m_b, v_b

    gath_up, gath_down = gathered_wait(gathered_b, d_b, "b")
    adam_big("w_down", gath_down, pos_arr)
    gathered_a, tok = gathered_start(exchange_a, deltas["w_down"], "a")
    adam_big("w_up", gath_up, tok)
    gath_in, gath_out = gathered_wait(gathered_a, deltas["w_up"], "a")
    adam_big("w_in", gath_in, pos_arr)
    adam_big("w_out", gath_out, pos_arr)

    outs = [total, grad_x[None]]
    for group in (grads, deltas, new_m, new_v):
        outs.extend(group[n].reshape(args[n].shape) for n in names)
    return tuple(outs)
```
